```python
import math
import jax, jax.numpy as jnp
from jax import lax
import numpy as np

D_MODEL = 1024
BATCH = 16
SEQ = 2048
DEPTH = 4

N_MEM = 256
HEAD_DIM = 64
N_MEM_HEADS = 4
MEM_W = N_MEM_HEADS * HEAD_DIM
MAIN_W = D_MODEL - MEM_W
MIX_W = MAIN_W + MEM_W
POOL_WINDOWS = (2, 4, 8, 16)
POOL_GROUP = MAIN_W // len(POOL_WINDOWS)
DIL_PATTERNS = ((128, 1), (512, 4), (2048, 16))
N_GROUPS = len(DIL_PATTERNS)
HEADS_PER_GROUP = MAIN_W // (N_GROUPS * HEAD_DIM)
N_DIL_HEADS = N_GROUPS * HEADS_PER_GROUP
N_A_LAYERS = DEPTH // 2
N_B_LAYERS = DEPTH - N_A_LAYERS
D_FF = ((8 * D_MODEL + 3 * 256 - 1) // (3 * 256)) * 256
ROPE_THETA = 10000.0
EPS = 1e-6
NEG = -1e30

kernel_name = "yoco_pool_dilated_hybrid"


def rms_norm(x, g):
    xf = x.astype(jnp.float32)
    y = xf * lax.rsqrt(jnp.mean(xf * xf, axis=-1, keepdims=True) + EPS)
    return (y * g.astype(jnp.float32)).astype(x.dtype)


def rope(t, positions):
    half = HEAD_DIM // 2
    freqs = ROPE_THETA ** (-jnp.arange(half, dtype=jnp.float32) / half)
    ang = positions.astype(jnp.float32)[..., None] * freqs
    cos = jnp.cos(ang)[:, :, None, :]
    sin = jnp.sin(ang)[:, :, None, :]
    tf = t.astype(jnp.float32)
    t1, t2 = tf[..., :half], tf[..., half:]
    return jnp.concatenate([t1 * cos - t2 * sin, t1 * sin + t2 * cos], axis=-1).astype(t.dtype)


def pool_mixer(u, w_pool, scale):
    S = u.shape[1]
    uf = u.astype(jnp.float32)
    c = jnp.cumsum(uf, axis=1)
    t = jnp.arange(S)
    outs = []
    for gi, w in enumerate(POOL_WINDOWS):
        sl = slice(gi * POOL_GROUP, (gi + 1) * POOL_GROUP)
        cg = c[..., sl]
        shifted = jnp.pad(cg, ((0, 0), (w, 0), (0, 0)))[:, :S]
        cnt = jnp.minimum(t + 1, w).astype(jnp.float32)[None, :, None]
        outs.append((cg - shifted) / cnt - uf[..., sl])
    p = jnp.stack(outs, axis=2).astype(u.dtype)
    y = jnp.einsum('bsgc,gce->bsge', p, w_pool).reshape(u.shape)
    return y * scale


def to_strided(t, d):
    B, S = t.shape[:2]
    return t.reshape((B, S // d, d) + t.shape[2:]).swapaxes(1, 2)


def kv_blocks(t, d, steps):
    s = to_strided(t, d)
    B, _, L, H, hd = s.shape
    nb = -(-L // steps)
    Lp = nb * steps
    s = jnp.pad(s, ((0, 0), (0, 0), (steps, Lp - L), (0, 0), (0, 0)))
    s = s.reshape(B, d, nb + 1, steps, H, hd)
    return jnp.concatenate([s[:, :, :-1], s[:, :, 1:]], axis=3)


def band_mask(nb, steps):
    i = jnp.arange(steps)[:, None] + steps
    j = jnp.arange(2 * steps)[None, :]
    rel = i - j
    m = (rel >= 0) & (rel <= steps)
    n = jnp.arange(nb)[:, None, None]
    return m[None] & ((n > 0) | (j >= steps)[None])


def dilated_group_attn(q, kk, vv, d, steps):
    B, S, H, hd = q.shape
    L = S // d
    nb = kk.shape[2]
    Lp = nb * steps
    qs = jnp.pad(to_strided(q, d), ((0, 0), (0, 0), (0, Lp - L), (0, 0), (0, 0)))
    qs = qs.reshape(B, d, nb, steps, H, hd)
    s = jnp.einsum('bdnqhe,bdnkhe->bdnhqk', qs, kk).astype(jnp.float32) * (HEAD_DIM ** -0.5)
    s = jnp.where(band_mask(nb, steps)[None, None, :, None], s, NEG)
    lse = jax.nn.logsumexp(s, axis=-1)
    p = jnp.exp(s - lse[..., None]).astype(vv.dtype)
    o = jnp.einsum('bdnhqk,bdnkhe->bdnqhe', p, vv)
    o = o.reshape(B, d, Lp, H, hd)[:, :, :L].swapaxes(1, 2).reshape(B, S, H, hd)
    lse = lse.transpose(0, 1, 2, 4, 3).reshape(B, d, Lp, H)[:, :, :L].swapaxes(1, 2).reshape(B, S, H)
    return o, lse


def build_shared_kv(x, positions, kv_norm, w_kv):
    B, S, _ = x.shape
    kv = rms_norm(x, kv_norm) @ w_kv
    k = rope(kv[..., :MAIN_W].reshape(B, S, N_DIL_HEADS, HEAD_DIM), positions)
    v = kv[..., MAIN_W:].reshape(B, S, N_DIL_HEADS, HEAD_DIM)
    k = k.reshape(B, S, N_GROUPS, HEADS_PER_GROUP, HEAD_DIM)
    v = v.reshape(B, S, N_GROUPS, HEADS_PER_GROUP, HEAD_DIM)
    shared = []
    for g, (win, d) in enumerate(DIL_PATTERNS):
        steps = win // d
        shared.append((kv_blocks(k[:, :, g], d, steps), kv_blocks(v[:, :, g], d, steps)))
    return shared


def dilated_mixer(zq, positions, shared):
    B, S, _ = zq.shape
    q = rope(zq.reshape(B, S, N_DIL_HEADS, HEAD_DIM), positions)
    q = q.reshape(B, S, N_GROUPS, HEADS_PER_GROUP, HEAD_DIM)
    outs, lses = [], []
    for g, (win, d) in enumerate(DIL_PATTERNS):
        kk, vv = shared[g]
        o, l = dilated_group_attn(q[:, :, g], kk, vv, d, win // d)
        outs.append(o)
        lses.append(l)
    o = jnp.stack(outs, axis=2)
    alpha = jax.nn.softmax(jnp.stack(lses, axis=2), axis=2)
    return (o * alpha[..., None].astype(o.dtype)).reshape(B, S, MAIN_W)


def memory_attn(zm, mem, g, w_mkv):
    B, S, _ = zm.shape
    M = mem.shape[1]
    q = zm.reshape(B, S, N_MEM_HEADS, HEAD_DIM)
    kv = rms_norm(mem, g) @ w_mkv
    km = kv[..., :MEM_W].reshape(B, M, N_MEM_HEADS, HEAD_DIM)
    vm = kv[..., MEM_W:].reshape(B, M, N_MEM_HEADS, HEAD_DIM)
    s = jnp.einsum('bshe,bmhe->bhsm', q, km).astype(jnp.float32) * (HEAD_DIM ** -0.5)
    p = jax.nn.softmax(s, axis=-1).astype(vm.dtype)
    return jnp.einsum('bhsm,bmhe->bshe', p, vm).reshape(B, S, MEM_W)


def _fwd_setup_inputs(seed: int = 0) -> dict:
    key = jax.random.key(seed)
    ks = jax.random.split(key, 14)
    f32 = jnp.float32
    nrm = lambda k, shape, fan: jax.random.normal(k, shape, f32) * (fan ** -0.5)
    return {
        "x": jax.random.normal(ks[0], (BATCH, SEQ, D_MODEL), f32),
        "mem": jax.random.normal(ks[1], (BATCH, N_MEM, D_MODEL), f32),
        "positions": jnp.broadcast_to(jnp.arange(SEQ, dtype=jnp.int32)[None], (BATCH, SEQ)),
        "norm_gains": 1.0 + 0.05 * jax.random.normal(ks[2], (DEPTH, 4, D_MODEL), f32),
        "mem_norm": 1.0 + 0.05 * jax.random.normal(ks[3], (DEPTH, D_MODEL), f32),
        "w_in": nrm(ks[4], (DEPTH, D_MODEL, MIX_W), D_MODEL),
        "w_mem_kv": nrm(ks[5], (DEPTH, D_MODEL, 2 * MEM_W), D_MODEL),
        "w_out": nrm(ks[6], (DEPTH, MIX_W, D_MODEL), MIX_W),
        "w_pool": nrm(ks[7], (N_A_LAYERS, len(POOL_WINDOWS), POOL_GROUP, POOL_GROUP), POOL_GROUP),
        "pool_scale": 1.0 + 0.1 * jax.random.normal(ks[8], (N_A_LAYERS, MAIN_W), f32),
        "kv_norm": 1.0 + 0.05 * jax.random.normal(ks[9], (D_MODEL,), f32),
        "w_kv": nrm(ks[10], (D_MODEL, 2 * MAIN_W), D_MODEL),
        "w_gate_up": nrm(ks[11], (DEPTH, D_MODEL, 2 * D_FF), D_MODEL),
        "w_down": nrm(ks[12], (DEPTH, D_FF, D_MODEL), D_FF),
    }


def _fwd_reference(x, mem, positions, norm_gains, mem_norm, w_in, w_mem_kv, w_out,
              w_pool, pool_scale, kv_norm, w_kv, w_gate_up, w_down):
    shared = None
    for l in range(DEPTH):
        h = rms_norm(x, norm_gains[l, 0])
        z = h @ w_in[l]
        z_main, z_mem = z[..., :MAIN_W], z[..., MAIN_W:]
        if l < N_A_LAYERS:
            y_main = pool_mixer(z_main, w_pool[l], pool_scale[l])
        else:
            y_main = dilated_mixer(z_main, positions, shared)
        y_mem = memory_attn(z_mem, mem, mem_norm[l], w_mem_kv[l])
        y = jnp.concatenate([y_main, y_mem], axis=-1) @ w_out[l]
        x = x + rms_norm(y, norm_gains[l, 1])
        h = rms_norm(x, norm_gains[l, 2])
        gu = h @ w_gate_up[l]
        y = (jax.nn.silu(gu[..., :D_FF]) * gu[..., D_FF:]) @ w_down[l]
        x = x + rms_norm(y, norm_gains[l, 3])
        if l == N_A_LAYERS - 1:
            shared = build_shared_kv(x, positions, kv_norm, w_kv)
    return x


import jax as _jax
import jax.numpy as _jnp

TWIN_FORMAT = 'train_step'
FWD_PARAMS = ['x', 'mem', 'positions', 'norm_gains', 'mem_norm', 'w_in', 'w_mem_kv', 'w_out', 'w_pool', 'pool_scale', 'kv_norm', 'w_kv', 'w_gate_up', 'w_down']
TWIN_WEIGHTS = ['norm_gains', 'mem_norm', 'w_in', 'w_mem_kv', 'w_out', 'w_pool', 'pool_scale', 'kv_norm', 'w_kv', 'w_gate_up', 'w_down']
TWIN_DIFF_INPUT = 'x'
TWIN_INPUTS = ['x', 'mem', 'positions', 'norm_gains', 'mem_norm', 'w_in', 'w_mem_kv', 'w_out', 'w_pool', 'pool_scale', 'kv_norm', 'w_kv', 'w_gate_up', 'w_down', 'loss_target', 'm_norm_gains', 'm_mem_norm', 'm_w_in', 'm_w_mem_kv', 'm_w_out', 'm_w_pool', 'm_pool_scale', 'm_kv_norm', 'm_w_kv', 'm_w_gate_up', 'm_w_down', 'v_norm_gains', 'v_mem_norm', 'v_w_in', 'v_w_mem_kv', 'v_w_out', 'v_w_pool', 'v_pool_scale', 'v_kv_norm', 'v_w_kv', 'v_w_gate_up', 'v_w_down']
TWIN_OUTPUTS = ['loss', 'grad_x', 'grad_norm_gains', 'grad_mem_norm', 'grad_w_in', 'grad_w_mem_kv', 'grad_w_out', 'grad_w_pool', 'grad_pool_scale', 'grad_kv_norm', 'grad_w_kv', 'grad_w_gate_up', 'grad_w_down', 'delta_norm_gains', 'delta_mem_norm', 'delta_w_in', 'delta_w_mem_kv', 'delta_w_out', 'delta_w_pool', 'delta_pool_scale', 'delta_kv_norm', 'delta_w_kv', 'delta_w_gate_up', 'delta_w_down', 'new_m_norm_gains', 'new_m_mem_norm', 'new_m_w_in', 'new_m_w_mem_kv', 'new_m_w_out', 'new_m_w_pool', 'new_m_pool_scale', 'new_m_kv_norm', 'new_m_w_kv', 'new_m_w_gate_up', 'new_m_w_down', 'new_v_norm_gains', 'new_v_mem_norm', 'new_v_w_in', 'new_v_w_mem_kv', 'new_v_w_out', 'new_v_w_pool', 'new_v_pool_scale', 'new_v_kv_norm', 'new_v_w_kv', 'new_v_w_gate_up', 'new_v_w_down']
TWIN_LEAF_KINDS = {'loss': 'loss', 'grad_x': 'grad_x', 'grad_norm_gains': 'grad_w', 'grad_mem_norm': 'grad_w', 'grad_w_in': 'grad_w', 'grad_w_mem_kv': 'grad_w', 'grad_w_out': 'grad_w', 'grad_w_pool': 'grad_w', 'grad_pool_scale': 'grad_w', 'grad_kv_norm': 'grad_w', 'grad_w_kv': 'grad_w', 'grad_w_gate_up': 'grad_w', 'grad_w_down': 'grad_w', 'delta_norm_gains': 'delta_w', 'delta_mem_norm': 'delta_w', 'delta_w_in': 'delta_w', 'delta_w_mem_kv': 'delta_w', 'delta_w_out': 'delta_w', 'delta_w_pool': 'delta_w', 'delta_pool_scale': 'delta_w', 'delta_kv_norm': 'delta_w', 'delta_w_kv': 'delta_w', 'delta_w_gate_up': 'delta_w', 'delta_w_down': 'delta_w', 'new_m_norm_gains': 'new_m', 'new_m_mem_norm': 'new_m', 'new_m_w_in': 'new_m', 'new_m_w_mem_kv': 'new_m', 'new_m_w_out': 'new_m', 'new_m_w_pool': 'new_m', 'new_m_pool_scale': 'new_m', 'new_m_kv_norm': 'new_m', 'new_m_w_kv': 'new_m', 'new_m_w_gate_up': 'new_m', 'new_m_w_down': 'new_m', 'new_v_norm_gains': 'new_v', 'new_v_mem_norm': 'new_v', 'new_v_w_in': 'new_v', 'new_v_w_mem_kv': 'new_v', 'new_v_w_out': 'new_v', 'new_v_w_pool': 'new_v', 'new_v_pool_scale': 'new_v', 'new_v_kv_norm': 'new_v', 'new_v_w_kv': 'new_v', 'new_v_w_gate_up': 'new_v', 'new_v_w_down': 'new_v'}


def _forward(args):
    return _fwd_reference(*[args[k] for k in FWD_PARAMS])


def _output_shape():
    out = _jax.eval_shape(lambda: _forward(_fwd_setup_inputs(0)))
    return out.shape, out.dtype

N_MICROBATCH = 1
ADAM_LR = 0.001
ADAM_B1 = 0.9
ADAM_B2 = 0.999
ADAM_EPS = 1e-08
ADAM_WD = 0.01
ADAM_STEP = 10
PER_EXAMPLE_BATCH_AXIS = {'x': 0, 'mem': 0, 'positions': 0, 'loss_target': 0}
SHARED_INPUTS = []
_WEIGHT_DTYPES = {'norm_gains': _jnp.float32, 'mem_norm': _jnp.float32, 'w_in': _jnp.float32, 'w_mem_kv': _jnp.float32, 'w_out': _jnp.float32, 'w_pool': _jnp.float32, 'pool_scale': _jnp.float32, 'kv_norm': _jnp.float32, 'w_kv': _jnp.float32, 'w_gate_up': _jnp.float32, 'w_down': _jnp.float32}
MOMENT_SCALE = {'norm_gains': 2.198325e+01, 'mem_norm': 2.718279e+00, 'w_in': 1.522715e+00, 'w_mem_kv': 3.488410e+00, 'w_out': 2.448914e+00, 'w_pool': 2.362278e+00, 'pool_scale': 2.702659e+00, 'kv_norm': 1.427798e+00, 'w_kv': 1.168664e+00, 'w_gate_up': 6.921838e-01, 'w_down': 1.273189e+00}


def _to_microbatches(a, axis):
    t = _jnp.moveaxis(a, axis, 0)
    t = t.reshape((N_MICROBATCH, t.shape[0] // N_MICROBATCH) + t.shape[1:])
    return _jnp.moveaxis(t, 1, axis + 1)


def setup_inputs(seed: int = 0) -> dict:
    inp = _fwd_setup_inputs(seed)
    key = _jax.random.fold_in(_jax.random.key(seed), 7919)
    shape, _ = _output_shape()
    out = dict(inp)
    out["loss_target"] = _jax.random.normal(_jax.random.fold_in(key, 0), shape, _jnp.float32)
    for i, name in enumerate(TWIN_WEIGHTS):
        w = inp[name].astype(_jnp.float32)
        if MOMENT_SCALE is None:
            s = _jnp.sqrt(_jnp.mean(_jnp.square(w)) + 1e-30)
        else:
            s = MOMENT_SCALE[name]
        km, kv = _jax.random.split(_jax.random.fold_in(key, i + 1))
        out[name] = w
        out["m_" + name] = s * _jax.random.normal(km, w.shape, _jnp.float32)
        out["v_" + name] = (s * s) * _jax.random.uniform(kv, w.shape, _jnp.float32, 0.5, 1.5)
    if N_MICROBATCH > 1:
        for name, axis in PER_EXAMPLE_BATCH_AXIS.items():
            out[name] = _to_microbatches(out[name], axis)
    return {'x': out['x'], 'mem': out['mem'], 'positions': out['positions'], 'norm_gains': out['norm_gains'], 'mem_norm': out['mem_norm'], 'w_in': out['w_in'], 'w_mem_kv': out['w_mem_kv'], 'w_out': out['w_out'], 'w_pool': out['w_pool'], 'pool_scale': out['pool_scale'], 'kv_norm': out['kv_norm'], 'w_kv': out['w_kv'], 'w_gate_up': out['w_gate_up'], 'w_down': out['w_down'], 'loss_target': out['loss_target'], 'm_norm_gains': out['m_norm_gains'], 'm_mem_norm': out['m_mem_norm'], 'm_w_in': out['m_w_in'], 'm_w_mem_kv': out['m_w_mem_kv'], 'm_w_out': out['m_w_out'], 'm_w_pool': out['m_w_pool'], 'm_pool_scale': out['m_pool_scale'], 'm_kv_norm': out['m_kv_norm'], 'm_w_kv': out['m_w_kv'], 'm_w_gate_up': out['m_w_gate_up'], 'm_w_down': out['m_w_down'], 'v_norm_gains': out['v_norm_gains'], 'v_mem_norm': out['v_mem_norm'], 'v_w_in': out['v_w_in'], 'v_w_mem_kv': out['v_w_mem_kv'], 'v_w_out': out['v_w_out'], 'v_w_pool': out['v_w_pool'], 'v_pool_scale': out['v_pool_scale'], 'v_kv_norm': out['v_kv_norm'], 'v_w_kv': out['v_w_kv'], 'v_w_gate_up': out['v_w_gate_up'], 'v_w_down': out['v_w_down']}


def _loss(weights, diff, rest, loss_target):
    with _jax.named_scope("forward"):
        args = {**rest, TWIN_DIFF_INPUT: diff, **{k: w.astype(_WEIGHT_DTYPES[k]) for k, w in weights.items()}}
        y = _forward(args)
    with _jax.named_scope("loss_head"):
        err = _jnp.square(y.astype(_jnp.float32) - loss_target)
        return 0.5 * _jnp.sum(_jnp.mean(err, axis=-1)) if err.ndim else 0.5 * err


def _adamw(w, g, m, v):
    m = ADAM_B1 * m + (1.0 - ADAM_B1) * g
    v = ADAM_B2 * v + (1.0 - ADAM_B2) * _jnp.square(g)
    m_hat = m / (1.0 - ADAM_B1 ** ADAM_STEP)
    v_hat = v / (1.0 - ADAM_B2 ** ADAM_STEP)
    delta = -ADAM_LR * (m_hat / (_jnp.sqrt(v_hat) + ADAM_EPS) + ADAM_WD * w)
    return delta, m, v


def reference(x, mem, positions, norm_gains, mem_norm, w_in, w_mem_kv, w_out, w_pool, pool_scale, kv_norm, w_kv, w_gate_up, w_down, loss_target, m_norm_gains, m_mem_norm, m_w_in, m_w_mem_kv, m_w_out, m_w_pool, m_pool_scale, m_kv_norm, m_w_kv, m_w_gate_up, m_w_down, v_norm_gains, v_mem_norm, v_w_in, v_w_mem_kv, v_w_out, v_w_pool, v_pool_scale, v_kv_norm, v_w_kv, v_w_gate_up, v_w_down):
    given = dict(x=x, mem=mem, positions=positions, norm_gains=norm_gains, mem_norm=mem_norm, w_in=w_in, w_mem_kv=w_mem_kv, w_out=w_out, w_pool=w_pool, pool_scale=pool_scale, kv_norm=kv_norm, w_kv=w_kv, w_gate_up=w_gate_up, w_down=w_down, loss_target=loss_target, m_norm_gains=m_norm_gains, m_mem_norm=m_mem_norm, m_w_in=m_w_in, m_w_mem_kv=m_w_mem_kv, m_w_out=m_w_out, m_w_pool=m_w_pool, m_pool_scale=m_pool_scale, m_kv_norm=m_kv_norm, m_w_kv=m_w_kv, m_w_gate_up=m_w_gate_up, m_w_down=m_w_down, v_norm_gains=v_norm_gains, v_mem_norm=v_mem_norm, v_w_in=v_w_in, v_w_mem_kv=v_w_mem_kv, v_w_out=v_w_out, v_w_pool=v_w_pool, v_pool_scale=v_pool_scale, v_kv_norm=v_kv_norm, v_w_kv=v_w_kv, v_w_gate_up=v_w_gate_up, v_w_down=v_w_down)
    weights = {n: given[n] for n in TWIN_WEIGHTS}
    shared = {n: given[n] for n in SHARED_INPUTS}
    per_example = {n: given[n] for n in ['x', 'mem', 'positions']}
    grad_fn = _jax.value_and_grad(_loss, argnums=(0, 1))

    def one_microbatch(ex, loss_target):
        ex = dict(ex)
        diff = ex.pop(TWIN_DIFF_INPUT)
        return grad_fn(weights, diff, {**shared, **ex}, loss_target)

    if N_MICROBATCH == 1:
        loss, (grad_w, grad_x) = one_microbatch(per_example, given["loss_target"])
    else:
        def body(carry, xs):
            loss_sum, grad_sum = carry
            l_k, (gw_k, gx_k) = one_microbatch(xs[0], xs[1])
            with _jax.named_scope("update"):
                return (loss_sum + l_k, _jax.tree.map(_jnp.add, grad_sum, gw_k)), gx_k

        init = (_jnp.zeros((), _jnp.float32), _jax.tree.map(_jnp.zeros_like, weights))
        (loss, grad_w), grad_x = _jax.lax.scan(body, init, (per_example, given["loss_target"]))
    with _jax.named_scope("update"):
        delta_w, new_m, new_v = {}, {}, {}
        for n in TWIN_WEIGHTS:
            delta_w[n], new_m[n], new_v[n] = _adamw(weights[n], grad_w[n], given["m_" + n], given["v_" + n])
    return (loss, grad_x, *[grad_w[n] for n in TWIN_WEIGHTS], *[delta_w[n] for n in TWIN_WEIGHTS],
            *[new_m[n] for n in TWIN_WEIGHTS], *[new_v[n] for n in TWIN_WEIGHTS])
```

```python
import functools
import math

import numpy as np
import jax
import jax.numpy as jnp
from jax import lax
from jax.experimental import pallas as pl
from jax.experimental.pallas import tpu as pltpu

f32 = jnp.float32
bf16 = jnp.bfloat16

D_MODEL = 1024
SEQ = 2048
DEPTH = 4
N_MEM = 256
HEAD_DIM = 64
N_MEM_HEADS = 4
MEM_W = 256
MAIN_W = 768
POOL_WINDOWS = (2, 4, 8, 16)
POOL_GROUP = 192
POOL_HALO = 16
DIL_PATTERNS = ((128, 1), (512, 4), (2048, 16))
N_GROUPS = 3
GROUP_W = 256
BAND = 128
N_A_LAYERS = 2
D_FF = 2816
ROPE_THETA = 10000.0
EPS = 1e-6
NEG = -1e30
SCALE = HEAD_DIM ** -0.5
N_DEV = 8

ADAM_LR = 0.001
ADAM_B1 = 0.9
ADAM_B2 = 0.999
ADAM_EPS = 1e-08
ADAM_WD = 0.01
ADAM_STEP = 10

VMEM_LIMIT_BYTES = 56 * 1024 * 1024
MESH = pl.DeviceIdType.MESH

NT = (((1,), (1,)), ((), ()))
TN = (((0,), (0,)), ((), ()))


def _params(sem=None):
    return pltpu.CompilerParams(dimension_semantics=sem, vmem_limit_bytes=VMEM_LIMIT_BYTES)


def _tile(n, cands):
    for c in cands:
        if n % c == 0:
            return c
    return n


def _sds(shape, dtype):
    return jax.ShapeDtypeStruct(tuple(shape), dtype)


def _rms_r(v):
    return lax.rsqrt(jnp.mean(v * v, axis=-1, keepdims=True) + EPS)


def rms_matmul(x, gain, w, *, name, out_dtype):
    M, K = x.shape
    N = w.shape[1]
    tm = min(512, M)
    tn = _tile(N, (512, 256, 128))

    def body(x_ref, g_ref, w_ref, z_ref, h_ref):
        @pl.when(pl.program_id(1) == 0)
        def _():
            xv = x_ref[...]
            h_ref[...] = (xv * _rms_r(xv) * g_ref[...]).astype(bf16)

        z_ref[...] = jnp.dot(h_ref[...], w_ref[...], preferred_element_type=f32).astype(z_ref.dtype)

    return pl.pallas_call(
        body, name=name, grid=(M // tm, N // tn),
        in_specs=[pl.BlockSpec((tm, K), lambda i, j: (i, 0)),
                  pl.BlockSpec((1, K), lambda i, j: (0, 0)),
                  pl.BlockSpec((K, tn), lambda i, j: (0, j))],
        out_specs=[pl.BlockSpec((tm, tn), lambda i, j: (i, j)),
                   pl.BlockSpec((tm, K), lambda i, j: (i, 0))],
        out_shape=[_sds((M, N), out_dtype), _sds((M, K), bf16)],
        compiler_params=_params(("parallel", "arbitrary")),
    )(x, gain, w)


def matmul_rms_res(a, w, gain, res, *, name):
    M, K = a.shape
    N = w.shape[1]
    tm = min(512, M)

    def body(a_ref, w_ref, g_ref, r_ref, y_ref, x_ref):
        y = jnp.dot(a_ref[...], w_ref[...], preferred_element_type=f32)
        y_ref[...] = y.astype(bf16)
        x_ref[...] = r_ref[...] + y * _rms_r(y) * g_ref[...]

    return pl.pallas_call(
        body, name=name, grid=(M // tm,),
        in_specs=[pl.BlockSpec((tm, K), lambda i: (i, 0)),
                  pl.BlockSpec((K, N), lambda i: (0, 0)),
                  pl.BlockSpec((1, N), lambda i: (0, 0)),
                  pl.BlockSpec((tm, N), lambda i: (i, 0))],
        out_specs=[pl.BlockSpec((tm, N), lambda i: (i, 0)),
                   pl.BlockSpec((tm, N), lambda i: (i, 0))],
        out_shape=[_sds((M, N), bf16), _sds((M, N), f32)],
        compiler_params=_params(("parallel",)),
    )(a, w, gain, res)


def matmul(a, b, *, name, mode, out_dtype):
    if mode == "nt":
        M, K = a.shape
        N = b.shape[0]
    else:
        K, M = a.shape
        N = b.shape[1]
    tm = _tile(M, (512, 256, 128))
    tn = _tile(N, (512, 256, 128))
    tk = K if K <= 2048 else _tile(K, (2048, 1408, 1024, 512))
    nk = K // tk
    dims = NT if mode == "nt" else TN

    def body(a_ref, b_ref, o_ref, acc_ref):
        k = pl.program_id(2)

        @pl.when(k == 0)
        def _():
            acc_ref[...] = jnp.zeros_like(acc_ref)

        acc_ref[...] += lax.dot_general(a_ref[...].astype(bf16), b_ref[...].astype(bf16), dims,
                                        preferred_element_type=f32)

        @pl.when(k == nk - 1)
        def _():
            o_ref[...] = acc_ref[...].astype(o_ref.dtype)

    if mode == "nt":
        in_specs = [pl.BlockSpec((tm, tk), lambda i, j, k: (i, k)),
                    pl.BlockSpec((tn, tk), lambda i, j, k: (j, k))]
    else:
        in_specs = [pl.BlockSpec((tk, tm), lambda i, j, k: (k, i)),
                    pl.BlockSpec((tk, tn), lambda i, j, k: (k, j))]
    return pl.pallas_call(
        body, name=name, grid=(M // tm, N // tn, nk),
        in_specs=in_specs,
        out_specs=pl.BlockSpec((tm, tn), lambda i, j, k: (i, j)),
        out_shape=_sds((M, N), out_dtype),
        scratch_shapes=[pltpu.VMEM((tm, tn), f32)],
        compiler_params=_params(("parallel", "parallel", "arbitrary")),
    )(a, b)


def swiglu_fwd(gu, *, name):
    M = gu.shape[0]
    tm = min(256, M)

    def body(g_ref, u_ref, a_ref):
        g = g_ref[...].astype(f32)
        s = 1.0 / (1.0 + jnp.exp(-g))
        a_ref[...] = (g * s * u_ref[...].astype(f32)).astype(bf16)

    return pl.pallas_call(
        body, name=name, grid=(M // tm,),
        in_specs=[pl.BlockSpec((tm, D_FF), lambda i: (i, 0)),
                  pl.BlockSpec((tm, D_FF), lambda i: (i, 1))],
        out_specs=pl.BlockSpec((tm, D_FF), lambda i: (i, 0)),
        out_shape=_sds((M, D_FF), bf16),
        compiler_params=_params(("parallel",)),
    )(gu, gu)


def swiglu_bwd(gu, da, *, name):
    M = gu.shape[0]
    tm = min(256, M)

    def body(g_ref, u_ref, da_ref, o_ref):
        g = g_ref[...].astype(f32)
        u = u_ref[...].astype(f32)
        da = da_ref[...].astype(f32)
        s = 1.0 / (1.0 + jnp.exp(-g))
        o_ref[:, :D_FF] = (da * u * s * (1.0 + g * (1.0 - s))).astype(bf16)
        o_ref[:, D_FF:] = (da * g * s).astype(bf16)

    return pl.pallas_call(
        body, name=name, grid=(M // tm,),
        in_specs=[pl.BlockSpec((tm, D_FF), lambda i: (i, 0)),
                  pl.BlockSpec((tm, D_FF), lambda i: (i, 1)),
                  pl.BlockSpec((tm, D_FF), lambda i: (i, 0))],
        out_specs=pl.BlockSpec((tm, 2 * D_FF), lambda i: (i, 0)),
        out_shape=_sds((M, 2 * D_FF), bf16),
        compiler_params=_params(("parallel",)),
    )(gu, gu, da)


def rms_bwd(y, gain, dn, res, *, name, out_dtype):
    M, N = y.shape
    tm = min(512, M)
    has_res = res is not None

    def body(*refs):
        if has_res:
            y_ref, g_ref, dn_ref, r_ref, dy_ref, dg_ref = refs
        else:
            y_ref, g_ref, dn_ref, dy_ref, dg_ref = refs
        yv = y_ref[...].astype(f32)
        dn = dn_ref[...].astype(f32)
        r = _rms_r(yv)
        q = dn * g_ref[...]
        dy = r * q - yv * (r * r * r) * jnp.mean(q * yv, axis=-1, keepdims=True)
        if has_res:
            dy = dy + r_ref[...]
        dy_ref[...] = dy.astype(dy_ref.dtype)

        @pl.when(pl.program_id(0) == 0)
        def _():
            dg_ref[...] = jnp.zeros_like(dg_ref)

        dg_ref[...] += jnp.sum(dn * yv * r, axis=0, keepdims=True)

    row = pl.BlockSpec((tm, N), lambda i: (i, 0))
    vec = pl.BlockSpec((1, N), lambda i: (0, 0))
    args = [y, gain, dn] + ([res] if has_res else [])
    return pl.pallas_call(
        body, name=name, grid=(M // tm,),
        in_specs=[row, vec, row] + ([row] if has_res else []),
        out_specs=[row, vec],
        out_shape=[_sds((M, N), out_dtype), _sds((1, N), f32)],
        compiler_params=_params(("arbitrary",)),
    )(*args)


def loss_head(x, target, *, name):
    M, N = x.shape
    tm = min(512, M)

    def body(x_ref, t_ref, dx_ref, l_ref):
        e = x_ref[...] - t_ref[...]
        dx_ref[...] = e * (1.0 / N)

        @pl.when(pl.program_id(0) == 0)
        def _():
            l_ref[...] = jnp.zeros_like(l_ref)

        l_ref[...] += jnp.sum(jnp.sum(e * e, axis=0, keepdims=True), axis=1, keepdims=True)

    row = pl.BlockSpec((tm, N), lambda i: (i, 0))
    return pl.pallas_call(
        body, name=name, grid=(M // tm,),
        in_specs=[row, row],
        out_specs=[row, pl.BlockSpec((8, 128), lambda i: (0, 0))],
        out_shape=[_sds((M, N), f32), _sds((8, 128), f32)],
        compiler_params=_params(("arbitrary",)),
    )(x, target)


def _pool_select(a1, a2, a3, a4):
    col = lax.broadcasted_iota(jnp.int32, (1, MAIN_W), 1) // POOL_GROUP
    return jnp.where(col == 0, a1, jnp.where(col == 1, a2, jnp.where(col == 2, a3, a4)))


def _pool_count(t):
    col = lax.broadcasted_iota(jnp.int32, (1, MAIN_W), 1) // POOL_GROUP
    win = jnp.where(col == 0, 2, jnp.where(col == 1, 4, jnp.where(col == 2, 8, 16)))
    return jnp.minimum(t + 1, win).astype(f32)


def pool_fwd(z, wbd, scale, *, name):
    M = z.shape[0]
    tm = 256
    nper = SEQ // tm
    hb = tm // POOL_HALO

    def body(zc_ref, zh_ref, w_ref, s_ref, p_ref, y_ref):
        i = pl.program_id(0)
        seq_blk = i % nper
        halo = jnp.where(seq_blk == 0, 0.0, zh_ref[...])
        u = zc_ref[...]
        ext = jnp.concatenate([halo, u], axis=0)
        a1 = ext + pltpu.roll(ext, 1, 0)
        a2 = a1 + pltpu.roll(a1, 2, 0)
        a3 = a2 + pltpu.roll(a2, 4, 0)
        a4 = a3 + pltpu.roll(a3, 8, 0)
        sums = _pool_select(a1, a2, a3, a4)[POOL_HALO:]
        t = seq_blk * tm + lax.broadcasted_iota(jnp.int32, (tm, 1), 0)
        p = (sums / _pool_count(t) - u).astype(bf16)
        p_ref[...] = p
        y_ref[...] = (jnp.dot(p, w_ref[...], preferred_element_type=f32) * s_ref[...]).astype(bf16)

    return pl.pallas_call(
        body, name=name, grid=(M // tm,),
        in_specs=[pl.BlockSpec((tm, MAIN_W), lambda i: (i, 0)),
                  pl.BlockSpec((POOL_HALO, MAIN_W), lambda i: (jnp.maximum(i * hb - 1, 0), 0)),
                  pl.BlockSpec((MAIN_W, MAIN_W), lambda i: (0, 0)),
                  pl.BlockSpec((1, MAIN_W), lambda i: (0, 0))],
        out_specs=[pl.BlockSpec((tm, MAIN_W), lambda i: (i, 0)),
                   pl.BlockSpec((tm, MAIN_W), lambda i: (i, 0))],
        out_shape=[_sds((M, MAIN_W), bf16), _sds((M, MAIN_W), bf16)],
        compiler_params=_params(("parallel",)),
    )(z, z, wbd, scale)


def pool_bwd(dyc, p, wbd, scale, *, name):
    M = p.shape[0]
    tm = 256
    nper = SEQ // tm
    hb = tm // POOL_HALO
    last_hb = M // POOL_HALO - 1

    def body(dy_ref, dyh_ref, p_ref, w_ref, s_ref, dz_ref, dw_ref, ds_ref):
        i = pl.program_id(0)
        seq_blk = i % nper
        dy = dy_ref[...].astype(f32)
        pv = p_ref[...]
        w = w_ref[...]
        sc = s_ref[...]

        @pl.when(i == 0)
        def _():
            dw_ref[...] = jnp.zeros_like(dw_ref)
            ds_ref[...] = jnp.zeros_like(ds_ref)

        v = jnp.dot(pv, w, preferred_element_type=f32)
        ds_ref[...] += jnp.sum(dy * v, axis=0, keepdims=True)
        dv = (dy * sc).astype(bf16)
        dw_ref[...] += lax.dot_general(pv, dv, TN, preferred_element_type=f32)
        dp = lax.dot_general(dv, w, NT, preferred_element_type=f32)
        dvh = jnp.where(seq_blk == nper - 1, 0.0, dyh_ref[...].astype(f32) * sc).astype(bf16)
        dph = lax.dot_general(dvh, w, NT, preferred_element_type=f32)
        ext = jnp.concatenate([dp, dph], axis=0)
        n = tm + POOL_HALO
        t = seq_blk * tm + lax.broadcasted_iota(jnp.int32, (n, 1), 0)
        e = ext / _pool_count(t)
        b1 = e + pltpu.roll(e, n - 1, 0)
        b2 = b1 + pltpu.roll(b1, n - 2, 0)
        b3 = b2 + pltpu.roll(b2, n - 4, 0)
        b4 = b3 + pltpu.roll(b3, n - 8, 0)
        dz_ref[...] = (_pool_select(b1, b2, b3, b4)[:tm] - dp).astype(dz_ref.dtype)

    return pl.pallas_call(
        body, name=name, grid=(M // tm,),
        in_specs=[pl.BlockSpec((tm, MAIN_W), lambda i: (i, 0)),
                  pl.BlockSpec((POOL_HALO, MAIN_W), lambda i: (jnp.minimum((i + 1) * hb, last_hb), 0)),
                  pl.BlockSpec((tm, MAIN_W), lambda i: (i, 0)),
                  pl.BlockSpec((MAIN_W, MAIN_W), lambda i: (0, 0)),
                  pl.BlockSpec((1, MAIN_W), lambda i: (0, 0))],
        out_specs=[pl.BlockSpec((tm, MAIN_W), lambda i: (i, 0)),
                   pl.BlockSpec((MAIN_W, MAIN_W), lambda i: (0, 0)),
                   pl.BlockSpec((1, MAIN_W), lambda i: (0, 0))],
        out_shape=[_sds((M, MAIN_W), bf16), _sds((MAIN_W, MAIN_W), f32), _sds((1, MAIN_W), f32)],
        compiler_params=_params(("arbitrary",)),
    )(dyc, dyc, p, wbd, scale)


def _mem_probs(q, kv, h):
    hs = slice(h * HEAD_DIM, (h + 1) * HEAD_DIM)
    qh = q[:, hs]
    kh = kv[:, hs]
    s = lax.dot_general(qh, kh, NT, preferred_element_type=f32) * SCALE
    m = jnp.max(s, axis=-1, keepdims=True)
    e = jnp.exp(s - m)
    return qh, kh, e / jnp.sum(e, axis=-1, keepdims=True)


def memattn_fwd(z, kvm, *, name, n_seq):
    M = z.shape[0]
    tq = 512
    nq = SEQ // tq

    def body(q_ref, kv_ref, o_ref):
        q = q_ref[...].astype(bf16)
        kv = kv_ref[...]
        outs = []
        for h in range(N_MEM_HEADS):
            _, _, p = _mem_probs(q, kv, h)
            vh = kv[:, MEM_W + h * HEAD_DIM: MEM_W + (h + 1) * HEAD_DIM]
            outs.append(jnp.dot(p.astype(bf16), vh, preferred_element_type=f32))
        o_ref[...] = jnp.concatenate(outs, axis=1).astype(bf16)

    return pl.pallas_call(
        body, name=name, grid=(n_seq, nq),
        in_specs=[pl.BlockSpec((tq, MEM_W), lambda b, i: (b * nq + i, 3)),
                  pl.BlockSpec((N_MEM, 2 * MEM_W), lambda b, i: (b, 0))],
        out_specs=pl.BlockSpec((tq, MEM_W), lambda b, i: (b * nq + i, 0)),
        out_shape=_sds((M, MEM_W), bf16),
        compiler_params=_params(("parallel", "parallel")),
    )(z, kvm)


def memattn_bwd(z, kvm, dyc, *, name, n_seq):
    M = z.shape[0]
    tq = 512
    nq = SEQ // tq

    def body(q_ref, kv_ref, dy_ref, dq_ref, dkv_ref):
        q = q_ref[...].astype(bf16)
        kv = kv_ref[...]
        dy = dy_ref[...].astype(bf16)
        dqs, dks, dvs = [], [], []
        for h in range(N_MEM_HEADS):
            hs = slice(h * HEAD_DIM, (h + 1) * HEAD_DIM)
            qh, kh, p = _mem_probs(q, kv, h)
            vh = kv[:, MEM_W + h * HEAD_DIM: MEM_W + (h + 1) * HEAD_DIM]
            dyh = dy[:, hs]
            dvs.append(lax.dot_general(p.astype(bf16), dyh, TN, preferred_element_type=f32))
            dp = lax.dot_general(dyh, vh, NT, preferred_element_type=f32)
            ds = (p * (dp - jnp.sum(dp * p, axis=-1, keepdims=True)) * SCALE).astype(bf16)
            dqs.append(jnp.dot(ds, kh, preferred_element_type=f32))
            dks.append(lax.dot_general(ds, qh, TN, preferred_element_type=f32))
        dq_ref[...] = jnp.concatenate(dqs, axis=1).astype(bf16)

        @pl.when(pl.program_id(1) == 0)
        def _():
            dkv_ref[...] = jnp.zeros_like(dkv_ref)

        dkv_ref[...] += jnp.concatenate(dks + dvs, axis=1)

    return pl.pallas_call(
        body, name=name, grid=(n_seq, nq),
        in_specs=[pl.BlockSpec((tq, MEM_W), lambda b, i: (b * nq + i, 3)),
                  pl.BlockSpec((N_MEM, 2 * MEM_W), lambda b, i: (b, 0)),
                  pl.BlockSpec((tq, MEM_W), lambda b, i: (b * nq + i, 3))],
        out_specs=[pl.BlockSpec((tq, MEM_W), lambda b, i: (b * nq + i, 0)),
                   pl.BlockSpec((N_MEM, 2 * MEM_W), lambda b, i: (b, 0))],
        out_shape=[_sds((M, MEM_W), bf16), _sds((n_seq * N_MEM, 2 * MEM_W), f32)],
        compiler_params=_params(("parallel", "arbitrary")),
    )(z, kvm, dyc)


def rope_tables(pos, *, name):
    M = pos.shape[0]
    tm = min(1024, M)
    half = HEAD_DIM // 2
    inv = ROPE_THETA ** (-np.arange(half, dtype=np.float64) / half)
    inv128 = jnp.asarray(np.tile(inv, 4)[None, :], f32)
    sign128 = jnp.asarray(np.tile(np.concatenate([-np.ones(half), np.ones(half)]), 2)[None, :], f32)

    def body(p_ref, f_ref, s_ref, cos_ref, sin_ref):
        ang = p_ref[...] * f_ref[...]
        cos_ref[...] = jnp.cos(ang)
        sin_ref[...] = jnp.sin(ang) * s_ref[...]

    return pl.pallas_call(
        body, name=name, grid=(M // tm,),
        in_specs=[pl.BlockSpec((tm, 1), lambda i: (i, 0)),
                  pl.BlockSpec((1, 128), lambda i: (0, 0)),
                  pl.BlockSpec((1, 128), lambda i: (0, 0))],
        out_specs=[pl.BlockSpec((tm, 128), lambda i: (i, 0)),
                   pl.BlockSpec((tm, 128), lambda i: (i, 0))],
        out_shape=[_sds((M, 128), f32), _sds((M, 128), f32)],
        compiler_params=_params(("parallel",)),
    )(pos, inv128, sign128)


def _swap_halves(x):
    w = x.shape[1]
    first = (lax.broadcasted_iota(jnp.int32, (1, w), 1) % HEAD_DIM) < (HEAD_DIM // 2)
    return jnp.where(first, pltpu.roll(x, w - HEAD_DIM // 2, 1), pltpu.roll(x, HEAD_DIM // 2, 1))


def rope_fwd(src, cos, sin, *, name):
    M = src.shape[0]
    tm = min(512, M)

    def body(x_ref, c_ref, s_ref, o_ref):
        x = x_ref[...].astype(f32)
        c = jnp.tile(c_ref[...], (1, MAIN_W // 128))
        s = jnp.tile(s_ref[...], (1, MAIN_W // 128))
        o_ref[...] = (x * c + _swap_halves(x) * s).astype(bf16)

    return pl.pallas_call(
        body, name=name, grid=(M // tm,),
        in_specs=[pl.BlockSpec((tm, MAIN_W), lambda i: (i, 0)),
                  pl.BlockSpec((tm, 128), lambda i: (i, 0)),
                  pl.BlockSpec((tm, 128), lambda i: (i, 0))],
        out_specs=pl.BlockSpec((tm, MAIN_W), lambda i: (i, 0)),
        out_shape=_sds((M, MAIN_W), bf16),
        compiler_params=_params(("parallel",)),
    )(src, cos, sin)


def group_sum(groups, cos, sin, *, name, rotate):
    M = groups[0][0].shape[0]
    tm = min(512, M)
    counts = [len(g) for g in groups]
    flat = [a for g in groups for a in g]

    def body(*refs):
        part_refs = refs[:len(flat)]
        c_ref, s_ref, o_ref = refs[len(flat):]
        cols, k = [], 0
        for n in counts:
            acc = part_refs[k][...]
            for r in part_refs[k + 1:k + n]:
                acc = acc + r[...]
            cols.append(acc)
            k += n
        d = jnp.concatenate(cols, axis=1)
        if rotate:
            c = jnp.tile(c_ref[...], (1, MAIN_W // 128))
            s = jnp.tile(s_ref[...], (1, MAIN_W // 128))
            d = d * c - _swap_halves(d) * s
        o_ref[...] = d.astype(bf16)

    part = pl.BlockSpec((tm, GROUP_W), lambda i: (i, 0))
    tab = pl.BlockSpec((tm, 128), lambda i: (i, 0))
    return pl.pallas_call(
        body, name=name, grid=(M // tm,),
        in_specs=[part] * len(flat) + [tab, tab],
        out_specs=pl.BlockSpec((tm, MAIN_W), lambda i: (i, 0)),
        out_shape=_sds((M, MAIN_W), bf16),
        compiler_params=_params(("parallel",)),
    )(*flat, cos, sin)


def _band_mask(n):
    i = lax.broadcasted_iota(jnp.int32, (BAND, 2 * BAND), 0)
    j = lax.broadcasted_iota(jnp.int32, (BAND, 2 * BAND), 1)
    return (j >= i) & (j <= i + BAND) & ((n > 0) | (j >= BAND))


def _heads(x):
    return [x[:, h * HEAD_DIM:(h + 1) * HEAD_DIM] for h in range(GROUP_W // HEAD_DIM)]


def _bcast_heads(cols):
    return jnp.concatenate([jnp.broadcast_to(c, (c.shape[0], HEAD_DIM)) for c in cols], axis=1)


def _dil_views(n_seq, dil, q, k, kv):
    L = SEQ // dil
    return (q.reshape(n_seq, L, dil * MAIN_W), k.reshape(n_seq, L, dil * MAIN_W),
            kv.reshape(n_seq, L, dil * 2 * MAIN_W))


def _grp_view(a, n_seq, dil):
    return a.reshape(n_seq, SEQ // dil, dil * GROUP_W)


def dil_fwd(q, k, kv, g, dil, *, name, n_seq):
    M = q.shape[0]
    L = SEQ // dil
    nb = L // BAND
    q3, k3, v3 = _dil_views(n_seq, dil, q, k, kv)

    def body(q_ref, kp_ref, ko_ref, vp_ref, vo_ref, o_ref, l_ref):
        mask = _band_mask(pl.program_id(2))
        kc = jnp.concatenate([kp_ref[...], ko_ref[...]], axis=0)
        vc = jnp.concatenate([vp_ref[...], vo_ref[...]], axis=0)
        outs, lses = [], []
        for qh, kh, vh in zip(_heads(q_ref[...]), _heads(kc), _heads(vc)):
            s = lax.dot_general(qh, kh, NT, preferred_element_type=f32) * SCALE
            s = jnp.where(mask, s, NEG)
            m = jnp.max(s, axis=-1, keepdims=True)
            lse = m + jnp.log(jnp.sum(jnp.exp(s - m), axis=-1, keepdims=True))
            p = jnp.exp(s - lse).astype(bf16)
            outs.append(jnp.dot(p, vh, preferred_element_type=f32))
            lses.append(lse)
        o_ref[...] = jnp.concatenate(outs, axis=1)
        l_ref[...] = _bcast_heads(lses)

    blk = (None, BAND, GROUP_W)
    prev = lambda n: jnp.maximum(n - 1, 0)
    out = pl.BlockSpec(blk, lambda b, r, n: (b, n, r))
    o, lse = pl.pallas_call(
        body, name=name, grid=(n_seq, dil, nb),
        in_specs=[pl.BlockSpec(blk, lambda b, r, n: (b, n, r * 3 + g)),
                  pl.BlockSpec(blk, lambda b, r, n: (b, prev(n), r * 3 + g)),
                  pl.BlockSpec(blk, lambda b, r, n: (b, n, r * 3 + g)),
                  pl.BlockSpec(blk, lambda b, r, n: (b, prev(n), r * 6 + 3 + g)),
                  pl.BlockSpec(blk, lambda b, r, n: (b, n, r * 6 + 3 + g))],
        out_specs=[out, out],
        out_shape=[_sds((n_seq, L, dil * GROUP_W), f32)] * 2,
        compiler_params=_params(("parallel", "parallel", "arbitrary")),
    )(q3, k3, k3, v3, v3)
    return o.reshape(M, GROUP_W), lse.reshape(M, GROUP_W)


def combine_fwd(os_, lses, *, name):
    M = os_[0].shape[0]
    tm = min(512, M)

    def body(o0, o1, o2, l0, l1, l2, y_ref):
        ls = [l0[...], l1[...], l2[...]]
        m = jnp.maximum(jnp.maximum(ls[0], ls[1]), ls[2])
        es = [jnp.exp(l - m) for l in ls]
        inv = 1.0 / (es[0] + es[1] + es[2])
        y_ref[...] = jnp.concatenate([o[...] * e * inv for o, e in zip((o0, o1, o2), es)], axis=1).astype(bf16)

    part = pl.BlockSpec((tm, GROUP_W), lambda i: (i, 0))
    return pl.pallas_call(
        body, name=name, grid=(M // tm,),
        in_specs=[part] * 6,
        out_specs=pl.BlockSpec((tm, MAIN_W), lambda i: (i, 0)),
        out_shape=_sds((M, MAIN_W), bf16),
        compiler_params=_params(("parallel",)),
    )(*os_, *lses)


def combine_bwd(dyc, os_, lses, *, name):
    M = os_[0].shape[0]
    tm = min(512, M)

    def body(dy_ref, o0, o1, o2, l0, l1, l2, d0, d1, d2, c0, c1, c2):
        r = lax.broadcasted_iota(jnp.int32, (GROUP_W, GROUP_W), 0) // HEAD_DIM
        c = lax.broadcasted_iota(jnp.int32, (GROUP_W, GROUP_W), 1) // HEAD_DIM
        ones = (r == c).astype(f32)
        dy = dy_ref[...].astype(f32)
        ls = [l0[...], l1[...], l2[...]]
        m = jnp.maximum(jnp.maximum(ls[0], ls[1]), ls[2])
        es = [jnp.exp(l - m) for l in ls]
        inv = 1.0 / (es[0] + es[1] + es[2])
        total = 0.0
        alphas = []
        for g, (o, e, d_ref) in enumerate(zip((o0, o1, o2), es, (d0, d1, d2))):
            a = e * inv
            dyg = dy[:, g * GROUP_W:(g + 1) * GROUP_W]
            d_ref[...] = (dyg * a).astype(bf16)
            dsum = jnp.dot(dyg * o[...], ones, precision=lax.Precision.HIGHEST, preferred_element_type=f32)
            total = total + a * dsum
            alphas.append(a)
        for a, c_ref in zip(alphas, (c0, c1, c2)):
            c_ref[...] = -a * total

    part = pl.BlockSpec((tm, GROUP_W), lambda i: (i, 0))
    outs = pl.pallas_call(
        body, name=name, grid=(M // tm,),
        in_specs=[pl.BlockSpec((tm, MAIN_W), lambda i: (i, 0))] + [part] * 6,
        out_specs=[part] * 6,
        out_shape=[_sds((M, GROUP_W), bf16)] * 3 + [_sds((M, GROUP_W), f32)] * 3,
        compiler_params=_params(("parallel",)),
    )(dyc, *os_, *lses)
    return outs[:3], outs[3:]


def dil_bwd_dq(q, k, kv, do, cc, lse, g, dil, *, name, n_seq):
    M = q.shape[0]
    L = SEQ // dil
    nb = L // BAND
    q3, k3, v3 = _dil_views(n_seq, dil, q, k, kv)
    do3, c3, l3 = (_grp_view(a, n_seq, dil) for a in (do, cc, lse))

    def body(q_ref, kp_ref, ko_ref, vp_ref, vo_ref, do_ref, c_ref, l_ref, dq_ref):
        mask = _band_mask(pl.program_id(2))
        kc = jnp.concatenate([kp_ref[...], ko_ref[...]], axis=0)
        vc = jnp.concatenate([vp_ref[...], vo_ref[...]], axis=0)
        cv = c_ref[...]
        lv = l_ref[...]
        outs = []
        for h, (qh, kh, vh, doh) in enumerate(zip(_heads(q_ref[...]), _heads(kc), _heads(vc), _heads(do_ref[...]))):
            s = lax.dot_general(qh, kh, NT, preferred_element_type=f32) * SCALE
            p = jnp.exp(jnp.where(mask, s, NEG) - lv[:, h * HEAD_DIM:h * HEAD_DIM + 1])
            dp = lax.dot_general(doh, vh, NT, preferred_element_type=f32)
            ds = (p * (dp + cv[:, h * HEAD_DIM:h * HEAD_DIM + 1]) * SCALE).astype(bf16)
            outs.append(jnp.dot(ds, kh, preferred_element_type=f32))
        dq_ref[...] = jnp.concatenate(outs, axis=1)

    blk = (None, BAND, GROUP_W)
    prev = lambda n: jnp.maximum(n - 1, 0)
    grp = pl.BlockSpec(blk, lambda b, r, n: (b, n, r))
    dq = pl.pallas_call(
        body, name=name, grid=(n_seq, dil, nb),
        in_specs=[pl.BlockSpec(blk, lambda b, r, n: (b, n, r * 3 + g)),
                  pl.BlockSpec(blk, lambda b, r, n: (b, prev(n), r * 3 + g)),
                  pl.BlockSpec(blk, lambda b, r, n: (b, n, r * 3 + g)),
                  pl.BlockSpec(blk, lambda b, r, n: (b, prev(n), r * 6 + 3 + g)),
                  pl.BlockSpec(blk, lambda b, r, n: (b, n, r * 6 + 3 + g)),
                  grp, grp, grp],
        out_specs=grp,
        out_shape=_sds((n_seq, L, dil * GROUP_W), f32),
        compiler_params=_params(("parallel", "parallel", "arbitrary")),
    )(q3, k3, k3, v3, v3, do3, c3, l3)
    return dq.reshape(M, GROUP_W)


def dil_bwd_dkv(q, k, kv, do, cc, lse, g, dil, *, name, n_seq):
    M = q.shape[0]
    L = SEQ // dil
    nb = L // BAND
    q3, k3, v3 = _dil_views(n_seq, dil, q, k, kv)
    do3, c3, l3 = (_grp_view(a, n_seq, dil) for a in (do, cc, lse))

    def body(k_ref, v_ref, qo_ref, qn_ref, doo_ref, don_ref, co_ref, cn_ref, lo_ref, ln_ref, dk_ref, dv_ref):
        n = pl.program_id(2)
        i = lax.broadcasted_iota(jnp.int32, (BAND, BAND), 0)
        j = lax.broadcasted_iota(jnp.int32, (BAND, BAND), 1)
        masks = (j <= i, (j >= i) & (n + 1 < nb))
        dks, dvs = [], []
        for h, (kh, vh) in enumerate(zip(_heads(k_ref[...]), _heads(v_ref[...]))):
            hs = slice(h * HEAD_DIM, (h + 1) * HEAD_DIM)
            h1 = slice(h * HEAD_DIM, h * HEAD_DIM + 1)
            dk = jnp.zeros((BAND, HEAD_DIM), f32)
            dv = jnp.zeros((BAND, HEAD_DIM), f32)
            for mask, q_ref, do_ref, c_ref, l_ref in ((masks[0], qo_ref, doo_ref, co_ref, lo_ref),
                                                      (masks[1], qn_ref, don_ref, cn_ref, ln_ref)):
                qh = q_ref[:, hs]
                doh = do_ref[:, hs]
                s = lax.dot_general(qh, kh, NT, preferred_element_type=f32) * SCALE
                p = jnp.exp(jnp.where(mask, s, NEG) - l_ref[:, h1])
                dv = dv + lax.dot_general(p.astype(bf16), doh, TN, preferred_element_type=f32)
                dp = lax.dot_general(doh, vh, NT, preferred_element_type=f32)
                ds = (p * (dp + c_ref[:, h1]) * SCALE).astype(bf16)
                dk = dk + lax.dot_general(ds, qh, TN, preferred_element_type=f32)
            dks.append(dk)
            dvs.append(dv)
        dk_ref[...] = jnp.concatenate(dks, axis=1)
        dv_ref[...] = jnp.concatenate(dvs, axis=1)

    blk = (None, BAND, GROUP_W)
    nxt = lambda n: jnp.minimum(n + 1, nb - 1)
    own = pl.BlockSpec(blk, lambda b, r, n: (b, n, r))
    nx = pl.BlockSpec(blk, lambda b, r, n: (b, nxt(n), r))
    dk, dv = pl.pallas_call(
        body, name=name, grid=(n_seq, dil, nb),
        in_specs=[pl.BlockSpec(blk, lambda b, r, n: (b, n, r * 3 + g)),
                  pl.BlockSpec(blk, lambda b, r, n: (b, n, r * 6 + 3 + g)),
                  pl.BlockSpec(blk, lambda b, r, n: (b, n, r * 3 + g)),
                  pl.BlockSpec(blk, lambda b, r, n: (b, nxt(n), r * 3 + g)),
                  own, nx, own, nx, own, nx],
        out_specs=[own, own],
        out_shape=[_sds((n_seq, L, dil * GROUP_W), f32)] * 2,
        compiler_params=_params(("parallel", "parallel", "arbitrary")),
    )(k3, v3, q3, q3, do3, do3, c3, c3, l3, l3)
    return dk.reshape(M, GROUP_W), dv.reshape(M, GROUP_W)


def _blockdiag(wp):
    out = jnp.zeros((MAIN_W, MAIN_W), wp.dtype)
    for gi in range(len(POOL_WINDOWS)):
        sl = slice(gi * POOL_GROUP, (gi + 1) * POOL_GROUP)
        out = out.at[sl, sl].set(wp[gi])
    return out


def _unblockdiag(w):
    return jnp.stack([w[gi * POOL_GROUP:(gi + 1) * POOL_GROUP, gi * POOL_GROUP:(gi + 1) * POOL_GROUP]
                      for gi in range(len(POOL_WINDOWS))])


def local_step(x, mem, positions, target, W):
    n_seq = x.shape[0]
    M = n_seq * SEQ
    xs = x.reshape(M, D_MODEL)
    mems = mem.reshape(n_seq * N_MEM, D_MODEL)
    pos = positions.reshape(M, 1).astype(f32)
    cos, sin = rope_tables(pos, name="rope_tables")
    gains = W["norm_gains"]

    def gain(l, k):
        return gains[l, k].reshape(1, D_MODEL)

    saved = []
    kvs = None
    for l in range(DEPTH):
        sv = {"x": xs}
        z, h1 = rms_matmul(xs, gain(l, 0), W["w_in"][l], name=f"l{l}_in", out_dtype=f32)
        kvm, mn = rms_matmul(mems, W["mem_norm"][l].reshape(1, D_MODEL), W["w_mem_kv"][l],
                             name=f"l{l}_memkv", out_dtype=bf16)
        y_mem = memattn_fwd(z, kvm, name=f"l{l}_memattn", n_seq=n_seq)
        sv.update(z=z, h1=h1, kvm=kvm, mn=mn)
        if l < N_A_LAYERS:
            wbd = _blockdiag(W["w_pool"][l])
            psc = W["pool_scale"][l].reshape(1, MAIN_W)
            p, y_main = pool_fwd(z, wbd, psc, name=f"l{l}_pool")
            sv.update(p=p, wbd=wbd, psc=psc)
        else:
            qrot = rope_fwd(z, cos, sin, name=f"l{l}_ropeq")
            os_, lses = [], []
            for g, (_, dil) in enumerate(DIL_PATTERNS):
                o, lse = dil_fwd(qrot, kvs["krot"], kvs["kv"], g, dil, name=f"l{l}_dil{g}", n_seq=n_seq)
                os_.append(o)
                lses.append(lse)
            y_main = combine_fwd(os_, lses, name=f"l{l}_comb")
            sv.update(qrot=qrot, os=os_, lses=lses)
        ycat = jnp.concatenate([y_main, y_mem], axis=1)
        y, x1 = matmul_rms_res(ycat, W["w_out"][l], gain(l, 1), xs, name=f"l{l}_out")
        gu, h2 = rms_matmul(x1, gain(l, 2), W["w_gate_up"][l], name=f"l{l}_gu", out_dtype=bf16)
        a = swiglu_fwd(gu, name=f"l{l}_swiglu")
        y2, x2 = matmul_rms_res(a, W["w_down"][l], gain(l, 3), x1, name=f"l{l}_down")
        sv.update(ycat=ycat, y=y, x1=x1, gu=gu, h2=h2, a=a, y2=y2)
        saved.append(sv)
        xs = x2
        if l == N_A_LAYERS - 1:
            kv, hkv = rms_matmul(xs, W["kv_norm"].reshape(1, D_MODEL), W["w_kv"], name="kv_proj", out_dtype=bf16)
            krot = rope_fwd(kv, cos, sin, name="ropek")
            kvs = {"kv": kv, "hkv": hkv, "krot": krot, "x": xs}

    dx, sq = loss_head(xs, target.reshape(M, D_MODEL), name="loss_head")

    G = {k: [None] * DEPTH for k in ("w_in", "w_mem_kv", "w_out", "w_gate_up", "w_down", "mem_norm")}
    G["norm_gains"] = [[None] * 4 for _ in range(DEPTH)]
    G["w_pool"] = [None] * N_A_LAYERS
    G["pool_scale"] = [None] * N_A_LAYERS
    dk_parts = [[] for _ in range(N_GROUPS)]
    dv_parts = [[] for _ in range(N_GROUPS)]

    for l in reversed(range(DEPTH)):
        sv = saved[l]
        dy2, G["norm_gains"][l][3] = rms_bwd(sv["y2"], gain(l, 3), dx, None, name=f"l{l}_b_n3", out_dtype=bf16)
        da = matmul(dy2, W["w_down"][l], name=f"l{l}_b_da", mode="nt", out_dtype=bf16)
        G["w_down"][l] = matmul(sv["a"], dy2, name=f"l{l}_b_wd", mode="tn", out_dtype=f32)
        dgu = swiglu_bwd(sv["gu"], da, name=f"l{l}_b_swiglu")
        dh2 = matmul(dgu, W["w_gate_up"][l], name=f"l{l}_b_dh2", mode="nt", out_dtype=bf16)
        G["w_gate_up"][l] = matmul(sv["h2"], dgu, name=f"l{l}_b_wgu", mode="tn", out_dtype=f32)
        dx1, G["norm_gains"][l][2] = rms_bwd(sv["x1"], gain(l, 2), dh2, dx, name=f"l{l}_b_n2", out_dtype=f32)
        dy, G["norm_gains"][l][1] = rms_bwd(sv["y"], gain(l, 1), dx1, None, name=f"l{l}_b_n1", out_dtype=bf16)
        dycat = matmul(dy, W["w_out"][l], name=f"l{l}_b_dycat", mode="nt", out_dtype=bf16)
        G["w_out"][l] = matmul(sv["ycat"], dy, name=f"l{l}_b_wout", mode="tn", out_dtype=f32)
        dq_mem, dkvm = memattn_bwd(sv["z"], sv["kvm"], dycat, name=f"l{l}_b_memattn", n_seq=n_seq)
        dmn = matmul(dkvm, W["w_mem_kv"][l], name=f"l{l}_b_dmn", mode="nt", out_dtype=bf16)
        G["w_mem_kv"][l] = matmul(sv["mn"], dkvm, name=f"l{l}_b_wmkv", mode="tn", out_dtype=f32)
        _, G["mem_norm"][l] = rms_bwd(mems, W["mem_norm"][l].reshape(1, D_MODEL), dmn, None,
                                      name=f"l{l}_b_nmem", out_dtype=bf16)
        if l < N_A_LAYERS:
            dz_main, dwbd, dps = pool_bwd(dycat, sv["p"], sv["wbd"], sv["psc"], name=f"l{l}_b_pool")
            G["w_pool"][l] = _unblockdiag(dwbd)
            G["pool_scale"][l] = dps.reshape(MAIN_W)
        else:
            dos, ccs = combine_bwd(dycat, sv["os"], sv["lses"], name=f"l{l}_b_comb")
            dqs = []
            for g, (_, dil) in enumerate(DIL_PATTERNS):
                args = (sv["qrot"], kvs["krot"], kvs["kv"], dos[g], ccs[g], sv["lses"][g], g, dil)
                dqs.append([dil_bwd_dq(*args, name=f"l{l}_b_dq{g}", n_seq=n_seq)])
                dk, dv = dil_bwd_dkv(*args, name=f"l{l}_b_dkv{g}", n_seq=n_seq)
                dk_parts[g].append(dk)
                dv_parts[g].append(dv)
            dz_main = group_sum(dqs, cos, sin, name=f"l{l}_b_ropeq", rotate=True)
        dz = jnp.concatenate([dz_main, dq_mem], axis=1)
        dh1 = matmul(dz, W["w_in"][l], name=f"l{l}_b_dh1", mode="nt", out_dtype=bf16)
        G["w_in"][l] = matmul(sv["h1"], dz, name=f"l{l}_b_win", mode="tn", out_dtype=f32)
        dx, G["norm_gains"][l][0] = rms_bwd(sv["x"], gain(l, 0), dh1, dx1, name=f"l{l}_b_n0", out_dtype=f32)
        if l == N_A_LAYERS:
            dkr = group_sum(dk_parts, cos, sin, name="b_ropek", rotate=True)
            dvs = group_sum(dv_parts, cos, sin, name="b_sumv", rotate=False)
            dkv = jnp.concatenate([dkr, dvs], axis=1)
            dhkv = matmul(dkv, W["w_kv"], name="b_dhkv", mode="nt", out_dtype=bf16)
            G["w_kv"] = matmul(kvs["hkv"], dkv, name="b_wkv", mode="tn", out_dtype=f32)
            dx, gkn = rms_bwd(kvs["x"], W["kv_norm"].reshape(1, D_MODEL), dhkv, dx, name="b_nkv", out_dtype=f32)
            G["kv_norm"] = gkn.reshape(D_MODEL)

    grads = {k: jnp.stack(G[k]) for k in ("w_in", "w_mem_kv", "w_out", "w_gate_up", "w_down", "w_pool", "pool_scale")}
    grads["mem_norm"] = jnp.concatenate(G["mem_norm"], axis=0)
    grads["norm_gains"] = jnp.stack([jnp.concatenate(r, axis=0) for r in G["norm_gains"]])
    grads["w_kv"] = G["w_kv"]
    grads["kv_norm"] = G["kv_norm"]
    return sq[0, 0], dx.reshape(n_seq, SEQ, D_MODEL), grads


def exchange(items, *, name):
    n = len(items)
    kinds = [k for _, k in items]
    arrays = [a for a, _ in items]
    out_shapes = [_sds((N_DEV,) + a.shape if k == "gather" else a.shape, a.dtype) for a, k in items]

    def body(*refs):
        in_refs = refs[:n]
        out_refs = refs[n:2 * n]
        send_sems, recv_sems, local_sems = refs[2 * n:]
        x, y, c = lax.axis_index("x"), lax.axis_index("y"), lax.axis_index("c")
        me = 4 * x + 2 * y + c

        def peer(k):
            px = 1 - x if k & 4 else x
            py = 1 - y if k & 2 else y
            pc = 1 - c if k & 1 else c
            return (px, py, pc), 4 * px + 2 * py + pc

        def src_for(i, idx):
            return in_refs[i] if kinds[i] == "gather" else in_refs[i].at[idx]

        local = [pltpu.make_async_copy(src_for(i, me), out_refs[i].at[me], local_sems.at[i]) for i in range(n)]
        for cp in local:
            cp.start()
        sends = []
        for k in range(1, N_DEV):
            dev, idx = peer(k)
            for i in range(n):
                cp = pltpu.make_async_remote_copy(
                    src_ref=src_for(i, idx), dst_ref=out_refs[i].at[me],
                    send_sem=send_sems.at[i, k - 1], recv_sem=recv_sems.at[i, k - 1],
                    device_id=dev, device_id_type=MESH)
                cp.start()
                sends.append(cp)
        for k in range(1, N_DEV):
            dev, idx = peer(k)
            for i in range(n):
                pltpu.make_async_remote_copy(
                    src_ref=src_for(i, idx), dst_ref=out_refs[i].at[idx],
                    send_sem=send_sems.at[i, k - 1], recv_sem=recv_sems.at[i, k - 1],
                    device_id=dev, device_id_type=MESH).wait_recv()
        for cp in sends:
            cp.wait_send()
        for cp in local:
            cp.wait()

    any_spec = pl.BlockSpec(memory_space=pl.ANY)
    return pl.pallas_call(
        body, name=name,
        in_specs=[any_spec] * n, out_specs=[any_spec] * n, out_shape=out_shapes,
        scratch_shapes=[pltpu.SemaphoreType.DMA((n, N_DEV - 1)), pltpu.SemaphoreType.DMA((n, N_DEV - 1)),
                        pltpu.SemaphoreType.DMA((n,))],
    )(*arrays)


def adamw(slots, w, m, v, *, name):
    R, C = w.shape
    tr = _tile(R, (256, 128, 64, 32, 16, 8))
    c1 = 1.0 - ADAM_B1 ** ADAM_STEP
    c2 = 1.0 - ADAM_B2 ** ADAM_STEP

    def body(s_ref, w_ref, m_ref, v_ref, g_ref, d_ref, m2_ref, v2_ref):
        g = s_ref[0]
        for d in range(1, N_DEV):
            g = g + s_ref[d]
        m2 = ADAM_B1 * m_ref[...] + (1.0 - ADAM_B1) * g
        v2 = ADAM_B2 * v_ref[...] + (1.0 - ADAM_B2) * (g * g)
        g_ref[...] = g
        m2_ref[...] = m2
        v2_ref[...] = v2
        d_ref[...] = -ADAM_LR * ((m2 / c1) / (jnp.sqrt(v2 / c2) + ADAM_EPS) + ADAM_WD * w_ref[...])

    blk = pl.BlockSpec((tr, C), lambda i: (i, 0))
    return pl.pallas_call(
        body, name=name, grid=(R // tr,),
        in_specs=[pl.BlockSpec((N_DEV, tr, C), lambda i: (0, i, 0)), blk, blk, blk],
        out_specs=[blk] * 4,
        out_shape=[_sds((R, C), f32)] * 4,
        compiler_params=_params(("parallel",)),
    )(slots, w, m, v)


WEIGHTS = ("norm_gains", "mem_norm", "w_in", "w_mem_kv", "w_out", "w_pool", "pool_scale", "kv_norm", "w_kv",
           "w_gate_up", "w_down")
SHARD_AXIS = {"norm_gains": 2, "w_in": 1, "w_mem_kv": 1, "w_out": 1, "pool_scale": 1, "w_kv": 1, "w_gate_up": 2,
              "w_down": 1}
BIG = ("w_in", "w_mem_kv", "w_out", "w_kv", "w_gate_up", "w_down")
SMALL_ROWS = 24
POOL_SHARD = MAIN_W // N_DEV


def _to_full(name, gathered):
    ax = SHARD_AXIS[name]
    t = jnp.moveaxis(gathered, 0, ax)
    return t.reshape(t.shape[:ax] + (t.shape[ax] * t.shape[ax + 1],) + t.shape[ax + 2:])


def _to_shards(name, full):
    ax = SHARD_AXIS[name]
    t = full.reshape(full.shape[:ax] + (N_DEV, full.shape[ax] // N_DEV) + full.shape[ax + 1:])
    return jnp.moveaxis(t, ax, 0)


def _pack_small(gains, pscale):
    lead = gains.shape[:-3]
    g = gains.reshape(lead + (16, 128))
    p = jnp.zeros(lead + (8, 128), f32).at[..., :2, :POOL_SHARD].set(pscale)
    return jnp.concatenate([g, p], axis=-2)


def _unpack_small(a):
    return a[:16].reshape(4, 4, 128), a[16:18, :POOL_SHARD]


def _pack_repl(mem_norm, kv_norm):
    return jnp.concatenate([mem_norm, kv_norm.reshape(1, D_MODEL), jnp.zeros((3, D_MODEL), f32)], axis=0)


def _unpack_repl(a):
    return a[:4], a[4]


def kernel(x, mem, positions, norm_gains, mem_norm, w_in, w_mem_kv, w_out, w_pool, pool_scale, kv_norm, w_kv, w_gate_up, w_down, loss_target, m_norm_gains, m_mem_norm, m_w_in, m_w_mem_kv, m_w_out, m_w_pool, m_pool_scale, m_kv_norm, m_w_kv, m_w_gate_up, m_w_down, v_norm_gains, v_mem_norm, v_w_in, v_w_mem_kv, v_w_out, v_w_pool, v_pool_scale, v_kv_norm, v_w_kv, v_w_gate_up, v_w_down):
    w = dict(norm_gains=norm_gains, mem_norm=mem_norm, w_in=w_in, w_mem_kv=w_mem_kv, w_out=w_out, w_pool=w_pool,
             pool_scale=pool_scale, kv_norm=kv_norm, w_kv=w_kv, w_gate_up=w_gate_up, w_down=w_down)
    m = dict(norm_gains=m_norm_gains, mem_norm=m_mem_norm, w_in=m_w_in, w_mem_kv=m_w_mem_kv, w_out=m_w_out,
             w_pool=m_w_pool, pool_scale=m_pool_scale, kv_norm=m_kv_norm, w_kv=m_w_kv, w_gate_up=m_w_gate_up,
             w_down=m_w_down)
    v = dict(norm_gains=v_norm_gains, mem_norm=v_mem_norm, w_in=v_w_in, w_mem_kv=v_w_mem_kv, w_out=v_w_out,
             w_pool=v_w_pool, pool_scale=v_pool_scale, kv_norm=v_kv_norm, w_kv=v_w_kv, w_gate_up=v_w_gate_up,
             w_down=v_w_down)

    small = _pack_small(norm_gains, pool_scale)
    got = exchange([(w[k].astype(bf16), "gather") for k in BIG] + [(small, "gather")], name="gather_weights")
    W = {k: _to_full(k, a) for k, a in zip(BIG, got)}
    W["norm_gains"] = _to_full("norm_gains", got[-1][:, :16].reshape(N_DEV, 4, 4, 128))
    W["pool_scale"] = _to_full("pool_scale", got[-1][:, 16:18, :POOL_SHARD])
    W["mem_norm"] = mem_norm
    W["kv_norm"] = kv_norm
    W["w_pool"] = w_pool.astype(bf16)

    sq, grad_x, G = local_step(x, mem, positions, loss_target, W)
    loss = lax.psum(0.5 * sq / D_MODEL, ("x", "y", "c"))

    gsmall = _pack_small(_to_shards("norm_gains", G["norm_gains"]), _to_shards("pool_scale", G["pool_scale"]))
    grepl = _pack_repl(G["mem_norm"], G["kv_norm"])
    gpool = G["w_pool"].reshape(-1, POOL_GROUP)
    parts = exchange([(_to_shards(k, G[k]), "scatter") for k in BIG]
                     + [(gsmall, "scatter"), (grepl, "gather"), (gpool, "gather")], name="exchange_grads")

    def two_d(a):
        return a.reshape(-1, a.shape[-1])

    out = {}
    for k, slots in zip(BIG, parts):
        res = adamw(slots.reshape(N_DEV, -1, slots.shape[-1]), two_d(w[k]), two_d(m[k]), two_d(v[k]), name=f"adamw_{k}")
        out[k] = [r.reshape(w[k].shape) for r in res]
    res = adamw(parts[len(BIG)], small, _pack_small(m_norm_gains, m_pool_scale), _pack_small(v_norm_gains, v_pool_scale),
                name="adamw_small")
    out["norm_gains"], out["pool_scale"] = zip(*[_unpack_small(r) for r in res])
    res = adamw(parts[len(BIG) + 1], _pack_repl(mem_norm, kv_norm), _pack_repl(m_mem_norm, m_kv_norm),
                _pack_repl(v_mem_norm, v_kv_norm), name="adamw_repl")
    out["mem_norm"], out["kv_norm"] = zip(*[_unpack_repl(r) for r in res])
    res = adamw(parts[len(BIG) + 2], two_d(w_pool), two_d(m_w_pool), two_d(v_w_pool), name="adamw_w_pool")
    out["w_pool"] = [r.reshape(w_pool.shape) for r in res]

    return (loss, grad_x, *[out[k][0] for k in WEIGHTS], *[out[k][1] for k in WEIGHTS],
            *[out[k][2] for k in WEIGHTS], *[out[k][3] for k in WEIGHTS])
```

```python
import functools
import math

import numpy as np
import jax
import jax.numpy as jnp
from jax import lax
from jax.experimental import pallas as pl
from jax.experimental.pallas import tpu as pltpu

f32 = jnp.float32
bf16 = jnp.bfloat16

D_MODEL = 1024
SEQ = 2048
DEPTH = 4
N_MEM = 256
HEAD_DIM = 64
N_MEM_HEADS = 4
MEM_W = 256
MAIN_W = 768
POOL_WINDOWS = (2, 4, 8, 16)
POOL_GROUP = 192
POOL_HALO = 16
DIL_PATTERNS = ((128, 1), (512, 4), (2048, 16))
N_GROUPS = 3
GROUP_W = 256
BAND = 128
N_A_LAYERS = 2
D_FF = 2816
ROPE_THETA = 10000.0
EPS = 1e-6
NEG = -1e30
SCALE = HEAD_DIM ** -0.5
N_DEV = 8

ADAM_LR = 0.001
ADAM_B1 = 0.9
ADAM_B2 = 0.999
ADAM_EPS = 1e-08
ADAM_WD = 0.01
ADAM_STEP = 10

VMEM_LIMIT_BYTES = 56 * 1024 * 1024
MESH = pl.DeviceIdType.MESH

NT = (((1,), (1,)), ((), ()))
TN = (((0,), (0,)), ((), ()))


def _params(sem=None):
    return pltpu.CompilerParams(dimension_semantics=sem, vmem_limit_bytes=VMEM_LIMIT_BYTES)


def _tile(n, cands):
    for c in cands:
        if n % c == 0:
            return c
    return n


def _sds(shape, dtype):
    return jax.ShapeDtypeStruct(tuple(shape), dtype)


def _rms_r(v):
    return lax.rsqrt(jnp.mean(v * v, axis=-1, keepdims=True) + EPS)


def rms_matmul(x, gain, w, *, name, out_dtype, after=None):
    M, K = x.shape
    tm = min(512, M)
    sharded = w.ndim == 3
    order = [] if after is None else [after]

    def body(x_ref, g_ref, w_ref, *refs):
        z_ref, h_ref = refs[len(order):]

        @pl.when(pl.program_id(1) == 0)
        def _():
            xv = x_ref[...]
            h_ref[...] = (xv * _rms_r(xv) * g_ref[...]).astype(bf16)

        z_ref[...] = jnp.dot(h_ref[...], w_ref[...], preferred_element_type=f32).astype(z_ref.dtype)

    if sharded:
        nj, _, C = w.shape
        w_spec = pl.BlockSpec((None, K, C), lambda i, j: (j, 0, 0))
        z_spec = pl.BlockSpec((None, tm, C), lambda i, j: (j, i, 0))
        z_shape = _sds((nj, M, C), out_dtype)
    else:
        N = w.shape[1]
        tn = _tile(N, (512, 256, 128))
        nj = N // tn
        w_spec = pl.BlockSpec((K, tn), lambda i, j: (0, j))
        z_spec = pl.BlockSpec((tm, tn), lambda i, j: (i, j))
        z_shape = _sds((M, N), out_dtype)
    return pl.pallas_call(
        body, name=name, grid=(M // tm, nj),
        in_specs=[pl.BlockSpec((tm, K), lambda i, j: (i, 0)),
                  pl.BlockSpec((1, K), lambda i, j: (0, 0)),
                  w_spec] + [pl.BlockSpec(memory_space=pl.ANY)] * len(order),
        out_specs=[z_spec, pl.BlockSpec((tm, K), lambda i, j: (i, 0))],
        out_shape=[z_shape, _sds((M, K), bf16)],
        compiler_params=_params(("parallel", "arbitrary")),
    )(x, gain, w, *order)


def matmul_rms_res(a, w, gain, res, *, name):
    if a.ndim == 2:
        a = a[None]
    nk, M, C = a.shape
    N = w.shape[1]
    tm = min(512, M)

    def body(a_ref, w_ref, g_ref, r_ref, y_ref, x_ref, acc_ref):
        k = pl.program_id(1)
        part = jnp.dot(a_ref[...], w_ref[...], preferred_element_type=f32)

        @pl.when(k == 0)
        def _():
            acc_ref[...] = part

        @pl.when(k > 0)
        def _():
            acc_ref[...] += part

        @pl.when(k == nk - 1)
        def _():
            y = acc_ref[...]
            y_ref[...] = y.astype(bf16)
            x_ref[...] = r_ref[...] + y * _rms_r(y) * g_ref[...]

    row = pl.BlockSpec((tm, N), lambda i, k: (i, 0))
    return pl.pallas_call(
        body, name=name, grid=(M // tm, nk),
        in_specs=[pl.BlockSpec((None, tm, C), lambda i, k: (k, i, 0)),
                  pl.BlockSpec((C, N), lambda i, k: (k, 0)),
                  pl.BlockSpec((1, N), lambda i, k: (0, 0)),
                  row],
        out_specs=[row, row],
        out_shape=[_sds((M, N), bf16), _sds((M, N), f32)],
        scratch_shapes=[pltpu.VMEM((tm, N), f32)],
        compiler_params=_params(("parallel", "arbitrary")),
    )(a, w, gain, res)


def _mm_call(a, b, dims, grid, a_spec, b_spec, o_spec, o_shape, acc_shape, *, name):
    nk = grid[2]

    def body(a_ref, b_ref, o_ref, acc_ref):
        k = pl.program_id(2)
        part = lax.dot_general(a_ref[...].astype(bf16), b_ref[...].astype(bf16), dims, preferred_element_type=f32)

        @pl.when(k == 0)
        def _():
            acc_ref[...] = part

        @pl.when(k > 0)
        def _():
            acc_ref[...] += part

        @pl.when(k == nk - 1)
        def _():
            o_ref[...] = acc_ref[...].astype(o_ref.dtype)

    return pl.pallas_call(
        body, name=name, grid=grid, in_specs=[a_spec, b_spec], out_specs=o_spec, out_shape=o_shape,
        scratch_shapes=[pltpu.VMEM(acc_shape, f32)],
        compiler_params=_params(("parallel", "parallel", "arbitrary")),
    )(a, b)


def _ktile(K):
    return K if K <= 2048 else _tile(K, (2048, 1408, 1024, 512))


def matmul_nt(a, b, *, name, out_dtype, out_chunk=None):
    if a.ndim == 3:
        nk, M, C = a.shape
        N = b.shape[1]
        tm, tn = _tile(M, (512, 256, 128)), _tile(N, (512, 256, 128))
        return _mm_call(a, b, NT, (M // tm, N // tn, nk),
                        pl.BlockSpec((None, tm, C), lambda i, j, k: (k, i, 0)),
                        pl.BlockSpec((None, tn, C), lambda i, j, k: (k, j, 0)),
                        pl.BlockSpec((tm, tn), lambda i, j, k: (i, j)),
                        _sds((M, N), out_dtype), (tm, tn), name=name)
    M, K = a.shape
    N = b.shape[0]
    tm, tk = _tile(M, (512, 256, 128)), _ktile(K)
    a_spec = pl.BlockSpec((tm, tk), lambda i, j, k: (i, k))
    if out_chunk:
        tn = out_chunk
        o_spec = pl.BlockSpec((None, tm, tn), lambda i, j, k: (j, i, 0))
        o_shape = _sds((N // tn, M, tn), out_dtype)
    else:
        tn = _tile(N, (512, 256, 128))
        o_spec = pl.BlockSpec((tm, tn), lambda i, j, k: (i, j))
        o_shape = _sds((M, N), out_dtype)
    return _mm_call(a, b, NT, (M // tm, N // tn, K // tk), a_spec,
                    pl.BlockSpec((tn, tk), lambda i, j, k: (j, k)), o_spec, o_shape, (tm, tn), name=name)


def matmul_tn(a, b, *, name, out_dtype):
    K = a.shape[-2]
    tk = _tile(K, (1024, 512))
    if a.ndim == 3:
        nch, _, tm = a.shape
        M = nch * tm
        a_spec = pl.BlockSpec((None, tk, tm), lambda i, j, k: (i, k, 0))
    else:
        M = a.shape[1]
        tm = _tile(M, (512, 256, 128))
        a_spec = pl.BlockSpec((tk, tm), lambda i, j, k: (k, i))
    if b.ndim == 3:
        nj, _, tn = b.shape
        b_spec = pl.BlockSpec((None, tk, tn), lambda i, j, k: (j, k, 0))
        o_spec = pl.BlockSpec((None, tm, tn), lambda i, j, k: (j, i, 0))
        o_shape = _sds((nj, M, tn), out_dtype)
    else:
        N = b.shape[1]
        tn = _tile(N, (512, 256, 128))
        nj = N // tn
        b_spec = pl.BlockSpec((tk, tn), lambda i, j, k: (k, j))
        o_spec = pl.BlockSpec((tm, tn), lambda i, j, k: (i, j))
        o_shape = _sds((M, N), out_dtype)
    return _mm_call(a, b, TN, (M // tm, nj, K // tk), a_spec, b_spec, o_spec, o_shape, (tm, tn), name=name)


def swiglu_fwd(gu, *, name):
    _, M, C = gu.shape
    tm = min(512, M)
    half = N_DEV // 2

    def body(gu_ref, a_ref):
        g = gu_ref[0].astype(f32)
        s = 1.0 / (1.0 + jnp.exp(-g))
        a_ref[...] = (g * s * gu_ref[1].astype(f32)).astype(bf16)

    return pl.pallas_call(
        body, name=name, grid=(M // tm, half),
        in_specs=[pl.BlockSpec((2, None, tm, C), lambda i, k: (0, k, i, 0))],
        out_specs=pl.BlockSpec((None, tm, C), lambda i, k: (k, i, 0)),
        out_shape=_sds((half, M, C), bf16),
        compiler_params=_params(("parallel", "parallel")),
    )(gu.reshape(2, half, M, C))


def swiglu_bwd(gu, da, *, name):
    _, M, C = gu.shape
    tm = min(512, M)
    half = N_DEV // 2

    def body(gu_ref, da_ref, o_ref):
        g = gu_ref[0].astype(f32)
        u = gu_ref[1].astype(f32)
        da = da_ref[...].astype(f32)
        s = 1.0 / (1.0 + jnp.exp(-g))
        o_ref[0] = (da * u * s * (1.0 + g * (1.0 - s))).astype(bf16)
        o_ref[1] = (da * g * s).astype(bf16)

    pair = pl.BlockSpec((2, None, tm, C), lambda i, k: (0, k, i, 0))
    out = pl.pallas_call(
        body, name=name, grid=(M // tm, half),
        in_specs=[pair, pl.BlockSpec((None, tm, C), lambda i, k: (k, i, 0))],
        out_specs=pair,
        out_shape=_sds((2, half, M, C), bf16),
        compiler_params=_params(("parallel", "parallel")),
    )(gu.reshape(2, half, M, C), da)
    return out.reshape(N_DEV, M, C)


def rms_bwd(y, gain, dn, res, *, name, out_dtype, after=None):
    M, N = y.shape
    tm = min(512, M)
    has_res = res is not None
    order = [] if after is None else [after]

    def body(*refs):
        y_ref, g_ref, dn_ref = refs[:3]
        r_ref = refs[3] if has_res else None
        dy_ref, dg_ref = refs[-2:]
        yv = y_ref[...].astype(f32)
        dn = dn_ref[...].astype(f32)
        r = _rms_r(yv)
        q = dn * g_ref[...]
        dy = r * q - yv * (r * r * r) * jnp.mean(q * yv, axis=-1, keepdims=True)
        if has_res:
            dy = dy + r_ref[...]
        dy_ref[...] = dy.astype(dy_ref.dtype)

        @pl.when(pl.program_id(0) == 0)
        def _():
            dg_ref[...] = jnp.zeros_like(dg_ref)

        dg_ref[...] += jnp.sum(dn * yv * r, axis=0, keepdims=True)

    row = pl.BlockSpec((tm, N), lambda i: (i, 0))
    vec = pl.BlockSpec((1, N), lambda i: (0, 0))
    args = [y, gain, dn] + ([res] if has_res else []) + order
    return pl.pallas_call(
        body, name=name, grid=(M // tm,),
        in_specs=[row, vec, row] + ([row] if has_res else []) + [pl.BlockSpec(memory_space=pl.ANY)] * len(order),
        out_specs=[row, vec],
        out_shape=[_sds((M, N), out_dtype), _sds((1, N), f32)],
        compiler_params=_params(("arbitrary",)),
    )(*args)


def loss_head(x, target, *, name):
    M, N = x.shape
    tm = min(512, M)

    def body(x_ref, t_ref, dx_ref, l_ref):
        e = x_ref[...] - t_ref[...]
        dx_ref[...] = e * (1.0 / N)

        @pl.when(pl.program_id(0) == 0)
        def _():
            l_ref[...] = jnp.zeros_like(l_ref)

        l_ref[...] += jnp.sum(jnp.sum(e * e, axis=0, keepdims=True), axis=1, keepdims=True)

    row = pl.BlockSpec((tm, N), lambda i: (i, 0))
    return pl.pallas_call(
        body, name=name, grid=(M // tm,),
        in_specs=[row, row],
        out_specs=[row, pl.BlockSpec((8, 128), lambda i: (0, 0))],
        out_shape=[_sds((M, N), f32), _sds((8, 128), f32)],
        compiler_params=_params(("arbitrary",)),
    )(x, target)


def _pool_select(a1, a2, a3, a4):
    col = lax.broadcasted_iota(jnp.int32, (1, MAIN_W), 1) // POOL_GROUP
    return jnp.where(col == 0, a1, jnp.where(col == 1, a2, jnp.where(col == 2, a3, a4)))


def _pool_count(t):
    col = lax.broadcasted_iota(jnp.int32, (1, MAIN_W), 1) // POOL_GROUP
    win = jnp.where(col == 0, 2, jnp.where(col == 1, 4, jnp.where(col == 2, 8, 16)))
    return jnp.minimum(t + 1, win).astype(f32)


def pool_fwd(z, wbd, scale, *, name):
    M = z.shape[0]
    tm = 256
    nper = SEQ // tm
    hb = tm // POOL_HALO

    def body(zc_ref, zh_ref, w_ref, s_ref, p_ref, y_ref):
        i = pl.program_id(0)
        seq_blk = i % nper
        halo = jnp.where(seq_blk == 0, 0.0, zh_ref[...])
        u = zc_ref[...]
        ext = jnp.concatenate([halo, u], axis=0)
        a1 = ext + pltpu.roll(ext, 1, 0)
        a2 = a1 + pltpu.roll(a1, 2, 0)
        a3 = a2 + pltpu.roll(a2, 4, 0)
        a4 = a3 + pltpu.roll(a3, 8, 0)
        sums = _pool_select(a1, a2, a3, a4)[POOL_HALO:]
        t = seq_blk * tm + lax.broadcasted_iota(jnp.int32, (tm, 1), 0)
        p = (sums / _pool_count(t) - u).astype(bf16)
        p_ref[...] = p
        y_ref[...] = (jnp.dot(p, w_ref[...], preferred_element_type=f32) * s_ref[...]).astype(bf16)

    return pl.pallas_call(
        body, name=name, grid=(M // tm,),
        in_specs=[pl.BlockSpec((tm, MAIN_W), lambda i: (i, 0)),
                  pl.BlockSpec((POOL_HALO, MAIN_W), lambda i: (jnp.maximum(i * hb - 1, 0), 0)),
                  pl.BlockSpec((MAIN_W, MAIN_W), lambda i: (0, 0)),
                  pl.BlockSpec((1, MAIN_W), lambda i: (0, 0))],
        out_specs=[pl.BlockSpec((tm, MAIN_W), lambda i: (i, 0)),
                   pl.BlockSpec((tm, MAIN_W), lambda i: (i, 0))],
        out_shape=[_sds((M, MAIN_W), bf16), _sds((M, MAIN_W), bf16)],
        compiler_params=_params(("parallel",)),
    )(z, z, wbd, scale)


def pool_bwd(dyc, p, wbd, scale, *, name):
    M = p.shape[0]
    tm = 256
    nper = SEQ // tm
    hb = tm // POOL_HALO
    last_hb = M // POOL_HALO - 1

    def body(dy_ref, dyh_ref, p_ref, w_ref, s_ref, dz_ref, dw_ref, ds_ref):
        i = pl.program_id(0)
        seq_blk = i % nper
        dy = dy_ref[...].astype(f32)
        pv = p_ref[...]
        w = w_ref[...]
        sc = s_ref[...]

        @pl.when(i == 0)
        def _():
            dw_ref[...] = jnp.zeros_like(dw_ref)
            ds_ref[...] = jnp.zeros_like(ds_ref)

        v = jnp.dot(pv, w, preferred_element_type=f32)
        ds_ref[...] += jnp.sum(dy * v, axis=0, keepdims=True)
        dv = (dy * sc).astype(bf16)
        dw_ref[...] += lax.dot_general(pv, dv, TN, preferred_element_type=f32)
        dp = lax.dot_general(dv, w, NT, preferred_element_type=f32)
        dvh = jnp.where(seq_blk == nper - 1, 0.0, dyh_ref[...].astype(f32) * sc).astype(bf16)
        dph = lax.dot_general(dvh, w, NT, preferred_element_type=f32)
        ext = jnp.concatenate([dp, dph], axis=0)
        n = tm + POOL_HALO
        t = seq_blk * tm + lax.broadcasted_iota(jnp.int32, (n, 1), 0)
        e = ext / _pool_count(t)
        b1 = e + pltpu.roll(e, n - 1, 0)
        b2 = b1 + pltpu.roll(b1, n - 2, 0)
        b3 = b2 + pltpu.roll(b2, n - 4, 0)
        b4 = b3 + pltpu.roll(b3, n - 8, 0)
        dz_ref[...] = (_pool_select(b1, b2, b3, b4)[:tm] - dp).astype(dz_ref.dtype)

    return pl.pallas_call(
        body, name=name, grid=(M // tm,),
        in_specs=[pl.BlockSpec((tm, MAIN_W), lambda i: (i, 0)),
                  pl.BlockSpec((POOL_HALO, MAIN_W), lambda i: (jnp.minimum((i + 1) * hb, last_hb), 0)),
                  pl.BlockSpec((tm, MAIN_W), lambda i: (i, 0)),
                  pl.BlockSpec((MAIN_W, MAIN_W), lambda i: (0, 0)),
                  pl.BlockSpec((1, MAIN_W), lambda i: (0, 0))],
        out_specs=[pl.BlockSpec((tm, MAIN_W), lambda i: (i, 0)),
                   pl.BlockSpec((MAIN_W, MAIN_W), lambda i: (0, 0)),
                   pl.BlockSpec((1, MAIN_W), lambda i: (0, 0))],
        out_shape=[_sds((M, MAIN_W), bf16), _sds((MAIN_W, MAIN_W), f32), _sds((1, MAIN_W), f32)],
        compiler_params=_params(("arbitrary",)),
    )(dyc, dyc, p, wbd, scale)


def _mem_probs(q, kv, h):
    hs = slice(h * HEAD_DIM, (h + 1) * HEAD_DIM)
    qh = q[:, hs]
    kh = kv[:, hs]
    s = lax.dot_general(qh, kh, NT, preferred_element_type=f32) * SCALE
    m = jnp.max(s, axis=-1, keepdims=True)
    e = jnp.exp(s - m)
    return qh, kh, e / jnp.sum(e, axis=-1, keepdims=True)


def memattn_fwd(z, kvm, *, name, n_seq):
    M = z.shape[0]
    tq = 512
    nq = SEQ // tq

    def body(q_ref, kv_ref, o_ref):
        q = q_ref[...].astype(bf16)
        kv = kv_ref[...]
        outs = []
        for h in range(N_MEM_HEADS):
            _, _, p = _mem_probs(q, kv, h)
            vh = kv[:, MEM_W + h * HEAD_DIM: MEM_W + (h + 1) * HEAD_DIM]
            outs.append(jnp.dot(p.astype(bf16), vh, preferred_element_type=f32))
        o_ref[...] = jnp.concatenate(outs, axis=1).astype(bf16)

    return pl.pallas_call(
        body, name=name, grid=(n_seq, nq),
        in_specs=[pl.BlockSpec((tq, MEM_W), lambda b, i: (b * nq + i, 3)),
                  pl.BlockSpec((N_MEM, 2 * MEM_W), lambda b, i: (b, 0))],
        out_specs=pl.BlockSpec((tq, MEM_W), lambda b, i: (b * nq + i, 0)),
        out_shape=_sds((M, MEM_W), bf16),
        compiler_params=_params(("parallel", "parallel")),
    )(z, kvm)


def memattn_bwd(z, kvm, dyc, *, name, n_seq):
    M = z.shape[0]
    tq = 512
    nq = SEQ // tq

    def body(q_ref, kv_ref, dy_ref, dq_ref, dkv_ref):
        q = q_ref[...].astype(bf16)
        kv = kv_ref[...]
        dy = dy_ref[...].astype(bf16)
        dqs, dks, dvs = [], [], []
        for h in range(N_MEM_HEADS):
            hs = slice(h * HEAD_DIM, (h + 1) * HEAD_DIM)
            qh, kh, p = _mem_probs(q, kv, h)
            vh = kv[:, MEM_W + h * HEAD_DIM: MEM_W + (h + 1) * HEAD_DIM]
            dyh = dy[:, hs]
            dvs.append(lax.dot_general(p.astype(bf16), dyh, TN, preferred_element_type=f32))
            dp = lax.dot_general(dyh, vh, NT, preferred_element_type=f32)
            ds = (p * (dp - jnp.sum(dp * p, axis=-1, keepdims=True)) * SCALE).astype(bf16)
            dqs.append(jnp.dot(ds, kh, preferred_element_type=f32))
            dks.append(lax.dot_general(ds, qh, TN, preferred_element_type=f32))
        dq_ref[...] = jnp.concatenate(dqs, axis=1).astype(bf16)

        @pl.when(pl.program_id(1) == 0)
        def _():
            dkv_ref[...] = jnp.zeros_like(dkv_ref)

        dkv_ref[...] += jnp.concatenate(dks + dvs, axis=1)

    return pl.pallas_call(
        body, name=name, grid=(n_seq, nq),
        in_specs=[pl.BlockSpec((tq, MEM_W), lambda b, i: (b * nq + i, 3)),
                  pl.BlockSpec((N_MEM, 2 * MEM_W), lambda b, i: (b, 0)),
                  pl.BlockSpec((tq, MEM_W), lambda b, i: (b * nq + i, 3))],
        out_specs=[pl.BlockSpec((tq, MEM_W), lambda b, i: (b * nq + i, 0)),
                   pl.BlockSpec((N_MEM, 2 * MEM_W), lambda b, i: (b, 0))],
        out_shape=[_sds((M, MEM_W), bf16), _sds((n_seq * N_MEM, 2 * MEM_W), f32)],
        compiler_params=_params(("parallel", "arbitrary")),
    )(z, kvm, dyc)


def rope_tables(pos, *, name):
    M = pos.shape[0]
    tm = min(1024, M)
    half = HEAD_DIM // 2
    inv = ROPE_THETA ** (-np.arange(half, dtype=np.float64) / half)
    inv128 = jnp.asarray(np.tile(inv, 4)[None, :], f32)
    sign128 = jnp.asarray(np.tile(np.concatenate([-np.ones(half), np.ones(half)]), 2)[None, :], f32)

    def body(p_ref, f_ref, s_ref, cos_ref, sin_ref):
        ang = p_ref[...] * f_ref[...]
        cos_ref[...] = jnp.cos(ang)
        sin_ref[...] = jnp.sin(ang) * s_ref[...]

    return pl.pallas_call(
        body, name=name, grid=(M // tm,),
        in_specs=[pl.BlockSpec((tm, 1), lambda i: (i, 0)),
                  pl.BlockSpec((1, 128), lambda i: (0, 0)),
                  pl.BlockSpec((1, 128), lambda i: (0, 0))],
        out_specs=[pl.BlockSpec((tm, 128), lambda i: (i, 0)),
                   pl.BlockSpec((tm, 128), lambda i: (i, 0))],
        out_shape=[_sds((M, 128), f32), _sds((M, 128), f32)],
        compiler_params=_params(("parallel",)),
    )(pos, inv128, sign128)


def _swap_halves(x):
    w = x.shape[1]
    first = (lax.broadcasted_iota(jnp.int32, (1, w), 1) % HEAD_DIM) < (HEAD_DIM // 2)
    return jnp.where(first, pltpu.roll(x, w - HEAD_DIM // 2, 1), pltpu.roll(x, HEAD_DIM // 2, 1))


def rope_fwd(src, cos, sin, *, name):
    M = src.shape[0]
    tm = min(512, M)

    def body(x_ref, c_ref, s_ref, o_ref):
        x = x_ref[...].astype(f32)
        c = jnp.tile(c_ref[...], (1, MAIN_W // 128))
        s = jnp.tile(s_ref[...], (1, MAIN_W // 128))
        o_ref[...] = (x * c + _swap_halves(x) * s).astype(bf16)

    return pl.pallas_call(
        body, name=name, grid=(M // tm,),
        in_specs=[pl.BlockSpec((tm, MAIN_W), lambda i: (i, 0)),
                  pl.BlockSpec((tm, 128), lambda i: (i, 0)),
                  pl.BlockSpec((tm, 128), lambda i: (i, 0))],
        out_specs=pl.BlockSpec((tm, MAIN_W), lambda i: (i, 0)),
        out_shape=_sds((M, MAIN_W), bf16),
        compiler_params=_params(("parallel",)),
    )(src, cos, sin)


def group_sum(groups, cos, sin, *, name, rotate):
    M = groups[0][0].shape[0]
    tm = min(512, M)
    counts = [len(g) for g in groups]
    flat = [a for g in groups for a in g]

    def body(*refs):
        part_refs = refs[:len(flat)]
        c_ref, s_ref, o_ref = refs[len(flat):]
        cols, k = [], 0
        for n in counts:
            acc = part_refs[k][...]
            for r in part_refs[k + 1:k + n]:
                acc = acc + r[...]
            cols.append(acc)
            k += n
        d = jnp.concatenate(cols, axis=1)
        if rotate:
            c = jnp.tile(c_ref[...], (1, MAIN_W // 128))
            s = jnp.tile(s_ref[...], (1, MAIN_W // 128))
            d = d * c - _swap_halves(d) * s
        o_ref[...] = d.astype(bf16)

    part = pl.BlockSpec((tm, GROUP_W), lambda i: (i, 0))
    tab = pl.BlockSpec((tm, 128), lambda i: (i, 0))
    return pl.pallas_call(
        body, name=name, grid=(M // tm,),
        in_specs=[part] * len(flat) + [tab, tab],
        out_specs=pl.BlockSpec((tm, MAIN_W), lambda i: (i, 0)),
        out_shape=_sds((M, MAIN_W), bf16),
        compiler_params=_params(("parallel",)),
    )(*flat, cos, sin)


def _band_mask(n):
    i = lax.broadcasted_iota(jnp.int32, (BAND, 2 * BAND), 0)
    j = lax.broadcasted_iota(jnp.int32, (BAND, 2 * BAND), 1)
    return (j >= i) & (j <= i + BAND) & ((n > 0) | (j >= BAND))


def _heads(x):
    return [x[:, h * HEAD_DIM:(h + 1) * HEAD_DIM] for h in range(GROUP_W // HEAD_DIM)]


def _bcast_heads(cols):
    return jnp.concatenate([jnp.broadcast_to(c, (c.shape[0], HEAD_DIM)) for c in cols], axis=1)


def _dil_views(n_seq, dil, q, k, kv):
    L = SEQ // dil
    return (q.reshape(n_seq, L, dil * MAIN_W), k.reshape(n_seq, L, dil * MAIN_W),
            kv.reshape(n_seq, L, dil * 2 * MAIN_W))


def _grp_view(a, n_seq, dil):
    return a.reshape(n_seq, SEQ // dil, dil * GROUP_W)


def dil_fwd(q, k, kv, g, dil, *, name, n_seq):
    M = q.shape[0]
    L = SEQ // dil
    nb = L // BAND
    q3, k3, v3 = _dil_views(n_seq, dil, q, k, kv)

    def body(q_ref, kp_ref, ko_ref, vp_ref, vo_ref, o_ref, l_ref):
        mask = _band_mask(pl.program_id(2))
        kc = jnp.concatenate([kp_ref[...], ko_ref[...]], axis=0)
        vc = jnp.concatenate([vp_ref[...], vo_ref[...]], axis=0)
        outs, lses = [], []
        for qh, kh, vh in zip(_heads(q_ref[...]), _heads(kc), _heads(vc)):
            s = lax.dot_general(qh, kh, NT, preferred_element_type=f32) * SCALE
            s = jnp.where(mask, s, NEG)
            m = jnp.max(s, axis=-1, keepdims=True)
            lse = m + jnp.log(jnp.sum(jnp.exp(s - m), axis=-1, keepdims=True))
            p = jnp.exp(s - lse).astype(bf16)
            outs.append(jnp.dot(p, vh, preferred_element_type=f32))
            lses.append(lse)
        o_ref[...] = jnp.concatenate(outs, axis=1)
        l_ref[...] = _bcast_heads(lses)

    blk = (None, BAND, GROUP_W)
    prev = lambda n: jnp.maximum(n - 1, 0)
    out = pl.BlockSpec(blk, lambda b, r, n: (b, n, r))
    o, lse = pl.pallas_call(
        body, name=name, grid=(n_seq, dil, nb),
        in_specs=[pl.BlockSpec(blk, lambda b, r, n: (b, n, r * 3 + g)),
                  pl.BlockSpec(blk, lambda b, r, n: (b, prev(n), r * 3 + g)),
                  pl.BlockSpec(blk, lambda b, r, n: (b, n, r * 3 + g)),
                  pl.BlockSpec(blk, lambda b, r, n: (b, prev(n), r * 6 + 3 + g)),
                  pl.BlockSpec(blk, lambda b, r, n: (b, n, r * 6 + 3 + g))],
        out_specs=[out, out],
        out_shape=[_sds((n_seq, L, dil * GROUP_W), f32)] * 2,
        compiler_params=_params(("parallel", "parallel", "arbitrary")),
    )(q3, k3, k3, v3, v3)
    return o.reshape(M, GROUP_W), lse.reshape(M, GROUP_W)


def combine_fwd(os_, lses, *, name):
    M = os_[0].shape[0]
    tm = min(512, M)

    def body(o0, o1, o2, l0, l1, l2, y_ref):
        ls = [l0[...], l1[...], l2[...]]
        m = jnp.maximum(jnp.maximum(ls[0], ls[1]), ls[2])
        es = [jnp.exp(l - m) for l in ls]
        inv = 1.0 / (es[0] + es[1] + es[2])
        y_ref[...] = jnp.concatenate([o[...] * e * inv for o, e in zip((o0, o1, o2), es)], axis=1).astype(bf16)

    part = pl.BlockSpec((tm, GROUP_W), lambda i: (i, 0))
    return pl.pallas_call(
        body, name=name, grid=(M // tm,),
        in_specs=[part] * 6,
        out_specs=pl.BlockSpec((tm, MAIN_W), lambda i: (i, 0)),
        out_shape=_sds((M, MAIN_W), bf16),
        compiler_params=_params(("parallel",)),
    )(*os_, *lses)


def combine_bwd(dyc, os_, lses, *, name):
    M = os_[0].shape[0]
    tm = min(512, M)

    def body(dy_ref, o0, o1, o2, l0, l1, l2, d0, d1, d2, c0, c1, c2):
        r = lax.broadcasted_iota(jnp.int32, (GROUP_W, GROUP_W), 0) // HEAD_DIM
        c = lax.broadcasted_iota(jnp.int32, (GROUP_W, GROUP_W), 1) // HEAD_DIM
        ones = (r == c).astype(f32)
        dy = dy_ref[...].astype(f32)
        ls = [l0[...], l1[...], l2[...]]
        m = jnp.maximum(jnp.maximum(ls[0], ls[1]), ls[2])
        es = [jnp.exp(l - m) for l in ls]
        inv = 1.0 / (es[0] + es[1] + es[2])
        total = 0.0
        alphas = []
        for g, (o, e, d_ref) in enumerate(zip((o0, o1, o2), es, (d0, d1, d2))):
            a = e * inv
            dyg = dy[:, g * GROUP_W:(g + 1) * GROUP_W]
            d_ref[...] = (dyg * a).astype(bf16)
            dsum = jnp.dot(dyg * o[...], ones, precision=lax.Precision.HIGHEST, preferred_element_type=f32)
            total = total + a * dsum
            alphas.append(a)
        for a, c_ref in zip(alphas, (c0, c1, c2)):
            c_ref[...] = -a * total

    part = pl.BlockSpec((tm, GROUP_W), lambda i: (i, 0))
    outs = pl.pallas_call(
        body, name=name, grid=(M // tm,),
        in_specs=[pl.BlockSpec((tm, MAIN_W), lambda i: (i, 0))] + [part] * 6,
        out_specs=[part] * 6,
        out_shape=[_sds((M, GROUP_W), bf16)] * 3 + [_sds((M, GROUP_W), f32)] * 3,
        compiler_params=_params(("parallel",)),
    )(dyc, *os_, *lses)
    return outs[:3], outs[3:]


def dil_bwd_dq(q, k, kv, do, cc, lse, g, dil, *, name, n_seq):
    M = q.shape[0]
    L = SEQ // dil
    nb = L // BAND
    q3, k3, v3 = _dil_views(n_seq, dil, q, k, kv)
    do3, c3, l3 = (_grp_view(a, n_seq, dil) for a in (do, cc, lse))

    def body(q_ref, kp_ref, ko_ref, vp_ref, vo_ref, do_ref, c_ref, l_ref, dq_ref):
        mask = _band_mask(pl.program_id(2))
        kc = jnp.concatenate([kp_ref[...], ko_ref[...]], axis=0)
        vc = jnp.concatenate([vp_ref[...], vo_ref[...]], axis=0)
        cv = c_ref[...]
        lv = l_ref[...]
        outs = []
        for h, (qh, kh, vh, doh) in enumerate(zip(_heads(q_ref[...]), _heads(kc), _heads(vc), _heads(do_ref[...]))):
            s = lax.dot_general(qh, kh, NT, preferred_element_type=f32) * SCALE
            p = jnp.exp(jnp.where(mask, s, NEG) - lv[:, h * HEAD_DIM:h * HEAD_DIM + 1])
            dp = lax.dot_general(doh, vh, NT, preferred_element_type=f32)
            ds = (p * (dp + cv[:, h * HEAD_DIM:h * HEAD_DIM + 1]) * SCALE).astype(bf16)
            outs.append(jnp.dot(ds, kh, preferred_element_type=f32))
        dq_ref[...] = jnp.concatenate(outs, axis=1)

    blk = (None, BAND, GROUP_W)
    prev = lambda n: jnp.maximum(n - 1, 0)
    grp = pl.BlockSpec(blk, lambda b, r, n: (b, n, r))
    dq = pl.pallas_call(
        body, name=name, grid=(n_seq, dil, nb),
        in_specs=[pl.BlockSpec(blk, lambda b, r, n: (b, n, r * 3 + g)),
                  pl.BlockSpec(blk, lambda b, r, n: (b, prev(n), r * 3 + g)),
                  pl.BlockSpec(blk, lambda b, r, n: (b, n, r * 3 + g)),
                  pl.BlockSpec(blk, lambda b, r, n: (b, prev(n), r * 6 + 3 + g)),
                  pl.BlockSpec(blk, lambda b, r, n: (b, n, r * 6 + 3 + g)),
                  grp, grp, grp],
        out_specs=grp,
        out_shape=_sds((n_seq, L, dil * GROUP_W), f32),
        compiler_params=_params(("parallel", "parallel", "arbitrary")),
    )(q3, k3, k3, v3, v3, do3, c3, l3)
    return dq.reshape(M, GROUP_W)


def dil_bwd_dkv(q, k, kv, do, cc, lse, g, dil, *, name, n_seq):
    M = q.shape[0]
    L = SEQ // dil
    nb = L // BAND
    q3, k3, v3 = _dil_views(n_seq, dil, q, k, kv)
    do3, c3, l3 = (_grp_view(a, n_seq, dil) for a in (do, cc, lse))

    def body(k_ref, v_ref, qo_ref, qn_ref, doo_ref, don_ref, co_ref, cn_ref, lo_ref, ln_ref, dk_ref, dv_ref):
        n = pl.program_id(2)
        i = lax.broadcasted_iota(jnp.int32, (BAND, BAND), 0)
        j = lax.broadcasted_iota(jnp.int32, (BAND, BAND), 1)
        masks = (j <= i, (j >= i) & (n + 1 < nb))
        dks, dvs = [], []
        for h, (kh, vh) in enumerate(zip(_heads(k_ref[...]), _heads(v_ref[...]))):
            hs = slice(h * HEAD_DIM, (h + 1) * HEAD_DIM)
            h1 = slice(h * HEAD_DIM, h * HEAD_DIM + 1)
            dk = jnp.zeros((BAND, HEAD_DIM), f32)
            dv = jnp.zeros((BAND, HEAD_DIM), f32)
            for mask, q_ref, do_ref, c_ref, l_ref in ((masks[0], qo_ref, doo_ref, co_ref, lo_ref),
                                                      (masks[1], qn_ref, don_ref, cn_ref, ln_ref)):
                qh = q_ref[:, hs]
                doh = do_ref[:, hs]
                s = lax.dot_general(qh, kh, NT, preferred_element_type=f32) * SCALE
                p = jnp.exp(jnp.where(mask, s, NEG) - l_ref[:, h1])
                dv = dv + lax.dot_general(p.astype(bf16), doh, TN, preferred_element_type=f32)
                dp = lax.dot_general(doh, vh, NT, preferred_element_type=f32)
                ds = (p * (dp + c_ref[:, h1]) * SCALE).astype(bf16)
                dk = dk + lax.dot_general(ds, qh, TN, preferred_element_type=f32)
            dks.append(dk)
            dvs.append(dv)
        dk_ref[...] = jnp.concatenate(dks, axis=1)
        dv_ref[...] = jnp.concatenate(dvs, axis=1)

    blk = (None, BAND, GROUP_W)
    nxt = lambda n: jnp.minimum(n + 1, nb - 1)
    own = pl.BlockSpec(blk, lambda b, r, n: (b, n, r))
    nx = pl.BlockSpec(blk, lambda b, r, n: (b, nxt(n), r))
    dk, dv = pl.pallas_call(
        body, name=name, grid=(n_seq, dil, nb),
        in_specs=[pl.BlockSpec(blk, lambda b, r, n: (b, n, r * 3 + g)),
                  pl.BlockSpec(blk, lambda b, r, n: (b, n, r * 6 + 3 + g)),
                  pl.BlockSpec(blk, lambda b, r, n: (b, n, r * 3 + g)),
                  pl.BlockSpec(blk, lambda b, r, n: (b, nxt(n), r * 3 + g)),
                  own, nx, own, nx, own, nx],
        out_specs=[own, own],
        out_shape=[_sds((n_seq, L, dil * GROUP_W), f32)] * 2,
        compiler_params=_params(("parallel", "parallel", "arbitrary")),
    )(k3, v3, q3, q3, do3, do3, c3, c3, l3, l3)
    return dk.reshape(M, GROUP_W), dv.reshape(M, GROUP_W)


def _blockdiag(wp):
    out = jnp.zeros((MAIN_W, MAIN_W), wp.dtype)
    for gi in range(len(POOL_WINDOWS)):
        sl = slice(gi * POOL_GROUP, (gi + 1) * POOL_GROUP)
        out = out.at[sl, sl].set(wp[gi])
    return out


def _unblockdiag(w):
    return jnp.stack([w[gi * POOL_GROUP:(gi + 1) * POOL_GROUP, gi * POOL_GROUP:(gi + 1) * POOL_GROUP]
                      for gi in range(len(POOL_WINDOWS))])


def local_step(x, mem, positions, target, P, layer_weights, kv_weight, emit_grads):
    n_seq = x.shape[0]
    M = n_seq * SEQ
    xs = x.reshape(M, D_MODEL)
    mems = mem.reshape(n_seq * N_MEM, D_MODEL)
    pos = positions.reshape(M, 1).astype(f32)
    cos, sin = rope_tables(pos, name="rope_tables")
    gains = P["norm_gains"]

    def gain(l, k):
        return gains[l, k].reshape(1, D_MODEL)

    saved = []
    kvs = None
    for l in range(DEPTH):
        W, started = layer_weights(l, xs)
        sv = {"x": xs, "W": W}
        z, h1 = rms_matmul(xs, gain(l, 0), W["w_in"], name=f"l{l}_in", out_dtype=f32, after=started)
        kvm, mn = rms_matmul(mems, P["mem_norm"][l].reshape(1, D_MODEL), W["w_mem_kv"],
                             name=f"l{l}_memkv", out_dtype=bf16)
        y_mem = memattn_fwd(z, kvm, name=f"l{l}_memattn", n_seq=n_seq)
        sv.update(z=z, h1=h1, kvm=kvm, mn=mn)
        if l < N_A_LAYERS:
            wbd = _blockdiag(P["w_pool"][l].astype(bf16))
            psc = P["pool_scale"][l].reshape(1, MAIN_W)
            p, y_main = pool_fwd(z, wbd, psc, name=f"l{l}_pool")
            sv.update(p=p, wbd=wbd, psc=psc)
        else:
            qrot = rope_fwd(z, cos, sin, name=f"l{l}_ropeq")
            os_, lses = [], []
            for g, (_, dil) in enumerate(DIL_PATTERNS):
                o, lse = dil_fwd(qrot, kvs["krot"], kvs["kv"], g, dil, name=f"l{l}_dil{g}", n_seq=n_seq)
                os_.append(o)
                lses.append(lse)
            y_main = combine_fwd(os_, lses, name=f"l{l}_comb")
            sv.update(qrot=qrot, os=os_, lses=lses)
        ycat = jnp.concatenate([y_main, y_mem], axis=1)
        y, x1 = matmul_rms_res(ycat, W["w_out"], gain(l, 1), xs, name=f"l{l}_out")
        gu, h2 = rms_matmul(x1, gain(l, 2), W["w_gate_up"], name=f"l{l}_gu", out_dtype=bf16)
        a = swiglu_fwd(gu, name=f"l{l}_swiglu")
        y2, x2 = matmul_rms_res(a, W["w_down"], gain(l, 3), x1, name=f"l{l}_down")
        sv.update(ycat=ycat, y=y, x1=x1, gu=gu, h2=h2, a=a, y2=y2)
        saved.append(sv)
        xs = x2
        if l == N_A_LAYERS - 1:
            w_kv = kv_weight(xs)
            kv, hkv = rms_matmul(xs, P["kv_norm"].reshape(1, D_MODEL), w_kv, name="kv_proj", out_dtype=bf16)
            krot = rope_fwd(kv, cos, sin, name="ropek")
            kvs = {"kv": kv, "hkv": hkv, "krot": krot, "x": xs, "w_kv": w_kv}

    dx, sq = loss_head(xs, target.reshape(M, D_MODEL), name="loss_head")

    G = {"mem_norm": [None] * DEPTH, "norm_gains": [[None] * 4 for _ in range(DEPTH)],
         "w_pool": [None] * N_A_LAYERS, "pool_scale": [None] * N_A_LAYERS}
    dk_parts = [[] for _ in range(N_GROUPS)]
    dv_parts = [[] for _ in range(N_GROUPS)]
    C_FF = D_FF // (N_DEV // 2)
    emitted = None

    for l in reversed(range(DEPTH)):
        sv = saved[l]
        W = sv["W"]
        gw = {}
        dy2, G["norm_gains"][l][3] = rms_bwd(sv["y2"], gain(l, 3), dx, None, name=f"l{l}_b_n3", out_dtype=bf16,
                                             after=emitted)
        da = matmul_nt(dy2, W["w_down"], name=f"l{l}_b_da", out_dtype=bf16, out_chunk=C_FF)
        gw["w_down"] = matmul_tn(sv["a"], dy2, name=f"l{l}_b_wd", out_dtype=bf16)
        dgu = swiglu_bwd(sv["gu"], da, name=f"l{l}_b_swiglu")
        dh2 = matmul_nt(dgu, W["w_gate_up"], name=f"l{l}_b_dh2", out_dtype=bf16)
        gw["w_gate_up"] = matmul_tn(sv["h2"], dgu, name=f"l{l}_b_wgu", out_dtype=bf16)
        dx1, G["norm_gains"][l][2] = rms_bwd(sv["x1"], gain(l, 2), dh2, dx, name=f"l{l}_b_n2", out_dtype=f32)
        emitted = emit_grads(l, "ffn", gw)
        gw = {}
        dy, G["norm_gains"][l][1] = rms_bwd(sv["y"], gain(l, 1), dx1, None, name=f"l{l}_b_n1", out_dtype=bf16,
                                            after=emitted)
        dycat = matmul_nt(dy, W["w_out"], name=f"l{l}_b_dycat", out_dtype=bf16)
        gw["w_out"] = matmul_tn(sv["ycat"], dy, name=f"l{l}_b_wout", out_dtype=bf16)
        dq_mem, dkvm = memattn_bwd(sv["z"], sv["kvm"], dycat, name=f"l{l}_b_memattn", n_seq=n_seq)
        dmn = matmul_nt(dkvm, W["w_mem_kv"], name=f"l{l}_b_dmn", out_dtype=bf16)
        gw["w_mem_kv"] = matmul_tn(sv["mn"], dkvm, name=f"l{l}_b_wmkv", out_dtype=bf16)
        _, G["mem_norm"][l] = rms_bwd(mems, P["mem_norm"][l].reshape(1, D_MODEL), dmn, None,
                                      name=f"l{l}_b_nmem", out_dtype=bf16)
        if l < N_A_LAYERS:
            dz_main, dwbd, dps = pool_bwd(dycat, sv["p"], sv["wbd"], sv["psc"], name=f"l{l}_b_pool")
            G["w_pool"][l] = _unblockdiag(dwbd)
            G["pool_scale"][l] = dps.reshape(MAIN_W)
        else:
            dos, ccs = combine_bwd(dycat, sv["os"], sv["lses"], name=f"l{l}_b_comb")
            dqs = []
            for g, (_, dil) in enumerate(DIL_PATTERNS):
                args = (sv["qrot"], kvs["krot"], kvs["kv"], dos[g], ccs[g], sv["lses"][g], g, dil)
                dqs.append([dil_bwd_dq(*args, name=f"l{l}_b_dq{g}", n_seq=n_seq)])
                dk, dv = dil_bwd_dkv(*args, name=f"l{l}_b_dkv{g}", n_seq=n_seq)
                dk_parts[g].append(dk)
                dv_parts[g].append(dv)
            dz_main = group_sum(dqs, cos, sin, name=f"l{l}_b_ropeq", rotate=True)
        dz = jnp.concatenate([dz_main, dq_mem], axis=1)
        dh1 = matmul_nt(dz, W["w_in"], name=f"l{l}_b_dh1", out_dtype=bf16)
        gw["w_in"] = matmul_tn(sv["h1"], dz, name=f"l{l}_b_win", out_dtype=bf16)
        dx, G["norm_gains"][l][0] = rms_bwd(sv["x"], gain(l, 0), dh1, dx1, name=f"l{l}_b_n0", out_dtype=f32)
        if l == N_A_LAYERS:
            dkr = group_sum(dk_parts, cos, sin, name="b_ropek", rotate=True)
            dvs = group_sum(dv_parts, cos, sin, name="b_sumv", rotate=False)
            dkv = jnp.concatenate([dkr, dvs], axis=1)
            dhkv = matmul_nt(dkv, kvs["w_kv"], name="b_dhkv", out_dtype=bf16)
            gw["w_kv"] = matmul_tn(kvs["hkv"], dkv, name="b_wkv", out_dtype=bf16)
            dx, gkn = rms_bwd(kvs["x"], P["kv_norm"].reshape(1, D_MODEL), dhkv, dx, name="b_nkv", out_dtype=f32)
            G["kv_norm"] = gkn.reshape(D_MODEL)
        emitted = emit_grads(l, "mix", gw)

    small = {"w_pool": jnp.stack(G["w_pool"]), "pool_scale": jnp.stack(G["pool_scale"]),
             "mem_norm": jnp.concatenate(G["mem_norm"], axis=0),
             "norm_gains": jnp.stack([jnp.concatenate(r, axis=0) for r in G["norm_gains"]]),
             "kv_norm": G["kv_norm"]}
    return sq[0, 0], dx.reshape(n_seq, SEQ, D_MODEL), small, emitted


def _peer(k):
    x, y, c = lax.axis_index("x"), lax.axis_index("y"), lax.axis_index("c")
    px = 1 - x if k & 4 else x
    py = 1 - y if k & 2 else y
    pc = 1 - c if k & 1 else c
    return (px, py, pc), 4 * px + 2 * py + pc


def _my_index():
    return 4 * lax.axis_index("x") + 2 * lax.axis_index("y") + lax.axis_index("c")


def _src_for(kinds, in_refs, i, idx):
    return in_refs[i] if kinds[i] == "gather" else in_refs[i].at[idx]


def _local_copies(kinds, in_refs, out_refs, local_sems):
    me = _my_index()
    return [pltpu.make_async_copy(_src_for(kinds, in_refs, i, me), out_refs[i].at[me], local_sems.at[i])
            for i in range(len(kinds))]


def _remote_copies(kinds, in_refs, out_refs, send_sems, recv_sems, *, arriving):
    me = _my_index()
    copies = []
    for k in range(1, N_DEV):
        dev, idx = _peer(k)
        for i in range(len(kinds)):
            j = i * (N_DEV - 1) + k - 1
            copies.append(pltpu.make_async_remote_copy(
                src_ref=_src_for(kinds, in_refs, i, idx), dst_ref=out_refs[i].at[idx if arriving else me],
                send_sem=send_sems.at[j], recv_sem=recv_sems.at[j], device_id=dev, device_id_type=MESH))
    return copies


def _out_shape(a, kind):
    return ((N_DEV,) + a.shape) if kind == "gather" else a.shape


def exchange(items, *, name):
    n = len(items)
    kinds = [k for _, k in items]

    def body(*refs):
        send_sems, recv_sems, local_sems = refs[2 * n:]
        local = _local_copies(kinds, refs[:n], refs[n:2 * n], local_sems)
        sends = _remote_copies(kinds, refs[:n], refs[n:2 * n], send_sems, recv_sems, arriving=False)
        for cp in local + sends:
            cp.start()
        for cp in _remote_copies(kinds, refs[:n], refs[n:2 * n], send_sems, recv_sems, arriving=True):
            cp.wait_recv()
        for cp in sends:
            cp.wait_send()
        for cp in local:
            cp.wait()

    any_spec = pl.BlockSpec(memory_space=pl.ANY)
    return pl.pallas_call(
        body, name=name,
        in_specs=[any_spec] * n, out_specs=[any_spec] * n,
        out_shape=[_sds(_out_shape(a, k), a.dtype) for a, k in items],
        scratch_shapes=[pltpu.SemaphoreType.DMA((n * (N_DEV - 1),)), pltpu.SemaphoreType.DMA((n * (N_DEV - 1),)),
                        pltpu.SemaphoreType.DMA((n,))],
    )(*[a for a, _ in items])


_HBM = pl.BlockSpec(memory_space=pltpu.HBM)
_SEM = pl.BlockSpec(memory_space=pltpu.SEMAPHORE)
_EFFECT = pltpu.SideEffectType.DATAFLOW_SIDE_EFFECTING


def exchange_start(items, after, *, name):
    n = len(items)
    kinds = [k for _, k in items]

    def body(*refs):
        in_refs, land_refs = refs[:n], refs[n:2 * n]
        send_sems, recv_sems, local_sems = refs[2 * n + 1:2 * n + 4]
        token = refs[-1]
        for cp in (_local_copies(kinds, in_refs, land_refs, local_sems)
                   + _remote_copies(kinds, in_refs, land_refs, send_sems, recv_sems, arriving=False)):
            cp.start()
        token[...] = jnp.zeros_like(token)

    srcs = [pltpu.with_memory_space_constraint(a, pltpu.HBM) for a, _ in items]
    lands = [pltpu.with_memory_space_constraint(lax.empty(_out_shape(a, k), a.dtype), pltpu.HBM) for a, k in items]
    outs = pl.pallas_call(
        body, name=name,
        out_shape=(pltpu.SemaphoreType.DMA((n * (N_DEV - 1),)), pltpu.SemaphoreType.DMA((n * (N_DEV - 1),)),
                   pltpu.SemaphoreType.DMA((n,)),
                   *[pltpu.HBM(a.shape, a.dtype) for a in srcs], *[pltpu.HBM(a.shape, a.dtype) for a in lands],
                   _sds((8, 128), f32)),
        in_specs=[_HBM] * (2 * n) + [pl.BlockSpec(memory_space=pl.ANY)],
        out_specs=(_SEM, _SEM, _SEM, *[_HBM] * (2 * n), pl.BlockSpec(memory_space=pltpu.VMEM)),
        input_output_aliases={i: 3 + i for i in range(2 * n)},
        compiler_params=pltpu.CompilerParams(has_side_effects=_EFFECT),
    )(*srcs, *lands, after)
    return {"kinds": kinds, "sems": outs[:3], "srcs": outs[3:3 + n], "lands": outs[3 + n:3 + 2 * n], "token": outs[-1]}


def exchange_wait(handle, after, *, name):
    kinds = handle["kinds"]
    n = len(kinds)

    def body(*refs):
        in_refs, land_refs = refs[:n], refs[n:2 * n]
        send_sems, recv_sems, local_sems = refs[2 * n:2 * n + 3]
        for cp in _remote_copies(kinds, in_refs, land_refs, send_sems, recv_sems, arriving=True):
            cp.wait_recv()
        for cp in _remote_copies(kinds, in_refs, land_refs, send_sems, recv_sems, arriving=False):
            cp.wait_send()
        for cp in _local_copies(kinds, in_refs, land_refs, local_sems):
            cp.wait()

    srcs, lands = list(handle["srcs"]), list(handle["lands"])
    after = list(after) if isinstance(after, (list, tuple)) else [after]
    outs = pl.pallas_call(
        body, name=name,
        out_shape=tuple(pltpu.HBM(a.shape, a.dtype) for a in srcs + lands),
        in_specs=[_HBM] * (2 * n) + [_SEM] * 3 + [pl.BlockSpec(memory_space=pl.ANY)] * len(after),
        out_specs=tuple([_HBM] * (2 * n)),
        input_output_aliases={i: i for i in range(2 * n)},
        compiler_params=pltpu.CompilerParams(has_side_effects=_EFFECT),
    )(*srcs, *lands, *handle["sems"], *after)
    return list(outs[n:])


def adamw(slots, w, m, v, *, name):
    R, C = w.shape
    tr = _tile(R, (256, 128, 64, 32, 16, 8))
    c1 = 1.0 - ADAM_B1 ** ADAM_STEP
    c2 = 1.0 - ADAM_B2 ** ADAM_STEP

    def body(s_ref, w_ref, m_ref, v_ref, g_ref, d_ref, m2_ref, v2_ref):
        g = s_ref[0].astype(f32)
        for d in range(1, N_DEV):
            g = g + s_ref[d].astype(f32)
        m2 = ADAM_B1 * m_ref[...] + (1.0 - ADAM_B1) * g
        v2 = ADAM_B2 * v_ref[...] + (1.0 - ADAM_B2) * (g * g)
        g_ref[...] = g
        m2_ref[...] = m2
        v2_ref[...] = v2
        d_ref[...] = -ADAM_LR * ((m2 / c1) / (jnp.sqrt(v2 / c2) + ADAM_EPS) + ADAM_WD * w_ref[...])

    blk = pl.BlockSpec((tr, C), lambda i: (i, 0))
    return pl.pallas_call(
        body, name=name, grid=(R // tr,),
        in_specs=[pl.BlockSpec((N_DEV, tr, C), lambda i: (0, i, 0)), blk, blk, blk],
        out_specs=[blk] * 4,
        out_shape=[_sds((R, C), f32)] * 4,
        compiler_params=_params(("parallel",)),
    )(slots, w, m, v)


WEIGHTS = ("norm_gains", "mem_norm", "w_in", "w_mem_kv", "w_out", "w_pool", "pool_scale", "kv_norm", "w_kv",
           "w_gate_up", "w_down")
LAYER_MATS = ("w_in", "w_mem_kv", "w_out", "w_gate_up", "w_down")
POOL_SHARD = MAIN_W // N_DEV
KV_SHARD = 2 * MAIN_W // N_DEV
LOOKAHEAD = 2


def _pack_small(gains, pscale):
    lead = gains.shape[:-3]
    g = gains.reshape(lead + (16, 128))
    p = jnp.zeros(lead + (8, 128), f32).at[..., :2, :POOL_SHARD].set(pscale)
    return jnp.concatenate([g, p], axis=-2)


def _unpack_small(a):
    return a[:16].reshape(4, 4, 128), a[16:18, :POOL_SHARD]


def _pack_repl(mem_norm, kv_norm):
    return jnp.concatenate([mem_norm, kv_norm.reshape(1, D_MODEL), jnp.zeros((3, D_MODEL), f32)], axis=0)


def _unpack_repl(a):
    return a[:4], a[4]


def kernel(x, mem, positions, norm_gains, mem_norm, w_in, w_mem_kv, w_out, w_pool, pool_scale, kv_norm, w_kv, w_gate_up, w_down, loss_target, m_norm_gains, m_mem_norm, m_w_in, m_w_mem_kv, m_w_out, m_w_pool, m_pool_scale, m_kv_norm, m_w_kv, m_w_gate_up, m_w_down, v_norm_gains, v_mem_norm, v_w_in, v_w_mem_kv, v_w_out, v_w_pool, v_pool_scale, v_kv_norm, v_w_kv, v_w_gate_up, v_w_down):
    w = dict(norm_gains=norm_gains, mem_norm=mem_norm, w_in=w_in, w_mem_kv=w_mem_kv, w_out=w_out, w_pool=w_pool,
             pool_scale=pool_scale, kv_norm=kv_norm, w_kv=w_kv, w_gate_up=w_gate_up, w_down=w_down)
    m = dict(norm_gains=m_norm_gains, mem_norm=m_mem_norm, w_in=m_w_in, w_mem_kv=m_w_mem_kv, w_out=m_w_out,
             w_pool=m_w_pool, pool_scale=m_pool_scale, kv_norm=m_kv_norm, w_kv=m_w_kv, w_gate_up=m_w_gate_up,
             w_down=m_w_down)
    v = dict(norm_gains=v_norm_gains, mem_norm=v_mem_norm, w_in=v_w_in, w_mem_kv=v_w_mem_kv, w_out=v_w_out,
             w_pool=v_w_pool, pool_scale=v_pool_scale, kv_norm=v_kv_norm, w_kv=v_w_kv, w_gate_up=v_w_gate_up,
             w_down=v_w_down)

    small = _pack_small(norm_gains, pool_scale)
    (gsmall,) = exchange([(small, "gather")], name="gather_small")
    P = {"norm_gains": jnp.moveaxis(gsmall[:, :16].reshape(N_DEV, 4, 4, 128), 0, 2).reshape(4, 4, D_MODEL),
         "pool_scale": jnp.moveaxis(gsmall[:, 16:18, :POOL_SHARD], 0, 1).reshape(2, MAIN_W),
         "mem_norm": mem_norm, "kv_norm": kv_norm, "w_pool": w_pool}

    def layer_items(l):
        items = [(w[k][l].astype(bf16), "gather") for k in LAYER_MATS]
        if l == N_A_LAYERS - 1:
            items.append((w_kv.astype(bf16), "gather"))
        return items

    handles = {}
    token = gsmall
    for l in range(LOOKAHEAD):
        handles[l] = exchange_start(layer_items(l), token, name=f"gather_start_l{l}")
        token = handles[l]["token"]
    landed = {}

    def layer_weights(l, after):
        got = exchange_wait(handles[l], token if l == 0 else after, name=f"gather_wait_l{l}")
        landed[l] = got
        started = None
        if l + LOOKAHEAD < DEPTH:
            handles[l + LOOKAHEAD] = exchange_start(layer_items(l + LOOKAHEAD), got[0], name=f"gather_start_l{l + LOOKAHEAD}")
            started = handles[l + LOOKAHEAD]["token"]
        W = dict(zip(LAYER_MATS, got))
        for k in ("w_in", "w_mem_kv", "w_out", "w_down"):
            W[k] = W[k].reshape(-1, W[k].shape[-1])
        return W, started

    def kv_weight(after):
        g = landed[N_A_LAYERS - 1][len(LAYER_MATS)]
        return jnp.moveaxis(g, 0, 1).reshape(D_MODEL, 2 * MAIN_W)

    ghandles = {}

    PARTS = {"ffn": ("w_gate_up", "w_down"), "mix": ("w_in", "w_mem_kv", "w_out")}

    def emit_grads(l, part, gw):
        items = [(gw[k].reshape((N_DEV, -1) + gw[k].shape[-1:]), "scatter") for k in PARTS[part]]
        if part == "mix" and l == N_A_LAYERS:
            items.append((jnp.moveaxis(gw["w_kv"].reshape(D_MODEL, N_DEV, KV_SHARD), 1, 0), "scatter"))
        ghandles[l, part] = exchange_start(items, items[0][0], name=f"scatter_start_{part}_l{l}")
        return ghandles[l, part]["token"]

    sq, grad_x, GS, emitted = local_step(x, mem, positions, loss_target, P, layer_weights, kv_weight, emit_grads)
    loss = lax.psum(0.5 * sq / D_MODEL, ("x", "y", "c"))

    gs = _pack_small(jnp.moveaxis(GS["norm_gains"].reshape(4, 4, N_DEV, 128), 2, 0),
                     jnp.moveaxis(GS["pool_scale"].reshape(2, N_DEV, POOL_SHARD), 1, 0))
    gs = gs + emitted[0, 0]
    parts_small, parts_repl, parts_pool = exchange(
        [(gs, "scatter"), (_pack_repl(GS["mem_norm"], GS["kv_norm"]), "gather"),
         (GS["w_pool"].reshape(-1, POOL_GROUP), "gather")], name="exchange_small_grads")

    def two_d(a):
        return a.reshape(-1, a.shape[-1])

    out = {k: [None] * DEPTH for k in LAYER_MATS}
    after = [parts_small]
    for l in reversed(range(DEPTH)):
        for part in ("ffn", "mix"):
            got = exchange_wait(ghandles[l, part], after, name=f"scatter_wait_{part}_l{l}")
            after = []
            for k, slots in zip(PARTS[part], got):
                out[k][l] = adamw(slots.reshape(N_DEV, -1, slots.shape[-1]), two_d(w[k][l]), two_d(m[k][l]),
                                  two_d(v[k][l]), name=f"adamw_{k}_l{l}")
                after.append(out[k][l][0])
            if part == "mix" and l == N_A_LAYERS:
                out["w_kv"] = adamw(got[-1], w_kv, m_w_kv, v_w_kv, name="adamw_w_kv")
                after.append(out["w_kv"][0])
    for k in LAYER_MATS:
        out[k] = [jnp.stack([out[k][l][j] for l in range(DEPTH)]).reshape(w[k].shape) for j in range(4)]

    res = adamw(parts_small, small, _pack_small(m_norm_gains, m_pool_scale), _pack_small(v_norm_gains, v_pool_scale),
                name="adamw_small")
    out["norm_gains"], out["pool_scale"] = zip(*[_unpack_small(r) for r in res])
    res = adamw(parts_repl, _pack_repl(mem_norm, kv_norm), _pack_repl(m_mem_norm, m_kv_norm),
                _pack_repl(v_mem_norm, v_kv_norm), name="adamw_repl")
    out["mem_norm"], out["kv_norm"] = zip(*[_unpack_repl(r) for r in res])
    res = adamw(parts_pool, two_d(w_pool), two_d(m_w_pool), two_d(v_w_pool), name="adamw_w_pool")
    out["w_pool"] = [r.reshape(w_pool.shape) for r in res]

    return (loss, grad_x, *[out[k][0] for k in WEIGHTS], *[out[k][1] for k in WEIGHTS],
            *[out[k][2] for k in WEIGHTS], *[out[k][3] for k in WEIGHTS])
```

```python
import numpy as np
import jax
import jax.numpy as jnp
from jax import lax
from jax.experimental import pallas as pl
from jax.experimental.pallas import tpu as pltpu

f32 = jnp.float32
bf16 = jnp.bfloat16

D_MODEL = 1024
SEQ = 2048
DEPTH = 4
N_MEM = 256
HEAD_DIM = 64
N_MEM_HEADS = 4
MEM_W = 256
MAIN_W = 768
POOL_WINDOWS = (2, 4, 8, 16)
POOL_GROUP = 192
POOL_HALO = 16
DIL_PATTERNS = ((128, 1), (512, 4), (2048, 16))
N_GROUPS = 3
GROUP_W = 256
BAND = 128
N_A_LAYERS = 2
D_FF = 2816
ROPE_THETA = 10000.0
EPS = 1e-6
NEG = -1e30
SCALE = HEAD_DIM ** -0.5
N_DEV = 8

ADAM_LR = 0.001
ADAM_B1 = 0.9
ADAM_B2 = 0.999
ADAM_EPS = 1e-08
ADAM_WD = 0.01
ADAM_STEP = 10

VMEM_LIMIT_BYTES = 56 * 1024 * 1024
MESH = pl.DeviceIdType.MESH

NN = (((1,), (0,)), ((), ()))
NT = (((1,), (1,)), ((), ()))
TN = (((0,), (0,)), ((), ()))


def _params(sem=None):
    return pltpu.CompilerParams(dimension_semantics=sem, vmem_limit_bytes=VMEM_LIMIT_BYTES)


def _tile(n, cands):
    for c in cands:
        if n % c == 0:
            return c
    return n


def _sds(shape, dtype):
    return jax.ShapeDtypeStruct(tuple(shape), dtype)


def _rms_r(v):
    return lax.rsqrt(jnp.mean(v * v, axis=-1, keepdims=True) + EPS)


def rms_matmul(x, gain, w, *, name, out_dtype, transposed=False, after=None):
    M, K = x.shape
    N = w.shape[0] if transposed else w.shape[1]
    tm = min(1024, M)
    tn = _tile(N, (512, 256, 128))
    order = [] if after is None else [after]

    def body(x_ref, g_ref, w_ref, *refs):
        z_ref, h_ref = refs[len(order):]

        @pl.when(pl.program_id(1) == 0)
        def _():
            xv = x_ref[...]
            h_ref[...] = (xv * _rms_r(xv) * g_ref[...]).astype(bf16)

        z_ref[...] = lax.dot_general(h_ref[...], w_ref[...], NT if transposed else NN,
                                     preferred_element_type=f32).astype(z_ref.dtype)

    w_spec = pl.BlockSpec((tn, K), lambda i, j: (j, 0)) if transposed else pl.BlockSpec((K, tn), lambda i, j: (0, j))
    return pl.pallas_call(
        body, name=name, grid=(M // tm, N // tn),
        in_specs=[pl.BlockSpec((tm, K), lambda i, j: (i, 0)),
                  pl.BlockSpec((1, K), lambda i, j: (0, 0)),
                  w_spec] + [pl.BlockSpec(memory_space=pl.ANY)] * len(order),
        out_specs=[pl.BlockSpec((tm, tn), lambda i, j: (i, j)), pl.BlockSpec((tm, K), lambda i, j: (i, 0))],
        out_shape=[_sds((M, N), out_dtype), _sds((M, K), bf16)],
        compiler_params=_params(("parallel", "arbitrary")),
    )(x, gain, w, *order)


def matmul_rms_res(a, w, gain, res, *, name):
    M, K = a.shape
    N = w.shape[1]
    tm = min(512, M)

    def body(a_ref, w_ref, g_ref, r_ref, y_ref, x_ref):
        y = jnp.dot(a_ref[...], w_ref[...], preferred_element_type=f32)
        y_ref[...] = y.astype(bf16)
        x_ref[...] = r_ref[...] + y * _rms_r(y) * g_ref[...]

    row = pl.BlockSpec((tm, N), lambda i: (i, 0))
    return pl.pallas_call(
        body, name=name, grid=(M // tm,),
        in_specs=[pl.BlockSpec((tm, K), lambda i: (i, 0)),
                  pl.BlockSpec((K, N), lambda i: (0, 0)),
                  pl.BlockSpec((1, N), lambda i: (0, 0)),
                  row],
        out_specs=[row, row],
        out_shape=[_sds((M, N), bf16), _sds((M, N), f32)],
        compiler_params=_params(("parallel",)),
    )(a, w, gain, res)


def matmul(a, b, dims, *, name, out_dtype):
    if dims is TN:
        K, M = a.shape
        tm = _tile(M, (512, 256, 128))
        a_spec = pl.BlockSpec((K, tm), lambda i: (0, i))
    else:
        M, K = a.shape
        tm = _tile(M, (512, 256, 128))
        a_spec = pl.BlockSpec((tm, K), lambda i: (i, 0))
    N = b.shape[0] if dims is NT else b.shape[1]

    def body(a_ref, b_ref, o_ref):
        o_ref[...] = lax.dot_general(a_ref[...].astype(bf16), b_ref[...].astype(bf16), dims,
                                     preferred_element_type=f32).astype(o_ref.dtype)

    return pl.pallas_call(
        body, name=name, grid=(M // tm,),
        in_specs=[a_spec, pl.BlockSpec(b.shape, lambda i: (0, 0))],
        out_specs=pl.BlockSpec((tm, N), lambda i: (i, 0)),
        out_shape=_sds((M, N), out_dtype),
        compiler_params=_params(("parallel",)),
    )(a, b)


def swiglu_fwd(gu, *, name):
    M = gu.shape[0]
    tm = min(512, M)

    def body(g_ref, u_ref, a_ref):
        g = g_ref[...].astype(f32)
        s = 1.0 / (1.0 + jnp.exp(-g))
        a_ref[...] = (g * s * u_ref[...].astype(f32)).astype(bf16)

    return pl.pallas_call(
        body, name=name, grid=(M // tm,),
        in_specs=[pl.BlockSpec((tm, D_FF), lambda i: (i, 0)),
                  pl.BlockSpec((tm, D_FF), lambda i: (i, 1))],
        out_specs=pl.BlockSpec((tm, D_FF), lambda i: (i, 0)),
        out_shape=_sds((M, D_FF), bf16),
        compiler_params=_params(("parallel",)),
    )(gu, gu)


def swiglu_bwd(gu, da, *, name):
    M = gu.shape[0]
    tm = min(512, M)

    def body(g_ref, u_ref, da_ref, o_ref):
        g = g_ref[...].astype(f32)
        u = u_ref[...].astype(f32)
        da = da_ref[...].astype(f32)
        s = 1.0 / (1.0 + jnp.exp(-g))
        o_ref[:, :D_FF] = (da * u * s * (1.0 + g * (1.0 - s))).astype(bf16)
        o_ref[:, D_FF:] = (da * g * s).astype(bf16)

    return pl.pallas_call(
        body, name=name, grid=(M // tm,),
        in_specs=[pl.BlockSpec((tm, D_FF), lambda i: (i, 0)),
                  pl.BlockSpec((tm, D_FF), lambda i: (i, 1)),
                  pl.BlockSpec((tm, D_FF), lambda i: (i, 0))],
        out_specs=pl.BlockSpec((tm, 2 * D_FF), lambda i: (i, 0)),
        out_shape=_sds((M, 2 * D_FF), bf16),
        compiler_params=_params(("parallel",)),
    )(gu, gu, da)


def rms_bwd(y, gain, dn, res, *, name, out_dtype, after=None):
    M, N = y.shape
    tm = min(512, M)
    has_res = res is not None
    order = [] if after is None else [after]

    def body(*refs):
        y_ref, g_ref, dn_ref = refs[:3]
        r_ref = refs[3] if has_res else None
        dy_ref, dg_ref = refs[-2:]
        yv = y_ref[...].astype(f32)
        dn = dn_ref[...].astype(f32)
        r = _rms_r(yv)
        q = dn * g_ref[...]
        dy = r * q - yv * (r * r * r) * jnp.mean(q * yv, axis=-1, keepdims=True)
        if has_res:
            dy = dy + r_ref[...]
        dy_ref[...] = dy.astype(dy_ref.dtype)

        @pl.when(pl.program_id(0) == 0)
        def _():
            dg_ref[...] = jnp.zeros_like(dg_ref)

        dg_ref[...] += jnp.sum(dn * yv * r, axis=0, keepdims=True)

    row = pl.BlockSpec((tm, N), lambda i: (i, 0))
    vec = pl.BlockSpec((1, N), lambda i: (0, 0))
    args = [y, gain, dn] + ([res] if has_res else []) + order
    return pl.pallas_call(
        body, name=name, grid=(M // tm,),
        in_specs=[row, vec, row] + ([row] if has_res else []) + [pl.BlockSpec(memory_space=pl.ANY)] * len(order),
        out_specs=[row, vec],
        out_shape=[_sds((M, N), out_dtype), _sds((1, N), f32)],
        compiler_params=_params(("arbitrary",)),
    )(*args)


def loss_head(x, target, *, name):
    M, N = x.shape
    tm = min(512, M)

    def body(x_ref, t_ref, dx_ref, l_ref):
        e = x_ref[...] - t_ref[...]
        dx_ref[...] = e * (1.0 / N)

        @pl.when(pl.program_id(0) == 0)
        def _():
            l_ref[...] = jnp.zeros_like(l_ref)

        l_ref[...] += jnp.sum(jnp.sum(e * e, axis=0, keepdims=True), axis=1, keepdims=True)

    row = pl.BlockSpec((tm, N), lambda i: (i, 0))
    return pl.pallas_call(
        body, name=name, grid=(M // tm,),
        in_specs=[row, row],
        out_specs=[row, pl.BlockSpec((8, 128), lambda i: (0, 0))],
        out_shape=[_sds((M, N), f32), _sds((8, 128), f32)],
        compiler_params=_params(("arbitrary",)),
    )(x, target)


def _pool_select(a1, a2, a3, a4):
    col = lax.broadcasted_iota(jnp.int32, (1, MAIN_W), 1) // POOL_GROUP
    return jnp.where(col == 0, a1, jnp.where(col == 1, a2, jnp.where(col == 2, a3, a4)))


def _pool_count(t):
    col = lax.broadcasted_iota(jnp.int32, (1, MAIN_W), 1) // POOL_GROUP
    win = jnp.where(col == 0, 2, jnp.where(col == 1, 4, jnp.where(col == 2, 8, 16)))
    return jnp.minimum(t + 1, win).astype(f32)


def pool_fwd(z, wbd, scale, *, name):
    M = z.shape[0]
    tm = 256
    nper = SEQ // tm
    hb = tm // POOL_HALO

    def body(zc_ref, zh_ref, w_ref, s_ref, p_ref, y_ref):
        i = pl.program_id(0)
        seq_blk = i % nper
        halo = jnp.where(seq_blk == 0, 0.0, zh_ref[...])
        u = zc_ref[...]
        ext = jnp.concatenate([halo, u], axis=0)
        a1 = ext + pltpu.roll(ext, 1, 0)
        a2 = a1 + pltpu.roll(a1, 2, 0)
        a3 = a2 + pltpu.roll(a2, 4, 0)
        a4 = a3 + pltpu.roll(a3, 8, 0)
        sums = _pool_select(a1, a2, a3, a4)[POOL_HALO:]
        t = seq_blk * tm + lax.broadcasted_iota(jnp.int32, (tm, 1), 0)
        p = (sums / _pool_count(t) - u).astype(bf16)
        p_ref[...] = p
        y_ref[...] = (jnp.dot(p, w_ref[...], preferred_element_type=f32) * s_ref[...]).astype(bf16)

    return pl.pallas_call(
        body, name=name, grid=(M // tm,),
        in_specs=[pl.BlockSpec((tm, MAIN_W), lambda i: (i, 0)),
                  pl.BlockSpec((POOL_HALO, MAIN_W), lambda i: (jnp.maximum(i * hb - 1, 0), 0)),
                  pl.BlockSpec((MAIN_W, MAIN_W), lambda i: (0, 0)),
                  pl.BlockSpec((1, MAIN_W), lambda i: (0, 0))],
        out_specs=[pl.BlockSpec((tm, MAIN_W), lambda i: (i, 0)),
                   pl.BlockSpec((tm, MAIN_W), lambda i: (i, 0))],
        out_shape=[_sds((M, MAIN_W), bf16), _sds((M, D_MODEL), bf16)],
        compiler_params=_params(("parallel",)),
    )(z, z, wbd, scale)


def pool_bwd(dyc, p, wbd, scale, *, name):
    M = p.shape[0]
    tm = 256
    nper = SEQ // tm
    hb = tm // POOL_HALO
    last_hb = M // POOL_HALO - 1

    def body(dy_ref, dyh_ref, p_ref, w_ref, s_ref, dz_ref, dw_ref, ds_ref):
        i = pl.program_id(0)
        seq_blk = i % nper
        dy = dy_ref[...].astype(f32)
        pv = p_ref[...]
        w = w_ref[...]
        sc = s_ref[...]

        @pl.when(i == 0)
        def _():
            dw_ref[...] = jnp.zeros_like(dw_ref)
            ds_ref[...] = jnp.zeros_like(ds_ref)

        v = jnp.dot(pv, w, preferred_element_type=f32)
        ds_ref[...] += jnp.sum(dy * v, axis=0, keepdims=True)
        dv = (dy * sc).astype(bf16)
        dw_ref[...] += lax.dot_general(pv, dv, TN, preferred_element_type=f32)
        dp = lax.dot_general(dv, w, NT, preferred_element_type=f32)
        dvh = jnp.where(seq_blk == nper - 1, 0.0, dyh_ref[...].astype(f32) * sc).astype(bf16)
        dph = lax.dot_general(dvh, w, NT, preferred_element_type=f32)
        ext = jnp.concatenate([dp, dph], axis=0)
        n = tm + POOL_HALO
        t = seq_blk * tm + lax.broadcasted_iota(jnp.int32, (n, 1), 0)
        e = ext / _pool_count(t)
        b1 = e + pltpu.roll(e, n - 1, 0)
        b2 = b1 + pltpu.roll(b1, n - 2, 0)
        b3 = b2 + pltpu.roll(b2, n - 4, 0)
        b4 = b3 + pltpu.roll(b3, n - 8, 0)
        dz_ref[...] = (_pool_select(b1, b2, b3, b4)[:tm] - dp).astype(dz_ref.dtype)

    return pl.pallas_call(
        body, name=name, grid=(M // tm,),
        in_specs=[pl.BlockSpec((tm, MAIN_W), lambda i: (i, 0)),
                  pl.BlockSpec((POOL_HALO, MAIN_W), lambda i: (jnp.minimum((i + 1) * hb, last_hb), 0)),
                  pl.BlockSpec((tm, MAIN_W), lambda i: (i, 0)),
                  pl.BlockSpec((MAIN_W, MAIN_W), lambda i: (0, 0)),
                  pl.BlockSpec((1, MAIN_W), lambda i: (0, 0))],
        out_specs=[pl.BlockSpec((tm, MAIN_W), lambda i: (i, 0)),
                   pl.BlockSpec((MAIN_W, MAIN_W), lambda i: (0, 0)),
                   pl.BlockSpec((1, MAIN_W), lambda i: (0, 0))],
        out_shape=[_sds((M, D_MODEL), bf16), _sds((MAIN_W, MAIN_W), f32), _sds((1, MAIN_W), f32)],
        compiler_params=_params(("arbitrary",)),
    )(dyc, dyc, p, wbd, scale)


def _mem_probs(q, kv, h):
    hs = slice(h * HEAD_DIM, (h + 1) * HEAD_DIM)
    qh = q[:, hs]
    kh = kv[:, hs]
    s = lax.dot_general(qh, kh, NT, preferred_element_type=f32) * SCALE
    m = jnp.max(s, axis=-1, keepdims=True)
    e = jnp.exp(s - m)
    return qh, kh, e / jnp.sum(e, axis=-1, keepdims=True)


def memattn_fwd(z, kvm, ycat, *, name, n_seq):
    M = z.shape[0]
    tq = 512
    nq = SEQ // tq

    def body(q_ref, kv_ref, _, o_ref):
        q = q_ref[...].astype(bf16)
        kv = kv_ref[...]
        outs = []
        for h in range(N_MEM_HEADS):
            _, _, p = _mem_probs(q, kv, h)
            vh = kv[:, MEM_W + h * HEAD_DIM: MEM_W + (h + 1) * HEAD_DIM]
            outs.append(jnp.dot(p.astype(bf16), vh, preferred_element_type=f32))
        o_ref[...] = jnp.concatenate(outs, axis=1).astype(bf16)

    return pl.pallas_call(
        body, name=name, grid=(n_seq, nq),
        in_specs=[pl.BlockSpec((tq, MEM_W), lambda b, i: (b * nq + i, 3)),
                  pl.BlockSpec((N_MEM, 2 * MEM_W), lambda b, i: (b, 0)),
                  pl.BlockSpec(memory_space=pl.ANY)],
        out_specs=pl.BlockSpec((tq, MEM_W), lambda b, i: (b * nq + i, 3)),
        out_shape=_sds((M, D_MODEL), bf16),
        input_output_aliases={2: 0},
        compiler_params=_params(("parallel", "parallel")),
    )(z, kvm, ycat)


def memattn_bwd(z, kvm, dyc, dz, *, name, n_seq):
    M = z.shape[0]
    tq = 512
    nq = SEQ // tq

    def body(q_ref, kv_ref, dy_ref, _, dq_ref, dkv_ref):
        q = q_ref[...].astype(bf16)
        kv = kv_ref[...]
        dy = dy_ref[...].astype(bf16)
        dqs, dks, dvs = [], [], []
        for h in range(N_MEM_HEADS):
            hs = slice(h * HEAD_DIM, (h + 1) * HEAD_DIM)
            qh, kh, p = _mem_probs(q, kv, h)
            vh = kv[:, MEM_W + h * HEAD_DIM: MEM_W + (h + 1) * HEAD_DIM]
            dyh = dy[:, hs]
            dvs.append(lax.dot_general(p.astype(bf16), dyh, TN, preferred_element_type=f32))
            dp = lax.dot_general(dyh, vh, NT, preferred_element_type=f32)
            ds = (p * (dp - jnp.sum(dp * p, axis=-1, keepdims=True)) * SCALE).astype(bf16)
            dqs.append(jnp.dot(ds, kh, preferred_element_type=f32))
            dks.append(lax.dot_general(ds, qh, TN, preferred_element_type=f32))
        dq_ref[...] = jnp.concatenate(dqs, axis=1).astype(bf16)

        @pl.when(pl.program_id(1) == 0)
        def _():
            dkv_ref[...] = jnp.zeros_like(dkv_ref)

        dkv_ref[...] += jnp.concatenate(dks + dvs, axis=1)

    return pl.pallas_call(
        body, name=name, grid=(n_seq, nq),
        in_specs=[pl.BlockSpec((tq, MEM_W), lambda b, i: (b * nq + i, 3)),
                  pl.BlockSpec((N_MEM, 2 * MEM_W), lambda b, i: (b, 0)),
                  pl.BlockSpec((tq, MEM_W), lambda b, i: (b * nq + i, 3)),
                  pl.BlockSpec(memory_space=pl.ANY)],
        out_specs=[pl.BlockSpec((tq, MEM_W), lambda b, i: (b * nq + i, 3)),
                   pl.BlockSpec((N_MEM, 2 * MEM_W), lambda b, i: (b, 0))],
        out_shape=[_sds((M, D_MODEL), bf16), _sds((n_seq * N_MEM, 2 * MEM_W), f32)],
        input_output_aliases={3: 0},
        compiler_params=_params(("parallel", "arbitrary")),
    )(z, kvm, dyc, dz)


def rope_tables(pos, *, name):
    M = pos.shape[0]
    tm = min(1024, M)
    half = HEAD_DIM // 2
    inv = ROPE_THETA ** (-np.arange(half, dtype=np.float64) / half)
    inv128 = jnp.asarray(np.tile(inv, 4)[None, :], f32)
    sign128 = jnp.asarray(np.tile(np.concatenate([-np.ones(half), np.ones(half)]), 2)[None, :], f32)

    def body(p_ref, f_ref, s_ref, cos_ref, sin_ref):
        ang = p_ref[...] * f_ref[...]
        cos_ref[...] = jnp.cos(ang)
        sin_ref[...] = jnp.sin(ang) * s_ref[...]

    return pl.pallas_call(
        body, name=name, grid=(M // tm,),
        in_specs=[pl.BlockSpec((tm, 1), lambda i: (i, 0)),
                  pl.BlockSpec((1, 128), lambda i: (0, 0)),
                  pl.BlockSpec((1, 128), lambda i: (0, 0))],
        out_specs=[pl.BlockSpec((tm, 128), lambda i: (i, 0)),
                   pl.BlockSpec((tm, 128), lambda i: (i, 0))],
        out_shape=[_sds((M, 128), f32), _sds((M, 128), f32)],
        compiler_params=_params(("parallel",)),
    )(pos, inv128, sign128)


def _swap_halves(x):
    w = x.shape[1]
    first = (lax.broadcasted_iota(jnp.int32, (1, w), 1) % HEAD_DIM) < (HEAD_DIM // 2)
    return jnp.where(first, pltpu.roll(x, w - HEAD_DIM // 2, 1), pltpu.roll(x, HEAD_DIM // 2, 1))


def rope_fwd(src, cos, sin, *, name):
    M = src.shape[0]
    tm = min(512, M)

    def body(x_ref, c_ref, s_ref, o_ref):
        x = x_ref[...].astype(f32)
        c = jnp.tile(c_ref[...], (1, MAIN_W // 128))
        s = jnp.tile(s_ref[...], (1, MAIN_W // 128))
        o_ref[...] = (x * c + _swap_halves(x) * s).astype(bf16)

    return pl.pallas_call(
        body, name=name, grid=(M // tm,),
        in_specs=[pl.BlockSpec((tm, MAIN_W), lambda i: (i, 0)),
                  pl.BlockSpec((tm, 128), lambda i: (i, 0)),
                  pl.BlockSpec((tm, 128), lambda i: (i, 0))],
        out_specs=pl.BlockSpec((tm, MAIN_W), lambda i: (i, 0)),
        out_shape=_sds((M, MAIN_W), bf16),
        compiler_params=_params(("parallel",)),
    )(src, cos, sin)


def group_sum(groups, cos, sin, *, name, rotate, width, col_block=0, into=None):
    M = groups[0][0].shape[0]
    tm = min(512, M)
    counts = [len(g) for g in groups]
    flat = [a for g in groups for a in g]
    extra = [] if into is None else [into]

    def body(*refs):
        part_refs = refs[:len(flat)]
        c_ref, s_ref = refs[len(flat):len(flat) + 2]
        o_ref = refs[-1]
        cols, k = [], 0
        for n in counts:
            acc = part_refs[k][...]
            for r in part_refs[k + 1:k + n]:
                acc = acc + r[...]
            cols.append(acc)
            k += n
        d = jnp.concatenate(cols, axis=1)
        if rotate:
            c = jnp.tile(c_ref[...], (1, MAIN_W // 128))
            s = jnp.tile(s_ref[...], (1, MAIN_W // 128))
            d = d * c - _swap_halves(d) * s
        o_ref[...] = d.astype(bf16)

    part = pl.BlockSpec((tm, GROUP_W), lambda i: (i, 0))
    tab = pl.BlockSpec((tm, 128), lambda i: (i, 0))
    return pl.pallas_call(
        body, name=name, grid=(M // tm,),
        in_specs=[part] * len(flat) + [tab, tab] + [pl.BlockSpec(memory_space=pl.ANY)] * len(extra),
        out_specs=pl.BlockSpec((tm, MAIN_W), lambda i: (i, col_block)),
        out_shape=_sds((M, width), bf16),
        input_output_aliases={len(flat) + 2: 0} if extra else {},
        compiler_params=_params(("parallel",)),
    )(*flat, cos, sin, *extra)


def _band_mask(n):
    i = lax.broadcasted_iota(jnp.int32, (BAND, 2 * BAND), 0)
    j = lax.broadcasted_iota(jnp.int32, (BAND, 2 * BAND), 1)
    return (j >= i) & (j <= i + BAND) & ((n > 0) | (j >= BAND))


def _heads(x):
    return [x[:, h * HEAD_DIM:(h + 1) * HEAD_DIM] for h in range(GROUP_W // HEAD_DIM)]


def _bcast_heads(cols):
    return jnp.concatenate([jnp.broadcast_to(c, (c.shape[0], HEAD_DIM)) for c in cols], axis=1)


def _dil_views(n_seq, dil, q, k, kv):
    L = SEQ // dil
    return (q.reshape(n_seq, L, dil * MAIN_W), k.reshape(n_seq, L, dil * MAIN_W),
            kv.reshape(n_seq, L, dil * 2 * MAIN_W))


def _grp_view(a, n_seq, dil):
    return a.reshape(n_seq, SEQ // dil, dil * GROUP_W)


def dil_fwd(q, k, kv, g, dil, *, name, n_seq):
    M = q.shape[0]
    L = SEQ // dil
    nb = L // BAND
    q3, k3, v3 = _dil_views(n_seq, dil, q, k, kv)

    def body(q_ref, kp_ref, ko_ref, vp_ref, vo_ref, o_ref, l_ref):
        mask = _band_mask(pl.program_id(2))
        kc = jnp.concatenate([kp_ref[...], ko_ref[...]], axis=0)
        vc = jnp.concatenate([vp_ref[...], vo_ref[...]], axis=0)
        outs, lses = [], []
        for qh, kh, vh in zip(_heads(q_ref[...]), _heads(kc), _heads(vc)):
            s = lax.dot_general(qh, kh, NT, preferred_element_type=f32) * SCALE
            s = jnp.where(mask, s, NEG)
            m = jnp.max(s, axis=-1, keepdims=True)
            lse = m + jnp.log(jnp.sum(jnp.exp(s - m), axis=-1, keepdims=True))
            p = jnp.exp(s - lse).astype(bf16)
            outs.append(jnp.dot(p, vh, preferred_element_type=f32))
            lses.append(lse)
        o_ref[...] = jnp.concatenate(outs, axis=1)
        l_ref[...] = _bcast_heads(lses)

    blk = (None, BAND, GROUP_W)
    prev = lambda n: jnp.maximum(n - 1, 0)
    out = pl.BlockSpec(blk, lambda b, r, n: (b, n, r))
    o, lse = pl.pallas_call(
        body, name=name, grid=(n_seq, dil, nb),
        in_specs=[pl.BlockSpec(blk, lambda b, r, n: (b, n, r * 3 + g)),
                  pl.BlockSpec(blk, lambda b, r, n: (b, prev(n), r * 3 + g)),
                  pl.BlockSpec(blk, lambda b, r, n: (b, n, r * 3 + g)),
                  pl.BlockSpec(blk, lambda b, r, n: (b, prev(n), r * 6 + 3 + g)),
                  pl.BlockSpec(blk, lambda b, r, n: (b, n, r * 6 + 3 + g))],
        out_specs=[out, out],
        out_shape=[_sds((n_seq, L, dil * GROUP_W), f32)] * 2,
        compiler_params=_params(("parallel", "parallel", "arbitrary")),
    )(q3, k3, k3, v3, v3)
    return o.reshape(M, GROUP_W), lse.reshape(M, GROUP_W)


def combine_fwd(os_, lses, *, name):
    M = os_[0].shape[0]
    tm = min(512, M)

    def body(o0, o1, o2, l0, l1, l2, y_ref):
        ls = [l0[...], l1[...], l2[...]]
        m = jnp.maximum(jnp.maximum(ls[0], ls[1]), ls[2])
        es = [jnp.exp(l - m) for l in ls]
        inv = 1.0 / (es[0] + es[1] + es[2])
        y_ref[...] = jnp.concatenate([o[...] * e * inv for o, e in zip((o0, o1, o2), es)], axis=1).astype(bf16)

    part = pl.BlockSpec((tm, GROUP_W), lambda i: (i, 0))
    return pl.pallas_call(
        body, name=name, grid=(M // tm,),
        in_specs=[part] * 6,
        out_specs=pl.BlockSpec((tm, MAIN_W), lambda i: (i, 0)),
        out_shape=_sds((M, D_MODEL), bf16),
        compiler_params=_params(("parallel",)),
    )(*os_, *lses)


def combine_bwd(dyc, os_, lses, *, name):
    M = os_[0].shape[0]
    tm = min(512, M)

    def body(dy_ref, o0, o1, o2, l0, l1, l2, d0, d1, d2, c0, c1, c2):
        r = lax.broadcasted_iota(jnp.int32, (GROUP_W, GROUP_W), 0) // HEAD_DIM
        c = lax.broadcasted_iota(jnp.int32, (GROUP_W, GROUP_W), 1) // HEAD_DIM
        ones = (r == c).astype(f32)
        dy = dy_ref[...].astype(f32)
        ls = [l0[...], l1[...], l2[...]]
        m = jnp.maximum(jnp.maximum(ls[0], ls[1]), ls[2])
        es = [jnp.exp(l - m) for l in ls]
        inv = 1.0 / (es[0] + es[1] + es[2])
        total = 0.0
        alphas = []
        for g, (o, e, d_ref) in enumerate(zip((o0, o1, o2), es, (d0, d1, d2))):
            a = e * inv
            dyg = dy[:, g * GROUP_W:(g + 1) * GROUP_W]
            d_ref[...] = (dyg * a).astype(bf16)
            dsum = jnp.dot(dyg * o[...], ones, precision=lax.Precision.HIGHEST, preferred_element_type=f32)
            total = total + a * dsum
            alphas.append(a)
        for a, c_ref in zip(alphas, (c0, c1, c2)):
            c_ref[...] = -a * total

    part = pl.BlockSpec((tm, GROUP_W), lambda i: (i, 0))
    outs = pl.pallas_call(
        body, name=name, grid=(M // tm,),
        in_specs=[pl.BlockSpec((tm, MAIN_W), lambda i: (i, 0))] + [part] * 6,
        out_specs=[part] * 6,
        out_shape=[_sds((M, GROUP_W), bf16)] * 3 + [_sds((M, GROUP_W), f32)] * 3,
        compiler_params=_params(("parallel",)),
    )(dyc, *os_, *lses)
    return outs[:3], outs[3:]


def dil_bwd_dq(q, k, kv, do, cc, lse, g, dil, *, name, n_seq):
    M = q.shape[0]
    L = SEQ // dil
    nb = L // BAND
    q3, k3, v3 = _dil_views(n_seq, dil, q, k, kv)
    do3, c3, l3 = (_grp_view(a, n_seq, dil) for a in (do, cc, lse))

    def body(q_ref, kp_ref, ko_ref, vp_ref, vo_ref, do_ref, c_ref, l_ref, dq_ref):
        mask = _band_mask(pl.program_id(2))
        kc = jnp.concatenate([kp_ref[...], ko_ref[...]], axis=0)
        vc = jnp.concatenate([vp_ref[...], vo_ref[...]], axis=0)
        cv = c_ref[...]
        lv = l_ref[...]
        outs = []
        for h, (qh, kh, vh, doh) in enumerate(zip(_heads(q_ref[...]), _heads(kc), _heads(vc), _heads(do_ref[...]))):
            s = lax.dot_general(qh, kh, NT, preferred_element_type=f32) * SCALE
            p = jnp.exp(jnp.where(mask, s, NEG) - lv[:, h * HEAD_DIM:h * HEAD_DIM + 1])
            dp = lax.dot_general(doh, vh, NT, preferred_element_type=f32)
            ds = (p * (dp + cv[:, h * HEAD_DIM:h * HEAD_DIM + 1]) * SCALE).astype(bf16)
            outs.append(jnp.dot(ds, kh, preferred_element_type=f32))
        dq_ref[...] = jnp.concatenate(outs, axis=1)

    blk = (None, BAND, GROUP_W)
    prev = lambda n: jnp.maximum(n - 1, 0)
    grp = pl.BlockSpec(blk, lambda b, r, n: (b, n, r))
    dq = pl.pallas_call(
        body, name=name, grid=(n_seq, dil, nb),
        in_specs=[pl.BlockSpec(blk, lambda b, r, n: (b, n, r * 3 + g)),
                  pl.BlockSpec(blk, lambda b, r, n: (b, prev(n), r * 3 + g)),
                  pl.BlockSpec(blk, lambda b, r, n: (b, n, r * 3 + g)),
                  pl.BlockSpec(blk, lambda b, r, n: (b, prev(n), r * 6 + 3 + g)),
                  pl.BlockSpec(blk, lambda b, r, n: (b, n, r * 6 + 3 + g)),
                  grp, grp, grp],
        out_specs=grp,
        out_shape=_sds((n_seq, L, dil * GROUP_W), f32),
        compiler_params=_params(("parallel", "parallel", "arbitrary")),
    )(q3, k3, k3, v3, v3, do3, c3, l3)
    return dq.reshape(M, GROUP_W)


def dil_bwd_dkv(q, k, kv, do, cc, lse, g, dil, *, name, n_seq):
    M = q.shape[0]
    L = SEQ // dil
    nb = L // BAND
    q3, k3, v3 = _dil_views(n_seq, dil, q, k, kv)
    do3, c3, l3 = (_grp_view(a, n_seq, dil) for a in (do, cc, lse))

    def body(k_ref, v_ref, qo_ref, qn_ref, doo_ref, don_ref, co_ref, cn_ref, lo_ref, ln_ref, dk_ref, dv_ref):
        n = pl.program_id(2)
        i = lax.broadcasted_iota(jnp.int32, (BAND, BAND), 0)
        j = lax.broadcasted_iota(jnp.int32, (BAND, BAND), 1)
        masks = (j <= i, (j >= i) & (n + 1 < nb))
        dks, dvs = [], []
        for h, (kh, vh) in enumerate(zip(_heads(k_ref[...]), _heads(v_ref[...]))):
            hs = slice(h * HEAD_DIM, (h + 1) * HEAD_DIM)
            h1 = slice(h * HEAD_DIM, h * HEAD_DIM + 1)
            dk = jnp.zeros((BAND, HEAD_DIM), f32)
            dv = jnp.zeros((BAND, HEAD_DIM), f32)
            for mask, q_ref, do_ref, c_ref, l_ref in ((masks[0], qo_ref, doo_ref, co_ref, lo_ref),
                                                      (masks[1], qn_ref, don_ref, cn_ref, ln_ref)):
                qh = q_ref[:, hs]
                doh = do_ref[:, hs]
                s = lax.dot_general(qh, kh, NT, preferred_element_type=f32) * SCALE
                p = jnp.exp(jnp.where(mask, s, NEG) - l_ref[:, h1])
                dv = dv + lax.dot_general(p.astype(bf16), doh, TN, preferred_element_type=f32)
                dp = lax.dot_general(doh, vh, NT, preferred_element_type=f32)
                ds = (p * (dp + c_ref[:, h1]) * SCALE).astype(bf16)
                dk = dk + lax.dot_general(ds, qh, TN, preferred_element_type=f32)
            dks.append(dk)
            dvs.append(dv)
        dk_ref[...] = jnp.concatenate(dks, axis=1)
        dv_ref[...] = jnp.concatenate(dvs, axis=1)

    blk = (None, BAND, GROUP_W)
    nxt = lambda n: jnp.minimum(n + 1, nb - 1)
    own = pl.BlockSpec(blk, lambda b, r, n: (b, n, r))
    nx = pl.BlockSpec(blk, lambda b, r, n: (b, nxt(n), r))
    dk, dv = pl.pallas_call(
        body, name=name, grid=(n_seq, dil, nb),
        in_specs=[pl.BlockSpec(blk, lambda b, r, n: (b, n, r * 3 + g)),
                  pl.BlockSpec(blk, lambda b, r, n: (b, n, r * 6 + 3 + g)),
                  pl.BlockSpec(blk, lambda b, r, n: (b, n, r * 3 + g)),
                  pl.BlockSpec(blk, lambda b, r, n: (b, nxt(n), r * 3 + g)),
                  own, nx, own, nx, own, nx],
        out_specs=[own, own],
        out_shape=[_sds((n_seq, L, dil * GROUP_W), f32)] * 2,
        compiler_params=_params(("parallel", "parallel", "arbitrary")),
    )(k3, v3, q3, q3, do3, do3, c3, c3, l3, l3)
    return dk.reshape(M, GROUP_W), dv.reshape(M, GROUP_W)


def _blockdiag(wp):
    out = jnp.zeros((MAIN_W, MAIN_W), wp.dtype)
    for gi in range(len(POOL_WINDOWS)):
        sl = slice(gi * POOL_GROUP, (gi + 1) * POOL_GROUP)
        out = out.at[sl, sl].set(wp[gi])
    return out


def _unblockdiag(w):
    return jnp.stack([w[gi * POOL_GROUP:(gi + 1) * POOL_GROUP, gi * POOL_GROUP:(gi + 1) * POOL_GROUP]
                      for gi in range(len(POOL_WINDOWS))])


def local_step(x, mem, positions, target, P, layer_weights, kv_weight, emit_grads):
    n_seq = x.shape[0]
    M = n_seq * SEQ
    xs = x.reshape(M, D_MODEL)
    mems = mem.reshape(n_seq * N_MEM, D_MODEL)
    pos = positions.reshape(M, 1).astype(f32)
    cos, sin = rope_tables(pos, name="rope_tables")
    gains = P["norm_gains"]

    def gain(l, k):
        return gains[l, k].reshape(1, D_MODEL)

    saved = []
    kvs = None
    for l in range(DEPTH):
        W, started = layer_weights(l, "mix", xs)
        sv = {"x": xs, "W": W}
        z, h1 = rms_matmul(xs, gain(l, 0), W["w_in"], name=f"l{l}_in", out_dtype=f32, after=started)
        kvm, mn = rms_matmul(mems, P["mem_norm"][l].reshape(1, D_MODEL), W["w_mem_kv"],
                             name=f"l{l}_memkv", out_dtype=bf16)
        sv.update(z=z, h1=h1, kvm=kvm, mn=mn)
        if l < N_A_LAYERS:
            wbd = _blockdiag(P["w_pool"][l].astype(bf16))
            psc = P["pool_scale"][l].reshape(1, MAIN_W)
            p, y_main = pool_fwd(z, wbd, psc, name=f"l{l}_pool")
            sv.update(p=p, wbd=wbd, psc=psc)
        else:
            qrot = rope_fwd(z, cos, sin, name=f"l{l}_ropeq")
            os_, lses = [], []
            for g, (_, dil) in enumerate(DIL_PATTERNS):
                o, lse = dil_fwd(qrot, kvs["krot"], kvs["kv"], g, dil, name=f"l{l}_dil{g}", n_seq=n_seq)
                os_.append(o)
                lses.append(lse)
            y_main = combine_fwd(os_, lses, name=f"l{l}_comb")
            sv.update(qrot=qrot, os=os_, lses=lses)
        ycat = memattn_fwd(z, kvm, y_main, name=f"l{l}_memattn", n_seq=n_seq)
        y, x1 = matmul_rms_res(ycat, W["w_out"], gain(l, 1), xs, name=f"l{l}_out")
        W.update(layer_weights(l, "ffn", x1)[0])
        gu, h2 = rms_matmul(x1, gain(l, 2), W["w_gate_up"], name=f"l{l}_gu", out_dtype=bf16, transposed=True)
        a = swiglu_fwd(gu, name=f"l{l}_swiglu")
        y2, x2 = matmul_rms_res(a, W["w_down"], gain(l, 3), x1, name=f"l{l}_down")
        sv.update(ycat=ycat, y=y, x1=x1, gu=gu, h2=h2, a=a, y2=y2)
        saved.append(sv)
        xs = x2
        if l == N_A_LAYERS - 1:
            w_kv = kv_weight(xs)
            kv, hkv = rms_matmul(xs, P["kv_norm"].reshape(1, D_MODEL), w_kv, name="kv_proj", out_dtype=bf16,
                                 transposed=True)
            krot = rope_fwd(kv, cos, sin, name="ropek")
            kvs = {"kv": kv, "hkv": hkv, "krot": krot, "x": xs, "w_kv": w_kv}

    dx, sq = loss_head(xs, target.reshape(M, D_MODEL), name="loss_head")

    G = {"mem_norm": [None] * DEPTH, "norm_gains": [[None] * 4 for _ in range(DEPTH)],
         "w_pool": [None] * N_A_LAYERS, "pool_scale": [None] * N_A_LAYERS}
    dk_parts = [[] for _ in range(N_GROUPS)]
    dv_parts = [[] for _ in range(N_GROUPS)]
    emitted = None

    for l in reversed(range(DEPTH)):
        sv = saved[l]
        W = sv["W"]
        gw = {}
        dy2, G["norm_gains"][l][3] = rms_bwd(sv["y2"], gain(l, 3), dx, None, name=f"l{l}_b_n3", out_dtype=bf16,
                                             after=emitted)
        da = matmul(dy2, W["w_down"], NT, name=f"l{l}_b_da", out_dtype=bf16)
        gw["w_down"] = matmul(sv["a"], dy2, TN, name=f"l{l}_b_wd", out_dtype=bf16)
        dgu = swiglu_bwd(sv["gu"], da, name=f"l{l}_b_swiglu")
        dh2 = matmul(dgu, W["w_gate_up"], NN, name=f"l{l}_b_dh2", out_dtype=bf16)
        gw["w_gate_up"] = matmul(dgu, sv["h2"], TN, name=f"l{l}_b_wgu", out_dtype=bf16)
        dx1, G["norm_gains"][l][2] = rms_bwd(sv["x1"], gain(l, 2), dh2, dx, name=f"l{l}_b_n2", out_dtype=f32)
        emitted = emit_grads(l, "ffn", gw)
        gw = {}
        dy, G["norm_gains"][l][1] = rms_bwd(sv["y"], gain(l, 1), dx1, None, name=f"l{l}_b_n1", out_dtype=bf16,
                                            after=emitted)
        dycat = matmul(dy, W["w_out"], NT, name=f"l{l}_b_dycat", out_dtype=bf16)
        gw["w_out"] = matmul(sv["ycat"], dy, TN, name=f"l{l}_b_wout", out_dtype=bf16)
        if l < N_A_LAYERS:
            dz, dwbd, dps = pool_bwd(dycat, sv["p"], sv["wbd"], sv["psc"], name=f"l{l}_b_pool")
            G["w_pool"][l] = _unblockdiag(dwbd)
            G["pool_scale"][l] = dps.reshape(MAIN_W)
        else:
            dos, ccs = combine_bwd(dycat, sv["os"], sv["lses"], name=f"l{l}_b_comb")
            dqs = []
            for g, (_, dil) in enumerate(DIL_PATTERNS):
                args = (sv["qrot"], kvs["krot"], kvs["kv"], dos[g], ccs[g], sv["lses"][g], g, dil)
                dqs.append([dil_bwd_dq(*args, name=f"l{l}_b_dq{g}", n_seq=n_seq)])
                dk, dv = dil_bwd_dkv(*args, name=f"l{l}_b_dkv{g}", n_seq=n_seq)
                dk_parts[g].append(dk)
                dv_parts[g].append(dv)
            dz = group_sum(dqs, cos, sin, name=f"l{l}_b_ropeq", rotate=True, width=D_MODEL)
        dz, dkvm = memattn_bwd(sv["z"], sv["kvm"], dycat, dz, name=f"l{l}_b_memattn", n_seq=n_seq)
        dmn = matmul(dkvm, W["w_mem_kv"], NT, name=f"l{l}_b_dmn", out_dtype=bf16)
        gw["w_mem_kv"] = matmul(sv["mn"], dkvm, TN, name=f"l{l}_b_wmkv", out_dtype=bf16)
        _, G["mem_norm"][l] = rms_bwd(mems, P["mem_norm"][l].reshape(1, D_MODEL), dmn, None,
                                      name=f"l{l}_b_nmem", out_dtype=bf16)
        dh1 = matmul(dz, W["w_in"], NT, name=f"l{l}_b_dh1", out_dtype=bf16)
        gw["w_in"] = matmul(sv["h1"], dz, TN, name=f"l{l}_b_win", out_dtype=bf16)
        dx, G["norm_gains"][l][0] = rms_bwd(sv["x"], gain(l, 0), dh1, dx1, name=f"l{l}_b_n0", out_dtype=f32)
        if l == N_A_LAYERS:
            dkv = group_sum(dk_parts, cos, sin, name="b_ropek", rotate=True, width=2 * MAIN_W)
            dkv = group_sum(dv_parts, cos, sin, name="b_sumv", rotate=False, width=2 * MAIN_W, col_block=1, into=dkv)
            dhkv = matmul(dkv, kvs["w_kv"], NN, name="b_dhkv", out_dtype=bf16)
            gw["w_kv"] = matmul(dkv, kvs["hkv"], TN, name="b_wkv", out_dtype=bf16)
            dx, gkn = rms_bwd(kvs["x"], P["kv_norm"].reshape(1, D_MODEL), dhkv, dx, name="b_nkv", out_dtype=f32)
            G["kv_norm"] = gkn.reshape(D_MODEL)
        emitted = emit_grads(l, "mix", gw)

    small = {"w_pool": jnp.stack(G["w_pool"]), "pool_scale": jnp.stack(G["pool_scale"]),
             "mem_norm": jnp.concatenate(G["mem_norm"], axis=0),
             "norm_gains": jnp.stack([jnp.concatenate(r, axis=0) for r in G["norm_gains"]]),
             "kv_norm": G["kv_norm"]}
    return sq[0, 0], dx.reshape(n_seq, SEQ, D_MODEL), small, emitted


def _peer(k):
    x, y, c = lax.axis_index("x"), lax.axis_index("y"), lax.axis_index("c")
    px = 1 - x if k & 4 else x
    py = 1 - y if k & 2 else y
    pc = 1 - c if k & 1 else c
    return (px, py, pc), 4 * px + 2 * py + pc


def _my_index():
    return 4 * lax.axis_index("x") + 2 * lax.axis_index("y") + lax.axis_index("c")


def _src_for(kinds, in_refs, i, idx):
    return in_refs[i] if kinds[i] == "gather" else in_refs[i].at[idx]


def _local_copies(kinds, in_refs, out_refs, local_sems):
    me = _my_index()
    return [pltpu.make_async_copy(_src_for(kinds, in_refs, i, me), out_refs[i].at[me], local_sems.at[i])
            for i in range(len(kinds))]


def _remote_copies(kinds, in_refs, out_refs, send_sems, recv_sems, *, arriving):
    me = _my_index()
    copies = []
    for k in range(1, N_DEV):
        dev, idx = _peer(k)
        for i in range(len(kinds)):
            j = i * (N_DEV - 1) + k - 1
            copies.append(pltpu.make_async_remote_copy(
                src_ref=_src_for(kinds, in_refs, i, idx), dst_ref=out_refs[i].at[idx if arriving else me],
                send_sem=send_sems.at[j], recv_sem=recv_sems.at[j], device_id=dev, device_id_type=MESH))
    return copies


def _out_shape(a, kind):
    return ((N_DEV,) + a.shape) if kind == "gather" else a.shape


def exchange(items, *, name, after=()):
    n = len(items)
    kinds = [k for _, k in items]
    after = list(after)

    def body(*refs):
        in_refs, out_refs = refs[:n], refs[n + len(after):2 * n + len(after)]
        send_sems, recv_sems, local_sems = refs[-3:]
        local = _local_copies(kinds, in_refs, out_refs, local_sems)
        sends = _remote_copies(kinds, in_refs, out_refs, send_sems, recv_sems, arriving=False)
        for cp in local + sends:
            cp.start()
        for cp in _remote_copies(kinds, in_refs, out_refs, send_sems, recv_sems, arriving=True):
            cp.wait_recv()
        for cp in sends:
            cp.wait_send()
        for cp in local:
            cp.wait()

    any_spec = pl.BlockSpec(memory_space=pl.ANY)
    return pl.pallas_call(
        body, name=name,
        in_specs=[any_spec] * (n + len(after)), out_specs=[any_spec] * n,
        out_shape=[_sds(_out_shape(a, k), a.dtype) for a, k in items],
        scratch_shapes=[pltpu.SemaphoreType.DMA((n * (N_DEV - 1),)), pltpu.SemaphoreType.DMA((n * (N_DEV - 1),)),
                        pltpu.SemaphoreType.DMA((n,))],
    )(*[a for a, _ in items], *after)


_HBM = pl.BlockSpec(memory_space=pltpu.HBM)
_SEM = pl.BlockSpec(memory_space=pltpu.SEMAPHORE)
_EFFECT = pltpu.SideEffectType.DATAFLOW_SIDE_EFFECTING


def exchange_start(items, after, *, name):
    n = len(items)
    kinds = [k for _, k in items]

    def body(*refs):
        in_refs, land_refs = refs[:n], refs[n:2 * n]
        send_sems, recv_sems, local_sems = refs[2 * n + 1:2 * n + 4]
        token = refs[-1]
        for cp in (_local_copies(kinds, in_refs, land_refs, local_sems)
                   + _remote_copies(kinds, in_refs, land_refs, send_sems, recv_sems, arriving=False)):
            cp.start()
        token[...] = jnp.zeros_like(token)

    srcs = [pltpu.with_memory_space_constraint(a, pltpu.HBM) for a, _ in items]
    lands = [pltpu.with_memory_space_constraint(lax.empty(_out_shape(a, k), a.dtype), pltpu.HBM) for a, k in items]
    outs = pl.pallas_call(
        body, name=name,
        out_shape=(pltpu.SemaphoreType.DMA((n * (N_DEV - 1),)), pltpu.SemaphoreType.DMA((n * (N_DEV - 1),)),
                   pltpu.SemaphoreType.DMA((n,)),
                   *[pltpu.HBM(a.shape, a.dtype) for a in srcs], *[pltpu.HBM(a.shape, a.dtype) for a in lands],
                   _sds((8, 128), f32)),
        in_specs=[_HBM] * (2 * n) + [pl.BlockSpec(memory_space=pl.ANY)],
        out_specs=(_SEM, _SEM, _SEM, *[_HBM] * (2 * n), pl.BlockSpec(memory_space=pltpu.VMEM)),
        input_output_aliases={i: 3 + i for i in range(2 * n)},
        compiler_params=pltpu.CompilerParams(has_side_effects=_EFFECT),
    )(*srcs, *lands, after)
    return {"kinds": kinds, "sems": outs[:3], "srcs": outs[3:3 + n], "lands": outs[3 + n:3 + 2 * n], "token": outs[-1]}


def exchange_wait(handle, after, *, name):
    kinds = handle["kinds"]
    n = len(kinds)

    def body(*refs):
        in_refs, land_refs = refs[:n], refs[n:2 * n]
        send_sems, recv_sems, local_sems = refs[2 * n:2 * n + 3]
        for cp in _remote_copies(kinds, in_refs, land_refs, send_sems, recv_sems, arriving=True):
            cp.wait_recv()
        for cp in _remote_copies(kinds, in_refs, land_refs, send_sems, recv_sems, arriving=False):
            cp.wait_send()
        for cp in _local_copies(kinds, in_refs, land_refs, local_sems):
            cp.wait()

    srcs, lands = list(handle["srcs"]), list(handle["lands"])
    after = list(after) if isinstance(after, (list, tuple)) else [after]
    outs = pl.pallas_call(
        body, name=name,
        out_shape=tuple(pltpu.HBM(a.shape, a.dtype) for a in srcs + lands),
        in_specs=[_HBM] * (2 * n) + [_SEM] * 3 + [pl.BlockSpec(memory_space=pl.ANY)] * len(after),
        out_specs=tuple([_HBM] * (2 * n)),
        input_output_aliases={i: i for i in range(2 * n)},
        compiler_params=pltpu.CompilerParams(has_side_effects=_EFFECT),
    )(*srcs, *lands, *handle["sems"], *after)
    return list(outs[n:])


def adamw(slots, w, m, v, *, name, layer=None, into=None):
    R, C = w.shape[-2:]
    tr = _tile(R, (256, 128, 64, 32, 16, 8))
    c1 = 1.0 - ADAM_B1 ** ADAM_STEP
    c2 = 1.0 - ADAM_B2 ** ADAM_STEP
    extra = [] if into is None else list(into)

    def body(s_ref, w_ref, m_ref, v_ref, *refs):
        g_ref, d_ref, m2_ref, v2_ref = refs[len(extra):]
        g = s_ref[0].astype(f32)
        for d in range(1, N_DEV):
            g = g + s_ref[d].astype(f32)
        m2 = ADAM_B1 * m_ref[...] + (1.0 - ADAM_B1) * g
        v2 = ADAM_B2 * v_ref[...] + (1.0 - ADAM_B2) * (g * g)
        g_ref[...] = g
        m2_ref[...] = m2
        v2_ref[...] = v2
        d_ref[...] = -ADAM_LR * ((m2 / c1) / (jnp.sqrt(v2 / c2) + ADAM_EPS) + ADAM_WD * w_ref[...])

    if layer is None:
        blk = pl.BlockSpec((tr, C), lambda i: (i, 0))
    else:
        blk = pl.BlockSpec((None, tr, C), lambda i: (layer, i, 0))
    return pl.pallas_call(
        body, name=name, grid=(R // tr,),
        in_specs=[pl.BlockSpec((N_DEV, tr, C), lambda i: (0, i, 0)), blk, blk, blk]
        + [pl.BlockSpec(memory_space=pl.ANY)] * len(extra),
        out_specs=[blk] * 4,
        out_shape=[_sds(w.shape, f32)] * 4,
        input_output_aliases={4 + j: j for j in range(len(extra))},
        compiler_params=_params(("parallel",)),
    )(slots, w, m, v, *extra)


WEIGHTS = ("norm_gains", "mem_norm", "w_in", "w_mem_kv", "w_out", "w_pool", "pool_scale", "kv_norm", "w_kv",
           "w_gate_up", "w_down")
LAYER_MATS = ("w_in", "w_mem_kv", "w_out", "w_gate_up", "w_down")
POOL_SHARD = MAIN_W // N_DEV
KV_SHARD = 2 * MAIN_W // N_DEV
LOOKAHEAD = 2


def _pack_small(gains, pscale):
    lead = gains.shape[:-3]
    g = gains.reshape(lead + (16, 128))
    p = jnp.zeros(lead + (8, 128), f32).at[..., :2, :POOL_SHARD].set(pscale)
    return jnp.concatenate([g, p], axis=-2)


def _unpack_small(a):
    return a[:16].reshape(4, 4, 128), a[16:18, :POOL_SHARD]


def _pack_repl(mem_norm, kv_norm):
    return jnp.concatenate([mem_norm, kv_norm.reshape(1, D_MODEL), jnp.zeros((3, D_MODEL), f32)], axis=0)


def _unpack_repl(a):
    return a[:4], a[4]


def kernel(x, mem, positions, norm_gains, mem_norm, w_in, w_mem_kv, w_out, w_pool, pool_scale, kv_norm, w_kv, w_gate_up, w_down, loss_target, m_norm_gains, m_mem_norm, m_w_in, m_w_mem_kv, m_w_out, m_w_pool, m_pool_scale, m_kv_norm, m_w_kv, m_w_gate_up, m_w_down, v_norm_gains, v_mem_norm, v_w_in, v_w_mem_kv, v_w_out, v_w_pool, v_pool_scale, v_kv_norm, v_w_kv, v_w_gate_up, v_w_down):
    w = dict(norm_gains=norm_gains, mem_norm=mem_norm, w_in=w_in, w_mem_kv=w_mem_kv, w_out=w_out, w_pool=w_pool,
             pool_scale=pool_scale, kv_norm=kv_norm, w_kv=w_kv, w_gate_up=w_gate_up, w_down=w_down)
    m = dict(norm_gains=m_norm_gains, mem_norm=m_mem_norm, w_in=m_w_in, w_mem_kv=m_w_mem_kv, w_out=m_w_out,
             w_pool=m_w_pool, pool_scale=m_pool_scale, kv_norm=m_kv_norm, w_kv=m_w_kv, w_gate_up=m_w_gate_up,
             w_down=m_w_down)
    v = dict(norm_gains=v_norm_gains, mem_norm=v_mem_norm, w_in=v_w_in, w_mem_kv=v_w_mem_kv, w_out=v_w_out,
             w_pool=v_w_pool, pool_scale=v_pool_scale, kv_norm=v_kv_norm, w_kv=v_w_kv, w_gate_up=v_w_gate_up,
             w_down=v_w_down)

    def transposed_view(d):
        d = dict(d)
        d["w_gate_up"] = jnp.swapaxes(d["w_gate_up"], 1, 2)
        d["w_kv"] = jnp.swapaxes(d["w_kv"], 0, 1)
        return d

    wv, mv, vv = transposed_view(w), transposed_view(m), transposed_view(v)

    small = _pack_small(norm_gains, pool_scale)
    (gsmall,) = exchange([(small, "gather")], name="gather_small")
    P = {"norm_gains": jnp.moveaxis(gsmall[:, :16].reshape(N_DEV, 4, 4, 128), 0, 2).reshape(4, 4, D_MODEL),
         "pool_scale": jnp.moveaxis(gsmall[:, 16:18, :POOL_SHARD], 0, 1).reshape(2, MAIN_W),
         "mem_norm": mem_norm, "kv_norm": kv_norm, "w_pool": w_pool}

    PARTS = {"mix": ("w_in", "w_mem_kv", "w_out"), "ffn": ("w_gate_up", "w_down")}

    def part_items(l, part):
        items = [(wv[k][l].astype(bf16), "gather") for k in PARTS[part]]
        if part == "ffn" and l == N_A_LAYERS - 1:
            items.append((wv["w_kv"].astype(bf16), "gather"))
        return items

    handles = {}

    def start_layer(l, after):
        for part in ("mix", "ffn"):
            handles[l, part] = exchange_start(part_items(l, part), after, name=f"gather_start_{part}_l{l}")
            after = handles[l, part]["token"]
        return after

    token = gsmall
    for l in range(LOOKAHEAD):
        token = start_layer(l, token)
    landed = {}

    def layer_weights(l, part, after):
        first = l == 0 and part == "mix"
        got = exchange_wait(handles[l, part], token if first else after, name=f"gather_wait_{part}_l{l}")
        landed[l, part] = got
        started = None
        if part == "mix" and l + LOOKAHEAD < DEPTH:
            started = start_layer(l + LOOKAHEAD, got[0])
        W = {k: g.reshape(-1, g.shape[-1]) for k, g in zip(PARTS[part], got)}
        return W, started

    def kv_weight(after):
        g = landed[N_A_LAYERS - 1, "ffn"][len(PARTS["ffn"])]
        return g.reshape(2 * MAIN_W, D_MODEL)

    ghandles = {}

    def emit_grads(l, part, gw):
        items = [(gw[k].reshape((N_DEV, -1) + gw[k].shape[-1:]), "scatter") for k in PARTS[part]]
        if part == "mix" and l == N_A_LAYERS:
            items.append((gw["w_kv"].reshape(N_DEV, KV_SHARD, D_MODEL), "scatter"))
        ghandles[l, part] = exchange_start(items, items[0][0], name=f"scatter_start_{part}_l{l}")
        return ghandles[l, part]["token"]

    sq, grad_x, GS, emitted = local_step(x, mem, positions, loss_target, P, layer_weights, kv_weight, emit_grads)
    loss = lax.psum(0.5 * sq / D_MODEL, ("x", "y", "c"))

    def two_d(a):
        return a.reshape(-1, a.shape[-1])

    out = {}
    after = [emitted]

    def finish_layer(l, after):
        for part in ("ffn", "mix"):
            got = exchange_wait(ghandles[l, part], after, name=f"scatter_wait_{part}_l{l}")
            after = []
            for k, slots in zip(PARTS[part], got):
                out[k] = adamw(slots, wv[k], mv[k], vv[k], name=f"adamw_{k}_l{l}", layer=l, into=out.get(k))
                after.append(out[k][0])
            if part == "mix" and l == N_A_LAYERS:
                out["w_kv"] = adamw(got[-1], wv["w_kv"], mv["w_kv"], vv["w_kv"], name="adamw_w_kv")
                after.append(out["w_kv"][0])
        return after

    for l in reversed(range(1, DEPTH)):
        after = finish_layer(l, after)

    gs = _pack_small(jnp.moveaxis(GS["norm_gains"].reshape(4, 4, N_DEV, 128), 2, 0),
                     jnp.moveaxis(GS["pool_scale"].reshape(2, N_DEV, POOL_SHARD), 1, 0))
    parts_small, parts_repl, parts_pool = exchange(
        [(gs, "scatter"), (_pack_repl(GS["mem_norm"], GS["kv_norm"]), "gather"),
         (GS["w_pool"].reshape(-1, POOL_GROUP), "gather")], name="exchange_small_grads", after=after)
    finish_layer(0, [parts_small])
    out["w_gate_up"] = [jnp.swapaxes(r, 1, 2) for r in out["w_gate_up"]]
    out["w_kv"] = [jnp.swapaxes(r, 0, 1) for r in out["w_kv"]]

    res = adamw(parts_small, small, _pack_small(m_norm_gains, m_pool_scale), _pack_small(v_norm_gains, v_pool_scale),
                name="adamw_small")
    out["norm_gains"], out["pool_scale"] = zip(*[_unpack_small(r) for r in res])
    res = adamw(parts_repl, _pack_repl(mem_norm, kv_norm), _pack_repl(m_mem_norm, m_kv_norm),
                _pack_repl(v_mem_norm, v_kv_norm), name="adamw_repl")
    out["mem_norm"], out["kv_norm"] = zip(*[_unpack_repl(r) for r in res])
    res = adamw(parts_pool, two_d(w_pool), two_d(m_w_pool), two_d(v_w_pool), name="adamw_w_pool")
    out["w_pool"] = [r.reshape(w_pool.shape) for r in res]

    return (loss, grad_x, *[out[k][0] for k in WEIGHTS], *[out[k][1] for k in WEIGHTS],
            *[out[k][2] for k in WEIGHTS], *[out[k][3] for k in WEIGHTS])
```

```python
import numpy as np
import jax
import jax.numpy as jnp
from jax import lax
from jax.experimental import pallas as pl
from jax.experimental.pallas import tpu as pltpu

f32 = jnp.float32
bf16 = jnp.bfloat16

D_MODEL = 1024
SEQ = 2048
DEPTH = 4
N_MEM = 256
HEAD_DIM = 64
N_MEM_HEADS = 4
MEM_W = 256
MAIN_W = 768
POOL_WINDOWS = (2, 4, 8, 16)
POOL_GROUP = 192
POOL_HALO = 16
DIL_PATTERNS = ((128, 1), (512, 4), (2048, 16))
N_GROUPS = 3
GROUP_W = 256
BAND = 128
N_A_LAYERS = 2
D_FF = 2816
ROPE_THETA = 10000.0
EPS = 1e-6
NEG = -1e30
SCALE = HEAD_DIM ** -0.5
N_DEV = 8

ADAM_LR = 0.001
ADAM_B1 = 0.9
ADAM_B2 = 0.999
ADAM_EPS = 1e-08
ADAM_WD = 0.01
ADAM_STEP = 10

VMEM_LIMIT_BYTES = 56 * 1024 * 1024
MESH = pl.DeviceIdType.MESH

NN = (((1,), (0,)), ((), ()))
NT = (((1,), (1,)), ((), ()))
TN = (((0,), (0,)), ((), ()))


def _params(sem=None):
    return pltpu.CompilerParams(dimension_semantics=sem, vmem_limit_bytes=VMEM_LIMIT_BYTES)


def _tile(n, cands):
    for c in cands:
        if n % c == 0:
            return c
    return n


def _sds(shape, dtype):
    return jax.ShapeDtypeStruct(tuple(shape), dtype)


def _rms_r(v):
    return lax.rsqrt(jnp.mean(v * v, axis=-1, keepdims=True) + EPS)


def rms_matmul(x, gain, w, *, name, out_dtype, transposed=False, after=None):
    M, K = x.shape
    N = w.shape[0] if transposed else w.shape[1]
    tm = min(1024, M)
    tn = _tile(N, (512, 256, 128))
    order = [] if after is None else [after]

    def body(x_ref, g_ref, w_ref, *refs):
        z_ref, h_ref = refs[len(order):]

        @pl.when(pl.program_id(1) == 0)
        def _():
            xv = x_ref[...]
            h_ref[...] = (xv * _rms_r(xv) * g_ref[...]).astype(bf16)

        z_ref[...] = lax.dot_general(h_ref[...], w_ref[...], NT if transposed else NN,
                                     preferred_element_type=f32).astype(z_ref.dtype)

    w_spec = pl.BlockSpec((tn, K), lambda i, j: (j, 0)) if transposed else pl.BlockSpec((K, tn), lambda i, j: (0, j))
    return pl.pallas_call(
        body, name=name, grid=(M // tm, N // tn),
        in_specs=[pl.BlockSpec((tm, K), lambda i, j: (i, 0)),
                  pl.BlockSpec((1, K), lambda i, j: (0, 0)),
                  w_spec] + [pl.BlockSpec(memory_space=pl.ANY)] * len(order),
        out_specs=[pl.BlockSpec((tm, tn), lambda i, j: (i, j)), pl.BlockSpec((tm, K), lambda i, j: (i, 0))],
        out_shape=[_sds((M, N), out_dtype), _sds((M, K), bf16)],
        compiler_params=_params(("parallel", "arbitrary")),
    )(x, gain, w, *order)


def matmul_rms_res(a, w, gain, res, *, name):
    M, K = a.shape
    N = w.shape[1]
    tm = min(512, M)

    def body(a_ref, w_ref, g_ref, r_ref, y_ref, x_ref):
        y = jnp.dot(a_ref[...], w_ref[...], preferred_element_type=f32)
        y_ref[...] = y.astype(bf16)
        x_ref[...] = r_ref[...] + y * _rms_r(y) * g_ref[...]

    row = pl.BlockSpec((tm, N), lambda i: (i, 0))
    return pl.pallas_call(
        body, name=name, grid=(M // tm,),
        in_specs=[pl.BlockSpec((tm, K), lambda i: (i, 0)),
                  pl.BlockSpec((K, N), lambda i: (0, 0)),
                  pl.BlockSpec((1, N), lambda i: (0, 0)),
                  row],
        out_specs=[row, row],
        out_shape=[_sds((M, N), bf16), _sds((M, N), f32)],
        compiler_params=_params(("parallel",)),
    )(a, w, gain, res)


def matmul(a, b, dims, *, name, out_dtype):
    if dims is TN:
        K, M = a.shape
        tm = _tile(M, (512, 256, 128))
        a_spec = pl.BlockSpec((K, tm), lambda i: (0, i))
    else:
        M, K = a.shape
        tm = _tile(M, (512, 256, 128))
        a_spec = pl.BlockSpec((tm, K), lambda i: (i, 0))
    N = b.shape[0] if dims is NT else b.shape[1]

    def body(a_ref, b_ref, o_ref):
        o_ref[...] = lax.dot_general(a_ref[...].astype(bf16), b_ref[...].astype(bf16), dims,
                                     preferred_element_type=f32).astype(o_ref.dtype)

    return pl.pallas_call(
        body, name=name, grid=(M // tm,),
        in_specs=[a_spec, pl.BlockSpec(b.shape, lambda i: (0, 0))],
        out_specs=pl.BlockSpec((tm, N), lambda i: (i, 0)),
        out_shape=_sds((M, N), out_dtype),
        compiler_params=_params(("parallel",)),
    )(a, b)


def rms_gate_up(x, gain, wt, *, name):
    M, K = x.shape
    tm = min(1024, M)
    tn = _tile(D_FF, (256, 128))
    nj = D_FF // tn

    def body(x_ref, gn_ref, wg_ref, wu_ref, g_ref, u_ref, a_ref, h_ref):
        @pl.when(pl.program_id(1) == 0)
        def _():
            xv = x_ref[...]
            h_ref[...] = (xv * _rms_r(xv) * gn_ref[...]).astype(bf16)

        h = h_ref[...]
        g = lax.dot_general(h, wg_ref[...], NT, preferred_element_type=f32)
        u = lax.dot_general(h, wu_ref[...], NT, preferred_element_type=f32)
        g_ref[...] = g.astype(bf16)
        u_ref[...] = u.astype(bf16)
        a_ref[...] = (g * (1.0 / (1.0 + jnp.exp(-g))) * u).astype(bf16)

    col = pl.BlockSpec((tm, tn), lambda i, j: (i, j))
    return pl.pallas_call(
        body, name=name, grid=(M // tm, nj),
        in_specs=[pl.BlockSpec((tm, K), lambda i, j: (i, 0)),
                  pl.BlockSpec((1, K), lambda i, j: (0, 0)),
                  pl.BlockSpec((tn, K), lambda i, j: (j, 0)),
                  pl.BlockSpec((tn, K), lambda i, j: (j + nj, 0))],
        out_specs=[col, col, col, pl.BlockSpec((tm, K), lambda i, j: (i, 0))],
        out_shape=[_sds((M, D_FF), bf16)] * 3 + [_sds((M, K), bf16)],
        compiler_params=_params(("parallel", "arbitrary")),
    )(x, gain, wt, wt)


def down_bwd(dy, w_down, g, u, *, name):
    M, K = dy.shape
    tm = min(512, M)

    def body(dy_ref, w_ref, g_ref, u_ref, o_ref):
        da = lax.dot_general(dy_ref[...], w_ref[...], NT, preferred_element_type=f32)
        g = g_ref[...].astype(f32)
        u = u_ref[...].astype(f32)
        s = 1.0 / (1.0 + jnp.exp(-g))
        o_ref[:, :D_FF] = (da * u * s * (1.0 + g * (1.0 - s))).astype(bf16)
        o_ref[:, D_FF:] = (da * g * s).astype(bf16)

    wide = pl.BlockSpec((tm, D_FF), lambda i: (i, 0))
    return pl.pallas_call(
        body, name=name, grid=(M // tm,),
        in_specs=[pl.BlockSpec((tm, K), lambda i: (i, 0)), pl.BlockSpec((D_FF, K), lambda i: (0, 0)), wide, wide],
        out_specs=pl.BlockSpec((tm, 2 * D_FF), lambda i: (i, 0)),
        out_shape=_sds((M, 2 * D_FF), bf16),
        compiler_params=_params(("parallel",)),
    )(dy, w_down, g, u)


def rms_bwd(y, gain, dn, res, *, name, out_dtype, after=None):
    M, N = y.shape
    tm = min(512, M)
    has_res = res is not None
    order = [] if after is None else [after]

    def body(*refs):
        y_ref, g_ref, dn_ref = refs[:3]
        r_ref = refs[3] if has_res else None
        dy_ref, dg_ref = refs[-2:]
        yv = y_ref[...].astype(f32)
        dn = dn_ref[...].astype(f32)
        r = _rms_r(yv)
        q = dn * g_ref[...]
        dy = r * q - yv * (r * r * r) * jnp.mean(q * yv, axis=-1, keepdims=True)
        if has_res:
            dy = dy + r_ref[...]
        dy_ref[...] = dy.astype(dy_ref.dtype)

        @pl.when(pl.program_id(0) == 0)
        def _():
            dg_ref[...] = jnp.zeros_like(dg_ref)

        dg_ref[...] += jnp.sum(dn * yv * r, axis=0, keepdims=True)

    row = pl.BlockSpec((tm, N), lambda i: (i, 0))
    vec = pl.BlockSpec((1, N), lambda i: (0, 0))
    args = [y, gain, dn] + ([res] if has_res else []) + order
    return pl.pallas_call(
        body, name=name, grid=(M // tm,),
        in_specs=[row, vec, row] + ([row] if has_res else []) + [pl.BlockSpec(memory_space=pl.ANY)] * len(order),
        out_specs=[row, vec],
        out_shape=[_sds((M, N), out_dtype), _sds((1, N), f32)],
        compiler_params=_params(("arbitrary",)),
    )(*args)


def loss_head(x, target, *, name):
    M, N = x.shape
    tm = min(512, M)

    def body(x_ref, t_ref, dx_ref, l_ref):
        e = x_ref[...] - t_ref[...]
        dx_ref[...] = e * (1.0 / N)

        @pl.when(pl.program_id(0) == 0)
        def _():
            l_ref[...] = jnp.zeros_like(l_ref)

        l_ref[...] += jnp.sum(jnp.sum(e * e, axis=0, keepdims=True), axis=1, keepdims=True)

    row = pl.BlockSpec((tm, N), lambda i: (i, 0))
    return pl.pallas_call(
        body, name=name, grid=(M // tm,),
        in_specs=[row, row],
        out_specs=[row, pl.BlockSpec((8, 128), lambda i: (0, 0))],
        out_shape=[_sds((M, N), f32), _sds((8, 128), f32)],
        compiler_params=_params(("arbitrary",)),
    )(x, target)


def _pool_select(a1, a2, a3, a4):
    col = lax.broadcasted_iota(jnp.int32, (1, MAIN_W), 1) // POOL_GROUP
    return jnp.where(col == 0, a1, jnp.where(col == 1, a2, jnp.where(col == 2, a3, a4)))


def _pool_count(t):
    col = lax.broadcasted_iota(jnp.int32, (1, MAIN_W), 1) // POOL_GROUP
    win = jnp.where(col == 0, 2, jnp.where(col == 1, 4, jnp.where(col == 2, 8, 16)))
    return jnp.minimum(t + 1, win).astype(f32)


def pool_fwd(z, wbd, scale, *, name):
    M = z.shape[0]
    tm = 256
    nper = SEQ // tm
    hb = tm // POOL_HALO

    def body(zc_ref, zh_ref, w_ref, s_ref, p_ref, y_ref):
        i = pl.program_id(0)
        seq_blk = i % nper
        halo = jnp.where(seq_blk == 0, 0.0, zh_ref[...])
        u = zc_ref[...]
        ext = jnp.concatenate([halo, u], axis=0)
        a1 = ext + pltpu.roll(ext, 1, 0)
        a2 = a1 + pltpu.roll(a1, 2, 0)
        a3 = a2 + pltpu.roll(a2, 4, 0)
        a4 = a3 + pltpu.roll(a3, 8, 0)
        sums = _pool_select(a1, a2, a3, a4)[POOL_HALO:]
        t = seq_blk * tm + lax.broadcasted_iota(jnp.int32, (tm, 1), 0)
        p = (sums / _pool_count(t) - u).astype(bf16)
        p_ref[...] = p
        y_ref[...] = (jnp.dot(p, w_ref[...], preferred_element_type=f32) * s_ref[...]).astype(bf16)

    return pl.pallas_call(
        body, name=name, grid=(M // tm,),
        in_specs=[pl.BlockSpec((tm, MAIN_W), lambda i: (i, 0)),
                  pl.BlockSpec((POOL_HALO, MAIN_W), lambda i: (jnp.maximum(i * hb - 1, 0), 0)),
                  pl.BlockSpec((MAIN_W, MAIN_W), lambda i: (0, 0)),
                  pl.BlockSpec((1, MAIN_W), lambda i: (0, 0))],
        out_specs=[pl.BlockSpec((tm, MAIN_W), lambda i: (i, 0)),
                   pl.BlockSpec((tm, MAIN_W), lambda i: (i, 0))],
        out_shape=[_sds((M, MAIN_W), bf16), _sds((M, D_MODEL), bf16)],
        compiler_params=_params(("parallel",)),
    )(z, z, wbd, scale)


def pool_bwd(dyc, p, wbd, scale, *, name):
    M = p.shape[0]
    tm = 256
    nper = SEQ // tm
    hb = tm // POOL_HALO
    last_hb = M // POOL_HALO - 1

    def body(dy_ref, dyh_ref, p_ref, w_ref, s_ref, dz_ref, dw_ref, ds_ref):
        i = pl.program_id(0)
        seq_blk = i % nper
        dy = dy_ref[...].astype(f32)
        pv = p_ref[...]
        w = w_ref[...]
        sc = s_ref[...]

        @pl.when(i == 0)
        def _():
            dw_ref[...] = jnp.zeros_like(dw_ref)
            ds_ref[...] = jnp.zeros_like(ds_ref)

        v = jnp.dot(pv, w, preferred_element_type=f32)
        ds_ref[...] += jnp.sum(dy * v, axis=0, keepdims=True)
        dv = (dy * sc).astype(bf16)
        dw_ref[...] += lax.dot_general(pv, dv, TN, preferred_element_type=f32)
        dp = lax.dot_general(dv, w, NT, preferred_element_type=f32)
        dvh = jnp.where(seq_blk == nper - 1, 0.0, dyh_ref[...].astype(f32) * sc).astype(bf16)
        dph = lax.dot_general(dvh, w, NT, preferred_element_type=f32)
        ext = jnp.concatenate([dp, dph], axis=0)
        n = tm + POOL_HALO
        t = seq_blk * tm + lax.broadcasted_iota(jnp.int32, (n, 1), 0)
        e = ext / _pool_count(t)
        b1 = e + pltpu.roll(e, n - 1, 0)
        b2 = b1 + pltpu.roll(b1, n - 2, 0)
        b3 = b2 + pltpu.roll(b2, n - 4, 0)
        b4 = b3 + pltpu.roll(b3, n - 8, 0)
        dz_ref[...] = (_pool_select(b1, b2, b3, b4)[:tm] - dp).astype(dz_ref.dtype)

    return pl.pallas_call(
        body, name=name, grid=(M // tm,),
        in_specs=[pl.BlockSpec((tm, MAIN_W), lambda i: (i, 0)),
                  pl.BlockSpec((POOL_HALO, MAIN_W), lambda i: (jnp.minimum((i + 1) * hb, last_hb), 0)),
                  pl.BlockSpec((tm, MAIN_W), lambda i: (i, 0)),
                  pl.BlockSpec((MAIN_W, MAIN_W), lambda i: (0, 0)),
                  pl.BlockSpec((1, MAIN_W), lambda i: (0, 0))],
        out_specs=[pl.BlockSpec((tm, MAIN_W), lambda i: (i, 0)),
                   pl.BlockSpec((MAIN_W, MAIN_W), lambda i: (0, 0)),
                   pl.BlockSpec((1, MAIN_W), lambda i: (0, 0))],
        out_shape=[_sds((M, D_MODEL), bf16), _sds((MAIN_W, MAIN_W), f32), _sds((1, MAIN_W), f32)],
        compiler_params=_params(("arbitrary",)),
    )(dyc, dyc, p, wbd, scale)


def _mem_probs(q, kv, h):
    hs = slice(h * HEAD_DIM, (h + 1) * HEAD_DIM)
    qh = q[:, hs]
    kh = kv[:, hs]
    s = lax.dot_general(qh, kh, NT, preferred_element_type=f32) * SCALE
    m = jnp.max(s, axis=-1, keepdims=True)
    e = jnp.exp(s - m)
    return qh, kh, e / jnp.sum(e, axis=-1, keepdims=True)


def memattn_fwd(z, kvm, ycat, *, name, n_seq):
    M = z.shape[0]
    tq = 512
    nq = SEQ // tq

    def body(q_ref, kv_ref, _, o_ref):
        q = q_ref[...].astype(bf16)
        kv = kv_ref[...]
        outs = []
        for h in range(N_MEM_HEADS):
            _, _, p = _mem_probs(q, kv, h)
            vh = kv[:, MEM_W + h * HEAD_DIM: MEM_W + (h + 1) * HEAD_DIM]
            outs.append(jnp.dot(p.astype(bf16), vh, preferred_element_type=f32))
        o_ref[...] = jnp.concatenate(outs, axis=1).astype(bf16)

    return pl.pallas_call(
        body, name=name, grid=(n_seq, nq),
        in_specs=[pl.BlockSpec((tq, MEM_W), lambda b, i: (b * nq + i, 3)),
                  pl.BlockSpec((N_MEM, 2 * MEM_W), lambda b, i: (b, 0)),
                  pl.BlockSpec(memory_space=pl.ANY)],
        out_specs=pl.BlockSpec((tq, MEM_W), lambda b, i: (b * nq + i, 3)),
        out_shape=_sds((M, D_MODEL), bf16),
        input_output_aliases={2: 0},
        compiler_params=_params(("parallel", "parallel")),
    )(z, kvm, ycat)


def memattn_bwd(z, kvm, dyc, dz, *, name, n_seq):
    M = z.shape[0]
    tq = 512
    nq = SEQ // tq

    def body(q_ref, kv_ref, dy_ref, _, dq_ref, dkv_ref):
        q = q_ref[...].astype(bf16)
        kv = kv_ref[...]
        dy = dy_ref[...].astype(bf16)
        dqs, dks, dvs = [], [], []
        for h in range(N_MEM_HEADS):
            hs = slice(h * HEAD_DIM, (h + 1) * HEAD_DIM)
            qh, kh, p = _mem_probs(q, kv, h)
            vh = kv[:, MEM_W + h * HEAD_DIM: MEM_W + (h + 1) * HEAD_DIM]
            dyh = dy[:, hs]
            dvs.append(lax.dot_general(p.astype(bf16), dyh, TN, preferred_element_type=f32))
            dp = lax.dot_general(dyh, vh, NT, preferred_element_type=f32)
            ds = (p * (dp - jnp.sum(dp * p, axis=-1, keepdims=True)) * SCALE).astype(bf16)
            dqs.append(jnp.dot(ds, kh, preferred_element_type=f32))
            dks.append(lax.dot_general(ds, qh, TN, preferred_element_type=f32))
        dq_ref[...] = jnp.concatenate(dqs, axis=1).astype(bf16)

        @pl.when(pl.program_id(1) == 0)
        def _():
            dkv_ref[...] = jnp.zeros_like(dkv_ref)

        dkv_ref[...] += jnp.concatenate(dks + dvs, axis=1)

    return pl.pallas_call(
        body, name=name, grid=(n_seq, nq),
        in_specs=[pl.BlockSpec((tq, MEM_W), lambda b, i: (b * nq + i, 3)),
                  pl.BlockSpec((N_MEM, 2 * MEM_W), lambda b, i: (b, 0)),
                  pl.BlockSpec((tq, MEM_W), lambda b, i: (b * nq + i, 3)),
                  pl.BlockSpec(memory_space=pl.ANY)],
        out_specs=[pl.BlockSpec((tq, MEM_W), lambda b, i: (b * nq + i, 3)),
                   pl.BlockSpec((N_MEM, 2 * MEM_W), lambda b, i: (b, 0))],
        out_shape=[_sds((M, D_MODEL), bf16), _sds((n_seq * N_MEM, 2 * MEM_W), f32)],
        input_output_aliases={3: 0},
        compiler_params=_params(("parallel", "arbitrary")),
    )(z, kvm, dyc, dz)


def rope_tables(pos, *, name):
    M = pos.shape[0]
    tm = min(1024, M)
    half = HEAD_DIM // 2
    inv = ROPE_THETA ** (-np.arange(half, dtype=np.float64) / half)
    inv128 = jnp.asarray(np.tile(inv, 4)[None, :], f32)
    sign128 = jnp.asarray(np.tile(np.concatenate([-np.ones(half), np.ones(half)]), 2)[None, :], f32)

    def body(p_ref, f_ref, s_ref, cos_ref, sin_ref):
        ang = p_ref[...] * f_ref[...]
        cos_ref[...] = jnp.cos(ang)
        sin_ref[...] = jnp.sin(ang) * s_ref[...]

    return pl.pallas_call(
        body, name=name, grid=(M // tm,),
        in_specs=[pl.BlockSpec((tm, 1), lambda i: (i, 0)),
                  pl.BlockSpec((1, 128), lambda i: (0, 0)),
                  pl.BlockSpec((1, 128), lambda i: (0, 0))],
        out_specs=[pl.BlockSpec((tm, 128), lambda i: (i, 0)),
                   pl.BlockSpec((tm, 128), lambda i: (i, 0))],
        out_shape=[_sds((M, 128), f32), _sds((M, 128), f32)],
        compiler_params=_params(("parallel",)),
    )(pos, inv128, sign128)


def _swap_halves(x):
    w = x.shape[1]
    first = (lax.broadcasted_iota(jnp.int32, (1, w), 1) % HEAD_DIM) < (HEAD_DIM // 2)
    return jnp.where(first, pltpu.roll(x, w - HEAD_DIM // 2, 1), pltpu.roll(x, HEAD_DIM // 2, 1))


def rope_fwd(src, cos, sin, *, name):
    M = src.shape[0]
    tm = min(512, M)

    def body(x_ref, c_ref, s_ref, o_ref):
        x = x_ref[...].astype(f32)
        c = jnp.tile(c_ref[...], (1, MAIN_W // 128))
        s = jnp.tile(s_ref[...], (1, MAIN_W // 128))
        o_ref[...] = (x * c + _swap_halves(x) * s).astype(bf16)

    return pl.pallas_call(
        body, name=name, grid=(M // tm,),
        in_specs=[pl.BlockSpec((tm, MAIN_W), lambda i: (i, 0)),
                  pl.BlockSpec((tm, 128), lambda i: (i, 0)),
                  pl.BlockSpec((tm, 128), lambda i: (i, 0))],
        out_specs=pl.BlockSpec((tm, MAIN_W), lambda i: (i, 0)),
        out_shape=_sds((M, MAIN_W), bf16),
        compiler_params=_params(("parallel",)),
    )(src, cos, sin)


def group_sum(groups, cos, sin, *, name, rotate, width, col_block=0, into=None):
    M = groups[0][0].shape[0]
    tm = min(512, M)
    counts = [len(g) for g in groups]
    flat = [a for g in groups for a in g]
    extra = [] if into is None else [into]

    def body(*refs):
        part_refs = refs[:len(flat)]
        c_ref, s_ref = refs[len(flat):len(flat) + 2]
        o_ref = refs[-1]
        cols, k = [], 0
        for n in counts:
            acc = part_refs[k][...]
            for r in part_refs[k + 1:k + n]:
                acc = acc + r[...]
            cols.append(acc)
            k += n
        d = jnp.concatenate(cols, axis=1)
        if rotate:
            c = jnp.tile(c_ref[...], (1, MAIN_W // 128))
            s = jnp.tile(s_ref[...], (1, MAIN_W // 128))
            d = d * c - _swap_halves(d) * s
        o_ref[...] = d.astype(bf16)

    part = pl.BlockSpec((tm, GROUP_W), lambda i: (i, 0))
    tab = pl.BlockSpec((tm, 128), lambda i: (i, 0))
    return pl.pallas_call(
        body, name=name, grid=(M // tm,),
        in_specs=[part] * len(flat) + [tab, tab] + [pl.BlockSpec(memory_space=pl.ANY)] * len(extra),
        out_specs=pl.BlockSpec((tm, MAIN_W), lambda i: (i, col_block)),
        out_shape=_sds((M, width), bf16),
        input_output_aliases={len(flat) + 2: 0} if extra else {},
        compiler_params=_params(("parallel",)),
    )(*flat, cos, sin, *extra)


def _band_mask(n):
    i = lax.broadcasted_iota(jnp.int32, (BAND, 2 * BAND), 0)
    j = lax.broadcasted_iota(jnp.int32, (BAND, 2 * BAND), 1)
    return (j >= i) & (j <= i + BAND) & ((n > 0) | (j >= BAND))


def _heads(x):
    return [x[:, h * HEAD_DIM:(h + 1) * HEAD_DIM] for h in range(GROUP_W // HEAD_DIM)]


def _bcast_heads(cols):
    return jnp.concatenate([jnp.broadcast_to(c, (c.shape[0], HEAD_DIM)) for c in cols], axis=1)


def _dil_views(n_seq, dil, q, k, kv):
    L = SEQ // dil
    return (q.reshape(n_seq, L, dil * MAIN_W), k.reshape(n_seq, L, dil * MAIN_W),
            kv.reshape(n_seq, L, dil * 2 * MAIN_W))


def _grp_view(a, n_seq, dil):
    return a.reshape(n_seq, SEQ // dil, dil * GROUP_W)


def dil_fwd(q, k, kv, g, dil, *, name, n_seq):
    M = q.shape[0]
    L = SEQ // dil
    nb = L // BAND
    q3, k3, v3 = _dil_views(n_seq, dil, q, k, kv)

    def body(q_ref, kp_ref, ko_ref, vp_ref, vo_ref, o_ref, l_ref):
        mask = _band_mask(pl.program_id(2))
        kc = jnp.concatenate([kp_ref[...], ko_ref[...]], axis=0)
        vc = jnp.concatenate([vp_ref[...], vo_ref[...]], axis=0)
        outs, lses = [], []
        for qh, kh, vh in zip(_heads(q_ref[...]), _heads(kc), _heads(vc)):
            s = lax.dot_general(qh, kh, NT, preferred_element_type=f32) * SCALE
            s = jnp.where(mask, s, NEG)
            m = jnp.max(s, axis=-1, keepdims=True)
            lse = m + jnp.log(jnp.sum(jnp.exp(s - m), axis=-1, keepdims=True))
            p = jnp.exp(s - lse).astype(bf16)
            outs.append(jnp.dot(p, vh, preferred_element_type=f32))
            lses.append(lse)
        o_ref[...] = jnp.concatenate(outs, axis=1)
        l_ref[...] = _bcast_heads(lses)

    blk = (None, BAND, GROUP_W)
    prev = lambda n: jnp.maximum(n - 1, 0)
    out = pl.BlockSpec(blk, lambda b, r, n: (b, n, r))
    o, lse = pl.pallas_call(
        body, name=name, grid=(n_seq, dil, nb),
        in_specs=[pl.BlockSpec(blk, lambda b, r, n: (b, n, r * 3 + g)),
                  pl.BlockSpec(blk, lambda b, r, n: (b, prev(n), r * 3 + g)),
                  pl.BlockSpec(blk, lambda b, r, n: (b, n, r * 3 + g)),
                  pl.BlockSpec(blk, lambda b, r, n: (b, prev(n), r * 6 + 3 + g)),
                  pl.BlockSpec(blk, lambda b, r, n: (b, n, r * 6 + 3 + g))],
        out_specs=[out, out],
        out_shape=[_sds((n_seq, L, dil * GROUP_W), f32)] * 2,
        compiler_params=_params(("parallel", "parallel", "arbitrary")),
    )(q3, k3, k3, v3, v3)
    return o.reshape(M, GROUP_W), lse.reshape(M, GROUP_W)


def combine_fwd(os_, lses, *, name):
    M = os_[0].shape[0]
    tm = min(512, M)

    def body(o0, o1, o2, l0, l1, l2, y_ref):
        ls = [l0[...], l1[...], l2[...]]
        m = jnp.maximum(jnp.maximum(ls[0], ls[1]), ls[2])
        es = [jnp.exp(l - m) for l in ls]
        inv = 1.0 / (es[0] + es[1] + es[2])
        y_ref[...] = jnp.concatenate([o[...] * e * inv for o, e in zip((o0, o1, o2), es)], axis=1).astype(bf16)

    part = pl.BlockSpec((tm, GROUP_W), lambda i: (i, 0))
    return pl.pallas_call(
        body, name=name, grid=(M // tm,),
        in_specs=[part] * 6,
        out_specs=pl.BlockSpec((tm, MAIN_W), lambda i: (i, 0)),
        out_shape=_sds((M, D_MODEL), bf16),
        compiler_params=_params(("parallel",)),
    )(*os_, *lses)


def combine_bwd(dyc, os_, lses, *, name):
    M = os_[0].shape[0]
    tm = min(512, M)

    def body(dy_ref, o0, o1, o2, l0, l1, l2, d0, d1, d2, c0, c1, c2):
        r = lax.broadcasted_iota(jnp.int32, (GROUP_W, GROUP_W), 0) // HEAD_DIM
        c = lax.broadcasted_iota(jnp.int32, (GROUP_W, GROUP_W), 1) // HEAD_DIM
        ones = (r == c).astype(f32)
        dy = dy_ref[...].astype(f32)
        ls = [l0[...], l1[...], l2[...]]
        m = jnp.maximum(jnp.maximum(ls[0], ls[1]), ls[2])
        es = [jnp.exp(l - m) for l in ls]
        inv = 1.0 / (es[0] + es[1] + es[2])
        total = 0.0
        alphas = []
        for g, (o, e, d_ref) in enumerate(zip((o0, o1, o2), es, (d0, d1, d2))):
            a = e * inv
            dyg = dy[:, g * GROUP_W:(g + 1) * GROUP_W]
            d_ref[...] = (dyg * a).astype(bf16)
            dsum = jnp.dot(dyg * o[...], ones, precision=lax.Precision.HIGHEST, preferred_element_type=f32)
            total = total + a * dsum
            alphas.append(a)
        for a, c_ref in zip(alphas, (c0, c1, c2)):
            c_ref[...] = -a * total

    part = pl.BlockSpec((tm, GROUP_W), lambda i: (i, 0))
    outs = pl.pallas_call(
        body, name=name, grid=(M // tm,),
        in_specs=[pl.BlockSpec((tm, MAIN_W), lambda i: (i, 0))] + [part] * 6,
        out_specs=[part] * 6,
        out_shape=[_sds((M, GROUP_W), bf16)] * 3 + [_sds((M, GROUP_W), f32)] * 3,
        compiler_params=_params(("parallel",)),
    )(dyc, *os_, *lses)
    return outs[:3], outs[3:]


def dil_bwd_dq(q, k, kv, do, cc, lse, g, dil, *, name, n_seq):
    M = q.shape[0]
    L = SEQ // dil
    nb = L // BAND
    q3, k3, v3 = _dil_views(n_seq, dil, q, k, kv)
    do3, c3, l3 = (_grp_view(a, n_seq, dil) for a in (do, cc, lse))

    def body(q_ref, kp_ref, ko_ref, vp_ref, vo_ref, do_ref, c_ref, l_ref, dq_ref):
        mask = _band_mask(pl.program_id(2))
        kc = jnp.concatenate([kp_ref[...], ko_ref[...]], axis=0)
        vc = jnp.concatenate([vp_ref[...], vo_ref[...]], axis=0)
        cv = c_ref[...]
        lv = l_ref[...]
        outs = []
        for h, (qh, kh, vh, doh) in enumerate(zip(_heads(q_ref[...]), _heads(kc), _heads(vc), _heads(do_ref[...]))):
            s = lax.dot_general(qh, kh, NT, preferred_element_type=f32) * SCALE
            p = jnp.exp(jnp.where(mask, s, NEG) - lv[:, h * HEAD_DIM:h * HEAD_DIM + 1])
            dp = lax.dot_general(doh, vh, NT, preferred_element_type=f32)
            ds = (p * (dp + cv[:, h * HEAD_DIM:h * HEAD_DIM + 1]) * SCALE).astype(bf16)
            outs.append(jnp.dot(ds, kh, preferred_element_type=f32))
        dq_ref[...] = jnp.concatenate(outs, axis=1)

    blk = (None, BAND, GROUP_W)
    prev = lambda n: jnp.maximum(n - 1, 0)
    grp = pl.BlockSpec(blk, lambda b, r, n: (b, n, r))
    dq = pl.pallas_call(
        body, name=name, grid=(n_seq, dil, nb),
        in_specs=[pl.BlockSpec(blk, lambda b, r, n: (b, n, r * 3 + g)),
                  pl.BlockSpec(blk, lambda b, r, n: (b, prev(n), r * 3 + g)),
                  pl.BlockSpec(blk, lambda b, r, n: (b, n, r * 3 + g)),
                  pl.BlockSpec(blk, lambda b, r, n: (b, prev(n), r * 6 + 3 + g)),
                  pl.BlockSpec(blk, lambda b, r, n: (b, n, r * 6 + 3 + g)),
                  grp, grp, grp],
        out_specs=grp,
        out_shape=_sds((n_seq, L, dil * GROUP_W), f32),
        compiler_params=_params(("parallel", "parallel", "arbitrary")),
    )(q3, k3, k3, v3, v3, do3, c3, l3)
    return dq.reshape(M, GROUP_W)


def dil_bwd_dkv(q, k, kv, do, cc, lse, g, dil, *, name, n_seq):
    M = q.shape[0]
    L = SEQ // dil
    nb = L // BAND
    q3, k3, v3 = _dil_views(n_seq, dil, q, k, kv)
    do3, c3, l3 = (_grp_view(a, n_seq, dil) for a in (do, cc, lse))

    def body(k_ref, v_ref, qo_ref, qn_ref, doo_ref, don_ref, co_ref, cn_ref, lo_ref, ln_ref, dk_ref, dv_ref):
        n = pl.program_id(2)
        i = lax.broadcasted_iota(jnp.int32, (BAND, BAND), 0)
        j = lax.broadcasted_iota(jnp.int32, (BAND, BAND), 1)
        masks = (j <= i, (j >= i) & (n + 1 < nb))
        dks, dvs = [], []
        for h, (kh, vh) in enumerate(zip(_heads(k_ref[...]), _heads(v_ref[...]))):
            hs = slice(h * HEAD_DIM, (h + 1) * HEAD_DIM)
            h1 = slice(h * HEAD_DIM, h * HEAD_DIM + 1)
            dk = jnp.zeros((BAND, HEAD_DIM), f32)
            dv = jnp.zeros((BAND, HEAD_DIM), f32)
            for mask, q_ref, do_ref, c_ref, l_ref in ((masks[0], qo_ref, doo_ref, co_ref, lo_ref),
                                                      (masks[1], qn_ref, don_ref, cn_ref, ln_ref)):
                qh = q_ref[:, hs]
                doh = do_ref[:, hs]
                s = lax.dot_general(qh, kh, NT, preferred_element_type=f32) * SCALE
                p = jnp.exp(jnp.where(mask, s, NEG) - l_ref[:, h1])
                dv = dv + lax.dot_general(p.astype(bf16), doh, TN, preferred_element_type=f32)
                dp = lax.dot_general(doh, vh, NT, preferred_element_type=f32)
                ds = (p * (dp + c_ref[:, h1]) * SCALE).astype(bf16)
                dk = dk + lax.dot_general(ds, qh, TN, preferred_element_type=f32)
            dks.append(dk)
            dvs.append(dv)
        dk_ref[...] = jnp.concatenate(dks, axis=1)
        dv_ref[...] = jnp.concatenate(dvs, axis=1)

    blk = (None, BAND, GROUP_W)
    nxt = lambda n: jnp.minimum(n + 1, nb - 1)
    own = pl.BlockSpec(blk, lambda b, r, n: (b, n, r))
    nx = pl.BlockSpec(blk, lambda b, r, n: (b, nxt(n), r))
    dk, dv = pl.pallas_call(
        body, name=name, grid=(n_seq, dil, nb),
        in_specs=[pl.BlockSpec(blk, lambda b, r, n: (b, n, r * 3 + g)),
                  pl.BlockSpec(blk, lambda b, r, n: (b, n, r * 6 + 3 + g)),
                  pl.BlockSpec(blk, lambda b, r, n: (b, n, r * 3 + g)),
                  pl.BlockSpec(blk, lambda b, r, n: (b, nxt(n), r * 3 + g)),
                  own, nx, own, nx, own, nx],
        out_specs=[own, own],
        out_shape=[_sds((n_seq, L, dil * GROUP_W), f32)] * 2,
        compiler_params=_params(("parallel", "parallel", "arbitrary")),
    )(k3, v3, q3, q3, do3, do3, c3, c3, l3, l3)
    return dk.reshape(M, GROUP_W), dv.reshape(M, GROUP_W)


def _blockdiag(wp):
    out = jnp.zeros((MAIN_W, MAIN_W), wp.dtype)
    for gi in range(len(POOL_WINDOWS)):
        sl = slice(gi * POOL_GROUP, (gi + 1) * POOL_GROUP)
        out = out.at[sl, sl].set(wp[gi])
    return out


def _unblockdiag(w):
    return jnp.stack([w[gi * POOL_GROUP:(gi + 1) * POOL_GROUP, gi * POOL_GROUP:(gi + 1) * POOL_GROUP]
                      for gi in range(len(POOL_WINDOWS))])


def local_step(x, mem, positions, target, P, layer_weights, kv_weight, emit_grads):
    n_seq = x.shape[0]
    M = n_seq * SEQ
    xs = x.reshape(M, D_MODEL)
    mems = mem.reshape(n_seq * N_MEM, D_MODEL)
    pos = positions.reshape(M, 1).astype(f32)
    cos, sin = rope_tables(pos, name="rope_tables")
    gains = P["norm_gains"]

    def gain(l, k):
        return gains[l, k].reshape(1, D_MODEL)

    saved = []
    kvs = None
    for l in range(DEPTH):
        W, started = layer_weights(l, "mix", xs)
        sv = {"x": xs, "W": W}
        z, h1 = rms_matmul(xs, gain(l, 0), W["w_in"], name=f"l{l}_in", out_dtype=f32, after=started)
        kvm, mn = rms_matmul(mems, P["mem_norm"][l].reshape(1, D_MODEL), W["w_mem_kv"],
                             name=f"l{l}_memkv", out_dtype=bf16)
        sv.update(z=z, h1=h1, kvm=kvm, mn=mn)
        if l < N_A_LAYERS:
            wbd = _blockdiag(P["w_pool"][l].astype(bf16))
            psc = P["pool_scale"][l].reshape(1, MAIN_W)
            p, y_main = pool_fwd(z, wbd, psc, name=f"l{l}_pool")
            sv.update(p=p, wbd=wbd, psc=psc)
        else:
            qrot = rope_fwd(z, cos, sin, name=f"l{l}_ropeq")
            os_, lses = [], []
            for g, (_, dil) in enumerate(DIL_PATTERNS):
                o, lse = dil_fwd(qrot, kvs["krot"], kvs["kv"], g, dil, name=f"l{l}_dil{g}", n_seq=n_seq)
                os_.append(o)
                lses.append(lse)
            y_main = combine_fwd(os_, lses, name=f"l{l}_comb")
            sv.update(qrot=qrot, os=os_, lses=lses)
        ycat = memattn_fwd(z, kvm, y_main, name=f"l{l}_memattn", n_seq=n_seq)
        y, x1 = matmul_rms_res(ycat, W["w_out"], gain(l, 1), xs, name=f"l{l}_out")
        W.update(layer_weights(l, "ffn", x1)[0])
        fg, fu, a, h2 = rms_gate_up(x1, gain(l, 2), W["w_gate_up"], name=f"l{l}_gu")
        y2, x2 = matmul_rms_res(a, W["w_down"], gain(l, 3), x1, name=f"l{l}_down")
        sv.update(ycat=ycat, y=y, x1=x1, fg=fg, fu=fu, h2=h2, a=a, y2=y2)
        saved.append(sv)
        xs = x2
        if l == N_A_LAYERS - 1:
            w_kv = kv_weight(xs)
            kv, hkv = rms_matmul(xs, P["kv_norm"].reshape(1, D_MODEL), w_kv, name="kv_proj", out_dtype=bf16,
                                 transposed=True)
            krot = rope_fwd(kv, cos, sin, name="ropek")
            kvs = {"kv": kv, "hkv": hkv, "krot": krot, "x": xs, "w_kv": w_kv}

    dx, sq = loss_head(xs, target.reshape(M, D_MODEL), name="loss_head")

    G = {"mem_norm": [None] * DEPTH, "norm_gains": [[None] * 4 for _ in range(DEPTH)],
         "pool_scale": [None] * N_A_LAYERS}
    dk_parts = [[] for _ in range(N_GROUPS)]
    dv_parts = [[] for _ in range(N_GROUPS)]
    emitted = None

    for l in reversed(range(DEPTH)):
        sv = saved[l]
        W = sv["W"]
        gw = {}
        dy2, G["norm_gains"][l][3] = rms_bwd(sv["y2"], gain(l, 3), dx, None, name=f"l{l}_b_n3", out_dtype=bf16,
                                             after=emitted)
        dgu = down_bwd(dy2, W["w_down"], sv["fg"], sv["fu"], name=f"l{l}_b_dgu")
        gw["w_down"] = matmul(sv["a"], dy2, TN, name=f"l{l}_b_wd", out_dtype=bf16)
        dh2 = matmul(dgu, W["w_gate_up"], NN, name=f"l{l}_b_dh2", out_dtype=bf16)
        gw["w_gate_up"] = matmul(dgu, sv["h2"], TN, name=f"l{l}_b_wgu", out_dtype=bf16)
        dx1, G["norm_gains"][l][2] = rms_bwd(sv["x1"], gain(l, 2), dh2, dx, name=f"l{l}_b_n2", out_dtype=f32)
        emitted = emit_grads(l, "ffn", gw)
        gw = {}
        dy, G["norm_gains"][l][1] = rms_bwd(sv["y"], gain(l, 1), dx1, None, name=f"l{l}_b_n1", out_dtype=bf16,
                                            after=emitted)
        dycat = matmul(dy, W["w_out"], NT, name=f"l{l}_b_dycat", out_dtype=bf16)
        gw["w_out"] = matmul(sv["ycat"], dy, TN, name=f"l{l}_b_wout", out_dtype=bf16)
        if l < N_A_LAYERS:
            dz, dwbd, dps = pool_bwd(dycat, sv["p"], sv["wbd"], sv["psc"], name=f"l{l}_b_pool")
            gw["w_pool"] = _unblockdiag(dwbd).reshape(MAIN_W, POOL_GROUP).astype(bf16)
            G["pool_scale"][l] = dps.reshape(MAIN_W)
        else:
            dos, ccs = combine_bwd(dycat, sv["os"], sv["lses"], name=f"l{l}_b_comb")
            dqs = []
            for g, (_, dil) in enumerate(DIL_PATTERNS):
                args = (sv["qrot"], kvs["krot"], kvs["kv"], dos[g], ccs[g], sv["lses"][g], g, dil)
                dqs.append([dil_bwd_dq(*args, name=f"l{l}_b_dq{g}", n_seq=n_seq)])
                dk, dv = dil_bwd_dkv(*args, name=f"l{l}_b_dkv{g}", n_seq=n_seq)
                dk_parts[g].append(dk)
                dv_parts[g].append(dv)
            dz = group_sum(dqs, cos, sin, name=f"l{l}_b_ropeq", rotate=True, width=D_MODEL)
        dz, dkvm = memattn_bwd(sv["z"], sv["kvm"], dycat, dz, name=f"l{l}_b_memattn", n_seq=n_seq)
        dmn = matmul(dkvm, W["w_mem_kv"], NT, name=f"l{l}_b_dmn", out_dtype=bf16)
        gw["w_mem_kv"] = matmul(sv["mn"], dkvm, TN, name=f"l{l}_b_wmkv", out_dtype=bf16)
        _, G["mem_norm"][l] = rms_bwd(mems, P["mem_norm"][l].reshape(1, D_MODEL), dmn, None,
                                      name=f"l{l}_b_nmem", out_dtype=bf16)
        dh1 = matmul(dz, W["w_in"], NT, name=f"l{l}_b_dh1", out_dtype=bf16)
        gw["w_in"] = matmul(sv["h1"], dz, TN, name=f"l{l}_b_win", out_dtype=bf16)
        dx, G["norm_gains"][l][0] = rms_bwd(sv["x"], gain(l, 0), dh1, dx1, name=f"l{l}_b_n0", out_dtype=f32)
        if l == N_A_LAYERS:
            dkv = group_sum(dk_parts, cos, sin, name="b_ropek", rotate=True, width=2 * MAIN_W)
            dkv = group_sum(dv_parts, cos, sin, name="b_sumv", rotate=False, width=2 * MAIN_W, col_block=1, into=dkv)
            dhkv = matmul(dkv, kvs["w_kv"], NN, name="b_dhkv", out_dtype=bf16)
            gw["w_kv"] = matmul(dkv, kvs["hkv"], TN, name="b_wkv", out_dtype=bf16)
            dx, gkn = rms_bwd(kvs["x"], P["kv_norm"].reshape(1, D_MODEL), dhkv, dx, name="b_nkv", out_dtype=f32)
            G["kv_norm"] = gkn.reshape(D_MODEL)
        emitted = emit_grads(l, "mix", gw)

    small = {"pool_scale": jnp.stack(G["pool_scale"]),
             "mem_norm": jnp.concatenate(G["mem_norm"], axis=0),
             "norm_gains": jnp.stack([jnp.concatenate(r, axis=0) for r in G["norm_gains"]]),
             "kv_norm": G["kv_norm"]}
    return sq[0, 0], dx.reshape(n_seq, SEQ, D_MODEL), small, emitted


def _peer(k):
    x, y, c = lax.axis_index("x"), lax.axis_index("y"), lax.axis_index("c")
    px = 1 - x if k & 4 else x
    py = 1 - y if k & 2 else y
    pc = 1 - c if k & 1 else c
    return (px, py, pc), 4 * px + 2 * py + pc


def _my_index():
    return 4 * lax.axis_index("x") + 2 * lax.axis_index("y") + lax.axis_index("c")


def _src_for(kinds, in_refs, i, idx):
    return in_refs[i] if kinds[i] == "gather" else in_refs[i].at[idx]


def _local_copies(kinds, in_refs, out_refs, local_sems):
    me = _my_index()
    return [pltpu.make_async_copy(_src_for(kinds, in_refs, i, me), out_refs[i].at[me], local_sems.at[i])
            for i in range(len(kinds))]


def _remote_copies(kinds, in_refs, out_refs, send_sems, recv_sems, *, arriving):
    me = _my_index()
    copies = []
    for k in range(1, N_DEV):
        dev, idx = _peer(k)
        for i in range(len(kinds)):
            j = i * (N_DEV - 1) + k - 1
            copies.append(pltpu.make_async_remote_copy(
                src_ref=_src_for(kinds, in_refs, i, idx), dst_ref=out_refs[i].at[idx if arriving else me],
                send_sem=send_sems.at[j], recv_sem=recv_sems.at[j], device_id=dev, device_id_type=MESH))
    return copies


def _out_shape(a, kind):
    return ((N_DEV,) + a.shape) if kind == "gather" else a.shape


def exchange(items, *, name, after=()):
    n = len(items)
    kinds = [k for _, k in items]
    after = list(after)

    def body(*refs):
        in_refs, out_refs = refs[:n], refs[n + len(after):2 * n + len(after)]
        send_sems, recv_sems, local_sems = refs[-3:]
        local = _local_copies(kinds, in_refs, out_refs, local_sems)
        sends = _remote_copies(kinds, in_refs, out_refs, send_sems, recv_sems, arriving=False)
        for cp in local + sends:
            cp.start()
        for cp in _remote_copies(kinds, in_refs, out_refs, send_sems, recv_sems, arriving=True):
            cp.wait_recv()
        for cp in sends:
            cp.wait_send()
        for cp in local:
            cp.wait()

    any_spec = pl.BlockSpec(memory_space=pl.ANY)
    return pl.pallas_call(
        body, name=name,
        in_specs=[any_spec] * (n + len(after)), out_specs=[any_spec] * n,
        out_shape=[_sds(_out_shape(a, k), a.dtype) for a, k in items],
        scratch_shapes=[pltpu.SemaphoreType.DMA((n * (N_DEV - 1),)), pltpu.SemaphoreType.DMA((n * (N_DEV - 1),)),
                        pltpu.SemaphoreType.DMA((n,))],
    )(*[a for a, _ in items], *after)


_HBM = pl.BlockSpec(memory_space=pltpu.HBM)
_SEM = pl.BlockSpec(memory_space=pltpu.SEMAPHORE)
_EFFECT = pltpu.SideEffectType.DATAFLOW_SIDE_EFFECTING


def exchange_start(items, after, *, name):
    n = len(items)
    kinds = [k for _, k in items]

    def body(*refs):
        in_refs, land_refs = refs[:n], refs[n:2 * n]
        send_sems, recv_sems, local_sems = refs[2 * n + 1:2 * n + 4]
        token = refs[-1]
        for cp in (_local_copies(kinds, in_refs, land_refs, local_sems)
                   + _remote_copies(kinds, in_refs, land_refs, send_sems, recv_sems, arriving=False)):
            cp.start()
        token[...] = jnp.zeros_like(token)

    srcs = [pltpu.with_memory_space_constraint(a, pltpu.HBM) for a, _ in items]
    lands = [pltpu.with_memory_space_constraint(lax.empty(_out_shape(a, k), a.dtype), pltpu.HBM) for a, k in items]
    outs = pl.pallas_call(
        body, name=name,
        out_shape=(pltpu.SemaphoreType.DMA((n * (N_DEV - 1),)), pltpu.SemaphoreType.DMA((n * (N_DEV - 1),)),
                   pltpu.SemaphoreType.DMA((n,)),
                   *[pltpu.HBM(a.shape, a.dtype) for a in srcs], *[pltpu.HBM(a.shape, a.dtype) for a in lands],
                   _sds((8, 128), f32)),
        in_specs=[_HBM] * (2 * n) + [pl.BlockSpec(memory_space=pl.ANY)],
        out_specs=(_SEM, _SEM, _SEM, *[_HBM] * (2 * n), pl.BlockSpec(memory_space=pltpu.VMEM)),
        input_output_aliases={i: 3 + i for i in range(2 * n)},
        compiler_params=pltpu.CompilerParams(has_side_effects=_EFFECT),
    )(*srcs, *lands, after)
    return {"kinds": kinds, "sems": outs[:3], "srcs": outs[3:3 + n], "lands": outs[3 + n:3 + 2 * n], "token": outs[-1]}


def exchange_wait(handle, after, *, name):
    kinds = handle["kinds"]
    n = len(kinds)

    def body(*refs):
        in_refs, land_refs = refs[:n], refs[n:2 * n]
        send_sems, recv_sems, local_sems = refs[2 * n:2 * n + 3]
        for cp in _remote_copies(kinds, in_refs, land_refs, send_sems, recv_sems, arriving=True):
            cp.wait_recv()
        for cp in _remote_copies(kinds, in_refs, land_refs, send_sems, recv_sems, arriving=False):
            cp.wait_send()
        for cp in _local_copies(kinds, in_refs, land_refs, local_sems):
            cp.wait()

    srcs, lands = list(handle["srcs"]), list(handle["lands"])
    after = list(after) if isinstance(after, (list, tuple)) else [after]
    outs = pl.pallas_call(
        body, name=name,
        out_shape=tuple(pltpu.HBM(a.shape, a.dtype) for a in srcs + lands),
        in_specs=[_HBM] * (2 * n) + [_SEM] * 3 + [pl.BlockSpec(memory_space=pl.ANY)] * len(after),
        out_specs=tuple([_HBM] * (2 * n)),
        input_output_aliases={i: i for i in range(2 * n)},
        compiler_params=pltpu.CompilerParams(has_side_effects=_EFFECT),
    )(*srcs, *lands, *handle["sems"], *after)
    return list(outs[n:])


def adamw(slots, w, m, v, *, name, layer=None, into=None):
    R, C = w.shape[-2:]
    tr = _tile(R, (256, 128, 64, 32, 16, 8))
    c1 = 1.0 - ADAM_B1 ** ADAM_STEP
    c2 = 1.0 - ADAM_B2 ** ADAM_STEP
    extra = [] if into is None else list(into)

    def body(s_ref, w_ref, m_ref, v_ref, *refs):
        g_ref, d_ref, m2_ref, v2_ref = refs[len(extra):]
        g = s_ref[0].astype(f32)
        for d in range(1, N_DEV):
            g = g + s_ref[d].astype(f32)
        m2 = ADAM_B1 * m_ref[...] + (1.0 - ADAM_B1) * g
        v2 = ADAM_B2 * v_ref[...] + (1.0 - ADAM_B2) * (g * g)
        g_ref[...] = g
        m2_ref[...] = m2
        v2_ref[...] = v2
        d_ref[...] = -ADAM_LR * ((m2 / c1) / (jnp.sqrt(v2 / c2) + ADAM_EPS) + ADAM_WD * w_ref[...])

    if layer is None:
        blk = pl.BlockSpec((tr, C), lambda i: (i, 0))
    else:
        blk = pl.BlockSpec((None, tr, C), lambda i: (layer, i, 0))
    return pl.pallas_call(
        body, name=name, grid=(R // tr,),
        in_specs=[pl.BlockSpec((N_DEV, tr, C), lambda i: (0, i, 0)), blk, blk, blk]
        + [pl.BlockSpec(memory_space=pl.ANY)] * len(extra),
        out_specs=[blk] * 4,
        out_shape=[_sds(w.shape, f32)] * 4,
        input_output_aliases={4 + j: j for j in range(len(extra))},
        compiler_params=_params(("parallel",)),
    )(slots, w, m, v, *extra)


WEIGHTS = ("norm_gains", "mem_norm", "w_in", "w_mem_kv", "w_out", "w_pool", "pool_scale", "kv_norm", "w_kv",
           "w_gate_up", "w_down")
LAYER_MATS = ("w_in", "w_mem_kv", "w_out", "w_gate_up", "w_down")
POOL_SHARD = MAIN_W // N_DEV
KV_SHARD = 2 * MAIN_W // N_DEV
LOOKAHEAD = 2


def _pack_small(gains, pscale):
    lead = gains.shape[:-3]
    g = gains.reshape(lead + (16, 128))
    p = jnp.zeros(lead + (8, 128), f32).at[..., :2, :POOL_SHARD].set(pscale)
    return jnp.concatenate([g, p], axis=-2)


def _unpack_small(a):
    return a[:16].reshape(4, 4, 128), a[16:18, :POOL_SHARD]


def _pack_repl(mem_norm, kv_norm):
    return jnp.concatenate([mem_norm, kv_norm.reshape(1, D_MODEL), jnp.zeros((3, D_MODEL), f32)], axis=0)


def _unpack_repl(a):
    return a[:4], a[4]


def kernel(x, mem, positions, norm_gains, mem_norm, w_in, w_mem_kv, w_out, w_pool, pool_scale, kv_norm, w_kv, w_gate_up, w_down, loss_target, m_norm_gains, m_mem_norm, m_w_in, m_w_mem_kv, m_w_out, m_w_pool, m_pool_scale, m_kv_norm, m_w_kv, m_w_gate_up, m_w_down, v_norm_gains, v_mem_norm, v_w_in, v_w_mem_kv, v_w_out, v_w_pool, v_pool_scale, v_kv_norm, v_w_kv, v_w_gate_up, v_w_down):
    w = dict(norm_gains=norm_gains, mem_norm=mem_norm, w_in=w_in, w_mem_kv=w_mem_kv, w_out=w_out, w_pool=w_pool,
             pool_scale=pool_scale, kv_norm=kv_norm, w_kv=w_kv, w_gate_up=w_gate_up, w_down=w_down)
    m = dict(norm_gains=m_norm_gains, mem_norm=m_mem_norm, w_in=m_w_in, w_mem_kv=m_w_mem_kv, w_out=m_w_out,
             w_pool=m_w_pool, pool_scale=m_pool_scale, kv_norm=m_kv_norm, w_kv=m_w_kv, w_gate_up=m_w_gate_up,
             w_down=m_w_down)
    v = dict(norm_gains=v_norm_gains, mem_norm=v_mem_norm, w_in=v_w_in, w_mem_kv=v_w_mem_kv, w_out=v_w_out,
             w_pool=v_w_pool, pool_scale=v_pool_scale, kv_norm=v_kv_norm, w_kv=v_w_kv, w_gate_up=v_w_gate_up,
             w_down=v_w_down)

    def transposed_view(d):
        d = dict(d)
        d["w_gate_up"] = jnp.swapaxes(d["w_gate_up"], 1, 2)
        d["w_kv"] = jnp.swapaxes(d["w_kv"], 0, 1)
        return d

    wv, mv, vv = transposed_view(w), transposed_view(m), transposed_view(v)

    small = _pack_small(norm_gains, pool_scale)
    (gsmall,) = exchange([(small, "gather")], name="gather_small")
    P = {"norm_gains": jnp.moveaxis(gsmall[:, :16].reshape(N_DEV, 4, 4, 128), 0, 2).reshape(4, 4, D_MODEL),
         "pool_scale": jnp.moveaxis(gsmall[:, 16:18, :POOL_SHARD], 0, 1).reshape(2, MAIN_W),
         "mem_norm": mem_norm, "kv_norm": kv_norm, "w_pool": w_pool}

    PARTS = {"mix": ("w_in", "w_mem_kv", "w_out"), "ffn": ("w_gate_up", "w_down")}

    def part_items(l, part):
        items = [(wv[k][l].astype(bf16), "gather") for k in PARTS[part]]
        if part == "ffn" and l == N_A_LAYERS - 1:
            items.append((wv["w_kv"].astype(bf16), "gather"))
        return items

    handles = {}

    def start_layer(l, after):
        for part in ("mix", "ffn"):
            handles[l, part] = exchange_start(part_items(l, part), after, name=f"gather_start_{part}_l{l}")
            after = handles[l, part]["token"]
        return after

    token = gsmall
    for l in range(LOOKAHEAD):
        token = start_layer(l, token)
    landed = {}

    def layer_weights(l, part, after):
        first = l == 0 and part == "mix"
        got = exchange_wait(handles[l, part], token if first else after, name=f"gather_wait_{part}_l{l}")
        landed[l, part] = got
        started = None
        if part == "mix" and l + LOOKAHEAD < DEPTH:
            started = start_layer(l + LOOKAHEAD, got[0])
        W = {k: g.reshape(-1, g.shape[-1]) for k, g in zip(PARTS[part], got)}
        return W, started

    def kv_weight(after):
        g = landed[N_A_LAYERS - 1, "ffn"][len(PARTS["ffn"])]
        return g.reshape(2 * MAIN_W, D_MODEL)

    ghandles = {}

    def emit_grads(l, part, gw):
        items = [(gw[k].reshape((N_DEV, -1) + gw[k].shape[-1:]), "scatter") for k in PARTS[part]]
        if part == "mix" and l == N_A_LAYERS:
            items.append((gw["w_kv"].reshape(N_DEV, KV_SHARD, D_MODEL), "scatter"))
        if part == "mix" and l < N_A_LAYERS:
            items.append((gw["w_pool"], "gather"))
        ghandles[l, part] = exchange_start(items, items[0][0], name=f"scatter_start_{part}_l{l}")
        return ghandles[l, part]["token"]

    sq, grad_x, GS, emitted = local_step(x, mem, positions, loss_target, P, layer_weights, kv_weight, emit_grads)
    loss = lax.psum(0.5 * sq / D_MODEL, ("x", "y", "c"))

    def pool3(a):
        return a.reshape(N_A_LAYERS, MAIN_W, POOL_GROUP)

    out = {}
    after = [emitted]

    def finish_layer(l, after):
        for part in ("ffn", "mix"):
            got = exchange_wait(ghandles[l, part], after, name=f"scatter_wait_{part}_l{l}")
            after = []
            for k, slots in zip(PARTS[part], got):
                out[k] = adamw(slots, wv[k], mv[k], vv[k], name=f"adamw_{k}_l{l}", layer=l, into=out.get(k))
                after.append(out[k][0])
            if part == "mix" and l == N_A_LAYERS:
                out["w_kv"] = adamw(got[-1], wv["w_kv"], mv["w_kv"], vv["w_kv"], name="adamw_w_kv")
                after.append(out["w_kv"][0])
            if part == "mix" and l < N_A_LAYERS:
                out["w_pool"] = adamw(got[-1], pool3(w_pool), pool3(m_w_pool), pool3(v_w_pool), name=f"adamw_w_pool_l{l}",
                                      layer=l, into=out.get("w_pool"))
                after.append(out["w_pool"][0])
        return after

    for l in reversed(range(1, DEPTH)):
        after = finish_layer(l, after)

    gs = _pack_small(jnp.moveaxis(GS["norm_gains"].reshape(4, 4, N_DEV, 128), 2, 0),
                     jnp.moveaxis(GS["pool_scale"].reshape(2, N_DEV, POOL_SHARD), 1, 0))
    parts_small, parts_repl = exchange(
        [(gs, "scatter"), (_pack_repl(GS["mem_norm"], GS["kv_norm"]), "gather")],
        name="exchange_small_grads", after=after)
    finish_layer(0, [parts_small])
    out["w_gate_up"] = [jnp.swapaxes(r, 1, 2) for r in out["w_gate_up"]]
    out["w_kv"] = [jnp.swapaxes(r, 0, 1) for r in out["w_kv"]]
    out["w_pool"] = [r.reshape(w_pool.shape) for r in out["w_pool"]]

    res = adamw(parts_small, small, _pack_small(m_norm_gains, m_pool_scale), _pack_small(v_norm_gains, v_pool_scale),
                name="adamw_small")
    out["norm_gains"], out["pool_scale"] = zip(*[_unpack_small(r) for r in res])
    res = adamw(parts_repl, _pack_repl(mem_norm, kv_norm), _pack_repl(m_mem_norm, m_kv_norm),
                _pack_repl(v_mem_norm, v_kv_norm), name="adamw_repl")
    out["mem_norm"], out["kv_norm"] = zip(*[_unpack_repl(r) for r in res])

    return (loss, grad_x, *[out[k][0] for k in WEIGHTS], *[out[k][1] for k in WEIGHTS],
            *[out[k][2] for k in WEIGHTS], *[out[k][3] for k in WEIGHTS])
```

```python
import numpy as np
import jax
import jax.numpy as jnp
from jax import lax
from jax.experimental import pallas as pl
from jax.experimental.pallas import tpu as pltpu

f32 = jnp.float32
bf16 = jnp.bfloat16

D_MODEL = 1024
SEQ = 2048
DEPTH = 4
N_MEM = 256
HEAD_DIM = 64
N_MEM_HEADS = 4
MEM_W = 256
MAIN_W = 768
POOL_WINDOWS = (2, 4, 8, 16)
POOL_GROUP = 192
POOL_HALO = 16
DIL_PATTERNS = ((128, 1), (512, 4), (2048, 16))
N_GROUPS = 3
GROUP_W = 256
BAND = 128
N_A_LAYERS = 2
D_FF = 2816
ROPE_THETA = 10000.0
EPS = 1e-6
NEG = -1e30
SCALE = HEAD_DIM ** -0.5
N_DEV = 8

ADAM_LR = 0.001
ADAM_B1 = 0.9
ADAM_B2 = 0.999
ADAM_EPS = 1e-08
ADAM_WD = 0.01
ADAM_STEP = 10

VMEM_LIMIT_BYTES = 56 * 1024 * 1024
MESH = pl.DeviceIdType.MESH

NN = (((1,), (0,)), ((), ()))
NT = (((1,), (1,)), ((), ()))
TN = (((0,), (0,)), ((), ()))


def _params(sem=None):
    return pltpu.CompilerParams(dimension_semantics=sem, vmem_limit_bytes=VMEM_LIMIT_BYTES)


def _tile(n, cands):
    for c in cands:
        if n % c == 0:
            return c
    return n


def _sds(shape, dtype):
    return jax.ShapeDtypeStruct(tuple(shape), dtype)


def _rms_r(v):
    return lax.rsqrt(jnp.mean(v * v, axis=-1, keepdims=True) + EPS)


def rms_matmul(x, gain, w, *, name, out_dtype, transposed=False, after=None):
    M, K = x.shape
    N = w.shape[0] if transposed else w.shape[1]
    tm = min(1024, M)
    tn = _tile(N, (512, 256, 128))
    order = [] if after is None else [after]

    def body(x_ref, g_ref, w_ref, *refs):
        z_ref, h_ref = refs[len(order):]

        @pl.when(pl.program_id(1) == 0)
        def _():
            xv = x_ref[...]
            h_ref[...] = (xv * _rms_r(xv) * g_ref[...]).astype(bf16)

        z_ref[...] = lax.dot_general(h_ref[...], w_ref[...], NT if transposed else NN,
                                     preferred_element_type=f32).astype(z_ref.dtype)

    w_spec = pl.BlockSpec((tn, K), lambda i, j: (j, 0)) if transposed else pl.BlockSpec((K, tn), lambda i, j: (0, j))
    return pl.pallas_call(
        body, name=name, grid=(M // tm, N // tn),
        in_specs=[pl.BlockSpec((tm, K), lambda i, j: (i, 0)),
                  pl.BlockSpec((1, K), lambda i, j: (0, 0)),
                  w_spec] + [pl.BlockSpec(memory_space=pl.ANY)] * len(order),
        out_specs=[pl.BlockSpec((tm, tn), lambda i, j: (i, j)), pl.BlockSpec((tm, K), lambda i, j: (i, 0))],
        out_shape=[_sds((M, N), out_dtype), _sds((M, K), bf16)],
        compiler_params=_params(("parallel", "arbitrary")),
    )(x, gain, w, *order)


def matmul_rms_res(a, w, gain, res, *, name):
    M, K = a.shape
    N = w.shape[1]
    tm = min(512, M)

    def body(a_ref, w_ref, g_ref, r_ref, y_ref, x_ref):
        y = jnp.dot(a_ref[...], w_ref[...], preferred_element_type=f32)
        y_ref[...] = y.astype(bf16)
        x_ref[...] = r_ref[...] + y * _rms_r(y) * g_ref[...]

    row = pl.BlockSpec((tm, N), lambda i: (i, 0))
    return pl.pallas_call(
        body, name=name, grid=(M // tm,),
        in_specs=[pl.BlockSpec((tm, K), lambda i: (i, 0)),
                  pl.BlockSpec((K, N), lambda i: (0, 0)),
                  pl.BlockSpec((1, N), lambda i: (0, 0)),
                  row],
        out_specs=[row, row],
        out_shape=[_sds((M, N), bf16), _sds((M, N), f32)],
        compiler_params=_params(("parallel",)),
    )(a, w, gain, res)


def matmul(a, b, dims, *, name, out_dtype):
    if dims is TN:
        K, M = a.shape
        tm = _tile(M, (512, 256, 128))
        a_spec = pl.BlockSpec((K, tm), lambda i: (0, i))
    else:
        M, K = a.shape
        tm = _tile(M, (512, 256, 128))
        a_spec = pl.BlockSpec((tm, K), lambda i: (i, 0))
    N = b.shape[0] if dims is NT else b.shape[1]

    def body(a_ref, b_ref, o_ref):
        o_ref[...] = lax.dot_general(a_ref[...].astype(bf16), b_ref[...].astype(bf16), dims,
                                     preferred_element_type=f32).astype(o_ref.dtype)

    return pl.pallas_call(
        body, name=name, grid=(M // tm,),
        in_specs=[a_spec, pl.BlockSpec(b.shape, lambda i: (0, 0))],
        out_specs=pl.BlockSpec((tm, N), lambda i: (i, 0)),
        out_shape=_sds((M, N), out_dtype),
        compiler_params=_params(("parallel",)),
    )(a, b)


def rms_gate_up(x, gain, wt, *, name):
    M, K = x.shape
    tm = min(1024, M)
    tn = _tile(D_FF, (256, 128))
    nj = D_FF // tn

    def body(x_ref, gn_ref, wg_ref, wu_ref, g_ref, u_ref, a_ref, h_ref):
        @pl.when(pl.program_id(1) == 0)
        def _():
            xv = x_ref[...]
            h_ref[...] = (xv * _rms_r(xv) * gn_ref[...]).astype(bf16)

        h = h_ref[...]
        g = lax.dot_general(h, wg_ref[...], NT, preferred_element_type=f32)
        u = lax.dot_general(h, wu_ref[...], NT, preferred_element_type=f32)
        g_ref[...] = g.astype(bf16)
        u_ref[...] = u.astype(bf16)
        a_ref[...] = (g * (1.0 / (1.0 + jnp.exp(-g))) * u).astype(bf16)

    col = pl.BlockSpec((tm, tn), lambda i, j: (i, j))
    return pl.pallas_call(
        body, name=name, grid=(M // tm, nj),
        in_specs=[pl.BlockSpec((tm, K), lambda i, j: (i, 0)),
                  pl.BlockSpec((1, K), lambda i, j: (0, 0)),
                  pl.BlockSpec((tn, K), lambda i, j: (j, 0)),
                  pl.BlockSpec((tn, K), lambda i, j: (j + nj, 0))],
        out_specs=[col, col, col, pl.BlockSpec((tm, K), lambda i, j: (i, 0))],
        out_shape=[_sds((M, D_FF), bf16)] * 3 + [_sds((M, K), bf16)],
        compiler_params=_params(("parallel", "arbitrary")),
    )(x, gain, wt, wt)


def down_bwd(dy, w_down, g, u, *, name):
    M, K = dy.shape
    tm = min(512, M)

    def body(dy_ref, w_ref, g_ref, u_ref, o_ref):
        da = lax.dot_general(dy_ref[...], w_ref[...], NT, preferred_element_type=f32)
        g = g_ref[...].astype(f32)
        u = u_ref[...].astype(f32)
        s = 1.0 / (1.0 + jnp.exp(-g))
        o_ref[:, :D_FF] = (da * u * s * (1.0 + g * (1.0 - s))).astype(bf16)
        o_ref[:, D_FF:] = (da * g * s).astype(bf16)

    wide = pl.BlockSpec((tm, D_FF), lambda i: (i, 0))
    return pl.pallas_call(
        body, name=name, grid=(M // tm,),
        in_specs=[pl.BlockSpec((tm, K), lambda i: (i, 0)), pl.BlockSpec((D_FF, K), lambda i: (0, 0)), wide, wide],
        out_specs=pl.BlockSpec((tm, 2 * D_FF), lambda i: (i, 0)),
        out_shape=_sds((M, 2 * D_FF), bf16),
        compiler_params=_params(("parallel",)),
    )(dy, w_down, g, u)


def rms_bwd(y, gain, dn, res, *, name, out_dtype, after=None):
    M, N = y.shape
    tm = min(512, M)
    has_res = res is not None
    order = [] if after is None else [after]

    def body(*refs):
        y_ref, g_ref, dn_ref = refs[:3]
        r_ref = refs[3] if has_res else None
        dy_ref, dg_ref = refs[-2:]
        yv = y_ref[...].astype(f32)
        dn = dn_ref[...].astype(f32)
        r = _rms_r(yv)
        q = dn * g_ref[...]
        dy = r * q - yv * (r * r * r) * jnp.mean(q * yv, axis=-1, keepdims=True)
        if has_res:
            dy = dy + r_ref[...]
        dy_ref[...] = dy.astype(dy_ref.dtype)

        @pl.when(pl.program_id(0) == 0)
        def _():
            dg_ref[...] = jnp.zeros_like(dg_ref)

        dg_ref[...] += jnp.sum(dn * yv * r, axis=0, keepdims=True)

    row = pl.BlockSpec((tm, N), lambda i: (i, 0))
    vec = pl.BlockSpec((1, N), lambda i: (0, 0))
    args = [y, gain, dn] + ([res] if has_res else []) + order
    return pl.pallas_call(
        body, name=name, grid=(M // tm,),
        in_specs=[row, vec, row] + ([row] if has_res else []) + [pl.BlockSpec(memory_space=pl.ANY)] * len(order),
        out_specs=[row, vec],
        out_shape=[_sds((M, N), out_dtype), _sds((1, N), f32)],
        compiler_params=_params(("arbitrary",)),
    )(*args)


def loss_head(x, target, *, name):
    M, N = x.shape
    tm = min(512, M)

    def body(x_ref, t_ref, dx_ref, l_ref):
        e = x_ref[...] - t_ref[...]
        dx_ref[...] = e * (1.0 / N)

        @pl.when(pl.program_id(0) == 0)
        def _():
            l_ref[...] = jnp.zeros_like(l_ref)

        l_ref[...] += jnp.sum(jnp.sum(e * e, axis=0, keepdims=True), axis=1, keepdims=True)

    row = pl.BlockSpec((tm, N), lambda i: (i, 0))
    return pl.pallas_call(
        body, name=name, grid=(M // tm,),
        in_specs=[row, row],
        out_specs=[row, pl.BlockSpec((8, 128), lambda i: (0, 0))],
        out_shape=[_sds((M, N), f32), _sds((8, 128), f32)],
        compiler_params=_params(("arbitrary",)),
    )(x, target)


def _pool_select(a1, a2, a3, a4):
    col = lax.broadcasted_iota(jnp.int32, (1, MAIN_W), 1) // POOL_GROUP
    return jnp.where(col == 0, a1, jnp.where(col == 1, a2, jnp.where(col == 2, a3, a4)))


def _pool_count(t):
    col = lax.broadcasted_iota(jnp.int32, (1, MAIN_W), 1) // POOL_GROUP
    win = jnp.where(col == 0, 2, jnp.where(col == 1, 4, jnp.where(col == 2, 8, 16)))
    return jnp.minimum(t + 1, win).astype(f32)


def pool_fwd(z, wbd, scale, *, name):
    M = z.shape[0]
    tm = 256
    nper = SEQ // tm
    hb = tm // POOL_HALO

    def body(zc_ref, zh_ref, w_ref, s_ref, p_ref, y_ref):
        i = pl.program_id(0)
        seq_blk = i % nper
        halo = jnp.where(seq_blk == 0, 0.0, zh_ref[...])
        u = zc_ref[...]
        ext = jnp.concatenate([halo, u], axis=0)
        a1 = ext + pltpu.roll(ext, 1, 0)
        a2 = a1 + pltpu.roll(a1, 2, 0)
        a3 = a2 + pltpu.roll(a2, 4, 0)
        a4 = a3 + pltpu.roll(a3, 8, 0)
        sums = _pool_select(a1, a2, a3, a4)[POOL_HALO:]
        t = seq_blk * tm + lax.broadcasted_iota(jnp.int32, (tm, 1), 0)
        p = (sums / _pool_count(t) - u).astype(bf16)
        p_ref[...] = p
        y_ref[...] = (jnp.dot(p, w_ref[...], preferred_element_type=f32) * s_ref[...]).astype(bf16)

    return pl.pallas_call(
        body, name=name, grid=(M // tm,),
        in_specs=[pl.BlockSpec((tm, MAIN_W), lambda i: (i, 0)),
                  pl.BlockSpec((POOL_HALO, MAIN_W), lambda i: (jnp.maximum(i * hb - 1, 0), 0)),
                  pl.BlockSpec((MAIN_W, MAIN_W), lambda i: (0, 0)),
                  pl.BlockSpec((1, MAIN_W), lambda i: (0, 0))],
        out_specs=[pl.BlockSpec((tm, MAIN_W), lambda i: (i, 0)),
                   pl.BlockSpec((tm, MAIN_W), lambda i: (i, 0))],
        out_shape=[_sds((M, MAIN_W), bf16), _sds((M, D_MODEL), bf16)],
        compiler_params=_params(("parallel",)),
    )(z, z, wbd, scale)


def pool_bwd(dyc, p, wbd, scale, *, name):
    M = p.shape[0]
    tm = 256
    nper = SEQ // tm
    hb = tm // POOL_HALO
    last_hb = M // POOL_HALO - 1

    def body(dy_ref, dyh_ref, p_ref, w_ref, s_ref, dz_ref, dw_ref, ds_ref):
        i = pl.program_id(0)
        seq_blk = i % nper
        dy = dy_ref[...].astype(f32)
        pv = p_ref[...]
        w = w_ref[...]
        sc = s_ref[...]

        @pl.when(i == 0)
        def _():
            dw_ref[...] = jnp.zeros_like(dw_ref)
            ds_ref[...] = jnp.zeros_like(ds_ref)

        v = jnp.dot(pv, w, preferred_element_type=f32)
        ds_ref[...] += jnp.sum(dy * v, axis=0, keepdims=True)
        dv = (dy * sc).astype(bf16)
        dw_ref[...] += lax.dot_general(pv, dv, TN, preferred_element_type=f32)
        dp = lax.dot_general(dv, w, NT, preferred_element_type=f32)
        dvh = jnp.where(seq_blk == nper - 1, 0.0, dyh_ref[...].astype(f32) * sc).astype(bf16)
        dph = lax.dot_general(dvh, w, NT, preferred_element_type=f32)
        ext = jnp.concatenate([dp, dph], axis=0)
        n = tm + POOL_HALO
        t = seq_blk * tm + lax.broadcasted_iota(jnp.int32, (n, 1), 0)
        e = ext / _pool_count(t)
        b1 = e + pltpu.roll(e, n - 1, 0)
        b2 = b1 + pltpu.roll(b1, n - 2, 0)
        b3 = b2 + pltpu.roll(b2, n - 4, 0)
        b4 = b3 + pltpu.roll(b3, n - 8, 0)
        dz_ref[...] = (_pool_select(b1, b2, b3, b4)[:tm] - dp).astype(dz_ref.dtype)

    return pl.pallas_call(
        body, name=name, grid=(M // tm,),
        in_specs=[pl.BlockSpec((tm, MAIN_W), lambda i: (i, 0)),
                  pl.BlockSpec((POOL_HALO, MAIN_W), lambda i: (jnp.minimum((i + 1) * hb, last_hb), 0)),
                  pl.BlockSpec((tm, MAIN_W), lambda i: (i, 0)),
                  pl.BlockSpec((MAIN_W, MAIN_W), lambda i: (0, 0)),
                  pl.BlockSpec((1, MAIN_W), lambda i: (0, 0))],
        out_specs=[pl.BlockSpec((tm, MAIN_W), lambda i: (i, 0)),
                   pl.BlockSpec((MAIN_W, MAIN_W), lambda i: (0, 0)),
                   pl.BlockSpec((1, MAIN_W), lambda i: (0, 0))],
        out_shape=[_sds((M, D_MODEL), bf16), _sds((MAIN_W, MAIN_W), f32), _sds((1, MAIN_W), f32)],
        compiler_params=_params(("arbitrary",)),
    )(dyc, dyc, p, wbd, scale)


def _mem_probs(q, kv, h):
    hs = slice(h * HEAD_DIM, (h + 1) * HEAD_DIM)
    qh = q[:, hs]
    kh = kv[:, hs]
    s = lax.dot_general(qh, kh, NT, preferred_element_type=f32) * SCALE
    m = jnp.max(s, axis=-1, keepdims=True)
    e = jnp.exp(s - m)
    return qh, kh, e / jnp.sum(e, axis=-1, keepdims=True)


def memattn_fwd(z, kvm, ycat, *, name, n_seq):
    M = z.shape[0]
    tq = 512
    nq = SEQ // tq

    def body(q_ref, kv_ref, _, o_ref):
        q = q_ref[...].astype(bf16)
        kv = kv_ref[...]
        outs = []
        for h in range(N_MEM_HEADS):
            _, _, p = _mem_probs(q, kv, h)
            vh = kv[:, MEM_W + h * HEAD_DIM: MEM_W + (h + 1) * HEAD_DIM]
            outs.append(jnp.dot(p.astype(bf16), vh, preferred_element_type=f32))
        o_ref[...] = jnp.concatenate(outs, axis=1).astype(bf16)

    return pl.pallas_call(
        body, name=name, grid=(n_seq, nq),
        in_specs=[pl.BlockSpec((tq, MEM_W), lambda b, i: (b * nq + i, 3)),
                  pl.BlockSpec((N_MEM, 2 * MEM_W), lambda b, i: (b, 0)),
                  pl.BlockSpec(memory_space=pl.ANY)],
        out_specs=pl.BlockSpec((tq, MEM_W), lambda b, i: (b * nq + i, 3)),
        out_shape=_sds((M, D_MODEL), bf16),
        input_output_aliases={2: 0},
        compiler_params=_params(("parallel", "parallel")),
    )(z, kvm, ycat)


def memattn_bwd(z, kvm, dyc, dz, *, name, n_seq):
    M = z.shape[0]
    tq = 512
    nq = SEQ // tq

    def body(q_ref, kv_ref, dy_ref, _, dq_ref, dkv_ref):
        q = q_ref[...].astype(bf16)
        kv = kv_ref[...]
        dy = dy_ref[...].astype(bf16)
        dqs, dks, dvs = [], [], []
        for h in range(N_MEM_HEADS):
            hs = slice(h * HEAD_DIM, (h + 1) * HEAD_DIM)
            qh, kh, p = _mem_probs(q, kv, h)
            vh = kv[:, MEM_W + h * HEAD_DIM: MEM_W + (h + 1) * HEAD_DIM]
            dyh = dy[:, hs]
            dvs.append(lax.dot_general(p.astype(bf16), dyh, TN, preferred_element_type=f32))
            dp = lax.dot_general(dyh, vh, NT, preferred_element_type=f32)
            ds = (p * (dp - jnp.sum(dp * p, axis=-1, keepdims=True)) * SCALE).astype(bf16)
            dqs.append(jnp.dot(ds, kh, preferred_element_type=f32))
            dks.append(lax.dot_general(ds, qh, TN, preferred_element_type=f32))
        dq_ref[...] = jnp.concatenate(dqs, axis=1).astype(bf16)

        @pl.when(pl.program_id(1) == 0)
        def _():
            dkv_ref[...] = jnp.zeros_like(dkv_ref)

        dkv_ref[...] += jnp.concatenate(dks + dvs, axis=1)

    return pl.pallas_call(
        body, name=name, grid=(n_seq, nq),
        in_specs=[pl.BlockSpec((tq, MEM_W), lambda b, i: (b * nq + i, 3)),
                  pl.BlockSpec((N_MEM, 2 * MEM_W), lambda b, i: (b, 0)),
                  pl.BlockSpec((tq, MEM_W), lambda b, i: (b * nq + i, 3)),
                  pl.BlockSpec(memory_space=pl.ANY)],
        out_specs=[pl.BlockSpec((tq, MEM_W), lambda b, i: (b * nq + i, 3)),
                   pl.BlockSpec((N_MEM, 2 * MEM_W), lambda b, i: (b, 0))],
        out_shape=[_sds((M, D_MODEL), bf16), _sds((n_seq * N_MEM, 2 * MEM_W), f32)],
        input_output_aliases={3: 0},
        compiler_params=_params(("parallel", "arbitrary")),
    )(z, kvm, dyc, dz)


def rope_tables(pos, *, name):
    M = pos.shape[0]
    tm = min(1024, M)
    half = HEAD_DIM // 2
    inv = ROPE_THETA ** (-np.arange(half, dtype=np.float64) / half)
    inv128 = jnp.asarray(np.tile(inv, 4)[None, :], f32)
    sign128 = jnp.asarray(np.tile(np.concatenate([-np.ones(half), np.ones(half)]), 2)[None, :], f32)

    def body(p_ref, f_ref, s_ref, cos_ref, sin_ref):
        ang = p_ref[...] * f_ref[...]
        cos_ref[...] = jnp.cos(ang)
        sin_ref[...] = jnp.sin(ang) * s_ref[...]

    return pl.pallas_call(
        body, name=name, grid=(M // tm,),
        in_specs=[pl.BlockSpec((tm, 1), lambda i: (i, 0)),
                  pl.BlockSpec((1, 128), lambda i: (0, 0)),
                  pl.BlockSpec((1, 128), lambda i: (0, 0))],
        out_specs=[pl.BlockSpec((tm, 128), lambda i: (i, 0)),
                   pl.BlockSpec((tm, 128), lambda i: (i, 0))],
        out_shape=[_sds((M, 128), f32), _sds((M, 128), f32)],
        compiler_params=_params(("parallel",)),
    )(pos, inv128, sign128)


def _swap_halves(x):
    w = x.shape[1]
    first = (lax.broadcasted_iota(jnp.int32, (1, w), 1) % HEAD_DIM) < (HEAD_DIM // 2)
    return jnp.where(first, pltpu.roll(x, w - HEAD_DIM // 2, 1), pltpu.roll(x, HEAD_DIM // 2, 1))


def rope_fwd(src, cos, sin, *, name):
    M = src.shape[0]
    tm = min(512, M)

    def body(x_ref, c_ref, s_ref, o_ref):
        x = x_ref[...].astype(f32)
        c = jnp.tile(c_ref[...], (1, MAIN_W // 128))
        s = jnp.tile(s_ref[...], (1, MAIN_W // 128))
        o_ref[...] = x * c + _swap_halves(x) * s

    return pl.pallas_call(
        body, name=name, grid=(M // tm,),
        in_specs=[pl.BlockSpec((tm, MAIN_W), lambda i: (i, 0)),
                  pl.BlockSpec((tm, 128), lambda i: (i, 0)),
                  pl.BlockSpec((tm, 128), lambda i: (i, 0))],
        out_specs=pl.BlockSpec((tm, MAIN_W), lambda i: (i, 0)),
        out_shape=_sds((M, MAIN_W), f32),
        compiler_params=_params(("parallel",)),
    )(src, cos, sin)


def group_sum(groups, cos, sin, *, name, rotate, width, col_block=0, into=None):
    M = groups[0][0].shape[0]
    tm = min(512, M)
    counts = [len(g) for g in groups]
    flat = [a for g in groups for a in g]
    extra = [] if into is None else [into]

    def body(*refs):
        part_refs = refs[:len(flat)]
        c_ref, s_ref = refs[len(flat):len(flat) + 2]
        o_ref = refs[-1]
        cols, k = [], 0
        for n in counts:
            acc = part_refs[k][...]
            for r in part_refs[k + 1:k + n]:
                acc = acc + r[...]
            cols.append(acc)
            k += n
        d = jnp.concatenate(cols, axis=1)
        if rotate:
            c = jnp.tile(c_ref[...], (1, MAIN_W // 128))
            s = jnp.tile(s_ref[...], (1, MAIN_W // 128))
            d = d * c - _swap_halves(d) * s
        o_ref[...] = d.astype(bf16)

    part = pl.BlockSpec((tm, GROUP_W), lambda i: (i, 0))
    tab = pl.BlockSpec((tm, 128), lambda i: (i, 0))
    return pl.pallas_call(
        body, name=name, grid=(M // tm,),
        in_specs=[part] * len(flat) + [tab, tab] + [pl.BlockSpec(memory_space=pl.ANY)] * len(extra),
        out_specs=pl.BlockSpec((tm, MAIN_W), lambda i: (i, col_block)),
        out_shape=_sds((M, width), bf16),
        input_output_aliases={len(flat) + 2: 0} if extra else {},
        compiler_params=_params(("parallel",)),
    )(*flat, cos, sin, *extra)


PAIR_W = 2 * HEAD_DIM
MIN_BLOCKS = 4


def _dil_geometry(dil):
    nsub = max(dil, MIN_BLOCKS)
    tb = BAND * nsub
    return nsub, tb, SEQ // tb


def _rows(ref, sub, dil):
    if dil == 1:
        return ref[sub * BAND:(sub + 1) * BAND, :]
    return ref[pl.ds(sub, BAND, stride=dil), :]


def _store_rows(ref, sub, dil, val):
    if dil == 1:
        ref[sub * BAND:(sub + 1) * BAND, :] = val
    else:
        ref[pl.ds(sub, BAND, stride=dil), :] = val


def _keys(prev_ref, own_ref, sub, dil):
    if prev_ref is None:
        return _rows(own_ref, sub, dil)
    if dil == 1:
        if sub == 0:
            return jnp.concatenate([_rows(prev_ref, prev_ref.shape[0] // BAND - 1, 1), _rows(own_ref, 0, 1)], axis=0)
        return own_ref[(sub - 1) * BAND:(sub + 1) * BAND, :]
    return jnp.concatenate([_rows(prev_ref, sub, dil), _rows(own_ref, sub, dil)], axis=0)


def _band_mask(nkeys, has_prev):
    i = lax.broadcasted_iota(jnp.int32, (BAND, nkeys), 0)
    j = lax.broadcasted_iota(jnp.int32, (BAND, nkeys), 1)
    if nkeys == BAND:
        return j <= i
    return (j >= i) & (j <= i + BAND) & (has_prev | (j >= BAND))


def _first_head():
    return lax.broadcasted_iota(jnp.int32, (1, PAIR_W), 1) < HEAD_DIM


def _col(x, hh):
    return x[:, hh * HEAD_DIM:hh * HEAD_DIM + 1]


def _pair_spec(tb, nblk, col0, which):
    def idx(b, p, i):
        if which < 0:
            i = jnp.maximum(i - 1, 0)
        elif which > 0:
            i = jnp.minimum(i + 1, nblk - 1)
        return (b * nblk + i, col0 + p)
    return pl.BlockSpec((tb, PAIR_W), idx)


def dil_fwd(q, k, kv, g, dil, *, name, n_seq):
    M = q.shape[0]
    nsub, tb, nblk = _dil_geometry(dil)
    with_prev = nblk > 1

    def body(*refs):
        if with_prev:
            q_ref, ko_ref, vo_ref, kp_ref, vp_ref, o_ref, l_ref = refs
        else:
            (q_ref, ko_ref, vo_ref, o_ref, l_ref), kp_ref, vp_ref = refs, None, None
        first = _first_head()
        blk = pl.program_id(2)
        for sub in range(nsub):
            qs = _rows(q_ref, sub, dil) * SCALE
            kc = _keys(kp_ref, ko_ref, sub, dil).astype(bf16)
            vc = _keys(vp_ref, vo_ref, sub, dil).astype(bf16)
            has_prev = True if (dil == 1 and sub > 0) else blk > 0
            mask = _band_mask(kc.shape[0], has_prev)
            outs, lses = [], []
            for hh in range(2):
                qm = jnp.where(first if hh == 0 else ~first, qs, 0.0).astype(bf16)
                s = jnp.where(mask, lax.dot_general(qm, kc, NT, preferred_element_type=f32), NEG)
                m = jnp.max(s, axis=-1, keepdims=True)
                e = jnp.exp(s - m)
                l = jnp.sum(e, axis=-1, keepdims=True)
                outs.append(jnp.dot(e.astype(bf16), vc, preferred_element_type=f32) * (1.0 / l))
                lses.append(jnp.broadcast_to(m + jnp.log(l), (BAND, PAIR_W)))
            _store_rows(o_ref, sub, dil, jnp.where(first, outs[0], outs[1]))
            _store_rows(l_ref, sub, dil, jnp.where(first, lses[0], lses[1]))

    ins = [(q, 2 * g, 0), (k, 2 * g, 0), (kv, 6 + 2 * g, 0)]
    if with_prev:
        ins += [(k, 2 * g, -1), (kv, 6 + 2 * g, -1)]
    out = _pair_spec(tb, nblk, 0, 0)
    return pl.pallas_call(
        body, name=name, grid=(n_seq, 2, nblk),
        in_specs=[_pair_spec(tb, nblk, c, w) for _, c, w in ins],
        out_specs=[out, out],
        out_shape=[_sds((M, GROUP_W), f32)] * 2,
        compiler_params=_params(("parallel", "parallel", "arbitrary")),
    )(*[a for a, _, _ in ins])


def combine_fwd(os_, lses, *, name):
    M = os_[0].shape[0]
    tm = min(512, M)

    def body(o0, o1, o2, l0, l1, l2, y_ref):
        ls = [l0[...], l1[...], l2[...]]
        m = jnp.maximum(jnp.maximum(ls[0], ls[1]), ls[2])
        es = [jnp.exp(l - m) for l in ls]
        inv = 1.0 / (es[0] + es[1] + es[2])
        y_ref[...] = jnp.concatenate([o[...] * e * inv for o, e in zip((o0, o1, o2), es)], axis=1).astype(bf16)

    part = pl.BlockSpec((tm, GROUP_W), lambda i: (i, 0))
    return pl.pallas_call(
        body, name=name, grid=(M // tm,),
        in_specs=[part] * 6,
        out_specs=pl.BlockSpec((tm, MAIN_W), lambda i: (i, 0)),
        out_shape=_sds((M, D_MODEL), bf16),
        compiler_params=_params(("parallel",)),
    )(*os_, *lses)


def combine_bwd(dyc, os_, lses, *, name):
    M = os_[0].shape[0]
    tm = min(512, M)

    def body(dy_ref, o0, o1, o2, l0, l1, l2, d0, d1, d2, c0, c1, c2):
        r = lax.broadcasted_iota(jnp.int32, (GROUP_W, GROUP_W), 0) // HEAD_DIM
        c = lax.broadcasted_iota(jnp.int32, (GROUP_W, GROUP_W), 1) // HEAD_DIM
        ones = (r == c).astype(f32)
        dy = dy_ref[...].astype(f32)
        ls = [l0[...], l1[...], l2[...]]
        m = jnp.maximum(jnp.maximum(ls[0], ls[1]), ls[2])
        es = [jnp.exp(l - m) for l in ls]
        inv = 1.0 / (es[0] + es[1] + es[2])
        total = 0.0
        alphas = []
        for g, (o, e, d_ref) in enumerate(zip((o0, o1, o2), es, (d0, d1, d2))):
            a = e * inv
            dyg = dy[:, g * GROUP_W:(g + 1) * GROUP_W]
            d_ref[...] = dyg * a
            dsum = jnp.dot(dyg * o[...], ones, precision=lax.Precision.HIGHEST, preferred_element_type=f32)
            total = total + a * dsum
            alphas.append(a)
        for a, c_ref in zip(alphas, (c0, c1, c2)):
            c_ref[...] = -a * total

    part = pl.BlockSpec((tm, GROUP_W), lambda i: (i, 0))
    outs = pl.pallas_call(
        body, name=name, grid=(M // tm,),
        in_specs=[pl.BlockSpec((tm, MAIN_W), lambda i: (i, 0))] + [part] * 6,
        out_specs=[part] * 6,
        out_shape=[_sds((M, GROUP_W), f32)] * 6,
        compiler_params=_params(("parallel",)),
    )(dyc, *os_, *lses)
    return outs[:3], outs[3:]


def dil_bwd_dq(q, k, kv, do, cc, lse, g, dil, *, name, n_seq):
    M = q.shape[0]
    nsub, tb, nblk = _dil_geometry(dil)
    with_prev = nblk > 1

    def body(*refs):
        if with_prev:
            q_ref, ko_ref, vo_ref, do_ref, c_ref, l_ref, kp_ref, vp_ref, dq_ref = refs
        else:
            (q_ref, ko_ref, vo_ref, do_ref, c_ref, l_ref, dq_ref), kp_ref, vp_ref = refs, None, None
        first = _first_head()
        blk = pl.program_id(2)
        for sub in range(nsub):
            qs = _rows(q_ref, sub, dil) * SCALE
            dos = _rows(do_ref, sub, dil)
            cs = _rows(c_ref, sub, dil)
            ls = _rows(l_ref, sub, dil)
            kc = _keys(kp_ref, ko_ref, sub, dil).astype(bf16)
            vc = _keys(vp_ref, vo_ref, sub, dil).astype(bf16)
            has_prev = True if (dil == 1 and sub > 0) else blk > 0
            mask = _band_mask(kc.shape[0], has_prev)
            outs = []
            for hh in range(2):
                lm = first if hh == 0 else ~first
                qm = jnp.where(lm, qs, 0.0).astype(bf16)
                dom = jnp.where(lm, dos, 0.0).astype(bf16)
                s = jnp.where(mask, lax.dot_general(qm, kc, NT, preferred_element_type=f32), NEG)
                p = jnp.exp(s - _col(ls, hh))
                dp = lax.dot_general(dom, vc, NT, preferred_element_type=f32)
                ds = (p * (dp + _col(cs, hh))).astype(bf16)
                outs.append(jnp.dot(ds, kc, preferred_element_type=f32) * SCALE)
            _store_rows(dq_ref, sub, dil, jnp.where(first, outs[0], outs[1]))

    ins = [(q, 2 * g, 0), (k, 2 * g, 0), (kv, 6 + 2 * g, 0), (do, 0, 0), (cc, 0, 0), (lse, 0, 0)]
    if with_prev:
        ins += [(k, 2 * g, -1), (kv, 6 + 2 * g, -1)]
    return pl.pallas_call(
        body, name=name, grid=(n_seq, 2, nblk),
        in_specs=[_pair_spec(tb, nblk, c, w) for _, c, w in ins],
        out_specs=_pair_spec(tb, nblk, 0, 0),
        out_shape=_sds((M, GROUP_W), f32),
        compiler_params=_params(("parallel", "parallel", "arbitrary")),
    )(*[a for a, _, _ in ins])


def dil_bwd_dkv(q, k, kv, do, cc, lse, g, dil, *, name, n_seq):
    M = q.shape[0]
    nsub, tb, nblk = _dil_geometry(dil)
    with_next = nblk > 1

    def body(*refs):
        if with_next:
            (k_ref, v_ref, q_ref, do_ref, c_ref, l_ref, qn_ref, don_ref, cn_ref, ln_ref, dk_ref, dv_ref) = refs
        else:
            (k_ref, v_ref, q_ref, do_ref, c_ref, l_ref, dk_ref, dv_ref) = refs
        first = _first_head()
        blk = pl.program_id(2)
        i = lax.broadcasted_iota(jnp.int32, (BAND, BAND), 0)
        j = lax.broadcasted_iota(jnp.int32, (BAND, BAND), 1)
        for sub in range(nsub):
            ks = _rows(k_ref, sub, dil)
            vs = _rows(v_ref, sub, dil)
            sets = [((q_ref, do_ref, c_ref, l_ref), sub, j <= i)]
            if dil == 1 and sub + 1 < nsub:
                sets.append(((q_ref, do_ref, c_ref, l_ref), sub + 1, j >= i))
            elif with_next:
                nsubq = 0 if dil == 1 else sub
                sets.append(((qn_ref, don_ref, cn_ref, ln_ref), nsubq, (j >= i) & (blk + 1 < nblk)))
            loaded = [(tuple(_rows(r, sq, dil) for r in rs), mask) for rs, sq, mask in sets]
            dks, dvs = [], []
            for hh in range(2):
                lm = first if hh == 0 else ~first
                km = jnp.where(lm, ks, 0.0).astype(bf16)
                vm = jnp.where(lm, vs, 0.0).astype(bf16)
                dk = jnp.zeros((BAND, PAIR_W), f32)
                dv = jnp.zeros((BAND, PAIR_W), f32)
                for (qs, dos, cs, ls), mask in loaded:
                    qb = qs.astype(bf16)
                    dob = dos.astype(bf16)
                    s = lax.dot_general(qb, km, NT, preferred_element_type=f32) * SCALE
                    p = jnp.exp(jnp.where(mask, s, NEG) - _col(ls, hh))
                    dv = dv + lax.dot_general(p.astype(bf16), dob, TN, preferred_element_type=f32)
                    dp = lax.dot_general(dob, vm, NT, preferred_element_type=f32)
                    ds = (p * (dp + _col(cs, hh)) * SCALE).astype(bf16)
                    dk = dk + lax.dot_general(ds, qb, TN, preferred_element_type=f32)
                dks.append(dk)
                dvs.append(dv)
            _store_rows(dk_ref, sub, dil, jnp.where(first, dks[0], dks[1]))
            _store_rows(dv_ref, sub, dil, jnp.where(first, dvs[0], dvs[1]))

    ins = [(k, 2 * g, 0), (kv, 6 + 2 * g, 0), (q, 2 * g, 0), (do, 0, 0), (cc, 0, 0), (lse, 0, 0)]
    if with_next:
        ins += [(q, 2 * g, 1), (do, 0, 1), (cc, 0, 1), (lse, 0, 1)]
    out = _pair_spec(tb, nblk, 0, 0)
    return pl.pallas_call(
        body, name=name, grid=(n_seq, 2, nblk),
        in_specs=[_pair_spec(tb, nblk, c, w) for _, c, w in ins],
        out_specs=[out, out],
        out_shape=[_sds((M, GROUP_W), f32)] * 2,
        compiler_params=_params(("parallel", "parallel", "arbitrary")),
    )(*[a for a, _, _ in ins])


def _blockdiag(wp):
    out = jnp.zeros((MAIN_W, MAIN_W), wp.dtype)
    for gi in range(len(POOL_WINDOWS)):
        sl = slice(gi * POOL_GROUP, (gi + 1) * POOL_GROUP)
        out = out.at[sl, sl].set(wp[gi])
    return out


def _unblockdiag(w):
    return jnp.stack([w[gi * POOL_GROUP:(gi + 1) * POOL_GROUP, gi * POOL_GROUP:(gi + 1) * POOL_GROUP]
                      for gi in range(len(POOL_WINDOWS))])


def local_step(x, mem, positions, target, P, layer_weights, kv_weight, emit_grads):
    n_seq = x.shape[0]
    M = n_seq * SEQ
    xs = x.reshape(M, D_MODEL)
    mems = mem.reshape(n_seq * N_MEM, D_MODEL)
    pos = positions.reshape(M, 1).astype(f32)
    cos, sin = rope_tables(pos, name="rope_tables")
    gains = P["norm_gains"]

    def gain(l, k):
        return gains[l, k].reshape(1, D_MODEL)

    saved = []
    kvs = None
    for l in range(DEPTH):
        W, started = layer_weights(l, "mix", xs)
        sv = {"x": xs, "W": W}
        z, h1 = rms_matmul(xs, gain(l, 0), W["w_in"], name=f"l{l}_in", out_dtype=f32, after=started)
        kvm, mn = rms_matmul(mems, P["mem_norm"][l].reshape(1, D_MODEL), W["w_mem_kv"],
                             name=f"l{l}_memkv", out_dtype=bf16)
        sv.update(z=z, h1=h1, kvm=kvm, mn=mn)
        if l < N_A_LAYERS:
            wbd = _blockdiag(P["w_pool"][l].astype(bf16))
            psc = P["pool_scale"][l].reshape(1, MAIN_W)
            p, y_main = pool_fwd(z, wbd, psc, name=f"l{l}_pool")
            sv.update(p=p, wbd=wbd, psc=psc)
        else:
            qrot = rope_fwd(z, cos, sin, name=f"l{l}_ropeq")
            os_, lses = [], []
            for g, (_, dil) in enumerate(DIL_PATTERNS):
                o, lse = dil_fwd(qrot, kvs["krot"], kvs["kv"], g, dil, name=f"l{l}_dil{g}", n_seq=n_seq)
                os_.append(o)
                lses.append(lse)
            y_main = combine_fwd(os_, lses, name=f"l{l}_comb")
            sv.update(qrot=qrot, os=os_, lses=lses)
        ycat = memattn_fwd(z, kvm, y_main, name=f"l{l}_memattn", n_seq=n_seq)
        y, x1 = matmul_rms_res(ycat, W["w_out"], gain(l, 1), xs, name=f"l{l}_out")
        W.update(layer_weights(l, "ffn", x1)[0])
        fg, fu, a, h2 = rms_gate_up(x1, gain(l, 2), W["w_gate_up"], name=f"l{l}_gu")
        y2, x2 = matmul_rms_res(a, W["w_down"], gain(l, 3), x1, name=f"l{l}_down")
        sv.update(ycat=ycat, y=y, x1=x1, fg=fg, fu=fu, h2=h2, a=a, y2=y2)
        saved.append(sv)
        xs = x2
        if l == N_A_LAYERS - 1:
            w_kv = kv_weight(xs)
            kv, hkv = rms_matmul(xs, P["kv_norm"].reshape(1, D_MODEL), w_kv, name="kv_proj", out_dtype=f32,
                                 transposed=True)
            krot = rope_fwd(kv, cos, sin, name="ropek")
            kvs = {"kv": kv, "hkv": hkv, "krot": krot, "x": xs, "w_kv": w_kv}

    dx, sq = loss_head(xs, target.reshape(M, D_MODEL), name="loss_head")

    G = {"mem_norm": [None] * DEPTH, "norm_gains": [[None] * 4 for _ in range(DEPTH)],
         "pool_scale": [None] * N_A_LAYERS}
    dk_parts = [[] for _ in range(N_GROUPS)]
    dv_parts = [[] for _ in range(N_GROUPS)]
    emitted = None

    for l in reversed(range(DEPTH)):
        sv = saved[l]
        W = sv["W"]
        gw = {}
        dy2, G["norm_gains"][l][3] = rms_bwd(sv["y2"], gain(l, 3), dx, None, name=f"l{l}_b_n3", out_dtype=bf16,
                                             after=emitted)
        dgu = down_bwd(dy2, W["w_down"], sv["fg"], sv["fu"], name=f"l{l}_b_dgu")
        gw["w_down"] = matmul(sv["a"], dy2, TN, name=f"l{l}_b_wd", out_dtype=bf16)
        dh2 = matmul(dgu, W["w_gate_up"], NN, name=f"l{l}_b_dh2", out_dtype=bf16)
        gw["w_gate_up"] = matmul(dgu, sv["h2"], TN, name=f"l{l}_b_wgu", out_dtype=bf16)
        dx1, G["norm_gains"][l][2] = rms_bwd(sv["x1"], gain(l, 2), dh2, dx, name=f"l{l}_b_n2", out_dtype=f32)
        emitted = emit_grads(l, "ffn", gw)
        gw = {}
        dy, G["norm_gains"][l][1] = rms_bwd(sv["y"], gain(l, 1), dx1, None, name=f"l{l}_b_n1", out_dtype=bf16,
                                            after=emitted)
        dycat = matmul(dy, W["w_out"], NT, name=f"l{l}_b_dycat", out_dtype=bf16)
        gw["w_out"] = matmul(sv["ycat"], dy, TN, name=f"l{l}_b_wout", out_dtype=bf16)
        if l < N_A_LAYERS:
            dz, dwbd, dps = pool_bwd(dycat, sv["p"], sv["wbd"], sv["psc"], name=f"l{l}_b_pool")
            gw["w_pool"] = _unblockdiag(dwbd).reshape(MAIN_W, POOL_GROUP).astype(bf16)
            G["pool_scale"][l] = dps.reshape(MAIN_W)
        else:
            dos, ccs = combine_bwd(dycat, sv["os"], sv["lses"], name=f"l{l}_b_comb")
            dqs = []
            for g, (_, dil) in enumerate(DIL_PATTERNS):
                args = (sv["qrot"], kvs["krot"], kvs["kv"], dos[g], ccs[g], sv["lses"][g], g, dil)
                dqs.append([dil_bwd_dq(*args, name=f"l{l}_b_dq{g}", n_seq=n_seq)])
                dk, dv = dil_bwd_dkv(*args, name=f"l{l}_b_dkv{g}", n_seq=n_seq)
                dk_parts[g].append(dk)
                dv_parts[g].append(dv)
            dz = group_sum(dqs, cos, sin, name=f"l{l}_b_ropeq", rotate=True, width=D_MODEL)
        dz, dkvm = memattn_bwd(sv["z"], sv["kvm"], dycat, dz, name=f"l{l}_b_memattn", n_seq=n_seq)
        dmn = matmul(dkvm, W["w_mem_kv"], NT, name=f"l{l}_b_dmn", out_dtype=bf16)
        gw["w_mem_kv"] = matmul(sv["mn"], dkvm, TN, name=f"l{l}_b_wmkv", out_dtype=bf16)
        _, G["mem_norm"][l] = rms_bwd(mems, P["mem_norm"][l].reshape(1, D_MODEL), dmn, None,
                                      name=f"l{l}_b_nmem", out_dtype=bf16)
        dh1 = matmul(dz, W["w_in"], NT, name=f"l{l}_b_dh1", out_dtype=bf16)
        gw["w_in"] = matmul(sv["h1"], dz, TN, name=f"l{l}_b_win", out_dtype=bf16)
        dx, G["norm_gains"][l][0] = rms_bwd(sv["x"], gain(l, 0), dh1, dx1, name=f"l{l}_b_n0", out_dtype=f32)
        if l == N_A_LAYERS:
            dkv = group_sum(dk_parts, cos, sin, name="b_ropek", rotate=True, width=2 * MAIN_W)
            dkv = group_sum(dv_parts, cos, sin, name="b_sumv", rotate=False, width=2 * MAIN_W, col_block=1, into=dkv)
            dhkv = matmul(dkv, kvs["w_kv"], NN, name="b_dhkv", out_dtype=bf16)
            gw["w_kv"] = matmul(dkv, kvs["hkv"], TN, name="b_wkv", out_dtype=bf16)
            dx, gkn = rms_bwd(kvs["x"], P["kv_norm"].reshape(1, D_MODEL), dhkv, dx, name="b_nkv", out_dtype=f32)
            G["kv_norm"] = gkn.reshape(D_MODEL)
        emitted = emit_grads(l, "mix", gw)

    small = {"pool_scale": jnp.stack(G["pool_scale"]),
             "mem_norm": jnp.concatenate(G["mem_norm"], axis=0),
             "norm_gains": jnp.stack([jnp.concatenate(r, axis=0) for r in G["norm_gains"]]),
             "kv_norm": G["kv_norm"]}
    return sq[0, 0], dx.reshape(n_seq, SEQ, D_MODEL), small, emitted


def _peer(k):
    x, y, c = lax.axis_index("x"), lax.axis_index("y"), lax.axis_index("c")
    px = 1 - x if k & 4 else x
    py = 1 - y if k & 2 else y
    pc = 1 - c if k & 1 else c
    return (px, py, pc), 4 * px + 2 * py + pc


def _my_index():
    return 4 * lax.axis_index("x") + 2 * lax.axis_index("y") + lax.axis_index("c")


def _src_for(kinds, in_refs, i, idx):
    return in_refs[i] if kinds[i] == "gather" else in_refs[i].at[idx]


def _local_copies(kinds, in_refs, out_refs, local_sems):
    me = _my_index()
    return [pltpu.make_async_copy(_src_for(kinds, in_refs, i, me), out_refs[i].at[me], local_sems.at[i])
            for i in range(len(kinds))]


def _remote_copies(kinds, in_refs, out_refs, send_sems, recv_sems, *, arriving):
    me = _my_index()
    copies = []
    for k in range(1, N_DEV):
        dev, idx = _peer(k)
        for i in range(len(kinds)):
            j = i * (N_DEV - 1) + k - 1
            copies.append(pltpu.make_async_remote_copy(
                src_ref=_src_for(kinds, in_refs, i, idx), dst_ref=out_refs[i].at[idx if arriving else me],
                send_sem=send_sems.at[j], recv_sem=recv_sems.at[j], device_id=dev, device_id_type=MESH))
    return copies


def _out_shape(a, kind):
    return ((N_DEV,) + a.shape) if kind == "gather" else a.shape


def exchange(items, *, name, after=()):
    n = len(items)
    kinds = [k for _, k in items]
    after = list(after)

    def body(*refs):
        in_refs, out_refs = refs[:n], refs[n + len(after):2 * n + len(after)]
        send_sems, recv_sems, local_sems = refs[-3:]
        local = _local_copies(kinds, in_refs, out_refs, local_sems)
        sends = _remote_copies(kinds, in_refs, out_refs, send_sems, recv_sems, arriving=False)
        for cp in local + sends:
            cp.start()
        for cp in _remote_copies(kinds, in_refs, out_refs, send_sems, recv_sems, arriving=True):
            cp.wait_recv()
        for cp in sends:
            cp.wait_send()
        for cp in local:
            cp.wait()

    any_spec = pl.BlockSpec(memory_space=pl.ANY)
    return pl.pallas_call(
        body, name=name,
        in_specs=[any_spec] * (n + len(after)), out_specs=[any_spec] * n,
        out_shape=[_sds(_out_shape(a, k), a.dtype) for a, k in items],
        scratch_shapes=[pltpu.SemaphoreType.DMA((n * (N_DEV - 1),)), pltpu.SemaphoreType.DMA((n * (N_DEV - 1),)),
                        pltpu.SemaphoreType.DMA((n,))],
    )(*[a for a, _ in items], *after)


_HBM = pl.BlockSpec(memory_space=pltpu.HBM)
_SEM = pl.BlockSpec(memory_space=pltpu.SEMAPHORE)
_EFFECT = pltpu.SideEffectType.DATAFLOW_SIDE_EFFECTING


def exchange_start(items, after, *, name):
    n = len(items)
    kinds = [k for _, k in items]

    def body(*refs):
        in_refs, land_refs = refs[:n], refs[n:2 * n]
        send_sems, recv_sems, local_sems = refs[2 * n + 1:2 * n + 4]
        token = refs[-1]
        for cp in (_local_copies(kinds, in_refs, land_refs, local_sems)
                   + _remote_copies(kinds, in_refs, land_refs, send_sems, recv_sems, arriving=False)):
            cp.start()
        token[...] = jnp.zeros_like(token)

    srcs = [pltpu.with_memory_space_constraint(a, pltpu.HBM) for a, _ in items]
    lands = [pltpu.with_memory_space_constraint(lax.empty(_out_shape(a, k), a.dtype), pltpu.HBM) for a, k in items]
    outs = pl.pallas_call(
        body, name=name,
        out_shape=(pltpu.SemaphoreType.DMA((n * (N_DEV - 1),)), pltpu.SemaphoreType.DMA((n * (N_DEV - 1),)),
                   pltpu.SemaphoreType.DMA((n,)),
                   *[pltpu.HBM(a.shape, a.dtype) for a in srcs], *[pltpu.HBM(a.shape, a.dtype) for a in lands],
                   _sds((8, 128), f32)),
        in_specs=[_HBM] * (2 * n) + [pl.BlockSpec(memory_space=pl.ANY)],
        out_specs=(_SEM, _SEM, _SEM, *[_HBM] * (2 * n), pl.BlockSpec(memory_space=pltpu.VMEM)),
        input_output_aliases={i: 3 + i for i in range(2 * n)},
        compiler_params=pltpu.CompilerParams(has_side_effects=_EFFECT),
    )(*srcs, *lands, after)
    return {"kinds": kinds, "sems": outs[:3], "srcs": outs[3:3 + n], "lands": outs[3 + n:3 + 2 * n], "token": outs[-1]}


def exchange_wait(handle, after, *, name):
    kinds = handle["kinds"]
    n = len(kinds)

    def body(*refs):
        in_refs, land_refs = refs[:n], refs[n:2 * n]
        send_sems, recv_sems, local_sems = refs[2 * n:2 * n + 3]
        for cp in _remote_copies(kinds, in_refs, land_refs, send_sems, recv_sems, arriving=True):
            cp.wait_recv()
        for cp in _remote_copies(kinds, in_refs, land_refs, send_sems, recv_sems, arriving=False):
            cp.wait_send()
        for cp in _local_copies(kinds, in_refs, land_refs, local_sems):
            cp.wait()

    srcs, lands = list(handle["srcs"]), list(handle["lands"])
    after = list(after) if isinstance(after, (list, tuple)) else [after]
    outs = pl.pallas_call(
        body, name=name,
        out_shape=tuple(pltpu.HBM(a.shape, a.dtype) for a in srcs + lands),
        in_specs=[_HBM] * (2 * n) + [_SEM] * 3 + [pl.BlockSpec(memory_space=pl.ANY)] * len(after),
        out_specs=tuple([_HBM] * (2 * n)),
        input_output_aliases={i: i for i in range(2 * n)},
        compiler_params=pltpu.CompilerParams(has_side_effects=_EFFECT),
    )(*srcs, *lands, *handle["sems"], *after)
    return list(outs[n:])


def adamw(slots, w, m, v, *, name, layer=None, into=None):
    R, C = w.shape[-2:]
    tr = _tile(R, (256, 128, 64, 32, 16, 8))
    c1 = 1.0 - ADAM_B1 ** ADAM_STEP
    c2 = 1.0 - ADAM_B2 ** ADAM_STEP
    extra = [] if into is None else list(into)

    def body(s_ref, w_ref, m_ref, v_ref, *refs):
        g_ref, d_ref, m2_ref, v2_ref = refs[len(extra):]
        g = s_ref[0].astype(f32)
        for d in range(1, N_DEV):
            g = g + s_ref[d].astype(f32)
        m2 = ADAM_B1 * m_ref[...] + (1.0 - ADAM_B1) * g
        v2 = ADAM_B2 * v_ref[...] + (1.0 - ADAM_B2) * (g * g)
        g_ref[...] = g
        m2_ref[...] = m2
        v2_ref[...] = v2
        d_ref[...] = -ADAM_LR * ((m2 / c1) / (jnp.sqrt(v2 / c2) + ADAM_EPS) + ADAM_WD * w_ref[...])

    if layer is None:
        blk = pl.BlockSpec((tr, C), lambda i: (i, 0))
    else:
        blk = pl.BlockSpec((None, tr, C), lambda i: (layer, i, 0))
    return pl.pallas_call(
        body, name=name, grid=(R // tr,),
        in_specs=[pl.BlockSpec((N_DEV, tr, C), lambda i: (0, i, 0)), blk, blk, blk]
        + [pl.BlockSpec(memory_space=pl.ANY)] * len(extra),
        out_specs=[blk] * 4,
        out_shape=[_sds(w.shape, f32)] * 4,
        input_output_aliases={4 + j: j for j in range(len(extra))},
        compiler_params=_params(("parallel",)),
    )(slots, w, m, v, *extra)


WEIGHTS = ("norm_gains", "mem_norm", "w_in", "w_mem_kv", "w_out", "w_pool", "pool_scale", "kv_norm", "w_kv",
           "w_gate_up", "w_down")
LAYER_MATS = ("w_in", "w_mem_kv", "w_out", "w_gate_up", "w_down")
POOL_SHARD = MAIN_W // N_DEV
KV_SHARD = 2 * MAIN_W // N_DEV
LOOKAHEAD = 2


def _pack_small(gains, pscale):
    lead = gains.shape[:-3]
    g = gains.reshape(lead + (16, 128))
    p = jnp.zeros(lead + (8, 128), f32).at[..., :2, :POOL_SHARD].set(pscale)
    return jnp.concatenate([g, p], axis=-2)


def _unpack_small(a):
    return a[:16].reshape(4, 4, 128), a[16:18, :POOL_SHARD]


def _pack_repl(mem_norm, kv_norm):
    return jnp.concatenate([mem_norm, kv_norm.reshape(1, D_MODEL), jnp.zeros((3, D_MODEL), f32)], axis=0)


def _unpack_repl(a):
    return a[:4], a[4]


def kernel(x, mem, positions, norm_gains, mem_norm, w_in, w_mem_kv, w_out, w_pool, pool_scale, kv_norm, w_kv, w_gate_up, w_down, loss_target, m_norm_gains, m_mem_norm, m_w_in, m_w_mem_kv, m_w_out, m_w_pool, m_pool_scale, m_kv_norm, m_w_kv, m_w_gate_up, m_w_down, v_norm_gains, v_mem_norm, v_w_in, v_w_mem_kv, v_w_out, v_w_pool, v_pool_scale, v_kv_norm, v_w_kv, v_w_gate_up, v_w_down):
    w = dict(norm_gains=norm_gains, mem_norm=mem_norm, w_in=w_in, w_mem_kv=w_mem_kv, w_out=w_out, w_pool=w_pool,
             pool_scale=pool_scale, kv_norm=kv_norm, w_kv=w_kv, w_gate_up=w_gate_up, w_down=w_down)
    m = dict(norm_gains=m_norm_gains, mem_norm=m_mem_norm, w_in=m_w_in, w_mem_kv=m_w_mem_kv, w_out=m_w_out,
             w_pool=m_w_pool, pool_scale=m_pool_scale, kv_norm=m_kv_norm, w_kv=m_w_kv, w_gate_up=m_w_gate_up,
             w_down=m_w_down)
    v = dict(norm_gains=v_norm_gains, mem_norm=v_mem_norm, w_in=v_w_in, w_mem_kv=v_w_mem_kv, w_out=v_w_out,
             w_pool=v_w_pool, pool_scale=v_pool_scale, kv_norm=v_kv_norm, w_kv=v_w_kv, w_gate_up=v_w_gate_up,
             w_down=v_w_down)

    def transposed_view(d):
        d = dict(d)
        d["w_gate_up"] = jnp.swapaxes(d["w_gate_up"], 1, 2)
        d["w_kv"] = jnp.swapaxes(d["w_kv"], 0, 1)
        return d

    wv, mv, vv = transposed_view(w), transposed_view(m), transposed_view(v)

    small = _pack_small(norm_gains, pool_scale)
    (gsmall,) = exchange([(small, "gather")], name="gather_small")
    P = {"norm_gains": jnp.moveaxis(gsmall[:, :16].reshape(N_DEV, 4, 4, 128), 0, 2).reshape(4, 4, D_MODEL),
         "pool_scale": jnp.moveaxis(gsmall[:, 16:18, :POOL_SHARD], 0, 1).reshape(2, MAIN_W),
         "mem_norm": mem_norm, "kv_norm": kv_norm, "w_pool": w_pool}

    PARTS = {"mix": ("w_in", "w_mem_kv", "w_out"), "ffn": ("w_gate_up", "w_down")}

    def part_items(l, part):
        items = [(wv[k][l].astype(bf16), "gather") for k in PARTS[part]]
        if part == "ffn" and l == N_A_LAYERS - 1:
            items.append((wv["w_kv"].astype(bf16), "gather"))
        return items

    handles = {}

    def start_layer(l, after):
        for part in ("mix", "ffn"):
            handles[l, part] = exchange_start(part_items(l, part), after, name=f"gather_start_{part}_l{l}")
            after = handles[l, part]["token"]
        return after

    token = gsmall
    for l in range(LOOKAHEAD):
        token = start_layer(l, token)
    landed = {}

    def layer_weights(l, part, after):
        first = l == 0 and part == "mix"
        got = exchange_wait(handles[l, part], token if first else after, name=f"gather_wait_{part}_l{l}")
        landed[l, part] = got
        started = None
        if part == "mix" and l + LOOKAHEAD < DEPTH:
            started = start_layer(l + LOOKAHEAD, got[0])
        W = {k: g.reshape(-1, g.shape[-1]) for k, g in zip(PARTS[part], got)}
        return W, started

    def kv_weight(after):
        g = landed[N_A_LAYERS - 1, "ffn"][len(PARTS["ffn"])]
        return g.reshape(2 * MAIN_W, D_MODEL)

    ghandles = {}

    def emit_grads(l, part, gw):
        items = [(gw[k].reshape((N_DEV, -1) + gw[k].shape[-1:]), "scatter") for k in PARTS[part]]
        if part == "mix" and l == N_A_LAYERS:
            items.append((gw["w_kv"].reshape(N_DEV, KV_SHARD, D_MODEL), "scatter"))
        if part == "mix" and l < N_A_LAYERS:
            items.append((gw["w_pool"], "gather"))
        ghandles[l, part] = exchange_start(items, items[0][0], name=f"scatter_start_{part}_l{l}")
        return ghandles[l, part]["token"]

    sq, grad_x, GS, emitted = local_step(x, mem, positions, loss_target, P, layer_weights, kv_weight, emit_grads)
    loss = lax.psum(0.5 * sq / D_MODEL, ("x", "y", "c"))

    def pool3(a):
        return a.reshape(N_A_LAYERS, MAIN_W, POOL_GROUP)

    out = {}
    after = [emitted]

    def finish_layer(l, after):
        for part in ("ffn", "mix"):
            got = exchange_wait(ghandles[l, part], after, name=f"scatter_wait_{part}_l{l}")
            after = []
            for k, slots in zip(PARTS[part], got):
                out[k] = adamw(slots, wv[k], mv[k], vv[k], name=f"adamw_{k}_l{l}", layer=l, into=out.get(k))
                after.append(out[k][0])
            if part == "mix" and l == N_A_LAYERS:
                out["w_kv"] = adamw(got[-1], wv["w_kv"], mv["w_kv"], vv["w_kv"], name="adamw_w_kv")
                after.append(out["w_kv"][0])
            if part == "mix" and l < N_A_LAYERS:
                out["w_pool"] = adamw(got[-1], pool3(w_pool), pool3(m_w_pool), pool3(v_w_pool), name=f"adamw_w_pool_l{l}",
                                      layer=l, into=out.get("w_pool"))
                after.append(out["w_pool"][0])
        return after

    for l in reversed(range(1, DEPTH)):
        after = finish_layer(l, after)

    gs = _pack_small(jnp.moveaxis(GS["norm_gains"].reshape(4, 4, N_DEV, 128), 2, 0),
                     jnp.moveaxis(GS["pool_scale"].reshape(2, N_DEV, POOL_SHARD), 1, 0))
    parts_small, parts_repl = exchange(
        [(gs, "scatter"), (_pack_repl(GS["mem_norm"], GS["kv_norm"]), "gather")],
        name="exchange_small_grads", after=after)
    finish_layer(0, [parts_small])
    out["w_gate_up"] = [jnp.swapaxes(r, 1, 2) for r in out["w_gate_up"]]
    out["w_kv"] = [jnp.swapaxes(r, 0, 1) for r in out["w_kv"]]
    out["w_pool"] = [r.reshape(w_pool.shape) for r in out["w_pool"]]

    res = adamw(parts_small, small, _pack_small(m_norm_gains, m_pool_scale), _pack_small(v_norm_gains, v_pool_scale),
                name="adamw_small")
    out["norm_gains"], out["pool_scale"] = zip(*[_unpack_small(r) for r in res])
    res = adamw(parts_repl, _pack_repl(mem_norm, kv_norm), _pack_repl(m_mem_norm, m_kv_norm),
                _pack_repl(v_mem_norm, v_kv_norm), name="adamw_repl")
    out["mem_norm"], out["kv_norm"] = zip(*[_unpack_repl(r) for r in res])

    return (loss, grad_x, *[out[k][0] for k in WEIGHTS], *[out[k][1] for k in WEIGHTS],
            *[out[k][2] for k in WEIGHTS], *[out[k][3] for k in WEIGHTS])
```

```python
import numpy as np
import jax
import jax.numpy as jnp
from jax import lax
from jax.experimental import pallas as pl
from jax.experimental.pallas import tpu as pltpu

f32 = jnp.float32
bf16 = jnp.bfloat16

D_MODEL = 1024
SEQ = 2048
DEPTH = 4
N_MEM = 256
HEAD_DIM = 64
N_MEM_HEADS = 4
MEM_W = 256
MAIN_W = 768
POOL_WINDOWS = (2, 4, 8, 16)
POOL_GROUP = 192
POOL_HALO = 16
DIL_PATTERNS = ((128, 1), (512, 4), (2048, 16))
N_GROUPS = 3
GROUP_W = 256
BAND = 128
N_A_LAYERS = 2
D_FF = 2816
ROPE_THETA = 10000.0
EPS = 1e-6
NEG = -1e30
SCALE = HEAD_DIM ** -0.5
N_DEV = 8

ADAM_LR = 0.001
ADAM_B1 = 0.9
ADAM_B2 = 0.999
ADAM_EPS = 1e-08
ADAM_WD = 0.01
ADAM_STEP = 10

VMEM_LIMIT_BYTES = 56 * 1024 * 1024
MESH = pl.DeviceIdType.MESH

NN = (((1,), (0,)), ((), ()))
NT = (((1,), (1,)), ((), ()))
TN = (((0,), (0,)), ((), ()))


def _params(sem=None):
    return pltpu.CompilerParams(dimension_semantics=sem, vmem_limit_bytes=VMEM_LIMIT_BYTES)


def _tile(n, cands):
    for c in cands:
        if n % c == 0:
            return c
    return n


def _sds(shape, dtype):
    return jax.ShapeDtypeStruct(tuple(shape), dtype)


def _rms_r(v):
    return lax.rsqrt(jnp.mean(v * v, axis=-1, keepdims=True) + EPS)


def rms_matmul(x, gain, w, *, name, out_dtype, transposed=False, after=None):
    M, K = x.shape
    N = w.shape[0] if transposed else w.shape[1]
    tm = min(1024, M)
    tn = _tile(N, (512, 256, 128))
    order = [] if after is None else [after]

    def body(x_ref, g_ref, w_ref, *refs):
        z_ref, h_ref = refs[len(order):]

        @pl.when(pl.program_id(1) == 0)
        def _():
            xv = x_ref[...]
            h_ref[...] = (xv * _rms_r(xv) * g_ref[...]).astype(bf16)

        z_ref[...] = lax.dot_general(h_ref[...], w_ref[...], NT if transposed else NN,
                                     preferred_element_type=f32).astype(z_ref.dtype)

    w_spec = pl.BlockSpec((tn, K), lambda i, j: (j, 0)) if transposed else pl.BlockSpec((K, tn), lambda i, j: (0, j))
    return pl.pallas_call(
        body, name=name, grid=(M // tm, N // tn),
        in_specs=[pl.BlockSpec((tm, K), lambda i, j: (i, 0)),
                  pl.BlockSpec((1, K), lambda i, j: (0, 0)),
                  w_spec] + [pl.BlockSpec(memory_space=pl.ANY)] * len(order),
        out_specs=[pl.BlockSpec((tm, tn), lambda i, j: (i, j)), pl.BlockSpec((tm, K), lambda i, j: (i, 0))],
        out_shape=[_sds((M, N), out_dtype), _sds((M, K), bf16)],
        compiler_params=_params(("parallel", "arbitrary")),
    )(x, gain, w, *order)


def matmul_rms_res(a, w, gain, res, *, name):
    M, K = a.shape
    N = w.shape[1]
    tm = min(512, M)

    def body(a_ref, w_ref, g_ref, r_ref, y_ref, x_ref):
        y = jnp.dot(a_ref[...], w_ref[...], preferred_element_type=f32)
        y_ref[...] = y.astype(bf16)
        x_ref[...] = r_ref[...] + y * _rms_r(y) * g_ref[...]

    row = pl.BlockSpec((tm, N), lambda i: (i, 0))
    return pl.pallas_call(
        body, name=name, grid=(M // tm,),
        in_specs=[pl.BlockSpec((tm, K), lambda i: (i, 0)),
                  pl.BlockSpec((K, N), lambda i: (0, 0)),
                  pl.BlockSpec((1, N), lambda i: (0, 0)),
                  row],
        out_specs=[row, row],
        out_shape=[_sds((M, N), bf16), _sds((M, N), f32)],
        compiler_params=_params(("parallel",)),
    )(a, w, gain, res)


def matmul(a, b, dims, *, name, out_dtype):
    if dims is TN:
        K, M = a.shape
        tm = _tile(M, (512, 256, 128))
        a_spec = pl.BlockSpec((K, tm), lambda i: (0, i))
    else:
        M, K = a.shape
        tm = _tile(M, (512, 256, 128))
        a_spec = pl.BlockSpec((tm, K), lambda i: (i, 0))
    N = b.shape[0] if dims is NT else b.shape[1]

    def body(a_ref, b_ref, o_ref):
        o_ref[...] = lax.dot_general(a_ref[...].astype(bf16), b_ref[...].astype(bf16), dims,
                                     preferred_element_type=f32).astype(o_ref.dtype)

    return pl.pallas_call(
        body, name=name, grid=(M // tm,),
        in_specs=[a_spec, pl.BlockSpec(b.shape, lambda i: (0, 0))],
        out_specs=pl.BlockSpec((tm, N), lambda i: (i, 0)),
        out_shape=_sds((M, N), out_dtype),
        compiler_params=_params(("parallel",)),
    )(a, b)


def rms_gate_up(x, gain, wt, *, name):
    M, K = x.shape
    tm = min(1024, M)
    tn = _tile(D_FF, (256, 128))
    nj = D_FF // tn

    def body(x_ref, gn_ref, wg_ref, wu_ref, g_ref, u_ref, a_ref, h_ref):
        @pl.when(pl.program_id(1) == 0)
        def _():
            xv = x_ref[...]
            h_ref[...] = (xv * _rms_r(xv) * gn_ref[...]).astype(bf16)

        h = h_ref[...]
        g = lax.dot_general(h, wg_ref[...], NT, preferred_element_type=f32)
        u = lax.dot_general(h, wu_ref[...], NT, preferred_element_type=f32)
        g_ref[...] = g.astype(bf16)
        u_ref[...] = u.astype(bf16)
        a_ref[...] = (g * (1.0 / (1.0 + jnp.exp(-g))) * u).astype(bf16)

    col = pl.BlockSpec((tm, tn), lambda i, j: (i, j))
    return pl.pallas_call(
        body, name=name, grid=(M // tm, nj),
        in_specs=[pl.BlockSpec((tm, K), lambda i, j: (i, 0)),
                  pl.BlockSpec((1, K), lambda i, j: (0, 0)),
                  pl.BlockSpec((tn, K), lambda i, j: (j, 0)),
                  pl.BlockSpec((tn, K), lambda i, j: (j + nj, 0))],
        out_specs=[col, col, col, pl.BlockSpec((tm, K), lambda i, j: (i, 0))],
        out_shape=[_sds((M, D_FF), bf16)] * 3 + [_sds((M, K), bf16)],
        compiler_params=_params(("parallel", "arbitrary")),
    )(x, gain, wt, wt)


def _rms_bwd_math(yv, gain, dn):
    r = _rms_r(yv)
    q = dn * gain
    dy = r * q - yv * (r * r * r) * jnp.mean(q * yv, axis=-1, keepdims=True)
    return dy, jnp.sum(dn * yv * r, axis=0, keepdims=True)


def _accumulate(ref, val):
    @pl.when(pl.program_id(0) == 0)
    def _():
        ref[...] = jnp.zeros_like(ref)

    ref[...] += val


def down_bwd(y, gain, dn, w_down, g, u, *, name, after=None):
    M, K = y.shape
    tm = min(512, M)
    order = [] if after is None else [after]

    def body(y_ref, gn_ref, dn_ref, w_ref, g_ref, u_ref, *refs):
        dy_ref, o_ref, dg_ref = refs[len(order):]
        dy, dgain = _rms_bwd_math(y_ref[...].astype(f32), gn_ref[...], dn_ref[...])
        dy = dy.astype(bf16)
        dy_ref[...] = dy
        _accumulate(dg_ref, dgain)
        da = lax.dot_general(dy, w_ref[...], NT, preferred_element_type=f32)
        g = g_ref[...].astype(f32)
        u = u_ref[...].astype(f32)
        s = 1.0 / (1.0 + jnp.exp(-g))
        o_ref[:, :D_FF] = (da * u * s * (1.0 + g * (1.0 - s))).astype(bf16)
        o_ref[:, D_FF:] = (da * g * s).astype(bf16)

    row = pl.BlockSpec((tm, K), lambda i: (i, 0))
    vec = pl.BlockSpec((1, K), lambda i: (0, 0))
    wide = pl.BlockSpec((tm, D_FF), lambda i: (i, 0))
    return pl.pallas_call(
        body, name=name, grid=(M // tm,),
        in_specs=[row, vec, row, pl.BlockSpec((D_FF, K), lambda i: (0, 0)), wide, wide]
        + [pl.BlockSpec(memory_space=pl.ANY)] * len(order),
        out_specs=[row, pl.BlockSpec((tm, 2 * D_FF), lambda i: (i, 0)), vec],
        out_shape=[_sds((M, K), bf16), _sds((M, 2 * D_FF), bf16), _sds((1, K), f32)],
        compiler_params=_params(("arbitrary",)),
    )(y, gain, dn, w_down, g, u, *order)


def rms_bwd_matmul(y, gain, dn, w, dims, *, name, after=None):
    M, K = y.shape
    N = w.shape[0] if dims is NT else w.shape[1]
    tm = min(512, M)
    order = [] if after is None else [after]

    def body(y_ref, gn_ref, dn_ref, w_ref, *refs):
        dy_ref, o_ref, dg_ref = refs[len(order):]
        dy, dgain = _rms_bwd_math(y_ref[...].astype(f32), gn_ref[...], dn_ref[...].astype(f32))
        dy = dy.astype(bf16)
        dy_ref[...] = dy
        _accumulate(dg_ref, dgain)
        o_ref[...] = lax.dot_general(dy, w_ref[...], dims, preferred_element_type=f32).astype(bf16)

    row = pl.BlockSpec((tm, K), lambda i: (i, 0))
    vec = pl.BlockSpec((1, K), lambda i: (0, 0))
    return pl.pallas_call(
        body, name=name, grid=(M // tm,),
        in_specs=[row, vec, row, pl.BlockSpec(w.shape, lambda i: (0, 0))]
        + [pl.BlockSpec(memory_space=pl.ANY)] * len(order),
        out_specs=[row, pl.BlockSpec((tm, N), lambda i: (i, 0)), vec],
        out_shape=[_sds((M, K), bf16), _sds((M, N), bf16), _sds((1, K), f32)],
        compiler_params=_params(("arbitrary",)),
    )(y, gain, dn, w, *order)


def matmul_rms_bwd(a, b, dims, y, gain, res, *, name):
    M, K = a.shape
    N = y.shape[1]
    tm = 256

    def body(a_ref, b_ref, y_ref, gn_ref, r_ref, dx_ref, dg_ref):
        dn = lax.dot_general(a_ref[...], b_ref[...], dims, preferred_element_type=f32)
        dy, dgain = _rms_bwd_math(y_ref[...], gn_ref[...], dn)
        dx_ref[...] = dy + r_ref[...]
        _accumulate(dg_ref, dgain)

    row = pl.BlockSpec((tm, N), lambda i: (i, 0))
    vec = pl.BlockSpec((1, N), lambda i: (0, 0))
    return pl.pallas_call(
        body, name=name, grid=(M // tm,),
        in_specs=[pl.BlockSpec((tm, K), lambda i: (i, 0)), pl.BlockSpec(b.shape, lambda i: (0, 0)), row, vec, row],
        out_specs=[row, vec],
        out_shape=[_sds((M, N), f32), _sds((1, N), f32)],
        compiler_params=_params(("arbitrary",)),
    )(a, b, y, gain, res)


def rms_bwd(y, gain, dn, res, *, name, out_dtype, after=None):
    M, N = y.shape
    tm = min(512, M)
    has_res = res is not None
    order = [] if after is None else [after]

    def body(*refs):
        y_ref, g_ref, dn_ref = refs[:3]
        r_ref = refs[3] if has_res else None
        dy_ref, dg_ref = refs[-2:]
        dy, dgain = _rms_bwd_math(y_ref[...].astype(f32), g_ref[...], dn_ref[...].astype(f32))
        if has_res:
            dy = dy + r_ref[...]
        dy_ref[...] = dy.astype(dy_ref.dtype)
        _accumulate(dg_ref, dgain)

    row = pl.BlockSpec((tm, N), lambda i: (i, 0))
    vec = pl.BlockSpec((1, N), lambda i: (0, 0))
    args = [y, gain, dn] + ([res] if has_res else []) + order
    return pl.pallas_call(
        body, name=name, grid=(M // tm,),
        in_specs=[row, vec, row] + ([row] if has_res else []) + [pl.BlockSpec(memory_space=pl.ANY)] * len(order),
        out_specs=[row, vec],
        out_shape=[_sds((M, N), out_dtype), _sds((1, N), f32)],
        compiler_params=_params(("arbitrary",)),
    )(*args)


def loss_head(x, target, *, name):
    M, N = x.shape
    tm = min(512, M)

    def body(x_ref, t_ref, dx_ref, l_ref):
        e = x_ref[...] - t_ref[...]
        dx_ref[...] = e * (1.0 / N)

        @pl.when(pl.program_id(0) == 0)
        def _():
            l_ref[...] = jnp.zeros_like(l_ref)

        l_ref[...] += jnp.sum(jnp.sum(e * e, axis=0, keepdims=True), axis=1, keepdims=True)

    row = pl.BlockSpec((tm, N), lambda i: (i, 0))
    return pl.pallas_call(
        body, name=name, grid=(M // tm,),
        in_specs=[row, row],
        out_specs=[row, pl.BlockSpec((8, 128), lambda i: (0, 0))],
        out_shape=[_sds((M, N), f32), _sds((8, 128), f32)],
        compiler_params=_params(("arbitrary",)),
    )(x, target)


def _pool_select(a1, a2, a3, a4):
    col = lax.broadcasted_iota(jnp.int32, (1, MAIN_W), 1) // POOL_GROUP
    return jnp.where(col == 0, a1, jnp.where(col == 1, a2, jnp.where(col == 2, a3, a4)))


def _pool_count(t):
    col = lax.broadcasted_iota(jnp.int32, (1, MAIN_W), 1) // POOL_GROUP
    win = jnp.where(col == 0, 2, jnp.where(col == 1, 4, jnp.where(col == 2, 8, 16)))
    return jnp.minimum(t + 1, win).astype(f32)


def pool_fwd(z, wbd, scale, *, name):
    M = z.shape[0]
    tm = 256
    nper = SEQ // tm
    hb = tm // POOL_HALO

    def body(zc_ref, zh_ref, w_ref, s_ref, p_ref, y_ref):
        i = pl.program_id(0)
        seq_blk = i % nper
        halo = jnp.where(seq_blk == 0, 0.0, zh_ref[...])
        u = zc_ref[...]
        ext = jnp.concatenate([halo, u], axis=0)
        a1 = ext + pltpu.roll(ext, 1, 0)
        a2 = a1 + pltpu.roll(a1, 2, 0)
        a3 = a2 + pltpu.roll(a2, 4, 0)
        a4 = a3 + pltpu.roll(a3, 8, 0)
        sums = _pool_select(a1, a2, a3, a4)[POOL_HALO:]
        t = seq_blk * tm + lax.broadcasted_iota(jnp.int32, (tm, 1), 0)
        p = (sums / _pool_count(t) - u).astype(bf16)
        p_ref[...] = p
        y_ref[...] = (jnp.dot(p, w_ref[...], preferred_element_type=f32) * s_ref[...]).astype(bf16)

    return pl.pallas_call(
        body, name=name, grid=(M // tm,),
        in_specs=[pl.BlockSpec((tm, MAIN_W), lambda i: (i, 0)),
                  pl.BlockSpec((POOL_HALO, MAIN_W), lambda i: (jnp.maximum(i * hb - 1, 0), 0)),
                  pl.BlockSpec((MAIN_W, MAIN_W), lambda i: (0, 0)),
                  pl.BlockSpec((1, MAIN_W), lambda i: (0, 0))],
        out_specs=[pl.BlockSpec((tm, MAIN_W), lambda i: (i, 0)),
                   pl.BlockSpec((tm, MAIN_W), lambda i: (i, 0))],
        out_shape=[_sds((M, MAIN_W), bf16), _sds((M, D_MODEL), bf16)],
        compiler_params=_params(("parallel",)),
    )(z, z, wbd, scale)


def pool_bwd(dyc, p, wbd, scale, *, name):
    M = p.shape[0]
    tm = 256
    nper = SEQ // tm
    hb = tm // POOL_HALO
    last_hb = M // POOL_HALO - 1

    def body(dy_ref, dyh_ref, p_ref, w_ref, s_ref, dz_ref, dw_ref, ds_ref):
        i = pl.program_id(0)
        seq_blk = i % nper
        dy = dy_ref[...].astype(f32)
        pv = p_ref[...]
        w = w_ref[...]
        sc = s_ref[...]

        @pl.when(i == 0)
        def _():
            dw_ref[...] = jnp.zeros_like(dw_ref)
            ds_ref[...] = jnp.zeros_like(ds_ref)

        v = jnp.dot(pv, w, preferred_element_type=f32)
        ds_ref[...] += jnp.sum(dy * v, axis=0, keepdims=True)
        dv = (dy * sc).astype(bf16)
        dw_ref[...] += lax.dot_general(pv, dv, TN, preferred_element_type=f32)
        dp = lax.dot_general(dv, w, NT, preferred_element_type=f32)
        dvh = jnp.where(seq_blk == nper - 1, 0.0, dyh_ref[...].astype(f32) * sc).astype(bf16)
        dph = lax.dot_general(dvh, w, NT, preferred_element_type=f32)
        ext = jnp.concatenate([dp, dph], axis=0)
        n = tm + POOL_HALO
        t = seq_blk * tm + lax.broadcasted_iota(jnp.int32, (n, 1), 0)
        e = ext / _pool_count(t)
        b1 = e + pltpu.roll(e, n - 1, 0)
        b2 = b1 + pltpu.roll(b1, n - 2, 0)
        b3 = b2 + pltpu.roll(b2, n - 4, 0)
        b4 = b3 + pltpu.roll(b3, n - 8, 0)
        dz_ref[...] = (_pool_select(b1, b2, b3, b4)[:tm] - dp).astype(dz_ref.dtype)

    return pl.pallas_call(
        body, name=name, grid=(M // tm,),
        in_specs=[pl.BlockSpec((tm, MAIN_W), lambda i: (i, 0)),
                  pl.BlockSpec((POOL_HALO, MAIN_W), lambda i: (jnp.minimum((i + 1) * hb, last_hb), 0)),
                  pl.BlockSpec((tm, MAIN_W), lambda i: (i, 0)),
                  pl.BlockSpec((MAIN_W, MAIN_W), lambda i: (0, 0)),
                  pl.BlockSpec((1, MAIN_W), lambda i: (0, 0))],
        out_specs=[pl.BlockSpec((tm, MAIN_W), lambda i: (i, 0)),
                   pl.BlockSpec((MAIN_W, MAIN_W), lambda i: (0, 0)),
                   pl.BlockSpec((1, MAIN_W), lambda i: (0, 0))],
        out_shape=[_sds((M, D_MODEL), bf16), _sds((MAIN_W, MAIN_W), f32), _sds((1, MAIN_W), f32)],
        compiler_params=_params(("arbitrary",)),
    )(dyc, dyc, p, wbd, scale)


def _mem_probs(q, kv, h):
    hs = slice(h * HEAD_DIM, (h + 1) * HEAD_DIM)
    qh = q[:, hs]
    kh = kv[:, hs]
    s = lax.dot_general(qh, kh, NT, preferred_element_type=f32) * SCALE
    m = jnp.max(s, axis=-1, keepdims=True)
    e = jnp.exp(s - m)
    return qh, kh, e / jnp.sum(e, axis=-1, keepdims=True)


def memattn_fwd(z, kvm, ycat, *, name, n_seq):
    M = z.shape[0]
    tq = 512
    nq = SEQ // tq

    def body(q_ref, kv_ref, _, o_ref):
        q = q_ref[...].astype(bf16)
        kv = kv_ref[...]
        outs = []
        for h in range(N_MEM_HEADS):
            _, _, p = _mem_probs(q, kv, h)
            vh = kv[:, MEM_W + h * HEAD_DIM: MEM_W + (h + 1) * HEAD_DIM]
            outs.append(jnp.dot(p.astype(bf16), vh, preferred_element_type=f32))
        o_ref[...] = jnp.concatenate(outs, axis=1).astype(bf16)

    return pl.pallas_call(
        body, name=name, grid=(n_seq, nq),
        in_specs=[pl.BlockSpec((tq, MEM_W), lambda b, i: (b * nq + i, 3)),
                  pl.BlockSpec((N_MEM, 2 * MEM_W), lambda b, i: (b, 0)),
                  pl.BlockSpec(memory_space=pl.ANY)],
        out_specs=pl.BlockSpec((tq, MEM_W), lambda b, i: (b * nq + i, 3)),
        out_shape=_sds((M, D_MODEL), bf16),
        input_output_aliases={2: 0},
        compiler_params=_params(("parallel", "parallel")),
    )(z, kvm, ycat)


def memattn_bwd(z, kvm, dyc, dz, *, name, n_seq):
    M = z.shape[0]
    tq = 512
    nq = SEQ // tq

    def body(q_ref, kv_ref, dy_ref, _, dq_ref, dkv_ref):
        q = q_ref[...].astype(bf16)
        kv = kv_ref[...]
        dy = dy_ref[...].astype(bf16)
        dqs, dks, dvs = [], [], []
        for h in range(N_MEM_HEADS):
            hs = slice(h * HEAD_DIM, (h + 1) * HEAD_DIM)
            qh, kh, p = _mem_probs(q, kv, h)
            vh = kv[:, MEM_W + h * HEAD_DIM: MEM_W + (h + 1) * HEAD_DIM]
            dyh = dy[:, hs]
            dvs.append(lax.dot_general(p.astype(bf16), dyh, TN, preferred_element_type=f32))
            dp = lax.dot_general(dyh, vh, NT, preferred_element_type=f32)
            ds = (p * (dp - jnp.sum(dp * p, axis=-1, keepdims=True)) * SCALE).astype(bf16)
            dqs.append(jnp.dot(ds, kh, preferred_element_type=f32))
            dks.append(lax.dot_general(ds, qh, TN, preferred_element_type=f32))
        dq_ref[...] = jnp.concatenate(dqs, axis=1).astype(bf16)

        @pl.when(pl.program_id(1) == 0)
        def _():
            dkv_ref[...] = jnp.zeros_like(dkv_ref)

        dkv_ref[...] += jnp.concatenate(dks + dvs, axis=1)

    return pl.pallas_call(
        body, name=name, grid=(n_seq, nq),
        in_specs=[pl.BlockSpec((tq, MEM_W), lambda b, i: (b * nq + i, 3)),
                  pl.BlockSpec((N_MEM, 2 * MEM_W), lambda b, i: (b, 0)),
                  pl.BlockSpec((tq, MEM_W), lambda b, i: (b * nq + i, 3)),
                  pl.BlockSpec(memory_space=pl.ANY)],
        out_specs=[pl.BlockSpec((tq, MEM_W), lambda b, i: (b * nq + i, 3)),
                   pl.BlockSpec((N_MEM, 2 * MEM_W), lambda b, i: (b, 0))],
        out_shape=[_sds((M, D_MODEL), bf16), _sds((n_seq * N_MEM, 2 * MEM_W), f32)],
        input_output_aliases={3: 0},
        compiler_params=_params(("parallel", "arbitrary")),
    )(z, kvm, dyc, dz)


def rope_tables(pos, *, name):
    M = pos.shape[0]
    tm = min(1024, M)
    half = HEAD_DIM // 2
    inv = ROPE_THETA ** (-np.arange(half, dtype=np.float64) / half)
    inv128 = jnp.asarray(np.tile(inv, 4)[None, :], f32)
    sign128 = jnp.asarray(np.tile(np.concatenate([-np.ones(half), np.ones(half)]), 2)[None, :], f32)

    def body(p_ref, f_ref, s_ref, cos_ref, sin_ref):
        ang = p_ref[...] * f_ref[...]
        cos_ref[...] = jnp.cos(ang)
        sin_ref[...] = jnp.sin(ang) * s_ref[...]

    return pl.pallas_call(
        body, name=name, grid=(M // tm,),
        in_specs=[pl.BlockSpec((tm, 1), lambda i: (i, 0)),
                  pl.BlockSpec((1, 128), lambda i: (0, 0)),
                  pl.BlockSpec((1, 128), lambda i: (0, 0))],
        out_specs=[pl.BlockSpec((tm, 128), lambda i: (i, 0)),
                   pl.BlockSpec((tm, 128), lambda i: (i, 0))],
        out_shape=[_sds((M, 128), f32), _sds((M, 128), f32)],
        compiler_params=_params(("parallel",)),
    )(pos, inv128, sign128)


def _swap_halves(x):
    w = x.shape[1]
    first = (lax.broadcasted_iota(jnp.int32, (1, w), 1) % HEAD_DIM) < (HEAD_DIM // 2)
    return jnp.where(first, pltpu.roll(x, w - HEAD_DIM // 2, 1), pltpu.roll(x, HEAD_DIM // 2, 1))


def rope_fwd(src, cos, sin, *, name):
    M = src.shape[0]
    tm = min(512, M)

    def body(x_ref, c_ref, s_ref, o_ref):
        x = x_ref[...].astype(f32)
        c = jnp.tile(c_ref[...], (1, MAIN_W // 128))
        s = jnp.tile(s_ref[...], (1, MAIN_W // 128))
        o_ref[...] = x * c + _swap_halves(x) * s

    return pl.pallas_call(
        body, name=name, grid=(M // tm,),
        in_specs=[pl.BlockSpec((tm, MAIN_W), lambda i: (i, 0)),
                  pl.BlockSpec((tm, 128), lambda i: (i, 0)),
                  pl.BlockSpec((tm, 128), lambda i: (i, 0))],
        out_specs=pl.BlockSpec((tm, MAIN_W), lambda i: (i, 0)),
        out_shape=_sds((M, MAIN_W), f32),
        compiler_params=_params(("parallel",)),
    )(src, cos, sin)


def group_sum(groups, cos, sin, *, name, rotate, width, col_block=0, into=None):
    M = groups[0][0].shape[0]
    tm = min(512, M)
    counts = [len(g) for g in groups]
    flat = [a for g in groups for a in g]
    extra = [] if into is None else [into]

    def body(*refs):
        part_refs = refs[:len(flat)]
        c_ref, s_ref = refs[len(flat):len(flat) + 2]
        o_ref = refs[-1]
        cols, k = [], 0
        for n in counts:
            acc = part_refs[k][...]
            for r in part_refs[k + 1:k + n]:
                acc = acc + r[...]
            cols.append(acc)
            k += n
        d = jnp.concatenate(cols, axis=1)
        if rotate:
            c = jnp.tile(c_ref[...], (1, MAIN_W // 128))
            s = jnp.tile(s_ref[...], (1, MAIN_W // 128))
            d = d * c - _swap_halves(d) * s
        o_ref[...] = d.astype(bf16)

    part = pl.BlockSpec((tm, GROUP_W), lambda i: (i, 0))
    tab = pl.BlockSpec((tm, 128), lambda i: (i, 0))
    return pl.pallas_call(
        body, name=name, grid=(M // tm,),
        in_specs=[part] * len(flat) + [tab, tab] + [pl.BlockSpec(memory_space=pl.ANY)] * len(extra),
        out_specs=pl.BlockSpec((tm, MAIN_W), lambda i: (i, col_block)),
        out_shape=_sds((M, width), bf16),
        input_output_aliases={len(flat) + 2: 0} if extra else {},
        compiler_params=_params(("parallel",)),
    )(*flat, cos, sin, *extra)


PAIR_W = 2 * HEAD_DIM
MIN_BLOCKS = 4


def _dil_geometry(dil):
    nsub = max(dil, MIN_BLOCKS)
    tb = BAND * nsub
    return nsub, tb, SEQ // tb


def _rows(ref, sub, dil):
    if dil == 1:
        return ref[sub * BAND:(sub + 1) * BAND, :]
    return ref[pl.ds(sub, BAND, stride=dil), :]


def _store_rows(ref, sub, dil, val):
    if dil == 1:
        ref[sub * BAND:(sub + 1) * BAND, :] = val
    else:
        ref[pl.ds(sub, BAND, stride=dil), :] = val


def _keys(prev_ref, own_ref, sub, dil):
    if prev_ref is None:
        return _rows(own_ref, sub, dil)
    if dil == 1:
        if sub == 0:
            return jnp.concatenate([_rows(prev_ref, prev_ref.shape[0] // BAND - 1, 1), _rows(own_ref, 0, 1)], axis=0)
        return own_ref[(sub - 1) * BAND:(sub + 1) * BAND, :]
    return jnp.concatenate([_rows(prev_ref, sub, dil), _rows(own_ref, sub, dil)], axis=0)


def _band_mask(nkeys, has_prev):
    i = lax.broadcasted_iota(jnp.int32, (BAND, nkeys), 0)
    j = lax.broadcasted_iota(jnp.int32, (BAND, nkeys), 1)
    if nkeys == BAND:
        return j <= i
    return (j >= i) & (j <= i + BAND) & (has_prev | (j >= BAND))


def _first_head():
    return lax.broadcasted_iota(jnp.int32, (1, PAIR_W), 1) < HEAD_DIM


def _col(x, hh):
    return x[:, hh * HEAD_DIM:hh * HEAD_DIM + 1]


def _pair_spec(tb, nblk, col0, which):
    def idx(b, p, i):
        if which < 0:
            i = jnp.maximum(i - 1, 0)
        elif which > 0:
            i = jnp.minimum(i + 1, nblk - 1)
        return (b * nblk + i, col0 + p)
    return pl.BlockSpec((tb, PAIR_W), idx)


def dil_fwd(q, k, kv, g, dil, *, name, n_seq):
    M = q.shape[0]
    nsub, tb, nblk = _dil_geometry(dil)
    with_prev = nblk > 1

    def body(*refs):
        if with_prev:
            q_ref, ko_ref, vo_ref, kp_ref, vp_ref, o_ref, l_ref = refs
        else:
            (q_ref, ko_ref, vo_ref, o_ref, l_ref), kp_ref, vp_ref = refs, None, None
        first = _first_head()
        blk = pl.program_id(2)
        for sub in range(nsub):
            qs = _rows(q_ref, sub, dil) * SCALE
            kc = _keys(kp_ref, ko_ref, sub, dil).astype(bf16)
            vc = _keys(vp_ref, vo_ref, sub, dil).astype(bf16)
            has_prev = True if (dil == 1 and sub > 0) else blk > 0
            mask = _band_mask(kc.shape[0], has_prev)
            outs, lses = [], []
            for hh in range(2):
                qm = jnp.where(first if hh == 0 else ~first, qs, 0.0).astype(bf16)
                s = jnp.where(mask, lax.dot_general(qm, kc, NT, preferred_element_type=f32), NEG)
                m = jnp.max(s, axis=-1, keepdims=True)
                e = jnp.exp(s - m)
                l = jnp.sum(e, axis=-1, keepdims=True)
                outs.append(jnp.dot(e.astype(bf16), vc, preferred_element_type=f32) * (1.0 / l))
                lses.append(jnp.broadcast_to(m + jnp.log(l), (BAND, PAIR_W)))
            _store_rows(o_ref, sub, dil, jnp.where(first, outs[0], outs[1]))
            _store_rows(l_ref, sub, dil, jnp.where(first, lses[0], lses[1]))

    ins = [(q, 2 * g, 0), (k, 2 * g, 0), (kv, 6 + 2 * g, 0)]
    if with_prev:
        ins += [(k, 2 * g, -1), (kv, 6 + 2 * g, -1)]
    out = _pair_spec(tb, nblk, 0, 0)
    return pl.pallas_call(
        body, name=name, grid=(n_seq, 2, nblk),
        in_specs=[_pair_spec(tb, nblk, c, w) for _, c, w in ins],
        out_specs=[out, out],
        out_shape=[_sds((M, GROUP_W), f32)] * 2,
        compiler_params=_params(("parallel", "parallel", "arbitrary")),
    )(*[a for a, _, _ in ins])


def combine_fwd(os_, lses, *, name):
    M = os_[0].shape[0]
    tm = min(512, M)

    def body(o0, o1, o2, l0, l1, l2, y_ref):
        ls = [l0[...], l1[...], l2[...]]
        m = jnp.maximum(jnp.maximum(ls[0], ls[1]), ls[2])
        es = [jnp.exp(l - m) for l in ls]
        inv = 1.0 / (es[0] + es[1] + es[2])
        y_ref[...] = jnp.concatenate([o[...] * e * inv for o, e in zip((o0, o1, o2), es)], axis=1).astype(bf16)

    part = pl.BlockSpec((tm, GROUP_W), lambda i: (i, 0))
    return pl.pallas_call(
        body, name=name, grid=(M // tm,),
        in_specs=[part] * 6,
        out_specs=pl.BlockSpec((tm, MAIN_W), lambda i: (i, 0)),
        out_shape=_sds((M, D_MODEL), bf16),
        compiler_params=_params(("parallel",)),
    )(*os_, *lses)


def combine_bwd(dyc, os_, lses, *, name):
    M = os_[0].shape[0]
    tm = min(512, M)

    def body(dy_ref, o0, o1, o2, l0, l1, l2, d0, d1, d2, c0, c1, c2):
        r = lax.broadcasted_iota(jnp.int32, (GROUP_W, GROUP_W), 0) // HEAD_DIM
        c = lax.broadcasted_iota(jnp.int32, (GROUP_W, GROUP_W), 1) // HEAD_DIM
        ones = (r == c).astype(f32)
        dy = dy_ref[...].astype(f32)
        ls = [l0[...], l1[...], l2[...]]
        m = jnp.maximum(jnp.maximum(ls[0], ls[1]), ls[2])
        es = [jnp.exp(l - m) for l in ls]
        inv = 1.0 / (es[0] + es[1] + es[2])
        total = 0.0
        alphas = []
        for g, (o, e, d_ref) in enumerate(zip((o0, o1, o2), es, (d0, d1, d2))):
            a = e * inv
            dyg = dy[:, g * GROUP_W:(g + 1) * GROUP_W]
            d_ref[...] = dyg * a
            dsum = jnp.dot(dyg * o[...], ones, precision=lax.Precision.HIGHEST, preferred_element_type=f32)
            total = total + a * dsum
            alphas.append(a)
        for a, c_ref in zip(alphas, (c0, c1, c2)):
            c_ref[...] = -a * total

    part = pl.BlockSpec((tm, GROUP_W), lambda i: (i, 0))
    outs = pl.pallas_call(
        body, name=name, grid=(M // tm,),
        in_specs=[pl.BlockSpec((tm, MAIN_W), lambda i: (i, 0))] + [part] * 6,
        out_specs=[part] * 6,
        out_shape=[_sds((M, GROUP_W), f32)] * 6,
        compiler_params=_params(("parallel",)),
    )(dyc, *os_, *lses)
    return outs[:3], outs[3:]


def dil_bwd_dq(q, k, kv, do, cc, lse, g, dil, *, name, n_seq):
    M = q.shape[0]
    nsub, tb, nblk = _dil_geometry(dil)
    with_prev = nblk > 1

    def body(*refs):
        if with_prev:
            q_ref, ko_ref, vo_ref, do_ref, c_ref, l_ref, kp_ref, vp_ref, dq_ref = refs
        else:
            (q_ref, ko_ref, vo_ref, do_ref, c_ref, l_ref, dq_ref), kp_ref, vp_ref = refs, None, None
        first = _first_head()
        blk = pl.program_id(2)
        for sub in range(nsub):
            qs = _rows(q_ref, sub, dil) * SCALE
            dos = _rows(do_ref, sub, dil)
            cs = _rows(c_ref, sub, dil)
            ls = _rows(l_ref, sub, dil)
            kc = _keys(kp_ref, ko_ref, sub, dil).astype(bf16)
            vc = _keys(vp_ref, vo_ref, sub, dil).astype(bf16)
            has_prev = True if (dil == 1 and sub > 0) else blk > 0
            mask = _band_mask(kc.shape[0], has_prev)
            outs = []
            for hh in range(2):
                lm = first if hh == 0 else ~first
                qm = jnp.where(lm, qs, 0.0).astype(bf16)
                dom = jnp.where(lm, dos, 0.0).astype(bf16)
                s = jnp.where(mask, lax.dot_general(qm, kc, NT, preferred_element_type=f32), NEG)
                p = jnp.exp(s - _col(ls, hh))
                dp = lax.dot_general(dom, vc, NT, preferred_element_type=f32)
                ds = (p * (dp + _col(cs, hh))).astype(bf16)
                outs.append(jnp.dot(ds, kc, preferred_element_type=f32) * SCALE)
            _store_rows(dq_ref, sub, dil, jnp.where(first, outs[0], outs[1]))

    ins = [(q, 2 * g, 0), (k, 2 * g, 0), (kv, 6 + 2 * g, 0), (do, 0, 0), (cc, 0, 0), (lse, 0, 0)]
    if with_prev:
        ins += [(k, 2 * g, -1), (kv, 6 + 2 * g, -1)]
    return pl.pallas_call(
        body, name=name, grid=(n_seq, 2, nblk),
        in_specs=[_pair_spec(tb, nblk, c, w) for _, c, w in ins],
        out_specs=_pair_spec(tb, nblk, 0, 0),
        out_shape=_sds((M, GROUP_W), f32),
        compiler_params=_params(("parallel", "parallel", "arbitrary")),
    )(*[a for a, _, _ in ins])


def dil_bwd_dkv(q, k, kv, do, cc, lse, g, dil, *, name, n_seq):
    M = q.shape[0]
    nsub, tb, nblk = _dil_geometry(dil)
    with_next = nblk > 1

    def body(*refs):
        if with_next:
            (k_ref, v_ref, q_ref, do_ref, c_ref, l_ref, qn_ref, don_ref, cn_ref, ln_ref, dk_ref, dv_ref) = refs
        else:
            (k_ref, v_ref, q_ref, do_ref, c_ref, l_ref, dk_ref, dv_ref) = refs
        first = _first_head()
        blk = pl.program_id(2)
        i = lax.broadcasted_iota(jnp.int32, (BAND, BAND), 0)
        j = lax.broadcasted_iota(jnp.int32, (BAND, BAND), 1)
        for sub in range(nsub):
            ks = _rows(k_ref, sub, dil)
            vs = _rows(v_ref, sub, dil)
            sets = [((q_ref, do_ref, c_ref, l_ref), sub, j <= i)]
            if dil == 1 and sub + 1 < nsub:
                sets.append(((q_ref, do_ref, c_ref, l_ref), sub + 1, j >= i))
            elif with_next:
                nsubq = 0 if dil == 1 else sub
                sets.append(((qn_ref, don_ref, cn_ref, ln_ref), nsubq, (j >= i) & (blk + 1 < nblk)))
            loaded = [(tuple(_rows(r, sq, dil) for r in rs), mask) for rs, sq, mask in sets]
            dks, dvs = [], []
            for hh in range(2):
                lm = first if hh == 0 else ~first
                km = jnp.where(lm, ks, 0.0).astype(bf16)
                vm = jnp.where(lm, vs, 0.0).astype(bf16)
                dk = jnp.zeros((BAND, PAIR_W), f32)
                dv = jnp.zeros((BAND, PAIR_W), f32)
                for (qs, dos, cs, ls), mask in loaded:
                    qb = qs.astype(bf16)
                    dob = dos.astype(bf16)
                    s = lax.dot_general(qb, km, NT, preferred_element_type=f32) * SCALE
                    p = jnp.exp(jnp.where(mask, s, NEG) - _col(ls, hh))
                    dv = dv + lax.dot_general(p.astype(bf16), dob, TN, preferred_element_type=f32)
                    dp = lax.dot_general(dob, vm, NT, preferred_element_type=f32)
                    ds = (p * (dp + _col(cs, hh)) * SCALE).astype(bf16)
                    dk = dk + lax.dot_general(ds, qb, TN, preferred_element_type=f32)
                dks.append(dk)
                dvs.append(dv)
            _store_rows(dk_ref, sub, dil, jnp.where(first, dks[0], dks[1]))
            _store_rows(dv_ref, sub, dil, jnp.where(first, dvs[0], dvs[1]))

    ins = [(k, 2 * g, 0), (kv, 6 + 2 * g, 0), (q, 2 * g, 0), (do, 0, 0), (cc, 0, 0), (lse, 0, 0)]
    if with_next:
        ins += [(q, 2 * g, 1), (do, 0, 1), (cc, 0, 1), (lse, 0, 1)]
    out = _pair_spec(tb, nblk, 0, 0)
    return pl.pallas_call(
        body, name=name, grid=(n_seq, 2, nblk),
        in_specs=[_pair_spec(tb, nblk, c, w) for _, c, w in ins],
        out_specs=[out, out],
        out_shape=[_sds((M, GROUP_W), f32)] * 2,
        compiler_params=_params(("parallel", "parallel", "arbitrary")),
    )(*[a for a, _, _ in ins])


def _blockdiag(wp):
    out = jnp.zeros((MAIN_W, MAIN_W), wp.dtype)
    for gi in range(len(POOL_WINDOWS)):
        sl = slice(gi * POOL_GROUP, (gi + 1) * POOL_GROUP)
        out = out.at[sl, sl].set(wp[gi])
    return out


def _unblockdiag(w):
    return jnp.stack([w[gi * POOL_GROUP:(gi + 1) * POOL_GROUP, gi * POOL_GROUP:(gi + 1) * POOL_GROUP]
                      for gi in range(len(POOL_WINDOWS))])


def local_step(x, mem, positions, target, P, layer_weights, kv_weight, emit_grads):
    n_seq = x.shape[0]
    M = n_seq * SEQ
    xs = x.reshape(M, D_MODEL)
    mems = mem.reshape(n_seq * N_MEM, D_MODEL)
    pos = positions.reshape(M, 1).astype(f32)
    cos, sin = rope_tables(pos, name="rope_tables")
    gains = P["norm_gains"]

    def gain(l, k):
        return gains[l, k].reshape(1, D_MODEL)

    saved = []
    kvs = None
    for l in range(DEPTH):
        W, started = layer_weights(l, "mix", xs)
        sv = {"x": xs, "W": W}
        z, h1 = rms_matmul(xs, gain(l, 0), W["w_in"], name=f"l{l}_in", out_dtype=f32, after=started)
        kvm, mn = rms_matmul(mems, P["mem_norm"][l].reshape(1, D_MODEL), W["w_mem_kv"],
                             name=f"l{l}_memkv", out_dtype=bf16)
        sv.update(z=z, h1=h1, kvm=kvm, mn=mn)
        if l < N_A_LAYERS:
            wbd = _blockdiag(P["w_pool"][l].astype(bf16))
            psc = P["pool_scale"][l].reshape(1, MAIN_W)
            p, y_main = pool_fwd(z, wbd, psc, name=f"l{l}_pool")
            sv.update(p=p, wbd=wbd, psc=psc)
        else:
            qrot = rope_fwd(z, cos, sin, name=f"l{l}_ropeq")
            os_, lses = [], []
            for g, (_, dil) in enumerate(DIL_PATTERNS):
                o, lse = dil_fwd(qrot, kvs["krot"], kvs["kv"], g, dil, name=f"l{l}_dil{g}", n_seq=n_seq)
                os_.append(o)
                lses.append(lse)
            y_main = combine_fwd(os_, lses, name=f"l{l}_comb")
            sv.update(qrot=qrot, os=os_, lses=lses)
        ycat = memattn_fwd(z, kvm, y_main, name=f"l{l}_memattn", n_seq=n_seq)
        y, x1 = matmul_rms_res(ycat, W["w_out"], gain(l, 1), xs, name=f"l{l}_out")
        W.update(layer_weights(l, "ffn", x1)[0])
        fg, fu, a, h2 = rms_gate_up(x1, gain(l, 2), W["w_gate_up"], name=f"l{l}_gu")
        y2, x2 = matmul_rms_res(a, W["w_down"], gain(l, 3), x1, name=f"l{l}_down")
        sv.update(ycat=ycat, y=y, x1=x1, fg=fg, fu=fu, h2=h2, a=a, y2=y2)
        saved.append(sv)
        xs = x2
        if l == N_A_LAYERS - 1:
            w_kv = kv_weight(xs)
            kv, hkv = rms_matmul(xs, P["kv_norm"].reshape(1, D_MODEL), w_kv, name="kv_proj", out_dtype=f32,
                                 transposed=True)
            krot = rope_fwd(kv, cos, sin, name="ropek")
            kvs = {"kv": kv, "hkv": hkv, "krot": krot, "x": xs, "w_kv": w_kv}

    dx, sq = loss_head(xs, target.reshape(M, D_MODEL), name="loss_head")

    G = {"mem_norm": [None] * DEPTH, "norm_gains": [[None] * 4 for _ in range(DEPTH)],
         "pool_scale": [None] * N_A_LAYERS}
    dk_parts = [[] for _ in range(N_GROUPS)]
    dv_parts = [[] for _ in range(N_GROUPS)]
    emitted = None

    for l in reversed(range(DEPTH)):
        sv = saved[l]
        W = sv["W"]
        gw = {}
        dy2, dgu, G["norm_gains"][l][3] = down_bwd(sv["y2"], gain(l, 3), dx, W["w_down"], sv["fg"], sv["fu"],
                                                   name=f"l{l}_b_dgu", after=emitted)
        gw["w_down"] = matmul(sv["a"], dy2, TN, name=f"l{l}_b_wd", out_dtype=bf16)
        dx1, G["norm_gains"][l][2] = matmul_rms_bwd(dgu, W["w_gate_up"], NN, sv["x1"], gain(l, 2), dx,
                                                    name=f"l{l}_b_dh2")
        gw["w_gate_up"] = matmul(dgu, sv["h2"], TN, name=f"l{l}_b_wgu", out_dtype=bf16)
        emitted = emit_grads(l, "ffn", gw)
        gw = {}
        dy, dycat, G["norm_gains"][l][1] = rms_bwd_matmul(sv["y"], gain(l, 1), dx1, W["w_out"], NT,
                                                          name=f"l{l}_b_dycat", after=emitted)
        gw["w_out"] = matmul(sv["ycat"], dy, TN, name=f"l{l}_b_wout", out_dtype=bf16)
        if l < N_A_LAYERS:
            dz, dwbd, dps = pool_bwd(dycat, sv["p"], sv["wbd"], sv["psc"], name=f"l{l}_b_pool")
            gw["w_pool"] = _unblockdiag(dwbd).reshape(MAIN_W, POOL_GROUP).astype(bf16)
            G["pool_scale"][l] = dps.reshape(MAIN_W)
        else:
            dos, ccs = combine_bwd(dycat, sv["os"], sv["lses"], name=f"l{l}_b_comb")
            dqs = []
            for g, (_, dil) in enumerate(DIL_PATTERNS):
                args = (sv["qrot"], kvs["krot"], kvs["kv"], dos[g], ccs[g], sv["lses"][g], g, dil)
                dqs.append([dil_bwd_dq(*args, name=f"l{l}_b_dq{g}", n_seq=n_seq)])
                dk, dv = dil_bwd_dkv(*args, name=f"l{l}_b_dkv{g}", n_seq=n_seq)
                dk_parts[g].append(dk)
                dv_parts[g].append(dv)
            dz = group_sum(dqs, cos, sin, name=f"l{l}_b_ropeq", rotate=True, width=D_MODEL)
        dz, dkvm = memattn_bwd(sv["z"], sv["kvm"], dycat, dz, name=f"l{l}_b_memattn", n_seq=n_seq)
        dmn = matmul(dkvm, W["w_mem_kv"], NT, name=f"l{l}_b_dmn", out_dtype=bf16)
        gw["w_mem_kv"] = matmul(sv["mn"], dkvm, TN, name=f"l{l}_b_wmkv", out_dtype=bf16)
        _, G["mem_norm"][l] = rms_bwd(mems, P["mem_norm"][l].reshape(1, D_MODEL), dmn, None,
                                      name=f"l{l}_b_nmem", out_dtype=bf16)
        gw["w_in"] = matmul(sv["h1"], dz, TN, name=f"l{l}_b_win", out_dtype=bf16)
        dx, G["norm_gains"][l][0] = matmul_rms_bwd(dz, W["w_in"], NT, sv["x"], gain(l, 0), dx1, name=f"l{l}_b_dh1")
        if l == N_A_LAYERS:
            dkv = group_sum(dk_parts, cos, sin, name="b_ropek", rotate=True, width=2 * MAIN_W)
            dkv = group_sum(dv_parts, cos, sin, name="b_sumv", rotate=False, width=2 * MAIN_W, col_block=1, into=dkv)
            gw["w_kv"] = matmul(dkv, kvs["hkv"], TN, name="b_wkv", out_dtype=bf16)
            dx, gkn = matmul_rms_bwd(dkv, kvs["w_kv"], NN, kvs["x"], P["kv_norm"].reshape(1, D_MODEL), dx,
                                     name="b_dhkv")
            G["kv_norm"] = gkn.reshape(D_MODEL)
        emitted = emit_grads(l, "mix", gw)

    small = {"pool_scale": jnp.stack(G["pool_scale"]),
             "mem_norm": jnp.concatenate(G["mem_norm"], axis=0),
             "norm_gains": jnp.stack([jnp.concatenate(r, axis=0) for r in G["norm_gains"]]),
             "kv_norm": G["kv_norm"]}
    return sq[0, 0], dx.reshape(n_seq, SEQ, D_MODEL), small, emitted


def _peer(k):
    x, y, c = lax.axis_index("x"), lax.axis_index("y"), lax.axis_index("c")
    px = 1 - x if k & 4 else x
    py = 1 - y if k & 2 else y
    pc = 1 - c if k & 1 else c
    return (px, py, pc), 4 * px + 2 * py + pc


def _my_index():
    return 4 * lax.axis_index("x") + 2 * lax.axis_index("y") + lax.axis_index("c")


def _src_for(kinds, in_refs, i, idx):
    return in_refs[i] if kinds[i] == "gather" else in_refs[i].at[idx]


def _local_copies(kinds, in_refs, out_refs, local_sems):
    me = _my_index()
    return [pltpu.make_async_copy(_src_for(kinds, in_refs, i, me), out_refs[i].at[me], local_sems.at[i])
            for i in range(len(kinds))]


def _remote_copies(kinds, in_refs, out_refs, send_sems, recv_sems, *, arriving):
    me = _my_index()
    copies = []
    for k in range(1, N_DEV):
        dev, idx = _peer(k)
        for i in range(len(kinds)):
            j = i * (N_DEV - 1) + k - 1
            copies.append(pltpu.make_async_remote_copy(
                src_ref=_src_for(kinds, in_refs, i, idx), dst_ref=out_refs[i].at[idx if arriving else me],
                send_sem=send_sems.at[j], recv_sem=recv_sems.at[j], device_id=dev, device_id_type=MESH))
    return copies


def _out_shape(a, kind):
    return ((N_DEV,) + a.shape) if kind == "gather" else a.shape


def exchange(items, *, name, after=()):
    n = len(items)
    kinds = [k for _, k in items]
    after = list(after)

    def body(*refs):
        in_refs, out_refs = refs[:n], refs[n + len(after):2 * n + len(after)]
        send_sems, recv_sems, local_sems = refs[-3:]
        local = _local_copies(kinds, in_refs, out_refs, local_sems)
        sends = _remote_copies(kinds, in_refs, out_refs, send_sems, recv_sems, arriving=False)
        for cp in local + sends:
            cp.start()
        for cp in _remote_copies(kinds, in_refs, out_refs, send_sems, recv_sems, arriving=True):
            cp.wait_recv()
        for cp in sends:
            cp.wait_send()
        for cp in local:
            cp.wait()

    any_spec = pl.BlockSpec(memory_space=pl.ANY)
    return pl.pallas_call(
        body, name=name,
        in_specs=[any_spec] * (n + len(after)), out_specs=[any_spec] * n,
        out_shape=[_sds(_out_shape(a, k), a.dtype) for a, k in items],
        scratch_shapes=[pltpu.SemaphoreType.DMA((n * (N_DEV - 1),)), pltpu.SemaphoreType.DMA((n * (N_DEV - 1),)),
                        pltpu.SemaphoreType.DMA((n,))],
    )(*[a for a, _ in items], *after)


_HBM = pl.BlockSpec(memory_space=pltpu.HBM)
_SEM = pl.BlockSpec(memory_space=pltpu.SEMAPHORE)
_EFFECT = pltpu.SideEffectType.DATAFLOW_SIDE_EFFECTING


def exchange_start(items, after, *, name):
    n = len(items)
    kinds = [k for _, k in items]

    def body(*refs):
        in_refs, land_refs = refs[:n], refs[n:2 * n]
        send_sems, recv_sems, local_sems = refs[2 * n + 1:2 * n + 4]
        token = refs[-1]
        for cp in (_local_copies(kinds, in_refs, land_refs, local_sems)
                   + _remote_copies(kinds, in_refs, land_refs, send_sems, recv_sems, arriving=False)):
            cp.start()
        token[...] = jnp.zeros_like(token)

    srcs = [pltpu.with_memory_space_constraint(a, pltpu.HBM) for a, _ in items]
    lands = [pltpu.with_memory_space_constraint(lax.empty(_out_shape(a, k), a.dtype), pltpu.HBM) for a, k in items]
    outs = pl.pallas_call(
        body, name=name,
        out_shape=(pltpu.SemaphoreType.DMA((n * (N_DEV - 1),)), pltpu.SemaphoreType.DMA((n * (N_DEV - 1),)),
                   pltpu.SemaphoreType.DMA((n,)),
                   *[pltpu.HBM(a.shape, a.dtype) for a in srcs], *[pltpu.HBM(a.shape, a.dtype) for a in lands],
                   _sds((8, 128), f32)),
        in_specs=[_HBM] * (2 * n) + [pl.BlockSpec(memory_space=pl.ANY)],
        out_specs=(_SEM, _SEM, _SEM, *[_HBM] * (2 * n), pl.BlockSpec(memory_space=pltpu.VMEM)),
        input_output_aliases={i: 3 + i for i in range(2 * n)},
        compiler_params=pltpu.CompilerParams(has_side_effects=_EFFECT),
    )(*srcs, *lands, after)
    return {"kinds": kinds, "sems": outs[:3], "srcs": outs[3:3 + n], "lands": outs[3 + n:3 + 2 * n], "token": outs[-1]}


def exchange_wait(handle, after, *, name):
    kinds = handle["kinds"]
    n = len(kinds)

    def body(*refs):
        in_refs, land_refs = refs[:n], refs[n:2 * n]
        send_sems, recv_sems, local_sems = refs[2 * n:2 * n + 3]
        for cp in _remote_copies(kinds, in_refs, land_refs, send_sems, recv_sems, arriving=True):
            cp.wait_recv()
        for cp in _remote_copies(kinds, in_refs, land_refs, send_sems, recv_sems, arriving=False):
            cp.wait_send()
        for cp in _local_copies(kinds, in_refs, land_refs, local_sems):
            cp.wait()

    srcs, lands = list(handle["srcs"]), list(handle["lands"])
    after = list(after) if isinstance(after, (list, tuple)) else [after]
    outs = pl.pallas_call(
        body, name=name,
        out_shape=tuple(pltpu.HBM(a.shape, a.dtype) for a in srcs + lands),
        in_specs=[_HBM] * (2 * n) + [_SEM] * 3 + [pl.BlockSpec(memory_space=pl.ANY)] * len(after),
        out_specs=tuple([_HBM] * (2 * n)),
        input_output_aliases={i: i for i in range(2 * n)},
        compiler_params=pltpu.CompilerParams(has_side_effects=_EFFECT),
    )(*srcs, *lands, *handle["sems"], *after)
    return list(outs[n:])


def adamw(slots, w, m, v, *, name, layer=None, into=None):
    R, C = w.shape[-2:]
    tr = _tile(R, (256, 128, 64, 32, 16, 8))
    c1 = 1.0 - ADAM_B1 ** ADAM_STEP
    c2 = 1.0 - ADAM_B2 ** ADAM_STEP
    extra = [] if into is None else list(into)

    def body(s_ref, w_ref, m_ref, v_ref, *refs):
        g_ref, d_ref, m2_ref, v2_ref = refs[len(extra):]
        g = s_ref[0].astype(f32)
        for d in range(1, N_DEV):
            g = g + s_ref[d].astype(f32)
        m2 = ADAM_B1 * m_ref[...] + (1.0 - ADAM_B1) * g
        v2 = ADAM_B2 * v_ref[...] + (1.0 - ADAM_B2) * (g * g)
        g_ref[...] = g
        m2_ref[...] = m2
        v2_ref[...] = v2
        d_ref[...] = -ADAM_LR * ((m2 / c1) / (jnp.sqrt(v2 / c2) + ADAM_EPS) + ADAM_WD * w_ref[...])

    if layer is None:
        blk = pl.BlockSpec((tr, C), lambda i: (i, 0))
    else:
        blk = pl.BlockSpec((None, tr, C), lambda i: (layer, i, 0))
    return pl.pallas_call(
        body, name=name, grid=(R // tr,),
        in_specs=[pl.BlockSpec((N_DEV, tr, C), lambda i: (0, i, 0)), blk, blk, blk]
        + [pl.BlockSpec(memory_space=pl.ANY)] * len(extra),
        out_specs=[blk] * 4,
        out_shape=[_sds(w.shape, f32)] * 4,
        input_output_aliases={4 + j: j for j in range(len(extra))},
        compiler_params=_params(("parallel",)),
    )(slots, w, m, v, *extra)


WEIGHTS = ("norm_gains", "mem_norm", "w_in", "w_mem_kv", "w_out", "w_pool", "pool_scale", "kv_norm", "w_kv",
           "w_gate_up", "w_down")
LAYER_MATS = ("w_in", "w_mem_kv", "w_out", "w_gate_up", "w_down")
POOL_SHARD = MAIN_W // N_DEV
KV_SHARD = 2 * MAIN_W // N_DEV
LOOKAHEAD = 1


def _pack_small(gains, pscale):
    lead = gains.shape[:-3]
    g = gains.reshape(lead + (16, 128))
    p = jnp.zeros(lead + (8, 128), f32).at[..., :2, :POOL_SHARD].set(pscale)
    return jnp.concatenate([g, p], axis=-2)


def _unpack_small(a):
    return a[:16].reshape(4, 4, 128), a[16:18, :POOL_SHARD]


def _pack_repl(mem_norm, kv_norm):
    return jnp.concatenate([mem_norm, kv_norm.reshape(1, D_MODEL), jnp.zeros((3, D_MODEL), f32)], axis=0)


def _unpack_repl(a):
    return a[:4], a[4]


def kernel(x, mem, positions, norm_gains, mem_norm, w_in, w_mem_kv, w_out, w_pool, pool_scale, kv_norm, w_kv, w_gate_up, w_down, loss_target, m_norm_gains, m_mem_norm, m_w_in, m_w_mem_kv, m_w_out, m_w_pool, m_pool_scale, m_kv_norm, m_w_kv, m_w_gate_up, m_w_down, v_norm_gains, v_mem_norm, v_w_in, v_w_mem_kv, v_w_out, v_w_pool, v_pool_scale, v_kv_norm, v_w_kv, v_w_gate_up, v_w_down):
    w = dict(norm_gains=norm_gains, mem_norm=mem_norm, w_in=w_in, w_mem_kv=w_mem_kv, w_out=w_out, w_pool=w_pool,
             pool_scale=pool_scale, kv_norm=kv_norm, w_kv=w_kv, w_gate_up=w_gate_up, w_down=w_down)
    m = dict(norm_gains=m_norm_gains, mem_norm=m_mem_norm, w_in=m_w_in, w_mem_kv=m_w_mem_kv, w_out=m_w_out,
             w_pool=m_w_pool, pool_scale=m_pool_scale, kv_norm=m_kv_norm, w_kv=m_w_kv, w_gate_up=m_w_gate_up,
             w_down=m_w_down)
    v = dict(norm_gains=v_norm_gains, mem_norm=v_mem_norm, w_in=v_w_in, w_mem_kv=v_w_mem_kv, w_out=v_w_out,
             w_pool=v_w_pool, pool_scale=v_pool_scale, kv_norm=v_kv_norm, w_kv=v_w_kv, w_gate_up=v_w_gate_up,
             w_down=v_w_down)

    def transposed_view(d):
        d = dict(d)
        d["w_gate_up"] = jnp.swapaxes(d["w_gate_up"], 1, 2)
        d["w_kv"] = jnp.swapaxes(d["w_kv"], 0, 1)
        return d

    wv, mv, vv = transposed_view(w), transposed_view(m), transposed_view(v)

    small = _pack_small(norm_gains, pool_scale)
    (gsmall,) = exchange([(small, "gather")], name="gather_small")
    P = {"norm_gains": jnp.moveaxis(gsmall[:, :16].reshape(N_DEV, 4, 4, 128), 0, 2).reshape(4, 4, D_MODEL),
         "pool_scale": jnp.moveaxis(gsmall[:, 16:18, :POOL_SHARD], 0, 1).reshape(2, MAIN_W),
         "mem_norm": mem_norm, "kv_norm": kv_norm, "w_pool": w_pool}

    PARTS = {"mix": ("w_in", "w_mem_kv", "w_out"), "ffn": ("w_gate_up", "w_down")}

    def part_items(l, part):
        items = [(wv[k][l].astype(bf16), "gather") for k in PARTS[part]]
        if part == "ffn" and l == N_A_LAYERS - 1:
            items.append((wv["w_kv"].astype(bf16), "gather"))
        return items

    handles = {}

    def start_layer(l, after):
        for part in ("mix", "ffn"):
            handles[l, part] = exchange_start(part_items(l, part), after, name=f"gather_start_{part}_l{l}")
            after = handles[l, part]["token"]
        return after

    token = gsmall
    for l in range(LOOKAHEAD):
        token = start_layer(l, token)
    landed = {}

    def layer_weights(l, part, after):
        first = l == 0 and part == "mix"
        got = exchange_wait(handles[l, part], token if first else after, name=f"gather_wait_{part}_l{l}")
        landed[l, part] = got
        started = None
        if part == "mix" and l + LOOKAHEAD < DEPTH:
            started = start_layer(l + LOOKAHEAD, got[0])
        W = {k: g.reshape(-1, g.shape[-1]) for k, g in zip(PARTS[part], got)}
        return W, started

    def kv_weight(after):
        g = landed[N_A_LAYERS - 1, "ffn"][len(PARTS["ffn"])]
        return g.reshape(2 * MAIN_W, D_MODEL)

    ghandles = {}

    def emit_grads(l, part, gw):
        items = [(gw[k].reshape((N_DEV, -1) + gw[k].shape[-1:]), "scatter") for k in PARTS[part]]
        if part == "mix" and l == N_A_LAYERS:
            items.append((gw["w_kv"].reshape(N_DEV, KV_SHARD, D_MODEL), "scatter"))
        if part == "mix" and l < N_A_LAYERS:
            items.append((gw["w_pool"], "gather"))
        ghandles[l, part] = exchange_start(items, items[0][0], name=f"scatter_start_{part}_l{l}")
        return ghandles[l, part]["token"]

    sq, grad_x, GS, emitted = local_step(x, mem, positions, loss_target, P, layer_weights, kv_weight, emit_grads)
    loss = lax.psum(0.5 * sq / D_MODEL, ("x", "y", "c"))

    def pool3(a):
        return a.reshape(N_A_LAYERS, MAIN_W, POOL_GROUP)

    out = {}
    after = [emitted]

    def finish_layer(l, after):
        for part in ("ffn", "mix"):
            got = exchange_wait(ghandles[l, part], after, name=f"scatter_wait_{part}_l{l}")
            after = []
            for k, slots in zip(PARTS[part], got):
                out[k] = adamw(slots, wv[k], mv[k], vv[k], name=f"adamw_{k}_l{l}", layer=l, into=out.get(k))
                after.append(out[k][0])
            if part == "mix" and l == N_A_LAYERS:
                out["w_kv"] = adamw(got[-1], wv["w_kv"], mv["w_kv"], vv["w_kv"], name="adamw_w_kv")
                after.append(out["w_kv"][0])
            if part == "mix" and l < N_A_LAYERS:
                out["w_pool"] = adamw(got[-1], pool3(w_pool), pool3(m_w_pool), pool3(v_w_pool), name=f"adamw_w_pool_l{l}",
                                      layer=l, into=out.get("w_pool"))
                after.append(out["w_pool"][0])
        return after

    for l in reversed(range(1, DEPTH)):
        after = finish_layer(l, after)

    gs = _pack_small(jnp.moveaxis(GS["norm_gains"].reshape(4, 4, N_DEV, 128), 2, 0),
                     jnp.moveaxis(GS["pool_scale"].reshape(2, N_DEV, POOL_SHARD), 1, 0))
    parts_small, parts_repl = exchange(
        [(gs, "scatter"), (_pack_repl(GS["mem_norm"], GS["kv_norm"]), "gather")],
        name="exchange_small_grads", after=after)
    finish_layer(0, [parts_small])
    out["w_gate_up"] = [jnp.swapaxes(r, 1, 2) for r in out["w_gate_up"]]
    out["w_kv"] = [jnp.swapaxes(r, 0, 1) for r in out["w_kv"]]
    out["w_pool"] = [r.reshape(w_pool.shape) for r in out["w_pool"]]

    res = adamw(parts_small, small, _pack_small(m_norm_gains, m_pool_scale), _pack_small(v_norm_gains, v_pool_scale),
                name="adamw_small")
    out["norm_gains"], out["pool_scale"] = zip(*[_unpack_small(r) for r in res])
    res = adamw(parts_repl, _pack_repl(mem_norm, kv_norm), _pack_repl(m_mem_norm, m_kv_norm),
                _pack_repl(v_mem_norm, v_kv_norm), name="adamw_repl")
    out["mem_norm"], out["kv_norm"] = zip(*[_unpack_repl(r) for r in res])

    return (loss, grad_x, *[out[k][0] for k in WEIGHTS], *[out[k][1] for k in WEIGHTS],
            *[out[k][2] for k in WEIGHTS], *[out[k][3] for k in WEIGHTS])
```

```python
import numpy as np
import jax
import jax.numpy as jnp
from jax import lax
from jax.experimental import pallas as pl
from jax.experimental.pallas import tpu as pltpu

f32 = jnp.float32
bf16 = jnp.bfloat16

D_MODEL = 1024
SEQ = 2048
DEPTH = 4
N_MEM = 256
HEAD_DIM = 64
N_MEM_HEADS = 4
MEM_W = 256
MAIN_W = 768
POOL_WINDOWS = (2, 4, 8, 16)
POOL_GROUP = 192
POOL_HALO = 16
DIL_PATTERNS = ((128, 1), (512, 4), (2048, 16))
N_GROUPS = 3
GROUP_W = 256
BAND = 128
N_A_LAYERS = 2
D_FF = 2816
ROPE_THETA = 10000.0
EPS = 1e-6
NEG = -1e30
SCALE = HEAD_DIM ** -0.5
N_DEV = 8

ADAM_LR = 0.001
ADAM_B1 = 0.9
ADAM_B2 = 0.999
ADAM_EPS = 1e-08
ADAM_WD = 0.01
ADAM_STEP = 10

VMEM_LIMIT_BYTES = 56 * 1024 * 1024
MESH = pl.DeviceIdType.MESH

NN = (((1,), (0,)), ((), ()))
NT = (((1,), (1,)), ((), ()))
TN = (((0,), (0,)), ((), ()))


def _params(sem=None):
    return pltpu.CompilerParams(dimension_semantics=sem, vmem_limit_bytes=VMEM_LIMIT_BYTES)


def _tile(n, cands):
    for c in cands:
        if n % c == 0:
            return c
    return n


def _sds(shape, dtype):
    return jax.ShapeDtypeStruct(tuple(shape), dtype)


def _rms_r(v):
    return lax.rsqrt(jnp.mean(v * v, axis=-1, keepdims=True) + EPS)


def rms_matmul(x, gain, w, *, name, out_dtype, transposed=False, after=None):
    M, K = x.shape
    N = w.shape[0] if transposed else w.shape[1]
    tm = min(1024, M)
    tn = _tile(N, (512, 256, 128))
    order = [] if after is None else [after]

    def body(x_ref, g_ref, w_ref, *refs):
        z_ref, h_ref = refs[len(order):]

        @pl.when(pl.program_id(1) == 0)
        def _():
            xv = x_ref[...]
            h_ref[...] = (xv * _rms_r(xv) * g_ref[...]).astype(bf16)

        z_ref[...] = lax.dot_general(h_ref[...], w_ref[...], NT if transposed else NN,
                                     preferred_element_type=f32).astype(z_ref.dtype)

    w_spec = pl.BlockSpec((tn, K), lambda i, j: (j, 0)) if transposed else pl.BlockSpec((K, tn), lambda i, j: (0, j))
    return pl.pallas_call(
        body, name=name, grid=(M // tm, N // tn),
        in_specs=[pl.BlockSpec((tm, K), lambda i, j: (i, 0)),
                  pl.BlockSpec((1, K), lambda i, j: (0, 0)),
                  w_spec] + [pl.BlockSpec(memory_space=pl.ANY)] * len(order),
        out_specs=[pl.BlockSpec((tm, tn), lambda i, j: (i, j)), pl.BlockSpec((tm, K), lambda i, j: (i, 0))],
        out_shape=[_sds((M, N), out_dtype), _sds((M, K), bf16)],
        compiler_params=_params(("parallel", "arbitrary")),
    )(x, gain, w, *order)


def matmul_rms_res(a, w, gain, res, *, name):
    M, K = a.shape
    N = w.shape[1]
    tm = min(512, M)

    def body(a_ref, w_ref, g_ref, r_ref, y_ref, x_ref):
        y = jnp.dot(a_ref[...], w_ref[...], preferred_element_type=f32)
        y_ref[...] = y.astype(bf16)
        x_ref[...] = r_ref[...] + y * _rms_r(y) * g_ref[...]

    row = pl.BlockSpec((tm, N), lambda i: (i, 0))
    return pl.pallas_call(
        body, name=name, grid=(M // tm,),
        in_specs=[pl.BlockSpec((tm, K), lambda i: (i, 0)),
                  pl.BlockSpec((K, N), lambda i: (0, 0)),
                  pl.BlockSpec((1, N), lambda i: (0, 0)),
                  row],
        out_specs=[row, row],
        out_shape=[_sds((M, N), bf16), _sds((M, N), f32)],
        compiler_params=_params(("parallel",)),
    )(a, w, gain, res)


def matmul(a, b, dims, *, name, out_dtype):
    if dims is TN:
        K, M = a.shape
        tm = _tile(M, (512, 256, 128))
        a_spec = pl.BlockSpec((K, tm), lambda i: (0, i))
    else:
        M, K = a.shape
        tm = _tile(M, (512, 256, 128))
        a_spec = pl.BlockSpec((tm, K), lambda i: (i, 0))
    N = b.shape[0] if dims is NT else b.shape[1]

    def body(a_ref, b_ref, o_ref):
        o_ref[...] = lax.dot_general(a_ref[...].astype(bf16), b_ref[...].astype(bf16), dims,
                                     preferred_element_type=f32).astype(o_ref.dtype)

    return pl.pallas_call(
        body, name=name, grid=(M // tm,),
        in_specs=[a_spec, pl.BlockSpec(b.shape, lambda i: (0, 0))],
        out_specs=pl.BlockSpec((tm, N), lambda i: (i, 0)),
        out_shape=_sds((M, N), out_dtype),
        compiler_params=_params(("parallel",)),
    )(a, b)


def rms_gate_up(x, gain, wt, *, name):
    M, K = x.shape
    tm = min(1024, M)
    tn = _tile(D_FF, (256, 128))
    nj = D_FF // tn

    def body(x_ref, gn_ref, wg_ref, wu_ref, g_ref, u_ref, a_ref, h_ref):
        @pl.when(pl.program_id(1) == 0)
        def _():
            xv = x_ref[...]
            h_ref[...] = (xv * _rms_r(xv) * gn_ref[...]).astype(bf16)

        h = h_ref[...]
        g = lax.dot_general(h, wg_ref[...], NT, preferred_element_type=f32)
        u = lax.dot_general(h, wu_ref[...], NT, preferred_element_type=f32)
        g_ref[...] = g.astype(bf16)
        u_ref[...] = u.astype(bf16)
        a_ref[...] = (g * (1.0 / (1.0 + jnp.exp(-g))) * u).astype(bf16)

    col = pl.BlockSpec((tm, tn), lambda i, j: (i, j))
    return pl.pallas_call(
        body, name=name, grid=(M // tm, nj),
        in_specs=[pl.BlockSpec((tm, K), lambda i, j: (i, 0)),
                  pl.BlockSpec((1, K), lambda i, j: (0, 0)),
                  pl.BlockSpec((tn, K), lambda i, j: (j, 0)),
                  pl.BlockSpec((tn, K), lambda i, j: (j + nj, 0))],
        out_specs=[col, col, col, pl.BlockSpec((tm, K), lambda i, j: (i, 0))],
        out_shape=[_sds((M, D_FF), bf16)] * 3 + [_sds((M, K), bf16)],
        compiler_params=_params(("parallel", "arbitrary")),
    )(x, gain, wt, wt)


def _rms_bwd_math(yv, gain, dn):
    r = _rms_r(yv)
    q = dn * gain
    dy = r * q - yv * (r * r * r) * jnp.mean(q * yv, axis=-1, keepdims=True)
    return dy, jnp.sum(dn * yv * r, axis=0, keepdims=True)


def _accumulate(ref, val):
    @pl.when(pl.program_id(0) == 0)
    def _():
        ref[...] = jnp.zeros_like(ref)

    ref[...] += val


def down_bwd(y, gain, dn, w_down, g, u, *, name, after=None):
    M, K = y.shape
    tm = min(512, M)
    order = [] if after is None else [after]

    def body(y_ref, gn_ref, dn_ref, w_ref, g_ref, u_ref, *refs):
        dy_ref, o_ref, dg_ref = refs[len(order):]
        dy, dgain = _rms_bwd_math(y_ref[...].astype(f32), gn_ref[...], dn_ref[...])
        dy = dy.astype(bf16)
        dy_ref[...] = dy
        _accumulate(dg_ref, dgain)
        da = lax.dot_general(dy, w_ref[...], NT, preferred_element_type=f32)
        g = g_ref[...].astype(f32)
        u = u_ref[...].astype(f32)
        s = 1.0 / (1.0 + jnp.exp(-g))
        o_ref[:, :D_FF] = (da * u * s * (1.0 + g * (1.0 - s))).astype(bf16)
        o_ref[:, D_FF:] = (da * g * s).astype(bf16)

    row = pl.BlockSpec((tm, K), lambda i: (i, 0))
    vec = pl.BlockSpec((1, K), lambda i: (0, 0))
    wide = pl.BlockSpec((tm, D_FF), lambda i: (i, 0))
    return pl.pallas_call(
        body, name=name, grid=(M // tm,),
        in_specs=[row, vec, row, pl.BlockSpec((D_FF, K), lambda i: (0, 0)), wide, wide]
        + [pl.BlockSpec(memory_space=pl.ANY)] * len(order),
        out_specs=[row, pl.BlockSpec((tm, 2 * D_FF), lambda i: (i, 0)), vec],
        out_shape=[_sds((M, K), bf16), _sds((M, 2 * D_FF), bf16), _sds((1, K), f32)],
        compiler_params=_params(("arbitrary",)),
    )(y, gain, dn, w_down, g, u, *order)


def rms_bwd_matmul(y, gain, dn, w, dims, *, name, after=None):
    M, K = y.shape
    N = w.shape[0] if dims is NT else w.shape[1]
    tm = min(512, M)
    order = [] if after is None else [after]

    def body(y_ref, gn_ref, dn_ref, w_ref, *refs):
        dy_ref, o_ref, dg_ref = refs[len(order):]
        dy, dgain = _rms_bwd_math(y_ref[...].astype(f32), gn_ref[...], dn_ref[...].astype(f32))
        dy = dy.astype(bf16)
        dy_ref[...] = dy
        _accumulate(dg_ref, dgain)
        o_ref[...] = lax.dot_general(dy, w_ref[...], dims, preferred_element_type=f32).astype(bf16)

    row = pl.BlockSpec((tm, K), lambda i: (i, 0))
    vec = pl.BlockSpec((1, K), lambda i: (0, 0))
    return pl.pallas_call(
        body, name=name, grid=(M // tm,),
        in_specs=[row, vec, row, pl.BlockSpec(w.shape, lambda i: (0, 0))]
        + [pl.BlockSpec(memory_space=pl.ANY)] * len(order),
        out_specs=[row, pl.BlockSpec((tm, N), lambda i: (i, 0)), vec],
        out_shape=[_sds((M, K), bf16), _sds((M, N), bf16), _sds((1, K), f32)],
        compiler_params=_params(("arbitrary",)),
    )(y, gain, dn, w, *order)


def matmul_rms_bwd(a, b, dims, y, gain, res, *, name):
    M, K = a.shape
    N = y.shape[1]
    tm = 256

    def body(a_ref, b_ref, y_ref, gn_ref, r_ref, dx_ref, dg_ref):
        dn = lax.dot_general(a_ref[...], b_ref[...], dims, preferred_element_type=f32)
        dy, dgain = _rms_bwd_math(y_ref[...], gn_ref[...], dn)
        dx_ref[...] = dy + r_ref[...]
        _accumulate(dg_ref, dgain)

    row = pl.BlockSpec((tm, N), lambda i: (i, 0))
    vec = pl.BlockSpec((1, N), lambda i: (0, 0))
    return pl.pallas_call(
        body, name=name, grid=(M // tm,),
        in_specs=[pl.BlockSpec((tm, K), lambda i: (i, 0)), pl.BlockSpec(b.shape, lambda i: (0, 0)), row, vec, row],
        out_specs=[row, vec],
        out_shape=[_sds((M, N), f32), _sds((1, N), f32)],
        compiler_params=_params(("arbitrary",)),
    )(a, b, y, gain, res)


def rms_bwd(y, gain, dn, res, *, name, out_dtype, after=None):
    M, N = y.shape
    tm = min(512, M)
    has_res = res is not None
    order = [] if after is None else [after]

    def body(*refs):
        y_ref, g_ref, dn_ref = refs[:3]
        r_ref = refs[3] if has_res else None
        dy_ref, dg_ref = refs[-2:]
        dy, dgain = _rms_bwd_math(y_ref[...].astype(f32), g_ref[...], dn_ref[...].astype(f32))
        if has_res:
            dy = dy + r_ref[...]
        dy_ref[...] = dy.astype(dy_ref.dtype)
        _accumulate(dg_ref, dgain)

    row = pl.BlockSpec((tm, N), lambda i: (i, 0))
    vec = pl.BlockSpec((1, N), lambda i: (0, 0))
    args = [y, gain, dn] + ([res] if has_res else []) + order
    return pl.pallas_call(
        body, name=name, grid=(M // tm,),
        in_specs=[row, vec, row] + ([row] if has_res else []) + [pl.BlockSpec(memory_space=pl.ANY)] * len(order),
        out_specs=[row, vec],
        out_shape=[_sds((M, N), out_dtype), _sds((1, N), f32)],
        compiler_params=_params(("arbitrary",)),
    )(*args)


def loss_head(x, target, *, name):
    M, N = x.shape
    tm = min(512, M)

    def body(x_ref, t_ref, dx_ref, l_ref):
        e = x_ref[...] - t_ref[...]
        dx_ref[...] = e * (1.0 / N)

        @pl.when(pl.program_id(0) == 0)
        def _():
            l_ref[...] = jnp.zeros_like(l_ref)

        l_ref[...] += jnp.sum(jnp.sum(e * e, axis=0, keepdims=True), axis=1, keepdims=True)

    row = pl.BlockSpec((tm, N), lambda i: (i, 0))
    return pl.pallas_call(
        body, name=name, grid=(M // tm,),
        in_specs=[row, row],
        out_specs=[row, pl.BlockSpec((8, 128), lambda i: (0, 0))],
        out_shape=[_sds((M, N), f32), _sds((8, 128), f32)],
        compiler_params=_params(("arbitrary",)),
    )(x, target)


def _pool_select(a1, a2, a3, a4):
    col = lax.broadcasted_iota(jnp.int32, (1, MAIN_W), 1) // POOL_GROUP
    return jnp.where(col == 0, a1, jnp.where(col == 1, a2, jnp.where(col == 2, a3, a4)))


def _pool_count(t):
    col = lax.broadcasted_iota(jnp.int32, (1, MAIN_W), 1) // POOL_GROUP
    win = jnp.where(col == 0, 2, jnp.where(col == 1, 4, jnp.where(col == 2, 8, 16)))
    return jnp.minimum(t + 1, win).astype(f32)


def pool_fwd(z, wbd, scale, *, name):
    M = z.shape[0]
    tm = 256
    nper = SEQ // tm
    hb = tm // POOL_HALO

    def body(zc_ref, zh_ref, w_ref, s_ref, p_ref, y_ref):
        i = pl.program_id(0)
        seq_blk = i % nper
        halo = jnp.where(seq_blk == 0, 0.0, zh_ref[...])
        u = zc_ref[...]
        ext = jnp.concatenate([halo, u], axis=0)
        a1 = ext + pltpu.roll(ext, 1, 0)
        a2 = a1 + pltpu.roll(a1, 2, 0)
        a3 = a2 + pltpu.roll(a2, 4, 0)
        a4 = a3 + pltpu.roll(a3, 8, 0)
        sums = _pool_select(a1, a2, a3, a4)[POOL_HALO:]
        t = seq_blk * tm + lax.broadcasted_iota(jnp.int32, (tm, 1), 0)
        p = (sums / _pool_count(t) - u).astype(bf16)
        p_ref[...] = p
        y_ref[...] = (jnp.dot(p, w_ref[...], preferred_element_type=f32) * s_ref[...]).astype(bf16)

    return pl.pallas_call(
        body, name=name, grid=(M // tm,),
        in_specs=[pl.BlockSpec((tm, MAIN_W), lambda i: (i, 0)),
                  pl.BlockSpec((POOL_HALO, MAIN_W), lambda i: (jnp.maximum(i * hb - 1, 0), 0)),
                  pl.BlockSpec((MAIN_W, MAIN_W), lambda i: (0, 0)),
                  pl.BlockSpec((1, MAIN_W), lambda i: (0, 0))],
        out_specs=[pl.BlockSpec((tm, MAIN_W), lambda i: (i, 0)),
                   pl.BlockSpec((tm, MAIN_W), lambda i: (i, 0))],
        out_shape=[_sds((M, MAIN_W), bf16), _sds((M, D_MODEL), bf16)],
        compiler_params=_params(("parallel",)),
    )(z, z, wbd, scale)


def pool_bwd(dyc, p, wbd, scale, *, name):
    M = p.shape[0]
    tm = 256
    nper = SEQ // tm
    hb = tm // POOL_HALO
    last_hb = M // POOL_HALO - 1

    def body(dy_ref, dyh_ref, p_ref, w_ref, s_ref, dz_ref, dw_ref, ds_ref):
        i = pl.program_id(0)
        seq_blk = i % nper
        dy = dy_ref[...].astype(f32)
        pv = p_ref[...]
        w = w_ref[...]
        sc = s_ref[...]

        @pl.when(i == 0)
        def _():
            dw_ref[...] = jnp.zeros_like(dw_ref)
            ds_ref[...] = jnp.zeros_like(ds_ref)

        v = jnp.dot(pv, w, preferred_element_type=f32)
        ds_ref[...] += jnp.sum(dy * v, axis=0, keepdims=True)
        dv = (dy * sc).astype(bf16)
        dw_ref[...] += lax.dot_general(pv, dv, TN, preferred_element_type=f32)
        dp = lax.dot_general(dv, w, NT, preferred_element_type=f32)
        dvh = jnp.where(seq_blk == nper - 1, 0.0, dyh_ref[...].astype(f32) * sc).astype(bf16)
        dph = lax.dot_general(dvh, w, NT, preferred_element_type=f32)
        ext = jnp.concatenate([dp, dph], axis=0)
        n = tm + POOL_HALO
        t = seq_blk * tm + lax.broadcasted_iota(jnp.int32, (n, 1), 0)
        e = ext / _pool_count(t)
        b1 = e + pltpu.roll(e, n - 1, 0)
        b2 = b1 + pltpu.roll(b1, n - 2, 0)
        b3 = b2 + pltpu.roll(b2, n - 4, 0)
        b4 = b3 + pltpu.roll(b3, n - 8, 0)
        dz_ref[...] = (_pool_select(b1, b2, b3, b4)[:tm] - dp).astype(dz_ref.dtype)

    return pl.pallas_call(
        body, name=name, grid=(M // tm,),
        in_specs=[pl.BlockSpec((tm, MAIN_W), lambda i: (i, 0)),
                  pl.BlockSpec((POOL_HALO, MAIN_W), lambda i: (jnp.minimum((i + 1) * hb, last_hb), 0)),
                  pl.BlockSpec((tm, MAIN_W), lambda i: (i, 0)),
                  pl.BlockSpec((MAIN_W, MAIN_W), lambda i: (0, 0)),
                  pl.BlockSpec((1, MAIN_W), lambda i: (0, 0))],
        out_specs=[pl.BlockSpec((tm, MAIN_W), lambda i: (i, 0)),
                   pl.BlockSpec((MAIN_W, MAIN_W), lambda i: (0, 0)),
                   pl.BlockSpec((1, MAIN_W), lambda i: (0, 0))],
        out_shape=[_sds((M, D_MODEL), bf16), _sds((MAIN_W, MAIN_W), f32), _sds((1, MAIN_W), f32)],
        compiler_params=_params(("arbitrary",)),
    )(dyc, dyc, p, wbd, scale)


def _mem_probs(q, kv, h):
    hs = slice(h * HEAD_DIM, (h + 1) * HEAD_DIM)
    qh = q[:, hs]
    kh = kv[:, hs]
    s = lax.dot_general(qh, kh, NT, preferred_element_type=f32) * SCALE
    m = jnp.max(s, axis=-1, keepdims=True)
    e = jnp.exp(s - m)
    return qh, kh, e / jnp.sum(e, axis=-1, keepdims=True)


def memattn_fwd(z, kvm, ycat, *, name, n_seq):
    M = z.shape[0]
    tq = 512
    nq = SEQ // tq

    def body(q_ref, kv_ref, _, o_ref):
        q = q_ref[...].astype(bf16)
        kv = kv_ref[...]
        outs = []
        for h in range(N_MEM_HEADS):
            _, _, p = _mem_probs(q, kv, h)
            vh = kv[:, MEM_W + h * HEAD_DIM: MEM_W + (h + 1) * HEAD_DIM]
            outs.append(jnp.dot(p.astype(bf16), vh, preferred_element_type=f32))
        o_ref[...] = jnp.concatenate(outs, axis=1).astype(bf16)

    return pl.pallas_call(
        body, name=name, grid=(n_seq, nq),
        in_specs=[pl.BlockSpec((tq, MEM_W), lambda b, i: (b * nq + i, 3)),
                  pl.BlockSpec((N_MEM, 2 * MEM_W), lambda b, i: (b, 0)),
                  pl.BlockSpec(memory_space=pl.ANY)],
        out_specs=pl.BlockSpec((tq, MEM_W), lambda b, i: (b * nq + i, 3)),
        out_shape=_sds((M, D_MODEL), bf16),
        input_output_aliases={2: 0},
        compiler_params=_params(("parallel", "parallel")),
    )(z, kvm, ycat)


def memattn_bwd(z, kvm, dyc, dz, *, name, n_seq):
    M = z.shape[0]
    tq = 512
    nq = SEQ // tq

    def body(q_ref, kv_ref, dy_ref, _, dq_ref, dkv_ref):
        q = q_ref[...].astype(bf16)
        kv = kv_ref[...]
        dy = dy_ref[...].astype(bf16)
        dqs, dks, dvs = [], [], []
        for h in range(N_MEM_HEADS):
            hs = slice(h * HEAD_DIM, (h + 1) * HEAD_DIM)
            qh, kh, p = _mem_probs(q, kv, h)
            vh = kv[:, MEM_W + h * HEAD_DIM: MEM_W + (h + 1) * HEAD_DIM]
            dyh = dy[:, hs]
            dvs.append(lax.dot_general(p.astype(bf16), dyh, TN, preferred_element_type=f32))
            dp = lax.dot_general(dyh, vh, NT, preferred_element_type=f32)
            ds = (p * (dp - jnp.sum(dp * p, axis=-1, keepdims=True)) * SCALE).astype(bf16)
            dqs.append(jnp.dot(ds, kh, preferred_element_type=f32))
            dks.append(lax.dot_general(ds, qh, TN, preferred_element_type=f32))
        dq_ref[...] = jnp.concatenate(dqs, axis=1).astype(bf16)

        @pl.when(pl.program_id(1) == 0)
        def _():
            dkv_ref[...] = jnp.zeros_like(dkv_ref)

        dkv_ref[...] += jnp.concatenate(dks + dvs, axis=1)

    return pl.pallas_call(
        body, name=name, grid=(n_seq, nq),
        in_specs=[pl.BlockSpec((tq, MEM_W), lambda b, i: (b * nq + i, 3)),
                  pl.BlockSpec((N_MEM, 2 * MEM_W), lambda b, i: (b, 0)),
                  pl.BlockSpec((tq, MEM_W), lambda b, i: (b * nq + i, 3)),
                  pl.BlockSpec(memory_space=pl.ANY)],
        out_specs=[pl.BlockSpec((tq, MEM_W), lambda b, i: (b * nq + i, 3)),
                   pl.BlockSpec((N_MEM, 2 * MEM_W), lambda b, i: (b, 0))],
        out_shape=[_sds((M, D_MODEL), bf16), _sds((n_seq * N_MEM, 2 * MEM_W), f32)],
        input_output_aliases={3: 0},
        compiler_params=_params(("parallel", "arbitrary")),
    )(z, kvm, dyc, dz)


def rope_tables(pos, *, name):
    M = pos.shape[0]
    tm = min(1024, M)
    half = HEAD_DIM // 2
    inv = ROPE_THETA ** (-np.arange(half, dtype=np.float64) / half)
    inv128 = jnp.asarray(np.tile(inv, 4)[None, :], f32)
    sign128 = jnp.asarray(np.tile(np.concatenate([-np.ones(half), np.ones(half)]), 2)[None, :], f32)

    def body(p_ref, f_ref, s_ref, cos_ref, sin_ref):
        ang = p_ref[...] * f_ref[...]
        cos_ref[...] = jnp.cos(ang)
        sin_ref[...] = jnp.sin(ang) * s_ref[...]

    return pl.pallas_call(
        body, name=name, grid=(M // tm,),
        in_specs=[pl.BlockSpec((tm, 1), lambda i: (i, 0)),
                  pl.BlockSpec((1, 128), lambda i: (0, 0)),
                  pl.BlockSpec((1, 128), lambda i: (0, 0))],
        out_specs=[pl.BlockSpec((tm, 128), lambda i: (i, 0)),
                   pl.BlockSpec((tm, 128), lambda i: (i, 0))],
        out_shape=[_sds((M, 128), f32), _sds((M, 128), f32)],
        compiler_params=_params(("parallel",)),
    )(pos, inv128, sign128)


def _swap_halves(x):
    w = x.shape[1]
    first = (lax.broadcasted_iota(jnp.int32, (1, w), 1) % HEAD_DIM) < (HEAD_DIM // 2)
    return jnp.where(first, pltpu.roll(x, w - HEAD_DIM // 2, 1), pltpu.roll(x, HEAD_DIM // 2, 1))


def rope_fwd(src, cos, sin, *, name):
    M = src.shape[0]
    tm = min(512, M)

    def body(x_ref, c_ref, s_ref, o_ref):
        x = x_ref[...].astype(f32)
        c = jnp.tile(c_ref[...], (1, MAIN_W // 128))
        s = jnp.tile(s_ref[...], (1, MAIN_W // 128))
        o_ref[...] = x * c + _swap_halves(x) * s

    return pl.pallas_call(
        body, name=name, grid=(M // tm,),
        in_specs=[pl.BlockSpec((tm, MAIN_W), lambda i: (i, 0)),
                  pl.BlockSpec((tm, 128), lambda i: (i, 0)),
                  pl.BlockSpec((tm, 128), lambda i: (i, 0))],
        out_specs=pl.BlockSpec((tm, MAIN_W), lambda i: (i, 0)),
        out_shape=_sds((M, MAIN_W), f32),
        compiler_params=_params(("parallel",)),
    )(src, cos, sin)


def group_sum(groups, cos, sin, *, name, rotate, width, col_block=0, into=None):
    M = groups[0][0].shape[0]
    tm = min(512, M)
    counts = [len(g) for g in groups]
    flat = [a for g in groups for a in g]
    extra = [] if into is None else [into]

    def body(*refs):
        part_refs = refs[:len(flat)]
        c_ref, s_ref = refs[len(flat):len(flat) + 2]
        o_ref = refs[-1]
        cols, k = [], 0
        for n in counts:
            acc = part_refs[k][...]
            for r in part_refs[k + 1:k + n]:
                acc = acc + r[...]
            cols.append(acc)
            k += n
        d = jnp.concatenate(cols, axis=1)
        if rotate:
            c = jnp.tile(c_ref[...], (1, MAIN_W // 128))
            s = jnp.tile(s_ref[...], (1, MAIN_W // 128))
            d = d * c - _swap_halves(d) * s
        o_ref[...] = d.astype(bf16)

    part = pl.BlockSpec((tm, GROUP_W), lambda i: (i, 0))
    tab = pl.BlockSpec((tm, 128), lambda i: (i, 0))
    return pl.pallas_call(
        body, name=name, grid=(M // tm,),
        in_specs=[part] * len(flat) + [tab, tab] + [pl.BlockSpec(memory_space=pl.ANY)] * len(extra),
        out_specs=pl.BlockSpec((tm, MAIN_W), lambda i: (i, col_block)),
        out_shape=_sds((M, width), bf16),
        input_output_aliases={len(flat) + 2: 0} if extra else {},
        compiler_params=_params(("parallel",)),
    )(*flat, cos, sin, *extra)


PAIR_W = 2 * HEAD_DIM
MIN_BLOCKS = 4


def _dil_geometry(dil):
    nsub = max(dil, MIN_BLOCKS)
    tb = BAND * nsub
    return nsub, tb, SEQ // tb


def _rows(ref, sub, dil):
    if dil == 1:
        return ref[sub * BAND:(sub + 1) * BAND, :]
    return ref[pl.ds(sub, BAND, stride=dil), :]


def _store_rows(ref, sub, dil, val):
    if dil == 1:
        ref[sub * BAND:(sub + 1) * BAND, :] = val
    else:
        ref[pl.ds(sub, BAND, stride=dil), :] = val


def _keys(prev_ref, own_ref, sub, dil):
    if prev_ref is None:
        return _rows(own_ref, sub, dil)
    if dil == 1:
        if sub == 0:
            return jnp.concatenate([_rows(prev_ref, prev_ref.shape[0] // BAND - 1, 1), _rows(own_ref, 0, 1)], axis=0)
        return own_ref[(sub - 1) * BAND:(sub + 1) * BAND, :]
    return jnp.concatenate([_rows(prev_ref, sub, dil), _rows(own_ref, sub, dil)], axis=0)


def _band_mask(nkeys, has_prev):
    i = lax.broadcasted_iota(jnp.int32, (BAND, nkeys), 0)
    j = lax.broadcasted_iota(jnp.int32, (BAND, nkeys), 1)
    if nkeys == BAND:
        return j <= i
    return (j >= i) & (j <= i + BAND) & (has_prev | (j >= BAND))


def _first_head():
    return lax.broadcasted_iota(jnp.int32, (1, PAIR_W), 1) < HEAD_DIM


def _col(x, hh):
    return x[:, hh * HEAD_DIM:hh * HEAD_DIM + 1]


def _pair_spec(tb, nblk, col0, which):
    def idx(b, p, i):
        if which < 0:
            i = jnp.maximum(i - 1, 0)
        elif which > 0:
            i = jnp.minimum(i + 1, nblk - 1)
        return (b * nblk + i, col0 + p)
    return pl.BlockSpec((tb, PAIR_W), idx)


def dil_fwd(q, k, kv, g, dil, *, name, n_seq):
    M = q.shape[0]
    nsub, tb, nblk = _dil_geometry(dil)
    with_prev = nblk > 1

    def body(*refs):
        if with_prev:
            q_ref, ko_ref, vo_ref, kp_ref, vp_ref, o_ref, l_ref = refs
        else:
            (q_ref, ko_ref, vo_ref, o_ref, l_ref), kp_ref, vp_ref = refs, None, None
        first = _first_head()
        blk = pl.program_id(2)
        for sub in range(nsub):
            qs = _rows(q_ref, sub, dil) * SCALE
            kc = _keys(kp_ref, ko_ref, sub, dil).astype(bf16)
            vc = _keys(vp_ref, vo_ref, sub, dil).astype(bf16)
            has_prev = True if (dil == 1 and sub > 0) else blk > 0
            mask = _band_mask(kc.shape[0], has_prev)
            outs, lses = [], []
            for hh in range(2):
                qm = jnp.where(first if hh == 0 else ~first, qs, 0.0).astype(bf16)
                s = jnp.where(mask, lax.dot_general(qm, kc, NT, preferred_element_type=f32), NEG)
                m = jnp.max(s, axis=-1, keepdims=True)
                e = jnp.exp(s - m)
                l = jnp.sum(e, axis=-1, keepdims=True)
                outs.append(jnp.dot(e.astype(bf16), vc, preferred_element_type=f32) * (1.0 / l))
                lses.append(jnp.broadcast_to(m + jnp.log(l), (BAND, PAIR_W)))
            _store_rows(o_ref, sub, dil, jnp.where(first, outs[0], outs[1]))
            _store_rows(l_ref, sub, dil, jnp.where(first, lses[0], lses[1]))

    ins = [(q, 2 * g, 0), (k, 2 * g, 0), (kv, 6 + 2 * g, 0)]
    if with_prev:
        ins += [(k, 2 * g, -1), (kv, 6 + 2 * g, -1)]
    out = _pair_spec(tb, nblk, 0, 0)
    return pl.pallas_call(
        body, name=name, grid=(n_seq, 2, nblk),
        in_specs=[_pair_spec(tb, nblk, c, w) for _, c, w in ins],
        out_specs=[out, out],
        out_shape=[_sds((M, GROUP_W), f32)] * 2,
        compiler_params=_params(("parallel", "parallel", "arbitrary")),
    )(*[a for a, _, _ in ins])


def combine_fwd(os_, lses, *, name):
    M = os_[0].shape[0]
    tm = min(512, M)

    def body(o0, o1, o2, l0, l1, l2, y_ref):
        ls = [l0[...], l1[...], l2[...]]
        m = jnp.maximum(jnp.maximum(ls[0], ls[1]), ls[2])
        es = [jnp.exp(l - m) for l in ls]
        inv = 1.0 / (es[0] + es[1] + es[2])
        y_ref[...] = jnp.concatenate([o[...] * e * inv for o, e in zip((o0, o1, o2), es)], axis=1).astype(bf16)

    part = pl.BlockSpec((tm, GROUP_W), lambda i: (i, 0))
    return pl.pallas_call(
        body, name=name, grid=(M // tm,),
        in_specs=[part] * 6,
        out_specs=pl.BlockSpec((tm, MAIN_W), lambda i: (i, 0)),
        out_shape=_sds((M, D_MODEL), bf16),
        compiler_params=_params(("parallel",)),
    )(*os_, *lses)


def combine_bwd(dyc, os_, lses, *, name):
    M = os_[0].shape[0]
    tm = min(512, M)

    def body(dy_ref, o0, o1, o2, l0, l1, l2, d0, d1, d2, c0, c1, c2):
        r = lax.broadcasted_iota(jnp.int32, (GROUP_W, GROUP_W), 0) // HEAD_DIM
        c = lax.broadcasted_iota(jnp.int32, (GROUP_W, GROUP_W), 1) // HEAD_DIM
        ones = (r == c).astype(f32)
        dy = dy_ref[...].astype(f32)
        ls = [l0[...], l1[...], l2[...]]
        m = jnp.maximum(jnp.maximum(ls[0], ls[1]), ls[2])
        es = [jnp.exp(l - m) for l in ls]
        inv = 1.0 / (es[0] + es[1] + es[2])
        total = 0.0
        alphas = []
        for g, (o, e, d_ref) in enumerate(zip((o0, o1, o2), es, (d0, d1, d2))):
            a = e * inv
            dyg = dy[:, g * GROUP_W:(g + 1) * GROUP_W]
            d_ref[...] = dyg * a
            dsum = jnp.dot(dyg * o[...], ones, precision=lax.Precision.HIGHEST, preferred_element_type=f32)
            total = total + a * dsum
            alphas.append(a)
        for a, c_ref in zip(alphas, (c0, c1, c2)):
            c_ref[...] = -a * total

    part = pl.BlockSpec((tm, GROUP_W), lambda i: (i, 0))
    outs = pl.pallas_call(
        body, name=name, grid=(M // tm,),
        in_specs=[pl.BlockSpec((tm, MAIN_W), lambda i: (i, 0))] + [part] * 6,
        out_specs=[part] * 6,
        out_shape=[_sds((M, GROUP_W), f32)] * 6,
        compiler_params=_params(("parallel",)),
    )(dyc, *os_, *lses)
    return outs[:3], outs[3:]


def dil_bwd_dq(q, k, kv, do, cc, lse, g, dil, *, name, n_seq):
    M = q.shape[0]
    nsub, tb, nblk = _dil_geometry(dil)
    with_prev = nblk > 1

    def body(*refs):
        if with_prev:
            q_ref, ko_ref, vo_ref, do_ref, c_ref, l_ref, kp_ref, vp_ref, dq_ref = refs
        else:
            (q_ref, ko_ref, vo_ref, do_ref, c_ref, l_ref, dq_ref), kp_ref, vp_ref = refs, None, None
        first = _first_head()
        blk = pl.program_id(2)
        for sub in range(nsub):
            qs = _rows(q_ref, sub, dil) * SCALE
            dos = _rows(do_ref, sub, dil)
            cs = _rows(c_ref, sub, dil)
            ls = _rows(l_ref, sub, dil)
            kc = _keys(kp_ref, ko_ref, sub, dil).astype(bf16)
            vc = _keys(vp_ref, vo_ref, sub, dil).astype(bf16)
            has_prev = True if (dil == 1 and sub > 0) else blk > 0
            mask = _band_mask(kc.shape[0], has_prev)
            outs = []
            for hh in range(2):
                lm = first if hh == 0 else ~first
                qm = jnp.where(lm, qs, 0.0).astype(bf16)
                dom = jnp.where(lm, dos, 0.0).astype(bf16)
                s = jnp.where(mask, lax.dot_general(qm, kc, NT, preferred_element_type=f32), NEG)
                p = jnp.exp(s - _col(ls, hh))
                dp = lax.dot_general(dom, vc, NT, preferred_element_type=f32)
                ds = (p * (dp + _col(cs, hh))).astype(bf16)
                outs.append(jnp.dot(ds, kc, preferred_element_type=f32) * SCALE)
            _store_rows(dq_ref, sub, dil, jnp.where(first, outs[0], outs[1]))

    ins = [(q, 2 * g, 0), (k, 2 * g, 0), (kv, 6 + 2 * g, 0), (do, 0, 0), (cc, 0, 0), (lse, 0, 0)]
    if with_prev:
        ins += [(k, 2 * g, -1), (kv, 6 + 2 * g, -1)]
    return pl.pallas_call(
        body, name=name, grid=(n_seq, 2, nblk),
        in_specs=[_pair_spec(tb, nblk, c, w) for _, c, w in ins],
        out_specs=_pair_spec(tb, nblk, 0, 0),
        out_shape=_sds((M, GROUP_W), f32),
        compiler_params=_params(("parallel", "parallel", "arbitrary")),
    )(*[a for a, _, _ in ins])


def dil_bwd_dkv(q, k, kv, do, cc, lse, g, dil, *, name, n_seq):
    M = q.shape[0]
    nsub, tb, nblk = _dil_geometry(dil)
    with_next = nblk > 1

    def body(*refs):
        if with_next:
            (k_ref, v_ref, q_ref, do_ref, c_ref, l_ref, qn_ref, don_ref, cn_ref, ln_ref, dk_ref, dv_ref) = refs
        else:
            (k_ref, v_ref, q_ref, do_ref, c_ref, l_ref, dk_ref, dv_ref) = refs
        first = _first_head()
        blk = pl.program_id(2)
        i = lax.broadcasted_iota(jnp.int32, (BAND, BAND), 0)
        j = lax.broadcasted_iota(jnp.int32, (BAND, BAND), 1)
        for sub in range(nsub):
            ks = _rows(k_ref, sub, dil)
            vs = _rows(v_ref, sub, dil)
            sets = [((q_ref, do_ref, c_ref, l_ref), sub, j <= i)]
            if dil == 1 and sub + 1 < nsub:
                sets.append(((q_ref, do_ref, c_ref, l_ref), sub + 1, j >= i))
            elif with_next:
                nsubq = 0 if dil == 1 else sub
                sets.append(((qn_ref, don_ref, cn_ref, ln_ref), nsubq, (j >= i) & (blk + 1 < nblk)))
            loaded = [(tuple(_rows(r, sq, dil) for r in rs), mask) for rs, sq, mask in sets]
            dks, dvs = [], []
            for hh in range(2):
                lm = first if hh == 0 else ~first
                km = jnp.where(lm, ks, 0.0).astype(bf16)
                vm = jnp.where(lm, vs, 0.0).astype(bf16)
                dk = jnp.zeros((BAND, PAIR_W), f32)
                dv = jnp.zeros((BAND, PAIR_W), f32)
                for (qs, dos, cs, ls), mask in loaded:
                    qb = qs.astype(bf16)
                    dob = dos.astype(bf16)
                    s = lax.dot_general(qb, km, NT, preferred_element_type=f32) * SCALE
                    p = jnp.exp(jnp.where(mask, s, NEG) - _col(ls, hh))
                    dv = dv + lax.dot_general(p.astype(bf16), dob, TN, preferred_element_type=f32)
                    dp = lax.dot_general(dob, vm, NT, preferred_element_type=f32)
                    ds = (p * (dp + _col(cs, hh)) * SCALE).astype(bf16)
                    dk = dk + lax.dot_general(ds, qb, TN, preferred_element_type=f32)
                dks.append(dk)
                dvs.append(dv)
            _store_rows(dk_ref, sub, dil, jnp.where(first, dks[0], dks[1]))
            _store_rows(dv_ref, sub, dil, jnp.where(first, dvs[0], dvs[1]))

    ins = [(k, 2 * g, 0), (kv, 6 + 2 * g, 0), (q, 2 * g, 0), (do, 0, 0), (cc, 0, 0), (lse, 0, 0)]
    if with_next:
        ins += [(q, 2 * g, 1), (do, 0, 1), (cc, 0, 1), (lse, 0, 1)]
    out = _pair_spec(tb, nblk, 0, 0)
    return pl.pallas_call(
        body, name=name, grid=(n_seq, 2, nblk),
        in_specs=[_pair_spec(tb, nblk, c, w) for _, c, w in ins],
        out_specs=[out, out],
        out_shape=[_sds((M, GROUP_W), f32)] * 2,
        compiler_params=_params(("parallel", "parallel", "arbitrary")),
    )(*[a for a, _, _ in ins])


def _blockdiag(wp):
    out = jnp.zeros((MAIN_W, MAIN_W), wp.dtype)
    for gi in range(len(POOL_WINDOWS)):
        sl = slice(gi * POOL_GROUP, (gi + 1) * POOL_GROUP)
        out = out.at[sl, sl].set(wp[gi])
    return out


def _unblockdiag(w):
    return jnp.stack([w[gi * POOL_GROUP:(gi + 1) * POOL_GROUP, gi * POOL_GROUP:(gi + 1) * POOL_GROUP]
                      for gi in range(len(POOL_WINDOWS))])


def local_step(x, mem, positions, target, P, layer_weights, kv_weight, emit_grads):
    n_seq = x.shape[0]
    M = n_seq * SEQ
    xs = x.reshape(M, D_MODEL)
    mems = mem.reshape(n_seq * N_MEM, D_MODEL)
    pos = positions.reshape(M, 1).astype(f32)
    cos, sin = rope_tables(pos, name="rope_tables")
    gains = P["norm_gains"]

    def gain(l, k):
        return gains[l, k].reshape(1, D_MODEL)

    saved = []
    kvs = None
    for l in range(DEPTH):
        W, started = layer_weights(l, "mix", xs)
        sv = {"x": xs, "W": W}
        z, h1 = rms_matmul(xs, gain(l, 0), W["w_in"], name=f"l{l}_in", out_dtype=f32, after=started)
        kvm, mn = rms_matmul(mems, P["mem_norm"][l].reshape(1, D_MODEL), W["w_mem_kv"],
                             name=f"l{l}_memkv", out_dtype=bf16)
        sv.update(z=z, h1=h1, kvm=kvm, mn=mn)
        if l < N_A_LAYERS:
            wbd = _blockdiag(P["w_pool"][l].astype(bf16))
            psc = P["pool_scale"][l].reshape(1, MAIN_W)
            p, y_main = pool_fwd(z, wbd, psc, name=f"l{l}_pool")
            sv.update(p=p, wbd=wbd, psc=psc)
        else:
            qrot = rope_fwd(z, cos, sin, name=f"l{l}_ropeq")
            os_, lses = [], []
            for g, (_, dil) in enumerate(DIL_PATTERNS):
                o, lse = dil_fwd(qrot, kvs["krot"], kvs["kv"], g, dil, name=f"l{l}_dil{g}", n_seq=n_seq)
                os_.append(o)
                lses.append(lse)
            y_main = combine_fwd(os_, lses, name=f"l{l}_comb")
            sv.update(qrot=qrot, os=os_, lses=lses)
        ycat = memattn_fwd(z, kvm, y_main, name=f"l{l}_memattn", n_seq=n_seq)
        y, x1 = matmul_rms_res(ycat, W["w_out"], gain(l, 1), xs, name=f"l{l}_out")
        W.update(layer_weights(l, "gu", x1)[0])
        fg, fu, a, h2 = rms_gate_up(x1, gain(l, 2), W["w_gate_up"], name=f"l{l}_gu")
        W.update(layer_weights(l, "down", a)[0])
        y2, x2 = matmul_rms_res(a, W["w_down"], gain(l, 3), x1, name=f"l{l}_down")
        sv.update(ycat=ycat, y=y, x1=x1, fg=fg, fu=fu, h2=h2, a=a, y2=y2)
        saved.append(sv)
        xs = x2
        if l == N_A_LAYERS - 1:
            w_kv = kv_weight(xs)
            kv, hkv = rms_matmul(xs, P["kv_norm"].reshape(1, D_MODEL), w_kv, name="kv_proj", out_dtype=f32,
                                 transposed=True)
            krot = rope_fwd(kv, cos, sin, name="ropek")
            kvs = {"kv": kv, "hkv": hkv, "krot": krot, "x": xs, "w_kv": w_kv}

    dx, sq = loss_head(xs, target.reshape(M, D_MODEL), name="loss_head")

    G = {"mem_norm": [None] * DEPTH, "norm_gains": [[None] * 4 for _ in range(DEPTH)],
         "pool_scale": [None] * N_A_LAYERS}
    dk_parts = [[] for _ in range(N_GROUPS)]
    dv_parts = [[] for _ in range(N_GROUPS)]
    emitted = None

    for l in reversed(range(DEPTH)):
        sv = saved[l]
        W = sv["W"]
        gw = {}
        dy2, dgu, G["norm_gains"][l][3] = down_bwd(sv["y2"], gain(l, 3), dx, W["w_down"], sv["fg"], sv["fu"],
                                                   name=f"l{l}_b_dgu", after=emitted)
        gw["w_down"] = matmul(sv["a"], dy2, TN, name=f"l{l}_b_wd", out_dtype=bf16)
        dx1, G["norm_gains"][l][2] = matmul_rms_bwd(dgu, W["w_gate_up"], NN, sv["x1"], gain(l, 2), dx,
                                                    name=f"l{l}_b_dh2")
        gw["w_gate_up"] = matmul(dgu, sv["h2"], TN, name=f"l{l}_b_wgu", out_dtype=bf16)
        emitted = emit_grads(l, "ffn", gw)
        gw = {}
        dy, dycat, G["norm_gains"][l][1] = rms_bwd_matmul(sv["y"], gain(l, 1), dx1, W["w_out"], NT,
                                                          name=f"l{l}_b_dycat", after=emitted)
        gw["w_out"] = matmul(sv["ycat"], dy, TN, name=f"l{l}_b_wout", out_dtype=bf16)
        if l < N_A_LAYERS:
            dz, dwbd, dps = pool_bwd(dycat, sv["p"], sv["wbd"], sv["psc"], name=f"l{l}_b_pool")
            gw["w_pool"] = _unblockdiag(dwbd).reshape(MAIN_W, POOL_GROUP).astype(bf16)
            G["pool_scale"][l] = dps.reshape(MAIN_W)
        else:
            dos, ccs = combine_bwd(dycat, sv["os"], sv["lses"], name=f"l{l}_b_comb")
            dqs = []
            for g, (_, dil) in enumerate(DIL_PATTERNS):
                args = (sv["qrot"], kvs["krot"], kvs["kv"], dos[g], ccs[g], sv["lses"][g], g, dil)
                dqs.append([dil_bwd_dq(*args, name=f"l{l}_b_dq{g}", n_seq=n_seq)])
                dk, dv = dil_bwd_dkv(*args, name=f"l{l}_b_dkv{g}", n_seq=n_seq)
                dk_parts[g].append(dk)
                dv_parts[g].append(dv)
            dz = group_sum(dqs, cos, sin, name=f"l{l}_b_ropeq", rotate=True, width=D_MODEL)
        dz, dkvm = memattn_bwd(sv["z"], sv["kvm"], dycat, dz, name=f"l{l}_b_memattn", n_seq=n_seq)
        dmn = matmul(dkvm, W["w_mem_kv"], NT, name=f"l{l}_b_dmn", out_dtype=bf16)
        gw["w_mem_kv"] = matmul(sv["mn"], dkvm, TN, name=f"l{l}_b_wmkv", out_dtype=bf16)
        _, G["mem_norm"][l] = rms_bwd(mems, P["mem_norm"][l].reshape(1, D_MODEL), dmn, None,
                                      name=f"l{l}_b_nmem", out_dtype=bf16)
        gw["w_in"] = matmul(sv["h1"], dz, TN, name=f"l{l}_b_win", out_dtype=bf16)
        dx, G["norm_gains"][l][0] = matmul_rms_bwd(dz, W["w_in"], NT, sv["x"], gain(l, 0), dx1, name=f"l{l}_b_dh1")
        if l == N_A_LAYERS:
            dkv = group_sum(dk_parts, cos, sin, name="b_ropek", rotate=True, width=2 * MAIN_W)
            dkv = group_sum(dv_parts, cos, sin, name="b_sumv", rotate=False, width=2 * MAIN_W, col_block=1, into=dkv)
            gw["w_kv"] = matmul(dkv, kvs["hkv"], TN, name="b_wkv", out_dtype=bf16)
            dx, gkn = matmul_rms_bwd(dkv, kvs["w_kv"], NN, kvs["x"], P["kv_norm"].reshape(1, D_MODEL), dx,
                                     name="b_dhkv")
            G["kv_norm"] = gkn.reshape(D_MODEL)
        emitted = emit_grads(l, "mix", gw)

    small = {"pool_scale": jnp.stack(G["pool_scale"]),
             "mem_norm": jnp.concatenate(G["mem_norm"], axis=0),
             "norm_gains": jnp.stack([jnp.concatenate(r, axis=0) for r in G["norm_gains"]]),
             "kv_norm": G["kv_norm"]}
    return sq[0, 0], dx.reshape(n_seq, SEQ, D_MODEL), small, emitted


def _peer(k):
    x, y, c = lax.axis_index("x"), lax.axis_index("y"), lax.axis_index("c")
    px = 1 - x if k & 4 else x
    py = 1 - y if k & 2 else y
    pc = 1 - c if k & 1 else c
    return (px, py, pc), 4 * px + 2 * py + pc


def _my_index():
    return 4 * lax.axis_index("x") + 2 * lax.axis_index("y") + lax.axis_index("c")


def _src_for(kinds, in_refs, i, idx):
    return in_refs[i] if kinds[i] == "gather" else in_refs[i].at[idx]


def _local_copies(kinds, in_refs, out_refs, local_sems):
    me = _my_index()
    return [pltpu.make_async_copy(_src_for(kinds, in_refs, i, me), out_refs[i].at[me], local_sems.at[i])
            for i in range(len(kinds))]


def _remote_copies(kinds, in_refs, out_refs, send_sems, recv_sems, *, arriving):
    me = _my_index()
    copies = []
    for k in range(1, N_DEV):
        dev, idx = _peer(k)
        for i in range(len(kinds)):
            j = i * (N_DEV - 1) + k - 1
            copies.append(pltpu.make_async_remote_copy(
                src_ref=_src_for(kinds, in_refs, i, idx), dst_ref=out_refs[i].at[idx if arriving else me],
                send_sem=send_sems.at[j], recv_sem=recv_sems.at[j], device_id=dev, device_id_type=MESH))
    return copies


def _out_shape(a, kind):
    return ((N_DEV,) + a.shape) if kind == "gather" else a.shape


def exchange(items, *, name, after=()):
    n = len(items)
    kinds = [k for _, k in items]
    after = list(after)

    def body(*refs):
        in_refs, out_refs = refs[:n], refs[n + len(after):2 * n + len(after)]
        send_sems, recv_sems, local_sems = refs[-3:]
        local = _local_copies(kinds, in_refs, out_refs, local_sems)
        sends = _remote_copies(kinds, in_refs, out_refs, send_sems, recv_sems, arriving=False)
        for cp in local + sends:
            cp.start()
        for cp in _remote_copies(kinds, in_refs, out_refs, send_sems, recv_sems, arriving=True):
            cp.wait_recv()
        for cp in sends:
            cp.wait_send()
        for cp in local:
            cp.wait()

    any_spec = pl.BlockSpec(memory_space=pl.ANY)
    return pl.pallas_call(
        body, name=name,
        in_specs=[any_spec] * (n + len(after)), out_specs=[any_spec] * n,
        out_shape=[_sds(_out_shape(a, k), a.dtype) for a, k in items],
        scratch_shapes=[pltpu.SemaphoreType.DMA((n * (N_DEV - 1),)), pltpu.SemaphoreType.DMA((n * (N_DEV - 1),)),
                        pltpu.SemaphoreType.DMA((n,))],
    )(*[a for a, _ in items], *after)


_HBM = pl.BlockSpec(memory_space=pltpu.HBM)
_SEM = pl.BlockSpec(memory_space=pltpu.SEMAPHORE)
_EFFECT = pltpu.SideEffectType.DATAFLOW_SIDE_EFFECTING


def exchange_start(items, after, *, name):
    n = len(items)
    kinds = [k for _, k in items]

    def body(*refs):
        in_refs, land_refs = refs[:n], refs[n:2 * n]
        send_sems, recv_sems, local_sems = refs[2 * n + 1:2 * n + 4]
        token = refs[-1]
        for cp in (_local_copies(kinds, in_refs, land_refs, local_sems)
                   + _remote_copies(kinds, in_refs, land_refs, send_sems, recv_sems, arriving=False)):
            cp.start()
        token[...] = jnp.zeros_like(token)

    srcs = [pltpu.with_memory_space_constraint(a, pltpu.HBM) for a, _ in items]
    lands = [pltpu.with_memory_space_constraint(lax.empty(_out_shape(a, k), a.dtype), pltpu.HBM) for a, k in items]
    outs = pl.pallas_call(
        body, name=name,
        out_shape=(pltpu.SemaphoreType.DMA((n * (N_DEV - 1),)), pltpu.SemaphoreType.DMA((n * (N_DEV - 1),)),
                   pltpu.SemaphoreType.DMA((n,)),
                   *[pltpu.HBM(a.shape, a.dtype) for a in srcs], *[pltpu.HBM(a.shape, a.dtype) for a in lands],
                   _sds((8, 128), f32)),
        in_specs=[_HBM] * (2 * n) + [pl.BlockSpec(memory_space=pl.ANY)],
        out_specs=(_SEM, _SEM, _SEM, *[_HBM] * (2 * n), pl.BlockSpec(memory_space=pltpu.VMEM)),
        input_output_aliases={i: 3 + i for i in range(2 * n)},
        compiler_params=pltpu.CompilerParams(has_side_effects=_EFFECT),
    )(*srcs, *lands, after)
    return {"kinds": kinds, "sems": outs[:3], "srcs": outs[3:3 + n], "lands": outs[3 + n:3 + 2 * n], "token": outs[-1]}


def exchange_wait(handle, after, *, name):
    kinds = handle["kinds"]
    n = len(kinds)

    def body(*refs):
        in_refs, land_refs = refs[:n], refs[n:2 * n]
        send_sems, recv_sems, local_sems = refs[2 * n:2 * n + 3]
        for cp in _remote_copies(kinds, in_refs, land_refs, send_sems, recv_sems, arriving=True):
            cp.wait_recv()
        for cp in _remote_copies(kinds, in_refs, land_refs, send_sems, recv_sems, arriving=False):
            cp.wait_send()
        for cp in _local_copies(kinds, in_refs, land_refs, local_sems):
            cp.wait()

    srcs, lands = list(handle["srcs"]), list(handle["lands"])
    after = list(after) if isinstance(after, (list, tuple)) else [after]
    outs = pl.pallas_call(
        body, name=name,
        out_shape=tuple(pltpu.HBM(a.shape, a.dtype) for a in srcs + lands),
        in_specs=[_HBM] * (2 * n) + [_SEM] * 3 + [pl.BlockSpec(memory_space=pl.ANY)] * len(after),
        out_specs=tuple([_HBM] * (2 * n)),
        input_output_aliases={i: i for i in range(2 * n)},
        compiler_params=pltpu.CompilerParams(has_side_effects=_EFFECT),
    )(*srcs, *lands, *handle["sems"], *after)
    return list(outs[n:])


def adamw(slots, w, m, v, *, name, layer=None, into=None):
    R, C = w.shape[-2:]
    tr = _tile(R, (256, 128, 64, 32, 16, 8))
    c1 = 1.0 - ADAM_B1 ** ADAM_STEP
    c2 = 1.0 - ADAM_B2 ** ADAM_STEP
    extra = [] if into is None else list(into)

    def body(s_ref, w_ref, m_ref, v_ref, *refs):
        g_ref, d_ref, m2_ref, v2_ref = refs[len(extra):]
        g = s_ref[0].astype(f32)
        for d in range(1, N_DEV):
            g = g + s_ref[d].astype(f32)
        m2 = ADAM_B1 * m_ref[...] + (1.0 - ADAM_B1) * g
        v2 = ADAM_B2 * v_ref[...] + (1.0 - ADAM_B2) * (g * g)
        g_ref[...] = g
        m2_ref[...] = m2
        v2_ref[...] = v2
        d_ref[...] = -ADAM_LR * ((m2 / c1) / (jnp.sqrt(v2 / c2) + ADAM_EPS) + ADAM_WD * w_ref[...])

    if layer is None:
        blk = pl.BlockSpec((tr, C), lambda i: (i, 0))
    else:
        blk = pl.BlockSpec((None, tr, C), lambda i: (layer, i, 0))
    return pl.pallas_call(
        body, name=name, grid=(R // tr,),
        in_specs=[pl.BlockSpec((N_DEV, tr, C), lambda i: (0, i, 0)), blk, blk, blk]
        + [pl.BlockSpec(memory_space=pl.ANY)] * len(extra),
        out_specs=[blk] * 4,
        out_shape=[_sds(w.shape, f32)] * 4,
        input_output_aliases={4 + j: j for j in range(len(extra))},
        compiler_params=_params(("parallel",)),
    )(slots, w, m, v, *extra)


WEIGHTS = ("norm_gains", "mem_norm", "w_in", "w_mem_kv", "w_out", "w_pool", "pool_scale", "kv_norm", "w_kv",
           "w_gate_up", "w_down")
LAYER_MATS = ("w_in", "w_mem_kv", "w_out", "w_gate_up", "w_down")
POOL_SHARD = MAIN_W // N_DEV
KV_SHARD = 2 * MAIN_W // N_DEV
LOOKAHEAD = 2


def _pack_small(gains, pscale):
    lead = gains.shape[:-3]
    g = gains.reshape(lead + (16, 128))
    p = jnp.zeros(lead + (8, 128), f32).at[..., :2, :POOL_SHARD].set(pscale)
    return jnp.concatenate([g, p], axis=-2)


def _unpack_small(a):
    return a[:16].reshape(4, 4, 128), a[16:18, :POOL_SHARD]


def _pack_repl(mem_norm, kv_norm):
    return jnp.concatenate([mem_norm, kv_norm.reshape(1, D_MODEL), jnp.zeros((3, D_MODEL), f32)], axis=0)


def _unpack_repl(a):
    return a[:4], a[4]


def kernel(x, mem, positions, norm_gains, mem_norm, w_in, w_mem_kv, w_out, w_pool, pool_scale, kv_norm, w_kv, w_gate_up, w_down, loss_target, m_norm_gains, m_mem_norm, m_w_in, m_w_mem_kv, m_w_out, m_w_pool, m_pool_scale, m_kv_norm, m_w_kv, m_w_gate_up, m_w_down, v_norm_gains, v_mem_norm, v_w_in, v_w_mem_kv, v_w_out, v_w_pool, v_pool_scale, v_kv_norm, v_w_kv, v_w_gate_up, v_w_down):
    w = dict(norm_gains=norm_gains, mem_norm=mem_norm, w_in=w_in, w_mem_kv=w_mem_kv, w_out=w_out, w_pool=w_pool,
             pool_scale=pool_scale, kv_norm=kv_norm, w_kv=w_kv, w_gate_up=w_gate_up, w_down=w_down)
    m = dict(norm_gains=m_norm_gains, mem_norm=m_mem_norm, w_in=m_w_in, w_mem_kv=m_w_mem_kv, w_out=m_w_out,
             w_pool=m_w_pool, pool_scale=m_pool_scale, kv_norm=m_kv_norm, w_kv=m_w_kv, w_gate_up=m_w_gate_up,
             w_down=m_w_down)
    v = dict(norm_gains=v_norm_gains, mem_norm=v_mem_norm, w_in=v_w_in, w_mem_kv=v_w_mem_kv, w_out=v_w_out,
             w_pool=v_w_pool, pool_scale=v_pool_scale, kv_norm=v_kv_norm, w_kv=v_w_kv, w_gate_up=v_w_gate_up,
             w_down=v_w_down)

    def transposed_view(d):
        d = dict(d)
        d["w_gate_up"] = jnp.swapaxes(d["w_gate_up"], 1, 2)
        d["w_kv"] = jnp.swapaxes(d["w_kv"], 0, 1)
        return d

    wv, mv, vv = transposed_view(w), transposed_view(m), transposed_view(v)

    small = _pack_small(norm_gains, pool_scale)
    (gsmall,) = exchange([(small, "gather")], name="gather_small")
    P = {"norm_gains": jnp.moveaxis(gsmall[:, :16].reshape(N_DEV, 4, 4, 128), 0, 2).reshape(4, 4, D_MODEL),
         "pool_scale": jnp.moveaxis(gsmall[:, 16:18, :POOL_SHARD], 0, 1).reshape(2, MAIN_W),
         "mem_norm": mem_norm, "kv_norm": kv_norm, "w_pool": w_pool}

    PARTS = {"mix": ("w_in", "w_mem_kv", "w_out"), "ffn": ("w_gate_up", "w_down"), "gu": ("w_gate_up",),
             "down": ("w_down",)}

    def parts_of(l):
        return ("mix", "gu", "down") if l == 0 else ("mix", "ffn")

    def part_items(l, part):
        items = [(wv[k][l].astype(bf16), "gather") for k in PARTS[part]]
        if part == "ffn" and l == N_A_LAYERS - 1:
            items.append((wv["w_kv"].astype(bf16), "gather"))
        return items

    handles = {}

    def start_layer(l, after):
        for part in parts_of(l):
            handles[l, part] = exchange_start(part_items(l, part), after, name=f"gather_start_{part}_l{l}")
            after = handles[l, part]["token"]
        return after

    token = gsmall
    for l in range(LOOKAHEAD):
        token = start_layer(l, token)
    landed = {}

    def layer_weights(l, part, after):
        if part not in parts_of(l):
            if part == "down":
                return {}, None
            part = "ffn"
        first = l == 0 and part == "mix"
        got = exchange_wait(handles[l, part], token if first else after, name=f"gather_wait_{part}_l{l}")
        landed[l, part] = got
        started = None
        if part == "mix" and l + LOOKAHEAD < DEPTH:
            started = start_layer(l + LOOKAHEAD, got[0])
        W = {k: g.reshape(-1, g.shape[-1]) for k, g in zip(PARTS[part], got)}
        return W, started

    def kv_weight(after):
        g = landed[N_A_LAYERS - 1, "ffn"][len(PARTS["ffn"])]
        return g.reshape(2 * MAIN_W, D_MODEL)

    ghandles = {}

    def emit_grads(l, part, gw):
        items = [(gw[k].reshape((N_DEV, -1) + gw[k].shape[-1:]), "scatter") for k in PARTS[part]]
        if part == "mix" and l == N_A_LAYERS:
            items.append((gw["w_kv"].reshape(N_DEV, KV_SHARD, D_MODEL), "scatter"))
        if part == "mix" and l < N_A_LAYERS:
            items.append((gw["w_pool"], "gather"))
        ghandles[l, part] = exchange_start(items, items[0][0], name=f"scatter_start_{part}_l{l}")
        return ghandles[l, part]["token"]

    sq, grad_x, GS, emitted = local_step(x, mem, positions, loss_target, P, layer_weights, kv_weight, emit_grads)
    loss = lax.psum(0.5 * sq / D_MODEL, ("x", "y", "c"))

    def pool3(a):
        return a.reshape(N_A_LAYERS, MAIN_W, POOL_GROUP)

    out = {}
    after = [emitted]

    def finish_layer(l, after):
        for part in ("ffn", "mix"):
            got = exchange_wait(ghandles[l, part], after, name=f"scatter_wait_{part}_l{l}")
            after = []
            for k, slots in zip(PARTS[part], got):
                out[k] = adamw(slots, wv[k], mv[k], vv[k], name=f"adamw_{k}_l{l}", layer=l, into=out.get(k))
                after.append(out[k][0])
            if part == "mix" and l == N_A_LAYERS:
                out["w_kv"] = adamw(got[-1], wv["w_kv"], mv["w_kv"], vv["w_kv"], name="adamw_w_kv")
                after.append(out["w_kv"][0])
            if part == "mix" and l < N_A_LAYERS:
                out["w_pool"] = adamw(got[-1], pool3(w_pool), pool3(m_w_pool), pool3(v_w_pool), name=f"adamw_w_pool_l{l}",
                                      layer=l, into=out.get("w_pool"))
                after.append(out["w_pool"][0])
        return after

    for l in reversed(range(1, DEPTH)):
        after = finish_layer(l, after)

    gs = _pack_small(jnp.moveaxis(GS["norm_gains"].reshape(4, 4, N_DEV, 128), 2, 0),
                     jnp.moveaxis(GS["pool_scale"].reshape(2, N_DEV, POOL_SHARD), 1, 0))
    parts_small, parts_repl = exchange(
        [(gs, "scatter"), (_pack_repl(GS["mem_norm"], GS["kv_norm"]), "gather")],
        name="exchange_small_grads", after=after)
    finish_layer(0, [parts_small])
    out["w_gate_up"] = [jnp.swapaxes(r, 1, 2) for r in out["w_gate_up"]]
    out["w_kv"] = [jnp.swapaxes(r, 0, 1) for r in out["w_kv"]]
    out["w_pool"] = [r.reshape(w_pool.shape) for r in out["w_pool"]]

    res = adamw(parts_small, small, _pack_small(m_norm_gains, m_pool_scale), _pack_small(v_norm_gains, v_pool_scale),
                name="adamw_small")
    out["norm_gains"], out["pool_scale"] = zip(*[_unpack_small(r) for r in res])
    res = adamw(parts_repl, _pack_repl(mem_norm, kv_norm), _pack_repl(m_mem_norm, m_kv_norm),
                _pack_repl(v_mem_norm, v_kv_norm), name="adamw_repl")
    out["mem_norm"], out["kv_norm"] = zip(*[_unpack_repl(r) for r in res])

    return (loss, grad_x, *[out[k][0] for k in WEIGHTS], *[out[k][1] for k in WEIGHTS],
            *[out[k][2] for k in WEIGHTS], *[out[k][3] for k in WEIGHTS])
```

```python
import numpy as np
import jax
import jax.numpy as jnp
from jax import lax
from jax.experimental import pallas as pl
from jax.experimental.pallas import tpu as pltpu

f32 = jnp.float32
bf16 = jnp.bfloat16

D_MODEL = 1024
SEQ = 2048
DEPTH = 4
N_MEM = 256
HEAD_DIM = 64
N_MEM_HEADS = 4
MEM_W = 256
MAIN_W = 768
POOL_WINDOWS = (2, 4, 8, 16)
POOL_GROUP = 192
POOL_HALO = 16
DIL_PATTERNS = ((128, 1), (512, 4), (2048, 16))
N_GROUPS = 3
GROUP_W = 256
BAND = 128
N_A_LAYERS = 2
D_FF = 2816
ROPE_THETA = 10000.0
EPS = 1e-6
NEG = -1e30
SCALE = HEAD_DIM ** -0.5
N_DEV = 8

ADAM_LR = 0.001
ADAM_B1 = 0.9
ADAM_B2 = 0.999
ADAM_EPS = 1e-08
ADAM_WD = 0.01
ADAM_STEP = 10

VMEM_LIMIT_BYTES = 56 * 1024 * 1024
MESH = pl.DeviceIdType.MESH

NN = (((1,), (0,)), ((), ()))
NT = (((1,), (1,)), ((), ()))
TN = (((0,), (0,)), ((), ()))


def _params(sem=None):
    return pltpu.CompilerParams(dimension_semantics=sem, vmem_limit_bytes=VMEM_LIMIT_BYTES)


def _tile(n, cands):
    for c in cands:
        if n % c == 0:
            return c
    return n


def _sds(shape, dtype):
    return jax.ShapeDtypeStruct(tuple(shape), dtype)


def _rms_r(v):
    return lax.rsqrt(jnp.mean(v * v, axis=-1, keepdims=True) + EPS)


def rms_matmul(x, gain, w, *, name, out_dtype, transposed=False, after=None):
    M, K = x.shape
    N = w.shape[0] if transposed else w.shape[1]
    tm = min(1024, M)
    tn = _tile(N, (512, 256, 128))
    order = [] if after is None else [after]

    def body(x_ref, g_ref, w_ref, *refs):
        z_ref, h_ref = refs[len(order):]

        @pl.when(pl.program_id(1) == 0)
        def _():
            xv = x_ref[...]
            h_ref[...] = (xv * _rms_r(xv) * g_ref[...]).astype(bf16)

        z_ref[...] = lax.dot_general(h_ref[...], w_ref[...], NT if transposed else NN,
                                     preferred_element_type=f32).astype(z_ref.dtype)

    w_spec = pl.BlockSpec((tn, K), lambda i, j: (j, 0)) if transposed else pl.BlockSpec((K, tn), lambda i, j: (0, j))
    return pl.pallas_call(
        body, name=name, grid=(M // tm, N // tn),
        in_specs=[pl.BlockSpec((tm, K), lambda i, j: (i, 0)),
                  pl.BlockSpec((1, K), lambda i, j: (0, 0)),
                  w_spec] + [pl.BlockSpec(memory_space=pl.ANY)] * len(order),
        out_specs=[pl.BlockSpec((tm, tn), lambda i, j: (i, j)), pl.BlockSpec((tm, K), lambda i, j: (i, 0))],
        out_shape=[_sds((M, N), out_dtype), _sds((M, K), bf16)],
        compiler_params=_params(("parallel", "arbitrary")),
    )(x, gain, w, *order)


def matmul_rms_res(a, w, gain, res, *, name):
    M, K = a.shape
    N = w.shape[1]
    tm = min(512, M)

    def body(a_ref, w_ref, g_ref, r_ref, y_ref, x_ref):
        y = jnp.dot(a_ref[...], w_ref[...], preferred_element_type=f32)
        y_ref[...] = y.astype(bf16)
        x_ref[...] = r_ref[...] + y * _rms_r(y) * g_ref[...]

    row = pl.BlockSpec((tm, N), lambda i: (i, 0))
    return pl.pallas_call(
        body, name=name, grid=(M // tm,),
        in_specs=[pl.BlockSpec((tm, K), lambda i: (i, 0)),
                  pl.BlockSpec((K, N), lambda i: (0, 0)),
                  pl.BlockSpec((1, N), lambda i: (0, 0)),
                  row],
        out_specs=[row, row],
        out_shape=[_sds((M, N), bf16), _sds((M, N), f32)],
        compiler_params=_params(("parallel",)),
    )(a, w, gain, res)


def matmul(a, b, dims, *, name, out_dtype):
    if dims is TN:
        K, M = a.shape
        tm = _tile(M, (512, 256, 128))
        a_spec = pl.BlockSpec((K, tm), lambda i: (0, i))
    else:
        M, K = a.shape
        tm = _tile(M, (512, 256, 128))
        a_spec = pl.BlockSpec((tm, K), lambda i: (i, 0))
    N = b.shape[0] if dims is NT else b.shape[1]

    def body(a_ref, b_ref, o_ref):
        o_ref[...] = lax.dot_general(a_ref[...].astype(bf16), b_ref[...].astype(bf16), dims,
                                     preferred_element_type=f32).astype(o_ref.dtype)

    return pl.pallas_call(
        body, name=name, grid=(M // tm,),
        in_specs=[a_spec, pl.BlockSpec(b.shape, lambda i: (0, 0))],
        out_specs=pl.BlockSpec((tm, N), lambda i: (i, 0)),
        out_shape=_sds((M, N), out_dtype),
        compiler_params=_params(("parallel",)),
    )(a, b)


def rms_gate_up(x, gain, wt, *, name):
    M, K = x.shape
    tm = min(1024, M)
    tn = _tile(D_FF, (256, 128))
    nj = D_FF // tn

    def body(x_ref, gn_ref, wg_ref, wu_ref, g_ref, u_ref, a_ref, h_ref):
        @pl.when(pl.program_id(1) == 0)
        def _():
            xv = x_ref[...]
            h_ref[...] = (xv * _rms_r(xv) * gn_ref[...]).astype(bf16)

        h = h_ref[...]
        g = lax.dot_general(h, wg_ref[...], NT, preferred_element_type=f32)
        u = lax.dot_general(h, wu_ref[...], NT, preferred_element_type=f32)
        g_ref[...] = g.astype(bf16)
        u_ref[...] = u.astype(bf16)
        a_ref[...] = (g * (1.0 / (1.0 + jnp.exp(-g))) * u).astype(bf16)

    col = pl.BlockSpec((tm, tn), lambda i, j: (i, j))
    return pl.pallas_call(
        body, name=name, grid=(M // tm, nj),
        in_specs=[pl.BlockSpec((tm, K), lambda i, j: (i, 0)),
                  pl.BlockSpec((1, K), lambda i, j: (0, 0)),
                  pl.BlockSpec((tn, K), lambda i, j: (j, 0)),
                  pl.BlockSpec((tn, K), lambda i, j: (j + nj, 0))],
        out_specs=[col, col, col, pl.BlockSpec((tm, K), lambda i, j: (i, 0))],
        out_shape=[_sds((M, D_FF), bf16)] * 3 + [_sds((M, K), bf16)],
        compiler_params=_params(("parallel", "arbitrary")),
    )(x, gain, wt, wt)


def _rms_bwd_math(yv, gain, dn):
    r = _rms_r(yv)
    q = dn * gain
    dy = r * q - yv * (r * r * r) * jnp.mean(q * yv, axis=-1, keepdims=True)
    return dy, jnp.sum(dn * yv * r, axis=0, keepdims=True)


def _accumulate(ref, val):
    @pl.when(pl.program_id(0) == 0)
    def _():
        ref[...] = jnp.zeros_like(ref)

    ref[...] += val


def down_bwd(y, gain, dn, w_down, g, u, *, name, after=None):
    M, K = y.shape
    tm = min(512, M)
    order = [] if after is None else [after]

    def body(y_ref, gn_ref, dn_ref, w_ref, g_ref, u_ref, *refs):
        dy_ref, o_ref, dg_ref = refs[len(order):]
        dy, dgain = _rms_bwd_math(y_ref[...].astype(f32), gn_ref[...], dn_ref[...])
        dy = dy.astype(bf16)
        dy_ref[...] = dy
        _accumulate(dg_ref, dgain)
        da = lax.dot_general(dy, w_ref[...], NT, preferred_element_type=f32)
        g = g_ref[...].astype(f32)
        u = u_ref[...].astype(f32)
        s = 1.0 / (1.0 + jnp.exp(-g))
        o_ref[:, :D_FF] = (da * u * s * (1.0 + g * (1.0 - s))).astype(bf16)
        o_ref[:, D_FF:] = (da * g * s).astype(bf16)

    row = pl.BlockSpec((tm, K), lambda i: (i, 0))
    vec = pl.BlockSpec((1, K), lambda i: (0, 0))
    wide = pl.BlockSpec((tm, D_FF), lambda i: (i, 0))
    return pl.pallas_call(
        body, name=name, grid=(M // tm,),
        in_specs=[row, vec, row, pl.BlockSpec((D_FF, K), lambda i: (0, 0)), wide, wide]
        + [pl.BlockSpec(memory_space=pl.ANY)] * len(order),
        out_specs=[row, pl.BlockSpec((tm, 2 * D_FF), lambda i: (i, 0)), vec],
        out_shape=[_sds((M, K), bf16), _sds((M, 2 * D_FF), bf16), _sds((1, K), f32)],
        compiler_params=_params(("arbitrary",)),
    )(y, gain, dn, w_down, g, u, *order)


def rms_bwd_matmul(y, gain, dn, w, dims, *, name, after=None):
    M, K = y.shape
    N = w.shape[0] if dims is NT else w.shape[1]
    tm = min(512, M)
    order = [] if after is None else [after]

    def body(y_ref, gn_ref, dn_ref, w_ref, *refs):
        dy_ref, o_ref, dg_ref = refs[len(order):]
        dy, dgain = _rms_bwd_math(y_ref[...].astype(f32), gn_ref[...], dn_ref[...].astype(f32))
        dy = dy.astype(bf16)
        dy_ref[...] = dy
        _accumulate(dg_ref, dgain)
        o_ref[...] = lax.dot_general(dy, w_ref[...], dims, preferred_element_type=f32).astype(bf16)

    row = pl.BlockSpec((tm, K), lambda i: (i, 0))
    vec = pl.BlockSpec((1, K), lambda i: (0, 0))
    return pl.pallas_call(
        body, name=name, grid=(M // tm,),
        in_specs=[row, vec, row, pl.BlockSpec(w.shape, lambda i: (0, 0))]
        + [pl.BlockSpec(memory_space=pl.ANY)] * len(order),
        out_specs=[row, pl.BlockSpec((tm, N), lambda i: (i, 0)), vec],
        out_shape=[_sds((M, K), bf16), _sds((M, N), bf16), _sds((1, K), f32)],
        compiler_params=_params(("arbitrary",)),
    )(y, gain, dn, w, *order)


def matmul_rms_bwd(a, b, dims, y, gain, res, *, name):
    M, K = a.shape
    N = y.shape[1]
    tm = 256

    def body(a_ref, b_ref, y_ref, gn_ref, r_ref, dx_ref, dg_ref):
        dn = lax.dot_general(a_ref[...], b_ref[...], dims, preferred_element_type=f32)
        dy, dgain = _rms_bwd_math(y_ref[...], gn_ref[...], dn)
        dx_ref[...] = dy + r_ref[...]
        _accumulate(dg_ref, dgain)

    row = pl.BlockSpec((tm, N), lambda i: (i, 0))
    vec = pl.BlockSpec((1, N), lambda i: (0, 0))
    return pl.pallas_call(
        body, name=name, grid=(M // tm,),
        in_specs=[pl.BlockSpec((tm, K), lambda i: (i, 0)), pl.BlockSpec(b.shape, lambda i: (0, 0)), row, vec, row],
        out_specs=[row, vec],
        out_shape=[_sds((M, N), f32), _sds((1, N), f32)],
        compiler_params=_params(("arbitrary",)),
    )(a, b, y, gain, res)


def rms_bwd(y, gain, dn, res, *, name, out_dtype, after=None):
    M, N = y.shape
    tm = min(512, M)
    has_res = res is not None
    order = [] if after is None else [after]

    def body(*refs):
        y_ref, g_ref, dn_ref = refs[:3]
        r_ref = refs[3] if has_res else None
        dy_ref, dg_ref = refs[-2:]
        dy, dgain = _rms_bwd_math(y_ref[...].astype(f32), g_ref[...], dn_ref[...].astype(f32))
        if has_res:
            dy = dy + r_ref[...]
        dy_ref[...] = dy.astype(dy_ref.dtype)
        _accumulate(dg_ref, dgain)

    row = pl.BlockSpec((tm, N), lambda i: (i, 0))
    vec = pl.BlockSpec((1, N), lambda i: (0, 0))
    args = [y, gain, dn] + ([res] if has_res else []) + order
    return pl.pallas_call(
        body, name=name, grid=(M // tm,),
        in_specs=[row, vec, row] + ([row] if has_res else []) + [pl.BlockSpec(memory_space=pl.ANY)] * len(order),
        out_specs=[row, vec],
        out_shape=[_sds((M, N), out_dtype), _sds((1, N), f32)],
        compiler_params=_params(("arbitrary",)),
    )(*args)


def loss_head(x, target, *, name):
    M, N = x.shape
    tm = min(512, M)

    def body(x_ref, t_ref, dx_ref, l_ref):
        e = x_ref[...] - t_ref[...]
        dx_ref[...] = e * (1.0 / N)

        @pl.when(pl.program_id(0) == 0)
        def _():
            l_ref[...] = jnp.zeros_like(l_ref)

        l_ref[...] += jnp.sum(jnp.sum(e * e, axis=0, keepdims=True), axis=1, keepdims=True)

    row = pl.BlockSpec((tm, N), lambda i: (i, 0))
    return pl.pallas_call(
        body, name=name, grid=(M // tm,),
        in_specs=[row, row],
        out_specs=[row, pl.BlockSpec((8, 128), lambda i: (0, 0))],
        out_shape=[_sds((M, N), f32), _sds((8, 128), f32)],
        compiler_params=_params(("arbitrary",)),
    )(x, target)


def _pool_select(a1, a2, a3, a4):
    col = lax.broadcasted_iota(jnp.int32, (1, MAIN_W), 1) // POOL_GROUP
    return jnp.where(col == 0, a1, jnp.where(col == 1, a2, jnp.where(col == 2, a3, a4)))


def _pool_count(t):
    col = lax.broadcasted_iota(jnp.int32, (1, MAIN_W), 1) // POOL_GROUP
    win = jnp.where(col == 0, 2, jnp.where(col == 1, 4, jnp.where(col == 2, 8, 16)))
    return jnp.minimum(t + 1, win).astype(f32)


def pool_fwd(z, wbd, scale, *, name):
    M = z.shape[0]
    tm = 256
    nper = SEQ // tm
    hb = tm // POOL_HALO

    def body(zc_ref, zh_ref, w_ref, s_ref, p_ref, y_ref):
        i = pl.program_id(0)
        seq_blk = i % nper
        halo = jnp.where(seq_blk == 0, 0.0, zh_ref[...])
        u = zc_ref[...]
        ext = jnp.concatenate([halo, u], axis=0)
        a1 = ext + pltpu.roll(ext, 1, 0)
        a2 = a1 + pltpu.roll(a1, 2, 0)
        a3 = a2 + pltpu.roll(a2, 4, 0)
        a4 = a3 + pltpu.roll(a3, 8, 0)
        sums = _pool_select(a1, a2, a3, a4)[POOL_HALO:]
        t = seq_blk * tm + lax.broadcasted_iota(jnp.int32, (tm, 1), 0)
        p = (sums / _pool_count(t) - u).astype(bf16)
        p_ref[...] = p
        y_ref[...] = (jnp.dot(p, w_ref[...], preferred_element_type=f32) * s_ref[...]).astype(bf16)

    return pl.pallas_call(
        body, name=name, grid=(M // tm,),
        in_specs=[pl.BlockSpec((tm, MAIN_W), lambda i: (i, 0)),
                  pl.BlockSpec((POOL_HALO, MAIN_W), lambda i: (jnp.maximum(i * hb - 1, 0), 0)),
                  pl.BlockSpec((MAIN_W, MAIN_W), lambda i: (0, 0)),
                  pl.BlockSpec((1, MAIN_W), lambda i: (0, 0))],
        out_specs=[pl.BlockSpec((tm, MAIN_W), lambda i: (i, 0)),
                   pl.BlockSpec((tm, MAIN_W), lambda i: (i, 0))],
        out_shape=[_sds((M, MAIN_W), bf16), _sds((M, D_MODEL), bf16)],
        compiler_params=_params(("parallel",)),
    )(z, z, wbd, scale)


def pool_bwd(dyc, p, wbd, scale, *, name):
    M = p.shape[0]
    tm = 256
    nper = SEQ // tm
    hb = tm // POOL_HALO
    last_hb = M // POOL_HALO - 1

    def body(dy_ref, dyh_ref, p_ref, w_ref, s_ref, dz_ref, dw_ref, ds_ref):
        i = pl.program_id(0)
        seq_blk = i % nper
        dy = dy_ref[...].astype(f32)
        pv = p_ref[...]
        w = w_ref[...]
        sc = s_ref[...]

        @pl.when(i == 0)
        def _():
            dw_ref[...] = jnp.zeros_like(dw_ref)
            ds_ref[...] = jnp.zeros_like(ds_ref)

        v = jnp.dot(pv, w, preferred_element_type=f32)
        ds_ref[...] += jnp.sum(dy * v, axis=0, keepdims=True)
        dv = (dy * sc).astype(bf16)
        dw_ref[...] += lax.dot_general(pv, dv, TN, preferred_element_type=f32)
        dp = lax.dot_general(dv, w, NT, preferred_element_type=f32)
        dvh = jnp.where(seq_blk == nper - 1, 0.0, dyh_ref[...].astype(f32) * sc).astype(bf16)
        dph = lax.dot_general(dvh, w, NT, preferred_element_type=f32)
        ext = jnp.concatenate([dp, dph], axis=0)
        n = tm + POOL_HALO
        t = seq_blk * tm + lax.broadcasted_iota(jnp.int32, (n, 1), 0)
        e = ext / _pool_count(t)
        b1 = e + pltpu.roll(e, n - 1, 0)
        b2 = b1 + pltpu.roll(b1, n - 2, 0)
        b3 = b2 + pltpu.roll(b2, n - 4, 0)
        b4 = b3 + pltpu.roll(b3, n - 8, 0)
        dz_ref[...] = (_pool_select(b1, b2, b3, b4)[:tm] - dp).astype(dz_ref.dtype)

    return pl.pallas_call(
        body, name=name, grid=(M // tm,),
        in_specs=[pl.BlockSpec((tm, MAIN_W), lambda i: (i, 0)),
                  pl.BlockSpec((POOL_HALO, MAIN_W), lambda i: (jnp.minimum((i + 1) * hb, last_hb), 0)),
                  pl.BlockSpec((tm, MAIN_W), lambda i: (i, 0)),
                  pl.BlockSpec((MAIN_W, MAIN_W), lambda i: (0, 0)),
                  pl.BlockSpec((1, MAIN_W), lambda i: (0, 0))],
        out_specs=[pl.BlockSpec((tm, MAIN_W), lambda i: (i, 0)),
                   pl.BlockSpec((MAIN_W, MAIN_W), lambda i: (0, 0)),
                   pl.BlockSpec((1, MAIN_W), lambda i: (0, 0))],
        out_shape=[_sds((M, D_MODEL), bf16), _sds((MAIN_W, MAIN_W), f32), _sds((1, MAIN_W), f32)],
        compiler_params=_params(("arbitrary",)),
    )(dyc, dyc, p, wbd, scale)


def _mem_probs(q, kv, h):
    hs = slice(h * HEAD_DIM, (h + 1) * HEAD_DIM)
    qh = q[:, hs]
    kh = kv[:, hs]
    s = lax.dot_general(qh, kh, NT, preferred_element_type=f32) * SCALE
    m = jnp.max(s, axis=-1, keepdims=True)
    e = jnp.exp(s - m)
    return qh, kh, e / jnp.sum(e, axis=-1, keepdims=True)


def memattn_fwd(z, kvm, ycat, *, name, n_seq):
    M = z.shape[0]
    tq = 512
    nq = SEQ // tq

    def body(q_ref, kv_ref, _, o_ref):
        q = q_ref[...].astype(bf16)
        kv = kv_ref[...]
        outs = []
        for h in range(N_MEM_HEADS):
            _, _, p = _mem_probs(q, kv, h)
            vh = kv[:, MEM_W + h * HEAD_DIM: MEM_W + (h + 1) * HEAD_DIM]
            outs.append(jnp.dot(p.astype(bf16), vh, preferred_element_type=f32))
        o_ref[...] = jnp.concatenate(outs, axis=1).astype(bf16)

    return pl.pallas_call(
        body, name=name, grid=(n_seq, nq),
        in_specs=[pl.BlockSpec((tq, MEM_W), lambda b, i: (b * nq + i, 3)),
                  pl.BlockSpec((N_MEM, 2 * MEM_W), lambda b, i: (b, 0)),
                  pl.BlockSpec(memory_space=pl.ANY)],
        out_specs=pl.BlockSpec((tq, MEM_W), lambda b, i: (b * nq + i, 3)),
        out_shape=_sds((M, D_MODEL), bf16),
        input_output_aliases={2: 0},
        compiler_params=_params(("parallel", "parallel")),
    )(z, kvm, ycat)


def memattn_bwd(z, kvm, dyc, dz, *, name, n_seq):
    M = z.shape[0]
    tq = 512
    nq = SEQ // tq

    def body(q_ref, kv_ref, dy_ref, _, dq_ref, dkv_ref):
        q = q_ref[...].astype(bf16)
        kv = kv_ref[...]
        dy = dy_ref[...].astype(bf16)
        dqs, dks, dvs = [], [], []
        for h in range(N_MEM_HEADS):
            hs = slice(h * HEAD_DIM, (h + 1) * HEAD_DIM)
            qh, kh, p = _mem_probs(q, kv, h)
            vh = kv[:, MEM_W + h * HEAD_DIM: MEM_W + (h + 1) * HEAD_DIM]
            dyh = dy[:, hs]
            dvs.append(lax.dot_general(p.astype(bf16), dyh, TN, preferred_element_type=f32))
            dp = lax.dot_general(dyh, vh, NT, preferred_element_type=f32)
            ds = (p * (dp - jnp.sum(dp * p, axis=-1, keepdims=True)) * SCALE).astype(bf16)
            dqs.append(jnp.dot(ds, kh, preferred_element_type=f32))
            dks.append(lax.dot_general(ds, qh, TN, preferred_element_type=f32))
        dq_ref[...] = jnp.concatenate(dqs, axis=1).astype(bf16)

        @pl.when(pl.program_id(1) == 0)
        def _():
            dkv_ref[...] = jnp.zeros_like(dkv_ref)

        dkv_ref[...] += jnp.concatenate(dks + dvs, axis=1)

    return pl.pallas_call(
        body, name=name, grid=(n_seq, nq),
        in_specs=[pl.BlockSpec((tq, MEM_W), lambda b, i: (b * nq + i, 3)),
                  pl.BlockSpec((N_MEM, 2 * MEM_W), lambda b, i: (b, 0)),
                  pl.BlockSpec((tq, MEM_W), lambda b, i: (b * nq + i, 3)),
                  pl.BlockSpec(memory_space=pl.ANY)],
        out_specs=[pl.BlockSpec((tq, MEM_W), lambda b, i: (b * nq + i, 3)),
                   pl.BlockSpec((N_MEM, 2 * MEM_W), lambda b, i: (b, 0))],
        out_shape=[_sds((M, D_MODEL), bf16), _sds((n_seq * N_MEM, 2 * MEM_W), f32)],
        input_output_aliases={3: 0},
        compiler_params=_params(("parallel", "arbitrary")),
    )(z, kvm, dyc, dz)


def rope_tables(pos, *, name):
    M = pos.shape[0]
    tm = min(1024, M)
    half = HEAD_DIM // 2
    inv = ROPE_THETA ** (-np.arange(half, dtype=np.float64) / half)
    inv128 = jnp.asarray(np.tile(inv, 4)[None, :], f32)
    sign128 = jnp.asarray(np.tile(np.concatenate([-np.ones(half), np.ones(half)]), 2)[None, :], f32)

    def body(p_ref, f_ref, s_ref, cos_ref, sin_ref):
        ang = p_ref[...] * f_ref[...]
        cos_ref[...] = jnp.cos(ang)
        sin_ref[...] = jnp.sin(ang) * s_ref[...]

    return pl.pallas_call(
        body, name=name, grid=(M // tm,),
        in_specs=[pl.BlockSpec((tm, 1), lambda i: (i, 0)),
                  pl.BlockSpec((1, 128), lambda i: (0, 0)),
                  pl.BlockSpec((1, 128), lambda i: (0, 0))],
        out_specs=[pl.BlockSpec((tm, 128), lambda i: (i, 0)),
                   pl.BlockSpec((tm, 128), lambda i: (i, 0))],
        out_shape=[_sds((M, 128), f32), _sds((M, 128), f32)],
        compiler_params=_params(("parallel",)),
    )(pos, inv128, sign128)


def _swap_halves(x):
    w = x.shape[1]
    first = (lax.broadcasted_iota(jnp.int32, (1, w), 1) % HEAD_DIM) < (HEAD_DIM // 2)
    return jnp.where(first, pltpu.roll(x, w - HEAD_DIM // 2, 1), pltpu.roll(x, HEAD_DIM // 2, 1))


def rope_fwd(src, cos, sin, *, name):
    M = src.shape[0]
    tm = min(512, M)

    def body(x_ref, c_ref, s_ref, o_ref):
        x = x_ref[...].astype(f32)
        c = jnp.tile(c_ref[...], (1, MAIN_W // 128))
        s = jnp.tile(s_ref[...], (1, MAIN_W // 128))
        o_ref[...] = x * c + _swap_halves(x) * s

    return pl.pallas_call(
        body, name=name, grid=(M // tm,),
        in_specs=[pl.BlockSpec((tm, MAIN_W), lambda i: (i, 0)),
                  pl.BlockSpec((tm, 128), lambda i: (i, 0)),
                  pl.BlockSpec((tm, 128), lambda i: (i, 0))],
        out_specs=pl.BlockSpec((tm, MAIN_W), lambda i: (i, 0)),
        out_shape=_sds((M, MAIN_W), f32),
        compiler_params=_params(("parallel",)),
    )(src, cos, sin)


def group_sum(groups, cos, sin, *, name, rotate, width, col_block=0, into=None):
    M = groups[0][0].shape[0]
    tm = min(512, M)
    counts = [len(g) for g in groups]
    flat = [a for g in groups for a in g]
    extra = [] if into is None else [into]

    def body(*refs):
        part_refs = refs[:len(flat)]
        c_ref, s_ref = refs[len(flat):len(flat) + 2]
        o_ref = refs[-1]
        cols, k = [], 0
        for n in counts:
            acc = part_refs[k][...]
            for r in part_refs[k + 1:k + n]:
                acc = acc + r[...]
            cols.append(acc)
            k += n
        d = jnp.concatenate(cols, axis=1)
        if rotate:
            c = jnp.tile(c_ref[...], (1, MAIN_W // 128))
            s = jnp.tile(s_ref[...], (1, MAIN_W // 128))
            d = d * c - _swap_halves(d) * s
        o_ref[...] = d.astype(bf16)

    part = pl.BlockSpec((tm, GROUP_W), lambda i: (i, 0))
    tab = pl.BlockSpec((tm, 128), lambda i: (i, 0))
    return pl.pallas_call(
        body, name=name, grid=(M // tm,),
        in_specs=[part] * len(flat) + [tab, tab] + [pl.BlockSpec(memory_space=pl.ANY)] * len(extra),
        out_specs=pl.BlockSpec((tm, MAIN_W), lambda i: (i, col_block)),
        out_shape=_sds((M, width), bf16),
        input_output_aliases={len(flat) + 2: 0} if extra else {},
        compiler_params=_params(("parallel",)),
    )(*flat, cos, sin, *extra)


PAIR_W = 2 * HEAD_DIM
MIN_BLOCKS = 8


def _dil_geometry(dil):
    nsub = max(dil, MIN_BLOCKS)
    tb = BAND * nsub
    return nsub, tb, SEQ // tb


def _rows(ref, sub, dil):
    if dil == 1:
        return ref[sub * BAND:(sub + 1) * BAND, :]
    nl, r = divmod(sub, dil)
    return ref[pl.ds(nl * BAND * dil + r, BAND, stride=dil), :]


def _store_rows(ref, sub, dil, val):
    if dil == 1:
        ref[sub * BAND:(sub + 1) * BAND, :] = val
    else:
        nl, r = divmod(sub, dil)
        ref[pl.ds(nl * BAND * dil + r, BAND, stride=dil), :] = val


def _keys(prev_ref, own_ref, sub, dil):
    nsub = own_ref.shape[0] // BAND
    if sub >= dil:
        prev = _rows(own_ref, sub - dil, dil)
    elif prev_ref is None:
        return _rows(own_ref, sub, dil)
    else:
        prev = _rows(prev_ref, nsub - dil + sub, dil)
    return jnp.concatenate([prev, _rows(own_ref, sub, dil)], axis=0)


def _band_mask(nkeys, has_prev):
    i = lax.broadcasted_iota(jnp.int32, (BAND, nkeys), 0)
    j = lax.broadcasted_iota(jnp.int32, (BAND, nkeys), 1)
    if nkeys == BAND:
        return j <= i
    return (j >= i) & (j <= i + BAND) & (has_prev | (j >= BAND))


def _first_head():
    return lax.broadcasted_iota(jnp.int32, (1, PAIR_W), 1) < HEAD_DIM


def _col(x, hh):
    return x[:, hh * HEAD_DIM:hh * HEAD_DIM + 1]


def _pair_spec(tb, nblk, col0, which):
    def idx(b, p, i):
        if which < 0:
            i = jnp.maximum(i - 1, 0)
        elif which > 0:
            i = jnp.minimum(i + 1, nblk - 1)
        return (b * nblk + i, col0 + p)
    return pl.BlockSpec((tb, PAIR_W), idx)


def dil_fwd(q, k, kv, g, dil, *, name, n_seq):
    M = q.shape[0]
    nsub, tb, nblk = _dil_geometry(dil)
    with_prev = nblk > 1

    def body(*refs):
        if with_prev:
            q_ref, ko_ref, vo_ref, kp_ref, vp_ref, o_ref, l_ref = refs
        else:
            (q_ref, ko_ref, vo_ref, o_ref, l_ref), kp_ref, vp_ref = refs, None, None
        first = _first_head()
        blk = pl.program_id(2)
        for sub in range(nsub):
            qs = _rows(q_ref, sub, dil) * SCALE
            kc = _keys(kp_ref, ko_ref, sub, dil).astype(bf16)
            vc = _keys(vp_ref, vo_ref, sub, dil).astype(bf16)
            has_prev = True if sub >= dil else blk > 0
            mask = _band_mask(kc.shape[0], has_prev)
            outs, lses = [], []
            for hh in range(2):
                qm = jnp.where(first if hh == 0 else ~first, qs, 0.0).astype(bf16)
                s = jnp.where(mask, lax.dot_general(qm, kc, NT, preferred_element_type=f32), NEG)
                m = jnp.max(s, axis=-1, keepdims=True)
                e = jnp.exp(s - m)
                l = jnp.sum(e, axis=-1, keepdims=True)
                outs.append(jnp.dot(e.astype(bf16), vc, preferred_element_type=f32) * (1.0 / l))
                lses.append(jnp.broadcast_to(m + jnp.log(l), (BAND, PAIR_W)))
            _store_rows(o_ref, sub, dil, jnp.where(first, outs[0], outs[1]))
            _store_rows(l_ref, sub, dil, jnp.where(first, lses[0], lses[1]))

    ins = [(q, 2 * g, 0), (k, 2 * g, 0), (kv, 6 + 2 * g, 0)]
    if with_prev:
        ins += [(k, 2 * g, -1), (kv, 6 + 2 * g, -1)]
    out = _pair_spec(tb, nblk, 0, 0)
    return pl.pallas_call(
        body, name=name, grid=(n_seq, 2, nblk),
        in_specs=[_pair_spec(tb, nblk, c, w) for _, c, w in ins],
        out_specs=[out, out],
        out_shape=[_sds((M, GROUP_W), f32)] * 2,
        compiler_params=_params(("parallel", "parallel", "arbitrary")),
    )(*[a for a, _, _ in ins])


def combine_fwd(os_, lses, *, name):
    M = os_[0].shape[0]
    tm = min(512, M)

    def body(o0, o1, o2, l0, l1, l2, y_ref):
        ls = [l0[...], l1[...], l2[...]]
        m = jnp.maximum(jnp.maximum(ls[0], ls[1]), ls[2])
        es = [jnp.exp(l - m) for l in ls]
        inv = 1.0 / (es[0] + es[1] + es[2])
        y_ref[...] = jnp.concatenate([o[...] * e * inv for o, e in zip((o0, o1, o2), es)], axis=1).astype(bf16)

    part = pl.BlockSpec((tm, GROUP_W), lambda i: (i, 0))
    return pl.pallas_call(
        body, name=name, grid=(M // tm,),
        in_specs=[part] * 6,
        out_specs=pl.BlockSpec((tm, MAIN_W), lambda i: (i, 0)),
        out_shape=_sds((M, D_MODEL), bf16),
        compiler_params=_params(("parallel",)),
    )(*os_, *lses)


def combine_bwd(dyc, os_, lses, *, name):
    M = os_[0].shape[0]
    tm = min(512, M)

    def body(dy_ref, o0, o1, o2, l0, l1, l2, d0, d1, d2, c0, c1, c2):
        r = lax.broadcasted_iota(jnp.int32, (GROUP_W, GROUP_W), 0) // HEAD_DIM
        c = lax.broadcasted_iota(jnp.int32, (GROUP_W, GROUP_W), 1) // HEAD_DIM
        ones = (r == c).astype(f32)
        dy = dy_ref[...].astype(f32)
        ls = [l0[...], l1[...], l2[...]]
        m = jnp.maximum(jnp.maximum(ls[0], ls[1]), ls[2])
        es = [jnp.exp(l - m) for l in ls]
        inv = 1.0 / (es[0] + es[1] + es[2])
        total = 0.0
        alphas = []
        for g, (o, e, d_ref) in enumerate(zip((o0, o1, o2), es, (d0, d1, d2))):
            a = e * inv
            dyg = dy[:, g * GROUP_W:(g + 1) * GROUP_W]
            d_ref[...] = dyg * a
            dsum = jnp.dot(dyg * o[...], ones, precision=lax.Precision.HIGHEST, preferred_element_type=f32)
            total = total + a * dsum
            alphas.append(a)
        for a, c_ref in zip(alphas, (c0, c1, c2)):
            c_ref[...] = -a * total

    part = pl.BlockSpec((tm, GROUP_W), lambda i: (i, 0))
    outs = pl.pallas_call(
        body, name=name, grid=(M // tm,),
        in_specs=[pl.BlockSpec((tm, MAIN_W), lambda i: (i, 0))] + [part] * 6,
        out_specs=[part] * 6,
        out_shape=[_sds((M, GROUP_W), f32)] * 6,
        compiler_params=_params(("parallel",)),
    )(dyc, *os_, *lses)
    return outs[:3], outs[3:]


def dil_bwd_dq(q, k, kv, do, cc, lse, g, dil, *, name, n_seq):
    M = q.shape[0]
    nsub, tb, nblk = _dil_geometry(dil)
    with_prev = nblk > 1

    def body(*refs):
        if with_prev:
            q_ref, ko_ref, vo_ref, do_ref, c_ref, l_ref, kp_ref, vp_ref, dq_ref = refs
        else:
            (q_ref, ko_ref, vo_ref, do_ref, c_ref, l_ref, dq_ref), kp_ref, vp_ref = refs, None, None
        first = _first_head()
        blk = pl.program_id(2)
        for sub in range(nsub):
            qs = _rows(q_ref, sub, dil) * SCALE
            dos = _rows(do_ref, sub, dil)
            cs = _rows(c_ref, sub, dil)
            ls = _rows(l_ref, sub, dil)
            kc = _keys(kp_ref, ko_ref, sub, dil).astype(bf16)
            vc = _keys(vp_ref, vo_ref, sub, dil).astype(bf16)
            has_prev = True if sub >= dil else blk > 0
            mask = _band_mask(kc.shape[0], has_prev)
            outs = []
            for hh in range(2):
                lm = first if hh == 0 else ~first
                qm = jnp.where(lm, qs, 0.0).astype(bf16)
                dom = jnp.where(lm, dos, 0.0).astype(bf16)
                s = jnp.where(mask, lax.dot_general(qm, kc, NT, preferred_element_type=f32), NEG)
                p = jnp.exp(s - _col(ls, hh))
                dp = lax.dot_general(dom, vc, NT, preferred_element_type=f32)
                ds = (p * (dp + _col(cs, hh))).astype(bf16)
                outs.append(jnp.dot(ds, kc, preferred_element_type=f32) * SCALE)
            _store_rows(dq_ref, sub, dil, jnp.where(first, outs[0], outs[1]))

    ins = [(q, 2 * g, 0), (k, 2 * g, 0), (kv, 6 + 2 * g, 0), (do, 0, 0), (cc, 0, 0), (lse, 0, 0)]
    if with_prev:
        ins += [(k, 2 * g, -1), (kv, 6 + 2 * g, -1)]
    return pl.pallas_call(
        body, name=name, grid=(n_seq, 2, nblk),
        in_specs=[_pair_spec(tb, nblk, c, w) for _, c, w in ins],
        out_specs=_pair_spec(tb, nblk, 0, 0),
        out_shape=_sds((M, GROUP_W), f32),
        compiler_params=_params(("parallel", "parallel", "arbitrary")),
    )(*[a for a, _, _ in ins])


def dil_bwd_dkv(q, k, kv, do, cc, lse, g, dil, *, name, n_seq):
    M = q.shape[0]
    nsub, tb, nblk = _dil_geometry(dil)
    with_next = nblk > 1

    def body(*refs):
        if with_next:
            (k_ref, v_ref, q_ref, do_ref, c_ref, l_ref, qn_ref, don_ref, cn_ref, ln_ref, dk_ref, dv_ref) = refs
        else:
            (k_ref, v_ref, q_ref, do_ref, c_ref, l_ref, dk_ref, dv_ref) = refs
        first = _first_head()
        blk = pl.program_id(2)
        i = lax.broadcasted_iota(jnp.int32, (BAND, BAND), 0)
        j = lax.broadcasted_iota(jnp.int32, (BAND, BAND), 1)
        for sub in range(nsub):
            ks = _rows(k_ref, sub, dil)
            vs = _rows(v_ref, sub, dil)
            sets = [((q_ref, do_ref, c_ref, l_ref), sub, j <= i)]
            if sub + dil < nsub:
                sets.append(((q_ref, do_ref, c_ref, l_ref), sub + dil, j >= i))
            elif with_next:
                sets.append(((qn_ref, don_ref, cn_ref, ln_ref), sub + dil - nsub, (j >= i) & (blk + 1 < nblk)))
            loaded = [(tuple(_rows(r, sq, dil) for r in rs), mask) for rs, sq, mask in sets]
            dks, dvs = [], []
            for hh in range(2):
                lm = first if hh == 0 else ~first
                km = jnp.where(lm, ks, 0.0).astype(bf16)
                vm = jnp.where(lm, vs, 0.0).astype(bf16)
                dk = jnp.zeros((BAND, PAIR_W), f32)
                dv = jnp.zeros((BAND, PAIR_W), f32)
                for (qs, dos, cs, ls), mask in loaded:
                    qb = qs.astype(bf16)
                    dob = dos.astype(bf16)
                    s = lax.dot_general(qb, km, NT, preferred_element_type=f32) * SCALE
                    p = jnp.exp(jnp.where(mask, s, NEG) - _col(ls, hh))
                    dv = dv + lax.dot_general(p.astype(bf16), dob, TN, preferred_element_type=f32)
                    dp = lax.dot_general(dob, vm, NT, preferred_element_type=f32)
                    ds = (p * (dp + _col(cs, hh)) * SCALE).astype(bf16)
                    dk = dk + lax.dot_general(ds, qb, TN, preferred_element_type=f32)
                dks.append(dk)
                dvs.append(dv)
            _store_rows(dk_ref, sub, dil, jnp.where(first, dks[0], dks[1]))
            _store_rows(dv_ref, sub, dil, jnp.where(first, dvs[0], dvs[1]))

    ins = [(k, 2 * g, 0), (kv, 6 + 2 * g, 0), (q, 2 * g, 0), (do, 0, 0), (cc, 0, 0), (lse, 0, 0)]
    if with_next:
        ins += [(q, 2 * g, 1), (do, 0, 1), (cc, 0, 1), (lse, 0, 1)]
    out = _pair_spec(tb, nblk, 0, 0)
    return pl.pallas_call(
        body, name=name, grid=(n_seq, 2, nblk),
        in_specs=[_pair_spec(tb, nblk, c, w) for _, c, w in ins],
        out_specs=[out, out],
        out_shape=[_sds((M, GROUP_W), f32)] * 2,
        compiler_params=_params(("parallel", "parallel", "arbitrary")),
    )(*[a for a, _, _ in ins])


def _blockdiag(wp):
    out = jnp.zeros((MAIN_W, MAIN_W), wp.dtype)
    for gi in range(len(POOL_WINDOWS)):
        sl = slice(gi * POOL_GROUP, (gi + 1) * POOL_GROUP)
        out = out.at[sl, sl].set(wp[gi])
    return out


def _unblockdiag(w):
    return jnp.stack([w[gi * POOL_GROUP:(gi + 1) * POOL_GROUP, gi * POOL_GROUP:(gi + 1) * POOL_GROUP]
                      for gi in range(len(POOL_WINDOWS))])


def local_step(x, mem, positions, target, P, layer_weights, kv_weight, emit_grads):
    n_seq = x.shape[0]
    M = n_seq * SEQ
    xs = x.reshape(M, D_MODEL)
    mems = mem.reshape(n_seq * N_MEM, D_MODEL)
    pos = positions.reshape(M, 1).astype(f32)
    cos, sin = rope_tables(pos, name="rope_tables")
    gains = P["norm_gains"]

    def gain(l, k):
        return gains[l, k].reshape(1, D_MODEL)

    saved = []
    kvs = None
    for l in range(DEPTH):
        W, started = layer_weights(l, "mix", xs)
        sv = {"x": xs, "W": W}
        z, h1 = rms_matmul(xs, gain(l, 0), W["w_in"], name=f"l{l}_in", out_dtype=f32, after=started)
        kvm, mn = rms_matmul(mems, P["mem_norm"][l].reshape(1, D_MODEL), W["w_mem_kv"],
                             name=f"l{l}_memkv", out_dtype=bf16)
        sv.update(z=z, h1=h1, kvm=kvm, mn=mn)
        if l < N_A_LAYERS:
            wbd = _blockdiag(P["w_pool"][l].astype(bf16))
            psc = P["pool_scale"][l].reshape(1, MAIN_W)
            p, y_main = pool_fwd(z, wbd, psc, name=f"l{l}_pool")
            sv.update(p=p, wbd=wbd, psc=psc)
        else:
            qrot = rope_fwd(z, cos, sin, name=f"l{l}_ropeq")
            os_, lses = [], []
            for g, (_, dil) in enumerate(DIL_PATTERNS):
                o, lse = dil_fwd(qrot, kvs["krot"], kvs["kv"], g, dil, name=f"l{l}_dil{g}", n_seq=n_seq)
                os_.append(o)
                lses.append(lse)
            y_main = combine_fwd(os_, lses, name=f"l{l}_comb")
            sv.update(qrot=qrot, os=os_, lses=lses)
        ycat = memattn_fwd(z, kvm, y_main, name=f"l{l}_memattn", n_seq=n_seq)
        y, x1 = matmul_rms_res(ycat, W["w_out"], gain(l, 1), xs, name=f"l{l}_out")
        W.update(layer_weights(l, "gu", x1)[0])
        fg, fu, a, h2 = rms_gate_up(x1, gain(l, 2), W["w_gate_up"], name=f"l{l}_gu")
        W.update(layer_weights(l, "down", a)[0])
        y2, x2 = matmul_rms_res(a, W["w_down"], gain(l, 3), x1, name=f"l{l}_down")
        sv.update(ycat=ycat, y=y, x1=x1, fg=fg, fu=fu, h2=h2, a=a, y2=y2)
        saved.append(sv)
        xs = x2
        if l == N_A_LAYERS - 1:
            w_kv = kv_weight(xs)
            kv, hkv = rms_matmul(xs, P["kv_norm"].reshape(1, D_MODEL), w_kv, name="kv_proj", out_dtype=f32,
                                 transposed=True)
            krot = rope_fwd(kv, cos, sin, name="ropek")
            kvs = {"kv": kv, "hkv": hkv, "krot": krot, "x": xs, "w_kv": w_kv}

    dx, sq = loss_head(xs, target.reshape(M, D_MODEL), name="loss_head")

    G = {"mem_norm": [None] * DEPTH, "norm_gains": [[None] * 4 for _ in range(DEPTH)],
         "pool_scale": [None] * N_A_LAYERS}
    dk_parts = [[] for _ in range(N_GROUPS)]
    dv_parts = [[] for _ in range(N_GROUPS)]
    emitted = None

    for l in reversed(range(DEPTH)):
        sv = saved[l]
        W = sv["W"]
        gw = {}
        dy2, dgu, G["norm_gains"][l][3] = down_bwd(sv["y2"], gain(l, 3), dx, W["w_down"], sv["fg"], sv["fu"],
                                                   name=f"l{l}_b_dgu", after=emitted)
        gw["w_down"] = matmul(sv["a"], dy2, TN, name=f"l{l}_b_wd", out_dtype=bf16)
        dx1, G["norm_gains"][l][2] = matmul_rms_bwd(dgu, W["w_gate_up"], NN, sv["x1"], gain(l, 2), dx,
                                                    name=f"l{l}_b_dh2")
        gw["w_gate_up"] = matmul(dgu, sv["h2"], TN, name=f"l{l}_b_wgu", out_dtype=bf16)
        emitted = emit_grads(l, "ffn", gw)
        gw = {}
        dy, dycat, G["norm_gains"][l][1] = rms_bwd_matmul(sv["y"], gain(l, 1), dx1, W["w_out"], NT,
                                                          name=f"l{l}_b_dycat", after=emitted)
        gw["w_out"] = matmul(sv["ycat"], dy, TN, name=f"l{l}_b_wout", out_dtype=bf16)
        if l < N_A_LAYERS:
            dz, dwbd, dps = pool_bwd(dycat, sv["p"], sv["wbd"], sv["psc"], name=f"l{l}_b_pool")
            gw["w_pool"] = _unblockdiag(dwbd).reshape(MAIN_W, POOL_GROUP).astype(bf16)
            G["pool_scale"][l] = dps.reshape(MAIN_W)
        else:
            dos, ccs = combine_bwd(dycat, sv["os"], sv["lses"], name=f"l{l}_b_comb")
            dqs = []
            for g, (_, dil) in enumerate(DIL_PATTERNS):
                args = (sv["qrot"], kvs["krot"], kvs["kv"], dos[g], ccs[g], sv["lses"][g], g, dil)
                dqs.append([dil_bwd_dq(*args, name=f"l{l}_b_dq{g}", n_seq=n_seq)])
                dk, dv = dil_bwd_dkv(*args, name=f"l{l}_b_dkv{g}", n_seq=n_seq)
                dk_parts[g].append(dk)
                dv_parts[g].append(dv)
            dz = group_sum(dqs, cos, sin, name=f"l{l}_b_ropeq", rotate=True, width=D_MODEL)
        dz, dkvm = memattn_bwd(sv["z"], sv["kvm"], dycat, dz, name=f"l{l}_b_memattn", n_seq=n_seq)
        dmn = matmul(dkvm, W["w_mem_kv"], NT, name=f"l{l}_b_dmn", out_dtype=bf16)
        gw["w_mem_kv"] = matmul(sv["mn"], dkvm, TN, name=f"l{l}_b_wmkv", out_dtype=bf16)
        _, G["mem_norm"][l] = rms_bwd(mems, P["mem_norm"][l].reshape(1, D_MODEL), dmn, None,
                                      name=f"l{l}_b_nmem", out_dtype=bf16)
        gw["w_in"] = matmul(sv["h1"], dz, TN, name=f"l{l}_b_win", out_dtype=bf16)
        dx, G["norm_gains"][l][0] = matmul_rms_bwd(dz, W["w_in"], NT, sv["x"], gain(l, 0), dx1, name=f"l{l}_b_dh1")
        if l == N_A_LAYERS:
            dkv = group_sum(dk_parts, cos, sin, name="b_ropek", rotate=True, width=2 * MAIN_W)
            dkv = group_sum(dv_parts, cos, sin, name="b_sumv", rotate=False, width=2 * MAIN_W, col_block=1, into=dkv)
            gw["w_kv"] = matmul(dkv, kvs["hkv"], TN, name="b_wkv", out_dtype=bf16)
            dx, gkn = matmul_rms_bwd(dkv, kvs["w_kv"], NN, kvs["x"], P["kv_norm"].reshape(1, D_MODEL), dx,
                                     name="b_dhkv")
            G["kv_norm"] = gkn.reshape(D_MODEL)
        emitted = emit_grads(l, "mix", gw)

    small = {"pool_scale": jnp.stack(G["pool_scale"]),
             "mem_norm": jnp.concatenate(G["mem_norm"], axis=0),
             "norm_gains": jnp.stack([jnp.concatenate(r, axis=0) for r in G["norm_gains"]]),
             "kv_norm": G["kv_norm"]}
    return sq[0, 0], dx.reshape(n_seq, SEQ, D_MODEL), small, emitted


def _peer(k):
    x, y, c = lax.axis_index("x"), lax.axis_index("y"), lax.axis_index("c")
    px = 1 - x if k & 4 else x
    py = 1 - y if k & 2 else y
    pc = 1 - c if k & 1 else c
    return (px, py, pc), 4 * px + 2 * py + pc


def _my_index():
    return 4 * lax.axis_index("x") + 2 * lax.axis_index("y") + lax.axis_index("c")


def _src_for(kinds, in_refs, i, idx):
    return in_refs[i] if kinds[i] == "gather" else in_refs[i].at[idx]


def _local_copies(kinds, in_refs, out_refs, local_sems):
    me = _my_index()
    return [pltpu.make_async_copy(_src_for(kinds, in_refs, i, me), out_refs[i].at[me], local_sems.at[i])
            for i in range(len(kinds))]


def _remote_copies(kinds, in_refs, out_refs, send_sems, recv_sems, *, arriving):
    me = _my_index()
    copies = []
    for k in range(1, N_DEV):
        dev, idx = _peer(k)
        for i in range(len(kinds)):
            j = i * (N_DEV - 1) + k - 1
            copies.append(pltpu.make_async_remote_copy(
                src_ref=_src_for(kinds, in_refs, i, idx), dst_ref=out_refs[i].at[idx if arriving else me],
                send_sem=send_sems.at[j], recv_sem=recv_sems.at[j], device_id=dev, device_id_type=MESH))
    return copies


def _out_shape(a, kind):
    return ((N_DEV,) + a.shape) if kind == "gather" else a.shape


def exchange(items, *, name, after=()):
    n = len(items)
    kinds = [k for _, k in items]
    after = list(after)

    def body(*refs):
        in_refs, out_refs = refs[:n], refs[n + len(after):2 * n + len(after)]
        send_sems, recv_sems, local_sems = refs[-3:]
        local = _local_copies(kinds, in_refs, out_refs, local_sems)
        sends = _remote_copies(kinds, in_refs, out_refs, send_sems, recv_sems, arriving=False)
        for cp in local + sends:
            cp.start()
        for cp in _remote_copies(kinds, in_refs, out_refs, send_sems, recv_sems, arriving=True):
            cp.wait_recv()
        for cp in sends:
            cp.wait_send()
        for cp in local:
            cp.wait()

    any_spec = pl.BlockSpec(memory_space=pl.ANY)
    return pl.pallas_call(
        body, name=name,
        in_specs=[any_spec] * (n + len(after)), out_specs=[any_spec] * n,
        out_shape=[_sds(_out_shape(a, k), a.dtype) for a, k in items],
        scratch_shapes=[pltpu.SemaphoreType.DMA((n * (N_DEV - 1),)), pltpu.SemaphoreType.DMA((n * (N_DEV - 1),)),
                        pltpu.SemaphoreType.DMA((n,))],
    )(*[a for a, _ in items], *after)


_HBM = pl.BlockSpec(memory_space=pltpu.HBM)
_SEM = pl.BlockSpec(memory_space=pltpu.SEMAPHORE)
_EFFECT = pltpu.SideEffectType.DATAFLOW_SIDE_EFFECTING


def exchange_start(items, after, *, name):
    n = len(items)
    kinds = [k for _, k in items]

    def body(*refs):
        in_refs, land_refs = refs[:n], refs[n:2 * n]
        send_sems, recv_sems, local_sems = refs[2 * n + 1:2 * n + 4]
        token = refs[-1]
        for cp in (_local_copies(kinds, in_refs, land_refs, local_sems)
                   + _remote_copies(kinds, in_refs, land_refs, send_sems, recv_sems, arriving=False)):
            cp.start()
        token[...] = jnp.zeros_like(token)

    srcs = [pltpu.with_memory_space_constraint(a, pltpu.HBM) for a, _ in items]
    lands = [pltpu.with_memory_space_constraint(lax.empty(_out_shape(a, k), a.dtype), pltpu.HBM) for a, k in items]
    outs = pl.pallas_call(
        body, name=name,
        out_shape=(pltpu.SemaphoreType.DMA((n * (N_DEV - 1),)), pltpu.SemaphoreType.DMA((n * (N_DEV - 1),)),
                   pltpu.SemaphoreType.DMA((n,)),
                   *[pltpu.HBM(a.shape, a.dtype) for a in srcs], *[pltpu.HBM(a.shape, a.dtype) for a in lands],
                   _sds((8, 128), f32)),
        in_specs=[_HBM] * (2 * n) + [pl.BlockSpec(memory_space=pl.ANY)],
        out_specs=(_SEM, _SEM, _SEM, *[_HBM] * (2 * n), pl.BlockSpec(memory_space=pltpu.VMEM)),
        input_output_aliases={i: 3 + i for i in range(2 * n)},
        compiler_params=pltpu.CompilerParams(has_side_effects=_EFFECT),
    )(*srcs, *lands, after)
    return {"kinds": kinds, "sems": outs[:3], "srcs": outs[3:3 + n], "lands": outs[3 + n:3 + 2 * n], "token": outs[-1]}


def exchange_wait(handle, after, *, name):
    kinds = handle["kinds"]
    n = len(kinds)

    def body(*refs):
        in_refs, land_refs = refs[:n], refs[n:2 * n]
        send_sems, recv_sems, local_sems = refs[2 * n:2 * n + 3]
        for cp in _remote_copies(kinds, in_refs, land_refs, send_sems, recv_sems, arriving=True):
            cp.wait_recv()
        for cp in _remote_copies(kinds, in_refs, land_refs, send_sems, recv_sems, arriving=False):
            cp.wait_send()
        for cp in _local_copies(kinds, in_refs, land_refs, local_sems):
            cp.wait()

    srcs, lands = list(handle["srcs"]), list(handle["lands"])
    after = list(after) if isinstance(after, (list, tuple)) else [after]
    outs = pl.pallas_call(
        body, name=name,
        out_shape=tuple(pltpu.HBM(a.shape, a.dtype) for a in srcs + lands),
        in_specs=[_HBM] * (2 * n) + [_SEM] * 3 + [pl.BlockSpec(memory_space=pl.ANY)] * len(after),
        out_specs=tuple([_HBM] * (2 * n)),
        input_output_aliases={i: i for i in range(2 * n)},
        compiler_params=pltpu.CompilerParams(has_side_effects=_EFFECT),
    )(*srcs, *lands, *handle["sems"], *after)
    return list(outs[n:])


def adamw(slots, w, m, v, *, name, layer=None, into=None):
    R, C = w.shape[-2:]
    tr = _tile(R, (256, 128, 64, 32, 16, 8))
    c1 = 1.0 - ADAM_B1 ** ADAM_STEP
    c2 = 1.0 - ADAM_B2 ** ADAM_STEP
    extra = [] if into is None else list(into)

    def body(s_ref, w_ref, m_ref, v_ref, *refs):
        g_ref, d_ref, m2_ref, v2_ref = refs[len(extra):]
        g = s_ref[0].astype(f32)
        for d in range(1, N_DEV):
            g = g + s_ref[d].astype(f32)
        m2 = ADAM_B1 * m_ref[...] + (1.0 - ADAM_B1) * g
        v2 = ADAM_B2 * v_ref[...] + (1.0 - ADAM_B2) * (g * g)
        g_ref[...] = g
        m2_ref[...] = m2
        v2_ref[...] = v2
        d_ref[...] = -ADAM_LR * ((m2 / c1) / (jnp.sqrt(v2 / c2) + ADAM_EPS) + ADAM_WD * w_ref[...])

    if layer is None:
        blk = pl.BlockSpec((tr, C), lambda i: (i, 0))
    else:
        blk = pl.BlockSpec((None, tr, C), lambda i: (layer, i, 0))
    return pl.pallas_call(
        body, name=name, grid=(R // tr,),
        in_specs=[pl.BlockSpec((N_DEV, tr, C), lambda i: (0, i, 0)), blk, blk, blk]
        + [pl.BlockSpec(memory_space=pl.ANY)] * len(extra),
        out_specs=[blk] * 4,
        out_shape=[_sds(w.shape, f32)] * 4,
        input_output_aliases={4 + j: j for j in range(len(extra))},
        compiler_params=_params(("parallel",)),
    )(slots, w, m, v, *extra)


WEIGHTS = ("norm_gains", "mem_norm", "w_in", "w_mem_kv", "w_out", "w_pool", "pool_scale", "kv_norm", "w_kv",
           "w_gate_up", "w_down")
LAYER_MATS = ("w_in", "w_mem_kv", "w_out", "w_gate_up", "w_down")
POOL_SHARD = MAIN_W // N_DEV
KV_SHARD = 2 * MAIN_W // N_DEV
LOOKAHEAD = 2


def _pack_small(gains, pscale):
    lead = gains.shape[:-3]
    g = gains.reshape(lead + (16, 128))
    p = jnp.zeros(lead + (8, 128), f32).at[..., :2, :POOL_SHARD].set(pscale)
    return jnp.concatenate([g, p], axis=-2)


def _unpack_small(a):
    return a[:16].reshape(4, 4, 128), a[16:18, :POOL_SHARD]


def _pack_repl(mem_norm, kv_norm):
    return jnp.concatenate([mem_norm, kv_norm.reshape(1, D_MODEL), jnp.zeros((3, D_MODEL), f32)], axis=0)


def _unpack_repl(a):
    return a[:4], a[4]


def kernel(x, mem, positions, norm_gains, mem_norm, w_in, w_mem_kv, w_out, w_pool, pool_scale, kv_norm, w_kv, w_gate_up, w_down, loss_target, m_norm_gains, m_mem_norm, m_w_in, m_w_mem_kv, m_w_out, m_w_pool, m_pool_scale, m_kv_norm, m_w_kv, m_w_gate_up, m_w_down, v_norm_gains, v_mem_norm, v_w_in, v_w_mem_kv, v_w_out, v_w_pool, v_pool_scale, v_kv_norm, v_w_kv, v_w_gate_up, v_w_down):
    w = dict(norm_gains=norm_gains, mem_norm=mem_norm, w_in=w_in, w_mem_kv=w_mem_kv, w_out=w_out, w_pool=w_pool,
             pool_scale=pool_scale, kv_norm=kv_norm, w_kv=w_kv, w_gate_up=w_gate_up, w_down=w_down)
    m = dict(norm_gains=m_norm_gains, mem_norm=m_mem_norm, w_in=m_w_in, w_mem_kv=m_w_mem_kv, w_out=m_w_out,
             w_pool=m_w_pool, pool_scale=m_pool_scale, kv_norm=m_kv_norm, w_kv=m_w_kv, w_gate_up=m_w_gate_up,
             w_down=m_w_down)
    v = dict(norm_gains=v_norm_gains, mem_norm=v_mem_norm, w_in=v_w_in, w_mem_kv=v_w_mem_kv, w_out=v_w_out,
             w_pool=v_w_pool, pool_scale=v_pool_scale, kv_norm=v_kv_norm, w_kv=v_w_kv, w_gate_up=v_w_gate_up,
             w_down=v_w_down)

    def transposed_view(d):
        d = dict(d)
        d["w_gate_up"] = jnp.swapaxes(d["w_gate_up"], 1, 2)
        d["w_kv"] = jnp.swapaxes(d["w_kv"], 0, 1)
        return d

    wv, mv, vv = transposed_view(w), transposed_view(m), transposed_view(v)

    small = _pack_small(norm_gains, pool_scale)
    (gsmall,) = exchange([(small, "gather")], name="gather_small")
    P = {"norm_gains": jnp.moveaxis(gsmall[:, :16].reshape(N_DEV, 4, 4, 128), 0, 2).reshape(4, 4, D_MODEL),
         "pool_scale": jnp.moveaxis(gsmall[:, 16:18, :POOL_SHARD], 0, 1).reshape(2, MAIN_W),
         "mem_norm": mem_norm, "kv_norm": kv_norm, "w_pool": w_pool}

    PARTS = {"mix": ("w_in", "w_mem_kv", "w_out"), "ffn": ("w_gate_up", "w_down"), "gu": ("w_gate_up",),
             "down": ("w_down",)}

    def parts_of(l):
        return ("mix", "gu", "down") if l == 0 else ("mix", "ffn")

    def part_items(l, part):
        items = [(wv[k][l].astype(bf16), "gather") for k in PARTS[part]]
        if part == "ffn" and l == N_A_LAYERS - 1:
            items.append((wv["w_kv"].astype(bf16), "gather"))
        return items

    handles = {}

    def start_layer(l, after):
        for part in parts_of(l):
            handles[l, part] = exchange_start(part_items(l, part), after, name=f"gather_start_{part}_l{l}")
            after = handles[l, part]["token"]
        return after

    token = gsmall
    for l in range(LOOKAHEAD):
        token = start_layer(l, token)
    landed = {}

    def layer_weights(l, part, after):
        if part not in parts_of(l):
            if part == "down":
                return {}, None
            part = "ffn"
        first = l == 0 and part == "mix"
        got = exchange_wait(handles[l, part], token if first else after, name=f"gather_wait_{part}_l{l}")
        landed[l, part] = got
        started = None
        if part == "mix" and l + LOOKAHEAD < DEPTH:
            started = start_layer(l + LOOKAHEAD, got[0])
        W = {k: g.reshape(-1, g.shape[-1]) for k, g in zip(PARTS[part], got)}
        return W, started

    def kv_weight(after):
        g = landed[N_A_LAYERS - 1, "ffn"][len(PARTS["ffn"])]
        return g.reshape(2 * MAIN_W, D_MODEL)

    ghandles = {}

    def emit_grads(l, part, gw):
        items = [(gw[k].reshape((N_DEV, -1) + gw[k].shape[-1:]), "scatter") for k in PARTS[part]]
        if part == "mix" and l == N_A_LAYERS:
            items.append((gw["w_kv"].reshape(N_DEV, KV_SHARD, D_MODEL), "scatter"))
        if part == "mix" and l < N_A_LAYERS:
            items.append((gw["w_pool"], "gather"))
        ghandles[l, part] = exchange_start(items, gsmall, name=f"scatter_start_{part}_l{l}")
        return ghandles[l, part]["token"]

    sq, grad_x, GS, emitted = local_step(x, mem, positions, loss_target, P, layer_weights, kv_weight, emit_grads)
    loss = lax.psum(0.5 * sq / D_MODEL, ("x", "y", "c"))

    def pool3(a):
        return a.reshape(N_A_LAYERS, MAIN_W, POOL_GROUP)

    out = {}
    after = [emitted]

    def finish_layer(l, after):
        for part in ("ffn", "mix"):
            got = exchange_wait(ghandles[l, part], after, name=f"scatter_wait_{part}_l{l}")
            after = []
            for k, slots in zip(PARTS[part], got):
                out[k] = adamw(slots, wv[k], mv[k], vv[k], name=f"adamw_{k}_l{l}", layer=l, into=out.get(k))
                after.append(out[k][0])
            if part == "mix" and l == N_A_LAYERS:
                out["w_kv"] = adamw(got[-1], wv["w_kv"], mv["w_kv"], vv["w_kv"], name="adamw_w_kv")
                after.append(out["w_kv"][0])
            if part == "mix" and l < N_A_LAYERS:
                out["w_pool"] = adamw(got[-1], pool3(w_pool), pool3(m_w_pool), pool3(v_w_pool), name=f"adamw_w_pool_l{l}",
                                      layer=l, into=out.get("w_pool"))
                after.append(out["w_pool"][0])
        return after

    for l in reversed(range(1, DEPTH)):
        after = finish_layer(l, after)

    gs = _pack_small(jnp.moveaxis(GS["norm_gains"].reshape(4, 4, N_DEV, 128), 2, 0),
                     jnp.moveaxis(GS["pool_scale"].reshape(2, N_DEV, POOL_SHARD), 1, 0))
    parts_small, parts_repl = exchange(
        [(gs, "scatter"), (_pack_repl(GS["mem_norm"], GS["kv_norm"]), "gather")],
        name="exchange_small_grads", after=after)
    finish_layer(0, [parts_small])
    out["w_gate_up"] = [jnp.swapaxes(r, 1, 2) for r in out["w_gate_up"]]
    out["w_kv"] = [jnp.swapaxes(r, 0, 1) for r in out["w_kv"]]
    out["w_pool"] = [r.reshape(w_pool.shape) for r in out["w_pool"]]

    res = adamw(parts_small, small, _pack_small(m_norm_gains, m_pool_scale), _pack_small(v_norm_gains, v_pool_scale),
                name="adamw_small")
    out["norm_gains"], out["pool_scale"] = zip(*[_unpack_small(r) for r in res])
    res = adamw(parts_repl, _pack_repl(mem_norm, kv_norm), _pack_repl(m_mem_norm, m_kv_norm),
                _pack_repl(v_mem_norm, v_kv_norm), name="adamw_repl")
    out["mem_norm"], out["kv_norm"] = zip(*[_unpack_repl(r) for r in res])

    return (loss, grad_x, *[out[k][0] for k in WEIGHTS], *[out[k][1] for k in WEIGHTS],
            *[out[k][2] for k in WEIGHTS], *[out[k][3] for k in WEIGHTS])
```

```python
import numpy as np
import jax
import jax.numpy as jnp
from jax import lax
from jax.experimental import pallas as pl
from jax.experimental.pallas import tpu as pltpu

f32 = jnp.float32
bf16 = jnp.bfloat16

D_MODEL = 1024
SEQ = 2048
DEPTH = 4
N_MEM = 256
HEAD_DIM = 64
N_MEM_HEADS = 4
MEM_W = 256
MAIN_W = 768
POOL_WINDOWS = (2, 4, 8, 16)
POOL_GROUP = 192
POOL_HALO = 16
DIL_PATTERNS = ((128, 1), (512, 4), (2048, 16))
N_GROUPS = 3
GROUP_W = 256
BAND = 128
N_A_LAYERS = 2
D_FF = 2816
ROPE_THETA = 10000.0
EPS = 1e-6
NEG = -1e30
SCALE = HEAD_DIM ** -0.5
N_DEV = 8

ADAM_LR = 0.001
ADAM_B1 = 0.9
ADAM_B2 = 0.999
ADAM_EPS = 1e-08
ADAM_WD = 0.01
ADAM_STEP = 10

VMEM_LIMIT_BYTES = 56 * 1024 * 1024
MESH = pl.DeviceIdType.MESH

NN = (((1,), (0,)), ((), ()))
NT = (((1,), (1,)), ((), ()))
TN = (((0,), (0,)), ((), ()))


def _params(sem=None):
    return pltpu.CompilerParams(dimension_semantics=sem, vmem_limit_bytes=VMEM_LIMIT_BYTES)


def _tile(n, cands):
    for c in cands:
        if n % c == 0:
            return c
    return n


def _sds(shape, dtype):
    return jax.ShapeDtypeStruct(tuple(shape), dtype)


def _rms_r(v):
    return lax.rsqrt(jnp.mean(v * v, axis=-1, keepdims=True) + EPS)


def rms_matmul(x, gain, w, *, name, out_dtype, transposed=False, after=None):
    M, K = x.shape
    N = w.shape[0] if transposed else w.shape[1]
    tm = min(1024, M)
    tn = _tile(N, (512, 256, 128))
    order = [] if after is None else [after]

    def body(x_ref, g_ref, w_ref, *refs):
        z_ref, h_ref = refs[len(order):]

        @pl.when(pl.program_id(1) == 0)
        def _():
            xv = x_ref[...]
            h_ref[...] = (xv * _rms_r(xv) * g_ref[...]).astype(bf16)

        z_ref[...] = lax.dot_general(h_ref[...], w_ref[...], NT if transposed else NN,
                                     preferred_element_type=f32).astype(z_ref.dtype)

    w_spec = pl.BlockSpec((tn, K), lambda i, j: (j, 0)) if transposed else pl.BlockSpec((K, tn), lambda i, j: (0, j))
    return pl.pallas_call(
        body, name=name, grid=(M // tm, N // tn),
        in_specs=[pl.BlockSpec((tm, K), lambda i, j: (i, 0)),
                  pl.BlockSpec((1, K), lambda i, j: (0, 0)),
                  w_spec] + [pl.BlockSpec(memory_space=pl.ANY)] * len(order),
        out_specs=[pl.BlockSpec((tm, tn), lambda i, j: (i, j)), pl.BlockSpec((tm, K), lambda i, j: (i, 0))],
        out_shape=[_sds((M, N), out_dtype), _sds((M, K), bf16)],
        compiler_params=_params(("parallel", "arbitrary")),
    )(x, gain, w, *order)


def matmul_rms_res(a, w, gain, res, *, name):
    M, K = a.shape
    N = w.shape[1]
    tm = min(512, M)

    def body(a_ref, w_ref, g_ref, r_ref, y_ref, x_ref):
        y = jnp.dot(a_ref[...], w_ref[...], preferred_element_type=f32)
        y_ref[...] = y.astype(bf16)
        x_ref[...] = r_ref[...] + y * _rms_r(y) * g_ref[...]

    row = pl.BlockSpec((tm, N), lambda i: (i, 0))
    return pl.pallas_call(
        body, name=name, grid=(M // tm,),
        in_specs=[pl.BlockSpec((tm, K), lambda i: (i, 0)),
                  pl.BlockSpec((K, N), lambda i: (0, 0)),
                  pl.BlockSpec((1, N), lambda i: (0, 0)),
                  row],
        out_specs=[row, row],
        out_shape=[_sds((M, N), bf16), _sds((M, N), f32)],
        compiler_params=_params(("parallel",)),
    )(a, w, gain, res)


def matmul(a, b, dims, *, name, out_dtype):
    if dims is TN:
        K, M = a.shape
        tm = _tile(M, (512, 256, 128))
        a_spec = pl.BlockSpec((K, tm), lambda i: (0, i))
    else:
        M, K = a.shape
        tm = _tile(M, (512, 256, 128))
        a_spec = pl.BlockSpec((tm, K), lambda i: (i, 0))
    N = b.shape[0] if dims is NT else b.shape[1]

    def body(a_ref, b_ref, o_ref):
        o_ref[...] = lax.dot_general(a_ref[...].astype(bf16), b_ref[...].astype(bf16), dims,
                                     preferred_element_type=f32).astype(o_ref.dtype)

    return pl.pallas_call(
        body, name=name, grid=(M // tm,),
        in_specs=[a_spec, pl.BlockSpec(b.shape, lambda i: (0, 0))],
        out_specs=pl.BlockSpec((tm, N), lambda i: (i, 0)),
        out_shape=_sds((M, N), out_dtype),
        compiler_params=_params(("parallel",)),
    )(a, b)


def rms_gate_up(x, gain, wt, *, name):
    M, K = x.shape
    tm = min(1024, M)
    tn = _tile(D_FF, (256, 128))
    nj = D_FF // tn

    def body(x_ref, gn_ref, wg_ref, wu_ref, g_ref, u_ref, a_ref, h_ref):
        @pl.when(pl.program_id(1) == 0)
        def _():
            xv = x_ref[...]
            h_ref[...] = (xv * _rms_r(xv) * gn_ref[...]).astype(bf16)

        h = h_ref[...]
        g = lax.dot_general(h, wg_ref[...], NT, preferred_element_type=f32)
        u = lax.dot_general(h, wu_ref[...], NT, preferred_element_type=f32)
        g_ref[...] = g.astype(bf16)
        u_ref[...] = u.astype(bf16)
        a_ref[...] = (g * (1.0 / (1.0 + jnp.exp(-g))) * u).astype(bf16)

    col = pl.BlockSpec((tm, tn), lambda i, j: (i, j))
    return pl.pallas_call(
        body, name=name, grid=(M // tm, nj),
        in_specs=[pl.BlockSpec((tm, K), lambda i, j: (i, 0)),
                  pl.BlockSpec((1, K), lambda i, j: (0, 0)),
                  pl.BlockSpec((tn, K), lambda i, j: (j, 0)),
                  pl.BlockSpec((tn, K), lambda i, j: (j + nj, 0))],
        out_specs=[col, col, col, pl.BlockSpec((tm, K), lambda i, j: (i, 0))],
        out_shape=[_sds((M, D_FF), bf16)] * 3 + [_sds((M, K), bf16)],
        compiler_params=_params(("parallel", "arbitrary")),
    )(x, gain, wt, wt)


def _rms_bwd_math(yv, gain, dn):
    r = _rms_r(yv)
    q = dn * gain
    dy = r * q - yv * (r * r * r) * jnp.mean(q * yv, axis=-1, keepdims=True)
    return dy, jnp.sum(dn * yv * r, axis=0, keepdims=True)


def _accumulate(ref, val):
    @pl.when(pl.program_id(0) == 0)
    def _():
        ref[...] = jnp.zeros_like(ref)

    ref[...] += val


def down_bwd(y, gain, dn, w_down, g, u, *, name, after=None):
    M, K = y.shape
    tm = min(512, M)
    order = [] if after is None else [after]

    def body(y_ref, gn_ref, dn_ref, w_ref, g_ref, u_ref, *refs):
        dy_ref, o_ref, dg_ref = refs[len(order):]
        dy, dgain = _rms_bwd_math(y_ref[...].astype(f32), gn_ref[...], dn_ref[...])
        dy = dy.astype(bf16)
        dy_ref[...] = dy
        _accumulate(dg_ref, dgain)
        da = lax.dot_general(dy, w_ref[...], NT, preferred_element_type=f32)
        g = g_ref[...].astype(f32)
        u = u_ref[...].astype(f32)
        s = 1.0 / (1.0 + jnp.exp(-g))
        o_ref[:, :D_FF] = (da * u * s * (1.0 + g * (1.0 - s))).astype(bf16)
        o_ref[:, D_FF:] = (da * g * s).astype(bf16)

    row = pl.BlockSpec((tm, K), lambda i: (i, 0))
    vec = pl.BlockSpec((1, K), lambda i: (0, 0))
    wide = pl.BlockSpec((tm, D_FF), lambda i: (i, 0))
    return pl.pallas_call(
        body, name=name, grid=(M // tm,),
        in_specs=[row, vec, row, pl.BlockSpec((D_FF, K), lambda i: (0, 0)), wide, wide]
        + [pl.BlockSpec(memory_space=pl.ANY)] * len(order),
        out_specs=[row, pl.BlockSpec((tm, 2 * D_FF), lambda i: (i, 0)), vec],
        out_shape=[_sds((M, K), bf16), _sds((M, 2 * D_FF), bf16), _sds((1, K), f32)],
        compiler_params=_params(("arbitrary",)),
    )(y, gain, dn, w_down, g, u, *order)


def rms_bwd_matmul(y, gain, dn, w, dims, *, name, after=None):
    M, K = y.shape
    N = w.shape[0] if dims is NT else w.shape[1]
    tm = min(512, M)
    order = [] if after is None else [after]

    def body(y_ref, gn_ref, dn_ref, w_ref, *refs):
        dy_ref, o_ref, dg_ref = refs[len(order):]
        dy, dgain = _rms_bwd_math(y_ref[...].astype(f32), gn_ref[...], dn_ref[...].astype(f32))
        dy = dy.astype(bf16)
        dy_ref[...] = dy
        _accumulate(dg_ref, dgain)
        o_ref[...] = lax.dot_general(dy, w_ref[...], dims, preferred_element_type=f32).astype(bf16)

    row = pl.BlockSpec((tm, K), lambda i: (i, 0))
    vec = pl.BlockSpec((1, K), lambda i: (0, 0))
    return pl.pallas_call(
        body, name=name, grid=(M // tm,),
        in_specs=[row, vec, row, pl.BlockSpec(w.shape, lambda i: (0, 0))]
        + [pl.BlockSpec(memory_space=pl.ANY)] * len(order),
        out_specs=[row, pl.BlockSpec((tm, N), lambda i: (i, 0)), vec],
        out_shape=[_sds((M, K), bf16), _sds((M, N), bf16), _sds((1, K), f32)],
        compiler_params=_params(("arbitrary",)),
    )(y, gain, dn, w, *order)


def matmul_rms_bwd(a, b, dims, y, gain, res, *, name):
    M, K = a.shape
    N = y.shape[1]
    tm = 256

    def body(a_ref, b_ref, y_ref, gn_ref, r_ref, dx_ref, dg_ref):
        dn = lax.dot_general(a_ref[...], b_ref[...], dims, preferred_element_type=f32)
        dy, dgain = _rms_bwd_math(y_ref[...], gn_ref[...], dn)
        dx_ref[...] = dy + r_ref[...]
        _accumulate(dg_ref, dgain)

    row = pl.BlockSpec((tm, N), lambda i: (i, 0))
    vec = pl.BlockSpec((1, N), lambda i: (0, 0))
    return pl.pallas_call(
        body, name=name, grid=(M // tm,),
        in_specs=[pl.BlockSpec((tm, K), lambda i: (i, 0)), pl.BlockSpec(b.shape, lambda i: (0, 0)), row, vec, row],
        out_specs=[row, vec],
        out_shape=[_sds((M, N), f32), _sds((1, N), f32)],
        compiler_params=_params(("arbitrary",)),
    )(a, b, y, gain, res)


def rms_bwd(y, gain, dn, res, *, name, out_dtype, after=None):
    M, N = y.shape
    tm = min(512, M)
    has_res = res is not None
    order = [] if after is None else [after]

    def body(*refs):
        y_ref, g_ref, dn_ref = refs[:3]
        r_ref = refs[3] if has_res else None
        dy_ref, dg_ref = refs[-2:]
        dy, dgain = _rms_bwd_math(y_ref[...].astype(f32), g_ref[...], dn_ref[...].astype(f32))
        if has_res:
            dy = dy + r_ref[...]
        dy_ref[...] = dy.astype(dy_ref.dtype)
        _accumulate(dg_ref, dgain)

    row = pl.BlockSpec((tm, N), lambda i: (i, 0))
    vec = pl.BlockSpec((1, N), lambda i: (0, 0))
    args = [y, gain, dn] + ([res] if has_res else []) + order
    return pl.pallas_call(
        body, name=name, grid=(M // tm,),
        in_specs=[row, vec, row] + ([row] if has_res else []) + [pl.BlockSpec(memory_space=pl.ANY)] * len(order),
        out_specs=[row, vec],
        out_shape=[_sds((M, N), out_dtype), _sds((1, N), f32)],
        compiler_params=_params(("arbitrary",)),
    )(*args)


def loss_head(x, target, *, name):
    M, N = x.shape
    tm = min(512, M)

    def body(x_ref, t_ref, dx_ref, l_ref):
        e = x_ref[...] - t_ref[...]
        dx_ref[...] = e * (1.0 / N)

        @pl.when(pl.program_id(0) == 0)
        def _():
            l_ref[...] = jnp.zeros_like(l_ref)

        l_ref[...] += jnp.sum(jnp.sum(e * e, axis=0, keepdims=True), axis=1, keepdims=True)

    row = pl.BlockSpec((tm, N), lambda i: (i, 0))
    return pl.pallas_call(
        body, name=name, grid=(M // tm,),
        in_specs=[row, row],
        out_specs=[row, pl.BlockSpec((8, 128), lambda i: (0, 0))],
        out_shape=[_sds((M, N), f32), _sds((8, 128), f32)],
        compiler_params=_params(("arbitrary",)),
    )(x, target)


def _pool_select(a1, a2, a3, a4):
    col = lax.broadcasted_iota(jnp.int32, (1, MAIN_W), 1) // POOL_GROUP
    return jnp.where(col == 0, a1, jnp.where(col == 1, a2, jnp.where(col == 2, a3, a4)))


def _pool_count(t):
    col = lax.broadcasted_iota(jnp.int32, (1, MAIN_W), 1) // POOL_GROUP
    win = jnp.where(col == 0, 2, jnp.where(col == 1, 4, jnp.where(col == 2, 8, 16)))
    return jnp.minimum(t + 1, win).astype(f32)


def pool_fwd(z, wbd, scale, *, name):
    M = z.shape[0]
    tm = 256
    nper = SEQ // tm
    hb = tm // POOL_HALO

    def body(zc_ref, zh_ref, w_ref, s_ref, p_ref, y_ref):
        i = pl.program_id(0)
        seq_blk = i % nper
        halo = jnp.where(seq_blk == 0, 0.0, zh_ref[...])
        u = zc_ref[...]
        ext = jnp.concatenate([halo, u], axis=0)
        a1 = ext + pltpu.roll(ext, 1, 0)
        a2 = a1 + pltpu.roll(a1, 2, 0)
        a3 = a2 + pltpu.roll(a2, 4, 0)
        a4 = a3 + pltpu.roll(a3, 8, 0)
        sums = _pool_select(a1, a2, a3, a4)[POOL_HALO:]
        t = seq_blk * tm + lax.broadcasted_iota(jnp.int32, (tm, 1), 0)
        p = (sums / _pool_count(t) - u).astype(bf16)
        p_ref[...] = p
        y_ref[...] = (jnp.dot(p, w_ref[...], preferred_element_type=f32) * s_ref[...]).astype(bf16)

    return pl.pallas_call(
        body, name=name, grid=(M // tm,),
        in_specs=[pl.BlockSpec((tm, MAIN_W), lambda i: (i, 0)),
                  pl.BlockSpec((POOL_HALO, MAIN_W), lambda i: (jnp.maximum(i * hb - 1, 0), 0)),
                  pl.BlockSpec((MAIN_W, MAIN_W), lambda i: (0, 0)),
                  pl.BlockSpec((1, MAIN_W), lambda i: (0, 0))],
        out_specs=[pl.BlockSpec((tm, MAIN_W), lambda i: (i, 0)),
                   pl.BlockSpec((tm, MAIN_W), lambda i: (i, 0))],
        out_shape=[_sds((M, MAIN_W), bf16), _sds((M, D_MODEL), bf16)],
        compiler_params=_params(("parallel",)),
    )(z, z, wbd, scale)


def pool_bwd(dyc, p, wbd, scale, *, name):
    M = p.shape[0]
    tm = 256
    nper = SEQ // tm
    hb = tm // POOL_HALO
    last_hb = M // POOL_HALO - 1

    def body(dy_ref, dyh_ref, p_ref, w_ref, s_ref, dz_ref, dw_ref, ds_ref):
        i = pl.program_id(0)
        seq_blk = i % nper
        dy = dy_ref[...].astype(f32)
        pv = p_ref[...]
        w = w_ref[...]
        sc = s_ref[...]

        @pl.when(i == 0)
        def _():
            dw_ref[...] = jnp.zeros_like(dw_ref)
            ds_ref[...] = jnp.zeros_like(ds_ref)

        v = jnp.dot(pv, w, preferred_element_type=f32)
        ds_ref[...] += jnp.sum(dy * v, axis=0, keepdims=True)
        dv = (dy * sc).astype(bf16)
        dw_ref[...] += lax.dot_general(pv, dv, TN, preferred_element_type=f32)
        dp = lax.dot_general(dv, w, NT, preferred_element_type=f32)
        dvh = jnp.where(seq_blk == nper - 1, 0.0, dyh_ref[...].astype(f32) * sc).astype(bf16)
        dph = lax.dot_general(dvh, w, NT, preferred_element_type=f32)
        ext = jnp.concatenate([dp, dph], axis=0)
        n = tm + POOL_HALO
        t = seq_blk * tm + lax.broadcasted_iota(jnp.int32, (n, 1), 0)
        e = ext / _pool_count(t)
        b1 = e + pltpu.roll(e, n - 1, 0)
        b2 = b1 + pltpu.roll(b1, n - 2, 0)
        b3 = b2 + pltpu.roll(b2, n - 4, 0)
        b4 = b3 + pltpu.roll(b3, n - 8, 0)
        dz_ref[...] = (_pool_select(b1, b2, b3, b4)[:tm] - dp).astype(dz_ref.dtype)

    return pl.pallas_call(
        body, name=name, grid=(M // tm,),
        in_specs=[pl.BlockSpec((tm, MAIN_W), lambda i: (i, 0)),
                  pl.BlockSpec((POOL_HALO, MAIN_W), lambda i: (jnp.minimum((i + 1) * hb, last_hb), 0)),
                  pl.BlockSpec((tm, MAIN_W), lambda i: (i, 0)),
                  pl.BlockSpec((MAIN_W, MAIN_W), lambda i: (0, 0)),
                  pl.BlockSpec((1, MAIN_W), lambda i: (0, 0))],
        out_specs=[pl.BlockSpec((tm, MAIN_W), lambda i: (i, 0)),
                   pl.BlockSpec((MAIN_W, MAIN_W), lambda i: (0, 0)),
                   pl.BlockSpec((1, MAIN_W), lambda i: (0, 0))],
        out_shape=[_sds((M, D_MODEL), bf16), _sds((MAIN_W, MAIN_W), f32), _sds((1, MAIN_W), f32)],
        compiler_params=_params(("arbitrary",)),
    )(dyc, dyc, p, wbd, scale)


def _mem_probs(q, kv, h):
    hs = slice(h * HEAD_DIM, (h + 1) * HEAD_DIM)
    qh = q[:, hs]
    kh = kv[:, hs]
    s = lax.dot_general(qh, kh, NT, preferred_element_type=f32) * SCALE
    m = jnp.max(s, axis=-1, keepdims=True)
    e = jnp.exp(s - m)
    return qh, kh, e / jnp.sum(e, axis=-1, keepdims=True)


def memattn_fwd(z, kvm, ycat, *, name, n_seq):
    M = z.shape[0]
    tq = 512
    nq = SEQ // tq

    def body(q_ref, kv_ref, _, o_ref):
        q = q_ref[...].astype(bf16)
        kv = kv_ref[...]
        outs = []
        for h in range(N_MEM_HEADS):
            _, _, p = _mem_probs(q, kv, h)
            vh = kv[:, MEM_W + h * HEAD_DIM: MEM_W + (h + 1) * HEAD_DIM]
            outs.append(jnp.dot(p.astype(bf16), vh, preferred_element_type=f32))
        o_ref[...] = jnp.concatenate(outs, axis=1).astype(bf16)

    return pl.pallas_call(
        body, name=name, grid=(n_seq, nq),
        in_specs=[pl.BlockSpec((tq, MEM_W), lambda b, i: (b * nq + i, 3)),
                  pl.BlockSpec((N_MEM, 2 * MEM_W), lambda b, i: (b, 0)),
                  pl.BlockSpec(memory_space=pl.ANY)],
        out_specs=pl.BlockSpec((tq, MEM_W), lambda b, i: (b * nq + i, 3)),
        out_shape=_sds((M, D_MODEL), bf16),
        input_output_aliases={2: 0},
        compiler_params=_params(("parallel", "parallel")),
    )(z, kvm, ycat)


def memattn_bwd(z, kvm, dyc, dz, *, name, n_seq):
    M = z.shape[0]
    tq = 512
    nq = SEQ // tq

    def body(q_ref, kv_ref, dy_ref, _, dq_ref, dkv_ref):
        q = q_ref[...].astype(bf16)
        kv = kv_ref[...]
        dy = dy_ref[...].astype(bf16)
        dqs, dks, dvs = [], [], []
        for h in range(N_MEM_HEADS):
            hs = slice(h * HEAD_DIM, (h + 1) * HEAD_DIM)
            qh, kh, p = _mem_probs(q, kv, h)
            vh = kv[:, MEM_W + h * HEAD_DIM: MEM_W + (h + 1) * HEAD_DIM]
            dyh = dy[:, hs]
            dvs.append(lax.dot_general(p.astype(bf16), dyh, TN, preferred_element_type=f32))
            dp = lax.dot_general(dyh, vh, NT, preferred_element_type=f32)
            ds = (p * (dp - jnp.sum(dp * p, axis=-1, keepdims=True)) * SCALE).astype(bf16)
            dqs.append(jnp.dot(ds, kh, preferred_element_type=f32))
            dks.append(lax.dot_general(ds, qh, TN, preferred_element_type=f32))
        dq_ref[...] = jnp.concatenate(dqs, axis=1).astype(bf16)

        @pl.when(pl.program_id(1) == 0)
        def _():
            dkv_ref[...] = jnp.zeros_like(dkv_ref)

        dkv_ref[...] += jnp.concatenate(dks + dvs, axis=1)

    return pl.pallas_call(
        body, name=name, grid=(n_seq, nq),
        in_specs=[pl.BlockSpec((tq, MEM_W), lambda b, i: (b * nq + i, 3)),
                  pl.BlockSpec((N_MEM, 2 * MEM_W), lambda b, i: (b, 0)),
                  pl.BlockSpec((tq, MEM_W), lambda b, i: (b * nq + i, 3)),
                  pl.BlockSpec(memory_space=pl.ANY)],
        out_specs=[pl.BlockSpec((tq, MEM_W), lambda b, i: (b * nq + i, 3)),
                   pl.BlockSpec((N_MEM, 2 * MEM_W), lambda b, i: (b, 0))],
        out_shape=[_sds((M, D_MODEL), bf16), _sds((n_seq * N_MEM, 2 * MEM_W), f32)],
        input_output_aliases={3: 0},
        compiler_params=_params(("parallel", "arbitrary")),
    )(z, kvm, dyc, dz)


def rope_tables(pos, *, name):
    M = pos.shape[0]
    tm = min(1024, M)
    half = HEAD_DIM // 2
    inv = ROPE_THETA ** (-np.arange(half, dtype=np.float64) / half)
    inv128 = jnp.asarray(np.tile(inv, 4)[None, :], f32)
    sign128 = jnp.asarray(np.tile(np.concatenate([-np.ones(half), np.ones(half)]), 2)[None, :], f32)

    def body(p_ref, f_ref, s_ref, cos_ref, sin_ref):
        ang = p_ref[...] * f_ref[...]
        cos_ref[...] = jnp.cos(ang)
        sin_ref[...] = jnp.sin(ang) * s_ref[...]

    return pl.pallas_call(
        body, name=name, grid=(M // tm,),
        in_specs=[pl.BlockSpec((tm, 1), lambda i: (i, 0)),
                  pl.BlockSpec((1, 128), lambda i: (0, 0)),
                  pl.BlockSpec((1, 128), lambda i: (0, 0))],
        out_specs=[pl.BlockSpec((tm, 128), lambda i: (i, 0)),
                   pl.BlockSpec((tm, 128), lambda i: (i, 0))],
        out_shape=[_sds((M, 128), f32), _sds((M, 128), f32)],
        compiler_params=_params(("parallel",)),
    )(pos, inv128, sign128)


def _swap_halves(x):
    w = x.shape[1]
    first = (lax.broadcasted_iota(jnp.int32, (1, w), 1) % HEAD_DIM) < (HEAD_DIM // 2)
    return jnp.where(first, pltpu.roll(x, w - HEAD_DIM // 2, 1), pltpu.roll(x, HEAD_DIM // 2, 1))


def rope_fwd(src, cos, sin, *, name):
    M = src.shape[0]
    tm = min(512, M)

    def body(x_ref, c_ref, s_ref, o_ref):
        x = x_ref[...].astype(f32)
        c = jnp.tile(c_ref[...], (1, MAIN_W // 128))
        s = jnp.tile(s_ref[...], (1, MAIN_W // 128))
        o_ref[...] = x * c + _swap_halves(x) * s

    return pl.pallas_call(
        body, name=name, grid=(M // tm,),
        in_specs=[pl.BlockSpec((tm, MAIN_W), lambda i: (i, 0)),
                  pl.BlockSpec((tm, 128), lambda i: (i, 0)),
                  pl.BlockSpec((tm, 128), lambda i: (i, 0))],
        out_specs=pl.BlockSpec((tm, MAIN_W), lambda i: (i, 0)),
        out_shape=_sds((M, MAIN_W), f32),
        compiler_params=_params(("parallel",)),
    )(src, cos, sin)


def group_sum(groups, cos, sin, *, name, rotate, width, col_block=0, into=None):
    M = groups[0][0].shape[0]
    tm = min(512, M)
    counts = [len(g) for g in groups]
    flat = [a for g in groups for a in g]
    extra = [] if into is None else [into]

    def body(*refs):
        part_refs = refs[:len(flat)]
        c_ref, s_ref = refs[len(flat):len(flat) + 2]
        o_ref = refs[-1]
        cols, k = [], 0
        for n in counts:
            acc = part_refs[k][...]
            for r in part_refs[k + 1:k + n]:
                acc = acc + r[...]
            cols.append(acc)
            k += n
        d = jnp.concatenate(cols, axis=1)
        if rotate:
            c = jnp.tile(c_ref[...], (1, MAIN_W // 128))
            s = jnp.tile(s_ref[...], (1, MAIN_W // 128))
            d = d * c - _swap_halves(d) * s
        o_ref[...] = d.astype(bf16)

    part = pl.BlockSpec((tm, GROUP_W), lambda i: (i, 0))
    tab = pl.BlockSpec((tm, 128), lambda i: (i, 0))
    return pl.pallas_call(
        body, name=name, grid=(M // tm,),
        in_specs=[part] * len(flat) + [tab, tab] + [pl.BlockSpec(memory_space=pl.ANY)] * len(extra),
        out_specs=pl.BlockSpec((tm, MAIN_W), lambda i: (i, col_block)),
        out_shape=_sds((M, width), bf16),
        input_output_aliases={len(flat) + 2: 0} if extra else {},
        compiler_params=_params(("parallel",)),
    )(*flat, cos, sin, *extra)


PAIR_W = 2 * HEAD_DIM
MIN_BLOCKS = 8


def _dil_geometry(dil):
    nsub = max(dil, MIN_BLOCKS)
    tb = BAND * nsub
    return nsub, tb, SEQ // tb


def _rows(ref, sub, dil):
    if dil == 1:
        return ref[sub * BAND:(sub + 1) * BAND, :]
    nl, r = divmod(sub, dil)
    return ref[pl.ds(nl * BAND * dil + r, BAND, stride=dil), :]


def _store_rows(ref, sub, dil, val):
    if dil == 1:
        ref[sub * BAND:(sub + 1) * BAND, :] = val
    else:
        nl, r = divmod(sub, dil)
        ref[pl.ds(nl * BAND * dil + r, BAND, stride=dil), :] = val


def _keys(prev_ref, own_ref, sub, dil):
    nsub = own_ref.shape[0] // BAND
    if sub >= dil:
        prev = _rows(own_ref, sub - dil, dil)
    elif prev_ref is None:
        return _rows(own_ref, sub, dil)
    else:
        prev = _rows(prev_ref, nsub - dil + sub, dil)
    return jnp.concatenate([prev, _rows(own_ref, sub, dil)], axis=0)


def _band_mask(nkeys, has_prev):
    i = lax.broadcasted_iota(jnp.int32, (BAND, nkeys), 0)
    j = lax.broadcasted_iota(jnp.int32, (BAND, nkeys), 1)
    if nkeys == BAND:
        return j <= i
    return (j >= i) & (j <= i + BAND) & (has_prev | (j >= BAND))


def _first_head():
    return lax.broadcasted_iota(jnp.int32, (1, PAIR_W), 1) < HEAD_DIM


def _col(x, hh):
    return x[:, hh * HEAD_DIM:hh * HEAD_DIM + 1]


def _pair_spec(tb, nblk, col0, which):
    def idx(b, p, i):
        if which < 0:
            i = jnp.maximum(i - 1, 0)
        elif which > 0:
            i = jnp.minimum(i + 1, nblk - 1)
        return (b * nblk + i, col0 + p)
    return pl.BlockSpec((tb, PAIR_W), idx)


def dil_fwd(q, k, kv, g, dil, *, name, n_seq):
    M = q.shape[0]
    nsub, tb, nblk = _dil_geometry(dil)
    with_prev = nblk > 1

    def body(*refs):
        if with_prev:
            q_ref, ko_ref, vo_ref, kp_ref, vp_ref, o_ref, l_ref = refs
        else:
            (q_ref, ko_ref, vo_ref, o_ref, l_ref), kp_ref, vp_ref = refs, None, None
        first = _first_head()
        blk = pl.program_id(2)
        for sub in range(nsub):
            qs = _rows(q_ref, sub, dil) * SCALE
            kc = _keys(kp_ref, ko_ref, sub, dil).astype(bf16)
            vc = _keys(vp_ref, vo_ref, sub, dil).astype(bf16)
            has_prev = True if sub >= dil else blk > 0
            mask = _band_mask(kc.shape[0], has_prev)
            outs, lses = [], []
            for hh in range(2):
                qm = jnp.where(first if hh == 0 else ~first, qs, 0.0).astype(bf16)
                s = jnp.where(mask, lax.dot_general(qm, kc, NT, preferred_element_type=f32), NEG)
                m = jnp.max(s, axis=-1, keepdims=True)
                e = jnp.exp(s - m)
                l = jnp.sum(e, axis=-1, keepdims=True)
                outs.append(jnp.dot(e.astype(bf16), vc, preferred_element_type=f32) * (1.0 / l))
                lses.append(jnp.broadcast_to(m + jnp.log(l), (BAND, PAIR_W)))
            _store_rows(o_ref, sub, dil, jnp.where(first, outs[0], outs[1]))
            _store_rows(l_ref, sub, dil, jnp.where(first, lses[0], lses[1]))

    ins = [(q, 2 * g, 0), (k, 2 * g, 0), (kv, 6 + 2 * g, 0)]
    if with_prev:
        ins += [(k, 2 * g, -1), (kv, 6 + 2 * g, -1)]
    out = _pair_spec(tb, nblk, 0, 0)
    return pl.pallas_call(
        body, name=name, grid=(n_seq, 2, nblk),
        in_specs=[_pair_spec(tb, nblk, c, w) for _, c, w in ins],
        out_specs=[out, out],
        out_shape=[_sds((M, GROUP_W), f32)] * 2,
        compiler_params=_params(("parallel", "parallel", "arbitrary")),
    )(*[a for a, _, _ in ins])


def combine_fwd(os_, lses, *, name):
    M = os_[0].shape[0]
    tm = min(512, M)

    def body(o0, o1, o2, l0, l1, l2, y_ref):
        ls = [l0[...], l1[...], l2[...]]
        m = jnp.maximum(jnp.maximum(ls[0], ls[1]), ls[2])
        es = [jnp.exp(l - m) for l in ls]
        inv = 1.0 / (es[0] + es[1] + es[2])
        y_ref[...] = jnp.concatenate([o[...] * e * inv for o, e in zip((o0, o1, o2), es)], axis=1).astype(bf16)

    part = pl.BlockSpec((tm, GROUP_W), lambda i: (i, 0))
    return pl.pallas_call(
        body, name=name, grid=(M // tm,),
        in_specs=[part] * 6,
        out_specs=pl.BlockSpec((tm, MAIN_W), lambda i: (i, 0)),
        out_shape=_sds((M, D_MODEL), bf16),
        compiler_params=_params(("parallel",)),
    )(*os_, *lses)


def combine_bwd(dyc, os_, lses, *, name):
    M = os_[0].shape[0]
    tm = min(512, M)

    def body(dy_ref, o0, o1, o2, l0, l1, l2, d0, d1, d2, c0, c1, c2):
        r = lax.broadcasted_iota(jnp.int32, (GROUP_W, GROUP_W), 0) // HEAD_DIM
        c = lax.broadcasted_iota(jnp.int32, (GROUP_W, GROUP_W), 1) // HEAD_DIM
        ones = (r == c).astype(f32)
        dy = dy_ref[...].astype(f32)
        ls = [l0[...], l1[...], l2[...]]
        m = jnp.maximum(jnp.maximum(ls[0], ls[1]), ls[2])
        es = [jnp.exp(l - m) for l in ls]
        inv = 1.0 / (es[0] + es[1] + es[2])
        total = 0.0
        alphas = []
        for g, (o, e, d_ref) in enumerate(zip((o0, o1, o2), es, (d0, d1, d2))):
            a = e * inv
            dyg = dy[:, g * GROUP_W:(g + 1) * GROUP_W]
            d_ref[...] = dyg * a
            dsum = jnp.dot(dyg * o[...], ones, precision=lax.Precision.HIGHEST, preferred_element_type=f32)
            total = total + a * dsum
            alphas.append(a)
        for a, c_ref in zip(alphas, (c0, c1, c2)):
            c_ref[...] = -a * total

    part = pl.BlockSpec((tm, GROUP_W), lambda i: (i, 0))
    outs = pl.pallas_call(
        body, name=name, grid=(M // tm,),
        in_specs=[pl.BlockSpec((tm, MAIN_W), lambda i: (i, 0))] + [part] * 6,
        out_specs=[part] * 6,
        out_shape=[_sds((M, GROUP_W), f32)] * 6,
        compiler_params=_params(("parallel",)),
    )(dyc, *os_, *lses)
    return outs[:3], outs[3:]


def dil_bwd(q, k, kv, do, cc, lse, g, dil, *, name, n_seq):
    M = q.shape[0]
    nsub = SEQ // BAND
    per_res = nsub // dil

    def body(q_ref, k_ref, v_ref, do_ref, c_ref, l_ref, dq_ref, dk_ref, dv_ref):
        first = _first_head()
        for r in range(dil):
            carry = None
            for nl in range(per_res):
                sub = nl * dil + r
                qs = _rows(q_ref, sub, dil) * SCALE
                dos = _rows(do_ref, sub, dil)
                cs = _rows(c_ref, sub, dil)
                ls = _rows(l_ref, sub, dil)
                kc = _keys(None, k_ref, sub, dil).astype(bf16)
                vc = _keys(None, v_ref, sub, dil).astype(bf16)
                nkeys = kc.shape[0]
                mask = _band_mask(nkeys, True)
                dqs = []
                dkc = jnp.zeros((nkeys, PAIR_W), f32)
                dvc = jnp.zeros((nkeys, PAIR_W), f32)
                for hh in range(2):
                    lm = first if hh == 0 else ~first
                    qm = jnp.where(lm, qs, 0.0).astype(bf16)
                    dom = jnp.where(lm, dos, 0.0).astype(bf16)
                    s = jnp.where(mask, lax.dot_general(qm, kc, NT, preferred_element_type=f32), NEG)
                    p = jnp.exp(s - _col(ls, hh))
                    dp = lax.dot_general(dom, vc, NT, preferred_element_type=f32)
                    ds = (p * (dp + _col(cs, hh))).astype(bf16)
                    dqs.append(jnp.dot(ds, kc, preferred_element_type=f32) * SCALE)
                    dkc = dkc + lax.dot_general(ds, qm, TN, preferred_element_type=f32)
                    dvc = dvc + lax.dot_general(p.astype(bf16), dom, TN, preferred_element_type=f32)
                _store_rows(dq_ref, sub, dil, jnp.where(first, dqs[0], dqs[1]))
                if nkeys == 2 * BAND:
                    _store_rows(dk_ref, sub - dil, dil, carry[0] + dkc[:BAND])
                    _store_rows(dv_ref, sub - dil, dil, carry[1] + dvc[:BAND])
                    carry = (dkc[BAND:], dvc[BAND:])
                else:
                    carry = (dkc, dvc)
            _store_rows(dk_ref, (per_res - 1) * dil + r, dil, carry[0])
            _store_rows(dv_ref, (per_res - 1) * dil + r, dil, carry[1])

    def spec(col0):
        return pl.BlockSpec((SEQ, PAIR_W), lambda b, p: (b, col0 + p))

    out = spec(0)
    return pl.pallas_call(
        body, name=name, grid=(n_seq, 2),
        in_specs=[spec(2 * g), spec(2 * g), spec(6 + 2 * g), spec(0), spec(0), spec(0)],
        out_specs=[out, out, out],
        out_shape=[_sds((M, GROUP_W), f32)] * 3,
        compiler_params=_params(("parallel", "parallel")),
    )(q, k, kv, do, cc, lse)


def _blockdiag(wp):
    out = jnp.zeros((MAIN_W, MAIN_W), wp.dtype)
    for gi in range(len(POOL_WINDOWS)):
        sl = slice(gi * POOL_GROUP, (gi + 1) * POOL_GROUP)
        out = out.at[sl, sl].set(wp[gi])
    return out


def _unblockdiag(w):
    return jnp.stack([w[gi * POOL_GROUP:(gi + 1) * POOL_GROUP, gi * POOL_GROUP:(gi + 1) * POOL_GROUP]
                      for gi in range(len(POOL_WINDOWS))])


def local_step(x, mem, positions, target, P, layer_weights, kv_weight, emit_grads):
    n_seq = x.shape[0]
    M = n_seq * SEQ
    xs = x.reshape(M, D_MODEL)
    mems = mem.reshape(n_seq * N_MEM, D_MODEL)
    pos = positions.reshape(M, 1).astype(f32)
    cos, sin = rope_tables(pos, name="rope_tables")
    gains = P["norm_gains"]

    def gain(l, k):
        return gains[l, k].reshape(1, D_MODEL)

    saved = []
    kvs = None
    for l in range(DEPTH):
        W, started = layer_weights(l, "mix", xs)
        sv = {"x": xs, "W": W}
        z, h1 = rms_matmul(xs, gain(l, 0), W["w_in"], name=f"l{l}_in", out_dtype=f32, after=started)
        kvm, mn = rms_matmul(mems, P["mem_norm"][l].reshape(1, D_MODEL), W["w_mem_kv"],
                             name=f"l{l}_memkv", out_dtype=bf16)
        sv.update(z=z, h1=h1, kvm=kvm, mn=mn)
        if l < N_A_LAYERS:
            wbd = _blockdiag(P["w_pool"][l].astype(bf16))
            psc = P["pool_scale"][l].reshape(1, MAIN_W)
            p, y_main = pool_fwd(z, wbd, psc, name=f"l{l}_pool")
            sv.update(p=p, wbd=wbd, psc=psc)
        else:
            qrot = rope_fwd(z, cos, sin, name=f"l{l}_ropeq")
            os_, lses = [], []
            for g, (_, dil) in enumerate(DIL_PATTERNS):
                o, lse = dil_fwd(qrot, kvs["krot"], kvs["kv"], g, dil, name=f"l{l}_dil{g}", n_seq=n_seq)
                os_.append(o)
                lses.append(lse)
            y_main = combine_fwd(os_, lses, name=f"l{l}_comb")
            sv.update(qrot=qrot, os=os_, lses=lses)
        ycat = memattn_fwd(z, kvm, y_main, name=f"l{l}_memattn", n_seq=n_seq)
        y, x1 = matmul_rms_res(ycat, W["w_out"], gain(l, 1), xs, name=f"l{l}_out")
        W.update(layer_weights(l, "gu", x1)[0])
        fg, fu, a, h2 = rms_gate_up(x1, gain(l, 2), W["w_gate_up"], name=f"l{l}_gu")
        W.update(layer_weights(l, "down", a)[0])
        y2, x2 = matmul_rms_res(a, W["w_down"], gain(l, 3), x1, name=f"l{l}_down")
        sv.update(ycat=ycat, y=y, x1=x1, fg=fg, fu=fu, h2=h2, a=a, y2=y2)
        saved.append(sv)
        xs = x2
        if l == N_A_LAYERS - 1:
            w_kv = kv_weight(xs)
            kv, hkv = rms_matmul(xs, P["kv_norm"].reshape(1, D_MODEL), w_kv, name="kv_proj", out_dtype=f32,
                                 transposed=True)
            krot = rope_fwd(kv, cos, sin, name="ropek")
            kvs = {"kv": kv, "hkv": hkv, "krot": krot, "x": xs, "w_kv": w_kv}

    dx, sq = loss_head(xs, target.reshape(M, D_MODEL), name="loss_head")

    G = {"mem_norm": [None] * DEPTH, "norm_gains": [[None] * 4 for _ in range(DEPTH)],
         "pool_scale": [None] * N_A_LAYERS}
    dk_parts = [[] for _ in range(N_GROUPS)]
    dv_parts = [[] for _ in range(N_GROUPS)]
    emitted = None

    for l in reversed(range(DEPTH)):
        sv = saved[l]
        W = sv["W"]
        gw = {}
        dy2, dgu, G["norm_gains"][l][3] = down_bwd(sv["y2"], gain(l, 3), dx, W["w_down"], sv["fg"], sv["fu"],
                                                   name=f"l{l}_b_dgu", after=emitted)
        gw["w_down"] = matmul(sv["a"], dy2, TN, name=f"l{l}_b_wd", out_dtype=bf16)
        dx1, G["norm_gains"][l][2] = matmul_rms_bwd(dgu, W["w_gate_up"], NN, sv["x1"], gain(l, 2), dx,
                                                    name=f"l{l}_b_dh2")
        gw["w_gate_up"] = matmul(dgu, sv["h2"], TN, name=f"l{l}_b_wgu", out_dtype=bf16)
        emitted = emit_grads(l, "ffn", gw)
        gw = {}
        dy, dycat, G["norm_gains"][l][1] = rms_bwd_matmul(sv["y"], gain(l, 1), dx1, W["w_out"], NT,
                                                          name=f"l{l}_b_dycat", after=emitted)
        gw["w_out"] = matmul(sv["ycat"], dy, TN, name=f"l{l}_b_wout", out_dtype=bf16)
        if l < N_A_LAYERS:
            dz, dwbd, dps = pool_bwd(dycat, sv["p"], sv["wbd"], sv["psc"], name=f"l{l}_b_pool")
            gw["w_pool"] = _unblockdiag(dwbd).reshape(MAIN_W, POOL_GROUP).astype(bf16)
            G["pool_scale"][l] = dps.reshape(MAIN_W)
        else:
            dos, ccs = combine_bwd(dycat, sv["os"], sv["lses"], name=f"l{l}_b_comb")
            dqs = []
            for g, (_, dil) in enumerate(DIL_PATTERNS):
                args = (sv["qrot"], kvs["krot"], kvs["kv"], dos[g], ccs[g], sv["lses"][g], g, dil)
                dq, dk, dv = dil_bwd(*args, name=f"l{l}_b_dil{g}", n_seq=n_seq)
                dqs.append([dq])
                dk_parts[g].append(dk)
                dv_parts[g].append(dv)
            dz = group_sum(dqs, cos, sin, name=f"l{l}_b_ropeq", rotate=True, width=D_MODEL)
        dz, dkvm = memattn_bwd(sv["z"], sv["kvm"], dycat, dz, name=f"l{l}_b_memattn", n_seq=n_seq)
        dmn = matmul(dkvm, W["w_mem_kv"], NT, name=f"l{l}_b_dmn", out_dtype=bf16)
        gw["w_mem_kv"] = matmul(sv["mn"], dkvm, TN, name=f"l{l}_b_wmkv", out_dtype=bf16)
        _, G["mem_norm"][l] = rms_bwd(mems, P["mem_norm"][l].reshape(1, D_MODEL), dmn, None,
                                      name=f"l{l}_b_nmem", out_dtype=bf16)
        gw["w_in"] = matmul(sv["h1"], dz, TN, name=f"l{l}_b_win", out_dtype=bf16)
        dx, G["norm_gains"][l][0] = matmul_rms_bwd(dz, W["w_in"], NT, sv["x"], gain(l, 0), dx1, name=f"l{l}_b_dh1")
        if l == N_A_LAYERS:
            dkv = group_sum(dk_parts, cos, sin, name="b_ropek", rotate=True, width=2 * MAIN_W)
            dkv = group_sum(dv_parts, cos, sin, name="b_sumv", rotate=False, width=2 * MAIN_W, col_block=1, into=dkv)
            gw["w_kv"] = matmul(dkv, kvs["hkv"], TN, name="b_wkv", out_dtype=bf16)
            dx, gkn = matmul_rms_bwd(dkv, kvs["w_kv"], NN, kvs["x"], P["kv_norm"].reshape(1, D_MODEL), dx,
                                     name="b_dhkv")
            G["kv_norm"] = gkn.reshape(D_MODEL)
        emitted = emit_grads(l, "mix", gw)

    small = {"pool_scale": jnp.stack(G["pool_scale"]),
             "mem_norm": jnp.concatenate(G["mem_norm"], axis=0),
             "norm_gains": jnp.stack([jnp.concatenate(r, axis=0) for r in G["norm_gains"]]),
             "kv_norm": G["kv_norm"]}
    return sq[0, 0], dx.reshape(n_seq, SEQ, D_MODEL), small, emitted


def _peer(k):
    x, y, c = lax.axis_index("x"), lax.axis_index("y"), lax.axis_index("c")
    px = 1 - x if k & 4 else x
    py = 1 - y if k & 2 else y
    pc = 1 - c if k & 1 else c
    return (px, py, pc), 4 * px + 2 * py + pc


def _my_index():
    return 4 * lax.axis_index("x") + 2 * lax.axis_index("y") + lax.axis_index("c")


def _src_for(kinds, in_refs, i, idx):
    return in_refs[i] if kinds[i] == "gather" else in_refs[i].at[idx]


def _local_copies(kinds, in_refs, out_refs, local_sems):
    me = _my_index()
    return [pltpu.make_async_copy(_src_for(kinds, in_refs, i, me), out_refs[i].at[me], local_sems.at[i])
            for i in range(len(kinds))]


def _remote_copies(kinds, in_refs, out_refs, send_sems, recv_sems, *, arriving):
    me = _my_index()
    copies = []
    for k in range(1, N_DEV):
        dev, idx = _peer(k)
        for i in range(len(kinds)):
            j = i * (N_DEV - 1) + k - 1
            copies.append(pltpu.make_async_remote_copy(
                src_ref=_src_for(kinds, in_refs, i, idx), dst_ref=out_refs[i].at[idx if arriving else me],
                send_sem=send_sems.at[j], recv_sem=recv_sems.at[j], device_id=dev, device_id_type=MESH))
    return copies


def _out_shape(a, kind):
    return ((N_DEV,) + a.shape) if kind == "gather" else a.shape


def exchange(items, *, name, after=()):
    n = len(items)
    kinds = [k for _, k in items]
    after = list(after)

    def body(*refs):
        in_refs, out_refs = refs[:n], refs[n + len(after):2 * n + len(after)]
        send_sems, recv_sems, local_sems = refs[-3:]
        local = _local_copies(kinds, in_refs, out_refs, local_sems)
        sends = _remote_copies(kinds, in_refs, out_refs, send_sems, recv_sems, arriving=False)
        for cp in local + sends:
            cp.start()
        for cp in _remote_copies(kinds, in_refs, out_refs, send_sems, recv_sems, arriving=True):
            cp.wait_recv()
        for cp in sends:
            cp.wait_send()
        for cp in local:
            cp.wait()

    any_spec = pl.BlockSpec(memory_space=pl.ANY)
    return pl.pallas_call(
        body, name=name,
        in_specs=[any_spec] * (n + len(after)), out_specs=[any_spec] * n,
        out_shape=[_sds(_out_shape(a, k), a.dtype) for a, k in items],
        scratch_shapes=[pltpu.SemaphoreType.DMA((n * (N_DEV - 1),)), pltpu.SemaphoreType.DMA((n * (N_DEV - 1),)),
                        pltpu.SemaphoreType.DMA((n,))],
    )(*[a for a, _ in items], *after)


_HBM = pl.BlockSpec(memory_space=pltpu.HBM)
_SEM = pl.BlockSpec(memory_space=pltpu.SEMAPHORE)
_EFFECT = pltpu.SideEffectType.DATAFLOW_SIDE_EFFECTING


def exchange_start(items, after, *, name):
    n = len(items)
    kinds = [k for _, k in items]

    def body(*refs):
        in_refs, land_refs = refs[:n], refs[n:2 * n]
        send_sems, recv_sems, local_sems = refs[2 * n + 1:2 * n + 4]
        token = refs[-1]
        for cp in (_local_copies(kinds, in_refs, land_refs, local_sems)
                   + _remote_copies(kinds, in_refs, land_refs, send_sems, recv_sems, arriving=False)):
            cp.start()
        token[...] = jnp.zeros_like(token)

    srcs = [pltpu.with_memory_space_constraint(a, pltpu.HBM) for a, _ in items]
    lands = [pltpu.with_memory_space_constraint(lax.empty(_out_shape(a, k), a.dtype), pltpu.HBM) for a, k in items]
    outs = pl.pallas_call(
        body, name=name,
        out_shape=(pltpu.SemaphoreType.DMA((n * (N_DEV - 1),)), pltpu.SemaphoreType.DMA((n * (N_DEV - 1),)),
                   pltpu.SemaphoreType.DMA((n,)),
                   *[pltpu.HBM(a.shape, a.dtype) for a in srcs], *[pltpu.HBM(a.shape, a.dtype) for a in lands],
                   _sds((8, 128), f32)),
        in_specs=[_HBM] * (2 * n) + [pl.BlockSpec(memory_space=pl.ANY)],
        out_specs=(_SEM, _SEM, _SEM, *[_HBM] * (2 * n), pl.BlockSpec(memory_space=pltpu.VMEM)),
        input_output_aliases={i: 3 + i for i in range(2 * n)},
        compiler_params=pltpu.CompilerParams(has_side_effects=_EFFECT),
    )(*srcs, *lands, after)
    return {"kinds": kinds, "sems": outs[:3], "srcs": outs[3:3 + n], "lands": outs[3 + n:3 + 2 * n], "token": outs[-1]}


def exchange_wait(handle, after, *, name):
    kinds = handle["kinds"]
    n = len(kinds)

    def body(*refs):
        in_refs, land_refs = refs[:n], refs[n:2 * n]
        send_sems, recv_sems, local_sems = refs[2 * n:2 * n + 3]
        for cp in _remote_copies(kinds, in_refs, land_refs, send_sems, recv_sems, arriving=True):
            cp.wait_recv()
        for cp in _remote_copies(kinds, in_refs, land_refs, send_sems, recv_sems, arriving=False):
            cp.wait_send()
        for cp in _local_copies(kinds, in_refs, land_refs, local_sems):
            cp.wait()

    srcs, lands = list(handle["srcs"]), list(handle["lands"])
    after = list(after) if isinstance(after, (list, tuple)) else [after]
    outs = pl.pallas_call(
        body, name=name,
        out_shape=tuple(pltpu.HBM(a.shape, a.dtype) for a in srcs + lands),
        in_specs=[_HBM] * (2 * n) + [_SEM] * 3 + [pl.BlockSpec(memory_space=pl.ANY)] * len(after),
        out_specs=tuple([_HBM] * (2 * n)),
        input_output_aliases={i: i for i in range(2 * n)},
        compiler_params=pltpu.CompilerParams(has_side_effects=_EFFECT),
    )(*srcs, *lands, *handle["sems"], *after)
    return list(outs[n:])


def adamw(slots, w, m, v, *, name, layer=None, into=None):
    R, C = w.shape[-2:]
    tr = _tile(R, (256, 128, 64, 32, 16, 8))
    c1 = 1.0 - ADAM_B1 ** ADAM_STEP
    c2 = 1.0 - ADAM_B2 ** ADAM_STEP
    extra = [] if into is None else list(into)

    def body(s_ref, w_ref, m_ref, v_ref, *refs):
        g_ref, d_ref, m2_ref, v2_ref = refs[len(extra):]
        g = s_ref[0].astype(f32)
        for d in range(1, N_DEV):
            g = g + s_ref[d].astype(f32)
        m2 = ADAM_B1 * m_ref[...] + (1.0 - ADAM_B1) * g
        v2 = ADAM_B2 * v_ref[...] + (1.0 - ADAM_B2) * (g * g)
        g_ref[...] = g
        m2_ref[...] = m2
        v2_ref[...] = v2
        d_ref[...] = -ADAM_LR * ((m2 / c1) / (jnp.sqrt(v2 / c2) + ADAM_EPS) + ADAM_WD * w_ref[...])

    if layer is None:
        blk = pl.BlockSpec((tr, C), lambda i: (i, 0))
    else:
        blk = pl.BlockSpec((None, tr, C), lambda i: (layer, i, 0))
    return pl.pallas_call(
        body, name=name, grid=(R // tr,),
        in_specs=[pl.BlockSpec((N_DEV, tr, C), lambda i: (0, i, 0)), blk, blk, blk]
        + [pl.BlockSpec(memory_space=pl.ANY)] * len(extra),
        out_specs=[blk] * 4,
        out_shape=[_sds(w.shape, f32)] * 4,
        input_output_aliases={4 + j: j for j in range(len(extra))},
        compiler_params=_params(("parallel",)),
    )(slots, w, m, v, *extra)


WEIGHTS = ("norm_gains", "mem_norm", "w_in", "w_mem_kv", "w_out", "w_pool", "pool_scale", "kv_norm", "w_kv",
           "w_gate_up", "w_down")
LAYER_MATS = ("w_in", "w_mem_kv", "w_out", "w_gate_up", "w_down")
POOL_SHARD = MAIN_W // N_DEV
KV_SHARD = 2 * MAIN_W // N_DEV
LOOKAHEAD = 2


def _pack_small(gains, pscale):
    lead = gains.shape[:-3]
    g = gains.reshape(lead + (16, 128))
    p = jnp.zeros(lead + (8, 128), f32).at[..., :2, :POOL_SHARD].set(pscale)
    return jnp.concatenate([g, p], axis=-2)


def _unpack_small(a):
    return a[:16].reshape(4, 4, 128), a[16:18, :POOL_SHARD]


def _pack_repl(mem_norm, kv_norm):
    return jnp.concatenate([mem_norm, kv_norm.reshape(1, D_MODEL), jnp.zeros((3, D_MODEL), f32)], axis=0)


def _unpack_repl(a):
    return a[:4], a[4]


def kernel(x, mem, positions, norm_gains, mem_norm, w_in, w_mem_kv, w_out, w_pool, pool_scale, kv_norm, w_kv, w_gate_up, w_down, loss_target, m_norm_gains, m_mem_norm, m_w_in, m_w_mem_kv, m_w_out, m_w_pool, m_pool_scale, m_kv_norm, m_w_kv, m_w_gate_up, m_w_down, v_norm_gains, v_mem_norm, v_w_in, v_w_mem_kv, v_w_out, v_w_pool, v_pool_scale, v_kv_norm, v_w_kv, v_w_gate_up, v_w_down):
    w = dict(norm_gains=norm_gains, mem_norm=mem_norm, w_in=w_in, w_mem_kv=w_mem_kv, w_out=w_out, w_pool=w_pool,
             pool_scale=pool_scale, kv_norm=kv_norm, w_kv=w_kv, w_gate_up=w_gate_up, w_down=w_down)
    m = dict(norm_gains=m_norm_gains, mem_norm=m_mem_norm, w_in=m_w_in, w_mem_kv=m_w_mem_kv, w_out=m_w_out,
             w_pool=m_w_pool, pool_scale=m_pool_scale, kv_norm=m_kv_norm, w_kv=m_w_kv, w_gate_up=m_w_gate_up,
             w_down=m_w_down)
    v = dict(norm_gains=v_norm_gains, mem_norm=v_mem_norm, w_in=v_w_in, w_mem_kv=v_w_mem_kv, w_out=v_w_out,
             w_pool=v_w_pool, pool_scale=v_pool_scale, kv_norm=v_kv_norm, w_kv=v_w_kv, w_gate_up=v_w_gate_up,
             w_down=v_w_down)

    def transposed_view(d):
        d = dict(d)
        d["w_gate_up"] = jnp.swapaxes(d["w_gate_up"], 1, 2)
        d["w_kv"] = jnp.swapaxes(d["w_kv"], 0, 1)
        return d

    wv, mv, vv = transposed_view(w), transposed_view(m), transposed_view(v)

    small = _pack_small(norm_gains, pool_scale)
    (gsmall,) = exchange([(small, "gather")], name="gather_small")
    P = {"norm_gains": jnp.moveaxis(gsmall[:, :16].reshape(N_DEV, 4, 4, 128), 0, 2).reshape(4, 4, D_MODEL),
         "pool_scale": jnp.moveaxis(gsmall[:, 16:18, :POOL_SHARD], 0, 1).reshape(2, MAIN_W),
         "mem_norm": mem_norm, "kv_norm": kv_norm, "w_pool": w_pool}

    PARTS = {"mix": ("w_in", "w_mem_kv", "w_out"), "ffn": ("w_gate_up", "w_down"), "gu": ("w_gate_up",),
             "down": ("w_down",)}

    def parts_of(l):
        return ("mix", "gu", "down") if l == 0 else ("mix", "ffn")

    def part_items(l, part):
        items = [(wv[k][l].astype(bf16), "gather") for k in PARTS[part]]
        if part == "ffn" and l == N_A_LAYERS - 1:
            items.append((wv["w_kv"].astype(bf16), "gather"))
        return items

    handles = {}

    def start_layer(l, after):
        for part in parts_of(l):
            handles[l, part] = exchange_start(part_items(l, part), after, name=f"gather_start_{part}_l{l}")
            after = handles[l, part]["token"]
        return after

    token = gsmall
    for l in range(LOOKAHEAD):
        token = start_layer(l, token)
    landed = {}

    def layer_weights(l, part, after):
        if part not in parts_of(l):
            if part == "down":
                return {}, None
            part = "ffn"
        first = l == 0 and part == "mix"
        got = exchange_wait(handles[l, part], token if first else after, name=f"gather_wait_{part}_l{l}")
        landed[l, part] = got
        started = None
        if part == "mix" and l + LOOKAHEAD < DEPTH:
            started = start_layer(l + LOOKAHEAD, got[0])
        W = {k: g.reshape(-1, g.shape[-1]) for k, g in zip(PARTS[part], got)}
        return W, started

    def kv_weight(after):
        g = landed[N_A_LAYERS - 1, "ffn"][len(PARTS["ffn"])]
        return g.reshape(2 * MAIN_W, D_MODEL)

    ghandles = {}

    def emit_grads(l, part, gw):
        items = [(gw[k].reshape((N_DEV, -1) + gw[k].shape[-1:]), "scatter") for k in PARTS[part]]
        if part == "mix" and l == N_A_LAYERS:
            items.append((gw["w_kv"].reshape(N_DEV, KV_SHARD, D_MODEL), "scatter"))
        if part == "mix" and l < N_A_LAYERS:
            items.append((gw["w_pool"], "gather"))
        ghandles[l, part] = exchange_start(items, gsmall, name=f"scatter_start_{part}_l{l}")
        return ghandles[l, part]["token"]

    sq, grad_x, GS, emitted = local_step(x, mem, positions, loss_target, P, layer_weights, kv_weight, emit_grads)
    loss = lax.psum(0.5 * sq / D_MODEL, ("x", "y", "c"))

    def pool3(a):
        return a.reshape(N_A_LAYERS, MAIN_W, POOL_GROUP)

    out = {}
    after = [emitted]

    def finish_layer(l, after):
        for part in ("ffn", "mix"):
            got = exchange_wait(ghandles[l, part], after, name=f"scatter_wait_{part}_l{l}")
            after = []
            for k, slots in zip(PARTS[part], got):
                out[k] = adamw(slots, wv[k], mv[k], vv[k], name=f"adamw_{k}_l{l}", layer=l, into=out.get(k))
                after.append(out[k][0])
            if part == "mix" and l == N_A_LAYERS:
                out["w_kv"] = adamw(got[-1], wv["w_kv"], mv["w_kv"], vv["w_kv"], name="adamw_w_kv")
                after.append(out["w_kv"][0])
            if part == "mix" and l < N_A_LAYERS:
                out["w_pool"] = adamw(got[-1], pool3(w_pool), pool3(m_w_pool), pool3(v_w_pool), name=f"adamw_w_pool_l{l}",
                                      layer=l, into=out.get("w_pool"))
                after.append(out["w_pool"][0])
        return after

    for l in reversed(range(1, DEPTH)):
        after = finish_layer(l, after)

    gs = _pack_small(jnp.moveaxis(GS["norm_gains"].reshape(4, 4, N_DEV, 128), 2, 0),
                     jnp.moveaxis(GS["pool_scale"].reshape(2, N_DEV, POOL_SHARD), 1, 0))
    parts_small, parts_repl = exchange(
        [(gs, "scatter"), (_pack_repl(GS["mem_norm"], GS["kv_norm"]), "gather")],
        name="exchange_small_grads", after=after)
    finish_layer(0, [parts_small])
    out["w_gate_up"] = [jnp.swapaxes(r, 1, 2) for r in out["w_gate_up"]]
    out["w_kv"] = [jnp.swapaxes(r, 0, 1) for r in out["w_kv"]]
    out["w_pool"] = [r.reshape(w_pool.shape) for r in out["w_pool"]]

    res = adamw(parts_small, small, _pack_small(m_norm_gains, m_pool_scale), _pack_small(v_norm_gains, v_pool_scale),
                name="adamw_small")
    out["norm_gains"], out["pool_scale"] = zip(*[_unpack_small(r) for r in res])
    res = adamw(parts_repl, _pack_repl(mem_norm, kv_norm), _pack_repl(m_mem_norm, m_kv_norm),
                _pack_repl(v_mem_norm, v_kv_norm), name="adamw_repl")
    out["mem_norm"], out["kv_norm"] = zip(*[_unpack_repl(r) for r in res])

    return (loss, grad_x, *[out[k][0] for k in WEIGHTS], *[out[k][1] for k in WEIGHTS],
            *[out[k][2] for k in WEIGHTS], *[out[k][3] for k in WEIGHTS])
```

```python
import numpy as np
import jax
import jax.numpy as jnp
from jax import lax
from jax.experimental import pallas as pl
from jax.experimental.pallas import tpu as pltpu

f32 = jnp.float32
bf16 = jnp.bfloat16

D_MODEL = 1024
SEQ = 2048
DEPTH = 4
N_MEM = 256
HEAD_DIM = 64
N_MEM_HEADS = 4
MEM_W = 256
MAIN_W = 768
POOL_WINDOWS = (2, 4, 8, 16)
POOL_GROUP = 192
POOL_HALO = 16
DIL_PATTERNS = ((128, 1), (512, 4), (2048, 16))
N_GROUPS = 3
GROUP_W = 256
BAND = 128
N_A_LAYERS = 2
D_FF = 2816
ROPE_THETA = 10000.0
EPS = 1e-6
NEG = -1e30
SCALE = HEAD_DIM ** -0.5
N_DEV = 8

ADAM_LR = 0.001
ADAM_B1 = 0.9
ADAM_B2 = 0.999
ADAM_EPS = 1e-08
ADAM_WD = 0.01
ADAM_STEP = 10

VMEM_LIMIT_BYTES = 56 * 1024 * 1024
MESH = pl.DeviceIdType.MESH

NN = (((1,), (0,)), ((), ()))
NT = (((1,), (1,)), ((), ()))
TN = (((0,), (0,)), ((), ()))


def _params(sem=None):
    return pltpu.CompilerParams(dimension_semantics=sem, vmem_limit_bytes=VMEM_LIMIT_BYTES)


def _tile(n, cands):
    for c in cands:
        if n % c == 0:
            return c
    return n


def _sds(shape, dtype):
    return jax.ShapeDtypeStruct(tuple(shape), dtype)


def _rms_r(v):
    return lax.rsqrt(jnp.mean(v * v, axis=-1, keepdims=True) + EPS)


def rms_matmul(x, gain, w, *, name, out_dtype, transposed=False, after=None):
    M, K = x.shape
    N = w.shape[0] if transposed else w.shape[1]
    tm = min(1024, M)
    tn = _tile(N, (512, 256, 128))
    order = [] if after is None else [after]

    def body(x_ref, g_ref, w_ref, *refs):
        z_ref, h_ref = refs[len(order):]

        @pl.when(pl.program_id(1) == 0)
        def _():
            xv = x_ref[...]
            h_ref[...] = (xv * _rms_r(xv) * g_ref[...]).astype(bf16)

        z_ref[...] = lax.dot_general(h_ref[...], w_ref[...], NT if transposed else NN,
                                     preferred_element_type=f32).astype(z_ref.dtype)

    w_spec = pl.BlockSpec((tn, K), lambda i, j: (j, 0)) if transposed else pl.BlockSpec((K, tn), lambda i, j: (0, j))
    return pl.pallas_call(
        body, name=name, grid=(M // tm, N // tn),
        in_specs=[pl.BlockSpec((tm, K), lambda i, j: (i, 0)),
                  pl.BlockSpec((1, K), lambda i, j: (0, 0)),
                  w_spec] + [pl.BlockSpec(memory_space=pl.ANY)] * len(order),
        out_specs=[pl.BlockSpec((tm, tn), lambda i, j: (i, j)), pl.BlockSpec((tm, K), lambda i, j: (i, 0))],
        out_shape=[_sds((M, N), out_dtype), _sds((M, K), bf16)],
        compiler_params=_params(("parallel", "arbitrary")),
    )(x, gain, w, *order)


def matmul_rms_res(a, w, gain, res, *, name):
    M, K = a.shape
    N = w.shape[1]
    tm = min(512, M)

    def body(a_ref, w_ref, g_ref, r_ref, y_ref, x_ref):
        y = jnp.dot(a_ref[...], w_ref[...], preferred_element_type=f32)
        y_ref[...] = y.astype(bf16)
        x_ref[...] = r_ref[...] + y * _rms_r(y) * g_ref[...]

    row = pl.BlockSpec((tm, N), lambda i: (i, 0))
    return pl.pallas_call(
        body, name=name, grid=(M // tm,),
        in_specs=[pl.BlockSpec((tm, K), lambda i: (i, 0)),
                  pl.BlockSpec((K, N), lambda i: (0, 0)),
                  pl.BlockSpec((1, N), lambda i: (0, 0)),
                  row],
        out_specs=[row, row],
        out_shape=[_sds((M, N), bf16), _sds((M, N), f32)],
        compiler_params=_params(("parallel",)),
    )(a, w, gain, res)


def matmul(a, b, dims, *, name, out_dtype):
    if dims is TN:
        K, M = a.shape
        tm = _tile(M, (512, 256, 128))
        a_spec = pl.BlockSpec((K, tm), lambda i: (0, i))
    else:
        M, K = a.shape
        tm = _tile(M, (512, 256, 128))
        a_spec = pl.BlockSpec((tm, K), lambda i: (i, 0))
    N = b.shape[0] if dims is NT else b.shape[1]

    def body(a_ref, b_ref, o_ref):
        o_ref[...] = lax.dot_general(a_ref[...].astype(bf16), b_ref[...].astype(bf16), dims,
                                     preferred_element_type=f32).astype(o_ref.dtype)

    return pl.pallas_call(
        body, name=name, grid=(M // tm,),
        in_specs=[a_spec, pl.BlockSpec(b.shape, lambda i: (0, 0))],
        out_specs=pl.BlockSpec((tm, N), lambda i: (i, 0)),
        out_shape=_sds((M, N), out_dtype),
        compiler_params=_params(("parallel",)),
    )(a, b)


def rms_gate_up(x, gain, wt, *, name):
    M, K = x.shape
    tm = min(2048, M)
    tn = _tile(D_FF, (256, 128))
    nj = D_FF // tn

    def body(x_ref, gn_ref, wg_ref, wu_ref, g_ref, u_ref, a_ref, h_ref):
        @pl.when(pl.program_id(1) == 0)
        def _():
            xv = x_ref[...]
            h_ref[...] = (xv * _rms_r(xv) * gn_ref[...]).astype(bf16)

        h = h_ref[...]
        g = lax.dot_general(h, wg_ref[...], NT, preferred_element_type=f32)
        u = lax.dot_general(h, wu_ref[...], NT, preferred_element_type=f32)
        g_ref[...] = g.astype(bf16)
        u_ref[...] = u.astype(bf16)
        a_ref[...] = (g * (1.0 / (1.0 + jnp.exp(-g))) * u).astype(bf16)

    col = pl.BlockSpec((tm, tn), lambda i, j: (i, j))
    return pl.pallas_call(
        body, name=name, grid=(M // tm, nj),
        in_specs=[pl.BlockSpec((tm, K), lambda i, j: (i, 0)),
                  pl.BlockSpec((1, K), lambda i, j: (0, 0)),
                  pl.BlockSpec((tn, K), lambda i, j: (j, 0)),
                  pl.BlockSpec((tn, K), lambda i, j: (j + nj, 0))],
        out_specs=[col, col, col, pl.BlockSpec((tm, K), lambda i, j: (i, 0))],
        out_shape=[_sds((M, D_FF), bf16)] * 3 + [_sds((M, K), bf16)],
        compiler_params=_params(("parallel", "arbitrary")),
    )(x, gain, wt, wt)


def _rms_bwd_math(yv, gain, dn):
    r = _rms_r(yv)
    q = dn * gain
    dy = r * q - yv * (r * r * r) * jnp.mean(q * yv, axis=-1, keepdims=True)
    return dy, jnp.sum(dn * yv * r, axis=0, keepdims=True)


def _accumulate(ref, val):
    @pl.when(pl.program_id(0) == 0)
    def _():
        ref[...] = jnp.zeros_like(ref)

    ref[...] += val


def down_bwd(y, gain, dn, w_down, g, u, *, name, after=None):
    M, K = y.shape
    tm = min(512, M)
    order = [] if after is None else [after]

    def body(y_ref, gn_ref, dn_ref, w_ref, g_ref, u_ref, *refs):
        dy_ref, o_ref, dg_ref = refs[len(order):]
        dy, dgain = _rms_bwd_math(y_ref[...].astype(f32), gn_ref[...], dn_ref[...])
        dy = dy.astype(bf16)
        dy_ref[...] = dy
        _accumulate(dg_ref, dgain)
        da = lax.dot_general(dy, w_ref[...], NT, preferred_element_type=f32)
        g = g_ref[...].astype(f32)
        u = u_ref[...].astype(f32)
        s = 1.0 / (1.0 + jnp.exp(-g))
        o_ref[:, :D_FF] = (da * u * s * (1.0 + g * (1.0 - s))).astype(bf16)
        o_ref[:, D_FF:] = (da * g * s).astype(bf16)

    row = pl.BlockSpec((tm, K), lambda i: (i, 0))
    vec = pl.BlockSpec((1, K), lambda i: (0, 0))
    wide = pl.BlockSpec((tm, D_FF), lambda i: (i, 0))
    return pl.pallas_call(
        body, name=name, grid=(M // tm,),
        in_specs=[row, vec, row, pl.BlockSpec((D_FF, K), lambda i: (0, 0)), wide, wide]
        + [pl.BlockSpec(memory_space=pl.ANY)] * len(order),
        out_specs=[row, pl.BlockSpec((tm, 2 * D_FF), lambda i: (i, 0)), vec],
        out_shape=[_sds((M, K), bf16), _sds((M, 2 * D_FF), bf16), _sds((1, K), f32)],
        compiler_params=_params(("arbitrary",)),
    )(y, gain, dn, w_down, g, u, *order)


def rms_bwd_matmul(y, gain, dn, w, dims, *, name, after=None):
    M, K = y.shape
    N = w.shape[0] if dims is NT else w.shape[1]
    tm = min(512, M)
    order = [] if after is None else [after]

    def body(y_ref, gn_ref, dn_ref, w_ref, *refs):
        dy_ref, o_ref, dg_ref = refs[len(order):]
        dy, dgain = _rms_bwd_math(y_ref[...].astype(f32), gn_ref[...], dn_ref[...].astype(f32))
        dy = dy.astype(bf16)
        dy_ref[...] = dy
        _accumulate(dg_ref, dgain)
        o_ref[...] = lax.dot_general(dy, w_ref[...], dims, preferred_element_type=f32).astype(bf16)

    row = pl.BlockSpec((tm, K), lambda i: (i, 0))
    vec = pl.BlockSpec((1, K), lambda i: (0, 0))
    return pl.pallas_call(
        body, name=name, grid=(M // tm,),
        in_specs=[row, vec, row, pl.BlockSpec(w.shape, lambda i: (0, 0))]
        + [pl.BlockSpec(memory_space=pl.ANY)] * len(order),
        out_specs=[row, pl.BlockSpec((tm, N), lambda i: (i, 0)), vec],
        out_shape=[_sds((M, K), bf16), _sds((M, N), bf16), _sds((1, K), f32)],
        compiler_params=_params(("arbitrary",)),
    )(y, gain, dn, w, *order)


def matmul_rms_bwd(a, b, dims, y, gain, res, *, name):
    M, K = a.shape
    N = y.shape[1]
    tm = 512

    def body(a_ref, b_ref, y_ref, gn_ref, r_ref, dx_ref, dg_ref):
        dn = lax.dot_general(a_ref[...], b_ref[...], dims, preferred_element_type=f32)
        dy, dgain = _rms_bwd_math(y_ref[...], gn_ref[...], dn)
        dx_ref[...] = dy + r_ref[...]
        _accumulate(dg_ref, dgain)

    row = pl.BlockSpec((tm, N), lambda i: (i, 0))
    vec = pl.BlockSpec((1, N), lambda i: (0, 0))
    return pl.pallas_call(
        body, name=name, grid=(M // tm,),
        in_specs=[pl.BlockSpec((tm, K), lambda i: (i, 0)), pl.BlockSpec(b.shape, lambda i: (0, 0)), row, vec, row],
        out_specs=[row, vec],
        out_shape=[_sds((M, N), f32), _sds((1, N), f32)],
        compiler_params=_params(("arbitrary",)),
    )(a, b, y, gain, res)


def rms_bwd(y, gain, dn, res, *, name, out_dtype, after=None):
    M, N = y.shape
    tm = min(512, M)
    has_res = res is not None
    order = [] if after is None else [after]

    def body(*refs):
        y_ref, g_ref, dn_ref = refs[:3]
        r_ref = refs[3] if has_res else None
        dy_ref, dg_ref = refs[-2:]
        dy, dgain = _rms_bwd_math(y_ref[...].astype(f32), g_ref[...], dn_ref[...].astype(f32))
        if has_res:
            dy = dy + r_ref[...]
        dy_ref[...] = dy.astype(dy_ref.dtype)
        _accumulate(dg_ref, dgain)

    row = pl.BlockSpec((tm, N), lambda i: (i, 0))
    vec = pl.BlockSpec((1, N), lambda i: (0, 0))
    args = [y, gain, dn] + ([res] if has_res else []) + order
    return pl.pallas_call(
        body, name=name, grid=(M // tm,),
        in_specs=[row, vec, row] + ([row] if has_res else []) + [pl.BlockSpec(memory_space=pl.ANY)] * len(order),
        out_specs=[row, vec],
        out_shape=[_sds((M, N), out_dtype), _sds((1, N), f32)],
        compiler_params=_params(("arbitrary",)),
    )(*args)


def loss_head(x, target, *, name):
    M, N = x.shape
    tm = min(512, M)

    def body(x_ref, t_ref, dx_ref, l_ref):
        e = x_ref[...] - t_ref[...]
        dx_ref[...] = e * (1.0 / N)

        @pl.when(pl.program_id(0) == 0)
        def _():
            l_ref[...] = jnp.zeros_like(l_ref)

        l_ref[...] += jnp.sum(jnp.sum(e * e, axis=0, keepdims=True), axis=1, keepdims=True)

    row = pl.BlockSpec((tm, N), lambda i: (i, 0))
    return pl.pallas_call(
        body, name=name, grid=(M // tm,),
        in_specs=[row, row],
        out_specs=[row, pl.BlockSpec((8, 128), lambda i: (0, 0))],
        out_shape=[_sds((M, N), f32), _sds((8, 128), f32)],
        compiler_params=_params(("arbitrary",)),
    )(x, target)


def _pool_select(a1, a2, a3, a4):
    col = lax.broadcasted_iota(jnp.int32, (1, MAIN_W), 1) // POOL_GROUP
    return jnp.where(col == 0, a1, jnp.where(col == 1, a2, jnp.where(col == 2, a3, a4)))


def _pool_count(t):
    col = lax.broadcasted_iota(jnp.int32, (1, MAIN_W), 1) // POOL_GROUP
    win = jnp.where(col == 0, 2, jnp.where(col == 1, 4, jnp.where(col == 2, 8, 16)))
    return jnp.minimum(t + 1, win).astype(f32)


def pool_fwd(z, wbd, scale, *, name):
    M = z.shape[0]
    tm = 256
    nper = SEQ // tm
    hb = tm // POOL_HALO

    def body(zc_ref, zh_ref, w_ref, s_ref, p_ref, y_ref):
        i = pl.program_id(0)
        seq_blk = i % nper
        halo = jnp.where(seq_blk == 0, 0.0, zh_ref[...])
        u = zc_ref[...]
        ext = jnp.concatenate([halo, u], axis=0)
        a1 = ext + pltpu.roll(ext, 1, 0)
        a2 = a1 + pltpu.roll(a1, 2, 0)
        a3 = a2 + pltpu.roll(a2, 4, 0)
        a4 = a3 + pltpu.roll(a3, 8, 0)
        sums = _pool_select(a1, a2, a3, a4)[POOL_HALO:]
        t = seq_blk * tm + lax.broadcasted_iota(jnp.int32, (tm, 1), 0)
        p = (sums / _pool_count(t) - u).astype(bf16)
        p_ref[...] = p
        y_ref[...] = (jnp.dot(p, w_ref[...], preferred_element_type=f32) * s_ref[...]).astype(bf16)

    return pl.pallas_call(
        body, name=name, grid=(M // tm,),
        in_specs=[pl.BlockSpec((tm, MAIN_W), lambda i: (i, 0)),
                  pl.BlockSpec((POOL_HALO, MAIN_W), lambda i: (jnp.maximum(i * hb - 1, 0), 0)),
                  pl.BlockSpec((MAIN_W, MAIN_W), lambda i: (0, 0)),
                  pl.BlockSpec((1, MAIN_W), lambda i: (0, 0))],
        out_specs=[pl.BlockSpec((tm, MAIN_W), lambda i: (i, 0)),
                   pl.BlockSpec((tm, MAIN_W), lambda i: (i, 0))],
        out_shape=[_sds((M, MAIN_W), bf16), _sds((M, D_MODEL), bf16)],
        compiler_params=_params(("parallel",)),
    )(z, z, wbd, scale)


def pool_bwd(dyc, p, wbd, scale, *, name):
    M = p.shape[0]
    tm = 256
    nper = SEQ // tm
    hb = tm // POOL_HALO
    last_hb = M // POOL_HALO - 1

    def body(dy_ref, dyh_ref, p_ref, w_ref, s_ref, dz_ref, dw_ref, ds_ref):
        i = pl.program_id(0)
        seq_blk = i % nper
        dy = dy_ref[...].astype(f32)
        pv = p_ref[...]
        w = w_ref[...]
        sc = s_ref[...]

        @pl.when(i == 0)
        def _():
            dw_ref[...] = jnp.zeros_like(dw_ref)
            ds_ref[...] = jnp.zeros_like(ds_ref)

        v = jnp.dot(pv, w, preferred_element_type=f32)
        ds_ref[...] += jnp.sum(dy * v, axis=0, keepdims=True)
        dv = (dy * sc).astype(bf16)
        dw_ref[...] += lax.dot_general(pv, dv, TN, preferred_element_type=f32)
        dp = lax.dot_general(dv, w, NT, preferred_element_type=f32)
        dvh = jnp.where(seq_blk == nper - 1, 0.0, dyh_ref[...].astype(f32) * sc).astype(bf16)
        dph = lax.dot_general(dvh, w, NT, preferred_element_type=f32)
        ext = jnp.concatenate([dp, dph], axis=0)
        n = tm + POOL_HALO
        t = seq_blk * tm + lax.broadcasted_iota(jnp.int32, (n, 1), 0)
        e = ext / _pool_count(t)
        b1 = e + pltpu.roll(e, n - 1, 0)
        b2 = b1 + pltpu.roll(b1, n - 2, 0)
        b3 = b2 + pltpu.roll(b2, n - 4, 0)
        b4 = b3 + pltpu.roll(b3, n - 8, 0)
        dz_ref[...] = (_pool_select(b1, b2, b3, b4)[:tm] - dp).astype(dz_ref.dtype)

    return pl.pallas_call(
        body, name=name, grid=(M // tm,),
        in_specs=[pl.BlockSpec((tm, MAIN_W), lambda i: (i, 0)),
                  pl.BlockSpec((POOL_HALO, MAIN_W), lambda i: (jnp.minimum((i + 1) * hb, last_hb), 0)),
                  pl.BlockSpec((tm, MAIN_W), lambda i: (i, 0)),
                  pl.BlockSpec((MAIN_W, MAIN_W), lambda i: (0, 0)),
                  pl.BlockSpec((1, MAIN_W), lambda i: (0, 0))],
        out_specs=[pl.BlockSpec((tm, MAIN_W), lambda i: (i, 0)),
                   pl.BlockSpec((MAIN_W, MAIN_W), lambda i: (0, 0)),
                   pl.BlockSpec((1, MAIN_W), lambda i: (0, 0))],
        out_shape=[_sds((M, D_MODEL), bf16), _sds((MAIN_W, MAIN_W), f32), _sds((1, MAIN_W), f32)],
        compiler_params=_params(("arbitrary",)),
    )(dyc, dyc, p, wbd, scale)


def _mem_probs(q, kv, h):
    hs = slice(h * HEAD_DIM, (h + 1) * HEAD_DIM)
    qh = q[:, hs]
    kh = kv[:, hs]
    s = lax.dot_general(qh, kh, NT, preferred_element_type=f32) * SCALE
    m = jnp.max(s, axis=-1, keepdims=True)
    e = jnp.exp(s - m)
    return qh, kh, e / jnp.sum(e, axis=-1, keepdims=True)


def memattn_fwd(z, kvm, ycat, *, name, n_seq):
    M = z.shape[0]
    tq = 512
    nq = SEQ // tq

    def body(q_ref, kv_ref, _, o_ref):
        q = q_ref[...].astype(bf16)
        kv = kv_ref[...]
        outs = []
        for h in range(N_MEM_HEADS):
            _, _, p = _mem_probs(q, kv, h)
            vh = kv[:, MEM_W + h * HEAD_DIM: MEM_W + (h + 1) * HEAD_DIM]
            outs.append(jnp.dot(p.astype(bf16), vh, preferred_element_type=f32))
        o_ref[...] = jnp.concatenate(outs, axis=1).astype(bf16)

    return pl.pallas_call(
        body, name=name, grid=(n_seq, nq),
        in_specs=[pl.BlockSpec((tq, MEM_W), lambda b, i: (b * nq + i, 3)),
                  pl.BlockSpec((N_MEM, 2 * MEM_W), lambda b, i: (b, 0)),
                  pl.BlockSpec(memory_space=pl.ANY)],
        out_specs=pl.BlockSpec((tq, MEM_W), lambda b, i: (b * nq + i, 3)),
        out_shape=_sds((M, D_MODEL), bf16),
        input_output_aliases={2: 0},
        compiler_params=_params(("parallel", "parallel")),
    )(z, kvm, ycat)


def memattn_bwd(z, kvm, dyc, dz, *, name, n_seq):
    M = z.shape[0]
    tq = 512
    nq = SEQ // tq

    def body(q_ref, kv_ref, dy_ref, _, dq_ref, dkv_ref):
        q = q_ref[...].astype(bf16)
        kv = kv_ref[...]
        dy = dy_ref[...].astype(bf16)
        dqs, dks, dvs = [], [], []
        for h in range(N_MEM_HEADS):
            hs = slice(h * HEAD_DIM, (h + 1) * HEAD_DIM)
            qh, kh, p = _mem_probs(q, kv, h)
            vh = kv[:, MEM_W + h * HEAD_DIM: MEM_W + (h + 1) * HEAD_DIM]
            dyh = dy[:, hs]
            dvs.append(lax.dot_general(p.astype(bf16), dyh, TN, preferred_element_type=f32))
            dp = lax.dot_general(dyh, vh, NT, preferred_element_type=f32)
            ds = (p * (dp - jnp.sum(dp * p, axis=-1, keepdims=True)) * SCALE).astype(bf16)
            dqs.append(jnp.dot(ds, kh, preferred_element_type=f32))
            dks.append(lax.dot_general(ds, qh, TN, preferred_element_type=f32))
        dq_ref[...] = jnp.concatenate(dqs, axis=1).astype(bf16)

        @pl.when(pl.program_id(1) == 0)
        def _():
            dkv_ref[...] = jnp.zeros_like(dkv_ref)

        dkv_ref[...] += jnp.concatenate(dks + dvs, axis=1)

    return pl.pallas_call(
        body, name=name, grid=(n_seq, nq),
        in_specs=[pl.BlockSpec((tq, MEM_W), lambda b, i: (b * nq + i, 3)),
                  pl.BlockSpec((N_MEM, 2 * MEM_W), lambda b, i: (b, 0)),
                  pl.BlockSpec((tq, MEM_W), lambda b, i: (b * nq + i, 3)),
                  pl.BlockSpec(memory_space=pl.ANY)],
        out_specs=[pl.BlockSpec((tq, MEM_W), lambda b, i: (b * nq + i, 3)),
                   pl.BlockSpec((N_MEM, 2 * MEM_W), lambda b, i: (b, 0))],
        out_shape=[_sds((M, D_MODEL), bf16), _sds((n_seq * N_MEM, 2 * MEM_W), f32)],
        input_output_aliases={3: 0},
        compiler_params=_params(("parallel", "arbitrary")),
    )(z, kvm, dyc, dz)


def rope_tables(pos, *, name):
    M = pos.shape[0]
    tm = min(1024, M)
    half = HEAD_DIM // 2
    inv = ROPE_THETA ** (-np.arange(half, dtype=np.float64) / half)
    inv128 = jnp.asarray(np.tile(inv, 4)[None, :], f32)
    sign128 = jnp.asarray(np.tile(np.concatenate([-np.ones(half), np.ones(half)]), 2)[None, :], f32)

    def body(p_ref, f_ref, s_ref, cos_ref, sin_ref):
        ang = p_ref[...] * f_ref[...]
        cos_ref[...] = jnp.cos(ang)
        sin_ref[...] = jnp.sin(ang) * s_ref[...]

    return pl.pallas_call(
        body, name=name, grid=(M // tm,),
        in_specs=[pl.BlockSpec((tm, 1), lambda i: (i, 0)),
                  pl.BlockSpec((1, 128), lambda i: (0, 0)),
                  pl.BlockSpec((1, 128), lambda i: (0, 0))],
        out_specs=[pl.BlockSpec((tm, 128), lambda i: (i, 0)),
                   pl.BlockSpec((tm, 128), lambda i: (i, 0))],
        out_shape=[_sds((M, 128), f32), _sds((M, 128), f32)],
        compiler_params=_params(("parallel",)),
    )(pos, inv128, sign128)


def _swap_halves(x):
    w = x.shape[1]
    first = (lax.broadcasted_iota(jnp.int32, (1, w), 1) % HEAD_DIM) < (HEAD_DIM // 2)
    return jnp.where(first, pltpu.roll(x, w - HEAD_DIM // 2, 1), pltpu.roll(x, HEAD_DIM // 2, 1))


def rope_fwd(src, cos, sin, *, name):
    M = src.shape[0]
    tm = min(512, M)

    def body(x_ref, c_ref, s_ref, o_ref):
        x = x_ref[...].astype(f32)
        c = jnp.tile(c_ref[...], (1, MAIN_W // 128))
        s = jnp.tile(s_ref[...], (1, MAIN_W // 128))
        o_ref[...] = x * c + _swap_halves(x) * s

    return pl.pallas_call(
        body, name=name, grid=(M // tm,),
        in_specs=[pl.BlockSpec((tm, MAIN_W), lambda i: (i, 0)),
                  pl.BlockSpec((tm, 128), lambda i: (i, 0)),
                  pl.BlockSpec((tm, 128), lambda i: (i, 0))],
        out_specs=pl.BlockSpec((tm, MAIN_W), lambda i: (i, 0)),
        out_shape=_sds((M, MAIN_W), f32),
        compiler_params=_params(("parallel",)),
    )(src, cos, sin)


def group_sum(groups, cos, sin, *, name, rotate, width, col_block=0, into=None):
    M = groups[0][0].shape[0]
    tm = min(512, M)
    counts = [len(g) for g in groups]
    flat = [a for g in groups for a in g]
    extra = [] if into is None else [into]

    def body(*refs):
        part_refs = refs[:len(flat)]
        c_ref, s_ref = refs[len(flat):len(flat) + 2]
        o_ref = refs[-1]
        cols, k = [], 0
        for n in counts:
            acc = part_refs[k][...]
            for r in part_refs[k + 1:k + n]:
                acc = acc + r[...]
            cols.append(acc)
            k += n
        d = jnp.concatenate(cols, axis=1)
        if rotate:
            c = jnp.tile(c_ref[...], (1, MAIN_W // 128))
            s = jnp.tile(s_ref[...], (1, MAIN_W // 128))
            d = d * c - _swap_halves(d) * s
        o_ref[...] = d.astype(bf16)

    part = pl.BlockSpec((tm, GROUP_W), lambda i: (i, 0))
    tab = pl.BlockSpec((tm, 128), lambda i: (i, 0))
    return pl.pallas_call(
        body, name=name, grid=(M // tm,),
        in_specs=[part] * len(flat) + [tab, tab] + [pl.BlockSpec(memory_space=pl.ANY)] * len(extra),
        out_specs=pl.BlockSpec((tm, MAIN_W), lambda i: (i, col_block)),
        out_shape=_sds((M, width), bf16),
        input_output_aliases={len(flat) + 2: 0} if extra else {},
        compiler_params=_params(("parallel",)),
    )(*flat, cos, sin, *extra)


PAIR_W = 2 * HEAD_DIM
MIN_BLOCKS = 8


def _dil_geometry(dil):
    nsub = max(dil, MIN_BLOCKS)
    tb = BAND * nsub
    return nsub, tb, SEQ // tb


def _rows(ref, sub, dil):
    if dil == 1:
        return ref[sub * BAND:(sub + 1) * BAND, :]
    nl, r = divmod(sub, dil)
    return ref[pl.ds(nl * BAND * dil + r, BAND, stride=dil), :]


def _store_rows(ref, sub, dil, val):
    if dil == 1:
        ref[sub * BAND:(sub + 1) * BAND, :] = val
    else:
        nl, r = divmod(sub, dil)
        ref[pl.ds(nl * BAND * dil + r, BAND, stride=dil), :] = val


def _keys(prev_ref, own_ref, sub, dil):
    nsub = own_ref.shape[0] // BAND
    if sub >= dil:
        prev = _rows(own_ref, sub - dil, dil)
    elif prev_ref is None:
        return _rows(own_ref, sub, dil)
    else:
        prev = _rows(prev_ref, nsub - dil + sub, dil)
    return jnp.concatenate([prev, _rows(own_ref, sub, dil)], axis=0)


def _band_mask(nkeys, has_prev):
    i = lax.broadcasted_iota(jnp.int32, (BAND, nkeys), 0)
    j = lax.broadcasted_iota(jnp.int32, (BAND, nkeys), 1)
    if nkeys == BAND:
        return j <= i
    return (j >= i) & (j <= i + BAND) & (has_prev | (j >= BAND))


def _first_head():
    return lax.broadcasted_iota(jnp.int32, (1, PAIR_W), 1) < HEAD_DIM


def _col(x, hh):
    return x[:, hh * HEAD_DIM:hh * HEAD_DIM + 1]


def _pair_spec(tb, nblk, col0, which):
    def idx(b, p, i):
        if which < 0:
            i = jnp.maximum(i - 1, 0)
        elif which > 0:
            i = jnp.minimum(i + 1, nblk - 1)
        return (b * nblk + i, col0 + p)
    return pl.BlockSpec((tb, PAIR_W), idx)


def dil_fwd(q, k, kv, g, dil, *, name, n_seq):
    M = q.shape[0]
    nsub, tb, nblk = _dil_geometry(dil)
    with_prev = nblk > 1

    def body(*refs):
        if with_prev:
            q_ref, ko_ref, vo_ref, kp_ref, vp_ref, o_ref, l_ref = refs
        else:
            (q_ref, ko_ref, vo_ref, o_ref, l_ref), kp_ref, vp_ref = refs, None, None
        first = _first_head()
        blk = pl.program_id(2)
        for sub in range(nsub):
            qs = _rows(q_ref, sub, dil) * SCALE
            kc = _keys(kp_ref, ko_ref, sub, dil).astype(bf16)
            vc = _keys(vp_ref, vo_ref, sub, dil).astype(bf16)
            has_prev = True if sub >= dil else blk > 0
            mask = _band_mask(kc.shape[0], has_prev)
            outs, lses = [], []
            for hh in range(2):
                qm = jnp.where(first if hh == 0 else ~first, qs, 0.0).astype(bf16)
                s = jnp.where(mask, lax.dot_general(qm, kc, NT, preferred_element_type=f32), NEG)
                m = jnp.max(s, axis=-1, keepdims=True)
                e = jnp.exp(s - m)
                l = jnp.sum(e, axis=-1, keepdims=True)
                outs.append(jnp.dot(e.astype(bf16), vc, preferred_element_type=f32) * (1.0 / l))
                lses.append(jnp.broadcast_to(m + jnp.log(l), (BAND, PAIR_W)))
            _store_rows(o_ref, sub, dil, jnp.where(first, outs[0], outs[1]))
            _store_rows(l_ref, sub, dil, jnp.where(first, lses[0], lses[1]))

    ins = [(q, 2 * g, 0), (k, 2 * g, 0), (kv, 6 + 2 * g, 0)]
    if with_prev:
        ins += [(k, 2 * g, -1), (kv, 6 + 2 * g, -1)]
    out = _pair_spec(tb, nblk, 0, 0)
    return pl.pallas_call(
        body, name=name, grid=(n_seq, 2, nblk),
        in_specs=[_pair_spec(tb, nblk, c, w) for _, c, w in ins],
        out_specs=[out, out],
        out_shape=[_sds((M, GROUP_W), f32)] * 2,
        compiler_params=_params(("parallel", "parallel", "arbitrary")),
    )(*[a for a, _, _ in ins])


def combine_fwd(os_, lses, *, name):
    M = os_[0].shape[0]
    tm = min(512, M)

    def body(o0, o1, o2, l0, l1, l2, y_ref):
        ls = [l0[...], l1[...], l2[...]]
        m = jnp.maximum(jnp.maximum(ls[0], ls[1]), ls[2])
        es = [jnp.exp(l - m) for l in ls]
        inv = 1.0 / (es[0] + es[1] + es[2])
        y_ref[...] = jnp.concatenate([o[...] * e * inv for o, e in zip((o0, o1, o2), es)], axis=1).astype(bf16)

    part = pl.BlockSpec((tm, GROUP_W), lambda i: (i, 0))
    return pl.pallas_call(
        body, name=name, grid=(M // tm,),
        in_specs=[part] * 6,
        out_specs=pl.BlockSpec((tm, MAIN_W), lambda i: (i, 0)),
        out_shape=_sds((M, D_MODEL), bf16),
        compiler_params=_params(("parallel",)),
    )(*os_, *lses)


def combine_bwd(dyc, os_, lses, *, name):
    M = os_[0].shape[0]
    tm = min(512, M)

    def body(dy_ref, o0, o1, o2, l0, l1, l2, d0, d1, d2, c0, c1, c2):
        r = lax.broadcasted_iota(jnp.int32, (GROUP_W, GROUP_W), 0) // HEAD_DIM
        c = lax.broadcasted_iota(jnp.int32, (GROUP_W, GROUP_W), 1) // HEAD_DIM
        ones = (r == c).astype(f32)
        dy = dy_ref[...].astype(f32)
        ls = [l0[...], l1[...], l2[...]]
        m = jnp.maximum(jnp.maximum(ls[0], ls[1]), ls[2])
        es = [jnp.exp(l - m) for l in ls]
        inv = 1.0 / (es[0] + es[1] + es[2])
        total = 0.0
        alphas = []
        for g, (o, e, d_ref) in enumerate(zip((o0, o1, o2), es, (d0, d1, d2))):
            a = e * inv
            dyg = dy[:, g * GROUP_W:(g + 1) * GROUP_W]
            d_ref[...] = dyg * a
            dsum = jnp.dot(dyg * o[...], ones, precision=lax.Precision.HIGHEST, preferred_element_type=f32)
            total = total + a * dsum
            alphas.append(a)
        for a, c_ref in zip(alphas, (c0, c1, c2)):
            c_ref[...] = -a * total

    part = pl.BlockSpec((tm, GROUP_W), lambda i: (i, 0))
    outs = pl.pallas_call(
        body, name=name, grid=(M // tm,),
        in_specs=[pl.BlockSpec((tm, MAIN_W), lambda i: (i, 0))] + [part] * 6,
        out_specs=[part] * 6,
        out_shape=[_sds((M, GROUP_W), f32)] * 6,
        compiler_params=_params(("parallel",)),
    )(dyc, *os_, *lses)
    return outs[:3], outs[3:]


def dil_bwd(q, k, kv, do, cc, lse, g, dil, *, name, n_seq):
    M = q.shape[0]
    nsub = SEQ // BAND
    per_res = nsub // dil

    def body(q_ref, k_ref, v_ref, do_ref, c_ref, l_ref, dq_ref, dk_ref, dv_ref):
        first = _first_head()
        for r in range(dil):
            carry = None
            for nl in range(per_res):
                sub = nl * dil + r
                qs = _rows(q_ref, sub, dil) * SCALE
                dos = _rows(do_ref, sub, dil)
                cs = _rows(c_ref, sub, dil)
                ls = _rows(l_ref, sub, dil)
                kc = _keys(None, k_ref, sub, dil).astype(bf16)
                vc = _keys(None, v_ref, sub, dil).astype(bf16)
                nkeys = kc.shape[0]
                mask = _band_mask(nkeys, True)
                dqs = []
                dkc = jnp.zeros((nkeys, PAIR_W), f32)
                dvc = jnp.zeros((nkeys, PAIR_W), f32)
                for hh in range(2):
                    lm = first if hh == 0 else ~first
                    qm = jnp.where(lm, qs, 0.0).astype(bf16)
                    dom = jnp.where(lm, dos, 0.0).astype(bf16)
                    s = jnp.where(mask, lax.dot_general(qm, kc, NT, preferred_element_type=f32), NEG)
                    p = jnp.exp(s - _col(ls, hh))
                    dp = lax.dot_general(dom, vc, NT, preferred_element_type=f32)
                    ds = (p * (dp + _col(cs, hh))).astype(bf16)
                    dqs.append(jnp.dot(ds, kc, preferred_element_type=f32) * SCALE)
                    dkc = dkc + lax.dot_general(ds, qm, TN, preferred_element_type=f32)
                    dvc = dvc + lax.dot_general(p.astype(bf16), dom, TN, preferred_element_type=f32)
                _store_rows(dq_ref, sub, dil, jnp.where(first, dqs[0], dqs[1]))
                if nkeys == 2 * BAND:
                    _store_rows(dk_ref, sub - dil, dil, carry[0] + dkc[:BAND])
                    _store_rows(dv_ref, sub - dil, dil, carry[1] + dvc[:BAND])
                    carry = (dkc[BAND:], dvc[BAND:])
                else:
                    carry = (dkc, dvc)
            _store_rows(dk_ref, (per_res - 1) * dil + r, dil, carry[0])
            _store_rows(dv_ref, (per_res - 1) * dil + r, dil, carry[1])

    def spec(col0):
        return pl.BlockSpec((SEQ, PAIR_W), lambda b, p: (b, col0 + p))

    out = spec(0)
    return pl.pallas_call(
        body, name=name, grid=(n_seq, 2),
        in_specs=[spec(2 * g), spec(2 * g), spec(6 + 2 * g), spec(0), spec(0), spec(0)],
        out_specs=[out, out, out],
        out_shape=[_sds((M, GROUP_W), f32)] * 3,
        compiler_params=_params(("parallel", "parallel")),
    )(q, k, kv, do, cc, lse)


def _blockdiag(wp):
    out = jnp.zeros((MAIN_W, MAIN_W), wp.dtype)
    for gi in range(len(POOL_WINDOWS)):
        sl = slice(gi * POOL_GROUP, (gi + 1) * POOL_GROUP)
        out = out.at[sl, sl].set(wp[gi])
    return out


def _unblockdiag(w):
    return jnp.stack([w[gi * POOL_GROUP:(gi + 1) * POOL_GROUP, gi * POOL_GROUP:(gi + 1) * POOL_GROUP]
                      for gi in range(len(POOL_WINDOWS))])


def local_step(x, mem, positions, target, P, layer_weights, kv_weight, emit_grads):
    n_seq = x.shape[0]
    M = n_seq * SEQ
    xs = x.reshape(M, D_MODEL)
    mems = mem.reshape(n_seq * N_MEM, D_MODEL)
    pos = positions.reshape(M, 1).astype(f32)
    cos, sin = rope_tables(pos, name="rope_tables")
    gains = P["norm_gains"]

    def gain(l, k):
        return gains[l, k].reshape(1, D_MODEL)

    saved = []
    kvs = None
    for l in range(DEPTH):
        W, started = layer_weights(l, "mix", xs)
        sv = {"x": xs, "W": W}
        z, h1 = rms_matmul(xs, gain(l, 0), W["w_in"], name=f"l{l}_in", out_dtype=f32, after=started)
        kvm, mn = rms_matmul(mems, P["mem_norm"][l].reshape(1, D_MODEL), W["w_mem_kv"],
                             name=f"l{l}_memkv", out_dtype=bf16)
        sv.update(z=z, h1=h1, kvm=kvm, mn=mn)
        if l < N_A_LAYERS:
            wbd = _blockdiag(P["w_pool"][l].astype(bf16))
            psc = P["pool_scale"][l].reshape(1, MAIN_W)
            p, y_main = pool_fwd(z, wbd, psc, name=f"l{l}_pool")
            sv.update(p=p, wbd=wbd, psc=psc)
        else:
            qrot = rope_fwd(z, cos, sin, name=f"l{l}_ropeq")
            os_, lses = [], []
            for g, (_, dil) in enumerate(DIL_PATTERNS):
                o, lse = dil_fwd(qrot, kvs["krot"], kvs["kv"], g, dil, name=f"l{l}_dil{g}", n_seq=n_seq)
                os_.append(o)
                lses.append(lse)
            y_main = combine_fwd(os_, lses, name=f"l{l}_comb")
            sv.update(qrot=qrot, os=os_, lses=lses)
        ycat = memattn_fwd(z, kvm, y_main, name=f"l{l}_memattn", n_seq=n_seq)
        y, x1 = matmul_rms_res(ycat, W["w_out"], gain(l, 1), xs, name=f"l{l}_out")
        W.update(layer_weights(l, "gu", x1)[0])
        fg, fu, a, h2 = rms_gate_up(x1, gain(l, 2), W["w_gate_up"], name=f"l{l}_gu")
        W.update(layer_weights(l, "down", a)[0])
        y2, x2 = matmul_rms_res(a, W["w_down"], gain(l, 3), x1, name=f"l{l}_down")
        sv.update(ycat=ycat, y=y, x1=x1, fg=fg, fu=fu, h2=h2, a=a, y2=y2)
        saved.append(sv)
        xs = x2
        if l == N_A_LAYERS - 1:
            w_kv = kv_weight(xs)
            kv, hkv = rms_matmul(xs, P["kv_norm"].reshape(1, D_MODEL), w_kv, name="kv_proj", out_dtype=f32,
                                 transposed=True)
            krot = rope_fwd(kv, cos, sin, name="ropek")
            kvs = {"kv": kv, "hkv": hkv, "krot": krot, "x": xs, "w_kv": w_kv}

    dx, sq = loss_head(xs, target.reshape(M, D_MODEL), name="loss_head")

    G = {"mem_norm": [None] * DEPTH, "norm_gains": [[None] * 4 for _ in range(DEPTH)],
         "pool_scale": [None] * N_A_LAYERS}
    dk_parts = [[] for _ in range(N_GROUPS)]
    dv_parts = [[] for _ in range(N_GROUPS)]
    emitted = None

    for l in reversed(range(DEPTH)):
        sv = saved[l]
        W = sv["W"]
        gw = {}
        dy2, dgu, G["norm_gains"][l][3] = down_bwd(sv["y2"], gain(l, 3), dx, W["w_down"], sv["fg"], sv["fu"],
                                                   name=f"l{l}_b_dgu", after=emitted)
        gw["w_down"] = matmul(sv["a"], dy2, TN, name=f"l{l}_b_wd", out_dtype=bf16)
        dx1, G["norm_gains"][l][2] = matmul_rms_bwd(dgu, W["w_gate_up"], NN, sv["x1"], gain(l, 2), dx,
                                                    name=f"l{l}_b_dh2")
        gw["w_gate_up"] = matmul(dgu, sv["h2"], TN, name=f"l{l}_b_wgu", out_dtype=bf16)
        emitted = emit_grads(l, "ffn", gw)
        gw = {}
        dy, dycat, G["norm_gains"][l][1] = rms_bwd_matmul(sv["y"], gain(l, 1), dx1, W["w_out"], NT,
                                                          name=f"l{l}_b_dycat", after=emitted)
        gw["w_out"] = matmul(sv["ycat"], dy, TN, name=f"l{l}_b_wout", out_dtype=bf16)
        if l < N_A_LAYERS:
            dz, dwbd, dps = pool_bwd(dycat, sv["p"], sv["wbd"], sv["psc"], name=f"l{l}_b_pool")
            gw["w_pool"] = _unblockdiag(dwbd).reshape(MAIN_W, POOL_GROUP).astype(bf16)
            G["pool_scale"][l] = dps.reshape(MAIN_W)
        else:
            dos, ccs = combine_bwd(dycat, sv["os"], sv["lses"], name=f"l{l}_b_comb")
            dqs = []
            for g, (_, dil) in enumerate(DIL_PATTERNS):
                args = (sv["qrot"], kvs["krot"], kvs["kv"], dos[g], ccs[g], sv["lses"][g], g, dil)
                dq, dk, dv = dil_bwd(*args, name=f"l{l}_b_dil{g}", n_seq=n_seq)
                dqs.append([dq])
                dk_parts[g].append(dk)
                dv_parts[g].append(dv)
            dz = group_sum(dqs, cos, sin, name=f"l{l}_b_ropeq", rotate=True, width=D_MODEL)
        dz, dkvm = memattn_bwd(sv["z"], sv["kvm"], dycat, dz, name=f"l{l}_b_memattn", n_seq=n_seq)
        dmn = matmul(dkvm, W["w_mem_kv"], NT, name=f"l{l}_b_dmn", out_dtype=bf16)
        gw["w_mem_kv"] = matmul(sv["mn"], dkvm, TN, name=f"l{l}_b_wmkv", out_dtype=bf16)
        _, G["mem_norm"][l] = rms_bwd(mems, P["mem_norm"][l].reshape(1, D_MODEL), dmn, None,
                                      name=f"l{l}_b_nmem", out_dtype=bf16)
        gw["w_in"] = matmul(sv["h1"], dz, TN, name=f"l{l}_b_win", out_dtype=bf16)
        dx, G["norm_gains"][l][0] = matmul_rms_bwd(dz, W["w_in"], NT, sv["x"], gain(l, 0), dx1, name=f"l{l}_b_dh1")
        if l == N_A_LAYERS:
            dkv = group_sum(dk_parts, cos, sin, name="b_ropek", rotate=True, width=2 * MAIN_W)
            dkv = group_sum(dv_parts, cos, sin, name="b_sumv", rotate=False, width=2 * MAIN_W, col_block=1, into=dkv)
            gw["w_kv"] = matmul(dkv, kvs["hkv"], TN, name="b_wkv", out_dtype=bf16)
            dx, gkn = matmul_rms_bwd(dkv, kvs["w_kv"], NN, kvs["x"], P["kv_norm"].reshape(1, D_MODEL), dx,
                                     name="b_dhkv")
            G["kv_norm"] = gkn.reshape(D_MODEL)
        emitted = emit_grads(l, "mix", gw)

    small = {"pool_scale": jnp.stack(G["pool_scale"]),
             "mem_norm": jnp.concatenate(G["mem_norm"], axis=0),
             "norm_gains": jnp.stack([jnp.concatenate(r, axis=0) for r in G["norm_gains"]]),
             "kv_norm": G["kv_norm"]}
    return sq[0, 0], dx.reshape(n_seq, SEQ, D_MODEL), small, emitted


def _peer(k):
    x, y, c = lax.axis_index("x"), lax.axis_index("y"), lax.axis_index("c")
    px = 1 - x if k & 4 else x
    py = 1 - y if k & 2 else y
    pc = 1 - c if k & 1 else c
    return (px, py, pc), 4 * px + 2 * py + pc


def _my_index():
    return 4 * lax.axis_index("x") + 2 * lax.axis_index("y") + lax.axis_index("c")


def _src_for(kinds, in_refs, i, idx):
    return in_refs[i] if kinds[i] == "gather" else in_refs[i].at[idx]


def _local_copies(kinds, in_refs, out_refs, local_sems):
    me = _my_index()
    return [pltpu.make_async_copy(_src_for(kinds, in_refs, i, me), out_refs[i].at[me], local_sems.at[i])
            for i in range(len(kinds))]


def _remote_copies(kinds, in_refs, out_refs, send_sems, recv_sems, *, arriving):
    me = _my_index()
    copies = []
    for k in range(1, N_DEV):
        dev, idx = _peer(k)
        for i in range(len(kinds)):
            j = i * (N_DEV - 1) + k - 1
            copies.append(pltpu.make_async_remote_copy(
                src_ref=_src_for(kinds, in_refs, i, idx), dst_ref=out_refs[i].at[idx if arriving else me],
                send_sem=send_sems.at[j], recv_sem=recv_sems.at[j], device_id=dev, device_id_type=MESH))
    return copies


def _out_shape(a, kind):
    return ((N_DEV,) + a.shape) if kind == "gather" else a.shape


def exchange(items, *, name, after=()):
    n = len(items)
    kinds = [k for _, k in items]
    after = list(after)

    def body(*refs):
        in_refs, out_refs = refs[:n], refs[n + len(after):2 * n + len(after)]
        send_sems, recv_sems, local_sems = refs[-3:]
        local = _local_copies(kinds, in_refs, out_refs, local_sems)
        sends = _remote_copies(kinds, in_refs, out_refs, send_sems, recv_sems, arriving=False)
        for cp in local + sends:
            cp.start()
        for cp in _remote_copies(kinds, in_refs, out_refs, send_sems, recv_sems, arriving=True):
            cp.wait_recv()
        for cp in sends:
            cp.wait_send()
        for cp in local:
            cp.wait()

    any_spec = pl.BlockSpec(memory_space=pl.ANY)
    return pl.pallas_call(
        body, name=name,
        in_specs=[any_spec] * (n + len(after)), out_specs=[any_spec] * n,
        out_shape=[_sds(_out_shape(a, k), a.dtype) for a, k in items],
        scratch_shapes=[pltpu.SemaphoreType.DMA((n * (N_DEV - 1),)), pltpu.SemaphoreType.DMA((n * (N_DEV - 1),)),
                        pltpu.SemaphoreType.DMA((n,))],
    )(*[a for a, _ in items], *after)


_HBM = pl.BlockSpec(memory_space=pltpu.HBM)
_SEM = pl.BlockSpec(memory_space=pltpu.SEMAPHORE)
_EFFECT = pltpu.SideEffectType.DATAFLOW_SIDE_EFFECTING


def exchange_start(items, after, *, name):
    n = len(items)
    kinds = [k for _, k in items]

    def body(*refs):
        in_refs, land_refs = refs[:n], refs[n:2 * n]
        send_sems, recv_sems, local_sems = refs[2 * n + 1:2 * n + 4]
        token = refs[-1]
        for cp in (_local_copies(kinds, in_refs, land_refs, local_sems)
                   + _remote_copies(kinds, in_refs, land_refs, send_sems, recv_sems, arriving=False)):
            cp.start()
        token[...] = jnp.zeros_like(token)

    srcs = [pltpu.with_memory_space_constraint(a, pltpu.HBM) for a, _ in items]
    lands = [pltpu.with_memory_space_constraint(lax.empty(_out_shape(a, k), a.dtype), pltpu.HBM) for a, k in items]
    outs = pl.pallas_call(
        body, name=name,
        out_shape=(pltpu.SemaphoreType.DMA((n * (N_DEV - 1),)), pltpu.SemaphoreType.DMA((n * (N_DEV - 1),)),
                   pltpu.SemaphoreType.DMA((n,)),
                   *[pltpu.HBM(a.shape, a.dtype) for a in srcs], *[pltpu.HBM(a.shape, a.dtype) for a in lands],
                   _sds((8, 128), f32)),
        in_specs=[_HBM] * (2 * n) + [pl.BlockSpec(memory_space=pl.ANY)],
        out_specs=(_SEM, _SEM, _SEM, *[_HBM] * (2 * n), pl.BlockSpec(memory_space=pltpu.VMEM)),
        input_output_aliases={i: 3 + i for i in range(2 * n)},
        compiler_params=pltpu.CompilerParams(has_side_effects=_EFFECT),
    )(*srcs, *lands, after)
    return {"kinds": kinds, "sems": outs[:3], "srcs": outs[3:3 + n], "lands": outs[3 + n:3 + 2 * n], "token": outs[-1]}


def exchange_wait(handle, after, *, name):
    kinds = handle["kinds"]
    n = len(kinds)

    def body(*refs):
        in_refs, land_refs = refs[:n], refs[n:2 * n]
        send_sems, recv_sems, local_sems = refs[2 * n:2 * n + 3]
        for cp in _remote_copies(kinds, in_refs, land_refs, send_sems, recv_sems, arriving=True):
            cp.wait_recv()
        for cp in _remote_copies(kinds, in_refs, land_refs, send_sems, recv_sems, arriving=False):
            cp.wait_send()
        for cp in _local_copies(kinds, in_refs, land_refs, local_sems):
            cp.wait()

    srcs, lands = list(handle["srcs"]), list(handle["lands"])
    after = list(after) if isinstance(after, (list, tuple)) else [after]
    outs = pl.pallas_call(
        body, name=name,
        out_shape=tuple(pltpu.HBM(a.shape, a.dtype) for a in srcs + lands),
        in_specs=[_HBM] * (2 * n) + [_SEM] * 3 + [pl.BlockSpec(memory_space=pl.ANY)] * len(after),
        out_specs=tuple([_HBM] * (2 * n)),
        input_output_aliases={i: i for i in range(2 * n)},
        compiler_params=pltpu.CompilerParams(has_side_effects=_EFFECT),
    )(*srcs, *lands, *handle["sems"], *after)
    return list(outs[n:])


def adamw(slots, w, m, v, *, name, layer=None, into=None):
    R, C = w.shape[-2:]
    tr = _tile(R, (256, 128, 64, 32, 16, 8))
    c1 = 1.0 - ADAM_B1 ** ADAM_STEP
    c2 = 1.0 - ADAM_B2 ** ADAM_STEP
    extra = [] if into is None else list(into)

    def body(s_ref, w_ref, m_ref, v_ref, *refs):
        g_ref, d_ref, m2_ref, v2_ref = refs[len(extra):]
        g = s_ref[0].astype(f32)
        for d in range(1, N_DEV):
            g = g + s_ref[d].astype(f32)
        m2 = ADAM_B1 * m_ref[...] + (1.0 - ADAM_B1) * g
        v2 = ADAM_B2 * v_ref[...] + (1.0 - ADAM_B2) * (g * g)
        g_ref[...] = g
        m2_ref[...] = m2
        v2_ref[...] = v2
        d_ref[...] = -ADAM_LR * ((m2 / c1) / (jnp.sqrt(v2 / c2) + ADAM_EPS) + ADAM_WD * w_ref[...])

    if layer is None:
        blk = pl.BlockSpec((tr, C), lambda i: (i, 0))
    else:
        blk = pl.BlockSpec((None, tr, C), lambda i: (layer, i, 0))
    return pl.pallas_call(
        body, name=name, grid=(R // tr,),
        in_specs=[pl.BlockSpec((N_DEV, tr, C), lambda i: (0, i, 0)), blk, blk, blk]
        + [pl.BlockSpec(memory_space=pl.ANY)] * len(extra),
        out_specs=[blk] * 4,
        out_shape=[_sds(w.shape, f32)] * 4,
        input_output_aliases={4 + j: j for j in range(len(extra))},
        compiler_params=_params(("parallel",)),
    )(slots, w, m, v, *extra)


WEIGHTS = ("norm_gains", "mem_norm", "w_in", "w_mem_kv", "w_out", "w_pool", "pool_scale", "kv_norm", "w_kv",
           "w_gate_up", "w_down")
LAYER_MATS = ("w_in", "w_mem_kv", "w_out", "w_gate_up", "w_down")
POOL_SHARD = MAIN_W // N_DEV
KV_SHARD = 2 * MAIN_W // N_DEV
LOOKAHEAD = 2


def _pack_small(gains, pscale):
    lead = gains.shape[:-3]
    g = gains.reshape(lead + (16, 128))
    p = jnp.zeros(lead + (8, 128), f32).at[..., :2, :POOL_SHARD].set(pscale)
    return jnp.concatenate([g, p], axis=-2)


def _unpack_small(a):
    return a[:16].reshape(4, 4, 128), a[16:18, :POOL_SHARD]


def _pack_repl(mem_norm, kv_norm):
    return jnp.concatenate([mem_norm, kv_norm.reshape(1, D_MODEL), jnp.zeros((3, D_MODEL), f32)], axis=0)


def _unpack_repl(a):
    return a[:4], a[4]


def kernel(x, mem, positions, norm_gains, mem_norm, w_in, w_mem_kv, w_out, w_pool, pool_scale, kv_norm, w_kv, w_gate_up, w_down, loss_target, m_norm_gains, m_mem_norm, m_w_in, m_w_mem_kv, m_w_out, m_w_pool, m_pool_scale, m_kv_norm, m_w_kv, m_w_gate_up, m_w_down, v_norm_gains, v_mem_norm, v_w_in, v_w_mem_kv, v_w_out, v_w_pool, v_pool_scale, v_kv_norm, v_w_kv, v_w_gate_up, v_w_down):
    w = dict(norm_gains=norm_gains, mem_norm=mem_norm, w_in=w_in, w_mem_kv=w_mem_kv, w_out=w_out, w_pool=w_pool,
             pool_scale=pool_scale, kv_norm=kv_norm, w_kv=w_kv, w_gate_up=w_gate_up, w_down=w_down)
    m = dict(norm_gains=m_norm_gains, mem_norm=m_mem_norm, w_in=m_w_in, w_mem_kv=m_w_mem_kv, w_out=m_w_out,
             w_pool=m_w_pool, pool_scale=m_pool_scale, kv_norm=m_kv_norm, w_kv=m_w_kv, w_gate_up=m_w_gate_up,
             w_down=m_w_down)
    v = dict(norm_gains=v_norm_gains, mem_norm=v_mem_norm, w_in=v_w_in, w_mem_kv=v_w_mem_kv, w_out=v_w_out,
             w_pool=v_w_pool, pool_scale=v_pool_scale, kv_norm=v_kv_norm, w_kv=v_w_kv, w_gate_up=v_w_gate_up,
             w_down=v_w_down)

    def transposed_view(d):
        d = dict(d)
        d["w_gate_up"] = jnp.swapaxes(d["w_gate_up"], 1, 2)
        d["w_kv"] = jnp.swapaxes(d["w_kv"], 0, 1)
        return d

    wv, mv, vv = transposed_view(w), transposed_view(m), transposed_view(v)

    small = _pack_small(norm_gains, pool_scale)
    (gsmall,) = exchange([(small, "gather")], name="gather_small")
    P = {"norm_gains": jnp.moveaxis(gsmall[:, :16].reshape(N_DEV, 4, 4, 128), 0, 2).reshape(4, 4, D_MODEL),
         "pool_scale": jnp.moveaxis(gsmall[:, 16:18, :POOL_SHARD], 0, 1).reshape(2, MAIN_W),
         "mem_norm": mem_norm, "kv_norm": kv_norm, "w_pool": w_pool}

    PARTS = {"mix": ("w_in", "w_mem_kv", "w_out"), "ffn": ("w_gate_up", "w_down"), "gu": ("w_gate_up",),
             "down": ("w_down",)}

    def parts_of(l):
        return ("mix", "gu", "down") if l == 0 else ("mix", "ffn")

    def part_items(l, part):
        items = [(wv[k][l].astype(bf16), "gather") for k in PARTS[part]]
        if part == "ffn" and l == N_A_LAYERS - 1:
            items.append((wv["w_kv"].astype(bf16), "gather"))
        return items

    handles = {}

    def start_layer(l, after):
        for part in parts_of(l):
            handles[l, part] = exchange_start(part_items(l, part), after, name=f"gather_start_{part}_l{l}")
            after = handles[l, part]["token"]
        return after

    token = gsmall
    for l in range(LOOKAHEAD):
        token = start_layer(l, token)
    landed = {}

    def layer_weights(l, part, after):
        if part not in parts_of(l):
            if part == "down":
                return {}, None
            part = "ffn"
        first = l == 0 and part == "mix"
        got = exchange_wait(handles[l, part], token if first else after, name=f"gather_wait_{part}_l{l}")
        landed[l, part] = got
        started = None
        if part == "mix" and l + LOOKAHEAD < DEPTH:
            started = start_layer(l + LOOKAHEAD, got[0])
        W = {k: g.reshape(-1, g.shape[-1]) for k, g in zip(PARTS[part], got)}
        return W, started

    def kv_weight(after):
        g = landed[N_A_LAYERS - 1, "ffn"][len(PARTS["ffn"])]
        return g.reshape(2 * MAIN_W, D_MODEL)

    ghandles = {}

    def emit_grads(l, part, gw):
        items = [(gw[k].reshape((N_DEV, -1) + gw[k].shape[-1:]), "scatter") for k in PARTS[part]]
        if part == "mix" and l == N_A_LAYERS:
            items.append((gw["w_kv"].reshape(N_DEV, KV_SHARD, D_MODEL), "scatter"))
        if part == "mix" and l < N_A_LAYERS:
            items.append((gw["w_pool"], "gather"))
        ghandles[l, part] = exchange_start(items, gsmall, name=f"scatter_start_{part}_l{l}")
        return ghandles[l, part]["token"]

    sq, grad_x, GS, emitted = local_step(x, mem, positions, loss_target, P, layer_weights, kv_weight, emit_grads)
    loss = lax.psum(0.5 * sq / D_MODEL, ("x", "y", "c"))

    def pool3(a):
        return a.reshape(N_A_LAYERS, MAIN_W, POOL_GROUP)

    out = {}
    after = [emitted]

    def finish_layer(l, after):
        for part in ("ffn", "mix"):
            got = exchange_wait(ghandles[l, part], after, name=f"scatter_wait_{part}_l{l}")
            after = []
            for k, slots in zip(PARTS[part], got):
                out[k] = adamw(slots, wv[k], mv[k], vv[k], name=f"adamw_{k}_l{l}", layer=l, into=out.get(k))
                after.append(out[k][0])
            if part == "mix" and l == N_A_LAYERS:
                out["w_kv"] = adamw(got[-1], wv["w_kv"], mv["w_kv"], vv["w_kv"], name="adamw_w_kv")
                after.append(out["w_kv"][0])
            if part == "mix" and l < N_A_LAYERS:
                out["w_pool"] = adamw(got[-1], pool3(w_pool), pool3(m_w_pool), pool3(v_w_pool), name=f"adamw_w_pool_l{l}",
                                      layer=l, into=out.get("w_pool"))
                after.append(out["w_pool"][0])
        return after

    for l in reversed(range(1, DEPTH)):
        after = finish_layer(l, after)

    gs = _pack_small(jnp.moveaxis(GS["norm_gains"].reshape(4, 4, N_DEV, 128), 2, 0),
                     jnp.moveaxis(GS["pool_scale"].reshape(2, N_DEV, POOL_SHARD), 1, 0))
    parts_small, parts_repl = exchange(
        [(gs, "scatter"), (_pack_repl(GS["mem_norm"], GS["kv_norm"]), "gather")],
        name="exchange_small_grads", after=after)
    finish_layer(0, [parts_small])
    out["w_gate_up"] = [jnp.swapaxes(r, 1, 2) for r in out["w_gate_up"]]
    out["w_kv"] = [jnp.swapaxes(r, 0, 1) for r in out["w_kv"]]
    out["w_pool"] = [r.reshape(w_pool.shape) for r in out["w_pool"]]

    res = adamw(parts_small, small, _pack_small(m_norm_gains, m_pool_scale), _pack_small(v_norm_gains, v_pool_scale),
                name="adamw_small")
    out["norm_gains"], out["pool_scale"] = zip(*[_unpack_small(r) for r in res])
    res = adamw(parts_repl, _pack_repl(mem_norm, kv_norm), _pack_repl(m_mem_norm, m_kv_norm),
                _pack_repl(v_mem_norm, v_kv_norm), name="adamw_repl")
    out["mem_norm"], out["kv_norm"] = zip(*[_unpack_repl(r) for r in res])

    return (loss, grad_x, *[out[k][0] for k in WEIGHTS], *[out[k][1] for k in WEIGHTS],
            *[out[k][2] for k in WEIGHTS], *[out[k][3] for k in WEIGHTS])
```

```python
import numpy as np
import jax
import jax.numpy as jnp
from jax import lax
from jax.experimental import pallas as pl
from jax.experimental.pallas import tpu as pltpu

f32 = jnp.float32
bf16 = jnp.bfloat16

D_MODEL = 1024
SEQ = 2048
DEPTH = 4
N_MEM = 256
HEAD_DIM = 64
N_MEM_HEADS = 4
MEM_W = 256
MAIN_W = 768
POOL_WINDOWS = (2, 4, 8, 16)
POOL_GROUP = 192
POOL_HALO = 16
DIL_PATTERNS = ((128, 1), (512, 4), (2048, 16))
N_GROUPS = 3
GROUP_W = 256
BAND = 128
N_A_LAYERS = 2
D_FF = 2816
ROPE_THETA = 10000.0
EPS = 1e-6
NEG = -1e30
SCALE = HEAD_DIM ** -0.5
N_DEV = 8

ADAM_LR = 0.001
ADAM_B1 = 0.9
ADAM_B2 = 0.999
ADAM_EPS = 1e-08
ADAM_WD = 0.01
ADAM_STEP = 10

VMEM_LIMIT_BYTES = 56 * 1024 * 1024
MESH = pl.DeviceIdType.MESH

NN = (((1,), (0,)), ((), ()))
NT = (((1,), (1,)), ((), ()))
TN = (((0,), (0,)), ((), ()))


def _params(sem=None):
    return pltpu.CompilerParams(dimension_semantics=sem, vmem_limit_bytes=VMEM_LIMIT_BYTES)


def _tile(n, cands):
    for c in cands:
        if n % c == 0:
            return c
    return n


def _sds(shape, dtype):
    return jax.ShapeDtypeStruct(tuple(shape), dtype)


def _rms_r(v):
    return lax.rsqrt(jnp.mean(v * v, axis=-1, keepdims=True) + EPS)


def rms_matmul(x, gain, w, *, name, out_dtype, transposed=False, after=None):
    M, K = x.shape
    N = w.shape[0] if transposed else w.shape[1]
    tm = min(2048, M)
    tn = _tile(N, (512, 256, 128))
    order = [] if after is None else [after]

    def body(x_ref, g_ref, w_ref, *refs):
        z_ref, h_ref = refs[len(order):]

        @pl.when(pl.program_id(1) == 0)
        def _():
            xv = x_ref[...]
            h_ref[...] = (xv * _rms_r(xv) * g_ref[...]).astype(bf16)

        z_ref[...] = lax.dot_general(h_ref[...], w_ref[...], NT if transposed else NN,
                                     preferred_element_type=f32).astype(z_ref.dtype)

    w_spec = pl.BlockSpec((tn, K), lambda i, j: (j, 0)) if transposed else pl.BlockSpec((K, tn), lambda i, j: (0, j))
    return pl.pallas_call(
        body, name=name, grid=(M // tm, N // tn),
        in_specs=[pl.BlockSpec((tm, K), lambda i, j: (i, 0)),
                  pl.BlockSpec((1, K), lambda i, j: (0, 0)),
                  w_spec] + [pl.BlockSpec(memory_space=pl.ANY)] * len(order),
        out_specs=[pl.BlockSpec((tm, tn), lambda i, j: (i, j)), pl.BlockSpec((tm, K), lambda i, j: (i, 0))],
        out_shape=[_sds((M, N), out_dtype), _sds((M, K), bf16)],
        compiler_params=_params(("parallel", "arbitrary")),
    )(x, gain, w, *order)


def matmul_rms_res(a, w, gain, res, *, name):
    M, K = a.shape
    N = w.shape[1]
    tm = min(1024, M)

    def body(a_ref, w_ref, g_ref, r_ref, y_ref, x_ref):
        y = jnp.dot(a_ref[...], w_ref[...], preferred_element_type=f32)
        y_ref[...] = y.astype(bf16)
        x_ref[...] = r_ref[...] + y * _rms_r(y) * g_ref[...]

    row = pl.BlockSpec((tm, N), lambda i: (i, 0))
    return pl.pallas_call(
        body, name=name, grid=(M // tm,),
        in_specs=[pl.BlockSpec((tm, K), lambda i: (i, 0)),
                  pl.BlockSpec((K, N), lambda i: (0, 0)),
                  pl.BlockSpec((1, N), lambda i: (0, 0)),
                  row],
        out_specs=[row, row],
        out_shape=[_sds((M, N), bf16), _sds((M, N), f32)],
        compiler_params=_params(("parallel",)),
    )(a, w, gain, res)


def matmul(a, b, dims, *, name, out_dtype):
    if dims is TN:
        K, M = a.shape
        tm = _tile(M, (512, 256, 128))
        a_spec = pl.BlockSpec((K, tm), lambda i: (0, i))
    else:
        M, K = a.shape
        tm = _tile(M, (1024, 512, 256, 128))
        a_spec = pl.BlockSpec((tm, K), lambda i: (i, 0))
    N = b.shape[0] if dims is NT else b.shape[1]

    def body(a_ref, b_ref, o_ref):
        o_ref[...] = lax.dot_general(a_ref[...].astype(bf16), b_ref[...].astype(bf16), dims,
                                     preferred_element_type=f32).astype(o_ref.dtype)

    return pl.pallas_call(
        body, name=name, grid=(M // tm,),
        in_specs=[a_spec, pl.BlockSpec(b.shape, lambda i: (0, 0))],
        out_specs=pl.BlockSpec((tm, N), lambda i: (i, 0)),
        out_shape=_sds((M, N), out_dtype),
        compiler_params=_params(("parallel",)),
    )(a, b)


def rms_gate_up(x, gain, wt, *, name):
    M, K = x.shape
    tm = min(2048, M)
    tn = _tile(D_FF, (256, 128))
    nj = D_FF // tn

    def body(x_ref, gn_ref, wg_ref, wu_ref, g_ref, u_ref, a_ref, h_ref):
        @pl.when(pl.program_id(1) == 0)
        def _():
            xv = x_ref[...]
            h_ref[...] = (xv * _rms_r(xv) * gn_ref[...]).astype(bf16)

        h = h_ref[...]
        g = lax.dot_general(h, wg_ref[...], NT, preferred_element_type=f32)
        u = lax.dot_general(h, wu_ref[...], NT, preferred_element_type=f32)
        g_ref[...] = g.astype(bf16)
        u_ref[...] = u.astype(bf16)
        a_ref[...] = (g * (1.0 / (1.0 + jnp.exp(-g))) * u).astype(bf16)

    col = pl.BlockSpec((tm, tn), lambda i, j: (i, j))
    return pl.pallas_call(
        body, name=name, grid=(M // tm, nj),
        in_specs=[pl.BlockSpec((tm, K), lambda i, j: (i, 0)),
                  pl.BlockSpec((1, K), lambda i, j: (0, 0)),
                  pl.BlockSpec((tn, K), lambda i, j: (j, 0)),
                  pl.BlockSpec((tn, K), lambda i, j: (j + nj, 0))],
        out_specs=[col, col, col, pl.BlockSpec((tm, K), lambda i, j: (i, 0))],
        out_shape=[_sds((M, D_FF), bf16)] * 3 + [_sds((M, K), bf16)],
        compiler_params=_params(("parallel", "arbitrary")),
    )(x, gain, wt, wt)


def _rms_bwd_math(yv, gain, dn):
    r = _rms_r(yv)
    q = dn * gain
    dy = r * q - yv * (r * r * r) * jnp.mean(q * yv, axis=-1, keepdims=True)
    return dy, jnp.sum(dn * yv * r, axis=0, keepdims=True)


def _accumulate(ref, val):
    @pl.when(pl.program_id(0) == 0)
    def _():
        ref[...] = jnp.zeros_like(ref)

    ref[...] += val


def down_bwd(y, gain, dn, w_down, g, u, *, name, after=None):
    M, K = y.shape
    tm = min(512, M)
    order = [] if after is None else [after]

    def body(y_ref, gn_ref, dn_ref, w_ref, g_ref, u_ref, *refs):
        dy_ref, o_ref, dg_ref = refs[len(order):]
        dy, dgain = _rms_bwd_math(y_ref[...].astype(f32), gn_ref[...], dn_ref[...])
        dy = dy.astype(bf16)
        dy_ref[...] = dy
        _accumulate(dg_ref, dgain)
        da = lax.dot_general(dy, w_ref[...], NT, preferred_element_type=f32)
        g = g_ref[...].astype(f32)
        u = u_ref[...].astype(f32)
        s = 1.0 / (1.0 + jnp.exp(-g))
        o_ref[:, :D_FF] = (da * u * s * (1.0 + g * (1.0 - s))).astype(bf16)
        o_ref[:, D_FF:] = (da * g * s).astype(bf16)

    row = pl.BlockSpec((tm, K), lambda i: (i, 0))
    vec = pl.BlockSpec((1, K), lambda i: (0, 0))
    wide = pl.BlockSpec((tm, D_FF), lambda i: (i, 0))
    return pl.pallas_call(
        body, name=name, grid=(M // tm,),
        in_specs=[row, vec, row, pl.BlockSpec((D_FF, K), lambda i: (0, 0)), wide, wide]
        + [pl.BlockSpec(memory_space=pl.ANY)] * len(order),
        out_specs=[row, pl.BlockSpec((tm, 2 * D_FF), lambda i: (i, 0)), vec],
        out_shape=[_sds((M, K), bf16), _sds((M, 2 * D_FF), bf16), _sds((1, K), f32)],
        compiler_params=_params(("arbitrary",)),
    )(y, gain, dn, w_down, g, u, *order)


def rms_bwd_matmul(y, gain, dn, w, dims, *, name, after=None):
    M, K = y.shape
    N = w.shape[0] if dims is NT else w.shape[1]
    tm = min(1024, M)
    order = [] if after is None else [after]

    def body(y_ref, gn_ref, dn_ref, w_ref, *refs):
        dy_ref, o_ref, dg_ref = refs[len(order):]
        dy, dgain = _rms_bwd_math(y_ref[...].astype(f32), gn_ref[...], dn_ref[...].astype(f32))
        dy = dy.astype(bf16)
        dy_ref[...] = dy
        _accumulate(dg_ref, dgain)
        o_ref[...] = lax.dot_general(dy, w_ref[...], dims, preferred_element_type=f32).astype(bf16)

    row = pl.BlockSpec((tm, K), lambda i: (i, 0))
    vec = pl.BlockSpec((1, K), lambda i: (0, 0))
    return pl.pallas_call(
        body, name=name, grid=(M // tm,),
        in_specs=[row, vec, row, pl.BlockSpec(w.shape, lambda i: (0, 0))]
        + [pl.BlockSpec(memory_space=pl.ANY)] * len(order),
        out_specs=[row, pl.BlockSpec((tm, N), lambda i: (i, 0)), vec],
        out_shape=[_sds((M, K), bf16), _sds((M, N), bf16), _sds((1, K), f32)],
        compiler_params=_params(("arbitrary",)),
    )(y, gain, dn, w, *order)


def matmul_rms_bwd(a, b, dims, y, gain, res, *, name):
    M, K = a.shape
    N = y.shape[1]
    tm = 512

    def body(a_ref, b_ref, y_ref, gn_ref, r_ref, dx_ref, dg_ref):
        dn = lax.dot_general(a_ref[...], b_ref[...], dims, preferred_element_type=f32)
        dy, dgain = _rms_bwd_math(y_ref[...], gn_ref[...], dn)
        dx_ref[...] = dy + r_ref[...]
        _accumulate(dg_ref, dgain)

    row = pl.BlockSpec((tm, N), lambda i: (i, 0))
    vec = pl.BlockSpec((1, N), lambda i: (0, 0))
    return pl.pallas_call(
        body, name=name, grid=(M // tm,),
        in_specs=[pl.BlockSpec((tm, K), lambda i: (i, 0)), pl.BlockSpec(b.shape, lambda i: (0, 0)), row, vec, row],
        out_specs=[row, vec],
        out_shape=[_sds((M, N), f32), _sds((1, N), f32)],
        compiler_params=_params(("arbitrary",)),
    )(a, b, y, gain, res)


def rms_bwd(y, gain, dn, res, *, name, out_dtype, after=None):
    M, N = y.shape
    tm = min(512, M)
    has_res = res is not None
    order = [] if after is None else [after]

    def body(*refs):
        y_ref, g_ref, dn_ref = refs[:3]
        r_ref = refs[3] if has_res else None
        dy_ref, dg_ref = refs[-2:]
        dy, dgain = _rms_bwd_math(y_ref[...].astype(f32), g_ref[...], dn_ref[...].astype(f32))
        if has_res:
            dy = dy + r_ref[...]
        dy_ref[...] = dy.astype(dy_ref.dtype)
        _accumulate(dg_ref, dgain)

    row = pl.BlockSpec((tm, N), lambda i: (i, 0))
    vec = pl.BlockSpec((1, N), lambda i: (0, 0))
    args = [y, gain, dn] + ([res] if has_res else []) + order
    return pl.pallas_call(
        body, name=name, grid=(M // tm,),
        in_specs=[row, vec, row] + ([row] if has_res else []) + [pl.BlockSpec(memory_space=pl.ANY)] * len(order),
        out_specs=[row, vec],
        out_shape=[_sds((M, N), out_dtype), _sds((1, N), f32)],
        compiler_params=_params(("arbitrary",)),
    )(*args)


def loss_head(x, target, *, name):
    M, N = x.shape
    tm = min(512, M)

    def body(x_ref, t_ref, dx_ref, l_ref):
        e = x_ref[...] - t_ref[...]
        dx_ref[...] = e * (1.0 / N)

        @pl.when(pl.program_id(0) == 0)
        def _():
            l_ref[...] = jnp.zeros_like(l_ref)

        l_ref[...] += jnp.sum(jnp.sum(e * e, axis=0, keepdims=True), axis=1, keepdims=True)

    row = pl.BlockSpec((tm, N), lambda i: (i, 0))
    return pl.pallas_call(
        body, name=name, grid=(M // tm,),
        in_specs=[row, row],
        out_specs=[row, pl.BlockSpec((8, 128), lambda i: (0, 0))],
        out_shape=[_sds((M, N), f32), _sds((8, 128), f32)],
        compiler_params=_params(("arbitrary",)),
    )(x, target)


def _pool_select(a1, a2, a3, a4):
    col = lax.broadcasted_iota(jnp.int32, (1, MAIN_W), 1) // POOL_GROUP
    return jnp.where(col == 0, a1, jnp.where(col == 1, a2, jnp.where(col == 2, a3, a4)))


def _pool_count(t):
    col = lax.broadcasted_iota(jnp.int32, (1, MAIN_W), 1) // POOL_GROUP
    win = jnp.where(col == 0, 2, jnp.where(col == 1, 4, jnp.where(col == 2, 8, 16)))
    return jnp.minimum(t + 1, win).astype(f32)


def pool_fwd(z, wbd, scale, *, name):
    M = z.shape[0]
    tm = 256
    nper = SEQ // tm
    hb = tm // POOL_HALO

    def body(zc_ref, zh_ref, w_ref, s_ref, p_ref, y_ref):
        i = pl.program_id(0)
        seq_blk = i % nper
        halo = jnp.where(seq_blk == 0, 0.0, zh_ref[...])
        u = zc_ref[...]
        ext = jnp.concatenate([halo, u], axis=0)
        a1 = ext + pltpu.roll(ext, 1, 0)
        a2 = a1 + pltpu.roll(a1, 2, 0)
        a3 = a2 + pltpu.roll(a2, 4, 0)
        a4 = a3 + pltpu.roll(a3, 8, 0)
        sums = _pool_select(a1, a2, a3, a4)[POOL_HALO:]
        t = seq_blk * tm + lax.broadcasted_iota(jnp.int32, (tm, 1), 0)
        p = (sums / _pool_count(t) - u).astype(bf16)
        p_ref[...] = p
        y_ref[...] = (jnp.dot(p, w_ref[...], preferred_element_type=f32) * s_ref[...]).astype(bf16)

    return pl.pallas_call(
        body, name=name, grid=(M // tm,),
        in_specs=[pl.BlockSpec((tm, MAIN_W), lambda i: (i, 0)),
                  pl.BlockSpec((POOL_HALO, MAIN_W), lambda i: (jnp.maximum(i * hb - 1, 0), 0)),
                  pl.BlockSpec((MAIN_W, MAIN_W), lambda i: (0, 0)),
                  pl.BlockSpec((1, MAIN_W), lambda i: (0, 0))],
        out_specs=[pl.BlockSpec((tm, MAIN_W), lambda i: (i, 0)),
                   pl.BlockSpec((tm, MAIN_W), lambda i: (i, 0))],
        out_shape=[_sds((M, MAIN_W), bf16), _sds((M, D_MODEL), bf16)],
        compiler_params=_params(("parallel",)),
    )(z, z, wbd, scale)


def pool_bwd(dyc, p, wbd, scale, *, name):
    M = p.shape[0]
    tm = 256
    nper = SEQ // tm
    hb = tm // POOL_HALO
    last_hb = M // POOL_HALO - 1

    def body(dy_ref, dyh_ref, p_ref, w_ref, s_ref, dz_ref, dw_ref, ds_ref):
        i = pl.program_id(0)
        seq_blk = i % nper
        dy = dy_ref[...].astype(f32)
        pv = p_ref[...]
        w = w_ref[...]
        sc = s_ref[...]

        @pl.when(i == 0)
        def _():
            dw_ref[...] = jnp.zeros_like(dw_ref)
            ds_ref[...] = jnp.zeros_like(ds_ref)

        v = jnp.dot(pv, w, preferred_element_type=f32)
        ds_ref[...] += jnp.sum(dy * v, axis=0, keepdims=True)
        dv = (dy * sc).astype(bf16)
        dw_ref[...] += lax.dot_general(pv, dv, TN, preferred_element_type=f32)
        dp = lax.dot_general(dv, w, NT, preferred_element_type=f32)
        dvh = jnp.where(seq_blk == nper - 1, 0.0, dyh_ref[...].astype(f32) * sc).astype(bf16)
        dph = lax.dot_general(dvh, w, NT, preferred_element_type=f32)
        ext = jnp.concatenate([dp, dph], axis=0)
        n = tm + POOL_HALO
        t = seq_blk * tm + lax.broadcasted_iota(jnp.int32, (n, 1), 0)
        e = ext / _pool_count(t)
        b1 = e + pltpu.roll(e, n - 1, 0)
        b2 = b1 + pltpu.roll(b1, n - 2, 0)
        b3 = b2 + pltpu.roll(b2, n - 4, 0)
        b4 = b3 + pltpu.roll(b3, n - 8, 0)
        dz_ref[...] = (_pool_select(b1, b2, b3, b4)[:tm] - dp).astype(dz_ref.dtype)

    return pl.pallas_call(
        body, name=name, grid=(M // tm,),
        in_specs=[pl.BlockSpec((tm, MAIN_W), lambda i: (i, 0)),
                  pl.BlockSpec((POOL_HALO, MAIN_W), lambda i: (jnp.minimum((i + 1) * hb, last_hb), 0)),
                  pl.BlockSpec((tm, MAIN_W), lambda i: (i, 0)),
                  pl.BlockSpec((MAIN_W, MAIN_W), lambda i: (0, 0)),
                  pl.BlockSpec((1, MAIN_W), lambda i: (0, 0))],
        out_specs=[pl.BlockSpec((tm, MAIN_W), lambda i: (i, 0)),
                   pl.BlockSpec((MAIN_W, MAIN_W), lambda i: (0, 0)),
                   pl.BlockSpec((1, MAIN_W), lambda i: (0, 0))],
        out_shape=[_sds((M, D_MODEL), bf16), _sds((MAIN_W, MAIN_W), f32), _sds((1, MAIN_W), f32)],
        compiler_params=_params(("arbitrary",)),
    )(dyc, dyc, p, wbd, scale)


def _mem_heads(q, kv):
    first = _first_head()
    for pr in range(N_MEM_HEADS // 2):
        cols = slice(pr * PAIR_W, (pr + 1) * PAIR_W)
        qp = q[:, cols] * SCALE
        kp = kv[:, cols]
        vp = kv[:, MEM_W + pr * PAIR_W: MEM_W + (pr + 1) * PAIR_W]
        for hh in range(2):
            lm = first if hh == 0 else ~first
            qm = jnp.where(lm, qp, 0.0).astype(bf16)
            s = lax.dot_general(qm, kp, NT, preferred_element_type=f32)
            e = jnp.exp(s - jnp.max(s, axis=-1, keepdims=True))
            yield lm, qm, kp, vp, e, jnp.sum(e, axis=-1, keepdims=True)


def memattn_fwd(z, kvm, ycat, *, name, n_seq):
    M = z.shape[0]
    tq = 512
    nq = SEQ // tq

    def body(q_ref, kv_ref, _, o_ref):
        first = _first_head()
        outs = []
        for lm, _, _, vp, e, l in _mem_heads(q_ref[...], kv_ref[...]):
            outs.append(jnp.dot(e.astype(bf16), vp, preferred_element_type=f32) * (1.0 / l))
        pairs = [jnp.where(first, outs[2 * pr], outs[2 * pr + 1]) for pr in range(N_MEM_HEADS // 2)]
        o_ref[...] = jnp.concatenate(pairs, axis=1).astype(bf16)

    return pl.pallas_call(
        body, name=name, grid=(n_seq, nq),
        in_specs=[pl.BlockSpec((tq, MEM_W), lambda b, i: (b * nq + i, 3)),
                  pl.BlockSpec((N_MEM, 2 * MEM_W), lambda b, i: (b, 0)),
                  pl.BlockSpec(memory_space=pl.ANY)],
        out_specs=pl.BlockSpec((tq, MEM_W), lambda b, i: (b * nq + i, 3)),
        out_shape=_sds((M, D_MODEL), bf16),
        input_output_aliases={2: 0},
        compiler_params=_params(("parallel", "parallel")),
    )(z, kvm, ycat)


def memattn_bwd(z, kvm, dyc, dz, *, name, n_seq):
    M = z.shape[0]
    tq = 512
    nq = SEQ // tq

    def body(q_ref, kv_ref, dy_ref, _, dq_ref, dkv_ref):
        first = _first_head()
        dy = dy_ref[...].astype(f32)
        dqs, dks, dvs = [], [], []
        for h, (lm, qm, kp, vp, e, l) in enumerate(_mem_heads(q_ref[...], kv_ref[...])):
            pr = h // 2
            p = e * (1.0 / l)
            dym = jnp.where(lm, dy[:, pr * PAIR_W:(pr + 1) * PAIR_W], 0.0).astype(bf16)
            dp = lax.dot_general(dym, vp, NT, preferred_element_type=f32)
            ds = (p * (dp - jnp.sum(dp * p, axis=-1, keepdims=True))).astype(bf16)
            dqs.append(jnp.dot(ds, kp, preferred_element_type=f32) * SCALE)
            dk = lax.dot_general(ds, qm, TN, preferred_element_type=f32)
            dv = lax.dot_general(p.astype(bf16), dym, TN, preferred_element_type=f32)
            if h % 2 == 0:
                dks.append(dk)
                dvs.append(dv)
            else:
                dks[pr] = dks[pr] + dk
                dvs[pr] = dvs[pr] + dv
        pairs = [jnp.where(first, dqs[2 * pr], dqs[2 * pr + 1]) for pr in range(N_MEM_HEADS // 2)]
        dq_ref[...] = jnp.concatenate(pairs, axis=1).astype(bf16)

        @pl.when(pl.program_id(1) == 0)
        def _():
            dkv_ref[...] = jnp.zeros_like(dkv_ref)

        dkv_ref[...] += jnp.concatenate(dks + dvs, axis=1)

    return pl.pallas_call(
        body, name=name, grid=(n_seq, nq),
        in_specs=[pl.BlockSpec((tq, MEM_W), lambda b, i: (b * nq + i, 3)),
                  pl.BlockSpec((N_MEM, 2 * MEM_W), lambda b, i: (b, 0)),
                  pl.BlockSpec((tq, MEM_W), lambda b, i: (b * nq + i, 3)),
                  pl.BlockSpec(memory_space=pl.ANY)],
        out_specs=[pl.BlockSpec((tq, MEM_W), lambda b, i: (b * nq + i, 3)),
                   pl.BlockSpec((N_MEM, 2 * MEM_W), lambda b, i: (b, 0))],
        out_shape=[_sds((M, D_MODEL), bf16), _sds((n_seq * N_MEM, 2 * MEM_W), f32)],
        input_output_aliases={3: 0},
        compiler_params=_params(("parallel", "arbitrary")),
    )(z, kvm, dyc, dz)


def rope_tables(pos, *, name):
    M = pos.shape[0]
    tm = min(1024, M)
    half = HEAD_DIM // 2
    inv = ROPE_THETA ** (-np.arange(half, dtype=np.float64) / half)
    inv128 = jnp.asarray(np.tile(inv, 4)[None, :], f32)
    sign128 = jnp.asarray(np.tile(np.concatenate([-np.ones(half), np.ones(half)]), 2)[None, :], f32)

    def body(p_ref, f_ref, s_ref, cos_ref, sin_ref):
        ang = p_ref[...] * f_ref[...]
        cos_ref[...] = jnp.cos(ang)
        sin_ref[...] = jnp.sin(ang) * s_ref[...]

    return pl.pallas_call(
        body, name=name, grid=(M // tm,),
        in_specs=[pl.BlockSpec((tm, 1), lambda i: (i, 0)),
                  pl.BlockSpec((1, 128), lambda i: (0, 0)),
                  pl.BlockSpec((1, 128), lambda i: (0, 0))],
        out_specs=[pl.BlockSpec((tm, 128), lambda i: (i, 0)),
                   pl.BlockSpec((tm, 128), lambda i: (i, 0))],
        out_shape=[_sds((M, 128), f32), _sds((M, 128), f32)],
        compiler_params=_params(("parallel",)),
    )(pos, inv128, sign128)


def _swap_halves(x):
    w = x.shape[1]
    first = (lax.broadcasted_iota(jnp.int32, (1, w), 1) % HEAD_DIM) < (HEAD_DIM // 2)
    return jnp.where(first, pltpu.roll(x, w - HEAD_DIM // 2, 1), pltpu.roll(x, HEAD_DIM // 2, 1))


def rope_fwd(src, cos, sin, *, name):
    M = src.shape[0]
    tm = min(512, M)

    def body(x_ref, c_ref, s_ref, o_ref):
        x = x_ref[...].astype(f32)
        c = jnp.tile(c_ref[...], (1, MAIN_W // 128))
        s = jnp.tile(s_ref[...], (1, MAIN_W // 128))
        o_ref[...] = x * c + _swap_halves(x) * s

    return pl.pallas_call(
        body, name=name, grid=(M // tm,),
        in_specs=[pl.BlockSpec((tm, MAIN_W), lambda i: (i, 0)),
                  pl.BlockSpec((tm, 128), lambda i: (i, 0)),
                  pl.BlockSpec((tm, 128), lambda i: (i, 0))],
        out_specs=pl.BlockSpec((tm, MAIN_W), lambda i: (i, 0)),
        out_shape=_sds((M, MAIN_W), f32),
        compiler_params=_params(("parallel",)),
    )(src, cos, sin)


def group_sum(groups, cos, sin, *, name, rotate, width, col_block=0, into=None):
    M = groups[0][0].shape[0]
    tm = min(512, M)
    counts = [len(g) for g in groups]
    flat = [a for g in groups for a in g]
    extra = [] if into is None else [into]

    def body(*refs):
        part_refs = refs[:len(flat)]
        c_ref, s_ref = refs[len(flat):len(flat) + 2]
        o_ref = refs[-1]
        cols, k = [], 0
        for n in counts:
            acc = part_refs[k][...]
            for r in part_refs[k + 1:k + n]:
                acc = acc + r[...]
            cols.append(acc)
            k += n
        d = jnp.concatenate(cols, axis=1)
        if rotate:
            c = jnp.tile(c_ref[...], (1, MAIN_W // 128))
            s = jnp.tile(s_ref[...], (1, MAIN_W // 128))
            d = d * c - _swap_halves(d) * s
        o_ref[...] = d.astype(bf16)

    part = pl.BlockSpec((tm, GROUP_W), lambda i: (i, 0))
    tab = pl.BlockSpec((tm, 128), lambda i: (i, 0))
    return pl.pallas_call(
        body, name=name, grid=(M // tm,),
        in_specs=[part] * len(flat) + [tab, tab] + [pl.BlockSpec(memory_space=pl.ANY)] * len(extra),
        out_specs=pl.BlockSpec((tm, MAIN_W), lambda i: (i, col_block)),
        out_shape=_sds((M, width), bf16),
        input_output_aliases={len(flat) + 2: 0} if extra else {},
        compiler_params=_params(("parallel",)),
    )(*flat, cos, sin, *extra)


PAIR_W = 2 * HEAD_DIM
MIN_BLOCKS = 8


def _dil_geometry(dil):
    nsub = max(dil, MIN_BLOCKS)
    tb = BAND * nsub
    return nsub, tb, SEQ // tb


def _rows(ref, sub, dil):
    if dil == 1:
        return ref[sub * BAND:(sub + 1) * BAND, :]
    nl, r = divmod(sub, dil)
    return ref[pl.ds(nl * BAND * dil + r, BAND, stride=dil), :]


def _store_rows(ref, sub, dil, val):
    if dil == 1:
        ref[sub * BAND:(sub + 1) * BAND, :] = val
    else:
        nl, r = divmod(sub, dil)
        ref[pl.ds(nl * BAND * dil + r, BAND, stride=dil), :] = val


def _keys(prev_ref, own_ref, sub, dil):
    nsub = own_ref.shape[0] // BAND
    if sub >= dil:
        prev = _rows(own_ref, sub - dil, dil)
    elif prev_ref is None:
        return _rows(own_ref, sub, dil)
    else:
        prev = _rows(prev_ref, nsub - dil + sub, dil)
    return jnp.concatenate([prev, _rows(own_ref, sub, dil)], axis=0)


def _band_mask(nkeys, has_prev):
    i = lax.broadcasted_iota(jnp.int32, (BAND, nkeys), 0)
    j = lax.broadcasted_iota(jnp.int32, (BAND, nkeys), 1)
    if nkeys == BAND:
        return j <= i
    return (j >= i) & (j <= i + BAND) & (has_prev | (j >= BAND))


def _first_head():
    return lax.broadcasted_iota(jnp.int32, (1, PAIR_W), 1) < HEAD_DIM


def _col(x, hh):
    return x[:, hh * HEAD_DIM:hh * HEAD_DIM + 1]


def _pair_spec(tb, nblk, col0, which):
    def idx(b, p, i):
        if which < 0:
            i = jnp.maximum(i - 1, 0)
        elif which > 0:
            i = jnp.minimum(i + 1, nblk - 1)
        return (b * nblk + i, col0 + p)
    return pl.BlockSpec((tb, PAIR_W), idx)


def dil_fwd(q, k, kv, g, dil, *, name, n_seq):
    M = q.shape[0]
    nsub, tb, nblk = _dil_geometry(dil)
    with_prev = nblk > 1

    def body(*refs):
        if with_prev:
            q_ref, ko_ref, vo_ref, kp_ref, vp_ref, o_ref, l_ref = refs
        else:
            (q_ref, ko_ref, vo_ref, o_ref, l_ref), kp_ref, vp_ref = refs, None, None
        first = _first_head()
        blk = pl.program_id(2)
        for sub in range(nsub):
            qs = _rows(q_ref, sub, dil) * SCALE
            kc = _keys(kp_ref, ko_ref, sub, dil).astype(bf16)
            vc = _keys(vp_ref, vo_ref, sub, dil).astype(bf16)
            has_prev = True if sub >= dil else blk > 0
            mask = _band_mask(kc.shape[0], has_prev)
            outs, lses = [], []
            for hh in range(2):
                qm = jnp.where(first if hh == 0 else ~first, qs, 0.0).astype(bf16)
                s = jnp.where(mask, lax.dot_general(qm, kc, NT, preferred_element_type=f32), NEG)
                m = jnp.max(s, axis=-1, keepdims=True)
                e = jnp.exp(s - m)
                l = jnp.sum(e, axis=-1, keepdims=True)
                outs.append(jnp.dot(e.astype(bf16), vc, preferred_element_type=f32) * (1.0 / l))
                lses.append(jnp.broadcast_to(m + jnp.log(l), (BAND, PAIR_W)))
            _store_rows(o_ref, sub, dil, jnp.where(first, outs[0], outs[1]))
            _store_rows(l_ref, sub, dil, jnp.where(first, lses[0], lses[1]))

    ins = [(q, 2 * g, 0), (k, 2 * g, 0), (kv, 6 + 2 * g, 0)]
    if with_prev:
        ins += [(k, 2 * g, -1), (kv, 6 + 2 * g, -1)]
    out = _pair_spec(tb, nblk, 0, 0)
    return pl.pallas_call(
        body, name=name, grid=(n_seq, 2, nblk),
        in_specs=[_pair_spec(tb, nblk, c, w) for _, c, w in ins],
        out_specs=[out, out],
        out_shape=[_sds((M, GROUP_W), f32)] * 2,
        compiler_params=_params(("parallel", "parallel", "arbitrary")),
    )(*[a for a, _, _ in ins])


def combine_fwd(os_, lses, *, name):
    M = os_[0].shape[0]
    tm = min(512, M)

    def body(o0, o1, o2, l0, l1, l2, y_ref):
        ls = [l0[...], l1[...], l2[...]]
        m = jnp.maximum(jnp.maximum(ls[0], ls[1]), ls[2])
        es = [jnp.exp(l - m) for l in ls]
        inv = 1.0 / (es[0] + es[1] + es[2])
        y_ref[...] = jnp.concatenate([o[...] * e * inv for o, e in zip((o0, o1, o2), es)], axis=1).astype(bf16)

    part = pl.BlockSpec((tm, GROUP_W), lambda i: (i, 0))
    return pl.pallas_call(
        body, name=name, grid=(M // tm,),
        in_specs=[part] * 6,
        out_specs=pl.BlockSpec((tm, MAIN_W), lambda i: (i, 0)),
        out_shape=_sds((M, D_MODEL), bf16),
        compiler_params=_params(("parallel",)),
    )(*os_, *lses)


def combine_bwd(dyc, os_, lses, *, name):
    M = os_[0].shape[0]
    tm = min(512, M)

    def body(dy_ref, o0, o1, o2, l0, l1, l2, d0, d1, d2, c0, c1, c2):
        r = lax.broadcasted_iota(jnp.int32, (GROUP_W, GROUP_W), 0) // HEAD_DIM
        c = lax.broadcasted_iota(jnp.int32, (GROUP_W, GROUP_W), 1) // HEAD_DIM
        ones = (r == c).astype(f32)
        dy = dy_ref[...].astype(f32)
        ls = [l0[...], l1[...], l2[...]]
        m = jnp.maximum(jnp.maximum(ls[0], ls[1]), ls[2])
        es = [jnp.exp(l - m) for l in ls]
        inv = 1.0 / (es[0] + es[1] + es[2])
        total = 0.0
        alphas = []
        for g, (o, e, d_ref) in enumerate(zip((o0, o1, o2), es, (d0, d1, d2))):
            a = e * inv
            dyg = dy[:, g * GROUP_W:(g + 1) * GROUP_W]
            d_ref[...] = dyg * a
            dsum = jnp.dot(dyg * o[...], ones, precision=lax.Precision.HIGHEST, preferred_element_type=f32)
            total = total + a * dsum
            alphas.append(a)
        for a, c_ref in zip(alphas, (c0, c1, c2)):
            c_ref[...] = -a * total

    part = pl.BlockSpec((tm, GROUP_W), lambda i: (i, 0))
    outs = pl.pallas_call(
        body, name=name, grid=(M // tm,),
        in_specs=[pl.BlockSpec((tm, MAIN_W), lambda i: (i, 0))] + [part] * 6,
        out_specs=[part] * 6,
        out_shape=[_sds((M, GROUP_W), f32)] * 6,
        compiler_params=_params(("parallel",)),
    )(dyc, *os_, *lses)
    return outs[:3], outs[3:]


def dil_bwd(q, k, kv, do, cc, lse, g, dil, *, name, n_seq):
    M = q.shape[0]
    nsub = SEQ // BAND
    per_res = nsub // dil

    def body(q_ref, k_ref, v_ref, do_ref, c_ref, l_ref, dq_ref, dk_ref, dv_ref):
        first = _first_head()
        for r in range(dil):
            carry = None
            for nl in range(per_res):
                sub = nl * dil + r
                qs = _rows(q_ref, sub, dil) * SCALE
                dos = _rows(do_ref, sub, dil)
                cs = _rows(c_ref, sub, dil)
                ls = _rows(l_ref, sub, dil)
                kc = _keys(None, k_ref, sub, dil).astype(bf16)
                vc = _keys(None, v_ref, sub, dil).astype(bf16)
                nkeys = kc.shape[0]
                mask = _band_mask(nkeys, True)
                dqs = []
                dkc = jnp.zeros((nkeys, PAIR_W), f32)
                dvc = jnp.zeros((nkeys, PAIR_W), f32)
                for hh in range(2):
                    lm = first if hh == 0 else ~first
                    qm = jnp.where(lm, qs, 0.0).astype(bf16)
                    dom = jnp.where(lm, dos, 0.0).astype(bf16)
                    s = jnp.where(mask, lax.dot_general(qm, kc, NT, preferred_element_type=f32), NEG)
                    p = jnp.exp(s - _col(ls, hh))
                    dp = lax.dot_general(dom, vc, NT, preferred_element_type=f32)
                    ds = (p * (dp + _col(cs, hh))).astype(bf16)
                    dqs.append(jnp.dot(ds, kc, preferred_element_type=f32) * SCALE)
                    dkc = dkc + lax.dot_general(ds, qm, TN, preferred_element_type=f32)
                    dvc = dvc + lax.dot_general(p.astype(bf16), dom, TN, preferred_element_type=f32)
                _store_rows(dq_ref, sub, dil, jnp.where(first, dqs[0], dqs[1]))
                if nkeys == 2 * BAND:
                    _store_rows(dk_ref, sub - dil, dil, carry[0] + dkc[:BAND])
                    _store_rows(dv_ref, sub - dil, dil, carry[1] + dvc[:BAND])
                    carry = (dkc[BAND:], dvc[BAND:])
                else:
                    carry = (dkc, dvc)
            _store_rows(dk_ref, (per_res - 1) * dil + r, dil, carry[0])
            _store_rows(dv_ref, (per_res - 1) * dil + r, dil, carry[1])

    def spec(col0):
        return pl.BlockSpec((SEQ, PAIR_W), lambda b, p: (b, col0 + p))

    out = spec(0)
    return pl.pallas_call(
        body, name=name, grid=(n_seq, 2),
        in_specs=[spec(2 * g), spec(2 * g), spec(6 + 2 * g), spec(0), spec(0), spec(0)],
        out_specs=[out, out, out],
        out_shape=[_sds((M, GROUP_W), f32)] * 3,
        compiler_params=_params(("parallel", "parallel")),
    )(q, k, kv, do, cc, lse)


def _blockdiag(wp):
    out = jnp.zeros((MAIN_W, MAIN_W), wp.dtype)
    for gi in range(len(POOL_WINDOWS)):
        sl = slice(gi * POOL_GROUP, (gi + 1) * POOL_GROUP)
        out = out.at[sl, sl].set(wp[gi])
    return out


def _unblockdiag(w):
    return jnp.stack([w[gi * POOL_GROUP:(gi + 1) * POOL_GROUP, gi * POOL_GROUP:(gi + 1) * POOL_GROUP]
                      for gi in range(len(POOL_WINDOWS))])


def local_step(x, mem, positions, target, P, layer_weights, kv_weight, emit_grads):
    n_seq = x.shape[0]
    M = n_seq * SEQ
    xs = x.reshape(M, D_MODEL)
    mems = mem.reshape(n_seq * N_MEM, D_MODEL)
    pos = positions.reshape(M, 1).astype(f32)
    cos, sin = rope_tables(pos, name="rope_tables")
    gains = P["norm_gains"]

    def gain(l, k):
        return gains[l, k].reshape(1, D_MODEL)

    saved = []
    kvs = None
    for l in range(DEPTH):
        W, started = layer_weights(l, "mix", xs)
        sv = {"x": xs, "W": W}
        z, h1 = rms_matmul(xs, gain(l, 0), W["w_in"], name=f"l{l}_in", out_dtype=f32, after=started)
        kvm, mn = rms_matmul(mems, P["mem_norm"][l].reshape(1, D_MODEL), W["w_mem_kv"],
                             name=f"l{l}_memkv", out_dtype=bf16)
        sv.update(z=z, h1=h1, kvm=kvm, mn=mn)
        if l < N_A_LAYERS:
            wbd = _blockdiag(P["w_pool"][l].astype(bf16))
            psc = P["pool_scale"][l].reshape(1, MAIN_W)
            p, y_main = pool_fwd(z, wbd, psc, name=f"l{l}_pool")
            sv.update(p=p, wbd=wbd, psc=psc)
        else:
            qrot = rope_fwd(z, cos, sin, name=f"l{l}_ropeq")
            os_, lses = [], []
            for g, (_, dil) in enumerate(DIL_PATTERNS):
                o, lse = dil_fwd(qrot, kvs["krot"], kvs["kv"], g, dil, name=f"l{l}_dil{g}", n_seq=n_seq)
                os_.append(o)
                lses.append(lse)
            y_main = combine_fwd(os_, lses, name=f"l{l}_comb")
            sv.update(qrot=qrot, os=os_, lses=lses)
        ycat = memattn_fwd(z, kvm, y_main, name=f"l{l}_memattn", n_seq=n_seq)
        y, x1 = matmul_rms_res(ycat, W["w_out"], gain(l, 1), xs, name=f"l{l}_out")
        W.update(layer_weights(l, "gu", x1)[0])
        fg, fu, a, h2 = rms_gate_up(x1, gain(l, 2), W["w_gate_up"], name=f"l{l}_gu")
        W.update(layer_weights(l, "down", a)[0])
        y2, x2 = matmul_rms_res(a, W["w_down"], gain(l, 3), x1, name=f"l{l}_down")
        sv.update(ycat=ycat, y=y, x1=x1, fg=fg, fu=fu, h2=h2, a=a, y2=y2)
        saved.append(sv)
        xs = x2
        if l == N_A_LAYERS - 1:
            w_kv = kv_weight(xs)
            kv, hkv = rms_matmul(xs, P["kv_norm"].reshape(1, D_MODEL), w_kv, name="kv_proj", out_dtype=f32,
                                 transposed=True)
            krot = rope_fwd(kv, cos, sin, name="ropek")
            kvs = {"kv": kv, "hkv": hkv, "krot": krot, "x": xs, "w_kv": w_kv}

    dx, sq = loss_head(xs, target.reshape(M, D_MODEL), name="loss_head")

    G = {"mem_norm": [None] * DEPTH, "norm_gains": [[None] * 4 for _ in range(DEPTH)],
         "pool_scale": [None] * N_A_LAYERS}
    dk_parts = [[] for _ in range(N_GROUPS)]
    dv_parts = [[] for _ in range(N_GROUPS)]
    emitted = None

    for l in reversed(range(DEPTH)):
        sv = saved[l]
        W = sv["W"]
        gw = {}
        dy2, dgu, G["norm_gains"][l][3] = down_bwd(sv["y2"], gain(l, 3), dx, W["w_down"], sv["fg"], sv["fu"],
                                                   name=f"l{l}_b_dgu", after=emitted)
        gw["w_down"] = matmul(sv["a"], dy2, TN, name=f"l{l}_b_wd", out_dtype=bf16)
        dx1, G["norm_gains"][l][2] = matmul_rms_bwd(dgu, W["w_gate_up"], NN, sv["x1"], gain(l, 2), dx,
                                                    name=f"l{l}_b_dh2")
        gw["w_gate_up"] = matmul(dgu, sv["h2"], TN, name=f"l{l}_b_wgu", out_dtype=bf16)
        emitted = emit_grads(l, "ffn", gw)
        gw = {}
        dy, dycat, G["norm_gains"][l][1] = rms_bwd_matmul(sv["y"], gain(l, 1), dx1, W["w_out"], NT,
                                                          name=f"l{l}_b_dycat", after=emitted)
        gw["w_out"] = matmul(sv["ycat"], dy, TN, name=f"l{l}_b_wout", out_dtype=bf16)
        if l < N_A_LAYERS:
            dz, dwbd, dps = pool_bwd(dycat, sv["p"], sv["wbd"], sv["psc"], name=f"l{l}_b_pool")
            gw["w_pool"] = _unblockdiag(dwbd).reshape(MAIN_W, POOL_GROUP).astype(bf16)
            G["pool_scale"][l] = dps.reshape(MAIN_W)
        else:
            dos, ccs = combine_bwd(dycat, sv["os"], sv["lses"], name=f"l{l}_b_comb")
            dqs = []
            for g, (_, dil) in enumerate(DIL_PATTERNS):
                args = (sv["qrot"], kvs["krot"], kvs["kv"], dos[g], ccs[g], sv["lses"][g], g, dil)
                dq, dk, dv = dil_bwd(*args, name=f"l{l}_b_dil{g}", n_seq=n_seq)
                dqs.append([dq])
                dk_parts[g].append(dk)
                dv_parts[g].append(dv)
            dz = group_sum(dqs, cos, sin, name=f"l{l}_b_ropeq", rotate=True, width=D_MODEL)
        dz, dkvm = memattn_bwd(sv["z"], sv["kvm"], dycat, dz, name=f"l{l}_b_memattn", n_seq=n_seq)
        dmn = matmul(dkvm, W["w_mem_kv"], NT, name=f"l{l}_b_dmn", out_dtype=bf16)
        gw["w_mem_kv"] = matmul(sv["mn"], dkvm, TN, name=f"l{l}_b_wmkv", out_dtype=bf16)
        _, G["mem_norm"][l] = rms_bwd(mems, P["mem_norm"][l].reshape(1, D_MODEL), dmn, None,
                                      name=f"l{l}_b_nmem", out_dtype=bf16)
        gw["w_in"] = matmul(sv["h1"], dz, TN, name=f"l{l}_b_win", out_dtype=bf16)
        dx, G["norm_gains"][l][0] = matmul_rms_bwd(dz, W["w_in"], NT, sv["x"], gain(l, 0), dx1, name=f"l{l}_b_dh1")
        if l == N_A_LAYERS:
            dkv = group_sum(dk_parts, cos, sin, name="b_ropek", rotate=True, width=2 * MAIN_W)
            dkv = group_sum(dv_parts, cos, sin, name="b_sumv", rotate=False, width=2 * MAIN_W, col_block=1, into=dkv)
            gw["w_kv"] = matmul(dkv, kvs["hkv"], TN, name="b_wkv", out_dtype=bf16)
            dx, gkn = matmul_rms_bwd(dkv, kvs["w_kv"], NN, kvs["x"], P["kv_norm"].reshape(1, D_MODEL), dx,
                                     name="b_dhkv")
            G["kv_norm"] = gkn.reshape(D_MODEL)
        emitted = emit_grads(l, "mix", gw)

    small = {"pool_scale": jnp.stack(G["pool_scale"]),
             "mem_norm": jnp.concatenate(G["mem_norm"], axis=0),
             "norm_gains": jnp.stack([jnp.concatenate(r, axis=0) for r in G["norm_gains"]]),
             "kv_norm": G["kv_norm"]}
    return sq[0, 0], dx.reshape(n_seq, SEQ, D_MODEL), small, emitted


def _peer(k):
    x, y, c = lax.axis_index("x"), lax.axis_index("y"), lax.axis_index("c")
    px = 1 - x if k & 4 else x
    py = 1 - y if k & 2 else y
    pc = 1 - c if k & 1 else c
    return (px, py, pc), 4 * px + 2 * py + pc


def _my_index():
    return 4 * lax.axis_index("x") + 2 * lax.axis_index("y") + lax.axis_index("c")


def _src_for(kinds, in_refs, i, idx):
    return in_refs[i] if kinds[i] == "gather" else in_refs[i].at[idx]


def _local_copies(kinds, in_refs, out_refs, local_sems):
    me = _my_index()
    return [pltpu.make_async_copy(_src_for(kinds, in_refs, i, me), out_refs[i].at[me], local_sems.at[i])
            for i in range(len(kinds))]


def _remote_copies(kinds, in_refs, out_refs, send_sems, recv_sems, *, arriving):
    me = _my_index()
    copies = []
    for k in range(1, N_DEV):
        dev, idx = _peer(k)
        for i in range(len(kinds)):
            j = i * (N_DEV - 1) + k - 1
            copies.append(pltpu.make_async_remote_copy(
                src_ref=_src_for(kinds, in_refs, i, idx), dst_ref=out_refs[i].at[idx if arriving else me],
                send_sem=send_sems.at[j], recv_sem=recv_sems.at[j], device_id=dev, device_id_type=MESH))
    return copies


def _out_shape(a, kind):
    return ((N_DEV,) + a.shape) if kind == "gather" else a.shape


def exchange(items, *, name, after=()):
    n = len(items)
    kinds = [k for _, k in items]
    after = list(after)

    def body(*refs):
        in_refs, out_refs = refs[:n], refs[n + len(after):2 * n + len(after)]
        send_sems, recv_sems, local_sems = refs[-3:]
        local = _local_copies(kinds, in_refs, out_refs, local_sems)
        sends = _remote_copies(kinds, in_refs, out_refs, send_sems, recv_sems, arriving=False)
        for cp in local + sends:
            cp.start()
        for cp in _remote_copies(kinds, in_refs, out_refs, send_sems, recv_sems, arriving=True):
            cp.wait_recv()
        for cp in sends:
            cp.wait_send()
        for cp in local:
            cp.wait()

    any_spec = pl.BlockSpec(memory_space=pl.ANY)
    return pl.pallas_call(
        body, name=name,
        in_specs=[any_spec] * (n + len(after)), out_specs=[any_spec] * n,
        out_shape=[_sds(_out_shape(a, k), a.dtype) for a, k in items],
        scratch_shapes=[pltpu.SemaphoreType.DMA((n * (N_DEV - 1),)), pltpu.SemaphoreType.DMA((n * (N_DEV - 1),)),
                        pltpu.SemaphoreType.DMA((n,))],
    )(*[a for a, _ in items], *after)


_HBM = pl.BlockSpec(memory_space=pltpu.HBM)
_SEM = pl.BlockSpec(memory_space=pltpu.SEMAPHORE)
_EFFECT = pltpu.SideEffectType.DATAFLOW_SIDE_EFFECTING


def exchange_start(items, after, *, name):
    n = len(items)
    kinds = [k for _, k in items]

    def body(*refs):
        in_refs, land_refs = refs[:n], refs[n:2 * n]
        send_sems, recv_sems, local_sems = refs[2 * n + 1:2 * n + 4]
        token = refs[-1]
        for cp in (_local_copies(kinds, in_refs, land_refs, local_sems)
                   + _remote_copies(kinds, in_refs, land_refs, send_sems, recv_sems, arriving=False)):
            cp.start()
        token[...] = jnp.zeros_like(token)

    srcs = [pltpu.with_memory_space_constraint(a, pltpu.HBM) for a, _ in items]
    lands = [pltpu.with_memory_space_constraint(lax.empty(_out_shape(a, k), a.dtype), pltpu.HBM) for a, k in items]
    outs = pl.pallas_call(
        body, name=name,
        out_shape=(pltpu.SemaphoreType.DMA((n * (N_DEV - 1),)), pltpu.SemaphoreType.DMA((n * (N_DEV - 1),)),
                   pltpu.SemaphoreType.DMA((n,)),
                   *[pltpu.HBM(a.shape, a.dtype) for a in srcs], *[pltpu.HBM(a.shape, a.dtype) for a in lands],
                   _sds((8, 128), f32)),
        in_specs=[_HBM] * (2 * n) + [pl.BlockSpec(memory_space=pl.ANY)],
        out_specs=(_SEM, _SEM, _SEM, *[_HBM] * (2 * n), pl.BlockSpec(memory_space=pltpu.VMEM)),
        input_output_aliases={i: 3 + i for i in range(2 * n)},
        compiler_params=pltpu.CompilerParams(has_side_effects=_EFFECT),
    )(*srcs, *lands, after)
    return {"kinds": kinds, "sems": outs[:3], "srcs": outs[3:3 + n], "lands": outs[3 + n:3 + 2 * n], "token": outs[-1]}


def exchange_wait(handle, after, *, name):
    kinds = handle["kinds"]
    n = len(kinds)

    def body(*refs):
        in_refs, land_refs = refs[:n], refs[n:2 * n]
        send_sems, recv_sems, local_sems = refs[2 * n:2 * n + 3]
        for cp in _remote_copies(kinds, in_refs, land_refs, send_sems, recv_sems, arriving=True):
            cp.wait_recv()
        for cp in _remote_copies(kinds, in_refs, land_refs, send_sems, recv_sems, arriving=False):
            cp.wait_send()
        for cp in _local_copies(kinds, in_refs, land_refs, local_sems):
            cp.wait()

    srcs, lands = list(handle["srcs"]), list(handle["lands"])
    after = list(after) if isinstance(after, (list, tuple)) else [after]
    outs = pl.pallas_call(
        body, name=name,
        out_shape=tuple(pltpu.HBM(a.shape, a.dtype) for a in srcs + lands),
        in_specs=[_HBM] * (2 * n) + [_SEM] * 3 + [pl.BlockSpec(memory_space=pl.ANY)] * len(after),
        out_specs=tuple([_HBM] * (2 * n)),
        input_output_aliases={i: i for i in range(2 * n)},
        compiler_params=pltpu.CompilerParams(has_side_effects=_EFFECT),
    )(*srcs, *lands, *handle["sems"], *after)
    return list(outs[n:])


def adamw(slots, w, m, v, *, name, layer=None, into=None):
    R, C = w.shape[-2:]
    tr = _tile(R, (256, 128, 64, 32, 16, 8))
    c1 = 1.0 - ADAM_B1 ** ADAM_STEP
    c2 = 1.0 - ADAM_B2 ** ADAM_STEP
    extra = [] if into is None else list(into)

    def body(s_ref, w_ref, m_ref, v_ref, *refs):
        g_ref, d_ref, m2_ref, v2_ref = refs[len(extra):]
        g = s_ref[0].astype(f32)
        for d in range(1, N_DEV):
            g = g + s_ref[d].astype(f32)
        m2 = ADAM_B1 * m_ref[...] + (1.0 - ADAM_B1) * g
        v2 = ADAM_B2 * v_ref[...] + (1.0 - ADAM_B2) * (g * g)
        g_ref[...] = g
        m2_ref[...] = m2
        v2_ref[...] = v2
        d_ref[...] = -ADAM_LR * ((m2 / c1) / (jnp.sqrt(v2 / c2) + ADAM_EPS) + ADAM_WD * w_ref[...])

    if layer is None:
        blk = pl.BlockSpec((tr, C), lambda i: (i, 0))
    else:
        blk = pl.BlockSpec((None, tr, C), lambda i: (layer, i, 0))
    return pl.pallas_call(
        body, name=name, grid=(R // tr,),
        in_specs=[pl.BlockSpec((N_DEV, tr, C), lambda i: (0, i, 0)), blk, blk, blk]
        + [pl.BlockSpec(memory_space=pl.ANY)] * len(extra),
        out_specs=[blk] * 4,
        out_shape=[_sds(w.shape, f32)] * 4,
        input_output_aliases={4 + j: j for j in range(len(extra))},
        compiler_params=_params(("parallel",)),
    )(slots, w, m, v, *extra)


WEIGHTS = ("norm_gains", "mem_norm", "w_in", "w_mem_kv", "w_out", "w_pool", "pool_scale", "kv_norm", "w_kv",
           "w_gate_up", "w_down")
LAYER_MATS = ("w_in", "w_mem_kv", "w_out", "w_gate_up", "w_down")
POOL_SHARD = MAIN_W // N_DEV
KV_SHARD = 2 * MAIN_W // N_DEV
LOOKAHEAD = 2


def _pack_small(gains, pscale):
    lead = gains.shape[:-3]
    g = gains.reshape(lead + (16, 128))
    p = jnp.zeros(lead + (8, 128), f32).at[..., :2, :POOL_SHARD].set(pscale)
    return jnp.concatenate([g, p], axis=-2)


def _unpack_small(a):
    return a[:16].reshape(4, 4, 128), a[16:18, :POOL_SHARD]


def _pack_repl(mem_norm, kv_norm):
    return jnp.concatenate([mem_norm, kv_norm.reshape(1, D_MODEL), jnp.zeros((3, D_MODEL), f32)], axis=0)


def _unpack_repl(a):
    return a[:4], a[4]


def kernel(x, mem, positions, norm_gains, mem_norm, w_in, w_mem_kv, w_out, w_pool, pool_scale, kv_norm, w_kv, w_gate_up, w_down, loss_target, m_norm_gains, m_mem_norm, m_w_in, m_w_mem_kv, m_w_out, m_w_pool, m_pool_scale, m_kv_norm, m_w_kv, m_w_gate_up, m_w_down, v_norm_gains, v_mem_norm, v_w_in, v_w_mem_kv, v_w_out, v_w_pool, v_pool_scale, v_kv_norm, v_w_kv, v_w_gate_up, v_w_down):
    w = dict(norm_gains=norm_gains, mem_norm=mem_norm, w_in=w_in, w_mem_kv=w_mem_kv, w_out=w_out, w_pool=w_pool,
             pool_scale=pool_scale, kv_norm=kv_norm, w_kv=w_kv, w_gate_up=w_gate_up, w_down=w_down)
    m = dict(norm_gains=m_norm_gains, mem_norm=m_mem_norm, w_in=m_w_in, w_mem_kv=m_w_mem_kv, w_out=m_w_out,
             w_pool=m_w_pool, pool_scale=m_pool_scale, kv_norm=m_kv_norm, w_kv=m_w_kv, w_gate_up=m_w_gate_up,
             w_down=m_w_down)
    v = dict(norm_gains=v_norm_gains, mem_norm=v_mem_norm, w_in=v_w_in, w_mem_kv=v_w_mem_kv, w_out=v_w_out,
             w_pool=v_w_pool, pool_scale=v_pool_scale, kv_norm=v_kv_norm, w_kv=v_w_kv, w_gate_up=v_w_gate_up,
             w_down=v_w_down)

    def transposed_view(d):
        d = dict(d)
        d["w_gate_up"] = jnp.swapaxes(d["w_gate_up"], 1, 2)
        d["w_kv"] = jnp.swapaxes(d["w_kv"], 0, 1)
        return d

    wv, mv, vv = transposed_view(w), transposed_view(m), transposed_view(v)

    small = _pack_small(norm_gains, pool_scale)
    (gsmall,) = exchange([(small, "gather")], name="gather_small")
    P = {"norm_gains": jnp.moveaxis(gsmall[:, :16].reshape(N_DEV, 4, 4, 128), 0, 2).reshape(4, 4, D_MODEL),
         "pool_scale": jnp.moveaxis(gsmall[:, 16:18, :POOL_SHARD], 0, 1).reshape(2, MAIN_W),
         "mem_norm": mem_norm, "kv_norm": kv_norm, "w_pool": w_pool}

    PARTS = {"mix": ("w_in", "w_mem_kv", "w_out"), "ffn": ("w_gate_up", "w_down"), "gu": ("w_gate_up",),
             "down": ("w_down",)}

    def parts_of(l):
        return ("mix", "gu", "down") if l == 0 else ("mix", "ffn")

    def part_items(l, part):
        items = [(wv[k][l].astype(bf16), "gather") for k in PARTS[part]]
        if part == "ffn" and l == N_A_LAYERS - 1:
            items.append((wv["w_kv"].astype(bf16), "gather"))
        return items

    handles = {}

    def start_layer(l, after):
        for part in parts_of(l):
            handles[l, part] = exchange_start(part_items(l, part), after, name=f"gather_start_{part}_l{l}")
            after = handles[l, part]["token"]
        return after

    token = gsmall
    for l in range(LOOKAHEAD):
        token = start_layer(l, token)
    landed = {}

    def layer_weights(l, part, after):
        if part not in parts_of(l):
            if part == "down":
                return {}, None
            part = "ffn"
        first = l == 0 and part == "mix"
        got = exchange_wait(handles[l, part], token if first else after, name=f"gather_wait_{part}_l{l}")
        landed[l, part] = got
        started = None
        if part == "mix" and l + LOOKAHEAD < DEPTH:
            started = start_layer(l + LOOKAHEAD, got[0])
        W = {k: g.reshape(-1, g.shape[-1]) for k, g in zip(PARTS[part], got)}
        return W, started

    def kv_weight(after):
        g = landed[N_A_LAYERS - 1, "ffn"][len(PARTS["ffn"])]
        return g.reshape(2 * MAIN_W, D_MODEL)

    ghandles = {}

    def emit_grads(l, part, gw):
        items = [(gw[k].reshape((N_DEV, -1) + gw[k].shape[-1:]), "scatter") for k in PARTS[part]]
        if part == "mix" and l == N_A_LAYERS:
            items.append((gw["w_kv"].reshape(N_DEV, KV_SHARD, D_MODEL), "scatter"))
        if part == "mix" and l < N_A_LAYERS:
            items.append((gw["w_pool"], "gather"))
        ghandles[l, part] = exchange_start(items, gsmall, name=f"scatter_start_{part}_l{l}")
        return ghandles[l, part]["token"]

    sq, grad_x, GS, emitted = local_step(x, mem, positions, loss_target, P, layer_weights, kv_weight, emit_grads)
    loss = lax.psum(0.5 * sq / D_MODEL, ("x", "y", "c"))

    def pool3(a):
        return a.reshape(N_A_LAYERS, MAIN_W, POOL_GROUP)

    out = {}
    after = [emitted]

    def finish_layer(l, after):
        for part in ("ffn", "mix"):
            got = exchange_wait(ghandles[l, part], after, name=f"scatter_wait_{part}_l{l}")
            after = []
            for k, slots in zip(PARTS[part], got):
                out[k] = adamw(slots, wv[k], mv[k], vv[k], name=f"adamw_{k}_l{l}", layer=l, into=out.get(k))
                after.append(out[k][0])
            if part == "mix" and l == N_A_LAYERS:
                out["w_kv"] = adamw(got[-1], wv["w_kv"], mv["w_kv"], vv["w_kv"], name="adamw_w_kv")
                after.append(out["w_kv"][0])
            if part == "mix" and l < N_A_LAYERS:
                out["w_pool"] = adamw(got[-1], pool3(w_pool), pool3(m_w_pool), pool3(v_w_pool), name=f"adamw_w_pool_l{l}",
                                      layer=l, into=out.get("w_pool"))
                after.append(out["w_pool"][0])
        return after

    for l in reversed(range(1, DEPTH)):
        after = finish_layer(l, after)

    gs = _pack_small(jnp.moveaxis(GS["norm_gains"].reshape(4, 4, N_DEV, 128), 2, 0),
                     jnp.moveaxis(GS["pool_scale"].reshape(2, N_DEV, POOL_SHARD), 1, 0))
    parts_small, parts_repl = exchange(
        [(gs, "scatter"), (_pack_repl(GS["mem_norm"], GS["kv_norm"]), "gather")],
        name="exchange_small_grads", after=after)
    finish_layer(0, [parts_small])
    out["w_gate_up"] = [jnp.swapaxes(r, 1, 2) for r in out["w_gate_up"]]
    out["w_kv"] = [jnp.swapaxes(r, 0, 1) for r in out["w_kv"]]
    out["w_pool"] = [r.reshape(w_pool.shape) for r in out["w_pool"]]

    res = adamw(parts_small, small, _pack_small(m_norm_gains, m_pool_scale), _pack_small(v_norm_gains, v_pool_scale),
                name="adamw_small")
    out["norm_gains"], out["pool_scale"] = zip(*[_unpack_small(r) for r in res])
    res = adamw(parts_repl, _pack_repl(mem_norm, kv_norm), _pack_repl(m_mem_norm, m_kv_norm),
                _pack_repl(v_mem_norm, v_kv_norm), name="adamw_repl")
    out["mem_norm"], out["kv_norm"] = zip(*[_unpack_repl(r) for r in res])

    return (loss, grad_x, *[out[k][0] for k in WEIGHTS], *[out[k][1] for k in WEIGHTS],
            *[out[k][2] for k in WEIGHTS], *[out[k][3] for k in WEIGHTS])
```

```python
import numpy as np
import jax
import jax.numpy as jnp
from jax import lax
from jax.experimental import pallas as pl
from jax.experimental.pallas import tpu as pltpu

f32 = jnp.float32
bf16 = jnp.bfloat16

D_MODEL = 1024
SEQ = 2048
DEPTH = 4
N_MEM = 256
HEAD_DIM = 64
N_MEM_HEADS = 4
MEM_W = 256
MAIN_W = 768
POOL_WINDOWS = (2, 4, 8, 16)
POOL_GROUP = 192
POOL_HALO = 16
DIL_PATTERNS = ((128, 1), (512, 4), (2048, 16))
N_GROUPS = 3
GROUP_W = 256
BAND = 128
N_A_LAYERS = 2
D_FF = 2816
ROPE_THETA = 10000.0
EPS = 1e-6
NEG = -1e30
SCALE = HEAD_DIM ** -0.5
N_DEV = 8

ADAM_LR = 0.001
ADAM_B1 = 0.9
ADAM_B2 = 0.999
ADAM_EPS = 1e-08
ADAM_WD = 0.01
ADAM_STEP = 10

VMEM_LIMIT_BYTES = 56 * 1024 * 1024
MESH = pl.DeviceIdType.MESH

NN = (((1,), (0,)), ((), ()))
NT = (((1,), (1,)), ((), ()))
TN = (((0,), (0,)), ((), ()))


def _params(sem=None):
    return pltpu.CompilerParams(dimension_semantics=sem, vmem_limit_bytes=VMEM_LIMIT_BYTES)


def _tile(n, cands):
    for c in cands:
        if n % c == 0:
            return c
    return n


def _sds(shape, dtype):
    return jax.ShapeDtypeStruct(tuple(shape), dtype)


def _rms_r(v):
    return lax.rsqrt(jnp.mean(v * v, axis=-1, keepdims=True) + EPS)


def rms_matmul(x, gain, w, *, name, out_dtype, transposed=False, after=None):
    M, K = x.shape
    N = w.shape[0] if transposed else w.shape[1]
    tm = min(2048, M)
    tn = _tile(N, (512, 256, 128))
    order = [] if after is None else [after]

    def body(x_ref, g_ref, w_ref, *refs):
        z_ref, h_ref = refs[len(order):]

        @pl.when(pl.program_id(1) == 0)
        def _():
            xv = x_ref[...]
            h_ref[...] = (xv * _rms_r(xv) * g_ref[...]).astype(bf16)

        z_ref[...] = lax.dot_general(h_ref[...], w_ref[...], NT if transposed else NN,
                                     preferred_element_type=f32).astype(z_ref.dtype)

    w_spec = pl.BlockSpec((tn, K), lambda i, j: (j, 0)) if transposed else pl.BlockSpec((K, tn), lambda i, j: (0, j))
    return pl.pallas_call(
        body, name=name, grid=(M // tm, N // tn),
        in_specs=[pl.BlockSpec((tm, K), lambda i, j: (i, 0)),
                  pl.BlockSpec((1, K), lambda i, j: (0, 0)),
                  w_spec] + [pl.BlockSpec(memory_space=pl.ANY)] * len(order),
        out_specs=[pl.BlockSpec((tm, tn), lambda i, j: (i, j)), pl.BlockSpec((tm, K), lambda i, j: (i, 0))],
        out_shape=[_sds((M, N), out_dtype), _sds((M, K), bf16)],
        compiler_params=_params(("parallel", "arbitrary")),
    )(x, gain, w, *order)


def matmul_rms_res(a, w, gain, res, *, name):
    M, K = a.shape
    N = w.shape[1]
    tm = min(1024, M)

    def body(a_ref, w_ref, g_ref, r_ref, y_ref, x_ref):
        y = jnp.dot(a_ref[...], w_ref[...], preferred_element_type=f32)
        y_ref[...] = y.astype(bf16)
        x_ref[...] = r_ref[...] + y * _rms_r(y) * g_ref[...]

    row = pl.BlockSpec((tm, N), lambda i: (i, 0))
    return pl.pallas_call(
        body, name=name, grid=(M // tm,),
        in_specs=[pl.BlockSpec((tm, K), lambda i: (i, 0)),
                  pl.BlockSpec((K, N), lambda i: (0, 0)),
                  pl.BlockSpec((1, N), lambda i: (0, 0)),
                  row],
        out_specs=[row, row],
        out_shape=[_sds((M, N), bf16), _sds((M, N), f32)],
        compiler_params=_params(("parallel",)),
    )(a, w, gain, res)


def matmul(a, b, dims, *, name, out_dtype):
    if dims is TN:
        K, M = a.shape
        tm = _tile(M, (512, 256, 128))
        a_spec = pl.BlockSpec((K, tm), lambda i: (0, i))
    else:
        M, K = a.shape
        tm = _tile(M, (1024, 512, 256, 128))
        a_spec = pl.BlockSpec((tm, K), lambda i: (i, 0))
    N = b.shape[0] if dims is NT else b.shape[1]

    def body(a_ref, b_ref, o_ref):
        o_ref[...] = lax.dot_general(a_ref[...].astype(bf16), b_ref[...].astype(bf16), dims,
                                     preferred_element_type=f32).astype(o_ref.dtype)

    return pl.pallas_call(
        body, name=name, grid=(M // tm,),
        in_specs=[a_spec, pl.BlockSpec(b.shape, lambda i: (0, 0))],
        out_specs=pl.BlockSpec((tm, N), lambda i: (i, 0)),
        out_shape=_sds((M, N), out_dtype),
        compiler_params=_params(("parallel",)),
    )(a, b)


def rms_gate_up(x, gain, wt, *, name):
    M, K = x.shape
    tm = min(2048, M)
    tn = _tile(D_FF, (256, 128))
    nj = D_FF // tn

    def body(x_ref, gn_ref, wg_ref, wu_ref, g_ref, u_ref, a_ref, h_ref):
        @pl.when(pl.program_id(1) == 0)
        def _():
            xv = x_ref[...]
            h_ref[...] = (xv * _rms_r(xv) * gn_ref[...]).astype(bf16)

        h = h_ref[...]
        g = lax.dot_general(h, wg_ref[...], NT, preferred_element_type=f32)
        u = lax.dot_general(h, wu_ref[...], NT, preferred_element_type=f32)
        g_ref[...] = g.astype(bf16)
        u_ref[...] = u.astype(bf16)
        a_ref[...] = (g * (1.0 / (1.0 + jnp.exp(-g))) * u).astype(bf16)

    col = pl.BlockSpec((tm, tn), lambda i, j: (i, j))
    return pl.pallas_call(
        body, name=name, grid=(M // tm, nj),
        in_specs=[pl.BlockSpec((tm, K), lambda i, j: (i, 0)),
                  pl.BlockSpec((1, K), lambda i, j: (0, 0)),
                  pl.BlockSpec((tn, K), lambda i, j: (j, 0)),
                  pl.BlockSpec((tn, K), lambda i, j: (j + nj, 0))],
        out_specs=[col, col, col, pl.BlockSpec((tm, K), lambda i, j: (i, 0))],
        out_shape=[_sds((M, D_FF), bf16)] * 3 + [_sds((M, K), bf16)],
        compiler_params=_params(("parallel", "arbitrary")),
    )(x, gain, wt, wt)


def _rms_bwd_math(yv, gain, dn):
    r = _rms_r(yv)
    q = dn * gain
    dy = r * q - yv * (r * r * r) * jnp.mean(q * yv, axis=-1, keepdims=True)
    return dy, jnp.sum(dn * yv * r, axis=0, keepdims=True)


def _accumulate(ref, val):
    @pl.when(pl.program_id(0) == 0)
    def _():
        ref[...] = jnp.zeros_like(ref)

    ref[...] += val


def down_bwd(y, gain, dn, w_down, g, u, *, name, after=None):
    M, K = y.shape
    tm = min(512, M)
    order = [] if after is None else [after]

    def body(y_ref, gn_ref, dn_ref, w_ref, g_ref, u_ref, *refs):
        dy_ref, o_ref, dg_ref = refs[len(order):]
        dy, dgain = _rms_bwd_math(y_ref[...].astype(f32), gn_ref[...], dn_ref[...])
        dy = dy.astype(bf16)
        dy_ref[...] = dy
        _accumulate(dg_ref, dgain)
        da = lax.dot_general(dy, w_ref[...], NT, preferred_element_type=f32)
        g = g_ref[...].astype(f32)
        u = u_ref[...].astype(f32)
        s = 1.0 / (1.0 + jnp.exp(-g))
        o_ref[:, :D_FF] = (da * u * s * (1.0 + g * (1.0 - s))).astype(bf16)
        o_ref[:, D_FF:] = (da * g * s).astype(bf16)

    row = pl.BlockSpec((tm, K), lambda i: (i, 0))
    vec = pl.BlockSpec((1, K), lambda i: (0, 0))
    wide = pl.BlockSpec((tm, D_FF), lambda i: (i, 0))
    return pl.pallas_call(
        body, name=name, grid=(M // tm,),
        in_specs=[row, vec, row, pl.BlockSpec((D_FF, K), lambda i: (0, 0)), wide, wide]
        + [pl.BlockSpec(memory_space=pl.ANY)] * len(order),
        out_specs=[row, pl.BlockSpec((tm, 2 * D_FF), lambda i: (i, 0)), vec],
        out_shape=[_sds((M, K), bf16), _sds((M, 2 * D_FF), bf16), _sds((1, K), f32)],
        compiler_params=_params(("arbitrary",)),
    )(y, gain, dn, w_down, g, u, *order)


def rms_bwd_matmul(y, gain, dn, w, dims, *, name, after=None):
    M, K = y.shape
    N = w.shape[0] if dims is NT else w.shape[1]
    tm = min(1024, M)
    order = [] if after is None else [after]

    def body(y_ref, gn_ref, dn_ref, w_ref, *refs):
        dy_ref, o_ref, dg_ref = refs[len(order):]
        dy, dgain = _rms_bwd_math(y_ref[...].astype(f32), gn_ref[...], dn_ref[...].astype(f32))
        dy = dy.astype(bf16)
        dy_ref[...] = dy
        _accumulate(dg_ref, dgain)
        o_ref[...] = lax.dot_general(dy, w_ref[...], dims, preferred_element_type=f32).astype(bf16)

    row = pl.BlockSpec((tm, K), lambda i: (i, 0))
    vec = pl.BlockSpec((1, K), lambda i: (0, 0))
    return pl.pallas_call(
        body, name=name, grid=(M // tm,),
        in_specs=[row, vec, row, pl.BlockSpec(w.shape, lambda i: (0, 0))]
        + [pl.BlockSpec(memory_space=pl.ANY)] * len(order),
        out_specs=[row, pl.BlockSpec((tm, N), lambda i: (i, 0)), vec],
        out_shape=[_sds((M, K), bf16), _sds((M, N), bf16), _sds((1, K), f32)],
        compiler_params=_params(("arbitrary",)),
    )(y, gain, dn, w, *order)


def matmul_rms_bwd(a, b, dims, y, gain, res, *, name):
    M, K = a.shape
    N = y.shape[1]
    tm = 512

    def body(a_ref, b_ref, y_ref, gn_ref, r_ref, dx_ref, dg_ref):
        dn = lax.dot_general(a_ref[...], b_ref[...], dims, preferred_element_type=f32)
        dy, dgain = _rms_bwd_math(y_ref[...], gn_ref[...], dn)
        dx_ref[...] = dy + r_ref[...]
        _accumulate(dg_ref, dgain)

    row = pl.BlockSpec((tm, N), lambda i: (i, 0))
    vec = pl.BlockSpec((1, N), lambda i: (0, 0))
    return pl.pallas_call(
        body, name=name, grid=(M // tm,),
        in_specs=[pl.BlockSpec((tm, K), lambda i: (i, 0)), pl.BlockSpec(b.shape, lambda i: (0, 0)), row, vec, row],
        out_specs=[row, vec],
        out_shape=[_sds((M, N), f32), _sds((1, N), f32)],
        compiler_params=_params(("arbitrary",)),
    )(a, b, y, gain, res)


def rms_bwd(y, gain, dn, res, *, name, out_dtype, after=None):
    M, N = y.shape
    tm = min(512, M)
    has_res = res is not None
    order = [] if after is None else [after]

    def body(*refs):
        y_ref, g_ref, dn_ref = refs[:3]
        r_ref = refs[3] if has_res else None
        dy_ref, dg_ref = refs[-2:]
        dy, dgain = _rms_bwd_math(y_ref[...].astype(f32), g_ref[...], dn_ref[...].astype(f32))
        if has_res:
            dy = dy + r_ref[...]
        dy_ref[...] = dy.astype(dy_ref.dtype)
        _accumulate(dg_ref, dgain)

    row = pl.BlockSpec((tm, N), lambda i: (i, 0))
    vec = pl.BlockSpec((1, N), lambda i: (0, 0))
    args = [y, gain, dn] + ([res] if has_res else []) + order
    return pl.pallas_call(
        body, name=name, grid=(M // tm,),
        in_specs=[row, vec, row] + ([row] if has_res else []) + [pl.BlockSpec(memory_space=pl.ANY)] * len(order),
        out_specs=[row, vec],
        out_shape=[_sds((M, N), out_dtype), _sds((1, N), f32)],
        compiler_params=_params(("arbitrary",)),
    )(*args)


def loss_head(x, target, *, name):
    M, N = x.shape
    tm = min(512, M)

    def body(x_ref, t_ref, dx_ref, l_ref):
        e = x_ref[...] - t_ref[...]
        dx_ref[...] = e * (1.0 / N)

        @pl.when(pl.program_id(0) == 0)
        def _():
            l_ref[...] = jnp.zeros_like(l_ref)

        l_ref[...] += jnp.sum(jnp.sum(e * e, axis=0, keepdims=True), axis=1, keepdims=True)

    row = pl.BlockSpec((tm, N), lambda i: (i, 0))
    return pl.pallas_call(
        body, name=name, grid=(M // tm,),
        in_specs=[row, row],
        out_specs=[row, pl.BlockSpec((8, 128), lambda i: (0, 0))],
        out_shape=[_sds((M, N), f32), _sds((8, 128), f32)],
        compiler_params=_params(("arbitrary",)),
    )(x, target)


def _pool_select(a1, a2, a3, a4):
    col = lax.broadcasted_iota(jnp.int32, (1, MAIN_W), 1) // POOL_GROUP
    return jnp.where(col == 0, a1, jnp.where(col == 1, a2, jnp.where(col == 2, a3, a4)))


def _pool_count(t):
    col = lax.broadcasted_iota(jnp.int32, (1, MAIN_W), 1) // POOL_GROUP
    win = jnp.where(col == 0, 2, jnp.where(col == 1, 4, jnp.where(col == 2, 8, 16)))
    return jnp.minimum(t + 1, win).astype(f32)


def pool_fwd(z, wbd, scale, *, name):
    M = z.shape[0]
    tm = 256
    nper = SEQ // tm
    hb = tm // POOL_HALO

    def body(zc_ref, zh_ref, w_ref, s_ref, p_ref, y_ref):
        i = pl.program_id(0)
        seq_blk = i % nper
        halo = jnp.where(seq_blk == 0, 0.0, zh_ref[...])
        u = zc_ref[...]
        ext = jnp.concatenate([halo, u], axis=0)
        a1 = ext + pltpu.roll(ext, 1, 0)
        a2 = a1 + pltpu.roll(a1, 2, 0)
        a3 = a2 + pltpu.roll(a2, 4, 0)
        a4 = a3 + pltpu.roll(a3, 8, 0)
        sums = _pool_select(a1, a2, a3, a4)[POOL_HALO:]
        t = seq_blk * tm + lax.broadcasted_iota(jnp.int32, (tm, 1), 0)
        p = (sums / _pool_count(t) - u).astype(bf16)
        p_ref[...] = p
        y_ref[...] = (jnp.dot(p, w_ref[...], preferred_element_type=f32) * s_ref[...]).astype(bf16)

    return pl.pallas_call(
        body, name=name, grid=(M // tm,),
        in_specs=[pl.BlockSpec((tm, MAIN_W), lambda i: (i, 0)),
                  pl.BlockSpec((POOL_HALO, MAIN_W), lambda i: (jnp.maximum(i * hb - 1, 0), 0)),
                  pl.BlockSpec((MAIN_W, MAIN_W), lambda i: (0, 0)),
                  pl.BlockSpec((1, MAIN_W), lambda i: (0, 0))],
        out_specs=[pl.BlockSpec((tm, MAIN_W), lambda i: (i, 0)),
                   pl.BlockSpec((tm, MAIN_W), lambda i: (i, 0))],
        out_shape=[_sds((M, MAIN_W), bf16), _sds((M, D_MODEL), bf16)],
        compiler_params=_params(("parallel",)),
    )(z, z, wbd, scale)


def pool_bwd(dyc, p, wbd, scale, *, name):
    M = p.shape[0]
    tm = 256
    nper = SEQ // tm
    hb = tm // POOL_HALO
    last_hb = M // POOL_HALO - 1

    def body(dy_ref, dyh_ref, p_ref, w_ref, s_ref, dz_ref, dw_ref, ds_ref):
        i = pl.program_id(0)
        seq_blk = i % nper
        dy = dy_ref[...].astype(f32)
        pv = p_ref[...]
        w = w_ref[...]
        sc = s_ref[...]

        @pl.when(i == 0)
        def _():
            dw_ref[...] = jnp.zeros_like(dw_ref)
            ds_ref[...] = jnp.zeros_like(ds_ref)

        v = jnp.dot(pv, w, preferred_element_type=f32)
        ds_ref[...] += jnp.sum(dy * v, axis=0, keepdims=True)
        dv = (dy * sc).astype(bf16)
        dw_ref[...] += lax.dot_general(pv, dv, TN, preferred_element_type=f32)
        dp = lax.dot_general(dv, w, NT, preferred_element_type=f32)
        dvh = jnp.where(seq_blk == nper - 1, 0.0, dyh_ref[...].astype(f32) * sc).astype(bf16)
        dph = lax.dot_general(dvh, w, NT, preferred_element_type=f32)
        ext = jnp.concatenate([dp, dph], axis=0)
        n = tm + POOL_HALO
        t = seq_blk * tm + lax.broadcasted_iota(jnp.int32, (n, 1), 0)
        e = ext / _pool_count(t)
        b1 = e + pltpu.roll(e, n - 1, 0)
        b2 = b1 + pltpu.roll(b1, n - 2, 0)
        b3 = b2 + pltpu.roll(b2, n - 4, 0)
        b4 = b3 + pltpu.roll(b3, n - 8, 0)
        dz_ref[...] = (_pool_select(b1, b2, b3, b4)[:tm] - dp).astype(dz_ref.dtype)

    return pl.pallas_call(
        body, name=name, grid=(M // tm,),
        in_specs=[pl.BlockSpec((tm, MAIN_W), lambda i: (i, 0)),
                  pl.BlockSpec((POOL_HALO, MAIN_W), lambda i: (jnp.minimum((i + 1) * hb, last_hb), 0)),
                  pl.BlockSpec((tm, MAIN_W), lambda i: (i, 0)),
                  pl.BlockSpec((MAIN_W, MAIN_W), lambda i: (0, 0)),
                  pl.BlockSpec((1, MAIN_W), lambda i: (0, 0))],
        out_specs=[pl.BlockSpec((tm, MAIN_W), lambda i: (i, 0)),
                   pl.BlockSpec((MAIN_W, MAIN_W), lambda i: (0, 0)),
                   pl.BlockSpec((1, MAIN_W), lambda i: (0, 0))],
        out_shape=[_sds((M, D_MODEL), bf16), _sds((MAIN_W, MAIN_W), f32), _sds((1, MAIN_W), f32)],
        compiler_params=_params(("arbitrary",)),
    )(dyc, dyc, p, wbd, scale)


def _mem_heads(q, kv):
    first = _first_head()
    for pr in range(N_MEM_HEADS // 2):
        cols = slice(pr * PAIR_W, (pr + 1) * PAIR_W)
        qp = q[:, cols] * SCALE
        kp = kv[:, cols]
        vp = kv[:, MEM_W + pr * PAIR_W: MEM_W + (pr + 1) * PAIR_W]
        for hh in range(2):
            lm = first if hh == 0 else ~first
            qm = jnp.where(lm, qp, 0.0).astype(bf16)
            s = lax.dot_general(qm, kp, NT, preferred_element_type=f32)
            e = jnp.exp(s - jnp.max(s, axis=-1, keepdims=True))
            yield lm, qm, kp, vp, e, jnp.sum(e, axis=-1, keepdims=True)


def memattn_fwd(z, kvm, ycat, *, name, n_seq):
    M = z.shape[0]
    tq = 512
    nq = SEQ // tq

    def body(q_ref, kv_ref, _, o_ref):
        first = _first_head()
        outs = []
        for lm, _, _, vp, e, l in _mem_heads(q_ref[...], kv_ref[...]):
            outs.append(jnp.dot(e.astype(bf16), vp, preferred_element_type=f32) * (1.0 / l))
        pairs = [jnp.where(first, outs[2 * pr], outs[2 * pr + 1]) for pr in range(N_MEM_HEADS // 2)]
        o_ref[...] = jnp.concatenate(pairs, axis=1).astype(bf16)

    return pl.pallas_call(
        body, name=name, grid=(n_seq, nq),
        in_specs=[pl.BlockSpec((tq, MEM_W), lambda b, i: (b * nq + i, 3)),
                  pl.BlockSpec((N_MEM, 2 * MEM_W), lambda b, i: (b, 0)),
                  pl.BlockSpec(memory_space=pl.ANY)],
        out_specs=pl.BlockSpec((tq, MEM_W), lambda b, i: (b * nq + i, 3)),
        out_shape=_sds((M, D_MODEL), bf16),
        input_output_aliases={2: 0},
        compiler_params=_params(("parallel", "parallel")),
    )(z, kvm, ycat)


def memattn_bwd(z, kvm, dyc, dz, *, name, n_seq):
    M = z.shape[0]
    tq = 512
    nq = SEQ // tq

    def body(q_ref, kv_ref, dy_ref, _, dq_ref, dkv_ref):
        first = _first_head()
        dy = dy_ref[...].astype(f32)
        dqs, dks, dvs = [], [], []
        for h, (lm, qm, kp, vp, e, l) in enumerate(_mem_heads(q_ref[...], kv_ref[...])):
            pr = h // 2
            p = e * (1.0 / l)
            dym = jnp.where(lm, dy[:, pr * PAIR_W:(pr + 1) * PAIR_W], 0.0).astype(bf16)
            dp = lax.dot_general(dym, vp, NT, preferred_element_type=f32)
            ds = (p * (dp - jnp.sum(dp * p, axis=-1, keepdims=True))).astype(bf16)
            dqs.append(jnp.dot(ds, kp, preferred_element_type=f32) * SCALE)
            dk = lax.dot_general(ds, qm, TN, preferred_element_type=f32)
            dv = lax.dot_general(p.astype(bf16), dym, TN, preferred_element_type=f32)
            if h % 2 == 0:
                dks.append(dk)
                dvs.append(dv)
            else:
                dks[pr] = dks[pr] + dk
                dvs[pr] = dvs[pr] + dv
        pairs = [jnp.where(first, dqs[2 * pr], dqs[2 * pr + 1]) for pr in range(N_MEM_HEADS // 2)]
        dq_ref[...] = jnp.concatenate(pairs, axis=1).astype(bf16)

        @pl.when(pl.program_id(1) == 0)
        def _():
            dkv_ref[...] = jnp.zeros_like(dkv_ref)

        dkv_ref[...] += jnp.concatenate(dks + dvs, axis=1)

    return pl.pallas_call(
        body, name=name, grid=(n_seq, nq),
        in_specs=[pl.BlockSpec((tq, MEM_W), lambda b, i: (b * nq + i, 3)),
                  pl.BlockSpec((N_MEM, 2 * MEM_W), lambda b, i: (b, 0)),
                  pl.BlockSpec((tq, MEM_W), lambda b, i: (b * nq + i, 3)),
                  pl.BlockSpec(memory_space=pl.ANY)],
        out_specs=[pl.BlockSpec((tq, MEM_W), lambda b, i: (b * nq + i, 3)),
                   pl.BlockSpec((N_MEM, 2 * MEM_W), lambda b, i: (b, 0))],
        out_shape=[_sds((M, D_MODEL), bf16), _sds((n_seq * N_MEM, 2 * MEM_W), f32)],
        input_output_aliases={3: 0},
        compiler_params=_params(("parallel", "arbitrary")),
    )(z, kvm, dyc, dz)


def rope_tables(pos, *, name):
    M = pos.shape[0]
    tm = min(1024, M)
    half = HEAD_DIM // 2
    inv = ROPE_THETA ** (-np.arange(half, dtype=np.float64) / half)
    inv128 = jnp.asarray(np.tile(inv, 4)[None, :], f32)
    sign128 = jnp.asarray(np.tile(np.concatenate([-np.ones(half), np.ones(half)]), 2)[None, :], f32)

    def body(p_ref, f_ref, s_ref, cos_ref, sin_ref):
        ang = p_ref[...] * f_ref[...]
        cos_ref[...] = jnp.cos(ang)
        sin_ref[...] = jnp.sin(ang) * s_ref[...]

    return pl.pallas_call(
        body, name=name, grid=(M // tm,),
        in_specs=[pl.BlockSpec((tm, 1), lambda i: (i, 0)),
                  pl.BlockSpec((1, 128), lambda i: (0, 0)),
                  pl.BlockSpec((1, 128), lambda i: (0, 0))],
        out_specs=[pl.BlockSpec((tm, 128), lambda i: (i, 0)),
                   pl.BlockSpec((tm, 128), lambda i: (i, 0))],
        out_shape=[_sds((M, 128), f32), _sds((M, 128), f32)],
        compiler_params=_params(("parallel",)),
    )(pos, inv128, sign128)


def _swap_halves(x):
    w = x.shape[1]
    first = (lax.broadcasted_iota(jnp.int32, (1, w), 1) % HEAD_DIM) < (HEAD_DIM // 2)
    return jnp.where(first, pltpu.roll(x, w - HEAD_DIM // 2, 1), pltpu.roll(x, HEAD_DIM // 2, 1))


def rope_fwd(src, cos, sin, *, name):
    M = src.shape[0]
    tm = min(512, M)

    def body(x_ref, c_ref, s_ref, o_ref):
        x = x_ref[...].astype(f32)
        c = jnp.tile(c_ref[...], (1, MAIN_W // 128))
        s = jnp.tile(s_ref[...], (1, MAIN_W // 128))
        o_ref[...] = x * c + _swap_halves(x) * s

    return pl.pallas_call(
        body, name=name, grid=(M // tm,),
        in_specs=[pl.BlockSpec((tm, MAIN_W), lambda i: (i, 0)),
                  pl.BlockSpec((tm, 128), lambda i: (i, 0)),
                  pl.BlockSpec((tm, 128), lambda i: (i, 0))],
        out_specs=pl.BlockSpec((tm, MAIN_W), lambda i: (i, 0)),
        out_shape=_sds((M, MAIN_W), f32),
        compiler_params=_params(("parallel",)),
    )(src, cos, sin)


def group_sum(groups, cos, sin, *, name, rotate, width, col_block=0, into=None):
    M = groups[0][0].shape[0]
    tm = min(512, M)
    counts = [len(g) for g in groups]
    flat = [a for g in groups for a in g]
    extra = [] if into is None else [into]

    def body(*refs):
        part_refs = refs[:len(flat)]
        c_ref, s_ref = refs[len(flat):len(flat) + 2]
        o_ref = refs[-1]
        cols, k = [], 0
        for n in counts:
            acc = part_refs[k][...]
            for r in part_refs[k + 1:k + n]:
                acc = acc + r[...]
            cols.append(acc)
            k += n
        d = jnp.concatenate(cols, axis=1)
        if rotate:
            c = jnp.tile(c_ref[...], (1, MAIN_W // 128))
            s = jnp.tile(s_ref[...], (1, MAIN_W // 128))
            d = d * c - _swap_halves(d) * s
        o_ref[...] = d.astype(bf16)

    part = pl.BlockSpec((tm, GROUP_W), lambda i: (i, 0))
    tab = pl.BlockSpec((tm, 128), lambda i: (i, 0))
    return pl.pallas_call(
        body, name=name, grid=(M // tm,),
        in_specs=[part] * len(flat) + [tab, tab] + [pl.BlockSpec(memory_space=pl.ANY)] * len(extra),
        out_specs=pl.BlockSpec((tm, MAIN_W), lambda i: (i, col_block)),
        out_shape=_sds((M, width), bf16),
        input_output_aliases={len(flat) + 2: 0} if extra else {},
        compiler_params=_params(("parallel",)),
    )(*flat, cos, sin, *extra)


PAIR_W = 2 * HEAD_DIM
MIN_BLOCKS = 8


def _dil_geometry(dil):
    nsub = max(dil, MIN_BLOCKS)
    tb = BAND * nsub
    return nsub, tb, SEQ // tb


def _rows(ref, sub, dil):
    if dil == 1:
        return ref[sub * BAND:(sub + 1) * BAND, :]
    nl, r = divmod(sub, dil)
    return ref[pl.ds(nl * BAND * dil + r, BAND, stride=dil), :]


def _store_rows(ref, sub, dil, val):
    if dil == 1:
        ref[sub * BAND:(sub + 1) * BAND, :] = val
    else:
        nl, r = divmod(sub, dil)
        ref[pl.ds(nl * BAND * dil + r, BAND, stride=dil), :] = val


def _keys(prev_ref, own_ref, sub, dil):
    nsub = own_ref.shape[0] // BAND
    if sub >= dil:
        prev = _rows(own_ref, sub - dil, dil)
    elif prev_ref is None:
        return _rows(own_ref, sub, dil)
    else:
        prev = _rows(prev_ref, nsub - dil + sub, dil)
    return jnp.concatenate([prev, _rows(own_ref, sub, dil)], axis=0)


def _band_mask(nkeys, has_prev):
    i = lax.broadcasted_iota(jnp.int32, (BAND, nkeys), 0)
    j = lax.broadcasted_iota(jnp.int32, (BAND, nkeys), 1)
    if nkeys == BAND:
        return j <= i
    return (j >= i) & (j <= i + BAND) & (has_prev | (j >= BAND))


def _first_head():
    return lax.broadcasted_iota(jnp.int32, (1, PAIR_W), 1) < HEAD_DIM


def _col(x, hh):
    return x[:, hh * HEAD_DIM:hh * HEAD_DIM + 1]


def _pair_spec(tb, nblk, col0, which):
    def idx(b, p, i):
        if which < 0:
            i = jnp.maximum(i - 1, 0)
        elif which > 0:
            i = jnp.minimum(i + 1, nblk - 1)
        return (b * nblk + i, col0 + p)
    return pl.BlockSpec((tb, PAIR_W), idx)


def dil_fwd(q, k, kv, g, dil, *, name, n_seq):
    M = q.shape[0]
    nsub, tb, nblk = _dil_geometry(dil)
    with_prev = nblk > 1

    def body(*refs):
        if with_prev:
            q_ref, ko_ref, vo_ref, kp_ref, vp_ref, o_ref, l_ref = refs
        else:
            (q_ref, ko_ref, vo_ref, o_ref, l_ref), kp_ref, vp_ref = refs, None, None
        first = _first_head()
        blk = pl.program_id(2)
        for sub in range(nsub):
            qs = _rows(q_ref, sub, dil) * SCALE
            kc = _keys(kp_ref, ko_ref, sub, dil).astype(bf16)
            vc = _keys(vp_ref, vo_ref, sub, dil).astype(bf16)
            has_prev = True if sub >= dil else blk > 0
            mask = _band_mask(kc.shape[0], has_prev)
            outs, lses = [], []
            for hh in range(2):
                qm = jnp.where(first if hh == 0 else ~first, qs, 0.0).astype(bf16)
                s = jnp.where(mask, lax.dot_general(qm, kc, NT, preferred_element_type=f32), NEG)
                m = jnp.max(s, axis=-1, keepdims=True)
                e = jnp.exp(s - m)
                l = jnp.sum(e, axis=-1, keepdims=True)
                outs.append(jnp.dot(e.astype(bf16), vc, preferred_element_type=f32) * (1.0 / l))
                lses.append(jnp.broadcast_to(m + jnp.log(l), (BAND, PAIR_W)))
            _store_rows(o_ref, sub, dil, jnp.where(first, outs[0], outs[1]))
            _store_rows(l_ref, sub, dil, jnp.where(first, lses[0], lses[1]))

    ins = [(q, 2 * g, 0), (k, 2 * g, 0), (kv, 6 + 2 * g, 0)]
    if with_prev:
        ins += [(k, 2 * g, -1), (kv, 6 + 2 * g, -1)]
    out = _pair_spec(tb, nblk, 0, 0)
    return pl.pallas_call(
        body, name=name, grid=(n_seq, 2, nblk),
        in_specs=[_pair_spec(tb, nblk, c, w) for _, c, w in ins],
        out_specs=[out, out],
        out_shape=[_sds((M, GROUP_W), f32)] * 2,
        compiler_params=_params(("parallel", "parallel", "arbitrary")),
    )(*[a for a, _, _ in ins])


def combine_fwd(os_, lses, *, name):
    M = os_[0].shape[0]
    tm = min(512, M)

    def body(o0, o1, o2, l0, l1, l2, y_ref):
        ls = [l0[...], l1[...], l2[...]]
        m = jnp.maximum(jnp.maximum(ls[0], ls[1]), ls[2])
        es = [jnp.exp(l - m) for l in ls]
        inv = 1.0 / (es[0] + es[1] + es[2])
        y_ref[...] = jnp.concatenate([o[...] * e * inv for o, e in zip((o0, o1, o2), es)], axis=1).astype(bf16)

    part = pl.BlockSpec((tm, GROUP_W), lambda i: (i, 0))
    return pl.pallas_call(
        body, name=name, grid=(M // tm,),
        in_specs=[part] * 6,
        out_specs=pl.BlockSpec((tm, MAIN_W), lambda i: (i, 0)),
        out_shape=_sds((M, D_MODEL), bf16),
        compiler_params=_params(("parallel",)),
    )(*os_, *lses)


def combine_bwd(dyc, os_, lses, *, name):
    M = os_[0].shape[0]
    tm = min(512, M)

    def body(dy_ref, o0, o1, o2, l0, l1, l2, d0, d1, d2, c0, c1, c2):
        r = lax.broadcasted_iota(jnp.int32, (GROUP_W, GROUP_W), 0) // HEAD_DIM
        c = lax.broadcasted_iota(jnp.int32, (GROUP_W, GROUP_W), 1) // HEAD_DIM
        ones = (r == c).astype(f32)
        dy = dy_ref[...].astype(f32)
        ls = [l0[...], l1[...], l2[...]]
        m = jnp.maximum(jnp.maximum(ls[0], ls[1]), ls[2])
        es = [jnp.exp(l - m) for l in ls]
        inv = 1.0 / (es[0] + es[1] + es[2])
        total = 0.0
        alphas = []
        for g, (o, e, d_ref) in enumerate(zip((o0, o1, o2), es, (d0, d1, d2))):
            a = e * inv
            dyg = dy[:, g * GROUP_W:(g + 1) * GROUP_W]
            d_ref[...] = dyg * a
            dsum = jnp.dot(dyg * o[...], ones, precision=lax.Precision.HIGHEST, preferred_element_type=f32)
            total = total + a * dsum
            alphas.append(a)
        for a, c_ref in zip(alphas, (c0, c1, c2)):
            c_ref[...] = -a * total

    part = pl.BlockSpec((tm, GROUP_W), lambda i: (i, 0))
    outs = pl.pallas_call(
        body, name=name, grid=(M // tm,),
        in_specs=[pl.BlockSpec((tm, MAIN_W), lambda i: (i, 0))] + [part] * 6,
        out_specs=[part] * 6,
        out_shape=[_sds((M, GROUP_W), f32)] * 6,
        compiler_params=_params(("parallel",)),
    )(dyc, *os_, *lses)
    return outs[:3], outs[3:]


def dil_bwd(q, k, kv, do, cc, lse, g, dil, *, name, n_seq):
    M = q.shape[0]
    nsub = SEQ // BAND
    per_res = nsub // dil

    def body(q_ref, k_ref, v_ref, do_ref, c_ref, l_ref, dq_ref, dk_ref, dv_ref):
        first = _first_head()
        for r in range(dil):
            carry = None
            for nl in range(per_res):
                sub = nl * dil + r
                qs = _rows(q_ref, sub, dil) * SCALE
                dos = _rows(do_ref, sub, dil)
                cs = _rows(c_ref, sub, dil)
                ls = _rows(l_ref, sub, dil)
                kc = _keys(None, k_ref, sub, dil).astype(bf16)
                vc = _keys(None, v_ref, sub, dil).astype(bf16)
                nkeys = kc.shape[0]
                mask = _band_mask(nkeys, True)
                dqs = []
                dkc = jnp.zeros((nkeys, PAIR_W), f32)
                dvc = jnp.zeros((nkeys, PAIR_W), f32)
                for hh in range(2):
                    lm = first if hh == 0 else ~first
                    qm = jnp.where(lm, qs, 0.0).astype(bf16)
                    dom = jnp.where(lm, dos, 0.0).astype(bf16)
                    s = jnp.where(mask, lax.dot_general(qm, kc, NT, preferred_element_type=f32), NEG)
                    p = jnp.exp(s - _col(ls, hh))
                    dp = lax.dot_general(dom, vc, NT, preferred_element_type=f32)
                    ds = (p * (dp + _col(cs, hh))).astype(bf16)
                    dqs.append(jnp.dot(ds, kc, preferred_element_type=f32) * SCALE)
                    dkc = dkc + lax.dot_general(ds, qm, TN, preferred_element_type=f32)
                    dvc = dvc + lax.dot_general(p.astype(bf16), dom, TN, preferred_element_type=f32)
                _store_rows(dq_ref, sub, dil, jnp.where(first, dqs[0], dqs[1]))
                if nkeys == 2 * BAND:
                    _store_rows(dk_ref, sub - dil, dil, carry[0] + dkc[:BAND])
                    _store_rows(dv_ref, sub - dil, dil, carry[1] + dvc[:BAND])
                    carry = (dkc[BAND:], dvc[BAND:])
                else:
                    carry = (dkc, dvc)
            _store_rows(dk_ref, (per_res - 1) * dil + r, dil, carry[0])
            _store_rows(dv_ref, (per_res - 1) * dil + r, dil, carry[1])

    def spec(col0):
        return pl.BlockSpec((SEQ, PAIR_W), lambda b, p: (b, col0 + p))

    out = spec(0)
    return pl.pallas_call(
        body, name=name, grid=(n_seq, 2),
        in_specs=[spec(2 * g), spec(2 * g), spec(6 + 2 * g), spec(0), spec(0), spec(0)],
        out_specs=[out, out, out],
        out_shape=[_sds((M, GROUP_W), f32)] * 3,
        compiler_params=_params(("parallel", "parallel")),
    )(q, k, kv, do, cc, lse)


def _blockdiag(wp):
    out = jnp.zeros((MAIN_W, MAIN_W), wp.dtype)
    for gi in range(len(POOL_WINDOWS)):
        sl = slice(gi * POOL_GROUP, (gi + 1) * POOL_GROUP)
        out = out.at[sl, sl].set(wp[gi])
    return out


def _unblockdiag(w):
    return jnp.stack([w[gi * POOL_GROUP:(gi + 1) * POOL_GROUP, gi * POOL_GROUP:(gi + 1) * POOL_GROUP]
                      for gi in range(len(POOL_WINDOWS))])


def local_step(x, mem, positions, target, P, layer_weights, kv_weight, emit_grads):
    n_seq = x.shape[0]
    M = n_seq * SEQ
    xs = x.reshape(M, D_MODEL)
    mems = mem.reshape(n_seq * N_MEM, D_MODEL)
    pos = positions.reshape(M, 1).astype(f32)
    cos, sin = rope_tables(pos, name="rope_tables")
    gains = P["norm_gains"]

    def gain(l, k):
        return gains[l, k].reshape(1, D_MODEL)

    saved = []
    kvs = None
    for l in range(DEPTH):
        W, started = layer_weights(l, "mix", xs)
        sv = {"x": xs, "W": W}
        z, h1 = rms_matmul(xs, gain(l, 0), W["w_in"], name=f"l{l}_in", out_dtype=f32, after=started)
        kvm, mn = rms_matmul(mems, P["mem_norm"][l].reshape(1, D_MODEL), W["w_mem_kv"],
                             name=f"l{l}_memkv", out_dtype=bf16)
        sv.update(z=z, h1=h1, kvm=kvm, mn=mn)
        if l < N_A_LAYERS:
            wbd = _blockdiag(P["w_pool"][l].astype(bf16))
            psc = P["pool_scale"][l].reshape(1, MAIN_W)
            p, y_main = pool_fwd(z, wbd, psc, name=f"l{l}_pool")
            sv.update(p=p, wbd=wbd, psc=psc)
        else:
            qrot = rope_fwd(z, cos, sin, name=f"l{l}_ropeq")
            os_, lses = [], []
            for g, (_, dil) in enumerate(DIL_PATTERNS):
                o, lse = dil_fwd(qrot, kvs["krot"], kvs["kv"], g, dil, name=f"l{l}_dil{g}", n_seq=n_seq)
                os_.append(o)
                lses.append(lse)
            y_main = combine_fwd(os_, lses, name=f"l{l}_comb")
            sv.update(qrot=qrot, os=os_, lses=lses)
        ycat = memattn_fwd(z, kvm, y_main, name=f"l{l}_memattn", n_seq=n_seq)
        y, x1 = matmul_rms_res(ycat, W["w_out"], gain(l, 1), xs, name=f"l{l}_out")
        W.update(layer_weights(l, "gu", x1)[0])
        fg, fu, a, h2 = rms_gate_up(x1, gain(l, 2), W["w_gate_up"], name=f"l{l}_gu")
        W.update(layer_weights(l, "down", a)[0])
        y2, x2 = matmul_rms_res(a, W["w_down"], gain(l, 3), x1, name=f"l{l}_down")
        sv.update(ycat=ycat, y=y, x1=x1, fg=fg, fu=fu, h2=h2, a=a, y2=y2)
        saved.append(sv)
        xs = x2
        if l == N_A_LAYERS - 1:
            w_kv = kv_weight(xs)
            kv, hkv = rms_matmul(xs, P["kv_norm"].reshape(1, D_MODEL), w_kv, name="kv_proj", out_dtype=f32,
                                 transposed=True)
            krot = rope_fwd(kv, cos, sin, name="ropek")
            kvs = {"kv": kv, "hkv": hkv, "krot": krot, "x": xs, "w_kv": w_kv}

    dx, sq = loss_head(xs, target.reshape(M, D_MODEL), name="loss_head")

    G = {"mem_norm": [None] * DEPTH, "norm_gains": [[None] * 4 for _ in range(DEPTH)],
         "pool_scale": [None] * N_A_LAYERS}
    dk_parts = [[] for _ in range(N_GROUPS)]
    dv_parts = [[] for _ in range(N_GROUPS)]
    emitted = None

    for l in reversed(range(DEPTH)):
        sv = saved[l]
        W = sv["W"]
        gw = {}
        dy2, dgu, G["norm_gains"][l][3] = down_bwd(sv["y2"], gain(l, 3), dx, W["w_down"], sv["fg"], sv["fu"],
                                                   name=f"l{l}_b_dgu", after=emitted)
        gw["w_down"] = matmul(sv["a"], dy2, TN, name=f"l{l}_b_wd", out_dtype=bf16)
        dx1, G["norm_gains"][l][2] = matmul_rms_bwd(dgu, W["w_gate_up"], NN, sv["x1"], gain(l, 2), dx,
                                                    name=f"l{l}_b_dh2")
        gw["w_gate_up"] = matmul(dgu, sv["h2"], TN, name=f"l{l}_b_wgu", out_dtype=bf16)
        emitted = emit_grads(l, "ffn", gw)
        gw = {}
        dy, dycat, G["norm_gains"][l][1] = rms_bwd_matmul(sv["y"], gain(l, 1), dx1, W["w_out"], NT,
                                                          name=f"l{l}_b_dycat", after=emitted)
        gw["w_out"] = matmul(sv["ycat"], dy, TN, name=f"l{l}_b_wout", out_dtype=bf16)
        if l < N_A_LAYERS:
            dz, dwbd, dps = pool_bwd(dycat, sv["p"], sv["wbd"], sv["psc"], name=f"l{l}_b_pool")
            gw["w_pool"] = _unblockdiag(dwbd).reshape(MAIN_W, POOL_GROUP).astype(bf16)
            G["pool_scale"][l] = dps.reshape(MAIN_W)
        else:
            dos, ccs = combine_bwd(dycat, sv["os"], sv["lses"], name=f"l{l}_b_comb")
            dqs = []
            for g, (_, dil) in enumerate(DIL_PATTERNS):
                args = (sv["qrot"], kvs["krot"], kvs["kv"], dos[g], ccs[g], sv["lses"][g], g, dil)
                dq, dk, dv = dil_bwd(*args, name=f"l{l}_b_dil{g}", n_seq=n_seq)
                dqs.append([dq])
                dk_parts[g].append(dk)
                dv_parts[g].append(dv)
            dz = group_sum(dqs, cos, sin, name=f"l{l}_b_ropeq", rotate=True, width=D_MODEL)
        dz, dkvm = memattn_bwd(sv["z"], sv["kvm"], dycat, dz, name=f"l{l}_b_memattn", n_seq=n_seq)
        dmn = matmul(dkvm, W["w_mem_kv"], NT, name=f"l{l}_b_dmn", out_dtype=bf16)
        gw["w_mem_kv"] = matmul(sv["mn"], dkvm, TN, name=f"l{l}_b_wmkv", out_dtype=bf16)
        _, G["mem_norm"][l] = rms_bwd(mems, P["mem_norm"][l].reshape(1, D_MODEL), dmn, None,
                                      name=f"l{l}_b_nmem", out_dtype=bf16)
        gw["w_in"] = matmul(sv["h1"], dz, TN, name=f"l{l}_b_win", out_dtype=bf16)
        dx, G["norm_gains"][l][0] = matmul_rms_bwd(dz, W["w_in"], NT, sv["x"], gain(l, 0), dx1, name=f"l{l}_b_dh1")
        if l == N_A_LAYERS:
            dkv = group_sum(dk_parts, cos, sin, name="b_ropek", rotate=True, width=2 * MAIN_W)
            dkv = group_sum(dv_parts, cos, sin, name="b_sumv", rotate=False, width=2 * MAIN_W, col_block=1, into=dkv)
            gw["w_kv"] = matmul(dkv, kvs["hkv"], TN, name="b_wkv", out_dtype=bf16)
            dx, gkn = matmul_rms_bwd(dkv, kvs["w_kv"], NN, kvs["x"], P["kv_norm"].reshape(1, D_MODEL), dx,
                                     name="b_dhkv")
            G["kv_norm"] = gkn.reshape(D_MODEL)
        emitted = emit_grads(l, "mix", gw)

    small = {"pool_scale": jnp.stack(G["pool_scale"]),
             "mem_norm": jnp.concatenate(G["mem_norm"], axis=0),
             "norm_gains": jnp.stack([jnp.concatenate(r, axis=0) for r in G["norm_gains"]]),
             "kv_norm": G["kv_norm"]}
    return sq[0, 0], dx.reshape(n_seq, SEQ, D_MODEL), small, emitted


def _peer(k):
    x, y, c = lax.axis_index("x"), lax.axis_index("y"), lax.axis_index("c")
    px = 1 - x if k & 4 else x
    py = 1 - y if k & 2 else y
    pc = 1 - c if k & 1 else c
    return (px, py, pc), 4 * px + 2 * py + pc


def _my_index():
    return 4 * lax.axis_index("x") + 2 * lax.axis_index("y") + lax.axis_index("c")


def _src_for(kinds, in_refs, i, idx):
    return in_refs[i] if kinds[i] == "gather" else in_refs[i].at[idx]


def _local_copies(kinds, in_refs, out_refs, local_sems):
    me = _my_index()
    return [pltpu.make_async_copy(_src_for(kinds, in_refs, i, me), out_refs[i].at[me], local_sems.at[i])
            for i in range(len(kinds))]


def _remote_copies(kinds, in_refs, out_refs, send_sems, recv_sems, *, arriving):
    me = _my_index()
    copies = []
    for k in range(1, N_DEV):
        dev, idx = _peer(k)
        for i in range(len(kinds)):
            j = i * (N_DEV - 1) + k - 1
            copies.append(pltpu.make_async_remote_copy(
                src_ref=_src_for(kinds, in_refs, i, idx), dst_ref=out_refs[i].at[idx if arriving else me],
                send_sem=send_sems.at[j], recv_sem=recv_sems.at[j], device_id=dev, device_id_type=MESH))
    return copies


def _out_shape(a, kind):
    return ((N_DEV,) + a.shape) if kind == "gather" else a.shape


def exchange(items, *, name, after=()):
    n = len(items)
    kinds = [k for _, k in items]
    after = list(after)

    def body(*refs):
        in_refs, out_refs = refs[:n], refs[n + len(after):2 * n + len(after)]
        send_sems, recv_sems, local_sems = refs[-3:]
        local = _local_copies(kinds, in_refs, out_refs, local_sems)
        sends = _remote_copies(kinds, in_refs, out_refs, send_sems, recv_sems, arriving=False)
        for cp in local + sends:
            cp.start()
        for cp in _remote_copies(kinds, in_refs, out_refs, send_sems, recv_sems, arriving=True):
            cp.wait_recv()
        for cp in sends:
            cp.wait_send()
        for cp in local:
            cp.wait()

    any_spec = pl.BlockSpec(memory_space=pl.ANY)
    return pl.pallas_call(
        body, name=name,
        in_specs=[any_spec] * (n + len(after)), out_specs=[any_spec] * n,
        out_shape=[_sds(_out_shape(a, k), a.dtype) for a, k in items],
        scratch_shapes=[pltpu.SemaphoreType.DMA((n * (N_DEV - 1),)), pltpu.SemaphoreType.DMA((n * (N_DEV - 1),)),
                        pltpu.SemaphoreType.DMA((n,))],
    )(*[a for a, _ in items], *after)


_HBM = pl.BlockSpec(memory_space=pltpu.HBM)
_SEM = pl.BlockSpec(memory_space=pltpu.SEMAPHORE)
_EFFECT = pltpu.SideEffectType.DATAFLOW_SIDE_EFFECTING


def exchange_start(items, after, *, name):
    n = len(items)
    kinds = [k for _, k in items]

    def body(*refs):
        in_refs, land_refs = refs[:n], refs[n:2 * n]
        send_sems, recv_sems, local_sems = refs[2 * n + 1:2 * n + 4]
        token = refs[-1]
        for cp in (_local_copies(kinds, in_refs, land_refs, local_sems)
                   + _remote_copies(kinds, in_refs, land_refs, send_sems, recv_sems, arriving=False)):
            cp.start()
        token[...] = jnp.zeros_like(token)

    srcs = [pltpu.with_memory_space_constraint(a, pltpu.HBM) for a, _ in items]
    lands = [pltpu.with_memory_space_constraint(lax.empty(_out_shape(a, k), a.dtype), pltpu.HBM) for a, k in items]
    outs = pl.pallas_call(
        body, name=name,
        out_shape=(pltpu.SemaphoreType.DMA((n * (N_DEV - 1),)), pltpu.SemaphoreType.DMA((n * (N_DEV - 1),)),
                   pltpu.SemaphoreType.DMA((n,)),
                   *[pltpu.HBM(a.shape, a.dtype) for a in srcs], *[pltpu.HBM(a.shape, a.dtype) for a in lands],
                   _sds((8, 128), f32)),
        in_specs=[_HBM] * (2 * n) + [pl.BlockSpec(memory_space=pl.ANY)],
        out_specs=(_SEM, _SEM, _SEM, *[_HBM] * (2 * n), pl.BlockSpec(memory_space=pltpu.VMEM)),
        input_output_aliases={i: 3 + i for i in range(2 * n)},
        compiler_params=pltpu.CompilerParams(has_side_effects=_EFFECT),
    )(*srcs, *lands, after)
    return {"kinds": kinds, "sems": outs[:3], "srcs": outs[3:3 + n], "lands": outs[3 + n:3 + 2 * n], "token": outs[-1]}


def exchange_wait(handle, after, *, name):
    kinds = handle["kinds"]
    n = len(kinds)

    def body(*refs):
        in_refs, land_refs = refs[:n], refs[n:2 * n]
        send_sems, recv_sems, local_sems = refs[2 * n:2 * n + 3]
        for cp in _remote_copies(kinds, in_refs, land_refs, send_sems, recv_sems, arriving=True):
            cp.wait_recv()
        for cp in _remote_copies(kinds, in_refs, land_refs, send_sems, recv_sems, arriving=False):
            cp.wait_send()
        for cp in _local_copies(kinds, in_refs, land_refs, local_sems):
            cp.wait()

    srcs, lands = list(handle["srcs"]), list(handle["lands"])
    after = list(after) if isinstance(after, (list, tuple)) else [after]
    outs = pl.pallas_call(
        body, name=name,
        out_shape=tuple(pltpu.HBM(a.shape, a.dtype) for a in srcs + lands),
        in_specs=[_HBM] * (2 * n) + [_SEM] * 3 + [pl.BlockSpec(memory_space=pl.ANY)] * len(after),
        out_specs=tuple([_HBM] * (2 * n)),
        input_output_aliases={i: i for i in range(2 * n)},
        compiler_params=pltpu.CompilerParams(has_side_effects=_EFFECT),
    )(*srcs, *lands, *handle["sems"], *after)
    return list(outs[n:])


CHIP_MASKS = (2, 4, 6)


def _g2_first(in_refs, land_refs, send_sems, recv_sems, *, masks, arriving):
    me = _my_index()
    copies = []
    for i in range(len(land_refs)):
        for j, k in enumerate(masks):
            dev, idx = _peer(k)
            dst = land_refs[i].at[idx if arriving else me]
            copies.append(pltpu.make_async_remote_copy(
                src_ref=dst if in_refs is None else in_refs[i], dst_ref=dst,
                send_sem=send_sems.at[i * len(masks) + j], recv_sem=recv_sems.at[i * len(masks) + j],
                device_id=dev, device_id_type=MESH))
    return copies


def _g2_forward(land_refs, fwd_send, fwd_recv, *, arriving):
    sibling, _ = _peer(1)
    copies = []
    for i in range(len(land_refs)):
        for j, k in enumerate(CHIP_MASKS):
            _, idx = _peer(k | 1 if arriving else k)
            copies.append(pltpu.make_async_remote_copy(
                src_ref=land_refs[i].at[idx], dst_ref=land_refs[i].at[idx],
                send_sem=fwd_send.at[i * 3 + j], recv_sem=fwd_recv.at[i * 3 + j], device_id=sibling,
                device_id_type=MESH))
    return copies


def gather2_start(arrays, after, *, name):
    n = len(arrays)

    def body(*refs):
        in_refs, land_refs = refs[:n], refs[n:2 * n]
        ici_send, ici_recv, d2d_send, d2d_recv, local_sems = refs[2 * n + 1:2 * n + 6]
        token = refs[-1]
        ici = _g2_first(in_refs, land_refs, ici_send, ici_recv, masks=CHIP_MASKS, arriving=False)
        d2d = _g2_first(in_refs, land_refs, d2d_send, d2d_recv, masks=(1,), arriving=False)
        for cp in _local_copies(["gather"] * n, in_refs, land_refs, local_sems) + ici + d2d:
            cp.start()
        token[...] = jnp.zeros_like(token)

    srcs = [pltpu.with_memory_space_constraint(a, pltpu.HBM) for a in arrays]
    lands = [pltpu.with_memory_space_constraint(lax.empty((N_DEV,) + a.shape, a.dtype), pltpu.HBM) for a in arrays]
    sem = pltpu.SemaphoreType.DMA
    outs = pl.pallas_call(
        body, name=name,
        out_shape=(sem((3 * n,)), sem((3 * n,)), sem((n,)), sem((n,)), sem((n,)),
                   *[pltpu.HBM(a.shape, a.dtype) for a in srcs], *[pltpu.HBM(a.shape, a.dtype) for a in lands],
                   _sds((8, 128), f32)),
        in_specs=[_HBM] * (2 * n) + [pl.BlockSpec(memory_space=pl.ANY)],
        out_specs=(*[_SEM] * 5, *[_HBM] * (2 * n), pl.BlockSpec(memory_space=pltpu.VMEM)),
        input_output_aliases={i: 5 + i for i in range(2 * n)},
        compiler_params=pltpu.CompilerParams(has_side_effects=_EFFECT),
    )(*srcs, *lands, after)
    return {"n": n, "sems": outs[:5], "srcs": outs[5:5 + n], "lands": outs[5 + n:5 + 2 * n], "token": outs[-1]}


def gather2_forward(handle, after, *, name):
    n = handle["n"]

    def body(*refs):
        land_refs = refs[:n]
        ici_recv = refs[n]
        fwd_send, fwd_recv = refs[n + 2:n + 4]
        for cp in _g2_first(None, land_refs, fwd_send, ici_recv, masks=CHIP_MASKS, arriving=True):
            cp.wait_recv()
        for cp in _g2_forward(land_refs, fwd_send, fwd_recv, arriving=False):
            cp.start()

    lands = list(handle["lands"])
    sem = pltpu.SemaphoreType.DMA
    outs = pl.pallas_call(
        body, name=name,
        out_shape=(sem((3 * n,)), sem((3 * n,)), *[pltpu.HBM(a.shape, a.dtype) for a in lands]),
        in_specs=[_HBM] * n + [_SEM, pl.BlockSpec(memory_space=pl.ANY)],
        out_specs=(_SEM, _SEM, *[_HBM] * n),
        input_output_aliases={i: 2 + i for i in range(n)},
        compiler_params=pltpu.CompilerParams(has_side_effects=_EFFECT),
    )(*lands, handle["sems"][1], after)
    return dict(handle, fwd=outs[:2], lands=outs[2:])


def gather2_wait(handle, after, *, name):
    n = handle["n"]

    def body(*refs):
        in_refs, land_refs = refs[:n], refs[n:2 * n]
        ici_send, d2d_send, d2d_recv, local_sems, fwd_send, fwd_recv = refs[2 * n:2 * n + 6]
        for cp in _g2_first(in_refs, land_refs, d2d_send, d2d_recv, masks=(1,), arriving=True):
            cp.wait_recv()
        for cp in _g2_forward(land_refs, fwd_send, fwd_recv, arriving=True):
            cp.wait_recv()
        for cp in (_g2_first(in_refs, land_refs, ici_send, fwd_recv, masks=CHIP_MASKS, arriving=False)
                   + _g2_first(in_refs, land_refs, d2d_send, d2d_recv, masks=(1,), arriving=False)
                   + _g2_forward(land_refs, fwd_send, fwd_recv, arriving=False)):
            cp.wait_send()
        for cp in _local_copies(["gather"] * n, in_refs, land_refs, local_sems):
            cp.wait()

    srcs, lands = list(handle["srcs"]), list(handle["lands"])
    s = handle["sems"]
    outs = pl.pallas_call(
        body, name=name,
        out_shape=tuple(pltpu.HBM(a.shape, a.dtype) for a in srcs + lands),
        in_specs=[_HBM] * (2 * n) + [_SEM] * 6 + [pl.BlockSpec(memory_space=pl.ANY)],
        out_specs=tuple([_HBM] * (2 * n)),
        input_output_aliases={i: i for i in range(2 * n)},
        compiler_params=pltpu.CompilerParams(has_side_effects=_EFFECT),
    )(*srcs, *lands, s[0], s[2], s[3], s[4], *handle["fwd"], after)
    return list(outs[n:])


def adamw(slots, w, m, v, *, name, layer=None, into=None):
    R, C = w.shape[-2:]
    tr = _tile(R, (256, 128, 64, 32, 16, 8))
    c1 = 1.0 - ADAM_B1 ** ADAM_STEP
    c2 = 1.0 - ADAM_B2 ** ADAM_STEP
    extra = [] if into is None else list(into)

    def body(s_ref, w_ref, m_ref, v_ref, *refs):
        g_ref, d_ref, m2_ref, v2_ref = refs[len(extra):]
        g = s_ref[0].astype(f32)
        for d in range(1, N_DEV):
            g = g + s_ref[d].astype(f32)
        m2 = ADAM_B1 * m_ref[...] + (1.0 - ADAM_B1) * g
        v2 = ADAM_B2 * v_ref[...] + (1.0 - ADAM_B2) * (g * g)
        g_ref[...] = g
        m2_ref[...] = m2
        v2_ref[...] = v2
        d_ref[...] = -ADAM_LR * ((m2 / c1) / (jnp.sqrt(v2 / c2) + ADAM_EPS) + ADAM_WD * w_ref[...])

    if layer is None:
        blk = pl.BlockSpec((tr, C), lambda i: (i, 0))
    else:
        blk = pl.BlockSpec((None, tr, C), lambda i: (layer, i, 0))
    return pl.pallas_call(
        body, name=name, grid=(R // tr,),
        in_specs=[pl.BlockSpec((N_DEV, tr, C), lambda i: (0, i, 0)), blk, blk, blk]
        + [pl.BlockSpec(memory_space=pl.ANY)] * len(extra),
        out_specs=[blk] * 4,
        out_shape=[_sds(w.shape, f32)] * 4,
        input_output_aliases={4 + j: j for j in range(len(extra))},
        compiler_params=_params(("parallel",)),
    )(slots, w, m, v, *extra)


WEIGHTS = ("norm_gains", "mem_norm", "w_in", "w_mem_kv", "w_out", "w_pool", "pool_scale", "kv_norm", "w_kv",
           "w_gate_up", "w_down")
LAYER_MATS = ("w_in", "w_mem_kv", "w_out", "w_gate_up", "w_down")
POOL_SHARD = MAIN_W // N_DEV
KV_SHARD = 2 * MAIN_W // N_DEV
LOOKAHEAD = 2
TWO_LEVEL_LAYERS = (0, 1)


def _pack_small(gains, pscale):
    lead = gains.shape[:-3]
    g = gains.reshape(lead + (16, 128))
    p = jnp.zeros(lead + (8, 128), f32).at[..., :2, :POOL_SHARD].set(pscale)
    return jnp.concatenate([g, p], axis=-2)


def _unpack_small(a):
    return a[:16].reshape(4, 4, 128), a[16:18, :POOL_SHARD]


def _pack_repl(mem_norm, kv_norm):
    return jnp.concatenate([mem_norm, kv_norm.reshape(1, D_MODEL), jnp.zeros((3, D_MODEL), f32)], axis=0)


def _unpack_repl(a):
    return a[:4], a[4]


def kernel(x, mem, positions, norm_gains, mem_norm, w_in, w_mem_kv, w_out, w_pool, pool_scale, kv_norm, w_kv, w_gate_up, w_down, loss_target, m_norm_gains, m_mem_norm, m_w_in, m_w_mem_kv, m_w_out, m_w_pool, m_pool_scale, m_kv_norm, m_w_kv, m_w_gate_up, m_w_down, v_norm_gains, v_mem_norm, v_w_in, v_w_mem_kv, v_w_out, v_w_pool, v_pool_scale, v_kv_norm, v_w_kv, v_w_gate_up, v_w_down):
    w = dict(norm_gains=norm_gains, mem_norm=mem_norm, w_in=w_in, w_mem_kv=w_mem_kv, w_out=w_out, w_pool=w_pool,
             pool_scale=pool_scale, kv_norm=kv_norm, w_kv=w_kv, w_gate_up=w_gate_up, w_down=w_down)
    m = dict(norm_gains=m_norm_gains, mem_norm=m_mem_norm, w_in=m_w_in, w_mem_kv=m_w_mem_kv, w_out=m_w_out,
             w_pool=m_w_pool, pool_scale=m_pool_scale, kv_norm=m_kv_norm, w_kv=m_w_kv, w_gate_up=m_w_gate_up,
             w_down=m_w_down)
    v = dict(norm_gains=v_norm_gains, mem_norm=v_mem_norm, w_in=v_w_in, w_mem_kv=v_w_mem_kv, w_out=v_w_out,
             w_pool=v_w_pool, pool_scale=v_pool_scale, kv_norm=v_kv_norm, w_kv=v_w_kv, w_gate_up=v_w_gate_up,
             w_down=v_w_down)

    def transposed_view(d):
        d = dict(d)
        d["w_gate_up"] = jnp.swapaxes(d["w_gate_up"], 1, 2)
        d["w_kv"] = jnp.swapaxes(d["w_kv"], 0, 1)
        return d

    wv, mv, vv = transposed_view(w), transposed_view(m), transposed_view(v)

    small = _pack_small(norm_gains, pool_scale)
    (gsmall,) = exchange([(small, "gather")], name="gather_small")
    P = {"norm_gains": jnp.moveaxis(gsmall[:, :16].reshape(N_DEV, 4, 4, 128), 0, 2).reshape(4, 4, D_MODEL),
         "pool_scale": jnp.moveaxis(gsmall[:, 16:18, :POOL_SHARD], 0, 1).reshape(2, MAIN_W),
         "mem_norm": mem_norm, "kv_norm": kv_norm, "w_pool": w_pool}

    PARTS = {"mix": ("w_in", "w_mem_kv", "w_out"), "ffn": ("w_gate_up", "w_down"), "gu": ("w_gate_up",),
             "down": ("w_down",)}

    def parts_of(l):
        return ("mix", "gu", "down") if l == 0 else ("mix", "ffn")

    def part_items(l, part):
        items = [(wv[k][l].astype(bf16), "gather") for k in PARTS[part]]
        if part == "ffn" and l == N_A_LAYERS - 1:
            items.append((wv["w_kv"].astype(bf16), "gather"))
        return items

    handles = {}

    def start_layer(l, after):
        for part in parts_of(l):
            if l in TWO_LEVEL_LAYERS:
                handles[l, part] = gather2_start([a for a, _ in part_items(l, part)], after,
                                                 name=f"gather_start_{part}_l{l}")
            else:
                handles[l, part] = exchange_start(part_items(l, part), after, name=f"gather_start_{part}_l{l}")
            after = handles[l, part]["token"]
        return after

    token = gsmall
    for l in range(LOOKAHEAD):
        token = start_layer(l, token)
    landed = {}

    def layer_weights(l, part, after):
        if part not in parts_of(l):
            if part == "down":
                return {}, None
            part = "ffn"
        if l == 0 and part == "mix":
            after = token
        if l in TWO_LEVEL_LAYERS:
            passed = gather2_forward(handles[l, part], after, name=f"gather_forward_{part}_l{l}")
            got = gather2_wait(passed, after, name=f"gather_wait_{part}_l{l}")
        else:
            got = exchange_wait(handles[l, part], after, name=f"gather_wait_{part}_l{l}")
        landed[l, part] = got
        started = None
        if part == "mix" and l + LOOKAHEAD < DEPTH:
            started = start_layer(l + LOOKAHEAD, got[0])
        W = {k: g.reshape(-1, g.shape[-1]) for k, g in zip(PARTS[part], got)}
        return W, started

    def kv_weight(after):
        g = landed[N_A_LAYERS - 1, "ffn"][len(PARTS["ffn"])]
        return g.reshape(2 * MAIN_W, D_MODEL)

    ghandles = {}

    def emit_grads(l, part, gw):
        items = [(gw[k].reshape((N_DEV, -1) + gw[k].shape[-1:]), "scatter") for k in PARTS[part]]
        if part == "mix" and l == N_A_LAYERS:
            items.append((gw["w_kv"].reshape(N_DEV, KV_SHARD, D_MODEL), "scatter"))
        if part == "mix" and l < N_A_LAYERS:
            items.append((gw["w_pool"], "gather"))
        ghandles[l, part] = exchange_start(items, gsmall, name=f"scatter_start_{part}_l{l}")
        return ghandles[l, part]["token"]

    sq, grad_x, GS, emitted = local_step(x, mem, positions, loss_target, P, layer_weights, kv_weight, emit_grads)
    loss = lax.psum(0.5 * sq / D_MODEL, ("x", "y", "c"))

    def pool3(a):
        return a.reshape(N_A_LAYERS, MAIN_W, POOL_GROUP)

    out = {}
    after = [emitted]

    def finish_layer(l, after):
        for part in ("ffn", "mix"):
            got = exchange_wait(ghandles[l, part], after, name=f"scatter_wait_{part}_l{l}")
            after = []
            for k, slots in zip(PARTS[part], got):
                out[k] = adamw(slots, wv[k], mv[k], vv[k], name=f"adamw_{k}_l{l}", layer=l, into=out.get(k))
                after.append(out[k][0])
            if part == "mix" and l == N_A_LAYERS:
                out["w_kv"] = adamw(got[-1], wv["w_kv"], mv["w_kv"], vv["w_kv"], name="adamw_w_kv")
                after.append(out["w_kv"][0])
            if part == "mix" and l < N_A_LAYERS:
                out["w_pool"] = adamw(got[-1], pool3(w_pool), pool3(m_w_pool), pool3(v_w_pool), name=f"adamw_w_pool_l{l}",
                                      layer=l, into=out.get("w_pool"))
                after.append(out["w_pool"][0])
        return after

    for l in reversed(range(1, DEPTH)):
        after = finish_layer(l, after)

    gs = _pack_small(jnp.moveaxis(GS["norm_gains"].reshape(4, 4, N_DEV, 128), 2, 0),
                     jnp.moveaxis(GS["pool_scale"].reshape(2, N_DEV, POOL_SHARD), 1, 0))
    parts_small, parts_repl = exchange(
        [(gs, "scatter"), (_pack_repl(GS["mem_norm"], GS["kv_norm"]), "gather")],
        name="exchange_small_grads", after=after)
    finish_layer(0, [parts_small])
    out["w_gate_up"] = [jnp.swapaxes(r, 1, 2) for r in out["w_gate_up"]]
    out["w_kv"] = [jnp.swapaxes(r, 0, 1) for r in out["w_kv"]]
    out["w_pool"] = [r.reshape(w_pool.shape) for r in out["w_pool"]]

    res = adamw(parts_small, small, _pack_small(m_norm_gains, m_pool_scale), _pack_small(v_norm_gains, v_pool_scale),
                name="adamw_small")
    out["norm_gains"], out["pool_scale"] = zip(*[_unpack_small(r) for r in res])
    res = adamw(parts_repl, _pack_repl(mem_norm, kv_norm), _pack_repl(m_mem_norm, m_kv_norm),
                _pack_repl(v_mem_norm, v_kv_norm), name="adamw_repl")
    out["mem_norm"], out["kv_norm"] = zip(*[_unpack_repl(r) for r in res])

    return (loss, grad_x, *[out[k][0] for k in WEIGHTS], *[out[k][1] for k in WEIGHTS],
            *[out[k][2] for k in WEIGHTS], *[out[k][3] for k in WEIGHTS])
```

```python
import numpy as np
import jax
import jax.numpy as jnp
from jax import lax
from jax.experimental import pallas as pl
from jax.experimental.pallas import tpu as pltpu

f32 = jnp.float32
bf16 = jnp.bfloat16

D_MODEL = 1024
SEQ = 2048
DEPTH = 4
N_MEM = 256
HEAD_DIM = 64
N_MEM_HEADS = 4
MEM_W = 256
MAIN_W = 768
POOL_WINDOWS = (2, 4, 8, 16)
POOL_GROUP = 192
POOL_HALO = 16
DIL_PATTERNS = ((128, 1), (512, 4), (2048, 16))
N_GROUPS = 3
GROUP_W = 256
BAND = 128
N_A_LAYERS = 2
D_FF = 2816
ROPE_THETA = 10000.0
EPS = 1e-6
NEG = -1e30
SCALE = HEAD_DIM ** -0.5
N_DEV = 8

ADAM_LR = 0.001
ADAM_B1 = 0.9
ADAM_B2 = 0.999
ADAM_EPS = 1e-08
ADAM_WD = 0.01
ADAM_STEP = 10

VMEM_LIMIT_BYTES = 56 * 1024 * 1024
MESH = pl.DeviceIdType.MESH

NN = (((1,), (0,)), ((), ()))
NT = (((1,), (1,)), ((), ()))
TN = (((0,), (0,)), ((), ()))


def _params(sem=None):
    return pltpu.CompilerParams(dimension_semantics=sem, vmem_limit_bytes=VMEM_LIMIT_BYTES)


def _tile(n, cands):
    for c in cands:
        if n % c == 0:
            return c
    return n


def _sds(shape, dtype):
    return jax.ShapeDtypeStruct(tuple(shape), dtype)


def _rms_r(v):
    return lax.rsqrt(jnp.mean(v * v, axis=-1, keepdims=True) + EPS)


def rms_matmul(x, gain, w, *, name, out_dtype, transposed=False, after=None):
    M, K = x.shape
    N = w.shape[0] if transposed else w.shape[1]
    tm = min(2048, M)
    tn = _tile(N, (512, 256, 128))
    order = [] if after is None else [after]

    def body(x_ref, g_ref, w_ref, *refs):
        z_ref, h_ref = refs[len(order):]

        @pl.when(pl.program_id(1) == 0)
        def _():
            xv = x_ref[...]
            h_ref[...] = (xv * _rms_r(xv) * g_ref[...]).astype(bf16)

        z_ref[...] = lax.dot_general(h_ref[...], w_ref[...], NT if transposed else NN,
                                     preferred_element_type=f32).astype(z_ref.dtype)

    w_spec = pl.BlockSpec((tn, K), lambda i, j: (j, 0)) if transposed else pl.BlockSpec((K, tn), lambda i, j: (0, j))
    return pl.pallas_call(
        body, name=name, grid=(M // tm, N // tn),
        in_specs=[pl.BlockSpec((tm, K), lambda i, j: (i, 0)),
                  pl.BlockSpec((1, K), lambda i, j: (0, 0)),
                  w_spec] + [pl.BlockSpec(memory_space=pl.ANY)] * len(order),
        out_specs=[pl.BlockSpec((tm, tn), lambda i, j: (i, j)), pl.BlockSpec((tm, K), lambda i, j: (i, 0))],
        out_shape=[_sds((M, N), out_dtype), _sds((M, K), bf16)],
        compiler_params=_params(("parallel", "arbitrary")),
    )(x, gain, w, *order)


def matmul_rms_res(a, w, gain, res, *, name):
    M, K = a.shape
    N = w.shape[1]
    tm = min(1024, M)

    def body(a_ref, w_ref, g_ref, r_ref, y_ref, x_ref):
        y = jnp.dot(a_ref[...], w_ref[...], preferred_element_type=f32)
        y_ref[...] = y.astype(bf16)
        x_ref[...] = r_ref[...] + y * _rms_r(y) * g_ref[...]

    row = pl.BlockSpec((tm, N), lambda i: (i, 0))
    return pl.pallas_call(
        body, name=name, grid=(M // tm,),
        in_specs=[pl.BlockSpec((tm, K), lambda i: (i, 0)),
                  pl.BlockSpec((K, N), lambda i: (0, 0)),
                  pl.BlockSpec((1, N), lambda i: (0, 0)),
                  row],
        out_specs=[row, row],
        out_shape=[_sds((M, N), bf16), _sds((M, N), f32)],
        compiler_params=_params(("parallel",)),
    )(a, w, gain, res)


def matmul(a, b, dims, *, name, out_dtype):
    if dims is TN:
        K, M = a.shape
        tm = _tile(M, (512, 256, 128))
        a_spec = pl.BlockSpec((K, tm), lambda i: (0, i))
    else:
        M, K = a.shape
        tm = _tile(M, (1024, 512, 256, 128))
        a_spec = pl.BlockSpec((tm, K), lambda i: (i, 0))
    N = b.shape[0] if dims is NT else b.shape[1]

    def body(a_ref, b_ref, o_ref):
        o_ref[...] = lax.dot_general(a_ref[...].astype(bf16), b_ref[...].astype(bf16), dims,
                                     preferred_element_type=f32).astype(o_ref.dtype)

    return pl.pallas_call(
        body, name=name, grid=(M // tm,),
        in_specs=[a_spec, pl.BlockSpec(b.shape, lambda i: (0, 0))],
        out_specs=pl.BlockSpec((tm, N), lambda i: (i, 0)),
        out_shape=_sds((M, N), out_dtype),
        compiler_params=_params(("parallel",)),
    )(a, b)


def rms_gate_up(x, gain, wt, *, name):
    M, K = x.shape
    tm = min(2048, M)
    tn = _tile(D_FF, (256, 128))
    nj = D_FF // tn

    def body(x_ref, gn_ref, wg_ref, wu_ref, g_ref, u_ref, a_ref, h_ref):
        @pl.when(pl.program_id(1) == 0)
        def _():
            xv = x_ref[...]
            h_ref[...] = (xv * _rms_r(xv) * gn_ref[...]).astype(bf16)

        h = h_ref[...]
        g = lax.dot_general(h, wg_ref[...], NT, preferred_element_type=f32).astype(bf16)
        u = lax.dot_general(h, wu_ref[...], NT, preferred_element_type=f32).astype(bf16)
        g_ref[...] = g
        u_ref[...] = u
        a_ref[...] = g * (1.0 / (1.0 + jnp.exp(-g))) * u

    col = pl.BlockSpec((tm, tn), lambda i, j: (i, j))
    return pl.pallas_call(
        body, name=name, grid=(M // tm, nj),
        in_specs=[pl.BlockSpec((tm, K), lambda i, j: (i, 0)),
                  pl.BlockSpec((1, K), lambda i, j: (0, 0)),
                  pl.BlockSpec((tn, K), lambda i, j: (j, 0)),
                  pl.BlockSpec((tn, K), lambda i, j: (j + nj, 0))],
        out_specs=[col, col, col, pl.BlockSpec((tm, K), lambda i, j: (i, 0))],
        out_shape=[_sds((M, D_FF), bf16)] * 3 + [_sds((M, K), bf16)],
        compiler_params=_params(("parallel", "arbitrary")),
    )(x, gain, wt, wt)


def _rms_bwd_math(yv, gain, dn):
    r = _rms_r(yv)
    q = dn * gain
    dy = r * q - yv * (r * r * r) * jnp.mean(q * yv, axis=-1, keepdims=True)
    return dy, jnp.sum(dn * yv * r, axis=0, keepdims=True)


def _accumulate(ref, val):
    @pl.when(pl.program_id(0) == 0)
    def _():
        ref[...] = jnp.zeros_like(ref)

    ref[...] += val


def down_bwd(y, gain, dn, w_down, g, u, *, name, after=None):
    M, K = y.shape
    tm = min(512, M)
    order = [] if after is None else [after]

    def body(y_ref, gn_ref, dn_ref, w_ref, g_ref, u_ref, *refs):
        dy_ref, o_ref, dg_ref = refs[len(order):]
        dy, dgain = _rms_bwd_math(y_ref[...].astype(f32), gn_ref[...], dn_ref[...])
        dy = dy.astype(bf16)
        dy_ref[...] = dy
        _accumulate(dg_ref, dgain)
        da = lax.dot_general(dy, w_ref[...], NT, preferred_element_type=f32).astype(bf16)
        g = g_ref[...]
        s = 1.0 / (1.0 + jnp.exp(-g))
        o_ref[:, :D_FF] = da * u_ref[...] * s * (1.0 + g * (1.0 - s))
        o_ref[:, D_FF:] = da * g * s

    row = pl.BlockSpec((tm, K), lambda i: (i, 0))
    vec = pl.BlockSpec((1, K), lambda i: (0, 0))
    wide = pl.BlockSpec((tm, D_FF), lambda i: (i, 0))
    return pl.pallas_call(
        body, name=name, grid=(M // tm,),
        in_specs=[row, vec, row, pl.BlockSpec((D_FF, K), lambda i: (0, 0)), wide, wide]
        + [pl.BlockSpec(memory_space=pl.ANY)] * len(order),
        out_specs=[row, pl.BlockSpec((tm, 2 * D_FF), lambda i: (i, 0)), vec],
        out_shape=[_sds((M, K), bf16), _sds((M, 2 * D_FF), bf16), _sds((1, K), f32)],
        compiler_params=_params(("arbitrary",)),
    )(y, gain, dn, w_down, g, u, *order)


def rms_bwd_matmul(y, gain, dn, w, dims, *, name, after=None):
    M, K = y.shape
    N = w.shape[0] if dims is NT else w.shape[1]
    tm = min(1024, M)
    order = [] if after is None else [after]

    def body(y_ref, gn_ref, dn_ref, w_ref, *refs):
        dy_ref, o_ref, dg_ref = refs[len(order):]
        dy, dgain = _rms_bwd_math(y_ref[...].astype(f32), gn_ref[...], dn_ref[...].astype(f32))
        dy = dy.astype(bf16)
        dy_ref[...] = dy
        _accumulate(dg_ref, dgain)
        o_ref[...] = lax.dot_general(dy, w_ref[...], dims, preferred_element_type=f32).astype(bf16)

    row = pl.BlockSpec((tm, K), lambda i: (i, 0))
    vec = pl.BlockSpec((1, K), lambda i: (0, 0))
    return pl.pallas_call(
        body, name=name, grid=(M // tm,),
        in_specs=[row, vec, row, pl.BlockSpec(w.shape, lambda i: (0, 0))]
        + [pl.BlockSpec(memory_space=pl.ANY)] * len(order),
        out_specs=[row, pl.BlockSpec((tm, N), lambda i: (i, 0)), vec],
        out_shape=[_sds((M, K), bf16), _sds((M, N), bf16), _sds((1, K), f32)],
        compiler_params=_params(("arbitrary",)),
    )(y, gain, dn, w, *order)


def matmul_rms_bwd(a, b, dims, y, gain, res, *, name):
    M, K = a.shape
    N = y.shape[1]
    tm = 512

    def body(a_ref, b_ref, y_ref, gn_ref, r_ref, dx_ref, dg_ref):
        dn = lax.dot_general(a_ref[...], b_ref[...], dims, preferred_element_type=f32)
        dy, dgain = _rms_bwd_math(y_ref[...], gn_ref[...], dn)
        dx_ref[...] = dy + r_ref[...]
        _accumulate(dg_ref, dgain)

    row = pl.BlockSpec((tm, N), lambda i: (i, 0))
    vec = pl.BlockSpec((1, N), lambda i: (0, 0))
    return pl.pallas_call(
        body, name=name, grid=(M // tm,),
        in_specs=[pl.BlockSpec((tm, K), lambda i: (i, 0)), pl.BlockSpec(b.shape, lambda i: (0, 0)), row, vec, row],
        out_specs=[row, vec],
        out_shape=[_sds((M, N), f32), _sds((1, N), f32)],
        compiler_params=_params(("arbitrary",)),
    )(a, b, y, gain, res)


def rms_bwd(y, gain, dn, res, *, name, out_dtype, after=None):
    M, N = y.shape
    tm = min(512, M)
    has_res = res is not None
    order = [] if after is None else [after]

    def body(*refs):
        y_ref, g_ref, dn_ref = refs[:3]
        r_ref = refs[3] if has_res else None
        dy_ref, dg_ref = refs[-2:]
        dy, dgain = _rms_bwd_math(y_ref[...].astype(f32), g_ref[...], dn_ref[...].astype(f32))
        if has_res:
            dy = dy + r_ref[...]
        dy_ref[...] = dy.astype(dy_ref.dtype)
        _accumulate(dg_ref, dgain)

    row = pl.BlockSpec((tm, N), lambda i: (i, 0))
    vec = pl.BlockSpec((1, N), lambda i: (0, 0))
    args = [y, gain, dn] + ([res] if has_res else []) + order
    return pl.pallas_call(
        body, name=name, grid=(M // tm,),
        in_specs=[row, vec, row] + ([row] if has_res else []) + [pl.BlockSpec(memory_space=pl.ANY)] * len(order),
        out_specs=[row, vec],
        out_shape=[_sds((M, N), out_dtype), _sds((1, N), f32)],
        compiler_params=_params(("arbitrary",)),
    )(*args)


def loss_head(x, target, *, name):
    M, N = x.shape
    tm = min(512, M)

    def body(x_ref, t_ref, dx_ref, l_ref):
        e = x_ref[...] - t_ref[...]
        dx_ref[...] = e * (1.0 / N)

        @pl.when(pl.program_id(0) == 0)
        def _():
            l_ref[...] = jnp.zeros_like(l_ref)

        l_ref[...] += jnp.sum(jnp.sum(e * e, axis=0, keepdims=True), axis=1, keepdims=True)

    row = pl.BlockSpec((tm, N), lambda i: (i, 0))
    return pl.pallas_call(
        body, name=name, grid=(M // tm,),
        in_specs=[row, row],
        out_specs=[row, pl.BlockSpec((8, 128), lambda i: (0, 0))],
        out_shape=[_sds((M, N), f32), _sds((8, 128), f32)],
        compiler_params=_params(("arbitrary",)),
    )(x, target)


def _pool_select(a1, a2, a3, a4):
    col = lax.broadcasted_iota(jnp.int32, (1, MAIN_W), 1) // POOL_GROUP
    return jnp.where(col == 0, a1, jnp.where(col == 1, a2, jnp.where(col == 2, a3, a4)))


def _pool_count(t):
    col = lax.broadcasted_iota(jnp.int32, (1, MAIN_W), 1) // POOL_GROUP
    win = jnp.where(col == 0, 2, jnp.where(col == 1, 4, jnp.where(col == 2, 8, 16)))
    return jnp.minimum(t + 1, win).astype(f32)


def pool_fwd(z, wbd, scale, *, name):
    M = z.shape[0]
    tm = 256
    nper = SEQ // tm
    hb = tm // POOL_HALO

    def body(zc_ref, zh_ref, w_ref, s_ref, p_ref, y_ref):
        i = pl.program_id(0)
        seq_blk = i % nper
        halo = jnp.where(seq_blk == 0, 0.0, zh_ref[...])
        u = zc_ref[...]
        ext = jnp.concatenate([halo, u], axis=0)
        a1 = ext + pltpu.roll(ext, 1, 0)
        a2 = a1 + pltpu.roll(a1, 2, 0)
        a3 = a2 + pltpu.roll(a2, 4, 0)
        a4 = a3 + pltpu.roll(a3, 8, 0)
        sums = _pool_select(a1, a2, a3, a4)[POOL_HALO:]
        t = seq_blk * tm + lax.broadcasted_iota(jnp.int32, (tm, 1), 0)
        p = (sums / _pool_count(t) - u).astype(bf16)
        p_ref[...] = p
        y_ref[...] = (jnp.dot(p, w_ref[...], preferred_element_type=f32) * s_ref[...]).astype(bf16)

    return pl.pallas_call(
        body, name=name, grid=(M // tm,),
        in_specs=[pl.BlockSpec((tm, MAIN_W), lambda i: (i, 0)),
                  pl.BlockSpec((POOL_HALO, MAIN_W), lambda i: (jnp.maximum(i * hb - 1, 0), 0)),
                  pl.BlockSpec((MAIN_W, MAIN_W), lambda i: (0, 0)),
                  pl.BlockSpec((1, MAIN_W), lambda i: (0, 0))],
        out_specs=[pl.BlockSpec((tm, MAIN_W), lambda i: (i, 0)),
                   pl.BlockSpec((tm, MAIN_W), lambda i: (i, 0))],
        out_shape=[_sds((M, MAIN_W), bf16), _sds((M, D_MODEL), bf16)],
        compiler_params=_params(("parallel",)),
    )(z, z, wbd, scale)


def pool_bwd(dyc, p, wbd, scale, *, name):
    M = p.shape[0]
    tm = 256
    nper = SEQ // tm
    hb = tm // POOL_HALO
    last_hb = M // POOL_HALO - 1

    def body(dy_ref, dyh_ref, p_ref, w_ref, s_ref, dz_ref, dw_ref, ds_ref):
        i = pl.program_id(0)
        seq_blk = i % nper
        dy = dy_ref[...].astype(f32)
        pv = p_ref[...]
        w = w_ref[...]
        sc = s_ref[...]

        @pl.when(i == 0)
        def _():
            dw_ref[...] = jnp.zeros_like(dw_ref)
            ds_ref[...] = jnp.zeros_like(ds_ref)

        v = jnp.dot(pv, w, preferred_element_type=f32)
        ds_ref[...] += jnp.sum(dy * v, axis=0, keepdims=True)
        dv = (dy * sc).astype(bf16)
        dw_ref[...] += lax.dot_general(pv, dv, TN, preferred_element_type=f32)
        dp = lax.dot_general(dv, w, NT, preferred_element_type=f32)
        dvh = jnp.where(seq_blk == nper - 1, 0.0, dyh_ref[...].astype(f32) * sc).astype(bf16)
        dph = lax.dot_general(dvh, w, NT, preferred_element_type=f32)
        ext = jnp.concatenate([dp, dph], axis=0)
        n = tm + POOL_HALO
        t = seq_blk * tm + lax.broadcasted_iota(jnp.int32, (n, 1), 0)
        e = ext / _pool_count(t)
        b1 = e + pltpu.roll(e, n - 1, 0)
        b2 = b1 + pltpu.roll(b1, n - 2, 0)
        b3 = b2 + pltpu.roll(b2, n - 4, 0)
        b4 = b3 + pltpu.roll(b3, n - 8, 0)
        dz_ref[...] = (_pool_select(b1, b2, b3, b4)[:tm] - dp).astype(dz_ref.dtype)

    return pl.pallas_call(
        body, name=name, grid=(M // tm,),
        in_specs=[pl.BlockSpec((tm, MAIN_W), lambda i: (i, 0)),
                  pl.BlockSpec((POOL_HALO, MAIN_W), lambda i: (jnp.minimum((i + 1) * hb, last_hb), 0)),
                  pl.BlockSpec((tm, MAIN_W), lambda i: (i, 0)),
                  pl.BlockSpec((MAIN_W, MAIN_W), lambda i: (0, 0)),
                  pl.BlockSpec((1, MAIN_W), lambda i: (0, 0))],
        out_specs=[pl.BlockSpec((tm, MAIN_W), lambda i: (i, 0)),
                   pl.BlockSpec((MAIN_W, MAIN_W), lambda i: (0, 0)),
                   pl.BlockSpec((1, MAIN_W), lambda i: (0, 0))],
        out_shape=[_sds((M, D_MODEL), bf16), _sds((MAIN_W, MAIN_W), f32), _sds((1, MAIN_W), f32)],
        compiler_params=_params(("arbitrary",)),
    )(dyc, dyc, p, wbd, scale)


def _mem_heads(q, kv):
    first = _first_head()
    for pr in range(N_MEM_HEADS // 2):
        cols = slice(pr * PAIR_W, (pr + 1) * PAIR_W)
        qp = q[:, cols] * SCALE
        kp = kv[:, cols]
        vp = kv[:, MEM_W + pr * PAIR_W: MEM_W + (pr + 1) * PAIR_W]
        for hh in range(2):
            lm = first if hh == 0 else ~first
            qm = jnp.where(lm, qp, 0.0).astype(bf16)
            s = lax.dot_general(qm, kp, NT, preferred_element_type=f32)
            e = jnp.exp(s - jnp.max(s, axis=-1, keepdims=True))
            yield lm, qm, kp, vp, e, jnp.sum(e, axis=-1, keepdims=True)


def memattn_fwd(z, kvm, ycat, *, name, n_seq):
    M = z.shape[0]
    tq = 512
    nq = SEQ // tq

    def body(q_ref, kv_ref, _, o_ref):
        first = _first_head()
        outs = []
        for lm, _, _, vp, e, l in _mem_heads(q_ref[...], kv_ref[...]):
            outs.append(jnp.dot(e.astype(bf16), vp, preferred_element_type=f32) * (1.0 / l))
        pairs = [jnp.where(first, outs[2 * pr], outs[2 * pr + 1]) for pr in range(N_MEM_HEADS // 2)]
        o_ref[...] = jnp.concatenate(pairs, axis=1).astype(bf16)

    return pl.pallas_call(
        body, name=name, grid=(n_seq, nq),
        in_specs=[pl.BlockSpec((tq, MEM_W), lambda b, i: (b * nq + i, 3)),
                  pl.BlockSpec((N_MEM, 2 * MEM_W), lambda b, i: (b, 0)),
                  pl.BlockSpec(memory_space=pl.ANY)],
        out_specs=pl.BlockSpec((tq, MEM_W), lambda b, i: (b * nq + i, 3)),
        out_shape=_sds((M, D_MODEL), bf16),
        input_output_aliases={2: 0},
        compiler_params=_params(("parallel", "parallel")),
    )(z, kvm, ycat)


def memattn_bwd(z, kvm, dyc, dz, *, name, n_seq):
    M = z.shape[0]
    tq = 512
    nq = SEQ // tq

    def body(q_ref, kv_ref, dy_ref, _, dq_ref, dkv_ref):
        first = _first_head()
        dy = dy_ref[...].astype(f32)
        dqs, dks, dvs = [], [], []
        for h, (lm, qm, kp, vp, e, l) in enumerate(_mem_heads(q_ref[...], kv_ref[...])):
            pr = h // 2
            p = e * (1.0 / l)
            dym = jnp.where(lm, dy[:, pr * PAIR_W:(pr + 1) * PAIR_W], 0.0).astype(bf16)
            dp = lax.dot_general(dym, vp, NT, preferred_element_type=f32)
            ds = (p * (dp - jnp.sum(dp * p, axis=-1, keepdims=True))).astype(bf16)
            dqs.append(jnp.dot(ds, kp, preferred_element_type=f32) * SCALE)
            dk = lax.dot_general(ds, qm, TN, preferred_element_type=f32)
            dv = lax.dot_general(p.astype(bf16), dym, TN, preferred_element_type=f32)
            if h % 2 == 0:
                dks.append(dk)
                dvs.append(dv)
            else:
                dks[pr] = dks[pr] + dk
                dvs[pr] = dvs[pr] + dv
        pairs = [jnp.where(first, dqs[2 * pr], dqs[2 * pr + 1]) for pr in range(N_MEM_HEADS // 2)]
        dq_ref[...] = jnp.concatenate(pairs, axis=1).astype(bf16)

        @pl.when(pl.program_id(1) == 0)
        def _():
            dkv_ref[...] = jnp.zeros_like(dkv_ref)

        dkv_ref[...] += jnp.concatenate(dks + dvs, axis=1)

    return pl.pallas_call(
        body, name=name, grid=(n_seq, nq),
        in_specs=[pl.BlockSpec((tq, MEM_W), lambda b, i: (b * nq + i, 3)),
                  pl.BlockSpec((N_MEM, 2 * MEM_W), lambda b, i: (b, 0)),
                  pl.BlockSpec((tq, MEM_W), lambda b, i: (b * nq + i, 3)),
                  pl.BlockSpec(memory_space=pl.ANY)],
        out_specs=[pl.BlockSpec((tq, MEM_W), lambda b, i: (b * nq + i, 3)),
                   pl.BlockSpec((N_MEM, 2 * MEM_W), lambda b, i: (b, 0))],
        out_shape=[_sds((M, D_MODEL), bf16), _sds((n_seq * N_MEM, 2 * MEM_W), f32)],
        input_output_aliases={3: 0},
        compiler_params=_params(("parallel", "arbitrary")),
    )(z, kvm, dyc, dz)


def rope_tables(pos, *, name):
    M = pos.shape[0]
    tm = min(1024, M)
    half = HEAD_DIM // 2
    inv = ROPE_THETA ** (-np.arange(half, dtype=np.float64) / half)
    inv128 = jnp.asarray(np.tile(inv, 4)[None, :], f32)
    sign128 = jnp.asarray(np.tile(np.concatenate([-np.ones(half), np.ones(half)]), 2)[None, :], f32)

    def body(p_ref, f_ref, s_ref, cos_ref, sin_ref):
        ang = p_ref[...] * f_ref[...]
        cos_ref[...] = jnp.cos(ang)
        sin_ref[...] = jnp.sin(ang) * s_ref[...]

    return pl.pallas_call(
        body, name=name, grid=(M // tm,),
        in_specs=[pl.BlockSpec((tm, 1), lambda i: (i, 0)),
                  pl.BlockSpec((1, 128), lambda i: (0, 0)),
                  pl.BlockSpec((1, 128), lambda i: (0, 0))],
        out_specs=[pl.BlockSpec((tm, 128), lambda i: (i, 0)),
                   pl.BlockSpec((tm, 128), lambda i: (i, 0))],
        out_shape=[_sds((M, 128), f32), _sds((M, 128), f32)],
        compiler_params=_params(("parallel",)),
    )(pos, inv128, sign128)


def _swap_halves(x):
    w = x.shape[1]
    first = (lax.broadcasted_iota(jnp.int32, (1, w), 1) % HEAD_DIM) < (HEAD_DIM // 2)
    return jnp.where(first, pltpu.roll(x, w - HEAD_DIM // 2, 1), pltpu.roll(x, HEAD_DIM // 2, 1))


def rope_fwd(src, cos, sin, *, name):
    M = src.shape[0]
    tm = min(512, M)

    def body(x_ref, c_ref, s_ref, o_ref):
        x = x_ref[...].astype(f32)
        c = jnp.tile(c_ref[...], (1, MAIN_W // 128))
        s = jnp.tile(s_ref[...], (1, MAIN_W // 128))
        o_ref[...] = x * c + _swap_halves(x) * s

    return pl.pallas_call(
        body, name=name, grid=(M // tm,),
        in_specs=[pl.BlockSpec((tm, MAIN_W), lambda i: (i, 0)),
                  pl.BlockSpec((tm, 128), lambda i: (i, 0)),
                  pl.BlockSpec((tm, 128), lambda i: (i, 0))],
        out_specs=pl.BlockSpec((tm, MAIN_W), lambda i: (i, 0)),
        out_shape=_sds((M, MAIN_W), f32),
        compiler_params=_params(("parallel",)),
    )(src, cos, sin)


def group_sum(groups, cos, sin, *, name, rotate, width, col_block=0, into=None):
    M = groups[0][0].shape[0]
    tm = min(512, M)
    counts = [len(g) for g in groups]
    flat = [a for g in groups for a in g]
    extra = [] if into is None else [into]

    def body(*refs):
        part_refs = refs[:len(flat)]
        c_ref, s_ref = refs[len(flat):len(flat) + 2]
        o_ref = refs[-1]
        cols, k = [], 0
        for n in counts:
            acc = part_refs[k][...]
            for r in part_refs[k + 1:k + n]:
                acc = acc + r[...]
            cols.append(acc)
            k += n
        d = jnp.concatenate(cols, axis=1)
        if rotate:
            c = jnp.tile(c_ref[...], (1, MAIN_W // 128))
            s = jnp.tile(s_ref[...], (1, MAIN_W // 128))
            d = d * c - _swap_halves(d) * s
        o_ref[...] = d.astype(bf16)

    part = pl.BlockSpec((tm, GROUP_W), lambda i: (i, 0))
    tab = pl.BlockSpec((tm, 128), lambda i: (i, 0))
    return pl.pallas_call(
        body, name=name, grid=(M // tm,),
        in_specs=[part] * len(flat) + [tab, tab] + [pl.BlockSpec(memory_space=pl.ANY)] * len(extra),
        out_specs=pl.BlockSpec((tm, MAIN_W), lambda i: (i, col_block)),
        out_shape=_sds((M, width), bf16),
        input_output_aliases={len(flat) + 2: 0} if extra else {},
        compiler_params=_params(("parallel",)),
    )(*flat, cos, sin, *extra)


PAIR_W = 2 * HEAD_DIM
MIN_BLOCKS = 8


def _dil_geometry(dil):
    nsub = max(dil, MIN_BLOCKS)
    tb = BAND * nsub
    return nsub, tb, SEQ // tb


def _rows(ref, sub, dil):
    if dil == 1:
        return ref[sub * BAND:(sub + 1) * BAND, :]
    nl, r = divmod(sub, dil)
    return ref[pl.ds(nl * BAND * dil + r, BAND, stride=dil), :]


def _store_rows(ref, sub, dil, val):
    if dil == 1:
        ref[sub * BAND:(sub + 1) * BAND, :] = val
    else:
        nl, r = divmod(sub, dil)
        ref[pl.ds(nl * BAND * dil + r, BAND, stride=dil), :] = val


def _keys(prev_ref, own_ref, sub, dil):
    nsub = own_ref.shape[0] // BAND
    if sub >= dil:
        prev = _rows(own_ref, sub - dil, dil)
    elif prev_ref is None:
        return _rows(own_ref, sub, dil)
    else:
        prev = _rows(prev_ref, nsub - dil + sub, dil)
    return jnp.concatenate([prev, _rows(own_ref, sub, dil)], axis=0)


def _band_mask(nkeys, has_prev):
    i = lax.broadcasted_iota(jnp.int32, (BAND, nkeys), 0)
    j = lax.broadcasted_iota(jnp.int32, (BAND, nkeys), 1)
    if nkeys == BAND:
        return j <= i
    return (j >= i) & (j <= i + BAND) & (has_prev | (j >= BAND))


def _first_head():
    return lax.broadcasted_iota(jnp.int32, (1, PAIR_W), 1) < HEAD_DIM


def _col(x, hh):
    return x[:, hh * HEAD_DIM:hh * HEAD_DIM + 1]


def _pair_spec(tb, nblk, col0, which):
    def idx(b, p, i):
        if which < 0:
            i = jnp.maximum(i - 1, 0)
        elif which > 0:
            i = jnp.minimum(i + 1, nblk - 1)
        return (b * nblk + i, col0 + p)
    return pl.BlockSpec((tb, PAIR_W), idx)


def dil_fwd(q, k, kv, g, dil, *, name, n_seq):
    M = q.shape[0]
    nsub, tb, nblk = _dil_geometry(dil)
    with_prev = nblk > 1

    def body(*refs):
        if with_prev:
            q_ref, ko_ref, vo_ref, kp_ref, vp_ref, o_ref, l_ref = refs
        else:
            (q_ref, ko_ref, vo_ref, o_ref, l_ref), kp_ref, vp_ref = refs, None, None
        first = _first_head()
        blk = pl.program_id(2)
        for sub in range(nsub):
            qs = _rows(q_ref, sub, dil) * SCALE
            kc = _keys(kp_ref, ko_ref, sub, dil).astype(bf16)
            vc = _keys(vp_ref, vo_ref, sub, dil).astype(bf16)
            has_prev = True if sub >= dil else blk > 0
            mask = _band_mask(kc.shape[0], has_prev)
            outs, lses = [], []
            for hh in range(2):
                qm = jnp.where(first if hh == 0 else ~first, qs, 0.0).astype(bf16)
                s = jnp.where(mask, lax.dot_general(qm, kc, NT, preferred_element_type=f32), NEG)
                m = jnp.max(s, axis=-1, keepdims=True)
                e = jnp.exp(s - m)
                l = jnp.sum(e, axis=-1, keepdims=True)
                outs.append(jnp.dot(e.astype(bf16), vc, preferred_element_type=f32) * (1.0 / l))
                lses.append(jnp.broadcast_to(m + jnp.log(l), (BAND, PAIR_W)))
            _store_rows(o_ref, sub, dil, jnp.where(first, outs[0], outs[1]))
            _store_rows(l_ref, sub, dil, jnp.where(first, lses[0], lses[1]))

    ins = [(q, 2 * g, 0), (k, 2 * g, 0), (kv, 6 + 2 * g, 0)]
    if with_prev:
        ins += [(k, 2 * g, -1), (kv, 6 + 2 * g, -1)]
    out = _pair_spec(tb, nblk, 0, 0)
    return pl.pallas_call(
        body, name=name, grid=(n_seq, 2, nblk),
        in_specs=[_pair_spec(tb, nblk, c, w) for _, c, w in ins],
        out_specs=[out, out],
        out_shape=[_sds((M, GROUP_W), f32)] * 2,
        compiler_params=_params(("parallel", "parallel", "arbitrary")),
    )(*[a for a, _, _ in ins])


def combine_fwd(os_, lses, *, name):
    M = os_[0].shape[0]
    tm = min(512, M)

    def body(o0, o1, o2, l0, l1, l2, y_ref):
        ls = [l0[...], l1[...], l2[...]]
        m = jnp.maximum(jnp.maximum(ls[0], ls[1]), ls[2])
        es = [jnp.exp(l - m) for l in ls]
        inv = 1.0 / (es[0] + es[1] + es[2])
        y_ref[...] = jnp.concatenate([o[...] * e * inv for o, e in zip((o0, o1, o2), es)], axis=1).astype(bf16)

    part = pl.BlockSpec((tm, GROUP_W), lambda i: (i, 0))
    return pl.pallas_call(
        body, name=name, grid=(M // tm,),
        in_specs=[part] * 6,
        out_specs=pl.BlockSpec((tm, MAIN_W), lambda i: (i, 0)),
        out_shape=_sds((M, D_MODEL), bf16),
        compiler_params=_params(("parallel",)),
    )(*os_, *lses)


def combine_bwd(dyc, os_, lses, *, name):
    M = os_[0].shape[0]
    tm = min(512, M)

    def body(dy_ref, o0, o1, o2, l0, l1, l2, d0, d1, d2, c0, c1, c2):
        r = lax.broadcasted_iota(jnp.int32, (GROUP_W, GROUP_W), 0) // HEAD_DIM
        c = lax.broadcasted_iota(jnp.int32, (GROUP_W, GROUP_W), 1) // HEAD_DIM
        ones = (r == c).astype(f32)
        dy = dy_ref[...].astype(f32)
        ls = [l0[...], l1[...], l2[...]]
        m = jnp.maximum(jnp.maximum(ls[0], ls[1]), ls[2])
        es = [jnp.exp(l - m) for l in ls]
        inv = 1.0 / (es[0] + es[1] + es[2])
        total = 0.0
        alphas = []
        for g, (o, e, d_ref) in enumerate(zip((o0, o1, o2), es, (d0, d1, d2))):
            a = e * inv
            dyg = dy[:, g * GROUP_W:(g + 1) * GROUP_W]
            d_ref[...] = dyg * a
            dsum = jnp.dot(dyg * o[...], ones, precision=lax.Precision.HIGHEST, preferred_element_type=f32)
            total = total + a * dsum
            alphas.append(a)
        for a, c_ref in zip(alphas, (c0, c1, c2)):
            c_ref[...] = -a * total

    part = pl.BlockSpec((tm, GROUP_W), lambda i: (i, 0))
    outs = pl.pallas_call(
        body, name=name, grid=(M // tm,),
        in_specs=[pl.BlockSpec((tm, MAIN_W), lambda i: (i, 0))] + [part] * 6,
        out_specs=[part] * 6,
        out_shape=[_sds((M, GROUP_W), f32)] * 6,
        compiler_params=_params(("parallel",)),
    )(dyc, *os_, *lses)
    return outs[:3], outs[3:]


def dil_bwd(q, k, kv, do, cc, lse, g, dil, *, name, n_seq):
    M = q.shape[0]
    nsub = SEQ // BAND
    per_res = nsub // dil

    def body(q_ref, k_ref, v_ref, do_ref, c_ref, l_ref, dq_ref, dk_ref, dv_ref):
        first = _first_head()
        for r in range(dil):
            carry = None
            for nl in range(per_res):
                sub = nl * dil + r
                qs = _rows(q_ref, sub, dil) * SCALE
                dos = _rows(do_ref, sub, dil)
                cs = _rows(c_ref, sub, dil)
                ls = _rows(l_ref, sub, dil)
                kc = _keys(None, k_ref, sub, dil).astype(bf16)
                vc = _keys(None, v_ref, sub, dil).astype(bf16)
                nkeys = kc.shape[0]
                mask = _band_mask(nkeys, True)
                dqs = []
                dkc = jnp.zeros((nkeys, PAIR_W), f32)
                dvc = jnp.zeros((nkeys, PAIR_W), f32)
                for hh in range(2):
                    lm = first if hh == 0 else ~first
                    qm = jnp.where(lm, qs, 0.0).astype(bf16)
                    dom = jnp.where(lm, dos, 0.0).astype(bf16)
                    s = jnp.where(mask, lax.dot_general(qm, kc, NT, preferred_element_type=f32), NEG)
                    p = jnp.exp(s - _col(ls, hh))
                    dp = lax.dot_general(dom, vc, NT, preferred_element_type=f32)
                    ds = (p * (dp + _col(cs, hh))).astype(bf16)
                    dqs.append(jnp.dot(ds, kc, preferred_element_type=f32) * SCALE)
                    dkc = dkc + lax.dot_general(ds, qm, TN, preferred_element_type=f32)
                    dvc = dvc + lax.dot_general(p.astype(bf16), dom, TN, preferred_element_type=f32)
                _store_rows(dq_ref, sub, dil, jnp.where(first, dqs[0], dqs[1]))
                if nkeys == 2 * BAND:
                    _store_rows(dk_ref, sub - dil, dil, carry[0] + dkc[:BAND])
                    _store_rows(dv_ref, sub - dil, dil, carry[1] + dvc[:BAND])
                    carry = (dkc[BAND:], dvc[BAND:])
                else:
                    carry = (dkc, dvc)
            _store_rows(dk_ref, (per_res - 1) * dil + r, dil, carry[0])
            _store_rows(dv_ref, (per_res - 1) * dil + r, dil, carry[1])

    def spec(col0):
        return pl.BlockSpec((SEQ, PAIR_W), lambda b, p: (b, col0 + p))

    out = spec(0)
    return pl.pallas_call(
        body, name=name, grid=(n_seq, 2),
        in_specs=[spec(2 * g), spec(2 * g), spec(6 + 2 * g), spec(0), spec(0), spec(0)],
        out_specs=[out, out, out],
        out_shape=[_sds((M, GROUP_W), f32)] * 3,
        compiler_params=_params(("parallel", "parallel")),
    )(q, k, kv, do, cc, lse)


def _blockdiag(wp):
    out = jnp.zeros((MAIN_W, MAIN_W), wp.dtype)
    for gi in range(len(POOL_WINDOWS)):
        sl = slice(gi * POOL_GROUP, (gi + 1) * POOL_GROUP)
        out = out.at[sl, sl].set(wp[gi])
    return out


def _unblockdiag(w):
    return jnp.stack([w[gi * POOL_GROUP:(gi + 1) * POOL_GROUP, gi * POOL_GROUP:(gi + 1) * POOL_GROUP]
                      for gi in range(len(POOL_WINDOWS))])


def local_step(x, mem, positions, target, P, layer_weights, kv_weight, emit_grads):
    n_seq = x.shape[0]
    M = n_seq * SEQ
    xs = x.reshape(M, D_MODEL)
    mems = mem.reshape(n_seq * N_MEM, D_MODEL)
    pos = positions.reshape(M, 1).astype(f32)
    cos, sin = rope_tables(pos, name="rope_tables")
    gains = P["norm_gains"]

    def gain(l, k):
        return gains[l, k].reshape(1, D_MODEL)

    saved = []
    kvs = None
    for l in range(DEPTH):
        W, started = layer_weights(l, "mix", xs)
        sv = {"x": xs, "W": W}
        z, h1 = rms_matmul(xs, gain(l, 0), W["w_in"], name=f"l{l}_in", out_dtype=f32, after=started)
        kvm, mn = rms_matmul(mems, P["mem_norm"][l].reshape(1, D_MODEL), W["w_mem_kv"],
                             name=f"l{l}_memkv", out_dtype=bf16)
        sv.update(z=z, h1=h1, kvm=kvm, mn=mn)
        if l < N_A_LAYERS:
            wbd = _blockdiag(P["w_pool"][l].astype(bf16))
            psc = P["pool_scale"][l].reshape(1, MAIN_W)
            p, y_main = pool_fwd(z, wbd, psc, name=f"l{l}_pool")
            sv.update(p=p, wbd=wbd, psc=psc)
        else:
            qrot = rope_fwd(z, cos, sin, name=f"l{l}_ropeq")
            os_, lses = [], []
            for g, (_, dil) in enumerate(DIL_PATTERNS):
                o, lse = dil_fwd(qrot, kvs["krot"], kvs["kv"], g, dil, name=f"l{l}_dil{g}", n_seq=n_seq)
                os_.append(o)
                lses.append(lse)
            y_main = combine_fwd(os_, lses, name=f"l{l}_comb")
            sv.update(qrot=qrot, os=os_, lses=lses)
        ycat = memattn_fwd(z, kvm, y_main, name=f"l{l}_memattn", n_seq=n_seq)
        y, x1 = matmul_rms_res(ycat, W["w_out"], gain(l, 1), xs, name=f"l{l}_out")
        W.update(layer_weights(l, "gu", x1)[0])
        fg, fu, a, h2 = rms_gate_up(x1, gain(l, 2), W["w_gate_up"], name=f"l{l}_gu")
        W.update(layer_weights(l, "down", a)[0])
        y2, x2 = matmul_rms_res(a, W["w_down"], gain(l, 3), x1, name=f"l{l}_down")
        sv.update(ycat=ycat, y=y, x1=x1, fg=fg, fu=fu, h2=h2, a=a, y2=y2)
        saved.append(sv)
        xs = x2
        if l == N_A_LAYERS - 1:
            w_kv = kv_weight(xs)
            kv, hkv = rms_matmul(xs, P["kv_norm"].reshape(1, D_MODEL), w_kv, name="kv_proj", out_dtype=f32,
                                 transposed=True)
            krot = rope_fwd(kv, cos, sin, name="ropek")
            kvs = {"kv": kv, "hkv": hkv, "krot": krot, "x": xs, "w_kv": w_kv}

    dx, sq = loss_head(xs, target.reshape(M, D_MODEL), name="loss_head")

    G = {"mem_norm": [None] * DEPTH, "norm_gains": [[None] * 4 for _ in range(DEPTH)],
         "pool_scale": [None] * N_A_LAYERS}
    dk_parts = [[] for _ in range(N_GROUPS)]
    dv_parts = [[] for _ in range(N_GROUPS)]
    emitted = None

    for l in reversed(range(DEPTH)):
        sv = saved[l]
        W = sv["W"]
        gw = {}
        dy2, dgu, G["norm_gains"][l][3] = down_bwd(sv["y2"], gain(l, 3), dx, W["w_down"], sv["fg"], sv["fu"],
                                                   name=f"l{l}_b_dgu", after=emitted)
        gw["w_down"] = matmul(sv["a"], dy2, TN, name=f"l{l}_b_wd", out_dtype=bf16)
        dx1, G["norm_gains"][l][2] = matmul_rms_bwd(dgu, W["w_gate_up"], NN, sv["x1"], gain(l, 2), dx,
                                                    name=f"l{l}_b_dh2")
        gw["w_gate_up"] = matmul(dgu, sv["h2"], TN, name=f"l{l}_b_wgu", out_dtype=bf16)
        emitted = emit_grads(l, "ffn", gw)
        gw = {}
        dy, dycat, G["norm_gains"][l][1] = rms_bwd_matmul(sv["y"], gain(l, 1), dx1, W["w_out"], NT,
                                                          name=f"l{l}_b_dycat", after=emitted)
        gw["w_out"] = matmul(sv["ycat"], dy, TN, name=f"l{l}_b_wout", out_dtype=bf16)
        if l < N_A_LAYERS:
            dz, dwbd, dps = pool_bwd(dycat, sv["p"], sv["wbd"], sv["psc"], name=f"l{l}_b_pool")
            gw["w_pool"] = _unblockdiag(dwbd).reshape(MAIN_W, POOL_GROUP).astype(bf16)
            G["pool_scale"][l] = dps.reshape(MAIN_W)
        else:
            dos, ccs = combine_bwd(dycat, sv["os"], sv["lses"], name=f"l{l}_b_comb")
            dqs = []
            for g, (_, dil) in enumerate(DIL_PATTERNS):
                args = (sv["qrot"], kvs["krot"], kvs["kv"], dos[g], ccs[g], sv["lses"][g], g, dil)
                dq, dk, dv = dil_bwd(*args, name=f"l{l}_b_dil{g}", n_seq=n_seq)
                dqs.append([dq])
                dk_parts[g].append(dk)
                dv_parts[g].append(dv)
            dz = group_sum(dqs, cos, sin, name=f"l{l}_b_ropeq", rotate=True, width=D_MODEL)
        dz, dkvm = memattn_bwd(sv["z"], sv["kvm"], dycat, dz, name=f"l{l}_b_memattn", n_seq=n_seq)
        dmn = matmul(dkvm, W["w_mem_kv"], NT, name=f"l{l}_b_dmn", out_dtype=bf16)
        gw["w_mem_kv"] = matmul(sv["mn"], dkvm, TN, name=f"l{l}_b_wmkv", out_dtype=bf16)
        _, G["mem_norm"][l] = rms_bwd(mems, P["mem_norm"][l].reshape(1, D_MODEL), dmn, None,
                                      name=f"l{l}_b_nmem", out_dtype=bf16)
        gw["w_in"] = matmul(sv["h1"], dz, TN, name=f"l{l}_b_win", out_dtype=bf16)
        dx, G["norm_gains"][l][0] = matmul_rms_bwd(dz, W["w_in"], NT, sv["x"], gain(l, 0), dx1, name=f"l{l}_b_dh1")
        if l == N_A_LAYERS:
            dkv = group_sum(dk_parts, cos, sin, name="b_ropek", rotate=True, width=2 * MAIN_W)
            dkv = group_sum(dv_parts, cos, sin, name="b_sumv", rotate=False, width=2 * MAIN_W, col_block=1, into=dkv)
            gw["w_kv"] = matmul(dkv, kvs["hkv"], TN, name="b_wkv", out_dtype=bf16)
            dx, gkn = matmul_rms_bwd(dkv, kvs["w_kv"], NN, kvs["x"], P["kv_norm"].reshape(1, D_MODEL), dx,
                                     name="b_dhkv")
            G["kv_norm"] = gkn.reshape(D_MODEL)
        emitted = emit_grads(l, "mix", gw)

    small = {"pool_scale": jnp.stack(G["pool_scale"]),
             "mem_norm": jnp.concatenate(G["mem_norm"], axis=0),
             "norm_gains": jnp.stack([jnp.concatenate(r, axis=0) for r in G["norm_gains"]]),
             "kv_norm": G["kv_norm"]}
    return sq[0, 0], dx.reshape(n_seq, SEQ, D_MODEL), small, emitted


def _peer(k):
    x, y, c = lax.axis_index("x"), lax.axis_index("y"), lax.axis_index("c")
    px = 1 - x if k & 4 else x
    py = 1 - y if k & 2 else y
    pc = 1 - c if k & 1 else c
    return (px, py, pc), 4 * px + 2 * py + pc


def _my_index():
    return 4 * lax.axis_index("x") + 2 * lax.axis_index("y") + lax.axis_index("c")


def _src_for(kinds, in_refs, i, idx):
    return in_refs[i] if kinds[i] == "gather" else in_refs[i].at[idx]


def _local_copies(kinds, in_refs, out_refs, local_sems):
    me = _my_index()
    return [pltpu.make_async_copy(_src_for(kinds, in_refs, i, me), out_refs[i].at[me], local_sems.at[i])
            for i in range(len(kinds))]


def _remote_copies(kinds, in_refs, out_refs, send_sems, recv_sems, *, arriving):
    me = _my_index()
    copies = []
    for k in range(1, N_DEV):
        dev, idx = _peer(k)
        for i in range(len(kinds)):
            j = i * (N_DEV - 1) + k - 1
            copies.append(pltpu.make_async_remote_copy(
                src_ref=_src_for(kinds, in_refs, i, idx), dst_ref=out_refs[i].at[idx if arriving else me],
                send_sem=send_sems.at[j], recv_sem=recv_sems.at[j], device_id=dev, device_id_type=MESH))
    return copies


def _out_shape(a, kind):
    return ((N_DEV,) + a.shape) if kind == "gather" else a.shape


def exchange(items, *, name, after=()):
    n = len(items)
    kinds = [k for _, k in items]
    after = list(after)

    def body(*refs):
        in_refs, out_refs = refs[:n], refs[n + len(after):2 * n + len(after)]
        send_sems, recv_sems, local_sems = refs[-3:]
        local = _local_copies(kinds, in_refs, out_refs, local_sems)
        sends = _remote_copies(kinds, in_refs, out_refs, send_sems, recv_sems, arriving=False)
        for cp in local + sends:
            cp.start()
        for cp in _remote_copies(kinds, in_refs, out_refs, send_sems, recv_sems, arriving=True):
            cp.wait_recv()
        for cp in sends:
            cp.wait_send()
        for cp in local:
            cp.wait()

    any_spec = pl.BlockSpec(memory_space=pl.ANY)
    return pl.pallas_call(
        body, name=name,
        in_specs=[any_spec] * (n + len(after)), out_specs=[any_spec] * n,
        out_shape=[_sds(_out_shape(a, k), a.dtype) for a, k in items],
        scratch_shapes=[pltpu.SemaphoreType.DMA((n * (N_DEV - 1),)), pltpu.SemaphoreType.DMA((n * (N_DEV - 1),)),
                        pltpu.SemaphoreType.DMA((n,))],
    )(*[a for a, _ in items], *after)


_HBM = pl.BlockSpec(memory_space=pltpu.HBM)
_SEM = pl.BlockSpec(memory_space=pltpu.SEMAPHORE)
_EFFECT = pltpu.SideEffectType.DATAFLOW_SIDE_EFFECTING


def exchange_start(items, after, *, name):
    n = len(items)
    kinds = [k for _, k in items]

    def body(*refs):
        in_refs, land_refs = refs[:n], refs[n:2 * n]
        send_sems, recv_sems, local_sems = refs[2 * n + 1:2 * n + 4]
        token = refs[-1]
        for cp in (_local_copies(kinds, in_refs, land_refs, local_sems)
                   + _remote_copies(kinds, in_refs, land_refs, send_sems, recv_sems, arriving=False)):
            cp.start()
        token[...] = jnp.zeros_like(token)

    srcs = [pltpu.with_memory_space_constraint(a, pltpu.HBM) for a, _ in items]
    lands = [pltpu.with_memory_space_constraint(lax.empty(_out_shape(a, k), a.dtype), pltpu.HBM) for a, k in items]
    outs = pl.pallas_call(
        body, name=name,
        out_shape=(pltpu.SemaphoreType.DMA((n * (N_DEV - 1),)), pltpu.SemaphoreType.DMA((n * (N_DEV - 1),)),
                   pltpu.SemaphoreType.DMA((n,)),
                   *[pltpu.HBM(a.shape, a.dtype) for a in srcs], *[pltpu.HBM(a.shape, a.dtype) for a in lands],
                   _sds((8, 128), f32)),
        in_specs=[_HBM] * (2 * n) + [pl.BlockSpec(memory_space=pl.ANY)],
        out_specs=(_SEM, _SEM, _SEM, *[_HBM] * (2 * n), pl.BlockSpec(memory_space=pltpu.VMEM)),
        input_output_aliases={i: 3 + i for i in range(2 * n)},
        compiler_params=pltpu.CompilerParams(has_side_effects=_EFFECT),
    )(*srcs, *lands, after)
    return {"kinds": kinds, "sems": outs[:3], "srcs": outs[3:3 + n], "lands": outs[3 + n:3 + 2 * n], "token": outs[-1]}


def exchange_wait(handle, after, *, name):
    kinds = handle["kinds"]
    n = len(kinds)

    def body(*refs):
        in_refs, land_refs = refs[:n], refs[n:2 * n]
        send_sems, recv_sems, local_sems = refs[2 * n:2 * n + 3]
        for cp in _remote_copies(kinds, in_refs, land_refs, send_sems, recv_sems, arriving=True):
            cp.wait_recv()
        for cp in _remote_copies(kinds, in_refs, land_refs, send_sems, recv_sems, arriving=False):
            cp.wait_send()
        for cp in _local_copies(kinds, in_refs, land_refs, local_sems):
            cp.wait()

    srcs, lands = list(handle["srcs"]), list(handle["lands"])
    after = list(after) if isinstance(after, (list, tuple)) else [after]
    outs = pl.pallas_call(
        body, name=name,
        out_shape=tuple(pltpu.HBM(a.shape, a.dtype) for a in srcs + lands),
        in_specs=[_HBM] * (2 * n) + [_SEM] * 3 + [pl.BlockSpec(memory_space=pl.ANY)] * len(after),
        out_specs=tuple([_HBM] * (2 * n)),
        input_output_aliases={i: i for i in range(2 * n)},
        compiler_params=pltpu.CompilerParams(has_side_effects=_EFFECT),
    )(*srcs, *lands, *handle["sems"], *after)
    return list(outs[n:])


CHIP_MASKS = (2, 4, 6)


def _g2_first(in_refs, land_refs, send_sems, recv_sems, *, masks, arriving):
    me = _my_index()
    copies = []
    for i in range(len(land_refs)):
        for j, k in enumerate(masks):
            dev, idx = _peer(k)
            dst = land_refs[i].at[idx if arriving else me]
            copies.append(pltpu.make_async_remote_copy(
                src_ref=dst if in_refs is None else in_refs[i], dst_ref=dst,
                send_sem=send_sems.at[i * len(masks) + j], recv_sem=recv_sems.at[i * len(masks) + j],
                device_id=dev, device_id_type=MESH))
    return copies


def _g2_forward(land_refs, fwd_send, fwd_recv, *, arriving):
    sibling, _ = _peer(1)
    copies = []
    for i in range(len(land_refs)):
        for j, k in enumerate(CHIP_MASKS):
            _, idx = _peer(k | 1 if arriving else k)
            copies.append(pltpu.make_async_remote_copy(
                src_ref=land_refs[i].at[idx], dst_ref=land_refs[i].at[idx],
                send_sem=fwd_send.at[i * 3 + j], recv_sem=fwd_recv.at[i * 3 + j], device_id=sibling,
                device_id_type=MESH))
    return copies


def gather2_start(arrays, after, *, name):
    n = len(arrays)

    def body(*refs):
        in_refs, land_refs = refs[:n], refs[n:2 * n]
        ici_send, ici_recv, d2d_send, d2d_recv, local_sems = refs[2 * n + 1:2 * n + 6]
        token = refs[-1]
        ici = _g2_first(in_refs, land_refs, ici_send, ici_recv, masks=CHIP_MASKS, arriving=False)
        d2d = _g2_first(in_refs, land_refs, d2d_send, d2d_recv, masks=(1,), arriving=False)
        for cp in _local_copies(["gather"] * n, in_refs, land_refs, local_sems) + ici + d2d:
            cp.start()
        token[...] = jnp.zeros_like(token)

    srcs = [pltpu.with_memory_space_constraint(a, pltpu.HBM) for a in arrays]
    lands = [pltpu.with_memory_space_constraint(lax.empty((N_DEV,) + a.shape, a.dtype), pltpu.HBM) for a in arrays]
    sem = pltpu.SemaphoreType.DMA
    outs = pl.pallas_call(
        body, name=name,
        out_shape=(sem((3 * n,)), sem((3 * n,)), sem((n,)), sem((n,)), sem((n,)),
                   *[pltpu.HBM(a.shape, a.dtype) for a in srcs], *[pltpu.HBM(a.shape, a.dtype) for a in lands],
                   _sds((8, 128), f32)),
        in_specs=[_HBM] * (2 * n) + [pl.BlockSpec(memory_space=pl.ANY)],
        out_specs=(*[_SEM] * 5, *[_HBM] * (2 * n), pl.BlockSpec(memory_space=pltpu.VMEM)),
        input_output_aliases={i: 5 + i for i in range(2 * n)},
        compiler_params=pltpu.CompilerParams(has_side_effects=_EFFECT),
    )(*srcs, *lands, after)
    return {"n": n, "sems": outs[:5], "srcs": outs[5:5 + n], "lands": outs[5 + n:5 + 2 * n], "token": outs[-1]}


def gather2_forward(handle, after, *, name):
    n = handle["n"]

    def body(*refs):
        land_refs = refs[:n]
        ici_recv = refs[n]
        fwd_send, fwd_recv = refs[n + 2:n + 4]
        for cp in _g2_first(None, land_refs, fwd_send, ici_recv, masks=CHIP_MASKS, arriving=True):
            cp.wait_recv()
        for cp in _g2_forward(land_refs, fwd_send, fwd_recv, arriving=False):
            cp.start()

    lands = list(handle["lands"])
    sem = pltpu.SemaphoreType.DMA
    outs = pl.pallas_call(
        body, name=name,
        out_shape=(sem((3 * n,)), sem((3 * n,)), *[pltpu.HBM(a.shape, a.dtype) for a in lands]),
        in_specs=[_HBM] * n + [_SEM, pl.BlockSpec(memory_space=pl.ANY)],
        out_specs=(_SEM, _SEM, *[_HBM] * n),
        input_output_aliases={i: 2 + i for i in range(n)},
        compiler_params=pltpu.CompilerParams(has_side_effects=_EFFECT),
    )(*lands, handle["sems"][1], after)
    return dict(handle, fwd=outs[:2], lands=outs[2:])


def gather2_wait(handle, after, *, name):
    n = handle["n"]

    def body(*refs):
        in_refs, land_refs = refs[:n], refs[n:2 * n]
        ici_send, d2d_send, d2d_recv, local_sems, fwd_send, fwd_recv = refs[2 * n:2 * n + 6]
        for cp in _g2_first(in_refs, land_refs, d2d_send, d2d_recv, masks=(1,), arriving=True):
            cp.wait_recv()
        for cp in _g2_forward(land_refs, fwd_send, fwd_recv, arriving=True):
            cp.wait_recv()
        for cp in (_g2_first(in_refs, land_refs, ici_send, fwd_recv, masks=CHIP_MASKS, arriving=False)
                   + _g2_first(in_refs, land_refs, d2d_send, d2d_recv, masks=(1,), arriving=False)
                   + _g2_forward(land_refs, fwd_send, fwd_recv, arriving=False)):
            cp.wait_send()
        for cp in _local_copies(["gather"] * n, in_refs, land_refs, local_sems):
            cp.wait()

    srcs, lands = list(handle["srcs"]), list(handle["lands"])
    s = handle["sems"]
    outs = pl.pallas_call(
        body, name=name,
        out_shape=tuple(pltpu.HBM(a.shape, a.dtype) for a in srcs + lands),
        in_specs=[_HBM] * (2 * n) + [_SEM] * 6 + [pl.BlockSpec(memory_space=pl.ANY)],
        out_specs=tuple([_HBM] * (2 * n)),
        input_output_aliases={i: i for i in range(2 * n)},
        compiler_params=pltpu.CompilerParams(has_side_effects=_EFFECT),
    )(*srcs, *lands, s[0], s[2], s[3], s[4], *handle["fwd"], after)
    return list(outs[n:])


def adamw(slots, w, m, v, *, name, layer=None, into=None):
    R, C = w.shape[-2:]
    tr = _tile(R, (256, 128, 64, 32, 16, 8))
    c1 = 1.0 - ADAM_B1 ** ADAM_STEP
    c2 = 1.0 - ADAM_B2 ** ADAM_STEP
    extra = [] if into is None else list(into)

    def body(s_ref, w_ref, m_ref, v_ref, *refs):
        g_ref, d_ref, m2_ref, v2_ref = refs[len(extra):]
        g = s_ref[0].astype(f32)
        for d in range(1, N_DEV):
            g = g + s_ref[d].astype(f32)
        m2 = ADAM_B1 * m_ref[...] + (1.0 - ADAM_B1) * g
        v2 = ADAM_B2 * v_ref[...] + (1.0 - ADAM_B2) * (g * g)
        g_ref[...] = g
        m2_ref[...] = m2
        v2_ref[...] = v2
        d_ref[...] = -ADAM_LR * ((m2 / c1) / (jnp.sqrt(v2 / c2) + ADAM_EPS) + ADAM_WD * w_ref[...])

    if layer is None:
        blk = pl.BlockSpec((tr, C), lambda i: (i, 0))
    else:
        blk = pl.BlockSpec((None, tr, C), lambda i: (layer, i, 0))
    return pl.pallas_call(
        body, name=name, grid=(R // tr,),
        in_specs=[pl.BlockSpec((N_DEV, tr, C), lambda i: (0, i, 0)), blk, blk, blk]
        + [pl.BlockSpec(memory_space=pl.ANY)] * len(extra),
        out_specs=[blk] * 4,
        out_shape=[_sds(w.shape, f32)] * 4,
        input_output_aliases={4 + j: j for j in range(len(extra))},
        compiler_params=_params(("parallel",)),
    )(slots, w, m, v, *extra)


WEIGHTS = ("norm_gains", "mem_norm", "w_in", "w_mem_kv", "w_out", "w_pool", "pool_scale", "kv_norm", "w_kv",
           "w_gate_up", "w_down")
LAYER_MATS = ("w_in", "w_mem_kv", "w_out", "w_gate_up", "w_down")
POOL_SHARD = MAIN_W // N_DEV
KV_SHARD = 2 * MAIN_W // N_DEV
LOOKAHEAD = 2
TWO_LEVEL_LAYERS = (0, 1)


def _pack_small(gains, pscale):
    lead = gains.shape[:-3]
    g = gains.reshape(lead + (16, 128))
    p = jnp.zeros(lead + (8, 128), f32).at[..., :2, :POOL_SHARD].set(pscale)
    return jnp.concatenate([g, p], axis=-2)


def _unpack_small(a):
    return a[:16].reshape(4, 4, 128), a[16:18, :POOL_SHARD]


def _pack_repl(mem_norm, kv_norm):
    return jnp.concatenate([mem_norm, kv_norm.reshape(1, D_MODEL), jnp.zeros((3, D_MODEL), f32)], axis=0)


def _unpack_repl(a):
    return a[:4], a[4]


def kernel(x, mem, positions, norm_gains, mem_norm, w_in, w_mem_kv, w_out, w_pool, pool_scale, kv_norm, w_kv, w_gate_up, w_down, loss_target, m_norm_gains, m_mem_norm, m_w_in, m_w_mem_kv, m_w_out, m_w_pool, m_pool_scale, m_kv_norm, m_w_kv, m_w_gate_up, m_w_down, v_norm_gains, v_mem_norm, v_w_in, v_w_mem_kv, v_w_out, v_w_pool, v_pool_scale, v_kv_norm, v_w_kv, v_w_gate_up, v_w_down):
    w = dict(norm_gains=norm_gains, mem_norm=mem_norm, w_in=w_in, w_mem_kv=w_mem_kv, w_out=w_out, w_pool=w_pool,
             pool_scale=pool_scale, kv_norm=kv_norm, w_kv=w_kv, w_gate_up=w_gate_up, w_down=w_down)
    m = dict(norm_gains=m_norm_gains, mem_norm=m_mem_norm, w_in=m_w_in, w_mem_kv=m_w_mem_kv, w_out=m_w_out,
             w_pool=m_w_pool, pool_scale=m_pool_scale, kv_norm=m_kv_norm, w_kv=m_w_kv, w_gate_up=m_w_gate_up,
             w_down=m_w_down)
    v = dict(norm_gains=v_norm_gains, mem_norm=v_mem_norm, w_in=v_w_in, w_mem_kv=v_w_mem_kv, w_out=v_w_out,
             w_pool=v_w_pool, pool_scale=v_pool_scale, kv_norm=v_kv_norm, w_kv=v_w_kv, w_gate_up=v_w_gate_up,
             w_down=v_w_down)

    def transposed_view(d):
        d = dict(d)
        d["w_gate_up"] = jnp.swapaxes(d["w_gate_up"], 1, 2)
        d["w_kv"] = jnp.swapaxes(d["w_kv"], 0, 1)
        return d

    wv, mv, vv = transposed_view(w), transposed_view(m), transposed_view(v)

    small = _pack_small(norm_gains, pool_scale)
    (gsmall,) = exchange([(small, "gather")], name="gather_small")
    P = {"norm_gains": jnp.moveaxis(gsmall[:, :16].reshape(N_DEV, 4, 4, 128), 0, 2).reshape(4, 4, D_MODEL),
         "pool_scale": jnp.moveaxis(gsmall[:, 16:18, :POOL_SHARD], 0, 1).reshape(2, MAIN_W),
         "mem_norm": mem_norm, "kv_norm": kv_norm, "w_pool": w_pool}

    PARTS = {"mix": ("w_in", "w_mem_kv", "w_out"), "ffn": ("w_gate_up", "w_down"), "gu": ("w_gate_up",),
             "down": ("w_down",)}

    def parts_of(l):
        return ("mix", "gu", "down") if l == 0 else ("mix", "ffn")

    def part_items(l, part):
        items = [(wv[k][l].astype(bf16), "gather") for k in PARTS[part]]
        if part == "ffn" and l == N_A_LAYERS - 1:
            items.append((wv["w_kv"].astype(bf16), "gather"))
        return items

    handles = {}

    def start_layer(l, after):
        for part in parts_of(l):
            if l in TWO_LEVEL_LAYERS:
                handles[l, part] = gather2_start([a for a, _ in part_items(l, part)], after,
                                                 name=f"gather_start_{part}_l{l}")
            else:
                handles[l, part] = exchange_start(part_items(l, part), after, name=f"gather_start_{part}_l{l}")
            after = handles[l, part]["token"]
        return after

    token = gsmall
    for l in range(LOOKAHEAD):
        token = start_layer(l, token)
    landed = {}

    def layer_weights(l, part, after):
        if part not in parts_of(l):
            if part == "down":
                return {}, None
            part = "ffn"
        if l == 0 and part == "mix":
            after = token
        if l in TWO_LEVEL_LAYERS:
            passed = gather2_forward(handles[l, part], after, name=f"gather_forward_{part}_l{l}")
            got = gather2_wait(passed, after, name=f"gather_wait_{part}_l{l}")
        else:
            got = exchange_wait(handles[l, part], after, name=f"gather_wait_{part}_l{l}")
        landed[l, part] = got
        started = None
        if part == "mix" and l + LOOKAHEAD < DEPTH:
            started = start_layer(l + LOOKAHEAD, got[0])
        W = {k: g.reshape(-1, g.shape[-1]) for k, g in zip(PARTS[part], got)}
        return W, started

    def kv_weight(after):
        g = landed[N_A_LAYERS - 1, "ffn"][len(PARTS["ffn"])]
        return g.reshape(2 * MAIN_W, D_MODEL)

    ghandles = {}

    def emit_grads(l, part, gw):
        items = [(gw[k].reshape((N_DEV, -1) + gw[k].shape[-1:]), "scatter") for k in PARTS[part]]
        if part == "mix" and l == N_A_LAYERS:
            items.append((gw["w_kv"].reshape(N_DEV, KV_SHARD, D_MODEL), "scatter"))
        if part == "mix" and l < N_A_LAYERS:
            items.append((gw["w_pool"], "gather"))
        ghandles[l, part] = exchange_start(items, gsmall, name=f"scatter_start_{part}_l{l}")
        return ghandles[l, part]["token"]

    sq, grad_x, GS, emitted = local_step(x, mem, positions, loss_target, P, layer_weights, kv_weight, emit_grads)
    loss = lax.psum(0.5 * sq / D_MODEL, ("x", "y", "c"))

    def pool3(a):
        return a.reshape(N_A_LAYERS, MAIN_W, POOL_GROUP)

    out = {}
    after = [emitted]

    def finish_layer(l, after):
        for part in ("ffn", "mix"):
            got = exchange_wait(ghandles[l, part], after, name=f"scatter_wait_{part}_l{l}")
            after = []
            for k, slots in zip(PARTS[part], got):
                out[k] = adamw(slots, wv[k], mv[k], vv[k], name=f"adamw_{k}_l{l}", layer=l, into=out.get(k))
                after.append(out[k][0])
            if part == "mix" and l == N_A_LAYERS:
                out["w_kv"] = adamw(got[-1], wv["w_kv"], mv["w_kv"], vv["w_kv"], name="adamw_w_kv")
                after.append(out["w_kv"][0])
            if part == "mix" and l < N_A_LAYERS:
                out["w_pool"] = adamw(got[-1], pool3(w_pool), pool3(m_w_pool), pool3(v_w_pool), name=f"adamw_w_pool_l{l}",
                                      layer=l, into=out.get("w_pool"))
                after.append(out["w_pool"][0])
        return after

    for l in reversed(range(1, DEPTH)):
        after = finish_layer(l, after)

    gs = _pack_small(jnp.moveaxis(GS["norm_gains"].reshape(4, 4, N_DEV, 128), 2, 0),
                     jnp.moveaxis(GS["pool_scale"].reshape(2, N_DEV, POOL_SHARD), 1, 0))
    parts_small, parts_repl = exchange(
        [(gs, "scatter"), (_pack_repl(GS["mem_norm"], GS["kv_norm"]), "gather")],
        name="exchange_small_grads", after=after)
    finish_layer(0, [parts_small])
    out["w_gate_up"] = [jnp.swapaxes(r, 1, 2) for r in out["w_gate_up"]]
    out["w_kv"] = [jnp.swapaxes(r, 0, 1) for r in out["w_kv"]]
    out["w_pool"] = [r.reshape(w_pool.shape) for r in out["w_pool"]]

    res = adamw(parts_small, small, _pack_small(m_norm_gains, m_pool_scale), _pack_small(v_norm_gains, v_pool_scale),
                name="adamw_small")
    out["norm_gains"], out["pool_scale"] = zip(*[_unpack_small(r) for r in res])
    res = adamw(parts_repl, _pack_repl(mem_norm, kv_norm), _pack_repl(m_mem_norm, m_kv_norm),
                _pack_repl(v_mem_norm, v_kv_norm), name="adamw_repl")
    out["mem_norm"], out["kv_norm"] = zip(*[_unpack_repl(r) for r in res])

    return (loss, grad_x, *[out[k][0] for k in WEIGHTS], *[out[k][1] for k in WEIGHTS],
            *[out[k][2] for k in WEIGHTS], *[out[k][3] for k in WEIGHTS])
```

```python
import numpy as np
import jax
import jax.numpy as jnp
from jax import lax
from jax.experimental import pallas as pl
from jax.experimental.pallas import tpu as pltpu

f32 = jnp.float32
bf16 = jnp.bfloat16

D_MODEL = 1024
SEQ = 2048
DEPTH = 4
N_MEM = 256
HEAD_DIM = 64
N_MEM_HEADS = 4
MEM_W = 256
MAIN_W = 768
POOL_WINDOWS = (2, 4, 8, 16)
POOL_GROUP = 192
POOL_HALO = 16
DIL_PATTERNS = ((128, 1), (512, 4), (2048, 16))
N_GROUPS = 3
GROUP_W = 256
BAND = 128
N_A_LAYERS = 2
D_FF = 2816
ROPE_THETA = 10000.0
EPS = 1e-6
NEG = -1e30
SCALE = HEAD_DIM ** -0.5
N_DEV = 8

ADAM_LR = 0.001
ADAM_B1 = 0.9
ADAM_B2 = 0.999
ADAM_EPS = 1e-08
ADAM_WD = 0.01
ADAM_STEP = 10

VMEM_LIMIT_BYTES = 56 * 1024 * 1024
MESH = pl.DeviceIdType.MESH

NN = (((1,), (0,)), ((), ()))
NT = (((1,), (1,)), ((), ()))
TN = (((0,), (0,)), ((), ()))


def _params(sem=None):
    return pltpu.CompilerParams(dimension_semantics=sem, vmem_limit_bytes=VMEM_LIMIT_BYTES)


def _tile(n, cands):
    for c in cands:
        if n % c == 0:
            return c
    return n


def _sds(shape, dtype):
    return jax.ShapeDtypeStruct(tuple(shape), dtype)


def _rms_r(v):
    return lax.rsqrt(jnp.mean(v * v, axis=-1, keepdims=True) + EPS)


def rms_matmul(x, gain, w, *, name, out_dtype, transposed=False, after=None):
    M, K = x.shape
    N = w.shape[0] if transposed else w.shape[1]
    tm = min(2048, M)
    tn = _tile(N, (512, 256, 128))
    order = [] if after is None else [after]

    def body(x_ref, g_ref, w_ref, *refs):
        z_ref, h_ref = refs[len(order):]

        @pl.when(pl.program_id(1) == 0)
        def _():
            xv = x_ref[...]
            h_ref[...] = (xv * _rms_r(xv) * g_ref[...]).astype(bf16)

        z_ref[...] = lax.dot_general(h_ref[...], w_ref[...], NT if transposed else NN,
                                     preferred_element_type=f32).astype(z_ref.dtype)

    w_spec = pl.BlockSpec((tn, K), lambda i, j: (j, 0)) if transposed else pl.BlockSpec((K, tn), lambda i, j: (0, j))
    return pl.pallas_call(
        body, name=name, grid=(M // tm, N // tn),
        in_specs=[pl.BlockSpec((tm, K), lambda i, j: (i, 0)),
                  pl.BlockSpec((1, K), lambda i, j: (0, 0)),
                  w_spec] + [pl.BlockSpec(memory_space=pl.ANY)] * len(order),
        out_specs=[pl.BlockSpec((tm, tn), lambda i, j: (i, j)), pl.BlockSpec((tm, K), lambda i, j: (i, 0))],
        out_shape=[_sds((M, N), out_dtype), _sds((M, K), bf16)],
        compiler_params=_params(("parallel", "arbitrary")),
    )(x, gain, w, *order)


def matmul_rms_res(a, w, gain, res, *, name):
    M, K = a.shape
    N = w.shape[1]
    tm = min(1024, M)

    def body(a_ref, w_ref, g_ref, r_ref, y_ref, x_ref):
        y = jnp.dot(a_ref[...], w_ref[...], preferred_element_type=f32)
        y_ref[...] = y.astype(bf16)
        x_ref[...] = r_ref[...] + y * _rms_r(y) * g_ref[...]

    row = pl.BlockSpec((tm, N), lambda i: (i, 0))
    return pl.pallas_call(
        body, name=name, grid=(M // tm,),
        in_specs=[pl.BlockSpec((tm, K), lambda i: (i, 0)),
                  pl.BlockSpec((K, N), lambda i: (0, 0)),
                  pl.BlockSpec((1, N), lambda i: (0, 0)),
                  row],
        out_specs=[row, row],
        out_shape=[_sds((M, N), bf16), _sds((M, N), f32)],
        compiler_params=_params(("parallel",)),
    )(a, w, gain, res)


def matmul(a, b, dims, *, name, out_dtype):
    if dims is TN:
        K, M = a.shape
        tm = _tile(M, (512, 256, 128))
        a_spec = pl.BlockSpec((K, tm), lambda i: (0, i))
    else:
        M, K = a.shape
        tm = _tile(M, (1024, 512, 256, 128))
        a_spec = pl.BlockSpec((tm, K), lambda i: (i, 0))
    N = b.shape[0] if dims is NT else b.shape[1]

    def body(a_ref, b_ref, o_ref):
        o_ref[...] = lax.dot_general(a_ref[...].astype(bf16), b_ref[...].astype(bf16), dims,
                                     preferred_element_type=f32).astype(o_ref.dtype)

    return pl.pallas_call(
        body, name=name, grid=(M // tm,),
        in_specs=[a_spec, pl.BlockSpec(b.shape, lambda i: (0, 0))],
        out_specs=pl.BlockSpec((tm, N), lambda i: (i, 0)),
        out_shape=_sds((M, N), out_dtype),
        compiler_params=_params(("parallel",)),
    )(a, b)


def rms_gate_up(x, gain, wt, *, name):
    M, K = x.shape
    tm = min(2048, M)
    tn = _tile(D_FF, (256, 128))
    nj = D_FF // tn

    def body(x_ref, gn_ref, wg_ref, wu_ref, g_ref, u_ref, a_ref, h_ref):
        @pl.when(pl.program_id(1) == 0)
        def _():
            xv = x_ref[...]
            h_ref[...] = (xv * _rms_r(xv) * gn_ref[...]).astype(bf16)

        h = h_ref[...]
        g = lax.dot_general(h, wg_ref[...], NT, preferred_element_type=f32).astype(bf16)
        u = lax.dot_general(h, wu_ref[...], NT, preferred_element_type=f32).astype(bf16)
        g_ref[...] = g
        u_ref[...] = u
        a_ref[...] = g * (1.0 / (1.0 + jnp.exp(-g))) * u

    col = pl.BlockSpec((tm, tn), lambda i, j: (i, j))
    return pl.pallas_call(
        body, name=name, grid=(M // tm, nj),
        in_specs=[pl.BlockSpec((tm, K), lambda i, j: (i, 0)),
                  pl.BlockSpec((1, K), lambda i, j: (0, 0)),
                  pl.BlockSpec((tn, K), lambda i, j: (j, 0)),
                  pl.BlockSpec((tn, K), lambda i, j: (j + nj, 0))],
        out_specs=[col, col, col, pl.BlockSpec((tm, K), lambda i, j: (i, 0))],
        out_shape=[_sds((M, D_FF), bf16)] * 3 + [_sds((M, K), bf16)],
        compiler_params=_params(("parallel", "arbitrary")),
    )(x, gain, wt, wt)


def _rms_bwd_math(yv, gain, dn):
    r = _rms_r(yv)
    q = dn * gain
    dy = r * q - yv * (r * r * r) * jnp.mean(q * yv, axis=-1, keepdims=True)
    return dy, jnp.sum(dn * yv * r, axis=0, keepdims=True)


def _accumulate(ref, val):
    @pl.when(pl.program_id(0) == 0)
    def _():
        ref[...] = jnp.zeros_like(ref)

    ref[...] += val


def down_bwd(y, gain, dn, w_down, g, u, *, name, after=None):
    M, K = y.shape
    tm = min(512, M)
    order = [] if after is None else [after]

    def body(y_ref, gn_ref, dn_ref, w_ref, g_ref, u_ref, *refs):
        dy_ref, o_ref, dg_ref = refs[len(order):]
        dy, dgain = _rms_bwd_math(y_ref[...].astype(f32), gn_ref[...], dn_ref[...])
        dy = dy.astype(bf16)
        dy_ref[...] = dy
        _accumulate(dg_ref, dgain)
        da = lax.dot_general(dy, w_ref[...], NT, preferred_element_type=f32).astype(bf16)
        g = g_ref[...]
        s = 1.0 / (1.0 + jnp.exp(-g))
        o_ref[:, :D_FF] = da * u_ref[...] * s * (1.0 + g * (1.0 - s))
        o_ref[:, D_FF:] = da * g * s

    row = pl.BlockSpec((tm, K), lambda i: (i, 0))
    vec = pl.BlockSpec((1, K), lambda i: (0, 0))
    wide = pl.BlockSpec((tm, D_FF), lambda i: (i, 0))
    return pl.pallas_call(
        body, name=name, grid=(M // tm,),
        in_specs=[row, vec, row, pl.BlockSpec((D_FF, K), lambda i: (0, 0)), wide, wide]
        + [pl.BlockSpec(memory_space=pl.ANY)] * len(order),
        out_specs=[row, pl.BlockSpec((tm, 2 * D_FF), lambda i: (i, 0)), vec],
        out_shape=[_sds((M, K), bf16), _sds((M, 2 * D_FF), bf16), _sds((1, K), f32)],
        compiler_params=_params(("arbitrary",)),
    )(y, gain, dn, w_down, g, u, *order)


def rms_bwd_matmul(y, gain, dn, w, dims, *, name, after=None):
    M, K = y.shape
    N = w.shape[0] if dims is NT else w.shape[1]
    tm = min(1024, M)
    order = [] if after is None else [after]

    def body(y_ref, gn_ref, dn_ref, w_ref, *refs):
        dy_ref, o_ref, dg_ref = refs[len(order):]
        dy, dgain = _rms_bwd_math(y_ref[...].astype(f32), gn_ref[...], dn_ref[...].astype(f32))
        dy = dy.astype(bf16)
        dy_ref[...] = dy
        _accumulate(dg_ref, dgain)
        o_ref[...] = lax.dot_general(dy, w_ref[...], dims, preferred_element_type=f32).astype(bf16)

    row = pl.BlockSpec((tm, K), lambda i: (i, 0))
    vec = pl.BlockSpec((1, K), lambda i: (0, 0))
    return pl.pallas_call(
        body, name=name, grid=(M // tm,),
        in_specs=[row, vec, row, pl.BlockSpec(w.shape, lambda i: (0, 0))]
        + [pl.BlockSpec(memory_space=pl.ANY)] * len(order),
        out_specs=[row, pl.BlockSpec((tm, N), lambda i: (i, 0)), vec],
        out_shape=[_sds((M, K), bf16), _sds((M, N), bf16), _sds((1, K), f32)],
        compiler_params=_params(("arbitrary",)),
    )(y, gain, dn, w, *order)


def matmul_rms_bwd(a, b, dims, y, gain, res, *, name):
    M, K = a.shape
    N = y.shape[1]
    tm = min(512, M)

    def body(a_ref, b_ref, y_ref, gn_ref, r_ref, dx_ref, dg_ref):
        dn = lax.dot_general(a_ref[...].astype(bf16), b_ref[...], dims, preferred_element_type=f32)
        dy, dgain = _rms_bwd_math(y_ref[...], gn_ref[...], dn)
        dx_ref[...] = dy + r_ref[...]
        _accumulate(dg_ref, dgain)

    row = pl.BlockSpec((tm, N), lambda i: (i, 0))
    vec = pl.BlockSpec((1, N), lambda i: (0, 0))
    return pl.pallas_call(
        body, name=name, grid=(M // tm,),
        in_specs=[pl.BlockSpec((tm, K), lambda i: (i, 0)), pl.BlockSpec(b.shape, lambda i: (0, 0)), row, vec, row],
        out_specs=[row, vec],
        out_shape=[_sds((M, N), f32), _sds((1, N), f32)],
        compiler_params=_params(("arbitrary",)),
    )(a, b, y, gain, res)


def rms_bwd(y, gain, dn, res, *, name, out_dtype, after=None):
    M, N = y.shape
    tm = min(512, M)
    has_res = res is not None
    order = [] if after is None else [after]

    def body(*refs):
        y_ref, g_ref, dn_ref = refs[:3]
        r_ref = refs[3] if has_res else None
        dy_ref, dg_ref = refs[-2:]
        dy, dgain = _rms_bwd_math(y_ref[...].astype(f32), g_ref[...], dn_ref[...].astype(f32))
        if has_res:
            dy = dy + r_ref[...]
        dy_ref[...] = dy.astype(dy_ref.dtype)
        _accumulate(dg_ref, dgain)

    row = pl.BlockSpec((tm, N), lambda i: (i, 0))
    vec = pl.BlockSpec((1, N), lambda i: (0, 0))
    args = [y, gain, dn] + ([res] if has_res else []) + order
    return pl.pallas_call(
        body, name=name, grid=(M // tm,),
        in_specs=[row, vec, row] + ([row] if has_res else []) + [pl.BlockSpec(memory_space=pl.ANY)] * len(order),
        out_specs=[row, vec],
        out_shape=[_sds((M, N), out_dtype), _sds((1, N), f32)],
        compiler_params=_params(("arbitrary",)),
    )(*args)


def loss_head(x, target, *, name):
    M, N = x.shape
    tm = min(512, M)

    def body(x_ref, t_ref, dx_ref, l_ref):
        e = x_ref[...] - t_ref[...]
        dx_ref[...] = e * (1.0 / N)

        @pl.when(pl.program_id(0) == 0)
        def _():
            l_ref[...] = jnp.zeros_like(l_ref)

        l_ref[...] += jnp.sum(jnp.sum(e * e, axis=0, keepdims=True), axis=1, keepdims=True)

    row = pl.BlockSpec((tm, N), lambda i: (i, 0))
    return pl.pallas_call(
        body, name=name, grid=(M // tm,),
        in_specs=[row, row],
        out_specs=[row, pl.BlockSpec((8, 128), lambda i: (0, 0))],
        out_shape=[_sds((M, N), f32), _sds((8, 128), f32)],
        compiler_params=_params(("arbitrary",)),
    )(x, target)


def _pool_select(a1, a2, a3, a4):
    col = lax.broadcasted_iota(jnp.int32, (1, MAIN_W), 1) // POOL_GROUP
    return jnp.where(col == 0, a1, jnp.where(col == 1, a2, jnp.where(col == 2, a3, a4)))


def _pool_count(t):
    col = lax.broadcasted_iota(jnp.int32, (1, MAIN_W), 1) // POOL_GROUP
    win = jnp.where(col == 0, 2, jnp.where(col == 1, 4, jnp.where(col == 2, 8, 16)))
    return jnp.minimum(t + 1, win).astype(f32)


def pool_fwd(z, wbd, scale, *, name):
    M = z.shape[0]
    tm = 256
    nper = SEQ // tm
    hb = tm // POOL_HALO

    def body(zc_ref, zh_ref, w_ref, s_ref, p_ref, y_ref):
        i = pl.program_id(0)
        seq_blk = i % nper
        halo = jnp.where(seq_blk == 0, 0.0, zh_ref[...])
        u = zc_ref[...]
        ext = jnp.concatenate([halo, u], axis=0)
        a1 = ext + pltpu.roll(ext, 1, 0)
        a2 = a1 + pltpu.roll(a1, 2, 0)
        a3 = a2 + pltpu.roll(a2, 4, 0)
        a4 = a3 + pltpu.roll(a3, 8, 0)
        sums = _pool_select(a1, a2, a3, a4)[POOL_HALO:]
        t = seq_blk * tm + lax.broadcasted_iota(jnp.int32, (tm, 1), 0)
        p = (sums / _pool_count(t) - u).astype(bf16)
        p_ref[...] = p
        y_ref[...] = (jnp.dot(p, w_ref[...], preferred_element_type=f32) * s_ref[...]).astype(bf16)

    return pl.pallas_call(
        body, name=name, grid=(M // tm,),
        in_specs=[pl.BlockSpec((tm, MAIN_W), lambda i: (i, 0)),
                  pl.BlockSpec((POOL_HALO, MAIN_W), lambda i: (jnp.maximum(i * hb - 1, 0), 0)),
                  pl.BlockSpec((MAIN_W, MAIN_W), lambda i: (0, 0)),
                  pl.BlockSpec((1, MAIN_W), lambda i: (0, 0))],
        out_specs=[pl.BlockSpec((tm, MAIN_W), lambda i: (i, 0)),
                   pl.BlockSpec((tm, MAIN_W), lambda i: (i, 0))],
        out_shape=[_sds((M, MAIN_W), bf16), _sds((M, D_MODEL), bf16)],
        compiler_params=_params(("parallel",)),
    )(z, z, wbd, scale)


def pool_bwd(dyc, p, wbd, scale, *, name):
    M = p.shape[0]
    tm = 256
    nper = SEQ // tm
    hb = tm // POOL_HALO
    last_hb = M // POOL_HALO - 1

    def body(dy_ref, dyh_ref, p_ref, w_ref, s_ref, dz_ref, dw_ref, ds_ref):
        i = pl.program_id(0)
        seq_blk = i % nper
        dy = dy_ref[...].astype(f32)
        pv = p_ref[...]
        w = w_ref[...]
        sc = s_ref[...]

        @pl.when(i == 0)
        def _():
            dw_ref[...] = jnp.zeros_like(dw_ref)
            ds_ref[...] = jnp.zeros_like(ds_ref)

        v = jnp.dot(pv, w, preferred_element_type=f32)
        ds_ref[...] += jnp.sum(dy * v, axis=0, keepdims=True)
        dv = (dy * sc).astype(bf16)
        dw_ref[...] += lax.dot_general(pv, dv, TN, preferred_element_type=f32)
        dp = lax.dot_general(dv, w, NT, preferred_element_type=f32)
        dvh = jnp.where(seq_blk == nper - 1, 0.0, dyh_ref[...].astype(f32) * sc).astype(bf16)
        dph = lax.dot_general(dvh, w, NT, preferred_element_type=f32)
        ext = jnp.concatenate([dp, dph], axis=0)
        n = tm + POOL_HALO
        t = seq_blk * tm + lax.broadcasted_iota(jnp.int32, (n, 1), 0)
        e = ext / _pool_count(t)
        b1 = e + pltpu.roll(e, n - 1, 0)
        b2 = b1 + pltpu.roll(b1, n - 2, 0)
        b3 = b2 + pltpu.roll(b2, n - 4, 0)
        b4 = b3 + pltpu.roll(b3, n - 8, 0)
        dz_ref[...] = (_pool_select(b1, b2, b3, b4)[:tm] - dp).astype(dz_ref.dtype)

    return pl.pallas_call(
        body, name=name, grid=(M // tm,),
        in_specs=[pl.BlockSpec((tm, MAIN_W), lambda i: (i, 0)),
                  pl.BlockSpec((POOL_HALO, MAIN_W), lambda i: (jnp.minimum((i + 1) * hb, last_hb), 0)),
                  pl.BlockSpec((tm, MAIN_W), lambda i: (i, 0)),
                  pl.BlockSpec((MAIN_W, MAIN_W), lambda i: (0, 0)),
                  pl.BlockSpec((1, MAIN_W), lambda i: (0, 0))],
        out_specs=[pl.BlockSpec((tm, MAIN_W), lambda i: (i, 0)),
                   pl.BlockSpec((MAIN_W, MAIN_W), lambda i: (0, 0)),
                   pl.BlockSpec((1, MAIN_W), lambda i: (0, 0))],
        out_shape=[_sds((M, D_MODEL), bf16), _sds((MAIN_W, MAIN_W), f32), _sds((1, MAIN_W), f32)],
        compiler_params=_params(("arbitrary",)),
    )(dyc, dyc, p, wbd, scale)


def _mem_heads(q, kv):
    first = _first_head()
    for pr in range(N_MEM_HEADS // 2):
        cols = slice(pr * PAIR_W, (pr + 1) * PAIR_W)
        qp = q[:, cols] * SCALE
        kp = kv[:, cols]
        vp = kv[:, MEM_W + pr * PAIR_W: MEM_W + (pr + 1) * PAIR_W]
        for hh in range(2):
            lm = first if hh == 0 else ~first
            qm = jnp.where(lm, qp, 0.0).astype(bf16)
            s = lax.dot_general(qm, kp, NT, preferred_element_type=f32)
            e = jnp.exp(s - jnp.max(s, axis=-1, keepdims=True))
            yield lm, qm, kp, vp, e, jnp.sum(e, axis=-1, keepdims=True)


def memattn_fwd(z, kvm, ycat, *, name, n_seq):
    M = z.shape[0]
    tq = 512
    nq = SEQ // tq

    def body(q_ref, kv_ref, _, o_ref):
        first = _first_head()
        outs = []
        for lm, _, _, vp, e, l in _mem_heads(q_ref[...], kv_ref[...]):
            outs.append(jnp.dot(e.astype(bf16), vp, preferred_element_type=f32) * (1.0 / l))
        pairs = [jnp.where(first, outs[2 * pr], outs[2 * pr + 1]) for pr in range(N_MEM_HEADS // 2)]
        o_ref[...] = jnp.concatenate(pairs, axis=1).astype(bf16)

    return pl.pallas_call(
        body, name=name, grid=(n_seq, nq),
        in_specs=[pl.BlockSpec((tq, MEM_W), lambda b, i: (b * nq + i, 3)),
                  pl.BlockSpec((N_MEM, 2 * MEM_W), lambda b, i: (b, 0)),
                  pl.BlockSpec(memory_space=pl.ANY)],
        out_specs=pl.BlockSpec((tq, MEM_W), lambda b, i: (b * nq + i, 3)),
        out_shape=_sds((M, D_MODEL), bf16),
        input_output_aliases={2: 0},
        compiler_params=_params(("parallel", "parallel")),
    )(z, kvm, ycat)


def memattn_bwd(z, kvm, dyc, dz, *, name, n_seq):
    M = z.shape[0]
    tq = 512
    nq = SEQ // tq

    def body(q_ref, kv_ref, dy_ref, _, dq_ref, dkv_ref):
        first = _first_head()
        dy = dy_ref[...].astype(f32)
        dqs, dks, dvs = [], [], []
        for h, (lm, qm, kp, vp, e, l) in enumerate(_mem_heads(q_ref[...], kv_ref[...])):
            pr = h // 2
            p = e * (1.0 / l)
            dym = jnp.where(lm, dy[:, pr * PAIR_W:(pr + 1) * PAIR_W], 0.0).astype(bf16)
            dp = lax.dot_general(dym, vp, NT, preferred_element_type=f32)
            ds = (p * (dp - jnp.sum(dp * p, axis=-1, keepdims=True))).astype(bf16)
            dqs.append(jnp.dot(ds, kp, preferred_element_type=f32) * SCALE)
            dk = lax.dot_general(ds, qm, TN, preferred_element_type=f32)
            dv = lax.dot_general(p.astype(bf16), dym, TN, preferred_element_type=f32)
            if h % 2 == 0:
                dks.append(dk)
                dvs.append(dv)
            else:
                dks[pr] = dks[pr] + dk
                dvs[pr] = dvs[pr] + dv
        pairs = [jnp.where(first, dqs[2 * pr], dqs[2 * pr + 1]) for pr in range(N_MEM_HEADS // 2)]
        dq_ref[...] = jnp.concatenate(pairs, axis=1).astype(bf16)

        @pl.when(pl.program_id(1) == 0)
        def _():
            dkv_ref[...] = jnp.zeros_like(dkv_ref)

        dkv_ref[...] += jnp.concatenate(dks + dvs, axis=1)

    return pl.pallas_call(
        body, name=name, grid=(n_seq, nq),
        in_specs=[pl.BlockSpec((tq, MEM_W), lambda b, i: (b * nq + i, 3)),
                  pl.BlockSpec((N_MEM, 2 * MEM_W), lambda b, i: (b, 0)),
                  pl.BlockSpec((tq, MEM_W), lambda b, i: (b * nq + i, 3)),
                  pl.BlockSpec(memory_space=pl.ANY)],
        out_specs=[pl.BlockSpec((tq, MEM_W), lambda b, i: (b * nq + i, 3)),
                   pl.BlockSpec((N_MEM, 2 * MEM_W), lambda b, i: (b, 0))],
        out_shape=[_sds((M, D_MODEL), bf16), _sds((n_seq * N_MEM, 2 * MEM_W), f32)],
        input_output_aliases={3: 0},
        compiler_params=_params(("parallel", "arbitrary")),
    )(z, kvm, dyc, dz)


def rope_tables(pos, *, name):
    M = pos.shape[0]
    tm = min(1024, M)
    half = HEAD_DIM // 2
    inv = ROPE_THETA ** (-np.arange(half, dtype=np.float64) / half)
    inv128 = jnp.asarray(np.tile(inv, 4)[None, :], f32)
    sign128 = jnp.asarray(np.tile(np.concatenate([-np.ones(half), np.ones(half)]), 2)[None, :], f32)

    def body(p_ref, f_ref, s_ref, cos_ref, sin_ref):
        ang = p_ref[...] * f_ref[...]
        cos_ref[...] = jnp.cos(ang)
        sin_ref[...] = jnp.sin(ang) * s_ref[...]

    return pl.pallas_call(
        body, name=name, grid=(M // tm,),
        in_specs=[pl.BlockSpec((tm, 1), lambda i: (i, 0)),
                  pl.BlockSpec((1, 128), lambda i: (0, 0)),
                  pl.BlockSpec((1, 128), lambda i: (0, 0))],
        out_specs=[pl.BlockSpec((tm, 128), lambda i: (i, 0)),
                   pl.BlockSpec((tm, 128), lambda i: (i, 0))],
        out_shape=[_sds((M, 128), f32), _sds((M, 128), f32)],
        compiler_params=_params(("parallel",)),
    )(pos, inv128, sign128)


def _swap_halves(x):
    w = x.shape[1]
    first = (lax.broadcasted_iota(jnp.int32, (1, w), 1) % HEAD_DIM) < (HEAD_DIM // 2)
    return jnp.where(first, pltpu.roll(x, w - HEAD_DIM // 2, 1), pltpu.roll(x, HEAD_DIM // 2, 1))


def rope_fwd(src, cos, sin, *, name):
    M = src.shape[0]
    tm = min(512, M)

    def body(x_ref, c_ref, s_ref, o_ref):
        x = x_ref[...].astype(f32)
        c = jnp.tile(c_ref[...], (1, MAIN_W // 128))
        s = jnp.tile(s_ref[...], (1, MAIN_W // 128))
        o_ref[...] = x * c + _swap_halves(x) * s

    return pl.pallas_call(
        body, name=name, grid=(M // tm,),
        in_specs=[pl.BlockSpec((tm, MAIN_W), lambda i: (i, 0)),
                  pl.BlockSpec((tm, 128), lambda i: (i, 0)),
                  pl.BlockSpec((tm, 128), lambda i: (i, 0))],
        out_specs=pl.BlockSpec((tm, MAIN_W), lambda i: (i, 0)),
        out_shape=_sds((M, MAIN_W), f32),
        compiler_params=_params(("parallel",)),
    )(src, cos, sin)


def group_sum(groups, cos, sin, *, name, rotate, width, col_block=0, into=None):
    M = groups[0][0].shape[0]
    tm = min(512, M)
    counts = [len(g) for g in groups]
    flat = [a for g in groups for a in g]
    extra = [] if into is None else [into]

    def body(*refs):
        part_refs = refs[:len(flat)]
        c_ref, s_ref = refs[len(flat):len(flat) + 2]
        o_ref = refs[-1]
        cols, k = [], 0
        for n in counts:
            acc = part_refs[k][...]
            for r in part_refs[k + 1:k + n]:
                acc = acc + r[...]
            cols.append(acc)
            k += n
        d = jnp.concatenate(cols, axis=1)
        if rotate:
            c = jnp.tile(c_ref[...], (1, MAIN_W // 128))
            s = jnp.tile(s_ref[...], (1, MAIN_W // 128))
            d = d * c - _swap_halves(d) * s
        o_ref[...] = d.astype(bf16)

    part = pl.BlockSpec((tm, GROUP_W), lambda i: (i, 0))
    tab = pl.BlockSpec((tm, 128), lambda i: (i, 0))
    return pl.pallas_call(
        body, name=name, grid=(M // tm,),
        in_specs=[part] * len(flat) + [tab, tab] + [pl.BlockSpec(memory_space=pl.ANY)] * len(extra),
        out_specs=pl.BlockSpec((tm, MAIN_W), lambda i: (i, col_block)),
        out_shape=_sds((M, width), bf16),
        input_output_aliases={len(flat) + 2: 0} if extra else {},
        compiler_params=_params(("parallel",)),
    )(*flat, cos, sin, *extra)


PAIR_W = 2 * HEAD_DIM
MIN_BLOCKS = 8


def _dil_geometry(dil):
    nsub = max(dil, MIN_BLOCKS)
    tb = BAND * nsub
    return nsub, tb, SEQ // tb


REGROUP = 4


class _Regrouped:
    def __init__(self, ref):
        self.ref = ref
        self.shape = ref.shape

    def fill(self, src):
        q = self.shape[0] // REGROUP
        for r0 in range(REGROUP):
            self.ref[r0 * q:(r0 + 1) * q, :] = src[pl.ds(r0, q, stride=REGROUP), :]

    def drain(self, dst):
        q = self.shape[0] // REGROUP
        for r0 in range(REGROUP):
            dst[pl.ds(r0, q, stride=REGROUP), :] = self.ref[r0 * q:(r0 + 1) * q, :]

    def rows(self, sub, dil):
        nl, r = divmod(sub, dil)
        start = (r % REGROUP) * (self.shape[0] // REGROUP) + r // REGROUP + nl * BAND * (dil // REGROUP)
        return pl.ds(start, BAND, stride=dil // REGROUP)


def _regroups(dil):
    return dil % (4 * REGROUP) == 0


def _rows(ref, sub, dil):
    if isinstance(ref, _Regrouped):
        return ref.ref[ref.rows(sub, dil), :]
    if dil == 1:
        return ref[sub * BAND:(sub + 1) * BAND, :]
    nl, r = divmod(sub, dil)
    return ref[pl.ds(nl * BAND * dil + r, BAND, stride=dil), :]


def _store_rows(ref, sub, dil, val):
    if isinstance(ref, _Regrouped):
        ref.ref[ref.rows(sub, dil), :] = val
    elif dil == 1:
        ref[sub * BAND:(sub + 1) * BAND, :] = val
    else:
        nl, r = divmod(sub, dil)
        ref[pl.ds(nl * BAND * dil + r, BAND, stride=dil), :] = val


def _keys(prev_ref, own_ref, sub, dil):
    nsub = own_ref.shape[0] // BAND
    if sub >= dil:
        prev = _rows(own_ref, sub - dil, dil)
    elif prev_ref is None:
        return _rows(own_ref, sub, dil)
    else:
        prev = _rows(prev_ref, nsub - dil + sub, dil)
    return jnp.concatenate([prev, _rows(own_ref, sub, dil)], axis=0)


def _band_mask(nkeys, has_prev):
    i = lax.broadcasted_iota(jnp.int32, (BAND, nkeys), 0)
    j = lax.broadcasted_iota(jnp.int32, (BAND, nkeys), 1)
    if nkeys == BAND:
        return j <= i
    return (j >= i) & (j <= i + BAND) & (has_prev | (j >= BAND))


def _first_head():
    return lax.broadcasted_iota(jnp.int32, (1, PAIR_W), 1) < HEAD_DIM


def _col(x, hh):
    return x[:, hh * HEAD_DIM:hh * HEAD_DIM + 1]


def _pair_spec(tb, nblk, col0, which):
    def idx(b, p, i):
        if which < 0:
            i = jnp.maximum(i - 1, 0)
        elif which > 0:
            i = jnp.minimum(i + 1, nblk - 1)
        return (b * nblk + i, col0 + p)
    return pl.BlockSpec((tb, PAIR_W), idx)


def dil_fwd(q, k, kv, g, dil, *, name, n_seq):
    M = q.shape[0]
    nsub, tb, nblk = _dil_geometry(dil)
    with_prev = nblk > 1

    regroup = _regroups(dil)
    assert not (regroup and with_prev)

    def body(*refs):
        if with_prev:
            q_ref, ko_ref, vo_ref, kp_ref, vp_ref, o_ref, l_ref = refs
        else:
            (q_ref, ko_ref, vo_ref, o_ref, l_ref), kp_ref, vp_ref = refs[:5], None, None
        outs_to = ()
        if regroup:
            copies = [_Regrouped(s) for s in refs[5:]]
            for c, src in zip(copies, (q_ref, ko_ref, vo_ref)):
                c.fill(src)
            outs_to = ((copies[3], o_ref), (copies[4], l_ref))
            q_ref, ko_ref, vo_ref, o_ref, l_ref = copies
        first = _first_head()
        blk = pl.program_id(2)
        for sub in range(nsub):
            qs = _rows(q_ref, sub, dil) * SCALE
            kc = _keys(kp_ref, ko_ref, sub, dil).astype(bf16)
            vc = _keys(vp_ref, vo_ref, sub, dil).astype(bf16)
            has_prev = True if sub >= dil else blk > 0
            mask = _band_mask(kc.shape[0], has_prev)
            outs, lses = [], []
            for hh in range(2):
                qm = jnp.where(first if hh == 0 else ~first, qs, 0.0).astype(bf16)
                s = jnp.where(mask, lax.dot_general(qm, kc, NT, preferred_element_type=f32), NEG)
                m = jnp.max(s, axis=-1, keepdims=True)
                e = jnp.exp(s - m)
                l = jnp.sum(e, axis=-1, keepdims=True)
                outs.append(jnp.dot(e.astype(bf16), vc, preferred_element_type=f32) * (1.0 / l))
                lses.append(jnp.broadcast_to(m + jnp.log(l), (BAND, PAIR_W)))
            _store_rows(o_ref, sub, dil, jnp.where(first, outs[0], outs[1]))
            _store_rows(l_ref, sub, dil, jnp.where(first, lses[0], lses[1]))
        for c, dst in outs_to:
            c.drain(dst)

    ins = [(q, 2 * g, 0), (k, 2 * g, 0), (kv, 6 + 2 * g, 0)]
    if with_prev:
        ins += [(k, 2 * g, -1), (kv, 6 + 2 * g, -1)]
    out = _pair_spec(tb, nblk, 0, 0)
    return pl.pallas_call(
        body, name=name, grid=(n_seq, 2, nblk),
        in_specs=[_pair_spec(tb, nblk, c, w) for _, c, w in ins],
        out_specs=[out, out],
        out_shape=[_sds((M, GROUP_W), f32)] * 2,
        scratch_shapes=[pltpu.VMEM((tb, PAIR_W), f32)] * (5 if regroup else 0),
        compiler_params=_params(("parallel", "parallel", "arbitrary")),
    )(*[a for a, _, _ in ins])


def combine_fwd(os_, lses, *, name):
    M = os_[0].shape[0]
    tm = min(512, M)

    def body(o0, o1, o2, l0, l1, l2, y_ref):
        ls = [l0[...], l1[...], l2[...]]
        m = jnp.maximum(jnp.maximum(ls[0], ls[1]), ls[2])
        es = [jnp.exp(l - m) for l in ls]
        inv = 1.0 / (es[0] + es[1] + es[2])
        y_ref[...] = jnp.concatenate([o[...] * e * inv for o, e in zip((o0, o1, o2), es)], axis=1).astype(bf16)

    part = pl.BlockSpec((tm, GROUP_W), lambda i: (i, 0))
    return pl.pallas_call(
        body, name=name, grid=(M // tm,),
        in_specs=[part] * 6,
        out_specs=pl.BlockSpec((tm, MAIN_W), lambda i: (i, 0)),
        out_shape=_sds((M, D_MODEL), bf16),
        compiler_params=_params(("parallel",)),
    )(*os_, *lses)


def combine_bwd(dyc, os_, lses, *, name):
    M = os_[0].shape[0]
    tm = min(512, M)

    def body(dy_ref, o0, o1, o2, l0, l1, l2, d0, d1, d2, c0, c1, c2):
        r = lax.broadcasted_iota(jnp.int32, (GROUP_W, GROUP_W), 0) // HEAD_DIM
        c = lax.broadcasted_iota(jnp.int32, (GROUP_W, GROUP_W), 1) // HEAD_DIM
        ones = (r == c).astype(f32)
        dy = dy_ref[...].astype(f32)
        ls = [l0[...], l1[...], l2[...]]
        m = jnp.maximum(jnp.maximum(ls[0], ls[1]), ls[2])
        es = [jnp.exp(l - m) for l in ls]
        inv = 1.0 / (es[0] + es[1] + es[2])
        total = 0.0
        alphas = []
        for g, (o, e, d_ref) in enumerate(zip((o0, o1, o2), es, (d0, d1, d2))):
            a = e * inv
            dyg = dy[:, g * GROUP_W:(g + 1) * GROUP_W]
            d_ref[...] = dyg * a
            dsum = jnp.dot(dyg * o[...], ones, precision=lax.Precision.HIGHEST, preferred_element_type=f32)
            total = total + a * dsum
            alphas.append(a)
        for a, c_ref in zip(alphas, (c0, c1, c2)):
            c_ref[...] = -a * total

    part = pl.BlockSpec((tm, GROUP_W), lambda i: (i, 0))
    outs = pl.pallas_call(
        body, name=name, grid=(M // tm,),
        in_specs=[pl.BlockSpec((tm, MAIN_W), lambda i: (i, 0))] + [part] * 6,
        out_specs=[part] * 6,
        out_shape=[_sds((M, GROUP_W), f32)] * 6,
        compiler_params=_params(("parallel",)),
    )(dyc, *os_, *lses)
    return outs[:3], outs[3:]


def dil_bwd(q, k, kv, do, cc, lse, g, dil, *, name, n_seq):
    M = q.shape[0]
    nsub = SEQ // BAND
    per_res = nsub // dil

    regroup = _regroups(dil)

    def body(q_ref, k_ref, v_ref, do_ref, c_ref, l_ref, dq_ref, dk_ref, dv_ref, *scratch):
        outs_to = ()
        if regroup:
            copies = [_Regrouped(s) for s in scratch]
            for c, src in zip(copies, (q_ref, k_ref, v_ref, do_ref, c_ref, l_ref)):
                c.fill(src)
            outs_to = tuple(zip(copies[6:], (dq_ref, dk_ref, dv_ref)))
            q_ref, k_ref, v_ref, do_ref, c_ref, l_ref, dq_ref, dk_ref, dv_ref = copies
        first = _first_head()
        for r in range(dil):
            carry = None
            for nl in range(per_res):
                sub = nl * dil + r
                qs = _rows(q_ref, sub, dil) * SCALE
                dos = _rows(do_ref, sub, dil)
                cs = _rows(c_ref, sub, dil)
                ls = _rows(l_ref, sub, dil)
                kc = _keys(None, k_ref, sub, dil).astype(bf16)
                vc = _keys(None, v_ref, sub, dil).astype(bf16)
                nkeys = kc.shape[0]
                mask = _band_mask(nkeys, True)
                dqs = []
                dkc = jnp.zeros((nkeys, PAIR_W), f32)
                dvc = jnp.zeros((nkeys, PAIR_W), f32)
                for hh in range(2):
                    lm = first if hh == 0 else ~first
                    qm = jnp.where(lm, qs, 0.0).astype(bf16)
                    dom = jnp.where(lm, dos, 0.0).astype(bf16)
                    s = jnp.where(mask, lax.dot_general(qm, kc, NT, preferred_element_type=f32), NEG)
                    p = jnp.exp(s - _col(ls, hh))
                    dp = lax.dot_general(dom, vc, NT, preferred_element_type=f32)
                    ds = (p * (dp + _col(cs, hh))).astype(bf16)
                    dqs.append(jnp.dot(ds, kc, preferred_element_type=f32) * SCALE)
                    dkc = dkc + lax.dot_general(ds, qm, TN, preferred_element_type=f32)
                    dvc = dvc + lax.dot_general(p.astype(bf16), dom, TN, preferred_element_type=f32)
                _store_rows(dq_ref, sub, dil, jnp.where(first, dqs[0], dqs[1]))
                if nkeys == 2 * BAND:
                    _store_rows(dk_ref, sub - dil, dil, carry[0] + dkc[:BAND])
                    _store_rows(dv_ref, sub - dil, dil, carry[1] + dvc[:BAND])
                    carry = (dkc[BAND:], dvc[BAND:])
                else:
                    carry = (dkc, dvc)
            _store_rows(dk_ref, (per_res - 1) * dil + r, dil, carry[0])
            _store_rows(dv_ref, (per_res - 1) * dil + r, dil, carry[1])
        for c, dst in outs_to:
            c.drain(dst)

    def spec(col0):
        return pl.BlockSpec((SEQ, PAIR_W), lambda b, p: (b, col0 + p))

    out = spec(0)
    return pl.pallas_call(
        body, name=name, grid=(n_seq, 2),
        in_specs=[spec(2 * g), spec(2 * g), spec(6 + 2 * g), spec(0), spec(0), spec(0)],
        out_specs=[out, out, out],
        out_shape=[_sds((M, GROUP_W), f32)] * 3,
        scratch_shapes=[pltpu.VMEM((SEQ, PAIR_W), f32)] * (9 if regroup else 0),
        compiler_params=_params(("parallel", "parallel")),
    )(q, k, kv, do, cc, lse)


def _blockdiag(wp):
    out = jnp.zeros((MAIN_W, MAIN_W), wp.dtype)
    for gi in range(len(POOL_WINDOWS)):
        sl = slice(gi * POOL_GROUP, (gi + 1) * POOL_GROUP)
        out = out.at[sl, sl].set(wp[gi])
    return out


def _unblockdiag(w):
    return jnp.stack([w[gi * POOL_GROUP:(gi + 1) * POOL_GROUP, gi * POOL_GROUP:(gi + 1) * POOL_GROUP]
                      for gi in range(len(POOL_WINDOWS))])


def local_step(x, mem, positions, target, P, layer_weights, kv_weight, emit_grads):
    n_seq = x.shape[0]
    M = n_seq * SEQ
    xs = x.reshape(M, D_MODEL)
    mems = mem.reshape(n_seq * N_MEM, D_MODEL)
    pos = positions.reshape(M, 1).astype(f32)
    cos, sin = rope_tables(pos, name="rope_tables")
    gains = P["norm_gains"]

    def gain(l, k):
        return gains[l, k].reshape(1, D_MODEL)

    saved = []
    kvs = None
    for l in range(DEPTH):
        W, started = layer_weights(l, "mix", xs)
        sv = {"x": xs, "W": W}
        z, h1 = rms_matmul(xs, gain(l, 0), W["w_in"], name=f"l{l}_in", out_dtype=f32, after=started)
        kvm, mn = rms_matmul(mems, P["mem_norm"][l].reshape(1, D_MODEL), W["w_mem_kv"],
                             name=f"l{l}_memkv", out_dtype=bf16)
        sv.update(z=z, h1=h1, kvm=kvm, mn=mn)
        if l < N_A_LAYERS:
            wbd = _blockdiag(P["w_pool"][l].astype(bf16))
            psc = P["pool_scale"][l].reshape(1, MAIN_W)
            p, y_main = pool_fwd(z, wbd, psc, name=f"l{l}_pool")
            sv.update(p=p, wbd=wbd, psc=psc)
        else:
            qrot = rope_fwd(z, cos, sin, name=f"l{l}_ropeq")
            os_, lses = [], []
            for g, (_, dil) in enumerate(DIL_PATTERNS):
                o, lse = dil_fwd(qrot, kvs["krot"], kvs["kv"], g, dil, name=f"l{l}_dil{g}", n_seq=n_seq)
                os_.append(o)
                lses.append(lse)
            y_main = combine_fwd(os_, lses, name=f"l{l}_comb")
            sv.update(qrot=qrot, os=os_, lses=lses)
        ycat = memattn_fwd(z, kvm, y_main, name=f"l{l}_memattn", n_seq=n_seq)
        y, x1 = matmul_rms_res(ycat, W["w_out"], gain(l, 1), xs, name=f"l{l}_out")
        W.update(layer_weights(l, "gu", x1)[0])
        fg, fu, a, h2 = rms_gate_up(x1, gain(l, 2), W["w_gate_up"], name=f"l{l}_gu")
        W.update(layer_weights(l, "down", a)[0])
        y2, x2 = matmul_rms_res(a, W["w_down"], gain(l, 3), x1, name=f"l{l}_down")
        sv.update(ycat=ycat, y=y, x1=x1, fg=fg, fu=fu, h2=h2, a=a, y2=y2)
        saved.append(sv)
        xs = x2
        if l == N_A_LAYERS - 1:
            w_kv = kv_weight(xs)
            kv, hkv = rms_matmul(xs, P["kv_norm"].reshape(1, D_MODEL), w_kv, name="kv_proj", out_dtype=f32,
                                 transposed=True)
            krot = rope_fwd(kv, cos, sin, name="ropek")
            kvs = {"kv": kv, "hkv": hkv, "krot": krot, "x": xs, "w_kv": w_kv}

    dx, sq = loss_head(xs, target.reshape(M, D_MODEL), name="loss_head")

    G = {"mem_norm": [None] * DEPTH, "norm_gains": [[None] * 4 for _ in range(DEPTH)],
         "pool_scale": [None] * N_A_LAYERS}
    dk_parts = [[] for _ in range(N_GROUPS)]
    dv_parts = [[] for _ in range(N_GROUPS)]
    emitted = None

    for l in reversed(range(DEPTH)):
        sv = saved[l]
        W = sv["W"]
        gw = {}
        dy2, dgu, G["norm_gains"][l][3] = down_bwd(sv["y2"], gain(l, 3), dx, W["w_down"], sv["fg"], sv["fu"],
                                                   name=f"l{l}_b_dgu", after=emitted)
        gw["w_down"] = matmul(sv["a"], dy2, TN, name=f"l{l}_b_wd", out_dtype=bf16)
        dx1, G["norm_gains"][l][2] = matmul_rms_bwd(dgu, W["w_gate_up"], NN, sv["x1"], gain(l, 2), dx,
                                                    name=f"l{l}_b_dh2")
        gw["w_gate_up"] = matmul(dgu, sv["h2"], TN, name=f"l{l}_b_wgu", out_dtype=bf16)
        emitted = emit_grads(l, "ffn", gw)
        gw = {}
        dy, dycat, G["norm_gains"][l][1] = rms_bwd_matmul(sv["y"], gain(l, 1), dx1, W["w_out"], NT,
                                                          name=f"l{l}_b_dycat", after=emitted)
        gw["w_out"] = matmul(sv["ycat"], dy, TN, name=f"l{l}_b_wout", out_dtype=bf16)
        if l < N_A_LAYERS:
            dz, dwbd, dps = pool_bwd(dycat, sv["p"], sv["wbd"], sv["psc"], name=f"l{l}_b_pool")
            gw["w_pool"] = _unblockdiag(dwbd).reshape(MAIN_W, POOL_GROUP).astype(bf16)
            G["pool_scale"][l] = dps.reshape(MAIN_W)
        else:
            dos, ccs = combine_bwd(dycat, sv["os"], sv["lses"], name=f"l{l}_b_comb")
            dqs = []
            for g, (_, dil) in enumerate(DIL_PATTERNS):
                args = (sv["qrot"], kvs["krot"], kvs["kv"], dos[g], ccs[g], sv["lses"][g], g, dil)
                dq, dk, dv = dil_bwd(*args, name=f"l{l}_b_dil{g}", n_seq=n_seq)
                dqs.append([dq])
                dk_parts[g].append(dk)
                dv_parts[g].append(dv)
            dz = group_sum(dqs, cos, sin, name=f"l{l}_b_ropeq", rotate=True, width=D_MODEL)
        dz, dkvm = memattn_bwd(sv["z"], sv["kvm"], dycat, dz, name=f"l{l}_b_memattn", n_seq=n_seq)
        gw["w_mem_kv"] = matmul(sv["mn"], dkvm, TN, name=f"l{l}_b_wmkv", out_dtype=bf16)
        _, G["mem_norm"][l] = matmul_rms_bwd(dkvm, W["w_mem_kv"], NT, mems, P["mem_norm"][l].reshape(1, D_MODEL),
                                             mems, name=f"l{l}_b_dmn")
        gw["w_in"] = matmul(sv["h1"], dz, TN, name=f"l{l}_b_win", out_dtype=bf16)
        dx, G["norm_gains"][l][0] = matmul_rms_bwd(dz, W["w_in"], NT, sv["x"], gain(l, 0), dx1, name=f"l{l}_b_dh1")
        if l == N_A_LAYERS:
            dkv = group_sum(dk_parts, cos, sin, name="b_ropek", rotate=True, width=2 * MAIN_W)
            dkv = group_sum(dv_parts, cos, sin, name="b_sumv", rotate=False, width=2 * MAIN_W, col_block=1, into=dkv)
            gw["w_kv"] = matmul(dkv, kvs["hkv"], TN, name="b_wkv", out_dtype=bf16)
            dx, gkn = matmul_rms_bwd(dkv, kvs["w_kv"], NN, kvs["x"], P["kv_norm"].reshape(1, D_MODEL), dx,
                                     name="b_dhkv")
            G["kv_norm"] = gkn.reshape(D_MODEL)
        emitted = emit_grads(l, "mix", gw)

    small = {"pool_scale": jnp.stack(G["pool_scale"]),
             "mem_norm": jnp.concatenate(G["mem_norm"], axis=0),
             "norm_gains": jnp.stack([jnp.concatenate(r, axis=0) for r in G["norm_gains"]]),
             "kv_norm": G["kv_norm"]}
    return sq[0, 0], dx.reshape(n_seq, SEQ, D_MODEL), small, emitted


def _peer(k):
    x, y, c = lax.axis_index("x"), lax.axis_index("y"), lax.axis_index("c")
    px = 1 - x if k & 4 else x
    py = 1 - y if k & 2 else y
    pc = 1 - c if k & 1 else c
    return (px, py, pc), 4 * px + 2 * py + pc


def _my_index():
    return 4 * lax.axis_index("x") + 2 * lax.axis_index("y") + lax.axis_index("c")


def _src_for(kinds, in_refs, i, idx):
    return in_refs[i] if kinds[i] == "gather" else in_refs[i].at[idx]


def _local_copies(kinds, in_refs, out_refs, local_sems):
    me = _my_index()
    return [pltpu.make_async_copy(_src_for(kinds, in_refs, i, me), out_refs[i].at[me], local_sems.at[i])
            for i in range(len(kinds))]


def _remote_copies(kinds, in_refs, out_refs, send_sems, recv_sems, *, arriving):
    me = _my_index()
    copies = []
    for k in range(1, N_DEV):
        dev, idx = _peer(k)
        for i in range(len(kinds)):
            j = i * (N_DEV - 1) + k - 1
            copies.append(pltpu.make_async_remote_copy(
                src_ref=_src_for(kinds, in_refs, i, idx), dst_ref=out_refs[i].at[idx if arriving else me],
                send_sem=send_sems.at[j], recv_sem=recv_sems.at[j], device_id=dev, device_id_type=MESH))
    return copies


def _out_shape(a, kind):
    return ((N_DEV,) + a.shape) if kind == "gather" else a.shape


def exchange(items, *, name, after=()):
    n = len(items)
    kinds = [k for _, k in items]
    after = list(after)

    def body(*refs):
        in_refs, out_refs = refs[:n], refs[n + len(after):2 * n + len(after)]
        send_sems, recv_sems, local_sems = refs[-3:]
        local = _local_copies(kinds, in_refs, out_refs, local_sems)
        sends = _remote_copies(kinds, in_refs, out_refs, send_sems, recv_sems, arriving=False)
        for cp in local + sends:
            cp.start()
        for cp in _remote_copies(kinds, in_refs, out_refs, send_sems, recv_sems, arriving=True):
            cp.wait_recv()
        for cp in sends:
            cp.wait_send()
        for cp in local:
            cp.wait()

    any_spec = pl.BlockSpec(memory_space=pl.ANY)
    return pl.pallas_call(
        body, name=name,
        in_specs=[any_spec] * (n + len(after)), out_specs=[any_spec] * n,
        out_shape=[_sds(_out_shape(a, k), a.dtype) for a, k in items],
        scratch_shapes=[pltpu.SemaphoreType.DMA((n * (N_DEV - 1),)), pltpu.SemaphoreType.DMA((n * (N_DEV - 1),)),
                        pltpu.SemaphoreType.DMA((n,))],
    )(*[a for a, _ in items], *after)


_HBM = pl.BlockSpec(memory_space=pltpu.HBM)
_SEM = pl.BlockSpec(memory_space=pltpu.SEMAPHORE)
_EFFECT = pltpu.SideEffectType.DATAFLOW_SIDE_EFFECTING


def exchange_start(items, after, *, name):
    n = len(items)
    kinds = [k for _, k in items]

    def body(*refs):
        in_refs, land_refs = refs[:n], refs[n:2 * n]
        send_sems, recv_sems, local_sems = refs[2 * n + 1:2 * n + 4]
        token = refs[-1]
        for cp in (_local_copies(kinds, in_refs, land_refs, local_sems)
                   + _remote_copies(kinds, in_refs, land_refs, send_sems, recv_sems, arriving=False)):
            cp.start()
        token[...] = jnp.zeros_like(token)

    srcs = [pltpu.with_memory_space_constraint(a, pltpu.HBM) for a, _ in items]
    lands = [pltpu.with_memory_space_constraint(lax.empty(_out_shape(a, k), a.dtype), pltpu.HBM) for a, k in items]
    outs = pl.pallas_call(
        body, name=name,
        out_shape=(pltpu.SemaphoreType.DMA((n * (N_DEV - 1),)), pltpu.SemaphoreType.DMA((n * (N_DEV - 1),)),
                   pltpu.SemaphoreType.DMA((n,)),
                   *[pltpu.HBM(a.shape, a.dtype) for a in srcs], *[pltpu.HBM(a.shape, a.dtype) for a in lands],
                   _sds((8, 128), f32)),
        in_specs=[_HBM] * (2 * n) + [pl.BlockSpec(memory_space=pl.ANY)],
        out_specs=(_SEM, _SEM, _SEM, *[_HBM] * (2 * n), pl.BlockSpec(memory_space=pltpu.VMEM)),
        input_output_aliases={i: 3 + i for i in range(2 * n)},
        compiler_params=pltpu.CompilerParams(has_side_effects=_EFFECT),
    )(*srcs, *lands, after)
    return {"kinds": kinds, "sems": outs[:3], "srcs": outs[3:3 + n], "lands": outs[3 + n:3 + 2 * n], "token": outs[-1]}


def exchange_wait(handle, after, *, name):
    kinds = handle["kinds"]
    n = len(kinds)

    def body(*refs):
        in_refs, land_refs = refs[:n], refs[n:2 * n]
        send_sems, recv_sems, local_sems = refs[2 * n:2 * n + 3]
        for cp in _remote_copies(kinds, in_refs, land_refs, send_sems, recv_sems, arriving=True):
            cp.wait_recv()
        for cp in _remote_copies(kinds, in_refs, land_refs, send_sems, recv_sems, arriving=False):
            cp.wait_send()
        for cp in _local_copies(kinds, in_refs, land_refs, local_sems):
            cp.wait()

    srcs, lands = list(handle["srcs"]), list(handle["lands"])
    after = list(after) if isinstance(after, (list, tuple)) else [after]
    outs = pl.pallas_call(
        body, name=name,
        out_shape=tuple(pltpu.HBM(a.shape, a.dtype) for a in srcs + lands),
        in_specs=[_HBM] * (2 * n) + [_SEM] * 3 + [pl.BlockSpec(memory_space=pl.ANY)] * len(after),
        out_specs=tuple([_HBM] * (2 * n)),
        input_output_aliases={i: i for i in range(2 * n)},
        compiler_params=pltpu.CompilerParams(has_side_effects=_EFFECT),
    )(*srcs, *lands, *handle["sems"], *after)
    return list(outs[n:])


CHIP_MASKS = (2, 4, 6)


def _g2_first(in_refs, land_refs, send_sems, recv_sems, *, masks, arriving):
    me = _my_index()
    copies = []
    for i in range(len(land_refs)):
        for j, k in enumerate(masks):
            dev, idx = _peer(k)
            dst = land_refs[i].at[idx if arriving else me]
            copies.append(pltpu.make_async_remote_copy(
                src_ref=dst if in_refs is None else in_refs[i], dst_ref=dst,
                send_sem=send_sems.at[i * len(masks) + j], recv_sem=recv_sems.at[i * len(masks) + j],
                device_id=dev, device_id_type=MESH))
    return copies


def _g2_forward(land_refs, fwd_send, fwd_recv, *, arriving):
    sibling, _ = _peer(1)
    copies = []
    for i in range(len(land_refs)):
        for j, k in enumerate(CHIP_MASKS):
            _, idx = _peer(k | 1 if arriving else k)
            copies.append(pltpu.make_async_remote_copy(
                src_ref=land_refs[i].at[idx], dst_ref=land_refs[i].at[idx],
                send_sem=fwd_send.at[i * 3 + j], recv_sem=fwd_recv.at[i * 3 + j], device_id=sibling,
                device_id_type=MESH))
    return copies


def gather2_start(arrays, after, *, name):
    n = len(arrays)

    def body(*refs):
        in_refs, land_refs = refs[:n], refs[n:2 * n]
        ici_send, ici_recv, d2d_send, d2d_recv, local_sems = refs[2 * n + 1:2 * n + 6]
        token = refs[-1]
        ici = _g2_first(in_refs, land_refs, ici_send, ici_recv, masks=CHIP_MASKS, arriving=False)
        d2d = _g2_first(in_refs, land_refs, d2d_send, d2d_recv, masks=(1,), arriving=False)
        for cp in _local_copies(["gather"] * n, in_refs, land_refs, local_sems) + ici + d2d:
            cp.start()
        token[...] = jnp.zeros_like(token)

    srcs = [pltpu.with_memory_space_constraint(a, pltpu.HBM) for a in arrays]
    lands = [pltpu.with_memory_space_constraint(lax.empty((N_DEV,) + a.shape, a.dtype), pltpu.HBM) for a in arrays]
    sem = pltpu.SemaphoreType.DMA
    outs = pl.pallas_call(
        body, name=name,
        out_shape=(sem((3 * n,)), sem((3 * n,)), sem((n,)), sem((n,)), sem((n,)),
                   *[pltpu.HBM(a.shape, a.dtype) for a in srcs], *[pltpu.HBM(a.shape, a.dtype) for a in lands],
                   _sds((8, 128), f32)),
        in_specs=[_HBM] * (2 * n) + [pl.BlockSpec(memory_space=pl.ANY)],
        out_specs=(*[_SEM] * 5, *[_HBM] * (2 * n), pl.BlockSpec(memory_space=pltpu.VMEM)),
        input_output_aliases={i: 5 + i for i in range(2 * n)},
        compiler_params=pltpu.CompilerParams(has_side_effects=_EFFECT),
    )(*srcs, *lands, after)
    return {"n": n, "sems": outs[:5], "srcs": outs[5:5 + n], "lands": outs[5 + n:5 + 2 * n], "token": outs[-1]}


def gather2_forward(handle, after, *, name):
    n = handle["n"]

    def body(*refs):
        land_refs = refs[:n]
        ici_recv = refs[n]
        fwd_send, fwd_recv = refs[n + 2:n + 4]
        for cp in _g2_first(None, land_refs, fwd_send, ici_recv, masks=CHIP_MASKS, arriving=True):
            cp.wait_recv()
        for cp in _g2_forward(land_refs, fwd_send, fwd_recv, arriving=False):
            cp.start()

    lands = list(handle["lands"])
    sem = pltpu.SemaphoreType.DMA
    outs = pl.pallas_call(
        body, name=name,
        out_shape=(sem((3 * n,)), sem((3 * n,)), *[pltpu.HBM(a.shape, a.dtype) for a in lands]),
        in_specs=[_HBM] * n + [_SEM, pl.BlockSpec(memory_space=pl.ANY)],
        out_specs=(_SEM, _SEM, *[_HBM] * n),
        input_output_aliases={i: 2 + i for i in range(n)},
        compiler_params=pltpu.CompilerParams(has_side_effects=_EFFECT),
    )(*lands, handle["sems"][1], after)
    return dict(handle, fwd=outs[:2], lands=outs[2:])


def gather2_wait(handle, after, *, name):
    n = handle["n"]

    def body(*refs):
        in_refs, land_refs = refs[:n], refs[n:2 * n]
        ici_send, d2d_send, d2d_recv, local_sems, fwd_send, fwd_recv = refs[2 * n:2 * n + 6]
        for cp in _g2_first(in_refs, land_refs, d2d_send, d2d_recv, masks=(1,), arriving=True):
            cp.wait_recv()
        for cp in _g2_forward(land_refs, fwd_send, fwd_recv, arriving=True):
            cp.wait_recv()
        for cp in (_g2_first(in_refs, land_refs, ici_send, fwd_recv, masks=CHIP_MASKS, arriving=False)
                   + _g2_first(in_refs, land_refs, d2d_send, d2d_recv, masks=(1,), arriving=False)
                   + _g2_forward(land_refs, fwd_send, fwd_recv, arriving=False)):
            cp.wait_send()
        for cp in _local_copies(["gather"] * n, in_refs, land_refs, local_sems):
            cp.wait()

    srcs, lands = list(handle["srcs"]), list(handle["lands"])
    s = handle["sems"]
    outs = pl.pallas_call(
        body, name=name,
        out_shape=tuple(pltpu.HBM(a.shape, a.dtype) for a in srcs + lands),
        in_specs=[_HBM] * (2 * n) + [_SEM] * 6 + [pl.BlockSpec(memory_space=pl.ANY)],
        out_specs=tuple([_HBM] * (2 * n)),
        input_output_aliases={i: i for i in range(2 * n)},
        compiler_params=pltpu.CompilerParams(has_side_effects=_EFFECT),
    )(*srcs, *lands, s[0], s[2], s[3], s[4], *handle["fwd"], after)
    return list(outs[n:])


def adamw(slots, w, m, v, *, name, layer=None, into=None):
    R, C = w.shape[-2:]
    tr = _tile(R, (256, 128, 64, 32, 16, 8))
    c1 = 1.0 - ADAM_B1 ** ADAM_STEP
    c2 = 1.0 - ADAM_B2 ** ADAM_STEP
    extra = [] if into is None else list(into)

    def body(s_ref, w_ref, m_ref, v_ref, *refs):
        g_ref, d_ref, m2_ref, v2_ref = refs[len(extra):]
        g = s_ref[0].astype(f32)
        for d in range(1, N_DEV):
            g = g + s_ref[d].astype(f32)
        m2 = ADAM_B1 * m_ref[...] + (1.0 - ADAM_B1) * g
        v2 = ADAM_B2 * v_ref[...] + (1.0 - ADAM_B2) * (g * g)
        g_ref[...] = g
        m2_ref[...] = m2
        v2_ref[...] = v2
        d_ref[...] = -ADAM_LR * ((m2 / c1) / (jnp.sqrt(v2 / c2) + ADAM_EPS) + ADAM_WD * w_ref[...])

    if layer is None:
        blk = pl.BlockSpec((tr, C), lambda i: (i, 0))
    else:
        blk = pl.BlockSpec((None, tr, C), lambda i: (layer, i, 0))
    return pl.pallas_call(
        body, name=name, grid=(R // tr,),
        in_specs=[pl.BlockSpec((N_DEV, tr, C), lambda i: (0, i, 0)), blk, blk, blk]
        + [pl.BlockSpec(memory_space=pl.ANY)] * len(extra),
        out_specs=[blk] * 4,
        out_shape=[_sds(w.shape, f32)] * 4,
        input_output_aliases={4 + j: j for j in range(len(extra))},
        compiler_params=_params(("parallel",)),
    )(slots, w, m, v, *extra)


WEIGHTS = ("norm_gains", "mem_norm", "w_in", "w_mem_kv", "w_out", "w_pool", "pool_scale", "kv_norm", "w_kv",
           "w_gate_up", "w_down")
LAYER_MATS = ("w_in", "w_mem_kv", "w_out", "w_gate_up", "w_down")
POOL_SHARD = MAIN_W // N_DEV
KV_SHARD = 2 * MAIN_W // N_DEV
LOOKAHEAD = 2
TWO_LEVEL_LAYERS = (0, 1)


def _pack_small(gains, pscale):
    lead = gains.shape[:-3]
    g = gains.reshape(lead + (16, 128))
    p = jnp.zeros(lead + (8, 128), f32).at[..., :2, :POOL_SHARD].set(pscale)
    return jnp.concatenate([g, p], axis=-2)


def _unpack_small(a):
    return a[:16].reshape(4, 4, 128), a[16:18, :POOL_SHARD]


def _pack_repl(mem_norm, kv_norm):
    return jnp.concatenate([mem_norm, kv_norm.reshape(1, D_MODEL), jnp.zeros((3, D_MODEL), f32)], axis=0)


def _unpack_repl(a):
    return a[:4], a[4]


def kernel(x, mem, positions, norm_gains, mem_norm, w_in, w_mem_kv, w_out, w_pool, pool_scale, kv_norm, w_kv, w_gate_up, w_down, loss_target, m_norm_gains, m_mem_norm, m_w_in, m_w_mem_kv, m_w_out, m_w_pool, m_pool_scale, m_kv_norm, m_w_kv, m_w_gate_up, m_w_down, v_norm_gains, v_mem_norm, v_w_in, v_w_mem_kv, v_w_out, v_w_pool, v_pool_scale, v_kv_norm, v_w_kv, v_w_gate_up, v_w_down):
    w = dict(norm_gains=norm_gains, mem_norm=mem_norm, w_in=w_in, w_mem_kv=w_mem_kv, w_out=w_out, w_pool=w_pool,
             pool_scale=pool_scale, kv_norm=kv_norm, w_kv=w_kv, w_gate_up=w_gate_up, w_down=w_down)
    m = dict(norm_gains=m_norm_gains, mem_norm=m_mem_norm, w_in=m_w_in, w_mem_kv=m_w_mem_kv, w_out=m_w_out,
             w_pool=m_w_pool, pool_scale=m_pool_scale, kv_norm=m_kv_norm, w_kv=m_w_kv, w_gate_up=m_w_gate_up,
             w_down=m_w_down)
    v = dict(norm_gains=v_norm_gains, mem_norm=v_mem_norm, w_in=v_w_in, w_mem_kv=v_w_mem_kv, w_out=v_w_out,
             w_pool=v_w_pool, pool_scale=v_pool_scale, kv_norm=v_kv_norm, w_kv=v_w_kv, w_gate_up=v_w_gate_up,
             w_down=v_w_down)

    def transposed_view(d):
        d = dict(d)
        d["w_gate_up"] = jnp.swapaxes(d["w_gate_up"], 1, 2)
        d["w_kv"] = jnp.swapaxes(d["w_kv"], 0, 1)
        return d

    wv, mv, vv = transposed_view(w), transposed_view(m), transposed_view(v)

    small = _pack_small(norm_gains, pool_scale)
    (gsmall,) = exchange([(small, "gather")], name="gather_small")
    P = {"norm_gains": jnp.moveaxis(gsmall[:, :16].reshape(N_DEV, 4, 4, 128), 0, 2).reshape(4, 4, D_MODEL),
         "pool_scale": jnp.moveaxis(gsmall[:, 16:18, :POOL_SHARD], 0, 1).reshape(2, MAIN_W),
         "mem_norm": mem_norm, "kv_norm": kv_norm, "w_pool": w_pool}

    PARTS = {"mix": ("w_in", "w_mem_kv", "w_out"), "ffn": ("w_gate_up", "w_down"), "gu": ("w_gate_up",),
             "down": ("w_down",), "all": ("w_in", "w_mem_kv", "w_out", "w_gate_up", "w_down")}

    def parts_of(l):
        return (("mix", "gu", "down"), ("mix", "ffn"))[l] if l < 2 else ("all",)

    def part_items(l, part):
        items = [(wv[k][l].astype(bf16), "gather") for k in PARTS[part]]
        if part == "ffn" and l == N_A_LAYERS - 1:
            items.append((wv["w_kv"].astype(bf16), "gather"))
        return items

    handles = {}

    def start_layer(l, after):
        for part in parts_of(l):
            if l in TWO_LEVEL_LAYERS:
                handles[l, part] = gather2_start([a for a, _ in part_items(l, part)], after,
                                                 name=f"gather_start_{part}_l{l}")
            else:
                handles[l, part] = exchange_start(part_items(l, part), after, name=f"gather_start_{part}_l{l}")
            after = handles[l, part]["token"]
        return after

    token = gsmall
    for l in range(LOOKAHEAD):
        token = start_layer(l, token)
    landed = {}

    def layer_weights(l, part, after):
        if part not in parts_of(l):
            if part == "down" or (part == "gu" and "all" in parts_of(l)):
                return {}, None
            part = "all" if "all" in parts_of(l) else "ffn"
        if l == 0 and part == "mix":
            after = token
        if l in TWO_LEVEL_LAYERS:
            passed = gather2_forward(handles[l, part], after, name=f"gather_forward_{part}_l{l}")
            got = gather2_wait(passed, after, name=f"gather_wait_{part}_l{l}")
        else:
            got = exchange_wait(handles[l, part], after, name=f"gather_wait_{part}_l{l}")
        landed[l, part] = got
        started = None
        if part in ("mix", "all") and l + LOOKAHEAD < DEPTH:
            started = start_layer(l + LOOKAHEAD, got[0])
        W = {k: g.reshape(-1, g.shape[-1]) for k, g in zip(PARTS[part], got)}
        return W, started

    def kv_weight(after):
        g = landed[N_A_LAYERS - 1, "ffn"][len(PARTS["ffn"])]
        return g.reshape(2 * MAIN_W, D_MODEL)

    ghandles = {}

    pending = {}

    def gparts_of(l):
        return ("ffn", "mix") if l == 0 else ("all",)

    def emit_grads(l, part, gw):
        if part not in gparts_of(l):
            pending.setdefault(l, {}).update(gw)
            if part == "ffn":
                return None
            gw, part = pending[l], "all"
        items = [(gw[k].reshape((N_DEV, -1) + gw[k].shape[-1:]), "scatter") for k in PARTS[part]]
        if part != "ffn" and l == N_A_LAYERS:
            items.append((gw["w_kv"].reshape(N_DEV, KV_SHARD, D_MODEL), "scatter"))
        if part != "ffn" and l < N_A_LAYERS:
            items.append((gw["w_pool"], "gather"))
        ghandles[l, part] = exchange_start(items, gsmall, name=f"scatter_start_{part}_l{l}")
        return ghandles[l, part]["token"]

    sq, grad_x, GS, emitted = local_step(x, mem, positions, loss_target, P, layer_weights, kv_weight, emit_grads)
    loss = lax.psum(0.5 * sq / D_MODEL, ("x", "y", "c"))

    def pool3(a):
        return a.reshape(N_A_LAYERS, MAIN_W, POOL_GROUP)

    out = {}
    after = [emitted]

    def finish_layer(l, after):
        for part in gparts_of(l):
            got = exchange_wait(ghandles[l, part], after, name=f"scatter_wait_{part}_l{l}")
            after = []
            for k, slots in zip(PARTS[part], got):
                out[k] = adamw(slots, wv[k], mv[k], vv[k], name=f"adamw_{k}_l{l}", layer=l, into=out.get(k))
                after.append(out[k][0])
            if part != "ffn" and l == N_A_LAYERS:
                out["w_kv"] = adamw(got[-1], wv["w_kv"], mv["w_kv"], vv["w_kv"], name="adamw_w_kv")
                after.append(out["w_kv"][0])
            if part != "ffn" and l < N_A_LAYERS:
                out["w_pool"] = adamw(got[-1], pool3(w_pool), pool3(m_w_pool), pool3(v_w_pool), name=f"adamw_w_pool_l{l}",
                                      layer=l, into=out.get("w_pool"))
                after.append(out["w_pool"][0])
        return after

    for l in reversed(range(1, DEPTH)):
        after = finish_layer(l, after)

    gs = _pack_small(jnp.moveaxis(GS["norm_gains"].reshape(4, 4, N_DEV, 128), 2, 0),
                     jnp.moveaxis(GS["pool_scale"].reshape(2, N_DEV, POOL_SHARD), 1, 0))
    parts_small, parts_repl = exchange(
        [(gs, "scatter"), (_pack_repl(GS["mem_norm"], GS["kv_norm"]), "gather")],
        name="exchange_small_grads", after=after)
    finish_layer(0, [parts_small])
    out["w_gate_up"] = [jnp.swapaxes(r, 1, 2) for r in out["w_gate_up"]]
    out["w_kv"] = [jnp.swapaxes(r, 0, 1) for r in out["w_kv"]]
    out["w_pool"] = [r.reshape(w_pool.shape) for r in out["w_pool"]]

    res = adamw(parts_small, small, _pack_small(m_norm_gains, m_pool_scale), _pack_small(v_norm_gains, v_pool_scale),
                name="adamw_small")
    out["norm_gains"], out["pool_scale"] = zip(*[_unpack_small(r) for r in res])
    res = adamw(parts_repl, _pack_repl(mem_norm, kv_norm), _pack_repl(m_mem_norm, m_kv_norm),
                _pack_repl(v_mem_norm, v_kv_norm), name="adamw_repl")
    out["mem_norm"], out["kv_norm"] = zip(*[_unpack_repl(r) for r in res])

    return (loss, grad_x, *[out[k][0] for k in WEIGHTS], *[out[k][1] for k in WEIGHTS],
            *[out[k][2] for k in WEIGHTS], *[out[k][3] for k in WEIGHTS])
```

```python
import numpy as np
import jax
import jax.numpy as jnp
from jax import lax
from jax.experimental import pallas as pl
from jax.experimental.pallas import tpu as pltpu

f32 = jnp.float32
bf16 = jnp.bfloat16

D_MODEL = 1024
SEQ = 2048
DEPTH = 4
N_MEM = 256
HEAD_DIM = 64
N_MEM_HEADS = 4
MEM_W = 256
MAIN_W = 768
POOL_WINDOWS = (2, 4, 8, 16)
POOL_GROUP = 192
POOL_HALO = 16
DIL_PATTERNS = ((128, 1), (512, 4), (2048, 16))
N_GROUPS = 3
GROUP_W = 256
BAND = 128
N_A_LAYERS = 2
D_FF = 2816
ROPE_THETA = 10000.0
EPS = 1e-6
NEG = -1e30
SCALE = HEAD_DIM ** -0.5
N_DEV = 8

ADAM_LR = 0.001
ADAM_B1 = 0.9
ADAM_B2 = 0.999
ADAM_EPS = 1e-08
ADAM_WD = 0.01
ADAM_STEP = 10

VMEM_LIMIT_BYTES = 56 * 1024 * 1024
MESH = pl.DeviceIdType.MESH

NN = (((1,), (0,)), ((), ()))
NT = (((1,), (1,)), ((), ()))
TN = (((0,), (0,)), ((), ()))


def _params(sem=None):
    return pltpu.CompilerParams(dimension_semantics=sem, vmem_limit_bytes=VMEM_LIMIT_BYTES)


def _tile(n, cands):
    for c in cands:
        if n % c == 0:
            return c
    return n


def _sds(shape, dtype):
    return jax.ShapeDtypeStruct(tuple(shape), dtype)


def _rms_r(v):
    return lax.rsqrt(jnp.mean(v * v, axis=-1, keepdims=True) + EPS)


def rms_matmul(x, gain, w, *, name, out_dtype, transposed=False, after=None):
    M, K = x.shape
    N = w.shape[0] if transposed else w.shape[1]
    tm = min(512, M)
    order = [] if after is None else [after]

    def body(x_ref, g_ref, w_ref, *refs):
        z_ref, h_ref = refs[len(order):]
        xv = x_ref[...]
        h = (xv * _rms_r(xv) * g_ref[...]).astype(bf16)
        h_ref[...] = h
        z_ref[...] = lax.dot_general(h, w_ref[...], NT if transposed else NN,
                                     preferred_element_type=f32).astype(z_ref.dtype)

    return pl.pallas_call(
        body, name=name, grid=(M // tm,),
        in_specs=[pl.BlockSpec((tm, K), lambda i: (i, 0)),
                  pl.BlockSpec((1, K), lambda i: (0, 0)),
                  pl.BlockSpec(w.shape, lambda i: (0, 0))] + [pl.BlockSpec(memory_space=pl.ANY)] * len(order),
        out_specs=[pl.BlockSpec((tm, N), lambda i: (i, 0)), pl.BlockSpec((tm, K), lambda i: (i, 0))],
        out_shape=[_sds((M, N), out_dtype), _sds((M, K), bf16)],
        compiler_params=_params(("parallel",)),
    )(x, gain, w, *order)


def matmul_rms_res(a, w, gain, res, *, name):
    M, K = a.shape
    N = w.shape[1]
    tm = min(512, M)

    def body(a_ref, w_ref, g_ref, r_ref, y_ref, x_ref):
        y = jnp.dot(a_ref[...], w_ref[...], preferred_element_type=f32)
        y_ref[...] = y.astype(bf16)
        x_ref[...] = r_ref[...] + y * _rms_r(y) * g_ref[...]

    row = pl.BlockSpec((tm, N), lambda i: (i, 0))
    return pl.pallas_call(
        body, name=name, grid=(M // tm,),
        in_specs=[pl.BlockSpec((tm, K), lambda i: (i, 0)),
                  pl.BlockSpec((K, N), lambda i: (0, 0)),
                  pl.BlockSpec((1, N), lambda i: (0, 0)),
                  row],
        out_specs=[row, row],
        out_shape=[_sds((M, N), bf16), _sds((M, N), f32)],
        compiler_params=_params(("parallel",)),
    )(a, w, gain, res)


def matmul(a, b, dims, *, name, out_dtype):
    if dims is TN:
        K, M = a.shape
        tm = _tile(M, (512, 256, 128))
        a_spec = pl.BlockSpec((K, tm), lambda i: (0, i))
    else:
        M, K = a.shape
        tm = _tile(M, (1024, 512, 256, 128))
        a_spec = pl.BlockSpec((tm, K), lambda i: (i, 0))
    N = b.shape[0] if dims is NT else b.shape[1]

    def body(a_ref, b_ref, o_ref):
        o_ref[...] = lax.dot_general(a_ref[...].astype(bf16), b_ref[...].astype(bf16), dims,
                                     preferred_element_type=f32).astype(o_ref.dtype)

    return pl.pallas_call(
        body, name=name, grid=(M // tm,),
        in_specs=[a_spec, pl.BlockSpec(b.shape, lambda i: (0, 0))],
        out_specs=pl.BlockSpec((tm, N), lambda i: (i, 0)),
        out_shape=_sds((M, N), out_dtype),
        compiler_params=_params(("parallel",)),
    )(a, b)


def rms_gate_up(x, gain, wt, *, name):
    M, K = x.shape
    tm = min(2048, M)
    tn = _tile(D_FF, (256, 128))
    nj = D_FF // tn

    def body(x_ref, gn_ref, wg_ref, wu_ref, g_ref, u_ref, a_ref, h_ref):
        @pl.when(pl.program_id(1) == 0)
        def _():
            xv = x_ref[...]
            h_ref[...] = (xv * _rms_r(xv) * gn_ref[...]).astype(bf16)

        h = h_ref[...]
        g = lax.dot_general(h, wg_ref[...], NT, preferred_element_type=f32).astype(bf16)
        u = lax.dot_general(h, wu_ref[...], NT, preferred_element_type=f32).astype(bf16)
        g_ref[...] = g
        u_ref[...] = u
        a_ref[...] = g * (1.0 / (1.0 + jnp.exp(-g))) * u

    col = pl.BlockSpec((tm, tn), lambda i, j: (i, j))
    return pl.pallas_call(
        body, name=name, grid=(M // tm, nj),
        in_specs=[pl.BlockSpec((tm, K), lambda i, j: (i, 0)),
                  pl.BlockSpec((1, K), lambda i, j: (0, 0)),
                  pl.BlockSpec((tn, K), lambda i, j: (j, 0)),
                  pl.BlockSpec((tn, K), lambda i, j: (j + nj, 0))],
        out_specs=[col, col, col, pl.BlockSpec((tm, K), lambda i, j: (i, 0))],
        out_shape=[_sds((M, D_FF), bf16)] * 3 + [_sds((M, K), bf16)],
        compiler_params=_params(("parallel", "arbitrary")),
    )(x, gain, wt, wt)


def _rms_bwd_math(yv, gain, dn):
    r = _rms_r(yv)
    q = dn * gain
    dy = r * q - yv * (r * r * r) * jnp.mean(q * yv, axis=-1, keepdims=True)
    return dy, jnp.sum(dn * yv * r, axis=0, keepdims=True)


def _accumulate(ref, val):
    @pl.when(pl.program_id(0) == 0)
    def _():
        ref[...] = jnp.zeros_like(ref)

    ref[...] += val


def down_bwd(y, gain, dn, w_down, g, u, *, name, after=None):
    M, K = y.shape
    tm = min(512, M)
    order = [] if after is None else [after]

    def body(y_ref, gn_ref, dn_ref, w_ref, g_ref, u_ref, *refs):
        dy_ref, o_ref, dg_ref = refs[len(order):]
        dy, dgain = _rms_bwd_math(y_ref[...].astype(f32), gn_ref[...], dn_ref[...])
        dy = dy.astype(bf16)
        dy_ref[...] = dy
        _accumulate(dg_ref, dgain)
        da = lax.dot_general(dy, w_ref[...], NT, preferred_element_type=f32).astype(bf16)
        g = g_ref[...]
        s = 1.0 / (1.0 + jnp.exp(-g))
        o_ref[:, :D_FF] = da * u_ref[...] * s * (1.0 + g * (1.0 - s))
        o_ref[:, D_FF:] = da * g * s

    row = pl.BlockSpec((tm, K), lambda i: (i, 0))
    vec = pl.BlockSpec((1, K), lambda i: (0, 0))
    wide = pl.BlockSpec((tm, D_FF), lambda i: (i, 0))
    return pl.pallas_call(
        body, name=name, grid=(M // tm,),
        in_specs=[row, vec, row, pl.BlockSpec((D_FF, K), lambda i: (0, 0)), wide, wide]
        + [pl.BlockSpec(memory_space=pl.ANY)] * len(order),
        out_specs=[row, pl.BlockSpec((tm, 2 * D_FF), lambda i: (i, 0)), vec],
        out_shape=[_sds((M, K), bf16), _sds((M, 2 * D_FF), bf16), _sds((1, K), f32)],
        compiler_params=_params(("arbitrary",)),
    )(y, gain, dn, w_down, g, u, *order)


def rms_bwd_matmul(y, gain, dn, w, dims, *, name, after=None):
    M, K = y.shape
    N = w.shape[0] if dims is NT else w.shape[1]
    tm = min(1024, M)
    order = [] if after is None else [after]

    def body(y_ref, gn_ref, dn_ref, w_ref, *refs):
        dy_ref, o_ref, dg_ref = refs[len(order):]
        dy, dgain = _rms_bwd_math(y_ref[...].astype(f32), gn_ref[...], dn_ref[...].astype(f32))
        dy = dy.astype(bf16)
        dy_ref[...] = dy
        _accumulate(dg_ref, dgain)
        o_ref[...] = lax.dot_general(dy, w_ref[...], dims, preferred_element_type=f32).astype(bf16)

    row = pl.BlockSpec((tm, K), lambda i: (i, 0))
    vec = pl.BlockSpec((1, K), lambda i: (0, 0))
    return pl.pallas_call(
        body, name=name, grid=(M // tm,),
        in_specs=[row, vec, row, pl.BlockSpec(w.shape, lambda i: (0, 0))]
        + [pl.BlockSpec(memory_space=pl.ANY)] * len(order),
        out_specs=[row, pl.BlockSpec((tm, N), lambda i: (i, 0)), vec],
        out_shape=[_sds((M, K), bf16), _sds((M, N), bf16), _sds((1, K), f32)],
        compiler_params=_params(("arbitrary",)),
    )(y, gain, dn, w, *order)


def matmul_rms_bwd(a, b, dims, y, gain, res, *, name):
    M, K = a.shape
    N = y.shape[1]
    tm = min(512, M)

    def body(a_ref, b_ref, y_ref, gn_ref, r_ref, dx_ref, dg_ref):
        dn = lax.dot_general(a_ref[...].astype(bf16), b_ref[...], dims, preferred_element_type=f32)
        dy, dgain = _rms_bwd_math(y_ref[...], gn_ref[...], dn)
        dx_ref[...] = dy + r_ref[...]
        _accumulate(dg_ref, dgain)

    row = pl.BlockSpec((tm, N), lambda i: (i, 0))
    vec = pl.BlockSpec((1, N), lambda i: (0, 0))
    return pl.pallas_call(
        body, name=name, grid=(M // tm,),
        in_specs=[pl.BlockSpec((tm, K), lambda i: (i, 0)), pl.BlockSpec(b.shape, lambda i: (0, 0)), row, vec, row],
        out_specs=[row, vec],
        out_shape=[_sds((M, N), f32), _sds((1, N), f32)],
        compiler_params=_params(("arbitrary",)),
    )(a, b, y, gain, res)


def rms_bwd(y, gain, dn, res, *, name, out_dtype, after=None):
    M, N = y.shape
    tm = min(512, M)
    has_res = res is not None
    order = [] if after is None else [after]

    def body(*refs):
        y_ref, g_ref, dn_ref = refs[:3]
        r_ref = refs[3] if has_res else None
        dy_ref, dg_ref = refs[-2:]
        dy, dgain = _rms_bwd_math(y_ref[...].astype(f32), g_ref[...], dn_ref[...].astype(f32))
        if has_res:
            dy = dy + r_ref[...]
        dy_ref[...] = dy.astype(dy_ref.dtype)
        _accumulate(dg_ref, dgain)

    row = pl.BlockSpec((tm, N), lambda i: (i, 0))
    vec = pl.BlockSpec((1, N), lambda i: (0, 0))
    args = [y, gain, dn] + ([res] if has_res else []) + order
    return pl.pallas_call(
        body, name=name, grid=(M // tm,),
        in_specs=[row, vec, row] + ([row] if has_res else []) + [pl.BlockSpec(memory_space=pl.ANY)] * len(order),
        out_specs=[row, vec],
        out_shape=[_sds((M, N), out_dtype), _sds((1, N), f32)],
        compiler_params=_params(("arbitrary",)),
    )(*args)


def loss_head(x, target, *, name):
    M, N = x.shape
    tm = min(512, M)

    def body(x_ref, t_ref, dx_ref, l_ref):
        e = x_ref[...] - t_ref[...]
        dx_ref[...] = e * (1.0 / N)

        @pl.when(pl.program_id(0) == 0)
        def _():
            l_ref[...] = jnp.zeros_like(l_ref)

        l_ref[...] += jnp.sum(jnp.sum(e * e, axis=0, keepdims=True), axis=1, keepdims=True)

    row = pl.BlockSpec((tm, N), lambda i: (i, 0))
    return pl.pallas_call(
        body, name=name, grid=(M // tm,),
        in_specs=[row, row],
        out_specs=[row, pl.BlockSpec((8, 128), lambda i: (0, 0))],
        out_shape=[_sds((M, N), f32), _sds((8, 128), f32)],
        compiler_params=_params(("arbitrary",)),
    )(x, target)


def _pool_select(a1, a2, a3, a4):
    col = lax.broadcasted_iota(jnp.int32, (1, MAIN_W), 1) // POOL_GROUP
    return jnp.where(col == 0, a1, jnp.where(col == 1, a2, jnp.where(col == 2, a3, a4)))


def _pool_count(t):
    col = lax.broadcasted_iota(jnp.int32, (1, MAIN_W), 1) // POOL_GROUP
    win = jnp.where(col == 0, 2, jnp.where(col == 1, 4, jnp.where(col == 2, 8, 16)))
    return jnp.minimum(t + 1, win).astype(f32)


def pool_fwd(z, wbd, scale, *, name):
    M = z.shape[0]
    tm = 256
    nper = SEQ // tm
    hb = tm // POOL_HALO

    def body(zc_ref, zh_ref, w_ref, s_ref, p_ref, y_ref):
        i = pl.program_id(0)
        seq_blk = i % nper
        halo = jnp.where(seq_blk == 0, 0.0, zh_ref[...].astype(f32))
        u = zc_ref[...].astype(f32)
        ext = jnp.concatenate([halo, u], axis=0)
        a1 = ext + pltpu.roll(ext, 1, 0)
        a2 = a1 + pltpu.roll(a1, 2, 0)
        a3 = a2 + pltpu.roll(a2, 4, 0)
        a4 = a3 + pltpu.roll(a3, 8, 0)
        sums = _pool_select(a1, a2, a3, a4)[POOL_HALO:]
        t = seq_blk * tm + lax.broadcasted_iota(jnp.int32, (tm, 1), 0)
        p = (sums / _pool_count(t) - u).astype(bf16)
        p_ref[...] = p
        y_ref[...] = (jnp.dot(p, w_ref[...], preferred_element_type=f32) * s_ref[...]).astype(bf16)

    return pl.pallas_call(
        body, name=name, grid=(M // tm,),
        in_specs=[pl.BlockSpec((tm, MAIN_W), lambda i: (i, 0)),
                  pl.BlockSpec((POOL_HALO, MAIN_W), lambda i: (jnp.maximum(i * hb - 1, 0), 0)),
                  pl.BlockSpec((MAIN_W, MAIN_W), lambda i: (0, 0)),
                  pl.BlockSpec((1, MAIN_W), lambda i: (0, 0))],
        out_specs=[pl.BlockSpec((tm, MAIN_W), lambda i: (i, 0)),
                   pl.BlockSpec((tm, MAIN_W), lambda i: (i, 0))],
        out_shape=[_sds((M, MAIN_W), bf16), _sds((M, D_MODEL), bf16)],
        compiler_params=_params(("parallel",)),
    )(z, z, wbd, scale)


def pool_bwd(dyc, p, wbd, scale, *, name):
    M = p.shape[0]
    tm = 256
    nper = SEQ // tm
    hb = tm // POOL_HALO
    last_hb = M // POOL_HALO - 1

    def body(dy_ref, dyh_ref, p_ref, w_ref, s_ref, dz_ref, dw_ref, ds_ref):
        i = pl.program_id(0)
        seq_blk = i % nper
        dy = dy_ref[...].astype(f32)
        pv = p_ref[...]
        w = w_ref[...]
        sc = s_ref[...]

        @pl.when(i == 0)
        def _():
            dw_ref[...] = jnp.zeros_like(dw_ref)
            ds_ref[...] = jnp.zeros_like(ds_ref)

        v = jnp.dot(pv, w, preferred_element_type=f32)
        ds_ref[...] += jnp.sum(dy * v, axis=0, keepdims=True)
        dv = (dy * sc).astype(bf16)
        dw_ref[...] += lax.dot_general(pv, dv, TN, preferred_element_type=f32)
        dp = lax.dot_general(dv, w, NT, preferred_element_type=f32)
        dvh = jnp.where(seq_blk == nper - 1, 0.0, dyh_ref[...].astype(f32) * sc).astype(bf16)
        dph = lax.dot_general(dvh, w, NT, preferred_element_type=f32)
        ext = jnp.concatenate([dp, dph], axis=0)
        n = tm + POOL_HALO
        t = seq_blk * tm + lax.broadcasted_iota(jnp.int32, (n, 1), 0)
        e = ext / _pool_count(t)
        b1 = e + pltpu.roll(e, n - 1, 0)
        b2 = b1 + pltpu.roll(b1, n - 2, 0)
        b3 = b2 + pltpu.roll(b2, n - 4, 0)
        b4 = b3 + pltpu.roll(b3, n - 8, 0)
        dz_ref[...] = (_pool_select(b1, b2, b3, b4)[:tm] - dp).astype(dz_ref.dtype)

    return pl.pallas_call(
        body, name=name, grid=(M // tm,),
        in_specs=[pl.BlockSpec((tm, MAIN_W), lambda i: (i, 0)),
                  pl.BlockSpec((POOL_HALO, MAIN_W), lambda i: (jnp.minimum((i + 1) * hb, last_hb), 0)),
                  pl.BlockSpec((tm, MAIN_W), lambda i: (i, 0)),
                  pl.BlockSpec((MAIN_W, MAIN_W), lambda i: (0, 0)),
                  pl.BlockSpec((1, MAIN_W), lambda i: (0, 0))],
        out_specs=[pl.BlockSpec((tm, MAIN_W), lambda i: (i, 0)),
                   pl.BlockSpec((MAIN_W, MAIN_W), lambda i: (0, 0)),
                   pl.BlockSpec((1, MAIN_W), lambda i: (0, 0))],
        out_shape=[_sds((M, D_MODEL), bf16), _sds((MAIN_W, MAIN_W), f32), _sds((1, MAIN_W), f32)],
        compiler_params=_params(("arbitrary",)),
    )(dyc, dyc, p, wbd, scale)


def _mem_heads(q, kv):
    first = _first_head()
    for pr in range(N_MEM_HEADS // 2):
        cols = slice(pr * PAIR_W, (pr + 1) * PAIR_W)
        qp = q[:, cols] * SCALE
        kp = kv[:, cols]
        vp = kv[:, MEM_W + pr * PAIR_W: MEM_W + (pr + 1) * PAIR_W]
        for hh in range(2):
            lm = first if hh == 0 else ~first
            qm = jnp.where(lm, qp, 0.0).astype(bf16)
            s = lax.dot_general(qm, kp, NT, preferred_element_type=f32)
            e = jnp.exp(s - jnp.max(s, axis=-1, keepdims=True))
            yield lm, qm, kp, vp, e, jnp.sum(e, axis=-1, keepdims=True)


def memattn_fwd(z, kvm, ycat, *, name, n_seq):
    M = z.shape[0]
    tq = 512
    nq = SEQ // tq

    def body(q_ref, kv_ref, _, o_ref):
        first = _first_head()
        outs = []
        for lm, _, _, vp, e, l in _mem_heads(q_ref[...], kv_ref[...]):
            outs.append(jnp.dot(e.astype(bf16), vp, preferred_element_type=f32) * (1.0 / l))
        pairs = [jnp.where(first, outs[2 * pr], outs[2 * pr + 1]) for pr in range(N_MEM_HEADS // 2)]
        o_ref[...] = jnp.concatenate(pairs, axis=1).astype(bf16)

    return pl.pallas_call(
        body, name=name, grid=(n_seq, nq),
        in_specs=[pl.BlockSpec((tq, MEM_W), lambda b, i: (b * nq + i, 3)),
                  pl.BlockSpec((N_MEM, 2 * MEM_W), lambda b, i: (b, 0)),
                  pl.BlockSpec(memory_space=pl.ANY)],
        out_specs=pl.BlockSpec((tq, MEM_W), lambda b, i: (b * nq + i, 3)),
        out_shape=_sds((M, D_MODEL), bf16),
        input_output_aliases={2: 0},
        compiler_params=_params(("parallel", "parallel")),
    )(z, kvm, ycat)


def memattn_bwd(z, kvm, dyc, dz, *, name, n_seq):
    M = z.shape[0]
    tq = 512
    nq = SEQ // tq

    def body(q_ref, kv_ref, dy_ref, _, dq_ref, dkv_ref):
        first = _first_head()
        dy = dy_ref[...].astype(f32)
        dqs, dks, dvs = [], [], []
        for h, (lm, qm, kp, vp, e, l) in enumerate(_mem_heads(q_ref[...], kv_ref[...])):
            pr = h // 2
            p = e * (1.0 / l)
            dym = jnp.where(lm, dy[:, pr * PAIR_W:(pr + 1) * PAIR_W], 0.0).astype(bf16)
            dp = lax.dot_general(dym, vp, NT, preferred_element_type=f32)
            ds = (p * (dp - jnp.sum(dp * p, axis=-1, keepdims=True))).astype(bf16)
            dqs.append(jnp.dot(ds, kp, preferred_element_type=f32) * SCALE)
            dk = lax.dot_general(ds, qm, TN, preferred_element_type=f32)
            dv = lax.dot_general(p.astype(bf16), dym, TN, preferred_element_type=f32)
            if h % 2 == 0:
                dks.append(dk)
                dvs.append(dv)
            else:
                dks[pr] = dks[pr] + dk
                dvs[pr] = dvs[pr] + dv
        pairs = [jnp.where(first, dqs[2 * pr], dqs[2 * pr + 1]) for pr in range(N_MEM_HEADS // 2)]
        dq_ref[...] = jnp.concatenate(pairs, axis=1).astype(bf16)

        @pl.when(pl.program_id(1) == 0)
        def _():
            dkv_ref[...] = jnp.zeros_like(dkv_ref)

        dkv_ref[...] += jnp.concatenate(dks + dvs, axis=1)

    return pl.pallas_call(
        body, name=name, grid=(n_seq, nq),
        in_specs=[pl.BlockSpec((tq, MEM_W), lambda b, i: (b * nq + i, 3)),
                  pl.BlockSpec((N_MEM, 2 * MEM_W), lambda b, i: (b, 0)),
                  pl.BlockSpec((tq, MEM_W), lambda b, i: (b * nq + i, 3)),
                  pl.BlockSpec(memory_space=pl.ANY)],
        out_specs=[pl.BlockSpec((tq, MEM_W), lambda b, i: (b * nq + i, 3)),
                   pl.BlockSpec((N_MEM, 2 * MEM_W), lambda b, i: (b, 0))],
        out_shape=[_sds((M, D_MODEL), bf16), _sds((n_seq * N_MEM, 2 * MEM_W), f32)],
        input_output_aliases={3: 0},
        compiler_params=_params(("parallel", "arbitrary")),
    )(z, kvm, dyc, dz)


def rope_tables(pos, *, name):
    M = pos.shape[0]
    tm = min(1024, M)
    half = HEAD_DIM // 2
    inv = ROPE_THETA ** (-np.arange(half, dtype=np.float64) / half)
    inv128 = jnp.asarray(np.tile(inv, 4)[None, :], f32)
    sign128 = jnp.asarray(np.tile(np.concatenate([-np.ones(half), np.ones(half)]), 2)[None, :], f32)

    def body(p_ref, f_ref, s_ref, cos_ref, sin_ref):
        ang = p_ref[...] * f_ref[...]
        cos_ref[...] = jnp.cos(ang)
        sin_ref[...] = jnp.sin(ang) * s_ref[...]

    return pl.pallas_call(
        body, name=name, grid=(M // tm,),
        in_specs=[pl.BlockSpec((tm, 1), lambda i: (i, 0)),
                  pl.BlockSpec((1, 128), lambda i: (0, 0)),
                  pl.BlockSpec((1, 128), lambda i: (0, 0))],
        out_specs=[pl.BlockSpec((tm, 128), lambda i: (i, 0)),
                   pl.BlockSpec((tm, 128), lambda i: (i, 0))],
        out_shape=[_sds((M, 128), f32), _sds((M, 128), f32)],
        compiler_params=_params(("parallel",)),
    )(pos, inv128, sign128)


def _swap_halves(x):
    w = x.shape[1]
    first = (lax.broadcasted_iota(jnp.int32, (1, w), 1) % HEAD_DIM) < (HEAD_DIM // 2)
    return jnp.where(first, pltpu.roll(x, w - HEAD_DIM // 2, 1), pltpu.roll(x, HEAD_DIM // 2, 1))


def rope_fwd(src, cos, sin, *, name):
    M = src.shape[0]
    tm = min(512, M)

    def body(x_ref, c_ref, s_ref, o_ref):
        x = x_ref[...].astype(f32)
        c = jnp.tile(c_ref[...], (1, MAIN_W // 128))
        s = jnp.tile(s_ref[...], (1, MAIN_W // 128))
        o_ref[...] = x * c + _swap_halves(x) * s

    return pl.pallas_call(
        body, name=name, grid=(M // tm,),
        in_specs=[pl.BlockSpec((tm, MAIN_W), lambda i: (i, 0)),
                  pl.BlockSpec((tm, 128), lambda i: (i, 0)),
                  pl.BlockSpec((tm, 128), lambda i: (i, 0))],
        out_specs=pl.BlockSpec((tm, MAIN_W), lambda i: (i, 0)),
        out_shape=_sds((M, MAIN_W), f32),
        compiler_params=_params(("parallel",)),
    )(src, cos, sin)


def group_sum(groups, cos, sin, *, name, rotate, width, col_block=0, into=None):
    M = groups[0][0].shape[0]
    tm = min(512, M)
    counts = [len(g) for g in groups]
    flat = [a for g in groups for a in g]
    extra = [] if into is None else [into]

    def body(*refs):
        part_refs = refs[:len(flat)]
        c_ref, s_ref = refs[len(flat):len(flat) + 2]
        o_ref = refs[-1]
        cols, k = [], 0
        for n in counts:
            acc = part_refs[k][...]
            for r in part_refs[k + 1:k + n]:
                acc = acc + r[...]
            cols.append(acc)
            k += n
        d = jnp.concatenate(cols, axis=1)
        if rotate:
            c = jnp.tile(c_ref[...], (1, MAIN_W // 128))
            s = jnp.tile(s_ref[...], (1, MAIN_W // 128))
            d = d * c - _swap_halves(d) * s
        o_ref[...] = d.astype(bf16)

    part = pl.BlockSpec((tm, GROUP_W), lambda i: (i, 0))
    tab = pl.BlockSpec((tm, 128), lambda i: (i, 0))
    return pl.pallas_call(
        body, name=name, grid=(M // tm,),
        in_specs=[part] * len(flat) + [tab, tab] + [pl.BlockSpec(memory_space=pl.ANY)] * len(extra),
        out_specs=pl.BlockSpec((tm, MAIN_W), lambda i: (i, col_block)),
        out_shape=_sds((M, width), bf16),
        input_output_aliases={len(flat) + 2: 0} if extra else {},
        compiler_params=_params(("parallel",)),
    )(*flat, cos, sin, *extra)


PAIR_W = 2 * HEAD_DIM
MIN_BLOCKS = 8


def _dil_geometry(dil):
    nsub = max(dil, MIN_BLOCKS)
    tb = BAND * nsub
    return nsub, tb, SEQ // tb


REGROUP = 4


class _Regrouped:
    def __init__(self, ref):
        self.ref = ref
        self.shape = ref.shape

    def fill(self, src):
        q = self.shape[0] // REGROUP
        for r0 in range(REGROUP):
            self.ref[r0 * q:(r0 + 1) * q, :] = src[pl.ds(r0, q, stride=REGROUP), :]

    def drain(self, dst):
        q = self.shape[0] // REGROUP
        for r0 in range(REGROUP):
            dst[pl.ds(r0, q, stride=REGROUP), :] = self.ref[r0 * q:(r0 + 1) * q, :]

    def rows(self, sub, dil):
        nl, r = divmod(sub, dil)
        start = (r % REGROUP) * (self.shape[0] // REGROUP) + r // REGROUP + nl * BAND * (dil // REGROUP)
        return pl.ds(start, BAND, stride=dil // REGROUP)


def _regroups(dil):
    return dil % (4 * REGROUP) == 0


def _rows(ref, sub, dil):
    if isinstance(ref, _Regrouped):
        return ref.ref[ref.rows(sub, dil), :]
    if dil == 1:
        return ref[sub * BAND:(sub + 1) * BAND, :]
    nl, r = divmod(sub, dil)
    return ref[pl.ds(nl * BAND * dil + r, BAND, stride=dil), :]


def _store_rows(ref, sub, dil, val):
    if isinstance(ref, _Regrouped):
        ref.ref[ref.rows(sub, dil), :] = val
    elif dil == 1:
        ref[sub * BAND:(sub + 1) * BAND, :] = val
    else:
        nl, r = divmod(sub, dil)
        ref[pl.ds(nl * BAND * dil + r, BAND, stride=dil), :] = val


def _keys(prev_ref, own_ref, sub, dil):
    nsub = own_ref.shape[0] // BAND
    if sub >= dil:
        prev = _rows(own_ref, sub - dil, dil)
    elif prev_ref is None:
        return _rows(own_ref, sub, dil)
    else:
        prev = _rows(prev_ref, nsub - dil + sub, dil)
    return jnp.concatenate([prev, _rows(own_ref, sub, dil)], axis=0)


def _band_mask(nkeys, has_prev):
    i = lax.broadcasted_iota(jnp.int32, (BAND, nkeys), 0)
    j = lax.broadcasted_iota(jnp.int32, (BAND, nkeys), 1)
    if nkeys == BAND:
        return j <= i
    return (j >= i) & (j <= i + BAND) & (has_prev | (j >= BAND))


def _first_head():
    return lax.broadcasted_iota(jnp.int32, (1, PAIR_W), 1) < HEAD_DIM


def _col(x, hh):
    return x[:, hh * HEAD_DIM:hh * HEAD_DIM + 1]


def _pair_spec(tb, nblk, col0, which):
    def idx(b, p, i):
        if which < 0:
            i = jnp.maximum(i - 1, 0)
        elif which > 0:
            i = jnp.minimum(i + 1, nblk - 1)
        return (b * nblk + i, col0 + p)
    return pl.BlockSpec((tb, PAIR_W), idx)


def dil_fwd(q, k, kv, g, dil, *, name, n_seq):
    M = q.shape[0]
    nsub, tb, nblk = _dil_geometry(dil)
    with_prev = nblk > 1

    regroup = _regroups(dil)
    assert not (regroup and with_prev)

    def body(*refs):
        if with_prev:
            q_ref, ko_ref, vo_ref, kp_ref, vp_ref, o_ref, l_ref = refs
        else:
            (q_ref, ko_ref, vo_ref, o_ref, l_ref), kp_ref, vp_ref = refs[:5], None, None
        outs_to = ()
        if regroup:
            copies = [_Regrouped(s) for s in refs[5:]]
            for c, src in zip(copies, (q_ref, ko_ref, vo_ref)):
                c.fill(src)
            outs_to = ((copies[3], o_ref), (copies[4], l_ref))
            q_ref, ko_ref, vo_ref, o_ref, l_ref = copies
        first = _first_head()
        blk = pl.program_id(2)
        for sub in range(nsub):
            qs = _rows(q_ref, sub, dil) * SCALE
            kc = _keys(kp_ref, ko_ref, sub, dil).astype(bf16)
            vc = _keys(vp_ref, vo_ref, sub, dil).astype(bf16)
            has_prev = True if sub >= dil else blk > 0
            mask = _band_mask(kc.shape[0], has_prev)
            outs, lses = [], []
            for hh in range(2):
                qm = jnp.where(first if hh == 0 else ~first, qs, 0.0).astype(bf16)
                s = jnp.where(mask, lax.dot_general(qm, kc, NT, preferred_element_type=f32), NEG)
                m = jnp.max(s, axis=-1, keepdims=True)
                e = jnp.exp(s - m)
                l = jnp.sum(e, axis=-1, keepdims=True)
                outs.append(jnp.dot(e.astype(bf16), vc, preferred_element_type=f32) * (1.0 / l))
                lses.append(jnp.broadcast_to(m + jnp.log(l), (BAND, PAIR_W)))
            _store_rows(o_ref, sub, dil, jnp.where(first, outs[0], outs[1]))
            _store_rows(l_ref, sub, dil, jnp.where(first, lses[0], lses[1]))
        for c, dst in outs_to:
            c.drain(dst)

    ins = [(q, 2 * g, 0), (k, 2 * g, 0), (kv, 6 + 2 * g, 0)]
    if with_prev:
        ins += [(k, 2 * g, -1), (kv, 6 + 2 * g, -1)]
    out = _pair_spec(tb, nblk, 0, 0)
    return pl.pallas_call(
        body, name=name, grid=(n_seq, 2, nblk),
        in_specs=[_pair_spec(tb, nblk, c, w) for _, c, w in ins],
        out_specs=[out, out],
        out_shape=[_sds((M, GROUP_W), f32)] * 2,
        scratch_shapes=[pltpu.VMEM((tb, PAIR_W), f32)] * (5 if regroup else 0),
        compiler_params=_params(("parallel", "parallel", "arbitrary")),
    )(*[a for a, _, _ in ins])


def combine_fwd(os_, lses, *, name):
    M = os_[0].shape[0]
    tm = min(512, M)

    def body(o0, o1, o2, l0, l1, l2, y_ref):
        ls = [l0[...], l1[...], l2[...]]
        m = jnp.maximum(jnp.maximum(ls[0], ls[1]), ls[2])
        es = [jnp.exp(l - m) for l in ls]
        inv = 1.0 / (es[0] + es[1] + es[2])
        y_ref[...] = jnp.concatenate([o[...] * e * inv for o, e in zip((o0, o1, o2), es)], axis=1).astype(bf16)

    part = pl.BlockSpec((tm, GROUP_W), lambda i: (i, 0))
    return pl.pallas_call(
        body, name=name, grid=(M // tm,),
        in_specs=[part] * 6,
        out_specs=pl.BlockSpec((tm, MAIN_W), lambda i: (i, 0)),
        out_shape=_sds((M, D_MODEL), bf16),
        compiler_params=_params(("parallel",)),
    )(*os_, *lses)


def combine_bwd(dyc, os_, lses, *, name):
    M = os_[0].shape[0]
    tm = min(512, M)

    def body(dy_ref, o0, o1, o2, l0, l1, l2, d0, d1, d2, c0, c1, c2):
        r = lax.broadcasted_iota(jnp.int32, (GROUP_W, GROUP_W), 0) // HEAD_DIM
        c = lax.broadcasted_iota(jnp.int32, (GROUP_W, GROUP_W), 1) // HEAD_DIM
        ones = (r == c).astype(f32)
        dy = dy_ref[...].astype(f32)
        ls = [l0[...], l1[...], l2[...]]
        m = jnp.maximum(jnp.maximum(ls[0], ls[1]), ls[2])
        es = [jnp.exp(l - m) for l in ls]
        inv = 1.0 / (es[0] + es[1] + es[2])
        total = 0.0
        alphas = []
        for g, (o, e, d_ref) in enumerate(zip((o0, o1, o2), es, (d0, d1, d2))):
            a = e * inv
            dyg = dy[:, g * GROUP_W:(g + 1) * GROUP_W]
            d_ref[...] = dyg * a
            dsum = jnp.dot(dyg * o[...], ones, precision=lax.Precision.HIGHEST, preferred_element_type=f32)
            total = total + a * dsum
            alphas.append(a)
        for a, c_ref in zip(alphas, (c0, c1, c2)):
            c_ref[...] = -a * total

    part = pl.BlockSpec((tm, GROUP_W), lambda i: (i, 0))
    outs = pl.pallas_call(
        body, name=name, grid=(M // tm,),
        in_specs=[pl.BlockSpec((tm, MAIN_W), lambda i: (i, 0))] + [part] * 6,
        out_specs=[part] * 6,
        out_shape=[_sds((M, GROUP_W), f32)] * 6,
        compiler_params=_params(("parallel",)),
    )(dyc, *os_, *lses)
    return outs[:3], outs[3:]


def dil_bwd(q, k, kv, do, cc, lse, g, dil, *, name, n_seq):
    M = q.shape[0]
    nsub = SEQ // BAND
    per_res = nsub // dil

    regroup = _regroups(dil)

    def body(q_ref, k_ref, v_ref, do_ref, c_ref, l_ref, dq_ref, dk_ref, dv_ref, *scratch):
        outs_to = ()
        if regroup:
            copies = [_Regrouped(s) for s in scratch]
            for c, src in zip(copies, (q_ref, k_ref, v_ref, do_ref, c_ref, l_ref)):
                c.fill(src)
            outs_to = tuple(zip(copies[6:], (dq_ref, dk_ref, dv_ref)))
            q_ref, k_ref, v_ref, do_ref, c_ref, l_ref, dq_ref, dk_ref, dv_ref = copies
        first = _first_head()
        for r in range(dil):
            carry = None
            for nl in range(per_res):
                sub = nl * dil + r
                qs = _rows(q_ref, sub, dil) * SCALE
                dos = _rows(do_ref, sub, dil)
                cs = _rows(c_ref, sub, dil)
                ls = _rows(l_ref, sub, dil)
                kc = _keys(None, k_ref, sub, dil).astype(bf16)
                vc = _keys(None, v_ref, sub, dil).astype(bf16)
                nkeys = kc.shape[0]
                mask = _band_mask(nkeys, True)
                dqs = []
                dkc = jnp.zeros((nkeys, PAIR_W), f32)
                dvc = jnp.zeros((nkeys, PAIR_W), f32)
                for hh in range(2):
                    lm = first if hh == 0 else ~first
                    qm = jnp.where(lm, qs, 0.0).astype(bf16)
                    dom = jnp.where(lm, dos, 0.0).astype(bf16)
                    s = jnp.where(mask, lax.dot_general(qm, kc, NT, preferred_element_type=f32), NEG)
                    p = jnp.exp(s - _col(ls, hh))
                    dp = lax.dot_general(dom, vc, NT, preferred_element_type=f32)
                    ds = (p * (dp + _col(cs, hh))).astype(bf16)
                    dqs.append(jnp.dot(ds, kc, preferred_element_type=f32) * SCALE)
                    dkc = dkc + lax.dot_general(ds, qm, TN, preferred_element_type=f32)
                    dvc = dvc + lax.dot_general(p.astype(bf16), dom, TN, preferred_element_type=f32)
                _store_rows(dq_ref, sub, dil, jnp.where(first, dqs[0], dqs[1]))
                if nkeys == 2 * BAND:
                    _store_rows(dk_ref, sub - dil, dil, carry[0] + dkc[:BAND])
                    _store_rows(dv_ref, sub - dil, dil, carry[1] + dvc[:BAND])
                    carry = (dkc[BAND:], dvc[BAND:])
                else:
                    carry = (dkc, dvc)
            _store_rows(dk_ref, (per_res - 1) * dil + r, dil, carry[0])
            _store_rows(dv_ref, (per_res - 1) * dil + r, dil, carry[1])
        for c, dst in outs_to:
            c.drain(dst)

    def spec(col0):
        return pl.BlockSpec((SEQ, PAIR_W), lambda b, p: (b, col0 + p))

    out = spec(0)
    return pl.pallas_call(
        body, name=name, grid=(n_seq, 2),
        in_specs=[spec(2 * g), spec(2 * g), spec(6 + 2 * g), spec(0), spec(0), spec(0)],
        out_specs=[out, out, out],
        out_shape=[_sds((M, GROUP_W), f32)] * 3,
        scratch_shapes=[pltpu.VMEM((SEQ, PAIR_W), f32)] * (9 if regroup else 0),
        compiler_params=_params(("parallel", "parallel")),
    )(q, k, kv, do, cc, lse)


def _blockdiag(wp):
    out = jnp.zeros((MAIN_W, MAIN_W), wp.dtype)
    for gi in range(len(POOL_WINDOWS)):
        sl = slice(gi * POOL_GROUP, (gi + 1) * POOL_GROUP)
        out = out.at[sl, sl].set(wp[gi])
    return out


def _unblockdiag(w):
    return jnp.stack([w[gi * POOL_GROUP:(gi + 1) * POOL_GROUP, gi * POOL_GROUP:(gi + 1) * POOL_GROUP]
                      for gi in range(len(POOL_WINDOWS))])


def local_step(x, mem, positions, target, P, layer_weights, kv_weight, emit_grads):
    n_seq = x.shape[0]
    M = n_seq * SEQ
    xs = x.reshape(M, D_MODEL)
    mems = mem.reshape(n_seq * N_MEM, D_MODEL)
    pos = positions.reshape(M, 1).astype(f32)
    cos, sin = rope_tables(pos, name="rope_tables")
    gains = P["norm_gains"]

    def gain(l, k):
        return gains[l, k].reshape(1, D_MODEL)

    saved = []
    kvs = None
    for l in range(DEPTH):
        W, started = layer_weights(l, "mix", xs)
        sv = {"x": xs, "W": W}
        z, h1 = rms_matmul(xs, gain(l, 0), W["w_in"], name=f"l{l}_in", out_dtype=bf16, after=started)
        kvm, mn = rms_matmul(mems, P["mem_norm"][l].reshape(1, D_MODEL), W["w_mem_kv"],
                             name=f"l{l}_memkv", out_dtype=bf16)
        sv.update(z=z, h1=h1, kvm=kvm, mn=mn)
        if l < N_A_LAYERS:
            wbd = _blockdiag(P["w_pool"][l].astype(bf16))
            psc = P["pool_scale"][l].reshape(1, MAIN_W)
            p, y_main = pool_fwd(z, wbd, psc, name=f"l{l}_pool")
            sv.update(p=p, wbd=wbd, psc=psc)
        else:
            qrot = rope_fwd(z, cos, sin, name=f"l{l}_ropeq")
            os_, lses = [], []
            for g, (_, dil) in enumerate(DIL_PATTERNS):
                o, lse = dil_fwd(qrot, kvs["krot"], kvs["kv"], g, dil, name=f"l{l}_dil{g}", n_seq=n_seq)
                os_.append(o)
                lses.append(lse)
            y_main = combine_fwd(os_, lses, name=f"l{l}_comb")
            sv.update(qrot=qrot, os=os_, lses=lses)
        ycat = memattn_fwd(z, kvm, y_main, name=f"l{l}_memattn", n_seq=n_seq)
        y, x1 = matmul_rms_res(ycat, W["w_out"], gain(l, 1), xs, name=f"l{l}_out")
        W.update(layer_weights(l, "gu", x1)[0])
        fg, fu, a, h2 = rms_gate_up(x1, gain(l, 2), W["w_gate_up"], name=f"l{l}_gu")
        W.update(layer_weights(l, "down", a)[0])
        y2, x2 = matmul_rms_res(a, W["w_down"], gain(l, 3), x1, name=f"l{l}_down")
        sv.update(ycat=ycat, y=y, x1=x1, fg=fg, fu=fu, h2=h2, a=a, y2=y2)
        saved.append(sv)
        xs = x2
        if l == N_A_LAYERS - 1:
            w_kv = kv_weight(xs)
            kv, hkv = rms_matmul(xs, P["kv_norm"].reshape(1, D_MODEL), w_kv, name="kv_proj", out_dtype=f32,
                                 transposed=True)
            krot = rope_fwd(kv, cos, sin, name="ropek")
            kvs = {"kv": kv, "hkv": hkv, "krot": krot, "x": xs, "w_kv": w_kv}

    dx, sq = loss_head(xs, target.reshape(M, D_MODEL), name="loss_head")

    G = {"mem_norm": [None] * DEPTH, "norm_gains": [[None] * 4 for _ in range(DEPTH)],
         "pool_scale": [None] * N_A_LAYERS}
    dk_parts = [[] for _ in range(N_GROUPS)]
    dv_parts = [[] for _ in range(N_GROUPS)]
    emitted = None

    for l in reversed(range(DEPTH)):
        sv = saved[l]
        W = sv["W"]
        gw = {}
        dy2, dgu, G["norm_gains"][l][3] = down_bwd(sv["y2"], gain(l, 3), dx, W["w_down"], sv["fg"], sv["fu"],
                                                   name=f"l{l}_b_dgu", after=emitted)
        gw["w_down"] = matmul(sv["a"], dy2, TN, name=f"l{l}_b_wd", out_dtype=bf16)
        dx1, G["norm_gains"][l][2] = matmul_rms_bwd(dgu, W["w_gate_up"], NN, sv["x1"], gain(l, 2), dx,
                                                    name=f"l{l}_b_dh2")
        gw["w_gate_up"] = matmul(dgu, sv["h2"], TN, name=f"l{l}_b_wgu", out_dtype=bf16)
        emitted = emit_grads(l, "ffn", gw)
        gw = {}
        dy, dycat, G["norm_gains"][l][1] = rms_bwd_matmul(sv["y"], gain(l, 1), dx1, W["w_out"], NT,
                                                          name=f"l{l}_b_dycat", after=emitted)
        gw["w_out"] = matmul(sv["ycat"], dy, TN, name=f"l{l}_b_wout", out_dtype=bf16)
        if l < N_A_LAYERS:
            dz, dwbd, dps = pool_bwd(dycat, sv["p"], sv["wbd"], sv["psc"], name=f"l{l}_b_pool")
            gw["w_pool"] = _unblockdiag(dwbd).reshape(MAIN_W, POOL_GROUP).astype(bf16)
            G["pool_scale"][l] = dps.reshape(MAIN_W)
        else:
            dos, ccs = combine_bwd(dycat, sv["os"], sv["lses"], name=f"l{l}_b_comb")
            dqs = []
            for g, (_, dil) in enumerate(DIL_PATTERNS):
                args = (sv["qrot"], kvs["krot"], kvs["kv"], dos[g], ccs[g], sv["lses"][g], g, dil)
                dq, dk, dv = dil_bwd(*args, name=f"l{l}_b_dil{g}", n_seq=n_seq)
                dqs.append([dq])
                dk_parts[g].append(dk)
                dv_parts[g].append(dv)
            dz = group_sum(dqs, cos, sin, name=f"l{l}_b_ropeq", rotate=True, width=D_MODEL)
        dz, dkvm = memattn_bwd(sv["z"], sv["kvm"], dycat, dz, name=f"l{l}_b_memattn", n_seq=n_seq)
        gw["w_mem_kv"] = matmul(sv["mn"], dkvm, TN, name=f"l{l}_b_wmkv", out_dtype=bf16)
        _, G["mem_norm"][l] = matmul_rms_bwd(dkvm, W["w_mem_kv"], NT, mems, P["mem_norm"][l].reshape(1, D_MODEL),
                                             mems, name=f"l{l}_b_dmn")
        gw["w_in"] = matmul(sv["h1"], dz, TN, name=f"l{l}_b_win", out_dtype=bf16)
        dx, G["norm_gains"][l][0] = matmul_rms_bwd(dz, W["w_in"], NT, sv["x"], gain(l, 0), dx1, name=f"l{l}_b_dh1")
        if l == N_A_LAYERS:
            dkv = group_sum(dk_parts, cos, sin, name="b_ropek", rotate=True, width=2 * MAIN_W)
            dkv = group_sum(dv_parts, cos, sin, name="b_sumv", rotate=False, width=2 * MAIN_W, col_block=1, into=dkv)
            gw["w_kv"] = matmul(dkv, kvs["hkv"], TN, name="b_wkv", out_dtype=bf16)
            dx, gkn = matmul_rms_bwd(dkv, kvs["w_kv"], NN, kvs["x"], P["kv_norm"].reshape(1, D_MODEL), dx,
                                     name="b_dhkv")
            G["kv_norm"] = gkn.reshape(D_MODEL)
        emitted = emit_grads(l, "mix", gw)

    small = {"pool_scale": jnp.stack(G["pool_scale"]),
             "mem_norm": jnp.concatenate(G["mem_norm"], axis=0),
             "norm_gains": jnp.stack([jnp.concatenate(r, axis=0) for r in G["norm_gains"]]),
             "kv_norm": G["kv_norm"]}
    return sq[0, 0], dx.reshape(n_seq, SEQ, D_MODEL), small, emitted


def _peer(k):
    x, y, c = lax.axis_index("x"), lax.axis_index("y"), lax.axis_index("c")
    px = 1 - x if k & 4 else x
    py = 1 - y if k & 2 else y
    pc = 1 - c if k & 1 else c
    return (px, py, pc), 4 * px + 2 * py + pc


def _my_index():
    return 4 * lax.axis_index("x") + 2 * lax.axis_index("y") + lax.axis_index("c")


def _src_for(kinds, in_refs, i, idx):
    return in_refs[i] if kinds[i] == "gather" else in_refs[i].at[idx]


def _local_copies(kinds, in_refs, out_refs, local_sems):
    me = _my_index()
    return [pltpu.make_async_copy(_src_for(kinds, in_refs, i, me), out_refs[i].at[me], local_sems.at[i])
            for i in range(len(kinds))]


def _remote_copies(kinds, in_refs, out_refs, send_sems, recv_sems, *, arriving):
    me = _my_index()
    copies = []
    for k in range(1, N_DEV):
        dev, idx = _peer(k)
        for i in range(len(kinds)):
            j = i * (N_DEV - 1) + k - 1
            copies.append(pltpu.make_async_remote_copy(
                src_ref=_src_for(kinds, in_refs, i, idx), dst_ref=out_refs[i].at[idx if arriving else me],
                send_sem=send_sems.at[j], recv_sem=recv_sems.at[j], device_id=dev, device_id_type=MESH))
    return copies


def _out_shape(a, kind):
    return ((N_DEV,) + a.shape) if kind == "gather" else a.shape


def exchange(items, *, name, after=()):
    n = len(items)
    kinds = [k for _, k in items]
    after = list(after)

    def body(*refs):
        in_refs, out_refs = refs[:n], refs[n + len(after):2 * n + len(after)]
        send_sems, recv_sems, local_sems = refs[-3:]
        local = _local_copies(kinds, in_refs, out_refs, local_sems)
        sends = _remote_copies(kinds, in_refs, out_refs, send_sems, recv_sems, arriving=False)
        for cp in local + sends:
            cp.start()
        for cp in _remote_copies(kinds, in_refs, out_refs, send_sems, recv_sems, arriving=True):
            cp.wait_recv()
        for cp in sends:
            cp.wait_send()
        for cp in local:
            cp.wait()

    any_spec = pl.BlockSpec(memory_space=pl.ANY)
    return pl.pallas_call(
        body, name=name,
        in_specs=[any_spec] * (n + len(after)), out_specs=[any_spec] * n,
        out_shape=[_sds(_out_shape(a, k), a.dtype) for a, k in items],
        scratch_shapes=[pltpu.SemaphoreType.DMA((n * (N_DEV - 1),)), pltpu.SemaphoreType.DMA((n * (N_DEV - 1),)),
                        pltpu.SemaphoreType.DMA((n,))],
    )(*[a for a, _ in items], *after)


_HBM = pl.BlockSpec(memory_space=pltpu.HBM)
_SEM = pl.BlockSpec(memory_space=pltpu.SEMAPHORE)
_EFFECT = pltpu.SideEffectType.DATAFLOW_SIDE_EFFECTING


def exchange_start(items, after, *, name):
    n = len(items)
    kinds = [k for _, k in items]

    def body(*refs):
        in_refs, land_refs = refs[:n], refs[n:2 * n]
        send_sems, recv_sems, local_sems = refs[2 * n + 1:2 * n + 4]
        token = refs[-1]
        for cp in (_local_copies(kinds, in_refs, land_refs, local_sems)
                   + _remote_copies(kinds, in_refs, land_refs, send_sems, recv_sems, arriving=False)):
            cp.start()
        token[...] = jnp.zeros_like(token)

    srcs = [pltpu.with_memory_space_constraint(a, pltpu.HBM) for a, _ in items]
    lands = [pltpu.with_memory_space_constraint(lax.empty(_out_shape(a, k), a.dtype), pltpu.HBM) for a, k in items]
    outs = pl.pallas_call(
        body, name=name,
        out_shape=(pltpu.SemaphoreType.DMA((n * (N_DEV - 1),)), pltpu.SemaphoreType.DMA((n * (N_DEV - 1),)),
                   pltpu.SemaphoreType.DMA((n,)),
                   *[pltpu.HBM(a.shape, a.dtype) for a in srcs], *[pltpu.HBM(a.shape, a.dtype) for a in lands],
                   _sds((8, 128), f32)),
        in_specs=[_HBM] * (2 * n) + [pl.BlockSpec(memory_space=pl.ANY)],
        out_specs=(_SEM, _SEM, _SEM, *[_HBM] * (2 * n), pl.BlockSpec(memory_space=pltpu.VMEM)),
        input_output_aliases={i: 3 + i for i in range(2 * n)},
        compiler_params=pltpu.CompilerParams(has_side_effects=_EFFECT),
    )(*srcs, *lands, after)
    return {"kinds": kinds, "sems": outs[:3], "srcs": outs[3:3 + n], "lands": outs[3 + n:3 + 2 * n], "token": outs[-1]}


def exchange_wait(handle, after, *, name):
    kinds = handle["kinds"]
    n = len(kinds)

    def body(*refs):
        in_refs, land_refs = refs[:n], refs[n:2 * n]
        send_sems, recv_sems, local_sems = refs[2 * n:2 * n + 3]
        for cp in _remote_copies(kinds, in_refs, land_refs, send_sems, recv_sems, arriving=True):
            cp.wait_recv()
        for cp in _remote_copies(kinds, in_refs, land_refs, send_sems, recv_sems, arriving=False):
            cp.wait_send()
        for cp in _local_copies(kinds, in_refs, land_refs, local_sems):
            cp.wait()

    srcs, lands = list(handle["srcs"]), list(handle["lands"])
    after = list(after) if isinstance(after, (list, tuple)) else [after]
    outs = pl.pallas_call(
        body, name=name,
        out_shape=tuple(pltpu.HBM(a.shape, a.dtype) for a in srcs + lands),
        in_specs=[_HBM] * (2 * n) + [_SEM] * 3 + [pl.BlockSpec(memory_space=pl.ANY)] * len(after),
        out_specs=tuple([_HBM] * (2 * n)),
        input_output_aliases={i: i for i in range(2 * n)},
        compiler_params=pltpu.CompilerParams(has_side_effects=_EFFECT),
    )(*srcs, *lands, *handle["sems"], *after)
    return list(outs[n:])


CHIP_MASKS = (2, 4, 6)


def _g2_first(in_refs, land_refs, send_sems, recv_sems, *, masks, arriving):
    me = _my_index()
    copies = []
    for i in range(len(land_refs)):
        for j, k in enumerate(masks):
            dev, idx = _peer(k)
            dst = land_refs[i].at[idx if arriving else me]
            copies.append(pltpu.make_async_remote_copy(
                src_ref=dst if in_refs is None else in_refs[i], dst_ref=dst,
                send_sem=send_sems.at[i * len(masks) + j], recv_sem=recv_sems.at[i * len(masks) + j],
                device_id=dev, device_id_type=MESH))
    return copies


def _g2_forward(land_refs, fwd_send, fwd_recv, *, arriving):
    sibling, _ = _peer(1)
    copies = []
    for i in range(len(land_refs)):
        for j, k in enumerate(CHIP_MASKS):
            _, idx = _peer(k | 1 if arriving else k)
            copies.append(pltpu.make_async_remote_copy(
                src_ref=land_refs[i].at[idx], dst_ref=land_refs[i].at[idx],
                send_sem=fwd_send.at[i * 3 + j], recv_sem=fwd_recv.at[i * 3 + j], device_id=sibling,
                device_id_type=MESH))
    return copies


def gather2_start(arrays, after, *, name):
    n = len(arrays)

    def body(*refs):
        in_refs, land_refs = refs[:n], refs[n:2 * n]
        ici_send, ici_recv, d2d_send, d2d_recv, local_sems = refs[2 * n + 1:2 * n + 6]
        token = refs[-1]
        ici = _g2_first(in_refs, land_refs, ici_send, ici_recv, masks=CHIP_MASKS, arriving=False)
        d2d = _g2_first(in_refs, land_refs, d2d_send, d2d_recv, masks=(1,), arriving=False)
        for cp in _local_copies(["gather"] * n, in_refs, land_refs, local_sems) + ici + d2d:
            cp.start()
        token[...] = jnp.zeros_like(token)

    srcs = [pltpu.with_memory_space_constraint(a, pltpu.HBM) for a in arrays]
    lands = [pltpu.with_memory_space_constraint(lax.empty((N_DEV,) + a.shape, a.dtype), pltpu.HBM) for a in arrays]
    sem = pltpu.SemaphoreType.DMA
    outs = pl.pallas_call(
        body, name=name,
        out_shape=(sem((3 * n,)), sem((3 * n,)), sem((n,)), sem((n,)), sem((n,)),
                   *[pltpu.HBM(a.shape, a.dtype) for a in srcs], *[pltpu.HBM(a.shape, a.dtype) for a in lands],
                   _sds((8, 128), f32)),
        in_specs=[_HBM] * (2 * n) + [pl.BlockSpec(memory_space=pl.ANY)],
        out_specs=(*[_SEM] * 5, *[_HBM] * (2 * n), pl.BlockSpec(memory_space=pltpu.VMEM)),
        input_output_aliases={i: 5 + i for i in range(2 * n)},
        compiler_params=pltpu.CompilerParams(has_side_effects=_EFFECT),
    )(*srcs, *lands, after)
    return {"n": n, "sems": outs[:5], "srcs": outs[5:5 + n], "lands": outs[5 + n:5 + 2 * n], "token": outs[-1]}


def gather2_forward(handle, after, *, name):
    n = handle["n"]

    def body(*refs):
        land_refs = refs[:n]
        ici_recv = refs[n]
        fwd_send, fwd_recv = refs[n + 2:n + 4]
        for cp in _g2_first(None, land_refs, fwd_send, ici_recv, masks=CHIP_MASKS, arriving=True):
            cp.wait_recv()
        for cp in _g2_forward(land_refs, fwd_send, fwd_recv, arriving=False):
            cp.start()

    lands = list(handle["lands"])
    sem = pltpu.SemaphoreType.DMA
    outs = pl.pallas_call(
        body, name=name,
        out_shape=(sem((3 * n,)), sem((3 * n,)), *[pltpu.HBM(a.shape, a.dtype) for a in lands]),
        in_specs=[_HBM] * n + [_SEM, pl.BlockSpec(memory_space=pl.ANY)],
        out_specs=(_SEM, _SEM, *[_HBM] * n),
        input_output_aliases={i: 2 + i for i in range(n)},
        compiler_params=pltpu.CompilerParams(has_side_effects=_EFFECT),
    )(*lands, handle["sems"][1], after)
    return dict(handle, fwd=outs[:2], lands=outs[2:])


def gather2_wait(handle, after, *, name):
    n = handle["n"]

    def body(*refs):
        in_refs, land_refs = refs[:n], refs[n:2 * n]
        ici_send, d2d_send, d2d_recv, local_sems, fwd_send, fwd_recv = refs[2 * n:2 * n + 6]
        for cp in _g2_first(in_refs, land_refs, d2d_send, d2d_recv, masks=(1,), arriving=True):
            cp.wait_recv()
        for cp in _g2_forward(land_refs, fwd_send, fwd_recv, arriving=True):
            cp.wait_recv()
        for cp in (_g2_first(in_refs, land_refs, ici_send, fwd_recv, masks=CHIP_MASKS, arriving=False)
                   + _g2_first(in_refs, land_refs, d2d_send, d2d_recv, masks=(1,), arriving=False)
                   + _g2_forward(land_refs, fwd_send, fwd_recv, arriving=False)):
            cp.wait_send()
        for cp in _local_copies(["gather"] * n, in_refs, land_refs, local_sems):
            cp.wait()

    srcs, lands = list(handle["srcs"]), list(handle["lands"])
    s = handle["sems"]
    outs = pl.pallas_call(
        body, name=name,
        out_shape=tuple(pltpu.HBM(a.shape, a.dtype) for a in srcs + lands),
        in_specs=[_HBM] * (2 * n) + [_SEM] * 6 + [pl.BlockSpec(memory_space=pl.ANY)],
        out_specs=tuple([_HBM] * (2 * n)),
        input_output_aliases={i: i for i in range(2 * n)},
        compiler_params=pltpu.CompilerParams(has_side_effects=_EFFECT),
    )(*srcs, *lands, s[0], s[2], s[3], s[4], *handle["fwd"], after)
    return list(outs[n:])


def adamw(entries, *, name):
    c1 = 1.0 - ADAM_B1 ** ADAM_STEP
    c2 = 1.0 - ADAM_B2 ** ADAM_STEP
    tiles = [_tile(w.shape[-2], (64, 32, 16, 8)) for _, w, _, _, _, _ in entries]
    steps = [w.shape[-2] // tr for (_, w, _, _, _, _), tr in zip(entries, tiles)]
    n = len(entries)

    def body(*refs):
        i = pl.program_id(0)
        for e in range(n):
            s_ref, w_ref, m_ref, v_ref = refs[4 * e:4 * e + 4]
            g_ref, d_ref, m2_ref, v2_ref = refs[len(refs) - 4 * n + 4 * e:len(refs) - 4 * n + 4 * e + 4]

            @pl.when(i < steps[e])
            def _():
                g = s_ref[0].astype(f32)
                for d in range(1, N_DEV):
                    g = g + s_ref[d].astype(f32)
                m2 = ADAM_B1 * m_ref[...] + (1.0 - ADAM_B1) * g
                v2 = ADAM_B2 * v_ref[...] + (1.0 - ADAM_B2) * (g * g)
                g_ref[...] = g
                m2_ref[...] = m2
                v2_ref[...] = v2
                d_ref[...] = -ADAM_LR * ((m2 / c1) / (jnp.sqrt(v2 / c2) + ADAM_EPS) + ADAM_WD * w_ref[...])

    in_specs, out_specs, out_shape, args, extras, aliases = [], [], [], [], [], {}
    for e, ((slots, w, m, v, layer, into), tr, ns) in enumerate(zip(entries, tiles, steps)):
        C = w.shape[-1]
        row = lambda i, ns=ns: jnp.minimum(i, ns - 1)
        if layer is None:
            blk = pl.BlockSpec((tr, C), lambda i, row=row: (row(i), 0))
        else:
            blk = pl.BlockSpec((None, tr, C), lambda i, row=row, layer=layer: (layer, row(i), 0))
        in_specs += [pl.BlockSpec((N_DEV, tr, C), lambda i, row=row: (0, row(i), 0)), blk, blk, blk]
        args += [slots, w, m, v]
        out_specs += [blk] * 4
        out_shape += [_sds(w.shape, f32)] * 4
        if into is not None:
            for t, a in enumerate(into):
                aliases[4 * n + len(extras)] = 4 * e + t
                extras.append(a)
    outs = pl.pallas_call(
        body, name=name, grid=(max(steps),),
        in_specs=in_specs + [pl.BlockSpec(memory_space=pl.ANY)] * len(extras),
        out_specs=out_specs, out_shape=out_shape, input_output_aliases=aliases,
        compiler_params=_params(("arbitrary",)),
    )(*args, *extras)
    return [outs[4 * e:4 * e + 4] for e in range(n)]


WEIGHTS = ("norm_gains", "mem_norm", "w_in", "w_mem_kv", "w_out", "w_pool", "pool_scale", "kv_norm", "w_kv",
           "w_gate_up", "w_down")
LAYER_MATS = ("w_in", "w_mem_kv", "w_out", "w_gate_up", "w_down")
POOL_SHARD = MAIN_W // N_DEV
KV_SHARD = 2 * MAIN_W // N_DEV
LOOKAHEAD = 2
TWO_LEVEL_LAYERS = (0, 1)


def _pack_small(gains, pscale):
    lead = gains.shape[:-3]
    g = gains.reshape(lead + (16, 128))
    p = jnp.zeros(lead + (8, 128), f32).at[..., :2, :POOL_SHARD].set(pscale)
    return jnp.concatenate([g, p], axis=-2)


def _unpack_small(a):
    return a[:16].reshape(4, 4, 128), a[16:18, :POOL_SHARD]


def _pack_repl(mem_norm, kv_norm):
    return jnp.concatenate([mem_norm, kv_norm.reshape(1, D_MODEL), jnp.zeros((3, D_MODEL), f32)], axis=0)


def _unpack_repl(a):
    return a[:4], a[4]


def kernel(x, mem, positions, norm_gains, mem_norm, w_in, w_mem_kv, w_out, w_pool, pool_scale, kv_norm, w_kv, w_gate_up, w_down, loss_target, m_norm_gains, m_mem_norm, m_w_in, m_w_mem_kv, m_w_out, m_w_pool, m_pool_scale, m_kv_norm, m_w_kv, m_w_gate_up, m_w_down, v_norm_gains, v_mem_norm, v_w_in, v_w_mem_kv, v_w_out, v_w_pool, v_pool_scale, v_kv_norm, v_w_kv, v_w_gate_up, v_w_down):
    w = dict(norm_gains=norm_gains, mem_norm=mem_norm, w_in=w_in, w_mem_kv=w_mem_kv, w_out=w_out, w_pool=w_pool,
             pool_scale=pool_scale, kv_norm=kv_norm, w_kv=w_kv, w_gate_up=w_gate_up, w_down=w_down)
    m = dict(norm_gains=m_norm_gains, mem_norm=m_mem_norm, w_in=m_w_in, w_mem_kv=m_w_mem_kv, w_out=m_w_out,
             w_pool=m_w_pool, pool_scale=m_pool_scale, kv_norm=m_kv_norm, w_kv=m_w_kv, w_gate_up=m_w_gate_up,
             w_down=m_w_down)
    v = dict(norm_gains=v_norm_gains, mem_norm=v_mem_norm, w_in=v_w_in, w_mem_kv=v_w_mem_kv, w_out=v_w_out,
             w_pool=v_w_pool, pool_scale=v_pool_scale, kv_norm=v_kv_norm, w_kv=v_w_kv, w_gate_up=v_w_gate_up,
             w_down=v_w_down)

    def transposed_view(d):
        d = dict(d)
        d["w_gate_up"] = jnp.swapaxes(d["w_gate_up"], 1, 2)
        d["w_kv"] = jnp.swapaxes(d["w_kv"], 0, 1)
        return d

    wv, mv, vv = transposed_view(w), transposed_view(m), transposed_view(v)

    small = _pack_small(norm_gains, pool_scale)
    (gsmall,) = exchange([(small, "gather")], name="gather_small")
    P = {"norm_gains": jnp.moveaxis(gsmall[:, :16].reshape(N_DEV, 4, 4, 128), 0, 2).reshape(4, 4, D_MODEL),
         "pool_scale": jnp.moveaxis(gsmall[:, 16:18, :POOL_SHARD], 0, 1).reshape(2, MAIN_W),
         "mem_norm": mem_norm, "kv_norm": kv_norm, "w_pool": w_pool}

    PARTS = {"mix": ("w_in", "w_mem_kv", "w_out"), "ffn": ("w_gate_up", "w_down"), "gu": ("w_gate_up",),
             "down": ("w_down",), "all": ("w_in", "w_mem_kv", "w_out", "w_gate_up", "w_down")}

    def parts_of(l):
        return (("mix", "gu", "down"), ("mix", "ffn"))[l] if l < 2 else ("all",)

    def part_items(l, part):
        items = [(wv[k][l].astype(bf16), "gather") for k in PARTS[part]]
        if part == "ffn" and l == N_A_LAYERS - 1:
            items.append((wv["w_kv"].astype(bf16), "gather"))
        return items

    handles = {}

    def start_layer(l, after):
        for part in parts_of(l):
            if l in TWO_LEVEL_LAYERS:
                handles[l, part] = gather2_start([a for a, _ in part_items(l, part)], after,
                                                 name=f"gather_start_{part}_l{l}")
            else:
                handles[l, part] = exchange_start(part_items(l, part), after, name=f"gather_start_{part}_l{l}")
            after = handles[l, part]["token"]
        return after

    token = gsmall
    for l in range(LOOKAHEAD):
        token = start_layer(l, token)
    landed = {}

    def layer_weights(l, part, after):
        if part not in parts_of(l):
            if part == "down" or (part == "gu" and "all" in parts_of(l)):
                return {}, None
            part = "all" if "all" in parts_of(l) else "ffn"
        if l == 0 and part == "mix":
            after = token
        if l in TWO_LEVEL_LAYERS:
            passed = gather2_forward(handles[l, part], after, name=f"gather_forward_{part}_l{l}")
            got = gather2_wait(passed, after, name=f"gather_wait_{part}_l{l}")
        else:
            got = exchange_wait(handles[l, part], after, name=f"gather_wait_{part}_l{l}")
        landed[l, part] = got
        started = None
        if part in ("mix", "all") and l + LOOKAHEAD < DEPTH:
            started = start_layer(l + LOOKAHEAD, got[0])
        W = {k: g.reshape(-1, g.shape[-1]) for k, g in zip(PARTS[part], got)}
        return W, started

    def kv_weight(after):
        g = landed[N_A_LAYERS - 1, "ffn"][len(PARTS["ffn"])]
        return g.reshape(2 * MAIN_W, D_MODEL)

    ghandles = {}

    pending = {}

    def gparts_of(l):
        return ("ffn", "mix") if l == 0 else ("all",)

    def emit_grads(l, part, gw):
        if part not in gparts_of(l):
            pending.setdefault(l, {}).update(gw)
            if part == "ffn":
                return None
            gw, part = pending[l], "all"
        items = [(gw[k].reshape((N_DEV, -1) + gw[k].shape[-1:]), "scatter") for k in PARTS[part]]
        if part != "ffn" and l == N_A_LAYERS:
            items.append((gw["w_kv"].reshape(N_DEV, KV_SHARD, D_MODEL), "scatter"))
        if part != "ffn" and l < N_A_LAYERS:
            items.append((gw["w_pool"], "gather"))
        ghandles[l, part] = exchange_start(items, gsmall, name=f"scatter_start_{part}_l{l}")
        return ghandles[l, part]["token"]

    sq, grad_x, GS, emitted = local_step(x, mem, positions, loss_target, P, layer_weights, kv_weight, emit_grads)
    loss = lax.psum(0.5 * sq / D_MODEL, ("x", "y", "c"))

    def pool3(a):
        return a.reshape(N_A_LAYERS, MAIN_W, POOL_GROUP)

    out = {}
    after = [emitted]

    def finish_layer(l, after):
        for part in gparts_of(l):
            got = exchange_wait(ghandles[l, part], after, name=f"scatter_wait_{part}_l{l}")
            names = list(PARTS[part])
            entries = [(slots, wv[k], mv[k], vv[k], l, out.get(k)) for k, slots in zip(names, got)]
            if part != "ffn" and l == N_A_LAYERS:
                names.append("w_kv")
                entries.append((got[-1], wv["w_kv"], mv["w_kv"], vv["w_kv"], None, None))
            if part != "ffn" and l < N_A_LAYERS:
                names.append("w_pool")
                entries.append((got[-1], pool3(w_pool), pool3(m_w_pool), pool3(v_w_pool), l, out.get("w_pool")))
            out.update(zip(names, adamw(entries, name=f"adamw_{part}_l{l}")))
            after = [out[k][0] for k in names]
        return after

    for l in reversed(range(1, DEPTH)):
        after = finish_layer(l, after)

    gs = _pack_small(jnp.moveaxis(GS["norm_gains"].reshape(4, 4, N_DEV, 128), 2, 0),
                     jnp.moveaxis(GS["pool_scale"].reshape(2, N_DEV, POOL_SHARD), 1, 0))
    parts_small, parts_repl = exchange(
        [(gs, "scatter"), (_pack_repl(GS["mem_norm"], GS["kv_norm"]), "gather")],
        name="exchange_small_grads", after=after)
    finish_layer(0, [parts_small])
    out["w_gate_up"] = [jnp.swapaxes(r, 1, 2) for r in out["w_gate_up"]]
    out["w_kv"] = [jnp.swapaxes(r, 0, 1) for r in out["w_kv"]]
    out["w_pool"] = [r.reshape(w_pool.shape) for r in out["w_pool"]]

    res_small, res_repl = adamw(
        [(parts_small, small, _pack_small(m_norm_gains, m_pool_scale), _pack_small(v_norm_gains, v_pool_scale),
          None, None),
         (parts_repl, _pack_repl(mem_norm, kv_norm), _pack_repl(m_mem_norm, m_kv_norm),
          _pack_repl(v_mem_norm, v_kv_norm), None, None)], name="adamw_small")
    out["norm_gains"], out["pool_scale"] = zip(*[_unpack_small(r) for r in res_small])
    out["mem_norm"], out["kv_norm"] = zip(*[_unpack_repl(r) for r in res_repl])

    return (loss, grad_x, *[out[k][0] for k in WEIGHTS], *[out[k][1] for k in WEIGHTS],
            *[out[k][2] for k in WEIGHTS], *[out[k][3] for k in WEIGHTS])
```

```python
import numpy as np
import jax
import jax.numpy as jnp
from jax import lax
from jax.experimental import pallas as pl
from jax.experimental.pallas import tpu as pltpu

f32 = jnp.float32
bf16 = jnp.bfloat16

D_MODEL = 1024
SEQ = 2048
DEPTH = 4
N_MEM = 256
HEAD_DIM = 64
N_MEM_HEADS = 4
MEM_W = 256
MAIN_W = 768
POOL_WINDOWS = (2, 4, 8, 16)
POOL_GROUP = 192
POOL_HALO = 16
DIL_PATTERNS = ((128, 1), (512, 4), (2048, 16))
N_GROUPS = 3
GROUP_W = 256
BAND = 128
N_A_LAYERS = 2
D_FF = 2816
ROPE_THETA = 10000.0
EPS = 1e-6
NEG = -1e30
SCALE = HEAD_DIM ** -0.5
N_DEV = 8

ADAM_LR = 0.001
ADAM_B1 = 0.9
ADAM_B2 = 0.999
ADAM_EPS = 1e-08
ADAM_WD = 0.01
ADAM_STEP = 10

VMEM_LIMIT_BYTES = 56 * 1024 * 1024
MESH = pl.DeviceIdType.MESH

NN = (((1,), (0,)), ((), ()))
NT = (((1,), (1,)), ((), ()))
TN = (((0,), (0,)), ((), ()))


def _params(sem=None):
    return pltpu.CompilerParams(dimension_semantics=sem, vmem_limit_bytes=VMEM_LIMIT_BYTES)


def _tile(n, cands):
    for c in cands:
        if n % c == 0:
            return c
    return n


def _sds(shape, dtype):
    return jax.ShapeDtypeStruct(tuple(shape), dtype)


def _rms_r(v):
    return lax.rsqrt(jnp.mean(v * v, axis=-1, keepdims=True) + EPS)


def rms_matmul(x, gain, w, *, name, out_dtype, transposed=False, after=None):
    M, K = x.shape
    N = w.shape[0] if transposed else w.shape[1]
    tm = min(512, M)
    order = [] if after is None else [after]

    def body(x_ref, g_ref, w_ref, *refs):
        z_ref, h_ref = refs[len(order):]
        xv = x_ref[...]
        h = (xv * _rms_r(xv) * g_ref[...]).astype(bf16)
        h_ref[...] = h
        z_ref[...] = lax.dot_general(h, w_ref[...], NT if transposed else NN,
                                     preferred_element_type=f32).astype(z_ref.dtype)

    return pl.pallas_call(
        body, name=name, grid=(M // tm,),
        in_specs=[pl.BlockSpec((tm, K), lambda i: (i, 0)),
                  pl.BlockSpec((1, K), lambda i: (0, 0)),
                  pl.BlockSpec(w.shape, lambda i: (0, 0))] + [pl.BlockSpec(memory_space=pl.ANY)] * len(order),
        out_specs=[pl.BlockSpec((tm, N), lambda i: (i, 0)), pl.BlockSpec((tm, K), lambda i: (i, 0))],
        out_shape=[_sds((M, N), out_dtype), _sds((M, K), bf16)],
        compiler_params=_params(("parallel",)),
    )(x, gain, w, *order)


def matmul_rms_res(a, w, gain, res, *, name):
    M, K = a.shape
    N = w.shape[1]
    tm = min(512, M)

    def body(a_ref, w_ref, g_ref, r_ref, y_ref, x_ref):
        y = jnp.dot(a_ref[...], w_ref[...], preferred_element_type=f32)
        y_ref[...] = y.astype(bf16)
        x_ref[...] = r_ref[...] + y * _rms_r(y) * g_ref[...]

    row = pl.BlockSpec((tm, N), lambda i: (i, 0))
    return pl.pallas_call(
        body, name=name, grid=(M // tm,),
        in_specs=[pl.BlockSpec((tm, K), lambda i: (i, 0)),
                  pl.BlockSpec((K, N), lambda i: (0, 0)),
                  pl.BlockSpec((1, N), lambda i: (0, 0)),
                  row],
        out_specs=[row, row],
        out_shape=[_sds((M, N), bf16), _sds((M, N), f32)],
        compiler_params=_params(("parallel",)),
    )(a, w, gain, res)


def matmul(a, b, dims, *, name, out_dtype):
    if dims is TN:
        K, M = a.shape
        tm = _tile(M, (512, 256, 128))
        a_spec = pl.BlockSpec((K, tm), lambda i: (0, i))
    else:
        M, K = a.shape
        tm = _tile(M, (1024, 512, 256, 128))
        a_spec = pl.BlockSpec((tm, K), lambda i: (i, 0))
    N = b.shape[0] if dims is NT else b.shape[1]

    def body(a_ref, b_ref, o_ref):
        o_ref[...] = lax.dot_general(a_ref[...].astype(bf16), b_ref[...].astype(bf16), dims,
                                     preferred_element_type=f32).astype(o_ref.dtype)

    return pl.pallas_call(
        body, name=name, grid=(M // tm,),
        in_specs=[a_spec, pl.BlockSpec(b.shape, lambda i: (0, 0))],
        out_specs=pl.BlockSpec((tm, N), lambda i: (i, 0)),
        out_shape=_sds((M, N), out_dtype),
        compiler_params=_params(("parallel",)),
    )(a, b)


def rms_gate_up(x, gain, wt, *, name):
    M, K = x.shape
    tm = min(2048, M)
    tn = _tile(D_FF, (256, 128))
    nj = D_FF // tn

    def body(x_ref, gn_ref, wg_ref, wu_ref, g_ref, u_ref, a_ref, h_ref):
        @pl.when(pl.program_id(1) == 0)
        def _():
            xv = x_ref[...]
            h_ref[...] = (xv * _rms_r(xv) * gn_ref[...]).astype(bf16)

        h = h_ref[...]
        g = lax.dot_general(h, wg_ref[...], NT, preferred_element_type=f32).astype(bf16)
        u = lax.dot_general(h, wu_ref[...], NT, preferred_element_type=f32).astype(bf16)
        g_ref[...] = g
        u_ref[...] = u
        a_ref[...] = g * (1.0 / (1.0 + jnp.exp(-g))) * u

    col = pl.BlockSpec((tm, tn), lambda i, j: (i, j))
    return pl.pallas_call(
        body, name=name, grid=(M // tm, nj),
        in_specs=[pl.BlockSpec((tm, K), lambda i, j: (i, 0)),
                  pl.BlockSpec((1, K), lambda i, j: (0, 0)),
                  pl.BlockSpec((tn, K), lambda i, j: (j, 0)),
                  pl.BlockSpec((tn, K), lambda i, j: (j + nj, 0))],
        out_specs=[col, col, col, pl.BlockSpec((tm, K), lambda i, j: (i, 0))],
        out_shape=[_sds((M, D_FF), bf16)] * 3 + [_sds((M, K), bf16)],
        compiler_params=_params(("parallel", "arbitrary")),
    )(x, gain, wt, wt)


def _rms_bwd_math(yv, gain, dn):
    r = _rms_r(yv)
    q = dn * gain
    dy = r * q - yv * (r * r * r) * jnp.mean(q * yv, axis=-1, keepdims=True)
    return dy, jnp.sum(dn * yv * r, axis=0, keepdims=True)


def _accumulate(ref, val):
    @pl.when(pl.program_id(0) == 0)
    def _():
        ref[...] = jnp.zeros_like(ref)

    ref[...] += val


def down_bwd(y, gain, dn, w_down, g, u, *, name, after=None):
    M, K = y.shape
    tm = min(512, M)
    order = [] if after is None else [after]

    def body(y_ref, gn_ref, dn_ref, w_ref, g_ref, u_ref, *refs):
        dy_ref, o_ref, dg_ref = refs[len(order):]
        dy, dgain = _rms_bwd_math(y_ref[...].astype(f32), gn_ref[...], dn_ref[...])
        dy = dy.astype(bf16)
        dy_ref[...] = dy
        _accumulate(dg_ref, dgain)
        da = lax.dot_general(dy, w_ref[...], NT, preferred_element_type=f32).astype(bf16)
        g = g_ref[...]
        s = 1.0 / (1.0 + jnp.exp(-g))
        o_ref[:, :D_FF] = da * u_ref[...] * s * (1.0 + g * (1.0 - s))
        o_ref[:, D_FF:] = da * g * s

    row = pl.BlockSpec((tm, K), lambda i: (i, 0))
    vec = pl.BlockSpec((1, K), lambda i: (0, 0))
    wide = pl.BlockSpec((tm, D_FF), lambda i: (i, 0))
    return pl.pallas_call(
        body, name=name, grid=(M // tm,),
        in_specs=[row, vec, row, pl.BlockSpec((D_FF, K), lambda i: (0, 0)), wide, wide]
        + [pl.BlockSpec(memory_space=pl.ANY)] * len(order),
        out_specs=[row, pl.BlockSpec((tm, 2 * D_FF), lambda i: (i, 0)), vec],
        out_shape=[_sds((M, K), bf16), _sds((M, 2 * D_FF), bf16), _sds((1, K), f32)],
        compiler_params=_params(("arbitrary",)),
    )(y, gain, dn, w_down, g, u, *order)


def rms_bwd_matmul(y, gain, dn, w, dims, *, name, after=None):
    M, K = y.shape
    N = w.shape[0] if dims is NT else w.shape[1]
    tm = min(1024, M)
    order = [] if after is None else [after]

    def body(y_ref, gn_ref, dn_ref, w_ref, *refs):
        dy_ref, o_ref, dg_ref = refs[len(order):]
        dy, dgain = _rms_bwd_math(y_ref[...].astype(f32), gn_ref[...], dn_ref[...].astype(f32))
        dy = dy.astype(bf16)
        dy_ref[...] = dy
        _accumulate(dg_ref, dgain)
        o_ref[...] = lax.dot_general(dy, w_ref[...], dims, preferred_element_type=f32).astype(bf16)

    row = pl.BlockSpec((tm, K), lambda i: (i, 0))
    vec = pl.BlockSpec((1, K), lambda i: (0, 0))
    return pl.pallas_call(
        body, name=name, grid=(M // tm,),
        in_specs=[row, vec, row, pl.BlockSpec(w.shape, lambda i: (0, 0))]
        + [pl.BlockSpec(memory_space=pl.ANY)] * len(order),
        out_specs=[row, pl.BlockSpec((tm, N), lambda i: (i, 0)), vec],
        out_shape=[_sds((M, K), bf16), _sds((M, N), bf16), _sds((1, K), f32)],
        compiler_params=_params(("arbitrary",)),
    )(y, gain, dn, w, *order)


def matmul_rms_bwd(a, b, dims, y, gain, res, *, name, after=None):
    M, K = a.shape
    N = y.shape[1]
    tm = min(512, M)
    order = [] if after is None else [after]

    def body(a_ref, b_ref, y_ref, gn_ref, r_ref, *refs):
        dx_ref, dg_ref = refs[len(order):]
        dn = lax.dot_general(a_ref[...].astype(bf16), b_ref[...], dims, preferred_element_type=f32)
        dy, dgain = _rms_bwd_math(y_ref[...], gn_ref[...], dn)
        dx_ref[...] = dy + r_ref[...]
        _accumulate(dg_ref, dgain)

    row = pl.BlockSpec((tm, N), lambda i: (i, 0))
    vec = pl.BlockSpec((1, N), lambda i: (0, 0))
    return pl.pallas_call(
        body, name=name, grid=(M // tm,),
        in_specs=[pl.BlockSpec((tm, K), lambda i: (i, 0)), pl.BlockSpec(b.shape, lambda i: (0, 0)), row, vec, row]
        + [pl.BlockSpec(memory_space=pl.ANY)] * len(order),
        out_specs=[row, vec],
        out_shape=[_sds((M, N), f32), _sds((1, N), f32)],
        compiler_params=_params(("arbitrary",)),
    )(a, b, y, gain, res, *order)


def rms_bwd(y, gain, dn, res, *, name, out_dtype, after=None):
    M, N = y.shape
    tm = min(512, M)
    has_res = res is not None
    order = [] if after is None else [after]

    def body(*refs):
        y_ref, g_ref, dn_ref = refs[:3]
        r_ref = refs[3] if has_res else None
        dy_ref, dg_ref = refs[-2:]
        dy, dgain = _rms_bwd_math(y_ref[...].astype(f32), g_ref[...], dn_ref[...].astype(f32))
        if has_res:
            dy = dy + r_ref[...]
        dy_ref[...] = dy.astype(dy_ref.dtype)
        _accumulate(dg_ref, dgain)

    row = pl.BlockSpec((tm, N), lambda i: (i, 0))
    vec = pl.BlockSpec((1, N), lambda i: (0, 0))
    args = [y, gain, dn] + ([res] if has_res else []) + order
    return pl.pallas_call(
        body, name=name, grid=(M // tm,),
        in_specs=[row, vec, row] + ([row] if has_res else []) + [pl.BlockSpec(memory_space=pl.ANY)] * len(order),
        out_specs=[row, vec],
        out_shape=[_sds((M, N), out_dtype), _sds((1, N), f32)],
        compiler_params=_params(("arbitrary",)),
    )(*args)


def loss_head(x, target, *, name):
    M, N = x.shape
    tm = min(512, M)

    def body(x_ref, t_ref, dx_ref, l_ref):
        e = x_ref[...] - t_ref[...]
        dx_ref[...] = e * (1.0 / N)

        @pl.when(pl.program_id(0) == 0)
        def _():
            l_ref[...] = jnp.zeros_like(l_ref)

        l_ref[...] += jnp.sum(jnp.sum(e * e, axis=0, keepdims=True), axis=1, keepdims=True)

    row = pl.BlockSpec((tm, N), lambda i: (i, 0))
    return pl.pallas_call(
        body, name=name, grid=(M // tm,),
        in_specs=[row, row],
        out_specs=[row, pl.BlockSpec((8, 128), lambda i: (0, 0))],
        out_shape=[_sds((M, N), f32), _sds((8, 128), f32)],
        compiler_params=_params(("arbitrary",)),
    )(x, target)


def _pool_select(a1, a2, a3, a4):
    col = lax.broadcasted_iota(jnp.int32, (1, MAIN_W), 1) // POOL_GROUP
    return jnp.where(col == 0, a1, jnp.where(col == 1, a2, jnp.where(col == 2, a3, a4)))


def _pool_count(t):
    col = lax.broadcasted_iota(jnp.int32, (1, MAIN_W), 1) // POOL_GROUP
    win = jnp.where(col == 0, 2, jnp.where(col == 1, 4, jnp.where(col == 2, 8, 16)))
    return jnp.minimum(t + 1, win).astype(f32)


def pool_fwd(z, wbd, scale, *, name):
    M = z.shape[0]
    tm = 256
    nper = SEQ // tm
    hb = tm // POOL_HALO

    def body(zc_ref, zh_ref, w_ref, s_ref, p_ref, y_ref):
        i = pl.program_id(0)
        seq_blk = i % nper
        halo = jnp.where(seq_blk == 0, 0.0, zh_ref[...].astype(f32))
        u = zc_ref[...].astype(f32)
        ext = jnp.concatenate([halo, u], axis=0)
        a1 = ext + pltpu.roll(ext, 1, 0)
        a2 = a1 + pltpu.roll(a1, 2, 0)
        a3 = a2 + pltpu.roll(a2, 4, 0)
        a4 = a3 + pltpu.roll(a3, 8, 0)
        sums = _pool_select(a1, a2, a3, a4)[POOL_HALO:]
        t = seq_blk * tm + lax.broadcasted_iota(jnp.int32, (tm, 1), 0)
        p = (sums / _pool_count(t) - u).astype(bf16)
        p_ref[...] = p
        y_ref[...] = (jnp.dot(p, w_ref[...], preferred_element_type=f32) * s_ref[...]).astype(bf16)

    return pl.pallas_call(
        body, name=name, grid=(M // tm,),
        in_specs=[pl.BlockSpec((tm, MAIN_W), lambda i: (i, 0)),
                  pl.BlockSpec((POOL_HALO, MAIN_W), lambda i: (jnp.maximum(i * hb - 1, 0), 0)),
                  pl.BlockSpec((MAIN_W, MAIN_W), lambda i: (0, 0)),
                  pl.BlockSpec((1, MAIN_W), lambda i: (0, 0))],
        out_specs=[pl.BlockSpec((tm, MAIN_W), lambda i: (i, 0)),
                   pl.BlockSpec((tm, MAIN_W), lambda i: (i, 0))],
        out_shape=[_sds((M, MAIN_W), bf16), _sds((M, D_MODEL), bf16)],
        compiler_params=_params(("parallel",)),
    )(z, z, wbd, scale)


def pool_bwd(dyc, p, wbd, scale, *, name):
    M = p.shape[0]
    tm = 256
    nper = SEQ // tm
    hb = tm // POOL_HALO
    last_hb = M // POOL_HALO - 1

    def body(dy_ref, dyh_ref, p_ref, w_ref, s_ref, dz_ref, dw_ref, ds_ref):
        i = pl.program_id(0)
        seq_blk = i % nper
        dy = dy_ref[...].astype(f32)
        pv = p_ref[...]
        w = w_ref[...]
        sc = s_ref[...]

        @pl.when(i == 0)
        def _():
            dw_ref[...] = jnp.zeros_like(dw_ref)
            ds_ref[...] = jnp.zeros_like(ds_ref)

        v = jnp.dot(pv, w, preferred_element_type=f32)
        ds_ref[...] += jnp.sum(dy * v, axis=0, keepdims=True)
        dv = (dy * sc).astype(bf16)
        dw_ref[...] += lax.dot_general(pv, dv, TN, preferred_element_type=f32)
        dp = lax.dot_general(dv, w, NT, preferred_element_type=f32)
        dvh = jnp.where(seq_blk == nper - 1, 0.0, dyh_ref[...].astype(f32) * sc).astype(bf16)
        dph = lax.dot_general(dvh, w, NT, preferred_element_type=f32)
        ext = jnp.concatenate([dp, dph], axis=0)
        n = tm + POOL_HALO
        t = seq_blk * tm + lax.broadcasted_iota(jnp.int32, (n, 1), 0)
        e = ext / _pool_count(t)
        b1 = e + pltpu.roll(e, n - 1, 0)
        b2 = b1 + pltpu.roll(b1, n - 2, 0)
        b3 = b2 + pltpu.roll(b2, n - 4, 0)
        b4 = b3 + pltpu.roll(b3, n - 8, 0)
        dz_ref[...] = (_pool_select(b1, b2, b3, b4)[:tm] - dp).astype(dz_ref.dtype)

    return pl.pallas_call(
        body, name=name, grid=(M // tm,),
        in_specs=[pl.BlockSpec((tm, MAIN_W), lambda i: (i, 0)),
                  pl.BlockSpec((POOL_HALO, MAIN_W), lambda i: (jnp.minimum((i + 1) * hb, last_hb), 0)),
                  pl.BlockSpec((tm, MAIN_W), lambda i: (i, 0)),
                  pl.BlockSpec((MAIN_W, MAIN_W), lambda i: (0, 0)),
                  pl.BlockSpec((1, MAIN_W), lambda i: (0, 0))],
        out_specs=[pl.BlockSpec((tm, MAIN_W), lambda i: (i, 0)),
                   pl.BlockSpec((MAIN_W, MAIN_W), lambda i: (0, 0)),
                   pl.BlockSpec((1, MAIN_W), lambda i: (0, 0))],
        out_shape=[_sds((M, D_MODEL), bf16), _sds((MAIN_W, MAIN_W), f32), _sds((1, MAIN_W), f32)],
        compiler_params=_params(("arbitrary",)),
    )(dyc, dyc, p, wbd, scale)


def _mem_heads(q, kv):
    first = _first_head()
    for pr in range(N_MEM_HEADS // 2):
        cols = slice(pr * PAIR_W, (pr + 1) * PAIR_W)
        qp = q[:, cols] * SCALE
        kp = kv[:, cols]
        vp = kv[:, MEM_W + pr * PAIR_W: MEM_W + (pr + 1) * PAIR_W]
        for hh in range(2):
            lm = first if hh == 0 else ~first
            qm = jnp.where(lm, qp, 0.0).astype(bf16)
            s = lax.dot_general(qm, kp, NT, preferred_element_type=f32)
            e = jnp.exp(s - jnp.max(s, axis=-1, keepdims=True))
            yield lm, qm, kp, vp, e, jnp.sum(e, axis=-1, keepdims=True)


def memattn_fwd(z, kvm, ycat, *, name, n_seq):
    M = z.shape[0]
    tq = 512
    nq = SEQ // tq

    def body(q_ref, kv_ref, _, o_ref):
        first = _first_head()
        outs = []
        for lm, _, _, vp, e, l in _mem_heads(q_ref[...], kv_ref[...]):
            outs.append(jnp.dot(e.astype(bf16), vp, preferred_element_type=f32) * (1.0 / l))
        pairs = [jnp.where(first, outs[2 * pr], outs[2 * pr + 1]) for pr in range(N_MEM_HEADS // 2)]
        o_ref[...] = jnp.concatenate(pairs, axis=1).astype(bf16)

    return pl.pallas_call(
        body, name=name, grid=(n_seq, nq),
        in_specs=[pl.BlockSpec((tq, MEM_W), lambda b, i: (b * nq + i, 3)),
                  pl.BlockSpec((N_MEM, 2 * MEM_W), lambda b, i: (b, 0)),
                  pl.BlockSpec(memory_space=pl.ANY)],
        out_specs=pl.BlockSpec((tq, MEM_W), lambda b, i: (b * nq + i, 3)),
        out_shape=_sds((M, D_MODEL), bf16),
        input_output_aliases={2: 0},
        compiler_params=_params(("parallel", "parallel")),
    )(z, kvm, ycat)


def memattn_bwd(z, kvm, dyc, dz, *, name, n_seq):
    M = z.shape[0]
    tq = 512
    nq = SEQ // tq

    def body(q_ref, kv_ref, dy_ref, _, dq_ref, dkv_ref):
        first = _first_head()
        dy = dy_ref[...].astype(f32)
        dqs, dks, dvs = [], [], []
        for h, (lm, qm, kp, vp, e, l) in enumerate(_mem_heads(q_ref[...], kv_ref[...])):
            pr = h // 2
            p = e * (1.0 / l)
            dym = jnp.where(lm, dy[:, pr * PAIR_W:(pr + 1) * PAIR_W], 0.0).astype(bf16)
            dp = lax.dot_general(dym, vp, NT, preferred_element_type=f32)
            ds = (p * (dp - jnp.sum(dp * p, axis=-1, keepdims=True))).astype(bf16)
            dqs.append(jnp.dot(ds, kp, preferred_element_type=f32) * SCALE)
            dk = lax.dot_general(ds, qm, TN, preferred_element_type=f32)
            dv = lax.dot_general(p.astype(bf16), dym, TN, preferred_element_type=f32)
            if h % 2 == 0:
                dks.append(dk)
                dvs.append(dv)
            else:
                dks[pr] = dks[pr] + dk
                dvs[pr] = dvs[pr] + dv
        pairs = [jnp.where(first, dqs[2 * pr], dqs[2 * pr + 1]) for pr in range(N_MEM_HEADS // 2)]
        dq_ref[...] = jnp.concatenate(pairs, axis=1).astype(bf16)

        @pl.when(pl.program_id(1) == 0)
        def _():
            dkv_ref[...] = jnp.zeros_like(dkv_ref)

        dkv_ref[...] += jnp.concatenate(dks + dvs, axis=1)

    return pl.pallas_call(
        body, name=name, grid=(n_seq, nq),
        in_specs=[pl.BlockSpec((tq, MEM_W), lambda b, i: (b * nq + i, 3)),
                  pl.BlockSpec((N_MEM, 2 * MEM_W), lambda b, i: (b, 0)),
                  pl.BlockSpec((tq, MEM_W), lambda b, i: (b * nq + i, 3)),
                  pl.BlockSpec(memory_space=pl.ANY)],
        out_specs=[pl.BlockSpec((tq, MEM_W), lambda b, i: (b * nq + i, 3)),
                   pl.BlockSpec((N_MEM, 2 * MEM_W), lambda b, i: (b, 0))],
        out_shape=[_sds((M, D_MODEL), bf16), _sds((n_seq * N_MEM, 2 * MEM_W), f32)],
        input_output_aliases={3: 0},
        compiler_params=_params(("parallel", "arbitrary")),
    )(z, kvm, dyc, dz)


def rope_tables(pos, *, name):
    M = pos.shape[0]
    tm = min(1024, M)
    half = HEAD_DIM // 2
    inv = ROPE_THETA ** (-np.arange(half, dtype=np.float64) / half)
    inv128 = jnp.asarray(np.tile(inv, 4)[None, :], f32)
    sign128 = jnp.asarray(np.tile(np.concatenate([-np.ones(half), np.ones(half)]), 2)[None, :], f32)

    def body(p_ref, f_ref, s_ref, cos_ref, sin_ref):
        ang = p_ref[...] * f_ref[...]
        cos_ref[...] = jnp.cos(ang)
        sin_ref[...] = jnp.sin(ang) * s_ref[...]

    return pl.pallas_call(
        body, name=name, grid=(M // tm,),
        in_specs=[pl.BlockSpec((tm, 1), lambda i: (i, 0)),
                  pl.BlockSpec((1, 128), lambda i: (0, 0)),
                  pl.BlockSpec((1, 128), lambda i: (0, 0))],
        out_specs=[pl.BlockSpec((tm, 128), lambda i: (i, 0)),
                   pl.BlockSpec((tm, 128), lambda i: (i, 0))],
        out_shape=[_sds((M, 128), f32), _sds((M, 128), f32)],
        compiler_params=_params(("parallel",)),
    )(pos, inv128, sign128)


def _swap_halves(x):
    w = x.shape[1]
    first = (lax.broadcasted_iota(jnp.int32, (1, w), 1) % HEAD_DIM) < (HEAD_DIM // 2)
    return jnp.where(first, pltpu.roll(x, w - HEAD_DIM // 2, 1), pltpu.roll(x, HEAD_DIM // 2, 1))


def rope_fwd(src, cos, sin, *, name):
    M = src.shape[0]
    tm = min(512, M)

    def body(x_ref, c_ref, s_ref, o_ref):
        x = x_ref[...].astype(f32)
        c = jnp.tile(c_ref[...], (1, MAIN_W // 128))
        s = jnp.tile(s_ref[...], (1, MAIN_W // 128))
        o_ref[...] = x * c + _swap_halves(x) * s

    return pl.pallas_call(
        body, name=name, grid=(M // tm,),
        in_specs=[pl.BlockSpec((tm, MAIN_W), lambda i: (i, 0)),
                  pl.BlockSpec((tm, 128), lambda i: (i, 0)),
                  pl.BlockSpec((tm, 128), lambda i: (i, 0))],
        out_specs=pl.BlockSpec((tm, MAIN_W), lambda i: (i, 0)),
        out_shape=_sds((M, MAIN_W), f32),
        compiler_params=_params(("parallel",)),
    )(src, cos, sin)


def group_sum(groups, cos, sin, *, name, rotate, width, col_block=0, into=None):
    M = groups[0][0].shape[0]
    tm = min(512, M)
    counts = [len(g) for g in groups]
    flat = [a for g in groups for a in g]
    extra = [] if into is None else [into]

    def body(*refs):
        part_refs = refs[:len(flat)]
        c_ref, s_ref = refs[len(flat):len(flat) + 2]
        o_ref = refs[-1]
        cols, k = [], 0
        for n in counts:
            acc = part_refs[k][...]
            for r in part_refs[k + 1:k + n]:
                acc = acc + r[...]
            cols.append(acc)
            k += n
        d = jnp.concatenate(cols, axis=1)
        if rotate:
            c = jnp.tile(c_ref[...], (1, MAIN_W // 128))
            s = jnp.tile(s_ref[...], (1, MAIN_W // 128))
            d = d * c - _swap_halves(d) * s
        o_ref[...] = d.astype(bf16)

    part = pl.BlockSpec((tm, GROUP_W), lambda i: (i, 0))
    tab = pl.BlockSpec((tm, 128), lambda i: (i, 0))
    return pl.pallas_call(
        body, name=name, grid=(M // tm,),
        in_specs=[part] * len(flat) + [tab, tab] + [pl.BlockSpec(memory_space=pl.ANY)] * len(extra),
        out_specs=pl.BlockSpec((tm, MAIN_W), lambda i: (i, col_block)),
        out_shape=_sds((M, width), bf16),
        input_output_aliases={len(flat) + 2: 0} if extra else {},
        compiler_params=_params(("parallel",)),
    )(*flat, cos, sin, *extra)


PAIR_W = 2 * HEAD_DIM
MIN_BLOCKS = 8


def _dil_geometry(dil):
    nsub = max(dil, MIN_BLOCKS)
    tb = BAND * nsub
    return nsub, tb, SEQ // tb


REGROUP = 4


class _Regrouped:
    def __init__(self, ref):
        self.ref = ref
        self.shape = ref.shape

    def fill(self, src):
        q = self.shape[0] // REGROUP
        for r0 in range(REGROUP):
            self.ref[r0 * q:(r0 + 1) * q, :] = src[pl.ds(r0, q, stride=REGROUP), :]

    def drain(self, dst):
        q = self.shape[0] // REGROUP
        for r0 in range(REGROUP):
            dst[pl.ds(r0, q, stride=REGROUP), :] = self.ref[r0 * q:(r0 + 1) * q, :]

    def rows(self, sub, dil):
        nl, r = divmod(sub, dil)
        start = (r % REGROUP) * (self.shape[0] // REGROUP) + r // REGROUP + nl * BAND * (dil // REGROUP)
        return pl.ds(start, BAND, stride=dil // REGROUP)


def _regroups(dil):
    return dil % (4 * REGROUP) == 0


def _rows(ref, sub, dil):
    if isinstance(ref, _Regrouped):
        return ref.ref[ref.rows(sub, dil), :]
    if dil == 1:
        return ref[sub * BAND:(sub + 1) * BAND, :]
    nl, r = divmod(sub, dil)
    return ref[pl.ds(nl * BAND * dil + r, BAND, stride=dil), :]


def _store_rows(ref, sub, dil, val):
    if isinstance(ref, _Regrouped):
        ref.ref[ref.rows(sub, dil), :] = val
    elif dil == 1:
        ref[sub * BAND:(sub + 1) * BAND, :] = val
    else:
        nl, r = divmod(sub, dil)
        ref[pl.ds(nl * BAND * dil + r, BAND, stride=dil), :] = val


def _keys(prev_ref, own_ref, sub, dil):
    nsub = own_ref.shape[0] // BAND
    if sub >= dil:
        prev = _rows(own_ref, sub - dil, dil)
    elif prev_ref is None:
        return _rows(own_ref, sub, dil)
    else:
        prev = _rows(prev_ref, nsub - dil + sub, dil)
    return jnp.concatenate([prev, _rows(own_ref, sub, dil)], axis=0)


def _band_mask(nkeys, has_prev):
    i = lax.broadcasted_iota(jnp.int32, (BAND, nkeys), 0)
    j = lax.broadcasted_iota(jnp.int32, (BAND, nkeys), 1)
    if nkeys == BAND:
        return j <= i
    return (j >= i) & (j <= i + BAND) & (has_prev | (j >= BAND))


def _first_head():
    return lax.broadcasted_iota(jnp.int32, (1, PAIR_W), 1) < HEAD_DIM


def _col(x, hh):
    return x[:, hh * HEAD_DIM:hh * HEAD_DIM + 1]


def _pair_spec(tb, nblk, col0, which):
    def idx(b, p, i):
        if which < 0:
            i = jnp.maximum(i - 1, 0)
        elif which > 0:
            i = jnp.minimum(i + 1, nblk - 1)
        return (b * nblk + i, col0 + p)
    return pl.BlockSpec((tb, PAIR_W), idx)


def dil_fwd(q, k, kv, g, dil, *, name, n_seq):
    M = q.shape[0]
    nsub, tb, nblk = _dil_geometry(dil)
    with_prev = nblk > 1

    regroup = _regroups(dil)
    assert not (regroup and with_prev)

    def body(*refs):
        if with_prev:
            q_ref, ko_ref, vo_ref, kp_ref, vp_ref, o_ref, l_ref = refs
        else:
            (q_ref, ko_ref, vo_ref, o_ref, l_ref), kp_ref, vp_ref = refs[:5], None, None
        outs_to = ()
        if regroup:
            copies = [_Regrouped(s) for s in refs[5:]]
            for c, src in zip(copies, (q_ref, ko_ref, vo_ref)):
                c.fill(src)
            outs_to = ((copies[3], o_ref), (copies[4], l_ref))
            q_ref, ko_ref, vo_ref, o_ref, l_ref = copies
        first = _first_head()
        blk = pl.program_id(2)
        for sub in range(nsub):
            qs = _rows(q_ref, sub, dil) * SCALE
            kc = _keys(kp_ref, ko_ref, sub, dil).astype(bf16)
            vc = _keys(vp_ref, vo_ref, sub, dil).astype(bf16)
            has_prev = True if sub >= dil else blk > 0
            mask = _band_mask(kc.shape[0], has_prev)
            outs, lses = [], []
            for hh in range(2):
                qm = jnp.where(first if hh == 0 else ~first, qs, 0.0).astype(bf16)
                s = jnp.where(mask, lax.dot_general(qm, kc, NT, preferred_element_type=f32), NEG)
                m = jnp.max(s, axis=-1, keepdims=True)
                e = jnp.exp(s - m)
                l = jnp.sum(e, axis=-1, keepdims=True)
                outs.append(jnp.dot(e.astype(bf16), vc, preferred_element_type=f32) * (1.0 / l))
                lses.append(jnp.broadcast_to(m + jnp.log(l), (BAND, PAIR_W)))
            _store_rows(o_ref, sub, dil, jnp.where(first, outs[0], outs[1]))
            _store_rows(l_ref, sub, dil, jnp.where(first, lses[0], lses[1]))
        for c, dst in outs_to:
            c.drain(dst)

    ins = [(q, 2 * g, 0), (k, 2 * g, 0), (kv, 6 + 2 * g, 0)]
    if with_prev:
        ins += [(k, 2 * g, -1), (kv, 6 + 2 * g, -1)]
    out = _pair_spec(tb, nblk, 0, 0)
    return pl.pallas_call(
        body, name=name, grid=(n_seq, 2, nblk),
        in_specs=[_pair_spec(tb, nblk, c, w) for _, c, w in ins],
        out_specs=[out, out],
        out_shape=[_sds((M, GROUP_W), f32)] * 2,
        scratch_shapes=[pltpu.VMEM((tb, PAIR_W), f32)] * (5 if regroup else 0),
        compiler_params=_params(("parallel", "parallel", "arbitrary")),
    )(*[a for a, _, _ in ins])


def combine_fwd(os_, lses, *, name):
    M = os_[0].shape[0]
    tm = min(512, M)

    def body(o0, o1, o2, l0, l1, l2, y_ref):
        ls = [l0[...], l1[...], l2[...]]
        m = jnp.maximum(jnp.maximum(ls[0], ls[1]), ls[2])
        es = [jnp.exp(l - m) for l in ls]
        inv = 1.0 / (es[0] + es[1] + es[2])
        y_ref[...] = jnp.concatenate([o[...] * e * inv for o, e in zip((o0, o1, o2), es)], axis=1).astype(bf16)

    part = pl.BlockSpec((tm, GROUP_W), lambda i: (i, 0))
    return pl.pallas_call(
        body, name=name, grid=(M // tm,),
        in_specs=[part] * 6,
        out_specs=pl.BlockSpec((tm, MAIN_W), lambda i: (i, 0)),
        out_shape=_sds((M, D_MODEL), bf16),
        compiler_params=_params(("parallel",)),
    )(*os_, *lses)


def combine_bwd(dyc, os_, lses, *, name):
    M = os_[0].shape[0]
    tm = min(512, M)

    def body(dy_ref, o0, o1, o2, l0, l1, l2, d0, d1, d2, c0, c1, c2):
        r = lax.broadcasted_iota(jnp.int32, (GROUP_W, GROUP_W), 0) // HEAD_DIM
        c = lax.broadcasted_iota(jnp.int32, (GROUP_W, GROUP_W), 1) // HEAD_DIM
        ones = (r == c).astype(f32)
        dy = dy_ref[...].astype(f32)
        ls = [l0[...], l1[...], l2[...]]
        m = jnp.maximum(jnp.maximum(ls[0], ls[1]), ls[2])
        es = [jnp.exp(l - m) for l in ls]
        inv = 1.0 / (es[0] + es[1] + es[2])
        total = 0.0
        alphas = []
        for g, (o, e, d_ref) in enumerate(zip((o0, o1, o2), es, (d0, d1, d2))):
            a = e * inv
            dyg = dy[:, g * GROUP_W:(g + 1) * GROUP_W]
            d_ref[...] = dyg * a
            dsum = jnp.dot(dyg * o[...], ones, precision=lax.Precision.HIGHEST, preferred_element_type=f32)
            total = total + a * dsum
            alphas.append(a)
        for a, c_ref in zip(alphas, (c0, c1, c2)):
            c_ref[...] = -a * total

    part = pl.BlockSpec((tm, GROUP_W), lambda i: (i, 0))
    outs = pl.pallas_call(
        body, name=name, grid=(M // tm,),
        in_specs=[pl.BlockSpec((tm, MAIN_W), lambda i: (i, 0))] + [part] * 6,
        out_specs=[part] * 6,
        out_shape=[_sds((M, GROUP_W), f32)] * 6,
        compiler_params=_params(("parallel",)),
    )(dyc, *os_, *lses)
    return outs[:3], outs[3:]


def dil_bwd(q, k, kv, do, cc, lse, g, dil, *, name, n_seq):
    M = q.shape[0]
    nsub = SEQ // BAND
    per_res = nsub // dil

    regroup = _regroups(dil)

    def body(q_ref, k_ref, v_ref, do_ref, c_ref, l_ref, dq_ref, dk_ref, dv_ref, *scratch):
        outs_to = ()
        if regroup:
            copies = [_Regrouped(s) for s in scratch]
            for c, src in zip(copies, (q_ref, k_ref, v_ref, do_ref, c_ref, l_ref)):
                c.fill(src)
            outs_to = tuple(zip(copies[6:], (dq_ref, dk_ref, dv_ref)))
            q_ref, k_ref, v_ref, do_ref, c_ref, l_ref, dq_ref, dk_ref, dv_ref = copies
        first = _first_head()
        for r in range(dil):
            carry = None
            for nl in range(per_res):
                sub = nl * dil + r
                qs = _rows(q_ref, sub, dil) * SCALE
                dos = _rows(do_ref, sub, dil)
                cs = _rows(c_ref, sub, dil)
                ls = _rows(l_ref, sub, dil)
                kc = _keys(None, k_ref, sub, dil).astype(bf16)
                vc = _keys(None, v_ref, sub, dil).astype(bf16)
                nkeys = kc.shape[0]
                mask = _band_mask(nkeys, True)
                dqs = []
                dkc = jnp.zeros((nkeys, PAIR_W), f32)
                dvc = jnp.zeros((nkeys, PAIR_W), f32)
                for hh in range(2):
                    lm = first if hh == 0 else ~first
                    qm = jnp.where(lm, qs, 0.0).astype(bf16)
                    dom = jnp.where(lm, dos, 0.0).astype(bf16)
                    s = jnp.where(mask, lax.dot_general(qm, kc, NT, preferred_element_type=f32), NEG)
                    p = jnp.exp(s - _col(ls, hh))
                    dp = lax.dot_general(dom, vc, NT, preferred_element_type=f32)
                    ds = (p * (dp + _col(cs, hh))).astype(bf16)
                    dqs.append(jnp.dot(ds, kc, preferred_element_type=f32) * SCALE)
                    dkc = dkc + lax.dot_general(ds, qm, TN, preferred_element_type=f32)
                    dvc = dvc + lax.dot_general(p.astype(bf16), dom, TN, preferred_element_type=f32)
                _store_rows(dq_ref, sub, dil, jnp.where(first, dqs[0], dqs[1]))
                if nkeys == 2 * BAND:
                    _store_rows(dk_ref, sub - dil, dil, carry[0] + dkc[:BAND])
                    _store_rows(dv_ref, sub - dil, dil, carry[1] + dvc[:BAND])
                    carry = (dkc[BAND:], dvc[BAND:])
                else:
                    carry = (dkc, dvc)
            _store_rows(dk_ref, (per_res - 1) * dil + r, dil, carry[0])
            _store_rows(dv_ref, (per_res - 1) * dil + r, dil, carry[1])
        for c, dst in outs_to:
            c.drain(dst)

    def spec(col0):
        return pl.BlockSpec((SEQ, PAIR_W), lambda b, p: (b, col0 + p))

    out = spec(0)
    return pl.pallas_call(
        body, name=name, grid=(n_seq, 2),
        in_specs=[spec(2 * g), spec(2 * g), spec(6 + 2 * g), spec(0), spec(0), spec(0)],
        out_specs=[out, out, out],
        out_shape=[_sds((M, GROUP_W), f32)] * 3,
        scratch_shapes=[pltpu.VMEM((SEQ, PAIR_W), f32)] * (9 if regroup else 0),
        compiler_params=_params(("parallel", "parallel")),
    )(q, k, kv, do, cc, lse)


def _blockdiag(wp):
    out = jnp.zeros((MAIN_W, MAIN_W), wp.dtype)
    for gi in range(len(POOL_WINDOWS)):
        sl = slice(gi * POOL_GROUP, (gi + 1) * POOL_GROUP)
        out = out.at[sl, sl].set(wp[gi])
    return out


def _unblockdiag(w):
    return jnp.stack([w[gi * POOL_GROUP:(gi + 1) * POOL_GROUP, gi * POOL_GROUP:(gi + 1) * POOL_GROUP]
                      for gi in range(len(POOL_WINDOWS))])


def local_step(x, mem, positions, target, P, layer_weights, kv_weight, emit_grads):
    n_seq = x.shape[0]
    M = n_seq * SEQ
    xs = x.reshape(M, D_MODEL)
    mems = mem.reshape(n_seq * N_MEM, D_MODEL)
    pos = positions.reshape(M, 1).astype(f32)
    cos, sin = rope_tables(pos, name="rope_tables")
    gains = P["norm_gains"]

    def gain(l, k):
        return gains[l, k].reshape(1, D_MODEL)

    saved = []
    kvs = None
    for l in range(DEPTH):
        W, started = layer_weights(l, "mix", xs)
        sv = {"x": xs, "W": W}
        z, h1 = rms_matmul(xs, gain(l, 0), W["w_in"], name=f"l{l}_in", out_dtype=bf16, after=started)
        kvm, mn = rms_matmul(mems, P["mem_norm"][l].reshape(1, D_MODEL), W["w_mem_kv"],
                             name=f"l{l}_memkv", out_dtype=bf16)
        sv.update(z=z, h1=h1, kvm=kvm, mn=mn)
        if l < N_A_LAYERS:
            wbd = _blockdiag(P["w_pool"][l].astype(bf16))
            psc = P["pool_scale"][l].reshape(1, MAIN_W)
            p, y_main = pool_fwd(z, wbd, psc, name=f"l{l}_pool")
            sv.update(p=p, wbd=wbd, psc=psc)
        else:
            qrot = rope_fwd(z, cos, sin, name=f"l{l}_ropeq")
            os_, lses = [], []
            for g, (_, dil) in enumerate(DIL_PATTERNS):
                o, lse = dil_fwd(qrot, kvs["krot"], kvs["kv"], g, dil, name=f"l{l}_dil{g}", n_seq=n_seq)
                os_.append(o)
                lses.append(lse)
            y_main = combine_fwd(os_, lses, name=f"l{l}_comb")
            sv.update(qrot=qrot, os=os_, lses=lses)
        ycat = memattn_fwd(z, kvm, y_main, name=f"l{l}_memattn", n_seq=n_seq)
        y, x1 = matmul_rms_res(ycat, W["w_out"], gain(l, 1), xs, name=f"l{l}_out")
        W.update(layer_weights(l, "gu", x1)[0])
        fg, fu, a, h2 = rms_gate_up(x1, gain(l, 2), W["w_gate_up"], name=f"l{l}_gu")
        W.update(layer_weights(l, "down", a)[0])
        y2, x2 = matmul_rms_res(a, W["w_down"], gain(l, 3), x1, name=f"l{l}_down")
        sv.update(ycat=ycat, y=y, x1=x1, fg=fg, fu=fu, h2=h2, a=a, y2=y2)
        saved.append(sv)
        xs = x2
        if l == N_A_LAYERS - 1:
            w_kv = kv_weight(xs)
            kv, hkv = rms_matmul(xs, P["kv_norm"].reshape(1, D_MODEL), w_kv, name="kv_proj", out_dtype=f32,
                                 transposed=True)
            krot = rope_fwd(kv, cos, sin, name="ropek")
            kvs = {"kv": kv, "hkv": hkv, "krot": krot, "x": xs, "w_kv": w_kv}

    dx, sq = loss_head(xs, target.reshape(M, D_MODEL), name="loss_head")

    G = {"mem_norm": [None] * DEPTH, "norm_gains": [[None] * 4 for _ in range(DEPTH)],
         "pool_scale": [None] * N_A_LAYERS}
    dk_parts = [[] for _ in range(N_GROUPS)]
    dv_parts = [[] for _ in range(N_GROUPS)]
    emitted = None

    for l in reversed(range(DEPTH)):
        sv = saved[l]
        W = sv["W"]
        gw = {}
        dy2, dgu, G["norm_gains"][l][3] = down_bwd(sv["y2"], gain(l, 3), dx, W["w_down"], sv["fg"], sv["fu"],
                                                   name=f"l{l}_b_dgu", after=emitted)
        gw["w_down"] = matmul(sv["a"], dy2, TN, name=f"l{l}_b_wd", out_dtype=bf16)
        gw["w_gate_up"] = matmul(dgu, sv["h2"], TN, name=f"l{l}_b_wgu", out_dtype=bf16)
        emitted = emit_grads(l, "ffn", gw)
        dx1, G["norm_gains"][l][2] = matmul_rms_bwd(dgu, W["w_gate_up"], NN, sv["x1"], gain(l, 2), dx,
                                                    name=f"l{l}_b_dh2", after=emitted)
        gw = {}
        dy, dycat, G["norm_gains"][l][1] = rms_bwd_matmul(sv["y"], gain(l, 1), dx1, W["w_out"], NT,
                                                          name=f"l{l}_b_dycat", after=emitted)
        gw["w_out"] = matmul(sv["ycat"], dy, TN, name=f"l{l}_b_wout", out_dtype=bf16)
        if l < N_A_LAYERS:
            dz, dwbd, dps = pool_bwd(dycat, sv["p"], sv["wbd"], sv["psc"], name=f"l{l}_b_pool")
            gw["w_pool"] = _unblockdiag(dwbd).reshape(MAIN_W, POOL_GROUP).astype(bf16)
            G["pool_scale"][l] = dps.reshape(MAIN_W)
        else:
            dos, ccs = combine_bwd(dycat, sv["os"], sv["lses"], name=f"l{l}_b_comb")
            dqs = []
            for g, (_, dil) in enumerate(DIL_PATTERNS):
                args = (sv["qrot"], kvs["krot"], kvs["kv"], dos[g], ccs[g], sv["lses"][g], g, dil)
                dq, dk, dv = dil_bwd(*args, name=f"l{l}_b_dil{g}", n_seq=n_seq)
                dqs.append([dq])
                dk_parts[g].append(dk)
                dv_parts[g].append(dv)
            dz = group_sum(dqs, cos, sin, name=f"l{l}_b_ropeq", rotate=True, width=D_MODEL)
        dz, dkvm = memattn_bwd(sv["z"], sv["kvm"], dycat, dz, name=f"l{l}_b_memattn", n_seq=n_seq)
        gw["w_mem_kv"] = matmul(sv["mn"], dkvm, TN, name=f"l{l}_b_wmkv", out_dtype=bf16)
        _, G["mem_norm"][l] = matmul_rms_bwd(dkvm, W["w_mem_kv"], NT, mems, P["mem_norm"][l].reshape(1, D_MODEL),
                                             mems, name=f"l{l}_b_dmn")
        gw["w_in"] = matmul(sv["h1"], dz, TN, name=f"l{l}_b_win", out_dtype=bf16)
        if l != N_A_LAYERS:
            emitted = emit_grads(l, "mix", gw)
        dx, G["norm_gains"][l][0] = matmul_rms_bwd(dz, W["w_in"], NT, sv["x"], gain(l, 0), dx1, name=f"l{l}_b_dh1",
                                                   after=emitted)
        if l == N_A_LAYERS:
            dkv = group_sum(dk_parts, cos, sin, name="b_ropek", rotate=True, width=2 * MAIN_W)
            dkv = group_sum(dv_parts, cos, sin, name="b_sumv", rotate=False, width=2 * MAIN_W, col_block=1, into=dkv)
            gw["w_kv"] = matmul(dkv, kvs["hkv"], TN, name="b_wkv", out_dtype=bf16)
            dx, gkn = matmul_rms_bwd(dkv, kvs["w_kv"], NN, kvs["x"], P["kv_norm"].reshape(1, D_MODEL), dx,
                                     name="b_dhkv")
            G["kv_norm"] = gkn.reshape(D_MODEL)
            emitted = emit_grads(l, "mix", gw)

    small = {"pool_scale": jnp.stack(G["pool_scale"]),
             "mem_norm": jnp.concatenate(G["mem_norm"], axis=0),
             "norm_gains": jnp.stack([jnp.concatenate(r, axis=0) for r in G["norm_gains"]]),
             "kv_norm": G["kv_norm"]}
    return sq[0, 0], dx.reshape(n_seq, SEQ, D_MODEL), small, emitted


def _peer(k):
    x, y, c = lax.axis_index("x"), lax.axis_index("y"), lax.axis_index("c")
    px = 1 - x if k & 4 else x
    py = 1 - y if k & 2 else y
    pc = 1 - c if k & 1 else c
    return (px, py, pc), 4 * px + 2 * py + pc


def _my_index():
    return 4 * lax.axis_index("x") + 2 * lax.axis_index("y") + lax.axis_index("c")


def _src_for(kinds, in_refs, i, idx):
    return in_refs[i] if kinds[i] == "gather" else in_refs[i].at[idx]


def _local_copies(kinds, in_refs, out_refs, local_sems):
    me = _my_index()
    return [pltpu.make_async_copy(_src_for(kinds, in_refs, i, me), out_refs[i].at[me], local_sems.at[i])
            for i in range(len(kinds))]


def _remote_copies(kinds, in_refs, out_refs, send_sems, recv_sems, *, arriving):
    me = _my_index()
    copies = []
    for k in range(1, N_DEV):
        dev, idx = _peer(k)
        for i in range(len(kinds)):
            j = i * (N_DEV - 1) + k - 1
            copies.append(pltpu.make_async_remote_copy(
                src_ref=_src_for(kinds, in_refs, i, idx), dst_ref=out_refs[i].at[idx if arriving else me],
                send_sem=send_sems.at[j], recv_sem=recv_sems.at[j], device_id=dev, device_id_type=MESH))
    return copies


def _out_shape(a, kind):
    return ((N_DEV,) + a.shape) if kind == "gather" else a.shape


def exchange(items, *, name, after=()):
    n = len(items)
    kinds = [k for _, k in items]
    after = list(after)

    def body(*refs):
        in_refs, out_refs = refs[:n], refs[n + len(after):2 * n + len(after)]
        send_sems, recv_sems, local_sems = refs[-3:]
        local = _local_copies(kinds, in_refs, out_refs, local_sems)
        sends = _remote_copies(kinds, in_refs, out_refs, send_sems, recv_sems, arriving=False)
        for cp in local + sends:
            cp.start()
        for cp in _remote_copies(kinds, in_refs, out_refs, send_sems, recv_sems, arriving=True):
            cp.wait_recv()
        for cp in sends:
            cp.wait_send()
        for cp in local:
            cp.wait()

    any_spec = pl.BlockSpec(memory_space=pl.ANY)
    return pl.pallas_call(
        body, name=name,
        in_specs=[any_spec] * (n + len(after)), out_specs=[any_spec] * n,
        out_shape=[_sds(_out_shape(a, k), a.dtype) for a, k in items],
        scratch_shapes=[pltpu.SemaphoreType.DMA((n * (N_DEV - 1),)), pltpu.SemaphoreType.DMA((n * (N_DEV - 1),)),
                        pltpu.SemaphoreType.DMA((n,))],
    )(*[a for a, _ in items], *after)


_HBM = pl.BlockSpec(memory_space=pltpu.HBM)
_SEM = pl.BlockSpec(memory_space=pltpu.SEMAPHORE)
_EFFECT = pltpu.SideEffectType.DATAFLOW_SIDE_EFFECTING


def exchange_start(items, after, *, name):
    n = len(items)
    kinds = [k for _, k in items]

    def body(*refs):
        in_refs, land_refs = refs[:n], refs[n:2 * n]
        send_sems, recv_sems, local_sems = refs[2 * n + 1:2 * n + 4]
        token = refs[-1]
        for cp in (_local_copies(kinds, in_refs, land_refs, local_sems)
                   + _remote_copies(kinds, in_refs, land_refs, send_sems, recv_sems, arriving=False)):
            cp.start()
        token[...] = jnp.zeros_like(token)

    srcs = [pltpu.with_memory_space_constraint(a, pltpu.HBM) for a, _ in items]
    lands = [pltpu.with_memory_space_constraint(lax.empty(_out_shape(a, k), a.dtype), pltpu.HBM) for a, k in items]
    outs = pl.pallas_call(
        body, name=name,
        out_shape=(pltpu.SemaphoreType.DMA((n * (N_DEV - 1),)), pltpu.SemaphoreType.DMA((n * (N_DEV - 1),)),
                   pltpu.SemaphoreType.DMA((n,)),
                   *[pltpu.HBM(a.shape, a.dtype) for a in srcs], *[pltpu.HBM(a.shape, a.dtype) for a in lands],
                   _sds((8, 128), f32)),
        in_specs=[_HBM] * (2 * n) + [pl.BlockSpec(memory_space=pl.ANY)],
        out_specs=(_SEM, _SEM, _SEM, *[_HBM] * (2 * n), pl.BlockSpec(memory_space=pltpu.VMEM)),
        input_output_aliases={i: 3 + i for i in range(2 * n)},
        compiler_params=pltpu.CompilerParams(has_side_effects=_EFFECT),
    )(*srcs, *lands, after)
    return {"kinds": kinds, "sems": outs[:3], "srcs": outs[3:3 + n], "lands": outs[3 + n:3 + 2 * n], "token": outs[-1]}


def exchange_wait(handle, after, *, name):
    kinds = handle["kinds"]
    n = len(kinds)

    def body(*refs):
        in_refs, land_refs = refs[:n], refs[n:2 * n]
        send_sems, recv_sems, local_sems = refs[2 * n:2 * n + 3]
        for cp in _remote_copies(kinds, in_refs, land_refs, send_sems, recv_sems, arriving=True):
            cp.wait_recv()
        for cp in _remote_copies(kinds, in_refs, land_refs, send_sems, recv_sems, arriving=False):
            cp.wait_send()
        for cp in _local_copies(kinds, in_refs, land_refs, local_sems):
            cp.wait()

    srcs, lands = list(handle["srcs"]), list(handle["lands"])
    after = list(after) if isinstance(after, (list, tuple)) else [after]
    outs = pl.pallas_call(
        body, name=name,
        out_shape=tuple(pltpu.HBM(a.shape, a.dtype) for a in srcs + lands),
        in_specs=[_HBM] * (2 * n) + [_SEM] * 3 + [pl.BlockSpec(memory_space=pl.ANY)] * len(after),
        out_specs=tuple([_HBM] * (2 * n)),
        input_output_aliases={i: i for i in range(2 * n)},
        compiler_params=pltpu.CompilerParams(has_side_effects=_EFFECT),
    )(*srcs, *lands, *handle["sems"], *after)
    return list(outs[n:])


CHIP_MASKS = (2, 4, 6)


def _g2_first(in_refs, land_refs, send_sems, recv_sems, *, masks, arriving):
    me = _my_index()
    copies = []
    for i in range(len(land_refs)):
        for j, k in enumerate(masks):
            dev, idx = _peer(k)
            dst = land_refs[i].at[idx if arriving else me]
            copies.append(pltpu.make_async_remote_copy(
                src_ref=dst if in_refs is None else in_refs[i], dst_ref=dst,
                send_sem=send_sems.at[i * len(masks) + j], recv_sem=recv_sems.at[i * len(masks) + j],
                device_id=dev, device_id_type=MESH))
    return copies


def _g2_forward(land_refs, fwd_send, fwd_recv, *, arriving):
    sibling, _ = _peer(1)
    copies = []
    for i in range(len(land_refs)):
        for j, k in enumerate(CHIP_MASKS):
            _, idx = _peer(k | 1 if arriving else k)
            copies.append(pltpu.make_async_remote_copy(
                src_ref=land_refs[i].at[idx], dst_ref=land_refs[i].at[idx],
                send_sem=fwd_send.at[i * 3 + j], recv_sem=fwd_recv.at[i * 3 + j], device_id=sibling,
                device_id_type=MESH))
    return copies


def gather2_start(arrays, after, *, name):
    n = len(arrays)

    def body(*refs):
        in_refs, land_refs = refs[:n], refs[n:2 * n]
        ici_send, ici_recv, d2d_send, d2d_recv, local_sems = refs[2 * n + 1:2 * n + 6]
        token = refs[-1]
        ici = _g2_first(in_refs, land_refs, ici_send, ici_recv, masks=CHIP_MASKS, arriving=False)
        d2d = _g2_first(in_refs, land_refs, d2d_send, d2d_recv, masks=(1,), arriving=False)
        for cp in _local_copies(["gather"] * n, in_refs, land_refs, local_sems) + ici + d2d:
            cp.start()
        token[...] = jnp.zeros_like(token)

    srcs = [pltpu.with_memory_space_constraint(a, pltpu.HBM) for a in arrays]
    lands = [pltpu.with_memory_space_constraint(lax.empty((N_DEV,) + a.shape, a.dtype), pltpu.HBM) for a in arrays]
    sem = pltpu.SemaphoreType.DMA
    outs = pl.pallas_call(
        body, name=name,
        out_shape=(sem((3 * n,)), sem((3 * n,)), sem((n,)), sem((n,)), sem((n,)),
                   *[pltpu.HBM(a.shape, a.dtype) for a in srcs], *[pltpu.HBM(a.shape, a.dtype) for a in lands],
                   _sds((8, 128), f32)),
        in_specs=[_HBM] * (2 * n) + [pl.BlockSpec(memory_space=pl.ANY)],
        out_specs=(*[_SEM] * 5, *[_HBM] * (2 * n), pl.BlockSpec(memory_space=pltpu.VMEM)),
        input_output_aliases={i: 5 + i for i in range(2 * n)},
        compiler_params=pltpu.CompilerParams(has_side_effects=_EFFECT),
    )(*srcs, *lands, after)
    return {"n": n, "sems": outs[:5], "srcs": outs[5:5 + n], "lands": outs[5 + n:5 + 2 * n], "token": outs[-1]}


def gather2_forward(handle, after, *, name):
    n = handle["n"]

    def body(*refs):
        land_refs = refs[:n]
        ici_recv = refs[n]
        fwd_send, fwd_recv = refs[n + 2:n + 4]
        for cp in _g2_first(None, land_refs, fwd_send, ici_recv, masks=CHIP_MASKS, arriving=True):
            cp.wait_recv()
        for cp in _g2_forward(land_refs, fwd_send, fwd_recv, arriving=False):
            cp.start()

    lands = list(handle["lands"])
    sem = pltpu.SemaphoreType.DMA
    outs = pl.pallas_call(
        body, name=name,
        out_shape=(sem((3 * n,)), sem((3 * n,)), *[pltpu.HBM(a.shape, a.dtype) for a in lands]),
        in_specs=[_HBM] * n + [_SEM, pl.BlockSpec(memory_space=pl.ANY)],
        out_specs=(_SEM, _SEM, *[_HBM] * n),
        input_output_aliases={i: 2 + i for i in range(n)},
        compiler_params=pltpu.CompilerParams(has_side_effects=_EFFECT),
    )(*lands, handle["sems"][1], after)
    return dict(handle, fwd=outs[:2], lands=outs[2:])


def gather2_wait(handle, after, *, name):
    n = handle["n"]

    def body(*refs):
        in_refs, land_refs = refs[:n], refs[n:2 * n]
        ici_send, d2d_send, d2d_recv, local_sems, fwd_send, fwd_recv = refs[2 * n:2 * n + 6]
        for cp in _g2_first(in_refs, land_refs, d2d_send, d2d_recv, masks=(1,), arriving=True):
            cp.wait_recv()
        for cp in _g2_forward(land_refs, fwd_send, fwd_recv, arriving=True):
            cp.wait_recv()
        for cp in (_g2_first(in_refs, land_refs, ici_send, fwd_recv, masks=CHIP_MASKS, arriving=False)
                   + _g2_first(in_refs, land_refs, d2d_send, d2d_recv, masks=(1,), arriving=False)
                   + _g2_forward(land_refs, fwd_send, fwd_recv, arriving=False)):
            cp.wait_send()
        for cp in _local_copies(["gather"] * n, in_refs, land_refs, local_sems):
            cp.wait()

    srcs, lands = list(handle["srcs"]), list(handle["lands"])
    s = handle["sems"]
    outs = pl.pallas_call(
        body, name=name,
        out_shape=tuple(pltpu.HBM(a.shape, a.dtype) for a in srcs + lands),
        in_specs=[_HBM] * (2 * n) + [_SEM] * 6 + [pl.BlockSpec(memory_space=pl.ANY)],
        out_specs=tuple([_HBM] * (2 * n)),
        input_output_aliases={i: i for i in range(2 * n)},
        compiler_params=pltpu.CompilerParams(has_side_effects=_EFFECT),
    )(*srcs, *lands, s[0], s[2], s[3], s[4], *handle["fwd"], after)
    return list(outs[n:])


def adamw(entries, *, name):
    c1 = 1.0 - ADAM_B1 ** ADAM_STEP
    c2 = 1.0 - ADAM_B2 ** ADAM_STEP
    tiles = [_tile(w.shape[-2], (64, 32, 16, 8)) for _, w, _, _, _, _ in entries]
    steps = [w.shape[-2] // tr for (_, w, _, _, _, _), tr in zip(entries, tiles)]
    n = len(entries)

    def body(*refs):
        i = pl.program_id(0)
        for e in range(n):
            s_ref, w_ref, m_ref, v_ref = refs[4 * e:4 * e + 4]
            g_ref, d_ref, m2_ref, v2_ref = refs[len(refs) - 4 * n + 4 * e:len(refs) - 4 * n + 4 * e + 4]

            @pl.when(i < steps[e])
            def _():
                g = s_ref[0].astype(f32)
                for d in range(1, N_DEV):
                    g = g + s_ref[d].astype(f32)
                m2 = ADAM_B1 * m_ref[...] + (1.0 - ADAM_B1) * g
                v2 = ADAM_B2 * v_ref[...] + (1.0 - ADAM_B2) * (g * g)
                g_ref[...] = g
                m2_ref[...] = m2
                v2_ref[...] = v2
                d_ref[...] = -ADAM_LR * ((m2 / c1) / (jnp.sqrt(v2 / c2) + ADAM_EPS) + ADAM_WD * w_ref[...])

    in_specs, out_specs, out_shape, args, extras, aliases = [], [], [], [], [], {}
    for e, ((slots, w, m, v, layer, into), tr, ns) in enumerate(zip(entries, tiles, steps)):
        C = w.shape[-1]
        row = lambda i, ns=ns: jnp.minimum(i, ns - 1)
        if layer is None:
            blk = pl.BlockSpec((tr, C), lambda i, row=row: (row(i), 0))
        else:
            blk = pl.BlockSpec((None, tr, C), lambda i, row=row, layer=layer: (layer, row(i), 0))
        in_specs += [pl.BlockSpec((N_DEV, tr, C), lambda i, row=row: (0, row(i), 0)), blk, blk, blk]
        args += [slots, w, m, v]
        out_specs += [blk] * 4
        out_shape += [_sds(w.shape, f32)] * 4
        if into is not None:
            for t, a in enumerate(into):
                aliases[4 * n + len(extras)] = 4 * e + t
                extras.append(a)
    outs = pl.pallas_call(
        body, name=name, grid=(max(steps),),
        in_specs=in_specs + [pl.BlockSpec(memory_space=pl.ANY)] * len(extras),
        out_specs=out_specs, out_shape=out_shape, input_output_aliases=aliases,
        compiler_params=_params(("arbitrary",)),
    )(*args, *extras)
    return [outs[4 * e:4 * e + 4] for e in range(n)]


WEIGHTS = ("norm_gains", "mem_norm", "w_in", "w_mem_kv", "w_out", "w_pool", "pool_scale", "kv_norm", "w_kv",
           "w_gate_up", "w_down")
LAYER_MATS = ("w_in", "w_mem_kv", "w_out", "w_gate_up", "w_down")
POOL_SHARD = MAIN_W // N_DEV
KV_SHARD = 2 * MAIN_W // N_DEV
LOOKAHEAD = 2
TWO_LEVEL_LAYERS = (0, 1)


def _pack_small(gains, pscale):
    lead = gains.shape[:-3]
    g = gains.reshape(lead + (16, 128))
    p = jnp.zeros(lead + (8, 128), f32).at[..., :2, :POOL_SHARD].set(pscale)
    return jnp.concatenate([g, p], axis=-2)


def _unpack_small(a):
    return a[:16].reshape(4, 4, 128), a[16:18, :POOL_SHARD]


def _pack_repl(mem_norm, kv_norm):
    return jnp.concatenate([mem_norm, kv_norm.reshape(1, D_MODEL), jnp.zeros((3, D_MODEL), f32)], axis=0)


def _unpack_repl(a):
    return a[:4], a[4]


def kernel(x, mem, positions, norm_gains, mem_norm, w_in, w_mem_kv, w_out, w_pool, pool_scale, kv_norm, w_kv, w_gate_up, w_down, loss_target, m_norm_gains, m_mem_norm, m_w_in, m_w_mem_kv, m_w_out, m_w_pool, m_pool_scale, m_kv_norm, m_w_kv, m_w_gate_up, m_w_down, v_norm_gains, v_mem_norm, v_w_in, v_w_mem_kv, v_w_out, v_w_pool, v_pool_scale, v_kv_norm, v_w_kv, v_w_gate_up, v_w_down):
    w = dict(norm_gains=norm_gains, mem_norm=mem_norm, w_in=w_in, w_mem_kv=w_mem_kv, w_out=w_out, w_pool=w_pool,
             pool_scale=pool_scale, kv_norm=kv_norm, w_kv=w_kv, w_gate_up=w_gate_up, w_down=w_down)
    m = dict(norm_gains=m_norm_gains, mem_norm=m_mem_norm, w_in=m_w_in, w_mem_kv=m_w_mem_kv, w_out=m_w_out,
             w_pool=m_w_pool, pool_scale=m_pool_scale, kv_norm=m_kv_norm, w_kv=m_w_kv, w_gate_up=m_w_gate_up,
             w_down=m_w_down)
    v = dict(norm_gains=v_norm_gains, mem_norm=v_mem_norm, w_in=v_w_in, w_mem_kv=v_w_mem_kv, w_out=v_w_out,
             w_pool=v_w_pool, pool_scale=v_pool_scale, kv_norm=v_kv_norm, w_kv=v_w_kv, w_gate_up=v_w_gate_up,
             w_down=v_w_down)

    def transposed_view(d):
        d = dict(d)
        d["w_gate_up"] = jnp.swapaxes(d["w_gate_up"], 1, 2)
        d["w_kv"] = jnp.swapaxes(d["w_kv"], 0, 1)
        return d

    wv, mv, vv = transposed_view(w), transposed_view(m), transposed_view(v)

    small = _pack_small(norm_gains, pool_scale)
    (gsmall,) = exchange([(small, "gather")], name="gather_small")
    P = {"norm_gains": jnp.moveaxis(gsmall[:, :16].reshape(N_DEV, 4, 4, 128), 0, 2).reshape(4, 4, D_MODEL),
         "pool_scale": jnp.moveaxis(gsmall[:, 16:18, :POOL_SHARD], 0, 1).reshape(2, MAIN_W),
         "mem_norm": mem_norm, "kv_norm": kv_norm, "w_pool": w_pool}

    PARTS = {"mix": ("w_in", "w_mem_kv", "w_out"), "ffn": ("w_gate_up", "w_down"), "gu": ("w_gate_up",),
             "down": ("w_down",), "all": ("w_in", "w_mem_kv", "w_out", "w_gate_up", "w_down")}

    def parts_of(l):
        return (("mix", "gu", "down"), ("mix", "ffn"))[l] if l < 2 else ("all",)

    def part_items(l, part):
        items = [(wv[k][l].astype(bf16), "gather") for k in PARTS[part]]
        if part == "ffn" and l == N_A_LAYERS - 1:
            items.append((wv["w_kv"].astype(bf16), "gather"))
        return items

    handles = {}

    def start_layer(l, after):
        for part in parts_of(l):
            if l in TWO_LEVEL_LAYERS:
                handles[l, part] = gather2_start([a for a, _ in part_items(l, part)], after,
                                                 name=f"gather_start_{part}_l{l}")
            else:
                handles[l, part] = exchange_start(part_items(l, part), after, name=f"gather_start_{part}_l{l}")
            after = handles[l, part]["token"]
        return after

    token = gsmall
    for l in range(LOOKAHEAD):
        token = start_layer(l, token)
    landed = {}

    def layer_weights(l, part, after):
        if part not in parts_of(l):
            if part == "down" or (part == "gu" and "all" in parts_of(l)):
                return {}, None
            part = "all" if "all" in parts_of(l) else "ffn"
        if l == 0 and part == "mix":
            after = token
        if l in TWO_LEVEL_LAYERS:
            passed = gather2_forward(handles[l, part], after, name=f"gather_forward_{part}_l{l}")
            got = gather2_wait(passed, after, name=f"gather_wait_{part}_l{l}")
        else:
            got = exchange_wait(handles[l, part], after, name=f"gather_wait_{part}_l{l}")
        landed[l, part] = got
        started = None
        if part in ("mix", "all") and l + LOOKAHEAD < DEPTH:
            started = start_layer(l + LOOKAHEAD, got[0])
        W = {k: g.reshape(-1, g.shape[-1]) for k, g in zip(PARTS[part], got)}
        return W, started

    def kv_weight(after):
        g = landed[N_A_LAYERS - 1, "ffn"][len(PARTS["ffn"])]
        return g.reshape(2 * MAIN_W, D_MODEL)

    ghandles = {}

    pending = {}

    def gparts_of(l):
        return ("ffn", "mix") if l < 2 else ("all",)

    def emit_grads(l, part, gw):
        if part not in gparts_of(l):
            pending.setdefault(l, {}).update(gw)
            if part == "ffn":
                return None
            gw, part = pending[l], "all"
        items = [(gw[k].reshape((N_DEV, -1) + gw[k].shape[-1:]), "scatter") for k in PARTS[part]]
        if part != "ffn" and l == N_A_LAYERS:
            items.append((gw["w_kv"].reshape(N_DEV, KV_SHARD, D_MODEL), "scatter"))
        if part != "ffn" and l < N_A_LAYERS:
            items.append((gw["w_pool"], "gather"))
        ghandles[l, part] = exchange_start(items, gsmall, name=f"scatter_start_{part}_l{l}")
        return ghandles[l, part]["token"]

    sq, grad_x, GS, emitted = local_step(x, mem, positions, loss_target, P, layer_weights, kv_weight, emit_grads)
    loss = lax.psum(0.5 * sq / D_MODEL, ("x", "y", "c"))

    def pool3(a):
        return a.reshape(N_A_LAYERS, MAIN_W, POOL_GROUP)

    out = {}
    after = [emitted]

    def finish_layer(l, after):
        for part in gparts_of(l):
            got = exchange_wait(ghandles[l, part], after, name=f"scatter_wait_{part}_l{l}")
            names = list(PARTS[part])
            entries = [(slots, wv[k], mv[k], vv[k], l, out.get(k)) for k, slots in zip(names, got)]
            if part != "ffn" and l == N_A_LAYERS:
                names.append("w_kv")
                entries.append((got[-1], wv["w_kv"], mv["w_kv"], vv["w_kv"], None, None))
            if part != "ffn" and l < N_A_LAYERS:
                names.append("w_pool")
                entries.append((got[-1], pool3(w_pool), pool3(m_w_pool), pool3(v_w_pool), l, out.get("w_pool")))
            out.update(zip(names, adamw(entries, name=f"adamw_{part}_l{l}")))
            after = [out[k][0] for k in names]
        return after

    for l in reversed(range(1, DEPTH)):
        after = finish_layer(l, after)

    gs = _pack_small(jnp.moveaxis(GS["norm_gains"].reshape(4, 4, N_DEV, 128), 2, 0),
                     jnp.moveaxis(GS["pool_scale"].reshape(2, N_DEV, POOL_SHARD), 1, 0))
    parts_small, parts_repl = exchange(
        [(gs, "scatter"), (_pack_repl(GS["mem_norm"], GS["kv_norm"]), "gather")],
        name="exchange_small_grads", after=after)
    finish_layer(0, [parts_small])
    out["w_gate_up"] = [jnp.swapaxes(r, 1, 2) for r in out["w_gate_up"]]
    out["w_kv"] = [jnp.swapaxes(r, 0, 1) for r in out["w_kv"]]
    out["w_pool"] = [r.reshape(w_pool.shape) for r in out["w_pool"]]

    res_small, res_repl = adamw(
        [(parts_small, small, _pack_small(m_norm_gains, m_pool_scale), _pack_small(v_norm_gains, v_pool_scale),
          None, None),
         (parts_repl, _pack_repl(mem_norm, kv_norm), _pack_repl(m_mem_norm, m_kv_norm),
          _pack_repl(v_mem_norm, v_kv_norm), None, None)], name="adamw_small")
    out["norm_gains"], out["pool_scale"] = zip(*[_unpack_small(r) for r in res_small])
    out["mem_norm"], out["kv_norm"] = zip(*[_unpack_repl(r) for r in res_repl])

    return (loss, grad_x, *[out[k][0] for k in WEIGHTS], *[out[k][1] for k in WEIGHTS],
            *[out[k][2] for k in WEIGHTS], *[out[k][3] for k in WEIGHTS])
```

```python
import numpy as np
import jax
import jax.numpy as jnp
from jax import lax
from jax.experimental import pallas as pl
from jax.experimental.pallas import tpu as pltpu

f32 = jnp.float32
bf16 = jnp.bfloat16

D_MODEL = 1024
SEQ = 2048
DEPTH = 4
N_MEM = 256
HEAD_DIM = 64
N_MEM_HEADS = 4
MEM_W = 256
MAIN_W = 768
POOL_WINDOWS = (2, 4, 8, 16)
POOL_GROUP = 192
POOL_HALO = 16
DIL_PATTERNS = ((128, 1), (512, 4), (2048, 16))
N_GROUPS = 3
GROUP_W = 256
BAND = 128
N_A_LAYERS = 2
D_FF = 2816
ROPE_THETA = 10000.0
EPS = 1e-6
NEG = -1e30
SCALE = HEAD_DIM ** -0.5
N_DEV = 8

ADAM_LR = 0.001
ADAM_B1 = 0.9
ADAM_B2 = 0.999
ADAM_EPS = 1e-08
ADAM_WD = 0.01
ADAM_STEP = 10

VMEM_LIMIT_BYTES = 56 * 1024 * 1024
MESH = pl.DeviceIdType.MESH

NN = (((1,), (0,)), ((), ()))
NT = (((1,), (1,)), ((), ()))
TN = (((0,), (0,)), ((), ()))


def _params(sem=None):
    return pltpu.CompilerParams(dimension_semantics=sem, vmem_limit_bytes=VMEM_LIMIT_BYTES)


def _tile(n, cands):
    for c in cands:
        if n % c == 0:
            return c
    return n


def _sds(shape, dtype):
    return jax.ShapeDtypeStruct(tuple(shape), dtype)


def _rms_r(v):
    return lax.rsqrt(jnp.mean(v * v, axis=-1, keepdims=True) + EPS)


def rms_matmul(x, gain, w, *, name, out_dtype, transposed=False, after=None, rope=None):
    M, K = x.shape
    N = w.shape[0] if transposed else w.shape[1]
    tm = min(512, M)
    order = [] if after is None else [after]
    tables = [] if rope is None else list(rope)
    rot_spec = [] if rope is None else [pl.BlockSpec((tm, MAIN_W), lambda i: (i, 0))]
    rot_shape = [] if rope is None else [_sds((M, MAIN_W), f32)]

    def body(x_ref, g_ref, w_ref, *refs):
        z_ref, h_ref = refs[len(tables) + len(order):][:2]
        xv = x_ref[...]
        h = (xv * _rms_r(xv) * g_ref[...]).astype(bf16)
        h_ref[...] = h
        z = lax.dot_general(h, w_ref[...], NT if transposed else NN, preferred_element_type=f32)
        z_ref[...] = z.astype(z_ref.dtype)
        if tables:
            c = jnp.tile(refs[0][...], (1, MAIN_W // 128))
            s = jnp.tile(refs[1][...], (1, MAIN_W // 128))
            zr = z[:, :MAIN_W]
            refs[-1][...] = zr * c + _swap_halves(zr) * s

    tab = pl.BlockSpec((tm, 128), lambda i: (i, 0))
    return pl.pallas_call(
        body, name=name, grid=(M // tm,),
        in_specs=[pl.BlockSpec((tm, K), lambda i: (i, 0)),
                  pl.BlockSpec((1, K), lambda i: (0, 0)),
                  pl.BlockSpec(w.shape, lambda i: (0, 0))] + [tab] * len(tables)
        + [pl.BlockSpec(memory_space=pl.ANY)] * len(order),
        out_specs=[pl.BlockSpec((tm, N), lambda i: (i, 0)), pl.BlockSpec((tm, K), lambda i: (i, 0))] + rot_spec,
        out_shape=[_sds((M, N), out_dtype), _sds((M, K), bf16)] + rot_shape,
        compiler_params=_params(("parallel",)),
    )(x, gain, w, *tables, *order)


def matmul_rms_res(a, w, gain, res, *, name):
    M, K = a.shape
    N = w.shape[1]
    tm = min(512, M)

    def body(a_ref, w_ref, g_ref, r_ref, y_ref, x_ref):
        y = jnp.dot(a_ref[...], w_ref[...], preferred_element_type=f32)
        y_ref[...] = y.astype(bf16)
        x_ref[...] = r_ref[...] + y * _rms_r(y) * g_ref[...]

    row = pl.BlockSpec((tm, N), lambda i: (i, 0))
    return pl.pallas_call(
        body, name=name, grid=(M // tm,),
        in_specs=[pl.BlockSpec((tm, K), lambda i: (i, 0)),
                  pl.BlockSpec((K, N), lambda i: (0, 0)),
                  pl.BlockSpec((1, N), lambda i: (0, 0)),
                  row],
        out_specs=[row, row],
        out_shape=[_sds((M, N), bf16), _sds((M, N), f32)],
        compiler_params=_params(("parallel",)),
    )(a, w, gain, res)


def matmul(a, b, dims, *, name, out_dtype):
    if dims is TN:
        K, M = a.shape
        tm = _tile(M, (512, 256, 128))
        a_spec = pl.BlockSpec((K, tm), lambda i: (0, i))
    else:
        M, K = a.shape
        tm = _tile(M, (1024, 512, 256, 128))
        a_spec = pl.BlockSpec((tm, K), lambda i: (i, 0))
    N = b.shape[0] if dims is NT else b.shape[1]

    def body(a_ref, b_ref, o_ref):
        o_ref[...] = lax.dot_general(a_ref[...].astype(bf16), b_ref[...].astype(bf16), dims,
                                     preferred_element_type=f32).astype(o_ref.dtype)

    return pl.pallas_call(
        body, name=name, grid=(M // tm,),
        in_specs=[a_spec, pl.BlockSpec(b.shape, lambda i: (0, 0))],
        out_specs=pl.BlockSpec((tm, N), lambda i: (i, 0)),
        out_shape=_sds((M, N), out_dtype),
        compiler_params=_params(("parallel",)),
    )(a, b)


def rms_gate_up(x, gain, wt, *, name):
    M, K = x.shape
    tm = min(2048, M)
    tn = _tile(D_FF, (256, 128))
    nj = D_FF // tn

    def body(x_ref, gn_ref, wg_ref, wu_ref, g_ref, u_ref, a_ref, h_ref):
        @pl.when(pl.program_id(1) == 0)
        def _():
            xv = x_ref[...]
            h_ref[...] = (xv * _rms_r(xv) * gn_ref[...]).astype(bf16)

        h = h_ref[...]
        g = lax.dot_general(h, wg_ref[...], NT, preferred_element_type=f32).astype(bf16)
        u = lax.dot_general(h, wu_ref[...], NT, preferred_element_type=f32).astype(bf16)
        g_ref[...] = g
        u_ref[...] = u
        a_ref[...] = g * (1.0 / (1.0 + jnp.exp(-g))) * u

    col = pl.BlockSpec((tm, tn), lambda i, j: (i, j))
    return pl.pallas_call(
        body, name=name, grid=(M // tm, nj),
        in_specs=[pl.BlockSpec((tm, K), lambda i, j: (i, 0)),
                  pl.BlockSpec((1, K), lambda i, j: (0, 0)),
                  pl.BlockSpec((tn, K), lambda i, j: (j, 0)),
                  pl.BlockSpec((tn, K), lambda i, j: (j + nj, 0))],
        out_specs=[col, col, col, pl.BlockSpec((tm, K), lambda i, j: (i, 0))],
        out_shape=[_sds((M, D_FF), bf16)] * 3 + [_sds((M, K), bf16)],
        compiler_params=_params(("parallel", "arbitrary")),
    )(x, gain, wt, wt)


def _rms_bwd_math(yv, gain, dn):
    r = _rms_r(yv)
    q = dn * gain
    dy = r * q - yv * (r * r * r) * jnp.mean(q * yv, axis=-1, keepdims=True)
    return dy, jnp.sum(dn * yv * r, axis=0, keepdims=True)


def _accumulate(ref, val):
    @pl.when(pl.program_id(0) == 0)
    def _():
        ref[...] = jnp.zeros_like(ref)

    ref[...] += val


def down_bwd(y, gain, dn, w_down, g, u, *, name, after=None):
    M, K = y.shape
    tm = min(512, M)
    order = [] if after is None else [after]

    def body(y_ref, gn_ref, dn_ref, w_ref, g_ref, u_ref, *refs):
        dy_ref, o_ref, dg_ref = refs[len(order):]
        dy, dgain = _rms_bwd_math(y_ref[...].astype(f32), gn_ref[...], dn_ref[...])
        dy = dy.astype(bf16)
        dy_ref[...] = dy
        _accumulate(dg_ref, dgain)
        da = lax.dot_general(dy, w_ref[...], NT, preferred_element_type=f32).astype(bf16)
        g = g_ref[...]
        s = 1.0 / (1.0 + jnp.exp(-g))
        o_ref[:, :D_FF] = da * u_ref[...] * s * (1.0 + g * (1.0 - s))
        o_ref[:, D_FF:] = da * g * s

    row = pl.BlockSpec((tm, K), lambda i: (i, 0))
    vec = pl.BlockSpec((1, K), lambda i: (0, 0))
    wide = pl.BlockSpec((tm, D_FF), lambda i: (i, 0))
    return pl.pallas_call(
        body, name=name, grid=(M // tm,),
        in_specs=[row, vec, row, pl.BlockSpec((D_FF, K), lambda i: (0, 0)), wide, wide]
        + [pl.BlockSpec(memory_space=pl.ANY)] * len(order),
        out_specs=[row, pl.BlockSpec((tm, 2 * D_FF), lambda i: (i, 0)), vec],
        out_shape=[_sds((M, K), bf16), _sds((M, 2 * D_FF), bf16), _sds((1, K), f32)],
        compiler_params=_params(("arbitrary",)),
    )(y, gain, dn, w_down, g, u, *order)


def rms_bwd_matmul(y, gain, dn, w, dims, *, name, after=None):
    M, K = y.shape
    N = w.shape[0] if dims is NT else w.shape[1]
    tm = min(1024, M)
    order = [] if after is None else [after]

    def body(y_ref, gn_ref, dn_ref, w_ref, *refs):
        dy_ref, o_ref, dg_ref = refs[len(order):]
        dy, dgain = _rms_bwd_math(y_ref[...].astype(f32), gn_ref[...], dn_ref[...].astype(f32))
        dy = dy.astype(bf16)
        dy_ref[...] = dy
        _accumulate(dg_ref, dgain)
        o_ref[...] = lax.dot_general(dy, w_ref[...], dims, preferred_element_type=f32).astype(bf16)

    row = pl.BlockSpec((tm, K), lambda i: (i, 0))
    vec = pl.BlockSpec((1, K), lambda i: (0, 0))
    return pl.pallas_call(
        body, name=name, grid=(M // tm,),
        in_specs=[row, vec, row, pl.BlockSpec(w.shape, lambda i: (0, 0))]
        + [pl.BlockSpec(memory_space=pl.ANY)] * len(order),
        out_specs=[row, pl.BlockSpec((tm, N), lambda i: (i, 0)), vec],
        out_shape=[_sds((M, K), bf16), _sds((M, N), bf16), _sds((1, K), f32)],
        compiler_params=_params(("arbitrary",)),
    )(y, gain, dn, w, *order)


def matmul_rms_bwd(a, b, dims, y, gain, res, *, name, after=None):
    M, K = a.shape
    N = y.shape[1]
    tm = min(512, M)
    order = [] if after is None else [after]

    def body(a_ref, b_ref, y_ref, gn_ref, r_ref, *refs):
        dx_ref, dg_ref = refs[len(order):]
        dn = lax.dot_general(a_ref[...].astype(bf16), b_ref[...], dims, preferred_element_type=f32)
        dy, dgain = _rms_bwd_math(y_ref[...], gn_ref[...], dn)
        dx_ref[...] = dy + r_ref[...]
        _accumulate(dg_ref, dgain)

    row = pl.BlockSpec((tm, N), lambda i: (i, 0))
    vec = pl.BlockSpec((1, N), lambda i: (0, 0))
    return pl.pallas_call(
        body, name=name, grid=(M // tm,),
        in_specs=[pl.BlockSpec((tm, K), lambda i: (i, 0)), pl.BlockSpec(b.shape, lambda i: (0, 0)), row, vec, row]
        + [pl.BlockSpec(memory_space=pl.ANY)] * len(order),
        out_specs=[row, vec],
        out_shape=[_sds((M, N), f32), _sds((1, N), f32)],
        compiler_params=_params(("arbitrary",)),
    )(a, b, y, gain, res, *order)


def rms_bwd(y, gain, dn, res, *, name, out_dtype, after=None):
    M, N = y.shape
    tm = min(512, M)
    has_res = res is not None
    order = [] if after is None else [after]

    def body(*refs):
        y_ref, g_ref, dn_ref = refs[:3]
        r_ref = refs[3] if has_res else None
        dy_ref, dg_ref = refs[-2:]
        dy, dgain = _rms_bwd_math(y_ref[...].astype(f32), g_ref[...], dn_ref[...].astype(f32))
        if has_res:
            dy = dy + r_ref[...]
        dy_ref[...] = dy.astype(dy_ref.dtype)
        _accumulate(dg_ref, dgain)

    row = pl.BlockSpec((tm, N), lambda i: (i, 0))
    vec = pl.BlockSpec((1, N), lambda i: (0, 0))
    args = [y, gain, dn] + ([res] if has_res else []) + order
    return pl.pallas_call(
        body, name=name, grid=(M // tm,),
        in_specs=[row, vec, row] + ([row] if has_res else []) + [pl.BlockSpec(memory_space=pl.ANY)] * len(order),
        out_specs=[row, vec],
        out_shape=[_sds((M, N), out_dtype), _sds((1, N), f32)],
        compiler_params=_params(("arbitrary",)),
    )(*args)


def loss_head(x, target, *, name):
    M, N = x.shape
    tm = min(512, M)

    def body(x_ref, t_ref, dx_ref, l_ref):
        e = x_ref[...] - t_ref[...]
        dx_ref[...] = e * (1.0 / N)

        @pl.when(pl.program_id(0) == 0)
        def _():
            l_ref[...] = jnp.zeros_like(l_ref)

        l_ref[...] += jnp.sum(jnp.sum(e * e, axis=0, keepdims=True), axis=1, keepdims=True)

    row = pl.BlockSpec((tm, N), lambda i: (i, 0))
    return pl.pallas_call(
        body, name=name, grid=(M // tm,),
        in_specs=[row, row],
        out_specs=[row, pl.BlockSpec((8, 128), lambda i: (0, 0))],
        out_shape=[_sds((M, N), f32), _sds((8, 128), f32)],
        compiler_params=_params(("arbitrary",)),
    )(x, target)


def _pool_select(a1, a2, a3, a4):
    col = lax.broadcasted_iota(jnp.int32, (1, MAIN_W), 1) // POOL_GROUP
    return jnp.where(col == 0, a1, jnp.where(col == 1, a2, jnp.where(col == 2, a3, a4)))


def _pool_count(t):
    col = lax.broadcasted_iota(jnp.int32, (1, MAIN_W), 1) // POOL_GROUP
    win = jnp.where(col == 0, 2, jnp.where(col == 1, 4, jnp.where(col == 2, 8, 16)))
    return jnp.minimum(t + 1, win).astype(f32)


def pool_fwd(z, wbd, scale, *, name):
    M = z.shape[0]
    tm = 512
    nper = SEQ // tm
    hb = tm // POOL_HALO

    def body(zc_ref, zh_ref, w_ref, s_ref, p_ref, y_ref):
        i = pl.program_id(0)
        seq_blk = i % nper
        halo = jnp.where(seq_blk == 0, 0.0, zh_ref[...].astype(f32))
        u = zc_ref[...].astype(f32)
        ext = jnp.concatenate([halo, u], axis=0)
        a1 = ext + pltpu.roll(ext, 1, 0)
        a2 = a1 + pltpu.roll(a1, 2, 0)
        a3 = a2 + pltpu.roll(a2, 4, 0)
        a4 = a3 + pltpu.roll(a3, 8, 0)
        sums = _pool_select(a1, a2, a3, a4)[POOL_HALO:]
        t = seq_blk * tm + lax.broadcasted_iota(jnp.int32, (tm, 1), 0)
        p = (sums / _pool_count(t) - u).astype(bf16)
        p_ref[...] = p
        y_ref[...] = (jnp.dot(p, w_ref[...], preferred_element_type=f32) * s_ref[...]).astype(bf16)

    return pl.pallas_call(
        body, name=name, grid=(M // tm,),
        in_specs=[pl.BlockSpec((tm, MAIN_W), lambda i: (i, 0)),
                  pl.BlockSpec((POOL_HALO, MAIN_W), lambda i: (jnp.maximum(i * hb - 1, 0), 0)),
                  pl.BlockSpec((MAIN_W, MAIN_W), lambda i: (0, 0)),
                  pl.BlockSpec((1, MAIN_W), lambda i: (0, 0))],
        out_specs=[pl.BlockSpec((tm, MAIN_W), lambda i: (i, 0)),
                   pl.BlockSpec((tm, MAIN_W), lambda i: (i, 0))],
        out_shape=[_sds((M, MAIN_W), bf16), _sds((M, D_MODEL), bf16)],
        compiler_params=_params(("parallel",)),
    )(z, z, wbd, scale)


def pool_bwd(dyc, p, wbd, scale, *, name):
    M = p.shape[0]
    tm = 512
    nper = SEQ // tm
    hb = tm // POOL_HALO
    last_hb = M // POOL_HALO - 1

    def body(dy_ref, dyh_ref, p_ref, w_ref, s_ref, dz_ref, dw_ref, ds_ref):
        i = pl.program_id(0)
        seq_blk = i % nper
        dy = dy_ref[...].astype(f32)
        pv = p_ref[...]
        w = w_ref[...]
        sc = s_ref[...]

        @pl.when(i == 0)
        def _():
            dw_ref[...] = jnp.zeros_like(dw_ref)
            ds_ref[...] = jnp.zeros_like(ds_ref)

        v = jnp.dot(pv, w, preferred_element_type=f32)
        ds_ref[...] += jnp.sum(dy * v, axis=0, keepdims=True)
        dv = (dy * sc).astype(bf16)
        dw_ref[...] += lax.dot_general(pv, dv, TN, preferred_element_type=f32)
        dp = lax.dot_general(dv, w, NT, preferred_element_type=f32)
        dvh = jnp.where(seq_blk == nper - 1, 0.0, dyh_ref[...].astype(f32) * sc).astype(bf16)
        dph = lax.dot_general(dvh, w, NT, preferred_element_type=f32)
        ext = jnp.concatenate([dp, dph], axis=0)
        n = tm + POOL_HALO
        t = seq_blk * tm + lax.broadcasted_iota(jnp.int32, (n, 1), 0)
        e = ext / _pool_count(t)
        b1 = e + pltpu.roll(e, n - 1, 0)
        b2 = b1 + pltpu.roll(b1, n - 2, 0)
        b3 = b2 + pltpu.roll(b2, n - 4, 0)
        b4 = b3 + pltpu.roll(b3, n - 8, 0)
        dz_ref[...] = (_pool_select(b1, b2, b3, b4)[:tm] - dp).astype(dz_ref.dtype)

    return pl.pallas_call(
        body, name=name, grid=(M // tm,),
        in_specs=[pl.BlockSpec((tm, MAIN_W), lambda i: (i, 0)),
                  pl.BlockSpec((POOL_HALO, MAIN_W), lambda i: (jnp.minimum((i + 1) * hb, last_hb), 0)),
                  pl.BlockSpec((tm, MAIN_W), lambda i: (i, 0)),
                  pl.BlockSpec((MAIN_W, MAIN_W), lambda i: (0, 0)),
                  pl.BlockSpec((1, MAIN_W), lambda i: (0, 0))],
        out_specs=[pl.BlockSpec((tm, MAIN_W), lambda i: (i, 0)),
                   pl.BlockSpec((MAIN_W, MAIN_W), lambda i: (0, 0)),
                   pl.BlockSpec((1, MAIN_W), lambda i: (0, 0))],
        out_shape=[_sds((M, D_MODEL), bf16), _sds((MAIN_W, MAIN_W), f32), _sds((1, MAIN_W), f32)],
        compiler_params=_params(("arbitrary",)),
    )(dyc, dyc, p, wbd, scale)


def _mem_heads(q, kv):
    first = _first_head()
    for pr in range(N_MEM_HEADS // 2):
        cols = slice(pr * PAIR_W, (pr + 1) * PAIR_W)
        qp = q[:, cols] * SCALE
        kp = kv[:, cols]
        vp = kv[:, MEM_W + pr * PAIR_W: MEM_W + (pr + 1) * PAIR_W]
        for hh in range(2):
            lm = first if hh == 0 else ~first
            qm = jnp.where(lm, qp, 0.0).astype(bf16)
            s = lax.dot_general(qm, kp, NT, preferred_element_type=f32)
            e = jnp.exp(s - jnp.max(s, axis=-1, keepdims=True))
            yield lm, qm, kp, vp, e, jnp.sum(e, axis=-1, keepdims=True)


def memattn_fwd(z, kvm, ycat, *, name, n_seq):
    M = z.shape[0]
    tq = 512
    nq = SEQ // tq

    def body(q_ref, kv_ref, _, o_ref):
        first = _first_head()
        outs = []
        for lm, _, _, vp, e, l in _mem_heads(q_ref[...], kv_ref[...]):
            outs.append(jnp.dot(e.astype(bf16), vp, preferred_element_type=f32) * (1.0 / l))
        pairs = [jnp.where(first, outs[2 * pr], outs[2 * pr + 1]) for pr in range(N_MEM_HEADS // 2)]
        o_ref[...] = jnp.concatenate(pairs, axis=1).astype(bf16)

    return pl.pallas_call(
        body, name=name, grid=(n_seq, nq),
        in_specs=[pl.BlockSpec((tq, MEM_W), lambda b, i: (b * nq + i, 3)),
                  pl.BlockSpec((N_MEM, 2 * MEM_W), lambda b, i: (b, 0)),
                  pl.BlockSpec(memory_space=pl.ANY)],
        out_specs=pl.BlockSpec((tq, MEM_W), lambda b, i: (b * nq + i, 3)),
        out_shape=_sds((M, D_MODEL), bf16),
        input_output_aliases={2: 0},
        compiler_params=_params(("parallel", "parallel")),
    )(z, kvm, ycat)


def memattn_bwd(z, kvm, dyc, dz, *, name, n_seq):
    M = z.shape[0]
    tq = 512
    nq = SEQ // tq

    def body(q_ref, kv_ref, dy_ref, _, dq_ref, dkv_ref):
        first = _first_head()
        dy = dy_ref[...].astype(f32)
        dqs, dks, dvs = [], [], []
        for h, (lm, qm, kp, vp, e, l) in enumerate(_mem_heads(q_ref[...], kv_ref[...])):
            pr = h // 2
            p = e * (1.0 / l)
            dym = jnp.where(lm, dy[:, pr * PAIR_W:(pr + 1) * PAIR_W], 0.0).astype(bf16)
            dp = lax.dot_general(dym, vp, NT, preferred_element_type=f32)
            ds = (p * (dp - jnp.sum(dp * p, axis=-1, keepdims=True))).astype(bf16)
            dqs.append(jnp.dot(ds, kp, preferred_element_type=f32) * SCALE)
            dk = lax.dot_general(ds, qm, TN, preferred_element_type=f32)
            dv = lax.dot_general(p.astype(bf16), dym, TN, preferred_element_type=f32)
            if h % 2 == 0:
                dks.append(dk)
                dvs.append(dv)
            else:
                dks[pr] = dks[pr] + dk
                dvs[pr] = dvs[pr] + dv
        pairs = [jnp.where(first, dqs[2 * pr], dqs[2 * pr + 1]) for pr in range(N_MEM_HEADS // 2)]
        dq_ref[...] = jnp.concatenate(pairs, axis=1).astype(bf16)

        @pl.when(pl.program_id(1) == 0)
        def _():
            dkv_ref[...] = jnp.zeros_like(dkv_ref)

        dkv_ref[...] += jnp.concatenate(dks + dvs, axis=1)

    return pl.pallas_call(
        body, name=name, grid=(n_seq, nq),
        in_specs=[pl.BlockSpec((tq, MEM_W), lambda b, i: (b * nq + i, 3)),
                  pl.BlockSpec((N_MEM, 2 * MEM_W), lambda b, i: (b, 0)),
                  pl.BlockSpec((tq, MEM_W), lambda b, i: (b * nq + i, 3)),
                  pl.BlockSpec(memory_space=pl.ANY)],
        out_specs=[pl.BlockSpec((tq, MEM_W), lambda b, i: (b * nq + i, 3)),
                   pl.BlockSpec((N_MEM, 2 * MEM_W), lambda b, i: (b, 0))],
        out_shape=[_sds((M, D_MODEL), bf16), _sds((n_seq * N_MEM, 2 * MEM_W), f32)],
        input_output_aliases={3: 0},
        compiler_params=_params(("parallel", "arbitrary")),
    )(z, kvm, dyc, dz)


def rope_tables(pos, *, name):
    M = pos.shape[0]
    tm = min(1024, M)
    half = HEAD_DIM // 2
    inv = ROPE_THETA ** (-np.arange(half, dtype=np.float64) / half)
    inv128 = jnp.asarray(np.tile(inv, 4)[None, :], f32)
    sign128 = jnp.asarray(np.tile(np.concatenate([-np.ones(half), np.ones(half)]), 2)[None, :], f32)

    def body(p_ref, f_ref, s_ref, cos_ref, sin_ref):
        ang = p_ref[...] * f_ref[...]
        cos_ref[...] = jnp.cos(ang)
        sin_ref[...] = jnp.sin(ang) * s_ref[...]

    return pl.pallas_call(
        body, name=name, grid=(M // tm,),
        in_specs=[pl.BlockSpec((tm, 1), lambda i: (i, 0)),
                  pl.BlockSpec((1, 128), lambda i: (0, 0)),
                  pl.BlockSpec((1, 128), lambda i: (0, 0))],
        out_specs=[pl.BlockSpec((tm, 128), lambda i: (i, 0)),
                   pl.BlockSpec((tm, 128), lambda i: (i, 0))],
        out_shape=[_sds((M, 128), f32), _sds((M, 128), f32)],
        compiler_params=_params(("parallel",)),
    )(pos, inv128, sign128)


def _swap_halves(x):
    w = x.shape[1]
    first = (lax.broadcasted_iota(jnp.int32, (1, w), 1) % HEAD_DIM) < (HEAD_DIM // 2)
    return jnp.where(first, pltpu.roll(x, w - HEAD_DIM // 2, 1), pltpu.roll(x, HEAD_DIM // 2, 1))


def group_sum(groups, cos, sin, *, name, rotate, width, col_block=0, into=None):
    M = groups[0][0].shape[0]
    tm = min(512, M)
    counts = [len(g) for g in groups]
    flat = [a for g in groups for a in g]
    extra = [] if into is None else [into]

    def body(*refs):
        part_refs = refs[:len(flat)]
        c_ref, s_ref = refs[len(flat):len(flat) + 2]
        o_ref = refs[-1]
        cols, k = [], 0
        for n in counts:
            acc = part_refs[k][...]
            for r in part_refs[k + 1:k + n]:
                acc = acc + r[...]
            cols.append(acc)
            k += n
        d = jnp.concatenate(cols, axis=1)
        if rotate:
            c = jnp.tile(c_ref[...], (1, MAIN_W // 128))
            s = jnp.tile(s_ref[...], (1, MAIN_W // 128))
            d = d * c - _swap_halves(d) * s
        o_ref[...] = d.astype(bf16)

    part = pl.BlockSpec((tm, GROUP_W), lambda i: (i, 0))
    tab = pl.BlockSpec((tm, 128), lambda i: (i, 0))
    return pl.pallas_call(
        body, name=name, grid=(M // tm,),
        in_specs=[part] * len(flat) + [tab, tab] + [pl.BlockSpec(memory_space=pl.ANY)] * len(extra),
        out_specs=pl.BlockSpec((tm, MAIN_W), lambda i: (i, col_block)),
        out_shape=_sds((M, width), bf16),
        input_output_aliases={len(flat) + 2: 0} if extra else {},
        compiler_params=_params(("parallel",)),
    )(*flat, cos, sin, *extra)


PAIR_W = 2 * HEAD_DIM
MIN_BLOCKS = 8


def _dil_geometry(dil):
    nsub = max(dil, MIN_BLOCKS)
    tb = BAND * nsub
    return nsub, tb, SEQ // tb


REGROUP = 4


class _Regrouped:
    def __init__(self, ref):
        self.ref = ref
        self.shape = ref.shape

    def fill(self, src):
        q = self.shape[0] // REGROUP
        for r0 in range(REGROUP):
            self.ref[r0 * q:(r0 + 1) * q, :] = src[pl.ds(r0, q, stride=REGROUP), :]

    def drain(self, dst):
        q = self.shape[0] // REGROUP
        for r0 in range(REGROUP):
            dst[pl.ds(r0, q, stride=REGROUP), :] = self.ref[r0 * q:(r0 + 1) * q, :]

    def rows(self, sub, dil):
        nl, r = divmod(sub, dil)
        start = (r % REGROUP) * (self.shape[0] // REGROUP) + r // REGROUP + nl * BAND * (dil // REGROUP)
        return pl.ds(start, BAND, stride=dil // REGROUP)


def _regroups(dil):
    return dil % (4 * REGROUP) == 0


def _rows(ref, sub, dil):
    if isinstance(ref, _Regrouped):
        return ref.ref[ref.rows(sub, dil), :]
    if dil == 1:
        return ref[sub * BAND:(sub + 1) * BAND, :]
    nl, r = divmod(sub, dil)
    return ref[pl.ds(nl * BAND * dil + r, BAND, stride=dil), :]


def _store_rows(ref, sub, dil, val):
    if isinstance(ref, _Regrouped):
        ref.ref[ref.rows(sub, dil), :] = val
    elif dil == 1:
        ref[sub * BAND:(sub + 1) * BAND, :] = val
    else:
        nl, r = divmod(sub, dil)
        ref[pl.ds(nl * BAND * dil + r, BAND, stride=dil), :] = val


def _keys(prev_ref, own_ref, sub, dil):
    nsub = own_ref.shape[0] // BAND
    if sub >= dil:
        prev = _rows(own_ref, sub - dil, dil)
    elif prev_ref is None:
        return _rows(own_ref, sub, dil)
    else:
        prev = _rows(prev_ref, nsub - dil + sub, dil)
    return jnp.concatenate([prev, _rows(own_ref, sub, dil)], axis=0)


def _band_mask(nkeys, has_prev):
    i = lax.broadcasted_iota(jnp.int32, (BAND, nkeys), 0)
    j = lax.broadcasted_iota(jnp.int32, (BAND, nkeys), 1)
    if nkeys == BAND:
        return j <= i
    return (j >= i) & (j <= i + BAND) & (has_prev | (j >= BAND))


def _first_head():
    return lax.broadcasted_iota(jnp.int32, (1, PAIR_W), 1) < HEAD_DIM


def _col(x, hh):
    return x[:, hh * HEAD_DIM:hh * HEAD_DIM + 1]


def _pair_spec(tb, nblk, col0, which):
    def idx(b, p, i):
        if which < 0:
            i = jnp.maximum(i - 1, 0)
        elif which > 0:
            i = jnp.minimum(i + 1, nblk - 1)
        return (b * nblk + i, col0 + p)
    return pl.BlockSpec((tb, PAIR_W), idx)


def dil_fwd(q, k, kv, g, dil, *, name, n_seq):
    M = q.shape[0]
    nsub, tb, nblk = _dil_geometry(dil)
    with_prev = nblk > 1

    regroup = _regroups(dil)
    assert not (regroup and with_prev)

    def body(*refs):
        if with_prev:
            q_ref, ko_ref, vo_ref, kp_ref, vp_ref, o_ref, l_ref = refs
        else:
            (q_ref, ko_ref, vo_ref, o_ref, l_ref), kp_ref, vp_ref = refs[:5], None, None
        outs_to = ()
        if regroup:
            copies = [_Regrouped(s) for s in refs[5:]]
            for c, src in zip(copies, (q_ref, ko_ref, vo_ref)):
                c.fill(src)
            outs_to = ((copies[3], o_ref), (copies[4], l_ref))
            q_ref, ko_ref, vo_ref, o_ref, l_ref = copies
        first = _first_head()
        blk = pl.program_id(2)
        for sub in range(nsub):
            qs = _rows(q_ref, sub, dil) * SCALE
            kc = _keys(kp_ref, ko_ref, sub, dil).astype(bf16)
            vc = _keys(vp_ref, vo_ref, sub, dil).astype(bf16)
            has_prev = True if sub >= dil else blk > 0
            mask = _band_mask(kc.shape[0], has_prev)
            outs, lses = [], []
            for hh in range(2):
                qm = jnp.where(first if hh == 0 else ~first, qs, 0.0).astype(bf16)
                s = jnp.where(mask, lax.dot_general(qm, kc, NT, preferred_element_type=f32), NEG)
                m = jnp.max(s, axis=-1, keepdims=True)
                e = jnp.exp(s - m)
                l = jnp.sum(e, axis=-1, keepdims=True)
                outs.append(jnp.dot(e.astype(bf16), vc, preferred_element_type=f32) * (1.0 / l))
                lses.append(jnp.broadcast_to(m + jnp.log(l), (BAND, PAIR_W)))
            _store_rows(o_ref, sub, dil, jnp.where(first, outs[0], outs[1]))
            _store_rows(l_ref, sub, dil, jnp.where(first, lses[0], lses[1]))
        for c, dst in outs_to:
            c.drain(dst)

    ins = [(q, 2 * g, 0), (k, 2 * g, 0), (kv, 6 + 2 * g, 0)]
    if with_prev:
        ins += [(k, 2 * g, -1), (kv, 6 + 2 * g, -1)]
    out = _pair_spec(tb, nblk, 0, 0)
    return pl.pallas_call(
        body, name=name, grid=(n_seq, 2, nblk),
        in_specs=[_pair_spec(tb, nblk, c, w) for _, c, w in ins],
        out_specs=[out, out],
        out_shape=[_sds((M, GROUP_W), f32)] * 2,
        scratch_shapes=[pltpu.VMEM((tb, PAIR_W), f32)] * (5 if regroup else 0),
        compiler_params=_params(("parallel", "parallel", "arbitrary")),
    )(*[a for a, _, _ in ins])


def combine_fwd(os_, lses, *, name):
    M = os_[0].shape[0]
    tm = min(512, M)

    def body(o0, o1, o2, l0, l1, l2, y_ref):
        ls = [l0[...], l1[...], l2[...]]
        m = jnp.maximum(jnp.maximum(ls[0], ls[1]), ls[2])
        es = [jnp.exp(l - m) for l in ls]
        inv = 1.0 / (es[0] + es[1] + es[2])
        y_ref[...] = jnp.concatenate([o[...] * e * inv for o, e in zip((o0, o1, o2), es)], axis=1).astype(bf16)

    part = pl.BlockSpec((tm, GROUP_W), lambda i: (i, 0))
    return pl.pallas_call(
        body, name=name, grid=(M // tm,),
        in_specs=[part] * 6,
        out_specs=pl.BlockSpec((tm, MAIN_W), lambda i: (i, 0)),
        out_shape=_sds((M, D_MODEL), bf16),
        compiler_params=_params(("parallel",)),
    )(*os_, *lses)


def combine_bwd(dyc, os_, lses, *, name):
    M = os_[0].shape[0]
    tm = min(512, M)

    def body(dy_ref, o0, o1, o2, l0, l1, l2, d0, d1, d2, c0, c1, c2):
        r = lax.broadcasted_iota(jnp.int32, (GROUP_W, GROUP_W), 0) // HEAD_DIM
        c = lax.broadcasted_iota(jnp.int32, (GROUP_W, GROUP_W), 1) // HEAD_DIM
        ones = (r == c).astype(f32)
        dy = dy_ref[...].astype(f32)
        ls = [l0[...], l1[...], l2[...]]
        m = jnp.maximum(jnp.maximum(ls[0], ls[1]), ls[2])
        es = [jnp.exp(l - m) for l in ls]
        inv = 1.0 / (es[0] + es[1] + es[2])
        total = 0.0
        alphas = []
        for g, (o, e, d_ref) in enumerate(zip((o0, o1, o2), es, (d0, d1, d2))):
            a = e * inv
            dyg = dy[:, g * GROUP_W:(g + 1) * GROUP_W]
            d_ref[...] = dyg * a
            dsum = jnp.dot(dyg * o[...], ones, precision=lax.Precision.HIGHEST, preferred_element_type=f32)
            total = total + a * dsum
            alphas.append(a)
        for a, c_ref in zip(alphas, (c0, c1, c2)):
            c_ref[...] = -a * total

    part = pl.BlockSpec((tm, GROUP_W), lambda i: (i, 0))
    outs = pl.pallas_call(
        body, name=name, grid=(M // tm,),
        in_specs=[pl.BlockSpec((tm, MAIN_W), lambda i: (i, 0))] + [part] * 6,
        out_specs=[part] * 6,
        out_shape=[_sds((M, GROUP_W), f32)] * 6,
        compiler_params=_params(("parallel",)),
    )(dyc, *os_, *lses)
    return outs[:3], outs[3:]


def dil_bwd(q, k, kv, do, cc, lse, g, dil, *, name, n_seq):
    M = q.shape[0]
    nsub = SEQ // BAND
    per_res = nsub // dil

    regroup = _regroups(dil)

    def body(q_ref, k_ref, v_ref, do_ref, c_ref, l_ref, dq_ref, dk_ref, dv_ref, *scratch):
        outs_to = ()
        if regroup:
            copies = [_Regrouped(s) for s in scratch]
            for c, src in zip(copies, (q_ref, k_ref, v_ref, do_ref, c_ref, l_ref)):
                c.fill(src)
            outs_to = tuple(zip(copies[6:], (dq_ref, dk_ref, dv_ref)))
            q_ref, k_ref, v_ref, do_ref, c_ref, l_ref, dq_ref, dk_ref, dv_ref = copies
        first = _first_head()
        for r in range(dil):
            carry = None
            for nl in range(per_res):
                sub = nl * dil + r
                qs = _rows(q_ref, sub, dil) * SCALE
                dos = _rows(do_ref, sub, dil)
                cs = _rows(c_ref, sub, dil)
                ls = _rows(l_ref, sub, dil)
                kc = _keys(None, k_ref, sub, dil).astype(bf16)
                vc = _keys(None, v_ref, sub, dil).astype(bf16)
                nkeys = kc.shape[0]
                mask = _band_mask(nkeys, True)
                dqs = []
                dkc = jnp.zeros((nkeys, PAIR_W), f32)
                dvc = jnp.zeros((nkeys, PAIR_W), f32)
                for hh in range(2):
                    lm = first if hh == 0 else ~first
                    qm = jnp.where(lm, qs, 0.0).astype(bf16)
                    dom = jnp.where(lm, dos, 0.0).astype(bf16)
                    s = jnp.where(mask, lax.dot_general(qm, kc, NT, preferred_element_type=f32), NEG)
                    p = jnp.exp(s - _col(ls, hh))
                    dp = lax.dot_general(dom, vc, NT, preferred_element_type=f32)
                    ds = (p * (dp + _col(cs, hh))).astype(bf16)
                    dqs.append(jnp.dot(ds, kc, preferred_element_type=f32) * SCALE)
                    dkc = dkc + lax.dot_general(ds, qm, TN, preferred_element_type=f32)
                    dvc = dvc + lax.dot_general(p.astype(bf16), dom, TN, preferred_element_type=f32)
                _store_rows(dq_ref, sub, dil, jnp.where(first, dqs[0], dqs[1]))
                if nkeys == 2 * BAND:
                    _store_rows(dk_ref, sub - dil, dil, carry[0] + dkc[:BAND])
                    _store_rows(dv_ref, sub - dil, dil, carry[1] + dvc[:BAND])
                    carry = (dkc[BAND:], dvc[BAND:])
                else:
                    carry = (dkc, dvc)
            _store_rows(dk_ref, (per_res - 1) * dil + r, dil, carry[0])
            _store_rows(dv_ref, (per_res - 1) * dil + r, dil, carry[1])
        for c, dst in outs_to:
            c.drain(dst)

    def spec(col0):
        return pl.BlockSpec((SEQ, PAIR_W), lambda b, p: (b, col0 + p))

    out = spec(0)
    return pl.pallas_call(
        body, name=name, grid=(n_seq, 2),
        in_specs=[spec(2 * g), spec(2 * g), spec(6 + 2 * g), spec(0), spec(0), spec(0)],
        out_specs=[out, out, out],
        out_shape=[_sds((M, GROUP_W), f32)] * 3,
        scratch_shapes=[pltpu.VMEM((SEQ, PAIR_W), f32)] * (9 if regroup else 0),
        compiler_params=_params(("parallel", "parallel")),
    )(q, k, kv, do, cc, lse)


def _blockdiag(wp):
    out = jnp.zeros((MAIN_W, MAIN_W), wp.dtype)
    for gi in range(len(POOL_WINDOWS)):
        sl = slice(gi * POOL_GROUP, (gi + 1) * POOL_GROUP)
        out = out.at[sl, sl].set(wp[gi])
    return out


def _unblockdiag(w):
    return jnp.stack([w[gi * POOL_GROUP:(gi + 1) * POOL_GROUP, gi * POOL_GROUP:(gi + 1) * POOL_GROUP]
                      for gi in range(len(POOL_WINDOWS))])


def local_step(x, mem, positions, target, P, layer_weights, kv_weight, emit_grads):
    n_seq = x.shape[0]
    M = n_seq * SEQ
    xs = x.reshape(M, D_MODEL)
    mems = mem.reshape(n_seq * N_MEM, D_MODEL)
    pos = positions.reshape(M, 1).astype(f32)
    cos, sin = rope_tables(pos, name="rope_tables")
    gains = P["norm_gains"]

    def gain(l, k):
        return gains[l, k].reshape(1, D_MODEL)

    saved = []
    kvs = None
    for l in range(DEPTH):
        W, started = layer_weights(l, "mix", xs)
        sv = {"x": xs, "W": W}
        z, h1, *qrot = rms_matmul(xs, gain(l, 0), W["w_in"], name=f"l{l}_in", out_dtype=bf16, after=started,
                                  rope=None if l < N_A_LAYERS else (cos, sin))
        kvm, mn = rms_matmul(mems, P["mem_norm"][l].reshape(1, D_MODEL), W["w_mem_kv"],
                             name=f"l{l}_memkv", out_dtype=bf16)
        sv.update(z=z, h1=h1, kvm=kvm, mn=mn)
        if l < N_A_LAYERS:
            wbd = _blockdiag(P["w_pool"][l].astype(bf16))
            psc = P["pool_scale"][l].reshape(1, MAIN_W)
            p, y_main = pool_fwd(z, wbd, psc, name=f"l{l}_pool")
            sv.update(p=p, wbd=wbd, psc=psc)
        else:
            (qrot,) = qrot
            os_, lses = [], []
            for g, (_, dil) in enumerate(DIL_PATTERNS):
                o, lse = dil_fwd(qrot, kvs["krot"], kvs["kv"], g, dil, name=f"l{l}_dil{g}", n_seq=n_seq)
                os_.append(o)
                lses.append(lse)
            y_main = combine_fwd(os_, lses, name=f"l{l}_comb")
            sv.update(qrot=qrot, os=os_, lses=lses)
        ycat = memattn_fwd(z, kvm, y_main, name=f"l{l}_memattn", n_seq=n_seq)
        y, x1 = matmul_rms_res(ycat, W["w_out"], gain(l, 1), xs, name=f"l{l}_out")
        W.update(layer_weights(l, "gu", x1)[0])
        fg, fu, a, h2 = rms_gate_up(x1, gain(l, 2), W["w_gate_up"], name=f"l{l}_gu")
        W.update(layer_weights(l, "down", a)[0])
        y2, x2 = matmul_rms_res(a, W["w_down"], gain(l, 3), x1, name=f"l{l}_down")
        sv.update(ycat=ycat, y=y, x1=x1, fg=fg, fu=fu, h2=h2, a=a, y2=y2)
        saved.append(sv)
        xs = x2
        if l == N_A_LAYERS - 1:
            w_kv = kv_weight(xs)
            kv, hkv, krot = rms_matmul(xs, P["kv_norm"].reshape(1, D_MODEL), w_kv, name="kv_proj", out_dtype=f32,
                                       transposed=True, rope=(cos, sin))
            kvs = {"kv": kv, "hkv": hkv, "krot": krot, "x": xs, "w_kv": w_kv}

    dx, sq = loss_head(xs, target.reshape(M, D_MODEL), name="loss_head")

    G = {"mem_norm": [None] * DEPTH, "norm_gains": [[None] * 4 for _ in range(DEPTH)],
         "pool_scale": [None] * N_A_LAYERS}
    dk_parts = [[] for _ in range(N_GROUPS)]
    dv_parts = [[] for _ in range(N_GROUPS)]
    emitted = None

    for l in reversed(range(DEPTH)):
        sv = saved[l]
        W = sv["W"]
        gw = {}
        dy2, dgu, G["norm_gains"][l][3] = down_bwd(sv["y2"], gain(l, 3), dx, W["w_down"], sv["fg"], sv["fu"],
                                                   name=f"l{l}_b_dgu", after=emitted)
        gw["w_down"] = matmul(sv["a"], dy2, TN, name=f"l{l}_b_wd", out_dtype=bf16)
        gw["w_gate_up"] = matmul(dgu, sv["h2"], TN, name=f"l{l}_b_wgu", out_dtype=bf16)
        emitted = emit_grads(l, "ffn", gw)
        dx1, G["norm_gains"][l][2] = matmul_rms_bwd(dgu, W["w_gate_up"], NN, sv["x1"], gain(l, 2), dx,
                                                    name=f"l{l}_b_dh2", after=emitted)
        gw = {}
        dy, dycat, G["norm_gains"][l][1] = rms_bwd_matmul(sv["y"], gain(l, 1), dx1, W["w_out"], NT,
                                                          name=f"l{l}_b_dycat", after=emitted)
        gw["w_out"] = matmul(sv["ycat"], dy, TN, name=f"l{l}_b_wout", out_dtype=bf16)
        if l < N_A_LAYERS:
            dz, dwbd, dps = pool_bwd(dycat, sv["p"], sv["wbd"], sv["psc"], name=f"l{l}_b_pool")
            gw["w_pool"] = _unblockdiag(dwbd).reshape(MAIN_W, POOL_GROUP).astype(bf16)
            G["pool_scale"][l] = dps.reshape(MAIN_W)
        else:
            dos, ccs = combine_bwd(dycat, sv["os"], sv["lses"], name=f"l{l}_b_comb")
            dqs = []
            for g, (_, dil) in enumerate(DIL_PATTERNS):
                args = (sv["qrot"], kvs["krot"], kvs["kv"], dos[g], ccs[g], sv["lses"][g], g, dil)
                dq, dk, dv = dil_bwd(*args, name=f"l{l}_b_dil{g}", n_seq=n_seq)
                dqs.append([dq])
                dk_parts[g].append(dk)
                dv_parts[g].append(dv)
            dz = group_sum(dqs, cos, sin, name=f"l{l}_b_ropeq", rotate=True, width=D_MODEL)
        dz, dkvm = memattn_bwd(sv["z"], sv["kvm"], dycat, dz, name=f"l{l}_b_memattn", n_seq=n_seq)
        gw["w_mem_kv"] = matmul(sv["mn"], dkvm, TN, name=f"l{l}_b_wmkv", out_dtype=bf16)
        _, G["mem_norm"][l] = matmul_rms_bwd(dkvm, W["w_mem_kv"], NT, mems, P["mem_norm"][l].reshape(1, D_MODEL),
                                             mems, name=f"l{l}_b_dmn")
        gw["w_in"] = matmul(sv["h1"], dz, TN, name=f"l{l}_b_win", out_dtype=bf16)
        if l != N_A_LAYERS:
            emitted = emit_grads(l, "mix", gw)
        dx, G["norm_gains"][l][0] = matmul_rms_bwd(dz, W["w_in"], NT, sv["x"], gain(l, 0), dx1, name=f"l{l}_b_dh1",
                                                   after=emitted)
        if l == N_A_LAYERS:
            dkv = group_sum(dk_parts, cos, sin, name="b_ropek", rotate=True, width=2 * MAIN_W)
            dkv = group_sum(dv_parts, cos, sin, name="b_sumv", rotate=False, width=2 * MAIN_W, col_block=1, into=dkv)
            gw["w_kv"] = matmul(dkv, kvs["hkv"], TN, name="b_wkv", out_dtype=bf16)
            dx, gkn = matmul_rms_bwd(dkv, kvs["w_kv"], NN, kvs["x"], P["kv_norm"].reshape(1, D_MODEL), dx,
                                     name="b_dhkv")
            G["kv_norm"] = gkn.reshape(D_MODEL)
            emitted = emit_grads(l, "mix", gw)

    small = {"pool_scale": jnp.stack(G["pool_scale"]),
             "mem_norm": jnp.concatenate(G["mem_norm"], axis=0),
             "norm_gains": jnp.stack([jnp.concatenate(r, axis=0) for r in G["norm_gains"]]),
             "kv_norm": G["kv_norm"]}
    return sq[0, 0], dx.reshape(n_seq, SEQ, D_MODEL), small, emitted


def _peer(k):
    x, y, c = lax.axis_index("x"), lax.axis_index("y"), lax.axis_index("c")
    px = 1 - x if k & 4 else x
    py = 1 - y if k & 2 else y
    pc = 1 - c if k & 1 else c
    return (px, py, pc), 4 * px + 2 * py + pc


def _my_index():
    return 4 * lax.axis_index("x") + 2 * lax.axis_index("y") + lax.axis_index("c")


def _src_for(kinds, in_refs, i, idx):
    return in_refs[i] if kinds[i] == "gather" else in_refs[i].at[idx]


def _local_copies(kinds, in_refs, out_refs, local_sems):
    me = _my_index()
    return [pltpu.make_async_copy(_src_for(kinds, in_refs, i, me), out_refs[i].at[me], local_sems.at[i])
            for i in range(len(kinds))]


def _remote_copies(kinds, in_refs, out_refs, send_sems, recv_sems, *, arriving):
    me = _my_index()
    copies = []
    for k in range(1, N_DEV):
        dev, idx = _peer(k)
        for i in range(len(kinds)):
            j = i * (N_DEV - 1) + k - 1
            copies.append(pltpu.make_async_remote_copy(
                src_ref=_src_for(kinds, in_refs, i, idx), dst_ref=out_refs[i].at[idx if arriving else me],
                send_sem=send_sems.at[j], recv_sem=recv_sems.at[j], device_id=dev, device_id_type=MESH))
    return copies


def _out_shape(a, kind):
    return ((N_DEV,) + a.shape) if kind == "gather" else a.shape


def exchange(items, *, name, after=()):
    n = len(items)
    kinds = [k for _, k in items]
    after = list(after)

    def body(*refs):
        in_refs, out_refs = refs[:n], refs[n + len(after):2 * n + len(after)]
        send_sems, recv_sems, local_sems = refs[-3:]
        local = _local_copies(kinds, in_refs, out_refs, local_sems)
        sends = _remote_copies(kinds, in_refs, out_refs, send_sems, recv_sems, arriving=False)
        for cp in local + sends:
            cp.start()
        for cp in _remote_copies(kinds, in_refs, out_refs, send_sems, recv_sems, arriving=True):
            cp.wait_recv()
        for cp in sends:
            cp.wait_send()
        for cp in local:
            cp.wait()

    any_spec = pl.BlockSpec(memory_space=pl.ANY)
    return pl.pallas_call(
        body, name=name,
        in_specs=[any_spec] * (n + len(after)), out_specs=[any_spec] * n,
        out_shape=[_sds(_out_shape(a, k), a.dtype) for a, k in items],
        scratch_shapes=[pltpu.SemaphoreType.DMA((n * (N_DEV - 1),)), pltpu.SemaphoreType.DMA((n * (N_DEV - 1),)),
                        pltpu.SemaphoreType.DMA((n,))],
    )(*[a for a, _ in items], *after)


_HBM = pl.BlockSpec(memory_space=pltpu.HBM)
_SEM = pl.BlockSpec(memory_space=pltpu.SEMAPHORE)
_EFFECT = pltpu.SideEffectType.DATAFLOW_SIDE_EFFECTING


def exchange_start(items, after, *, name):
    n = len(items)
    kinds = [k for _, k in items]

    def body(*refs):
        in_refs, land_refs = refs[:n], refs[n:2 * n]
        send_sems, recv_sems, local_sems = refs[2 * n + 1:2 * n + 4]
        token = refs[-1]
        for cp in (_local_copies(kinds, in_refs, land_refs, local_sems)
                   + _remote_copies(kinds, in_refs, land_refs, send_sems, recv_sems, arriving=False)):
            cp.start()
        token[...] = jnp.zeros_like(token)

    srcs = [pltpu.with_memory_space_constraint(a, pltpu.HBM) for a, _ in items]
    lands = [pltpu.with_memory_space_constraint(lax.empty(_out_shape(a, k), a.dtype), pltpu.HBM) for a, k in items]
    outs = pl.pallas_call(
        body, name=name,
        out_shape=(pltpu.SemaphoreType.DMA((n * (N_DEV - 1),)), pltpu.SemaphoreType.DMA((n * (N_DEV - 1),)),
                   pltpu.SemaphoreType.DMA((n,)),
                   *[pltpu.HBM(a.shape, a.dtype) for a in srcs], *[pltpu.HBM(a.shape, a.dtype) for a in lands],
                   _sds((8, 128), f32)),
        in_specs=[_HBM] * (2 * n) + [pl.BlockSpec(memory_space=pl.ANY)],
        out_specs=(_SEM, _SEM, _SEM, *[_HBM] * (2 * n), pl.BlockSpec(memory_space=pltpu.VMEM)),
        input_output_aliases={i: 3 + i for i in range(2 * n)},
        compiler_params=pltpu.CompilerParams(has_side_effects=_EFFECT),
    )(*srcs, *lands, after)
    return {"kinds": kinds, "sems": outs[:3], "srcs": outs[3:3 + n], "lands": outs[3 + n:3 + 2 * n], "token": outs[-1]}


def exchange_wait(handle, after, *, name):
    kinds = handle["kinds"]
    n = len(kinds)

    def body(*refs):
        in_refs, land_refs = refs[:n], refs[n:2 * n]
        send_sems, recv_sems, local_sems = refs[2 * n:2 * n + 3]
        for cp in _remote_copies(kinds, in_refs, land_refs, send_sems, recv_sems, arriving=True):
            cp.wait_recv()
        for cp in _remote_copies(kinds, in_refs, land_refs, send_sems, recv_sems, arriving=False):
            cp.wait_send()
        for cp in _local_copies(kinds, in_refs, land_refs, local_sems):
            cp.wait()

    srcs, lands = list(handle["srcs"]), list(handle["lands"])
    after = list(after) if isinstance(after, (list, tuple)) else [after]
    outs = pl.pallas_call(
        body, name=name,
        out_shape=tuple(pltpu.HBM(a.shape, a.dtype) for a in srcs + lands),
        in_specs=[_HBM] * (2 * n) + [_SEM] * 3 + [pl.BlockSpec(memory_space=pl.ANY)] * len(after),
        out_specs=tuple([_HBM] * (2 * n)),
        input_output_aliases={i: i for i in range(2 * n)},
        compiler_params=pltpu.CompilerParams(has_side_effects=_EFFECT),
    )(*srcs, *lands, *handle["sems"], *after)
    return list(outs[n:])


CHIP_MASKS = (2, 4, 6)


def _g2_first(in_refs, land_refs, send_sems, recv_sems, *, masks, arriving):
    me = _my_index()
    copies = []
    for i in range(len(land_refs)):
        for j, k in enumerate(masks):
            dev, idx = _peer(k)
            dst = land_refs[i].at[idx if arriving else me]
            copies.append(pltpu.make_async_remote_copy(
                src_ref=dst if in_refs is None else in_refs[i], dst_ref=dst,
                send_sem=send_sems.at[i * len(masks) + j], recv_sem=recv_sems.at[i * len(masks) + j],
                device_id=dev, device_id_type=MESH))
    return copies


def _g2_forward(land_refs, fwd_send, fwd_recv, *, arriving):
    sibling, _ = _peer(1)
    copies = []
    for i in range(len(land_refs)):
        for j, k in enumerate(CHIP_MASKS):
            _, idx = _peer(k | 1 if arriving else k)
            copies.append(pltpu.make_async_remote_copy(
                src_ref=land_refs[i].at[idx], dst_ref=land_refs[i].at[idx],
                send_sem=fwd_send.at[i * 3 + j], recv_sem=fwd_recv.at[i * 3 + j], device_id=sibling,
                device_id_type=MESH))
    return copies


def gather2_start(arrays, after, *, name):
    n = len(arrays)

    def body(*refs):
        in_refs, land_refs = refs[:n], refs[n:2 * n]
        ici_send, ici_recv, d2d_send, d2d_recv, local_sems = refs[2 * n + 1:2 * n + 6]
        token = refs[-1]
        ici = _g2_first(in_refs, land_refs, ici_send, ici_recv, masks=CHIP_MASKS, arriving=False)
        d2d = _g2_first(in_refs, land_refs, d2d_send, d2d_recv, masks=(1,), arriving=False)
        for cp in _local_copies(["gather"] * n, in_refs, land_refs, local_sems) + ici + d2d:
            cp.start()
        token[...] = jnp.zeros_like(token)

    srcs = [pltpu.with_memory_space_constraint(a, pltpu.HBM) for a in arrays]
    lands = [pltpu.with_memory_space_constraint(lax.empty((N_DEV,) + a.shape, a.dtype), pltpu.HBM) for a in arrays]
    sem = pltpu.SemaphoreType.DMA
    outs = pl.pallas_call(
        body, name=name,
        out_shape=(sem((3 * n,)), sem((3 * n,)), sem((n,)), sem((n,)), sem((n,)),
                   *[pltpu.HBM(a.shape, a.dtype) for a in srcs], *[pltpu.HBM(a.shape, a.dtype) for a in lands],
                   _sds((8, 128), f32)),
        in_specs=[_HBM] * (2 * n) + [pl.BlockSpec(memory_space=pl.ANY)],
        out_specs=(*[_SEM] * 5, *[_HBM] * (2 * n), pl.BlockSpec(memory_space=pltpu.VMEM)),
        input_output_aliases={i: 5 + i for i in range(2 * n)},
        compiler_params=pltpu.CompilerParams(has_side_effects=_EFFECT),
    )(*srcs, *lands, after)
    return {"n": n, "sems": outs[:5], "srcs": outs[5:5 + n], "lands": outs[5 + n:5 + 2 * n], "token": outs[-1]}


def gather2_forward(handle, after, *, name):
    n = handle["n"]

    def body(*refs):
        land_refs = refs[:n]
        ici_recv = refs[n]
        fwd_send, fwd_recv = refs[n + 2:n + 4]
        for cp in _g2_first(None, land_refs, fwd_send, ici_recv, masks=CHIP_MASKS, arriving=True):
            cp.wait_recv()
        for cp in _g2_forward(land_refs, fwd_send, fwd_recv, arriving=False):
            cp.start()

    lands = list(handle["lands"])
    sem = pltpu.SemaphoreType.DMA
    outs = pl.pallas_call(
        body, name=name,
        out_shape=(sem((3 * n,)), sem((3 * n,)), *[pltpu.HBM(a.shape, a.dtype) for a in lands]),
        in_specs=[_HBM] * n + [_SEM, pl.BlockSpec(memory_space=pl.ANY)],
        out_specs=(_SEM, _SEM, *[_HBM] * n),
        input_output_aliases={i: 2 + i for i in range(n)},
        compiler_params=pltpu.CompilerParams(has_side_effects=_EFFECT),
    )(*lands, handle["sems"][1], after)
    return dict(handle, fwd=outs[:2], lands=outs[2:])


def gather2_wait(handle, after, *, name):
    n = handle["n"]

    def body(*refs):
        in_refs, land_refs = refs[:n], refs[n:2 * n]
        ici_send, d2d_send, d2d_recv, local_sems, fwd_send, fwd_recv = refs[2 * n:2 * n + 6]
        for cp in _g2_first(in_refs, land_refs, d2d_send, d2d_recv, masks=(1,), arriving=True):
            cp.wait_recv()
        for cp in _g2_forward(land_refs, fwd_send, fwd_recv, arriving=True):
            cp.wait_recv()
        for cp in (_g2_first(in_refs, land_refs, ici_send, fwd_recv, masks=CHIP_MASKS, arriving=False)
                   + _g2_first(in_refs, land_refs, d2d_send, d2d_recv, masks=(1,), arriving=False)
                   + _g2_forward(land_refs, fwd_send, fwd_recv, arriving=False)):
            cp.wait_send()
        for cp in _local_copies(["gather"] * n, in_refs, land_refs, local_sems):
            cp.wait()

    srcs, lands = list(handle["srcs"]), list(handle["lands"])
    s = handle["sems"]
    outs = pl.pallas_call(
        body, name=name,
        out_shape=tuple(pltpu.HBM(a.shape, a.dtype) for a in srcs + lands),
        in_specs=[_HBM] * (2 * n) + [_SEM] * 6 + [pl.BlockSpec(memory_space=pl.ANY)],
        out_specs=tuple([_HBM] * (2 * n)),
        input_output_aliases={i: i for i in range(2 * n)},
        compiler_params=pltpu.CompilerParams(has_side_effects=_EFFECT),
    )(*srcs, *lands, s[0], s[2], s[3], s[4], *handle["fwd"], after)
    return list(outs[n:])


def adamw(entries, *, name):
    c1 = 1.0 - ADAM_B1 ** ADAM_STEP
    c2 = 1.0 - ADAM_B2 ** ADAM_STEP
    tiles = [_tile(w.shape[-2], (64, 32, 16, 8)) for _, w, _, _, _, _ in entries]
    steps = [w.shape[-2] // tr for (_, w, _, _, _, _), tr in zip(entries, tiles)]
    n = len(entries)

    def body(*refs):
        i = pl.program_id(0)
        for e in range(n):
            s_ref, w_ref, m_ref, v_ref = refs[4 * e:4 * e + 4]
            g_ref, d_ref, m2_ref, v2_ref = refs[len(refs) - 4 * n + 4 * e:len(refs) - 4 * n + 4 * e + 4]

            @pl.when(i < steps[e])
            def _():
                g = s_ref[0].astype(f32)
                for d in range(1, N_DEV):
                    g = g + s_ref[d].astype(f32)
                m2 = ADAM_B1 * m_ref[...] + (1.0 - ADAM_B1) * g
                v2 = ADAM_B2 * v_ref[...] + (1.0 - ADAM_B2) * (g * g)
                g_ref[...] = g
                m2_ref[...] = m2
                v2_ref[...] = v2
                d_ref[...] = -ADAM_LR * ((m2 / c1) / (jnp.sqrt(v2 / c2) + ADAM_EPS) + ADAM_WD * w_ref[...])

    in_specs, out_specs, out_shape, args, extras, aliases = [], [], [], [], [], {}
    for e, ((slots, w, m, v, layer, into), tr, ns) in enumerate(zip(entries, tiles, steps)):
        C = w.shape[-1]
        row = lambda i, ns=ns: jnp.minimum(i, ns - 1)
        if layer is None:
            blk = pl.BlockSpec((tr, C), lambda i, row=row: (row(i), 0))
        else:
            blk = pl.BlockSpec((None, tr, C), lambda i, row=row, layer=layer: (layer, row(i), 0))
        in_specs += [pl.BlockSpec((N_DEV, tr, C), lambda i, row=row: (0, row(i), 0)), blk, blk, blk]
        args += [slots, w, m, v]
        out_specs += [blk] * 4
        out_shape += [_sds(w.shape, f32)] * 4
        if into is not None:
            for t, a in enumerate(into):
                aliases[4 * n + len(extras)] = 4 * e + t
                extras.append(a)
    outs = pl.pallas_call(
        body, name=name, grid=(max(steps),),
        in_specs=in_specs + [pl.BlockSpec(memory_space=pl.ANY)] * len(extras),
        out_specs=out_specs, out_shape=out_shape, input_output_aliases=aliases,
        compiler_params=_params(("arbitrary",)),
    )(*args, *extras)
    return [outs[4 * e:4 * e + 4] for e in range(n)]


WEIGHTS = ("norm_gains", "mem_norm", "w_in", "w_mem_kv", "w_out", "w_pool", "pool_scale", "kv_norm", "w_kv",
           "w_gate_up", "w_down")
LAYER_MATS = ("w_in", "w_mem_kv", "w_out", "w_gate_up", "w_down")
POOL_SHARD = MAIN_W // N_DEV
KV_SHARD = 2 * MAIN_W // N_DEV
LOOKAHEAD = 2
TWO_LEVEL_LAYERS = (0, 1)


def _pack_small(gains, pscale):
    lead = gains.shape[:-3]
    g = gains.reshape(lead + (16, 128))
    p = jnp.zeros(lead + (8, 128), f32).at[..., :2, :POOL_SHARD].set(pscale)
    return jnp.concatenate([g, p], axis=-2)


def _unpack_small(a):
    return a[:16].reshape(4, 4, 128), a[16:18, :POOL_SHARD]


def _pack_repl(mem_norm, kv_norm):
    return jnp.concatenate([mem_norm, kv_norm.reshape(1, D_MODEL), jnp.zeros((3, D_MODEL), f32)], axis=0)


def _unpack_repl(a):
    return a[:4], a[4]


def kernel(x, mem, positions, norm_gains, mem_norm, w_in, w_mem_kv, w_out, w_pool, pool_scale, kv_norm, w_kv, w_gate_up, w_down, loss_target, m_norm_gains, m_mem_norm, m_w_in, m_w_mem_kv, m_w_out, m_w_pool, m_pool_scale, m_kv_norm, m_w_kv, m_w_gate_up, m_w_down, v_norm_gains, v_mem_norm, v_w_in, v_w_mem_kv, v_w_out, v_w_pool, v_pool_scale, v_kv_norm, v_w_kv, v_w_gate_up, v_w_down):
    w = dict(norm_gains=norm_gains, mem_norm=mem_norm, w_in=w_in, w_mem_kv=w_mem_kv, w_out=w_out, w_pool=w_pool,
             pool_scale=pool_scale, kv_norm=kv_norm, w_kv=w_kv, w_gate_up=w_gate_up, w_down=w_down)
    m = dict(norm_gains=m_norm_gains, mem_norm=m_mem_norm, w_in=m_w_in, w_mem_kv=m_w_mem_kv, w_out=m_w_out,
             w_pool=m_w_pool, pool_scale=m_pool_scale, kv_norm=m_kv_norm, w_kv=m_w_kv, w_gate_up=m_w_gate_up,
             w_down=m_w_down)
    v = dict(norm_gains=v_norm_gains, mem_norm=v_mem_norm, w_in=v_w_in, w_mem_kv=v_w_mem_kv, w_out=v_w_out,
             w_pool=v_w_pool, pool_scale=v_pool_scale, kv_norm=v_kv_norm, w_kv=v_w_kv, w_gate_up=v_w_gate_up,
             w_down=v_w_down)

    def transposed_view(d):
        d = dict(d)
        d["w_gate_up"] = jnp.swapaxes(d["w_gate_up"], 1, 2)
        d["w_kv"] = jnp.swapaxes(d["w_kv"], 0, 1)
        return d

    wv, mv, vv = transposed_view(w), transposed_view(m), transposed_view(v)

    small = _pack_small(norm_gains, pool_scale)
    (gsmall,) = exchange([(small, "gather")], name="gather_small")
    P = {"norm_gains": jnp.moveaxis(gsmall[:, :16].reshape(N_DEV, 4, 4, 128), 0, 2).reshape(4, 4, D_MODEL),
         "pool_scale": jnp.moveaxis(gsmall[:, 16:18, :POOL_SHARD], 0, 1).reshape(2, MAIN_W),
         "mem_norm": mem_norm, "kv_norm": kv_norm, "w_pool": w_pool}

    PARTS = {"mix": ("w_in", "w_mem_kv", "w_out"), "ffn": ("w_gate_up", "w_down"), "gu": ("w_gate_up",),
             "down": ("w_down",), "all": ("w_in", "w_mem_kv", "w_out", "w_gate_up", "w_down")}

    def parts_of(l):
        return (("mix", "gu", "down"), ("mix", "ffn"))[l] if l < 2 else ("all",)

    def part_items(l, part):
        items = [(wv[k][l].astype(bf16), "gather") for k in PARTS[part]]
        if part == "ffn" and l == N_A_LAYERS - 1:
            items.append((wv["w_kv"].astype(bf16), "gather"))
        return items

    handles = {}

    def start_layer(l, after):
        for part in parts_of(l):
            if l in TWO_LEVEL_LAYERS:
                handles[l, part] = gather2_start([a for a, _ in part_items(l, part)], after,
                                                 name=f"gather_start_{part}_l{l}")
            else:
                handles[l, part] = exchange_start(part_items(l, part), after, name=f"gather_start_{part}_l{l}")
            after = handles[l, part]["token"]
        return after

    token = gsmall
    for l in range(LOOKAHEAD):
        token = start_layer(l, token)
    landed = {}

    def layer_weights(l, part, after):
        if part not in parts_of(l):
            if part == "down" or (part == "gu" and "all" in parts_of(l)):
                return {}, None
            part = "all" if "all" in parts_of(l) else "ffn"
        if l == 0 and part == "mix":
            after = token
        if l in TWO_LEVEL_LAYERS:
            passed = gather2_forward(handles[l, part], after, name=f"gather_forward_{part}_l{l}")
            got = gather2_wait(passed, after, name=f"gather_wait_{part}_l{l}")
        else:
            got = exchange_wait(handles[l, part], after, name=f"gather_wait_{part}_l{l}")
        landed[l, part] = got
        started = None
        if part in ("mix", "all") and l + LOOKAHEAD < DEPTH:
            started = start_layer(l + LOOKAHEAD, got[0])
        W = {k: g.reshape(-1, g.shape[-1]) for k, g in zip(PARTS[part], got)}
        return W, started

    def kv_weight(after):
        g = landed[N_A_LAYERS - 1, "ffn"][len(PARTS["ffn"])]
        return g.reshape(2 * MAIN_W, D_MODEL)

    ghandles = {}

    pending = {}

    def gparts_of(l):
        return ("ffn", "mix") if l < 2 else ("all",)

    def emit_grads(l, part, gw):
        if part not in gparts_of(l):
            pending.setdefault(l, {}).update(gw)
            if part == "ffn":
                return None
            gw, part = pending[l], "all"
        items = [(gw[k].reshape((N_DEV, -1) + gw[k].shape[-1:]), "scatter") for k in PARTS[part]]
        if part != "ffn" and l == N_A_LAYERS:
            items.append((gw["w_kv"].reshape(N_DEV, KV_SHARD, D_MODEL), "scatter"))
        if part != "ffn" and l < N_A_LAYERS:
            items.append((gw["w_pool"], "gather"))
        ghandles[l, part] = exchange_start(items, gsmall, name=f"scatter_start_{part}_l{l}")
        return ghandles[l, part]["token"]

    sq, grad_x, GS, emitted = local_step(x, mem, positions, loss_target, P, layer_weights, kv_weight, emit_grads)

    def pool3(a):
        return a.reshape(N_A_LAYERS, MAIN_W, POOL_GROUP)

    out = {}
    after = [emitted]

    def finish_layer(l, after):
        for part in gparts_of(l):
            got = exchange_wait(ghandles[l, part], after, name=f"scatter_wait_{part}_l{l}")
            names = list(PARTS[part])
            entries = [(slots, wv[k], mv[k], vv[k], l, out.get(k)) for k, slots in zip(names, got)]
            if part != "ffn" and l == N_A_LAYERS:
                names.append("w_kv")
                entries.append((got[-1], wv["w_kv"], mv["w_kv"], vv["w_kv"], None, None))
            if part != "ffn" and l < N_A_LAYERS:
                names.append("w_pool")
                entries.append((got[-1], pool3(w_pool), pool3(m_w_pool), pool3(v_w_pool), l, out.get("w_pool")))
            out.update(zip(names, adamw(entries, name=f"adamw_{part}_l{l}")))
            after = [out[k][0] for k in names]
        return after

    for l in reversed(range(1, DEPTH)):
        after = finish_layer(l, after)

    gs = _pack_small(jnp.moveaxis(GS["norm_gains"].reshape(4, 4, N_DEV, 128), 2, 0),
                     jnp.moveaxis(GS["pool_scale"].reshape(2, N_DEV, POOL_SHARD), 1, 0))
    parts_small, parts_repl, parts_sq = exchange(
        [(gs, "scatter"), (_pack_repl(GS["mem_norm"], GS["kv_norm"]), "gather"),
         (jnp.full((8, 128), sq, f32), "gather")],
        name="exchange_small_grads", after=after)
    loss = (0.5 / D_MODEL) * jnp.sum(parts_sq[:, 0, 0])
    finish_layer(0, [parts_small])
    out["w_gate_up"] = [jnp.swapaxes(r, 1, 2) for r in out["w_gate_up"]]
    out["w_kv"] = [jnp.swapaxes(r, 0, 1) for r in out["w_kv"]]
    out["w_pool"] = [r.reshape(w_pool.shape) for r in out["w_pool"]]

    res_small, res_repl = adamw(
        [(parts_small, small, _pack_small(m_norm_gains, m_pool_scale), _pack_small(v_norm_gains, v_pool_scale),
          None, None),
         (parts_repl, _pack_repl(mem_norm, kv_norm), _pack_repl(m_mem_norm, m_kv_norm),
          _pack_repl(v_mem_norm, v_kv_norm), None, None)], name="adamw_small")
    out["norm_gains"], out["pool_scale"] = zip(*[_unpack_small(r) for r in res_small])
    out["mem_norm"], out["kv_norm"] = zip(*[_unpack_repl(r) for r in res_repl])

    return (loss, grad_x, *[out[k][0] for k in WEIGHTS], *[out[k][1] for k in WEIGHTS],
            *[out[k][2] for k in WEIGHTS], *[out[k][3] for k in WEIGHTS])
```

```python
import numpy as np
import jax
import jax.numpy as jnp
from jax import lax
from jax.experimental import pallas as pl
from jax.experimental.pallas import tpu as pltpu

f32 = jnp.float32
bf16 = jnp.bfloat16

D_MODEL = 1024
SEQ = 2048
DEPTH = 4
N_MEM = 256
HEAD_DIM = 64
N_MEM_HEADS = 4
MEM_W = 256
MAIN_W = 768
POOL_WINDOWS = (2, 4, 8, 16)
POOL_GROUP = 192
POOL_HALO = 16
DIL_PATTERNS = ((128, 1), (512, 4), (2048, 16))
N_GROUPS = 3
GROUP_W = 256
BAND = 128
N_A_LAYERS = 2
D_FF = 2816
ROPE_THETA = 10000.0
EPS = 1e-6
NEG = -1e30
SCALE = HEAD_DIM ** -0.5
N_DEV = 8

ADAM_LR = 0.001
ADAM_B1 = 0.9
ADAM_B2 = 0.999
ADAM_EPS = 1e-08
ADAM_WD = 0.01
ADAM_STEP = 10

VMEM_LIMIT_BYTES = 56 * 1024 * 1024
MESH = pl.DeviceIdType.MESH

NN = (((1,), (0,)), ((), ()))
NT = (((1,), (1,)), ((), ()))
TN = (((0,), (0,)), ((), ()))


def _params(sem=None):
    return pltpu.CompilerParams(dimension_semantics=sem, vmem_limit_bytes=VMEM_LIMIT_BYTES)


def _tile(n, cands):
    for c in cands:
        if n % c == 0:
            return c
    return n


def _sds(shape, dtype):
    return jax.ShapeDtypeStruct(tuple(shape), dtype)


def _rms_r(v):
    return lax.rsqrt(jnp.mean(v * v, axis=-1, keepdims=True) + EPS)


def rms_matmul(x, gain, w, *, name, out_dtype, transposed=False, after=None, rope=None):
    M, K = x.shape
    N = w.shape[0] if transposed else w.shape[1]
    tm = min(512, M)
    order = [] if after is None else [after]
    tables = [] if rope is None else list(rope)
    rot_spec = [] if rope is None else [pl.BlockSpec((tm, MAIN_W), lambda i: (i, 0))]
    rot_shape = [] if rope is None else [_sds((M, MAIN_W), f32)]

    def body(x_ref, g_ref, w_ref, *refs):
        z_ref, h_ref = refs[len(tables) + len(order):][:2]
        xv = x_ref[...]
        h = (xv * _rms_r(xv) * g_ref[...]).astype(bf16)
        h_ref[...] = h
        z = lax.dot_general(h, w_ref[...], NT if transposed else NN, preferred_element_type=f32)
        z_ref[...] = z.astype(z_ref.dtype)
        if tables:
            c = jnp.tile(refs[0][...], (1, MAIN_W // 128))
            s = jnp.tile(refs[1][...], (1, MAIN_W // 128))
            zr = z[:, :MAIN_W]
            refs[-1][...] = zr * c + _swap_halves(zr) * s

    tab = pl.BlockSpec((tm, 128), lambda i: (i, 0))
    return pl.pallas_call(
        body, name=name, grid=(M // tm,),
        in_specs=[pl.BlockSpec((tm, K), lambda i: (i, 0)),
                  pl.BlockSpec((1, K), lambda i: (0, 0)),
                  pl.BlockSpec(w.shape, lambda i: (0, 0))] + [tab] * len(tables)
        + [pl.BlockSpec(memory_space=pl.ANY)] * len(order),
        out_specs=[pl.BlockSpec((tm, N), lambda i: (i, 0)), pl.BlockSpec((tm, K), lambda i: (i, 0))] + rot_spec,
        out_shape=[_sds((M, N), out_dtype), _sds((M, K), bf16)] + rot_shape,
        compiler_params=_params(("parallel",)),
    )(x, gain, w, *tables, *order)


def matmul_rms_res(a, w, gain, res, *, name, target=None):
    M, K = a.shape
    N = w.shape[1]
    tm = min(512, M)
    goal = [] if target is None else [target]

    def body(a_ref, w_ref, g_ref, r_ref, *refs):
        y_ref, x_ref = refs[len(goal):][:2]
        y = jnp.dot(a_ref[...], w_ref[...], preferred_element_type=f32)
        y_ref[...] = y.astype(bf16)
        x = r_ref[...] + y * _rms_r(y) * g_ref[...]
        if not goal:
            x_ref[...] = x
            return
        e = x - refs[0][...]
        x_ref[...] = e * (1.0 / N)
        _accumulate(refs[-1], jnp.sum(jnp.sum(e * e, axis=0, keepdims=True), axis=1, keepdims=True))

    row = pl.BlockSpec((tm, N), lambda i: (i, 0))
    return pl.pallas_call(
        body, name=name, grid=(M // tm,),
        in_specs=[pl.BlockSpec((tm, K), lambda i: (i, 0)),
                  pl.BlockSpec((K, N), lambda i: (0, 0)),
                  pl.BlockSpec((1, N), lambda i: (0, 0)),
                  row] + [row] * len(goal),
        out_specs=[row, row] + [pl.BlockSpec((8, 128), lambda i: (0, 0))] * len(goal),
        out_shape=[_sds((M, N), bf16), _sds((M, N), f32)] + [_sds((8, 128), f32)] * len(goal),
        compiler_params=_params(("arbitrary",) if goal else ("parallel",)),
    )(a, w, gain, res, *goal)


def matmul(a, b, dims, *, name, out_dtype):
    if dims is TN:
        K, M = a.shape
        tm = _tile(M, (512, 256, 128))
        a_spec = pl.BlockSpec((K, tm), lambda i: (0, i))
    else:
        M, K = a.shape
        tm = _tile(M, (1024, 512, 256, 128))
        a_spec = pl.BlockSpec((tm, K), lambda i: (i, 0))
    N = b.shape[0] if dims is NT else b.shape[1]

    def body(a_ref, b_ref, o_ref):
        o_ref[...] = lax.dot_general(a_ref[...].astype(bf16), b_ref[...].astype(bf16), dims,
                                     preferred_element_type=f32).astype(o_ref.dtype)

    return pl.pallas_call(
        body, name=name, grid=(M // tm,),
        in_specs=[a_spec, pl.BlockSpec(b.shape, lambda i: (0, 0))],
        out_specs=pl.BlockSpec((tm, N), lambda i: (i, 0)),
        out_shape=_sds((M, N), out_dtype),
        compiler_params=_params(("parallel",)),
    )(a, b)


def rms_gate_up(x, gain, wt, *, name):
    M, K = x.shape
    tm = min(2048, M)
    tn = _tile(D_FF, (256, 128))
    nj = D_FF // tn

    def body(x_ref, gn_ref, wg_ref, wu_ref, g_ref, u_ref, a_ref, h_ref):
        @pl.when(pl.program_id(1) == 0)
        def _():
            xv = x_ref[...]
            h_ref[...] = (xv * _rms_r(xv) * gn_ref[...]).astype(bf16)

        h = h_ref[...]
        g = lax.dot_general(h, wg_ref[...], NT, preferred_element_type=f32).astype(bf16)
        u = lax.dot_general(h, wu_ref[...], NT, preferred_element_type=f32).astype(bf16)
        g_ref[...] = g
        u_ref[...] = u
        a_ref[...] = g * (1.0 / (1.0 + jnp.exp(-g))) * u

    col = pl.BlockSpec((tm, tn), lambda i, j: (i, j))
    return pl.pallas_call(
        body, name=name, grid=(M // tm, nj),
        in_specs=[pl.BlockSpec((tm, K), lambda i, j: (i, 0)),
                  pl.BlockSpec((1, K), lambda i, j: (0, 0)),
                  pl.BlockSpec((tn, K), lambda i, j: (j, 0)),
                  pl.BlockSpec((tn, K), lambda i, j: (j + nj, 0))],
        out_specs=[col, col, col, pl.BlockSpec((tm, K), lambda i, j: (i, 0))],
        out_shape=[_sds((M, D_FF), bf16)] * 3 + [_sds((M, K), bf16)],
        compiler_params=_params(("parallel", "arbitrary")),
    )(x, gain, wt, wt)


def _rms_bwd_math(yv, gain, dn):
    r = _rms_r(yv)
    q = dn * gain
    dy = r * q - yv * (r * r * r) * jnp.mean(q * yv, axis=-1, keepdims=True)
    return dy, jnp.sum(dn * yv * r, axis=0, keepdims=True)


def _accumulate(ref, val):
    @pl.when(pl.program_id(0) == 0)
    def _():
        ref[...] = jnp.zeros_like(ref)

    ref[...] += val


def down_bwd(y, gain, dn, w_down, g, u, *, name, after=None):
    M, K = y.shape
    tm = min(512, M)
    order = [] if after is None else [after]

    def body(y_ref, gn_ref, dn_ref, w_ref, g_ref, u_ref, *refs):
        dy_ref, o_ref, dg_ref = refs[len(order):]
        dy, dgain = _rms_bwd_math(y_ref[...].astype(f32), gn_ref[...], dn_ref[...])
        dy = dy.astype(bf16)
        dy_ref[...] = dy
        _accumulate(dg_ref, dgain)
        da = lax.dot_general(dy, w_ref[...], NT, preferred_element_type=f32).astype(bf16)
        g = g_ref[...]
        s = 1.0 / (1.0 + jnp.exp(-g))
        o_ref[:, :D_FF] = da * u_ref[...] * s * (1.0 + g * (1.0 - s))
        o_ref[:, D_FF:] = da * g * s

    row = pl.BlockSpec((tm, K), lambda i: (i, 0))
    vec = pl.BlockSpec((1, K), lambda i: (0, 0))
    wide = pl.BlockSpec((tm, D_FF), lambda i: (i, 0))
    return pl.pallas_call(
        body, name=name, grid=(M // tm,),
        in_specs=[row, vec, row, pl.BlockSpec((D_FF, K), lambda i: (0, 0)), wide, wide]
        + [pl.BlockSpec(memory_space=pl.ANY)] * len(order),
        out_specs=[row, pl.BlockSpec((tm, 2 * D_FF), lambda i: (i, 0)), vec],
        out_shape=[_sds((M, K), bf16), _sds((M, 2 * D_FF), bf16), _sds((1, K), f32)],
        compiler_params=_params(("arbitrary",)),
    )(y, gain, dn, w_down, g, u, *order)


def rms_bwd_matmul(y, gain, dn, w, dims, *, name, after=None):
    M, K = y.shape
    N = w.shape[0] if dims is NT else w.shape[1]
    tm = min(1024, M)
    order = [] if after is None else [after]

    def body(y_ref, gn_ref, dn_ref, w_ref, *refs):
        dy_ref, o_ref, dg_ref = refs[len(order):]
        dy, dgain = _rms_bwd_math(y_ref[...].astype(f32), gn_ref[...], dn_ref[...].astype(f32))
        dy = dy.astype(bf16)
        dy_ref[...] = dy
        _accumulate(dg_ref, dgain)
        o_ref[...] = lax.dot_general(dy, w_ref[...], dims, preferred_element_type=f32).astype(bf16)

    row = pl.BlockSpec((tm, K), lambda i: (i, 0))
    vec = pl.BlockSpec((1, K), lambda i: (0, 0))
    return pl.pallas_call(
        body, name=name, grid=(M // tm,),
        in_specs=[row, vec, row, pl.BlockSpec(w.shape, lambda i: (0, 0))]
        + [pl.BlockSpec(memory_space=pl.ANY)] * len(order),
        out_specs=[row, pl.BlockSpec((tm, N), lambda i: (i, 0)), vec],
        out_shape=[_sds((M, K), bf16), _sds((M, N), bf16), _sds((1, K), f32)],
        compiler_params=_params(("arbitrary",)),
    )(y, gain, dn, w, *order)


def matmul_rms_bwd(a, b, dims, y, gain, res, *, name, after=None):
    M, K = a.shape
    N = y.shape[1]
    tm = min(512, M)
    order = [] if after is None else [after]

    def body(a_ref, b_ref, y_ref, gn_ref, r_ref, *refs):
        dx_ref, dg_ref = refs[len(order):]
        dn = lax.dot_general(a_ref[...].astype(bf16), b_ref[...], dims, preferred_element_type=f32)
        dy, dgain = _rms_bwd_math(y_ref[...], gn_ref[...], dn)
        dx_ref[...] = dy + r_ref[...]
        _accumulate(dg_ref, dgain)

    row = pl.BlockSpec((tm, N), lambda i: (i, 0))
    vec = pl.BlockSpec((1, N), lambda i: (0, 0))
    return pl.pallas_call(
        body, name=name, grid=(M // tm,),
        in_specs=[pl.BlockSpec((tm, K), lambda i: (i, 0)), pl.BlockSpec(b.shape, lambda i: (0, 0)), row, vec, row]
        + [pl.BlockSpec(memory_space=pl.ANY)] * len(order),
        out_specs=[row, vec],
        out_shape=[_sds((M, N), f32), _sds((1, N), f32)],
        compiler_params=_params(("arbitrary",)),
    )(a, b, y, gain, res, *order)


def rms_bwd(y, gain, dn, res, *, name, out_dtype, after=None):
    M, N = y.shape
    tm = min(512, M)
    has_res = res is not None
    order = [] if after is None else [after]

    def body(*refs):
        y_ref, g_ref, dn_ref = refs[:3]
        r_ref = refs[3] if has_res else None
        dy_ref, dg_ref = refs[-2:]
        dy, dgain = _rms_bwd_math(y_ref[...].astype(f32), g_ref[...], dn_ref[...].astype(f32))
        if has_res:
            dy = dy + r_ref[...]
        dy_ref[...] = dy.astype(dy_ref.dtype)
        _accumulate(dg_ref, dgain)

    row = pl.BlockSpec((tm, N), lambda i: (i, 0))
    vec = pl.BlockSpec((1, N), lambda i: (0, 0))
    args = [y, gain, dn] + ([res] if has_res else []) + order
    return pl.pallas_call(
        body, name=name, grid=(M // tm,),
        in_specs=[row, vec, row] + ([row] if has_res else []) + [pl.BlockSpec(memory_space=pl.ANY)] * len(order),
        out_specs=[row, vec],
        out_shape=[_sds((M, N), out_dtype), _sds((1, N), f32)],
        compiler_params=_params(("arbitrary",)),
    )(*args)


def _pool_select(a1, a2, a3, a4):
    col = lax.broadcasted_iota(jnp.int32, (1, MAIN_W), 1) // POOL_GROUP
    return jnp.where(col == 0, a1, jnp.where(col == 1, a2, jnp.where(col == 2, a3, a4)))


def _pool_count(t):
    col = lax.broadcasted_iota(jnp.int32, (1, MAIN_W), 1) // POOL_GROUP
    win = jnp.where(col == 0, 2, jnp.where(col == 1, 4, jnp.where(col == 2, 8, 16)))
    return jnp.minimum(t + 1, win).astype(f32)


def pool_fwd(z, wbd, scale, *, name):
    M = z.shape[0]
    tm = 512
    nper = SEQ // tm
    hb = tm // POOL_HALO

    def body(zc_ref, zh_ref, w_ref, s_ref, p_ref, y_ref):
        i = pl.program_id(0)
        seq_blk = i % nper
        halo = jnp.where(seq_blk == 0, 0.0, zh_ref[...].astype(f32))
        u = zc_ref[...].astype(f32)
        ext = jnp.concatenate([halo, u], axis=0)
        a1 = ext + pltpu.roll(ext, 1, 0)
        a2 = a1 + pltpu.roll(a1, 2, 0)
        a3 = a2 + pltpu.roll(a2, 4, 0)
        a4 = a3 + pltpu.roll(a3, 8, 0)
        sums = _pool_select(a1, a2, a3, a4)[POOL_HALO:]
        t = seq_blk * tm + lax.broadcasted_iota(jnp.int32, (tm, 1), 0)
        p = (sums / _pool_count(t) - u).astype(bf16)
        p_ref[...] = p
        y_ref[...] = (jnp.dot(p, w_ref[...], preferred_element_type=f32) * s_ref[...]).astype(bf16)

    return pl.pallas_call(
        body, name=name, grid=(M // tm,),
        in_specs=[pl.BlockSpec((tm, MAIN_W), lambda i: (i, 0)),
                  pl.BlockSpec((POOL_HALO, MAIN_W), lambda i: (jnp.maximum(i * hb - 1, 0), 0)),
                  pl.BlockSpec((MAIN_W, MAIN_W), lambda i: (0, 0)),
                  pl.BlockSpec((1, MAIN_W), lambda i: (0, 0))],
        out_specs=[pl.BlockSpec((tm, MAIN_W), lambda i: (i, 0)),
                   pl.BlockSpec((tm, MAIN_W), lambda i: (i, 0))],
        out_shape=[_sds((M, MAIN_W), bf16), _sds((M, D_MODEL), bf16)],
        compiler_params=_params(("parallel",)),
    )(z, z, wbd, scale)


def pool_bwd(dyc, p, wbd, scale, *, name):
    M = p.shape[0]
    tm = 512
    nper = SEQ // tm
    hb = tm // POOL_HALO
    last_hb = M // POOL_HALO - 1

    def body(dy_ref, dyh_ref, p_ref, w_ref, s_ref, dz_ref, dw_ref, ds_ref):
        i = pl.program_id(0)
        seq_blk = i % nper
        dy = dy_ref[...].astype(f32)
        pv = p_ref[...]
        w = w_ref[...]
        sc = s_ref[...]

        @pl.when(i == 0)
        def _():
            dw_ref[...] = jnp.zeros_like(dw_ref)
            ds_ref[...] = jnp.zeros_like(ds_ref)

        v = jnp.dot(pv, w, preferred_element_type=f32)
        ds_ref[...] += jnp.sum(dy * v, axis=0, keepdims=True)
        dv = (dy * sc).astype(bf16)
        dw_ref[...] += lax.dot_general(pv, dv, TN, preferred_element_type=f32)
        dp = lax.dot_general(dv, w, NT, preferred_element_type=f32)
        dvh = jnp.where(seq_blk == nper - 1, 0.0, dyh_ref[...].astype(f32) * sc).astype(bf16)
        dph = lax.dot_general(dvh, w, NT, preferred_element_type=f32)
        ext = jnp.concatenate([dp, dph], axis=0)
        n = tm + POOL_HALO
        t = seq_blk * tm + lax.broadcasted_iota(jnp.int32, (n, 1), 0)
        e = ext / _pool_count(t)
        b1 = e + pltpu.roll(e, n - 1, 0)
        b2 = b1 + pltpu.roll(b1, n - 2, 0)
        b3 = b2 + pltpu.roll(b2, n - 4, 0)
        b4 = b3 + pltpu.roll(b3, n - 8, 0)
        dz_ref[...] = (_pool_select(b1, b2, b3, b4)[:tm] - dp).astype(dz_ref.dtype)

    return pl.pallas_call(
        body, name=name, grid=(M // tm,),
        in_specs=[pl.BlockSpec((tm, MAIN_W), lambda i: (i, 0)),
                  pl.BlockSpec((POOL_HALO, MAIN_W), lambda i: (jnp.minimum((i + 1) * hb, last_hb), 0)),
                  pl.BlockSpec((tm, MAIN_W), lambda i: (i, 0)),
                  pl.BlockSpec((MAIN_W, MAIN_W), lambda i: (0, 0)),
                  pl.BlockSpec((1, MAIN_W), lambda i: (0, 0))],
        out_specs=[pl.BlockSpec((tm, MAIN_W), lambda i: (i, 0)),
                   pl.BlockSpec((MAIN_W, MAIN_W), lambda i: (0, 0)),
                   pl.BlockSpec((1, MAIN_W), lambda i: (0, 0))],
        out_shape=[_sds((M, D_MODEL), bf16), _sds((MAIN_W, MAIN_W), f32), _sds((1, MAIN_W), f32)],
        compiler_params=_params(("arbitrary",)),
    )(dyc, dyc, p, wbd, scale)


def _mem_heads(q, kv):
    first = _first_head()
    for pr in range(N_MEM_HEADS // 2):
        cols = slice(pr * PAIR_W, (pr + 1) * PAIR_W)
        qp = q[:, cols] * SCALE
        kp = kv[:, cols]
        vp = kv[:, MEM_W + pr * PAIR_W: MEM_W + (pr + 1) * PAIR_W]
        for hh in range(2):
            lm = first if hh == 0 else ~first
            qm = jnp.where(lm, qp, 0.0).astype(bf16)
            s = lax.dot_general(qm, kp, NT, preferred_element_type=f32)
            e = jnp.exp(s - jnp.max(s, axis=-1, keepdims=True))
            yield lm, qm, kp, vp, e, jnp.sum(e, axis=-1, keepdims=True)


def memattn_fwd(z, kvm, ycat, *, name, n_seq):
    M = z.shape[0]
    tq = 512
    nq = SEQ // tq

    def body(q_ref, kv_ref, _, o_ref):
        first = _first_head()
        outs = []
        for lm, _, _, vp, e, l in _mem_heads(q_ref[...], kv_ref[...]):
            outs.append(jnp.dot(e.astype(bf16), vp, preferred_element_type=f32) * (1.0 / l))
        pairs = [jnp.where(first, outs[2 * pr], outs[2 * pr + 1]) for pr in range(N_MEM_HEADS // 2)]
        o_ref[...] = jnp.concatenate(pairs, axis=1).astype(bf16)

    return pl.pallas_call(
        body, name=name, grid=(n_seq, nq),
        in_specs=[pl.BlockSpec((tq, MEM_W), lambda b, i: (b * nq + i, 3)),
                  pl.BlockSpec((N_MEM, 2 * MEM_W), lambda b, i: (b, 0)),
                  pl.BlockSpec(memory_space=pl.ANY)],
        out_specs=pl.BlockSpec((tq, MEM_W), lambda b, i: (b * nq + i, 3)),
        out_shape=_sds((M, D_MODEL), bf16),
        input_output_aliases={2: 0},
        compiler_params=_params(("parallel", "parallel")),
    )(z, kvm, ycat)


def memattn_bwd(z, kvm, dyc, dz, *, name, n_seq):
    M = z.shape[0]
    tq = 512
    nq = SEQ // tq

    def body(q_ref, kv_ref, dy_ref, _, dq_ref, dkv_ref):
        first = _first_head()
        dy = dy_ref[...].astype(f32)
        dqs, dks, dvs = [], [], []
        for h, (lm, qm, kp, vp, e, l) in enumerate(_mem_heads(q_ref[...], kv_ref[...])):
            pr = h // 2
            p = e * (1.0 / l)
            dym = jnp.where(lm, dy[:, pr * PAIR_W:(pr + 1) * PAIR_W], 0.0).astype(bf16)
            dp = lax.dot_general(dym, vp, NT, preferred_element_type=f32)
            ds = (p * (dp - jnp.sum(dp * p, axis=-1, keepdims=True))).astype(bf16)
            dqs.append(jnp.dot(ds, kp, preferred_element_type=f32) * SCALE)
            dk = lax.dot_general(ds, qm, TN, preferred_element_type=f32)
            dv = lax.dot_general(p.astype(bf16), dym, TN, preferred_element_type=f32)
            if h % 2 == 0:
                dks.append(dk)
                dvs.append(dv)
            else:
                dks[pr] = dks[pr] + dk
                dvs[pr] = dvs[pr] + dv
        pairs = [jnp.where(first, dqs[2 * pr], dqs[2 * pr + 1]) for pr in range(N_MEM_HEADS // 2)]
        dq_ref[...] = jnp.concatenate(pairs, axis=1).astype(bf16)

        @pl.when(pl.program_id(1) == 0)
        def _():
            dkv_ref[...] = jnp.zeros_like(dkv_ref)

        dkv_ref[...] += jnp.concatenate(dks + dvs, axis=1)

    return pl.pallas_call(
        body, name=name, grid=(n_seq, nq),
        in_specs=[pl.BlockSpec((tq, MEM_W), lambda b, i: (b * nq + i, 3)),
                  pl.BlockSpec((N_MEM, 2 * MEM_W), lambda b, i: (b, 0)),
                  pl.BlockSpec((tq, MEM_W), lambda b, i: (b * nq + i, 3)),
                  pl.BlockSpec(memory_space=pl.ANY)],
        out_specs=[pl.BlockSpec((tq, MEM_W), lambda b, i: (b * nq + i, 3)),
                   pl.BlockSpec((N_MEM, 2 * MEM_W), lambda b, i: (b, 0))],
        out_shape=[_sds((M, D_MODEL), bf16), _sds((n_seq * N_MEM, 2 * MEM_W), f32)],
        input_output_aliases={3: 0},
        compiler_params=_params(("parallel", "arbitrary")),
    )(z, kvm, dyc, dz)


def rope_tables(pos, *, name):
    M = pos.shape[0]
    tm = min(1024, M)
    half = HEAD_DIM // 2
    inv = ROPE_THETA ** (-np.arange(half, dtype=np.float64) / half)
    inv128 = jnp.asarray(np.tile(inv, 4)[None, :], f32)
    sign128 = jnp.asarray(np.tile(np.concatenate([-np.ones(half), np.ones(half)]), 2)[None, :], f32)

    def body(p_ref, f_ref, s_ref, cos_ref, sin_ref):
        ang = p_ref[...] * f_ref[...]
        cos_ref[...] = jnp.cos(ang)
        sin_ref[...] = jnp.sin(ang) * s_ref[...]

    return pl.pallas_call(
        body, name=name, grid=(M // tm,),
        in_specs=[pl.BlockSpec((tm, 1), lambda i: (i, 0)),
                  pl.BlockSpec((1, 128), lambda i: (0, 0)),
                  pl.BlockSpec((1, 128), lambda i: (0, 0))],
        out_specs=[pl.BlockSpec((tm, 128), lambda i: (i, 0)),
                   pl.BlockSpec((tm, 128), lambda i: (i, 0))],
        out_shape=[_sds((M, 128), f32), _sds((M, 128), f32)],
        compiler_params=_params(("parallel",)),
    )(pos, inv128, sign128)


def _swap_halves(x):
    w = x.shape[1]
    first = (lax.broadcasted_iota(jnp.int32, (1, w), 1) % HEAD_DIM) < (HEAD_DIM // 2)
    return jnp.where(first, pltpu.roll(x, w - HEAD_DIM // 2, 1), pltpu.roll(x, HEAD_DIM // 2, 1))


def group_sum(groups, cos, sin, *, name, rotate, width, col_block=0, into=None):
    M = groups[0][0].shape[0]
    tm = min(512, M)
    counts = [len(g) for g in groups]
    flat = [a for g in groups for a in g]
    extra = [] if into is None else [into]

    def body(*refs):
        part_refs = refs[:len(flat)]
        c_ref, s_ref = refs[len(flat):len(flat) + 2]
        o_ref = refs[-1]
        cols, k = [], 0
        for n in counts:
            acc = part_refs[k][...]
            for r in part_refs[k + 1:k + n]:
                acc = acc + r[...]
            cols.append(acc)
            k += n
        d = jnp.concatenate(cols, axis=1)
        if rotate:
            c = jnp.tile(c_ref[...], (1, MAIN_W // 128))
            s = jnp.tile(s_ref[...], (1, MAIN_W // 128))
            d = d * c - _swap_halves(d) * s
        o_ref[...] = d.astype(bf16)

    part = pl.BlockSpec((tm, GROUP_W), lambda i: (i, 0))
    tab = pl.BlockSpec((tm, 128), lambda i: (i, 0))
    return pl.pallas_call(
        body, name=name, grid=(M // tm,),
        in_specs=[part] * len(flat) + [tab, tab] + [pl.BlockSpec(memory_space=pl.ANY)] * len(extra),
        out_specs=pl.BlockSpec((tm, MAIN_W), lambda i: (i, col_block)),
        out_shape=_sds((M, width), bf16),
        input_output_aliases={len(flat) + 2: 0} if extra else {},
        compiler_params=_params(("parallel",)),
    )(*flat, cos, sin, *extra)


PAIR_W = 2 * HEAD_DIM
MIN_BLOCKS = 8


def _dil_geometry(dil):
    nsub = max(dil, MIN_BLOCKS)
    tb = BAND * nsub
    return nsub, tb, SEQ // tb


REGROUP = 4


class _Regrouped:
    def __init__(self, ref):
        self.ref = ref
        self.shape = ref.shape

    def fill(self, src):
        q = self.shape[0] // REGROUP
        for r0 in range(REGROUP):
            self.ref[r0 * q:(r0 + 1) * q, :] = src[pl.ds(r0, q, stride=REGROUP), :]

    def drain(self, dst):
        q = self.shape[0] // REGROUP
        for r0 in range(REGROUP):
            dst[pl.ds(r0, q, stride=REGROUP), :] = self.ref[r0 * q:(r0 + 1) * q, :]

    def rows(self, sub, dil):
        nl, r = divmod(sub, dil)
        start = (r % REGROUP) * (self.shape[0] // REGROUP) + r // REGROUP + nl * BAND * (dil // REGROUP)
        return pl.ds(start, BAND, stride=dil // REGROUP)


def _regroups(dil):
    return dil % (4 * REGROUP) == 0


def _rows(ref, sub, dil):
    if isinstance(ref, _Regrouped):
        return ref.ref[ref.rows(sub, dil), :]
    if dil == 1:
        return ref[sub * BAND:(sub + 1) * BAND, :]
    nl, r = divmod(sub, dil)
    return ref[pl.ds(nl * BAND * dil + r, BAND, stride=dil), :]


def _store_rows(ref, sub, dil, val):
    if isinstance(ref, _Regrouped):
        ref.ref[ref.rows(sub, dil), :] = val
    elif dil == 1:
        ref[sub * BAND:(sub + 1) * BAND, :] = val
    else:
        nl, r = divmod(sub, dil)
        ref[pl.ds(nl * BAND * dil + r, BAND, stride=dil), :] = val


def _keys(prev_ref, own_ref, sub, dil):
    nsub = own_ref.shape[0] // BAND
    if sub >= dil:
        prev = _rows(own_ref, sub - dil, dil)
    elif prev_ref is None:
        return _rows(own_ref, sub, dil)
    else:
        prev = _rows(prev_ref, nsub - dil + sub, dil)
    return jnp.concatenate([prev, _rows(own_ref, sub, dil)], axis=0)


def _band_mask(nkeys, has_prev):
    i = lax.broadcasted_iota(jnp.int32, (BAND, nkeys), 0)
    j = lax.broadcasted_iota(jnp.int32, (BAND, nkeys), 1)
    if nkeys == BAND:
        return j <= i
    return (j >= i) & (j <= i + BAND) & (has_prev | (j >= BAND))


def _first_head():
    return lax.broadcasted_iota(jnp.int32, (1, PAIR_W), 1) < HEAD_DIM


def _col(x, hh):
    return x[:, hh * HEAD_DIM:hh * HEAD_DIM + 1]


def _pair_spec(tb, nblk, col0, which):
    def idx(b, p, i):
        if which < 0:
            i = jnp.maximum(i - 1, 0)
        elif which > 0:
            i = jnp.minimum(i + 1, nblk - 1)
        return (b * nblk + i, col0 + p)
    return pl.BlockSpec((tb, PAIR_W), idx)


def dil_fwd(q, k, kv, g, dil, *, name, n_seq):
    M = q.shape[0]
    nsub, tb, nblk = _dil_geometry(dil)
    with_prev = nblk > 1

    regroup = _regroups(dil)
    assert not (regroup and with_prev)

    def body(*refs):
        if with_prev:
            q_ref, ko_ref, vo_ref, kp_ref, vp_ref, o_ref, l_ref = refs
        else:
            (q_ref, ko_ref, vo_ref, o_ref, l_ref), kp_ref, vp_ref = refs[:5], None, None
        outs_to = ()
        if regroup:
            copies = [_Regrouped(s) for s in refs[5:]]
            for c, src in zip(copies, (q_ref, ko_ref, vo_ref)):
                c.fill(src)
            outs_to = ((copies[3], o_ref), (copies[4], l_ref))
            q_ref, ko_ref, vo_ref, o_ref, l_ref = copies
        first = _first_head()
        blk = pl.program_id(2)
        for sub in range(nsub):
            qs = _rows(q_ref, sub, dil) * SCALE
            kc = _keys(kp_ref, ko_ref, sub, dil).astype(bf16)
            vc = _keys(vp_ref, vo_ref, sub, dil).astype(bf16)
            has_prev = True if sub >= dil else blk > 0
            mask = _band_mask(kc.shape[0], has_prev)
            outs, lses = [], []
            for hh in range(2):
                qm = jnp.where(first if hh == 0 else ~first, qs, 0.0).astype(bf16)
                s = jnp.where(mask, lax.dot_general(qm, kc, NT, preferred_element_type=f32), NEG)
                m = jnp.max(s, axis=-1, keepdims=True)
                e = jnp.exp(s - m)
                l = jnp.sum(e, axis=-1, keepdims=True)
                outs.append(jnp.dot(e.astype(bf16), vc, preferred_element_type=f32) * (1.0 / l))
                lses.append(jnp.broadcast_to(m + jnp.log(l), (BAND, PAIR_W)))
            _store_rows(o_ref, sub, dil, jnp.where(first, outs[0], outs[1]))
            _store_rows(l_ref, sub, dil, jnp.where(first, lses[0], lses[1]))
        for c, dst in outs_to:
            c.drain(dst)

    ins = [(q, 2 * g, 0), (k, 2 * g, 0), (kv, 6 + 2 * g, 0)]
    if with_prev:
        ins += [(k, 2 * g, -1), (kv, 6 + 2 * g, -1)]
    out = _pair_spec(tb, nblk, 0, 0)
    return pl.pallas_call(
        body, name=name, grid=(n_seq, 2, nblk),
        in_specs=[_pair_spec(tb, nblk, c, w) for _, c, w in ins],
        out_specs=[out, out],
        out_shape=[_sds((M, GROUP_W), f32)] * 2,
        scratch_shapes=[pltpu.VMEM((tb, PAIR_W), f32)] * (5 if regroup else 0),
        compiler_params=_params(("parallel", "parallel", "arbitrary")),
    )(*[a for a, _, _ in ins])


def combine_fwd(os_, lses, *, name):
    M = os_[0].shape[0]
    tm = min(512, M)

    def body(o0, o1, o2, l0, l1, l2, y_ref):
        ls = [l0[...], l1[...], l2[...]]
        m = jnp.maximum(jnp.maximum(ls[0], ls[1]), ls[2])
        es = [jnp.exp(l - m) for l in ls]
        inv = 1.0 / (es[0] + es[1] + es[2])
        y_ref[...] = jnp.concatenate([o[...] * e * inv for o, e in zip((o0, o1, o2), es)], axis=1).astype(bf16)

    part = pl.BlockSpec((tm, GROUP_W), lambda i: (i, 0))
    return pl.pallas_call(
        body, name=name, grid=(M // tm,),
        in_specs=[part] * 6,
        out_specs=pl.BlockSpec((tm, MAIN_W), lambda i: (i, 0)),
        out_shape=_sds((M, D_MODEL), bf16),
        compiler_params=_params(("parallel",)),
    )(*os_, *lses)


def combine_bwd(dyc, os_, lses, *, name):
    M = os_[0].shape[0]
    tm = min(512, M)

    def body(dy_ref, o0, o1, o2, l0, l1, l2, d0, d1, d2, c0, c1, c2):
        r = lax.broadcasted_iota(jnp.int32, (GROUP_W, GROUP_W), 0) // HEAD_DIM
        c = lax.broadcasted_iota(jnp.int32, (GROUP_W, GROUP_W), 1) // HEAD_DIM
        ones = (r == c).astype(f32)
        dy = dy_ref[...].astype(f32)
        ls = [l0[...], l1[...], l2[...]]
        m = jnp.maximum(jnp.maximum(ls[0], ls[1]), ls[2])
        es = [jnp.exp(l - m) for l in ls]
        inv = 1.0 / (es[0] + es[1] + es[2])
        total = 0.0
        alphas = []
        for g, (o, e, d_ref) in enumerate(zip((o0, o1, o2), es, (d0, d1, d2))):
            a = e * inv
            dyg = dy[:, g * GROUP_W:(g + 1) * GROUP_W]
            d_ref[...] = dyg * a
            dsum = jnp.dot(dyg * o[...], ones, precision=lax.Precision.HIGHEST, preferred_element_type=f32)
            total = total + a * dsum
            alphas.append(a)
        for a, c_ref in zip(alphas, (c0, c1, c2)):
            c_ref[...] = -a * total

    part = pl.BlockSpec((tm, GROUP_W), lambda i: (i, 0))
    outs = pl.pallas_call(
        body, name=name, grid=(M // tm,),
        in_specs=[pl.BlockSpec((tm, MAIN_W), lambda i: (i, 0))] + [part] * 6,
        out_specs=[part] * 6,
        out_shape=[_sds((M, GROUP_W), f32)] * 6,
        compiler_params=_params(("parallel",)),
    )(dyc, *os_, *lses)
    return outs[:3], outs[3:]


def dil_bwd(q, k, kv, do, cc, lse, cos, sin, g, dil, *, name, n_seq, into=None):
    M = q.shape[0]
    nsub = SEQ // BAND
    per_res = nsub // dil
    extra = [] if into is None else [into]

    regroup = _regroups(dil)

    def body(q_ref, k_ref, v_ref, do_ref, c_ref, l_ref, cos_ref, sin_ref, *refs):
        dz_ref, dk_ref, dv_ref, dq_ref, *scratch = refs[len(extra):]
        dq_rows = dq_ref
        outs_to = ()
        if regroup:
            copies = [_Regrouped(s) for s in scratch]
            for c, src in zip(copies, (q_ref, k_ref, v_ref, do_ref, c_ref, l_ref)):
                c.fill(src)
            outs_to = tuple(zip(copies[6:], (dq_rows, dk_ref, dv_ref)))
            q_ref, k_ref, v_ref, do_ref, c_ref, l_ref, dq_ref, dk_ref, dv_ref = copies
        first = _first_head()
        for r in range(dil):
            carry = None
            for nl in range(per_res):
                sub = nl * dil + r
                qs = _rows(q_ref, sub, dil) * SCALE
                dos = _rows(do_ref, sub, dil)
                cs = _rows(c_ref, sub, dil)
                ls = _rows(l_ref, sub, dil)
                kc = _keys(None, k_ref, sub, dil).astype(bf16)
                vc = _keys(None, v_ref, sub, dil).astype(bf16)
                nkeys = kc.shape[0]
                mask = _band_mask(nkeys, True)
                dqs = []
                dkc = jnp.zeros((nkeys, PAIR_W), f32)
                dvc = jnp.zeros((nkeys, PAIR_W), f32)
                for hh in range(2):
                    lm = first if hh == 0 else ~first
                    qm = jnp.where(lm, qs, 0.0).astype(bf16)
                    dom = jnp.where(lm, dos, 0.0).astype(bf16)
                    s = jnp.where(mask, lax.dot_general(qm, kc, NT, preferred_element_type=f32), NEG)
                    p = jnp.exp(s - _col(ls, hh))
                    dp = lax.dot_general(dom, vc, NT, preferred_element_type=f32)
                    ds = (p * (dp + _col(cs, hh))).astype(bf16)
                    dqs.append(jnp.dot(ds, kc, preferred_element_type=f32) * SCALE)
                    dkc = dkc + lax.dot_general(ds, qm, TN, preferred_element_type=f32)
                    dvc = dvc + lax.dot_general(p.astype(bf16), dom, TN, preferred_element_type=f32)
                _store_rows(dq_ref, sub, dil, jnp.where(first, dqs[0], dqs[1]))
                if nkeys == 2 * BAND:
                    _store_rows(dk_ref, sub - dil, dil, carry[0] + dkc[:BAND])
                    _store_rows(dv_ref, sub - dil, dil, carry[1] + dvc[:BAND])
                    carry = (dkc[BAND:], dvc[BAND:])
                else:
                    carry = (dkc, dvc)
            _store_rows(dk_ref, (per_res - 1) * dil + r, dil, carry[0])
            _store_rows(dv_ref, (per_res - 1) * dil + r, dil, carry[1])
        for c, dst in outs_to:
            c.drain(dst)
        d = dq_rows[...]
        dz_ref[...] = (d * cos_ref[...] - _swap_halves(d) * sin_ref[...]).astype(bf16)

    def spec(col0):
        return pl.BlockSpec((SEQ, PAIR_W), lambda b, p: (b, col0 + p))

    tab = pl.BlockSpec((SEQ, PAIR_W), lambda b, p: (b, 0))
    out = spec(0)
    return pl.pallas_call(
        body, name=name, grid=(n_seq, 2),
        in_specs=[spec(2 * g), spec(2 * g), spec(6 + 2 * g), spec(0), spec(0), spec(0), tab, tab]
        + [pl.BlockSpec(memory_space=pl.ANY)] * len(extra),
        out_specs=[spec(2 * g), out, out],
        out_shape=[_sds((M, D_MODEL), bf16)] + [_sds((M, GROUP_W), f32)] * 2,
        input_output_aliases={8: 0} if extra else {},
        scratch_shapes=[pltpu.VMEM((SEQ, PAIR_W), f32)] * (10 if regroup else 1),
        compiler_params=_params(("parallel", "parallel")),
    )(q, k, kv, do, cc, lse, cos, sin, *extra)


def _blockdiag(wp):
    out = jnp.zeros((MAIN_W, MAIN_W), wp.dtype)
    for gi in range(len(POOL_WINDOWS)):
        sl = slice(gi * POOL_GROUP, (gi + 1) * POOL_GROUP)
        out = out.at[sl, sl].set(wp[gi])
    return out


def _unblockdiag(w):
    return jnp.stack([w[gi * POOL_GROUP:(gi + 1) * POOL_GROUP, gi * POOL_GROUP:(gi + 1) * POOL_GROUP]
                      for gi in range(len(POOL_WINDOWS))])


def local_step(x, mem, positions, target, P, layer_weights, kv_weight, emit_grads):
    n_seq = x.shape[0]
    M = n_seq * SEQ
    xs = x.reshape(M, D_MODEL)
    mems = mem.reshape(n_seq * N_MEM, D_MODEL)
    pos = positions.reshape(M, 1).astype(f32)
    cos, sin = rope_tables(pos, name="rope_tables")
    gains = P["norm_gains"]

    def gain(l, k):
        return gains[l, k].reshape(1, D_MODEL)

    saved = []
    kvs = None
    for l in range(DEPTH):
        W, started = layer_weights(l, "mix", xs)
        sv = {"x": xs, "W": W}
        z, h1, *qrot = rms_matmul(xs, gain(l, 0), W["w_in"], name=f"l{l}_in", out_dtype=bf16, after=started,
                                  rope=None if l < N_A_LAYERS else (cos, sin))
        kvm, mn = rms_matmul(mems, P["mem_norm"][l].reshape(1, D_MODEL), W["w_mem_kv"],
                             name=f"l{l}_memkv", out_dtype=bf16)
        sv.update(z=z, h1=h1, kvm=kvm, mn=mn)
        if l < N_A_LAYERS:
            wbd = _blockdiag(P["w_pool"][l].astype(bf16))
            psc = P["pool_scale"][l].reshape(1, MAIN_W)
            p, y_main = pool_fwd(z, wbd, psc, name=f"l{l}_pool")
            sv.update(p=p, wbd=wbd, psc=psc)
        else:
            (qrot,) = qrot
            os_, lses = [], []
            for g, (_, dil) in enumerate(DIL_PATTERNS):
                o, lse = dil_fwd(qrot, kvs["krot"], kvs["kv"], g, dil, name=f"l{l}_dil{g}", n_seq=n_seq)
                os_.append(o)
                lses.append(lse)
            y_main = combine_fwd(os_, lses, name=f"l{l}_comb")
            sv.update(qrot=qrot, os=os_, lses=lses)
        ycat = memattn_fwd(z, kvm, y_main, name=f"l{l}_memattn", n_seq=n_seq)
        y, x1 = matmul_rms_res(ycat, W["w_out"], gain(l, 1), xs, name=f"l{l}_out")
        W.update(layer_weights(l, "gu", x1)[0])
        fg, fu, a, h2 = rms_gate_up(x1, gain(l, 2), W["w_gate_up"], name=f"l{l}_gu")
        W.update(layer_weights(l, "down", a)[0])
        y2, x2, *sq = matmul_rms_res(a, W["w_down"], gain(l, 3), x1, name=f"l{l}_down",
                                     target=target.reshape(M, D_MODEL) if l == DEPTH - 1 else None)
        sv.update(ycat=ycat, y=y, x1=x1, fg=fg, fu=fu, h2=h2, a=a, y2=y2)
        saved.append(sv)
        xs = x2
        if l == N_A_LAYERS - 1:
            w_kv = kv_weight(xs)
            kv, hkv, krot = rms_matmul(xs, P["kv_norm"].reshape(1, D_MODEL), w_kv, name="kv_proj", out_dtype=f32,
                                       transposed=True, rope=(cos, sin))
            kvs = {"kv": kv, "hkv": hkv, "krot": krot, "x": xs, "w_kv": w_kv}

    dx, (sq,) = xs, sq

    G = {"mem_norm": [None] * DEPTH, "norm_gains": [[None] * 4 for _ in range(DEPTH)],
         "pool_scale": [None] * N_A_LAYERS}
    dk_parts = [[] for _ in range(N_GROUPS)]
    dv_parts = [[] for _ in range(N_GROUPS)]
    emitted = None

    for l in reversed(range(DEPTH)):
        sv = saved[l]
        W = sv["W"]
        gw = {}
        dy2, dgu, G["norm_gains"][l][3] = down_bwd(sv["y2"], gain(l, 3), dx, W["w_down"], sv["fg"], sv["fu"],
                                                   name=f"l{l}_b_dgu", after=emitted)
        gw["w_down"] = matmul(sv["a"], dy2, TN, name=f"l{l}_b_wd", out_dtype=bf16)
        gw["w_gate_up"] = matmul(dgu, sv["h2"], TN, name=f"l{l}_b_wgu", out_dtype=bf16)
        emitted = emit_grads(l, "ffn", gw)
        dx1, G["norm_gains"][l][2] = matmul_rms_bwd(dgu, W["w_gate_up"], NN, sv["x1"], gain(l, 2), dx,
                                                    name=f"l{l}_b_dh2", after=emitted)
        gw = {}
        dy, dycat, G["norm_gains"][l][1] = rms_bwd_matmul(sv["y"], gain(l, 1), dx1, W["w_out"], NT,
                                                          name=f"l{l}_b_dycat", after=emitted)
        gw["w_out"] = matmul(sv["ycat"], dy, TN, name=f"l{l}_b_wout", out_dtype=bf16)
        if l < N_A_LAYERS:
            dz, dwbd, dps = pool_bwd(dycat, sv["p"], sv["wbd"], sv["psc"], name=f"l{l}_b_pool")
            gw["w_pool"] = _unblockdiag(dwbd).reshape(MAIN_W, POOL_GROUP).astype(bf16)
            G["pool_scale"][l] = dps.reshape(MAIN_W)
        else:
            dos, ccs = combine_bwd(dycat, sv["os"], sv["lses"], name=f"l{l}_b_comb")
            dz = None
            for g, (_, dil) in enumerate(DIL_PATTERNS):
                args = (sv["qrot"], kvs["krot"], kvs["kv"], dos[g], ccs[g], sv["lses"][g], cos, sin, g, dil)
                dz, dk, dv = dil_bwd(*args, name=f"l{l}_b_dil{g}", n_seq=n_seq, into=dz)
                dk_parts[g].append(dk)
                dv_parts[g].append(dv)
        dz, dkvm = memattn_bwd(sv["z"], sv["kvm"], dycat, dz, name=f"l{l}_b_memattn", n_seq=n_seq)
        gw["w_mem_kv"] = matmul(sv["mn"], dkvm, TN, name=f"l{l}_b_wmkv", out_dtype=bf16)
        _, G["mem_norm"][l] = matmul_rms_bwd(dkvm, W["w_mem_kv"], NT, mems, P["mem_norm"][l].reshape(1, D_MODEL),
                                             mems, name=f"l{l}_b_dmn")
        gw["w_in"] = matmul(sv["h1"], dz, TN, name=f"l{l}_b_win", out_dtype=bf16)
        if l != N_A_LAYERS:
            emitted = emit_grads(l, "mix", gw)
        dx, G["norm_gains"][l][0] = matmul_rms_bwd(dz, W["w_in"], NT, sv["x"], gain(l, 0), dx1, name=f"l{l}_b_dh1",
                                                   after=emitted)
        if l == N_A_LAYERS:
            dkv = group_sum(dk_parts, cos, sin, name="b_ropek", rotate=True, width=2 * MAIN_W)
            dkv = group_sum(dv_parts, cos, sin, name="b_sumv", rotate=False, width=2 * MAIN_W, col_block=1, into=dkv)
            gw["w_kv"] = matmul(dkv, kvs["hkv"], TN, name="b_wkv", out_dtype=bf16)
            dx, gkn = matmul_rms_bwd(dkv, kvs["w_kv"], NN, kvs["x"], P["kv_norm"].reshape(1, D_MODEL), dx,
                                     name="b_dhkv")
            G["kv_norm"] = gkn.reshape(D_MODEL)
            emitted = emit_grads(l, "mix", gw)

    small = {"pool_scale": jnp.stack(G["pool_scale"]),
             "mem_norm": jnp.concatenate(G["mem_norm"], axis=0),
             "norm_gains": jnp.stack([jnp.concatenate(r, axis=0) for r in G["norm_gains"]]),
             "kv_norm": G["kv_norm"]}
    return sq[0, 0], dx.reshape(n_seq, SEQ, D_MODEL), small, emitted


def _peer(k):
    x, y, c = lax.axis_index("x"), lax.axis_index("y"), lax.axis_index("c")
    px = 1 - x if k & 4 else x
    py = 1 - y if k & 2 else y
    pc = 1 - c if k & 1 else c
    return (px, py, pc), 4 * px + 2 * py + pc


def _my_index():
    return 4 * lax.axis_index("x") + 2 * lax.axis_index("y") + lax.axis_index("c")


def _src_for(kinds, in_refs, i, idx):
    return in_refs[i] if kinds[i] == "gather" else in_refs[i].at[idx]


def _local_copies(kinds, in_refs, out_refs, local_sems):
    me = _my_index()
    return [pltpu.make_async_copy(_src_for(kinds, in_refs, i, me), out_refs[i].at[me], local_sems.at[i])
            for i in range(len(kinds))]


def _remote_copies(kinds, in_refs, out_refs, send_sems, recv_sems, *, arriving):
    me = _my_index()
    copies = []
    for k in range(1, N_DEV):
        dev, idx = _peer(k)
        for i in range(len(kinds)):
            j = i * (N_DEV - 1) + k - 1
            copies.append(pltpu.make_async_remote_copy(
                src_ref=_src_for(kinds, in_refs, i, idx), dst_ref=out_refs[i].at[idx if arriving else me],
                send_sem=send_sems.at[j], recv_sem=recv_sems.at[j], device_id=dev, device_id_type=MESH))
    return copies


def _out_shape(a, kind):
    return ((N_DEV,) + a.shape) if kind == "gather" else a.shape


def exchange(items, *, name, after=()):
    n = len(items)
    kinds = [k for _, k in items]
    after = list(after)

    def body(*refs):
        in_refs, out_refs = refs[:n], refs[n + len(after):2 * n + len(after)]
        send_sems, recv_sems, local_sems = refs[-3:]
        local = _local_copies(kinds, in_refs, out_refs, local_sems)
        sends = _remote_copies(kinds, in_refs, out_refs, send_sems, recv_sems, arriving=False)
        for cp in local + sends:
            cp.start()
        for cp in _remote_copies(kinds, in_refs, out_refs, send_sems, recv_sems, arriving=True):
            cp.wait_recv()
        for cp in sends:
            cp.wait_send()
        for cp in local:
            cp.wait()

    any_spec = pl.BlockSpec(memory_space=pl.ANY)
    return pl.pallas_call(
        body, name=name,
        in_specs=[any_spec] * (n + len(after)), out_specs=[any_spec] * n,
        out_shape=[_sds(_out_shape(a, k), a.dtype) for a, k in items],
        scratch_shapes=[pltpu.SemaphoreType.DMA((n * (N_DEV - 1),)), pltpu.SemaphoreType.DMA((n * (N_DEV - 1),)),
                        pltpu.SemaphoreType.DMA((n,))],
    )(*[a for a, _ in items], *after)


_HBM = pl.BlockSpec(memory_space=pltpu.HBM)
_SEM = pl.BlockSpec(memory_space=pltpu.SEMAPHORE)
_EFFECT = pltpu.SideEffectType.DATAFLOW_SIDE_EFFECTING


def exchange_start(items, after, *, name):
    n = len(items)
    kinds = [k for _, k in items]

    def body(*refs):
        in_refs, land_refs = refs[:n], refs[n:2 * n]
        send_sems, recv_sems, local_sems = refs[2 * n + 1:2 * n + 4]
        token = refs[-1]
        for cp in (_local_copies(kinds, in_refs, land_refs, local_sems)
                   + _remote_copies(kinds, in_refs, land_refs, send_sems, recv_sems, arriving=False)):
            cp.start()
        token[...] = jnp.zeros_like(token)

    srcs = [pltpu.with_memory_space_constraint(a, pltpu.HBM) for a, _ in items]
    lands = [pltpu.with_memory_space_constraint(lax.empty(_out_shape(a, k), a.dtype), pltpu.HBM) for a, k in items]
    outs = pl.pallas_call(
        body, name=name,
        out_shape=(pltpu.SemaphoreType.DMA((n * (N_DEV - 1),)), pltpu.SemaphoreType.DMA((n * (N_DEV - 1),)),
                   pltpu.SemaphoreType.DMA((n,)),
                   *[pltpu.HBM(a.shape, a.dtype) for a in srcs], *[pltpu.HBM(a.shape, a.dtype) for a in lands],
                   _sds((8, 128), f32)),
        in_specs=[_HBM] * (2 * n) + [pl.BlockSpec(memory_space=pl.ANY)],
        out_specs=(_SEM, _SEM, _SEM, *[_HBM] * (2 * n), pl.BlockSpec(memory_space=pltpu.VMEM)),
        input_output_aliases={i: 3 + i for i in range(2 * n)},
        compiler_params=pltpu.CompilerParams(has_side_effects=_EFFECT),
    )(*srcs, *lands, after)
    return {"kinds": kinds, "sems": outs[:3], "srcs": outs[3:3 + n], "lands": outs[3 + n:3 + 2 * n], "token": outs[-1]}


def exchange_wait(handle, after, *, name):
    kinds = handle["kinds"]
    n = len(kinds)

    def body(*refs):
        in_refs, land_refs = refs[:n], refs[n:2 * n]
        send_sems, recv_sems, local_sems = refs[2 * n:2 * n + 3]
        for cp in _remote_copies(kinds, in_refs, land_refs, send_sems, recv_sems, arriving=True):
            cp.wait_recv()
        for cp in _remote_copies(kinds, in_refs, land_refs, send_sems, recv_sems, arriving=False):
            cp.wait_send()
        for cp in _local_copies(kinds, in_refs, land_refs, local_sems):
            cp.wait()

    srcs, lands = list(handle["srcs"]), list(handle["lands"])
    after = list(after) if isinstance(after, (list, tuple)) else [after]
    outs = pl.pallas_call(
        body, name=name,
        out_shape=tuple(pltpu.HBM(a.shape, a.dtype) for a in srcs + lands),
        in_specs=[_HBM] * (2 * n) + [_SEM] * 3 + [pl.BlockSpec(memory_space=pl.ANY)] * len(after),
        out_specs=tuple([_HBM] * (2 * n)),
        input_output_aliases={i: i for i in range(2 * n)},
        compiler_params=pltpu.CompilerParams(has_side_effects=_EFFECT),
    )(*srcs, *lands, *handle["sems"], *after)
    return list(outs[n:])


CHIP_MASKS = (2, 4, 6)


def _g2_first(in_refs, land_refs, send_sems, recv_sems, *, masks, arriving):
    me = _my_index()
    copies = []
    for i in range(len(land_refs)):
        for j, k in enumerate(masks):
            dev, idx = _peer(k)
            dst = land_refs[i].at[idx if arriving else me]
            copies.append(pltpu.make_async_remote_copy(
                src_ref=dst if in_refs is None else in_refs[i], dst_ref=dst,
                send_sem=send_sems.at[i * len(masks) + j], recv_sem=recv_sems.at[i * len(masks) + j],
                device_id=dev, device_id_type=MESH))
    return copies


def _g2_forward(land_refs, fwd_send, fwd_recv, *, arriving):
    sibling, _ = _peer(1)
    copies = []
    for i in range(len(land_refs)):
        for j, k in enumerate(CHIP_MASKS):
            _, idx = _peer(k | 1 if arriving else k)
            copies.append(pltpu.make_async_remote_copy(
                src_ref=land_refs[i].at[idx], dst_ref=land_refs[i].at[idx],
                send_sem=fwd_send.at[i * 3 + j], recv_sem=fwd_recv.at[i * 3 + j], device_id=sibling,
                device_id_type=MESH))
    return copies


def gather2_start(arrays, after, *, name):
    n = len(arrays)

    def body(*refs):
        in_refs, land_refs = refs[:n], refs[n:2 * n]
        ici_send, ici_recv, d2d_send, d2d_recv, local_sems = refs[2 * n + 1:2 * n + 6]
        token = refs[-1]
        ici = _g2_first(in_refs, land_refs, ici_send, ici_recv, masks=CHIP_MASKS, arriving=False)
        d2d = _g2_first(in_refs, land_refs, d2d_send, d2d_recv, masks=(1,), arriving=False)
        for cp in _local_copies(["gather"] * n, in_refs, land_refs, local_sems) + ici + d2d:
            cp.start()
        token[...] = jnp.zeros_like(token)

    srcs = [pltpu.with_memory_space_constraint(a, pltpu.HBM) for a in arrays]
    lands = [pltpu.with_memory_space_constraint(lax.empty((N_DEV,) + a.shape, a.dtype), pltpu.HBM) for a in arrays]
    sem = pltpu.SemaphoreType.DMA
    outs = pl.pallas_call(
        body, name=name,
        out_shape=(sem((3 * n,)), sem((3 * n,)), sem((n,)), sem((n,)), sem((n,)),
                   *[pltpu.HBM(a.shape, a.dtype) for a in srcs], *[pltpu.HBM(a.shape, a.dtype) for a in lands],
                   _sds((8, 128), f32)),
        in_specs=[_HBM] * (2 * n) + [pl.BlockSpec(memory_space=pl.ANY)],
        out_specs=(*[_SEM] * 5, *[_HBM] * (2 * n), pl.BlockSpec(memory_space=pltpu.VMEM)),
        input_output_aliases={i: 5 + i for i in range(2 * n)},
        compiler_params=pltpu.CompilerParams(has_side_effects=_EFFECT),
    )(*srcs, *lands, after)
    return {"n": n, "sems": outs[:5], "srcs": outs[5:5 + n], "lands": outs[5 + n:5 + 2 * n], "token": outs[-1]}


def gather2_forward(handle, after, *, name):
    n = handle["n"]

    def body(*refs):
        land_refs = refs[:n]
        ici_recv = refs[n]
        fwd_send, fwd_recv = refs[n + 2:n + 4]
        for cp in _g2_first(None, land_refs, fwd_send, ici_recv, masks=CHIP_MASKS, arriving=True):
            cp.wait_recv()
        for cp in _g2_forward(land_refs, fwd_send, fwd_recv, arriving=False):
            cp.start()

    lands = list(handle["lands"])
    sem = pltpu.SemaphoreType.DMA
    outs = pl.pallas_call(
        body, name=name,
        out_shape=(sem((3 * n,)), sem((3 * n,)), *[pltpu.HBM(a.shape, a.dtype) for a in lands]),
        in_specs=[_HBM] * n + [_SEM, pl.BlockSpec(memory_space=pl.ANY)],
        out_specs=(_SEM, _SEM, *[_HBM] * n),
        input_output_aliases={i: 2 + i for i in range(n)},
        compiler_params=pltpu.CompilerParams(has_side_effects=_EFFECT),
    )(*lands, handle["sems"][1], after)
    return dict(handle, fwd=outs[:2], lands=outs[2:])


def gather2_wait(handle, after, *, name):
    n = handle["n"]

    def body(*refs):
        in_refs, land_refs = refs[:n], refs[n:2 * n]
        ici_send, d2d_send, d2d_recv, local_sems, fwd_send, fwd_recv = refs[2 * n:2 * n + 6]
        for cp in _g2_first(in_refs, land_refs, d2d_send, d2d_recv, masks=(1,), arriving=True):
            cp.wait_recv()
        for cp in _g2_forward(land_refs, fwd_send, fwd_recv, arriving=True):
            cp.wait_recv()
        for cp in (_g2_first(in_refs, land_refs, ici_send, fwd_recv, masks=CHIP_MASKS, arriving=False)
                   + _g2_first(in_refs, land_refs, d2d_send, d2d_recv, masks=(1,), arriving=False)
                   + _g2_forward(land_refs, fwd_send, fwd_recv, arriving=False)):
            cp.wait_send()
        for cp in _local_copies(["gather"] * n, in_refs, land_refs, local_sems):
            cp.wait()

    srcs, lands = list(handle["srcs"]), list(handle["lands"])
    s = handle["sems"]
    outs = pl.pallas_call(
        body, name=name,
        out_shape=tuple(pltpu.HBM(a.shape, a.dtype) for a in srcs + lands),
        in_specs=[_HBM] * (2 * n) + [_SEM] * 6 + [pl.BlockSpec(memory_space=pl.ANY)],
        out_specs=tuple([_HBM] * (2 * n)),
        input_output_aliases={i: i for i in range(2 * n)},
        compiler_params=pltpu.CompilerParams(has_side_effects=_EFFECT),
    )(*srcs, *lands, s[0], s[2], s[3], s[4], *handle["fwd"], after)
    return list(outs[n:])


def adamw(entries, *, name):
    c1 = 1.0 - ADAM_B1 ** ADAM_STEP
    c2 = 1.0 - ADAM_B2 ** ADAM_STEP
    tiles = [_tile(w.shape[-2], (64, 32, 16, 8)) for _, w, _, _, _, _ in entries]
    steps = [w.shape[-2] // tr for (_, w, _, _, _, _), tr in zip(entries, tiles)]
    n = len(entries)

    def body(*refs):
        i = pl.program_id(0)
        for e in range(n):
            s_ref, w_ref, m_ref, v_ref = refs[4 * e:4 * e + 4]
            g_ref, d_ref, m2_ref, v2_ref = refs[len(refs) - 4 * n + 4 * e:len(refs) - 4 * n + 4 * e + 4]

            @pl.when(i < steps[e])
            def _():
                g = s_ref[0].astype(f32)
                for d in range(1, N_DEV):
                    g = g + s_ref[d].astype(f32)
                m2 = ADAM_B1 * m_ref[...] + (1.0 - ADAM_B1) * g
                v2 = ADAM_B2 * v_ref[...] + (1.0 - ADAM_B2) * (g * g)
                g_ref[...] = g
                m2_ref[...] = m2
                v2_ref[...] = v2
                d_ref[...] = -ADAM_LR * ((m2 / c1) / (jnp.sqrt(v2 / c2) + ADAM_EPS) + ADAM_WD * w_ref[...])

    in_specs, out_specs, out_shape, args, extras, aliases = [], [], [], [], [], {}
    for e, ((slots, w, m, v, layer, into), tr, ns) in enumerate(zip(entries, tiles, steps)):
        C = w.shape[-1]
        row = lambda i, ns=ns: jnp.minimum(i, ns - 1)
        if layer is None:
            blk = pl.BlockSpec((tr, C), lambda i, row=row: (row(i), 0))
        else:
            blk = pl.BlockSpec((None, tr, C), lambda i, row=row, layer=layer: (layer, row(i), 0))
        in_specs += [pl.BlockSpec((N_DEV, tr, C), lambda i, row=row: (0, row(i), 0)), blk, blk, blk]
        args += [slots, w, m, v]
        out_specs += [blk] * 4
        out_shape += [_sds(w.shape, f32)] * 4
        if into is not None:
            for t, a in enumerate(into):
                aliases[4 * n + len(extras)] = 4 * e + t
                extras.append(a)
    outs = pl.pallas_call(
        body, name=name, grid=(max(steps),),
        in_specs=in_specs + [pl.BlockSpec(memory_space=pl.ANY)] * len(extras),
        out_specs=out_specs, out_shape=out_shape, input_output_aliases=aliases,
        compiler_params=_params(("arbitrary",)),
    )(*args, *extras)
    return [outs[4 * e:4 * e + 4] for e in range(n)]


WEIGHTS = ("norm_gains", "mem_norm", "w_in", "w_mem_kv", "w_out", "w_pool", "pool_scale", "kv_norm", "w_kv",
           "w_gate_up", "w_down")
LAYER_MATS = ("w_in", "w_mem_kv", "w_out", "w_gate_up", "w_down")
POOL_SHARD = MAIN_W // N_DEV
KV_SHARD = 2 * MAIN_W // N_DEV
LOOKAHEAD = 2
TWO_LEVEL_LAYERS = (0, 1)


def _pack_small(gains, pscale):
    lead = gains.shape[:-3]
    g = gains.reshape(lead + (16, 128))
    p = jnp.zeros(lead + (8, 128), f32).at[..., :2, :POOL_SHARD].set(pscale)
    return jnp.concatenate([g, p], axis=-2)


def _unpack_small(a):
    return a[:16].reshape(4, 4, 128), a[16:18, :POOL_SHARD]


def _pack_repl(mem_norm, kv_norm):
    return jnp.concatenate([mem_norm, kv_norm.reshape(1, D_MODEL), jnp.zeros((3, D_MODEL), f32)], axis=0)


def _unpack_repl(a):
    return a[:4], a[4]


def kernel(x, mem, positions, norm_gains, mem_norm, w_in, w_mem_kv, w_out, w_pool, pool_scale, kv_norm, w_kv, w_gate_up, w_down, loss_target, m_norm_gains, m_mem_norm, m_w_in, m_w_mem_kv, m_w_out, m_w_pool, m_pool_scale, m_kv_norm, m_w_kv, m_w_gate_up, m_w_down, v_norm_gains, v_mem_norm, v_w_in, v_w_mem_kv, v_w_out, v_w_pool, v_pool_scale, v_kv_norm, v_w_kv, v_w_gate_up, v_w_down):
    w = dict(norm_gains=norm_gains, mem_norm=mem_norm, w_in=w_in, w_mem_kv=w_mem_kv, w_out=w_out, w_pool=w_pool,
             pool_scale=pool_scale, kv_norm=kv_norm, w_kv=w_kv, w_gate_up=w_gate_up, w_down=w_down)
    m = dict(norm_gains=m_norm_gains, mem_norm=m_mem_norm, w_in=m_w_in, w_mem_kv=m_w_mem_kv, w_out=m_w_out,
             w_pool=m_w_pool, pool_scale=m_pool_scale, kv_norm=m_kv_norm, w_kv=m_w_kv, w_gate_up=m_w_gate_up,
             w_down=m_w_down)
    v = dict(norm_gains=v_norm_gains, mem_norm=v_mem_norm, w_in=v_w_in, w_mem_kv=v_w_mem_kv, w_out=v_w_out,
             w_pool=v_w_pool, pool_scale=v_pool_scale, kv_norm=v_kv_norm, w_kv=v_w_kv, w_gate_up=v_w_gate_up,
             w_down=v_w_down)

    def transposed_view(d):
        d = dict(d)
        d["w_gate_up"] = jnp.swapaxes(d["w_gate_up"], 1, 2)
        d["w_kv"] = jnp.swapaxes(d["w_kv"], 0, 1)
        return d

    wv, mv, vv = transposed_view(w), transposed_view(m), transposed_view(v)

    small = _pack_small(norm_gains, pool_scale)
    (gsmall,) = exchange([(small, "gather")], name="gather_small")
    P = {"norm_gains": jnp.moveaxis(gsmall[:, :16].reshape(N_DEV, 4, 4, 128), 0, 2).reshape(4, 4, D_MODEL),
         "pool_scale": jnp.moveaxis(gsmall[:, 16:18, :POOL_SHARD], 0, 1).reshape(2, MAIN_W),
         "mem_norm": mem_norm, "kv_norm": kv_norm, "w_pool": w_pool}

    PARTS = {"mix": ("w_in", "w_mem_kv", "w_out"), "ffn": ("w_gate_up", "w_down"), "gu": ("w_gate_up",),
             "down": ("w_down",), "all": ("w_in", "w_mem_kv", "w_out", "w_gate_up", "w_down")}

    def parts_of(l):
        return (("mix", "gu", "down"), ("mix", "ffn"))[l] if l < 2 else ("all",)

    def part_items(l, part):
        items = [(wv[k][l].astype(bf16), "gather") for k in PARTS[part]]
        if part == "ffn" and l == N_A_LAYERS - 1:
            items.append((wv["w_kv"].astype(bf16), "gather"))
        return items

    handles = {}

    def start_layer(l, after):
        for part in parts_of(l):
            if l in TWO_LEVEL_LAYERS:
                handles[l, part] = gather2_start([a for a, _ in part_items(l, part)], after,
                                                 name=f"gather_start_{part}_l{l}")
            else:
                handles[l, part] = exchange_start(part_items(l, part), after, name=f"gather_start_{part}_l{l}")
            after = handles[l, part]["token"]
        return after

    token = gsmall
    for l in range(LOOKAHEAD):
        token = start_layer(l, token)
    landed = {}

    def layer_weights(l, part, after):
        if part not in parts_of(l):
            if part == "down" or (part == "gu" and "all" in parts_of(l)):
                return {}, None
            part = "all" if "all" in parts_of(l) else "ffn"
        if l == 0 and part == "mix":
            after = token
        if l in TWO_LEVEL_LAYERS:
            passed = gather2_forward(handles[l, part], after, name=f"gather_forward_{part}_l{l}")
            got = gather2_wait(passed, after, name=f"gather_wait_{part}_l{l}")
        else:
            got = exchange_wait(handles[l, part], after, name=f"gather_wait_{part}_l{l}")
        landed[l, part] = got
        started = None
        if part in ("mix", "all") and l + LOOKAHEAD < DEPTH:
            started = start_layer(l + LOOKAHEAD, got[0])
        W = {k: g.reshape(-1, g.shape[-1]) for k, g in zip(PARTS[part], got)}
        return W, started

    def kv_weight(after):
        g = landed[N_A_LAYERS - 1, "ffn"][len(PARTS["ffn"])]
        return g.reshape(2 * MAIN_W, D_MODEL)

    ghandles = {}

    pending = {}

    def gparts_of(l):
        return ("ffn", "mix") if l < 2 else ("all",)

    def emit_grads(l, part, gw):
        if part not in gparts_of(l):
            pending.setdefault(l, {}).update(gw)
            if part == "ffn":
                return None
            gw, part = pending[l], "all"
        items = [(gw[k].reshape((N_DEV, -1) + gw[k].shape[-1:]), "scatter") for k in PARTS[part]]
        if part != "ffn" and l == N_A_LAYERS:
            items.append((gw["w_kv"].reshape(N_DEV, KV_SHARD, D_MODEL), "scatter"))
        if part != "ffn" and l < N_A_LAYERS:
            items.append((gw["w_pool"], "gather"))
        ghandles[l, part] = exchange_start(items, gsmall, name=f"scatter_start_{part}_l{l}")
        return ghandles[l, part]["token"]

    sq, grad_x, GS, emitted = local_step(x, mem, positions, loss_target, P, layer_weights, kv_weight, emit_grads)

    def pool3(a):
        return a.reshape(N_A_LAYERS, MAIN_W, POOL_GROUP)

    out = {}
    after = [emitted]

    def finish_layer(l, after):
        for part in gparts_of(l):
            got = exchange_wait(ghandles[l, part], after, name=f"scatter_wait_{part}_l{l}")
            names = list(PARTS[part])
            entries = [(slots, wv[k], mv[k], vv[k], l, out.get(k)) for k, slots in zip(names, got)]
            if part != "ffn" and l == N_A_LAYERS:
                names.append("w_kv")
                entries.append((got[-1], wv["w_kv"], mv["w_kv"], vv["w_kv"], None, None))
            if part != "ffn" and l < N_A_LAYERS:
                names.append("w_pool")
                entries.append((got[-1], pool3(w_pool), pool3(m_w_pool), pool3(v_w_pool), l, out.get("w_pool")))
            out.update(zip(names, adamw(entries, name=f"adamw_{part}_l{l}")))
            after = [out[k][0] for k in names]
        return after

    for l in reversed(range(1, DEPTH)):
        after = finish_layer(l, after)

    gs = _pack_small(jnp.moveaxis(GS["norm_gains"].reshape(4, 4, N_DEV, 128), 2, 0),
                     jnp.moveaxis(GS["pool_scale"].reshape(2, N_DEV, POOL_SHARD), 1, 0))
    parts_small, parts_repl, parts_sq = exchange(
        [(gs, "scatter"), (_pack_repl(GS["mem_norm"], GS["kv_norm"]), "gather"),
         (jnp.full((8, 128), sq, f32), "gather")],
        name="exchange_small_grads", after=after)
    loss = (0.5 / D_MODEL) * jnp.sum(parts_sq[:, 0, 0])
    finish_layer(0, [parts_small])
    out["w_gate_up"] = [jnp.swapaxes(r, 1, 2) for r in out["w_gate_up"]]
    out["w_kv"] = [jnp.swapaxes(r, 0, 1) for r in out["w_kv"]]
    out["w_pool"] = [r.reshape(w_pool.shape) for r in out["w_pool"]]

    res_small, res_repl = adamw(
        [(parts_small, small, _pack_small(m_norm_gains, m_pool_scale), _pack_small(v_norm_gains, v_pool_scale),
          None, None),
         (parts_repl, _pack_repl(mem_norm, kv_norm), _pack_repl(m_mem_norm, m_kv_norm),
          _pack_repl(v_mem_norm, v_kv_norm), None, None)], name="adamw_small")
    out["norm_gains"], out["pool_scale"] = zip(*[_unpack_small(r) for r in res_small])
    out["mem_norm"], out["kv_norm"] = zip(*[_unpack_repl(r) for r in res_repl])

    return (loss, grad_x, *[out[k][0] for k in WEIGHTS], *[out[k][1] for k in WEIGHTS],
            *[out[k][2] for k in WEIGHTS], *[out[k][3] for k in WEIGHTS])
```

```python
import numpy as np
import jax
import jax.numpy as jnp
from jax import lax
from jax.experimental import pallas as pl
from jax.experimental.pallas import tpu as pltpu

f32 = jnp.float32
bf16 = jnp.bfloat16

D_MODEL = 1024
SEQ = 2048
DEPTH = 4
N_MEM = 256
HEAD_DIM = 64
N_MEM_HEADS = 4
MEM_W = 256
MAIN_W = 768
POOL_WINDOWS = (2, 4, 8, 16)
POOL_GROUP = 192
POOL_HALO = 16
DIL_PATTERNS = ((128, 1), (512, 4), (2048, 16))
N_GROUPS = 3
GROUP_W = 256
BAND = 128
N_A_LAYERS = 2
D_FF = 2816
ROPE_THETA = 10000.0
EPS = 1e-6
NEG = -1e30
SCALE = HEAD_DIM ** -0.5
N_DEV = 8

ADAM_LR = 0.001
ADAM_B1 = 0.9
ADAM_B2 = 0.999
ADAM_EPS = 1e-08
ADAM_WD = 0.01
ADAM_STEP = 10

VMEM_LIMIT_BYTES = 56 * 1024 * 1024
MESH = pl.DeviceIdType.MESH

NN = (((1,), (0,)), ((), ()))
NT = (((1,), (1,)), ((), ()))
TN = (((0,), (0,)), ((), ()))


def _params(sem=None):
    return pltpu.CompilerParams(dimension_semantics=sem, vmem_limit_bytes=VMEM_LIMIT_BYTES)


def _tile(n, cands):
    for c in cands:
        if n % c == 0:
            return c
    return n


def _sds(shape, dtype):
    return jax.ShapeDtypeStruct(tuple(shape), dtype)


def _rms_r(v):
    return lax.rsqrt(jnp.mean(v * v, axis=-1, keepdims=True) + EPS)


def rms_matmul(x, gain, w, *, name, out_dtype, transposed=False, after=None, rope=None):
    M, K = x.shape
    N = w.shape[0] if transposed else w.shape[1]
    tm = min(512, M)
    order = [] if after is None else [after]
    tables = [] if rope is None else list(rope)
    rot_spec = [] if rope is None else [pl.BlockSpec((tm, MAIN_W), lambda i: (i, 0))]
    rot_shape = [] if rope is None else [_sds((M, MAIN_W), f32)]

    def body(x_ref, g_ref, w_ref, *refs):
        z_ref, h_ref = refs[len(tables) + len(order):][:2]
        xv = x_ref[...]
        h = (xv * _rms_r(xv) * g_ref[...]).astype(bf16)
        h_ref[...] = h
        z = lax.dot_general(h, w_ref[...], NT if transposed else NN, preferred_element_type=f32)
        z_ref[...] = z.astype(z_ref.dtype)
        if tables:
            c = jnp.tile(refs[0][...], (1, MAIN_W // 128))
            s = jnp.tile(refs[1][...], (1, MAIN_W // 128))
            zr = z[:, :MAIN_W]
            refs[-1][...] = zr * c + _swap_halves(zr) * s

    tab = pl.BlockSpec((tm, 128), lambda i: (i, 0))
    return pl.pallas_call(
        body, name=name, grid=(M // tm,),
        in_specs=[pl.BlockSpec((tm, K), lambda i: (i, 0)),
                  pl.BlockSpec((1, K), lambda i: (0, 0)),
                  pl.BlockSpec(w.shape, lambda i: (0, 0))] + [tab] * len(tables)
        + [pl.BlockSpec(memory_space=pl.ANY)] * len(order),
        out_specs=[pl.BlockSpec((tm, N), lambda i: (i, 0)), pl.BlockSpec((tm, K), lambda i: (i, 0))] + rot_spec,
        out_shape=[_sds((M, N), out_dtype), _sds((M, K), bf16)] + rot_shape,
        compiler_params=_params(("parallel",)),
    )(x, gain, w, *tables, *order)


def matmul_rms_res(a, w, gain, res, *, name, target=None):
    M, K = a.shape
    N = w.shape[1]
    tm = min(512, M)
    goal = [] if target is None else [target]

    def body(a_ref, w_ref, g_ref, r_ref, *refs):
        y_ref, x_ref = refs[len(goal):][:2]
        y = jnp.dot(a_ref[...], w_ref[...], preferred_element_type=f32)
        y_ref[...] = y.astype(bf16)
        x = r_ref[...] + y * _rms_r(y) * g_ref[...]
        if not goal:
            x_ref[...] = x
            return
        e = x - refs[0][...]
        x_ref[...] = e * (1.0 / N)
        _accumulate(refs[-1], jnp.sum(jnp.sum(e * e, axis=0, keepdims=True), axis=1, keepdims=True))

    row = pl.BlockSpec((tm, N), lambda i: (i, 0))
    return pl.pallas_call(
        body, name=name, grid=(M // tm,),
        in_specs=[pl.BlockSpec((tm, K), lambda i: (i, 0)),
                  pl.BlockSpec((K, N), lambda i: (0, 0)),
                  pl.BlockSpec((1, N), lambda i: (0, 0)),
                  row] + [row] * len(goal),
        out_specs=[row, row] + [pl.BlockSpec((8, 128), lambda i: (0, 0))] * len(goal),
        out_shape=[_sds((M, N), bf16), _sds((M, N), f32)] + [_sds((8, 128), f32)] * len(goal),
        compiler_params=_params(("arbitrary",) if goal else ("parallel",)),
    )(a, w, gain, res, *goal)


def matmul(a, b, dims, *, name, out_dtype):
    if dims is TN:
        K, M = a.shape
        tm = _tile(M, (512, 256, 128))
        a_spec = pl.BlockSpec((K, tm), lambda i: (0, i))
    else:
        M, K = a.shape
        tm = _tile(M, (1024, 512, 256, 128))
        a_spec = pl.BlockSpec((tm, K), lambda i: (i, 0))
    N = b.shape[0] if dims is NT else b.shape[1]

    def body(a_ref, b_ref, o_ref):
        o_ref[...] = lax.dot_general(a_ref[...].astype(bf16), b_ref[...].astype(bf16), dims,
                                     preferred_element_type=f32).astype(o_ref.dtype)

    return pl.pallas_call(
        body, name=name, grid=(M // tm,),
        in_specs=[a_spec, pl.BlockSpec(b.shape, lambda i: (0, 0))],
        out_specs=pl.BlockSpec((tm, N), lambda i: (i, 0)),
        out_shape=_sds((M, N), out_dtype),
        compiler_params=_params(("parallel",)),
    )(a, b)


def rms_gate_up(x, gain, wt, *, name):
    M, K = x.shape
    tm = min(2048, M)
    tn = _tile(D_FF, (256, 128))
    nj = D_FF // tn

    def body(x_ref, gn_ref, wg_ref, wu_ref, g_ref, u_ref, a_ref, h_ref):
        @pl.when(pl.program_id(1) == 0)
        def _():
            xv = x_ref[...]
            h_ref[...] = (xv * _rms_r(xv) * gn_ref[...]).astype(bf16)

        h = h_ref[...]
        g = lax.dot_general(h, wg_ref[...], NT, preferred_element_type=f32).astype(bf16)
        u = lax.dot_general(h, wu_ref[...], NT, preferred_element_type=f32).astype(bf16)
        g_ref[...] = g
        u_ref[...] = u
        a_ref[...] = g * (1.0 / (1.0 + jnp.exp(-g))) * u

    col = pl.BlockSpec((tm, tn), lambda i, j: (i, j))
    return pl.pallas_call(
        body, name=name, grid=(M // tm, nj),
        in_specs=[pl.BlockSpec((tm, K), lambda i, j: (i, 0)),
                  pl.BlockSpec((1, K), lambda i, j: (0, 0)),
                  pl.BlockSpec((tn, K), lambda i, j: (j, 0)),
                  pl.BlockSpec((tn, K), lambda i, j: (j + nj, 0))],
        out_specs=[col, col, col, pl.BlockSpec((tm, K), lambda i, j: (i, 0))],
        out_shape=[_sds((M, D_FF), bf16)] * 3 + [_sds((M, K), bf16)],
        compiler_params=_params(("parallel", "arbitrary")),
    )(x, gain, wt, wt)


def _rms_bwd_math(yv, gain, dn):
    r = _rms_r(yv)
    q = dn * gain
    dy = r * q - yv * (r * r * r) * jnp.mean(q * yv, axis=-1, keepdims=True)
    return dy, jnp.sum(dn * yv * r, axis=0, keepdims=True)


def _accumulate(ref, val):
    @pl.when(pl.program_id(0) == 0)
    def _():
        ref[...] = jnp.zeros_like(ref)

    ref[...] += val


def down_bwd(y, gain, dn, w_down, g, u, *, name, after=None):
    M, K = y.shape
    tm = min(512, M)
    order = [] if after is None else [after]

    def body(y_ref, gn_ref, dn_ref, w_ref, g_ref, u_ref, *refs):
        dy_ref, o_ref, dg_ref = refs[len(order):]
        dy, dgain = _rms_bwd_math(y_ref[...].astype(f32), gn_ref[...], dn_ref[...])
        dy = dy.astype(bf16)
        dy_ref[...] = dy
        _accumulate(dg_ref, dgain)
        da = lax.dot_general(dy, w_ref[...], NT, preferred_element_type=f32).astype(bf16)
        g = g_ref[...]
        s = 1.0 / (1.0 + jnp.exp(-g))
        o_ref[:, :D_FF] = da * u_ref[...] * s * (1.0 + g * (1.0 - s))
        o_ref[:, D_FF:] = da * g * s

    row = pl.BlockSpec((tm, K), lambda i: (i, 0))
    vec = pl.BlockSpec((1, K), lambda i: (0, 0))
    wide = pl.BlockSpec((tm, D_FF), lambda i: (i, 0))
    return pl.pallas_call(
        body, name=name, grid=(M // tm,),
        in_specs=[row, vec, row, pl.BlockSpec((D_FF, K), lambda i: (0, 0)), wide, wide]
        + [pl.BlockSpec(memory_space=pl.ANY)] * len(order),
        out_specs=[row, pl.BlockSpec((tm, 2 * D_FF), lambda i: (i, 0)), vec],
        out_shape=[_sds((M, K), bf16), _sds((M, 2 * D_FF), bf16), _sds((1, K), f32)],
        compiler_params=_params(("arbitrary",)),
    )(y, gain, dn, w_down, g, u, *order)


def rms_bwd_matmul(y, gain, dn, w, dims, *, name, after=None):
    M, K = y.shape
    N = w.shape[0] if dims is NT else w.shape[1]
    tm = min(1024, M)
    order = [] if after is None else [after]

    def body(y_ref, gn_ref, dn_ref, w_ref, *refs):
        dy_ref, o_ref, dg_ref = refs[len(order):]
        dy, dgain = _rms_bwd_math(y_ref[...].astype(f32), gn_ref[...], dn_ref[...].astype(f32))
        dy = dy.astype(bf16)
        dy_ref[...] = dy
        _accumulate(dg_ref, dgain)
        o_ref[...] = lax.dot_general(dy, w_ref[...], dims, preferred_element_type=f32).astype(bf16)

    row = pl.BlockSpec((tm, K), lambda i: (i, 0))
    vec = pl.BlockSpec((1, K), lambda i: (0, 0))
    return pl.pallas_call(
        body, name=name, grid=(M // tm,),
        in_specs=[row, vec, row, pl.BlockSpec(w.shape, lambda i: (0, 0))]
        + [pl.BlockSpec(memory_space=pl.ANY)] * len(order),
        out_specs=[row, pl.BlockSpec((tm, N), lambda i: (i, 0)), vec],
        out_shape=[_sds((M, K), bf16), _sds((M, N), bf16), _sds((1, K), f32)],
        compiler_params=_params(("arbitrary",)),
    )(y, gain, dn, w, *order)


def matmul_rms_bwd(a, b, dims, y, gain, res, *, name, after=None):
    M, K = a.shape
    N = y.shape[1]
    tm = min(512, M)
    order = [] if after is None else [after]

    def body(a_ref, b_ref, y_ref, gn_ref, r_ref, *refs):
        dx_ref, dg_ref = refs[len(order):]
        dn = lax.dot_general(a_ref[...].astype(bf16), b_ref[...], dims, preferred_element_type=f32)
        dy, dgain = _rms_bwd_math(y_ref[...], gn_ref[...], dn)
        dx_ref[...] = dy + r_ref[...]
        _accumulate(dg_ref, dgain)

    row = pl.BlockSpec((tm, N), lambda i: (i, 0))
    vec = pl.BlockSpec((1, N), lambda i: (0, 0))
    return pl.pallas_call(
        body, name=name, grid=(M // tm,),
        in_specs=[pl.BlockSpec((tm, K), lambda i: (i, 0)), pl.BlockSpec(b.shape, lambda i: (0, 0)), row, vec, row]
        + [pl.BlockSpec(memory_space=pl.ANY)] * len(order),
        out_specs=[row, vec],
        out_shape=[_sds((M, N), f32), _sds((1, N), f32)],
        compiler_params=_params(("arbitrary",)),
    )(a, b, y, gain, res, *order)


def rms_bwd(y, gain, dn, res, *, name, out_dtype, after=None):
    M, N = y.shape
    tm = min(512, M)
    has_res = res is not None
    order = [] if after is None else [after]

    def body(*refs):
        y_ref, g_ref, dn_ref = refs[:3]
        r_ref = refs[3] if has_res else None
        dy_ref, dg_ref = refs[-2:]
        dy, dgain = _rms_bwd_math(y_ref[...].astype(f32), g_ref[...], dn_ref[...].astype(f32))
        if has_res:
            dy = dy + r_ref[...]
        dy_ref[...] = dy.astype(dy_ref.dtype)
        _accumulate(dg_ref, dgain)

    row = pl.BlockSpec((tm, N), lambda i: (i, 0))
    vec = pl.BlockSpec((1, N), lambda i: (0, 0))
    args = [y, gain, dn] + ([res] if has_res else []) + order
    return pl.pallas_call(
        body, name=name, grid=(M // tm,),
        in_specs=[row, vec, row] + ([row] if has_res else []) + [pl.BlockSpec(memory_space=pl.ANY)] * len(order),
        out_specs=[row, vec],
        out_shape=[_sds((M, N), out_dtype), _sds((1, N), f32)],
        compiler_params=_params(("arbitrary",)),
    )(*args)


def _pool_select(a1, a2, a3, a4):
    col = lax.broadcasted_iota(jnp.int32, (1, MAIN_W), 1) // POOL_GROUP
    return jnp.where(col == 0, a1, jnp.where(col == 1, a2, jnp.where(col == 2, a3, a4)))


def _pool_count(t):
    col = lax.broadcasted_iota(jnp.int32, (1, MAIN_W), 1) // POOL_GROUP
    win = jnp.where(col == 0, 2, jnp.where(col == 1, 4, jnp.where(col == 2, 8, 16)))
    return jnp.minimum(t + 1, win).astype(f32)


def pool_fwd(z, wbd, scale, *, name):
    M = z.shape[0]
    tm = 512
    nper = SEQ // tm
    hb = tm // POOL_HALO

    def body(zc_ref, zh_ref, w_ref, s_ref, p_ref, y_ref):
        i = pl.program_id(0)
        seq_blk = i % nper
        halo = jnp.where(seq_blk == 0, 0.0, zh_ref[...].astype(f32))
        u = zc_ref[...].astype(f32)
        ext = jnp.concatenate([halo, u], axis=0)
        a1 = ext + pltpu.roll(ext, 1, 0)
        a2 = a1 + pltpu.roll(a1, 2, 0)
        a3 = a2 + pltpu.roll(a2, 4, 0)
        a4 = a3 + pltpu.roll(a3, 8, 0)
        sums = _pool_select(a1, a2, a3, a4)[POOL_HALO:]
        t = seq_blk * tm + lax.broadcasted_iota(jnp.int32, (tm, 1), 0)
        p = (sums / _pool_count(t) - u).astype(bf16)
        p_ref[...] = p
        y_ref[...] = (jnp.dot(p, w_ref[...], preferred_element_type=f32) * s_ref[...]).astype(bf16)

    return pl.pallas_call(
        body, name=name, grid=(M // tm,),
        in_specs=[pl.BlockSpec((tm, MAIN_W), lambda i: (i, 0)),
                  pl.BlockSpec((POOL_HALO, MAIN_W), lambda i: (jnp.maximum(i * hb - 1, 0), 0)),
                  pl.BlockSpec((MAIN_W, MAIN_W), lambda i: (0, 0)),
                  pl.BlockSpec((1, MAIN_W), lambda i: (0, 0))],
        out_specs=[pl.BlockSpec((tm, MAIN_W), lambda i: (i, 0)),
                   pl.BlockSpec((tm, MAIN_W), lambda i: (i, 0))],
        out_shape=[_sds((M, MAIN_W), bf16), _sds((M, D_MODEL), bf16)],
        compiler_params=_params(("parallel",)),
    )(z, z, wbd, scale)


def pool_bwd(dyc, p, wbd, scale, *, name):
    M = p.shape[0]
    tm = 512
    nper = SEQ // tm
    hb = tm // POOL_HALO
    last_hb = M // POOL_HALO - 1

    def body(dy_ref, dyh_ref, p_ref, w_ref, s_ref, dz_ref, dw_ref, ds_ref):
        i = pl.program_id(0)
        seq_blk = i % nper
        dy = dy_ref[...].astype(f32)
        pv = p_ref[...]
        w = w_ref[...]
        sc = s_ref[...]

        @pl.when(i == 0)
        def _():
            dw_ref[...] = jnp.zeros_like(dw_ref)
            ds_ref[...] = jnp.zeros_like(ds_ref)

        v = jnp.dot(pv, w, preferred_element_type=f32)
        ds_ref[...] += jnp.sum(dy * v, axis=0, keepdims=True)
        dv = (dy * sc).astype(bf16)
        dw_ref[...] += lax.dot_general(pv, dv, TN, preferred_element_type=f32)
        dp = lax.dot_general(dv, w, NT, preferred_element_type=f32)
        dvh = jnp.where(seq_blk == nper - 1, 0.0, dyh_ref[...].astype(f32) * sc).astype(bf16)
        dph = lax.dot_general(dvh, w, NT, preferred_element_type=f32)
        ext = jnp.concatenate([dp, dph], axis=0)
        n = tm + POOL_HALO
        t = seq_blk * tm + lax.broadcasted_iota(jnp.int32, (n, 1), 0)
        e = ext / _pool_count(t)
        b1 = e + pltpu.roll(e, n - 1, 0)
        b2 = b1 + pltpu.roll(b1, n - 2, 0)
        b3 = b2 + pltpu.roll(b2, n - 4, 0)
        b4 = b3 + pltpu.roll(b3, n - 8, 0)
        dz_ref[...] = (_pool_select(b1, b2, b3, b4)[:tm] - dp).astype(dz_ref.dtype)

    return pl.pallas_call(
        body, name=name, grid=(M // tm,),
        in_specs=[pl.BlockSpec((tm, MAIN_W), lambda i: (i, 0)),
                  pl.BlockSpec((POOL_HALO, MAIN_W), lambda i: (jnp.minimum((i + 1) * hb, last_hb), 0)),
                  pl.BlockSpec((tm, MAIN_W), lambda i: (i, 0)),
                  pl.BlockSpec((MAIN_W, MAIN_W), lambda i: (0, 0)),
                  pl.BlockSpec((1, MAIN_W), lambda i: (0, 0))],
        out_specs=[pl.BlockSpec((tm, MAIN_W), lambda i: (i, 0)),
                   pl.BlockSpec((MAIN_W, MAIN_W), lambda i: (0, 0)),
                   pl.BlockSpec((1, MAIN_W), lambda i: (0, 0))],
        out_shape=[_sds((M, D_MODEL), bf16), _sds((MAIN_W, MAIN_W), f32), _sds((1, MAIN_W), f32)],
        compiler_params=_params(("arbitrary",)),
    )(dyc, dyc, p, wbd, scale)


def _mem_heads(q, kv):
    first = _first_head()
    for pr in range(N_MEM_HEADS // 2):
        cols = slice(pr * PAIR_W, (pr + 1) * PAIR_W)
        qp = q[:, cols] * SCALE
        kp = kv[:, cols]
        vp = kv[:, MEM_W + pr * PAIR_W: MEM_W + (pr + 1) * PAIR_W]
        for hh in range(2):
            lm = first if hh == 0 else ~first
            qm = jnp.where(lm, qp, 0.0).astype(bf16)
            s = lax.dot_general(qm, kp, NT, preferred_element_type=f32)
            e = jnp.exp(s - jnp.max(s, axis=-1, keepdims=True))
            yield lm, qm, kp, vp, e, jnp.sum(e, axis=-1, keepdims=True)


def memattn_fwd(z, kvm, ycat, *, name, n_seq):
    M = z.shape[0]
    tq = 1024
    nq = SEQ // tq

    def body(q_ref, kv_ref, _, o_ref):
        first = _first_head()
        outs = []
        for lm, _, _, vp, e, l in _mem_heads(q_ref[...], kv_ref[...]):
            outs.append(jnp.dot(e.astype(bf16), vp, preferred_element_type=f32) * (1.0 / l))
        pairs = [jnp.where(first, outs[2 * pr], outs[2 * pr + 1]) for pr in range(N_MEM_HEADS // 2)]
        o_ref[...] = jnp.concatenate(pairs, axis=1).astype(bf16)

    return pl.pallas_call(
        body, name=name, grid=(n_seq, nq),
        in_specs=[pl.BlockSpec((tq, MEM_W), lambda b, i: (b * nq + i, 3)),
                  pl.BlockSpec((N_MEM, 2 * MEM_W), lambda b, i: (b, 0)),
                  pl.BlockSpec(memory_space=pl.ANY)],
        out_specs=pl.BlockSpec((tq, MEM_W), lambda b, i: (b * nq + i, 3)),
        out_shape=_sds((M, D_MODEL), bf16),
        input_output_aliases={2: 0},
        compiler_params=_params(("parallel", "parallel")),
    )(z, kvm, ycat)


def memattn_bwd(z, kvm, dyc, dz, *, name, n_seq):
    M = z.shape[0]
    tq = 1024
    nq = SEQ // tq

    def body(q_ref, kv_ref, dy_ref, _, dq_ref, dkv_ref):
        first = _first_head()
        dy = dy_ref[...].astype(f32)
        dqs, dks, dvs = [], [], []
        for h, (lm, qm, kp, vp, e, l) in enumerate(_mem_heads(q_ref[...], kv_ref[...])):
            pr = h // 2
            p = e * (1.0 / l)
            dym = jnp.where(lm, dy[:, pr * PAIR_W:(pr + 1) * PAIR_W], 0.0).astype(bf16)
            dp = lax.dot_general(dym, vp, NT, preferred_element_type=f32)
            ds = (p * (dp - jnp.sum(dp * p, axis=-1, keepdims=True))).astype(bf16)
            dqs.append(jnp.dot(ds, kp, preferred_element_type=f32) * SCALE)
            dk = lax.dot_general(ds, qm, TN, preferred_element_type=f32)
            dv = lax.dot_general(p.astype(bf16), dym, TN, preferred_element_type=f32)
            if h % 2 == 0:
                dks.append(dk)
                dvs.append(dv)
            else:
                dks[pr] = dks[pr] + dk
                dvs[pr] = dvs[pr] + dv
        pairs = [jnp.where(first, dqs[2 * pr], dqs[2 * pr + 1]) for pr in range(N_MEM_HEADS // 2)]
        dq_ref[...] = jnp.concatenate(pairs, axis=1).astype(bf16)

        @pl.when(pl.program_id(1) == 0)
        def _():
            dkv_ref[...] = jnp.zeros_like(dkv_ref)

        dkv_ref[...] += jnp.concatenate(dks + dvs, axis=1)

    return pl.pallas_call(
        body, name=name, grid=(n_seq, nq),
        in_specs=[pl.BlockSpec((tq, MEM_W), lambda b, i: (b * nq + i, 3)),
                  pl.BlockSpec((N_MEM, 2 * MEM_W), lambda b, i: (b, 0)),
                  pl.BlockSpec((tq, MEM_W), lambda b, i: (b * nq + i, 3)),
                  pl.BlockSpec(memory_space=pl.ANY)],
        out_specs=[pl.BlockSpec((tq, MEM_W), lambda b, i: (b * nq + i, 3)),
                   pl.BlockSpec((N_MEM, 2 * MEM_W), lambda b, i: (b, 0))],
        out_shape=[_sds((M, D_MODEL), bf16), _sds((n_seq * N_MEM, 2 * MEM_W), f32)],
        input_output_aliases={3: 0},
        compiler_params=_params(("parallel", "arbitrary")),
    )(z, kvm, dyc, dz)


def rope_tables(pos, *, name):
    M = pos.shape[0]
    tm = min(1024, M)
    half = HEAD_DIM // 2
    inv = ROPE_THETA ** (-np.arange(half, dtype=np.float64) / half)
    inv128 = jnp.asarray(np.tile(inv, 4)[None, :], f32)
    sign128 = jnp.asarray(np.tile(np.concatenate([-np.ones(half), np.ones(half)]), 2)[None, :], f32)

    def body(p_ref, f_ref, s_ref, cos_ref, sin_ref):
        ang = p_ref[...] * f_ref[...]
        cos_ref[...] = jnp.cos(ang)
        sin_ref[...] = jnp.sin(ang) * s_ref[...]

    return pl.pallas_call(
        body, name=name, grid=(M // tm,),
        in_specs=[pl.BlockSpec((tm, 1), lambda i: (i, 0)),
                  pl.BlockSpec((1, 128), lambda i: (0, 0)),
                  pl.BlockSpec((1, 128), lambda i: (0, 0))],
        out_specs=[pl.BlockSpec((tm, 128), lambda i: (i, 0)),
                   pl.BlockSpec((tm, 128), lambda i: (i, 0))],
        out_shape=[_sds((M, 128), f32), _sds((M, 128), f32)],
        compiler_params=_params(("parallel",)),
    )(pos, inv128, sign128)


def _swap_halves(x):
    w = x.shape[1]
    first = (lax.broadcasted_iota(jnp.int32, (1, w), 1) % HEAD_DIM) < (HEAD_DIM // 2)
    return jnp.where(first, pltpu.roll(x, w - HEAD_DIM // 2, 1), pltpu.roll(x, HEAD_DIM // 2, 1))


def group_sum(groups, cos, sin, *, name, rotate, width, col_block=0, into=None):
    M = groups[0][0].shape[0]
    tm = min(512, M)
    counts = [len(g) for g in groups]
    flat = [a for g in groups for a in g]
    extra = [] if into is None else [into]

    def body(*refs):
        part_refs = refs[:len(flat)]
        c_ref, s_ref = refs[len(flat):len(flat) + 2]
        o_ref = refs[-1]
        cols, k = [], 0
        for n in counts:
            acc = part_refs[k][...]
            for r in part_refs[k + 1:k + n]:
                acc = acc + r[...]
            cols.append(acc)
            k += n
        d = jnp.concatenate(cols, axis=1)
        if rotate:
            c = jnp.tile(c_ref[...], (1, MAIN_W // 128))
            s = jnp.tile(s_ref[...], (1, MAIN_W // 128))
            d = d * c - _swap_halves(d) * s
        o_ref[...] = d.astype(bf16)

    part = pl.BlockSpec((tm, GROUP_W), lambda i: (i, 0))
    tab = pl.BlockSpec((tm, 128), lambda i: (i, 0))
    return pl.pallas_call(
        body, name=name, grid=(M // tm,),
        in_specs=[part] * len(flat) + [tab, tab] + [pl.BlockSpec(memory_space=pl.ANY)] * len(extra),
        out_specs=pl.BlockSpec((tm, MAIN_W), lambda i: (i, col_block)),
        out_shape=_sds((M, width), bf16),
        input_output_aliases={len(flat) + 2: 0} if extra else {},
        compiler_params=_params(("parallel",)),
    )(*flat, cos, sin, *extra)


PAIR_W = 2 * HEAD_DIM
MIN_BLOCKS = 8


def _dil_geometry(dil):
    nsub = max(dil, MIN_BLOCKS)
    tb = BAND * nsub
    return nsub, tb, SEQ // tb


REGROUP = 4


class _Regrouped:
    def __init__(self, ref):
        self.ref = ref
        self.shape = ref.shape

    def fill(self, src):
        q = self.shape[0] // REGROUP
        for r0 in range(REGROUP):
            self.ref[r0 * q:(r0 + 1) * q, :] = src[pl.ds(r0, q, stride=REGROUP), :]

    def drain(self, dst):
        q = self.shape[0] // REGROUP
        for r0 in range(REGROUP):
            dst[pl.ds(r0, q, stride=REGROUP), :] = self.ref[r0 * q:(r0 + 1) * q, :]

    def rows(self, sub, dil):
        nl, r = divmod(sub, dil)
        start = (r % REGROUP) * (self.shape[0] // REGROUP) + r // REGROUP + nl * BAND * (dil // REGROUP)
        return pl.ds(start, BAND, stride=dil // REGROUP)


def _regroups(dil):
    return dil % (4 * REGROUP) == 0


def _rows(ref, sub, dil):
    if isinstance(ref, _Regrouped):
        return ref.ref[ref.rows(sub, dil), :]
    if dil == 1:
        return ref[sub * BAND:(sub + 1) * BAND, :]
    nl, r = divmod(sub, dil)
    return ref[pl.ds(nl * BAND * dil + r, BAND, stride=dil), :]


def _store_rows(ref, sub, dil, val):
    if isinstance(ref, _Regrouped):
        ref.ref[ref.rows(sub, dil), :] = val
    elif dil == 1:
        ref[sub * BAND:(sub + 1) * BAND, :] = val
    else:
        nl, r = divmod(sub, dil)
        ref[pl.ds(nl * BAND * dil + r, BAND, stride=dil), :] = val


def _keys(prev_ref, own_ref, sub, dil):
    nsub = own_ref.shape[0] // BAND
    if sub >= dil:
        prev = _rows(own_ref, sub - dil, dil)
    elif prev_ref is None:
        return _rows(own_ref, sub, dil)
    else:
        prev = _rows(prev_ref, nsub - dil + sub, dil)
    return jnp.concatenate([prev, _rows(own_ref, sub, dil)], axis=0)


def _band_mask(nkeys, has_prev):
    i = lax.broadcasted_iota(jnp.int32, (BAND, nkeys), 0)
    j = lax.broadcasted_iota(jnp.int32, (BAND, nkeys), 1)
    if nkeys == BAND:
        return j <= i
    return (j >= i) & (j <= i + BAND) & (has_prev | (j >= BAND))


def _first_head():
    return lax.broadcasted_iota(jnp.int32, (1, PAIR_W), 1) < HEAD_DIM


def _col(x, hh):
    return x[:, hh * HEAD_DIM:hh * HEAD_DIM + 1]


def _pair_spec(tb, nblk, col0, which):
    def idx(b, p, i):
        if which < 0:
            i = jnp.maximum(i - 1, 0)
        elif which > 0:
            i = jnp.minimum(i + 1, nblk - 1)
        return (b * nblk + i, col0 + p)
    return pl.BlockSpec((tb, PAIR_W), idx)


def dil_fwd(q, k, kv, g, dil, *, name, n_seq):
    M = q.shape[0]
    nsub, tb, nblk = _dil_geometry(dil)
    with_prev = nblk > 1

    regroup = _regroups(dil)
    assert not (regroup and with_prev)

    def body(*refs):
        if with_prev:
            q_ref, ko_ref, vo_ref, kp_ref, vp_ref, o_ref, l_ref = refs
        else:
            (q_ref, ko_ref, vo_ref, o_ref, l_ref), kp_ref, vp_ref = refs[:5], None, None
        outs_to = ()
        if regroup:
            copies = [_Regrouped(s) for s in refs[5:]]
            for c, src in zip(copies, (q_ref, ko_ref, vo_ref)):
                c.fill(src)
            outs_to = ((copies[3], o_ref), (copies[4], l_ref))
            q_ref, ko_ref, vo_ref, o_ref, l_ref = copies
        first = _first_head()
        blk = pl.program_id(2)
        for sub in range(nsub):
            qs = _rows(q_ref, sub, dil) * SCALE
            kc = _keys(kp_ref, ko_ref, sub, dil).astype(bf16)
            vc = _keys(vp_ref, vo_ref, sub, dil).astype(bf16)
            has_prev = True if sub >= dil else blk > 0
            mask = _band_mask(kc.shape[0], has_prev)
            outs, lses = [], []
            for hh in range(2):
                qm = jnp.where(first if hh == 0 else ~first, qs, 0.0).astype(bf16)
                s = jnp.where(mask, lax.dot_general(qm, kc, NT, preferred_element_type=f32), NEG)
                m = jnp.max(s, axis=-1, keepdims=True)
                e = jnp.exp(s - m)
                l = jnp.sum(e, axis=-1, keepdims=True)
                outs.append(jnp.dot(e.astype(bf16), vc, preferred_element_type=f32) * (1.0 / l))
                lses.append(jnp.broadcast_to(m + jnp.log(l), (BAND, PAIR_W)))
            _store_rows(o_ref, sub, dil, jnp.where(first, outs[0], outs[1]))
            _store_rows(l_ref, sub, dil, jnp.where(first, lses[0], lses[1]))
        for c, dst in outs_to:
            c.drain(dst)

    ins = [(q, 2 * g, 0), (k, 2 * g, 0), (kv, 6 + 2 * g, 0)]
    if with_prev:
        ins += [(k, 2 * g, -1), (kv, 6 + 2 * g, -1)]
    out = _pair_spec(tb, nblk, 0, 0)
    return pl.pallas_call(
        body, name=name, grid=(n_seq, 2, nblk),
        in_specs=[_pair_spec(tb, nblk, c, w) for _, c, w in ins],
        out_specs=[out, out],
        out_shape=[_sds((M, GROUP_W), f32)] * 2,
        scratch_shapes=[pltpu.VMEM((tb, PAIR_W), f32)] * (5 if regroup else 0),
        compiler_params=_params(("parallel", "parallel", "arbitrary")),
    )(*[a for a, _, _ in ins])


def combine_fwd(os_, lses, *, name):
    M = os_[0].shape[0]
    tm = min(512, M)

    def body(o0, o1, o2, l0, l1, l2, y_ref):
        ls = [l0[...], l1[...], l2[...]]
        m = jnp.maximum(jnp.maximum(ls[0], ls[1]), ls[2])
        es = [jnp.exp(l - m) for l in ls]
        inv = 1.0 / (es[0] + es[1] + es[2])
        y_ref[...] = jnp.concatenate([o[...] * e * inv for o, e in zip((o0, o1, o2), es)], axis=1).astype(bf16)

    part = pl.BlockSpec((tm, GROUP_W), lambda i: (i, 0))
    return pl.pallas_call(
        body, name=name, grid=(M // tm,),
        in_specs=[part] * 6,
        out_specs=pl.BlockSpec((tm, MAIN_W), lambda i: (i, 0)),
        out_shape=_sds((M, D_MODEL), bf16),
        compiler_params=_params(("parallel",)),
    )(*os_, *lses)


def combine_bwd(dyc, os_, lses, *, name):
    M = os_[0].shape[0]
    tm = min(512, M)

    def body(dy_ref, o0, o1, o2, l0, l1, l2, d0, d1, d2, c0, c1, c2):
        r = lax.broadcasted_iota(jnp.int32, (GROUP_W, GROUP_W), 0) // HEAD_DIM
        c = lax.broadcasted_iota(jnp.int32, (GROUP_W, GROUP_W), 1) // HEAD_DIM
        ones = (r == c).astype(bf16)
        dy = dy_ref[...].astype(f32)
        ls = [l0[...], l1[...], l2[...]]
        m = jnp.maximum(jnp.maximum(ls[0], ls[1]), ls[2])
        es = [jnp.exp(l - m) for l in ls]
        inv = 1.0 / (es[0] + es[1] + es[2])
        total = 0.0
        alphas = []
        for g, (o, e, d_ref) in enumerate(zip((o0, o1, o2), es, (d0, d1, d2))):
            a = e * inv
            dyg = dy[:, g * GROUP_W:(g + 1) * GROUP_W]
            d_ref[...] = dyg * a
            prod = dyg * o[...]
            hi = prod.astype(bf16)
            lo = (prod - hi.astype(f32)).astype(bf16)
            dsum = jnp.dot(hi, ones, preferred_element_type=f32) + jnp.dot(lo, ones, preferred_element_type=f32)
            total = total + a * dsum
            alphas.append(a)
        for a, c_ref in zip(alphas, (c0, c1, c2)):
            c_ref[...] = -a * total

    part = pl.BlockSpec((tm, GROUP_W), lambda i: (i, 0))
    outs = pl.pallas_call(
        body, name=name, grid=(M // tm,),
        in_specs=[pl.BlockSpec((tm, MAIN_W), lambda i: (i, 0))] + [part] * 6,
        out_specs=[part] * 6,
        out_shape=[_sds((M, GROUP_W), f32)] * 6,
        compiler_params=_params(("parallel",)),
    )(dyc, *os_, *lses)
    return outs[:3], outs[3:]


def dil_bwd(q, k, kv, do, cc, lse, cos, sin, g, dil, *, name, n_seq, into=None):
    M = q.shape[0]
    nsub = SEQ // BAND
    per_res = nsub // dil
    extra = [] if into is None else [into]

    regroup = _regroups(dil)

    def body(q_ref, k_ref, v_ref, do_ref, c_ref, l_ref, cos_ref, sin_ref, *refs):
        dz_ref, dk_ref, dv_ref, dq_ref, *scratch = refs[len(extra):]
        dq_rows = dq_ref
        outs_to = ()
        if regroup:
            copies = [_Regrouped(s) for s in scratch]
            for c, src in zip(copies, (q_ref, k_ref, v_ref, do_ref, c_ref, l_ref)):
                c.fill(src)
            outs_to = tuple(zip(copies[6:], (dq_rows, dk_ref, dv_ref)))
            q_ref, k_ref, v_ref, do_ref, c_ref, l_ref, dq_ref, dk_ref, dv_ref = copies
        first = _first_head()
        for r in range(dil):
            carry = None
            for nl in range(per_res):
                sub = nl * dil + r
                qs = _rows(q_ref, sub, dil) * SCALE
                dos = _rows(do_ref, sub, dil)
                cs = _rows(c_ref, sub, dil)
                ls = _rows(l_ref, sub, dil)
                kc = _keys(None, k_ref, sub, dil).astype(bf16)
                vc = _keys(None, v_ref, sub, dil).astype(bf16)
                nkeys = kc.shape[0]
                mask = _band_mask(nkeys, True)
                dqs = []
                dkc = jnp.zeros((nkeys, PAIR_W), f32)
                dvc = jnp.zeros((nkeys, PAIR_W), f32)
                for hh in range(2):
                    lm = first if hh == 0 else ~first
                    qm = jnp.where(lm, qs, 0.0).astype(bf16)
                    dom = jnp.where(lm, dos, 0.0).astype(bf16)
                    s = jnp.where(mask, lax.dot_general(qm, kc, NT, preferred_element_type=f32), NEG)
                    p = jnp.exp(s - _col(ls, hh))
                    dp = lax.dot_general(dom, vc, NT, preferred_element_type=f32)
                    ds = (p * (dp + _col(cs, hh))).astype(bf16)
                    dqs.append(jnp.dot(ds, kc, preferred_element_type=f32) * SCALE)
                    dkc = dkc + lax.dot_general(ds, qm, TN, preferred_element_type=f32)
                    dvc = dvc + lax.dot_general(p.astype(bf16), dom, TN, preferred_element_type=f32)
                _store_rows(dq_ref, sub, dil, jnp.where(first, dqs[0], dqs[1]))
                if nkeys == 2 * BAND:
                    _store_rows(dk_ref, sub - dil, dil, carry[0] + dkc[:BAND])
                    _store_rows(dv_ref, sub - dil, dil, carry[1] + dvc[:BAND])
                    carry = (dkc[BAND:], dvc[BAND:])
                else:
                    carry = (dkc, dvc)
            _store_rows(dk_ref, (per_res - 1) * dil + r, dil, carry[0])
            _store_rows(dv_ref, (per_res - 1) * dil + r, dil, carry[1])
        for c, dst in outs_to:
            c.drain(dst)
        d = dq_rows[...]
        dz_ref[...] = (d * cos_ref[...] - _swap_halves(d) * sin_ref[...]).astype(bf16)

    def spec(col0):
        return pl.BlockSpec((SEQ, PAIR_W), lambda b, p: (b, col0 + p))

    tab = pl.BlockSpec((SEQ, PAIR_W), lambda b, p: (b, 0))
    out = spec(0)
    return pl.pallas_call(
        body, name=name, grid=(n_seq, 2),
        in_specs=[spec(2 * g), spec(2 * g), spec(6 + 2 * g), spec(0), spec(0), spec(0), tab, tab]
        + [pl.BlockSpec(memory_space=pl.ANY)] * len(extra),
        out_specs=[spec(2 * g), out, out],
        out_shape=[_sds((M, D_MODEL), bf16)] + [_sds((M, GROUP_W), f32)] * 2,
        input_output_aliases={8: 0} if extra else {},
        scratch_shapes=[pltpu.VMEM((SEQ, PAIR_W), f32)] * (10 if regroup else 1),
        compiler_params=_params(("parallel", "parallel")),
    )(q, k, kv, do, cc, lse, cos, sin, *extra)


def _blockdiag(wp):
    out = jnp.zeros((MAIN_W, MAIN_W), wp.dtype)
    for gi in range(len(POOL_WINDOWS)):
        sl = slice(gi * POOL_GROUP, (gi + 1) * POOL_GROUP)
        out = out.at[sl, sl].set(wp[gi])
    return out


def _unblockdiag(w):
    return jnp.stack([w[gi * POOL_GROUP:(gi + 1) * POOL_GROUP, gi * POOL_GROUP:(gi + 1) * POOL_GROUP]
                      for gi in range(len(POOL_WINDOWS))])


def local_step(x, mem, positions, target, P, layer_weights, kv_weight, emit_grads):
    n_seq = x.shape[0]
    M = n_seq * SEQ
    xs = x.reshape(M, D_MODEL)
    mems = mem.reshape(n_seq * N_MEM, D_MODEL)
    pos = positions.reshape(M, 1).astype(f32)
    cos, sin = rope_tables(pos, name="rope_tables")
    gains = P["norm_gains"]

    def gain(l, k):
        return gains[l, k].reshape(1, D_MODEL)

    saved = []
    kvs = None
    for l in range(DEPTH):
        W, started = layer_weights(l, "mix", xs)
        sv = {"x": xs, "W": W}
        z, h1, *qrot = rms_matmul(xs, gain(l, 0), W["w_in"], name=f"l{l}_in", out_dtype=bf16, after=started,
                                  rope=None if l < N_A_LAYERS else (cos, sin))
        kvm, mn = rms_matmul(mems, P["mem_norm"][l].reshape(1, D_MODEL), W["w_mem_kv"],
                             name=f"l{l}_memkv", out_dtype=bf16)
        sv.update(z=z, h1=h1, kvm=kvm, mn=mn)
        if l < N_A_LAYERS:
            wbd = _blockdiag(P["w_pool"][l].astype(bf16))
            psc = P["pool_scale"][l].reshape(1, MAIN_W)
            p, y_main = pool_fwd(z, wbd, psc, name=f"l{l}_pool")
            sv.update(p=p, wbd=wbd, psc=psc)
        else:
            (qrot,) = qrot
            os_, lses = [], []
            for g, (_, dil) in enumerate(DIL_PATTERNS):
                o, lse = dil_fwd(qrot, kvs["krot"], kvs["kv"], g, dil, name=f"l{l}_dil{g}", n_seq=n_seq)
                os_.append(o)
                lses.append(lse)
            y_main = combine_fwd(os_, lses, name=f"l{l}_comb")
            sv.update(qrot=qrot, os=os_, lses=lses)
        ycat = memattn_fwd(z, kvm, y_main, name=f"l{l}_memattn", n_seq=n_seq)
        y, x1 = matmul_rms_res(ycat, W["w_out"], gain(l, 1), xs, name=f"l{l}_out")
        W.update(layer_weights(l, "gu", x1)[0])
        fg, fu, a, h2 = rms_gate_up(x1, gain(l, 2), W["w_gate_up"], name=f"l{l}_gu")
        W.update(layer_weights(l, "down", a)[0])
        y2, x2, *sq = matmul_rms_res(a, W["w_down"], gain(l, 3), x1, name=f"l{l}_down",
                                     target=target.reshape(M, D_MODEL) if l == DEPTH - 1 else None)
        sv.update(ycat=ycat, y=y, x1=x1, fg=fg, fu=fu, h2=h2, a=a, y2=y2)
        saved.append(sv)
        xs = x2
        if l == N_A_LAYERS - 1:
            w_kv = kv_weight(xs)
            kv, hkv, krot = rms_matmul(xs, P["kv_norm"].reshape(1, D_MODEL), w_kv, name="kv_proj", out_dtype=f32,
                                       transposed=True, rope=(cos, sin))
            kvs = {"kv": kv, "hkv": hkv, "krot": krot, "x": xs, "w_kv": w_kv}

    dx, (sq,) = xs, sq

    G = {"mem_norm": [None] * DEPTH, "norm_gains": [[None] * 4 for _ in range(DEPTH)],
         "pool_scale": [None] * N_A_LAYERS}
    dk_parts = [[] for _ in range(N_GROUPS)]
    dv_parts = [[] for _ in range(N_GROUPS)]
    emitted = None

    for l in reversed(range(DEPTH)):
        sv = saved[l]
        W = sv["W"]
        gw = {}
        dy2, dgu, G["norm_gains"][l][3] = down_bwd(sv["y2"], gain(l, 3), dx, W["w_down"], sv["fg"], sv["fu"],
                                                   name=f"l{l}_b_dgu", after=emitted)
        gw["w_down"] = matmul(sv["a"], dy2, TN, name=f"l{l}_b_wd", out_dtype=bf16)
        gw["w_gate_up"] = matmul(dgu, sv["h2"], TN, name=f"l{l}_b_wgu", out_dtype=bf16)
        emitted = emit_grads(l, "ffn", gw)
        dx1, G["norm_gains"][l][2] = matmul_rms_bwd(dgu, W["w_gate_up"], NN, sv["x1"], gain(l, 2), dx,
                                                    name=f"l{l}_b_dh2", after=emitted)
        gw = {}
        dy, dycat, G["norm_gains"][l][1] = rms_bwd_matmul(sv["y"], gain(l, 1), dx1, W["w_out"], NT,
                                                          name=f"l{l}_b_dycat", after=emitted)
        gw["w_out"] = matmul(sv["ycat"], dy, TN, name=f"l{l}_b_wout", out_dtype=bf16)
        if l < N_A_LAYERS:
            dz, dwbd, dps = pool_bwd(dycat, sv["p"], sv["wbd"], sv["psc"], name=f"l{l}_b_pool")
            gw["w_pool"] = _unblockdiag(dwbd).reshape(MAIN_W, POOL_GROUP).astype(bf16)
            G["pool_scale"][l] = dps.reshape(MAIN_W)
        else:
            dos, ccs = combine_bwd(dycat, sv["os"], sv["lses"], name=f"l{l}_b_comb")
            dz = None
            for g, (_, dil) in enumerate(DIL_PATTERNS):
                args = (sv["qrot"], kvs["krot"], kvs["kv"], dos[g], ccs[g], sv["lses"][g], cos, sin, g, dil)
                dz, dk, dv = dil_bwd(*args, name=f"l{l}_b_dil{g}", n_seq=n_seq, into=dz)
                dk_parts[g].append(dk)
                dv_parts[g].append(dv)
        dz, dkvm = memattn_bwd(sv["z"], sv["kvm"], dycat, dz, name=f"l{l}_b_memattn", n_seq=n_seq)
        gw["w_mem_kv"] = matmul(sv["mn"], dkvm, TN, name=f"l{l}_b_wmkv", out_dtype=bf16)
        _, G["mem_norm"][l] = matmul_rms_bwd(dkvm, W["w_mem_kv"], NT, mems, P["mem_norm"][l].reshape(1, D_MODEL),
                                             mems, name=f"l{l}_b_dmn")
        gw["w_in"] = matmul(sv["h1"], dz, TN, name=f"l{l}_b_win", out_dtype=bf16)
        if l != N_A_LAYERS:
            emitted = emit_grads(l, "mix", gw)
        dx, G["norm_gains"][l][0] = matmul_rms_bwd(dz, W["w_in"], NT, sv["x"], gain(l, 0), dx1, name=f"l{l}_b_dh1",
                                                   after=emitted)
        if l == N_A_LAYERS:
            dkv = group_sum(dk_parts, cos, sin, name="b_ropek", rotate=True, width=2 * MAIN_W)
            dkv = group_sum(dv_parts, cos, sin, name="b_sumv", rotate=False, width=2 * MAIN_W, col_block=1, into=dkv)
            gw["w_kv"] = matmul(dkv, kvs["hkv"], TN, name="b_wkv", out_dtype=bf16)
            dx, gkn = matmul_rms_bwd(dkv, kvs["w_kv"], NN, kvs["x"], P["kv_norm"].reshape(1, D_MODEL), dx,
                                     name="b_dhkv")
            G["kv_norm"] = gkn.reshape(D_MODEL)
            emitted = emit_grads(l, "mix", gw)

    small = {"pool_scale": jnp.stack(G["pool_scale"]),
             "mem_norm": jnp.concatenate(G["mem_norm"], axis=0),
             "norm_gains": jnp.stack([jnp.concatenate(r, axis=0) for r in G["norm_gains"]]),
             "kv_norm": G["kv_norm"]}
    return sq[0, 0], dx.reshape(n_seq, SEQ, D_MODEL), small, emitted


def to_bf16_layers(stacks, *, name):
    L, n = stacks[0].shape[0], len(stacks)

    def body(*refs):
        ins, outs = refs[:n], refs[n:]
        for j in range(L):
            @pl.when(pl.program_id(0) == j)
            def _():
                for k in range(n):
                    outs[j * n + k][...] = ins[k][...].astype(bf16)

    outs = pl.pallas_call(
        body, name=name, grid=(L,),
        in_specs=[pl.BlockSpec((None,) + s.shape[1:], lambda l: (l, 0, 0)) for s in stacks],
        out_specs=[pl.BlockSpec(s.shape[1:], lambda l: (0, 0)) for _ in range(L) for s in stacks],
        out_shape=[_sds(s.shape[1:], bf16) for _ in range(L) for s in stacks],
        compiler_params=_params(("arbitrary",)),
    )(*stacks)
    return [outs[j * n:(j + 1) * n] for j in range(L)]


def _peer(k):
    x, y, c = lax.axis_index("x"), lax.axis_index("y"), lax.axis_index("c")
    px = 1 - x if k & 4 else x
    py = 1 - y if k & 2 else y
    pc = 1 - c if k & 1 else c
    return (px, py, pc), 4 * px + 2 * py + pc


def _my_index():
    return 4 * lax.axis_index("x") + 2 * lax.axis_index("y") + lax.axis_index("c")


def _src_for(kinds, in_refs, i, idx):
    return in_refs[i] if kinds[i] == "gather" else in_refs[i].at[idx]


def _local_copies(kinds, in_refs, out_refs, local_sems):
    me = _my_index()
    return [pltpu.make_async_copy(_src_for(kinds, in_refs, i, me), out_refs[i].at[me], local_sems.at[i])
            for i in range(len(kinds))]


def _remote_copies(kinds, in_refs, out_refs, send_sems, recv_sems, *, arriving):
    me = _my_index()
    copies = []
    for k in range(1, N_DEV):
        dev, idx = _peer(k)
        for i in range(len(kinds)):
            j = i * (N_DEV - 1) + k - 1
            copies.append(pltpu.make_async_remote_copy(
                src_ref=_src_for(kinds, in_refs, i, idx), dst_ref=out_refs[i].at[idx if arriving else me],
                send_sem=send_sems.at[j], recv_sem=recv_sems.at[j], device_id=dev, device_id_type=MESH))
    return copies


def _out_shape(a, kind):
    return ((N_DEV,) + a.shape) if kind == "gather" else a.shape


def exchange(items, *, name, after=()):
    n = len(items)
    kinds = [k for _, k in items]
    after = list(after)

    def body(*refs):
        in_refs, out_refs = refs[:n], refs[n + len(after):2 * n + len(after)]
        send_sems, recv_sems, local_sems = refs[-3:]
        local = _local_copies(kinds, in_refs, out_refs, local_sems)
        sends = _remote_copies(kinds, in_refs, out_refs, send_sems, recv_sems, arriving=False)
        for cp in local + sends:
            cp.start()
        for cp in _remote_copies(kinds, in_refs, out_refs, send_sems, recv_sems, arriving=True):
            cp.wait_recv()
        for cp in sends:
            cp.wait_send()
        for cp in local:
            cp.wait()

    any_spec = pl.BlockSpec(memory_space=pl.ANY)
    return pl.pallas_call(
        body, name=name,
        in_specs=[any_spec] * (n + len(after)), out_specs=[any_spec] * n,
        out_shape=[_sds(_out_shape(a, k), a.dtype) for a, k in items],
        scratch_shapes=[pltpu.SemaphoreType.DMA((n * (N_DEV - 1),)), pltpu.SemaphoreType.DMA((n * (N_DEV - 1),)),
                        pltpu.SemaphoreType.DMA((n,))],
    )(*[a for a, _ in items], *after)


_HBM = pl.BlockSpec(memory_space=pltpu.HBM)
_SEM = pl.BlockSpec(memory_space=pltpu.SEMAPHORE)
_EFFECT = pltpu.SideEffectType.DATAFLOW_SIDE_EFFECTING


def exchange_start(items, after, *, name):
    n = len(items)
    kinds = [k for _, k in items]

    def body(*refs):
        in_refs, land_refs = refs[:n], refs[n:2 * n]
        send_sems, recv_sems, local_sems = refs[2 * n + 1:2 * n + 4]
        token = refs[-1]
        for cp in (_local_copies(kinds, in_refs, land_refs, local_sems)
                   + _remote_copies(kinds, in_refs, land_refs, send_sems, recv_sems, arriving=False)):
            cp.start()
        token[...] = jnp.zeros_like(token)

    srcs = [pltpu.with_memory_space_constraint(a, pltpu.HBM) for a, _ in items]
    lands = [pltpu.with_memory_space_constraint(lax.empty(_out_shape(a, k), a.dtype), pltpu.HBM) for a, k in items]
    outs = pl.pallas_call(
        body, name=name,
        out_shape=(pltpu.SemaphoreType.DMA((n * (N_DEV - 1),)), pltpu.SemaphoreType.DMA((n * (N_DEV - 1),)),
                   pltpu.SemaphoreType.DMA((n,)),
                   *[pltpu.HBM(a.shape, a.dtype) for a in srcs], *[pltpu.HBM(a.shape, a.dtype) for a in lands],
                   _sds((8, 128), f32)),
        in_specs=[_HBM] * (2 * n) + [pl.BlockSpec(memory_space=pl.ANY)],
        out_specs=(_SEM, _SEM, _SEM, *[_HBM] * (2 * n), pl.BlockSpec(memory_space=pltpu.VMEM)),
        input_output_aliases={i: 3 + i for i in range(2 * n)},
        compiler_params=pltpu.CompilerParams(has_side_effects=_EFFECT),
    )(*srcs, *lands, after)
    return {"kinds": kinds, "sems": outs[:3], "srcs": outs[3:3 + n], "lands": outs[3 + n:3 + 2 * n], "token": outs[-1]}


def exchange_wait(handle, after, *, name):
    kinds = handle["kinds"]
    n = len(kinds)

    def body(*refs):
        in_refs, land_refs = refs[:n], refs[n:2 * n]
        send_sems, recv_sems, local_sems = refs[2 * n:2 * n + 3]
        for cp in _remote_copies(kinds, in_refs, land_refs, send_sems, recv_sems, arriving=True):
            cp.wait_recv()
        for cp in _remote_copies(kinds, in_refs, land_refs, send_sems, recv_sems, arriving=False):
            cp.wait_send()
        for cp in _local_copies(kinds, in_refs, land_refs, local_sems):
            cp.wait()

    srcs, lands = list(handle["srcs"]), list(handle["lands"])
    after = list(after) if isinstance(after, (list, tuple)) else [after]
    outs = pl.pallas_call(
        body, name=name,
        out_shape=tuple(pltpu.HBM(a.shape, a.dtype) for a in srcs + lands),
        in_specs=[_HBM] * (2 * n) + [_SEM] * 3 + [pl.BlockSpec(memory_space=pl.ANY)] * len(after),
        out_specs=tuple([_HBM] * (2 * n)),
        input_output_aliases={i: i for i in range(2 * n)},
        compiler_params=pltpu.CompilerParams(has_side_effects=_EFFECT),
    )(*srcs, *lands, *handle["sems"], *after)
    return list(outs[n:])


CHIP_MASKS = (2, 4, 6)


def _g2_first(in_refs, land_refs, send_sems, recv_sems, *, masks, arriving):
    me = _my_index()
    copies = []
    for i in range(len(land_refs)):
        for j, k in enumerate(masks):
            dev, idx = _peer(k)
            dst = land_refs[i].at[idx if arriving else me]
            copies.append(pltpu.make_async_remote_copy(
                src_ref=dst if in_refs is None else in_refs[i], dst_ref=dst,
                send_sem=send_sems.at[i * len(masks) + j], recv_sem=recv_sems.at[i * len(masks) + j],
                device_id=dev, device_id_type=MESH))
    return copies


def _g2_forward(land_refs, fwd_send, fwd_recv, *, arriving):
    sibling, _ = _peer(1)
    copies = []
    for i in range(len(land_refs)):
        for j, k in enumerate(CHIP_MASKS):
            _, idx = _peer(k | 1 if arriving else k)
            copies.append(pltpu.make_async_remote_copy(
                src_ref=land_refs[i].at[idx], dst_ref=land_refs[i].at[idx],
                send_sem=fwd_send.at[i * 3 + j], recv_sem=fwd_recv.at[i * 3 + j], device_id=sibling,
                device_id_type=MESH))
    return copies


def gather2_start(arrays, after, *, name):
    n = len(arrays)

    def body(*refs):
        in_refs, land_refs = refs[:n], refs[n:2 * n]
        ici_send, ici_recv, d2d_send, d2d_recv, local_sems = refs[2 * n + 1:2 * n + 6]
        token = refs[-1]
        ici = _g2_first(in_refs, land_refs, ici_send, ici_recv, masks=CHIP_MASKS, arriving=False)
        d2d = _g2_first(in_refs, land_refs, d2d_send, d2d_recv, masks=(1,), arriving=False)
        for cp in _local_copies(["gather"] * n, in_refs, land_refs, local_sems) + ici + d2d:
            cp.start()
        token[...] = jnp.zeros_like(token)

    srcs = [pltpu.with_memory_space_constraint(a, pltpu.HBM) for a in arrays]
    lands = [pltpu.with_memory_space_constraint(lax.empty((N_DEV,) + a.shape, a.dtype), pltpu.HBM) for a in arrays]
    sem = pltpu.SemaphoreType.DMA
    outs = pl.pallas_call(
        body, name=name,
        out_shape=(sem((3 * n,)), sem((3 * n,)), sem((n,)), sem((n,)), sem((n,)),
                   *[pltpu.HBM(a.shape, a.dtype) for a in srcs], *[pltpu.HBM(a.shape, a.dtype) for a in lands],
                   _sds((8, 128), f32)),
        in_specs=[_HBM] * (2 * n) + [pl.BlockSpec(memory_space=pl.ANY)],
        out_specs=(*[_SEM] * 5, *[_HBM] * (2 * n), pl.BlockSpec(memory_space=pltpu.VMEM)),
        input_output_aliases={i: 5 + i for i in range(2 * n)},
        compiler_params=pltpu.CompilerParams(has_side_effects=_EFFECT),
    )(*srcs, *lands, after)
    return {"n": n, "sems": outs[:5], "srcs": outs[5:5 + n], "lands": outs[5 + n:5 + 2 * n], "token": outs[-1]}


def gather2_forward(handle, after, *, name):
    n = handle["n"]

    def body(*refs):
        land_refs = refs[:n]
        ici_recv = refs[n]
        fwd_send, fwd_recv = refs[n + 2:n + 4]
        for cp in _g2_first(None, land_refs, fwd_send, ici_recv, masks=CHIP_MASKS, arriving=True):
            cp.wait_recv()
        for cp in _g2_forward(land_refs, fwd_send, fwd_recv, arriving=False):
            cp.start()

    lands = list(handle["lands"])
    sem = pltpu.SemaphoreType.DMA
    outs = pl.pallas_call(
        body, name=name,
        out_shape=(sem((3 * n,)), sem((3 * n,)), *[pltpu.HBM(a.shape, a.dtype) for a in lands]),
        in_specs=[_HBM] * n + [_SEM, pl.BlockSpec(memory_space=pl.ANY)],
        out_specs=(_SEM, _SEM, *[_HBM] * n),
        input_output_aliases={i: 2 + i for i in range(n)},
        compiler_params=pltpu.CompilerParams(has_side_effects=_EFFECT),
    )(*lands, handle["sems"][1], after)
    return dict(handle, fwd=outs[:2], lands=outs[2:])


def gather2_wait(handle, after, *, name):
    n = handle["n"]

    def body(*refs):
        in_refs, land_refs = refs[:n], refs[n:2 * n]
        ici_send, d2d_send, d2d_recv, local_sems, fwd_send, fwd_recv = refs[2 * n:2 * n + 6]
        for cp in _g2_first(in_refs, land_refs, d2d_send, d2d_recv, masks=(1,), arriving=True):
            cp.wait_recv()
        for cp in _g2_forward(land_refs, fwd_send, fwd_recv, arriving=True):
            cp.wait_recv()
        for cp in (_g2_first(in_refs, land_refs, ici_send, fwd_recv, masks=CHIP_MASKS, arriving=False)
                   + _g2_first(in_refs, land_refs, d2d_send, d2d_recv, masks=(1,), arriving=False)
                   + _g2_forward(land_refs, fwd_send, fwd_recv, arriving=False)):
            cp.wait_send()
        for cp in _local_copies(["gather"] * n, in_refs, land_refs, local_sems):
            cp.wait()

    srcs, lands = list(handle["srcs"]), list(handle["lands"])
    s = handle["sems"]
    outs = pl.pallas_call(
        body, name=name,
        out_shape=tuple(pltpu.HBM(a.shape, a.dtype) for a in srcs + lands),
        in_specs=[_HBM] * (2 * n) + [_SEM] * 6 + [pl.BlockSpec(memory_space=pl.ANY)],
        out_specs=tuple([_HBM] * (2 * n)),
        input_output_aliases={i: i for i in range(2 * n)},
        compiler_params=pltpu.CompilerParams(has_side_effects=_EFFECT),
    )(*srcs, *lands, s[0], s[2], s[3], s[4], *handle["fwd"], after)
    return list(outs[n:])


def adamw(entries, *, name):
    c1 = 1.0 - ADAM_B1 ** ADAM_STEP
    c2 = 1.0 - ADAM_B2 ** ADAM_STEP
    tiles = [_tile(w.shape[-2], (64, 32, 16, 8)) for _, w, _, _, _, _ in entries]
    steps = [w.shape[-2] // tr for (_, w, _, _, _, _), tr in zip(entries, tiles)]
    n = len(entries)

    def body(*refs):
        i = pl.program_id(0)
        for e in range(n):
            s_ref, w_ref, m_ref, v_ref = refs[4 * e:4 * e + 4]
            g_ref, d_ref, m2_ref, v2_ref = refs[len(refs) - 4 * n + 4 * e:len(refs) - 4 * n + 4 * e + 4]

            @pl.when(i < steps[e])
            def _():
                g = s_ref[0].astype(f32)
                for d in range(1, N_DEV):
                    g = g + s_ref[d].astype(f32)
                m2 = ADAM_B1 * m_ref[...] + (1.0 - ADAM_B1) * g
                v2 = ADAM_B2 * v_ref[...] + (1.0 - ADAM_B2) * (g * g)
                g_ref[...] = g
                m2_ref[...] = m2
                v2_ref[...] = v2
                d_ref[...] = -ADAM_LR * ((m2 / c1) / (jnp.sqrt(v2 / c2) + ADAM_EPS) + ADAM_WD * w_ref[...])

    in_specs, out_specs, out_shape, args, extras, aliases = [], [], [], [], [], {}
    for e, ((slots, w, m, v, layer, into), tr, ns) in enumerate(zip(entries, tiles, steps)):
        C = w.shape[-1]
        row = lambda i, ns=ns: jnp.minimum(i, ns - 1)
        if layer is None:
            blk = pl.BlockSpec((tr, C), lambda i, row=row: (row(i), 0))
        else:
            blk = pl.BlockSpec((None, tr, C), lambda i, row=row, layer=layer: (layer, row(i), 0))
        in_specs += [pl.BlockSpec((N_DEV, tr, C), lambda i, row=row: (0, row(i), 0)), blk, blk, blk]
        args += [slots, w, m, v]
        out_specs += [blk] * 4
        out_shape += [_sds(w.shape, f32)] * 4
        if into is not None:
            for t, a in enumerate(into):
                aliases[4 * n + len(extras)] = 4 * e + t
                extras.append(a)
    outs = pl.pallas_call(
        body, name=name, grid=(max(steps),),
        in_specs=in_specs + [pl.BlockSpec(memory_space=pl.ANY)] * len(extras),
        out_specs=out_specs, out_shape=out_shape, input_output_aliases=aliases,
        compiler_params=_params(("arbitrary",)),
    )(*args, *extras)
    return [outs[4 * e:4 * e + 4] for e in range(n)]


WEIGHTS = ("norm_gains", "mem_norm", "w_in", "w_mem_kv", "w_out", "w_pool", "pool_scale", "kv_norm", "w_kv",
           "w_gate_up", "w_down")
LAYER_MATS = ("w_in", "w_mem_kv", "w_out", "w_gate_up", "w_down")
POOL_SHARD = MAIN_W // N_DEV
KV_SHARD = 2 * MAIN_W // N_DEV
LOOKAHEAD = 2
TWO_LEVEL_LAYERS = (0, 1)


def _pack_small(gains, pscale):
    lead = gains.shape[:-3]
    g = gains.reshape(lead + (16, 128))
    p = jnp.zeros(lead + (8, 128), f32).at[..., :2, :POOL_SHARD].set(pscale)
    return jnp.concatenate([g, p], axis=-2)


def _unpack_small(a):
    return a[:16].reshape(4, 4, 128), a[16:18, :POOL_SHARD]


def _pack_repl(mem_norm, kv_norm):
    return jnp.concatenate([mem_norm, kv_norm.reshape(1, D_MODEL), jnp.zeros((3, D_MODEL), f32)], axis=0)


def _unpack_repl(a):
    return a[:4], a[4]


def kernel(x, mem, positions, norm_gains, mem_norm, w_in, w_mem_kv, w_out, w_pool, pool_scale, kv_norm, w_kv, w_gate_up, w_down, loss_target, m_norm_gains, m_mem_norm, m_w_in, m_w_mem_kv, m_w_out, m_w_pool, m_pool_scale, m_kv_norm, m_w_kv, m_w_gate_up, m_w_down, v_norm_gains, v_mem_norm, v_w_in, v_w_mem_kv, v_w_out, v_w_pool, v_pool_scale, v_kv_norm, v_w_kv, v_w_gate_up, v_w_down):
    w = dict(norm_gains=norm_gains, mem_norm=mem_norm, w_in=w_in, w_mem_kv=w_mem_kv, w_out=w_out, w_pool=w_pool,
             pool_scale=pool_scale, kv_norm=kv_norm, w_kv=w_kv, w_gate_up=w_gate_up, w_down=w_down)
    m = dict(norm_gains=m_norm_gains, mem_norm=m_mem_norm, w_in=m_w_in, w_mem_kv=m_w_mem_kv, w_out=m_w_out,
             w_pool=m_w_pool, pool_scale=m_pool_scale, kv_norm=m_kv_norm, w_kv=m_w_kv, w_gate_up=m_w_gate_up,
             w_down=m_w_down)
    v = dict(norm_gains=v_norm_gains, mem_norm=v_mem_norm, w_in=v_w_in, w_mem_kv=v_w_mem_kv, w_out=v_w_out,
             w_pool=v_w_pool, pool_scale=v_pool_scale, kv_norm=v_kv_norm, w_kv=v_w_kv, w_gate_up=v_w_gate_up,
             w_down=v_w_down)

    def transposed_view(d):
        d = dict(d)
        d["w_gate_up"] = jnp.swapaxes(d["w_gate_up"], 1, 2)
        d["w_kv"] = jnp.swapaxes(d["w_kv"], 0, 1)
        return d

    wv, mv, vv = transposed_view(w), transposed_view(m), transposed_view(v)

    small = _pack_small(norm_gains, pool_scale)
    (gsmall,) = exchange([(small, "gather")], name="gather_small")
    P = {"norm_gains": jnp.moveaxis(gsmall[:, :16].reshape(N_DEV, 4, 4, 128), 0, 2).reshape(4, 4, D_MODEL),
         "pool_scale": jnp.moveaxis(gsmall[:, 16:18, :POOL_SHARD], 0, 1).reshape(2, MAIN_W),
         "mem_norm": mem_norm, "kv_norm": kv_norm, "w_pool": w_pool}

    PARTS = {"mix": ("w_in", "w_mem_kv", "w_out"), "ffn": ("w_gate_up", "w_down"), "gu": ("w_gate_up",),
             "down": ("w_down",), "all": ("w_in", "w_mem_kv", "w_out", "w_gate_up", "w_down")}

    def parts_of(l):
        return (("mix", "gu", "down"), ("mix", "ffn"))[l] if l < 2 else ("all",)

    wb = [dict(zip(LAYER_MATS, mats)) for mats in to_bf16_layers([wv[k] for k in LAYER_MATS], name="weights_bf16")]

    def part_items(l, part):
        items = [(wb[l][k], "gather") for k in PARTS[part]]
        if part == "ffn" and l == N_A_LAYERS - 1:
            items.append((wv["w_kv"].astype(bf16), "gather"))
        return items

    handles = {}

    def start_layer(l, after):
        for part in parts_of(l):
            if l in TWO_LEVEL_LAYERS:
                handles[l, part] = gather2_start([a for a, _ in part_items(l, part)], after,
                                                 name=f"gather_start_{part}_l{l}")
            else:
                handles[l, part] = exchange_start(part_items(l, part), after, name=f"gather_start_{part}_l{l}")
            after = handles[l, part]["token"]
        return after

    token = gsmall
    for l in range(LOOKAHEAD):
        token = start_layer(l, token)
    landed = {}

    def layer_weights(l, part, after):
        if part not in parts_of(l):
            if part == "down" or (part == "gu" and "all" in parts_of(l)):
                return {}, None
            part = "all" if "all" in parts_of(l) else "ffn"
        if l == 0 and part == "mix":
            after = token
        if l in TWO_LEVEL_LAYERS:
            passed = gather2_forward(handles[l, part], after, name=f"gather_forward_{part}_l{l}")
            got = gather2_wait(passed, after, name=f"gather_wait_{part}_l{l}")
        else:
            got = exchange_wait(handles[l, part], after, name=f"gather_wait_{part}_l{l}")
        landed[l, part] = got
        started = None
        if part in ("mix", "all") and l + LOOKAHEAD < DEPTH:
            started = start_layer(l + LOOKAHEAD, got[0])
        W = {k: g.reshape(-1, g.shape[-1]) for k, g in zip(PARTS[part], got)}
        return W, started

    def kv_weight(after):
        g = landed[N_A_LAYERS - 1, "ffn"][len(PARTS["ffn"])]
        return g.reshape(2 * MAIN_W, D_MODEL)

    ghandles = {}

    pending = {}

    def gparts_of(l):
        return ("ffn", "mix") if l < 2 else ("all",)

    def emit_grads(l, part, gw):
        if part not in gparts_of(l):
            pending.setdefault(l, {}).update(gw)
            if part == "ffn":
                return None
            gw, part = pending[l], "all"
        items = [(gw[k].reshape((N_DEV, -1) + gw[k].shape[-1:]), "scatter") for k in PARTS[part]]
        if part != "ffn" and l == N_A_LAYERS:
            items.append((gw["w_kv"].reshape(N_DEV, KV_SHARD, D_MODEL), "scatter"))
        if part != "ffn" and l < N_A_LAYERS:
            items.append((gw["w_pool"], "gather"))
        ghandles[l, part] = exchange_start(items, gsmall, name=f"scatter_start_{part}_l{l}")
        return ghandles[l, part]["token"]

    sq, grad_x, GS, emitted = local_step(x, mem, positions, loss_target, P, layer_weights, kv_weight, emit_grads)

    def pool3(a):
        return a.reshape(N_A_LAYERS, MAIN_W, POOL_GROUP)

    out = {}
    after = [emitted]

    def finish_layer(l, after):
        for part in gparts_of(l):
            got = exchange_wait(ghandles[l, part], after, name=f"scatter_wait_{part}_l{l}")
            names = list(PARTS[part])
            entries = [(slots, wv[k], mv[k], vv[k], l, out.get(k)) for k, slots in zip(names, got)]
            if part != "ffn" and l == N_A_LAYERS:
                names.append("w_kv")
                entries.append((got[-1], wv["w_kv"], mv["w_kv"], vv["w_kv"], None, None))
            if part != "ffn" and l < N_A_LAYERS:
                names.append("w_pool")
                entries.append((got[-1], pool3(w_pool), pool3(m_w_pool), pool3(v_w_pool), l, out.get("w_pool")))
            out.update(zip(names, adamw(entries, name=f"adamw_{part}_l{l}")))
            after = [out[k][0] for k in names]
        return after

    for l in reversed(range(1, DEPTH)):
        after = finish_layer(l, after)

    gs = _pack_small(jnp.moveaxis(GS["norm_gains"].reshape(4, 4, N_DEV, 128), 2, 0),
                     jnp.moveaxis(GS["pool_scale"].reshape(2, N_DEV, POOL_SHARD), 1, 0))
    parts_small, parts_repl, parts_sq = exchange(
        [(gs, "scatter"), (_pack_repl(GS["mem_norm"], GS["kv_norm"]), "gather"),
         (jnp.full((8, 128), sq, f32), "gather")],
        name="exchange_small_grads", after=after)
    loss = (0.5 / D_MODEL) * jnp.sum(parts_sq[:, 0, 0])
    finish_layer(0, [parts_small])
    out["w_gate_up"] = [jnp.swapaxes(r, 1, 2) for r in out["w_gate_up"]]
    out["w_kv"] = [jnp.swapaxes(r, 0, 1) for r in out["w_kv"]]
    out["w_pool"] = [r.reshape(w_pool.shape) for r in out["w_pool"]]

    res_small, res_repl = adamw(
        [(parts_small, small, _pack_small(m_norm_gains, m_pool_scale), _pack_small(v_norm_gains, v_pool_scale),
          None, None),
         (parts_repl, _pack_repl(mem_norm, kv_norm), _pack_repl(m_mem_norm, m_kv_norm),
          _pack_repl(v_mem_norm, v_kv_norm), None, None)], name="adamw_small")
    out["norm_gains"], out["pool_scale"] = zip(*[_unpack_small(r) for r in res_small])
    out["mem_norm"], out["kv_norm"] = zip(*[_unpack_repl(r) for r in res_repl])

    return (loss, grad_x, *[out[k][0] for k in WEIGHTS], *[out[k][1] for k in WEIGHTS],
            *[out[k][2] for k in WEIGHTS], *[out[k][3] for k in WEIGHTS])
```

```python
import numpy as np
import jax
import jax.numpy as jnp
from jax import lax
from jax.experimental import pallas as pl
from jax.experimental.pallas import tpu as pltpu

f32 = jnp.float32
bf16 = jnp.bfloat16

D_MODEL = 1024
SEQ = 2048
DEPTH = 4
N_MEM = 256
HEAD_DIM = 64
N_MEM_HEADS = 4
MEM_W = 256
MAIN_W = 768
POOL_WINDOWS = (2, 4, 8, 16)
POOL_GROUP = 192
POOL_HALO = 16
DIL_PATTERNS = ((128, 1), (512, 4), (2048, 16))
N_GROUPS = 3
GROUP_W = 256
BAND = 128
N_A_LAYERS = 2
D_FF = 2816
ROPE_THETA = 10000.0
EPS = 1e-6
NEG = -1e30
SCALE = HEAD_DIM ** -0.5
N_DEV = 8

ADAM_LR = 0.001
ADAM_B1 = 0.9
ADAM_B2 = 0.999
ADAM_EPS = 1e-08
ADAM_WD = 0.01
ADAM_STEP = 10

VMEM_LIMIT_BYTES = 56 * 1024 * 1024
MESH = pl.DeviceIdType.MESH

NN = (((1,), (0,)), ((), ()))
NT = (((1,), (1,)), ((), ()))
TN = (((0,), (0,)), ((), ()))


def _params(sem=None):
    return pltpu.CompilerParams(dimension_semantics=sem, vmem_limit_bytes=VMEM_LIMIT_BYTES)


def _tile(n, cands):
    for c in cands:
        if n % c == 0:
            return c
    return n


def _sds(shape, dtype):
    return jax.ShapeDtypeStruct(tuple(shape), dtype)


def _rms_r(v):
    return lax.rsqrt(jnp.mean(v * v, axis=-1, keepdims=True) + EPS)


def rms_matmul(x, gain, w, *, name, out_dtype, transposed=False, after=None, rope=None):
    M, K = x.shape
    N = w.shape[0] if transposed else w.shape[1]
    tm = min(512, M)
    order = [] if after is None else [after]
    tables = [] if rope is None else list(rope)
    rot_spec = [] if rope is None else [pl.BlockSpec((tm, MAIN_W), lambda i: (i, 0))]
    rot_shape = [] if rope is None else [_sds((M, MAIN_W), f32)]

    def body(x_ref, g_ref, w_ref, *refs):
        z_ref, h_ref = refs[len(tables) + len(order):][:2]
        xv = x_ref[...]
        h = (xv * _rms_r(xv) * g_ref[...]).astype(bf16)
        h_ref[...] = h
        z = lax.dot_general(h, w_ref[...], NT if transposed else NN, preferred_element_type=f32)
        z_ref[...] = z.astype(z_ref.dtype)
        if tables:
            c = jnp.tile(refs[0][...], (1, MAIN_W // 128))
            s = jnp.tile(refs[1][...], (1, MAIN_W // 128))
            zr = z[:, :MAIN_W]
            refs[-1][...] = zr * c + _swap_halves(zr) * s

    tab = pl.BlockSpec((tm, 128), lambda i: (i, 0))
    return pl.pallas_call(
        body, name=name, grid=(M // tm,),
        in_specs=[pl.BlockSpec((tm, K), lambda i: (i, 0)),
                  pl.BlockSpec((1, K), lambda i: (0, 0)),
                  pl.BlockSpec(w.shape, lambda i: (0, 0))] + [tab] * len(tables)
        + [pl.BlockSpec(memory_space=pl.ANY)] * len(order),
        out_specs=[pl.BlockSpec((tm, N), lambda i: (i, 0)), pl.BlockSpec((tm, K), lambda i: (i, 0))] + rot_spec,
        out_shape=[_sds((M, N), out_dtype), _sds((M, K), bf16)] + rot_shape,
        compiler_params=_params(("parallel",)),
    )(x, gain, w, *tables, *order)


def matmul_rms_res(a, w, gain, res, *, name, target=None):
    M, K = a.shape
    N = w.shape[1]
    tm = min(512, M)
    goal = [] if target is None else [target]

    def body(a_ref, w_ref, g_ref, r_ref, *refs):
        y_ref, x_ref = refs[len(goal):][:2]
        y = jnp.dot(a_ref[...], w_ref[...], preferred_element_type=f32)
        y_ref[...] = y.astype(bf16)
        x = r_ref[...] + y * _rms_r(y) * g_ref[...]
        if not goal:
            x_ref[...] = x
            return
        e = x - refs[0][...]
        x_ref[...] = e * (1.0 / N)
        _accumulate(refs[-1], jnp.sum(jnp.sum(e * e, axis=0, keepdims=True), axis=1, keepdims=True))

    row = pl.BlockSpec((tm, N), lambda i: (i, 0))
    return pl.pallas_call(
        body, name=name, grid=(M // tm,),
        in_specs=[pl.BlockSpec((tm, K), lambda i: (i, 0)),
                  pl.BlockSpec((K, N), lambda i: (0, 0)),
                  pl.BlockSpec((1, N), lambda i: (0, 0)),
                  row] + [row] * len(goal),
        out_specs=[row, row] + [pl.BlockSpec((8, 128), lambda i: (0, 0))] * len(goal),
        out_shape=[_sds((M, N), bf16), _sds((M, N), f32)] + [_sds((8, 128), f32)] * len(goal),
        compiler_params=_params(("arbitrary",) if goal else ("parallel",)),
    )(a, w, gain, res, *goal)


def matmul(a, b, dims, *, name, out_dtype):
    if dims is TN:
        K, M = a.shape
        tm = _tile(M, (512, 256, 128))
        a_spec = pl.BlockSpec((K, tm), lambda i: (0, i))
    else:
        M, K = a.shape
        tm = _tile(M, (1024, 512, 256, 128))
        a_spec = pl.BlockSpec((tm, K), lambda i: (i, 0))
    N = b.shape[0] if dims is NT else b.shape[1]

    def body(a_ref, b_ref, o_ref):
        o_ref[...] = lax.dot_general(a_ref[...].astype(bf16), b_ref[...].astype(bf16), dims,
                                     preferred_element_type=f32).astype(o_ref.dtype)

    return pl.pallas_call(
        body, name=name, grid=(M // tm,),
        in_specs=[a_spec, pl.BlockSpec(b.shape, lambda i: (0, 0))],
        out_specs=pl.BlockSpec((tm, N), lambda i: (i, 0)),
        out_shape=_sds((M, N), out_dtype),
        compiler_params=_params(("parallel",)),
    )(a, b)


def rms_gate_up(x, gain, wt, *, name):
    M, K = x.shape
    tm = min(2048, M)
    tn = _tile(D_FF, (256, 128))
    nj = D_FF // tn

    def body(x_ref, gn_ref, wg_ref, wu_ref, g_ref, u_ref, a_ref, h_ref):
        @pl.when(pl.program_id(1) == 0)
        def _():
            xv = x_ref[...]
            h_ref[...] = (xv * _rms_r(xv) * gn_ref[...]).astype(bf16)

        h = h_ref[...]
        g = lax.dot_general(h, wg_ref[...], NT, preferred_element_type=f32).astype(bf16)
        u = lax.dot_general(h, wu_ref[...], NT, preferred_element_type=f32).astype(bf16)
        g_ref[...] = g
        u_ref[...] = u
        a_ref[...] = g * (1.0 / (1.0 + jnp.exp(-g))) * u

    col = pl.BlockSpec((tm, tn), lambda i, j: (i, j))
    return pl.pallas_call(
        body, name=name, grid=(M // tm, nj),
        in_specs=[pl.BlockSpec((tm, K), lambda i, j: (i, 0)),
                  pl.BlockSpec((1, K), lambda i, j: (0, 0)),
                  pl.BlockSpec((tn, K), lambda i, j: (j, 0)),
                  pl.BlockSpec((tn, K), lambda i, j: (j + nj, 0))],
        out_specs=[col, col, col, pl.BlockSpec((tm, K), lambda i, j: (i, 0))],
        out_shape=[_sds((M, D_FF), bf16)] * 3 + [_sds((M, K), bf16)],
        compiler_params=_params(("parallel", "arbitrary")),
    )(x, gain, wt, wt)


def _rms_bwd_math(yv, gain, dn):
    r = _rms_r(yv)
    q = dn * gain
    dy = r * q - yv * (r * r * r) * jnp.mean(q * yv, axis=-1, keepdims=True)
    return dy, jnp.sum(dn * yv * r, axis=0, keepdims=True)


def _accumulate(ref, val):
    @pl.when(pl.program_id(0) == 0)
    def _():
        ref[...] = jnp.zeros_like(ref)

    ref[...] += val


def down_bwd(y, gain, dn, w_down, g, u, *, name, after=None):
    M, K = y.shape
    tm = min(512, M)
    order = [] if after is None else [after]

    def body(y_ref, gn_ref, dn_ref, w_ref, g_ref, u_ref, *refs):
        dy_ref, o_ref, dg_ref = refs[len(order):]
        dy, dgain = _rms_bwd_math(y_ref[...].astype(f32), gn_ref[...], dn_ref[...])
        dy = dy.astype(bf16)
        dy_ref[...] = dy
        _accumulate(dg_ref, dgain)
        da = lax.dot_general(dy, w_ref[...], NT, preferred_element_type=f32).astype(bf16)
        g = g_ref[...]
        s = 1.0 / (1.0 + jnp.exp(-g))
        o_ref[:, :D_FF] = da * u_ref[...] * s * (1.0 + g * (1.0 - s))
        o_ref[:, D_FF:] = da * g * s

    row = pl.BlockSpec((tm, K), lambda i: (i, 0))
    vec = pl.BlockSpec((1, K), lambda i: (0, 0))
    wide = pl.BlockSpec((tm, D_FF), lambda i: (i, 0))
    return pl.pallas_call(
        body, name=name, grid=(M // tm,),
        in_specs=[row, vec, row, pl.BlockSpec((D_FF, K), lambda i: (0, 0)), wide, wide]
        + [pl.BlockSpec(memory_space=pl.ANY)] * len(order),
        out_specs=[row, pl.BlockSpec((tm, 2 * D_FF), lambda i: (i, 0)), vec],
        out_shape=[_sds((M, K), bf16), _sds((M, 2 * D_FF), bf16), _sds((1, K), f32)],
        compiler_params=_params(("arbitrary",)),
    )(y, gain, dn, w_down, g, u, *order)


def rms_bwd_matmul(y, gain, dn, w, dims, *, name, after=None):
    M, K = y.shape
    N = w.shape[0] if dims is NT else w.shape[1]
    tm = min(1024, M)
    order = [] if after is None else [after]

    def body(y_ref, gn_ref, dn_ref, w_ref, *refs):
        dy_ref, o_ref, dg_ref = refs[len(order):]
        dy, dgain = _rms_bwd_math(y_ref[...].astype(f32), gn_ref[...], dn_ref[...].astype(f32))
        dy = dy.astype(bf16)
        dy_ref[...] = dy
        _accumulate(dg_ref, dgain)
        o_ref[...] = lax.dot_general(dy, w_ref[...], dims, preferred_element_type=f32).astype(bf16)

    row = pl.BlockSpec((tm, K), lambda i: (i, 0))
    vec = pl.BlockSpec((1, K), lambda i: (0, 0))
    return pl.pallas_call(
        body, name=name, grid=(M // tm,),
        in_specs=[row, vec, row, pl.BlockSpec(w.shape, lambda i: (0, 0))]
        + [pl.BlockSpec(memory_space=pl.ANY)] * len(order),
        out_specs=[row, pl.BlockSpec((tm, N), lambda i: (i, 0)), vec],
        out_shape=[_sds((M, K), bf16), _sds((M, N), bf16), _sds((1, K), f32)],
        compiler_params=_params(("arbitrary",)),
    )(y, gain, dn, w, *order)


def matmul_rms_bwd(a, b, dims, y, gain, res, *, name, after=None):
    M, K = a.shape
    N = y.shape[1]
    tm = min(512, M)
    order = [] if after is None else [after]

    def body(a_ref, b_ref, y_ref, gn_ref, r_ref, *refs):
        dx_ref, dg_ref = refs[len(order):]
        dn = lax.dot_general(a_ref[...].astype(bf16), b_ref[...], dims, preferred_element_type=f32)
        dy, dgain = _rms_bwd_math(y_ref[...], gn_ref[...], dn)
        dx_ref[...] = dy + r_ref[...]
        _accumulate(dg_ref, dgain)

    row = pl.BlockSpec((tm, N), lambda i: (i, 0))
    vec = pl.BlockSpec((1, N), lambda i: (0, 0))
    return pl.pallas_call(
        body, name=name, grid=(M // tm,),
        in_specs=[pl.BlockSpec((tm, K), lambda i: (i, 0)), pl.BlockSpec(b.shape, lambda i: (0, 0)), row, vec, row]
        + [pl.BlockSpec(memory_space=pl.ANY)] * len(order),
        out_specs=[row, vec],
        out_shape=[_sds((M, N), f32), _sds((1, N), f32)],
        compiler_params=_params(("arbitrary",)),
    )(a, b, y, gain, res, *order)


def rms_bwd(y, gain, dn, res, *, name, out_dtype, after=None):
    M, N = y.shape
    tm = min(512, M)
    has_res = res is not None
    order = [] if after is None else [after]

    def body(*refs):
        y_ref, g_ref, dn_ref = refs[:3]
        r_ref = refs[3] if has_res else None
        dy_ref, dg_ref = refs[-2:]
        dy, dgain = _rms_bwd_math(y_ref[...].astype(f32), g_ref[...], dn_ref[...].astype(f32))
        if has_res:
            dy = dy + r_ref[...]
        dy_ref[...] = dy.astype(dy_ref.dtype)
        _accumulate(dg_ref, dgain)

    row = pl.BlockSpec((tm, N), lambda i: (i, 0))
    vec = pl.BlockSpec((1, N), lambda i: (0, 0))
    args = [y, gain, dn] + ([res] if has_res else []) + order
    return pl.pallas_call(
        body, name=name, grid=(M // tm,),
        in_specs=[row, vec, row] + ([row] if has_res else []) + [pl.BlockSpec(memory_space=pl.ANY)] * len(order),
        out_specs=[row, vec],
        out_shape=[_sds((M, N), out_dtype), _sds((1, N), f32)],
        compiler_params=_params(("arbitrary",)),
    )(*args)


def _pool_select(a1, a2, a3, a4):
    col = lax.broadcasted_iota(jnp.int32, (1, MAIN_W), 1) // POOL_GROUP
    return jnp.where(col == 0, a1, jnp.where(col == 1, a2, jnp.where(col == 2, a3, a4)))


def _pool_count(t):
    col = lax.broadcasted_iota(jnp.int32, (1, MAIN_W), 1) // POOL_GROUP
    win = jnp.where(col == 0, 2, jnp.where(col == 1, 4, jnp.where(col == 2, 8, 16)))
    return jnp.minimum(t + 1, win).astype(f32)


def pool_fwd(z, wbd, scale, *, name):
    M = z.shape[0]
    tm = 512
    nper = SEQ // tm
    hb = tm // POOL_HALO

    def body(zc_ref, zh_ref, w_ref, s_ref, p_ref, y_ref):
        i = pl.program_id(0)
        seq_blk = i % nper
        halo = jnp.where(seq_blk == 0, 0.0, zh_ref[...].astype(f32))
        u = zc_ref[...].astype(f32)
        ext = jnp.concatenate([halo, u], axis=0)
        a1 = ext + pltpu.roll(ext, 1, 0)
        a2 = a1 + pltpu.roll(a1, 2, 0)
        a3 = a2 + pltpu.roll(a2, 4, 0)
        a4 = a3 + pltpu.roll(a3, 8, 0)
        sums = _pool_select(a1, a2, a3, a4)[POOL_HALO:]
        t = seq_blk * tm + lax.broadcasted_iota(jnp.int32, (tm, 1), 0)
        p = (sums / _pool_count(t) - u).astype(bf16)
        p_ref[...] = p
        y_ref[...] = (jnp.dot(p, w_ref[...], preferred_element_type=f32) * s_ref[...]).astype(bf16)

    return pl.pallas_call(
        body, name=name, grid=(M // tm,),
        in_specs=[pl.BlockSpec((tm, MAIN_W), lambda i: (i, 0)),
                  pl.BlockSpec((POOL_HALO, MAIN_W), lambda i: (jnp.maximum(i * hb - 1, 0), 0)),
                  pl.BlockSpec((MAIN_W, MAIN_W), lambda i: (0, 0)),
                  pl.BlockSpec((1, MAIN_W), lambda i: (0, 0))],
        out_specs=[pl.BlockSpec((tm, MAIN_W), lambda i: (i, 0)),
                   pl.BlockSpec((tm, MAIN_W), lambda i: (i, 0))],
        out_shape=[_sds((M, MAIN_W), bf16), _sds((M, D_MODEL), bf16)],
        compiler_params=_params(("parallel",)),
    )(z, z, wbd, scale)


def pool_bwd(dyc, p, wbd, scale, *, name):
    M = p.shape[0]
    tm = 512
    nper = SEQ // tm
    hb = tm // POOL_HALO
    last_hb = M // POOL_HALO - 1

    def body(dy_ref, dyh_ref, p_ref, w_ref, s_ref, dz_ref, dw_ref, ds_ref):
        i = pl.program_id(0)
        seq_blk = i % nper
        dy = dy_ref[...].astype(f32)
        pv = p_ref[...]
        w = w_ref[...]
        sc = s_ref[...]

        @pl.when(i == 0)
        def _():
            dw_ref[...] = jnp.zeros_like(dw_ref)
            ds_ref[...] = jnp.zeros_like(ds_ref)

        v = jnp.dot(pv, w, preferred_element_type=f32)
        ds_ref[...] += jnp.sum(dy * v, axis=0, keepdims=True)
        dv = (dy * sc).astype(bf16)
        dw_ref[...] += lax.dot_general(pv, dv, TN, preferred_element_type=f32)
        dp = lax.dot_general(dv, w, NT, preferred_element_type=f32)
        dvh = jnp.where(seq_blk == nper - 1, 0.0, dyh_ref[...].astype(f32) * sc).astype(bf16)
        dph = lax.dot_general(dvh, w, NT, preferred_element_type=f32)
        ext = jnp.concatenate([dp, dph], axis=0)
        n = tm + POOL_HALO
        t = seq_blk * tm + lax.broadcasted_iota(jnp.int32, (n, 1), 0)
        e = ext / _pool_count(t)
        b1 = e + pltpu.roll(e, n - 1, 0)
        b2 = b1 + pltpu.roll(b1, n - 2, 0)
        b3 = b2 + pltpu.roll(b2, n - 4, 0)
        b4 = b3 + pltpu.roll(b3, n - 8, 0)
        dz_ref[...] = (_pool_select(b1, b2, b3, b4)[:tm] - dp).astype(dz_ref.dtype)

    return pl.pallas_call(
        body, name=name, grid=(M // tm,),
        in_specs=[pl.BlockSpec((tm, MAIN_W), lambda i: (i, 0)),
                  pl.BlockSpec((POOL_HALO, MAIN_W), lambda i: (jnp.minimum((i + 1) * hb, last_hb), 0)),
                  pl.BlockSpec((tm, MAIN_W), lambda i: (i, 0)),
                  pl.BlockSpec((MAIN_W, MAIN_W), lambda i: (0, 0)),
                  pl.BlockSpec((1, MAIN_W), lambda i: (0, 0))],
        out_specs=[pl.BlockSpec((tm, MAIN_W), lambda i: (i, 0)),
                   pl.BlockSpec((MAIN_W, MAIN_W), lambda i: (0, 0)),
                   pl.BlockSpec((1, MAIN_W), lambda i: (0, 0))],
        out_shape=[_sds((M, D_MODEL), bf16), _sds((MAIN_W, MAIN_W), f32), _sds((1, MAIN_W), f32)],
        compiler_params=_params(("arbitrary",)),
    )(dyc, dyc, p, wbd, scale)


def _mem_heads(q, kv):
    first = _first_head()
    for pr in range(N_MEM_HEADS // 2):
        cols = slice(pr * PAIR_W, (pr + 1) * PAIR_W)
        qp = q[:, cols] * SCALE
        kp = kv[:, cols]
        vp = kv[:, MEM_W + pr * PAIR_W: MEM_W + (pr + 1) * PAIR_W]
        for hh in range(2):
            lm = first if hh == 0 else ~first
            qm = jnp.where(lm, qp, 0.0).astype(bf16)
            s = lax.dot_general(qm, kp, NT, preferred_element_type=f32)
            e = jnp.exp(s - jnp.max(s, axis=-1, keepdims=True))
            yield lm, qm, kp, vp, e, jnp.sum(e, axis=-1, keepdims=True)


def memattn_fwd(z, kvm, ycat, *, name, n_seq):
    M = z.shape[0]
    tq = 1024
    nq = SEQ // tq

    def body(q_ref, kv_ref, _, o_ref):
        first = _first_head()
        outs = []
        for lm, _, _, vp, e, l in _mem_heads(q_ref[...], kv_ref[...]):
            outs.append(jnp.dot(e.astype(bf16), vp, preferred_element_type=f32) * (1.0 / l))
        pairs = [jnp.where(first, outs[2 * pr], outs[2 * pr + 1]) for pr in range(N_MEM_HEADS // 2)]
        o_ref[...] = jnp.concatenate(pairs, axis=1).astype(bf16)

    return pl.pallas_call(
        body, name=name, grid=(n_seq, nq),
        in_specs=[pl.BlockSpec((tq, MEM_W), lambda b, i: (b * nq + i, 3)),
                  pl.BlockSpec((N_MEM, 2 * MEM_W), lambda b, i: (b, 0)),
                  pl.BlockSpec(memory_space=pl.ANY)],
        out_specs=pl.BlockSpec((tq, MEM_W), lambda b, i: (b * nq + i, 3)),
        out_shape=_sds((M, D_MODEL), bf16),
        input_output_aliases={2: 0},
        compiler_params=_params(("parallel", "parallel")),
    )(z, kvm, ycat)


def memattn_bwd(z, kvm, dyc, dz, *, name, n_seq):
    M = z.shape[0]
    tq = 1024
    nq = SEQ // tq

    def body(q_ref, kv_ref, dy_ref, _, dq_ref, dkv_ref):
        first = _first_head()
        dy = dy_ref[...].astype(f32)
        dqs, dks, dvs = [], [], []
        for h, (lm, qm, kp, vp, e, l) in enumerate(_mem_heads(q_ref[...], kv_ref[...])):
            pr = h // 2
            p = e * (1.0 / l)
            dym = jnp.where(lm, dy[:, pr * PAIR_W:(pr + 1) * PAIR_W], 0.0).astype(bf16)
            dp = lax.dot_general(dym, vp, NT, preferred_element_type=f32)
            ds = (p * (dp - jnp.sum(dp * p, axis=-1, keepdims=True))).astype(bf16)
            dqs.append(jnp.dot(ds, kp, preferred_element_type=f32) * SCALE)
            dk = lax.dot_general(ds, qm, TN, preferred_element_type=f32)
            dv = lax.dot_general(p.astype(bf16), dym, TN, preferred_element_type=f32)
            if h % 2 == 0:
                dks.append(dk)
                dvs.append(dv)
            else:
                dks[pr] = dks[pr] + dk
                dvs[pr] = dvs[pr] + dv
        pairs = [jnp.where(first, dqs[2 * pr], dqs[2 * pr + 1]) for pr in range(N_MEM_HEADS // 2)]
        dq_ref[...] = jnp.concatenate(pairs, axis=1).astype(bf16)

        @pl.when(pl.program_id(1) == 0)
        def _():
            dkv_ref[...] = jnp.zeros_like(dkv_ref)

        dkv_ref[...] += jnp.concatenate(dks + dvs, axis=1)

    return pl.pallas_call(
        body, name=name, grid=(n_seq, nq),
        in_specs=[pl.BlockSpec((tq, MEM_W), lambda b, i: (b * nq + i, 3)),
                  pl.BlockSpec((N_MEM, 2 * MEM_W), lambda b, i: (b, 0)),
                  pl.BlockSpec((tq, MEM_W), lambda b, i: (b * nq + i, 3)),
                  pl.BlockSpec(memory_space=pl.ANY)],
        out_specs=[pl.BlockSpec((tq, MEM_W), lambda b, i: (b * nq + i, 3)),
                   pl.BlockSpec((N_MEM, 2 * MEM_W), lambda b, i: (b, 0))],
        out_shape=[_sds((M, D_MODEL), bf16), _sds((n_seq * N_MEM, 2 * MEM_W), f32)],
        input_output_aliases={3: 0},
        compiler_params=_params(("parallel", "arbitrary")),
    )(z, kvm, dyc, dz)


def rope_tables(pos, *, name):
    M = pos.shape[0]
    tm = min(1024, M)
    half = HEAD_DIM // 2
    inv = ROPE_THETA ** (-np.arange(half, dtype=np.float64) / half)
    inv128 = jnp.asarray(np.tile(inv, 4)[None, :], f32)
    sign128 = jnp.asarray(np.tile(np.concatenate([-np.ones(half), np.ones(half)]), 2)[None, :], f32)

    def body(p_ref, f_ref, s_ref, cos_ref, sin_ref):
        ang = p_ref[...] * f_ref[...]
        cos_ref[...] = jnp.cos(ang)
        sin_ref[...] = jnp.sin(ang) * s_ref[...]

    return pl.pallas_call(
        body, name=name, grid=(M // tm,),
        in_specs=[pl.BlockSpec((tm, 1), lambda i: (i, 0)),
                  pl.BlockSpec((1, 128), lambda i: (0, 0)),
                  pl.BlockSpec((1, 128), lambda i: (0, 0))],
        out_specs=[pl.BlockSpec((tm, 128), lambda i: (i, 0)),
                   pl.BlockSpec((tm, 128), lambda i: (i, 0))],
        out_shape=[_sds((M, 128), f32), _sds((M, 128), f32)],
        compiler_params=_params(("parallel",)),
    )(pos, inv128, sign128)


def _swap_halves(x):
    w = x.shape[1]
    first = (lax.broadcasted_iota(jnp.int32, (1, w), 1) % HEAD_DIM) < (HEAD_DIM // 2)
    return jnp.where(first, pltpu.roll(x, w - HEAD_DIM // 2, 1), pltpu.roll(x, HEAD_DIM // 2, 1))


def group_sum(groups, cos, sin, *, name, rotate, width, col_block=0, into=None):
    M = groups[0][0].shape[0]
    tm = min(512, M)
    counts = [len(g) for g in groups]
    flat = [a for g in groups for a in g]
    extra = [] if into is None else [into]

    def body(*refs):
        part_refs = refs[:len(flat)]
        c_ref, s_ref = refs[len(flat):len(flat) + 2]
        o_ref = refs[-1]
        cols, k = [], 0
        for n in counts:
            acc = part_refs[k][...]
            for r in part_refs[k + 1:k + n]:
                acc = acc + r[...]
            cols.append(acc)
            k += n
        d = jnp.concatenate(cols, axis=1)
        if rotate:
            c = jnp.tile(c_ref[...], (1, MAIN_W // 128))
            s = jnp.tile(s_ref[...], (1, MAIN_W // 128))
            d = d * c - _swap_halves(d) * s
        o_ref[...] = d.astype(bf16)

    part = pl.BlockSpec((tm, GROUP_W), lambda i: (i, 0))
    tab = pl.BlockSpec((tm, 128), lambda i: (i, 0))
    return pl.pallas_call(
        body, name=name, grid=(M // tm,),
        in_specs=[part] * len(flat) + [tab, tab] + [pl.BlockSpec(memory_space=pl.ANY)] * len(extra),
        out_specs=pl.BlockSpec((tm, MAIN_W), lambda i: (i, col_block)),
        out_shape=_sds((M, width), bf16),
        input_output_aliases={len(flat) + 2: 0} if extra else {},
        compiler_params=_params(("parallel",)),
    )(*flat, cos, sin, *extra)


PAIR_W = 2 * HEAD_DIM
MIN_BLOCKS = 8


FWD_TOGETHER = 4
BWD_TOGETHER = 4


def _dil_geometry(dil):
    nsub = max(dil, MIN_BLOCKS)
    tb = BAND * nsub
    return nsub, tb, SEQ // tb


REGROUP = 4


class _Regrouped:
    def __init__(self, ref):
        self.ref = ref
        self.shape = ref.shape

    def fill(self, src):
        q = self.shape[0] // REGROUP
        for r0 in range(REGROUP):
            self.ref[r0 * q:(r0 + 1) * q, :] = src[pl.ds(r0, q, stride=REGROUP), :]

    def drain(self, dst):
        q = self.shape[0] // REGROUP
        for r0 in range(REGROUP):
            dst[pl.ds(r0, q, stride=REGROUP), :] = self.ref[r0 * q:(r0 + 1) * q, :]

    def rows(self, sub, dil):
        nl, r = divmod(sub, dil)
        start = (r % REGROUP) * (self.shape[0] // REGROUP) + r // REGROUP + nl * BAND * (dil // REGROUP)
        return pl.ds(start, BAND, stride=dil // REGROUP)


def _regroups(dil):
    return dil % (4 * REGROUP) == 0


def _rows(ref, sub, dil):
    if isinstance(ref, _Regrouped):
        return ref.ref[ref.rows(sub, dil), :]
    if dil == 1:
        return ref[sub * BAND:(sub + 1) * BAND, :]
    nl, r = divmod(sub, dil)
    return ref[pl.ds(nl * BAND * dil + r, BAND, stride=dil), :]


def _store_rows(ref, sub, dil, val):
    if isinstance(ref, _Regrouped):
        ref.ref[ref.rows(sub, dil), :] = val
    elif dil == 1:
        ref[sub * BAND:(sub + 1) * BAND, :] = val
    else:
        nl, r = divmod(sub, dil)
        ref[pl.ds(nl * BAND * dil + r, BAND, stride=dil), :] = val


def _keys(prev_ref, own_ref, sub, dil):
    nsub = own_ref.shape[0] // BAND
    if sub >= dil:
        prev = _rows(own_ref, sub - dil, dil)
    elif prev_ref is None:
        return _rows(own_ref, sub, dil)
    else:
        prev = _rows(prev_ref, nsub - dil + sub, dil)
    return jnp.concatenate([prev, _rows(own_ref, sub, dil)], axis=0)


def _band_mask(nkeys, has_prev):
    i = lax.broadcasted_iota(jnp.int32, (BAND, nkeys), 0)
    j = lax.broadcasted_iota(jnp.int32, (BAND, nkeys), 1)
    if nkeys == BAND:
        return j <= i
    return (j >= i) & (j <= i + BAND) & (has_prev | (j >= BAND))


def _first_head():
    return lax.broadcasted_iota(jnp.int32, (1, PAIR_W), 1) < HEAD_DIM


def _col(x, hh):
    return x[:, hh * HEAD_DIM:hh * HEAD_DIM + 1]


def _pair_spec(tb, nblk, col0, which):
    def idx(b, p, i):
        if which < 0:
            i = jnp.maximum(i - 1, 0)
        elif which > 0:
            i = jnp.minimum(i + 1, nblk - 1)
        return (b * nblk + i, col0 + p)
    return pl.BlockSpec((tb, PAIR_W), idx)


def dil_fwd(q, k, kv, g, dil, *, name, n_seq):
    M = q.shape[0]
    nsub, tb, nblk = _dil_geometry(dil)
    with_prev = nblk > 1

    regroup = _regroups(dil)
    assert not (regroup and with_prev)

    def body(*refs):
        if with_prev:
            q_ref, ko_ref, vo_ref, kp_ref, vp_ref, o_ref, l_ref = refs
        else:
            (q_ref, ko_ref, vo_ref, o_ref, l_ref), kp_ref, vp_ref = refs[:5], None, None
        outs_to = ()
        if regroup:
            copies = [_Regrouped(s) for s in refs[5:]]
            for c, src in zip(copies, (q_ref, ko_ref, vo_ref)):
                c.fill(src)
            outs_to = ((copies[3], o_ref), (copies[4], l_ref))
            q_ref, ko_ref, vo_ref, o_ref, l_ref = copies
        first = _first_head()
        blk = pl.program_id(2)
        for sub0 in range(0, nsub, FWD_TOGETHER):
            subs = range(sub0, sub0 + FWD_TOGETHER)
            scores, values = [], []
            for sub in subs:
                qs = _rows(q_ref, sub, dil) * SCALE
                kc = _keys(kp_ref, ko_ref, sub, dil).astype(bf16)
                values.append(_keys(vp_ref, vo_ref, sub, dil).astype(bf16))
                has_prev = True if sub >= dil else blk > 0
                mask = _band_mask(kc.shape[0], has_prev)
                for hh in range(2):
                    qm = jnp.where(first if hh == 0 else ~first, qs, 0.0).astype(bf16)
                    scores.append(jnp.where(mask, lax.dot_general(qm, kc, NT, preferred_element_type=f32), NEG))
            soft = []
            for s in scores:
                m = jnp.max(s, axis=-1, keepdims=True)
                e = jnp.exp(s - m)
                l = jnp.sum(e, axis=-1, keepdims=True)
                soft.append((e.astype(bf16), 1.0 / l, jnp.broadcast_to(m + jnp.log(l), (BAND, PAIR_W))))
            outs = [jnp.dot(e, values[n // 2], preferred_element_type=f32) * inv for n, (e, inv, _) in enumerate(soft)]
            for n, sub in enumerate(subs):
                _store_rows(o_ref, sub, dil, jnp.where(first, outs[2 * n], outs[2 * n + 1]))
                _store_rows(l_ref, sub, dil, jnp.where(first, soft[2 * n][2], soft[2 * n + 1][2]))
        for c, dst in outs_to:
            c.drain(dst)

    ins = [(q, 2 * g, 0), (k, 2 * g, 0), (kv, 6 + 2 * g, 0)]
    if with_prev:
        ins += [(k, 2 * g, -1), (kv, 6 + 2 * g, -1)]
    out = _pair_spec(tb, nblk, 0, 0)
    return pl.pallas_call(
        body, name=name, grid=(n_seq, 2, nblk),
        in_specs=[_pair_spec(tb, nblk, c, w) for _, c, w in ins],
        out_specs=[out, out],
        out_shape=[_sds((M, GROUP_W), f32)] * 2,
        scratch_shapes=[pltpu.VMEM((tb, PAIR_W), f32)] * (5 if regroup else 0),
        compiler_params=_params(("parallel", "parallel", "arbitrary")),
    )(*[a for a, _, _ in ins])


def combine_fwd(os_, lses, *, name):
    M = os_[0].shape[0]
    tm = min(512, M)

    def body(o0, o1, o2, l0, l1, l2, y_ref):
        ls = [l0[...], l1[...], l2[...]]
        m = jnp.maximum(jnp.maximum(ls[0], ls[1]), ls[2])
        es = [jnp.exp(l - m) for l in ls]
        inv = 1.0 / (es[0] + es[1] + es[2])
        y_ref[...] = jnp.concatenate([o[...] * e * inv for o, e in zip((o0, o1, o2), es)], axis=1).astype(bf16)

    part = pl.BlockSpec((tm, GROUP_W), lambda i: (i, 0))
    return pl.pallas_call(
        body, name=name, grid=(M // tm,),
        in_specs=[part] * 6,
        out_specs=pl.BlockSpec((tm, MAIN_W), lambda i: (i, 0)),
        out_shape=_sds((M, D_MODEL), bf16),
        compiler_params=_params(("parallel",)),
    )(*os_, *lses)


def combine_bwd(dyc, os_, lses, *, name):
    M = os_[0].shape[0]
    tm = min(512, M)

    def body(dy_ref, o0, o1, o2, l0, l1, l2, d0, d1, d2, c0, c1, c2):
        r = lax.broadcasted_iota(jnp.int32, (GROUP_W, GROUP_W), 0) // HEAD_DIM
        c = lax.broadcasted_iota(jnp.int32, (GROUP_W, GROUP_W), 1) // HEAD_DIM
        ones = (r == c).astype(bf16)
        dy = dy_ref[...].astype(f32)
        ls = [l0[...], l1[...], l2[...]]
        m = jnp.maximum(jnp.maximum(ls[0], ls[1]), ls[2])
        es = [jnp.exp(l - m) for l in ls]
        inv = 1.0 / (es[0] + es[1] + es[2])
        total = 0.0
        alphas = []
        for g, (o, e, d_ref) in enumerate(zip((o0, o1, o2), es, (d0, d1, d2))):
            a = e * inv
            dyg = dy[:, g * GROUP_W:(g + 1) * GROUP_W]
            d_ref[...] = dyg * a
            prod = dyg * o[...]
            hi = prod.astype(bf16)
            lo = (prod - hi.astype(f32)).astype(bf16)
            dsum = jnp.dot(hi, ones, preferred_element_type=f32) + jnp.dot(lo, ones, preferred_element_type=f32)
            total = total + a * dsum
            alphas.append(a)
        for a, c_ref in zip(alphas, (c0, c1, c2)):
            c_ref[...] = -a * total

    part = pl.BlockSpec((tm, GROUP_W), lambda i: (i, 0))
    outs = pl.pallas_call(
        body, name=name, grid=(M // tm,),
        in_specs=[pl.BlockSpec((tm, MAIN_W), lambda i: (i, 0))] + [part] * 6,
        out_specs=[part] * 6,
        out_shape=[_sds((M, GROUP_W), f32)] * 6,
        compiler_params=_params(("parallel",)),
    )(dyc, *os_, *lses)
    return outs[:3], outs[3:]


def dil_bwd(q, k, kv, do, cc, lse, cos, sin, g, dil, *, name, n_seq, into=None):
    M = q.shape[0]
    nsub = SEQ // BAND
    per_res = nsub // dil
    extra = [] if into is None else [into]

    regroup = _regroups(dil)

    def body(q_ref, k_ref, v_ref, do_ref, c_ref, l_ref, cos_ref, sin_ref, *refs):
        dz_ref, dk_ref, dv_ref, dq_ref, *scratch = refs[len(extra):]
        dq_rows = dq_ref
        outs_to = ()
        if regroup:
            copies = [_Regrouped(s) for s in scratch]
            for c, src in zip(copies, (q_ref, k_ref, v_ref, do_ref, c_ref, l_ref)):
                c.fill(src)
            outs_to = tuple(zip(copies[6:], (dq_rows, dk_ref, dv_ref)))
            q_ref, k_ref, v_ref, do_ref, c_ref, l_ref, dq_ref, dk_ref, dv_ref = copies
        first = _first_head()
        order = [nl * dil + r for r in range(dil) for nl in range(per_res)]
        carry = None
        for at in range(0, nsub, BWD_TOGETHER):
            subs = order[at:at + BWD_TOGETHER]
            loaded, products = [], []
            for sub in subs:
                qs = _rows(q_ref, sub, dil) * SCALE
                dos = _rows(do_ref, sub, dil)
                kc = _keys(None, k_ref, sub, dil).astype(bf16)
                vc = _keys(None, v_ref, sub, dil).astype(bf16)
                mask = _band_mask(kc.shape[0], True)
                for hh in range(2):
                    lm = first if hh == 0 else ~first
                    qm = jnp.where(lm, qs, 0.0).astype(bf16)
                    dom = jnp.where(lm, dos, 0.0).astype(bf16)
                    loaded.append((qm, dom, kc))
                    products.append((jnp.where(mask, lax.dot_general(qm, kc, NT, preferred_element_type=f32), NEG),
                                     lax.dot_general(dom, vc, NT, preferred_element_type=f32)))
            weights = []
            for n, (s, dp) in enumerate(products):
                sub, hh = subs[n // 2], n % 2
                p = jnp.exp(s - _col(_rows(l_ref, sub, dil), hh))
                weights.append((p.astype(bf16), (p * (dp + _col(_rows(c_ref, sub, dil), hh))).astype(bf16)))
            results = []
            for (pb, ds), (qm, dom, kc) in zip(weights, loaded):
                results.append((jnp.dot(ds, kc, preferred_element_type=f32) * SCALE,
                                lax.dot_general(ds, qm, TN, preferred_element_type=f32),
                                lax.dot_general(pb, dom, TN, preferred_element_type=f32)))
            for n, sub in enumerate(subs):
                (dq0, dk0, dv0), (dq1, dk1, dv1) = results[2 * n], results[2 * n + 1]
                _store_rows(dq_ref, sub, dil, jnp.where(first, dq0, dq1))
                dkc, dvc = dk0 + dk1, dv0 + dv1
                if sub >= dil:
                    _store_rows(dk_ref, sub - dil, dil, carry[0] + dkc[:BAND])
                    _store_rows(dv_ref, sub - dil, dil, carry[1] + dvc[:BAND])
                    carry = (dkc[BAND:], dvc[BAND:])
                else:
                    carry = (dkc, dvc)
                if sub + dil >= nsub:
                    _store_rows(dk_ref, sub, dil, carry[0])
                    _store_rows(dv_ref, sub, dil, carry[1])
        for c, dst in outs_to:
            c.drain(dst)
        d = dq_rows[...]
        dz_ref[...] = (d * cos_ref[...] - _swap_halves(d) * sin_ref[...]).astype(bf16)

    def spec(col0):
        return pl.BlockSpec((SEQ, PAIR_W), lambda b, p: (b, col0 + p))

    tab = pl.BlockSpec((SEQ, PAIR_W), lambda b, p: (b, 0))
    out = spec(0)
    return pl.pallas_call(
        body, name=name, grid=(n_seq, 2),
        in_specs=[spec(2 * g), spec(2 * g), spec(6 + 2 * g), spec(0), spec(0), spec(0), tab, tab]
        + [pl.BlockSpec(memory_space=pl.ANY)] * len(extra),
        out_specs=[spec(2 * g), out, out],
        out_shape=[_sds((M, D_MODEL), bf16)] + [_sds((M, GROUP_W), f32)] * 2,
        input_output_aliases={8: 0} if extra else {},
        scratch_shapes=[pltpu.VMEM((SEQ, PAIR_W), f32)] * (10 if regroup else 1),
        compiler_params=_params(("parallel", "parallel")),
    )(q, k, kv, do, cc, lse, cos, sin, *extra)


def _blockdiag(wp):
    out = jnp.zeros((MAIN_W, MAIN_W), wp.dtype)
    for gi in range(len(POOL_WINDOWS)):
        sl = slice(gi * POOL_GROUP, (gi + 1) * POOL_GROUP)
        out = out.at[sl, sl].set(wp[gi])
    return out


def _unblockdiag(w):
    return jnp.stack([w[gi * POOL_GROUP:(gi + 1) * POOL_GROUP, gi * POOL_GROUP:(gi + 1) * POOL_GROUP]
                      for gi in range(len(POOL_WINDOWS))])


def local_step(x, mem, positions, target, P, layer_weights, kv_weight, emit_grads):
    n_seq = x.shape[0]
    M = n_seq * SEQ
    xs = x.reshape(M, D_MODEL)
    mems = mem.reshape(n_seq * N_MEM, D_MODEL)
    pos = positions.reshape(M, 1).astype(f32)
    cos, sin = rope_tables(pos, name="rope_tables")
    gains = P["norm_gains"]

    def gain(l, k):
        return gains[l, k].reshape(1, D_MODEL)

    saved = []
    kvs = None
    for l in range(DEPTH):
        W, started = layer_weights(l, "mix", xs)
        sv = {"x": xs, "W": W}
        z, h1, *qrot = rms_matmul(xs, gain(l, 0), W["w_in"], name=f"l{l}_in", out_dtype=bf16, after=started,
                                  rope=None if l < N_A_LAYERS else (cos, sin))
        kvm, mn = rms_matmul(mems, P["mem_norm"][l].reshape(1, D_MODEL), W["w_mem_kv"],
                             name=f"l{l}_memkv", out_dtype=bf16)
        sv.update(z=z, h1=h1, kvm=kvm, mn=mn)
        if l < N_A_LAYERS:
            wbd = _blockdiag(P["w_pool"][l].astype(bf16))
            psc = P["pool_scale"][l].reshape(1, MAIN_W)
            p, y_main = pool_fwd(z, wbd, psc, name=f"l{l}_pool")
            sv.update(p=p, wbd=wbd, psc=psc)
        else:
            (qrot,) = qrot
            os_, lses = [], []
            for g, (_, dil) in enumerate(DIL_PATTERNS):
                o, lse = dil_fwd(qrot, kvs["krot"], kvs["kv"], g, dil, name=f"l{l}_dil{g}", n_seq=n_seq)
                os_.append(o)
                lses.append(lse)
            y_main = combine_fwd(os_, lses, name=f"l{l}_comb")
            sv.update(qrot=qrot, os=os_, lses=lses)
        ycat = memattn_fwd(z, kvm, y_main, name=f"l{l}_memattn", n_seq=n_seq)
        y, x1 = matmul_rms_res(ycat, W["w_out"], gain(l, 1), xs, name=f"l{l}_out")
        W.update(layer_weights(l, "gu", x1)[0])
        fg, fu, a, h2 = rms_gate_up(x1, gain(l, 2), W["w_gate_up"], name=f"l{l}_gu")
        W.update(layer_weights(l, "down", a)[0])
        y2, x2, *sq = matmul_rms_res(a, W["w_down"], gain(l, 3), x1, name=f"l{l}_down",
                                     target=target.reshape(M, D_MODEL) if l == DEPTH - 1 else None)
        sv.update(ycat=ycat, y=y, x1=x1, fg=fg, fu=fu, h2=h2, a=a, y2=y2)
        saved.append(sv)
        xs = x2
        if l == N_A_LAYERS - 1:
            w_kv = kv_weight(xs)
            kv, hkv, krot = rms_matmul(xs, P["kv_norm"].reshape(1, D_MODEL), w_kv, name="kv_proj", out_dtype=f32,
                                       transposed=True, rope=(cos, sin))
            kvs = {"kv": kv, "hkv": hkv, "krot": krot, "x": xs, "w_kv": w_kv}

    dx, (sq,) = xs, sq

    G = {"mem_norm": [None] * DEPTH, "norm_gains": [[None] * 4 for _ in range(DEPTH)],
         "pool_scale": [None] * N_A_LAYERS}
    dk_parts = [[] for _ in range(N_GROUPS)]
    dv_parts = [[] for _ in range(N_GROUPS)]
    emitted = None

    for l in reversed(range(DEPTH)):
        sv = saved[l]
        W = sv["W"]
        gw = {}
        dy2, dgu, G["norm_gains"][l][3] = down_bwd(sv["y2"], gain(l, 3), dx, W["w_down"], sv["fg"], sv["fu"],
                                                   name=f"l{l}_b_dgu", after=emitted)
        gw["w_down"] = matmul(sv["a"], dy2, TN, name=f"l{l}_b_wd", out_dtype=bf16)
        gw["w_gate_up"] = matmul(dgu, sv["h2"], TN, name=f"l{l}_b_wgu", out_dtype=bf16)
        emitted = emit_grads(l, "ffn", gw)
        dx1, G["norm_gains"][l][2] = matmul_rms_bwd(dgu, W["w_gate_up"], NN, sv["x1"], gain(l, 2), dx,
                                                    name=f"l{l}_b_dh2", after=emitted)
        gw = {}
        dy, dycat, G["norm_gains"][l][1] = rms_bwd_matmul(sv["y"], gain(l, 1), dx1, W["w_out"], NT,
                                                          name=f"l{l}_b_dycat", after=emitted)
        gw["w_out"] = matmul(sv["ycat"], dy, TN, name=f"l{l}_b_wout", out_dtype=bf16)
        if l < N_A_LAYERS:
            dz, dwbd, dps = pool_bwd(dycat, sv["p"], sv["wbd"], sv["psc"], name=f"l{l}_b_pool")
            gw["w_pool"] = _unblockdiag(dwbd).reshape(MAIN_W, POOL_GROUP).astype(bf16)
            G["pool_scale"][l] = dps.reshape(MAIN_W)
        else:
            dos, ccs = combine_bwd(dycat, sv["os"], sv["lses"], name=f"l{l}_b_comb")
            dz = None
            for g, (_, dil) in enumerate(DIL_PATTERNS):
                args = (sv["qrot"], kvs["krot"], kvs["kv"], dos[g], ccs[g], sv["lses"][g], cos, sin, g, dil)
                dz, dk, dv = dil_bwd(*args, name=f"l{l}_b_dil{g}", n_seq=n_seq, into=dz)
                dk_parts[g].append(dk)
                dv_parts[g].append(dv)
        dz, dkvm = memattn_bwd(sv["z"], sv["kvm"], dycat, dz, name=f"l{l}_b_memattn", n_seq=n_seq)
        gw["w_mem_kv"] = matmul(sv["mn"], dkvm, TN, name=f"l{l}_b_wmkv", out_dtype=bf16)
        _, G["mem_norm"][l] = matmul_rms_bwd(dkvm, W["w_mem_kv"], NT, mems, P["mem_norm"][l].reshape(1, D_MODEL),
                                             mems, name=f"l{l}_b_dmn")
        gw["w_in"] = matmul(sv["h1"], dz, TN, name=f"l{l}_b_win", out_dtype=bf16)
        if l != N_A_LAYERS:
            emitted = emit_grads(l, "mix", gw)
        dx, G["norm_gains"][l][0] = matmul_rms_bwd(dz, W["w_in"], NT, sv["x"], gain(l, 0), dx1, name=f"l{l}_b_dh1",
                                                   after=emitted)
        if l == N_A_LAYERS:
            dkv = group_sum(dk_parts, cos, sin, name="b_ropek", rotate=True, width=2 * MAIN_W)
            dkv = group_sum(dv_parts, cos, sin, name="b_sumv", rotate=False, width=2 * MAIN_W, col_block=1, into=dkv)
            gw["w_kv"] = matmul(dkv, kvs["hkv"], TN, name="b_wkv", out_dtype=bf16)
            dx, gkn = matmul_rms_bwd(dkv, kvs["w_kv"], NN, kvs["x"], P["kv_norm"].reshape(1, D_MODEL), dx,
                                     name="b_dhkv")
            G["kv_norm"] = gkn.reshape(D_MODEL)
            emitted = emit_grads(l, "mix", gw)

    small = {"pool_scale": jnp.stack(G["pool_scale"]),
             "mem_norm": jnp.concatenate(G["mem_norm"], axis=0),
             "norm_gains": jnp.stack([jnp.concatenate(r, axis=0) for r in G["norm_gains"]]),
             "kv_norm": G["kv_norm"]}
    return sq[0, 0], dx.reshape(n_seq, SEQ, D_MODEL), small, emitted


def to_bf16_layers(stacks, *, name):
    L, n = stacks[0].shape[0], len(stacks)

    def body(*refs):
        ins, outs = refs[:n], refs[n:]
        for j in range(L):
            @pl.when(pl.program_id(0) == j)
            def _():
                for k in range(n):
                    outs[j * n + k][...] = ins[k][...].astype(bf16)

    outs = pl.pallas_call(
        body, name=name, grid=(L,),
        in_specs=[pl.BlockSpec((None,) + s.shape[1:], lambda l: (l, 0, 0)) for s in stacks],
        out_specs=[pl.BlockSpec(s.shape[1:], lambda l: (0, 0)) for _ in range(L) for s in stacks],
        out_shape=[_sds(s.shape[1:], bf16) for _ in range(L) for s in stacks],
        compiler_params=_params(("arbitrary",)),
    )(*stacks)
    return [outs[j * n:(j + 1) * n] for j in range(L)]


def _peer(k):
    x, y, c = lax.axis_index("x"), lax.axis_index("y"), lax.axis_index("c")
    px = 1 - x if k & 4 else x
    py = 1 - y if k & 2 else y
    pc = 1 - c if k & 1 else c
    return (px, py, pc), 4 * px + 2 * py + pc


def _my_index():
    return 4 * lax.axis_index("x") + 2 * lax.axis_index("y") + lax.axis_index("c")


def _src_for(kinds, in_refs, i, idx):
    return in_refs[i] if kinds[i] == "gather" else in_refs[i].at[idx]


def _local_copies(kinds, in_refs, out_refs, local_sems):
    me = _my_index()
    return [pltpu.make_async_copy(_src_for(kinds, in_refs, i, me), out_refs[i].at[me], local_sems.at[i])
            for i in range(len(kinds))]


def _remote_copies(kinds, in_refs, out_refs, send_sems, recv_sems, *, arriving):
    me = _my_index()
    copies = []
    for k in range(1, N_DEV):
        dev, idx = _peer(k)
        for i in range(len(kinds)):
            j = i * (N_DEV - 1) + k - 1
            copies.append(pltpu.make_async_remote_copy(
                src_ref=_src_for(kinds, in_refs, i, idx), dst_ref=out_refs[i].at[idx if arriving else me],
                send_sem=send_sems.at[j], recv_sem=recv_sems.at[j], device_id=dev, device_id_type=MESH))
    return copies


def _out_shape(a, kind):
    return ((N_DEV,) + a.shape) if kind == "gather" else a.shape


def exchange(items, *, name, after=()):
    n = len(items)
    kinds = [k for _, k in items]
    after = list(after)

    def body(*refs):
        in_refs, out_refs = refs[:n], refs[n + len(after):2 * n + len(after)]
        send_sems, recv_sems, local_sems = refs[-3:]
        local = _local_copies(kinds, in_refs, out_refs, local_sems)
        sends = _remote_copies(kinds, in_refs, out_refs, send_sems, recv_sems, arriving=False)
        for cp in local + sends:
            cp.start()
        for cp in _remote_copies(kinds, in_refs, out_refs, send_sems, recv_sems, arriving=True):
            cp.wait_recv()
        for cp in sends:
            cp.wait_send()
        for cp in local:
            cp.wait()

    any_spec = pl.BlockSpec(memory_space=pl.ANY)
    return pl.pallas_call(
        body, name=name,
        in_specs=[any_spec] * (n + len(after)), out_specs=[any_spec] * n,
        out_shape=[_sds(_out_shape(a, k), a.dtype) for a, k in items],
        scratch_shapes=[pltpu.SemaphoreType.DMA((n * (N_DEV - 1),)), pltpu.SemaphoreType.DMA((n * (N_DEV - 1),)),
                        pltpu.SemaphoreType.DMA((n,))],
    )(*[a for a, _ in items], *after)


_HBM = pl.BlockSpec(memory_space=pltpu.HBM)
_SEM = pl.BlockSpec(memory_space=pltpu.SEMAPHORE)
_EFFECT = pltpu.SideEffectType.DATAFLOW_SIDE_EFFECTING


def exchange_start(items, after, *, name):
    n = len(items)
    kinds = [k for _, k in items]

    def body(*refs):
        in_refs, land_refs = refs[:n], refs[n:2 * n]
        send_sems, recv_sems, local_sems = refs[2 * n + 1:2 * n + 4]
        token = refs[-1]
        for cp in (_local_copies(kinds, in_refs, land_refs, local_sems)
                   + _remote_copies(kinds, in_refs, land_refs, send_sems, recv_sems, arriving=False)):
            cp.start()
        token[...] = jnp.zeros_like(token)

    srcs = [pltpu.with_memory_space_constraint(a, pltpu.HBM) for a, _ in items]
    lands = [pltpu.with_memory_space_constraint(lax.empty(_out_shape(a, k), a.dtype), pltpu.HBM) for a, k in items]
    outs = pl.pallas_call(
        body, name=name,
        out_shape=(pltpu.SemaphoreType.DMA((n * (N_DEV - 1),)), pltpu.SemaphoreType.DMA((n * (N_DEV - 1),)),
                   pltpu.SemaphoreType.DMA((n,)),
                   *[pltpu.HBM(a.shape, a.dtype) for a in srcs], *[pltpu.HBM(a.shape, a.dtype) for a in lands],
                   _sds((8, 128), f32)),
        in_specs=[_HBM] * (2 * n) + [pl.BlockSpec(memory_space=pl.ANY)],
        out_specs=(_SEM, _SEM, _SEM, *[_HBM] * (2 * n), pl.BlockSpec(memory_space=pltpu.VMEM)),
        input_output_aliases={i: 3 + i for i in range(2 * n)},
        compiler_params=pltpu.CompilerParams(has_side_effects=_EFFECT),
    )(*srcs, *lands, after)
    return {"kinds": kinds, "sems": outs[:3], "srcs": outs[3:3 + n], "lands": outs[3 + n:3 + 2 * n], "token": outs[-1]}


def exchange_wait(handle, after, *, name):
    kinds = handle["kinds"]
    n = len(kinds)

    def body(*refs):
        in_refs, land_refs = refs[:n], refs[n:2 * n]
        send_sems, recv_sems, local_sems = refs[2 * n:2 * n + 3]
        for cp in _remote_copies(kinds, in_refs, land_refs, send_sems, recv_sems, arriving=True):
            cp.wait_recv()
        for cp in _remote_copies(kinds, in_refs, land_refs, send_sems, recv_sems, arriving=False):
            cp.wait_send()
        for cp in _local_copies(kinds, in_refs, land_refs, local_sems):
            cp.wait()

    srcs, lands = list(handle["srcs"]), list(handle["lands"])
    after = list(after) if isinstance(after, (list, tuple)) else [after]
    outs = pl.pallas_call(
        body, name=name,
        out_shape=tuple(pltpu.HBM(a.shape, a.dtype) for a in srcs + lands),
        in_specs=[_HBM] * (2 * n) + [_SEM] * 3 + [pl.BlockSpec(memory_space=pl.ANY)] * len(after),
        out_specs=tuple([_HBM] * (2 * n)),
        input_output_aliases={i: i for i in range(2 * n)},
        compiler_params=pltpu.CompilerParams(has_side_effects=_EFFECT),
    )(*srcs, *lands, *handle["sems"], *after)
    return list(outs[n:])


CHIP_MASKS = (2, 4, 6)


def _g2_first(in_refs, land_refs, send_sems, recv_sems, *, masks, arriving):
    me = _my_index()
    copies = []
    for i in range(len(land_refs)):
        for j, k in enumerate(masks):
            dev, idx = _peer(k)
            dst = land_refs[i].at[idx if arriving else me]
            copies.append(pltpu.make_async_remote_copy(
                src_ref=dst if in_refs is None else in_refs[i], dst_ref=dst,
                send_sem=send_sems.at[i * len(masks) + j], recv_sem=recv_sems.at[i * len(masks) + j],
                device_id=dev, device_id_type=MESH))
    return copies


def _g2_forward(land_refs, fwd_send, fwd_recv, *, arriving):
    sibling, _ = _peer(1)
    copies = []
    for i in range(len(land_refs)):
        for j, k in enumerate(CHIP_MASKS):
            _, idx = _peer(k | 1 if arriving else k)
            copies.append(pltpu.make_async_remote_copy(
                src_ref=land_refs[i].at[idx], dst_ref=land_refs[i].at[idx],
                send_sem=fwd_send.at[i * 3 + j], recv_sem=fwd_recv.at[i * 3 + j], device_id=sibling,
                device_id_type=MESH))
    return copies


def gather2_start(arrays, after, *, name):
    n = len(arrays)

    def body(*refs):
        in_refs, land_refs = refs[:n], refs[n:2 * n]
        ici_send, ici_recv, d2d_send, d2d_recv, local_sems = refs[2 * n + 1:2 * n + 6]
        token = refs[-1]
        ici = _g2_first(in_refs, land_refs, ici_send, ici_recv, masks=CHIP_MASKS, arriving=False)
        d2d = _g2_first(in_refs, land_refs, d2d_send, d2d_recv, masks=(1,), arriving=False)
        for cp in _local_copies(["gather"] * n, in_refs, land_refs, local_sems) + ici + d2d:
            cp.start()
        token[...] = jnp.zeros_like(token)

    srcs = [pltpu.with_memory_space_constraint(a, pltpu.HBM) for a in arrays]
    lands = [pltpu.with_memory_space_constraint(lax.empty((N_DEV,) + a.shape, a.dtype), pltpu.HBM) for a in arrays]
    sem = pltpu.SemaphoreType.DMA
    outs = pl.pallas_call(
        body, name=name,
        out_shape=(sem((3 * n,)), sem((3 * n,)), sem((n,)), sem((n,)), sem((n,)),
                   *[pltpu.HBM(a.shape, a.dtype) for a in srcs], *[pltpu.HBM(a.shape, a.dtype) for a in lands],
                   _sds((8, 128), f32)),
        in_specs=[_HBM] * (2 * n) + [pl.BlockSpec(memory_space=pl.ANY)],
        out_specs=(*[_SEM] * 5, *[_HBM] * (2 * n), pl.BlockSpec(memory_space=pltpu.VMEM)),
        input_output_aliases={i: 5 + i for i in range(2 * n)},
        compiler_params=pltpu.CompilerParams(has_side_effects=_EFFECT),
    )(*srcs, *lands, after)
    return {"n": n, "sems": outs[:5], "srcs": outs[5:5 + n], "lands": outs[5 + n:5 + 2 * n], "token": outs[-1]}


def gather2_forward(handle, after, *, name):
    n = handle["n"]

    def body(*refs):
        land_refs = refs[:n]
        ici_recv = refs[n]
        fwd_send, fwd_recv = refs[n + 2:n + 4]
        for cp in _g2_first(None, land_refs, fwd_send, ici_recv, masks=CHIP_MASKS, arriving=True):
            cp.wait_recv()
        for cp in _g2_forward(land_refs, fwd_send, fwd_recv, arriving=False):
            cp.start()

    lands = list(handle["lands"])
    sem = pltpu.SemaphoreType.DMA
    outs = pl.pallas_call(
        body, name=name,
        out_shape=(sem((3 * n,)), sem((3 * n,)), *[pltpu.HBM(a.shape, a.dtype) for a in lands]),
        in_specs=[_HBM] * n + [_SEM, pl.BlockSpec(memory_space=pl.ANY)],
        out_specs=(_SEM, _SEM, *[_HBM] * n),
        input_output_aliases={i: 2 + i for i in range(n)},
        compiler_params=pltpu.CompilerParams(has_side_effects=_EFFECT),
    )(*lands, handle["sems"][1], after)
    return dict(handle, fwd=outs[:2], lands=outs[2:])


def gather2_wait(handle, after, *, name):
    n = handle["n"]

    def body(*refs):
        in_refs, land_refs = refs[:n], refs[n:2 * n]
        ici_send, d2d_send, d2d_recv, local_sems, fwd_send, fwd_recv = refs[2 * n:2 * n + 6]
        for cp in _g2_first(in_refs, land_refs, d2d_send, d2d_recv, masks=(1,), arriving=True):
            cp.wait_recv()
        for cp in _g2_forward(land_refs, fwd_send, fwd_recv, arriving=True):
            cp.wait_recv()
        for cp in (_g2_first(in_refs, land_refs, ici_send, fwd_recv, masks=CHIP_MASKS, arriving=False)
                   + _g2_first(in_refs, land_refs, d2d_send, d2d_recv, masks=(1,), arriving=False)
                   + _g2_forward(land_refs, fwd_send, fwd_recv, arriving=False)):
            cp.wait_send()
        for cp in _local_copies(["gather"] * n, in_refs, land_refs, local_sems):
            cp.wait()

    srcs, lands = list(handle["srcs"]), list(handle["lands"])
    s = handle["sems"]
    outs = pl.pallas_call(
        body, name=name,
        out_shape=tuple(pltpu.HBM(a.shape, a.dtype) for a in srcs + lands),
        in_specs=[_HBM] * (2 * n) + [_SEM] * 6 + [pl.BlockSpec(memory_space=pl.ANY)],
        out_specs=tuple([_HBM] * (2 * n)),
        input_output_aliases={i: i for i in range(2 * n)},
        compiler_params=pltpu.CompilerParams(has_side_effects=_EFFECT),
    )(*srcs, *lands, s[0], s[2], s[3], s[4], *handle["fwd"], after)
    return list(outs[n:])


def adamw(entries, *, name):
    c1 = 1.0 - ADAM_B1 ** ADAM_STEP
    c2 = 1.0 - ADAM_B2 ** ADAM_STEP
    tiles = [_tile(w.shape[-2], (64, 32, 16, 8)) for _, w, _, _, _, _ in entries]
    steps = [w.shape[-2] // tr for (_, w, _, _, _, _), tr in zip(entries, tiles)]
    n = len(entries)

    def body(*refs):
        i = pl.program_id(0)
        for e in range(n):
            s_ref, w_ref, m_ref, v_ref = refs[4 * e:4 * e + 4]
            g_ref, d_ref, m2_ref, v2_ref = refs[len(refs) - 4 * n + 4 * e:len(refs) - 4 * n + 4 * e + 4]

            @pl.when(i < steps[e])
            def _():
                g = s_ref[0].astype(f32)
                for d in range(1, N_DEV):
                    g = g + s_ref[d].astype(f32)
                m2 = ADAM_B1 * m_ref[...] + (1.0 - ADAM_B1) * g
                v2 = ADAM_B2 * v_ref[...] + (1.0 - ADAM_B2) * (g * g)
                g_ref[...] = g
                m2_ref[...] = m2
                v2_ref[...] = v2
                d_ref[...] = -ADAM_LR * ((m2 / c1) / (jnp.sqrt(v2 / c2) + ADAM_EPS) + ADAM_WD * w_ref[...])

    in_specs, out_specs, out_shape, args, extras, aliases = [], [], [], [], [], {}
    for e, ((slots, w, m, v, layer, into), tr, ns) in enumerate(zip(entries, tiles, steps)):
        C = w.shape[-1]
        row = lambda i, ns=ns: jnp.minimum(i, ns - 1)
        if layer is None:
            blk = pl.BlockSpec((tr, C), lambda i, row=row: (row(i), 0))
        else:
            blk = pl.BlockSpec((None, tr, C), lambda i, row=row, layer=layer: (layer, row(i), 0))
        in_specs += [pl.BlockSpec((N_DEV, tr, C), lambda i, row=row: (0, row(i), 0)), blk, blk, blk]
        args += [slots, w, m, v]
        out_specs += [blk] * 4
        out_shape += [_sds(w.shape, f32)] * 4
        if into is not None:
            for t, a in enumerate(into):
                aliases[4 * n + len(extras)] = 4 * e + t
                extras.append(a)
    outs = pl.pallas_call(
        body, name=name, grid=(max(steps),),
        in_specs=in_specs + [pl.BlockSpec(memory_space=pl.ANY)] * len(extras),
        out_specs=out_specs, out_shape=out_shape, input_output_aliases=aliases,
        compiler_params=_params(("arbitrary",)),
    )(*args, *extras)
    return [outs[4 * e:4 * e + 4] for e in range(n)]


WEIGHTS = ("norm_gains", "mem_norm", "w_in", "w_mem_kv", "w_out", "w_pool", "pool_scale", "kv_norm", "w_kv",
           "w_gate_up", "w_down")
LAYER_MATS = ("w_in", "w_mem_kv", "w_out", "w_gate_up", "w_down")
POOL_SHARD = MAIN_W // N_DEV
KV_SHARD = 2 * MAIN_W // N_DEV
LOOKAHEAD = 2
TWO_LEVEL_LAYERS = (0, 1)


def _pack_small(gains, pscale):
    lead = gains.shape[:-3]
    g = gains.reshape(lead + (16, 128))
    p = jnp.zeros(lead + (8, 128), f32).at[..., :2, :POOL_SHARD].set(pscale)
    return jnp.concatenate([g, p], axis=-2)


def _unpack_small(a):
    return a[:16].reshape(4, 4, 128), a[16:18, :POOL_SHARD]


def _pack_repl(mem_norm, kv_norm):
    return jnp.concatenate([mem_norm, kv_norm.reshape(1, D_MODEL), jnp.zeros((3, D_MODEL), f32)], axis=0)


def _unpack_repl(a):
    return a[:4], a[4]


def kernel(x, mem, positions, norm_gains, mem_norm, w_in, w_mem_kv, w_out, w_pool, pool_scale, kv_norm, w_kv, w_gate_up, w_down, loss_target, m_norm_gains, m_mem_norm, m_w_in, m_w_mem_kv, m_w_out, m_w_pool, m_pool_scale, m_kv_norm, m_w_kv, m_w_gate_up, m_w_down, v_norm_gains, v_mem_norm, v_w_in, v_w_mem_kv, v_w_out, v_w_pool, v_pool_scale, v_kv_norm, v_w_kv, v_w_gate_up, v_w_down):
    w = dict(norm_gains=norm_gains, mem_norm=mem_norm, w_in=w_in, w_mem_kv=w_mem_kv, w_out=w_out, w_pool=w_pool,
             pool_scale=pool_scale, kv_norm=kv_norm, w_kv=w_kv, w_gate_up=w_gate_up, w_down=w_down)
    m = dict(norm_gains=m_norm_gains, mem_norm=m_mem_norm, w_in=m_w_in, w_mem_kv=m_w_mem_kv, w_out=m_w_out,
             w_pool=m_w_pool, pool_scale=m_pool_scale, kv_norm=m_kv_norm, w_kv=m_w_kv, w_gate_up=m_w_gate_up,
             w_down=m_w_down)
    v = dict(norm_gains=v_norm_gains, mem_norm=v_mem_norm, w_in=v_w_in, w_mem_kv=v_w_mem_kv, w_out=v_w_out,
             w_pool=v_w_pool, pool_scale=v_pool_scale, kv_norm=v_kv_norm, w_kv=v_w_kv, w_gate_up=v_w_gate_up,
             w_down=v_w_down)

    def transposed_view(d):
        d = dict(d)
        d["w_gate_up"] = jnp.swapaxes(d["w_gate_up"], 1, 2)
        d["w_kv"] = jnp.swapaxes(d["w_kv"], 0, 1)
        return d

    wv, mv, vv = transposed_view(w), transposed_view(m), transposed_view(v)

    small = _pack_small(norm_gains, pool_scale)
    (gsmall,) = exchange([(small, "gather")], name="gather_small")
    P = {"norm_gains": jnp.moveaxis(gsmall[:, :16].reshape(N_DEV, 4, 4, 128), 0, 2).reshape(4, 4, D_MODEL),
         "pool_scale": jnp.moveaxis(gsmall[:, 16:18, :POOL_SHARD], 0, 1).reshape(2, MAIN_W),
         "mem_norm": mem_norm, "kv_norm": kv_norm, "w_pool": w_pool}

    PARTS = {"mix": ("w_in", "w_mem_kv", "w_out"), "ffn": ("w_gate_up", "w_down"), "gu": ("w_gate_up",),
             "down": ("w_down",), "all": ("w_in", "w_mem_kv", "w_out", "w_gate_up", "w_down")}

    def parts_of(l):
        return (("mix", "gu", "down"), ("mix", "ffn"))[l] if l < 2 else ("all",)

    wb = [dict(zip(LAYER_MATS, mats)) for mats in to_bf16_layers([wv[k] for k in LAYER_MATS], name="weights_bf16")]

    def part_items(l, part):
        items = [(wb[l][k], "gather") for k in PARTS[part]]
        if part == "ffn" and l == N_A_LAYERS - 1:
            items.append((wv["w_kv"].astype(bf16), "gather"))
        return items

    handles = {}

    def start_layer(l, after):
        for part in parts_of(l):
            if l in TWO_LEVEL_LAYERS:
                handles[l, part] = gather2_start([a for a, _ in part_items(l, part)], after,
                                                 name=f"gather_start_{part}_l{l}")
            else:
                handles[l, part] = exchange_start(part_items(l, part), after, name=f"gather_start_{part}_l{l}")
            after = handles[l, part]["token"]
        return after

    token = gsmall
    for l in range(LOOKAHEAD):
        token = start_layer(l, token)
    landed = {}

    def layer_weights(l, part, after):
        if part not in parts_of(l):
            if part == "down" or (part == "gu" and "all" in parts_of(l)):
                return {}, None
            part = "all" if "all" in parts_of(l) else "ffn"
        if l == 0 and part == "mix":
            after = token
        if l in TWO_LEVEL_LAYERS:
            passed = gather2_forward(handles[l, part], after, name=f"gather_forward_{part}_l{l}")
            got = gather2_wait(passed, after, name=f"gather_wait_{part}_l{l}")
        else:
            got = exchange_wait(handles[l, part], after, name=f"gather_wait_{part}_l{l}")
        landed[l, part] = got
        started = None
        if part in ("mix", "all") and l + LOOKAHEAD < DEPTH:
            started = start_layer(l + LOOKAHEAD, got[0])
        W = {k: g.reshape(-1, g.shape[-1]) for k, g in zip(PARTS[part], got)}
        return W, started

    def kv_weight(after):
        g = landed[N_A_LAYERS - 1, "ffn"][len(PARTS["ffn"])]
        return g.reshape(2 * MAIN_W, D_MODEL)

    ghandles = {}

    pending = {}

    def gparts_of(l):
        return ("ffn", "mix") if l < 2 else ("all",)

    def emit_grads(l, part, gw):
        if part not in gparts_of(l):
            pending.setdefault(l, {}).update(gw)
            if part == "ffn":
                return None
            gw, part = pending[l], "all"
        items = [(gw[k].reshape((N_DEV, -1) + gw[k].shape[-1:]), "scatter") for k in PARTS[part]]
        if part != "ffn" and l == N_A_LAYERS:
            items.append((gw["w_kv"].reshape(N_DEV, KV_SHARD, D_MODEL), "scatter"))
        if part != "ffn" and l < N_A_LAYERS:
            items.append((gw["w_pool"], "gather"))
        ghandles[l, part] = exchange_start(items, gsmall, name=f"scatter_start_{part}_l{l}")
        return ghandles[l, part]["token"]

    sq, grad_x, GS, emitted = local_step(x, mem, positions, loss_target, P, layer_weights, kv_weight, emit_grads)

    def pool3(a):
        return a.reshape(N_A_LAYERS, MAIN_W, POOL_GROUP)

    out = {}
    after = [emitted]

    def finish_layer(l, after):
        for part in gparts_of(l):
            got = exchange_wait(ghandles[l, part], after, name=f"scatter_wait_{part}_l{l}")
            names = list(PARTS[part])
            entries = [(slots, wv[k], mv[k], vv[k], l, out.get(k)) for k, slots in zip(names, got)]
            if part != "ffn" and l == N_A_LAYERS:
                names.append("w_kv")
                entries.append((got[-1], wv["w_kv"], mv["w_kv"], vv["w_kv"], None, None))
            if part != "ffn" and l < N_A_LAYERS:
                names.append("w_pool")
                entries.append((got[-1], pool3(w_pool), pool3(m_w_pool), pool3(v_w_pool), l, out.get("w_pool")))
            out.update(zip(names, adamw(entries, name=f"adamw_{part}_l{l}")))
            after = [out[k][0] for k in names]
        return after

    for l in reversed(range(1, DEPTH)):
        after = finish_layer(l, after)

    gs = _pack_small(jnp.moveaxis(GS["norm_gains"].reshape(4, 4, N_DEV, 128), 2, 0),
                     jnp.moveaxis(GS["pool_scale"].reshape(2, N_DEV, POOL_SHARD), 1, 0))
    parts_small, parts_repl, parts_sq = exchange(
        [(gs, "scatter"), (_pack_repl(GS["mem_norm"], GS["kv_norm"]), "gather"),
         (jnp.full((8, 128), sq, f32), "gather")],
        name="exchange_small_grads", after=after)
    loss = (0.5 / D_MODEL) * jnp.sum(parts_sq[:, 0, 0])
    finish_layer(0, [parts_small])
    out["w_gate_up"] = [jnp.swapaxes(r, 1, 2) for r in out["w_gate_up"]]
    out["w_kv"] = [jnp.swapaxes(r, 0, 1) for r in out["w_kv"]]
    out["w_pool"] = [r.reshape(w_pool.shape) for r in out["w_pool"]]

    res_small, res_repl = adamw(
        [(parts_small, small, _pack_small(m_norm_gains, m_pool_scale), _pack_small(v_norm_gains, v_pool_scale),
          None, None),
         (parts_repl, _pack_repl(mem_norm, kv_norm), _pack_repl(m_mem_norm, m_kv_norm),
          _pack_repl(v_mem_norm, v_kv_norm), None, None)], name="adamw_small")
    out["norm_gains"], out["pool_scale"] = zip(*[_unpack_small(r) for r in res_small])
    out["mem_norm"], out["kv_norm"] = zip(*[_unpack_repl(r) for r in res_repl])

    return (loss, grad_x, *[out[k][0] for k in WEIGHTS], *[out[k][1] for k in WEIGHTS],
            *[out[k][2] for k in WEIGHTS], *[out[k][3] for k in WEIGHTS])
```

```python
import numpy as np
import jax
import jax.numpy as jnp
from jax import lax
from jax.experimental import pallas as pl
from jax.experimental.pallas import tpu as pltpu

f32 = jnp.float32
bf16 = jnp.bfloat16

D_MODEL = 1024
SEQ = 2048
DEPTH = 4
N_MEM = 256
HEAD_DIM = 64
N_MEM_HEADS = 4
MEM_W = 256
MAIN_W = 768
POOL_WINDOWS = (2, 4, 8, 16)
POOL_GROUP = 192
POOL_HALO = 16
DIL_PATTERNS = ((128, 1), (512, 4), (2048, 16))
N_GROUPS = 3
GROUP_W = 256
BAND = 128
N_A_LAYERS = 2
D_FF = 2816
ROPE_THETA = 10000.0
EPS = 1e-6
NEG = -1e30
SCALE = HEAD_DIM ** -0.5
N_DEV = 8

ADAM_LR = 0.001
ADAM_B1 = 0.9
ADAM_B2 = 0.999
ADAM_EPS = 1e-08
ADAM_WD = 0.01
ADAM_STEP = 10

VMEM_LIMIT_BYTES = 56 * 1024 * 1024
MESH = pl.DeviceIdType.MESH

NN = (((1,), (0,)), ((), ()))
NT = (((1,), (1,)), ((), ()))
TN = (((0,), (0,)), ((), ()))


def _params(sem=None):
    return pltpu.CompilerParams(dimension_semantics=sem, vmem_limit_bytes=VMEM_LIMIT_BYTES)


def _tile(n, cands):
    for c in cands:
        if n % c == 0:
            return c
    return n


def _sds(shape, dtype):
    return jax.ShapeDtypeStruct(tuple(shape), dtype)


def _rms_r(v):
    return lax.rsqrt(jnp.mean(v * v, axis=-1, keepdims=True) + EPS)


ROW_GROUPS = 2


def rms_matmul(x, gain, w, *, name, out_dtype, transposed=False, after=None, rope=None):
    M, K = x.shape
    N = w.shape[0] if transposed else w.shape[1]
    tm = min(512, M)
    order = [] if after is None else [after]
    tables = [] if rope is None else list(rope)
    rot_spec = [] if rope is None else [pl.BlockSpec((tm, MAIN_W), lambda i: (i, 0))]
    rot_shape = [] if rope is None else [_sds((M, MAIN_W), f32)]

    def body(x_ref, g_ref, w_ref, *refs):
        z_ref, h_ref = refs[len(tables) + len(order):][:2]
        groups = [slice(g * tm // ROW_GROUPS, (g + 1) * tm // ROW_GROUPS) for g in range(ROW_GROUPS)]
        hs = []
        for rows in groups:
            xv = x_ref[rows, :]
            hs.append((xv * _rms_r(xv) * g_ref[...]).astype(bf16))
            h_ref[rows, :] = hs[-1]
        zs = [lax.dot_general(h, w_ref[...], NT if transposed else NN, preferred_element_type=f32) for h in hs]
        for rows, z in zip(groups, zs):
            z_ref[rows, :] = z.astype(z_ref.dtype)
            if tables:
                c = jnp.tile(refs[0][rows, :], (1, MAIN_W // 128))
                s = jnp.tile(refs[1][rows, :], (1, MAIN_W // 128))
                zr = z[:, :MAIN_W]
                refs[-1][rows, :] = zr * c + _swap_halves(zr) * s

    tab = pl.BlockSpec((tm, 128), lambda i: (i, 0))
    return pl.pallas_call(
        body, name=name, grid=(M // tm,),
        in_specs=[pl.BlockSpec((tm, K), lambda i: (i, 0)),
                  pl.BlockSpec((1, K), lambda i: (0, 0)),
                  pl.BlockSpec(w.shape, lambda i: (0, 0))] + [tab] * len(tables)
        + [pl.BlockSpec(memory_space=pl.ANY)] * len(order),
        out_specs=[pl.BlockSpec((tm, N), lambda i: (i, 0)), pl.BlockSpec((tm, K), lambda i: (i, 0))] + rot_spec,
        out_shape=[_sds((M, N), out_dtype), _sds((M, K), bf16)] + rot_shape,
        compiler_params=_params(("parallel",)),
    )(x, gain, w, *tables, *order)


def matmul_rms_res(a, w, gain, res, *, name, target=None):
    M, K = a.shape
    N = w.shape[1]
    tm = min(512, M)
    goal = [] if target is None else [target]

    def body(a_ref, w_ref, g_ref, r_ref, *refs):
        y_ref, x_ref = refs[len(goal):][:2]
        groups = [slice(g * tm // ROW_GROUPS, (g + 1) * tm // ROW_GROUPS) for g in range(ROW_GROUPS)]
        ys = [jnp.dot(a_ref[rows, :], w_ref[...], preferred_element_type=f32) for rows in groups]
        sq = 0.0
        for rows, y in zip(groups, ys):
            y_ref[rows, :] = y.astype(bf16)
            x = r_ref[rows, :] + y * _rms_r(y) * g_ref[...]
            if goal:
                e = x - refs[0][rows, :]
                x = e * (1.0 / N)
                sq = sq + jnp.sum(jnp.sum(e * e, axis=0, keepdims=True), axis=1, keepdims=True)
            x_ref[rows, :] = x
        if goal:
            _accumulate(refs[-1], sq)

    row = pl.BlockSpec((tm, N), lambda i: (i, 0))
    return pl.pallas_call(
        body, name=name, grid=(M // tm,),
        in_specs=[pl.BlockSpec((tm, K), lambda i: (i, 0)),
                  pl.BlockSpec((K, N), lambda i: (0, 0)),
                  pl.BlockSpec((1, N), lambda i: (0, 0)),
                  row] + [row] * len(goal),
        out_specs=[row, row] + [pl.BlockSpec((8, 128), lambda i: (0, 0))] * len(goal),
        out_shape=[_sds((M, N), bf16), _sds((M, N), f32)] + [_sds((8, 128), f32)] * len(goal),
        compiler_params=_params(("arbitrary",) if goal else ("parallel",)),
    )(a, w, gain, res, *goal)


def matmul(a, b, dims, *, name, out_dtype):
    if dims is TN:
        K, M = a.shape
        tm = _tile(M, (512, 256, 128))
        a_spec = pl.BlockSpec((K, tm), lambda i: (0, i))
    else:
        M, K = a.shape
        tm = _tile(M, (1024, 512, 256, 128))
        a_spec = pl.BlockSpec((tm, K), lambda i: (i, 0))
    N = b.shape[0] if dims is NT else b.shape[1]

    def body(a_ref, b_ref, o_ref):
        o_ref[...] = lax.dot_general(a_ref[...].astype(bf16), b_ref[...].astype(bf16), dims,
                                     preferred_element_type=f32).astype(o_ref.dtype)

    return pl.pallas_call(
        body, name=name, grid=(M // tm,),
        in_specs=[a_spec, pl.BlockSpec(b.shape, lambda i: (0, 0))],
        out_specs=pl.BlockSpec((tm, N), lambda i: (i, 0)),
        out_shape=_sds((M, N), out_dtype),
        compiler_params=_params(("parallel",)),
    )(a, b)


def rms_gate_up(x, gain, wt, *, name):
    M, K = x.shape
    tm = min(2048, M)
    tn = _tile(D_FF, (256, 128))
    nj = D_FF // tn

    def body(x_ref, gn_ref, wg_ref, wu_ref, g_ref, u_ref, a_ref, h_ref):
        @pl.when(pl.program_id(1) == 0)
        def _():
            xv = x_ref[...]
            h_ref[...] = (xv * _rms_r(xv) * gn_ref[...]).astype(bf16)

        h = h_ref[...]
        g = lax.dot_general(h, wg_ref[...], NT, preferred_element_type=f32).astype(bf16)
        u = lax.dot_general(h, wu_ref[...], NT, preferred_element_type=f32).astype(bf16)
        g_ref[...] = g
        u_ref[...] = u
        a_ref[...] = g * (1.0 / (1.0 + jnp.exp(-g))) * u

    col = pl.BlockSpec((tm, tn), lambda i, j: (i, j))
    return pl.pallas_call(
        body, name=name, grid=(M // tm, nj),
        in_specs=[pl.BlockSpec((tm, K), lambda i, j: (i, 0)),
                  pl.BlockSpec((1, K), lambda i, j: (0, 0)),
                  pl.BlockSpec((tn, K), lambda i, j: (j, 0)),
                  pl.BlockSpec((tn, K), lambda i, j: (j + nj, 0))],
        out_specs=[col, col, col, pl.BlockSpec((tm, K), lambda i, j: (i, 0))],
        out_shape=[_sds((M, D_FF), bf16)] * 3 + [_sds((M, K), bf16)],
        compiler_params=_params(("parallel", "arbitrary")),
    )(x, gain, wt, wt)


def _rms_bwd_math(yv, gain, dn):
    r = _rms_r(yv)
    q = dn * gain
    dy = r * q - yv * (r * r * r) * jnp.mean(q * yv, axis=-1, keepdims=True)
    return dy, jnp.sum(dn * yv * r, axis=0, keepdims=True)


def _accumulate(ref, val):
    @pl.when(pl.program_id(0) == 0)
    def _():
        ref[...] = jnp.zeros_like(ref)

    ref[...] += val


def down_bwd(y, gain, dn, w_down, g, u, *, name, after=None):
    M, K = y.shape
    tm = min(512, M)
    order = [] if after is None else [after]

    def body(y_ref, gn_ref, dn_ref, w_ref, g_ref, u_ref, *refs):
        dy_ref, o_ref, dg_ref = refs[len(order):]
        groups = [slice(a * tm // ROW_GROUPS, (a + 1) * tm // ROW_GROUPS) for a in range(ROW_GROUPS)]
        dys, dgain = [], 0.0
        for rows in groups:
            dy, part = _rms_bwd_math(y_ref[rows, :].astype(f32), gn_ref[...], dn_ref[rows, :])
            dys.append(dy.astype(bf16))
            dy_ref[rows, :] = dys[-1]
            dgain = dgain + part
        _accumulate(dg_ref, dgain)
        das = [lax.dot_general(dy, w_ref[...], NT, preferred_element_type=f32).astype(bf16) for dy in dys]
        for rows, da in zip(groups, das):
            g = g_ref[rows, :]
            s = 1.0 / (1.0 + jnp.exp(-g))
            o_ref[rows, :D_FF] = da * u_ref[rows, :] * s * (1.0 + g * (1.0 - s))
            o_ref[rows, D_FF:] = da * g * s

    row = pl.BlockSpec((tm, K), lambda i: (i, 0))
    vec = pl.BlockSpec((1, K), lambda i: (0, 0))
    wide = pl.BlockSpec((tm, D_FF), lambda i: (i, 0))
    return pl.pallas_call(
        body, name=name, grid=(M // tm,),
        in_specs=[row, vec, row, pl.BlockSpec((D_FF, K), lambda i: (0, 0)), wide, wide]
        + [pl.BlockSpec(memory_space=pl.ANY)] * len(order),
        out_specs=[row, pl.BlockSpec((tm, 2 * D_FF), lambda i: (i, 0)), vec],
        out_shape=[_sds((M, K), bf16), _sds((M, 2 * D_FF), bf16), _sds((1, K), f32)],
        compiler_params=_params(("arbitrary",)),
    )(y, gain, dn, w_down, g, u, *order)


def rms_bwd_matmul(y, gain, dn, w, dims, *, name, after=None):
    M, K = y.shape
    N = w.shape[0] if dims is NT else w.shape[1]
    tm = min(1024, M)
    order = [] if after is None else [after]

    def body(y_ref, gn_ref, dn_ref, w_ref, *refs):
        dy_ref, o_ref, dg_ref = refs[len(order):]
        groups = [slice(g * tm // ROW_GROUPS, (g + 1) * tm // ROW_GROUPS) for g in range(ROW_GROUPS)]
        dys, dgain = [], 0.0
        for rows in groups:
            dy, part = _rms_bwd_math(y_ref[rows, :].astype(f32), gn_ref[...], dn_ref[rows, :].astype(f32))
            dys.append(dy.astype(bf16))
            dy_ref[rows, :] = dys[-1]
            dgain = dgain + part
        _accumulate(dg_ref, dgain)
        for rows, dy in zip(groups, dys):
            o_ref[rows, :] = lax.dot_general(dy, w_ref[...], dims, preferred_element_type=f32).astype(bf16)

    row = pl.BlockSpec((tm, K), lambda i: (i, 0))
    vec = pl.BlockSpec((1, K), lambda i: (0, 0))
    return pl.pallas_call(
        body, name=name, grid=(M // tm,),
        in_specs=[row, vec, row, pl.BlockSpec(w.shape, lambda i: (0, 0))]
        + [pl.BlockSpec(memory_space=pl.ANY)] * len(order),
        out_specs=[row, pl.BlockSpec((tm, N), lambda i: (i, 0)), vec],
        out_shape=[_sds((M, K), bf16), _sds((M, N), bf16), _sds((1, K), f32)],
        compiler_params=_params(("arbitrary",)),
    )(y, gain, dn, w, *order)


def matmul_rms_bwd(a, b, dims, y, gain, res, *, name, after=None):
    M, K = a.shape
    N = y.shape[1]
    tm = min(512, M)
    order = [] if after is None else [after]

    def body(a_ref, b_ref, y_ref, gn_ref, r_ref, *refs):
        dx_ref, dg_ref = refs[len(order):]
        groups = [slice(g * tm // ROW_GROUPS, (g + 1) * tm // ROW_GROUPS) for g in range(ROW_GROUPS)]
        dns = [lax.dot_general(a_ref[rows, :].astype(bf16), b_ref[...], dims, preferred_element_type=f32)
               for rows in groups]
        dgain = 0.0
        for rows, dn in zip(groups, dns):
            dy, part = _rms_bwd_math(y_ref[rows, :], gn_ref[...], dn)
            dx_ref[rows, :] = dy + r_ref[rows, :]
            dgain = dgain + part
        _accumulate(dg_ref, dgain)

    row = pl.BlockSpec((tm, N), lambda i: (i, 0))
    vec = pl.BlockSpec((1, N), lambda i: (0, 0))
    return pl.pallas_call(
        body, name=name, grid=(M // tm,),
        in_specs=[pl.BlockSpec((tm, K), lambda i: (i, 0)), pl.BlockSpec(b.shape, lambda i: (0, 0)), row, vec, row]
        + [pl.BlockSpec(memory_space=pl.ANY)] * len(order),
        out_specs=[row, vec],
        out_shape=[_sds((M, N), f32), _sds((1, N), f32)],
        compiler_params=_params(("arbitrary",)),
    )(a, b, y, gain, res, *order)


def rms_bwd(y, gain, dn, res, *, name, out_dtype, after=None):
    M, N = y.shape
    tm = min(512, M)
    has_res = res is not None
    order = [] if after is None else [after]

    def body(*refs):
        y_ref, g_ref, dn_ref = refs[:3]
        r_ref = refs[3] if has_res else None
        dy_ref, dg_ref = refs[-2:]
        dy, dgain = _rms_bwd_math(y_ref[...].astype(f32), g_ref[...], dn_ref[...].astype(f32))
        if has_res:
            dy = dy + r_ref[...]
        dy_ref[...] = dy.astype(dy_ref.dtype)
        _accumulate(dg_ref, dgain)

    row = pl.BlockSpec((tm, N), lambda i: (i, 0))
    vec = pl.BlockSpec((1, N), lambda i: (0, 0))
    args = [y, gain, dn] + ([res] if has_res else []) + order
    return pl.pallas_call(
        body, name=name, grid=(M // tm,),
        in_specs=[row, vec, row] + ([row] if has_res else []) + [pl.BlockSpec(memory_space=pl.ANY)] * len(order),
        out_specs=[row, vec],
        out_shape=[_sds((M, N), out_dtype), _sds((1, N), f32)],
        compiler_params=_params(("arbitrary",)),
    )(*args)


def _pool_select(a1, a2, a3, a4):
    col = lax.broadcasted_iota(jnp.int32, (1, MAIN_W), 1) // POOL_GROUP
    return jnp.where(col == 0, a1, jnp.where(col == 1, a2, jnp.where(col == 2, a3, a4)))


def _pool_count(t):
    col = lax.broadcasted_iota(jnp.int32, (1, MAIN_W), 1) // POOL_GROUP
    win = jnp.where(col == 0, 2, jnp.where(col == 1, 4, jnp.where(col == 2, 8, 16)))
    return jnp.minimum(t + 1, win).astype(f32)


def pool_fwd(z, wbd, scale, *, name):
    M = z.shape[0]
    tm = 512
    nper = SEQ // tm
    hb = tm // POOL_HALO

    def body(zc_ref, zh_ref, w_ref, s_ref, p_ref, y_ref):
        i = pl.program_id(0)
        seq_blk = i % nper
        halo = jnp.where(seq_blk == 0, 0.0, zh_ref[...].astype(f32))
        u = zc_ref[...].astype(f32)
        ext = jnp.concatenate([halo, u], axis=0)
        a1 = ext + pltpu.roll(ext, 1, 0)
        a2 = a1 + pltpu.roll(a1, 2, 0)
        a3 = a2 + pltpu.roll(a2, 4, 0)
        a4 = a3 + pltpu.roll(a3, 8, 0)
        sums = _pool_select(a1, a2, a3, a4)[POOL_HALO:]
        t = seq_blk * tm + lax.broadcasted_iota(jnp.int32, (tm, 1), 0)
        p = (sums / _pool_count(t) - u).astype(bf16)
        p_ref[...] = p
        y_ref[...] = (jnp.dot(p, w_ref[...], preferred_element_type=f32) * s_ref[...]).astype(bf16)

    return pl.pallas_call(
        body, name=name, grid=(M // tm,),
        in_specs=[pl.BlockSpec((tm, MAIN_W), lambda i: (i, 0)),
                  pl.BlockSpec((POOL_HALO, MAIN_W), lambda i: (jnp.maximum(i * hb - 1, 0), 0)),
                  pl.BlockSpec((MAIN_W, MAIN_W), lambda i: (0, 0)),
                  pl.BlockSpec((1, MAIN_W), lambda i: (0, 0))],
        out_specs=[pl.BlockSpec((tm, MAIN_W), lambda i: (i, 0)),
                   pl.BlockSpec((tm, MAIN_W), lambda i: (i, 0))],
        out_shape=[_sds((M, MAIN_W), bf16), _sds((M, D_MODEL), bf16)],
        compiler_params=_params(("parallel",)),
    )(z, z, wbd, scale)


def pool_bwd(dyc, p, wbd, scale, *, name):
    M = p.shape[0]
    tm = 512
    nper = SEQ // tm
    hb = tm // POOL_HALO
    last_hb = M // POOL_HALO - 1

    def body(dy_ref, dyh_ref, p_ref, w_ref, s_ref, dz_ref, dw_ref, ds_ref):
        i = pl.program_id(0)
        seq_blk = i % nper
        dy = dy_ref[...].astype(f32)
        pv = p_ref[...]
        w = w_ref[...]
        sc = s_ref[...]

        @pl.when(i == 0)
        def _():
            dw_ref[...] = jnp.zeros_like(dw_ref)
            ds_ref[...] = jnp.zeros_like(ds_ref)

        v = jnp.dot(pv, w, preferred_element_type=f32)
        ds_ref[...] += jnp.sum(dy * v, axis=0, keepdims=True)
        dv = (dy * sc).astype(bf16)
        dw_ref[...] += lax.dot_general(pv, dv, TN, preferred_element_type=f32)
        dp = lax.dot_general(dv, w, NT, preferred_element_type=f32)
        dvh = jnp.where(seq_blk == nper - 1, 0.0, dyh_ref[...].astype(f32) * sc).astype(bf16)
        dph = lax.dot_general(dvh, w, NT, preferred_element_type=f32)
        ext = jnp.concatenate([dp, dph], axis=0)
        n = tm + POOL_HALO
        t = seq_blk * tm + lax.broadcasted_iota(jnp.int32, (n, 1), 0)
        e = ext / _pool_count(t)
        b1 = e + pltpu.roll(e, n - 1, 0)
        b2 = b1 + pltpu.roll(b1, n - 2, 0)
        b3 = b2 + pltpu.roll(b2, n - 4, 0)
        b4 = b3 + pltpu.roll(b3, n - 8, 0)
        dz_ref[...] = (_pool_select(b1, b2, b3, b4)[:tm] - dp).astype(dz_ref.dtype)

    return pl.pallas_call(
        body, name=name, grid=(M // tm,),
        in_specs=[pl.BlockSpec((tm, MAIN_W), lambda i: (i, 0)),
                  pl.BlockSpec((POOL_HALO, MAIN_W), lambda i: (jnp.minimum((i + 1) * hb, last_hb), 0)),
                  pl.BlockSpec((tm, MAIN_W), lambda i: (i, 0)),
                  pl.BlockSpec((MAIN_W, MAIN_W), lambda i: (0, 0)),
                  pl.BlockSpec((1, MAIN_W), lambda i: (0, 0))],
        out_specs=[pl.BlockSpec((tm, MAIN_W), lambda i: (i, 0)),
                   pl.BlockSpec((MAIN_W, MAIN_W), lambda i: (0, 0)),
                   pl.BlockSpec((1, MAIN_W), lambda i: (0, 0))],
        out_shape=[_sds((M, D_MODEL), bf16), _sds((MAIN_W, MAIN_W), f32), _sds((1, MAIN_W), f32)],
        compiler_params=_params(("arbitrary",)),
    )(dyc, dyc, p, wbd, scale)


def _mem_heads(q, kv):
    first = _first_head()
    for pr in range(N_MEM_HEADS // 2):
        cols = slice(pr * PAIR_W, (pr + 1) * PAIR_W)
        qp = q[:, cols] * SCALE
        kp = kv[:, cols]
        vp = kv[:, MEM_W + pr * PAIR_W: MEM_W + (pr + 1) * PAIR_W]
        for hh in range(2):
            lm = first if hh == 0 else ~first
            qm = jnp.where(lm, qp, 0.0).astype(bf16)
            s = lax.dot_general(qm, kp, NT, preferred_element_type=f32)
            e = jnp.exp(s - jnp.max(s, axis=-1, keepdims=True))
            yield lm, qm, kp, vp, e, jnp.sum(e, axis=-1, keepdims=True)


def memattn_fwd(z, kvm, ycat, *, name, n_seq):
    M = z.shape[0]
    tq = 1024
    nq = SEQ // tq

    def body(q_ref, kv_ref, _, o_ref):
        first = _first_head()
        outs = []
        for lm, _, _, vp, e, l in _mem_heads(q_ref[...], kv_ref[...]):
            outs.append(jnp.dot(e.astype(bf16), vp, preferred_element_type=f32) * (1.0 / l))
        pairs = [jnp.where(first, outs[2 * pr], outs[2 * pr + 1]) for pr in range(N_MEM_HEADS // 2)]
        o_ref[...] = jnp.concatenate(pairs, axis=1).astype(bf16)

    return pl.pallas_call(
        body, name=name, grid=(n_seq, nq),
        in_specs=[pl.BlockSpec((tq, MEM_W), lambda b, i: (b * nq + i, 3)),
                  pl.BlockSpec((N_MEM, 2 * MEM_W), lambda b, i: (b, 0)),
                  pl.BlockSpec(memory_space=pl.ANY)],
        out_specs=pl.BlockSpec((tq, MEM_W), lambda b, i: (b * nq + i, 3)),
        out_shape=_sds((M, D_MODEL), bf16),
        input_output_aliases={2: 0},
        compiler_params=_params(("parallel", "parallel")),
    )(z, kvm, ycat)


def memattn_bwd(z, kvm, dyc, dz, *, name, n_seq):
    M = z.shape[0]
    tq = 1024
    nq = SEQ // tq

    def body(q_ref, kv_ref, dy_ref, _, dq_ref, dkv_ref):
        first = _first_head()
        dy = dy_ref[...].astype(f32)
        dqs, dks, dvs = [], [], []
        for h, (lm, qm, kp, vp, e, l) in enumerate(_mem_heads(q_ref[...], kv_ref[...])):
            pr = h // 2
            p = e * (1.0 / l)
            dym = jnp.where(lm, dy[:, pr * PAIR_W:(pr + 1) * PAIR_W], 0.0).astype(bf16)
            dp = lax.dot_general(dym, vp, NT, preferred_element_type=f32)
            ds = (p * (dp - jnp.sum(dp * p, axis=-1, keepdims=True))).astype(bf16)
            dqs.append(jnp.dot(ds, kp, preferred_element_type=f32) * SCALE)
            dk = lax.dot_general(ds, qm, TN, preferred_element_type=f32)
            dv = lax.dot_general(p.astype(bf16), dym, TN, preferred_element_type=f32)
            if h % 2 == 0:
                dks.append(dk)
                dvs.append(dv)
            else:
                dks[pr] = dks[pr] + dk
                dvs[pr] = dvs[pr] + dv
        pairs = [jnp.where(first, dqs[2 * pr], dqs[2 * pr + 1]) for pr in range(N_MEM_HEADS // 2)]
        dq_ref[...] = jnp.concatenate(pairs, axis=1).astype(bf16)

        @pl.when(pl.program_id(1) == 0)
        def _():
            dkv_ref[...] = jnp.zeros_like(dkv_ref)

        dkv_ref[...] += jnp.concatenate(dks + dvs, axis=1)

    return pl.pallas_call(
        body, name=name, grid=(n_seq, nq),
        in_specs=[pl.BlockSpec((tq, MEM_W), lambda b, i: (b * nq + i, 3)),
                  pl.BlockSpec((N_MEM, 2 * MEM_W), lambda b, i: (b, 0)),
                  pl.BlockSpec((tq, MEM_W), lambda b, i: (b * nq + i, 3)),
                  pl.BlockSpec(memory_space=pl.ANY)],
        out_specs=[pl.BlockSpec((tq, MEM_W), lambda b, i: (b * nq + i, 3)),
                   pl.BlockSpec((N_MEM, 2 * MEM_W), lambda b, i: (b, 0))],
        out_shape=[_sds((M, D_MODEL), bf16), _sds((n_seq * N_MEM, 2 * MEM_W), f32)],
        input_output_aliases={3: 0},
        compiler_params=_params(("parallel", "arbitrary")),
    )(z, kvm, dyc, dz)


def rope_tables(pos, *, name):
    M = pos.shape[0]
    tm = min(1024, M)
    half = HEAD_DIM // 2
    inv = ROPE_THETA ** (-np.arange(half, dtype=np.float64) / half)
    inv128 = jnp.asarray(np.tile(inv, 4)[None, :], f32)
    sign128 = jnp.asarray(np.tile(np.concatenate([-np.ones(half), np.ones(half)]), 2)[None, :], f32)

    def body(p_ref, f_ref, s_ref, cos_ref, sin_ref):
        ang = p_ref[...] * f_ref[...]
        cos_ref[...] = jnp.cos(ang)
        sin_ref[...] = jnp.sin(ang) * s_ref[...]

    return pl.pallas_call(
        body, name=name, grid=(M // tm,),
        in_specs=[pl.BlockSpec((tm, 1), lambda i: (i, 0)),
                  pl.BlockSpec((1, 128), lambda i: (0, 0)),
                  pl.BlockSpec((1, 128), lambda i: (0, 0))],
        out_specs=[pl.BlockSpec((tm, 128), lambda i: (i, 0)),
                   pl.BlockSpec((tm, 128), lambda i: (i, 0))],
        out_shape=[_sds((M, 128), f32), _sds((M, 128), f32)],
        compiler_params=_params(("parallel",)),
    )(pos, inv128, sign128)


def _swap_halves(x):
    w = x.shape[1]
    first = (lax.broadcasted_iota(jnp.int32, (1, w), 1) % HEAD_DIM) < (HEAD_DIM // 2)
    return jnp.where(first, pltpu.roll(x, w - HEAD_DIM // 2, 1), pltpu.roll(x, HEAD_DIM // 2, 1))


def group_sum(groups, cos, sin, *, name, rotate, width, col_block=0, into=None):
    M = groups[0][0].shape[0]
    tm = min(512, M)
    counts = [len(g) for g in groups]
    flat = [a for g in groups for a in g]
    extra = [] if into is None else [into]

    def body(*refs):
        part_refs = refs[:len(flat)]
        c_ref, s_ref = refs[len(flat):len(flat) + 2]
        o_ref = refs[-1]
        cols, k = [], 0
        for n in counts:
            acc = part_refs[k][...]
            for r in part_refs[k + 1:k + n]:
                acc = acc + r[...]
            cols.append(acc)
            k += n
        d = jnp.concatenate(cols, axis=1)
        if rotate:
            c = jnp.tile(c_ref[...], (1, MAIN_W // 128))
            s = jnp.tile(s_ref[...], (1, MAIN_W // 128))
            d = d * c - _swap_halves(d) * s
        o_ref[...] = d.astype(bf16)

    part = pl.BlockSpec((tm, GROUP_W), lambda i: (i, 0))
    tab = pl.BlockSpec((tm, 128), lambda i: (i, 0))
    return pl.pallas_call(
        body, name=name, grid=(M // tm,),
        in_specs=[part] * len(flat) + [tab, tab] + [pl.BlockSpec(memory_space=pl.ANY)] * len(extra),
        out_specs=pl.BlockSpec((tm, MAIN_W), lambda i: (i, col_block)),
        out_shape=_sds((M, width), bf16),
        input_output_aliases={len(flat) + 2: 0} if extra else {},
        compiler_params=_params(("parallel",)),
    )(*flat, cos, sin, *extra)


PAIR_W = 2 * HEAD_DIM
MIN_BLOCKS = 8


FWD_TOGETHER = 4
BWD_TOGETHER = 4


def _dil_geometry(dil):
    nsub = max(dil, MIN_BLOCKS)
    tb = BAND * nsub
    return nsub, tb, SEQ // tb


REGROUP = 4


class _Regrouped:
    def __init__(self, ref):
        self.ref = ref
        self.shape = ref.shape

    def fill(self, src):
        q = self.shape[0] // REGROUP
        for r0 in range(REGROUP):
            self.ref[r0 * q:(r0 + 1) * q, :] = src[pl.ds(r0, q, stride=REGROUP), :]

    def drain(self, dst):
        q = self.shape[0] // REGROUP
        for r0 in range(REGROUP):
            dst[pl.ds(r0, q, stride=REGROUP), :] = self.ref[r0 * q:(r0 + 1) * q, :]

    def rows(self, sub, dil):
        nl, r = divmod(sub, dil)
        start = (r % REGROUP) * (self.shape[0] // REGROUP) + r // REGROUP + nl * BAND * (dil // REGROUP)
        return pl.ds(start, BAND, stride=dil // REGROUP)


def _regroups(dil):
    return dil % (4 * REGROUP) == 0


def _rows(ref, sub, dil):
    if isinstance(ref, _Regrouped):
        return ref.ref[ref.rows(sub, dil), :]
    if dil == 1:
        return ref[sub * BAND:(sub + 1) * BAND, :]
    nl, r = divmod(sub, dil)
    return ref[pl.ds(nl * BAND * dil + r, BAND, stride=dil), :]


def _store_rows(ref, sub, dil, val):
    if isinstance(ref, _Regrouped):
        ref.ref[ref.rows(sub, dil), :] = val
    elif dil == 1:
        ref[sub * BAND:(sub + 1) * BAND, :] = val
    else:
        nl, r = divmod(sub, dil)
        ref[pl.ds(nl * BAND * dil + r, BAND, stride=dil), :] = val


def _keys(prev_ref, own_ref, sub, dil):
    nsub = own_ref.shape[0] // BAND
    if sub >= dil:
        prev = _rows(own_ref, sub - dil, dil)
    elif prev_ref is None:
        return _rows(own_ref, sub, dil)
    else:
        prev = _rows(prev_ref, nsub - dil + sub, dil)
    return jnp.concatenate([prev, _rows(own_ref, sub, dil)], axis=0)


def _band_mask(nkeys, has_prev):
    i = lax.broadcasted_iota(jnp.int32, (BAND, nkeys), 0)
    j = lax.broadcasted_iota(jnp.int32, (BAND, nkeys), 1)
    if nkeys == BAND:
        return j <= i
    return (j >= i) & (j <= i + BAND) & (has_prev | (j >= BAND))


def _first_head():
    return lax.broadcasted_iota(jnp.int32, (1, PAIR_W), 1) < HEAD_DIM


def _col(x, hh):
    return x[:, hh * HEAD_DIM:hh * HEAD_DIM + 1]


def _pair_spec(tb, nblk, col0, which):
    def idx(b, p, i):
        if which < 0:
            i = jnp.maximum(i - 1, 0)
        elif which > 0:
            i = jnp.minimum(i + 1, nblk - 1)
        return (b * nblk + i, col0 + p)
    return pl.BlockSpec((tb, PAIR_W), idx)


def dil_fwd(q, k, kv, g, dil, *, name, n_seq):
    M = q.shape[0]
    nsub, tb, nblk = _dil_geometry(dil)
    with_prev = nblk > 1

    regroup = _regroups(dil)
    assert not (regroup and with_prev)

    def body(*refs):
        if with_prev:
            q_ref, ko_ref, vo_ref, kp_ref, vp_ref, o_ref, l_ref = refs
        else:
            (q_ref, ko_ref, vo_ref, o_ref, l_ref), kp_ref, vp_ref = refs[:5], None, None
        outs_to = ()
        if regroup:
            copies = [_Regrouped(s) for s in refs[5:]]
            for c, src in zip(copies, (q_ref, ko_ref, vo_ref)):
                c.fill(src)
            outs_to = ((copies[3], o_ref), (copies[4], l_ref))
            q_ref, ko_ref, vo_ref, o_ref, l_ref = copies
        first = _first_head()
        blk = pl.program_id(2)
        for sub0 in range(0, nsub, FWD_TOGETHER):
            subs = range(sub0, sub0 + FWD_TOGETHER)
            scores, values = [], []
            for sub in subs:
                qs = _rows(q_ref, sub, dil) * SCALE
                kc = _keys(kp_ref, ko_ref, sub, dil).astype(bf16)
                values.append(_keys(vp_ref, vo_ref, sub, dil).astype(bf16))
                has_prev = True if sub >= dil else blk > 0
                mask = _band_mask(kc.shape[0], has_prev)
                for hh in range(2):
                    qm = jnp.where(first if hh == 0 else ~first, qs, 0.0).astype(bf16)
                    scores.append(jnp.where(mask, lax.dot_general(qm, kc, NT, preferred_element_type=f32), NEG))
            soft = []
            for s in scores:
                m = jnp.max(s, axis=-1, keepdims=True)
                e = jnp.exp(s - m)
                l = jnp.sum(e, axis=-1, keepdims=True)
                soft.append((e.astype(bf16), 1.0 / l, jnp.broadcast_to(m + jnp.log(l), (BAND, PAIR_W))))
            outs = [jnp.dot(e, values[n // 2], preferred_element_type=f32) * inv for n, (e, inv, _) in enumerate(soft)]
            for n, sub in enumerate(subs):
                _store_rows(o_ref, sub, dil, jnp.where(first, outs[2 * n], outs[2 * n + 1]))
                _store_rows(l_ref, sub, dil, jnp.where(first, soft[2 * n][2], soft[2 * n + 1][2]))
        for c, dst in outs_to:
            c.drain(dst)

    ins = [(q, 2 * g, 0), (k, 2 * g, 0), (kv, 6 + 2 * g, 0)]
    if with_prev:
        ins += [(k, 2 * g, -1), (kv, 6 + 2 * g, -1)]
    out = _pair_spec(tb, nblk, 0, 0)
    return pl.pallas_call(
        body, name=name, grid=(n_seq, 2, nblk),
        in_specs=[_pair_spec(tb, nblk, c, w) for _, c, w in ins],
        out_specs=[out, out],
        out_shape=[_sds((M, GROUP_W), f32)] * 2,
        scratch_shapes=[pltpu.VMEM((tb, PAIR_W), f32)] * (5 if regroup else 0),
        compiler_params=_params(("parallel", "parallel", "arbitrary")),
    )(*[a for a, _, _ in ins])


def combine_fwd(os_, lses, *, name):
    M = os_[0].shape[0]
    tm = min(512, M)

    def body(o0, o1, o2, l0, l1, l2, y_ref):
        ls = [l0[...], l1[...], l2[...]]
        m = jnp.maximum(jnp.maximum(ls[0], ls[1]), ls[2])
        es = [jnp.exp(l - m) for l in ls]
        inv = 1.0 / (es[0] + es[1] + es[2])
        y_ref[...] = jnp.concatenate([o[...] * e * inv for o, e in zip((o0, o1, o2), es)], axis=1).astype(bf16)

    part = pl.BlockSpec((tm, GROUP_W), lambda i: (i, 0))
    return pl.pallas_call(
        body, name=name, grid=(M // tm,),
        in_specs=[part] * 6,
        out_specs=pl.BlockSpec((tm, MAIN_W), lambda i: (i, 0)),
        out_shape=_sds((M, D_MODEL), bf16),
        compiler_params=_params(("parallel",)),
    )(*os_, *lses)


def combine_bwd(dyc, os_, lses, *, name):
    M = os_[0].shape[0]
    tm = min(512, M)

    def body(dy_ref, o0, o1, o2, l0, l1, l2, d0, d1, d2, c0, c1, c2):
        r = lax.broadcasted_iota(jnp.int32, (GROUP_W, GROUP_W), 0) // HEAD_DIM
        c = lax.broadcasted_iota(jnp.int32, (GROUP_W, GROUP_W), 1) // HEAD_DIM
        ones = (r == c).astype(bf16)
        dy = dy_ref[...].astype(f32)
        ls = [l0[...], l1[...], l2[...]]
        m = jnp.maximum(jnp.maximum(ls[0], ls[1]), ls[2])
        es = [jnp.exp(l - m) for l in ls]
        inv = 1.0 / (es[0] + es[1] + es[2])
        total = 0.0
        alphas = []
        for g, (o, e, d_ref) in enumerate(zip((o0, o1, o2), es, (d0, d1, d2))):
            a = e * inv
            dyg = dy[:, g * GROUP_W:(g + 1) * GROUP_W]
            d_ref[...] = dyg * a
            prod = dyg * o[...]
            hi = prod.astype(bf16)
            lo = (prod - hi.astype(f32)).astype(bf16)
            dsum = jnp.dot(hi, ones, preferred_element_type=f32) + jnp.dot(lo, ones, preferred_element_type=f32)
            total = total + a * dsum
            alphas.append(a)
        for a, c_ref in zip(alphas, (c0, c1, c2)):
            c_ref[...] = -a * total

    part = pl.BlockSpec((tm, GROUP_W), lambda i: (i, 0))
    outs = pl.pallas_call(
        body, name=name, grid=(M // tm,),
        in_specs=[pl.BlockSpec((tm, MAIN_W), lambda i: (i, 0))] + [part] * 6,
        out_specs=[part] * 6,
        out_shape=[_sds((M, GROUP_W), f32)] * 6,
        compiler_params=_params(("parallel",)),
    )(dyc, *os_, *lses)
    return outs[:3], outs[3:]


def dil_bwd(q, k, kv, do, cc, lse, cos, sin, g, dil, *, name, n_seq, into=None):
    M = q.shape[0]
    nsub = SEQ // BAND
    per_res = nsub // dil
    extra = [] if into is None else [into]

    regroup = _regroups(dil)

    def body(q_ref, k_ref, v_ref, do_ref, c_ref, l_ref, cos_ref, sin_ref, *refs):
        dz_ref, dk_ref, dv_ref, dq_ref, *scratch = refs[len(extra):]
        dq_rows = dq_ref
        outs_to = ()
        if regroup:
            copies = [_Regrouped(s) for s in scratch]
            for c, src in zip(copies, (q_ref, k_ref, v_ref, do_ref, c_ref, l_ref)):
                c.fill(src)
            outs_to = tuple(zip(copies[6:], (dq_rows, dk_ref, dv_ref)))
            q_ref, k_ref, v_ref, do_ref, c_ref, l_ref, dq_ref, dk_ref, dv_ref = copies
        first = _first_head()
        order = [nl * dil + r for r in range(dil) for nl in range(per_res)]
        carry = None
        for at in range(0, nsub, BWD_TOGETHER):
            subs = order[at:at + BWD_TOGETHER]
            loaded, products = [], []
            for sub in subs:
                qs = _rows(q_ref, sub, dil) * SCALE
                dos = _rows(do_ref, sub, dil)
                kc = _keys(None, k_ref, sub, dil).astype(bf16)
                vc = _keys(None, v_ref, sub, dil).astype(bf16)
                mask = _band_mask(kc.shape[0], True)
                for hh in range(2):
                    lm = first if hh == 0 else ~first
                    qm = jnp.where(lm, qs, 0.0).astype(bf16)
                    dom = jnp.where(lm, dos, 0.0).astype(bf16)
                    loaded.append((qm, dom, kc))
                    products.append((jnp.where(mask, lax.dot_general(qm, kc, NT, preferred_element_type=f32), NEG),
                                     lax.dot_general(dom, vc, NT, preferred_element_type=f32)))
            weights = []
            for n, (s, dp) in enumerate(products):
                sub, hh = subs[n // 2], n % 2
                p = jnp.exp(s - _col(_rows(l_ref, sub, dil), hh))
                weights.append((p.astype(bf16), (p * (dp + _col(_rows(c_ref, sub, dil), hh))).astype(bf16)))
            results = []
            for (pb, ds), (qm, dom, kc) in zip(weights, loaded):
                results.append((jnp.dot(ds, kc, preferred_element_type=f32) * SCALE,
                                lax.dot_general(ds, qm, TN, preferred_element_type=f32),
                                lax.dot_general(pb, dom, TN, preferred_element_type=f32)))
            for n, sub in enumerate(subs):
                (dq0, dk0, dv0), (dq1, dk1, dv1) = results[2 * n], results[2 * n + 1]
                _store_rows(dq_ref, sub, dil, jnp.where(first, dq0, dq1))
                dkc, dvc = dk0 + dk1, dv0 + dv1
                if sub >= dil:
                    _store_rows(dk_ref, sub - dil, dil, carry[0] + dkc[:BAND])
                    _store_rows(dv_ref, sub - dil, dil, carry[1] + dvc[:BAND])
                    carry = (dkc[BAND:], dvc[BAND:])
                else:
                    carry = (dkc, dvc)
                if sub + dil >= nsub:
                    _store_rows(dk_ref, sub, dil, carry[0])
                    _store_rows(dv_ref, sub, dil, carry[1])
        for c, dst in outs_to:
            c.drain(dst)
        d = dq_rows[...]
        dz_ref[...] = (d * cos_ref[...] - _swap_halves(d) * sin_ref[...]).astype(bf16)

    def spec(col0):
        return pl.BlockSpec((SEQ, PAIR_W), lambda b, p: (b, col0 + p))

    tab = pl.BlockSpec((SEQ, PAIR_W), lambda b, p: (b, 0))
    out = spec(0)
    return pl.pallas_call(
        body, name=name, grid=(n_seq, 2),
        in_specs=[spec(2 * g), spec(2 * g), spec(6 + 2 * g), spec(0), spec(0), spec(0), tab, tab]
        + [pl.BlockSpec(memory_space=pl.ANY)] * len(extra),
        out_specs=[spec(2 * g), out, out],
        out_shape=[_sds((M, D_MODEL), bf16)] + [_sds((M, GROUP_W), f32)] * 2,
        input_output_aliases={8: 0} if extra else {},
        scratch_shapes=[pltpu.VMEM((SEQ, PAIR_W), f32)] * (10 if regroup else 1),
        compiler_params=_params(("parallel", "parallel")),
    )(q, k, kv, do, cc, lse, cos, sin, *extra)


def _blockdiag(wp):
    out = jnp.zeros((MAIN_W, MAIN_W), wp.dtype)
    for gi in range(len(POOL_WINDOWS)):
        sl = slice(gi * POOL_GROUP, (gi + 1) * POOL_GROUP)
        out = out.at[sl, sl].set(wp[gi])
    return out


def _unblockdiag(w):
    return jnp.stack([w[gi * POOL_GROUP:(gi + 1) * POOL_GROUP, gi * POOL_GROUP:(gi + 1) * POOL_GROUP]
                      for gi in range(len(POOL_WINDOWS))])


def local_step(x, mem, positions, target, P, layer_weights, kv_weight, emit_grads):
    n_seq = x.shape[0]
    M = n_seq * SEQ
    xs = x.reshape(M, D_MODEL)
    mems = mem.reshape(n_seq * N_MEM, D_MODEL)
    pos = positions.reshape(M, 1).astype(f32)
    cos, sin = rope_tables(pos, name="rope_tables")
    gains = P["norm_gains"]

    def gain(l, k):
        return gains[l, k].reshape(1, D_MODEL)

    saved = []
    kvs = None
    for l in range(DEPTH):
        W, started = layer_weights(l, "mix", xs)
        sv = {"x": xs, "W": W}
        z, h1, *qrot = rms_matmul(xs, gain(l, 0), W["w_in"], name=f"l{l}_in", out_dtype=bf16, after=started,
                                  rope=None if l < N_A_LAYERS else (cos, sin))
        kvm, mn = rms_matmul(mems, P["mem_norm"][l].reshape(1, D_MODEL), W["w_mem_kv"],
                             name=f"l{l}_memkv", out_dtype=bf16)
        sv.update(z=z, h1=h1, kvm=kvm, mn=mn)
        if l < N_A_LAYERS:
            wbd = _blockdiag(P["w_pool"][l].astype(bf16))
            psc = P["pool_scale"][l].reshape(1, MAIN_W)
            p, y_main = pool_fwd(z, wbd, psc, name=f"l{l}_pool")
            sv.update(p=p, wbd=wbd, psc=psc)
        else:
            (qrot,) = qrot
            os_, lses = [], []
            for g, (_, dil) in enumerate(DIL_PATTERNS):
                o, lse = dil_fwd(qrot, kvs["krot"], kvs["kv"], g, dil, name=f"l{l}_dil{g}", n_seq=n_seq)
                os_.append(o)
                lses.append(lse)
            y_main = combine_fwd(os_, lses, name=f"l{l}_comb")
            sv.update(qrot=qrot, os=os_, lses=lses)
        ycat = memattn_fwd(z, kvm, y_main, name=f"l{l}_memattn", n_seq=n_seq)
        y, x1 = matmul_rms_res(ycat, W["w_out"], gain(l, 1), xs, name=f"l{l}_out")
        W.update(layer_weights(l, "gu", x1)[0])
        fg, fu, a, h2 = rms_gate_up(x1, gain(l, 2), W["w_gate_up"], name=f"l{l}_gu")
        W.update(layer_weights(l, "down", a)[0])
        y2, x2, *sq = matmul_rms_res(a, W["w_down"], gain(l, 3), x1, name=f"l{l}_down",
                                     target=target.reshape(M, D_MODEL) if l == DEPTH - 1 else None)
        sv.update(ycat=ycat, y=y, x1=x1, fg=fg, fu=fu, h2=h2, a=a, y2=y2)
        saved.append(sv)
        xs = x2
        if l == N_A_LAYERS - 1:
            w_kv = kv_weight(xs)
            kv, hkv, krot = rms_matmul(xs, P["kv_norm"].reshape(1, D_MODEL), w_kv, name="kv_proj", out_dtype=f32,
                                       transposed=True, rope=(cos, sin))
            kvs = {"kv": kv, "hkv": hkv, "krot": krot, "x": xs, "w_kv": w_kv}

    dx, (sq,) = xs, sq

    G = {"mem_norm": [None] * DEPTH, "norm_gains": [[None] * 4 for _ in range(DEPTH)],
         "pool_scale": [None] * N_A_LAYERS}
    dk_parts = [[] for _ in range(N_GROUPS)]
    dv_parts = [[] for _ in range(N_GROUPS)]
    emitted = None

    for l in reversed(range(DEPTH)):
        sv = saved[l]
        W = sv["W"]
        gw = {}
        dy2, dgu, G["norm_gains"][l][3] = down_bwd(sv["y2"], gain(l, 3), dx, W["w_down"], sv["fg"], sv["fu"],
                                                   name=f"l{l}_b_dgu", after=emitted)
        gw["w_down"] = matmul(sv["a"], dy2, TN, name=f"l{l}_b_wd", out_dtype=bf16)
        gw["w_gate_up"] = matmul(dgu, sv["h2"], TN, name=f"l{l}_b_wgu", out_dtype=bf16)
        emitted = emit_grads(l, "ffn", gw)
        dx1, G["norm_gains"][l][2] = matmul_rms_bwd(dgu, W["w_gate_up"], NN, sv["x1"], gain(l, 2), dx,
                                                    name=f"l{l}_b_dh2", after=emitted)
        gw = {}
        dy, dycat, G["norm_gains"][l][1] = rms_bwd_matmul(sv["y"], gain(l, 1), dx1, W["w_out"], NT,
                                                          name=f"l{l}_b_dycat", after=emitted)
        gw["w_out"] = matmul(sv["ycat"], dy, TN, name=f"l{l}_b_wout", out_dtype=bf16)
        if l < N_A_LAYERS:
            dz, dwbd, dps = pool_bwd(dycat, sv["p"], sv["wbd"], sv["psc"], name=f"l{l}_b_pool")
            gw["w_pool"] = _unblockdiag(dwbd).reshape(MAIN_W, POOL_GROUP).astype(bf16)
            G["pool_scale"][l] = dps.reshape(MAIN_W)
        else:
            dos, ccs = combine_bwd(dycat, sv["os"], sv["lses"], name=f"l{l}_b_comb")
            dz = None
            for g, (_, dil) in enumerate(DIL_PATTERNS):
                args = (sv["qrot"], kvs["krot"], kvs["kv"], dos[g], ccs[g], sv["lses"][g], cos, sin, g, dil)
                dz, dk, dv = dil_bwd(*args, name=f"l{l}_b_dil{g}", n_seq=n_seq, into=dz)
                dk_parts[g].append(dk)
                dv_parts[g].append(dv)
        dz, dkvm = memattn_bwd(sv["z"], sv["kvm"], dycat, dz, name=f"l{l}_b_memattn", n_seq=n_seq)
        gw["w_mem_kv"] = matmul(sv["mn"], dkvm, TN, name=f"l{l}_b_wmkv", out_dtype=bf16)
        _, G["mem_norm"][l] = matmul_rms_bwd(dkvm, W["w_mem_kv"], NT, mems, P["mem_norm"][l].reshape(1, D_MODEL),
                                             mems, name=f"l{l}_b_dmn")
        gw["w_in"] = matmul(sv["h1"], dz, TN, name=f"l{l}_b_win", out_dtype=bf16)
        if l != N_A_LAYERS:
            emitted = emit_grads(l, "mix", gw)
        dx, G["norm_gains"][l][0] = matmul_rms_bwd(dz, W["w_in"], NT, sv["x"], gain(l, 0), dx1, name=f"l{l}_b_dh1",
                                                   after=emitted)
        if l == N_A_LAYERS:
            dkv = group_sum(dk_parts, cos, sin, name="b_ropek", rotate=True, width=2 * MAIN_W)
            dkv = group_sum(dv_parts, cos, sin, name="b_sumv", rotate=False, width=2 * MAIN_W, col_block=1, into=dkv)
            gw["w_kv"] = matmul(dkv, kvs["hkv"], TN, name="b_wkv", out_dtype=bf16)
            dx, gkn = matmul_rms_bwd(dkv, kvs["w_kv"], NN, kvs["x"], P["kv_norm"].reshape(1, D_MODEL), dx,
                                     name="b_dhkv")
            G["kv_norm"] = gkn.reshape(D_MODEL)
            emitted = emit_grads(l, "mix", gw)

    small = {"pool_scale": jnp.stack(G["pool_scale"]),
             "mem_norm": jnp.concatenate(G["mem_norm"], axis=0),
             "norm_gains": jnp.stack([jnp.concatenate(r, axis=0) for r in G["norm_gains"]]),
             "kv_norm": G["kv_norm"]}
    return sq[0, 0], dx.reshape(n_seq, SEQ, D_MODEL), small, emitted


def to_bf16_layers(stacks, *, name):
    L, n = stacks[0].shape[0], len(stacks)

    def body(*refs):
        ins, outs = refs[:n], refs[n:]
        for j in range(L):
            @pl.when(pl.program_id(0) == j)
            def _():
                for k in range(n):
                    outs[j * n + k][...] = ins[k][...].astype(bf16)

    outs = pl.pallas_call(
        body, name=name, grid=(L,),
        in_specs=[pl.BlockSpec((None,) + s.shape[1:], lambda l: (l, 0, 0)) for s in stacks],
        out_specs=[pl.BlockSpec(s.shape[1:], lambda l: (0, 0)) for _ in range(L) for s in stacks],
        out_shape=[_sds(s.shape[1:], bf16) for _ in range(L) for s in stacks],
        compiler_params=_params(("arbitrary",)),
    )(*stacks)
    return [outs[j * n:(j + 1) * n] for j in range(L)]


def _peer(k):
    x, y, c = lax.axis_index("x"), lax.axis_index("y"), lax.axis_index("c")
    px = 1 - x if k & 4 else x
    py = 1 - y if k & 2 else y
    pc = 1 - c if k & 1 else c
    return (px, py, pc), 4 * px + 2 * py + pc


def _my_index():
    return 4 * lax.axis_index("x") + 2 * lax.axis_index("y") + lax.axis_index("c")


def _src_for(kinds, in_refs, i, idx):
    return in_refs[i] if kinds[i] == "gather" else in_refs[i].at[idx]


def _local_copies(kinds, in_refs, out_refs, local_sems):
    me = _my_index()
    return [pltpu.make_async_copy(_src_for(kinds, in_refs, i, me), out_refs[i].at[me], local_sems.at[i])
            for i in range(len(kinds))]


def _remote_copies(kinds, in_refs, out_refs, send_sems, recv_sems, *, arriving):
    me = _my_index()
    copies = []
    for k in range(1, N_DEV):
        dev, idx = _peer(k)
        for i in range(len(kinds)):
            j = i * (N_DEV - 1) + k - 1
            copies.append(pltpu.make_async_remote_copy(
                src_ref=_src_for(kinds, in_refs, i, idx), dst_ref=out_refs[i].at[idx if arriving else me],
                send_sem=send_sems.at[j], recv_sem=recv_sems.at[j], device_id=dev, device_id_type=MESH))
    return copies


def _out_shape(a, kind):
    return ((N_DEV,) + a.shape) if kind == "gather" else a.shape


def exchange(items, *, name, after=()):
    n = len(items)
    kinds = [k for _, k in items]
    after = list(after)

    def body(*refs):
        in_refs, out_refs = refs[:n], refs[n + len(after):2 * n + len(after)]
        send_sems, recv_sems, local_sems = refs[-3:]
        local = _local_copies(kinds, in_refs, out_refs, local_sems)
        sends = _remote_copies(kinds, in_refs, out_refs, send_sems, recv_sems, arriving=False)
        for cp in local + sends:
            cp.start()
        for cp in _remote_copies(kinds, in_refs, out_refs, send_sems, recv_sems, arriving=True):
            cp.wait_recv()
        for cp in sends:
            cp.wait_send()
        for cp in local:
            cp.wait()

    any_spec = pl.BlockSpec(memory_space=pl.ANY)
    return pl.pallas_call(
        body, name=name,
        in_specs=[any_spec] * (n + len(after)), out_specs=[any_spec] * n,
        out_shape=[_sds(_out_shape(a, k), a.dtype) for a, k in items],
        scratch_shapes=[pltpu.SemaphoreType.DMA((n * (N_DEV - 1),)), pltpu.SemaphoreType.DMA((n * (N_DEV - 1),)),
                        pltpu.SemaphoreType.DMA((n,))],
    )(*[a for a, _ in items], *after)


_HBM = pl.BlockSpec(memory_space=pltpu.HBM)
_SEM = pl.BlockSpec(memory_space=pltpu.SEMAPHORE)
_EFFECT = pltpu.SideEffectType.DATAFLOW_SIDE_EFFECTING


def exchange_start(items, after, *, name):
    n = len(items)
    kinds = [k for _, k in items]

    def body(*refs):
        in_refs, land_refs = refs[:n], refs[n:2 * n]
        send_sems, recv_sems, local_sems = refs[2 * n + 1:2 * n + 4]
        token = refs[-1]
        for cp in (_local_copies(kinds, in_refs, land_refs, local_sems)
                   + _remote_copies(kinds, in_refs, land_refs, send_sems, recv_sems, arriving=False)):
            cp.start()
        token[...] = jnp.zeros_like(token)

    srcs = [pltpu.with_memory_space_constraint(a, pltpu.HBM) for a, _ in items]
    lands = [pltpu.with_memory_space_constraint(lax.empty(_out_shape(a, k), a.dtype), pltpu.HBM) for a, k in items]
    outs = pl.pallas_call(
        body, name=name,
        out_shape=(pltpu.SemaphoreType.DMA((n * (N_DEV - 1),)), pltpu.SemaphoreType.DMA((n * (N_DEV - 1),)),
                   pltpu.SemaphoreType.DMA((n,)),
                   *[pltpu.HBM(a.shape, a.dtype) for a in srcs], *[pltpu.HBM(a.shape, a.dtype) for a in lands],
                   _sds((8, 128), f32)),
        in_specs=[_HBM] * (2 * n) + [pl.BlockSpec(memory_space=pl.ANY)],
        out_specs=(_SEM, _SEM, _SEM, *[_HBM] * (2 * n), pl.BlockSpec(memory_space=pltpu.VMEM)),
        input_output_aliases={i: 3 + i for i in range(2 * n)},
        compiler_params=pltpu.CompilerParams(has_side_effects=_EFFECT),
    )(*srcs, *lands, after)
    return {"kinds": kinds, "sems": outs[:3], "srcs": outs[3:3 + n], "lands": outs[3 + n:3 + 2 * n], "token": outs[-1]}


def exchange_wait(handle, after, *, name):
    kinds = handle["kinds"]
    n = len(kinds)

    def body(*refs):
        in_refs, land_refs = refs[:n], refs[n:2 * n]
        send_sems, recv_sems, local_sems = refs[2 * n:2 * n + 3]
        for cp in _remote_copies(kinds, in_refs, land_refs, send_sems, recv_sems, arriving=True):
            cp.wait_recv()
        for cp in _remote_copies(kinds, in_refs, land_refs, send_sems, recv_sems, arriving=False):
            cp.wait_send()
        for cp in _local_copies(kinds, in_refs, land_refs, local_sems):
            cp.wait()

    srcs, lands = list(handle["srcs"]), list(handle["lands"])
    after = list(after) if isinstance(after, (list, tuple)) else [after]
    outs = pl.pallas_call(
        body, name=name,
        out_shape=tuple(pltpu.HBM(a.shape, a.dtype) for a in srcs + lands),
        in_specs=[_HBM] * (2 * n) + [_SEM] * 3 + [pl.BlockSpec(memory_space=pl.ANY)] * len(after),
        out_specs=tuple([_HBM] * (2 * n)),
        input_output_aliases={i: i for i in range(2 * n)},
        compiler_params=pltpu.CompilerParams(has_side_effects=_EFFECT),
    )(*srcs, *lands, *handle["sems"], *after)
    return list(outs[n:])


CHIP_MASKS = (2, 4, 6)


def _g2_first(in_refs, land_refs, send_sems, recv_sems, *, masks, arriving):
    me = _my_index()
    copies = []
    for i in range(len(land_refs)):
        for j, k in enumerate(masks):
            dev, idx = _peer(k)
            dst = land_refs[i].at[idx if arriving else me]
            copies.append(pltpu.make_async_remote_copy(
                src_ref=dst if in_refs is None else in_refs[i], dst_ref=dst,
                send_sem=send_sems.at[i * len(masks) + j], recv_sem=recv_sems.at[i * len(masks) + j],
                device_id=dev, device_id_type=MESH))
    return copies


def _g2_forward(land_refs, fwd_send, fwd_recv, *, arriving):
    sibling, _ = _peer(1)
    copies = []
    for i in range(len(land_refs)):
        for j, k in enumerate(CHIP_MASKS):
            _, idx = _peer(k | 1 if arriving else k)
            copies.append(pltpu.make_async_remote_copy(
                src_ref=land_refs[i].at[idx], dst_ref=land_refs[i].at[idx],
                send_sem=fwd_send.at[i * 3 + j], recv_sem=fwd_recv.at[i * 3 + j], device_id=sibling,
                device_id_type=MESH))
    return copies


def gather2_start(arrays, after, *, name):
    n = len(arrays)

    def body(*refs):
        in_refs, land_refs = refs[:n], refs[n:2 * n]
        ici_send, ici_recv, d2d_send, d2d_recv, local_sems = refs[2 * n + 1:2 * n + 6]
        token = refs[-1]
        ici = _g2_first(in_refs, land_refs, ici_send, ici_recv, masks=CHIP_MASKS, arriving=False)
        d2d = _g2_first(in_refs, land_refs, d2d_send, d2d_recv, masks=(1,), arriving=False)
        for cp in _local_copies(["gather"] * n, in_refs, land_refs, local_sems) + ici + d2d:
            cp.start()
        token[...] = jnp.zeros_like(token)

    srcs = [pltpu.with_memory_space_constraint(a, pltpu.HBM) for a in arrays]
    lands = [pltpu.with_memory_space_constraint(lax.empty((N_DEV,) + a.shape, a.dtype), pltpu.HBM) for a in arrays]
    sem = pltpu.SemaphoreType.DMA
    outs = pl.pallas_call(
        body, name=name,
        out_shape=(sem((3 * n,)), sem((3 * n,)), sem((n,)), sem((n,)), sem((n,)),
                   *[pltpu.HBM(a.shape, a.dtype) for a in srcs], *[pltpu.HBM(a.shape, a.dtype) for a in lands],
                   _sds((8, 128), f32)),
        in_specs=[_HBM] * (2 * n) + [pl.BlockSpec(memory_space=pl.ANY)],
        out_specs=(*[_SEM] * 5, *[_HBM] * (2 * n), pl.BlockSpec(memory_space=pltpu.VMEM)),
        input_output_aliases={i: 5 + i for i in range(2 * n)},
        compiler_params=pltpu.CompilerParams(has_side_effects=_EFFECT),
    )(*srcs, *lands, after)
    return {"n": n, "sems": outs[:5], "srcs": outs[5:5 + n], "lands": outs[5 + n:5 + 2 * n], "token": outs[-1]}


def gather2_forward(handle, after, *, name):
    n = handle["n"]

    def body(*refs):
        land_refs = refs[:n]
        ici_recv = refs[n]
        fwd_send, fwd_recv = refs[n + 2:n + 4]
        for cp in _g2_first(None, land_refs, fwd_send, ici_recv, masks=CHIP_MASKS, arriving=True):
            cp.wait_recv()
        for cp in _g2_forward(land_refs, fwd_send, fwd_recv, arriving=False):
            cp.start()

    lands = list(handle["lands"])
    sem = pltpu.SemaphoreType.DMA
    outs = pl.pallas_call(
        body, name=name,
        out_shape=(sem((3 * n,)), sem((3 * n,)), *[pltpu.HBM(a.shape, a.dtype) for a in lands]),
        in_specs=[_HBM] * n + [_SEM, pl.BlockSpec(memory_space=pl.ANY)],
        out_specs=(_SEM, _SEM, *[_HBM] * n),
        input_output_aliases={i: 2 + i for i in range(n)},
        compiler_params=pltpu.CompilerParams(has_side_effects=_EFFECT),
    )(*lands, handle["sems"][1], after)
    return dict(handle, fwd=outs[:2], lands=outs[2:])


def gather2_wait(handle, after, *, name):
    n = handle["n"]

    def body(*refs):
        in_refs, land_refs = refs[:n], refs[n:2 * n]
        ici_send, d2d_send, d2d_recv, local_sems, fwd_send, fwd_recv = refs[2 * n:2 * n + 6]
        for cp in _g2_first(in_refs, land_refs, d2d_send, d2d_recv, masks=(1,), arriving=True):
            cp.wait_recv()
        for cp in _g2_forward(land_refs, fwd_send, fwd_recv, arriving=True):
            cp.wait_recv()
        for cp in (_g2_first(in_refs, land_refs, ici_send, fwd_recv, masks=CHIP_MASKS, arriving=False)
                   + _g2_first(in_refs, land_refs, d2d_send, d2d_recv, masks=(1,), arriving=False)
                   + _g2_forward(land_refs, fwd_send, fwd_recv, arriving=False)):
            cp.wait_send()
        for cp in _local_copies(["gather"] * n, in_refs, land_refs, local_sems):
            cp.wait()

    srcs, lands = list(handle["srcs"]), list(handle["lands"])
    s = handle["sems"]
    outs = pl.pallas_call(
        body, name=name,
        out_shape=tuple(pltpu.HBM(a.shape, a.dtype) for a in srcs + lands),
        in_specs=[_HBM] * (2 * n) + [_SEM] * 6 + [pl.BlockSpec(memory_space=pl.ANY)],
        out_specs=tuple([_HBM] * (2 * n)),
        input_output_aliases={i: i for i in range(2 * n)},
        compiler_params=pltpu.CompilerParams(has_side_effects=_EFFECT),
    )(*srcs, *lands, s[0], s[2], s[3], s[4], *handle["fwd"], after)
    return list(outs[n:])


def adamw(entries, *, name):
    c1 = 1.0 - ADAM_B1 ** ADAM_STEP
    c2 = 1.0 - ADAM_B2 ** ADAM_STEP
    tiles = [_tile(w.shape[-2], (64, 32, 16, 8)) for _, w, _, _, _, _ in entries]
    steps = [w.shape[-2] // tr for (_, w, _, _, _, _), tr in zip(entries, tiles)]
    n = len(entries)

    def body(*refs):
        i = pl.program_id(0)
        for e in range(n):
            s_ref, w_ref, m_ref, v_ref = refs[4 * e:4 * e + 4]
            g_ref, d_ref, m2_ref, v2_ref = refs[len(refs) - 4 * n + 4 * e:len(refs) - 4 * n + 4 * e + 4]

            @pl.when(i < steps[e])
            def _():
                g = s_ref[0].astype(f32)
                for d in range(1, N_DEV):
                    g = g + s_ref[d].astype(f32)
                m2 = ADAM_B1 * m_ref[...] + (1.0 - ADAM_B1) * g
                v2 = ADAM_B2 * v_ref[...] + (1.0 - ADAM_B2) * (g * g)
                g_ref[...] = g
                m2_ref[...] = m2
                v2_ref[...] = v2
                d_ref[...] = -ADAM_LR * ((m2 / c1) / (jnp.sqrt(v2 / c2) + ADAM_EPS) + ADAM_WD * w_ref[...])

    in_specs, out_specs, out_shape, args, extras, aliases = [], [], [], [], [], {}
    for e, ((slots, w, m, v, layer, into), tr, ns) in enumerate(zip(entries, tiles, steps)):
        C = w.shape[-1]
        row = lambda i, ns=ns: jnp.minimum(i, ns - 1)
        if layer is None:
            blk = pl.BlockSpec((tr, C), lambda i, row=row: (row(i), 0))
        else:
            blk = pl.BlockSpec((None, tr, C), lambda i, row=row, layer=layer: (layer, row(i), 0))
        in_specs += [pl.BlockSpec((N_DEV, tr, C), lambda i, row=row: (0, row(i), 0)), blk, blk, blk]
        args += [slots, w, m, v]
        out_specs += [blk] * 4
        out_shape += [_sds(w.shape, f32)] * 4
        if into is not None:
            for t, a in enumerate(into):
                aliases[4 * n + len(extras)] = 4 * e + t
                extras.append(a)
    outs = pl.pallas_call(
        body, name=name, grid=(max(steps),),
        in_specs=in_specs + [pl.BlockSpec(memory_space=pl.ANY)] * len(extras),
        out_specs=out_specs, out_shape=out_shape, input_output_aliases=aliases,
        compiler_params=_params(("arbitrary",)),
    )(*args, *extras)
    return [outs[4 * e:4 * e + 4] for e in range(n)]


WEIGHTS = ("norm_gains", "mem_norm", "w_in", "w_mem_kv", "w_out", "w_pool", "pool_scale", "kv_norm", "w_kv",
           "w_gate_up", "w_down")
LAYER_MATS = ("w_in", "w_mem_kv", "w_out", "w_gate_up", "w_down")
POOL_SHARD = MAIN_W // N_DEV
KV_SHARD = 2 * MAIN_W // N_DEV
LOOKAHEAD = 2
TWO_LEVEL_LAYERS = (0, 1)


def _pack_small(gains, pscale):
    lead = gains.shape[:-3]
    g = gains.reshape(lead + (16, 128))
    p = jnp.zeros(lead + (8, 128), f32).at[..., :2, :POOL_SHARD].set(pscale)
    return jnp.concatenate([g, p], axis=-2)


def _unpack_small(a):
    return a[:16].reshape(4, 4, 128), a[16:18, :POOL_SHARD]


def _pack_repl(mem_norm, kv_norm):
    return jnp.concatenate([mem_norm, kv_norm.reshape(1, D_MODEL), jnp.zeros((3, D_MODEL), f32)], axis=0)


def _unpack_repl(a):
    return a[:4], a[4]


def kernel(x, mem, positions, norm_gains, mem_norm, w_in, w_mem_kv, w_out, w_pool, pool_scale, kv_norm, w_kv, w_gate_up, w_down, loss_target, m_norm_gains, m_mem_norm, m_w_in, m_w_mem_kv, m_w_out, m_w_pool, m_pool_scale, m_kv_norm, m_w_kv, m_w_gate_up, m_w_down, v_norm_gains, v_mem_norm, v_w_in, v_w_mem_kv, v_w_out, v_w_pool, v_pool_scale, v_kv_norm, v_w_kv, v_w_gate_up, v_w_down):
    w = dict(norm_gains=norm_gains, mem_norm=mem_norm, w_in=w_in, w_mem_kv=w_mem_kv, w_out=w_out, w_pool=w_pool,
             pool_scale=pool_scale, kv_norm=kv_norm, w_kv=w_kv, w_gate_up=w_gate_up, w_down=w_down)
    m = dict(norm_gains=m_norm_gains, mem_norm=m_mem_norm, w_in=m_w_in, w_mem_kv=m_w_mem_kv, w_out=m_w_out,
             w_pool=m_w_pool, pool_scale=m_pool_scale, kv_norm=m_kv_norm, w_kv=m_w_kv, w_gate_up=m_w_gate_up,
             w_down=m_w_down)
    v = dict(norm_gains=v_norm_gains, mem_norm=v_mem_norm, w_in=v_w_in, w_mem_kv=v_w_mem_kv, w_out=v_w_out,
             w_pool=v_w_pool, pool_scale=v_pool_scale, kv_norm=v_kv_norm, w_kv=v_w_kv, w_gate_up=v_w_gate_up,
             w_down=v_w_down)

    def transposed_view(d):
        d = dict(d)
        d["w_gate_up"] = jnp.swapaxes(d["w_gate_up"], 1, 2)
        d["w_kv"] = jnp.swapaxes(d["w_kv"], 0, 1)
        return d

    wv, mv, vv = transposed_view(w), transposed_view(m), transposed_view(v)

    small = _pack_small(norm_gains, pool_scale)
    (gsmall,) = exchange([(small, "gather")], name="gather_small")
    P = {"norm_gains": jnp.moveaxis(gsmall[:, :16].reshape(N_DEV, 4, 4, 128), 0, 2).reshape(4, 4, D_MODEL),
         "pool_scale": jnp.moveaxis(gsmall[:, 16:18, :POOL_SHARD], 0, 1).reshape(2, MAIN_W),
         "mem_norm": mem_norm, "kv_norm": kv_norm, "w_pool": w_pool}

    PARTS = {"mix": ("w_in", "w_mem_kv", "w_out"), "ffn": ("w_gate_up", "w_down"), "gu": ("w_gate_up",),
             "down": ("w_down",), "all": ("w_in", "w_mem_kv", "w_out", "w_gate_up", "w_down")}

    def parts_of(l):
        return (("mix", "gu", "down"), ("mix", "ffn"))[l] if l < 2 else ("all",)

    wb = [dict(zip(LAYER_MATS, mats)) for mats in to_bf16_layers([wv[k] for k in LAYER_MATS], name="weights_bf16")]

    def part_items(l, part):
        items = [(wb[l][k], "gather") for k in PARTS[part]]
        if part == "ffn" and l == N_A_LAYERS - 1:
            items.append((wv["w_kv"].astype(bf16), "gather"))
        return items

    handles = {}

    def start_layer(l, after):
        for part in parts_of(l):
            if l in TWO_LEVEL_LAYERS:
                handles[l, part] = gather2_start([a for a, _ in part_items(l, part)], after,
                                                 name=f"gather_start_{part}_l{l}")
            else:
                handles[l, part] = exchange_start(part_items(l, part), after, name=f"gather_start_{part}_l{l}")
            after = handles[l, part]["token"]
        return after

    token = gsmall
    for l in range(LOOKAHEAD):
        token = start_layer(l, token)
    landed = {}

    def layer_weights(l, part, after):
        if part not in parts_of(l):
            if part == "down" or (part == "gu" and "all" in parts_of(l)):
                return {}, None
            part = "all" if "all" in parts_of(l) else "ffn"
        if l == 0 and part == "mix":
            after = token
        if l in TWO_LEVEL_LAYERS:
            passed = gather2_forward(handles[l, part], after, name=f"gather_forward_{part}_l{l}")
            got = gather2_wait(passed, after, name=f"gather_wait_{part}_l{l}")
        else:
            got = exchange_wait(handles[l, part], after, name=f"gather_wait_{part}_l{l}")
        landed[l, part] = got
        started = None
        if part in ("mix", "all") and l + LOOKAHEAD < DEPTH:
            started = start_layer(l + LOOKAHEAD, got[0])
        W = {k: g.reshape(-1, g.shape[-1]) for k, g in zip(PARTS[part], got)}
        return W, started

    def kv_weight(after):
        g = landed[N_A_LAYERS - 1, "ffn"][len(PARTS["ffn"])]
        return g.reshape(2 * MAIN_W, D_MODEL)

    ghandles = {}

    pending = {}

    def gparts_of(l):
        return ("ffn", "mix") if l < 2 else ("all",)

    def emit_grads(l, part, gw):
        if part not in gparts_of(l):
            pending.setdefault(l, {}).update(gw)
            if part == "ffn":
                return None
            gw, part = pending[l], "all"
        items = [(gw[k].reshape((N_DEV, -1) + gw[k].shape[-1:]), "scatter") for k in PARTS[part]]
        if part != "ffn" and l == N_A_LAYERS:
            items.append((gw["w_kv"].reshape(N_DEV, KV_SHARD, D_MODEL), "scatter"))
        if part != "ffn" and l < N_A_LAYERS:
            items.append((gw["w_pool"], "gather"))
        ghandles[l, part] = exchange_start(items, gsmall, name=f"scatter_start_{part}_l{l}")
        return ghandles[l, part]["token"]

    sq, grad_x, GS, emitted = local_step(x, mem, positions, loss_target, P, layer_weights, kv_weight, emit_grads)

    def pool3(a):
        return a.reshape(N_A_LAYERS, MAIN_W, POOL_GROUP)

    out = {}
    after = [emitted]

    def finish_layer(l, after):
        for part in gparts_of(l):
            got = exchange_wait(ghandles[l, part], after, name=f"scatter_wait_{part}_l{l}")
            names = list(PARTS[part])
            entries = [(slots, wv[k], mv[k], vv[k], l, out.get(k)) for k, slots in zip(names, got)]
            if part != "ffn" and l == N_A_LAYERS:
                names.append("w_kv")
                entries.append((got[-1], wv["w_kv"], mv["w_kv"], vv["w_kv"], None, None))
            if part != "ffn" and l < N_A_LAYERS:
                names.append("w_pool")
                entries.append((got[-1], pool3(w_pool), pool3(m_w_pool), pool3(v_w_pool), l, out.get("w_pool")))
            out.update(zip(names, adamw(entries, name=f"adamw_{part}_l{l}")))
            after = [out[k][0] for k in names]
        return after

    for l in reversed(range(1, DEPTH)):
        after = finish_layer(l, after)

    gs = _pack_small(jnp.moveaxis(GS["norm_gains"].reshape(4, 4, N_DEV, 128), 2, 0),
                     jnp.moveaxis(GS["pool_scale"].reshape(2, N_DEV, POOL_SHARD), 1, 0))
    parts_small, parts_repl, parts_sq = exchange(
        [(gs, "scatter"), (_pack_repl(GS["mem_norm"], GS["kv_norm"]), "gather"),
         (jnp.full((8, 128), sq, f32), "gather")],
        name="exchange_small_grads", after=after)
    loss = (0.5 / D_MODEL) * jnp.sum(parts_sq[:, 0, 0])
    finish_layer(0, [parts_small])
    out["w_gate_up"] = [jnp.swapaxes(r, 1, 2) for r in out["w_gate_up"]]
    out["w_kv"] = [jnp.swapaxes(r, 0, 1) for r in out["w_kv"]]
    out["w_pool"] = [r.reshape(w_pool.shape) for r in out["w_pool"]]

    res_small, res_repl = adamw(
        [(parts_small, small, _pack_small(m_norm_gains, m_pool_scale), _pack_small(v_norm_gains, v_pool_scale),
          None, None),
         (parts_repl, _pack_repl(mem_norm, kv_norm), _pack_repl(m_mem_norm, m_kv_norm),
          _pack_repl(v_mem_norm, v_kv_norm), None, None)], name="adamw_small")
    out["norm_gains"], out["pool_scale"] = zip(*[_unpack_small(r) for r in res_small])
    out["mem_norm"], out["kv_norm"] = zip(*[_unpack_repl(r) for r in res_repl])

    return (loss, grad_x, *[out[k][0] for k in WEIGHTS], *[out[k][1] for k in WEIGHTS],
            *[out[k][2] for k in WEIGHTS], *[out[k][3] for k in WEIGHTS])
```

```python
import numpy as np
import jax
import jax.numpy as jnp
from jax import lax
from jax.experimental import pallas as pl
from jax.experimental.pallas import tpu as pltpu

f32 = jnp.float32
bf16 = jnp.bfloat16

D_MODEL = 1024
SEQ = 2048
DEPTH = 4
N_MEM = 256
HEAD_DIM = 64
N_MEM_HEADS = 4
MEM_W = 256
MAIN_W = 768
POOL_WINDOWS = (2, 4, 8, 16)
POOL_GROUP = 192
POOL_HALO = 16
DIL_PATTERNS = ((128, 1), (512, 4), (2048, 16))
N_GROUPS = 3
GROUP_W = 256
BAND = 128
N_A_LAYERS = 2
D_FF = 2816
ROPE_THETA = 10000.0
EPS = 1e-6
NEG = -1e30
SCALE = HEAD_DIM ** -0.5
N_DEV = 8

ADAM_LR = 0.001
ADAM_B1 = 0.9
ADAM_B2 = 0.999
ADAM_EPS = 1e-08
ADAM_WD = 0.01
ADAM_STEP = 10

VMEM_LIMIT_BYTES = 56 * 1024 * 1024
MESH = pl.DeviceIdType.MESH

NN = (((1,), (0,)), ((), ()))
NT = (((1,), (1,)), ((), ()))
TN = (((0,), (0,)), ((), ()))


def _params(sem=None):
    return pltpu.CompilerParams(dimension_semantics=sem, vmem_limit_bytes=VMEM_LIMIT_BYTES)


def _tile(n, cands):
    for c in cands:
        if n % c == 0:
            return c
    return n


def _sds(shape, dtype):
    return jax.ShapeDtypeStruct(tuple(shape), dtype)


def _rms_r(v):
    return lax.rsqrt(jnp.mean(v * v, axis=-1, keepdims=True) + EPS)


ROW_GROUPS = 2


def rms_matmul(x, gain, w, *, name, out_dtype, transposed=False, after=None, rope=None):
    M, K = x.shape
    N = w.shape[0] if transposed else w.shape[1]
    tm = min(512, M)
    order = [] if after is None else [after]
    tables = [] if rope is None else list(rope)
    rot_spec = [] if rope is None else [pl.BlockSpec((tm, MAIN_W), lambda i: (i, 0))]
    rot_shape = [] if rope is None else [_sds((M, MAIN_W), f32)]

    def body(x_ref, g_ref, w_ref, *refs):
        z_ref, h_ref = refs[len(tables) + len(order):][:2]
        groups = [slice(g * tm // ROW_GROUPS, (g + 1) * tm // ROW_GROUPS) for g in range(ROW_GROUPS)]
        hs = []
        for rows in groups:
            xv = x_ref[rows, :]
            hs.append((xv * _rms_r(xv) * g_ref[...]).astype(bf16))
            h_ref[rows, :] = hs[-1]
        zs = [lax.dot_general(h, w_ref[...], NT if transposed else NN, preferred_element_type=f32) for h in hs]
        for rows, z in zip(groups, zs):
            z_ref[rows, :] = z.astype(z_ref.dtype)
            if tables:
                c = jnp.tile(refs[0][rows, :], (1, MAIN_W // 128))
                s = jnp.tile(refs[1][rows, :], (1, MAIN_W // 128))
                zr = z[:, :MAIN_W]
                refs[-1][rows, :] = zr * c + _swap_halves(zr) * s

    tab = pl.BlockSpec((tm, 128), lambda i: (i, 0))
    return pl.pallas_call(
        body, name=name, grid=(M // tm,),
        in_specs=[pl.BlockSpec((tm, K), lambda i: (i, 0)),
                  pl.BlockSpec((1, K), lambda i: (0, 0)),
                  pl.BlockSpec(w.shape, lambda i: (0, 0))] + [tab] * len(tables)
        + [pl.BlockSpec(memory_space=pl.ANY)] * len(order),
        out_specs=[pl.BlockSpec((tm, N), lambda i: (i, 0)), pl.BlockSpec((tm, K), lambda i: (i, 0))] + rot_spec,
        out_shape=[_sds((M, N), out_dtype), _sds((M, K), bf16)] + rot_shape,
        compiler_params=_params(("parallel",)),
    )(x, gain, w, *tables, *order)


def matmul_rms_res(a, w, gain, res, *, name, target=None, after=None):
    M, K = a.shape
    N = w.shape[1]
    tm = min(512, M)
    goal = [] if target is None else [target]
    order = [] if after is None else [after]

    def body(a_ref, w_ref, g_ref, r_ref, *refs):
        y_ref, x_ref = refs[len(goal) + len(order):][:2]
        groups = [slice(g * tm // ROW_GROUPS, (g + 1) * tm // ROW_GROUPS) for g in range(ROW_GROUPS)]
        ys = [jnp.dot(a_ref[rows, :], w_ref[...], preferred_element_type=f32) for rows in groups]
        sq = 0.0
        for rows, y in zip(groups, ys):
            y_ref[rows, :] = y.astype(bf16)
            x = r_ref[rows, :] + y * _rms_r(y) * g_ref[...]
            if goal:
                e = x - refs[0][rows, :]
                x = e * (1.0 / N)
                sq = sq + jnp.sum(jnp.sum(e * e, axis=0, keepdims=True), axis=1, keepdims=True)
            x_ref[rows, :] = x
        if goal:
            _accumulate(refs[-1], sq)

    row = pl.BlockSpec((tm, N), lambda i: (i, 0))
    return pl.pallas_call(
        body, name=name, grid=(M // tm,),
        in_specs=[pl.BlockSpec((tm, K), lambda i: (i, 0)),
                  pl.BlockSpec((K, N), lambda i: (0, 0)),
                  pl.BlockSpec((1, N), lambda i: (0, 0)),
                  row] + [row] * len(goal) + [pl.BlockSpec(memory_space=pl.ANY)] * len(order),
        out_specs=[row, row] + [pl.BlockSpec((8, 128), lambda i: (0, 0))] * len(goal),
        out_shape=[_sds((M, N), bf16), _sds((M, N), f32)] + [_sds((8, 128), f32)] * len(goal),
        compiler_params=_params(("arbitrary",) if goal else ("parallel",)),
    )(a, w, gain, res, *goal, *order)


def matmul(a, b, dims, *, name, out_dtype):
    if dims is TN:
        K, M = a.shape
        tm = _tile(M, (512, 256, 128))
        a_spec = pl.BlockSpec((K, tm), lambda i: (0, i))
    else:
        M, K = a.shape
        tm = _tile(M, (1024, 512, 256, 128))
        a_spec = pl.BlockSpec((tm, K), lambda i: (i, 0))
    N = b.shape[0] if dims is NT else b.shape[1]

    def body(a_ref, b_ref, o_ref):
        o_ref[...] = lax.dot_general(a_ref[...].astype(bf16), b_ref[...].astype(bf16), dims,
                                     preferred_element_type=f32).astype(o_ref.dtype)

    return pl.pallas_call(
        body, name=name, grid=(M // tm,),
        in_specs=[a_spec, pl.BlockSpec(b.shape, lambda i: (0, 0))],
        out_specs=pl.BlockSpec((tm, N), lambda i: (i, 0)),
        out_shape=_sds((M, N), out_dtype),
        compiler_params=_params(("parallel",)),
    )(a, b)


def rms_gate_up(x, gain, wt, *, name):
    M, K = x.shape
    tm = min(2048, M)
    tn = _tile(D_FF, (256, 128))
    nj = D_FF // tn

    def body(x_ref, gn_ref, wg_ref, wu_ref, g_ref, u_ref, a_ref, h_ref):
        @pl.when(pl.program_id(1) == 0)
        def _():
            xv = x_ref[...]
            h_ref[...] = (xv * _rms_r(xv) * gn_ref[...]).astype(bf16)

        h = h_ref[...]
        g = lax.dot_general(h, wg_ref[...], NT, preferred_element_type=f32).astype(bf16)
        u = lax.dot_general(h, wu_ref[...], NT, preferred_element_type=f32).astype(bf16)
        g_ref[...] = g
        u_ref[...] = u
        a_ref[...] = g * (1.0 / (1.0 + jnp.exp(-g))) * u

    col = pl.BlockSpec((tm, tn), lambda i, j: (i, j))
    return pl.pallas_call(
        body, name=name, grid=(M // tm, nj),
        in_specs=[pl.BlockSpec((tm, K), lambda i, j: (i, 0)),
                  pl.BlockSpec((1, K), lambda i, j: (0, 0)),
                  pl.BlockSpec((tn, K), lambda i, j: (j, 0)),
                  pl.BlockSpec((tn, K), lambda i, j: (j + nj, 0))],
        out_specs=[col, col, col, pl.BlockSpec((tm, K), lambda i, j: (i, 0))],
        out_shape=[_sds((M, D_FF), bf16)] * 3 + [_sds((M, K), bf16)],
        compiler_params=_params(("parallel", "arbitrary")),
    )(x, gain, wt, wt)


def _rms_bwd_math(yv, gain, dn):
    r = _rms_r(yv)
    q = dn * gain
    dy = r * q - yv * (r * r * r) * jnp.mean(q * yv, axis=-1, keepdims=True)
    return dy, jnp.sum(dn * yv * r, axis=0, keepdims=True)


def _accumulate(ref, val):
    @pl.when(pl.program_id(0) == 0)
    def _():
        ref[...] = jnp.zeros_like(ref)

    ref[...] += val


def down_bwd(y, gain, dn, w_down, g, u, *, name, after=None):
    M, K = y.shape
    tm = min(512, M)
    order = [] if after is None else [after]

    def body(y_ref, gn_ref, dn_ref, w_ref, g_ref, u_ref, *refs):
        dy_ref, o_ref, dg_ref = refs[len(order):]
        groups = [slice(a * tm // ROW_GROUPS, (a + 1) * tm // ROW_GROUPS) for a in range(ROW_GROUPS)]
        dys, dgain = [], 0.0
        for rows in groups:
            dy, part = _rms_bwd_math(y_ref[rows, :].astype(f32), gn_ref[...], dn_ref[rows, :])
            dys.append(dy.astype(bf16))
            dy_ref[rows, :] = dys[-1]
            dgain = dgain + part
        _accumulate(dg_ref, dgain)
        das = [lax.dot_general(dy, w_ref[...], NT, preferred_element_type=f32).astype(bf16) for dy in dys]
        for rows, da in zip(groups, das):
            g = g_ref[rows, :]
            s = 1.0 / (1.0 + jnp.exp(-g))
            o_ref[rows, :D_FF] = da * u_ref[rows, :] * s * (1.0 + g * (1.0 - s))
            o_ref[rows, D_FF:] = da * g * s

    row = pl.BlockSpec((tm, K), lambda i: (i, 0))
    vec = pl.BlockSpec((1, K), lambda i: (0, 0))
    wide = pl.BlockSpec((tm, D_FF), lambda i: (i, 0))
    return pl.pallas_call(
        body, name=name, grid=(M // tm,),
        in_specs=[row, vec, row, pl.BlockSpec((D_FF, K), lambda i: (0, 0)), wide, wide]
        + [pl.BlockSpec(memory_space=pl.ANY)] * len(order),
        out_specs=[row, pl.BlockSpec((tm, 2 * D_FF), lambda i: (i, 0)), vec],
        out_shape=[_sds((M, K), bf16), _sds((M, 2 * D_FF), bf16), _sds((1, K), f32)],
        compiler_params=_params(("arbitrary",)),
    )(y, gain, dn, w_down, g, u, *order)


def rms_bwd_matmul(y, gain, dn, w, dims, *, name, after=None):
    M, K = y.shape
    N = w.shape[0] if dims is NT else w.shape[1]
    tm = min(1024, M)
    order = [] if after is None else [after]

    def body(y_ref, gn_ref, dn_ref, w_ref, *refs):
        dy_ref, o_ref, dg_ref = refs[len(order):]
        groups = [slice(g * tm // ROW_GROUPS, (g + 1) * tm // ROW_GROUPS) for g in range(ROW_GROUPS)]
        dys, dgain = [], 0.0
        for rows in groups:
            dy, part = _rms_bwd_math(y_ref[rows, :].astype(f32), gn_ref[...], dn_ref[rows, :].astype(f32))
            dys.append(dy.astype(bf16))
            dy_ref[rows, :] = dys[-1]
            dgain = dgain + part
        _accumulate(dg_ref, dgain)
        for rows, dy in zip(groups, dys):
            o_ref[rows, :] = lax.dot_general(dy, w_ref[...], dims, preferred_element_type=f32).astype(bf16)

    row = pl.BlockSpec((tm, K), lambda i: (i, 0))
    vec = pl.BlockSpec((1, K), lambda i: (0, 0))
    return pl.pallas_call(
        body, name=name, grid=(M // tm,),
        in_specs=[row, vec, row, pl.BlockSpec(w.shape, lambda i: (0, 0))]
        + [pl.BlockSpec(memory_space=pl.ANY)] * len(order),
        out_specs=[row, pl.BlockSpec((tm, N), lambda i: (i, 0)), vec],
        out_shape=[_sds((M, K), bf16), _sds((M, N), bf16), _sds((1, K), f32)],
        compiler_params=_params(("arbitrary",)),
    )(y, gain, dn, w, *order)


def matmul_rms_bwd(a, b, dims, y, gain, res, *, name, after=None):
    M, K = a.shape
    N = y.shape[1]
    tm = min(512, M)
    order = [] if after is None else [after]

    def body(a_ref, b_ref, y_ref, gn_ref, r_ref, *refs):
        dx_ref, dg_ref = refs[len(order):]
        groups = [slice(g * tm // ROW_GROUPS, (g + 1) * tm // ROW_GROUPS) for g in range(ROW_GROUPS)]
        dns = [lax.dot_general(a_ref[rows, :].astype(bf16), b_ref[...], dims, preferred_element_type=f32)
               for rows in groups]
        dgain = 0.0
        for rows, dn in zip(groups, dns):
            dy, part = _rms_bwd_math(y_ref[rows, :], gn_ref[...], dn)
            dx_ref[rows, :] = dy + r_ref[rows, :]
            dgain = dgain + part
        _accumulate(dg_ref, dgain)

    row = pl.BlockSpec((tm, N), lambda i: (i, 0))
    vec = pl.BlockSpec((1, N), lambda i: (0, 0))
    return pl.pallas_call(
        body, name=name, grid=(M // tm,),
        in_specs=[pl.BlockSpec((tm, K), lambda i: (i, 0)), pl.BlockSpec(b.shape, lambda i: (0, 0)), row, vec, row]
        + [pl.BlockSpec(memory_space=pl.ANY)] * len(order),
        out_specs=[row, vec],
        out_shape=[_sds((M, N), f32), _sds((1, N), f32)],
        compiler_params=_params(("arbitrary",)),
    )(a, b, y, gain, res, *order)


def rms_bwd(y, gain, dn, res, *, name, out_dtype, after=None):
    M, N = y.shape
    tm = min(512, M)
    has_res = res is not None
    order = [] if after is None else [after]

    def body(*refs):
        y_ref, g_ref, dn_ref = refs[:3]
        r_ref = refs[3] if has_res else None
        dy_ref, dg_ref = refs[-2:]
        dy, dgain = _rms_bwd_math(y_ref[...].astype(f32), g_ref[...], dn_ref[...].astype(f32))
        if has_res:
            dy = dy + r_ref[...]
        dy_ref[...] = dy.astype(dy_ref.dtype)
        _accumulate(dg_ref, dgain)

    row = pl.BlockSpec((tm, N), lambda i: (i, 0))
    vec = pl.BlockSpec((1, N), lambda i: (0, 0))
    args = [y, gain, dn] + ([res] if has_res else []) + order
    return pl.pallas_call(
        body, name=name, grid=(M // tm,),
        in_specs=[row, vec, row] + ([row] if has_res else []) + [pl.BlockSpec(memory_space=pl.ANY)] * len(order),
        out_specs=[row, vec],
        out_shape=[_sds((M, N), out_dtype), _sds((1, N), f32)],
        compiler_params=_params(("arbitrary",)),
    )(*args)


def _pool_select(a1, a2, a3, a4):
    col = lax.broadcasted_iota(jnp.int32, (1, MAIN_W), 1) // POOL_GROUP
    return jnp.where(col == 0, a1, jnp.where(col == 1, a2, jnp.where(col == 2, a3, a4)))


def _pool_count(t):
    col = lax.broadcasted_iota(jnp.int32, (1, MAIN_W), 1) // POOL_GROUP
    win = jnp.where(col == 0, 2, jnp.where(col == 1, 4, jnp.where(col == 2, 8, 16)))
    return jnp.minimum(t + 1, win).astype(f32)


def pool_fwd(z, wbd, scale, *, name):
    M = z.shape[0]
    tm = 512
    nper = SEQ // tm
    hb = tm // POOL_HALO

    def body(zc_ref, zh_ref, w_ref, s_ref, p_ref, y_ref):
        i = pl.program_id(0)
        seq_blk = i % nper
        halo = jnp.where(seq_blk == 0, 0.0, zh_ref[...].astype(f32))
        u = zc_ref[...].astype(f32)
        ext = jnp.concatenate([halo, u], axis=0)
        a1 = ext + pltpu.roll(ext, 1, 0)
        a2 = a1 + pltpu.roll(a1, 2, 0)
        a3 = a2 + pltpu.roll(a2, 4, 0)
        a4 = a3 + pltpu.roll(a3, 8, 0)
        sums = _pool_select(a1, a2, a3, a4)[POOL_HALO:]
        t = seq_blk * tm + lax.broadcasted_iota(jnp.int32, (tm, 1), 0)
        p = (sums / _pool_count(t) - u).astype(bf16)
        p_ref[...] = p
        y_ref[...] = (jnp.dot(p, w_ref[...], preferred_element_type=f32) * s_ref[...]).astype(bf16)

    return pl.pallas_call(
        body, name=name, grid=(M // tm,),
        in_specs=[pl.BlockSpec((tm, MAIN_W), lambda i: (i, 0)),
                  pl.BlockSpec((POOL_HALO, MAIN_W), lambda i: (jnp.maximum(i * hb - 1, 0), 0)),
                  pl.BlockSpec((MAIN_W, MAIN_W), lambda i: (0, 0)),
                  pl.BlockSpec((1, MAIN_W), lambda i: (0, 0))],
        out_specs=[pl.BlockSpec((tm, MAIN_W), lambda i: (i, 0)),
                   pl.BlockSpec((tm, MAIN_W), lambda i: (i, 0))],
        out_shape=[_sds((M, MAIN_W), bf16), _sds((M, D_MODEL), bf16)],
        compiler_params=_params(("parallel",)),
    )(z, z, wbd, scale)


def pool_bwd(dyc, p, wbd, scale, *, name):
    M = p.shape[0]
    tm = 512
    nper = SEQ // tm
    hb = tm // POOL_HALO
    last_hb = M // POOL_HALO - 1

    def body(dy_ref, dyh_ref, p_ref, w_ref, s_ref, dz_ref, dw_ref, ds_ref):
        i = pl.program_id(0)
        seq_blk = i % nper
        dy = dy_ref[...].astype(f32)
        pv = p_ref[...]
        w = w_ref[...]
        sc = s_ref[...]

        @pl.when(i == 0)
        def _():
            dw_ref[...] = jnp.zeros_like(dw_ref)
            ds_ref[...] = jnp.zeros_like(ds_ref)

        v = jnp.dot(pv, w, preferred_element_type=f32)
        ds_ref[...] += jnp.sum(dy * v, axis=0, keepdims=True)
        dv = (dy * sc).astype(bf16)
        dw_ref[...] += lax.dot_general(pv, dv, TN, preferred_element_type=f32)
        dp = lax.dot_general(dv, w, NT, preferred_element_type=f32)
        dvh = jnp.where(seq_blk == nper - 1, 0.0, dyh_ref[...].astype(f32) * sc).astype(bf16)
        dph = lax.dot_general(dvh, w, NT, preferred_element_type=f32)
        ext = jnp.concatenate([dp, dph], axis=0)
        n = tm + POOL_HALO
        t = seq_blk * tm + lax.broadcasted_iota(jnp.int32, (n, 1), 0)
        e = ext / _pool_count(t)
        b1 = e + pltpu.roll(e, n - 1, 0)
        b2 = b1 + pltpu.roll(b1, n - 2, 0)
        b3 = b2 + pltpu.roll(b2, n - 4, 0)
        b4 = b3 + pltpu.roll(b3, n - 8, 0)
        dz_ref[...] = (_pool_select(b1, b2, b3, b4)[:tm] - dp).astype(dz_ref.dtype)

    return pl.pallas_call(
        body, name=name, grid=(M // tm,),
        in_specs=[pl.BlockSpec((tm, MAIN_W), lambda i: (i, 0)),
                  pl.BlockSpec((POOL_HALO, MAIN_W), lambda i: (jnp.minimum((i + 1) * hb, last_hb), 0)),
                  pl.BlockSpec((tm, MAIN_W), lambda i: (i, 0)),
                  pl.BlockSpec((MAIN_W, MAIN_W), lambda i: (0, 0)),
                  pl.BlockSpec((1, MAIN_W), lambda i: (0, 0))],
        out_specs=[pl.BlockSpec((tm, MAIN_W), lambda i: (i, 0)),
                   pl.BlockSpec((MAIN_W, MAIN_W), lambda i: (0, 0)),
                   pl.BlockSpec((1, MAIN_W), lambda i: (0, 0))],
        out_shape=[_sds((M, D_MODEL), bf16), _sds((MAIN_W, MAIN_W), f32), _sds((1, MAIN_W), f32)],
        compiler_params=_params(("arbitrary",)),
    )(dyc, dyc, p, wbd, scale)


def _mem_heads(q, kv):
    first = _first_head()
    for pr in range(N_MEM_HEADS // 2):
        cols = slice(pr * PAIR_W, (pr + 1) * PAIR_W)
        qp = q[:, cols] * SCALE
        kp = kv[:, cols]
        vp = kv[:, MEM_W + pr * PAIR_W: MEM_W + (pr + 1) * PAIR_W]
        for hh in range(2):
            lm = first if hh == 0 else ~first
            qm = jnp.where(lm, qp, 0.0).astype(bf16)
            s = lax.dot_general(qm, kp, NT, preferred_element_type=f32)
            e = jnp.exp(s - jnp.max(s, axis=-1, keepdims=True))
            yield lm, qm, kp, vp, e, jnp.sum(e, axis=-1, keepdims=True)


def memattn_fwd(z, kvm, ycat, *, name, n_seq):
    M = z.shape[0]
    tq = 1024
    nq = SEQ // tq

    def body(q_ref, kv_ref, _, o_ref):
        first = _first_head()
        outs = []
        for lm, _, _, vp, e, l in _mem_heads(q_ref[...], kv_ref[...]):
            outs.append(jnp.dot(e.astype(bf16), vp, preferred_element_type=f32) * (1.0 / l))
        pairs = [jnp.where(first, outs[2 * pr], outs[2 * pr + 1]) for pr in range(N_MEM_HEADS // 2)]
        o_ref[...] = jnp.concatenate(pairs, axis=1).astype(bf16)

    return pl.pallas_call(
        body, name=name, grid=(n_seq, nq),
        in_specs=[pl.BlockSpec((tq, MEM_W), lambda b, i: (b * nq + i, 3)),
                  pl.BlockSpec((N_MEM, 2 * MEM_W), lambda b, i: (b, 0)),
                  pl.BlockSpec(memory_space=pl.ANY)],
        out_specs=pl.BlockSpec((tq, MEM_W), lambda b, i: (b * nq + i, 3)),
        out_shape=_sds((M, D_MODEL), bf16),
        input_output_aliases={2: 0},
        compiler_params=_params(("parallel", "parallel")),
    )(z, kvm, ycat)


def memattn_bwd(z, kvm, dyc, dz, *, name, n_seq):
    M = z.shape[0]
    tq = 1024
    nq = SEQ // tq

    def body(q_ref, kv_ref, dy_ref, _, dq_ref, dkv_ref):
        first = _first_head()
        dy = dy_ref[...].astype(f32)
        dqs, dks, dvs = [], [], []
        for h, (lm, qm, kp, vp, e, l) in enumerate(_mem_heads(q_ref[...], kv_ref[...])):
            pr = h // 2
            p = e * (1.0 / l)
            dym = jnp.where(lm, dy[:, pr * PAIR_W:(pr + 1) * PAIR_W], 0.0).astype(bf16)
            dp = lax.dot_general(dym, vp, NT, preferred_element_type=f32)
            ds = (p * (dp - jnp.sum(dp * p, axis=-1, keepdims=True))).astype(bf16)
            dqs.append(jnp.dot(ds, kp, preferred_element_type=f32) * SCALE)
            dk = lax.dot_general(ds, qm, TN, preferred_element_type=f32)
            dv = lax.dot_general(p.astype(bf16), dym, TN, preferred_element_type=f32)
            if h % 2 == 0:
                dks.append(dk)
                dvs.append(dv)
            else:
                dks[pr] = dks[pr] + dk
                dvs[pr] = dvs[pr] + dv
        pairs = [jnp.where(first, dqs[2 * pr], dqs[2 * pr + 1]) for pr in range(N_MEM_HEADS // 2)]
        dq_ref[...] = jnp.concatenate(pairs, axis=1).astype(bf16)

        @pl.when(pl.program_id(1) == 0)
        def _():
            dkv_ref[...] = jnp.zeros_like(dkv_ref)

        dkv_ref[...] += jnp.concatenate(dks + dvs, axis=1)

    return pl.pallas_call(
        body, name=name, grid=(n_seq, nq),
        in_specs=[pl.BlockSpec((tq, MEM_W), lambda b, i: (b * nq + i, 3)),
                  pl.BlockSpec((N_MEM, 2 * MEM_W), lambda b, i: (b, 0)),
                  pl.BlockSpec((tq, MEM_W), lambda b, i: (b * nq + i, 3)),
                  pl.BlockSpec(memory_space=pl.ANY)],
        out_specs=[pl.BlockSpec((tq, MEM_W), lambda b, i: (b * nq + i, 3)),
                   pl.BlockSpec((N_MEM, 2 * MEM_W), lambda b, i: (b, 0))],
        out_shape=[_sds((M, D_MODEL), bf16), _sds((n_seq * N_MEM, 2 * MEM_W), f32)],
        input_output_aliases={3: 0},
        compiler_params=_params(("parallel", "arbitrary")),
    )(z, kvm, dyc, dz)


def rope_tables(pos, *, name):
    M = pos.shape[0]
    tm = min(1024, M)
    half = HEAD_DIM // 2
    inv = ROPE_THETA ** (-np.arange(half, dtype=np.float64) / half)
    inv128 = jnp.asarray(np.tile(inv, 4)[None, :], f32)
    sign128 = jnp.asarray(np.tile(np.concatenate([-np.ones(half), np.ones(half)]), 2)[None, :], f32)

    def body(p_ref, f_ref, s_ref, cos_ref, sin_ref):
        ang = p_ref[...] * f_ref[...]
        cos_ref[...] = jnp.cos(ang)
        sin_ref[...] = jnp.sin(ang) * s_ref[...]

    return pl.pallas_call(
        body, name=name, grid=(M // tm,),
        in_specs=[pl.BlockSpec((tm, 1), lambda i: (i, 0)),
                  pl.BlockSpec((1, 128), lambda i: (0, 0)),
                  pl.BlockSpec((1, 128), lambda i: (0, 0))],
        out_specs=[pl.BlockSpec((tm, 128), lambda i: (i, 0)),
                   pl.BlockSpec((tm, 128), lambda i: (i, 0))],
        out_shape=[_sds((M, 128), f32), _sds((M, 128), f32)],
        compiler_params=_params(("parallel",)),
    )(pos, inv128, sign128)


def _swap_halves(x):
    w = x.shape[1]
    first = (lax.broadcasted_iota(jnp.int32, (1, w), 1) % HEAD_DIM) < (HEAD_DIM // 2)
    return jnp.where(first, pltpu.roll(x, w - HEAD_DIM // 2, 1), pltpu.roll(x, HEAD_DIM // 2, 1))


def group_sum(groups, cos, sin, *, name, rotate, width, col_block=0, into=None):
    M = groups[0][0].shape[0]
    tm = min(512, M)
    counts = [len(g) for g in groups]
    flat = [a for g in groups for a in g]
    extra = [] if into is None else [into]

    def body(*refs):
        part_refs = refs[:len(flat)]
        c_ref, s_ref = refs[len(flat):len(flat) + 2]
        o_ref = refs[-1]
        cols, k = [], 0
        for n in counts:
            acc = part_refs[k][...]
            for r in part_refs[k + 1:k + n]:
                acc = acc + r[...]
            cols.append(acc)
            k += n
        d = jnp.concatenate(cols, axis=1)
        if rotate:
            c = jnp.tile(c_ref[...], (1, MAIN_W // 128))
            s = jnp.tile(s_ref[...], (1, MAIN_W // 128))
            d = d * c - _swap_halves(d) * s
        o_ref[...] = d.astype(bf16)

    part = pl.BlockSpec((tm, GROUP_W), lambda i: (i, 0))
    tab = pl.BlockSpec((tm, 128), lambda i: (i, 0))
    return pl.pallas_call(
        body, name=name, grid=(M // tm,),
        in_specs=[part] * len(flat) + [tab, tab] + [pl.BlockSpec(memory_space=pl.ANY)] * len(extra),
        out_specs=pl.BlockSpec((tm, MAIN_W), lambda i: (i, col_block)),
        out_shape=_sds((M, width), bf16),
        input_output_aliases={len(flat) + 2: 0} if extra else {},
        compiler_params=_params(("parallel",)),
    )(*flat, cos, sin, *extra)


PAIR_W = 2 * HEAD_DIM
MIN_BLOCKS = 8


FWD_TOGETHER = 4
BWD_TOGETHER = 4


def _dil_geometry(dil):
    nsub = max(dil, MIN_BLOCKS)
    tb = BAND * nsub
    return nsub, tb, SEQ // tb


REGROUP = 4


class _Regrouped:
    def __init__(self, ref):
        self.ref = ref
        self.shape = ref.shape

    def fill(self, src):
        q = self.shape[0] // REGROUP
        for r0 in range(REGROUP):
            self.ref[r0 * q:(r0 + 1) * q, :] = src[pl.ds(r0, q, stride=REGROUP), :]

    def drain(self, dst):
        q = self.shape[0] // REGROUP
        for r0 in range(REGROUP):
            dst[pl.ds(r0, q, stride=REGROUP), :] = self.ref[r0 * q:(r0 + 1) * q, :]

    def rows(self, sub, dil):
        nl, r = divmod(sub, dil)
        start = (r % REGROUP) * (self.shape[0] // REGROUP) + r // REGROUP + nl * BAND * (dil // REGROUP)
        return pl.ds(start, BAND, stride=dil // REGROUP)


def _regroups(dil):
    return dil % (4 * REGROUP) == 0


def _rows(ref, sub, dil):
    if isinstance(ref, _Regrouped):
        return ref.ref[ref.rows(sub, dil), :]
    if dil == 1:
        return ref[sub * BAND:(sub + 1) * BAND, :]
    nl, r = divmod(sub, dil)
    return ref[pl.ds(nl * BAND * dil + r, BAND, stride=dil), :]


def _store_rows(ref, sub, dil, val):
    if isinstance(ref, _Regrouped):
        ref.ref[ref.rows(sub, dil), :] = val
    elif dil == 1:
        ref[sub * BAND:(sub + 1) * BAND, :] = val
    else:
        nl, r = divmod(sub, dil)
        ref[pl.ds(nl * BAND * dil + r, BAND, stride=dil), :] = val


def _keys(prev_ref, own_ref, sub, dil):
    nsub = own_ref.shape[0] // BAND
    if sub >= dil:
        prev = _rows(own_ref, sub - dil, dil)
    elif prev_ref is None:
        return _rows(own_ref, sub, dil)
    else:
        prev = _rows(prev_ref, nsub - dil + sub, dil)
    return jnp.concatenate([prev, _rows(own_ref, sub, dil)], axis=0)


def _band_mask(nkeys, has_prev):
    i = lax.broadcasted_iota(jnp.int32, (BAND, nkeys), 0)
    j = lax.broadcasted_iota(jnp.int32, (BAND, nkeys), 1)
    if nkeys == BAND:
        return j <= i
    return (j >= i) & (j <= i + BAND) & (has_prev | (j >= BAND))


def _first_head():
    return lax.broadcasted_iota(jnp.int32, (1, PAIR_W), 1) < HEAD_DIM


def _col(x, hh):
    return x[:, hh * HEAD_DIM:hh * HEAD_DIM + 1]


def _pair_spec(tb, nblk, col0, which):
    def idx(b, p, i):
        if which < 0:
            i = jnp.maximum(i - 1, 0)
        elif which > 0:
            i = jnp.minimum(i + 1, nblk - 1)
        return (b * nblk + i, col0 + p)
    return pl.BlockSpec((tb, PAIR_W), idx)


def dil_fwd(q, k, kv, g, dil, *, name, n_seq):
    M = q.shape[0]
    nsub, tb, nblk = _dil_geometry(dil)
    with_prev = nblk > 1

    regroup = _regroups(dil)
    assert not (regroup and with_prev)

    def body(*refs):
        if with_prev:
            q_ref, ko_ref, vo_ref, kp_ref, vp_ref, o_ref, l_ref = refs
        else:
            (q_ref, ko_ref, vo_ref, o_ref, l_ref), kp_ref, vp_ref = refs[:5], None, None
        outs_to = ()
        if regroup:
            copies = [_Regrouped(s) for s in refs[5:]]
            for c, src in zip(copies, (q_ref, ko_ref, vo_ref)):
                c.fill(src)
            outs_to = ((copies[3], o_ref), (copies[4], l_ref))
            q_ref, ko_ref, vo_ref, o_ref, l_ref = copies
        first = _first_head()
        blk = pl.program_id(2)
        for sub0 in range(0, nsub, FWD_TOGETHER):
            subs = range(sub0, sub0 + FWD_TOGETHER)
            scores, values = [], []
            for sub in subs:
                qs = _rows(q_ref, sub, dil) * SCALE
                kc = _keys(kp_ref, ko_ref, sub, dil).astype(bf16)
                values.append(_keys(vp_ref, vo_ref, sub, dil).astype(bf16))
                has_prev = True if sub >= dil else blk > 0
                mask = _band_mask(kc.shape[0], has_prev)
                for hh in range(2):
                    qm = jnp.where(first if hh == 0 else ~first, qs, 0.0).astype(bf16)
                    scores.append(jnp.where(mask, lax.dot_general(qm, kc, NT, preferred_element_type=f32), NEG))
            soft = []
            for s in scores:
                m = jnp.max(s, axis=-1, keepdims=True)
                e = jnp.exp(s - m)
                l = jnp.sum(e, axis=-1, keepdims=True)
                soft.append((e.astype(bf16), 1.0 / l, jnp.broadcast_to(m + jnp.log(l), (BAND, PAIR_W))))
            outs = [jnp.dot(e, values[n // 2], preferred_element_type=f32) * inv for n, (e, inv, _) in enumerate(soft)]
            for n, sub in enumerate(subs):
                _store_rows(o_ref, sub, dil, jnp.where(first, outs[2 * n], outs[2 * n + 1]))
                _store_rows(l_ref, sub, dil, jnp.where(first, soft[2 * n][2], soft[2 * n + 1][2]))
        for c, dst in outs_to:
            c.drain(dst)

    ins = [(q, 2 * g, 0), (k, 2 * g, 0), (kv, 6 + 2 * g, 0)]
    if with_prev:
        ins += [(k, 2 * g, -1), (kv, 6 + 2 * g, -1)]
    out = _pair_spec(tb, nblk, 0, 0)
    return pl.pallas_call(
        body, name=name, grid=(n_seq, 2, nblk),
        in_specs=[_pair_spec(tb, nblk, c, w) for _, c, w in ins],
        out_specs=[out, out],
        out_shape=[_sds((M, GROUP_W), f32)] * 2,
        scratch_shapes=[pltpu.VMEM((tb, PAIR_W), f32)] * (5 if regroup else 0),
        compiler_params=_params(("parallel", "parallel", "arbitrary")),
    )(*[a for a, _, _ in ins])


def combine_fwd(os_, lses, *, name):
    M = os_[0].shape[0]
    tm = min(512, M)

    def body(o0, o1, o2, l0, l1, l2, y_ref):
        ls = [l0[...], l1[...], l2[...]]
        m = jnp.maximum(jnp.maximum(ls[0], ls[1]), ls[2])
        es = [jnp.exp(l - m) for l in ls]
        inv = 1.0 / (es[0] + es[1] + es[2])
        y_ref[...] = jnp.concatenate([o[...] * e * inv for o, e in zip((o0, o1, o2), es)], axis=1).astype(bf16)

    part = pl.BlockSpec((tm, GROUP_W), lambda i: (i, 0))
    return pl.pallas_call(
        body, name=name, grid=(M // tm,),
        in_specs=[part] * 6,
        out_specs=pl.BlockSpec((tm, MAIN_W), lambda i: (i, 0)),
        out_shape=_sds((M, D_MODEL), bf16),
        compiler_params=_params(("parallel",)),
    )(*os_, *lses)


def combine_bwd(dyc, os_, lses, *, name):
    M = os_[0].shape[0]
    tm = min(512, M)

    def body(dy_ref, o0, o1, o2, l0, l1, l2, d0, d1, d2, c0, c1, c2):
        r = lax.broadcasted_iota(jnp.int32, (GROUP_W, GROUP_W), 0) // HEAD_DIM
        c = lax.broadcasted_iota(jnp.int32, (GROUP_W, GROUP_W), 1) // HEAD_DIM
        ones = (r == c).astype(bf16)
        dy = dy_ref[...].astype(f32)
        ls = [l0[...], l1[...], l2[...]]
        m = jnp.maximum(jnp.maximum(ls[0], ls[1]), ls[2])
        es = [jnp.exp(l - m) for l in ls]
        inv = 1.0 / (es[0] + es[1] + es[2])
        total = 0.0
        alphas = []
        for g, (o, e, d_ref) in enumerate(zip((o0, o1, o2), es, (d0, d1, d2))):
            a = e * inv
            dyg = dy[:, g * GROUP_W:(g + 1) * GROUP_W]
            d_ref[...] = dyg * a
            prod = dyg * o[...]
            hi = prod.astype(bf16)
            lo = (prod - hi.astype(f32)).astype(bf16)
            dsum = jnp.dot(hi, ones, preferred_element_type=f32) + jnp.dot(lo, ones, preferred_element_type=f32)
            total = total + a * dsum
            alphas.append(a)
        for a, c_ref in zip(alphas, (c0, c1, c2)):
            c_ref[...] = -a * total

    part = pl.BlockSpec((tm, GROUP_W), lambda i: (i, 0))
    outs = pl.pallas_call(
        body, name=name, grid=(M // tm,),
        in_specs=[pl.BlockSpec((tm, MAIN_W), lambda i: (i, 0))] + [part] * 6,
        out_specs=[part] * 6,
        out_shape=[_sds((M, GROUP_W), f32)] * 6,
        compiler_params=_params(("parallel",)),
    )(dyc, *os_, *lses)
    return outs[:3], outs[3:]


def dil_bwd(q, k, kv, do, cc, lse, cos, sin, g, dil, *, name, n_seq, into=None):
    M = q.shape[0]
    nsub = SEQ // BAND
    per_res = nsub // dil
    extra = [] if into is None else [into]

    regroup = _regroups(dil)

    def body(q_ref, k_ref, v_ref, do_ref, c_ref, l_ref, cos_ref, sin_ref, *refs):
        dz_ref, dk_ref, dv_ref, dq_ref, *scratch = refs[len(extra):]
        dq_rows = dq_ref
        outs_to = ()
        if regroup:
            copies = [_Regrouped(s) for s in scratch]
            for c, src in zip(copies, (q_ref, k_ref, v_ref, do_ref, c_ref, l_ref)):
                c.fill(src)
            outs_to = tuple(zip(copies[6:], (dq_rows, dk_ref, dv_ref)))
            q_ref, k_ref, v_ref, do_ref, c_ref, l_ref, dq_ref, dk_ref, dv_ref = copies
        first = _first_head()
        order = [nl * dil + r for r in range(dil) for nl in range(per_res)]
        carry = None
        for at in range(0, nsub, BWD_TOGETHER):
            subs = order[at:at + BWD_TOGETHER]
            loaded, products = [], []
            for sub in subs:
                qs = _rows(q_ref, sub, dil) * SCALE
                dos = _rows(do_ref, sub, dil)
                kc = _keys(None, k_ref, sub, dil).astype(bf16)
                vc = _keys(None, v_ref, sub, dil).astype(bf16)
                mask = _band_mask(kc.shape[0], True)
                for hh in range(2):
                    lm = first if hh == 0 else ~first
                    qm = jnp.where(lm, qs, 0.0).astype(bf16)
                    dom = jnp.where(lm, dos, 0.0).astype(bf16)
                    loaded.append((qm, dom, kc))
                    products.append((jnp.where(mask, lax.dot_general(qm, kc, NT, preferred_element_type=f32), NEG),
                                     lax.dot_general(dom, vc, NT, preferred_element_type=f32)))
            weights = []
            for n, (s, dp) in enumerate(products):
                sub, hh = subs[n // 2], n % 2
                p = jnp.exp(s - _col(_rows(l_ref, sub, dil), hh))
                weights.append((p.astype(bf16), (p * (dp + _col(_rows(c_ref, sub, dil), hh))).astype(bf16)))
            results = []
            for (pb, ds), (qm, dom, kc) in zip(weights, loaded):
                results.append((jnp.dot(ds, kc, preferred_element_type=f32) * SCALE,
                                lax.dot_general(ds, qm, TN, preferred_element_type=f32),
                                lax.dot_general(pb, dom, TN, preferred_element_type=f32)))
            for n, sub in enumerate(subs):
                (dq0, dk0, dv0), (dq1, dk1, dv1) = results[2 * n], results[2 * n + 1]
                _store_rows(dq_ref, sub, dil, jnp.where(first, dq0, dq1))
                dkc, dvc = dk0 + dk1, dv0 + dv1
                if sub >= dil:
                    _store_rows(dk_ref, sub - dil, dil, carry[0] + dkc[:BAND])
                    _store_rows(dv_ref, sub - dil, dil, carry[1] + dvc[:BAND])
                    carry = (dkc[BAND:], dvc[BAND:])
                else:
                    carry = (dkc, dvc)
                if sub + dil >= nsub:
                    _store_rows(dk_ref, sub, dil, carry[0])
                    _store_rows(dv_ref, sub, dil, carry[1])
        for c, dst in outs_to:
            c.drain(dst)
        d = dq_rows[...]
        dz_ref[...] = (d * cos_ref[...] - _swap_halves(d) * sin_ref[...]).astype(bf16)

    def spec(col0):
        return pl.BlockSpec((SEQ, PAIR_W), lambda b, p: (b, col0 + p))

    tab = pl.BlockSpec((SEQ, PAIR_W), lambda b, p: (b, 0))
    out = spec(0)
    return pl.pallas_call(
        body, name=name, grid=(n_seq, 2),
        in_specs=[spec(2 * g), spec(2 * g), spec(6 + 2 * g), spec(0), spec(0), spec(0), tab, tab]
        + [pl.BlockSpec(memory_space=pl.ANY)] * len(extra),
        out_specs=[spec(2 * g), out, out],
        out_shape=[_sds((M, D_MODEL), bf16)] + [_sds((M, GROUP_W), f32)] * 2,
        input_output_aliases={8: 0} if extra else {},
        scratch_shapes=[pltpu.VMEM((SEQ, PAIR_W), f32)] * (10 if regroup else 1),
        compiler_params=_params(("parallel", "parallel")),
    )(q, k, kv, do, cc, lse, cos, sin, *extra)


def _blockdiag(wp):
    out = jnp.zeros((MAIN_W, MAIN_W), wp.dtype)
    for gi in range(len(POOL_WINDOWS)):
        sl = slice(gi * POOL_GROUP, (gi + 1) * POOL_GROUP)
        out = out.at[sl, sl].set(wp[gi])
    return out


def _unblockdiag(w):
    return jnp.stack([w[gi * POOL_GROUP:(gi + 1) * POOL_GROUP, gi * POOL_GROUP:(gi + 1) * POOL_GROUP]
                      for gi in range(len(POOL_WINDOWS))])


def local_step(x, mem, positions, target, P, layer_weights, kv_weight, emit_grads):
    n_seq = x.shape[0]
    M = n_seq * SEQ
    xs = x.reshape(M, D_MODEL)
    mems = mem.reshape(n_seq * N_MEM, D_MODEL)
    pos = positions.reshape(M, 1).astype(f32)
    cos, sin = rope_tables(pos, name="rope_tables")
    gains = P["norm_gains"]

    def gain(l, k):
        return gains[l, k].reshape(1, D_MODEL)

    saved = []
    kvs = None
    for l in range(DEPTH):
        W, started = layer_weights(l, "mix", xs)
        sv = {"x": xs, "W": W}
        z, h1, *qrot = rms_matmul(xs, gain(l, 0), W["w_in"], name=f"l{l}_in", out_dtype=bf16, after=started,
                                  rope=None if l < N_A_LAYERS else (cos, sin))
        kvm, mn = rms_matmul(mems, P["mem_norm"][l].reshape(1, D_MODEL), W["w_mem_kv"],
                             name=f"l{l}_memkv", out_dtype=bf16)
        sv.update(z=z, h1=h1, kvm=kvm, mn=mn)
        if l < N_A_LAYERS:
            wbd = _blockdiag(P["w_pool"][l].astype(bf16))
            psc = P["pool_scale"][l].reshape(1, MAIN_W)
            p, y_main = pool_fwd(z, wbd, psc, name=f"l{l}_pool")
            sv.update(p=p, wbd=wbd, psc=psc)
        else:
            (qrot,) = qrot
            os_, lses = [], []
            for g, (_, dil) in enumerate(DIL_PATTERNS):
                o, lse = dil_fwd(qrot, kvs["krot"], kvs["kv"], g, dil, name=f"l{l}_dil{g}", n_seq=n_seq)
                os_.append(o)
                lses.append(lse)
            y_main = combine_fwd(os_, lses, name=f"l{l}_comb")
            sv.update(qrot=qrot, os=os_, lses=lses)
        ycat = memattn_fwd(z, kvm, y_main, name=f"l{l}_memattn", n_seq=n_seq)
        y, x1 = matmul_rms_res(ycat, W["w_out"], gain(l, 1), xs, name=f"l{l}_out")
        W.update(layer_weights(l, "gu", x1)[0])
        fg, fu, a, h2 = rms_gate_up(x1, gain(l, 2), W["w_gate_up"], name=f"l{l}_gu")
        W_down, passing = layer_weights(l, "down", a)
        W.update(W_down)
        y2, x2, *sq = matmul_rms_res(a, W["w_down"], gain(l, 3), x1, name=f"l{l}_down", after=passing,
                                     target=target.reshape(M, D_MODEL) if l == DEPTH - 1 else None)
        sv.update(ycat=ycat, y=y, x1=x1, fg=fg, fu=fu, h2=h2, a=a, y2=y2)
        saved.append(sv)
        xs = x2
        if l == N_A_LAYERS - 1:
            w_kv = kv_weight(xs)
            kv, hkv, krot = rms_matmul(xs, P["kv_norm"].reshape(1, D_MODEL), w_kv, name="kv_proj", out_dtype=f32,
                                       transposed=True, rope=(cos, sin))
            kvs = {"kv": kv, "hkv": hkv, "krot": krot, "x": xs, "w_kv": w_kv}

    dx, (sq,) = xs, sq

    G = {"mem_norm": [None] * DEPTH, "norm_gains": [[None] * 4 for _ in range(DEPTH)],
         "pool_scale": [None] * N_A_LAYERS}
    dk_parts = [[] for _ in range(N_GROUPS)]
    dv_parts = [[] for _ in range(N_GROUPS)]
    emitted = None

    for l in reversed(range(DEPTH)):
        sv = saved[l]
        W = sv["W"]
        gw = {}
        dy2, dgu, G["norm_gains"][l][3] = down_bwd(sv["y2"], gain(l, 3), dx, W["w_down"], sv["fg"], sv["fu"],
                                                   name=f"l{l}_b_dgu", after=emitted)
        gw["w_down"] = matmul(sv["a"], dy2, TN, name=f"l{l}_b_wd", out_dtype=bf16)
        gw["w_gate_up"] = matmul(dgu, sv["h2"], TN, name=f"l{l}_b_wgu", out_dtype=bf16)
        emitted = emit_grads(l, "ffn", gw)
        dx1, G["norm_gains"][l][2] = matmul_rms_bwd(dgu, W["w_gate_up"], NN, sv["x1"], gain(l, 2), dx,
                                                    name=f"l{l}_b_dh2", after=emitted)
        gw = {}
        dy, dycat, G["norm_gains"][l][1] = rms_bwd_matmul(sv["y"], gain(l, 1), dx1, W["w_out"], NT,
                                                          name=f"l{l}_b_dycat", after=emitted)
        gw["w_out"] = matmul(sv["ycat"], dy, TN, name=f"l{l}_b_wout", out_dtype=bf16)
        if l < N_A_LAYERS:
            dz, dwbd, dps = pool_bwd(dycat, sv["p"], sv["wbd"], sv["psc"], name=f"l{l}_b_pool")
            gw["w_pool"] = _unblockdiag(dwbd).reshape(MAIN_W, POOL_GROUP).astype(bf16)
            G["pool_scale"][l] = dps.reshape(MAIN_W)
        else:
            dos, ccs = combine_bwd(dycat, sv["os"], sv["lses"], name=f"l{l}_b_comb")
            dz = None
            for g, (_, dil) in enumerate(DIL_PATTERNS):
                args = (sv["qrot"], kvs["krot"], kvs["kv"], dos[g], ccs[g], sv["lses"][g], cos, sin, g, dil)
                dz, dk, dv = dil_bwd(*args, name=f"l{l}_b_dil{g}", n_seq=n_seq, into=dz)
                dk_parts[g].append(dk)
                dv_parts[g].append(dv)
        dz, dkvm = memattn_bwd(sv["z"], sv["kvm"], dycat, dz, name=f"l{l}_b_memattn", n_seq=n_seq)
        gw["w_mem_kv"] = matmul(sv["mn"], dkvm, TN, name=f"l{l}_b_wmkv", out_dtype=bf16)
        _, G["mem_norm"][l] = matmul_rms_bwd(dkvm, W["w_mem_kv"], NT, mems, P["mem_norm"][l].reshape(1, D_MODEL),
                                             mems, name=f"l{l}_b_dmn")
        gw["w_in"] = matmul(sv["h1"], dz, TN, name=f"l{l}_b_win", out_dtype=bf16)
        if l != N_A_LAYERS:
            emitted = emit_grads(l, "mix", gw)
        dx, G["norm_gains"][l][0] = matmul_rms_bwd(dz, W["w_in"], NT, sv["x"], gain(l, 0), dx1, name=f"l{l}_b_dh1",
                                                   after=emitted)
        if l == N_A_LAYERS:
            dkv = group_sum(dk_parts, cos, sin, name="b_ropek", rotate=True, width=2 * MAIN_W)
            dkv = group_sum(dv_parts, cos, sin, name="b_sumv", rotate=False, width=2 * MAIN_W, col_block=1, into=dkv)
            gw["w_kv"] = matmul(dkv, kvs["hkv"], TN, name="b_wkv", out_dtype=bf16)
            dx, gkn = matmul_rms_bwd(dkv, kvs["w_kv"], NN, kvs["x"], P["kv_norm"].reshape(1, D_MODEL), dx,
                                     name="b_dhkv")
            G["kv_norm"] = gkn.reshape(D_MODEL)
            emitted = emit_grads(l, "mix", gw)

    small = {"pool_scale": jnp.stack(G["pool_scale"]),
             "mem_norm": jnp.concatenate(G["mem_norm"], axis=0),
             "norm_gains": jnp.stack([jnp.concatenate(r, axis=0) for r in G["norm_gains"]]),
             "kv_norm": G["kv_norm"]}
    return sq[0, 0], dx.reshape(n_seq, SEQ, D_MODEL), small, emitted


def to_bf16_layers(stacks, *, name):
    L, n = stacks[0].shape[0], len(stacks)

    def body(*refs):
        ins, outs = refs[:n], refs[n:]
        for j in range(L):
            @pl.when(pl.program_id(0) == j)
            def _():
                for k in range(n):
                    outs[j * n + k][...] = ins[k][...].astype(bf16)

    outs = pl.pallas_call(
        body, name=name, grid=(L,),
        in_specs=[pl.BlockSpec((None,) + s.shape[1:], lambda l: (l, 0, 0)) for s in stacks],
        out_specs=[pl.BlockSpec(s.shape[1:], lambda l: (0, 0)) for _ in range(L) for s in stacks],
        out_shape=[_sds(s.shape[1:], bf16) for _ in range(L) for s in stacks],
        compiler_params=_params(("arbitrary",)),
    )(*stacks)
    return [outs[j * n:(j + 1) * n] for j in range(L)]


def _peer(k):
    x, y, c = lax.axis_index("x"), lax.axis_index("y"), lax.axis_index("c")
    px = 1 - x if k & 4 else x
    py = 1 - y if k & 2 else y
    pc = 1 - c if k & 1 else c
    return (px, py, pc), 4 * px + 2 * py + pc


def _my_index():
    return 4 * lax.axis_index("x") + 2 * lax.axis_index("y") + lax.axis_index("c")


def _src_for(kinds, in_refs, i, idx):
    return in_refs[i] if kinds[i] == "gather" else in_refs[i].at[idx]


def _local_copies(kinds, in_refs, out_refs, local_sems):
    me = _my_index()
    return [pltpu.make_async_copy(_src_for(kinds, in_refs, i, me), out_refs[i].at[me], local_sems.at[i])
            for i in range(len(kinds))]


def _remote_copies(kinds, in_refs, out_refs, send_sems, recv_sems, *, arriving):
    me = _my_index()
    copies = []
    for k in range(1, N_DEV):
        dev, idx = _peer(k)
        for i in range(len(kinds)):
            j = i * (N_DEV - 1) + k - 1
            copies.append(pltpu.make_async_remote_copy(
                src_ref=_src_for(kinds, in_refs, i, idx), dst_ref=out_refs[i].at[idx if arriving else me],
                send_sem=send_sems.at[j], recv_sem=recv_sems.at[j], device_id=dev, device_id_type=MESH))
    return copies


def _out_shape(a, kind):
    return ((N_DEV,) + a.shape) if kind == "gather" else a.shape


def exchange(items, *, name, after=()):
    n = len(items)
    kinds = [k for _, k in items]
    after = list(after)

    def body(*refs):
        in_refs, out_refs = refs[:n], refs[n + len(after):2 * n + len(after)]
        send_sems, recv_sems, local_sems = refs[-3:]
        local = _local_copies(kinds, in_refs, out_refs, local_sems)
        sends = _remote_copies(kinds, in_refs, out_refs, send_sems, recv_sems, arriving=False)
        for cp in local + sends:
            cp.start()
        for cp in _remote_copies(kinds, in_refs, out_refs, send_sems, recv_sems, arriving=True):
            cp.wait_recv()
        for cp in sends:
            cp.wait_send()
        for cp in local:
            cp.wait()

    any_spec = pl.BlockSpec(memory_space=pl.ANY)
    return pl.pallas_call(
        body, name=name,
        in_specs=[any_spec] * (n + len(after)), out_specs=[any_spec] * n,
        out_shape=[_sds(_out_shape(a, k), a.dtype) for a, k in items],
        scratch_shapes=[pltpu.SemaphoreType.DMA((n * (N_DEV - 1),)), pltpu.SemaphoreType.DMA((n * (N_DEV - 1),)),
                        pltpu.SemaphoreType.DMA((n,))],
    )(*[a for a, _ in items], *after)


_HBM = pl.BlockSpec(memory_space=pltpu.HBM)
_SEM = pl.BlockSpec(memory_space=pltpu.SEMAPHORE)
_EFFECT = pltpu.SideEffectType.DATAFLOW_SIDE_EFFECTING


def exchange_start(items, after, *, name):
    n = len(items)
    kinds = [k for _, k in items]

    def body(*refs):
        in_refs, land_refs = refs[:n], refs[n:2 * n]
        send_sems, recv_sems, local_sems = refs[2 * n + 1:2 * n + 4]
        token = refs[-1]
        for cp in (_local_copies(kinds, in_refs, land_refs, local_sems)
                   + _remote_copies(kinds, in_refs, land_refs, send_sems, recv_sems, arriving=False)):
            cp.start()
        token[...] = jnp.zeros_like(token)

    srcs = [pltpu.with_memory_space_constraint(a, pltpu.HBM) for a, _ in items]
    lands = [pltpu.with_memory_space_constraint(lax.empty(_out_shape(a, k), a.dtype), pltpu.HBM) for a, k in items]
    outs = pl.pallas_call(
        body, name=name,
        out_shape=(pltpu.SemaphoreType.DMA((n * (N_DEV - 1),)), pltpu.SemaphoreType.DMA((n * (N_DEV - 1),)),
                   pltpu.SemaphoreType.DMA((n,)),
                   *[pltpu.HBM(a.shape, a.dtype) for a in srcs], *[pltpu.HBM(a.shape, a.dtype) for a in lands],
                   _sds((8, 128), f32)),
        in_specs=[_HBM] * (2 * n) + [pl.BlockSpec(memory_space=pl.ANY)],
        out_specs=(_SEM, _SEM, _SEM, *[_HBM] * (2 * n), pl.BlockSpec(memory_space=pltpu.VMEM)),
        input_output_aliases={i: 3 + i for i in range(2 * n)},
        compiler_params=pltpu.CompilerParams(has_side_effects=_EFFECT),
    )(*srcs, *lands, after)
    return {"kinds": kinds, "sems": outs[:3], "srcs": outs[3:3 + n], "lands": outs[3 + n:3 + 2 * n], "token": outs[-1]}


def exchange_wait(handle, after, *, name):
    kinds = handle["kinds"]
    n = len(kinds)

    def body(*refs):
        in_refs, land_refs = refs[:n], refs[n:2 * n]
        send_sems, recv_sems, local_sems = refs[2 * n:2 * n + 3]
        for cp in _remote_copies(kinds, in_refs, land_refs, send_sems, recv_sems, arriving=True):
            cp.wait_recv()
        for cp in _remote_copies(kinds, in_refs, land_refs, send_sems, recv_sems, arriving=False):
            cp.wait_send()
        for cp in _local_copies(kinds, in_refs, land_refs, local_sems):
            cp.wait()

    srcs, lands = list(handle["srcs"]), list(handle["lands"])
    after = list(after) if isinstance(after, (list, tuple)) else [after]
    outs = pl.pallas_call(
        body, name=name,
        out_shape=tuple(pltpu.HBM(a.shape, a.dtype) for a in srcs + lands),
        in_specs=[_HBM] * (2 * n) + [_SEM] * 3 + [pl.BlockSpec(memory_space=pl.ANY)] * len(after),
        out_specs=tuple([_HBM] * (2 * n)),
        input_output_aliases={i: i for i in range(2 * n)},
        compiler_params=pltpu.CompilerParams(has_side_effects=_EFFECT),
    )(*srcs, *lands, *handle["sems"], *after)
    return list(outs[n:])


CHIP_MASKS = (2, 4, 6)


def _g2_first(in_refs, land_refs, send_sems, recv_sems, *, masks, arriving):
    me = _my_index()
    copies = []
    for i in range(len(land_refs)):
        for j, k in enumerate(masks):
            dev, idx = _peer(k)
            dst = land_refs[i].at[idx if arriving else me]
            copies.append(pltpu.make_async_remote_copy(
                src_ref=dst if in_refs is None else in_refs[i], dst_ref=dst,
                send_sem=send_sems.at[i * len(masks) + j], recv_sem=recv_sems.at[i * len(masks) + j],
                device_id=dev, device_id_type=MESH))
    return copies


def _g2_forward(land_refs, fwd_send, fwd_recv, *, arriving):
    sibling, _ = _peer(1)
    copies = []
    for i in range(len(land_refs)):
        for j, k in enumerate(CHIP_MASKS):
            _, idx = _peer(k | 1 if arriving else k)
            copies.append(pltpu.make_async_remote_copy(
                src_ref=land_refs[i].at[idx], dst_ref=land_refs[i].at[idx],
                send_sem=fwd_send.at[i * 3 + j], recv_sem=fwd_recv.at[i * 3 + j], device_id=sibling,
                device_id_type=MESH))
    return copies


def gather2_start(arrays, after, *, name):
    n = len(arrays)

    def body(*refs):
        in_refs, land_refs = refs[:n], refs[n:2 * n]
        ici_send, ici_recv, d2d_send, d2d_recv, local_sems = refs[2 * n + 1:2 * n + 6]
        token = refs[-1]
        ici = _g2_first(in_refs, land_refs, ici_send, ici_recv, masks=CHIP_MASKS, arriving=False)
        d2d = _g2_first(in_refs, land_refs, d2d_send, d2d_recv, masks=(1,), arriving=False)
        for cp in _local_copies(["gather"] * n, in_refs, land_refs, local_sems) + ici + d2d:
            cp.start()
        token[...] = jnp.zeros_like(token)

    srcs = [pltpu.with_memory_space_constraint(a, pltpu.HBM) for a in arrays]
    lands = [pltpu.with_memory_space_constraint(lax.empty((N_DEV,) + a.shape, a.dtype), pltpu.HBM) for a in arrays]
    sem = pltpu.SemaphoreType.DMA
    outs = pl.pallas_call(
        body, name=name,
        out_shape=(sem((3 * n,)), sem((3 * n,)), sem((n,)), sem((n,)), sem((n,)),
                   *[pltpu.HBM(a.shape, a.dtype) for a in srcs], *[pltpu.HBM(a.shape, a.dtype) for a in lands],
                   _sds((8, 128), f32)),
        in_specs=[_HBM] * (2 * n) + [pl.BlockSpec(memory_space=pl.ANY)],
        out_specs=(*[_SEM] * 5, *[_HBM] * (2 * n), pl.BlockSpec(memory_space=pltpu.VMEM)),
        input_output_aliases={i: 5 + i for i in range(2 * n)},
        compiler_params=pltpu.CompilerParams(has_side_effects=_EFFECT),
    )(*srcs, *lands, after)
    return {"n": n, "sems": outs[:5], "srcs": outs[5:5 + n], "lands": outs[5 + n:5 + 2 * n], "token": outs[-1]}


def gather2_forward(handle, after, *, name):
    n = handle["n"]

    def body(*refs):
        land_refs = refs[:n]
        ici_recv = refs[n]
        fwd_send, fwd_recv = refs[n + 2:n + 4]
        for cp in _g2_first(None, land_refs, fwd_send, ici_recv, masks=CHIP_MASKS, arriving=True):
            cp.wait_recv()
        for cp in _g2_forward(land_refs, fwd_send, fwd_recv, arriving=False):
            cp.start()
        token = refs[-1]
        token[...] = jnp.zeros_like(token)

    lands = list(handle["lands"])
    sem = pltpu.SemaphoreType.DMA
    outs = pl.pallas_call(
        body, name=name,
        out_shape=(sem((3 * n,)), sem((3 * n,)), *[pltpu.HBM(a.shape, a.dtype) for a in lands], _sds((8, 128), f32)),
        in_specs=[_HBM] * n + [_SEM, pl.BlockSpec(memory_space=pl.ANY)],
        out_specs=(_SEM, _SEM, *[_HBM] * n, pl.BlockSpec(memory_space=pltpu.VMEM)),
        input_output_aliases={i: 2 + i for i in range(n)},
        compiler_params=pltpu.CompilerParams(has_side_effects=_EFFECT),
    )(*lands, handle["sems"][1], after)
    return dict(handle, fwd=outs[:2], lands=outs[2:2 + n], token=outs[-1])


def gather2_wait(handle, after, *, name):
    n = handle["n"]

    def body(*refs):
        in_refs, land_refs = refs[:n], refs[n:2 * n]
        ici_send, d2d_send, d2d_recv, local_sems, fwd_send, fwd_recv = refs[2 * n:2 * n + 6]
        for cp in _g2_first(in_refs, land_refs, d2d_send, d2d_recv, masks=(1,), arriving=True):
            cp.wait_recv()
        for cp in _g2_forward(land_refs, fwd_send, fwd_recv, arriving=True):
            cp.wait_recv()
        for cp in (_g2_first(in_refs, land_refs, ici_send, fwd_recv, masks=CHIP_MASKS, arriving=False)
                   + _g2_first(in_refs, land_refs, d2d_send, d2d_recv, masks=(1,), arriving=False)
                   + _g2_forward(land_refs, fwd_send, fwd_recv, arriving=False)):
            cp.wait_send()
        for cp in _local_copies(["gather"] * n, in_refs, land_refs, local_sems):
            cp.wait()

    srcs, lands = list(handle["srcs"]), list(handle["lands"])
    s = handle["sems"]
    outs = pl.pallas_call(
        body, name=name,
        out_shape=tuple(pltpu.HBM(a.shape, a.dtype) for a in srcs + lands),
        in_specs=[_HBM] * (2 * n) + [_SEM] * 6 + [pl.BlockSpec(memory_space=pl.ANY)],
        out_specs=tuple([_HBM] * (2 * n)),
        input_output_aliases={i: i for i in range(2 * n)},
        compiler_params=pltpu.CompilerParams(has_side_effects=_EFFECT),
    )(*srcs, *lands, s[0], s[2], s[3], s[4], *handle["fwd"], after)
    return list(outs[n:])


def adamw(entries, *, name):
    c1 = 1.0 - ADAM_B1 ** ADAM_STEP
    c2 = 1.0 - ADAM_B2 ** ADAM_STEP
    tiles = [_tile(w.shape[-2], (64, 32, 16, 8)) for _, w, _, _, _, _ in entries]
    steps = [w.shape[-2] // tr for (_, w, _, _, _, _), tr in zip(entries, tiles)]
    n = len(entries)

    def body(*refs):
        i = pl.program_id(0)
        for e in range(n):
            s_ref, w_ref, m_ref, v_ref = refs[4 * e:4 * e + 4]
            g_ref, d_ref, m2_ref, v2_ref = refs[len(refs) - 4 * n + 4 * e:len(refs) - 4 * n + 4 * e + 4]

            @pl.when(i < steps[e])
            def _():
                g = s_ref[0].astype(f32)
                for d in range(1, N_DEV):
                    g = g + s_ref[d].astype(f32)
                m2 = ADAM_B1 * m_ref[...] + (1.0 - ADAM_B1) * g
                v2 = ADAM_B2 * v_ref[...] + (1.0 - ADAM_B2) * (g * g)
                g_ref[...] = g
                m2_ref[...] = m2
                v2_ref[...] = v2
                d_ref[...] = -ADAM_LR * ((m2 / c1) / (jnp.sqrt(v2 / c2) + ADAM_EPS) + ADAM_WD * w_ref[...])

    in_specs, out_specs, out_shape, args, extras, aliases = [], [], [], [], [], {}
    for e, ((slots, w, m, v, layer, into), tr, ns) in enumerate(zip(entries, tiles, steps)):
        C = w.shape[-1]
        row = lambda i, ns=ns: jnp.minimum(i, ns - 1)
        if layer is None:
            blk = pl.BlockSpec((tr, C), lambda i, row=row: (row(i), 0))
        else:
            blk = pl.BlockSpec((None, tr, C), lambda i, row=row, layer=layer: (layer, row(i), 0))
        in_specs += [pl.BlockSpec((N_DEV, tr, C), lambda i, row=row: (0, row(i), 0)), blk, blk, blk]
        args += [slots, w, m, v]
        out_specs += [blk] * 4
        out_shape += [_sds(w.shape, f32)] * 4
        if into is not None:
            for t, a in enumerate(into):
                aliases[4 * n + len(extras)] = 4 * e + t
                extras.append(a)
    outs = pl.pallas_call(
        body, name=name, grid=(max(steps),),
        in_specs=in_specs + [pl.BlockSpec(memory_space=pl.ANY)] * len(extras),
        out_specs=out_specs, out_shape=out_shape, input_output_aliases=aliases,
        compiler_params=_params(("arbitrary",)),
    )(*args, *extras)
    return [outs[4 * e:4 * e + 4] for e in range(n)]


WEIGHTS = ("norm_gains", "mem_norm", "w_in", "w_mem_kv", "w_out", "w_pool", "pool_scale", "kv_norm", "w_kv",
           "w_gate_up", "w_down")
LAYER_MATS = ("w_in", "w_mem_kv", "w_out", "w_gate_up", "w_down")
POOL_SHARD = MAIN_W // N_DEV
KV_SHARD = 2 * MAIN_W // N_DEV
LOOKAHEAD = 2
TWO_LEVEL_LAYERS = (0, 1, 2)


def _pack_small(gains, pscale):
    lead = gains.shape[:-3]
    g = gains.reshape(lead + (16, 128))
    p = jnp.zeros(lead + (8, 128), f32).at[..., :2, :POOL_SHARD].set(pscale)
    return jnp.concatenate([g, p], axis=-2)


def _unpack_small(a):
    return a[:16].reshape(4, 4, 128), a[16:18, :POOL_SHARD]


def _pack_repl(mem_norm, kv_norm):
    return jnp.concatenate([mem_norm, kv_norm.reshape(1, D_MODEL), jnp.zeros((3, D_MODEL), f32)], axis=0)


def _unpack_repl(a):
    return a[:4], a[4]


def kernel(x, mem, positions, norm_gains, mem_norm, w_in, w_mem_kv, w_out, w_pool, pool_scale, kv_norm, w_kv, w_gate_up, w_down, loss_target, m_norm_gains, m_mem_norm, m_w_in, m_w_mem_kv, m_w_out, m_w_pool, m_pool_scale, m_kv_norm, m_w_kv, m_w_gate_up, m_w_down, v_norm_gains, v_mem_norm, v_w_in, v_w_mem_kv, v_w_out, v_w_pool, v_pool_scale, v_kv_norm, v_w_kv, v_w_gate_up, v_w_down):
    w = dict(norm_gains=norm_gains, mem_norm=mem_norm, w_in=w_in, w_mem_kv=w_mem_kv, w_out=w_out, w_pool=w_pool,
             pool_scale=pool_scale, kv_norm=kv_norm, w_kv=w_kv, w_gate_up=w_gate_up, w_down=w_down)
    m = dict(norm_gains=m_norm_gains, mem_norm=m_mem_norm, w_in=m_w_in, w_mem_kv=m_w_mem_kv, w_out=m_w_out,
             w_pool=m_w_pool, pool_scale=m_pool_scale, kv_norm=m_kv_norm, w_kv=m_w_kv, w_gate_up=m_w_gate_up,
             w_down=m_w_down)
    v = dict(norm_gains=v_norm_gains, mem_norm=v_mem_norm, w_in=v_w_in, w_mem_kv=v_w_mem_kv, w_out=v_w_out,
             w_pool=v_w_pool, pool_scale=v_pool_scale, kv_norm=v_kv_norm, w_kv=v_w_kv, w_gate_up=v_w_gate_up,
             w_down=v_w_down)

    def transposed_view(d):
        d = dict(d)
        d["w_gate_up"] = jnp.swapaxes(d["w_gate_up"], 1, 2)
        d["w_kv"] = jnp.swapaxes(d["w_kv"], 0, 1)
        return d

    wv, mv, vv = transposed_view(w), transposed_view(m), transposed_view(v)

    small = _pack_small(norm_gains, pool_scale)
    (gsmall,) = exchange([(small, "gather")], name="gather_small")
    P = {"norm_gains": jnp.moveaxis(gsmall[:, :16].reshape(N_DEV, 4, 4, 128), 0, 2).reshape(4, 4, D_MODEL),
         "pool_scale": jnp.moveaxis(gsmall[:, 16:18, :POOL_SHARD], 0, 1).reshape(2, MAIN_W),
         "mem_norm": mem_norm, "kv_norm": kv_norm, "w_pool": w_pool}

    PARTS = {"mix": ("w_in", "w_mem_kv", "w_out"), "ffn": ("w_gate_up", "w_down"), "gu": ("w_gate_up",),
             "down": ("w_down",), "all": ("w_in", "w_mem_kv", "w_out", "w_gate_up", "w_down")}

    def parts_of(l):
        return (("mix", "gu", "down"), ("mix", "ffn"))[l] if l < 2 else ("all",)

    wb = [dict(zip(LAYER_MATS, mats)) for mats in to_bf16_layers([wv[k] for k in LAYER_MATS], name="weights_bf16")]

    def part_items(l, part):
        items = [(wb[l][k], "gather") for k in PARTS[part]]
        if part == "ffn" and l == N_A_LAYERS - 1:
            items.append((wv["w_kv"].astype(bf16), "gather"))
        return items

    handles = {}

    def start_layer(l, after):
        for part in parts_of(l):
            if l in TWO_LEVEL_LAYERS:
                handles[l, part] = gather2_start([a for a, _ in part_items(l, part)], after,
                                                 name=f"gather_start_{part}_l{l}")
            else:
                handles[l, part] = exchange_start(part_items(l, part), after, name=f"gather_start_{part}_l{l}")
            after = handles[l, part]["token"]
        return after

    token = gsmall
    for l in range(LOOKAHEAD):
        token = start_layer(l, token)
    landed = {}

    passed_early = {}

    def layer_weights(l, part, after):
        if part not in parts_of(l):
            if part == "down" and l + 1 in TWO_LEVEL_LAYERS and parts_of(l + 1) == ("all",):
                passed_early[l + 1] = gather2_forward(handles[l + 1, "all"], after, name=f"gather_forward_all_l{l + 1}")
                return {}, passed_early[l + 1]["token"]
            if part == "down" or (part == "gu" and "all" in parts_of(l)):
                return {}, None
            part = "all" if "all" in parts_of(l) else "ffn"
        if l == 0 and part == "mix":
            after = token
        if l in TWO_LEVEL_LAYERS:
            passed = passed_early.pop(l, None)
            if passed is None:
                passed = gather2_forward(handles[l, part], after, name=f"gather_forward_{part}_l{l}")
            got = gather2_wait(passed, after, name=f"gather_wait_{part}_l{l}")
        else:
            got = exchange_wait(handles[l, part], after, name=f"gather_wait_{part}_l{l}")
        landed[l, part] = got
        started = None
        if part in ("mix", "all") and l + LOOKAHEAD < DEPTH:
            started = start_layer(l + LOOKAHEAD, got[0])
        W = {k: g.reshape(-1, g.shape[-1]) for k, g in zip(PARTS[part], got)}
        return W, started

    def kv_weight(after):
        g = landed[N_A_LAYERS - 1, "ffn"][len(PARTS["ffn"])]
        return g.reshape(2 * MAIN_W, D_MODEL)

    ghandles = {}

    pending = {}

    def gparts_of(l):
        return ("ffn", "mix") if l < 2 else ("all",)

    def emit_grads(l, part, gw):
        if part not in gparts_of(l):
            pending.setdefault(l, {}).update(gw)
            if part == "ffn":
                return None
            gw, part = pending[l], "all"
        items = [(gw[k].reshape((N_DEV, -1) + gw[k].shape[-1:]), "scatter") for k in PARTS[part]]
        if part != "ffn" and l == N_A_LAYERS:
            items.append((gw["w_kv"].reshape(N_DEV, KV_SHARD, D_MODEL), "scatter"))
        if part != "ffn" and l < N_A_LAYERS:
            items.append((gw["w_pool"], "gather"))
        ghandles[l, part] = exchange_start(items, gsmall, name=f"scatter_start_{part}_l{l}")
        return ghandles[l, part]["token"]

    sq, grad_x, GS, emitted = local_step(x, mem, positions, loss_target, P, layer_weights, kv_weight, emit_grads)

    def pool3(a):
        return a.reshape(N_A_LAYERS, MAIN_W, POOL_GROUP)

    out = {}
    after = [emitted]

    def finish_layer(l, after):
        for part in gparts_of(l):
            got = exchange_wait(ghandles[l, part], after, name=f"scatter_wait_{part}_l{l}")
            names = list(PARTS[part])
            entries = [(slots, wv[k], mv[k], vv[k], l, out.get(k)) for k, slots in zip(names, got)]
            if part != "ffn" and l == N_A_LAYERS:
                names.append("w_kv")
                entries.append((got[-1], wv["w_kv"], mv["w_kv"], vv["w_kv"], None, None))
            if part != "ffn" and l < N_A_LAYERS:
                names.append("w_pool")
                entries.append((got[-1], pool3(w_pool), pool3(m_w_pool), pool3(v_w_pool), l, out.get("w_pool")))
            out.update(zip(names, adamw(entries, name=f"adamw_{part}_l{l}")))
            after = [out[k][0] for k in names]
        return after

    for l in reversed(range(1, DEPTH)):
        after = finish_layer(l, after)

    gs = _pack_small(jnp.moveaxis(GS["norm_gains"].reshape(4, 4, N_DEV, 128), 2, 0),
                     jnp.moveaxis(GS["pool_scale"].reshape(2, N_DEV, POOL_SHARD), 1, 0))
    parts_small, parts_repl, parts_sq = exchange(
        [(gs, "scatter"), (_pack_repl(GS["mem_norm"], GS["kv_norm"]), "gather"),
         (jnp.full((8, 128), sq, f32), "gather")],
        name="exchange_small_grads", after=after)
    loss = (0.5 / D_MODEL) * jnp.sum(parts_sq[:, 0, 0])
    finish_layer(0, [parts_small])
    out["w_gate_up"] = [jnp.swapaxes(r, 1, 2) for r in out["w_gate_up"]]
    out["w_kv"] = [jnp.swapaxes(r, 0, 1) for r in out["w_kv"]]
    out["w_pool"] = [r.reshape(w_pool.shape) for r in out["w_pool"]]

    res_small, res_repl = adamw(
        [(parts_small, small, _pack_small(m_norm_gains, m_pool_scale), _pack_small(v_norm_gains, v_pool_scale),
          None, None),
         (parts_repl, _pack_repl(mem_norm, kv_norm), _pack_repl(m_mem_norm, m_kv_norm),
          _pack_repl(v_mem_norm, v_kv_norm), None, None)], name="adamw_small")
    out["norm_gains"], out["pool_scale"] = zip(*[_unpack_small(r) for r in res_small])
    out["mem_norm"], out["kv_norm"] = zip(*[_unpack_repl(r) for r in res_repl])

    return (loss, grad_x, *[out[k][0] for k in WEIGHTS], *[out[k][1] for k in WEIGHTS],
            *[out[k][2] for k in WEIGHTS], *[out[k][3] for k in WEIGHTS])
```

```python
import numpy as np
import jax
import jax.numpy as jnp
from jax import lax
from jax.experimental import pallas as pl
from jax.experimental.pallas import tpu as pltpu

f32 = jnp.float32
bf16 = jnp.bfloat16

D_MODEL = 1024
SEQ = 2048
DEPTH = 4
N_MEM = 256
HEAD_DIM = 64
N_MEM_HEADS = 4
MEM_W = 256
MAIN_W = 768
POOL_WINDOWS = (2, 4, 8, 16)
POOL_GROUP = 192
POOL_HALO = 16
DIL_PATTERNS = ((128, 1), (512, 4), (2048, 16))
N_GROUPS = 3
GROUP_W = 256
BAND = 128
N_A_LAYERS = 2
D_FF = 2816
ROPE_THETA = 10000.0
EPS = 1e-6
NEG = -1e30
SCALE = HEAD_DIM ** -0.5
N_DEV = 8

ADAM_LR = 0.001
ADAM_B1 = 0.9
ADAM_B2 = 0.999
ADAM_EPS = 1e-08
ADAM_WD = 0.01
ADAM_STEP = 10

VMEM_LIMIT_BYTES = 56 * 1024 * 1024
MESH = pl.DeviceIdType.MESH

NN = (((1,), (0,)), ((), ()))
NT = (((1,), (1,)), ((), ()))
TN = (((0,), (0,)), ((), ()))


def _params(sem=None):
    return pltpu.CompilerParams(dimension_semantics=sem, vmem_limit_bytes=VMEM_LIMIT_BYTES)


def _tile(n, cands):
    for c in cands:
        if n % c == 0:
            return c
    return n


def _sds(shape, dtype):
    return jax.ShapeDtypeStruct(tuple(shape), dtype)


def _rms_r(v):
    return lax.rsqrt(jnp.mean(v * v, axis=-1, keepdims=True) + EPS)


ROW_GROUPS = 2


def rms_matmul(x, gain, w, *, name, out_dtype, transposed=False, after=None, rope=None):
    M, K = x.shape
    N = w.shape[0] if transposed else w.shape[1]
    tm = min(512, M)
    order = [] if after is None else [after]
    tables = [] if rope is None else list(rope)
    rot_spec = [] if rope is None else [pl.BlockSpec((tm, MAIN_W), lambda i: (i, 0))]
    rot_shape = [] if rope is None else [_sds((M, MAIN_W), f32)]

    def body(x_ref, g_ref, w_ref, *refs):
        z_ref, h_ref = refs[len(tables) + len(order):][:2]
        groups = [slice(g * tm // ROW_GROUPS, (g + 1) * tm // ROW_GROUPS) for g in range(ROW_GROUPS)]
        hs = []
        for rows in groups:
            xv = x_ref[rows, :]
            hs.append((xv * _rms_r(xv) * g_ref[...]).astype(bf16))
            h_ref[rows, :] = hs[-1]
        zs = [lax.dot_general(h, w_ref[...], NT if transposed else NN, preferred_element_type=f32) for h in hs]
        for rows, z in zip(groups, zs):
            z_ref[rows, :] = z.astype(z_ref.dtype)
            if tables:
                c = jnp.tile(refs[0][rows, :], (1, MAIN_W // 128))
                s = jnp.tile(refs[1][rows, :], (1, MAIN_W // 128))
                zr = z[:, :MAIN_W]
                refs[-1][rows, :] = zr * c + _swap_halves(zr) * s

    tab = pl.BlockSpec((tm, 128), lambda i: (i, 0))
    return pl.pallas_call(
        body, name=name, grid=(M // tm,),
        in_specs=[pl.BlockSpec((tm, K), lambda i: (i, 0)),
                  pl.BlockSpec((1, K), lambda i: (0, 0)),
                  pl.BlockSpec(w.shape, lambda i: (0, 0))] + [tab] * len(tables)
        + [pl.BlockSpec(memory_space=pl.ANY)] * len(order),
        out_specs=[pl.BlockSpec((tm, N), lambda i: (i, 0)), pl.BlockSpec((tm, K), lambda i: (i, 0))] + rot_spec,
        out_shape=[_sds((M, N), out_dtype), _sds((M, K), bf16)] + rot_shape,
        compiler_params=_params(("parallel",)),
    )(x, gain, w, *tables, *order)


def matmul_rms_res(a, w, gain, res, *, name, target=None, after=None):
    M, K = a.shape
    N = w.shape[1]
    tm = min(512, M)
    goal = [] if target is None else [target]
    order = [] if after is None else [after]

    def body(a_ref, w_ref, g_ref, r_ref, *refs):
        y_ref, x_ref = refs[len(goal) + len(order):][:2]
        groups = [slice(g * tm // ROW_GROUPS, (g + 1) * tm // ROW_GROUPS) for g in range(ROW_GROUPS)]
        ys = [jnp.dot(a_ref[rows, :], w_ref[...], preferred_element_type=f32) for rows in groups]
        sq = 0.0
        for rows, y in zip(groups, ys):
            y_ref[rows, :] = y.astype(bf16)
            x = r_ref[rows, :] + y * _rms_r(y) * g_ref[...]
            if goal:
                e = x - refs[0][rows, :]
                x = e * (1.0 / N)
                sq = sq + jnp.sum(jnp.sum(e * e, axis=0, keepdims=True), axis=1, keepdims=True)
            x_ref[rows, :] = x
        if goal:
            _accumulate(refs[-1], sq)

    row = pl.BlockSpec((tm, N), lambda i: (i, 0))
    return pl.pallas_call(
        body, name=name, grid=(M // tm,),
        in_specs=[pl.BlockSpec((tm, K), lambda i: (i, 0)),
                  pl.BlockSpec((K, N), lambda i: (0, 0)),
                  pl.BlockSpec((1, N), lambda i: (0, 0)),
                  row] + [row] * len(goal) + [pl.BlockSpec(memory_space=pl.ANY)] * len(order),
        out_specs=[row, row] + [pl.BlockSpec((8, 128), lambda i: (0, 0))] * len(goal),
        out_shape=[_sds((M, N), bf16), _sds((M, N), f32)] + [_sds((8, 128), f32)] * len(goal),
        compiler_params=_params(("arbitrary",) if goal else ("parallel",)),
    )(a, w, gain, res, *goal, *order)


def matmul(a, b, dims, *, name, out_dtype):
    if dims is TN:
        K, M = a.shape
        tm = _tile(M, (512, 256, 128))
        a_spec = pl.BlockSpec((K, tm), lambda i: (0, i))
    else:
        M, K = a.shape
        tm = _tile(M, (1024, 512, 256, 128))
        a_spec = pl.BlockSpec((tm, K), lambda i: (i, 0))
    N = b.shape[0] if dims is NT else b.shape[1]

    def body(a_ref, b_ref, o_ref):
        o_ref[...] = lax.dot_general(a_ref[...].astype(bf16), b_ref[...].astype(bf16), dims,
                                     preferred_element_type=f32).astype(o_ref.dtype)

    return pl.pallas_call(
        body, name=name, grid=(M // tm,),
        in_specs=[a_spec, pl.BlockSpec(b.shape, lambda i: (0, 0))],
        out_specs=pl.BlockSpec((tm, N), lambda i: (i, 0)),
        out_shape=_sds((M, N), out_dtype),
        compiler_params=_params(("parallel",)),
    )(a, b)


def rms_gate_up(x, gain, wt, *, name):
    M, K = x.shape
    tm = min(2048, M)
    tn = _tile(D_FF, (256, 128))
    nj = D_FF // tn

    def body(x_ref, gn_ref, wg_ref, wu_ref, g_ref, u_ref, a_ref, h_ref):
        @pl.when(pl.program_id(1) == 0)
        def _():
            xv = x_ref[...]
            h_ref[...] = (xv * _rms_r(xv) * gn_ref[...]).astype(bf16)

        h = h_ref[...]
        g = lax.dot_general(h, wg_ref[...], NT, preferred_element_type=f32).astype(bf16)
        u = lax.dot_general(h, wu_ref[...], NT, preferred_element_type=f32).astype(bf16)
        g_ref[...] = g
        u_ref[...] = u
        a_ref[...] = g * (1.0 / (1.0 + jnp.exp(-g))) * u

    col = pl.BlockSpec((tm, tn), lambda i, j: (i, j))
    return pl.pallas_call(
        body, name=name, grid=(M // tm, nj),
        in_specs=[pl.BlockSpec((tm, K), lambda i, j: (i, 0)),
                  pl.BlockSpec((1, K), lambda i, j: (0, 0)),
                  pl.BlockSpec((tn, K), lambda i, j: (j, 0)),
                  pl.BlockSpec((tn, K), lambda i, j: (j + nj, 0))],
        out_specs=[col, col, col, pl.BlockSpec((tm, K), lambda i, j: (i, 0))],
        out_shape=[_sds((M, D_FF), bf16)] * 3 + [_sds((M, K), bf16)],
        compiler_params=_params(("parallel", "arbitrary")),
    )(x, gain, wt, wt)


def _rms_bwd_math(yv, gain, dn):
    r = _rms_r(yv)
    q = dn * gain
    dy = r * q - yv * (r * r * r) * jnp.mean(q * yv, axis=-1, keepdims=True)
    return dy, jnp.sum(dn * yv * r, axis=0, keepdims=True)


def _accumulate(ref, val):
    @pl.when(pl.program_id(0) == 0)
    def _():
        ref[...] = jnp.zeros_like(ref)

    ref[...] += val


def down_bwd(y, gain, dn, w_down, g, u, *, name, after=None):
    M, K = y.shape
    tm = min(512, M)
    order = [] if after is None else [after]

    def body(y_ref, gn_ref, dn_ref, w_ref, g_ref, u_ref, *refs):
        dy_ref, o_ref, dg_ref = refs[len(order):]
        groups = [slice(a * tm // ROW_GROUPS, (a + 1) * tm // ROW_GROUPS) for a in range(ROW_GROUPS)]
        dys, dgain = [], 0.0
        for rows in groups:
            dy, part = _rms_bwd_math(y_ref[rows, :].astype(f32), gn_ref[...], dn_ref[rows, :])
            dys.append(dy.astype(bf16))
            dy_ref[rows, :] = dys[-1]
            dgain = dgain + part
        _accumulate(dg_ref, dgain)
        das = [lax.dot_general(dy, w_ref[...], NT, preferred_element_type=f32).astype(bf16) for dy in dys]
        for rows, da in zip(groups, das):
            g = g_ref[rows, :]
            s = 1.0 / (1.0 + jnp.exp(-g))
            o_ref[rows, :D_FF] = da * u_ref[rows, :] * s * (1.0 + g * (1.0 - s))
            o_ref[rows, D_FF:] = da * g * s

    row = pl.BlockSpec((tm, K), lambda i: (i, 0))
    vec = pl.BlockSpec((1, K), lambda i: (0, 0))
    wide = pl.BlockSpec((tm, D_FF), lambda i: (i, 0))
    return pl.pallas_call(
        body, name=name, grid=(M // tm,),
        in_specs=[row, vec, row, pl.BlockSpec((D_FF, K), lambda i: (0, 0)), wide, wide]
        + [pl.BlockSpec(memory_space=pl.ANY)] * len(order),
        out_specs=[row, pl.BlockSpec((tm, 2 * D_FF), lambda i: (i, 0)), vec],
        out_shape=[_sds((M, K), bf16), _sds((M, 2 * D_FF), bf16), _sds((1, K), f32)],
        compiler_params=_params(("arbitrary",)),
    )(y, gain, dn, w_down, g, u, *order)


def rms_bwd_matmul(y, gain, dn, w, dims, *, name, after=None):
    M, K = y.shape
    N = w.shape[0] if dims is NT else w.shape[1]
    tm = min(1024, M)
    order = [] if after is None else [after]

    def body(y_ref, gn_ref, dn_ref, w_ref, *refs):
        dy_ref, o_ref, dg_ref = refs[len(order):]
        groups = [slice(g * tm // ROW_GROUPS, (g + 1) * tm // ROW_GROUPS) for g in range(ROW_GROUPS)]
        dys, dgain = [], 0.0
        for rows in groups:
            dy, part = _rms_bwd_math(y_ref[rows, :].astype(f32), gn_ref[...], dn_ref[rows, :].astype(f32))
            dys.append(dy.astype(bf16))
            dy_ref[rows, :] = dys[-1]
            dgain = dgain + part
        _accumulate(dg_ref, dgain)
        for rows, dy in zip(groups, dys):
            o_ref[rows, :] = lax.dot_general(dy, w_ref[...], dims, preferred_element_type=f32).astype(bf16)

    row = pl.BlockSpec((tm, K), lambda i: (i, 0))
    vec = pl.BlockSpec((1, K), lambda i: (0, 0))
    return pl.pallas_call(
        body, name=name, grid=(M // tm,),
        in_specs=[row, vec, row, pl.BlockSpec(w.shape, lambda i: (0, 0))]
        + [pl.BlockSpec(memory_space=pl.ANY)] * len(order),
        out_specs=[row, pl.BlockSpec((tm, N), lambda i: (i, 0)), vec],
        out_shape=[_sds((M, K), bf16), _sds((M, N), bf16), _sds((1, K), f32)],
        compiler_params=_params(("arbitrary",)),
    )(y, gain, dn, w, *order)


def matmul_rms_bwd(a, b, dims, y, gain, res, *, name, after=None):
    M, K = a.shape
    N = y.shape[1]
    tm = min(512, M)
    order = [] if after is None else [after]

    def body(a_ref, b_ref, y_ref, gn_ref, r_ref, *refs):
        dx_ref, dg_ref = refs[len(order):]
        groups = [slice(g * tm // ROW_GROUPS, (g + 1) * tm // ROW_GROUPS) for g in range(ROW_GROUPS)]
        dns = [lax.dot_general(a_ref[rows, :].astype(bf16), b_ref[...], dims, preferred_element_type=f32)
               for rows in groups]
        dgain = 0.0
        for rows, dn in zip(groups, dns):
            dy, part = _rms_bwd_math(y_ref[rows, :], gn_ref[...], dn)
            dx_ref[rows, :] = dy + r_ref[rows, :]
            dgain = dgain + part
        _accumulate(dg_ref, dgain)

    row = pl.BlockSpec((tm, N), lambda i: (i, 0))
    vec = pl.BlockSpec((1, N), lambda i: (0, 0))
    return pl.pallas_call(
        body, name=name, grid=(M // tm,),
        in_specs=[pl.BlockSpec((tm, K), lambda i: (i, 0)), pl.BlockSpec(b.shape, lambda i: (0, 0)), row, vec, row]
        + [pl.BlockSpec(memory_space=pl.ANY)] * len(order),
        out_specs=[row, vec],
        out_shape=[_sds((M, N), f32), _sds((1, N), f32)],
        compiler_params=_params(("arbitrary",)),
    )(a, b, y, gain, res, *order)


def rms_bwd(y, gain, dn, res, *, name, out_dtype, after=None):
    M, N = y.shape
    tm = min(512, M)
    has_res = res is not None
    order = [] if after is None else [after]

    def body(*refs):
        y_ref, g_ref, dn_ref = refs[:3]
        r_ref = refs[3] if has_res else None
        dy_ref, dg_ref = refs[-2:]
        dy, dgain = _rms_bwd_math(y_ref[...].astype(f32), g_ref[...], dn_ref[...].astype(f32))
        if has_res:
            dy = dy + r_ref[...]
        dy_ref[...] = dy.astype(dy_ref.dtype)
        _accumulate(dg_ref, dgain)

    row = pl.BlockSpec((tm, N), lambda i: (i, 0))
    vec = pl.BlockSpec((1, N), lambda i: (0, 0))
    args = [y, gain, dn] + ([res] if has_res else []) + order
    return pl.pallas_call(
        body, name=name, grid=(M // tm,),
        in_specs=[row, vec, row] + ([row] if has_res else []) + [pl.BlockSpec(memory_space=pl.ANY)] * len(order),
        out_specs=[row, vec],
        out_shape=[_sds((M, N), out_dtype), _sds((1, N), f32)],
        compiler_params=_params(("arbitrary",)),
    )(*args)


def _pool_select(a1, a2, a3, a4):
    col = lax.broadcasted_iota(jnp.int32, (1, MAIN_W), 1) // POOL_GROUP
    return jnp.where(col == 0, a1, jnp.where(col == 1, a2, jnp.where(col == 2, a3, a4)))


def _pool_count(t):
    col = lax.broadcasted_iota(jnp.int32, (1, MAIN_W), 1) // POOL_GROUP
    win = jnp.where(col == 0, 2, jnp.where(col == 1, 4, jnp.where(col == 2, 8, 16)))
    return jnp.minimum(t + 1, win).astype(f32)


def pool_fwd(z, wbd, scale, *, name):
    M = z.shape[0]
    tm = 512
    nper = SEQ // tm
    hb = tm // POOL_HALO

    def body(zc_ref, zh_ref, w_ref, s_ref, p_ref, y_ref):
        i = pl.program_id(0)
        seq_blk = i % nper
        halo = jnp.where(seq_blk == 0, 0.0, zh_ref[...].astype(f32))
        u = zc_ref[...].astype(f32)
        ext = jnp.concatenate([halo, u], axis=0)
        a1 = ext + pltpu.roll(ext, 1, 0)
        a2 = a1 + pltpu.roll(a1, 2, 0)
        a3 = a2 + pltpu.roll(a2, 4, 0)
        a4 = a3 + pltpu.roll(a3, 8, 0)
        sums = _pool_select(a1, a2, a3, a4)[POOL_HALO:]
        t = seq_blk * tm + lax.broadcasted_iota(jnp.int32, (tm, 1), 0)
        p = (sums / _pool_count(t) - u).astype(bf16)
        p_ref[...] = p
        y_ref[...] = (jnp.dot(p, w_ref[...], preferred_element_type=f32) * s_ref[...]).astype(bf16)

    return pl.pallas_call(
        body, name=name, grid=(M // tm,),
        in_specs=[pl.BlockSpec((tm, MAIN_W), lambda i: (i, 0)),
                  pl.BlockSpec((POOL_HALO, MAIN_W), lambda i: (jnp.maximum(i * hb - 1, 0), 0)),
                  pl.BlockSpec((MAIN_W, MAIN_W), lambda i: (0, 0)),
                  pl.BlockSpec((1, MAIN_W), lambda i: (0, 0))],
        out_specs=[pl.BlockSpec((tm, MAIN_W), lambda i: (i, 0)),
                   pl.BlockSpec((tm, MAIN_W), lambda i: (i, 0))],
        out_shape=[_sds((M, MAIN_W), bf16), _sds((M, D_MODEL), bf16)],
        compiler_params=_params(("parallel",)),
    )(z, z, wbd, scale)


def pool_bwd(dyc, p, wbd, scale, *, name):
    M = p.shape[0]
    tm = 512
    nper = SEQ // tm
    hb = tm // POOL_HALO
    last_hb = M // POOL_HALO - 1

    def body(dy_ref, dyh_ref, p_ref, w_ref, s_ref, dz_ref, dw_ref, ds_ref):
        i = pl.program_id(0)
        seq_blk = i % nper
        dy = dy_ref[...].astype(f32)
        pv = p_ref[...]
        w = w_ref[...]
        sc = s_ref[...]

        @pl.when(i == 0)
        def _():
            dw_ref[...] = jnp.zeros_like(dw_ref)
            ds_ref[...] = jnp.zeros_like(ds_ref)

        v = jnp.dot(pv, w, preferred_element_type=f32)
        ds_ref[...] += jnp.sum(dy * v, axis=0, keepdims=True)
        dv = (dy * sc).astype(bf16)
        dw_ref[...] += lax.dot_general(pv, dv, TN, preferred_element_type=f32)
        dp = lax.dot_general(dv, w, NT, preferred_element_type=f32)
        dvh = jnp.where(seq_blk == nper - 1, 0.0, dyh_ref[...].astype(f32) * sc).astype(bf16)
        dph = lax.dot_general(dvh, w, NT, preferred_element_type=f32)
        ext = jnp.concatenate([dp, dph], axis=0)
        n = tm + POOL_HALO
        t = seq_blk * tm + lax.broadcasted_iota(jnp.int32, (n, 1), 0)
        e = ext / _pool_count(t)
        b1 = e + pltpu.roll(e, n - 1, 0)
        b2 = b1 + pltpu.roll(b1, n - 2, 0)
        b3 = b2 + pltpu.roll(b2, n - 4, 0)
        b4 = b3 + pltpu.roll(b3, n - 8, 0)
        dz_ref[...] = (_pool_select(b1, b2, b3, b4)[:tm] - dp).astype(dz_ref.dtype)

    return pl.pallas_call(
        body, name=name, grid=(M // tm,),
        in_specs=[pl.BlockSpec((tm, MAIN_W), lambda i: (i, 0)),
                  pl.BlockSpec((POOL_HALO, MAIN_W), lambda i: (jnp.minimum((i + 1) * hb, last_hb), 0)),
                  pl.BlockSpec((tm, MAIN_W), lambda i: (i, 0)),
                  pl.BlockSpec((MAIN_W, MAIN_W), lambda i: (0, 0)),
                  pl.BlockSpec((1, MAIN_W), lambda i: (0, 0))],
        out_specs=[pl.BlockSpec((tm, MAIN_W), lambda i: (i, 0)),
                   pl.BlockSpec((MAIN_W, MAIN_W), lambda i: (0, 0)),
                   pl.BlockSpec((1, MAIN_W), lambda i: (0, 0))],
        out_shape=[_sds((M, D_MODEL), bf16), _sds((MAIN_W, MAIN_W), f32), _sds((1, MAIN_W), f32)],
        compiler_params=_params(("arbitrary",)),
    )(dyc, dyc, p, wbd, scale)


def _mem_heads(q, kv):
    first = _first_head()
    for pr in range(N_MEM_HEADS // 2):
        cols = slice(pr * PAIR_W, (pr + 1) * PAIR_W)
        qp = q[:, cols] * SCALE
        kp = kv[:, cols]
        vp = kv[:, MEM_W + pr * PAIR_W: MEM_W + (pr + 1) * PAIR_W]
        for hh in range(2):
            lm = first if hh == 0 else ~first
            qm = jnp.where(lm, qp, 0.0).astype(bf16)
            s = lax.dot_general(qm, kp, NT, preferred_element_type=f32)
            e = jnp.exp(s - jnp.max(s, axis=-1, keepdims=True))
            yield lm, qm, kp, vp, e, jnp.sum(e, axis=-1, keepdims=True)


def memattn_fwd(z, kvm, ycat, *, name, n_seq):
    M = z.shape[0]
    tq = 1024
    nq = SEQ // tq

    def body(q_ref, kv_ref, _, o_ref):
        first = _first_head()
        outs = []
        for lm, _, _, vp, e, l in _mem_heads(q_ref[...], kv_ref[...]):
            outs.append(jnp.dot(e.astype(bf16), vp, preferred_element_type=f32) * (1.0 / l))
        pairs = [jnp.where(first, outs[2 * pr], outs[2 * pr + 1]) for pr in range(N_MEM_HEADS // 2)]
        o_ref[...] = jnp.concatenate(pairs, axis=1).astype(bf16)

    return pl.pallas_call(
        body, name=name, grid=(n_seq, nq),
        in_specs=[pl.BlockSpec((tq, MEM_W), lambda b, i: (b * nq + i, 3)),
                  pl.BlockSpec((N_MEM, 2 * MEM_W), lambda b, i: (b, 0)),
                  pl.BlockSpec(memory_space=pl.ANY)],
        out_specs=pl.BlockSpec((tq, MEM_W), lambda b, i: (b * nq + i, 3)),
        out_shape=_sds((M, D_MODEL), bf16),
        input_output_aliases={2: 0},
        compiler_params=_params(("parallel", "parallel")),
    )(z, kvm, ycat)


def memattn_bwd(z, kvm, dyc, dz, *, name, n_seq):
    M = z.shape[0]
    tq = 1024
    nq = SEQ // tq

    def body(q_ref, kv_ref, dy_ref, _, dq_ref, dkv_ref):
        first = _first_head()
        dy = dy_ref[...].astype(f32)
        dqs, dks, dvs = [], [], []
        for h, (lm, qm, kp, vp, e, l) in enumerate(_mem_heads(q_ref[...], kv_ref[...])):
            pr = h // 2
            p = e * (1.0 / l)
            dym = jnp.where(lm, dy[:, pr * PAIR_W:(pr + 1) * PAIR_W], 0.0).astype(bf16)
            dp = lax.dot_general(dym, vp, NT, preferred_element_type=f32)
            ds = (p * (dp - jnp.sum(dp * p, axis=-1, keepdims=True))).astype(bf16)
            dqs.append(jnp.dot(ds, kp, preferred_element_type=f32) * SCALE)
            dk = lax.dot_general(ds, qm, TN, preferred_element_type=f32)
            dv = lax.dot_general(p.astype(bf16), dym, TN, preferred_element_type=f32)
            if h % 2 == 0:
                dks.append(dk)
                dvs.append(dv)
            else:
                dks[pr] = dks[pr] + dk
                dvs[pr] = dvs[pr] + dv
        pairs = [jnp.where(first, dqs[2 * pr], dqs[2 * pr + 1]) for pr in range(N_MEM_HEADS // 2)]
        dq_ref[...] = jnp.concatenate(pairs, axis=1).astype(bf16)

        @pl.when(pl.program_id(1) == 0)
        def _():
            dkv_ref[...] = jnp.zeros_like(dkv_ref)

        dkv_ref[...] += jnp.concatenate(dks + dvs, axis=1)

    return pl.pallas_call(
        body, name=name, grid=(n_seq, nq),
        in_specs=[pl.BlockSpec((tq, MEM_W), lambda b, i: (b * nq + i, 3)),
                  pl.BlockSpec((N_MEM, 2 * MEM_W), lambda b, i: (b, 0)),
                  pl.BlockSpec((tq, MEM_W), lambda b, i: (b * nq + i, 3)),
                  pl.BlockSpec(memory_space=pl.ANY)],
        out_specs=[pl.BlockSpec((tq, MEM_W), lambda b, i: (b * nq + i, 3)),
                   pl.BlockSpec((N_MEM, 2 * MEM_W), lambda b, i: (b, 0))],
        out_shape=[_sds((M, D_MODEL), bf16), _sds((n_seq * N_MEM, 2 * MEM_W), f32)],
        input_output_aliases={3: 0},
        compiler_params=_params(("parallel", "arbitrary")),
    )(z, kvm, dyc, dz)


def rope_tables(pos, *, name):
    M = pos.shape[0]
    tm = min(1024, M)
    half = HEAD_DIM // 2
    inv = ROPE_THETA ** (-np.arange(half, dtype=np.float64) / half)
    inv128 = jnp.asarray(np.tile(inv, 4)[None, :], f32)
    sign128 = jnp.asarray(np.tile(np.concatenate([-np.ones(half), np.ones(half)]), 2)[None, :], f32)

    def body(p_ref, f_ref, s_ref, cos_ref, sin_ref):
        ang = p_ref[...] * f_ref[...]
        cos_ref[...] = jnp.cos(ang)
        sin_ref[...] = jnp.sin(ang) * s_ref[...]

    return pl.pallas_call(
        body, name=name, grid=(M // tm,),
        in_specs=[pl.BlockSpec((tm, 1), lambda i: (i, 0)),
                  pl.BlockSpec((1, 128), lambda i: (0, 0)),
                  pl.BlockSpec((1, 128), lambda i: (0, 0))],
        out_specs=[pl.BlockSpec((tm, 128), lambda i: (i, 0)),
                   pl.BlockSpec((tm, 128), lambda i: (i, 0))],
        out_shape=[_sds((M, 128), f32), _sds((M, 128), f32)],
        compiler_params=_params(("parallel",)),
    )(pos, inv128, sign128)


def _swap_halves(x):
    w = x.shape[1]
    first = (lax.broadcasted_iota(jnp.int32, (1, w), 1) % HEAD_DIM) < (HEAD_DIM // 2)
    return jnp.where(first, pltpu.roll(x, w - HEAD_DIM // 2, 1), pltpu.roll(x, HEAD_DIM // 2, 1))


def group_sum(groups, cos, sin, *, name, rotate, width, col_block=0, into=None):
    M = groups[0][0].shape[0]
    tm = min(512, M)
    counts = [len(g) for g in groups]
    flat = [a for g in groups for a in g]
    extra = [] if into is None else [into]

    def body(*refs):
        part_refs = refs[:len(flat)]
        c_ref, s_ref = refs[len(flat):len(flat) + 2]
        o_ref = refs[-1]
        cols, k = [], 0
        for n in counts:
            acc = part_refs[k][...]
            for r in part_refs[k + 1:k + n]:
                acc = acc + r[...]
            cols.append(acc)
            k += n
        d = jnp.concatenate(cols, axis=1)
        if rotate:
            c = jnp.tile(c_ref[...], (1, MAIN_W // 128))
            s = jnp.tile(s_ref[...], (1, MAIN_W // 128))
            d = d * c - _swap_halves(d) * s
        o_ref[...] = d.astype(bf16)

    part = pl.BlockSpec((tm, GROUP_W), lambda i: (i, 0))
    tab = pl.BlockSpec((tm, 128), lambda i: (i, 0))
    return pl.pallas_call(
        body, name=name, grid=(M // tm,),
        in_specs=[part] * len(flat) + [tab, tab] + [pl.BlockSpec(memory_space=pl.ANY)] * len(extra),
        out_specs=pl.BlockSpec((tm, MAIN_W), lambda i: (i, col_block)),
        out_shape=_sds((M, width), bf16),
        input_output_aliases={len(flat) + 2: 0} if extra else {},
        compiler_params=_params(("parallel",)),
    )(*flat, cos, sin, *extra)


PAIR_W = 2 * HEAD_DIM
MIN_BLOCKS = 16


FWD_TOGETHER = 4
BWD_TOGETHER = 4


def _dil_geometry(dil):
    nsub = max(dil, MIN_BLOCKS)
    tb = BAND * nsub
    return nsub, tb, SEQ // tb


REGROUP = 4


class _Regrouped:
    def __init__(self, ref):
        self.ref = ref
        self.shape = ref.shape

    def fill(self, src):
        q = self.shape[0] // REGROUP
        for r0 in range(REGROUP):
            self.ref[r0 * q:(r0 + 1) * q, :] = src[pl.ds(r0, q, stride=REGROUP), :]

    def drain(self, dst):
        q = self.shape[0] // REGROUP
        for r0 in range(REGROUP):
            dst[pl.ds(r0, q, stride=REGROUP), :] = self.ref[r0 * q:(r0 + 1) * q, :]

    def rows(self, sub, dil):
        nl, r = divmod(sub, dil)
        start = (r % REGROUP) * (self.shape[0] // REGROUP) + r // REGROUP + nl * BAND * (dil // REGROUP)
        return pl.ds(start, BAND, stride=dil // REGROUP)


def _regroups(dil):
    return dil % (4 * REGROUP) == 0


def _rows(ref, sub, dil):
    if isinstance(ref, _Regrouped):
        return ref.ref[ref.rows(sub, dil), :]
    if dil == 1:
        return ref[sub * BAND:(sub + 1) * BAND, :]
    nl, r = divmod(sub, dil)
    return ref[pl.ds(nl * BAND * dil + r, BAND, stride=dil), :]


def _store_rows(ref, sub, dil, val):
    if isinstance(ref, _Regrouped):
        ref.ref[ref.rows(sub, dil), :] = val
    elif dil == 1:
        ref[sub * BAND:(sub + 1) * BAND, :] = val
    else:
        nl, r = divmod(sub, dil)
        ref[pl.ds(nl * BAND * dil + r, BAND, stride=dil), :] = val


def _keys(prev_ref, own_ref, sub, dil):
    nsub = own_ref.shape[0] // BAND
    if sub >= dil:
        prev = _rows(own_ref, sub - dil, dil)
    elif prev_ref is None:
        return _rows(own_ref, sub, dil)
    else:
        prev = _rows(prev_ref, nsub - dil + sub, dil)
    return jnp.concatenate([prev, _rows(own_ref, sub, dil)], axis=0)


def _band_mask(nkeys, has_prev):
    i = lax.broadcasted_iota(jnp.int32, (BAND, nkeys), 0)
    j = lax.broadcasted_iota(jnp.int32, (BAND, nkeys), 1)
    if nkeys == BAND:
        return j <= i
    return (j >= i) & (j <= i + BAND) & (has_prev | (j >= BAND))


def _first_head():
    return lax.broadcasted_iota(jnp.int32, (1, PAIR_W), 1) < HEAD_DIM


def _col(x, hh):
    return x[:, hh * HEAD_DIM:hh * HEAD_DIM + 1]


def _pair_spec(tb, nblk, col0, which):
    def idx(b, p, i):
        if which < 0:
            i = jnp.maximum(i - 1, 0)
        elif which > 0:
            i = jnp.minimum(i + 1, nblk - 1)
        return (b * nblk + i, col0 + p)
    return pl.BlockSpec((tb, PAIR_W), idx)


def dil_fwd(q, k, kv, g, dil, *, name, n_seq):
    M = q.shape[0]
    nsub, tb, nblk = _dil_geometry(dil)
    with_prev = nblk > 1

    regroup = _regroups(dil)
    assert not (regroup and with_prev)

    def body(*refs):
        if with_prev:
            q_ref, ko_ref, vo_ref, kp_ref, vp_ref, o_ref, l_ref = refs
        else:
            (q_ref, ko_ref, vo_ref, o_ref, l_ref), kp_ref, vp_ref = refs[:5], None, None
        outs_to = ()
        if regroup:
            copies = [_Regrouped(s) for s in refs[5:]]
            for c, src in zip(copies, (q_ref, ko_ref, vo_ref)):
                c.fill(src)
            outs_to = ((copies[3], o_ref), (copies[4], l_ref))
            q_ref, ko_ref, vo_ref, o_ref, l_ref = copies
        first = _first_head()
        blk = pl.program_id(2)
        for sub0 in range(0, nsub, FWD_TOGETHER):
            subs = range(sub0, sub0 + FWD_TOGETHER)
            scores, values = [], []
            for sub in subs:
                qs = _rows(q_ref, sub, dil) * SCALE
                kc = _keys(kp_ref, ko_ref, sub, dil).astype(bf16)
                values.append(_keys(vp_ref, vo_ref, sub, dil).astype(bf16))
                has_prev = True if sub >= dil else blk > 0
                mask = _band_mask(kc.shape[0], has_prev)
                for hh in range(2):
                    qm = jnp.where(first if hh == 0 else ~first, qs, 0.0).astype(bf16)
                    scores.append(jnp.where(mask, lax.dot_general(qm, kc, NT, preferred_element_type=f32), NEG))
            soft = []
            for s in scores:
                m = jnp.max(s, axis=-1, keepdims=True)
                e = jnp.exp(s - m)
                l = jnp.sum(e, axis=-1, keepdims=True)
                soft.append((e.astype(bf16), 1.0 / l, jnp.broadcast_to(m + jnp.log(l), (BAND, PAIR_W))))
            outs = [jnp.dot(e, values[n // 2], preferred_element_type=f32) * inv for n, (e, inv, _) in enumerate(soft)]
            for n, sub in enumerate(subs):
                _store_rows(o_ref, sub, dil, jnp.where(first, outs[2 * n], outs[2 * n + 1]))
                _store_rows(l_ref, sub, dil, jnp.where(first, soft[2 * n][2], soft[2 * n + 1][2]))
        for c, dst in outs_to:
            c.drain(dst)

    ins = [(q, 2 * g, 0), (k, 2 * g, 0), (kv, 6 + 2 * g, 0)]
    if with_prev:
        ins += [(k, 2 * g, -1), (kv, 6 + 2 * g, -1)]
    out = _pair_spec(tb, nblk, 0, 0)
    return pl.pallas_call(
        body, name=name, grid=(n_seq, 2, nblk),
        in_specs=[_pair_spec(tb, nblk, c, w) for _, c, w in ins],
        out_specs=[out, out],
        out_shape=[_sds((M, GROUP_W), f32)] * 2,
        scratch_shapes=[pltpu.VMEM((tb, PAIR_W), f32)] * (5 if regroup else 0),
        compiler_params=_params(("parallel", "parallel", "arbitrary")),
    )(*[a for a, _, _ in ins])


def combine_fwd(os_, lses, *, name):
    M = os_[0].shape[0]
    tm = min(512, M)

    def body(o0, o1, o2, l0, l1, l2, y_ref):
        ls = [l0[...], l1[...], l2[...]]
        m = jnp.maximum(jnp.maximum(ls[0], ls[1]), ls[2])
        es = [jnp.exp(l - m) for l in ls]
        inv = 1.0 / (es[0] + es[1] + es[2])
        y_ref[...] = jnp.concatenate([o[...] * e * inv for o, e in zip((o0, o1, o2), es)], axis=1).astype(bf16)

    part = pl.BlockSpec((tm, GROUP_W), lambda i: (i, 0))
    return pl.pallas_call(
        body, name=name, grid=(M // tm,),
        in_specs=[part] * 6,
        out_specs=pl.BlockSpec((tm, MAIN_W), lambda i: (i, 0)),
        out_shape=_sds((M, D_MODEL), bf16),
        compiler_params=_params(("parallel",)),
    )(*os_, *lses)


def combine_bwd(dyc, os_, lses, *, name):
    M = os_[0].shape[0]
    tm = min(512, M)

    def body(dy_ref, o0, o1, o2, l0, l1, l2, d0, d1, d2, c0, c1, c2):
        r = lax.broadcasted_iota(jnp.int32, (GROUP_W, GROUP_W), 0) // HEAD_DIM
        c = lax.broadcasted_iota(jnp.int32, (GROUP_W, GROUP_W), 1) // HEAD_DIM
        ones = (r == c).astype(bf16)
        dy = dy_ref[...].astype(f32)
        ls = [l0[...], l1[...], l2[...]]
        m = jnp.maximum(jnp.maximum(ls[0], ls[1]), ls[2])
        es = [jnp.exp(l - m) for l in ls]
        inv = 1.0 / (es[0] + es[1] + es[2])
        total = 0.0
        alphas = []
        for g, (o, e, d_ref) in enumerate(zip((o0, o1, o2), es, (d0, d1, d2))):
            a = e * inv
            dyg = dy[:, g * GROUP_W:(g + 1) * GROUP_W]
            d_ref[...] = dyg * a
            prod = dyg * o[...]
            hi = prod.astype(bf16)
            lo = (prod - hi.astype(f32)).astype(bf16)
            dsum = jnp.dot(hi, ones, preferred_element_type=f32) + jnp.dot(lo, ones, preferred_element_type=f32)
            total = total + a * dsum
            alphas.append(a)
        for a, c_ref in zip(alphas, (c0, c1, c2)):
            c_ref[...] = -a * total

    part = pl.BlockSpec((tm, GROUP_W), lambda i: (i, 0))
    outs = pl.pallas_call(
        body, name=name, grid=(M // tm,),
        in_specs=[pl.BlockSpec((tm, MAIN_W), lambda i: (i, 0))] + [part] * 6,
        out_specs=[part] * 6,
        out_shape=[_sds((M, GROUP_W), f32)] * 6,
        compiler_params=_params(("parallel",)),
    )(dyc, *os_, *lses)
    return outs[:3], outs[3:]


def dil_bwd(q, k, kv, do, cc, lse, cos, sin, g, dil, *, name, n_seq, into=None):
    M = q.shape[0]
    nsub = SEQ // BAND
    per_res = nsub // dil
    extra = [] if into is None else [into]

    regroup = _regroups(dil)

    def body(q_ref, k_ref, v_ref, do_ref, c_ref, l_ref, cos_ref, sin_ref, *refs):
        dz_ref, dk_ref, dv_ref, dq_ref, *scratch = refs[len(extra):]
        dq_rows = dq_ref
        outs_to = ()
        if regroup:
            copies = [_Regrouped(s) for s in scratch]
            for c, src in zip(copies, (q_ref, k_ref, v_ref, do_ref, c_ref, l_ref)):
                c.fill(src)
            outs_to = tuple(zip(copies[6:], (dq_rows, dk_ref, dv_ref)))
            q_ref, k_ref, v_ref, do_ref, c_ref, l_ref, dq_ref, dk_ref, dv_ref = copies
        first = _first_head()
        order = [nl * dil + r for r in range(dil) for nl in range(per_res)]
        carry = None
        for at in range(0, nsub, BWD_TOGETHER):
            subs = order[at:at + BWD_TOGETHER]
            loaded, products = [], []
            for sub in subs:
                qs = _rows(q_ref, sub, dil) * SCALE
                dos = _rows(do_ref, sub, dil)
                kc = _keys(None, k_ref, sub, dil).astype(bf16)
                vc = _keys(None, v_ref, sub, dil).astype(bf16)
                mask = _band_mask(kc.shape[0], True)
                for hh in range(2):
                    lm = first if hh == 0 else ~first
                    qm = jnp.where(lm, qs, 0.0).astype(bf16)
                    dom = jnp.where(lm, dos, 0.0).astype(bf16)
                    loaded.append((qm, dom, kc))
                    products.append((jnp.where(mask, lax.dot_general(qm, kc, NT, preferred_element_type=f32), NEG),
                                     lax.dot_general(dom, vc, NT, preferred_element_type=f32)))
            weights = []
            for n, (s, dp) in enumerate(products):
                sub, hh = subs[n // 2], n % 2
                p = jnp.exp(s - _col(_rows(l_ref, sub, dil), hh))
                weights.append((p.astype(bf16), (p * (dp + _col(_rows(c_ref, sub, dil), hh))).astype(bf16)))
            results = []
            for (pb, ds), (qm, dom, kc) in zip(weights, loaded):
                results.append((jnp.dot(ds, kc, preferred_element_type=f32) * SCALE,
                                lax.dot_general(ds, qm, TN, preferred_element_type=f32),
                                lax.dot_general(pb, dom, TN, preferred_element_type=f32)))
            for n, sub in enumerate(subs):
                (dq0, dk0, dv0), (dq1, dk1, dv1) = results[2 * n], results[2 * n + 1]
                _store_rows(dq_ref, sub, dil, jnp.where(first, dq0, dq1))
                dkc, dvc = dk0 + dk1, dv0 + dv1
                if sub >= dil:
                    _store_rows(dk_ref, sub - dil, dil, carry[0] + dkc[:BAND])
                    _store_rows(dv_ref, sub - dil, dil, carry[1] + dvc[:BAND])
                    carry = (dkc[BAND:], dvc[BAND:])
                else:
                    carry = (dkc, dvc)
                if sub + dil >= nsub:
                    _store_rows(dk_ref, sub, dil, carry[0])
                    _store_rows(dv_ref, sub, dil, carry[1])
        for c, dst in outs_to:
            c.drain(dst)
        d = dq_rows[...]
        dz_ref[...] = (d * cos_ref[...] - _swap_halves(d) * sin_ref[...]).astype(bf16)

    def spec(col0):
        return pl.BlockSpec((SEQ, PAIR_W), lambda b, p: (b, col0 + p))

    tab = pl.BlockSpec((SEQ, PAIR_W), lambda b, p: (b, 0))
    out = spec(0)
    return pl.pallas_call(
        body, name=name, grid=(n_seq, 2),
        in_specs=[spec(2 * g), spec(2 * g), spec(6 + 2 * g), spec(0), spec(0), spec(0), tab, tab]
        + [pl.BlockSpec(memory_space=pl.ANY)] * len(extra),
        out_specs=[spec(2 * g), out, out],
        out_shape=[_sds((M, D_MODEL), bf16)] + [_sds((M, GROUP_W), f32)] * 2,
        input_output_aliases={8: 0} if extra else {},
        scratch_shapes=[pltpu.VMEM((SEQ, PAIR_W), f32)] * (10 if regroup else 1),
        compiler_params=_params(("parallel", "parallel")),
    )(q, k, kv, do, cc, lse, cos, sin, *extra)


def _blockdiag(wp):
    out = jnp.zeros((MAIN_W, MAIN_W), wp.dtype)
    for gi in range(len(POOL_WINDOWS)):
        sl = slice(gi * POOL_GROUP, (gi + 1) * POOL_GROUP)
        out = out.at[sl, sl].set(wp[gi])
    return out


def _unblockdiag(w):
    return jnp.stack([w[gi * POOL_GROUP:(gi + 1) * POOL_GROUP, gi * POOL_GROUP:(gi + 1) * POOL_GROUP]
                      for gi in range(len(POOL_WINDOWS))])


def local_step(x, mem, positions, target, P, layer_weights, kv_weight, emit_grads):
    n_seq = x.shape[0]
    M = n_seq * SEQ
    xs = x.reshape(M, D_MODEL)
    mems = mem.reshape(n_seq * N_MEM, D_MODEL)
    pos = positions.reshape(M, 1).astype(f32)
    cos, sin = rope_tables(pos, name="rope_tables")
    gains = P["norm_gains"]

    def gain(l, k):
        return gains[l, k].reshape(1, D_MODEL)

    saved = []
    kvs = None
    for l in range(DEPTH):
        W, started = layer_weights(l, "mix", xs)
        sv = {"x": xs, "W": W}
        z, h1, *qrot = rms_matmul(xs, gain(l, 0), W["w_in"], name=f"l{l}_in", out_dtype=bf16, after=started,
                                  rope=None if l < N_A_LAYERS else (cos, sin))
        kvm, mn = rms_matmul(mems, P["mem_norm"][l].reshape(1, D_MODEL), W["w_mem_kv"],
                             name=f"l{l}_memkv", out_dtype=bf16)
        sv.update(z=z, h1=h1, kvm=kvm, mn=mn)
        if l < N_A_LAYERS:
            wbd = _blockdiag(P["w_pool"][l].astype(bf16))
            psc = P["pool_scale"][l].reshape(1, MAIN_W)
            p, y_main = pool_fwd(z, wbd, psc, name=f"l{l}_pool")
            sv.update(p=p, wbd=wbd, psc=psc)
        else:
            (qrot,) = qrot
            os_, lses = [], []
            for g, (_, dil) in enumerate(DIL_PATTERNS):
                o, lse = dil_fwd(qrot, kvs["krot"], kvs["kv"], g, dil, name=f"l{l}_dil{g}", n_seq=n_seq)
                os_.append(o)
                lses.append(lse)
            y_main = combine_fwd(os_, lses, name=f"l{l}_comb")
            sv.update(qrot=qrot, os=os_, lses=lses)
        ycat = memattn_fwd(z, kvm, y_main, name=f"l{l}_memattn", n_seq=n_seq)
        y, x1 = matmul_rms_res(ycat, W["w_out"], gain(l, 1), xs, name=f"l{l}_out")
        W.update(layer_weights(l, "gu", x1)[0])
        fg, fu, a, h2 = rms_gate_up(x1, gain(l, 2), W["w_gate_up"], name=f"l{l}_gu")
        W_down, passing = layer_weights(l, "down", a)
        W.update(W_down)
        y2, x2, *sq = matmul_rms_res(a, W["w_down"], gain(l, 3), x1, name=f"l{l}_down", after=passing,
                                     target=target.reshape(M, D_MODEL) if l == DEPTH - 1 else None)
        sv.update(ycat=ycat, y=y, x1=x1, fg=fg, fu=fu, h2=h2, a=a, y2=y2)
        saved.append(sv)
        xs = x2
        if l == N_A_LAYERS - 1:
            w_kv = kv_weight(xs)
            kv, hkv, krot = rms_matmul(xs, P["kv_norm"].reshape(1, D_MODEL), w_kv, name="kv_proj", out_dtype=f32,
                                       transposed=True, rope=(cos, sin))
            kvs = {"kv": kv, "hkv": hkv, "krot": krot, "x": xs, "w_kv": w_kv}

    dx, (sq,) = xs, sq

    G = {"mem_norm": [None] * DEPTH, "norm_gains": [[None] * 4 for _ in range(DEPTH)],
         "pool_scale": [None] * N_A_LAYERS}
    dk_parts = [[] for _ in range(N_GROUPS)]
    dv_parts = [[] for _ in range(N_GROUPS)]
    emitted = None

    for l in reversed(range(DEPTH)):
        sv = saved[l]
        W = sv["W"]
        gw = {}
        dy2, dgu, G["norm_gains"][l][3] = down_bwd(sv["y2"], gain(l, 3), dx, W["w_down"], sv["fg"], sv["fu"],
                                                   name=f"l{l}_b_dgu", after=emitted)
        gw["w_down"] = matmul(sv["a"], dy2, TN, name=f"l{l}_b_wd", out_dtype=bf16)
        gw["w_gate_up"] = matmul(dgu, sv["h2"], TN, name=f"l{l}_b_wgu", out_dtype=bf16)
        emitted = emit_grads(l, "ffn", gw)
        dx1, G["norm_gains"][l][2] = matmul_rms_bwd(dgu, W["w_gate_up"], NN, sv["x1"], gain(l, 2), dx,
                                                    name=f"l{l}_b_dh2", after=emitted)
        gw = {}
        dy, dycat, G["norm_gains"][l][1] = rms_bwd_matmul(sv["y"], gain(l, 1), dx1, W["w_out"], NT,
                                                          name=f"l{l}_b_dycat", after=emitted)
        gw["w_out"] = matmul(sv["ycat"], dy, TN, name=f"l{l}_b_wout", out_dtype=bf16)
        if l < N_A_LAYERS:
            dz, dwbd, dps = pool_bwd(dycat, sv["p"], sv["wbd"], sv["psc"], name=f"l{l}_b_pool")
            gw["w_pool"] = _unblockdiag(dwbd).reshape(MAIN_W, POOL_GROUP).astype(bf16)
            G["pool_scale"][l] = dps.reshape(MAIN_W)
        else:
            dos, ccs = combine_bwd(dycat, sv["os"], sv["lses"], name=f"l{l}_b_comb")
            dz = None
            for g, (_, dil) in enumerate(DIL_PATTERNS):
                args = (sv["qrot"], kvs["krot"], kvs["kv"], dos[g], ccs[g], sv["lses"][g], cos, sin, g, dil)
                dz, dk, dv = dil_bwd(*args, name=f"l{l}_b_dil{g}", n_seq=n_seq, into=dz)
                dk_parts[g].append(dk)
                dv_parts[g].append(dv)
        dz, dkvm = memattn_bwd(sv["z"], sv["kvm"], dycat, dz, name=f"l{l}_b_memattn", n_seq=n_seq)
        gw["w_mem_kv"] = matmul(sv["mn"], dkvm, TN, name=f"l{l}_b_wmkv", out_dtype=bf16)
        _, G["mem_norm"][l] = matmul_rms_bwd(dkvm, W["w_mem_kv"], NT, mems, P["mem_norm"][l].reshape(1, D_MODEL),
                                             mems, name=f"l{l}_b_dmn")
        gw["w_in"] = matmul(sv["h1"], dz, TN, name=f"l{l}_b_win", out_dtype=bf16)
        if l != N_A_LAYERS:
            emitted = emit_grads(l, "mix", gw)
        dx, G["norm_gains"][l][0] = matmul_rms_bwd(dz, W["w_in"], NT, sv["x"], gain(l, 0), dx1, name=f"l{l}_b_dh1",
                                                   after=emitted)
        if l == N_A_LAYERS:
            dkv = group_sum(dk_parts, cos, sin, name="b_ropek", rotate=True, width=2 * MAIN_W)
            dkv = group_sum(dv_parts, cos, sin, name="b_sumv", rotate=False, width=2 * MAIN_W, col_block=1, into=dkv)
            gw["w_kv"] = matmul(dkv, kvs["hkv"], TN, name="b_wkv", out_dtype=bf16)
            dx, gkn = matmul_rms_bwd(dkv, kvs["w_kv"], NN, kvs["x"], P["kv_norm"].reshape(1, D_MODEL), dx,
                                     name="b_dhkv")
            G["kv_norm"] = gkn.reshape(D_MODEL)
            emitted = emit_grads(l, "mix", gw)

    small = {"pool_scale": jnp.stack(G["pool_scale"]),
             "mem_norm": jnp.concatenate(G["mem_norm"], axis=0),
             "norm_gains": jnp.stack([jnp.concatenate(r, axis=0) for r in G["norm_gains"]]),
             "kv_norm": G["kv_norm"]}
    return sq[0, 0], dx.reshape(n_seq, SEQ, D_MODEL), small, emitted


def to_bf16_layers(stacks, after, *, name):
    L, n = stacks[0].shape[0], len(stacks)

    def body(*refs):
        ins, outs = refs[:n], refs[n + 1:]
        for j in range(L):
            @pl.when(pl.program_id(0) == j)
            def _():
                for k in range(n):
                    outs[j * n + k][...] = ins[k][...].astype(bf16)

    outs = pl.pallas_call(
        body, name=name, grid=(L,),
        in_specs=[pl.BlockSpec((None,) + s.shape[1:], lambda l: (l, 0, 0)) for s in stacks]
        + [pl.BlockSpec(memory_space=pl.ANY)],
        out_specs=[pl.BlockSpec(s.shape[1:], lambda l: (0, 0)) for _ in range(L) for s in stacks],
        out_shape=[_sds(s.shape[1:], bf16) for _ in range(L) for s in stacks],
        compiler_params=_params(("arbitrary",)),
    )(*stacks, after)
    return [outs[j * n:(j + 1) * n] for j in range(L)]


def _peer(k):
    x, y, c = lax.axis_index("x"), lax.axis_index("y"), lax.axis_index("c")
    px = 1 - x if k & 4 else x
    py = 1 - y if k & 2 else y
    pc = 1 - c if k & 1 else c
    return (px, py, pc), 4 * px + 2 * py + pc


def _my_index():
    return 4 * lax.axis_index("x") + 2 * lax.axis_index("y") + lax.axis_index("c")


def _src_for(kinds, in_refs, i, idx):
    return in_refs[i] if kinds[i] == "gather" else in_refs[i].at[idx]


def _local_copies(kinds, in_refs, out_refs, local_sems):
    me = _my_index()
    return [pltpu.make_async_copy(_src_for(kinds, in_refs, i, me), out_refs[i].at[me], local_sems.at[i])
            for i in range(len(kinds))]


def _remote_copies(kinds, in_refs, out_refs, send_sems, recv_sems, *, arriving):
    me = _my_index()
    copies = []
    for k in range(1, N_DEV):
        dev, idx = _peer(k)
        for i in range(len(kinds)):
            j = i * (N_DEV - 1) + k - 1
            copies.append(pltpu.make_async_remote_copy(
                src_ref=_src_for(kinds, in_refs, i, idx), dst_ref=out_refs[i].at[idx if arriving else me],
                send_sem=send_sems.at[j], recv_sem=recv_sems.at[j], device_id=dev, device_id_type=MESH))
    return copies


def _out_shape(a, kind):
    return ((N_DEV,) + a.shape) if kind == "gather" else a.shape


def exchange(items, *, name, after=()):
    n = len(items)
    kinds = [k for _, k in items]
    after = list(after)

    def body(*refs):
        in_refs, out_refs = refs[:n], refs[n + len(after):2 * n + len(after)]
        send_sems, recv_sems, local_sems = refs[-3:]
        local = _local_copies(kinds, in_refs, out_refs, local_sems)
        sends = _remote_copies(kinds, in_refs, out_refs, send_sems, recv_sems, arriving=False)
        for cp in local + sends:
            cp.start()
        for cp in _remote_copies(kinds, in_refs, out_refs, send_sems, recv_sems, arriving=True):
            cp.wait_recv()
        for cp in sends:
            cp.wait_send()
        for cp in local:
            cp.wait()

    any_spec = pl.BlockSpec(memory_space=pl.ANY)
    return pl.pallas_call(
        body, name=name,
        in_specs=[any_spec] * (n + len(after)), out_specs=[any_spec] * n,
        out_shape=[_sds(_out_shape(a, k), a.dtype) for a, k in items],
        scratch_shapes=[pltpu.SemaphoreType.DMA((n * (N_DEV - 1),)), pltpu.SemaphoreType.DMA((n * (N_DEV - 1),)),
                        pltpu.SemaphoreType.DMA((n,))],
    )(*[a for a, _ in items], *after)


_HBM = pl.BlockSpec(memory_space=pltpu.HBM)
_SEM = pl.BlockSpec(memory_space=pltpu.SEMAPHORE)
_EFFECT = pltpu.SideEffectType.DATAFLOW_SIDE_EFFECTING


def exchange_start(items, after, *, name):
    n = len(items)
    kinds = [k for _, k in items]

    def body(*refs):
        in_refs, land_refs = refs[:n], refs[n:2 * n]
        send_sems, recv_sems, local_sems = refs[2 * n + 1:2 * n + 4]
        token = refs[-1]
        for cp in (_local_copies(kinds, in_refs, land_refs, local_sems)
                   + _remote_copies(kinds, in_refs, land_refs, send_sems, recv_sems, arriving=False)):
            cp.start()
        token[...] = jnp.zeros_like(token)

    srcs = [pltpu.with_memory_space_constraint(a, pltpu.HBM) for a, _ in items]
    lands = [pltpu.with_memory_space_constraint(lax.empty(_out_shape(a, k), a.dtype), pltpu.HBM) for a, k in items]
    outs = pl.pallas_call(
        body, name=name,
        out_shape=(pltpu.SemaphoreType.DMA((n * (N_DEV - 1),)), pltpu.SemaphoreType.DMA((n * (N_DEV - 1),)),
                   pltpu.SemaphoreType.DMA((n,)),
                   *[pltpu.HBM(a.shape, a.dtype) for a in srcs], *[pltpu.HBM(a.shape, a.dtype) for a in lands],
                   _sds((8, 128), f32)),
        in_specs=[_HBM] * (2 * n) + [pl.BlockSpec(memory_space=pl.ANY)],
        out_specs=(_SEM, _SEM, _SEM, *[_HBM] * (2 * n), pl.BlockSpec(memory_space=pltpu.VMEM)),
        input_output_aliases={i: 3 + i for i in range(2 * n)},
        compiler_params=pltpu.CompilerParams(has_side_effects=_EFFECT),
    )(*srcs, *lands, after)
    return {"kinds": kinds, "sems": outs[:3], "srcs": outs[3:3 + n], "lands": outs[3 + n:3 + 2 * n], "token": outs[-1]}


def exchange_wait(handle, after, *, name):
    kinds = handle["kinds"]
    n = len(kinds)

    def body(*refs):
        in_refs, land_refs = refs[:n], refs[n:2 * n]
        send_sems, recv_sems, local_sems = refs[2 * n:2 * n + 3]
        for cp in _remote_copies(kinds, in_refs, land_refs, send_sems, recv_sems, arriving=True):
            cp.wait_recv()
        for cp in _remote_copies(kinds, in_refs, land_refs, send_sems, recv_sems, arriving=False):
            cp.wait_send()
        for cp in _local_copies(kinds, in_refs, land_refs, local_sems):
            cp.wait()

    srcs, lands = list(handle["srcs"]), list(handle["lands"])
    after = list(after) if isinstance(after, (list, tuple)) else [after]
    outs = pl.pallas_call(
        body, name=name,
        out_shape=tuple(pltpu.HBM(a.shape, a.dtype) for a in srcs + lands),
        in_specs=[_HBM] * (2 * n) + [_SEM] * 3 + [pl.BlockSpec(memory_space=pl.ANY)] * len(after),
        out_specs=tuple([_HBM] * (2 * n)),
        input_output_aliases={i: i for i in range(2 * n)},
        compiler_params=pltpu.CompilerParams(has_side_effects=_EFFECT),
    )(*srcs, *lands, *handle["sems"], *after)
    return list(outs[n:])


CHIP_MASKS = (2, 4, 6)


def _g2_first(in_refs, land_refs, send_sems, recv_sems, *, masks, arriving):
    me = _my_index()
    copies = []
    for i in range(len(land_refs)):
        for j, k in enumerate(masks):
            dev, idx = _peer(k)
            dst = land_refs[i].at[idx if arriving else me]
            copies.append(pltpu.make_async_remote_copy(
                src_ref=dst if in_refs is None else in_refs[i], dst_ref=dst,
                send_sem=send_sems.at[i * len(masks) + j], recv_sem=recv_sems.at[i * len(masks) + j],
                device_id=dev, device_id_type=MESH))
    return copies


def _g2_forward(land_refs, fwd_send, fwd_recv, *, arriving):
    sibling, _ = _peer(1)
    copies = []
    for i in range(len(land_refs)):
        for j, k in enumerate(CHIP_MASKS):
            _, idx = _peer(k | 1 if arriving else k)
            copies.append(pltpu.make_async_remote_copy(
                src_ref=land_refs[i].at[idx], dst_ref=land_refs[i].at[idx],
                send_sem=fwd_send.at[i * 3 + j], recv_sem=fwd_recv.at[i * 3 + j], device_id=sibling,
                device_id_type=MESH))
    return copies


def gather2_start(arrays, after, *, name):
    n = len(arrays)

    def body(*refs):
        in_refs, land_refs = refs[:n], refs[n:2 * n]
        ici_send, ici_recv, d2d_send, d2d_recv, local_sems = refs[2 * n + 1:2 * n + 6]
        token = refs[-1]
        ici = _g2_first(in_refs, land_refs, ici_send, ici_recv, masks=CHIP_MASKS, arriving=False)
        d2d = _g2_first(in_refs, land_refs, d2d_send, d2d_recv, masks=(1,), arriving=False)
        for cp in _local_copies(["gather"] * n, in_refs, land_refs, local_sems) + ici + d2d:
            cp.start()
        token[...] = jnp.zeros_like(token)

    srcs = [pltpu.with_memory_space_constraint(a, pltpu.HBM) for a in arrays]
    lands = [pltpu.with_memory_space_constraint(lax.empty((N_DEV,) + a.shape, a.dtype), pltpu.HBM) for a in arrays]
    sem = pltpu.SemaphoreType.DMA
    outs = pl.pallas_call(
        body, name=name,
        out_shape=(sem((3 * n,)), sem((3 * n,)), sem((n,)), sem((n,)), sem((n,)),
                   *[pltpu.HBM(a.shape, a.dtype) for a in srcs], *[pltpu.HBM(a.shape, a.dtype) for a in lands],
                   _sds((8, 128), f32)),
        in_specs=[_HBM] * (2 * n) + [pl.BlockSpec(memory_space=pl.ANY)],
        out_specs=(*[_SEM] * 5, *[_HBM] * (2 * n), pl.BlockSpec(memory_space=pltpu.VMEM)),
        input_output_aliases={i: 5 + i for i in range(2 * n)},
        compiler_params=pltpu.CompilerParams(has_side_effects=_EFFECT),
    )(*srcs, *lands, after)
    return {"n": n, "sems": outs[:5], "srcs": outs[5:5 + n], "lands": outs[5 + n:5 + 2 * n], "token": outs[-1]}


def gather2_forward(handle, after, *, name):
    n = handle["n"]

    def body(*refs):
        land_refs = refs[:n]
        ici_recv = refs[n]
        fwd_send, fwd_recv = refs[n + 2:n + 4]
        for cp in _g2_first(None, land_refs, fwd_send, ici_recv, masks=CHIP_MASKS, arriving=True):
            cp.wait_recv()
        for cp in _g2_forward(land_refs, fwd_send, fwd_recv, arriving=False):
            cp.start()
        token = refs[-1]
        token[...] = jnp.zeros_like(token)

    lands = list(handle["lands"])
    sem = pltpu.SemaphoreType.DMA
    outs = pl.pallas_call(
        body, name=name,
        out_shape=(sem((3 * n,)), sem((3 * n,)), *[pltpu.HBM(a.shape, a.dtype) for a in lands], _sds((8, 128), f32)),
        in_specs=[_HBM] * n + [_SEM, pl.BlockSpec(memory_space=pl.ANY)],
        out_specs=(_SEM, _SEM, *[_HBM] * n, pl.BlockSpec(memory_space=pltpu.VMEM)),
        input_output_aliases={i: 2 + i for i in range(n)},
        compiler_params=pltpu.CompilerParams(has_side_effects=_EFFECT),
    )(*lands, handle["sems"][1], after)
    return dict(handle, fwd=outs[:2], lands=outs[2:2 + n], token=outs[-1])


def gather2_wait(handle, after, *, name):
    n = handle["n"]

    def body(*refs):
        in_refs, land_refs = refs[:n], refs[n:2 * n]
        ici_send, d2d_send, d2d_recv, local_sems, fwd_send, fwd_recv = refs[2 * n:2 * n + 6]
        for cp in _g2_first(in_refs, land_refs, d2d_send, d2d_recv, masks=(1,), arriving=True):
            cp.wait_recv()
        for cp in _g2_forward(land_refs, fwd_send, fwd_recv, arriving=True):
            cp.wait_recv()
        for cp in (_g2_first(in_refs, land_refs, ici_send, fwd_recv, masks=CHIP_MASKS, arriving=False)
                   + _g2_first(in_refs, land_refs, d2d_send, d2d_recv, masks=(1,), arriving=False)
                   + _g2_forward(land_refs, fwd_send, fwd_recv, arriving=False)):
            cp.wait_send()
        for cp in _local_copies(["gather"] * n, in_refs, land_refs, local_sems):
            cp.wait()

    srcs, lands = list(handle["srcs"]), list(handle["lands"])
    s = handle["sems"]
    outs = pl.pallas_call(
        body, name=name,
        out_shape=tuple(pltpu.HBM(a.shape, a.dtype) for a in srcs + lands),
        in_specs=[_HBM] * (2 * n) + [_SEM] * 6 + [pl.BlockSpec(memory_space=pl.ANY)],
        out_specs=tuple([_HBM] * (2 * n)),
        input_output_aliases={i: i for i in range(2 * n)},
        compiler_params=pltpu.CompilerParams(has_side_effects=_EFFECT),
    )(*srcs, *lands, s[0], s[2], s[3], s[4], *handle["fwd"], after)
    return list(outs[n:])


def adamw(entries, *, name):
    c1 = 1.0 - ADAM_B1 ** ADAM_STEP
    c2 = 1.0 - ADAM_B2 ** ADAM_STEP
    tiles = [_tile(w.shape[-2], (64, 32, 16, 8)) for _, w, _, _, _, _ in entries]
    steps = [w.shape[-2] // tr for (_, w, _, _, _, _), tr in zip(entries, tiles)]
    n = len(entries)

    def body(*refs):
        i = pl.program_id(0)
        for e in range(n):
            s_ref, w_ref, m_ref, v_ref = refs[4 * e:4 * e + 4]
            g_ref, d_ref, m2_ref, v2_ref = refs[len(refs) - 4 * n + 4 * e:len(refs) - 4 * n + 4 * e + 4]

            @pl.when(i < steps[e])
            def _():
                g = s_ref[0].astype(f32)
                for d in range(1, N_DEV):
                    g = g + s_ref[d].astype(f32)
                m2 = ADAM_B1 * m_ref[...] + (1.0 - ADAM_B1) * g
                v2 = ADAM_B2 * v_ref[...] + (1.0 - ADAM_B2) * (g * g)
                g_ref[...] = g
                m2_ref[...] = m2
                v2_ref[...] = v2
                d_ref[...] = -ADAM_LR * ((m2 / c1) / (jnp.sqrt(v2 / c2) + ADAM_EPS) + ADAM_WD * w_ref[...])

    in_specs, out_specs, out_shape, args, extras, aliases = [], [], [], [], [], {}
    for e, ((slots, w, m, v, layer, into), tr, ns) in enumerate(zip(entries, tiles, steps)):
        C = w.shape[-1]
        row = lambda i, ns=ns: jnp.minimum(i, ns - 1)
        if layer is None:
            blk = pl.BlockSpec((tr, C), lambda i, row=row: (row(i), 0))
        else:
            blk = pl.BlockSpec((None, tr, C), lambda i, row=row, layer=layer: (layer, row(i), 0))
        in_specs += [pl.BlockSpec((N_DEV, tr, C), lambda i, row=row: (0, row(i), 0)), blk, blk, blk]
        args += [slots, w, m, v]
        out_specs += [blk] * 4
        out_shape += [_sds(w.shape, f32)] * 4
        if into is not None:
            for t, a in enumerate(into):
                aliases[4 * n + len(extras)] = 4 * e + t
                extras.append(a)
    outs = pl.pallas_call(
        body, name=name, grid=(max(steps),),
        in_specs=in_specs + [pl.BlockSpec(memory_space=pl.ANY)] * len(extras),
        out_specs=out_specs, out_shape=out_shape, input_output_aliases=aliases,
        compiler_params=_params(("arbitrary",)),
    )(*args, *extras)
    return [outs[4 * e:4 * e + 4] for e in range(n)]


WEIGHTS = ("norm_gains", "mem_norm", "w_in", "w_mem_kv", "w_out", "w_pool", "pool_scale", "kv_norm", "w_kv",
           "w_gate_up", "w_down")
LAYER_MATS = ("w_in", "w_mem_kv", "w_out", "w_gate_up", "w_down")
POOL_SHARD = MAIN_W // N_DEV
KV_SHARD = 2 * MAIN_W // N_DEV
LOOKAHEAD = 2
TWO_LEVEL_LAYERS = (0, 1, 2)


def _pack_small(gains, pscale):
    lead = gains.shape[:-3]
    g = gains.reshape(lead + (16, 128))
    p = jnp.zeros(lead + (8, 128), f32).at[..., :2, :POOL_SHARD].set(pscale)
    return jnp.concatenate([g, p], axis=-2)


def _unpack_small(a):
    return a[:16].reshape(4, 4, 128), a[16:18, :POOL_SHARD]


def _pack_repl(mem_norm, kv_norm):
    return jnp.concatenate([mem_norm, kv_norm.reshape(1, D_MODEL), jnp.zeros((3, D_MODEL), f32)], axis=0)


def _unpack_repl(a):
    return a[:4], a[4]


def kernel(x, mem, positions, norm_gains, mem_norm, w_in, w_mem_kv, w_out, w_pool, pool_scale, kv_norm, w_kv, w_gate_up, w_down, loss_target, m_norm_gains, m_mem_norm, m_w_in, m_w_mem_kv, m_w_out, m_w_pool, m_pool_scale, m_kv_norm, m_w_kv, m_w_gate_up, m_w_down, v_norm_gains, v_mem_norm, v_w_in, v_w_mem_kv, v_w_out, v_w_pool, v_pool_scale, v_kv_norm, v_w_kv, v_w_gate_up, v_w_down):
    w = dict(norm_gains=norm_gains, mem_norm=mem_norm, w_in=w_in, w_mem_kv=w_mem_kv, w_out=w_out, w_pool=w_pool,
             pool_scale=pool_scale, kv_norm=kv_norm, w_kv=w_kv, w_gate_up=w_gate_up, w_down=w_down)
    m = dict(norm_gains=m_norm_gains, mem_norm=m_mem_norm, w_in=m_w_in, w_mem_kv=m_w_mem_kv, w_out=m_w_out,
             w_pool=m_w_pool, pool_scale=m_pool_scale, kv_norm=m_kv_norm, w_kv=m_w_kv, w_gate_up=m_w_gate_up,
             w_down=m_w_down)
    v = dict(norm_gains=v_norm_gains, mem_norm=v_mem_norm, w_in=v_w_in, w_mem_kv=v_w_mem_kv, w_out=v_w_out,
             w_pool=v_w_pool, pool_scale=v_pool_scale, kv_norm=v_kv_norm, w_kv=v_w_kv, w_gate_up=v_w_gate_up,
             w_down=v_w_down)

    def transposed_view(d):
        d = dict(d)
        d["w_gate_up"] = jnp.swapaxes(d["w_gate_up"], 1, 2)
        d["w_kv"] = jnp.swapaxes(d["w_kv"], 0, 1)
        return d

    wv, mv, vv = transposed_view(w), transposed_view(m), transposed_view(v)

    PARTS = {"mix": ("w_in", "w_mem_kv", "w_out"), "ffn": ("w_gate_up", "w_down"), "gu": ("w_gate_up",),
             "down": ("w_down",), "all": ("w_in", "w_mem_kv", "w_out", "w_gate_up", "w_down")}

    handles = {(0, "mix"): gather2_start([wv[k][0].astype(bf16) for k in PARTS["mix"]], x, name="gather_start_mix_l0")}
    first = handles[0, "mix"]["token"]

    small = _pack_small(norm_gains, pool_scale)
    (gsmall,) = exchange([(small, "gather")], name="gather_small", after=[first])
    P = {"norm_gains": jnp.moveaxis(gsmall[:, :16].reshape(N_DEV, 4, 4, 128), 0, 2).reshape(4, 4, D_MODEL),
         "pool_scale": jnp.moveaxis(gsmall[:, 16:18, :POOL_SHARD], 0, 1).reshape(2, MAIN_W),
         "mem_norm": mem_norm, "kv_norm": kv_norm, "w_pool": w_pool}

    def parts_of(l):
        return (("mix", "gu", "down"), ("mix", "ffn"))[l] if l < 2 else ("all",)

    wb = [dict(zip(LAYER_MATS, mats))
          for mats in to_bf16_layers([wv[k] for k in LAYER_MATS], first, name="weights_bf16")]

    def part_items(l, part):
        items = [(wb[l][k], "gather") for k in PARTS[part]]
        if part == "ffn" and l == N_A_LAYERS - 1:
            items.append((wv["w_kv"].astype(bf16), "gather"))
        return items

    def start_layer(l, after):
        for part in parts_of(l):
            if (l, part) in handles:
                continue
            if l in TWO_LEVEL_LAYERS:
                handles[l, part] = gather2_start([a for a, _ in part_items(l, part)], after,
                                                 name=f"gather_start_{part}_l{l}")
            else:
                handles[l, part] = exchange_start(part_items(l, part), after, name=f"gather_start_{part}_l{l}")
            after = handles[l, part]["token"]
        return after

    token = gsmall
    for l in range(LOOKAHEAD):
        token = start_layer(l, token)
    landed = {}

    passed_early = {}

    def layer_weights(l, part, after):
        if part not in parts_of(l):
            if part == "down" and l + 1 in TWO_LEVEL_LAYERS and parts_of(l + 1) == ("all",):
                passed_early[l + 1] = gather2_forward(handles[l + 1, "all"], after, name=f"gather_forward_all_l{l + 1}")
                return {}, passed_early[l + 1]["token"]
            if part == "down" or (part == "gu" and "all" in parts_of(l)):
                return {}, None
            part = "all" if "all" in parts_of(l) else "ffn"
        if l == 0 and part == "mix":
            after = token
        if l in TWO_LEVEL_LAYERS:
            passed = passed_early.pop(l, None)
            if passed is None:
                passed = gather2_forward(handles[l, part], after, name=f"gather_forward_{part}_l{l}")
            got = gather2_wait(passed, after, name=f"gather_wait_{part}_l{l}")
        else:
            got = exchange_wait(handles[l, part], after, name=f"gather_wait_{part}_l{l}")
        landed[l, part] = got
        started = None
        if part in ("mix", "all") and l + LOOKAHEAD < DEPTH:
            started = start_layer(l + LOOKAHEAD, got[0])
        W = {k: g.reshape(-1, g.shape[-1]) for k, g in zip(PARTS[part], got)}
        return W, started

    def kv_weight(after):
        g = landed[N_A_LAYERS - 1, "ffn"][len(PARTS["ffn"])]
        return g.reshape(2 * MAIN_W, D_MODEL)

    ghandles = {}

    pending = {}

    def gparts_of(l):
        return ("ffn", "mix") if l < 2 else ("all",)

    def emit_grads(l, part, gw):
        if part not in gparts_of(l):
            pending.setdefault(l, {}).update(gw)
            if part == "ffn":
                return None
            gw, part = pending[l], "all"
        items = [(gw[k].reshape((N_DEV, -1) + gw[k].shape[-1:]), "scatter") for k in PARTS[part]]
        if part != "ffn" and l == N_A_LAYERS:
            items.append((gw["w_kv"].reshape(N_DEV, KV_SHARD, D_MODEL), "scatter"))
        if part != "ffn" and l < N_A_LAYERS:
            items.append((gw["w_pool"], "gather"))
        ghandles[l, part] = exchange_start(items, gsmall, name=f"scatter_start_{part}_l{l}")
        return ghandles[l, part]["token"]

    sq, grad_x, GS, emitted = local_step(x, mem, positions, loss_target, P, layer_weights, kv_weight, emit_grads)

    def pool3(a):
        return a.reshape(N_A_LAYERS, MAIN_W, POOL_GROUP)

    out = {}
    after = [emitted]

    def finish_layer(l, after):
        for part in gparts_of(l):
            got = exchange_wait(ghandles[l, part], after, name=f"scatter_wait_{part}_l{l}")
            names = list(PARTS[part])
            entries = [(slots, wv[k], mv[k], vv[k], l, out.get(k)) for k, slots in zip(names, got)]
            if part != "ffn" and l == N_A_LAYERS:
                names.append("w_kv")
                entries.append((got[-1], wv["w_kv"], mv["w_kv"], vv["w_kv"], None, None))
            if part != "ffn" and l < N_A_LAYERS:
                names.append("w_pool")
                entries.append((got[-1], pool3(w_pool), pool3(m_w_pool), pool3(v_w_pool), l, out.get("w_pool")))
            out.update(zip(names, adamw(entries, name=f"adamw_{part}_l{l}")))
            after = [out[k][0] for k in names]
        return after

    for l in reversed(range(1, DEPTH)):
        after = finish_layer(l, after)

    gs = _pack_small(jnp.moveaxis(GS["norm_gains"].reshape(4, 4, N_DEV, 128), 2, 0),
                     jnp.moveaxis(GS["pool_scale"].reshape(2, N_DEV, POOL_SHARD), 1, 0))
    parts_small, parts_repl, parts_sq = exchange(
        [(gs, "scatter"), (_pack_repl(GS["mem_norm"], GS["kv_norm"]), "gather"),
         (jnp.full((8, 128), sq, f32), "gather")],
        name="exchange_small_grads", after=after)
    loss = (0.5 / D_MODEL) * jnp.sum(parts_sq[:, 0, 0])
    finish_layer(0, [parts_small])
    out["w_gate_up"] = [jnp.swapaxes(r, 1, 2) for r in out["w_gate_up"]]
    out["w_kv"] = [jnp.swapaxes(r, 0, 1) for r in out["w_kv"]]
    out["w_pool"] = [r.reshape(w_pool.shape) for r in out["w_pool"]]

    res_small, res_repl = adamw(
        [(parts_small, small, _pack_small(m_norm_gains, m_pool_scale), _pack_small(v_norm_gains, v_pool_scale),
          None, None),
         (parts_repl, _pack_repl(mem_norm, kv_norm), _pack_repl(m_mem_norm, m_kv_norm),
          _pack_repl(v_mem_norm, v_kv_norm), None, None)], name="adamw_small")
    out["norm_gains"], out["pool_scale"] = zip(*[_unpack_small(r) for r in res_small])
    out["mem_norm"], out["kv_norm"] = zip(*[_unpack_repl(r) for r in res_repl])

    return (loss, grad_x, *[out[k][0] for k in WEIGHTS], *[out[k][1] for k in WEIGHTS],
            *[out[k][2] for k in WEIGHTS], *[out[k][3] for k in WEIGHTS])
```

```python
import numpy as np
import jax
import jax.numpy as jnp
from jax import lax
from jax.experimental import pallas as pl
from jax.experimental.pallas import tpu as pltpu

f32 = jnp.float32
bf16 = jnp.bfloat16

D_MODEL = 1024
SEQ = 2048
DEPTH = 4
N_MEM = 256
HEAD_DIM = 64
N_MEM_HEADS = 4
MEM_W = 256
MAIN_W = 768
POOL_WINDOWS = (2, 4, 8, 16)
POOL_GROUP = 192
POOL_HALO = 16
DIL_PATTERNS = ((128, 1), (512, 4), (2048, 16))
N_GROUPS = 3
GROUP_W = 256
BAND = 128
N_A_LAYERS = 2
D_FF = 2816
ROPE_THETA = 10000.0
EPS = 1e-6
NEG = -1e30
SCALE = HEAD_DIM ** -0.5
N_DEV = 8

ADAM_LR = 0.001
ADAM_B1 = 0.9
ADAM_B2 = 0.999
ADAM_EPS = 1e-08
ADAM_WD = 0.01
ADAM_STEP = 10

VMEM_LIMIT_BYTES = 56 * 1024 * 1024
MESH = pl.DeviceIdType.MESH

NN = (((1,), (0,)), ((), ()))
NT = (((1,), (1,)), ((), ()))
TN = (((0,), (0,)), ((), ()))


def _params(sem=None):
    return pltpu.CompilerParams(dimension_semantics=sem, vmem_limit_bytes=VMEM_LIMIT_BYTES)


def _tile(n, cands):
    for c in cands:
        if n % c == 0:
            return c
    return n


def _sds(shape, dtype):
    return jax.ShapeDtypeStruct(tuple(shape), dtype)


def _rms_r(v):
    return lax.rsqrt(jnp.mean(v * v, axis=-1, keepdims=True) + EPS)


ROW_GROUPS = 2


def rms_matmul(x, gain, w, *, name, out_dtype, transposed=False, after=None, rope=None):
    M, K = x.shape
    N = w.shape[0] if transposed else w.shape[1]
    tm = min(512, M)
    order = [] if after is None else [after]
    tables = [] if rope is None else list(rope)
    rot_spec = [] if rope is None else [pl.BlockSpec((tm, MAIN_W), lambda i: (i, 0))]
    rot_shape = [] if rope is None else [_sds((M, MAIN_W), f32)]

    def body(x_ref, g_ref, w_ref, *refs):
        z_ref, h_ref = refs[len(tables) + len(order):][:2]
        groups = [slice(g * tm // ROW_GROUPS, (g + 1) * tm // ROW_GROUPS) for g in range(ROW_GROUPS)]
        hs = []
        for rows in groups:
            xv = x_ref[rows, :]
            hs.append((xv * _rms_r(xv) * g_ref[...]).astype(bf16))
            h_ref[rows, :] = hs[-1]
        zs = [lax.dot_general(h, w_ref[...], NT if transposed else NN, preferred_element_type=f32) for h in hs]
        for rows, z in zip(groups, zs):
            z_ref[rows, :] = z.astype(z_ref.dtype)
            if tables:
                c = jnp.tile(refs[0][rows, :], (1, MAIN_W // 128))
                s = jnp.tile(refs[1][rows, :], (1, MAIN_W // 128))
                zr = z[:, :MAIN_W]
                refs[-1][rows, :] = zr * c + _swap_halves(zr) * s

    tab = pl.BlockSpec((tm, 128), lambda i: (i, 0))
    return pl.pallas_call(
        body, name=name, grid=(M // tm,),
        in_specs=[pl.BlockSpec((tm, K), lambda i: (i, 0)),
                  pl.BlockSpec((1, K), lambda i: (0, 0)),
                  pl.BlockSpec(w.shape, lambda i: (0, 0))] + [tab] * len(tables)
        + [pl.BlockSpec(memory_space=pl.ANY)] * len(order),
        out_specs=[pl.BlockSpec((tm, N), lambda i: (i, 0)), pl.BlockSpec((tm, K), lambda i: (i, 0))] + rot_spec,
        out_shape=[_sds((M, N), out_dtype), _sds((M, K), bf16)] + rot_shape,
        compiler_params=_params(("parallel",)),
    )(x, gain, w, *tables, *order)


def matmul_rms_res(a, w, gain, res, *, name, target=None, after=None):
    M, K = a.shape
    N = w.shape[1]
    tm = min(512, M)
    goal = [] if target is None else [target]
    order = [] if after is None else [after]

    def body(a_ref, w_ref, g_ref, r_ref, *refs):
        y_ref, x_ref = refs[len(goal) + len(order):][:2]
        groups = [slice(g * tm // ROW_GROUPS, (g + 1) * tm // ROW_GROUPS) for g in range(ROW_GROUPS)]
        ys = [jnp.dot(a_ref[rows, :], w_ref[...], preferred_element_type=f32) for rows in groups]
        sq = 0.0
        for rows, y in zip(groups, ys):
            y_ref[rows, :] = y.astype(bf16)
            x = r_ref[rows, :] + y * _rms_r(y) * g_ref[...]
            if goal:
                e = x - refs[0][rows, :]
                x = e * (1.0 / N)
                sq = sq + jnp.sum(jnp.sum(e * e, axis=0, keepdims=True), axis=1, keepdims=True)
            x_ref[rows, :] = x
        if goal:
            _accumulate(refs[-1], sq)

    row = pl.BlockSpec((tm, N), lambda i: (i, 0))
    return pl.pallas_call(
        body, name=name, grid=(M // tm,),
        in_specs=[pl.BlockSpec((tm, K), lambda i: (i, 0)),
                  pl.BlockSpec((K, N), lambda i: (0, 0)),
                  pl.BlockSpec((1, N), lambda i: (0, 0)),
                  row] + [row] * len(goal) + [pl.BlockSpec(memory_space=pl.ANY)] * len(order),
        out_specs=[row, row] + [pl.BlockSpec((8, 128), lambda i: (0, 0))] * len(goal),
        out_shape=[_sds((M, N), bf16), _sds((M, N), f32)] + [_sds((8, 128), f32)] * len(goal),
        compiler_params=_params(("arbitrary",) if goal else ("parallel",)),
    )(a, w, gain, res, *goal, *order)


def matmul(a, b, dims, *, name, out_dtype):
    if dims is TN:
        K, M = a.shape
        tm = _tile(M, (512, 256, 128))
        a_spec = pl.BlockSpec((K, tm), lambda i: (0, i))
    else:
        M, K = a.shape
        tm = _tile(M, (1024, 512, 256, 128))
        a_spec = pl.BlockSpec((tm, K), lambda i: (i, 0))
    N = b.shape[0] if dims is NT else b.shape[1]

    def body(a_ref, b_ref, o_ref):
        o_ref[...] = lax.dot_general(a_ref[...].astype(bf16), b_ref[...].astype(bf16), dims,
                                     preferred_element_type=f32).astype(o_ref.dtype)

    return pl.pallas_call(
        body, name=name, grid=(M // tm,),
        in_specs=[a_spec, pl.BlockSpec(b.shape, lambda i: (0, 0))],
        out_specs=pl.BlockSpec((tm, N), lambda i: (i, 0)),
        out_shape=_sds((M, N), out_dtype),
        compiler_params=_params(("parallel",)),
    )(a, b)


def rms_gate_up(x, gain, wt, *, name):
    M, K = x.shape
    tm = min(2048, M)
    tn = _tile(D_FF, (256, 128))
    nj = D_FF // tn

    def body(x_ref, gn_ref, wg_ref, wu_ref, g_ref, u_ref, a_ref, h_ref):
        @pl.when(pl.program_id(1) == 0)
        def _():
            xv = x_ref[...]
            h_ref[...] = (xv * _rms_r(xv) * gn_ref[...]).astype(bf16)

        h = h_ref[...]
        g = lax.dot_general(h, wg_ref[...], NT, preferred_element_type=f32).astype(bf16)
        u = lax.dot_general(h, wu_ref[...], NT, preferred_element_type=f32).astype(bf16)
        g_ref[...] = g
        u_ref[...] = u
        a_ref[...] = g * (1.0 / (1.0 + jnp.exp(-g))) * u

    col = pl.BlockSpec((tm, tn), lambda i, j: (i, j))
    return pl.pallas_call(
        body, name=name, grid=(M // tm, nj),
        in_specs=[pl.BlockSpec((tm, K), lambda i, j: (i, 0)),
                  pl.BlockSpec((1, K), lambda i, j: (0, 0)),
                  pl.BlockSpec((tn, K), lambda i, j: (j, 0)),
                  pl.BlockSpec((tn, K), lambda i, j: (j + nj, 0))],
        out_specs=[col, col, col, pl.BlockSpec((tm, K), lambda i, j: (i, 0))],
        out_shape=[_sds((M, D_FF), bf16)] * 3 + [_sds((M, K), bf16)],
        compiler_params=_params(("parallel", "arbitrary")),
    )(x, gain, wt, wt)


def _rms_bwd_math(yv, gain, dn):
    r = _rms_r(yv)
    q = dn * gain
    dy = r * q - yv * (r * r * r) * jnp.mean(q * yv, axis=-1, keepdims=True)
    return dy, jnp.sum(dn * yv * r, axis=0, keepdims=True)


def _accumulate(ref, val):
    @pl.when(pl.program_id(0) == 0)
    def _():
        ref[...] = jnp.zeros_like(ref)

    ref[...] += val


def down_bwd(y, gain, dn, w_down, g, u, *, name, after=None):
    M, K = y.shape
    tm = min(512, M)
    order = [] if after is None else [after]

    def body(y_ref, gn_ref, dn_ref, w_ref, g_ref, u_ref, *refs):
        dy_ref, o_ref, dg_ref = refs[len(order):]
        groups = [slice(a * tm // ROW_GROUPS, (a + 1) * tm // ROW_GROUPS) for a in range(ROW_GROUPS)]
        dys, dgain = [], 0.0
        for rows in groups:
            dy, part = _rms_bwd_math(y_ref[rows, :].astype(f32), gn_ref[...], dn_ref[rows, :])
            dys.append(dy.astype(bf16))
            dy_ref[rows, :] = dys[-1]
            dgain = dgain + part
        _accumulate(dg_ref, dgain)
        das = [lax.dot_general(dy, w_ref[...], NT, preferred_element_type=f32).astype(bf16) for dy in dys]
        for rows, da in zip(groups, das):
            g = g_ref[rows, :]
            s = 1.0 / (1.0 + jnp.exp(-g))
            o_ref[rows, :D_FF] = da * u_ref[rows, :] * s * (1.0 + g * (1.0 - s))
            o_ref[rows, D_FF:] = da * g * s

    row = pl.BlockSpec((tm, K), lambda i: (i, 0))
    vec = pl.BlockSpec((1, K), lambda i: (0, 0))
    wide = pl.BlockSpec((tm, D_FF), lambda i: (i, 0))
    return pl.pallas_call(
        body, name=name, grid=(M // tm,),
        in_specs=[row, vec, row, pl.BlockSpec((D_FF, K), lambda i: (0, 0)), wide, wide]
        + [pl.BlockSpec(memory_space=pl.ANY)] * len(order),
        out_specs=[row, pl.BlockSpec((tm, 2 * D_FF), lambda i: (i, 0)), vec],
        out_shape=[_sds((M, K), bf16), _sds((M, 2 * D_FF), bf16), _sds((1, K), f32)],
        compiler_params=_params(("arbitrary",)),
    )(y, gain, dn, w_down, g, u, *order)


def rms_bwd_matmul(y, gain, dn, w, dims, *, name, after=None):
    M, K = y.shape
    N = w.shape[0] if dims is NT else w.shape[1]
    tm = min(1024, M)
    order = [] if after is None else [after]

    def body(y_ref, gn_ref, dn_ref, w_ref, *refs):
        dy_ref, o_ref, dg_ref = refs[len(order):]
        groups = [slice(g * tm // ROW_GROUPS, (g + 1) * tm // ROW_GROUPS) for g in range(ROW_GROUPS)]
        dys, dgain = [], 0.0
        for rows in groups:
            dy, part = _rms_bwd_math(y_ref[rows, :].astype(f32), gn_ref[...], dn_ref[rows, :].astype(f32))
            dys.append(dy.astype(bf16))
            dy_ref[rows, :] = dys[-1]
            dgain = dgain + part
        _accumulate(dg_ref, dgain)
        for rows, dy in zip(groups, dys):
            o_ref[rows, :] = lax.dot_general(dy, w_ref[...], dims, preferred_element_type=f32).astype(bf16)

    row = pl.BlockSpec((tm, K), lambda i: (i, 0))
    vec = pl.BlockSpec((1, K), lambda i: (0, 0))
    return pl.pallas_call(
        body, name=name, grid=(M // tm,),
        in_specs=[row, vec, row, pl.BlockSpec(w.shape, lambda i: (0, 0))]
        + [pl.BlockSpec(memory_space=pl.ANY)] * len(order),
        out_specs=[row, pl.BlockSpec((tm, N), lambda i: (i, 0)), vec],
        out_shape=[_sds((M, K), bf16), _sds((M, N), bf16), _sds((1, K), f32)],
        compiler_params=_params(("arbitrary",)),
    )(y, gain, dn, w, *order)


def matmul_rms_bwd(a, b, dims, y, gain, res, *, name, after=None):
    M, K = a.shape
    N = y.shape[1]
    tm = min(512, M)
    order = [] if after is None else [after]

    def body(a_ref, b_ref, y_ref, gn_ref, r_ref, *refs):
        dx_ref, dg_ref = refs[len(order):]
        groups = [slice(g * tm // ROW_GROUPS, (g + 1) * tm // ROW_GROUPS) for g in range(ROW_GROUPS)]
        dns = [lax.dot_general(a_ref[rows, :].astype(bf16), b_ref[...], dims, preferred_element_type=f32)
               for rows in groups]
        dgain = 0.0
        for rows, dn in zip(groups, dns):
            dy, part = _rms_bwd_math(y_ref[rows, :], gn_ref[...], dn)
            dx_ref[rows, :] = dy + r_ref[rows, :]
            dgain = dgain + part
        _accumulate(dg_ref, dgain)

    row = pl.BlockSpec((tm, N), lambda i: (i, 0))
    vec = pl.BlockSpec((1, N), lambda i: (0, 0))
    return pl.pallas_call(
        body, name=name, grid=(M // tm,),
        in_specs=[pl.BlockSpec((tm, K), lambda i: (i, 0)), pl.BlockSpec(b.shape, lambda i: (0, 0)), row, vec, row]
        + [pl.BlockSpec(memory_space=pl.ANY)] * len(order),
        out_specs=[row, vec],
        out_shape=[_sds((M, N), f32), _sds((1, N), f32)],
        compiler_params=_params(("arbitrary",)),
    )(a, b, y, gain, res, *order)


def rms_bwd(y, gain, dn, res, *, name, out_dtype, after=None):
    M, N = y.shape
    tm = min(512, M)
    has_res = res is not None
    order = [] if after is None else [after]

    def body(*refs):
        y_ref, g_ref, dn_ref = refs[:3]
        r_ref = refs[3] if has_res else None
        dy_ref, dg_ref = refs[-2:]
        dy, dgain = _rms_bwd_math(y_ref[...].astype(f32), g_ref[...], dn_ref[...].astype(f32))
        if has_res:
            dy = dy + r_ref[...]
        dy_ref[...] = dy.astype(dy_ref.dtype)
        _accumulate(dg_ref, dgain)

    row = pl.BlockSpec((tm, N), lambda i: (i, 0))
    vec = pl.BlockSpec((1, N), lambda i: (0, 0))
    args = [y, gain, dn] + ([res] if has_res else []) + order
    return pl.pallas_call(
        body, name=name, grid=(M // tm,),
        in_specs=[row, vec, row] + ([row] if has_res else []) + [pl.BlockSpec(memory_space=pl.ANY)] * len(order),
        out_specs=[row, vec],
        out_shape=[_sds((M, N), out_dtype), _sds((1, N), f32)],
        compiler_params=_params(("arbitrary",)),
    )(*args)


def _pool_select(a1, a2, a3, a4):
    col = lax.broadcasted_iota(jnp.int32, (1, MAIN_W), 1) // POOL_GROUP
    return jnp.where(col == 0, a1, jnp.where(col == 1, a2, jnp.where(col == 2, a3, a4)))


def _pool_count(t):
    col = lax.broadcasted_iota(jnp.int32, (1, MAIN_W), 1) // POOL_GROUP
    win = jnp.where(col == 0, 2, jnp.where(col == 1, 4, jnp.where(col == 2, 8, 16)))
    return jnp.minimum(t + 1, win).astype(f32)


def pool_fwd(z, wbd, scale, *, name):
    M = z.shape[0]
    tm = 512
    nper = SEQ // tm
    hb = tm // POOL_HALO

    def body(zc_ref, zh_ref, w_ref, s_ref, p_ref, y_ref):
        i = pl.program_id(0)
        seq_blk = i % nper
        halo = jnp.where(seq_blk == 0, 0.0, zh_ref[...].astype(f32))
        u = zc_ref[...].astype(f32)
        ext = jnp.concatenate([halo, u], axis=0)
        a1 = ext + pltpu.roll(ext, 1, 0)
        a2 = a1 + pltpu.roll(a1, 2, 0)
        a3 = a2 + pltpu.roll(a2, 4, 0)
        a4 = a3 + pltpu.roll(a3, 8, 0)
        sums = _pool_select(a1, a2, a3, a4)[POOL_HALO:]
        t = seq_blk * tm + lax.broadcasted_iota(jnp.int32, (tm, 1), 0)
        p = (sums / _pool_count(t) - u).astype(bf16)
        p_ref[...] = p
        y_ref[...] = (jnp.dot(p, w_ref[...], preferred_element_type=f32) * s_ref[...]).astype(bf16)

    return pl.pallas_call(
        body, name=name, grid=(M // tm,),
        in_specs=[pl.BlockSpec((tm, MAIN_W), lambda i: (i, 0)),
                  pl.BlockSpec((POOL_HALO, MAIN_W), lambda i: (jnp.maximum(i * hb - 1, 0), 0)),
                  pl.BlockSpec((MAIN_W, MAIN_W), lambda i: (0, 0)),
                  pl.BlockSpec((1, MAIN_W), lambda i: (0, 0))],
        out_specs=[pl.BlockSpec((tm, MAIN_W), lambda i: (i, 0)),
                   pl.BlockSpec((tm, MAIN_W), lambda i: (i, 0))],
        out_shape=[_sds((M, MAIN_W), bf16), _sds((M, D_MODEL), bf16)],
        compiler_params=_params(("parallel",)),
    )(z, z, wbd, scale)


def pool_bwd(dyc, p, wbd, scale, *, name):
    M = p.shape[0]
    tm = 512
    nper = SEQ // tm
    hb = tm // POOL_HALO
    last_hb = M // POOL_HALO - 1

    def body(dy_ref, dyh_ref, p_ref, w_ref, s_ref, dz_ref, dw_ref, ds_ref):
        i = pl.program_id(0)
        seq_blk = i % nper
        dy = dy_ref[...].astype(f32)
        pv = p_ref[...]
        w = w_ref[...]
        sc = s_ref[...]

        @pl.when(i == 0)
        def _():
            dw_ref[...] = jnp.zeros_like(dw_ref)
            ds_ref[...] = jnp.zeros_like(ds_ref)

        v = jnp.dot(pv, w, preferred_element_type=f32)
        ds_ref[...] += jnp.sum(dy * v, axis=0, keepdims=True)
        dv = (dy * sc).astype(bf16)
        dw_ref[...] += lax.dot_general(pv, dv, TN, preferred_element_type=f32)
        dp = lax.dot_general(dv, w, NT, preferred_element_type=f32)
        dvh = jnp.where(seq_blk == nper - 1, 0.0, dyh_ref[...].astype(f32) * sc).astype(bf16)
        dph = lax.dot_general(dvh, w, NT, preferred_element_type=f32)
        ext = jnp.concatenate([dp, dph], axis=0)
        n = tm + POOL_HALO
        t = seq_blk * tm + lax.broadcasted_iota(jnp.int32, (n, 1), 0)
        e = ext / _pool_count(t)
        b1 = e + pltpu.roll(e, n - 1, 0)
        b2 = b1 + pltpu.roll(b1, n - 2, 0)
        b3 = b2 + pltpu.roll(b2, n - 4, 0)
        b4 = b3 + pltpu.roll(b3, n - 8, 0)
        dz_ref[...] = (_pool_select(b1, b2, b3, b4)[:tm] - dp).astype(dz_ref.dtype)

    return pl.pallas_call(
        body, name=name, grid=(M // tm,),
        in_specs=[pl.BlockSpec((tm, MAIN_W), lambda i: (i, 0)),
                  pl.BlockSpec((POOL_HALO, MAIN_W), lambda i: (jnp.minimum((i + 1) * hb, last_hb), 0)),
                  pl.BlockSpec((tm, MAIN_W), lambda i: (i, 0)),
                  pl.BlockSpec((MAIN_W, MAIN_W), lambda i: (0, 0)),
                  pl.BlockSpec((1, MAIN_W), lambda i: (0, 0))],
        out_specs=[pl.BlockSpec((tm, MAIN_W), lambda i: (i, 0)),
                   pl.BlockSpec((MAIN_W, MAIN_W), lambda i: (0, 0)),
                   pl.BlockSpec((1, MAIN_W), lambda i: (0, 0))],
        out_shape=[_sds((M, D_MODEL), bf16), _sds((MAIN_W, MAIN_W), f32), _sds((1, MAIN_W), f32)],
        compiler_params=_params(("arbitrary",)),
    )(dyc, dyc, p, wbd, scale)


def _mem_heads(q, kv):
    first = _first_head()
    for pr in range(N_MEM_HEADS // 2):
        cols = slice(pr * PAIR_W, (pr + 1) * PAIR_W)
        qp = q[:, cols] * SCALE
        kp = kv[:, cols]
        vp = kv[:, MEM_W + pr * PAIR_W: MEM_W + (pr + 1) * PAIR_W]
        for hh in range(2):
            lm = first if hh == 0 else ~first
            qm = jnp.where(lm, qp, 0.0).astype(bf16)
            s = lax.dot_general(qm, kp, NT, preferred_element_type=f32)
            e = jnp.exp(s - jnp.max(s, axis=-1, keepdims=True))
            yield lm, qm, kp, vp, e, jnp.sum(e, axis=-1, keepdims=True)


def memattn_fwd(z, kvm, ycat, *, name, n_seq):
    M = z.shape[0]
    tq = 1024
    nq = SEQ // tq

    def body(q_ref, kv_ref, _, o_ref):
        first = _first_head()
        outs = []
        for lm, _, _, vp, e, l in _mem_heads(q_ref[...], kv_ref[...]):
            outs.append(jnp.dot(e.astype(bf16), vp, preferred_element_type=f32) * (1.0 / l))
        pairs = [jnp.where(first, outs[2 * pr], outs[2 * pr + 1]) for pr in range(N_MEM_HEADS // 2)]
        o_ref[...] = jnp.concatenate(pairs, axis=1).astype(bf16)

    return pl.pallas_call(
        body, name=name, grid=(n_seq, nq),
        in_specs=[pl.BlockSpec((tq, MEM_W), lambda b, i: (b * nq + i, 3)),
                  pl.BlockSpec((N_MEM, 2 * MEM_W), lambda b, i: (b, 0)),
                  pl.BlockSpec(memory_space=pl.ANY)],
        out_specs=pl.BlockSpec((tq, MEM_W), lambda b, i: (b * nq + i, 3)),
        out_shape=_sds((M, D_MODEL), bf16),
        input_output_aliases={2: 0},
        compiler_params=_params(("parallel", "parallel")),
    )(z, kvm, ycat)


def memattn_bwd(z, kvm, dyc, dz, *, name, n_seq):
    M = z.shape[0]
    tq = 1024
    nq = SEQ // tq

    def body(q_ref, kv_ref, dy_ref, _, dq_ref, dkv_ref):
        first = _first_head()
        dy = dy_ref[...].astype(f32)
        dqs, dks, dvs = [], [], []
        for h, (lm, qm, kp, vp, e, l) in enumerate(_mem_heads(q_ref[...], kv_ref[...])):
            pr = h // 2
            p = e * (1.0 / l)
            dym = jnp.where(lm, dy[:, pr * PAIR_W:(pr + 1) * PAIR_W], 0.0).astype(bf16)
            dp = lax.dot_general(dym, vp, NT, preferred_element_type=f32)
            ds = (p * (dp - jnp.sum(dp * p, axis=-1, keepdims=True))).astype(bf16)
            dqs.append(jnp.dot(ds, kp, preferred_element_type=f32) * SCALE)
            dk = lax.dot_general(ds, qm, TN, preferred_element_type=f32)
            dv = lax.dot_general(p.astype(bf16), dym, TN, preferred_element_type=f32)
            if h % 2 == 0:
                dks.append(dk)
                dvs.append(dv)
            else:
                dks[pr] = dks[pr] + dk
                dvs[pr] = dvs[pr] + dv
        pairs = [jnp.where(first, dqs[2 * pr], dqs[2 * pr + 1]) for pr in range(N_MEM_HEADS // 2)]
        dq_ref[...] = jnp.concatenate(pairs, axis=1).astype(bf16)

        @pl.when(pl.program_id(1) == 0)
        def _():
            dkv_ref[...] = jnp.zeros_like(dkv_ref)

        dkv_ref[...] += jnp.concatenate(dks + dvs, axis=1)

    return pl.pallas_call(
        body, name=name, grid=(n_seq, nq),
        in_specs=[pl.BlockSpec((tq, MEM_W), lambda b, i: (b * nq + i, 3)),
                  pl.BlockSpec((N_MEM, 2 * MEM_W), lambda b, i: (b, 0)),
                  pl.BlockSpec((tq, MEM_W), lambda b, i: (b * nq + i, 3)),
                  pl.BlockSpec(memory_space=pl.ANY)],
        out_specs=[pl.BlockSpec((tq, MEM_W), lambda b, i: (b * nq + i, 3)),
                   pl.BlockSpec((N_MEM, 2 * MEM_W), lambda b, i: (b, 0))],
        out_shape=[_sds((M, D_MODEL), bf16), _sds((n_seq * N_MEM, 2 * MEM_W), f32)],
        input_output_aliases={3: 0},
        compiler_params=_params(("parallel", "arbitrary")),
    )(z, kvm, dyc, dz)


def rope_tables(pos, *, name):
    M = pos.shape[0]
    tm = min(1024, M)
    half = HEAD_DIM // 2
    inv = ROPE_THETA ** (-np.arange(half, dtype=np.float64) / half)
    inv128 = jnp.asarray(np.tile(inv, 4)[None, :], f32)
    sign128 = jnp.asarray(np.tile(np.concatenate([-np.ones(half), np.ones(half)]), 2)[None, :], f32)

    def body(p_ref, f_ref, s_ref, cos_ref, sin_ref):
        ang = p_ref[...] * f_ref[...]
        cos_ref[...] = jnp.cos(ang)
        sin_ref[...] = jnp.sin(ang) * s_ref[...]

    return pl.pallas_call(
        body, name=name, grid=(M // tm,),
        in_specs=[pl.BlockSpec((tm, 1), lambda i: (i, 0)),
                  pl.BlockSpec((1, 128), lambda i: (0, 0)),
                  pl.BlockSpec((1, 128), lambda i: (0, 0))],
        out_specs=[pl.BlockSpec((tm, 128), lambda i: (i, 0)),
                   pl.BlockSpec((tm, 128), lambda i: (i, 0))],
        out_shape=[_sds((M, 128), f32), _sds((M, 128), f32)],
        compiler_params=_params(("parallel",)),
    )(pos, inv128, sign128)


def _swap_halves(x):
    w = x.shape[1]
    first = (lax.broadcasted_iota(jnp.int32, (1, w), 1) % HEAD_DIM) < (HEAD_DIM // 2)
    return jnp.where(first, pltpu.roll(x, w - HEAD_DIM // 2, 1), pltpu.roll(x, HEAD_DIM // 2, 1))


def group_sum(groups, cos, sin, *, name, rotate, width, col_block=0, into=None):
    M = groups[0][0].shape[0]
    tm = min(512, M)
    counts = [len(g) for g in groups]
    flat = [a for g in groups for a in g]
    extra = [] if into is None else [into]

    def body(*refs):
        part_refs = refs[:len(flat)]
        c_ref, s_ref = refs[len(flat):len(flat) + 2]
        o_ref = refs[-1]
        cols, k = [], 0
        for n in counts:
            acc = part_refs[k][...]
            for r in part_refs[k + 1:k + n]:
                acc = acc + r[...]
            cols.append(acc)
            k += n
        d = jnp.concatenate(cols, axis=1)
        if rotate:
            c = jnp.tile(c_ref[...], (1, MAIN_W // 128))
            s = jnp.tile(s_ref[...], (1, MAIN_W // 128))
            d = d * c - _swap_halves(d) * s
        o_ref[...] = d.astype(bf16)

    part = pl.BlockSpec((tm, GROUP_W), lambda i: (i, 0))
    tab = pl.BlockSpec((tm, 128), lambda i: (i, 0))
    return pl.pallas_call(
        body, name=name, grid=(M // tm,),
        in_specs=[part] * len(flat) + [tab, tab] + [pl.BlockSpec(memory_space=pl.ANY)] * len(extra),
        out_specs=pl.BlockSpec((tm, MAIN_W), lambda i: (i, col_block)),
        out_shape=_sds((M, width), bf16),
        input_output_aliases={len(flat) + 2: 0} if extra else {},
        compiler_params=_params(("parallel",)),
    )(*flat, cos, sin, *extra)


PAIR_W = 2 * HEAD_DIM
MIN_BLOCKS = 16


FWD_TOGETHER = 4
BWD_TOGETHER = 4


def _dil_geometry(dil):
    nsub = max(dil, MIN_BLOCKS)
    tb = BAND * nsub
    return nsub, tb, SEQ // tb


REGROUP = 4


class _Regrouped:
    def __init__(self, ref):
        self.ref = ref
        self.shape = ref.shape

    def fill(self, src):
        q = self.shape[0] // REGROUP
        for r0 in range(REGROUP):
            self.ref[r0 * q:(r0 + 1) * q, :] = src[pl.ds(r0, q, stride=REGROUP), :]

    def drain(self, dst):
        q = self.shape[0] // REGROUP
        for r0 in range(REGROUP):
            dst[pl.ds(r0, q, stride=REGROUP), :] = self.ref[r0 * q:(r0 + 1) * q, :]

    def rows(self, sub, dil):
        nl, r = divmod(sub, dil)
        start = (r % REGROUP) * (self.shape[0] // REGROUP) + r // REGROUP + nl * BAND * (dil // REGROUP)
        return pl.ds(start, BAND, stride=dil // REGROUP)


def _regroups(dil):
    return dil % (4 * REGROUP) == 0


def _rows(ref, sub, dil):
    if isinstance(ref, _Regrouped):
        return ref.ref[ref.rows(sub, dil), :]
    if dil == 1:
        return ref[sub * BAND:(sub + 1) * BAND, :]
    nl, r = divmod(sub, dil)
    return ref[pl.ds(nl * BAND * dil + r, BAND, stride=dil), :]


def _store_rows(ref, sub, dil, val):
    if isinstance(ref, _Regrouped):
        ref.ref[ref.rows(sub, dil), :] = val
    elif dil == 1:
        ref[sub * BAND:(sub + 1) * BAND, :] = val
    else:
        nl, r = divmod(sub, dil)
        ref[pl.ds(nl * BAND * dil + r, BAND, stride=dil), :] = val


def _keys(prev_ref, own_ref, sub, dil):
    nsub = own_ref.shape[0] // BAND
    if sub >= dil:
        prev = _rows(own_ref, sub - dil, dil)
    elif prev_ref is None:
        return _rows(own_ref, sub, dil)
    else:
        prev = _rows(prev_ref, nsub - dil + sub, dil)
    return jnp.concatenate([prev, _rows(own_ref, sub, dil)], axis=0)


def _band_mask(nkeys, has_prev):
    i = lax.broadcasted_iota(jnp.int32, (BAND, nkeys), 0)
    j = lax.broadcasted_iota(jnp.int32, (BAND, nkeys), 1)
    if nkeys == BAND:
        return j <= i
    return (j >= i) & (j <= i + BAND) & (has_prev | (j >= BAND))


def _first_head():
    return lax.broadcasted_iota(jnp.int32, (1, PAIR_W), 1) < HEAD_DIM


def _col(x, hh):
    return x[:, hh * HEAD_DIM:hh * HEAD_DIM + 1]


def _pair_spec(tb, nblk, col0, which):
    def idx(b, p, i):
        if which < 0:
            i = jnp.maximum(i - 1, 0)
        elif which > 0:
            i = jnp.minimum(i + 1, nblk - 1)
        return (b * nblk + i, col0 + p)
    return pl.BlockSpec((tb, PAIR_W), idx)


def dil_fwd(q, k, kv, g, dil, *, name, n_seq):
    M = q.shape[0]
    nsub, tb, nblk = _dil_geometry(dil)
    with_prev = nblk > 1

    regroup = _regroups(dil)
    assert not (regroup and with_prev)

    def body(*refs):
        if with_prev:
            q_ref, ko_ref, vo_ref, kp_ref, vp_ref, o_ref, l_ref = refs
        else:
            (q_ref, ko_ref, vo_ref, o_ref, l_ref), kp_ref, vp_ref = refs[:5], None, None
        outs_to = ()
        if regroup:
            copies = [_Regrouped(s) for s in refs[5:]]
            for c, src in zip(copies, (q_ref, ko_ref, vo_ref)):
                c.fill(src)
            outs_to = ((copies[3], o_ref), (copies[4], l_ref))
            q_ref, ko_ref, vo_ref, o_ref, l_ref = copies
        first = _first_head()
        blk = pl.program_id(2)
        for sub0 in range(0, nsub, FWD_TOGETHER):
            subs = range(sub0, sub0 + FWD_TOGETHER)
            scores, values = [], []
            for sub in subs:
                qs = _rows(q_ref, sub, dil) * SCALE
                kc = _keys(kp_ref, ko_ref, sub, dil).astype(bf16)
                values.append(_keys(vp_ref, vo_ref, sub, dil).astype(bf16))
                has_prev = True if sub >= dil else blk > 0
                mask = _band_mask(kc.shape[0], has_prev)
                for hh in range(2):
                    qm = jnp.where(first if hh == 0 else ~first, qs, 0.0).astype(bf16)
                    scores.append(jnp.where(mask, lax.dot_general(qm, kc, NT, preferred_element_type=f32), NEG))
            soft = []
            for s in scores:
                m = jnp.max(s, axis=-1, keepdims=True)
                e = jnp.exp(s - m)
                l = jnp.sum(e, axis=-1, keepdims=True)
                soft.append((e.astype(bf16), 1.0 / l, jnp.broadcast_to(m + jnp.log(l), (BAND, PAIR_W))))
            outs = [jnp.dot(e, values[n // 2], preferred_element_type=f32) * inv for n, (e, inv, _) in enumerate(soft)]
            for n, sub in enumerate(subs):
                _store_rows(o_ref, sub, dil, jnp.where(first, outs[2 * n], outs[2 * n + 1]))
                _store_rows(l_ref, sub, dil, jnp.where(first, soft[2 * n][2], soft[2 * n + 1][2]))
        for c, dst in outs_to:
            c.drain(dst)

    ins = [(q, 2 * g, 0), (k, 2 * g, 0), (kv, 6 + 2 * g, 0)]
    if with_prev:
        ins += [(k, 2 * g, -1), (kv, 6 + 2 * g, -1)]
    out = _pair_spec(tb, nblk, 0, 0)
    return pl.pallas_call(
        body, name=name, grid=(n_seq, 2, nblk),
        in_specs=[_pair_spec(tb, nblk, c, w) for _, c, w in ins],
        out_specs=[out, out],
        out_shape=[_sds((M, GROUP_W), f32)] * 2,
        scratch_shapes=[pltpu.VMEM((tb, PAIR_W), f32)] * (5 if regroup else 0),
        compiler_params=_params(("parallel", "parallel", "arbitrary")),
    )(*[a for a, _, _ in ins])


def combine_fwd(os_, lses, *, name):
    M = os_[0].shape[0]
    tm = min(512, M)

    def body(o0, o1, o2, l0, l1, l2, y_ref):
        ls = [l0[...], l1[...], l2[...]]
        m = jnp.maximum(jnp.maximum(ls[0], ls[1]), ls[2])
        es = [jnp.exp(l - m) for l in ls]
        inv = 1.0 / (es[0] + es[1] + es[2])
        y_ref[...] = jnp.concatenate([o[...] * e * inv for o, e in zip((o0, o1, o2), es)], axis=1).astype(bf16)

    part = pl.BlockSpec((tm, GROUP_W), lambda i: (i, 0))
    return pl.pallas_call(
        body, name=name, grid=(M // tm,),
        in_specs=[part] * 6,
        out_specs=pl.BlockSpec((tm, MAIN_W), lambda i: (i, 0)),
        out_shape=_sds((M, D_MODEL), bf16),
        compiler_params=_params(("parallel",)),
    )(*os_, *lses)


def combine_bwd(dyc, os_, lses, *, name):
    M = os_[0].shape[0]
    tm = min(512, M)

    def body(dy_ref, o0, o1, o2, l0, l1, l2, d0, d1, d2, c0, c1, c2):
        r = lax.broadcasted_iota(jnp.int32, (GROUP_W, GROUP_W), 0) // HEAD_DIM
        c = lax.broadcasted_iota(jnp.int32, (GROUP_W, GROUP_W), 1) // HEAD_DIM
        ones = (r == c).astype(bf16)
        dy = dy_ref[...].astype(f32)
        ls = [l0[...], l1[...], l2[...]]
        m = jnp.maximum(jnp.maximum(ls[0], ls[1]), ls[2])
        es = [jnp.exp(l - m) for l in ls]
        inv = 1.0 / (es[0] + es[1] + es[2])
        total = 0.0
        alphas = []
        for g, (o, e, d_ref) in enumerate(zip((o0, o1, o2), es, (d0, d1, d2))):
            a = e * inv
            dyg = dy[:, g * GROUP_W:(g + 1) * GROUP_W]
            d_ref[...] = dyg * a
            prod = dyg * o[...]
            hi = prod.astype(bf16)
            lo = (prod - hi.astype(f32)).astype(bf16)
            dsum = jnp.dot(hi, ones, preferred_element_type=f32) + jnp.dot(lo, ones, preferred_element_type=f32)
            total = total + a * dsum
            alphas.append(a)
        for a, c_ref in zip(alphas, (c0, c1, c2)):
            c_ref[...] = -a * total

    part = pl.BlockSpec((tm, GROUP_W), lambda i: (i, 0))
    outs = pl.pallas_call(
        body, name=name, grid=(M // tm,),
        in_specs=[pl.BlockSpec((tm, MAIN_W), lambda i: (i, 0))] + [part] * 6,
        out_specs=[part] * 6,
        out_shape=[_sds((M, GROUP_W), f32)] * 6,
        compiler_params=_params(("parallel",)),
    )(dyc, *os_, *lses)
    return outs[:3], outs[3:]


def dil_bwd(q, k, kv, do, cc, lse, cos, sin, g, dil, *, name, n_seq, into=None):
    M = q.shape[0]
    nsub = SEQ // BAND
    per_res = nsub // dil
    extra = [] if into is None else [into]

    regroup = _regroups(dil)

    def body(q_ref, k_ref, v_ref, do_ref, c_ref, l_ref, cos_ref, sin_ref, *refs):
        dz_ref, dk_ref, dv_ref, dq_ref, *scratch = refs[len(extra):]
        dq_rows = dq_ref
        outs_to = ()
        if regroup:
            copies = [_Regrouped(s) for s in scratch]
            for c, src in zip(copies, (q_ref, k_ref, v_ref, do_ref, c_ref, l_ref)):
                c.fill(src)
            outs_to = tuple(zip(copies[6:], (dq_rows, dk_ref, dv_ref)))
            q_ref, k_ref, v_ref, do_ref, c_ref, l_ref, dq_ref, dk_ref, dv_ref = copies
        first = _first_head()
        order = [nl * dil + r for r in range(dil) for nl in range(per_res)]
        carry = None
        for at in range(0, nsub, BWD_TOGETHER):
            subs = order[at:at + BWD_TOGETHER]
            loaded, products = [], []
            for sub in subs:
                qs = _rows(q_ref, sub, dil) * SCALE
                dos = _rows(do_ref, sub, dil)
                kc = _keys(None, k_ref, sub, dil).astype(bf16)
                vc = _keys(None, v_ref, sub, dil).astype(bf16)
                mask = _band_mask(kc.shape[0], True)
                for hh in range(2):
                    lm = first if hh == 0 else ~first
                    qm = jnp.where(lm, qs, 0.0).astype(bf16)
                    dom = jnp.where(lm, dos, 0.0).astype(bf16)
                    loaded.append((qm, dom, kc))
                    products.append((jnp.where(mask, lax.dot_general(qm, kc, NT, preferred_element_type=f32), NEG),
                                     lax.dot_general(dom, vc, NT, preferred_element_type=f32)))
            weights = []
            for n, (s, dp) in enumerate(products):
                sub, hh = subs[n // 2], n % 2
                p = jnp.exp(s - _col(_rows(l_ref, sub, dil), hh))
                weights.append((p.astype(bf16), (p * (dp + _col(_rows(c_ref, sub, dil), hh))).astype(bf16)))
            results = []
            for (pb, ds), (qm, dom, kc) in zip(weights, loaded):
                results.append((jnp.dot(ds, kc, preferred_element_type=f32) * SCALE,
                                lax.dot_general(ds, qm, TN, preferred_element_type=f32),
                                lax.dot_general(pb, dom, TN, preferred_element_type=f32)))
            for n, sub in enumerate(subs):
                (dq0, dk0, dv0), (dq1, dk1, dv1) = results[2 * n], results[2 * n + 1]
                _store_rows(dq_ref, sub, dil, jnp.where(first, dq0, dq1))
                dkc, dvc = dk0 + dk1, dv0 + dv1
                if sub >= dil:
                    _store_rows(dk_ref, sub - dil, dil, carry[0] + dkc[:BAND])
                    _store_rows(dv_ref, sub - dil, dil, carry[1] + dvc[:BAND])
                    carry = (dkc[BAND:], dvc[BAND:])
                else:
                    carry = (dkc, dvc)
                if sub + dil >= nsub:
                    _store_rows(dk_ref, sub, dil, carry[0])
                    _store_rows(dv_ref, sub, dil, carry[1])
        for c, dst in outs_to:
            c.drain(dst)
        d = dq_rows[...]
        dz_ref[...] = (d * cos_ref[...] - _swap_halves(d) * sin_ref[...]).astype(bf16)

    def spec(col0):
        return pl.BlockSpec((SEQ, PAIR_W), lambda b, p: (b, col0 + p))

    tab = pl.BlockSpec((SEQ, PAIR_W), lambda b, p: (b, 0))
    out = spec(0)
    return pl.pallas_call(
        body, name=name, grid=(n_seq, 2),
        in_specs=[spec(2 * g), spec(2 * g), spec(6 + 2 * g), spec(0), spec(0), spec(0), tab, tab]
        + [pl.BlockSpec(memory_space=pl.ANY)] * len(extra),
        out_specs=[spec(2 * g), out, out],
        out_shape=[_sds((M, D_MODEL), bf16)] + [_sds((M, GROUP_W), f32)] * 2,
        input_output_aliases={8: 0} if extra else {},
        scratch_shapes=[pltpu.VMEM((SEQ, PAIR_W), f32)] * (10 if regroup else 1),
        compiler_params=_params(("parallel", "parallel")),
    )(q, k, kv, do, cc, lse, cos, sin, *extra)


def _blockdiag(wp):
    out = jnp.zeros((MAIN_W, MAIN_W), wp.dtype)
    for gi in range(len(POOL_WINDOWS)):
        sl = slice(gi * POOL_GROUP, (gi + 1) * POOL_GROUP)
        out = out.at[sl, sl].set(wp[gi])
    return out


def _unblockdiag(w):
    return jnp.stack([w[gi * POOL_GROUP:(gi + 1) * POOL_GROUP, gi * POOL_GROUP:(gi + 1) * POOL_GROUP]
                      for gi in range(len(POOL_WINDOWS))])


def local_step(x, mem, positions, target, P, layer_weights, kv_weight, emit_grads):
    n_seq = x.shape[0]
    M = n_seq * SEQ
    xs = x.reshape(M, D_MODEL)
    mems = mem.reshape(n_seq * N_MEM, D_MODEL)
    pos = positions.reshape(M, 1).astype(f32)
    cos, sin = rope_tables(pos, name="rope_tables")
    gains = P["norm_gains"]

    def gain(l, k):
        return gains[l, k].reshape(1, D_MODEL)

    saved = []
    kvs = None
    for l in range(DEPTH):
        W, started = layer_weights(l, "mix", xs)
        sv = {"x": xs, "W": W}
        z, h1, *qrot = rms_matmul(xs, gain(l, 0), W["w_in"], name=f"l{l}_in", out_dtype=bf16, after=started,
                                  rope=None if l < N_A_LAYERS else (cos, sin))
        kvm, mn = rms_matmul(mems, P["mem_norm"][l].reshape(1, D_MODEL), W["w_mem_kv"],
                             name=f"l{l}_memkv", out_dtype=bf16)
        sv.update(z=z, h1=h1, kvm=kvm, mn=mn)
        if l < N_A_LAYERS:
            wbd = _blockdiag(P["w_pool"][l].astype(bf16))
            psc = P["pool_scale"][l].reshape(1, MAIN_W)
            p, y_main = pool_fwd(z, wbd, psc, name=f"l{l}_pool")
            sv.update(p=p, wbd=wbd, psc=psc)
        else:
            (qrot,) = qrot
            os_, lses = [], []
            for g, (_, dil) in enumerate(DIL_PATTERNS):
                o, lse = dil_fwd(qrot, kvs["krot"], kvs["kv"], g, dil, name=f"l{l}_dil{g}", n_seq=n_seq)
                os_.append(o)
                lses.append(lse)
            y_main = combine_fwd(os_, lses, name=f"l{l}_comb")
            sv.update(qrot=qrot, os=os_, lses=lses)
        ycat = memattn_fwd(z, kvm, y_main, name=f"l{l}_memattn", n_seq=n_seq)
        y, x1 = matmul_rms_res(ycat, W["w_out"], gain(l, 1), xs, name=f"l{l}_out")
        W.update(layer_weights(l, "gu", x1)[0])
        fg, fu, a, h2 = rms_gate_up(x1, gain(l, 2), W["w_gate_up"], name=f"l{l}_gu")
        W_down, passing = layer_weights(l, "down", a)
        W.update(W_down)
        y2, x2, *sq = matmul_rms_res(a, W["w_down"], gain(l, 3), x1, name=f"l{l}_down", after=passing,
                                     target=target.reshape(M, D_MODEL) if l == DEPTH - 1 else None)
        sv.update(ycat=ycat, y=y, x1=x1, fg=fg, fu=fu, h2=h2, a=a, y2=y2)
        saved.append(sv)
        xs = x2
        if l == N_A_LAYERS - 1:
            w_kv = kv_weight(xs)
            kv, hkv, krot = rms_matmul(xs, P["kv_norm"].reshape(1, D_MODEL), w_kv, name="kv_proj", out_dtype=f32,
                                       transposed=True, rope=(cos, sin))
            kvs = {"kv": kv, "hkv": hkv, "krot": krot, "x": xs, "w_kv": w_kv}

    dx, (sq,) = xs, sq

    G = {"mem_norm": [None] * DEPTH, "norm_gains": [[None] * 4 for _ in range(DEPTH)],
         "pool_scale": [None] * N_A_LAYERS}
    dk_parts = [[] for _ in range(N_GROUPS)]
    dv_parts = [[] for _ in range(N_GROUPS)]
    emitted = None

    for l in reversed(range(DEPTH)):
        sv = saved[l]
        W = sv["W"]
        gw = {}
        dy2, dgu, G["norm_gains"][l][3] = down_bwd(sv["y2"], gain(l, 3), dx, W["w_down"], sv["fg"], sv["fu"],
                                                   name=f"l{l}_b_dgu", after=emitted)
        gw["w_down"] = matmul(sv["a"], dy2, TN, name=f"l{l}_b_wd", out_dtype=bf16)
        gw["w_gate_up"] = matmul(dgu, sv["h2"], TN, name=f"l{l}_b_wgu", out_dtype=bf16)
        emitted = emit_grads(l, "ffn", gw)
        dx1, G["norm_gains"][l][2] = matmul_rms_bwd(dgu, W["w_gate_up"], NN, sv["x1"], gain(l, 2), dx,
                                                    name=f"l{l}_b_dh2", after=emitted)
        gw = {}
        dy, dycat, G["norm_gains"][l][1] = rms_bwd_matmul(sv["y"], gain(l, 1), dx1, W["w_out"], NT,
                                                          name=f"l{l}_b_dycat", after=emitted)
        gw["w_out"] = matmul(sv["ycat"], dy, TN, name=f"l{l}_b_wout", out_dtype=bf16)
        if l < N_A_LAYERS:
            dz, dwbd, dps = pool_bwd(dycat, sv["p"], sv["wbd"], sv["psc"], name=f"l{l}_b_pool")
            gw["w_pool"] = _unblockdiag(dwbd).reshape(MAIN_W, POOL_GROUP).astype(bf16)
            G["pool_scale"][l] = dps.reshape(MAIN_W)
        else:
            dos, ccs = combine_bwd(dycat, sv["os"], sv["lses"], name=f"l{l}_b_comb")
            dz = None
            for g, (_, dil) in enumerate(DIL_PATTERNS):
                args = (sv["qrot"], kvs["krot"], kvs["kv"], dos[g], ccs[g], sv["lses"][g], cos, sin, g, dil)
                dz, dk, dv = dil_bwd(*args, name=f"l{l}_b_dil{g}", n_seq=n_seq, into=dz)
                dk_parts[g].append(dk)
                dv_parts[g].append(dv)
        dz, dkvm = memattn_bwd(sv["z"], sv["kvm"], dycat, dz, name=f"l{l}_b_memattn", n_seq=n_seq)
        gw["w_mem_kv"] = matmul(sv["mn"], dkvm, TN, name=f"l{l}_b_wmkv", out_dtype=bf16)
        _, G["mem_norm"][l] = matmul_rms_bwd(dkvm, W["w_mem_kv"], NT, mems, P["mem_norm"][l].reshape(1, D_MODEL),
                                             mems, name=f"l{l}_b_dmn")
        gw["w_in"] = matmul(sv["h1"], dz, TN, name=f"l{l}_b_win", out_dtype=bf16)
        if l != N_A_LAYERS:
            emitted = emit_grads(l, "mix", gw)
        dx, G["norm_gains"][l][0] = matmul_rms_bwd(dz, W["w_in"], NT, sv["x"], gain(l, 0), dx1, name=f"l{l}_b_dh1",
                                                   after=emitted)
        if l == N_A_LAYERS:
            dkv = group_sum(dk_parts, cos, sin, name="b_ropek", rotate=True, width=2 * MAIN_W)
            dkv = group_sum(dv_parts, cos, sin, name="b_sumv", rotate=False, width=2 * MAIN_W, col_block=1, into=dkv)
            gw["w_kv"] = matmul(dkv, kvs["hkv"], TN, name="b_wkv", out_dtype=bf16)
            dx, gkn = matmul_rms_bwd(dkv, kvs["w_kv"], NN, kvs["x"], P["kv_norm"].reshape(1, D_MODEL), dx,
                                     name="b_dhkv")
            G["kv_norm"] = gkn.reshape(D_MODEL)
            emitted = emit_grads(l, "mix", gw)

    small = {"pool_scale": jnp.stack(G["pool_scale"]),
             "mem_norm": jnp.concatenate(G["mem_norm"], axis=0),
             "norm_gains": jnp.stack([jnp.concatenate(r, axis=0) for r in G["norm_gains"]]),
             "kv_norm": G["kv_norm"]}
    return sq[0, 0], dx.reshape(n_seq, SEQ, D_MODEL), small, emitted


def to_bf16_layers(stacks, after, *, name):
    L, n = stacks[0].shape[0], len(stacks)

    def body(*refs):
        ins, outs = refs[:n], refs[n + 1:]
        for j in range(L):
            @pl.when(pl.program_id(0) == j)
            def _():
                for k in range(n):
                    outs[j * n + k][...] = ins[k][...].astype(bf16)

    outs = pl.pallas_call(
        body, name=name, grid=(L,),
        in_specs=[pl.BlockSpec((None,) + s.shape[1:], lambda l: (l, 0, 0)) for s in stacks]
        + [pl.BlockSpec(memory_space=pl.ANY)],
        out_specs=[pl.BlockSpec(s.shape[1:], lambda l: (0, 0)) for _ in range(L) for s in stacks],
        out_shape=[_sds(s.shape[1:], bf16) for _ in range(L) for s in stacks],
        compiler_params=_params(("arbitrary",)),
    )(*stacks, after)
    return [outs[j * n:(j + 1) * n] for j in range(L)]


def _peer(k):
    x, y, c = lax.axis_index("x"), lax.axis_index("y"), lax.axis_index("c")
    px = 1 - x if k & 4 else x
    py = 1 - y if k & 2 else y
    pc = 1 - c if k & 1 else c
    return (px, py, pc), 4 * px + 2 * py + pc


def _my_index():
    return 4 * lax.axis_index("x") + 2 * lax.axis_index("y") + lax.axis_index("c")


def _src_for(kinds, in_refs, i, idx):
    return in_refs[i] if kinds[i] == "gather" else in_refs[i].at[idx]


def _local_copies(kinds, in_refs, out_refs, local_sems):
    me = _my_index()
    return [pltpu.make_async_copy(_src_for(kinds, in_refs, i, me), out_refs[i].at[me], local_sems.at[i])
            for i in range(len(kinds))]


def _remote_copies(kinds, in_refs, out_refs, send_sems, recv_sems, *, arriving):
    me = _my_index()
    copies = []
    for k in range(1, N_DEV):
        dev, idx = _peer(k)
        for i in range(len(kinds)):
            j = i * (N_DEV - 1) + k - 1
            copies.append(pltpu.make_async_remote_copy(
                src_ref=_src_for(kinds, in_refs, i, idx), dst_ref=out_refs[i].at[idx if arriving else me],
                send_sem=send_sems.at[j], recv_sem=recv_sems.at[j], device_id=dev, device_id_type=MESH))
    return copies


def _out_shape(a, kind):
    return ((N_DEV,) + a.shape) if kind == "gather" else a.shape


def exchange(items, *, name, after=()):
    n = len(items)
    kinds = [k for _, k in items]
    after = list(after)

    def body(*refs):
        in_refs, out_refs = refs[:n], refs[n + len(after):2 * n + len(after)]
        send_sems, recv_sems, local_sems = refs[-3:]
        local = _local_copies(kinds, in_refs, out_refs, local_sems)
        sends = _remote_copies(kinds, in_refs, out_refs, send_sems, recv_sems, arriving=False)
        for cp in local + sends:
            cp.start()
        for cp in _remote_copies(kinds, in_refs, out_refs, send_sems, recv_sems, arriving=True):
            cp.wait_recv()
        for cp in sends:
            cp.wait_send()
        for cp in local:
            cp.wait()

    any_spec = pl.BlockSpec(memory_space=pl.ANY)
    return pl.pallas_call(
        body, name=name,
        in_specs=[any_spec] * (n + len(after)), out_specs=[any_spec] * n,
        out_shape=[_sds(_out_shape(a, k), a.dtype) for a, k in items],
        scratch_shapes=[pltpu.SemaphoreType.DMA((n * (N_DEV - 1),)), pltpu.SemaphoreType.DMA((n * (N_DEV - 1),)),
                        pltpu.SemaphoreType.DMA((n,))],
    )(*[a for a, _ in items], *after)


_HBM = pl.BlockSpec(memory_space=pltpu.HBM)
_SEM = pl.BlockSpec(memory_space=pltpu.SEMAPHORE)
_EFFECT = pltpu.SideEffectType.DATAFLOW_SIDE_EFFECTING


def exchange_start(items, after, *, name):
    n = len(items)
    kinds = [k for _, k in items]

    def body(*refs):
        in_refs, land_refs = refs[:n], refs[n:2 * n]
        send_sems, recv_sems, local_sems = refs[2 * n + 1:2 * n + 4]
        token = refs[-1]
        for cp in (_local_copies(kinds, in_refs, land_refs, local_sems)
                   + _remote_copies(kinds, in_refs, land_refs, send_sems, recv_sems, arriving=False)):
            cp.start()
        token[...] = jnp.zeros_like(token)

    srcs = [pltpu.with_memory_space_constraint(a, pltpu.HBM) for a, _ in items]
    lands = [pltpu.with_memory_space_constraint(lax.empty(_out_shape(a, k), a.dtype), pltpu.HBM) for a, k in items]
    outs = pl.pallas_call(
        body, name=name,
        out_shape=(pltpu.SemaphoreType.DMA((n * (N_DEV - 1),)), pltpu.SemaphoreType.DMA((n * (N_DEV - 1),)),
                   pltpu.SemaphoreType.DMA((n,)),
                   *[pltpu.HBM(a.shape, a.dtype) for a in srcs], *[pltpu.HBM(a.shape, a.dtype) for a in lands],
                   _sds((8, 128), f32)),
        in_specs=[_HBM] * (2 * n) + [pl.BlockSpec(memory_space=pl.ANY)],
        out_specs=(_SEM, _SEM, _SEM, *[_HBM] * (2 * n), pl.BlockSpec(memory_space=pltpu.VMEM)),
        input_output_aliases={i: 3 + i for i in range(2 * n)},
        compiler_params=pltpu.CompilerParams(has_side_effects=_EFFECT),
    )(*srcs, *lands, after)
    return {"kinds": kinds, "sems": outs[:3], "srcs": outs[3:3 + n], "lands": outs[3 + n:3 + 2 * n], "token": outs[-1]}


def exchange_wait(handle, after, *, name):
    kinds = handle["kinds"]
    n = len(kinds)

    def body(*refs):
        in_refs, land_refs = refs[:n], refs[n:2 * n]
        send_sems, recv_sems, local_sems = refs[2 * n:2 * n + 3]
        for cp in _remote_copies(kinds, in_refs, land_refs, send_sems, recv_sems, arriving=True):
            cp.wait_recv()
        for cp in _remote_copies(kinds, in_refs, land_refs, send_sems, recv_sems, arriving=False):
            cp.wait_send()
        for cp in _local_copies(kinds, in_refs, land_refs, local_sems):
            cp.wait()

    srcs, lands = list(handle["srcs"]), list(handle["lands"])
    after = list(after) if isinstance(after, (list, tuple)) else [after]
    outs = pl.pallas_call(
        body, name=name,
        out_shape=tuple(pltpu.HBM(a.shape, a.dtype) for a in srcs + lands),
        in_specs=[_HBM] * (2 * n) + [_SEM] * 3 + [pl.BlockSpec(memory_space=pl.ANY)] * len(after),
        out_specs=tuple([_HBM] * (2 * n)),
        input_output_aliases={i: i for i in range(2 * n)},
        compiler_params=pltpu.CompilerParams(has_side_effects=_EFFECT),
    )(*srcs, *lands, *handle["sems"], *after)
    return list(outs[n:])


CHIP_MASKS = (2, 4, 6)


def _g2_first(in_refs, land_refs, send_sems, recv_sems, *, masks, arriving):
    me = _my_index()
    copies = []
    for i in range(len(land_refs)):
        for j, k in enumerate(masks):
            dev, idx = _peer(k)
            dst = land_refs[i].at[idx if arriving else me]
            copies.append(pltpu.make_async_remote_copy(
                src_ref=dst if in_refs is None else in_refs[i], dst_ref=dst,
                send_sem=send_sems.at[i * len(masks) + j], recv_sem=recv_sems.at[i * len(masks) + j],
                device_id=dev, device_id_type=MESH))
    return copies


def _g2_forward(land_refs, fwd_send, fwd_recv, *, arriving):
    sibling, _ = _peer(1)
    copies = []
    for i in range(len(land_refs)):
        for j, k in enumerate(CHIP_MASKS):
            _, idx = _peer(k | 1 if arriving else k)
            copies.append(pltpu.make_async_remote_copy(
                src_ref=land_refs[i].at[idx], dst_ref=land_refs[i].at[idx],
                send_sem=fwd_send.at[i * 3 + j], recv_sem=fwd_recv.at[i * 3 + j], device_id=sibling,
                device_id_type=MESH))
    return copies


def gather2_start(arrays, after, *, name):
    n = len(arrays)

    def body(*refs):
        in_refs, land_refs = refs[:n], refs[n:2 * n]
        ici_send, ici_recv, d2d_send, d2d_recv, local_sems = refs[2 * n + 1:2 * n + 6]
        token = refs[-1]
        ici = _g2_first(in_refs, land_refs, ici_send, ici_recv, masks=CHIP_MASKS, arriving=False)
        d2d = _g2_first(in_refs, land_refs, d2d_send, d2d_recv, masks=(1,), arriving=False)
        for cp in _local_copies(["gather"] * n, in_refs, land_refs, local_sems) + ici + d2d:
            cp.start()
        token[...] = jnp.zeros_like(token)

    srcs = [pltpu.with_memory_space_constraint(a, pltpu.HBM) for a in arrays]
    lands = [pltpu.with_memory_space_constraint(lax.empty((N_DEV,) + a.shape, a.dtype), pltpu.HBM) for a in arrays]
    sem = pltpu.SemaphoreType.DMA
    outs = pl.pallas_call(
        body, name=name,
        out_shape=(sem((3 * n,)), sem((3 * n,)), sem((n,)), sem((n,)), sem((n,)),
                   *[pltpu.HBM(a.shape, a.dtype) for a in srcs], *[pltpu.HBM(a.shape, a.dtype) for a in lands],
                   _sds((8, 128), f32)),
        in_specs=[_HBM] * (2 * n) + [pl.BlockSpec(memory_space=pl.ANY)],
        out_specs=(*[_SEM] * 5, *[_HBM] * (2 * n), pl.BlockSpec(memory_space=pltpu.VMEM)),
        input_output_aliases={i: 5 + i for i in range(2 * n)},
        compiler_params=pltpu.CompilerParams(has_side_effects=_EFFECT),
    )(*srcs, *lands, after)
    return {"n": n, "sems": outs[:5], "srcs": outs[5:5 + n], "lands": outs[5 + n:5 + 2 * n], "token": outs[-1]}


def gather2_forward(handle, after, *, name):
    n = handle["n"]

    def body(*refs):
        land_refs = refs[:n]
        ici_recv = refs[n]
        fwd_send, fwd_recv = refs[n + 2:n + 4]
        for cp in _g2_first(None, land_refs, fwd_send, ici_recv, masks=CHIP_MASKS, arriving=True):
            cp.wait_recv()
        for cp in _g2_forward(land_refs, fwd_send, fwd_recv, arriving=False):
            cp.start()
        token = refs[-1]
        token[...] = jnp.zeros_like(token)

    lands = list(handle["lands"])
    sem = pltpu.SemaphoreType.DMA
    outs = pl.pallas_call(
        body, name=name,
        out_shape=(sem((3 * n,)), sem((3 * n,)), *[pltpu.HBM(a.shape, a.dtype) for a in lands], _sds((8, 128), f32)),
        in_specs=[_HBM] * n + [_SEM, pl.BlockSpec(memory_space=pl.ANY)],
        out_specs=(_SEM, _SEM, *[_HBM] * n, pl.BlockSpec(memory_space=pltpu.VMEM)),
        input_output_aliases={i: 2 + i for i in range(n)},
        compiler_params=pltpu.CompilerParams(has_side_effects=_EFFECT),
    )(*lands, handle["sems"][1], after)
    return dict(handle, fwd=outs[:2], lands=outs[2:2 + n], token=outs[-1])


def gather2_wait(handle, after, *, name):
    n = handle["n"]

    def body(*refs):
        in_refs, land_refs = refs[:n], refs[n:2 * n]
        ici_send, d2d_send, d2d_recv, local_sems, fwd_send, fwd_recv = refs[2 * n:2 * n + 6]
        for cp in _g2_first(in_refs, land_refs, d2d_send, d2d_recv, masks=(1,), arriving=True):
            cp.wait_recv()
        for cp in _g2_forward(land_refs, fwd_send, fwd_recv, arriving=True):
            cp.wait_recv()
        for cp in (_g2_first(in_refs, land_refs, ici_send, fwd_recv, masks=CHIP_MASKS, arriving=False)
                   + _g2_first(in_refs, land_refs, d2d_send, d2d_recv, masks=(1,), arriving=False)
                   + _g2_forward(land_refs, fwd_send, fwd_recv, arriving=False)):
            cp.wait_send()
        for cp in _local_copies(["gather"] * n, in_refs, land_refs, local_sems):
            cp.wait()

    srcs, lands = list(handle["srcs"]), list(handle["lands"])
    s = handle["sems"]
    outs = pl.pallas_call(
        body, name=name,
        out_shape=tuple(pltpu.HBM(a.shape, a.dtype) for a in srcs + lands),
        in_specs=[_HBM] * (2 * n) + [_SEM] * 6 + [pl.BlockSpec(memory_space=pl.ANY)],
        out_specs=tuple([_HBM] * (2 * n)),
        input_output_aliases={i: i for i in range(2 * n)},
        compiler_params=pltpu.CompilerParams(has_side_effects=_EFFECT),
    )(*srcs, *lands, s[0], s[2], s[3], s[4], *handle["fwd"], after)
    return list(outs[n:])


def adamw(entries, *, name):
    c1 = 1.0 - ADAM_B1 ** ADAM_STEP
    c2 = 1.0 - ADAM_B2 ** ADAM_STEP
    tiles = [_tile(w.shape[-2], (64, 32, 16, 8)) for _, w, _, _, _, _ in entries]
    steps = [w.shape[-2] // tr for (_, w, _, _, _, _), tr in zip(entries, tiles)]
    n = len(entries)

    def body(*refs):
        i = pl.program_id(0)
        for e in range(n):
            s_ref, w_ref, m_ref, v_ref = refs[4 * e:4 * e + 4]
            g_ref, d_ref, m2_ref, v2_ref = refs[len(refs) - 4 * n + 4 * e:len(refs) - 4 * n + 4 * e + 4]

            @pl.when(i < steps[e])
            def _():
                g = s_ref[0].astype(f32)
                for d in range(1, N_DEV):
                    g = g + s_ref[d].astype(f32)
                m2 = ADAM_B1 * m_ref[...] + (1.0 - ADAM_B1) * g
                v2 = ADAM_B2 * v_ref[...] + (1.0 - ADAM_B2) * (g * g)
                g_ref[...] = g
                m2_ref[...] = m2
                v2_ref[...] = v2
                d_ref[...] = -ADAM_LR * ((m2 / c1) / (jnp.sqrt(v2 / c2) + ADAM_EPS) + ADAM_WD * w_ref[...])

    in_specs, out_specs, out_shape, args, extras, aliases = [], [], [], [], [], {}
    for e, ((slots, w, m, v, layer, into), tr, ns) in enumerate(zip(entries, tiles, steps)):
        C = w.shape[-1]
        row = lambda i, ns=ns: jnp.minimum(i, ns - 1)
        if layer is None:
            blk = pl.BlockSpec((tr, C), lambda i, row=row: (row(i), 0))
        else:
            blk = pl.BlockSpec((None, tr, C), lambda i, row=row, layer=layer: (layer, row(i), 0))
        in_specs += [pl.BlockSpec((N_DEV, tr, C), lambda i, row=row: (0, row(i), 0)), blk, blk, blk]
        args += [slots, w, m, v]
        out_specs += [blk] * 4
        out_shape += [_sds(w.shape, f32)] * 4
        if into is not None:
            for t, a in enumerate(into):
                aliases[4 * n + len(extras)] = 4 * e + t
                extras.append(a)
    outs = pl.pallas_call(
        body, name=name, grid=(max(steps),),
        in_specs=in_specs + [pl.BlockSpec(memory_space=pl.ANY)] * len(extras),
        out_specs=out_specs, out_shape=out_shape, input_output_aliases=aliases,
        compiler_params=_params(("arbitrary",)),
    )(*args, *extras)
    return [outs[4 * e:4 * e + 4] for e in range(n)]


WEIGHTS = ("norm_gains", "mem_norm", "w_in", "w_mem_kv", "w_out", "w_pool", "pool_scale", "kv_norm", "w_kv",
           "w_gate_up", "w_down")
LAYER_MATS = ("w_in", "w_mem_kv", "w_out", "w_gate_up", "w_down")
POOL_SHARD = MAIN_W // N_DEV
KV_SHARD = 2 * MAIN_W // N_DEV
LOOKAHEAD = 2
TWO_LEVEL_LAYERS = (0, 1, 2)


def _pack_small(gains, pscale):
    lead = gains.shape[:-3]
    g = gains.reshape(lead + (16, 128))
    p = jnp.zeros(lead + (8, 128), f32).at[..., :2, :POOL_SHARD].set(pscale)
    return jnp.concatenate([g, p], axis=-2)


def _unpack_small(a):
    return a[:16].reshape(4, 4, 128), a[16:18, :POOL_SHARD]


def _pack_repl(mem_norm, kv_norm):
    return jnp.concatenate([mem_norm, kv_norm.reshape(1, D_MODEL), jnp.zeros((3, D_MODEL), f32)], axis=0)


def _unpack_repl(a):
    return a[:4], a[4]


def kernel(x, mem, positions, norm_gains, mem_norm, w_in, w_mem_kv, w_out, w_pool, pool_scale, kv_norm, w_kv, w_gate_up, w_down, loss_target, m_norm_gains, m_mem_norm, m_w_in, m_w_mem_kv, m_w_out, m_w_pool, m_pool_scale, m_kv_norm, m_w_kv, m_w_gate_up, m_w_down, v_norm_gains, v_mem_norm, v_w_in, v_w_mem_kv, v_w_out, v_w_pool, v_pool_scale, v_kv_norm, v_w_kv, v_w_gate_up, v_w_down):
    w = dict(norm_gains=norm_gains, mem_norm=mem_norm, w_in=w_in, w_mem_kv=w_mem_kv, w_out=w_out, w_pool=w_pool,
             pool_scale=pool_scale, kv_norm=kv_norm, w_kv=w_kv, w_gate_up=w_gate_up, w_down=w_down)
    m = dict(norm_gains=m_norm_gains, mem_norm=m_mem_norm, w_in=m_w_in, w_mem_kv=m_w_mem_kv, w_out=m_w_out,
             w_pool=m_w_pool, pool_scale=m_pool_scale, kv_norm=m_kv_norm, w_kv=m_w_kv, w_gate_up=m_w_gate_up,
             w_down=m_w_down)
    v = dict(norm_gains=v_norm_gains, mem_norm=v_mem_norm, w_in=v_w_in, w_mem_kv=v_w_mem_kv, w_out=v_w_out,
             w_pool=v_w_pool, pool_scale=v_pool_scale, kv_norm=v_kv_norm, w_kv=v_w_kv, w_gate_up=v_w_gate_up,
             w_down=v_w_down)

    def transposed_view(d):
        d = dict(d)
        d["w_gate_up"] = jnp.swapaxes(d["w_gate_up"], 1, 2)
        d["w_kv"] = jnp.swapaxes(d["w_kv"], 0, 1)
        return d

    wv, mv, vv = transposed_view(w), transposed_view(m), transposed_view(v)

    PARTS = {"mix": ("w_in", "w_mem_kv", "w_out"), "ffn": ("w_gate_up", "w_down"), "gu": ("w_gate_up",),
             "down": ("w_down",), "all": ("w_in", "w_mem_kv", "w_out", "w_gate_up", "w_down")}

    small = _pack_small(norm_gains, pool_scale)
    handles = {(0, "mix"): gather2_start([wv[k][0].astype(bf16) for k in PARTS["mix"]] + [small], x,
                                         name="gather_start_mix_l0")}
    first = handles[0, "mix"]["token"]

    def parts_of(l):
        return (("mix", "gu", "down"), ("mix", "ffn"))[l] if l < 2 else ("all",)

    wb = [dict(zip(LAYER_MATS, mats))
          for mats in to_bf16_layers([wv[k] for k in LAYER_MATS], first, name="weights_bf16")]

    def part_items(l, part):
        items = [(wb[l][k], "gather") for k in PARTS[part]]
        if part == "ffn" and l == N_A_LAYERS - 1:
            items.append((wv["w_kv"].astype(bf16), "gather"))
        return items

    def start_layer(l, after):
        for part in parts_of(l):
            if (l, part) in handles:
                continue
            if l in TWO_LEVEL_LAYERS:
                handles[l, part] = gather2_start([a for a, _ in part_items(l, part)], after,
                                                 name=f"gather_start_{part}_l{l}")
            else:
                handles[l, part] = exchange_start(part_items(l, part), after, name=f"gather_start_{part}_l{l}")
            after = handles[l, part]["token"]
        return after

    token = first
    for l in range(LOOKAHEAD):
        token = start_layer(l, token)
    landed, fetched = {}, {}

    passed_early = {}

    def layer_weights(l, part, after):
        if (l, part) in fetched:
            return fetched[l, part]
        if part not in parts_of(l):
            if part == "down" and l + 1 in TWO_LEVEL_LAYERS and parts_of(l + 1) == ("all",):
                passed_early[l + 1] = gather2_forward(handles[l + 1, "all"], after, name=f"gather_forward_all_l{l + 1}")
                return {}, passed_early[l + 1]["token"]
            if part == "down" or (part == "gu" and "all" in parts_of(l)):
                return {}, None
            part = "all" if "all" in parts_of(l) else "ffn"
        if l == 0 and part == "mix":
            after = token
        if l in TWO_LEVEL_LAYERS:
            passed = passed_early.pop(l, None)
            if passed is None:
                passed = gather2_forward(handles[l, part], after, name=f"gather_forward_{part}_l{l}")
            got = gather2_wait(passed, after, name=f"gather_wait_{part}_l{l}")
        else:
            got = exchange_wait(handles[l, part], after, name=f"gather_wait_{part}_l{l}")
        landed[l, part] = got
        started = None
        if part in ("mix", "all") and l + LOOKAHEAD < DEPTH:
            started = start_layer(l + LOOKAHEAD, got[0])
        W = {k: g.reshape(-1, g.shape[-1]) for k, g in zip(PARTS[part], got)}
        fetched[l, part] = (W, started)
        return W, started

    layer_weights(0, "mix", x)
    gsmall = landed[0, "mix"][len(PARTS["mix"])]
    P = {"norm_gains": jnp.moveaxis(gsmall[:, :16].reshape(N_DEV, 4, 4, 128), 0, 2).reshape(4, 4, D_MODEL),
         "pool_scale": jnp.moveaxis(gsmall[:, 16:18, :POOL_SHARD], 0, 1).reshape(2, MAIN_W),
         "mem_norm": mem_norm, "kv_norm": kv_norm, "w_pool": w_pool}

    def kv_weight(after):
        g = landed[N_A_LAYERS - 1, "ffn"][len(PARTS["ffn"])]
        return g.reshape(2 * MAIN_W, D_MODEL)

    ghandles = {}

    pending = {}

    def gparts_of(l):
        return ("ffn", "mix") if l < 2 else ("all",)

    def emit_grads(l, part, gw):
        if part not in gparts_of(l):
            pending.setdefault(l, {}).update(gw)
            if part == "ffn":
                return None
            gw, part = pending[l], "all"
        items = [(gw[k].reshape((N_DEV, -1) + gw[k].shape[-1:]), "scatter") for k in PARTS[part]]
        if part != "ffn" and l == N_A_LAYERS:
            items.append((gw["w_kv"].reshape(N_DEV, KV_SHARD, D_MODEL), "scatter"))
        if part != "ffn" and l < N_A_LAYERS:
            items.append((gw["w_pool"], "gather"))
        ghandles[l, part] = exchange_start(items, gsmall, name=f"scatter_start_{part}_l{l}")
        return ghandles[l, part]["token"]

    sq, grad_x, GS, emitted = local_step(x, mem, positions, loss_target, P, layer_weights, kv_weight, emit_grads)

    def pool3(a):
        return a.reshape(N_A_LAYERS, MAIN_W, POOL_GROUP)

    out = {}
    after = [emitted]

    def finish_layer(l, after):
        for part in gparts_of(l):
            got = exchange_wait(ghandles[l, part], after, name=f"scatter_wait_{part}_l{l}")
            names = list(PARTS[part])
            entries = [(slots, wv[k], mv[k], vv[k], l, out.get(k)) for k, slots in zip(names, got)]
            if part != "ffn" and l == N_A_LAYERS:
                names.append("w_kv")
                entries.append((got[-1], wv["w_kv"], mv["w_kv"], vv["w_kv"], None, None))
            if part != "ffn" and l < N_A_LAYERS:
                names.append("w_pool")
                entries.append((got[-1], pool3(w_pool), pool3(m_w_pool), pool3(v_w_pool), l, out.get("w_pool")))
            out.update(zip(names, adamw(entries, name=f"adamw_{part}_l{l}")))
            after = [out[k][0] for k in names]
        return after

    for l in reversed(range(1, DEPTH)):
        after = finish_layer(l, after)

    gs = _pack_small(jnp.moveaxis(GS["norm_gains"].reshape(4, 4, N_DEV, 128), 2, 0),
                     jnp.moveaxis(GS["pool_scale"].reshape(2, N_DEV, POOL_SHARD), 1, 0))
    parts_small, parts_repl, parts_sq = exchange(
        [(gs, "scatter"), (_pack_repl(GS["mem_norm"], GS["kv_norm"]), "gather"),
         (jnp.full((8, 128), sq, f32), "gather")],
        name="exchange_small_grads", after=after)
    loss = (0.5 / D_MODEL) * jnp.sum(parts_sq[:, 0, 0])
    finish_layer(0, [parts_small])
    out["w_gate_up"] = [jnp.swapaxes(r, 1, 2) for r in out["w_gate_up"]]
    out["w_kv"] = [jnp.swapaxes(r, 0, 1) for r in out["w_kv"]]
    out["w_pool"] = [r.reshape(w_pool.shape) for r in out["w_pool"]]

    res_small, res_repl = adamw(
        [(parts_small, small, _pack_small(m_norm_gains, m_pool_scale), _pack_small(v_norm_gains, v_pool_scale),
          None, None),
         (parts_repl, _pack_repl(mem_norm, kv_norm), _pack_repl(m_mem_norm, m_kv_norm),
          _pack_repl(v_mem_norm, v_kv_norm), None, None)], name="adamw_small")
    out["norm_gains"], out["pool_scale"] = zip(*[_unpack_small(r) for r in res_small])
    out["mem_norm"], out["kv_norm"] = zip(*[_unpack_repl(r) for r in res_repl])

    return (loss, grad_x, *[out[k][0] for k in WEIGHTS], *[out[k][1] for k in WEIGHTS],
            *[out[k][2] for k in WEIGHTS], *[out[k][3] for k in WEIGHTS])
```

```python
import numpy as np
import jax
import jax.numpy as jnp
from jax import lax
from jax.experimental import pallas as pl
from jax.experimental.pallas import tpu as pltpu

f32 = jnp.float32
bf16 = jnp.bfloat16

D_MODEL = 1024
SEQ = 2048
DEPTH = 4
N_MEM = 256
HEAD_DIM = 64
N_MEM_HEADS = 4
MEM_W = 256
MAIN_W = 768
POOL_WINDOWS = (2, 4, 8, 16)
POOL_GROUP = 192
POOL_HALO = 16
DIL_PATTERNS = ((128, 1), (512, 4), (2048, 16))
N_GROUPS = 3
GROUP_W = 256
BAND = 128
N_A_LAYERS = 2
D_FF = 2816
ROPE_THETA = 10000.0
EPS = 1e-6
NEG = -1e30
SCALE = HEAD_DIM ** -0.5
N_DEV = 8

ADAM_LR = 0.001
ADAM_B1 = 0.9
ADAM_B2 = 0.999
ADAM_EPS = 1e-08
ADAM_WD = 0.01
ADAM_STEP = 10

VMEM_LIMIT_BYTES = 56 * 1024 * 1024
MESH = pl.DeviceIdType.MESH

NN = (((1,), (0,)), ((), ()))
NT = (((1,), (1,)), ((), ()))
TN = (((0,), (0,)), ((), ()))


def _params(sem=None):
    return pltpu.CompilerParams(dimension_semantics=sem, vmem_limit_bytes=VMEM_LIMIT_BYTES)


def _tile(n, cands):
    for c in cands:
        if n % c == 0:
            return c
    return n


def _sds(shape, dtype):
    return jax.ShapeDtypeStruct(tuple(shape), dtype)


def _rms_r(v):
    return lax.rsqrt(jnp.mean(v * v, axis=-1, keepdims=True) + EPS)


ROW_GROUPS = 2


def rms_matmul(x, gain, w, *, name, out_dtype, transposed=False, after=None, rope=None):
    M, K = x.shape
    N = w.shape[0] if transposed else w.shape[1]
    tm = min(512, M)
    order = [] if after is None else [after]
    tables = [] if rope is None else list(rope)
    rot_spec = [] if rope is None else [pl.BlockSpec((tm, MAIN_W), lambda i: (i, 0))]
    rot_shape = [] if rope is None else [_sds((M, MAIN_W), f32)]

    def body(x_ref, g_ref, w_ref, *refs):
        z_ref, h_ref = refs[len(tables) + len(order):][:2]
        groups = [slice(g * tm // ROW_GROUPS, (g + 1) * tm // ROW_GROUPS) for g in range(ROW_GROUPS)]
        hs = []
        for rows in groups:
            xv = x_ref[rows, :]
            hs.append((xv * _rms_r(xv) * g_ref[...]).astype(bf16))
            h_ref[rows, :] = hs[-1]
        zs = [lax.dot_general(h, w_ref[...], NT if transposed else NN, preferred_element_type=f32) for h in hs]
        for rows, z in zip(groups, zs):
            z_ref[rows, :] = z.astype(z_ref.dtype)
            if tables:
                c = jnp.tile(refs[0][rows, :], (1, MAIN_W // 128))
                s = jnp.tile(refs[1][rows, :], (1, MAIN_W // 128))
                zr = z[:, :MAIN_W]
                refs[-1][rows, :] = zr * c + _swap_halves(zr) * s

    tab = pl.BlockSpec((tm, 128), lambda i: (i, 0))
    return pl.pallas_call(
        body, name=name, grid=(M // tm,),
        in_specs=[pl.BlockSpec((tm, K), lambda i: (i, 0)),
                  pl.BlockSpec((1, K), lambda i: (0, 0)),
                  pl.BlockSpec(w.shape, lambda i: (0, 0))] + [tab] * len(tables)
        + [pl.BlockSpec(memory_space=pl.ANY)] * len(order),
        out_specs=[pl.BlockSpec((tm, N), lambda i: (i, 0)), pl.BlockSpec((tm, K), lambda i: (i, 0))] + rot_spec,
        out_shape=[_sds((M, N), out_dtype), _sds((M, K), bf16)] + rot_shape,
        compiler_params=_params(("parallel",)),
    )(x, gain, w, *tables, *order)


def matmul_rms_res(a, w, gain, res, *, name, target=None, after=None):
    M, K = a.shape
    N = w.shape[1]
    tm = min(512, M)
    goal = [] if target is None else [target]
    order = [] if after is None else [after]

    def body(a_ref, w_ref, g_ref, r_ref, *refs):
        y_ref, x_ref = refs[len(goal) + len(order):][:2]
        groups = [slice(g * tm // ROW_GROUPS, (g + 1) * tm // ROW_GROUPS) for g in range(ROW_GROUPS)]
        ys = [jnp.dot(a_ref[rows, :], w_ref[...], preferred_element_type=f32) for rows in groups]
        sq = 0.0
        for rows, y in zip(groups, ys):
            y_ref[rows, :] = y.astype(bf16)
            x = r_ref[rows, :] + y * _rms_r(y) * g_ref[...]
            if goal:
                e = x - refs[0][rows, :]
                x = e * (1.0 / N)
                sq = sq + jnp.sum(jnp.sum(e * e, axis=0, keepdims=True), axis=1, keepdims=True)
            x_ref[rows, :] = x
        if goal:
            _accumulate(refs[-1], sq)

    row = pl.BlockSpec((tm, N), lambda i: (i, 0))
    return pl.pallas_call(
        body, name=name, grid=(M // tm,),
        in_specs=[pl.BlockSpec((tm, K), lambda i: (i, 0)),
                  pl.BlockSpec((K, N), lambda i: (0, 0)),
                  pl.BlockSpec((1, N), lambda i: (0, 0)),
                  row] + [row] * len(goal) + [pl.BlockSpec(memory_space=pl.ANY)] * len(order),
        out_specs=[row, row] + [pl.BlockSpec((8, 128), lambda i: (0, 0))] * len(goal),
        out_shape=[_sds((M, N), bf16), _sds((M, N), f32)] + [_sds((8, 128), f32)] * len(goal),
        compiler_params=_params(("arbitrary",) if goal else ("parallel",)),
    )(a, w, gain, res, *goal, *order)


def matmul(a, b, dims, *, name, out_dtype):
    if dims is TN:
        K, M = a.shape
        tm = _tile(M, (512, 256, 128))
        a_spec = pl.BlockSpec((K, tm), lambda i: (0, i))
    else:
        M, K = a.shape
        tm = _tile(M, (1024, 512, 256, 128))
        a_spec = pl.BlockSpec((tm, K), lambda i: (i, 0))
    N = b.shape[0] if dims is NT else b.shape[1]

    def body(a_ref, b_ref, o_ref):
        o_ref[...] = lax.dot_general(a_ref[...].astype(bf16), b_ref[...].astype(bf16), dims,
                                     preferred_element_type=f32).astype(o_ref.dtype)

    return pl.pallas_call(
        body, name=name, grid=(M // tm,),
        in_specs=[a_spec, pl.BlockSpec(b.shape, lambda i: (0, 0))],
        out_specs=pl.BlockSpec((tm, N), lambda i: (i, 0)),
        out_shape=_sds((M, N), out_dtype),
        compiler_params=_params(("parallel",)),
    )(a, b)


def rms_gate_up(x, gain, wt, *, name):
    M, K = x.shape
    tm = min(2048, M)
    tn = _tile(D_FF, (256, 128))
    nj = D_FF // tn

    def body(x_ref, gn_ref, wg_ref, wu_ref, g_ref, u_ref, a_ref, h_ref):
        @pl.when(pl.program_id(1) == 0)
        def _():
            xv = x_ref[...]
            h_ref[...] = (xv * _rms_r(xv) * gn_ref[...]).astype(bf16)

        h = h_ref[...]
        g = lax.dot_general(h, wg_ref[...], NT, preferred_element_type=f32).astype(bf16)
        u = lax.dot_general(h, wu_ref[...], NT, preferred_element_type=f32).astype(bf16)
        g_ref[...] = g
        u_ref[...] = u
        a_ref[...] = g * (1.0 / (1.0 + jnp.exp(-g))) * u

    col = pl.BlockSpec((tm, tn), lambda i, j: (i, j))
    return pl.pallas_call(
        body, name=name, grid=(M // tm, nj),
        in_specs=[pl.BlockSpec((tm, K), lambda i, j: (i, 0)),
                  pl.BlockSpec((1, K), lambda i, j: (0, 0)),
                  pl.BlockSpec((tn, K), lambda i, j: (j, 0)),
                  pl.BlockSpec((tn, K), lambda i, j: (j + nj, 0))],
        out_specs=[col, col, col, pl.BlockSpec((tm, K), lambda i, j: (i, 0))],
        out_shape=[_sds((M, D_FF), bf16)] * 3 + [_sds((M, K), bf16)],
        compiler_params=_params(("parallel", "arbitrary")),
    )(x, gain, wt, wt)


def _rms_bwd_math(yv, gain, dn):
    r = _rms_r(yv)
    q = dn * gain
    dy = r * q - yv * (r * r * r) * jnp.mean(q * yv, axis=-1, keepdims=True)
    return dy, jnp.sum(dn * yv * r, axis=0, keepdims=True)


def _accumulate(ref, val):
    @pl.when(pl.program_id(0) == 0)
    def _():
        ref[...] = jnp.zeros_like(ref)

    ref[...] += val


def down_bwd(y, gain, dn, w_down, g, u, *, name, after=None):
    M, K = y.shape
    tm = min(512, M)
    order = [] if after is None else [after]

    def body(y_ref, gn_ref, dn_ref, w_ref, g_ref, u_ref, *refs):
        dy_ref, o_ref, dg_ref = refs[len(order):]
        groups = [slice(a * tm // ROW_GROUPS, (a + 1) * tm // ROW_GROUPS) for a in range(ROW_GROUPS)]
        dys, dgain = [], 0.0
        for rows in groups:
            dy, part = _rms_bwd_math(y_ref[rows, :].astype(f32), gn_ref[...], dn_ref[rows, :])
            dys.append(dy.astype(bf16))
            dy_ref[rows, :] = dys[-1]
            dgain = dgain + part
        _accumulate(dg_ref, dgain)
        das = [lax.dot_general(dy, w_ref[...], NT, preferred_element_type=f32).astype(bf16) for dy in dys]
        for rows, da in zip(groups, das):
            g = g_ref[rows, :]
            s = 1.0 / (1.0 + jnp.exp(-g))
            o_ref[rows, :D_FF] = da * u_ref[rows, :] * s * (1.0 + g * (1.0 - s))
            o_ref[rows, D_FF:] = da * g * s

    row = pl.BlockSpec((tm, K), lambda i: (i, 0))
    vec = pl.BlockSpec((1, K), lambda i: (0, 0))
    wide = pl.BlockSpec((tm, D_FF), lambda i: (i, 0))
    return pl.pallas_call(
        body, name=name, grid=(M // tm,),
        in_specs=[row, vec, row, pl.BlockSpec((D_FF, K), lambda i: (0, 0)), wide, wide]
        + [pl.BlockSpec(memory_space=pl.ANY)] * len(order),
        out_specs=[row, pl.BlockSpec((tm, 2 * D_FF), lambda i: (i, 0)), vec],
        out_shape=[_sds((M, K), bf16), _sds((M, 2 * D_FF), bf16), _sds((1, K), f32)],
        compiler_params=_params(("arbitrary",)),
    )(y, gain, dn, w_down, g, u, *order)


def rms_bwd_matmul(y, gain, dn, w, dims, *, name, after=None):
    M, K = y.shape
    N = w.shape[0] if dims is NT else w.shape[1]
    tm = min(1024, M)
    order = [] if after is None else [after]

    def body(y_ref, gn_ref, dn_ref, w_ref, *refs):
        dy_ref, o_ref, dg_ref = refs[len(order):]
        groups = [slice(g * tm // ROW_GROUPS, (g + 1) * tm // ROW_GROUPS) for g in range(ROW_GROUPS)]
        dys, dgain = [], 0.0
        for rows in groups:
            dy, part = _rms_bwd_math(y_ref[rows, :].astype(f32), gn_ref[...], dn_ref[rows, :].astype(f32))
            dys.append(dy.astype(bf16))
            dy_ref[rows, :] = dys[-1]
            dgain = dgain + part
        _accumulate(dg_ref, dgain)
        for rows, dy in zip(groups, dys):
            o_ref[rows, :] = lax.dot_general(dy, w_ref[...], dims, preferred_element_type=f32).astype(bf16)

    row = pl.BlockSpec((tm, K), lambda i: (i, 0))
    vec = pl.BlockSpec((1, K), lambda i: (0, 0))
    return pl.pallas_call(
        body, name=name, grid=(M // tm,),
        in_specs=[row, vec, row, pl.BlockSpec(w.shape, lambda i: (0, 0))]
        + [pl.BlockSpec(memory_space=pl.ANY)] * len(order),
        out_specs=[row, pl.BlockSpec((tm, N), lambda i: (i, 0)), vec],
        out_shape=[_sds((M, K), bf16), _sds((M, N), bf16), _sds((1, K), f32)],
        compiler_params=_params(("arbitrary",)),
    )(y, gain, dn, w, *order)


def matmul_rms_bwd(a, b, dims, y, gain, res, *, name, after=None):
    M, K = a.shape
    N = y.shape[1]
    tm = min(512, M)
    order = [] if after is None else [after]

    def body(a_ref, b_ref, y_ref, gn_ref, r_ref, *refs):
        dx_ref, dg_ref = refs[len(order):]
        groups = [slice(g * tm // ROW_GROUPS, (g + 1) * tm // ROW_GROUPS) for g in range(ROW_GROUPS)]
        dns = [lax.dot_general(a_ref[rows, :].astype(bf16), b_ref[...], dims, preferred_element_type=f32)
               for rows in groups]
        dgain = 0.0
        for rows, dn in zip(groups, dns):
            dy, part = _rms_bwd_math(y_ref[rows, :], gn_ref[...], dn)
            dx_ref[rows, :] = dy + r_ref[rows, :]
            dgain = dgain + part
        _accumulate(dg_ref, dgain)

    row = pl.BlockSpec((tm, N), lambda i: (i, 0))
    vec = pl.BlockSpec((1, N), lambda i: (0, 0))
    return pl.pallas_call(
        body, name=name, grid=(M // tm,),
        in_specs=[pl.BlockSpec((tm, K), lambda i: (i, 0)), pl.BlockSpec(b.shape, lambda i: (0, 0)), row, vec, row]
        + [pl.BlockSpec(memory_space=pl.ANY)] * len(order),
        out_specs=[row, vec],
        out_shape=[_sds((M, N), f32), _sds((1, N), f32)],
        compiler_params=_params(("arbitrary",)),
    )(a, b, y, gain, res, *order)


def _pool_select(a1, a2, a3, a4):
    col = lax.broadcasted_iota(jnp.int32, (1, MAIN_W), 1) // POOL_GROUP
    return jnp.where(col == 0, a1, jnp.where(col == 1, a2, jnp.where(col == 2, a3, a4)))


def _pool_count(t):
    col = lax.broadcasted_iota(jnp.int32, (1, MAIN_W), 1) // POOL_GROUP
    win = jnp.where(col == 0, 2, jnp.where(col == 1, 4, jnp.where(col == 2, 8, 16)))
    return jnp.minimum(t + 1, win).astype(f32)


def pool_fwd(z, wbd, scale, *, name):
    M = z.shape[0]
    tm = 512
    nper = SEQ // tm
    hb = tm // POOL_HALO

    def body(zc_ref, zh_ref, w_ref, s_ref, p_ref, y_ref):
        i = pl.program_id(0)
        seq_blk = i % nper
        halo = jnp.where(seq_blk == 0, 0.0, zh_ref[...].astype(f32))
        u = zc_ref[...].astype(f32)
        ext = jnp.concatenate([halo, u], axis=0)
        a1 = ext + pltpu.roll(ext, 1, 0)
        a2 = a1 + pltpu.roll(a1, 2, 0)
        a3 = a2 + pltpu.roll(a2, 4, 0)
        a4 = a3 + pltpu.roll(a3, 8, 0)
        sums = _pool_select(a1, a2, a3, a4)[POOL_HALO:]
        t = seq_blk * tm + lax.broadcasted_iota(jnp.int32, (tm, 1), 0)
        p = (sums / _pool_count(t) - u).astype(bf16)
        p_ref[...] = p
        y_ref[...] = (jnp.dot(p, w_ref[...], preferred_element_type=f32) * s_ref[...]).astype(bf16)

    return pl.pallas_call(
        body, name=name, grid=(M // tm,),
        in_specs=[pl.BlockSpec((tm, MAIN_W), lambda i: (i, 0)),
                  pl.BlockSpec((POOL_HALO, MAIN_W), lambda i: (jnp.maximum(i * hb - 1, 0), 0)),
                  pl.BlockSpec((MAIN_W, MAIN_W), lambda i: (0, 0)),
                  pl.BlockSpec((1, MAIN_W), lambda i: (0, 0))],
        out_specs=[pl.BlockSpec((tm, MAIN_W), lambda i: (i, 0)),
                   pl.BlockSpec((tm, MAIN_W), lambda i: (i, 0))],
        out_shape=[_sds((M, MAIN_W), bf16), _sds((M, D_MODEL), bf16)],
        compiler_params=_params(("parallel",)),
    )(z, z, wbd, scale)


def pool_bwd(dyc, p, wbd, scale, *, name):
    M = p.shape[0]
    tm = 512
    nper = SEQ // tm
    hb = tm // POOL_HALO
    last_hb = M // POOL_HALO - 1

    def body(dy_ref, dyh_ref, p_ref, w_ref, s_ref, dz_ref, dw_ref, ds_ref):
        i = pl.program_id(0)
        seq_blk = i % nper
        dy = dy_ref[...].astype(f32)
        pv = p_ref[...]
        w = w_ref[...]
        sc = s_ref[...]

        @pl.when(i == 0)
        def _():
            dw_ref[...] = jnp.zeros_like(dw_ref)
            ds_ref[...] = jnp.zeros_like(ds_ref)

        v = jnp.dot(pv, w, preferred_element_type=f32)
        ds_ref[...] += jnp.sum(dy * v, axis=0, keepdims=True)
        dv = (dy * sc).astype(bf16)
        dw_ref[...] += lax.dot_general(pv, dv, TN, preferred_element_type=f32)
        dp = lax.dot_general(dv, w, NT, preferred_element_type=f32)
        dvh = jnp.where(seq_blk == nper - 1, 0.0, dyh_ref[...].astype(f32) * sc).astype(bf16)
        dph = lax.dot_general(dvh, w, NT, preferred_element_type=f32)
        ext = jnp.concatenate([dp, dph], axis=0)
        n = tm + POOL_HALO
        t = seq_blk * tm + lax.broadcasted_iota(jnp.int32, (n, 1), 0)
        e = ext / _pool_count(t)
        b1 = e + pltpu.roll(e, n - 1, 0)
        b2 = b1 + pltpu.roll(b1, n - 2, 0)
        b3 = b2 + pltpu.roll(b2, n - 4, 0)
        b4 = b3 + pltpu.roll(b3, n - 8, 0)
        dz_ref[...] = (_pool_select(b1, b2, b3, b4)[:tm] - dp).astype(dz_ref.dtype)

    return pl.pallas_call(
        body, name=name, grid=(M // tm,),
        in_specs=[pl.BlockSpec((tm, MAIN_W), lambda i: (i, 0)),
                  pl.BlockSpec((POOL_HALO, MAIN_W), lambda i: (jnp.minimum((i + 1) * hb, last_hb), 0)),
                  pl.BlockSpec((tm, MAIN_W), lambda i: (i, 0)),
                  pl.BlockSpec((MAIN_W, MAIN_W), lambda i: (0, 0)),
                  pl.BlockSpec((1, MAIN_W), lambda i: (0, 0))],
        out_specs=[pl.BlockSpec((tm, MAIN_W), lambda i: (i, 0)),
                   pl.BlockSpec((MAIN_W, MAIN_W), lambda i: (0, 0)),
                   pl.BlockSpec((1, MAIN_W), lambda i: (0, 0))],
        out_shape=[_sds((M, D_MODEL), bf16), _sds((MAIN_W, MAIN_W), f32), _sds((1, MAIN_W), f32)],
        compiler_params=_params(("arbitrary",)),
    )(dyc, dyc, p, wbd, scale)


def _mem_heads(q, kv):
    first = _first_head()
    for pr in range(N_MEM_HEADS // 2):
        cols = slice(pr * PAIR_W, (pr + 1) * PAIR_W)
        qp = q[:, cols] * SCALE
        kp = kv[:, cols]
        vp = kv[:, MEM_W + pr * PAIR_W: MEM_W + (pr + 1) * PAIR_W]
        for hh in range(2):
            lm = first if hh == 0 else ~first
            qm = jnp.where(lm, qp, 0.0).astype(bf16)
            s = lax.dot_general(qm, kp, NT, preferred_element_type=f32)
            e = jnp.exp(s - jnp.max(s, axis=-1, keepdims=True))
            yield lm, qm, kp, vp, e, jnp.sum(e, axis=-1, keepdims=True)


def memattn_fwd(z, kvm, ycat, *, name, n_seq):
    M = z.shape[0]
    tq = SEQ
    nq = SEQ // tq

    def body(q_ref, kv_ref, _, o_ref):
        first = _first_head()
        outs = []
        for lm, _, _, vp, e, l in _mem_heads(q_ref[...], kv_ref[...]):
            outs.append(jnp.dot(e.astype(bf16), vp, preferred_element_type=f32) * (1.0 / l))
        pairs = [jnp.where(first, outs[2 * pr], outs[2 * pr + 1]) for pr in range(N_MEM_HEADS // 2)]
        o_ref[...] = jnp.concatenate(pairs, axis=1).astype(bf16)

    return pl.pallas_call(
        body, name=name, grid=(n_seq, nq),
        in_specs=[pl.BlockSpec((tq, MEM_W), lambda b, i: (b * nq + i, 3)),
                  pl.BlockSpec((N_MEM, 2 * MEM_W), lambda b, i: (b, 0)),
                  pl.BlockSpec(memory_space=pl.ANY)],
        out_specs=pl.BlockSpec((tq, MEM_W), lambda b, i: (b * nq + i, 3)),
        out_shape=_sds((M, D_MODEL), bf16),
        input_output_aliases={2: 0},
        compiler_params=_params(("parallel", "parallel")),
    )(z, kvm, ycat)


def memattn_bwd(z, kvm, dyc, dz, *, name, n_seq):
    M = z.shape[0]
    tq = SEQ
    nq = SEQ // tq

    def body(q_ref, kv_ref, dy_ref, _, dq_ref, dkv_ref):
        first = _first_head()
        dy = dy_ref[...].astype(f32)
        dqs, dks, dvs = [], [], []
        for h, (lm, qm, kp, vp, e, l) in enumerate(_mem_heads(q_ref[...], kv_ref[...])):
            pr = h // 2
            p = e * (1.0 / l)
            dym = jnp.where(lm, dy[:, pr * PAIR_W:(pr + 1) * PAIR_W], 0.0).astype(bf16)
            dp = lax.dot_general(dym, vp, NT, preferred_element_type=f32)
            ds = (p * (dp - jnp.sum(dp * p, axis=-1, keepdims=True))).astype(bf16)
            dqs.append(jnp.dot(ds, kp, preferred_element_type=f32) * SCALE)
            dk = lax.dot_general(ds, qm, TN, preferred_element_type=f32)
            dv = lax.dot_general(p.astype(bf16), dym, TN, preferred_element_type=f32)
            if h % 2 == 0:
                dks.append(dk)
                dvs.append(dv)
            else:
                dks[pr] = dks[pr] + dk
                dvs[pr] = dvs[pr] + dv
        pairs = [jnp.where(first, dqs[2 * pr], dqs[2 * pr + 1]) for pr in range(N_MEM_HEADS // 2)]
        dq_ref[...] = jnp.concatenate(pairs, axis=1).astype(bf16)

        @pl.when(pl.program_id(1) == 0)
        def _():
            dkv_ref[...] = jnp.zeros_like(dkv_ref)

        dkv_ref[...] += jnp.concatenate(dks + dvs, axis=1)

    return pl.pallas_call(
        body, name=name, grid=(n_seq, nq),
        in_specs=[pl.BlockSpec((tq, MEM_W), lambda b, i: (b * nq + i, 3)),
                  pl.BlockSpec((N_MEM, 2 * MEM_W), lambda b, i: (b, 0)),
                  pl.BlockSpec((tq, MEM_W), lambda b, i: (b * nq + i, 3)),
                  pl.BlockSpec(memory_space=pl.ANY)],
        out_specs=[pl.BlockSpec((tq, MEM_W), lambda b, i: (b * nq + i, 3)),
                   pl.BlockSpec((N_MEM, 2 * MEM_W), lambda b, i: (b, 0))],
        out_shape=[_sds((M, D_MODEL), bf16), _sds((n_seq * N_MEM, 2 * MEM_W), f32)],
        input_output_aliases={3: 0},
        compiler_params=_params(("parallel", "arbitrary")),
    )(z, kvm, dyc, dz)


def rope_tables(pos, *, name):
    M = pos.shape[0]
    tm = min(1024, M)
    half = HEAD_DIM // 2
    inv = ROPE_THETA ** (-np.arange(half, dtype=np.float64) / half)
    inv128 = jnp.asarray(np.tile(inv, 4)[None, :], f32)
    sign128 = jnp.asarray(np.tile(np.concatenate([-np.ones(half), np.ones(half)]), 2)[None, :], f32)

    def body(p_ref, f_ref, s_ref, cos_ref, sin_ref):
        ang = p_ref[...] * f_ref[...]
        cos_ref[...] = jnp.cos(ang)
        sin_ref[...] = jnp.sin(ang) * s_ref[...]

    return pl.pallas_call(
        body, name=name, grid=(M // tm,),
        in_specs=[pl.BlockSpec((tm, 1), lambda i: (i, 0)),
                  pl.BlockSpec((1, 128), lambda i: (0, 0)),
                  pl.BlockSpec((1, 128), lambda i: (0, 0))],
        out_specs=[pl.BlockSpec((tm, 128), lambda i: (i, 0)),
                   pl.BlockSpec((tm, 128), lambda i: (i, 0))],
        out_shape=[_sds((M, 128), f32), _sds((M, 128), f32)],
        compiler_params=_params(("parallel",)),
    )(pos, inv128, sign128)


def _swap_halves(x):
    w = x.shape[1]
    first = (lax.broadcasted_iota(jnp.int32, (1, w), 1) % HEAD_DIM) < (HEAD_DIM // 2)
    return jnp.where(first, pltpu.roll(x, w - HEAD_DIM // 2, 1), pltpu.roll(x, HEAD_DIM // 2, 1))


def group_sum(groups, cos, sin, *, name, rotate, width, col_block=0, into=None):
    M = groups[0][0].shape[0]
    tm = min(512, M)
    counts = [len(g) for g in groups]
    flat = [a for g in groups for a in g]
    extra = [] if into is None else [into]

    def body(*refs):
        part_refs = refs[:len(flat)]
        c_ref, s_ref = refs[len(flat):len(flat) + 2]
        o_ref = refs[-1]
        cols, k = [], 0
        for n in counts:
            acc = part_refs[k][...]
            for r in part_refs[k + 1:k + n]:
                acc = acc + r[...]
            cols.append(acc)
            k += n
        d = jnp.concatenate(cols, axis=1)
        if rotate:
            c = jnp.tile(c_ref[...], (1, MAIN_W // 128))
            s = jnp.tile(s_ref[...], (1, MAIN_W // 128))
            d = d * c - _swap_halves(d) * s
        o_ref[...] = d.astype(bf16)

    part = pl.BlockSpec((tm, GROUP_W), lambda i: (i, 0))
    tab = pl.BlockSpec((tm, 128), lambda i: (i, 0))
    return pl.pallas_call(
        body, name=name, grid=(M // tm,),
        in_specs=[part] * len(flat) + [tab, tab] + [pl.BlockSpec(memory_space=pl.ANY)] * len(extra),
        out_specs=pl.BlockSpec((tm, MAIN_W), lambda i: (i, col_block)),
        out_shape=_sds((M, width), bf16),
        input_output_aliases={len(flat) + 2: 0} if extra else {},
        compiler_params=_params(("parallel",)),
    )(*flat, cos, sin, *extra)


PAIR_W = 2 * HEAD_DIM
MIN_BLOCKS = 16


FWD_TOGETHER = 4
BWD_TOGETHER = 4


def _dil_geometry(dil):
    nsub = max(dil, MIN_BLOCKS)
    tb = BAND * nsub
    return nsub, tb, SEQ // tb


REGROUP = 4


class _Regrouped:
    def __init__(self, ref):
        self.ref = ref
        self.shape = ref.shape

    def fill(self, src):
        q = self.shape[0] // REGROUP
        for r0 in range(REGROUP):
            self.ref[r0 * q:(r0 + 1) * q, :] = src[pl.ds(r0, q, stride=REGROUP), :]

    def drain(self, dst):
        q = self.shape[0] // REGROUP
        for r0 in range(REGROUP):
            dst[pl.ds(r0, q, stride=REGROUP), :] = self.ref[r0 * q:(r0 + 1) * q, :]

    def rows(self, sub, dil):
        nl, r = divmod(sub, dil)
        start = (r % REGROUP) * (self.shape[0] // REGROUP) + r // REGROUP + nl * BAND * (dil // REGROUP)
        return pl.ds(start, BAND, stride=dil // REGROUP)


def _regroups(dil):
    return dil % (4 * REGROUP) == 0


def _rows(ref, sub, dil):
    if isinstance(ref, _Regrouped):
        return ref.ref[ref.rows(sub, dil), :]
    if dil == 1:
        return ref[sub * BAND:(sub + 1) * BAND, :]
    nl, r = divmod(sub, dil)
    return ref[pl.ds(nl * BAND * dil + r, BAND, stride=dil), :]


def _store_rows(ref, sub, dil, val):
    if isinstance(ref, _Regrouped):
        ref.ref[ref.rows(sub, dil), :] = val
    elif dil == 1:
        ref[sub * BAND:(sub + 1) * BAND, :] = val
    else:
        nl, r = divmod(sub, dil)
        ref[pl.ds(nl * BAND * dil + r, BAND, stride=dil), :] = val


def _keys(prev_ref, own_ref, sub, dil):
    nsub = own_ref.shape[0] // BAND
    if sub >= dil:
        prev = _rows(own_ref, sub - dil, dil)
    elif prev_ref is None:
        return _rows(own_ref, sub, dil)
    else:
        prev = _rows(prev_ref, nsub - dil + sub, dil)
    return jnp.concatenate([prev, _rows(own_ref, sub, dil)], axis=0)


def _band_mask(nkeys, has_prev):
    i = lax.broadcasted_iota(jnp.int32, (BAND, nkeys), 0)
    j = lax.broadcasted_iota(jnp.int32, (BAND, nkeys), 1)
    if nkeys == BAND:
        return j <= i
    return (j >= i) & (j <= i + BAND) & (has_prev | (j >= BAND))


def _first_head():
    return lax.broadcasted_iota(jnp.int32, (1, PAIR_W), 1) < HEAD_DIM


def _col(x, hh):
    return x[:, hh * HEAD_DIM:hh * HEAD_DIM + 1]


def _pair_spec(tb, nblk, col0, which):
    def idx(b, p, i):
        if which < 0:
            i = jnp.maximum(i - 1, 0)
        elif which > 0:
            i = jnp.minimum(i + 1, nblk - 1)
        return (b * nblk + i, col0 + p)
    return pl.BlockSpec((tb, PAIR_W), idx)


def dil_fwd(q, k, kv, g, dil, *, name, n_seq):
    M = q.shape[0]
    nsub, tb, nblk = _dil_geometry(dil)
    with_prev = nblk > 1

    regroup = _regroups(dil)
    assert not (regroup and with_prev)

    def body(*refs):
        if with_prev:
            q_ref, ko_ref, vo_ref, kp_ref, vp_ref, o_ref, l_ref = refs
        else:
            (q_ref, ko_ref, vo_ref, o_ref, l_ref), kp_ref, vp_ref = refs[:5], None, None
        outs_to = ()
        if regroup:
            copies = [_Regrouped(s) for s in refs[5:]]
            for c, src in zip(copies, (q_ref, ko_ref, vo_ref)):
                c.fill(src)
            outs_to = ((copies[3], o_ref), (copies[4], l_ref))
            q_ref, ko_ref, vo_ref, o_ref, l_ref = copies
        first = _first_head()
        blk = pl.program_id(2)
        for sub0 in range(0, nsub, FWD_TOGETHER):
            subs = range(sub0, sub0 + FWD_TOGETHER)
            scores, values = [], []
            for sub in subs:
                qs = _rows(q_ref, sub, dil) * SCALE
                kc = _keys(kp_ref, ko_ref, sub, dil).astype(bf16)
                values.append(_keys(vp_ref, vo_ref, sub, dil).astype(bf16))
                has_prev = True if sub >= dil else blk > 0
                mask = _band_mask(kc.shape[0], has_prev)
                for hh in range(2):
                    qm = jnp.where(first if hh == 0 else ~first, qs, 0.0).astype(bf16)
                    scores.append(jnp.where(mask, lax.dot_general(qm, kc, NT, preferred_element_type=f32), NEG))
            soft = []
            for s in scores:
                m = jnp.max(s, axis=-1, keepdims=True)
                e = jnp.exp(s - m)
                l = jnp.sum(e, axis=-1, keepdims=True)
                soft.append((e.astype(bf16), 1.0 / l, jnp.broadcast_to(m + jnp.log(l), (BAND, PAIR_W))))
            outs = [jnp.dot(e, values[n // 2], preferred_element_type=f32) * inv for n, (e, inv, _) in enumerate(soft)]
            for n, sub in enumerate(subs):
                _store_rows(o_ref, sub, dil, jnp.where(first, outs[2 * n], outs[2 * n + 1]))
                _store_rows(l_ref, sub, dil, jnp.where(first, soft[2 * n][2], soft[2 * n + 1][2]))
        for c, dst in outs_to:
            c.drain(dst)

    ins = [(q, 2 * g, 0), (k, 2 * g, 0), (kv, 6 + 2 * g, 0)]
    if with_prev:
        ins += [(k, 2 * g, -1), (kv, 6 + 2 * g, -1)]
    out = _pair_spec(tb, nblk, 0, 0)
    return pl.pallas_call(
        body, name=name, grid=(n_seq, 2, nblk),
        in_specs=[_pair_spec(tb, nblk, c, w) for _, c, w in ins],
        out_specs=[out, out],
        out_shape=[_sds((M, GROUP_W), f32)] * 2,
        scratch_shapes=[pltpu.VMEM((tb, PAIR_W), f32)] * (5 if regroup else 0),
        compiler_params=_params(("parallel", "parallel", "arbitrary")),
    )(*[a for a, _, _ in ins])


def combine_fwd(os_, lses, *, name):
    M = os_[0].shape[0]
    tm = min(512, M)

    def body(o0, o1, o2, l0, l1, l2, y_ref):
        ls = [l0[...], l1[...], l2[...]]
        m = jnp.maximum(jnp.maximum(ls[0], ls[1]), ls[2])
        es = [jnp.exp(l - m) for l in ls]
        inv = 1.0 / (es[0] + es[1] + es[2])
        y_ref[...] = jnp.concatenate([o[...] * e * inv for o, e in zip((o0, o1, o2), es)], axis=1).astype(bf16)

    part = pl.BlockSpec((tm, GROUP_W), lambda i: (i, 0))
    return pl.pallas_call(
        body, name=name, grid=(M // tm,),
        in_specs=[part] * 6,
        out_specs=pl.BlockSpec((tm, MAIN_W), lambda i: (i, 0)),
        out_shape=_sds((M, D_MODEL), bf16),
        compiler_params=_params(("parallel",)),
    )(*os_, *lses)


def combine_bwd(dyc, os_, lses, *, name):
    M = os_[0].shape[0]
    tm = min(512, M)

    def body(dy_ref, o0, o1, o2, l0, l1, l2, d0, d1, d2, c0, c1, c2):
        r = lax.broadcasted_iota(jnp.int32, (GROUP_W, GROUP_W), 0) // HEAD_DIM
        c = lax.broadcasted_iota(jnp.int32, (GROUP_W, GROUP_W), 1) // HEAD_DIM
        ones = (r == c).astype(bf16)
        dy = dy_ref[...].astype(f32)
        ls = [l0[...], l1[...], l2[...]]
        m = jnp.maximum(jnp.maximum(ls[0], ls[1]), ls[2])
        es = [jnp.exp(l - m) for l in ls]
        inv = 1.0 / (es[0] + es[1] + es[2])
        total = 0.0
        alphas = []
        for g, (o, e, d_ref) in enumerate(zip((o0, o1, o2), es, (d0, d1, d2))):
            a = e * inv
            dyg = dy[:, g * GROUP_W:(g + 1) * GROUP_W]
            d_ref[...] = dyg * a
            prod = dyg * o[...]
            hi = prod.astype(bf16)
            lo = (prod - hi.astype(f32)).astype(bf16)
            dsum = jnp.dot(hi, ones, preferred_element_type=f32) + jnp.dot(lo, ones, preferred_element_type=f32)
            total = total + a * dsum
            alphas.append(a)
        for a, c_ref in zip(alphas, (c0, c1, c2)):
            c_ref[...] = -a * total

    part = pl.BlockSpec((tm, GROUP_W), lambda i: (i, 0))
    outs = pl.pallas_call(
        body, name=name, grid=(M // tm,),
        in_specs=[pl.BlockSpec((tm, MAIN_W), lambda i: (i, 0))] + [part] * 6,
        out_specs=[part] * 6,
        out_shape=[_sds((M, GROUP_W), f32)] * 6,
        compiler_params=_params(("parallel",)),
    )(dyc, *os_, *lses)
    return outs[:3], outs[3:]


def dil_bwd(q, k, kv, do, cc, lse, cos, sin, g, dil, *, name, n_seq, into=None):
    M = q.shape[0]
    nsub = SEQ // BAND
    per_res = nsub // dil
    extra = [] if into is None else [into]

    regroup = _regroups(dil)

    def body(q_ref, k_ref, v_ref, do_ref, c_ref, l_ref, cos_ref, sin_ref, *refs):
        dz_ref, dk_ref, dv_ref, dq_ref, *scratch = refs[len(extra):]
        dq_rows = dq_ref
        outs_to = ()
        if regroup:
            copies = [_Regrouped(s) for s in scratch]
            for c, src in zip(copies, (q_ref, k_ref, v_ref, do_ref, c_ref, l_ref)):
                c.fill(src)
            outs_to = tuple(zip(copies[6:], (dq_rows, dk_ref, dv_ref)))
            q_ref, k_ref, v_ref, do_ref, c_ref, l_ref, dq_ref, dk_ref, dv_ref = copies
        first = _first_head()
        order = [nl * dil + r for r in range(dil) for nl in range(per_res)]
        carry = None
        for at in range(0, nsub, BWD_TOGETHER):
            subs = order[at:at + BWD_TOGETHER]
            loaded, products = [], []
            for sub in subs:
                qs = _rows(q_ref, sub, dil) * SCALE
                dos = _rows(do_ref, sub, dil)
                kc = _keys(None, k_ref, sub, dil).astype(bf16)
                vc = _keys(None, v_ref, sub, dil).astype(bf16)
                mask = _band_mask(kc.shape[0], True)
                for hh in range(2):
                    lm = first if hh == 0 else ~first
                    qm = jnp.where(lm, qs, 0.0).astype(bf16)
                    dom = jnp.where(lm, dos, 0.0).astype(bf16)
                    loaded.append((qm, dom, kc))
                    products.append((jnp.where(mask, lax.dot_general(qm, kc, NT, preferred_element_type=f32), NEG),
                                     lax.dot_general(dom, vc, NT, preferred_element_type=f32)))
            weights = []
            for n, (s, dp) in enumerate(products):
                sub, hh = subs[n // 2], n % 2
                p = jnp.exp(s - _col(_rows(l_ref, sub, dil), hh))
                weights.append((p.astype(bf16), (p * (dp + _col(_rows(c_ref, sub, dil), hh))).astype(bf16)))
            results = []
            for (pb, ds), (qm, dom, kc) in zip(weights, loaded):
                results.append((jnp.dot(ds, kc, preferred_element_type=f32) * SCALE,
                                lax.dot_general(ds, qm, TN, preferred_element_type=f32),
                                lax.dot_general(pb, dom, TN, preferred_element_type=f32)))
            for n, sub in enumerate(subs):
                (dq0, dk0, dv0), (dq1, dk1, dv1) = results[2 * n], results[2 * n + 1]
                _store_rows(dq_ref, sub, dil, jnp.where(first, dq0, dq1))
                dkc, dvc = dk0 + dk1, dv0 + dv1
                if sub >= dil:
                    _store_rows(dk_ref, sub - dil, dil, carry[0] + dkc[:BAND])
                    _store_rows(dv_ref, sub - dil, dil, carry[1] + dvc[:BAND])
                    carry = (dkc[BAND:], dvc[BAND:])
                else:
                    carry = (dkc, dvc)
                if sub + dil >= nsub:
                    _store_rows(dk_ref, sub, dil, carry[0])
                    _store_rows(dv_ref, sub, dil, carry[1])
        for c, dst in outs_to:
            c.drain(dst)
        d = dq_rows[...]
        dz_ref[...] = (d * cos_ref[...] - _swap_halves(d) * sin_ref[...]).astype(bf16)

    def spec(col0):
        return pl.BlockSpec((SEQ, PAIR_W), lambda b, p: (b, col0 + p))

    tab = pl.BlockSpec((SEQ, PAIR_W), lambda b, p: (b, 0))
    out = spec(0)
    return pl.pallas_call(
        body, name=name, grid=(n_seq, 2),
        in_specs=[spec(2 * g), spec(2 * g), spec(6 + 2 * g), spec(0), spec(0), spec(0), tab, tab]
        + [pl.BlockSpec(memory_space=pl.ANY)] * len(extra),
        out_specs=[spec(2 * g), out, out],
        out_shape=[_sds((M, D_MODEL), bf16)] + [_sds((M, GROUP_W), f32)] * 2,
        input_output_aliases={8: 0} if extra else {},
        scratch_shapes=[pltpu.VMEM((SEQ, PAIR_W), f32)] * (10 if regroup else 1),
        compiler_params=_params(("parallel", "parallel")),
    )(q, k, kv, do, cc, lse, cos, sin, *extra)


def _blockdiag(wp):
    out = jnp.zeros((MAIN_W, MAIN_W), wp.dtype)
    for gi in range(len(POOL_WINDOWS)):
        sl = slice(gi * POOL_GROUP, (gi + 1) * POOL_GROUP)
        out = out.at[sl, sl].set(wp[gi])
    return out


def _unblockdiag(w):
    return jnp.stack([w[gi * POOL_GROUP:(gi + 1) * POOL_GROUP, gi * POOL_GROUP:(gi + 1) * POOL_GROUP]
                      for gi in range(len(POOL_WINDOWS))])


def local_step(x, mem, positions, target, P, layer_weights, kv_weight, emit_grads):
    n_seq = x.shape[0]
    M = n_seq * SEQ
    xs = x.reshape(M, D_MODEL)
    mems = mem.reshape(n_seq * N_MEM, D_MODEL)
    pos = positions.reshape(M, 1).astype(f32)
    cos, sin = rope_tables(pos, name="rope_tables")
    gains = P["norm_gains"]

    def gain(l, k):
        return gains[l, k].reshape(1, D_MODEL)

    saved = []
    kvs = None
    for l in range(DEPTH):
        W, started = layer_weights(l, "mix", xs)
        sv = {"x": xs, "W": W}
        z, h1, *qrot = rms_matmul(xs, gain(l, 0), W["w_in"], name=f"l{l}_in", out_dtype=bf16, after=started,
                                  rope=None if l < N_A_LAYERS else (cos, sin))
        kvm, mn = rms_matmul(mems, P["mem_norm"][l].reshape(1, D_MODEL), W["w_mem_kv"],
                             name=f"l{l}_memkv", out_dtype=bf16)
        sv.update(z=z, h1=h1, kvm=kvm, mn=mn)
        if l < N_A_LAYERS:
            wbd = _blockdiag(P["w_pool"][l].astype(bf16))
            psc = P["pool_scale"][l].reshape(1, MAIN_W)
            p, y_main = pool_fwd(z, wbd, psc, name=f"l{l}_pool")
            sv.update(p=p, wbd=wbd, psc=psc)
        else:
            (qrot,) = qrot
            os_, lses = [], []
            for g, (_, dil) in enumerate(DIL_PATTERNS):
                o, lse = dil_fwd(qrot, kvs["krot"], kvs["kv"], g, dil, name=f"l{l}_dil{g}", n_seq=n_seq)
                os_.append(o)
                lses.append(lse)
            y_main = combine_fwd(os_, lses, name=f"l{l}_comb")
            sv.update(qrot=qrot, os=os_, lses=lses)
        ycat = memattn_fwd(z, kvm, y_main, name=f"l{l}_memattn", n_seq=n_seq)
        y, x1 = matmul_rms_res(ycat, W["w_out"], gain(l, 1), xs, name=f"l{l}_out")
        W.update(layer_weights(l, "gu", x1)[0])
        fg, fu, a, h2 = rms_gate_up(x1, gain(l, 2), W["w_gate_up"], name=f"l{l}_gu")
        W_down, passing = layer_weights(l, "down", a)
        W.update(W_down)
        y2, x2, *sq = matmul_rms_res(a, W["w_down"], gain(l, 3), x1, name=f"l{l}_down", after=passing,
                                     target=target.reshape(M, D_MODEL) if l == DEPTH - 1 else None)
        sv.update(ycat=ycat, y=y, x1=x1, fg=fg, fu=fu, h2=h2, a=a, y2=y2)
        saved.append(sv)
        xs = x2
        if l == N_A_LAYERS - 1:
            w_kv = kv_weight(xs)
            kv, hkv, krot = rms_matmul(xs, P["kv_norm"].reshape(1, D_MODEL), w_kv, name="kv_proj", out_dtype=f32,
                                       transposed=True, rope=(cos, sin))
            kvs = {"kv": kv, "hkv": hkv, "krot": krot, "x": xs, "w_kv": w_kv}

    dx, (sq,) = xs, sq

    G = {"mem_norm": [None] * DEPTH, "norm_gains": [[None] * 4 for _ in range(DEPTH)],
         "pool_scale": [None] * N_A_LAYERS}
    dk_parts = [[] for _ in range(N_GROUPS)]
    dv_parts = [[] for _ in range(N_GROUPS)]
    emitted = None

    for l in reversed(range(DEPTH)):
        sv = saved[l]
        W = sv["W"]
        gw = {}
        dy2, dgu, G["norm_gains"][l][3] = down_bwd(sv["y2"], gain(l, 3), dx, W["w_down"], sv["fg"], sv["fu"],
                                                   name=f"l{l}_b_dgu", after=emitted)
        gw["w_down"] = matmul(sv["a"], dy2, TN, name=f"l{l}_b_wd", out_dtype=bf16)
        gw["w_gate_up"] = matmul(dgu, sv["h2"], TN, name=f"l{l}_b_wgu", out_dtype=bf16)
        emitted = emit_grads(l, "ffn", gw)
        dx1, G["norm_gains"][l][2] = matmul_rms_bwd(dgu, W["w_gate_up"], NN, sv["x1"], gain(l, 2), dx,
                                                    name=f"l{l}_b_dh2", after=emitted)
        gw = {}
        dy, dycat, G["norm_gains"][l][1] = rms_bwd_matmul(sv["y"], gain(l, 1), dx1, W["w_out"], NT,
                                                          name=f"l{l}_b_dycat", after=emitted)
        gw["w_out"] = matmul(sv["ycat"], dy, TN, name=f"l{l}_b_wout", out_dtype=bf16)
        if l < N_A_LAYERS:
            dz, dwbd, dps = pool_bwd(dycat, sv["p"], sv["wbd"], sv["psc"], name=f"l{l}_b_pool")
            gw["w_pool"] = _unblockdiag(dwbd).reshape(MAIN_W, POOL_GROUP).astype(bf16)
            G["pool_scale"][l] = dps.reshape(MAIN_W)
        else:
            dos, ccs = combine_bwd(dycat, sv["os"], sv["lses"], name=f"l{l}_b_comb")
            dz = None
            for g, (_, dil) in enumerate(DIL_PATTERNS):
                args = (sv["qrot"], kvs["krot"], kvs["kv"], dos[g], ccs[g], sv["lses"][g], cos, sin, g, dil)
                dz, dk, dv = dil_bwd(*args, name=f"l{l}_b_dil{g}", n_seq=n_seq, into=dz)
                dk_parts[g].append(dk)
                dv_parts[g].append(dv)
        dz, dkvm = memattn_bwd(sv["z"], sv["kvm"], dycat, dz, name=f"l{l}_b_memattn", n_seq=n_seq)
        gw["w_mem_kv"] = matmul(sv["mn"], dkvm, TN, name=f"l{l}_b_wmkv", out_dtype=bf16)
        _, G["mem_norm"][l] = matmul_rms_bwd(dkvm, W["w_mem_kv"], NT, mems, P["mem_norm"][l].reshape(1, D_MODEL),
                                             mems, name=f"l{l}_b_dmn")
        gw["w_in"] = matmul(sv["h1"], dz, TN, name=f"l{l}_b_win", out_dtype=bf16)
        if l != N_A_LAYERS:
            emitted = emit_grads(l, "mix", gw)
        dx, G["norm_gains"][l][0] = matmul_rms_bwd(dz, W["w_in"], NT, sv["x"], gain(l, 0), dx1, name=f"l{l}_b_dh1",
                                                   after=emitted)
        if l == N_A_LAYERS:
            dkv = group_sum(dk_parts, cos, sin, name="b_ropek", rotate=True, width=2 * MAIN_W)
            dkv = group_sum(dv_parts, cos, sin, name="b_sumv", rotate=False, width=2 * MAIN_W, col_block=1, into=dkv)
            gw["w_kv"] = matmul(dkv, kvs["hkv"], TN, name="b_wkv", out_dtype=bf16)
            dx, gkn = matmul_rms_bwd(dkv, kvs["w_kv"], NN, kvs["x"], P["kv_norm"].reshape(1, D_MODEL), dx,
                                     name="b_dhkv")
            G["kv_norm"] = gkn.reshape(D_MODEL)
            emitted = emit_grads(l, "mix", gw)

    small = {"pool_scale": jnp.stack(G["pool_scale"]),
             "mem_norm": jnp.concatenate(G["mem_norm"], axis=0),
             "norm_gains": jnp.stack([jnp.concatenate(r, axis=0) for r in G["norm_gains"]]),
             "kv_norm": G["kv_norm"]}
    return sq[0, 0], dx.reshape(n_seq, SEQ, D_MODEL), small, emitted


def to_bf16_layers(stacks, after, *, name):
    L, n = stacks[0].shape[0], len(stacks)

    def body(*refs):
        ins, outs = refs[:n], refs[n + 1:]
        for j in range(L):
            @pl.when(pl.program_id(0) == j)
            def _():
                for k in range(n):
                    outs[j * n + k][...] = ins[k][...].astype(bf16)

    outs = pl.pallas_call(
        body, name=name, grid=(L,),
        in_specs=[pl.BlockSpec((None,) + s.shape[1:], lambda l: (l, 0, 0)) for s in stacks]
        + [pl.BlockSpec(memory_space=pl.ANY)],
        out_specs=[pl.BlockSpec(s.shape[1:], lambda l: (0, 0)) for _ in range(L) for s in stacks],
        out_shape=[_sds(s.shape[1:], bf16) for _ in range(L) for s in stacks],
        compiler_params=_params(("arbitrary",)),
    )(*stacks, after)
    return [outs[j * n:(j + 1) * n] for j in range(L)]


def _peer(k):
    x, y, c = lax.axis_index("x"), lax.axis_index("y"), lax.axis_index("c")
    px = 1 - x if k & 4 else x
    py = 1 - y if k & 2 else y
    pc = 1 - c if k & 1 else c
    return (px, py, pc), 4 * px + 2 * py + pc


def _my_index():
    return 4 * lax.axis_index("x") + 2 * lax.axis_index("y") + lax.axis_index("c")


def _src_for(kinds, in_refs, i, idx):
    return in_refs[i] if kinds[i] == "gather" else in_refs[i].at[idx]


def _local_copies(kinds, in_refs, out_refs, local_sems):
    me = _my_index()
    return [pltpu.make_async_copy(_src_for(kinds, in_refs, i, me), out_refs[i].at[me], local_sems.at[i])
            for i in range(len(kinds))]


def _remote_copies(kinds, in_refs, out_refs, send_sems, recv_sems, *, arriving):
    me = _my_index()
    copies = []
    for k in range(1, N_DEV):
        dev, idx = _peer(k)
        for i in range(len(kinds)):
            j = i * (N_DEV - 1) + k - 1
            copies.append(pltpu.make_async_remote_copy(
                src_ref=_src_for(kinds, in_refs, i, idx), dst_ref=out_refs[i].at[idx if arriving else me],
                send_sem=send_sems.at[j], recv_sem=recv_sems.at[j], device_id=dev, device_id_type=MESH))
    return copies


def _out_shape(a, kind):
    return ((N_DEV,) + a.shape) if kind == "gather" else a.shape


def exchange(items, *, name, after=()):
    n = len(items)
    kinds = [k for _, k in items]
    after = list(after)

    def body(*refs):
        in_refs, out_refs = refs[:n], refs[n + len(after):2 * n + len(after)]
        send_sems, recv_sems, local_sems = refs[-3:]
        local = _local_copies(kinds, in_refs, out_refs, local_sems)
        sends = _remote_copies(kinds, in_refs, out_refs, send_sems, recv_sems, arriving=False)
        for cp in local + sends:
            cp.start()
        for cp in _remote_copies(kinds, in_refs, out_refs, send_sems, recv_sems, arriving=True):
            cp.wait_recv()
        for cp in sends:
            cp.wait_send()
        for cp in local:
            cp.wait()

    any_spec = pl.BlockSpec(memory_space=pl.ANY)
    return pl.pallas_call(
        body, name=name,
        in_specs=[any_spec] * (n + len(after)), out_specs=[any_spec] * n,
        out_shape=[_sds(_out_shape(a, k), a.dtype) for a, k in items],
        scratch_shapes=[pltpu.SemaphoreType.DMA((n * (N_DEV - 1),)), pltpu.SemaphoreType.DMA((n * (N_DEV - 1),)),
                        pltpu.SemaphoreType.DMA((n,))],
    )(*[a for a, _ in items], *after)


_HBM = pl.BlockSpec(memory_space=pltpu.HBM)
_SEM = pl.BlockSpec(memory_space=pltpu.SEMAPHORE)
_EFFECT = pltpu.SideEffectType.DATAFLOW_SIDE_EFFECTING


def exchange_start(items, after, *, name):
    n = len(items)
    kinds = [k for _, k in items]

    def body(*refs):
        in_refs, land_refs = refs[:n], refs[n:2 * n]
        send_sems, recv_sems, local_sems = refs[2 * n + 1:2 * n + 4]
        token = refs[-1]
        for cp in (_local_copies(kinds, in_refs, land_refs, local_sems)
                   + _remote_copies(kinds, in_refs, land_refs, send_sems, recv_sems, arriving=False)):
            cp.start()
        token[...] = jnp.zeros_like(token)

    srcs = [pltpu.with_memory_space_constraint(a, pltpu.HBM) for a, _ in items]
    lands = [pltpu.with_memory_space_constraint(lax.empty(_out_shape(a, k), a.dtype), pltpu.HBM) for a, k in items]
    outs = pl.pallas_call(
        body, name=name,
        out_shape=(pltpu.SemaphoreType.DMA((n * (N_DEV - 1),)), pltpu.SemaphoreType.DMA((n * (N_DEV - 1),)),
                   pltpu.SemaphoreType.DMA((n,)),
                   *[pltpu.HBM(a.shape, a.dtype) for a in srcs], *[pltpu.HBM(a.shape, a.dtype) for a in lands],
                   _sds((8, 128), f32)),
        in_specs=[_HBM] * (2 * n) + [pl.BlockSpec(memory_space=pl.ANY)],
        out_specs=(_SEM, _SEM, _SEM, *[_HBM] * (2 * n), pl.BlockSpec(memory_space=pltpu.VMEM)),
        input_output_aliases={i: 3 + i for i in range(2 * n)},
        compiler_params=pltpu.CompilerParams(has_side_effects=_EFFECT),
    )(*srcs, *lands, after)
    return {"kinds": kinds, "sems": outs[:3], "srcs": outs[3:3 + n], "lands": outs[3 + n:3 + 2 * n], "token": outs[-1]}


def exchange_wait(handle, after, *, name):
    kinds = handle["kinds"]
    n = len(kinds)

    def body(*refs):
        in_refs, land_refs = refs[:n], refs[n:2 * n]
        send_sems, recv_sems, local_sems = refs[2 * n:2 * n + 3]
        for cp in _remote_copies(kinds, in_refs, land_refs, send_sems, recv_sems, arriving=True):
            cp.wait_recv()
        for cp in _remote_copies(kinds, in_refs, land_refs, send_sems, recv_sems, arriving=False):
            cp.wait_send()
        for cp in _local_copies(kinds, in_refs, land_refs, local_sems):
            cp.wait()

    srcs, lands = list(handle["srcs"]), list(handle["lands"])
    after = list(after) if isinstance(after, (list, tuple)) else [after]
    outs = pl.pallas_call(
        body, name=name,
        out_shape=tuple(pltpu.HBM(a.shape, a.dtype) for a in srcs + lands),
        in_specs=[_HBM] * (2 * n) + [_SEM] * 3 + [pl.BlockSpec(memory_space=pl.ANY)] * len(after),
        out_specs=tuple([_HBM] * (2 * n)),
        input_output_aliases={i: i for i in range(2 * n)},
        compiler_params=pltpu.CompilerParams(has_side_effects=_EFFECT),
    )(*srcs, *lands, *handle["sems"], *after)
    return list(outs[n:])


CHIP_MASKS = (2, 4, 6)


def _g2_first(in_refs, land_refs, send_sems, recv_sems, *, masks, arriving):
    me = _my_index()
    copies = []
    for i in range(len(land_refs)):
        for j, k in enumerate(masks):
            dev, idx = _peer(k)
            dst = land_refs[i].at[idx if arriving else me]
            copies.append(pltpu.make_async_remote_copy(
                src_ref=dst if in_refs is None else in_refs[i], dst_ref=dst,
                send_sem=send_sems.at[i * len(masks) + j], recv_sem=recv_sems.at[i * len(masks) + j],
                device_id=dev, device_id_type=MESH))
    return copies


def _g2_forward(land_refs, fwd_send, fwd_recv, *, arriving):
    sibling, _ = _peer(1)
    copies = []
    for i in range(len(land_refs)):
        for j, k in enumerate(CHIP_MASKS):
            _, idx = _peer(k | 1 if arriving else k)
            copies.append(pltpu.make_async_remote_copy(
                src_ref=land_refs[i].at[idx], dst_ref=land_refs[i].at[idx],
                send_sem=fwd_send.at[i * 3 + j], recv_sem=fwd_recv.at[i * 3 + j], device_id=sibling,
                device_id_type=MESH))
    return copies


def gather2_start(arrays, after, *, name):
    n = len(arrays)

    def body(*refs):
        in_refs, land_refs = refs[:n], refs[n:2 * n]
        ici_send, ici_recv, d2d_send, d2d_recv, local_sems = refs[2 * n + 1:2 * n + 6]
        token = refs[-1]
        ici = _g2_first(in_refs, land_refs, ici_send, ici_recv, masks=CHIP_MASKS, arriving=False)
        d2d = _g2_first(in_refs, land_refs, d2d_send, d2d_recv, masks=(1,), arriving=False)
        for cp in _local_copies(["gather"] * n, in_refs, land_refs, local_sems) + ici + d2d:
            cp.start()
        token[...] = jnp.zeros_like(token)

    srcs = [pltpu.with_memory_space_constraint(a, pltpu.HBM) for a in arrays]
    lands = [pltpu.with_memory_space_constraint(lax.empty((N_DEV,) + a.shape, a.dtype), pltpu.HBM) for a in arrays]
    sem = pltpu.SemaphoreType.DMA
    outs = pl.pallas_call(
        body, name=name,
        out_shape=(sem((3 * n,)), sem((3 * n,)), sem((n,)), sem((n,)), sem((n,)),
                   *[pltpu.HBM(a.shape, a.dtype) for a in srcs], *[pltpu.HBM(a.shape, a.dtype) for a in lands],
                   _sds((8, 128), f32)),
        in_specs=[_HBM] * (2 * n) + [pl.BlockSpec(memory_space=pl.ANY)],
        out_specs=(*[_SEM] * 5, *[_HBM] * (2 * n), pl.BlockSpec(memory_space=pltpu.VMEM)),
        input_output_aliases={i: 5 + i for i in range(2 * n)},
        compiler_params=pltpu.CompilerParams(has_side_effects=_EFFECT),
    )(*srcs, *lands, after)
    return {"n": n, "sems": outs[:5], "srcs": outs[5:5 + n], "lands": outs[5 + n:5 + 2 * n], "token": outs[-1]}


def gather2_forward(handle, after, *, name):
    n = handle["n"]

    def body(*refs):
        land_refs = refs[:n]
        ici_recv = refs[n]
        fwd_send, fwd_recv = refs[n + 2:n + 4]
        for cp in _g2_first(None, land_refs, fwd_send, ici_recv, masks=CHIP_MASKS, arriving=True):
            cp.wait_recv()
        for cp in _g2_forward(land_refs, fwd_send, fwd_recv, arriving=False):
            cp.start()
        token = refs[-1]
        token[...] = jnp.zeros_like(token)

    lands = list(handle["lands"])
    sem = pltpu.SemaphoreType.DMA
    outs = pl.pallas_call(
        body, name=name,
        out_shape=(sem((3 * n,)), sem((3 * n,)), *[pltpu.HBM(a.shape, a.dtype) for a in lands], _sds((8, 128), f32)),
        in_specs=[_HBM] * n + [_SEM, pl.BlockSpec(memory_space=pl.ANY)],
        out_specs=(_SEM, _SEM, *[_HBM] * n, pl.BlockSpec(memory_space=pltpu.VMEM)),
        input_output_aliases={i: 2 + i for i in range(n)},
        compiler_params=pltpu.CompilerParams(has_side_effects=_EFFECT),
    )(*lands, handle["sems"][1], after)
    return dict(handle, fwd=outs[:2], lands=outs[2:2 + n], token=outs[-1])


def gather2_wait(handle, after, *, name):
    n = handle["n"]

    def body(*refs):
        in_refs, land_refs = refs[:n], refs[n:2 * n]
        ici_send, d2d_send, d2d_recv, local_sems, fwd_send, fwd_recv = refs[2 * n:2 * n + 6]
        for cp in _g2_first(in_refs, land_refs, d2d_send, d2d_recv, masks=(1,), arriving=True):
            cp.wait_recv()
        for cp in _g2_forward(land_refs, fwd_send, fwd_recv, arriving=True):
            cp.wait_recv()
        for cp in (_g2_first(in_refs, land_refs, ici_send, fwd_recv, masks=CHIP_MASKS, arriving=False)
                   + _g2_first(in_refs, land_refs, d2d_send, d2d_recv, masks=(1,), arriving=False)
                   + _g2_forward(land_refs, fwd_send, fwd_recv, arriving=False)):
            cp.wait_send()
        for cp in _local_copies(["gather"] * n, in_refs, land_refs, local_sems):
            cp.wait()

    srcs, lands = list(handle["srcs"]), list(handle["lands"])
    s = handle["sems"]
    outs = pl.pallas_call(
        body, name=name,
        out_shape=tuple(pltpu.HBM(a.shape, a.dtype) for a in srcs + lands),
        in_specs=[_HBM] * (2 * n) + [_SEM] * 6 + [pl.BlockSpec(memory_space=pl.ANY)],
        out_specs=tuple([_HBM] * (2 * n)),
        input_output_aliases={i: i for i in range(2 * n)},
        compiler_params=pltpu.CompilerParams(has_side_effects=_EFFECT),
    )(*srcs, *lands, s[0], s[2], s[3], s[4], *handle["fwd"], after)
    return list(outs[n:])


def adamw(entries, *, name):
    c1 = 1.0 - ADAM_B1 ** ADAM_STEP
    c2 = 1.0 - ADAM_B2 ** ADAM_STEP
    tiles = [_tile(w.shape[-2], (64, 32, 16, 8)) for _, w, _, _, _, _ in entries]
    steps = [w.shape[-2] // tr for (_, w, _, _, _, _), tr in zip(entries, tiles)]
    n = len(entries)

    def body(*refs):
        i = pl.program_id(0)
        for e in range(n):
            s_ref, w_ref, m_ref, v_ref = refs[4 * e:4 * e + 4]
            g_ref, d_ref, m2_ref, v2_ref = refs[len(refs) - 4 * n + 4 * e:len(refs) - 4 * n + 4 * e + 4]

            @pl.when(i < steps[e])
            def _():
                g = s_ref[0].astype(f32)
                for d in range(1, N_DEV):
                    g = g + s_ref[d].astype(f32)
                m2 = ADAM_B1 * m_ref[...] + (1.0 - ADAM_B1) * g
                v2 = ADAM_B2 * v_ref[...] + (1.0 - ADAM_B2) * (g * g)
                g_ref[...] = g
                m2_ref[...] = m2
                v2_ref[...] = v2
                d_ref[...] = -ADAM_LR * ((m2 / c1) / (jnp.sqrt(v2 / c2) + ADAM_EPS) + ADAM_WD * w_ref[...])

    in_specs, out_specs, out_shape, args, extras, aliases = [], [], [], [], [], {}
    for e, ((slots, w, m, v, layer, into), tr, ns) in enumerate(zip(entries, tiles, steps)):
        C = w.shape[-1]
        row = lambda i, ns=ns: jnp.minimum(i, ns - 1)
        if layer is None:
            blk = pl.BlockSpec((tr, C), lambda i, row=row: (row(i), 0))
        else:
            blk = pl.BlockSpec((None, tr, C), lambda i, row=row, layer=layer: (layer, row(i), 0))
        in_specs += [pl.BlockSpec((N_DEV, tr, C), lambda i, row=row: (0, row(i), 0)), blk, blk, blk]
        args += [slots, w, m, v]
        out_specs += [blk] * 4
        out_shape += [_sds(w.shape, f32)] * 4
        if into is not None:
            for t, a in enumerate(into):
                aliases[4 * n + len(extras)] = 4 * e + t
                extras.append(a)
    outs = pl.pallas_call(
        body, name=name, grid=(max(steps),),
        in_specs=in_specs + [pl.BlockSpec(memory_space=pl.ANY)] * len(extras),
        out_specs=out_specs, out_shape=out_shape, input_output_aliases=aliases,
        compiler_params=_params(("arbitrary",)),
    )(*args, *extras)
    return [outs[4 * e:4 * e + 4] for e in range(n)]


WEIGHTS = ("norm_gains", "mem_norm", "w_in", "w_mem_kv", "w_out", "w_pool", "pool_scale", "kv_norm", "w_kv",
           "w_gate_up", "w_down")
LAYER_MATS = ("w_in", "w_mem_kv", "w_out", "w_gate_up", "w_down")
POOL_SHARD = MAIN_W // N_DEV
KV_SHARD = 2 * MAIN_W // N_DEV
LOOKAHEAD = 2
TWO_LEVEL_LAYERS = (0, 1, 2)


def _pack_small(gains, pscale):
    lead = gains.shape[:-3]
    g = gains.reshape(lead + (16, 128))
    p = jnp.zeros(lead + (8, 128), f32).at[..., :2, :POOL_SHARD].set(pscale)
    return jnp.concatenate([g, p], axis=-2)


def _unpack_small(a):
    return a[:16].reshape(4, 4, 128), a[16:18, :POOL_SHARD]


def _pack_repl(mem_norm, kv_norm):
    return jnp.concatenate([mem_norm, kv_norm.reshape(1, D_MODEL), jnp.zeros((3, D_MODEL), f32)], axis=0)


def _unpack_repl(a):
    return a[:4], a[4]


def kernel(x, mem, positions, norm_gains, mem_norm, w_in, w_mem_kv, w_out, w_pool, pool_scale, kv_norm, w_kv, w_gate_up, w_down, loss_target, m_norm_gains, m_mem_norm, m_w_in, m_w_mem_kv, m_w_out, m_w_pool, m_pool_scale, m_kv_norm, m_w_kv, m_w_gate_up, m_w_down, v_norm_gains, v_mem_norm, v_w_in, v_w_mem_kv, v_w_out, v_w_pool, v_pool_scale, v_kv_norm, v_w_kv, v_w_gate_up, v_w_down):
    w = dict(norm_gains=norm_gains, mem_norm=mem_norm, w_in=w_in, w_mem_kv=w_mem_kv, w_out=w_out, w_pool=w_pool,
             pool_scale=pool_scale, kv_norm=kv_norm, w_kv=w_kv, w_gate_up=w_gate_up, w_down=w_down)
    m = dict(norm_gains=m_norm_gains, mem_norm=m_mem_norm, w_in=m_w_in, w_mem_kv=m_w_mem_kv, w_out=m_w_out,
             w_pool=m_w_pool, pool_scale=m_pool_scale, kv_norm=m_kv_norm, w_kv=m_w_kv, w_gate_up=m_w_gate_up,
             w_down=m_w_down)
    v = dict(norm_gains=v_norm_gains, mem_norm=v_mem_norm, w_in=v_w_in, w_mem_kv=v_w_mem_kv, w_out=v_w_out,
             w_pool=v_w_pool, pool_scale=v_pool_scale, kv_norm=v_kv_norm, w_kv=v_w_kv, w_gate_up=v_w_gate_up,
             w_down=v_w_down)

    def transposed_view(d):
        d = dict(d)
        d["w_gate_up"] = jnp.swapaxes(d["w_gate_up"], 1, 2)
        d["w_kv"] = jnp.swapaxes(d["w_kv"], 0, 1)
        return d

    wv, mv, vv = transposed_view(w), transposed_view(m), transposed_view(v)

    PARTS = {"mix": ("w_in", "w_mem_kv", "w_out"), "ffn": ("w_gate_up", "w_down"), "gu": ("w_gate_up",),
             "down": ("w_down",), "all": ("w_in", "w_mem_kv", "w_out", "w_gate_up", "w_down")}

    small = _pack_small(norm_gains, pool_scale)
    handles = {(0, "mix"): gather2_start([wv[k][0].astype(bf16) for k in PARTS["mix"]] + [small], x,
                                         name="gather_start_mix_l0")}
    first = handles[0, "mix"]["token"]

    def parts_of(l):
        return (("mix", "gu", "down"), ("mix", "ffn"))[l] if l < 2 else ("all",)

    wb = [dict(zip(LAYER_MATS, mats))
          for mats in to_bf16_layers([wv[k] for k in LAYER_MATS], first, name="weights_bf16")]

    def part_items(l, part):
        items = [(wb[l][k], "gather") for k in PARTS[part]]
        if part == "ffn" and l == N_A_LAYERS - 1:
            items.append((wv["w_kv"].astype(bf16), "gather"))
        return items

    def start_layer(l, after):
        for part in parts_of(l):
            if (l, part) in handles:
                continue
            if l in TWO_LEVEL_LAYERS:
                handles[l, part] = gather2_start([a for a, _ in part_items(l, part)], after,
                                                 name=f"gather_start_{part}_l{l}")
            else:
                handles[l, part] = exchange_start(part_items(l, part), after, name=f"gather_start_{part}_l{l}")
            after = handles[l, part]["token"]
        return after

    token = first
    for l in range(LOOKAHEAD):
        token = start_layer(l, token)
    landed, fetched = {}, {}

    passed_early = {}

    def layer_weights(l, part, after):
        if (l, part) in fetched:
            return fetched[l, part]
        if part not in parts_of(l):
            if part == "down" and l + 1 in TWO_LEVEL_LAYERS and parts_of(l + 1) == ("all",):
                passed_early[l + 1] = gather2_forward(handles[l + 1, "all"], after, name=f"gather_forward_all_l{l + 1}")
                return {}, passed_early[l + 1]["token"]
            if part == "down" or (part == "gu" and "all" in parts_of(l)):
                return {}, None
            part = "all" if "all" in parts_of(l) else "ffn"
        if l == 0 and part == "mix":
            after = token
        if l in TWO_LEVEL_LAYERS:
            passed = passed_early.pop(l, None)
            if passed is None:
                passed = gather2_forward(handles[l, part], after, name=f"gather_forward_{part}_l{l}")
            got = gather2_wait(passed, after, name=f"gather_wait_{part}_l{l}")
        else:
            got = exchange_wait(handles[l, part], after, name=f"gather_wait_{part}_l{l}")
        landed[l, part] = got
        started = None
        if part in ("mix", "all") and l + LOOKAHEAD < DEPTH:
            started = start_layer(l + LOOKAHEAD, got[0])
        W = {k: g.reshape(-1, g.shape[-1]) for k, g in zip(PARTS[part], got)}
        fetched[l, part] = (W, started)
        return W, started

    layer_weights(0, "mix", x)
    gsmall = landed[0, "mix"][len(PARTS["mix"])]
    P = {"norm_gains": jnp.moveaxis(gsmall[:, :16].reshape(N_DEV, 4, 4, 128), 0, 2).reshape(4, 4, D_MODEL),
         "pool_scale": jnp.moveaxis(gsmall[:, 16:18, :POOL_SHARD], 0, 1).reshape(2, MAIN_W),
         "mem_norm": mem_norm, "kv_norm": kv_norm, "w_pool": w_pool}

    def kv_weight(after):
        g = landed[N_A_LAYERS - 1, "ffn"][len(PARTS["ffn"])]
        return g.reshape(2 * MAIN_W, D_MODEL)

    ghandles = {}

    pending = {}

    def gparts_of(l):
        return ("ffn", "mix") if l < 2 else ("all",)

    def emit_grads(l, part, gw):
        if part not in gparts_of(l):
            pending.setdefault(l, {}).update(gw)
            if part == "ffn":
                return None
            gw, part = pending[l], "all"
        items = [(gw[k].reshape((N_DEV, -1) + gw[k].shape[-1:]), "scatter") for k in PARTS[part]]
        if part != "ffn" and l == N_A_LAYERS:
            items.append((gw["w_kv"].reshape(N_DEV, KV_SHARD, D_MODEL), "scatter"))
        if part != "ffn" and l < N_A_LAYERS:
            items.append((gw["w_pool"], "gather"))
        ghandles[l, part] = exchange_start(items, gsmall, name=f"scatter_start_{part}_l{l}")
        return ghandles[l, part]["token"]

    sq, grad_x, GS, emitted = local_step(x, mem, positions, loss_target, P, layer_weights, kv_weight, emit_grads)

    def pool3(a):
        return a.reshape(N_A_LAYERS, MAIN_W, POOL_GROUP)

    out = {}
    after = [emitted]

    def finish_layer(l, after):
        for part in gparts_of(l):
            got = exchange_wait(ghandles[l, part], after, name=f"scatter_wait_{part}_l{l}")
            names = list(PARTS[part])
            entries = [(slots, wv[k], mv[k], vv[k], l, out.get(k)) for k, slots in zip(names, got)]
            if part != "ffn" and l == N_A_LAYERS:
                names.append("w_kv")
                entries.append((got[-1], wv["w_kv"], mv["w_kv"], vv["w_kv"], None, None))
            if part != "ffn" and l < N_A_LAYERS:
                names.append("w_pool")
                entries.append((got[-1], pool3(w_pool), pool3(m_w_pool), pool3(v_w_pool), l, out.get("w_pool")))
            out.update(zip(names, adamw(entries, name=f"adamw_{part}_l{l}")))
            after = [out[k][0] for k in names]
        return after

    for l in reversed(range(1, DEPTH)):
        after = finish_layer(l, after)

    gs = _pack_small(jnp.moveaxis(GS["norm_gains"].reshape(4, 4, N_DEV, 128), 2, 0),
                     jnp.moveaxis(GS["pool_scale"].reshape(2, N_DEV, POOL_SHARD), 1, 0))
    parts_small, parts_repl, parts_sq = exchange(
        [(gs, "scatter"), (_pack_repl(GS["mem_norm"], GS["kv_norm"]), "gather"),
         (jnp.full((8, 128), sq, f32), "gather")],
        name="exchange_small_grads", after=after)
    loss = (0.5 / D_MODEL) * jnp.sum(parts_sq[:, 0, 0])
    finish_layer(0, [parts_small])
    out["w_gate_up"] = [jnp.swapaxes(r, 1, 2) for r in out["w_gate_up"]]
    out["w_kv"] = [jnp.swapaxes(r, 0, 1) for r in out["w_kv"]]
    out["w_pool"] = [r.reshape(w_pool.shape) for r in out["w_pool"]]

    res_small, res_repl = adamw(
        [(parts_small, small, _pack_small(m_norm_gains, m_pool_scale), _pack_small(v_norm_gains, v_pool_scale),
          None, None),
         (parts_repl, _pack_repl(mem_norm, kv_norm), _pack_repl(m_mem_norm, m_kv_norm),
          _pack_repl(v_mem_norm, v_kv_norm), None, None)], name="adamw_small")
    out["norm_gains"], out["pool_scale"] = zip(*[_unpack_small(r) for r in res_small])
    out["mem_norm"], out["kv_norm"] = zip(*[_unpack_repl(r) for r in res_repl])

    return (loss, grad_x, *[out[k][0] for k in WEIGHTS], *[out[k][1] for k in WEIGHTS],
            *[out[k][2] for k in WEIGHTS], *[out[k][3] for k in WEIGHTS])
```

```python
import numpy as np
import jax
import jax.numpy as jnp
from jax import lax
from jax.experimental import pallas as pl
from jax.experimental.pallas import tpu as pltpu

f32 = jnp.float32
bf16 = jnp.bfloat16

D_MODEL = 1024
SEQ = 2048
DEPTH = 4
N_MEM = 256
HEAD_DIM = 64
N_MEM_HEADS = 4
MEM_W = 256
MAIN_W = 768
POOL_WINDOWS = (2, 4, 8, 16)
POOL_GROUP = 192
POOL_HALO = 16
DIL_PATTERNS = ((128, 1), (512, 4), (2048, 16))
N_GROUPS = 3
GROUP_W = 256
BAND = 128
N_A_LAYERS = 2
D_FF = 2816
ROPE_THETA = 10000.0
EPS = 1e-6
NEG = -1e30
SCALE = HEAD_DIM ** -0.5
N_DEV = 8

ADAM_LR = 0.001
ADAM_B1 = 0.9
ADAM_B2 = 0.999
ADAM_EPS = 1e-08
ADAM_WD = 0.01
ADAM_STEP = 10

VMEM_LIMIT_BYTES = 56 * 1024 * 1024
MESH = pl.DeviceIdType.MESH

NN = (((1,), (0,)), ((), ()))
NT = (((1,), (1,)), ((), ()))
TN = (((0,), (0,)), ((), ()))


def _params(sem=None):
    return pltpu.CompilerParams(dimension_semantics=sem, vmem_limit_bytes=VMEM_LIMIT_BYTES)


def _tile(n, cands):
    for c in cands:
        if n % c == 0:
            return c
    return n


def _sds(shape, dtype):
    return jax.ShapeDtypeStruct(tuple(shape), dtype)


def _rms_r(v):
    return lax.rsqrt(jnp.mean(v * v, axis=-1, keepdims=True) + EPS)


ROW_GROUPS = 2


def rms_matmul(x, gain, w, *, name, out_dtype, transposed=False, after=None, rope=None):
    M, K = x.shape
    N = w.shape[0] if transposed else w.shape[1]
    tm = min(512, M)
    order = [] if after is None else [after]
    tables = [] if rope is None else list(rope)
    rot_spec = [] if rope is None else [pl.BlockSpec((tm, MAIN_W), lambda i: (i, 0))]
    rot_shape = [] if rope is None else [_sds((M, MAIN_W), f32)]

    def body(x_ref, g_ref, w_ref, *refs):
        z_ref, h_ref = refs[len(tables) + len(order):][:2]
        groups = [slice(g * tm // ROW_GROUPS, (g + 1) * tm // ROW_GROUPS) for g in range(ROW_GROUPS)]
        hs = []
        for rows in groups:
            xv = x_ref[rows, :]
            hs.append((xv * _rms_r(xv) * g_ref[...]).astype(bf16))
            h_ref[rows, :] = hs[-1]
        zs = [lax.dot_general(h, w_ref[...], NT if transposed else NN, preferred_element_type=f32) for h in hs]
        for rows, z in zip(groups, zs):
            z_ref[rows, :] = z.astype(z_ref.dtype)
            if tables:
                c = jnp.tile(refs[0][rows, :], (1, MAIN_W // 128))
                s = jnp.tile(refs[1][rows, :], (1, MAIN_W // 128))
                zr = z[:, :MAIN_W]
                refs[-1][rows, :] = zr * c + _swap_halves(zr) * s

    tab = pl.BlockSpec((tm, 128), lambda i: (i, 0))
    return pl.pallas_call(
        body, name=name, grid=(M // tm,),
        in_specs=[pl.BlockSpec((tm, K), lambda i: (i, 0)),
                  pl.BlockSpec((1, K), lambda i: (0, 0)),
                  pl.BlockSpec(w.shape, lambda i: (0, 0))] + [tab] * len(tables)
        + [pl.BlockSpec(memory_space=pl.ANY)] * len(order),
        out_specs=[pl.BlockSpec((tm, N), lambda i: (i, 0)), pl.BlockSpec((tm, K), lambda i: (i, 0))] + rot_spec,
        out_shape=[_sds((M, N), out_dtype), _sds((M, K), bf16)] + rot_shape,
        compiler_params=_params(("parallel",)),
    )(x, gain, w, *tables, *order)


def matmul_rms_res(a, w, gain, res, *, name, target=None, after=None):
    M, K = a.shape
    N = w.shape[1]
    tm = min(512, M)
    goal = [] if target is None else [target]
    order = [] if after is None else [after]

    def body(a_ref, w_ref, g_ref, r_ref, *refs):
        y_ref, x_ref = refs[len(goal) + len(order):][:2]
        groups = [slice(g * tm // ROW_GROUPS, (g + 1) * tm // ROW_GROUPS) for g in range(ROW_GROUPS)]
        ys = [jnp.dot(a_ref[rows, :], w_ref[...], preferred_element_type=f32) for rows in groups]
        sq = 0.0
        for rows, y in zip(groups, ys):
            y_ref[rows, :] = y.astype(bf16)
            x = r_ref[rows, :] + y * _rms_r(y) * g_ref[...]
            if goal:
                e = x - refs[0][rows, :]
                x = e * (1.0 / N)
                sq = sq + jnp.sum(jnp.sum(e * e, axis=0, keepdims=True), axis=1, keepdims=True)
            x_ref[rows, :] = x
        if goal:
            _accumulate(refs[-1], sq)

    row = pl.BlockSpec((tm, N), lambda i: (i, 0))
    return pl.pallas_call(
        body, name=name, grid=(M // tm,),
        in_specs=[pl.BlockSpec((tm, K), lambda i: (i, 0)),
                  pl.BlockSpec((K, N), lambda i: (0, 0)),
                  pl.BlockSpec((1, N), lambda i: (0, 0)),
                  row] + [row] * len(goal) + [pl.BlockSpec(memory_space=pl.ANY)] * len(order),
        out_specs=[row, row] + [pl.BlockSpec((8, 128), lambda i: (0, 0))] * len(goal),
        out_shape=[_sds((M, N), bf16), _sds((M, N), f32)] + [_sds((8, 128), f32)] * len(goal),
        compiler_params=_params(("arbitrary",) if goal else ("parallel",)),
    )(a, w, gain, res, *goal, *order)


def matmul(a, b, dims, *, name, out_dtype):
    if dims is TN:
        K, M = a.shape
        tm = _tile(M, (512, 256, 128))
        a_spec = pl.BlockSpec((K, tm), lambda i: (0, i))
    else:
        M, K = a.shape
        tm = _tile(M, (1024, 512, 256, 128))
        a_spec = pl.BlockSpec((tm, K), lambda i: (i, 0))
    N = b.shape[0] if dims is NT else b.shape[1]

    def body(a_ref, b_ref, o_ref):
        o_ref[...] = lax.dot_general(a_ref[...].astype(bf16), b_ref[...].astype(bf16), dims,
                                     preferred_element_type=f32).astype(o_ref.dtype)

    return pl.pallas_call(
        body, name=name, grid=(M // tm,),
        in_specs=[a_spec, pl.BlockSpec(b.shape, lambda i: (0, 0))],
        out_specs=pl.BlockSpec((tm, N), lambda i: (i, 0)),
        out_shape=_sds((M, N), out_dtype),
        compiler_params=_params(("parallel",)),
    )(a, b)


def rms_gate_up(x, gain, wt, *, name):
    M, K = x.shape
    tm = min(2048, M)
    tn = _tile(D_FF, (256, 128))
    nj = D_FF // tn

    def body(x_ref, gn_ref, wg_ref, wu_ref, g_ref, u_ref, a_ref, h_ref):
        @pl.when(pl.program_id(1) == 0)
        def _():
            xv = x_ref[...]
            h_ref[...] = (xv * _rms_r(xv) * gn_ref[...]).astype(bf16)

        h = h_ref[...]
        g = lax.dot_general(h, wg_ref[...], NT, preferred_element_type=f32).astype(bf16)
        u = lax.dot_general(h, wu_ref[...], NT, preferred_element_type=f32).astype(bf16)
        g_ref[...] = g
        u_ref[...] = u
        a_ref[...] = g * (1.0 / (1.0 + jnp.exp(-g))) * u

    col = pl.BlockSpec((tm, tn), lambda i, j: (i, j))
    return pl.pallas_call(
        body, name=name, grid=(M // tm, nj),
        in_specs=[pl.BlockSpec((tm, K), lambda i, j: (i, 0)),
                  pl.BlockSpec((1, K), lambda i, j: (0, 0)),
                  pl.BlockSpec((tn, K), lambda i, j: (j, 0)),
                  pl.BlockSpec((tn, K), lambda i, j: (j + nj, 0))],
        out_specs=[col, col, col, pl.BlockSpec((tm, K), lambda i, j: (i, 0))],
        out_shape=[_sds((M, D_FF), bf16)] * 3 + [_sds((M, K), bf16)],
        compiler_params=_params(("parallel", "arbitrary")),
    )(x, gain, wt, wt)


def _rms_bwd_math(yv, gain, dn):
    r = _rms_r(yv)
    q = dn * gain
    dy = r * q - yv * (r * r * r) * jnp.mean(q * yv, axis=-1, keepdims=True)
    return dy, jnp.sum(dn * yv * r, axis=0, keepdims=True)


def _accumulate(ref, val):
    @pl.when(pl.program_id(0) == 0)
    def _():
        ref[...] = jnp.zeros_like(ref)

    ref[...] += val


def down_bwd(y, gain, dn, w_down, g, u, *, name, after=None):
    M, K = y.shape
    tm = min(512, M)
    order = [] if after is None else [after]

    def body(y_ref, gn_ref, dn_ref, w_ref, g_ref, u_ref, *refs):
        dy_ref, o_ref, dg_ref = refs[len(order):]
        groups = [slice(a * tm // ROW_GROUPS, (a + 1) * tm // ROW_GROUPS) for a in range(ROW_GROUPS)]
        dys, dgain = [], 0.0
        for rows in groups:
            dy, part = _rms_bwd_math(y_ref[rows, :].astype(f32), gn_ref[...], dn_ref[rows, :])
            dys.append(dy.astype(bf16))
            dy_ref[rows, :] = dys[-1]
            dgain = dgain + part
        _accumulate(dg_ref, dgain)
        das = [lax.dot_general(dy, w_ref[...], NT, preferred_element_type=f32).astype(bf16) for dy in dys]
        for rows, da in zip(groups, das):
            g = g_ref[rows, :]
            s = 1.0 / (1.0 + jnp.exp(-g))
            o_ref[rows, :D_FF] = da * u_ref[rows, :] * s * (1.0 + g * (1.0 - s))
            o_ref[rows, D_FF:] = da * g * s

    row = pl.BlockSpec((tm, K), lambda i: (i, 0))
    vec = pl.BlockSpec((1, K), lambda i: (0, 0))
    wide = pl.BlockSpec((tm, D_FF), lambda i: (i, 0))
    return pl.pallas_call(
        body, name=name, grid=(M // tm,),
        in_specs=[row, vec, row, pl.BlockSpec((D_FF, K), lambda i: (0, 0)), wide, wide]
        + [pl.BlockSpec(memory_space=pl.ANY)] * len(order),
        out_specs=[row, pl.BlockSpec((tm, 2 * D_FF), lambda i: (i, 0)), vec],
        out_shape=[_sds((M, K), bf16), _sds((M, 2 * D_FF), bf16), _sds((1, K), f32)],
        compiler_params=_params(("arbitrary",)),
    )(y, gain, dn, w_down, g, u, *order)


def rms_bwd_matmul(y, gain, dn, w, dims, *, name, after=None):
    M, K = y.shape
    N = w.shape[0] if dims is NT else w.shape[1]
    tm = min(1024, M)
    order = [] if after is None else [after]

    def body(y_ref, gn_ref, dn_ref, w_ref, *refs):
        dy_ref, o_ref, dg_ref = refs[len(order):]
        groups = [slice(g * tm // ROW_GROUPS, (g + 1) * tm // ROW_GROUPS) for g in range(ROW_GROUPS)]
        dys, dgain = [], 0.0
        for rows in groups:
            dy, part = _rms_bwd_math(y_ref[rows, :].astype(f32), gn_ref[...], dn_ref[rows, :].astype(f32))
            dys.append(dy.astype(bf16))
            dy_ref[rows, :] = dys[-1]
            dgain = dgain + part
        _accumulate(dg_ref, dgain)
        for rows, dy in zip(groups, dys):
            o_ref[rows, :] = lax.dot_general(dy, w_ref[...], dims, preferred_element_type=f32).astype(bf16)

    row = pl.BlockSpec((tm, K), lambda i: (i, 0))
    vec = pl.BlockSpec((1, K), lambda i: (0, 0))
    return pl.pallas_call(
        body, name=name, grid=(M // tm,),
        in_specs=[row, vec, row, pl.BlockSpec(w.shape, lambda i: (0, 0))]
        + [pl.BlockSpec(memory_space=pl.ANY)] * len(order),
        out_specs=[row, pl.BlockSpec((tm, N), lambda i: (i, 0)), vec],
        out_shape=[_sds((M, K), bf16), _sds((M, N), bf16), _sds((1, K), f32)],
        compiler_params=_params(("arbitrary",)),
    )(y, gain, dn, w, *order)


W_CHUNKS = 4
W_BY_HAND_MIN_K = 4096


def matmul_rms_bwd(a, b, dims, y, gain, res, *, name, after=None):
    M, K = a.shape
    N = y.shape[1]
    tm = min(512, M)
    order = [] if after is None else [after]
    by_hand = dims is NN and K >= W_BY_HAND_MIN_K
    kc = K // W_CHUNKS

    def body(a_ref, b_ref, y_ref, gn_ref, r_ref, *refs):
        dx_ref, dg_ref = refs[len(order):][:2]
        groups = [slice(g * tm // ROW_GROUPS, (g + 1) * tm // ROW_GROUPS) for g in range(ROW_GROUPS)]
        if by_hand:
            w_ref, sems = refs[-2:]
            copies = [pltpu.make_async_copy(b_ref.at[pl.ds(c * kc, kc), :], w_ref.at[pl.ds(c * kc, kc), :], sems.at[c])
                      for c in range(W_CHUNKS)]

            @pl.when(pl.program_id(0) == 0)
            def _():
                for cp in copies:
                    cp.start()

            dns = [0.0] * ROW_GROUPS
            for c, cp in enumerate(copies):
                pl.when(pl.program_id(0) == 0)(cp.wait)
                cols = slice(c * kc, (c + 1) * kc)
                dns = [dn + jnp.dot(a_ref[rows, cols].astype(bf16), w_ref[cols, :], preferred_element_type=f32)
                       for dn, rows in zip(dns, groups)]
        else:
            dns = [lax.dot_general(a_ref[rows, :].astype(bf16), b_ref[...], dims, preferred_element_type=f32)
                   for rows in groups]
        dgain = 0.0
        for rows, dn in zip(groups, dns):
            dy, part = _rms_bwd_math(y_ref[rows, :], gn_ref[...], dn)
            dx_ref[rows, :] = dy + r_ref[rows, :]
            dgain = dgain + part
        _accumulate(dg_ref, dgain)

    row = pl.BlockSpec((tm, N), lambda i: (i, 0))
    vec = pl.BlockSpec((1, N), lambda i: (0, 0))
    b_spec = pl.BlockSpec(memory_space=pl.ANY) if by_hand else pl.BlockSpec(b.shape, lambda i: (0, 0))
    return pl.pallas_call(
        body, name=name, grid=(M // tm,),
        in_specs=[pl.BlockSpec((tm, K), lambda i: (i, 0)), b_spec, row, vec, row]
        + [pl.BlockSpec(memory_space=pl.ANY)] * len(order),
        out_specs=[row, vec],
        out_shape=[_sds((M, N), f32), _sds((1, N), f32)],
        scratch_shapes=[pltpu.VMEM(b.shape, b.dtype), pltpu.SemaphoreType.DMA((W_CHUNKS,))] if by_hand else [],
        compiler_params=_params(("arbitrary",)),
    )(a, b, y, gain, res, *order)


def _pool_select(a1, a2, a3, a4):
    col = lax.broadcasted_iota(jnp.int32, (1, MAIN_W), 1) // POOL_GROUP
    return jnp.where(col == 0, a1, jnp.where(col == 1, a2, jnp.where(col == 2, a3, a4)))


def _pool_count(t):
    col = lax.broadcasted_iota(jnp.int32, (1, MAIN_W), 1) // POOL_GROUP
    win = jnp.where(col == 0, 2, jnp.where(col == 1, 4, jnp.where(col == 2, 8, 16)))
    return jnp.minimum(t + 1, win).astype(f32)


def pool_fwd(z, wbd, scale, *, name):
    M = z.shape[0]
    tm = 512
    nper = SEQ // tm
    hb = tm // POOL_HALO

    def body(zc_ref, zh_ref, w_ref, s_ref, p_ref, y_ref):
        i = pl.program_id(0)
        seq_blk = i % nper
        halo = jnp.where(seq_blk == 0, 0.0, zh_ref[...].astype(f32))
        u = zc_ref[...].astype(f32)
        ext = jnp.concatenate([halo, u], axis=0)
        a1 = ext + pltpu.roll(ext, 1, 0)
        a2 = a1 + pltpu.roll(a1, 2, 0)
        a3 = a2 + pltpu.roll(a2, 4, 0)
        a4 = a3 + pltpu.roll(a3, 8, 0)
        sums = _pool_select(a1, a2, a3, a4)[POOL_HALO:]
        t = seq_blk * tm + lax.broadcasted_iota(jnp.int32, (tm, 1), 0)
        p = (sums / _pool_count(t) - u).astype(bf16)
        p_ref[...] = p
        y_ref[...] = (jnp.dot(p, w_ref[...], preferred_element_type=f32) * s_ref[...]).astype(bf16)

    return pl.pallas_call(
        body, name=name, grid=(M // tm,),
        in_specs=[pl.BlockSpec((tm, MAIN_W), lambda i: (i, 0)),
                  pl.BlockSpec((POOL_HALO, MAIN_W), lambda i: (jnp.maximum(i * hb - 1, 0), 0)),
                  pl.BlockSpec((MAIN_W, MAIN_W), lambda i: (0, 0)),
                  pl.BlockSpec((1, MAIN_W), lambda i: (0, 0))],
        out_specs=[pl.BlockSpec((tm, MAIN_W), lambda i: (i, 0)),
                   pl.BlockSpec((tm, MAIN_W), lambda i: (i, 0))],
        out_shape=[_sds((M, MAIN_W), bf16), _sds((M, D_MODEL), bf16)],
        compiler_params=_params(("parallel",)),
    )(z, z, wbd, scale)


def pool_bwd(dyc, p, wbd, scale, *, name):
    M = p.shape[0]
    tm = 512
    nper = SEQ // tm
    hb = tm // POOL_HALO
    last_hb = M // POOL_HALO - 1

    def body(dy_ref, dyh_ref, p_ref, w_ref, s_ref, dz_ref, dw_ref, ds_ref):
        i = pl.program_id(0)
        seq_blk = i % nper
        dy = dy_ref[...].astype(f32)
        pv = p_ref[...]
        w = w_ref[...]
        sc = s_ref[...]

        @pl.when(i == 0)
        def _():
            dw_ref[...] = jnp.zeros_like(dw_ref)
            ds_ref[...] = jnp.zeros_like(ds_ref)

        v = jnp.dot(pv, w, preferred_element_type=f32)
        ds_ref[...] += jnp.sum(dy * v, axis=0, keepdims=True)
        dv = (dy * sc).astype(bf16)
        dw_ref[...] += lax.dot_general(pv, dv, TN, preferred_element_type=f32)
        dp = lax.dot_general(dv, w, NT, preferred_element_type=f32)
        dvh = jnp.where(seq_blk == nper - 1, 0.0, dyh_ref[...].astype(f32) * sc).astype(bf16)
        dph = lax.dot_general(dvh, w, NT, preferred_element_type=f32)
        ext = jnp.concatenate([dp, dph], axis=0)
        n = tm + POOL_HALO
        t = seq_blk * tm + lax.broadcasted_iota(jnp.int32, (n, 1), 0)
        e = ext / _pool_count(t)
        b1 = e + pltpu.roll(e, n - 1, 0)
        b2 = b1 + pltpu.roll(b1, n - 2, 0)
        b3 = b2 + pltpu.roll(b2, n - 4, 0)
        b4 = b3 + pltpu.roll(b3, n - 8, 0)
        dz_ref[...] = (_pool_select(b1, b2, b3, b4)[:tm] - dp).astype(dz_ref.dtype)

    return pl.pallas_call(
        body, name=name, grid=(M // tm,),
        in_specs=[pl.BlockSpec((tm, MAIN_W), lambda i: (i, 0)),
                  pl.BlockSpec((POOL_HALO, MAIN_W), lambda i: (jnp.minimum((i + 1) * hb, last_hb), 0)),
                  pl.BlockSpec((tm, MAIN_W), lambda i: (i, 0)),
                  pl.BlockSpec((MAIN_W, MAIN_W), lambda i: (0, 0)),
                  pl.BlockSpec((1, MAIN_W), lambda i: (0, 0))],
        out_specs=[pl.BlockSpec((tm, MAIN_W), lambda i: (i, 0)),
                   pl.BlockSpec((MAIN_W, MAIN_W), lambda i: (0, 0)),
                   pl.BlockSpec((1, MAIN_W), lambda i: (0, 0))],
        out_shape=[_sds((M, D_MODEL), bf16), _sds((MAIN_W, MAIN_W), f32), _sds((1, MAIN_W), f32)],
        compiler_params=_params(("arbitrary",)),
    )(dyc, dyc, p, wbd, scale)


def _mem_heads(q, kv):
    first = _first_head()
    for pr in range(N_MEM_HEADS // 2):
        cols = slice(pr * PAIR_W, (pr + 1) * PAIR_W)
        qp = q[:, cols] * SCALE
        kp = kv[:, cols]
        vp = kv[:, MEM_W + pr * PAIR_W: MEM_W + (pr + 1) * PAIR_W]
        for hh in range(2):
            lm = first if hh == 0 else ~first
            qm = jnp.where(lm, qp, 0.0).astype(bf16)
            s = lax.dot_general(qm, kp, NT, preferred_element_type=f32)
            e = jnp.exp(s - jnp.max(s, axis=-1, keepdims=True))
            yield lm, qm, kp, vp, e, jnp.sum(e, axis=-1, keepdims=True)


def memattn_fwd(z, kvm, ycat, *, name, n_seq):
    M = z.shape[0]
    tq = SEQ
    nq = SEQ // tq

    def body(q_ref, kv_ref, _, o_ref):
        first = _first_head()
        outs = []
        for lm, _, _, vp, e, l in _mem_heads(q_ref[...], kv_ref[...]):
            outs.append(jnp.dot(e.astype(bf16), vp, preferred_element_type=f32) * (1.0 / l))
        pairs = [jnp.where(first, outs[2 * pr], outs[2 * pr + 1]) for pr in range(N_MEM_HEADS // 2)]
        o_ref[...] = jnp.concatenate(pairs, axis=1).astype(bf16)

    return pl.pallas_call(
        body, name=name, grid=(n_seq, nq),
        in_specs=[pl.BlockSpec((tq, MEM_W), lambda b, i: (b * nq + i, 3)),
                  pl.BlockSpec((N_MEM, 2 * MEM_W), lambda b, i: (b, 0)),
                  pl.BlockSpec(memory_space=pl.ANY)],
        out_specs=pl.BlockSpec((tq, MEM_W), lambda b, i: (b * nq + i, 3)),
        out_shape=_sds((M, D_MODEL), bf16),
        input_output_aliases={2: 0},
        compiler_params=_params(("parallel", "parallel")),
    )(z, kvm, ycat)


def memattn_bwd(z, kvm, dyc, dz, *, name, n_seq):
    M = z.shape[0]
    tq = SEQ
    nq = SEQ // tq

    def body(q_ref, kv_ref, dy_ref, _, dq_ref, dkv_ref):
        first = _first_head()
        dy = dy_ref[...].astype(f32)
        dqs, dks, dvs = [], [], []
        for h, (lm, qm, kp, vp, e, l) in enumerate(_mem_heads(q_ref[...], kv_ref[...])):
            pr = h // 2
            p = e * (1.0 / l)
            dym = jnp.where(lm, dy[:, pr * PAIR_W:(pr + 1) * PAIR_W], 0.0).astype(bf16)
            dp = lax.dot_general(dym, vp, NT, preferred_element_type=f32)
            ds = (p * (dp - jnp.sum(dp * p, axis=-1, keepdims=True))).astype(bf16)
            dqs.append(jnp.dot(ds, kp, preferred_element_type=f32) * SCALE)
            dk = lax.dot_general(ds, qm, TN, preferred_element_type=f32)
            dv = lax.dot_general(p.astype(bf16), dym, TN, preferred_element_type=f32)
            if h % 2 == 0:
                dks.append(dk)
                dvs.append(dv)
            else:
                dks[pr] = dks[pr] + dk
                dvs[pr] = dvs[pr] + dv
        pairs = [jnp.where(first, dqs[2 * pr], dqs[2 * pr + 1]) for pr in range(N_MEM_HEADS // 2)]
        dq_ref[...] = jnp.concatenate(pairs, axis=1).astype(bf16)

        @pl.when(pl.program_id(1) == 0)
        def _():
            dkv_ref[...] = jnp.zeros_like(dkv_ref)

        dkv_ref[...] += jnp.concatenate(dks + dvs, axis=1)

    return pl.pallas_call(
        body, name=name, grid=(n_seq, nq),
        in_specs=[pl.BlockSpec((tq, MEM_W), lambda b, i: (b * nq + i, 3)),
                  pl.BlockSpec((N_MEM, 2 * MEM_W), lambda b, i: (b, 0)),
                  pl.BlockSpec((tq, MEM_W), lambda b, i: (b * nq + i, 3)),
                  pl.BlockSpec(memory_space=pl.ANY)],
        out_specs=[pl.BlockSpec((tq, MEM_W), lambda b, i: (b * nq + i, 3)),
                   pl.BlockSpec((N_MEM, 2 * MEM_W), lambda b, i: (b, 0))],
        out_shape=[_sds((M, D_MODEL), bf16), _sds((n_seq * N_MEM, 2 * MEM_W), f32)],
        input_output_aliases={3: 0},
        compiler_params=_params(("parallel", "arbitrary")),
    )(z, kvm, dyc, dz)


def rope_tables(pos, *, name):
    M = pos.shape[0]
    tm = min(1024, M)
    half = HEAD_DIM // 2
    inv = ROPE_THETA ** (-np.arange(half, dtype=np.float64) / half)
    inv128 = jnp.asarray(np.tile(inv, 4)[None, :], f32)
    sign128 = jnp.asarray(np.tile(np.concatenate([-np.ones(half), np.ones(half)]), 2)[None, :], f32)

    def body(p_ref, f_ref, s_ref, cos_ref, sin_ref):
        ang = p_ref[...] * f_ref[...]
        cos_ref[...] = jnp.cos(ang)
        sin_ref[...] = jnp.sin(ang) * s_ref[...]

    return pl.pallas_call(
        body, name=name, grid=(M // tm,),
        in_specs=[pl.BlockSpec((tm, 1), lambda i: (i, 0)),
                  pl.BlockSpec((1, 128), lambda i: (0, 0)),
                  pl.BlockSpec((1, 128), lambda i: (0, 0))],
        out_specs=[pl.BlockSpec((tm, 128), lambda i: (i, 0)),
                   pl.BlockSpec((tm, 128), lambda i: (i, 0))],
        out_shape=[_sds((M, 128), f32), _sds((M, 128), f32)],
        compiler_params=_params(("parallel",)),
    )(pos, inv128, sign128)


def _swap_halves(x):
    w = x.shape[1]
    first = (lax.broadcasted_iota(jnp.int32, (1, w), 1) % HEAD_DIM) < (HEAD_DIM // 2)
    return jnp.where(first, pltpu.roll(x, w - HEAD_DIM // 2, 1), pltpu.roll(x, HEAD_DIM // 2, 1))


def group_sum(groups, cos, sin, *, name, rotate, width, col_block=0, into=None):
    M = groups[0][0].shape[0]
    tm = min(512, M)
    counts = [len(g) for g in groups]
    flat = [a for g in groups for a in g]
    extra = [] if into is None else [into]

    def body(*refs):
        part_refs = refs[:len(flat)]
        c_ref, s_ref = refs[len(flat):len(flat) + 2]
        o_ref = refs[-1]
        cols, k = [], 0
        for n in counts:
            acc = part_refs[k][...]
            for r in part_refs[k + 1:k + n]:
                acc = acc + r[...]
            cols.append(acc)
            k += n
        d = jnp.concatenate(cols, axis=1)
        if rotate:
            c = jnp.tile(c_ref[...], (1, MAIN_W // 128))
            s = jnp.tile(s_ref[...], (1, MAIN_W // 128))
            d = d * c - _swap_halves(d) * s
        o_ref[...] = d.astype(bf16)

    part = pl.BlockSpec((tm, GROUP_W), lambda i: (i, 0))
    tab = pl.BlockSpec((tm, 128), lambda i: (i, 0))
    return pl.pallas_call(
        body, name=name, grid=(M // tm,),
        in_specs=[part] * len(flat) + [tab, tab] + [pl.BlockSpec(memory_space=pl.ANY)] * len(extra),
        out_specs=pl.BlockSpec((tm, MAIN_W), lambda i: (i, col_block)),
        out_shape=_sds((M, width), bf16),
        input_output_aliases={len(flat) + 2: 0} if extra else {},
        compiler_params=_params(("parallel",)),
    )(*flat, cos, sin, *extra)


PAIR_W = 2 * HEAD_DIM
MIN_BLOCKS = 16


FWD_TOGETHER = 4
BWD_TOGETHER = 4


def _dil_geometry(dil):
    nsub = max(dil, MIN_BLOCKS)
    tb = BAND * nsub
    return nsub, tb, SEQ // tb


REGROUP = 4


class _Regrouped:
    def __init__(self, ref):
        self.ref = ref
        self.shape = ref.shape

    def fill(self, src):
        q = self.shape[0] // REGROUP
        for r0 in range(REGROUP):
            self.ref[r0 * q:(r0 + 1) * q, :] = src[pl.ds(r0, q, stride=REGROUP), :]

    def drain(self, dst):
        q = self.shape[0] // REGROUP
        for r0 in range(REGROUP):
            dst[pl.ds(r0, q, stride=REGROUP), :] = self.ref[r0 * q:(r0 + 1) * q, :]

    def rows(self, sub, dil):
        nl, r = divmod(sub, dil)
        start = (r % REGROUP) * (self.shape[0] // REGROUP) + r // REGROUP + nl * BAND * (dil // REGROUP)
        return pl.ds(start, BAND, stride=dil // REGROUP)


def _regroups(dil):
    return dil % (4 * REGROUP) == 0


def _rows(ref, sub, dil):
    if isinstance(ref, _Regrouped):
        return ref.ref[ref.rows(sub, dil), :]
    if dil == 1:
        return ref[sub * BAND:(sub + 1) * BAND, :]
    nl, r = divmod(sub, dil)
    return ref[pl.ds(nl * BAND * dil + r, BAND, stride=dil), :]


def _store_rows(ref, sub, dil, val):
    if isinstance(ref, _Regrouped):
        ref.ref[ref.rows(sub, dil), :] = val
    elif dil == 1:
        ref[sub * BAND:(sub + 1) * BAND, :] = val
    else:
        nl, r = divmod(sub, dil)
        ref[pl.ds(nl * BAND * dil + r, BAND, stride=dil), :] = val


def _keys(prev_ref, own_ref, sub, dil):
    nsub = own_ref.shape[0] // BAND
    if sub >= dil:
        prev = _rows(own_ref, sub - dil, dil)
    elif prev_ref is None:
        return _rows(own_ref, sub, dil)
    else:
        prev = _rows(prev_ref, nsub - dil + sub, dil)
    return jnp.concatenate([prev, _rows(own_ref, sub, dil)], axis=0)


def _band_mask(nkeys, has_prev):
    i = lax.broadcasted_iota(jnp.int32, (BAND, nkeys), 0)
    j = lax.broadcasted_iota(jnp.int32, (BAND, nkeys), 1)
    if nkeys == BAND:
        return j <= i
    return (j >= i) & (j <= i + BAND) & (has_prev | (j >= BAND))


def _first_head():
    return lax.broadcasted_iota(jnp.int32, (1, PAIR_W), 1) < HEAD_DIM


def _col(x, hh):
    return x[:, hh * HEAD_DIM:hh * HEAD_DIM + 1]


def _pair_spec(tb, nblk, col0, which):
    def idx(b, p, i):
        if which < 0:
            i = jnp.maximum(i - 1, 0)
        elif which > 0:
            i = jnp.minimum(i + 1, nblk - 1)
        return (b * nblk + i, col0 + p)
    return pl.BlockSpec((tb, PAIR_W), idx)


def dil_fwd(q, k, kv, g, dil, *, name, n_seq):
    M = q.shape[0]
    nsub, tb, nblk = _dil_geometry(dil)
    with_prev = nblk > 1

    regroup = _regroups(dil)
    assert not (regroup and with_prev)

    def body(*refs):
        if with_prev:
            q_ref, ko_ref, vo_ref, kp_ref, vp_ref, o_ref, l_ref = refs
        else:
            (q_ref, ko_ref, vo_ref, o_ref, l_ref), kp_ref, vp_ref = refs[:5], None, None
        outs_to = ()
        if regroup:
            copies = [_Regrouped(s) for s in refs[5:]]
            for c, src in zip(copies, (q_ref, ko_ref, vo_ref)):
                c.fill(src)
            outs_to = ((copies[3], o_ref), (copies[4], l_ref))
            q_ref, ko_ref, vo_ref, o_ref, l_ref = copies
        first = _first_head()
        blk = pl.program_id(2)
        for sub0 in range(0, nsub, FWD_TOGETHER):
            subs = range(sub0, sub0 + FWD_TOGETHER)
            scores, values = [], []
            for sub in subs:
                qs = _rows(q_ref, sub, dil) * SCALE
                kc = _keys(kp_ref, ko_ref, sub, dil).astype(bf16)
                values.append(_keys(vp_ref, vo_ref, sub, dil).astype(bf16))
                has_prev = True if sub >= dil else blk > 0
                mask = _band_mask(kc.shape[0], has_prev)
                for hh in range(2):
                    qm = jnp.where(first if hh == 0 else ~first, qs, 0.0).astype(bf16)
                    scores.append(jnp.where(mask, lax.dot_general(qm, kc, NT, preferred_element_type=f32), NEG))
            soft = []
            for s in scores:
                m = jnp.max(s, axis=-1, keepdims=True)
                e = jnp.exp(s - m)
                l = jnp.sum(e, axis=-1, keepdims=True)
                soft.append((e.astype(bf16), 1.0 / l, jnp.broadcast_to(m + jnp.log(l), (BAND, PAIR_W))))
            outs = [jnp.dot(e, values[n // 2], preferred_element_type=f32) * inv for n, (e, inv, _) in enumerate(soft)]
            for n, sub in enumerate(subs):
                _store_rows(o_ref, sub, dil, jnp.where(first, outs[2 * n], outs[2 * n + 1]))
                _store_rows(l_ref, sub, dil, jnp.where(first, soft[2 * n][2], soft[2 * n + 1][2]))
        for c, dst in outs_to:
            c.drain(dst)

    ins = [(q, 2 * g, 0), (k, 2 * g, 0), (kv, 6 + 2 * g, 0)]
    if with_prev:
        ins += [(k, 2 * g, -1), (kv, 6 + 2 * g, -1)]
    out = _pair_spec(tb, nblk, 0, 0)
    return pl.pallas_call(
        body, name=name, grid=(n_seq, 2, nblk),
        in_specs=[_pair_spec(tb, nblk, c, w) for _, c, w in ins],
        out_specs=[out, out],
        out_shape=[_sds((M, GROUP_W), f32)] * 2,
        scratch_shapes=[pltpu.VMEM((tb, PAIR_W), f32)] * (5 if regroup else 0),
        compiler_params=_params(("parallel", "parallel", "arbitrary")),
    )(*[a for a, _, _ in ins])


def combine_fwd(os_, lses, *, name):
    M = os_[0].shape[0]
    tm = min(512, M)

    def body(o0, o1, o2, l0, l1, l2, y_ref):
        ls = [l0[...], l1[...], l2[...]]
        m = jnp.maximum(jnp.maximum(ls[0], ls[1]), ls[2])
        es = [jnp.exp(l - m) for l in ls]
        inv = 1.0 / (es[0] + es[1] + es[2])
        y_ref[...] = jnp.concatenate([o[...] * e * inv for o, e in zip((o0, o1, o2), es)], axis=1).astype(bf16)

    part = pl.BlockSpec((tm, GROUP_W), lambda i: (i, 0))
    return pl.pallas_call(
        body, name=name, grid=(M // tm,),
        in_specs=[part] * 6,
        out_specs=pl.BlockSpec((tm, MAIN_W), lambda i: (i, 0)),
        out_shape=_sds((M, D_MODEL), bf16),
        compiler_params=_params(("parallel",)),
    )(*os_, *lses)


def combine_bwd(dyc, os_, lses, *, name):
    M = os_[0].shape[0]
    tm = min(512, M)

    def body(dy_ref, o0, o1, o2, l0, l1, l2, d0, d1, d2, c0, c1, c2):
        r = lax.broadcasted_iota(jnp.int32, (GROUP_W, GROUP_W), 0) // HEAD_DIM
        c = lax.broadcasted_iota(jnp.int32, (GROUP_W, GROUP_W), 1) // HEAD_DIM
        ones = (r == c).astype(bf16)
        dy = dy_ref[...].astype(f32)
        ls = [l0[...], l1[...], l2[...]]
        m = jnp.maximum(jnp.maximum(ls[0], ls[1]), ls[2])
        es = [jnp.exp(l - m) for l in ls]
        inv = 1.0 / (es[0] + es[1] + es[2])
        total = 0.0
        alphas = []
        for g, (o, e, d_ref) in enumerate(zip((o0, o1, o2), es, (d0, d1, d2))):
            a = e * inv
            dyg = dy[:, g * GROUP_W:(g + 1) * GROUP_W]
            d_ref[...] = dyg * a
            prod = dyg * o[...]
            hi = prod.astype(bf16)
            lo = (prod - hi.astype(f32)).astype(bf16)
            dsum = jnp.dot(hi, ones, preferred_element_type=f32) + jnp.dot(lo, ones, preferred_element_type=f32)
            total = total + a * dsum
            alphas.append(a)
        for a, c_ref in zip(alphas, (c0, c1, c2)):
            c_ref[...] = -a * total

    part = pl.BlockSpec((tm, GROUP_W), lambda i: (i, 0))
    outs = pl.pallas_call(
        body, name=name, grid=(M // tm,),
        in_specs=[pl.BlockSpec((tm, MAIN_W), lambda i: (i, 0))] + [part] * 6,
        out_specs=[part] * 6,
        out_shape=[_sds((M, GROUP_W), f32)] * 6,
        compiler_params=_params(("parallel",)),
    )(dyc, *os_, *lses)
    return outs[:3], outs[3:]


def dil_bwd(q, k, kv, do, cc, lse, cos, sin, g, dil, *, name, n_seq, into=None):
    M = q.shape[0]
    nsub = SEQ // BAND
    per_res = nsub // dil
    extra = [] if into is None else [into]

    regroup = _regroups(dil)

    def body(q_ref, k_ref, v_ref, do_ref, c_ref, l_ref, cos_ref, sin_ref, *refs):
        dz_ref, dk_ref, dv_ref, dq_ref, *scratch = refs[len(extra):]
        dq_rows = dq_ref
        outs_to = ()
        if regroup:
            copies = [_Regrouped(s) for s in scratch]
            for c, src in zip(copies, (q_ref, k_ref, v_ref, do_ref, c_ref, l_ref)):
                c.fill(src)
            outs_to = tuple(zip(copies[6:], (dq_rows, dk_ref, dv_ref)))
            q_ref, k_ref, v_ref, do_ref, c_ref, l_ref, dq_ref, dk_ref, dv_ref = copies
        first = _first_head()
        order = [nl * dil + r for r in range(dil) for nl in range(per_res)]
        carry = None
        for at in range(0, nsub, BWD_TOGETHER):
            subs = order[at:at + BWD_TOGETHER]
            loaded, products = [], []
            for sub in subs:
                qs = _rows(q_ref, sub, dil) * SCALE
                dos = _rows(do_ref, sub, dil)
                kc = _keys(None, k_ref, sub, dil).astype(bf16)
                vc = _keys(None, v_ref, sub, dil).astype(bf16)
                mask = _band_mask(kc.shape[0], True)
                for hh in range(2):
                    lm = first if hh == 0 else ~first
                    qm = jnp.where(lm, qs, 0.0).astype(bf16)
                    dom = jnp.where(lm, dos, 0.0).astype(bf16)
                    loaded.append((qm, dom, kc))
                    products.append((jnp.where(mask, lax.dot_general(qm, kc, NT, preferred_element_type=f32), NEG),
                                     lax.dot_general(dom, vc, NT, preferred_element_type=f32)))
            weights = []
            for n, (s, dp) in enumerate(products):
                sub, hh = subs[n // 2], n % 2
                p = jnp.exp(s - _col(_rows(l_ref, sub, dil), hh))
                weights.append((p.astype(bf16), (p * (dp + _col(_rows(c_ref, sub, dil), hh))).astype(bf16)))
            results = []
            for (pb, ds), (qm, dom, kc) in zip(weights, loaded):
                results.append((jnp.dot(ds, kc, preferred_element_type=f32) * SCALE,
                                lax.dot_general(ds, qm, TN, preferred_element_type=f32),
                                lax.dot_general(pb, dom, TN, preferred_element_type=f32)))
            for n, sub in enumerate(subs):
                (dq0, dk0, dv0), (dq1, dk1, dv1) = results[2 * n], results[2 * n + 1]
                _store_rows(dq_ref, sub, dil, jnp.where(first, dq0, dq1))
                dkc, dvc = dk0 + dk1, dv0 + dv1
                if sub >= dil:
                    _store_rows(dk_ref, sub - dil, dil, carry[0] + dkc[:BAND])
                    _store_rows(dv_ref, sub - dil, dil, carry[1] + dvc[:BAND])
                    carry = (dkc[BAND:], dvc[BAND:])
                else:
                    carry = (dkc, dvc)
                if sub + dil >= nsub:
                    _store_rows(dk_ref, sub, dil, carry[0])
                    _store_rows(dv_ref, sub, dil, carry[1])
        for c, dst in outs_to:
            c.drain(dst)
        d = dq_rows[...]
        dz_ref[...] = (d * cos_ref[...] - _swap_halves(d) * sin_ref[...]).astype(bf16)

    def spec(col0):
        return pl.BlockSpec((SEQ, PAIR_W), lambda b, p: (b, col0 + p))

    tab = pl.BlockSpec((SEQ, PAIR_W), lambda b, p: (b, 0))
    out = spec(0)
    return pl.pallas_call(
        body, name=name, grid=(n_seq, 2),
        in_specs=[spec(2 * g), spec(2 * g), spec(6 + 2 * g), spec(0), spec(0), spec(0), tab, tab]
        + [pl.BlockSpec(memory_space=pl.ANY)] * len(extra),
        out_specs=[spec(2 * g), out, out],
        out_shape=[_sds((M, D_MODEL), bf16)] + [_sds((M, GROUP_W), f32)] * 2,
        input_output_aliases={8: 0} if extra else {},
        scratch_shapes=[pltpu.VMEM((SEQ, PAIR_W), f32)] * (10 if regroup else 1),
        compiler_params=_params(("parallel", "parallel")),
    )(q, k, kv, do, cc, lse, cos, sin, *extra)


def _blockdiag(wp):
    out = jnp.zeros((MAIN_W, MAIN_W), wp.dtype)
    for gi in range(len(POOL_WINDOWS)):
        sl = slice(gi * POOL_GROUP, (gi + 1) * POOL_GROUP)
        out = out.at[sl, sl].set(wp[gi])
    return out


def _unblockdiag(w):
    return jnp.stack([w[gi * POOL_GROUP:(gi + 1) * POOL_GROUP, gi * POOL_GROUP:(gi + 1) * POOL_GROUP]
                      for gi in range(len(POOL_WINDOWS))])


def local_step(x, mem, positions, target, P, layer_weights, kv_weight, emit_grads):
    n_seq = x.shape[0]
    M = n_seq * SEQ
    xs = x.reshape(M, D_MODEL)
    mems = mem.reshape(n_seq * N_MEM, D_MODEL)
    pos = positions.reshape(M, 1).astype(f32)
    cos, sin = rope_tables(pos, name="rope_tables")
    gains = P["norm_gains"]

    def gain(l, k):
        return gains[l, k].reshape(1, D_MODEL)

    saved = []
    kvs = None
    for l in range(DEPTH):
        W, started = layer_weights(l, "mix", xs)
        sv = {"x": xs, "W": W}
        z, h1, *qrot = rms_matmul(xs, gain(l, 0), W["w_in"], name=f"l{l}_in", out_dtype=bf16, after=started,
                                  rope=None if l < N_A_LAYERS else (cos, sin))
        kvm, mn = rms_matmul(mems, P["mem_norm"][l].reshape(1, D_MODEL), W["w_mem_kv"],
                             name=f"l{l}_memkv", out_dtype=bf16)
        sv.update(z=z, h1=h1, kvm=kvm, mn=mn)
        if l < N_A_LAYERS:
            wbd = _blockdiag(P["w_pool"][l].astype(bf16))
            psc = P["pool_scale"][l].reshape(1, MAIN_W)
            p, y_main = pool_fwd(z, wbd, psc, name=f"l{l}_pool")
            sv.update(p=p, wbd=wbd, psc=psc)
        else:
            (qrot,) = qrot
            os_, lses = [], []
            for g, (_, dil) in enumerate(DIL_PATTERNS):
                o, lse = dil_fwd(qrot, kvs["krot"], kvs["kv"], g, dil, name=f"l{l}_dil{g}", n_seq=n_seq)
                os_.append(o)
                lses.append(lse)
            y_main = combine_fwd(os_, lses, name=f"l{l}_comb")
            sv.update(qrot=qrot, os=os_, lses=lses)
        ycat = memattn_fwd(z, kvm, y_main, name=f"l{l}_memattn", n_seq=n_seq)
        y, x1 = matmul_rms_res(ycat, W["w_out"], gain(l, 1), xs, name=f"l{l}_out")
        W.update(layer_weights(l, "gu", x1)[0])
        fg, fu, a, h2 = rms_gate_up(x1, gain(l, 2), W["w_gate_up"], name=f"l{l}_gu")
        W_down, passing = layer_weights(l, "down", a)
        W.update(W_down)
        y2, x2, *sq = matmul_rms_res(a, W["w_down"], gain(l, 3), x1, name=f"l{l}_down", after=passing,
                                     target=target.reshape(M, D_MODEL) if l == DEPTH - 1 else None)
        sv.update(ycat=ycat, y=y, x1=x1, fg=fg, fu=fu, h2=h2, a=a, y2=y2)
        saved.append(sv)
        xs = x2
        if l == N_A_LAYERS - 1:
            w_kv = kv_weight(xs)
            kv, hkv, krot = rms_matmul(xs, P["kv_norm"].reshape(1, D_MODEL), w_kv, name="kv_proj", out_dtype=f32,
                                       transposed=True, rope=(cos, sin))
            kvs = {"kv": kv, "hkv": hkv, "krot": krot, "x": xs, "w_kv": w_kv}

    dx, (sq,) = xs, sq

    G = {"mem_norm": [None] * DEPTH, "norm_gains": [[None] * 4 for _ in range(DEPTH)],
         "pool_scale": [None] * N_A_LAYERS}
    dk_parts = [[] for _ in range(N_GROUPS)]
    dv_parts = [[] for _ in range(N_GROUPS)]
    emitted = None

    for l in reversed(range(DEPTH)):
        sv = saved[l]
        W = sv["W"]
        gw = {}
        dy2, dgu, G["norm_gains"][l][3] = down_bwd(sv["y2"], gain(l, 3), dx, W["w_down"], sv["fg"], sv["fu"],
                                                   name=f"l{l}_b_dgu", after=emitted)
        gw["w_down"] = matmul(sv["a"], dy2, TN, name=f"l{l}_b_wd", out_dtype=bf16)
        gw["w_gate_up"] = matmul(dgu, sv["h2"], TN, name=f"l{l}_b_wgu", out_dtype=bf16)
        emitted = emit_grads(l, "ffn", gw)
        dx1, G["norm_gains"][l][2] = matmul_rms_bwd(dgu, W["w_gate_up"], NN, sv["x1"], gain(l, 2), dx,
                                                    name=f"l{l}_b_dh2", after=emitted)
        gw = {}
        dy, dycat, G["norm_gains"][l][1] = rms_bwd_matmul(sv["y"], gain(l, 1), dx1, W["w_out"], NT,
                                                          name=f"l{l}_b_dycat", after=emitted)
        gw["w_out"] = matmul(sv["ycat"], dy, TN, name=f"l{l}_b_wout", out_dtype=bf16)
        if l < N_A_LAYERS:
            dz, dwbd, dps = pool_bwd(dycat, sv["p"], sv["wbd"], sv["psc"], name=f"l{l}_b_pool")
            gw["w_pool"] = _unblockdiag(dwbd).reshape(MAIN_W, POOL_GROUP).astype(bf16)
            G["pool_scale"][l] = dps.reshape(MAIN_W)
        else:
            dos, ccs = combine_bwd(dycat, sv["os"], sv["lses"], name=f"l{l}_b_comb")
            dz = None
            for g, (_, dil) in enumerate(DIL_PATTERNS):
                args = (sv["qrot"], kvs["krot"], kvs["kv"], dos[g], ccs[g], sv["lses"][g], cos, sin, g, dil)
                dz, dk, dv = dil_bwd(*args, name=f"l{l}_b_dil{g}", n_seq=n_seq, into=dz)
                dk_parts[g].append(dk)
                dv_parts[g].append(dv)
        dz, dkvm = memattn_bwd(sv["z"], sv["kvm"], dycat, dz, name=f"l{l}_b_memattn", n_seq=n_seq)
        gw["w_mem_kv"] = matmul(sv["mn"], dkvm, TN, name=f"l{l}_b_wmkv", out_dtype=bf16)
        _, G["mem_norm"][l] = matmul_rms_bwd(dkvm, W["w_mem_kv"], NT, mems, P["mem_norm"][l].reshape(1, D_MODEL),
                                             mems, name=f"l{l}_b_dmn")
        gw["w_in"] = matmul(sv["h1"], dz, TN, name=f"l{l}_b_win", out_dtype=bf16)
        if l != N_A_LAYERS:
            emitted = emit_grads(l, "mix", gw)
        dx, G["norm_gains"][l][0] = matmul_rms_bwd(dz, W["w_in"], NT, sv["x"], gain(l, 0), dx1, name=f"l{l}_b_dh1",
                                                   after=emitted)
        if l == N_A_LAYERS:
            dkv = group_sum(dk_parts, cos, sin, name="b_ropek", rotate=True, width=2 * MAIN_W)
            dkv = group_sum(dv_parts, cos, sin, name="b_sumv", rotate=False, width=2 * MAIN_W, col_block=1, into=dkv)
            gw["w_kv"] = matmul(dkv, kvs["hkv"], TN, name="b_wkv", out_dtype=bf16)
            dx, gkn = matmul_rms_bwd(dkv, kvs["w_kv"], NN, kvs["x"], P["kv_norm"].reshape(1, D_MODEL), dx,
                                     name="b_dhkv")
            G["kv_norm"] = gkn.reshape(D_MODEL)
            emitted = emit_grads(l, "mix", gw)

    small = {"pool_scale": jnp.stack(G["pool_scale"]),
             "mem_norm": jnp.concatenate(G["mem_norm"], axis=0),
             "norm_gains": jnp.stack([jnp.concatenate(r, axis=0) for r in G["norm_gains"]]),
             "kv_norm": G["kv_norm"]}
    return sq[0, 0], dx.reshape(n_seq, SEQ, D_MODEL), small, emitted


def to_bf16_layers(stacks, after, *, name):
    L, n = stacks[0].shape[0], len(stacks)

    def body(*refs):
        ins, outs = refs[:n], refs[n + 1:]
        for j in range(L):
            @pl.when(pl.program_id(0) == j)
            def _():
                for k in range(n):
                    outs[j * n + k][...] = ins[k][...].astype(bf16)

    outs = pl.pallas_call(
        body, name=name, grid=(L,),
        in_specs=[pl.BlockSpec((None,) + s.shape[1:], lambda l: (l, 0, 0)) for s in stacks]
        + [pl.BlockSpec(memory_space=pl.ANY)],
        out_specs=[pl.BlockSpec(s.shape[1:], lambda l: (0, 0)) for _ in range(L) for s in stacks],
        out_shape=[_sds(s.shape[1:], bf16) for _ in range(L) for s in stacks],
        compiler_params=_params(("arbitrary",)),
    )(*stacks, after)
    return [outs[j * n:(j + 1) * n] for j in range(L)]


def _peer(k):
    x, y, c = lax.axis_index("x"), lax.axis_index("y"), lax.axis_index("c")
    px = 1 - x if k & 4 else x
    py = 1 - y if k & 2 else y
    pc = 1 - c if k & 1 else c
    return (px, py, pc), 4 * px + 2 * py + pc


def _my_index():
    return 4 * lax.axis_index("x") + 2 * lax.axis_index("y") + lax.axis_index("c")


def _src_for(kinds, in_refs, i, idx):
    return in_refs[i] if kinds[i] == "gather" else in_refs[i].at[idx]


def _local_copies(kinds, in_refs, out_refs, local_sems):
    me = _my_index()
    return [pltpu.make_async_copy(_src_for(kinds, in_refs, i, me), out_refs[i].at[me], local_sems.at[i])
            for i in range(len(kinds))]


def _remote_copies(kinds, in_refs, out_refs, send_sems, recv_sems, *, arriving):
    me = _my_index()
    copies = []
    for k in range(1, N_DEV):
        dev, idx = _peer(k)
        for i in range(len(kinds)):
            j = i * (N_DEV - 1) + k - 1
            copies.append(pltpu.make_async_remote_copy(
                src_ref=_src_for(kinds, in_refs, i, idx), dst_ref=out_refs[i].at[idx if arriving else me],
                send_sem=send_sems.at[j], recv_sem=recv_sems.at[j], device_id=dev, device_id_type=MESH))
    return copies


def _out_shape(a, kind):
    return ((N_DEV,) + a.shape) if kind == "gather" else a.shape


def exchange(items, *, name, after=()):
    n = len(items)
    kinds = [k for _, k in items]
    after = list(after)

    def body(*refs):
        in_refs, out_refs = refs[:n], refs[n + len(after):2 * n + len(after)]
        send_sems, recv_sems, local_sems = refs[-3:]
        local = _local_copies(kinds, in_refs, out_refs, local_sems)
        sends = _remote_copies(kinds, in_refs, out_refs, send_sems, recv_sems, arriving=False)
        for cp in local + sends:
            cp.start()
        for cp in _remote_copies(kinds, in_refs, out_refs, send_sems, recv_sems, arriving=True):
            cp.wait_recv()
        for cp in sends:
            cp.wait_send()
        for cp in local:
            cp.wait()

    any_spec = pl.BlockSpec(memory_space=pl.ANY)
    return pl.pallas_call(
        body, name=name,
        in_specs=[any_spec] * (n + len(after)), out_specs=[any_spec] * n,
        out_shape=[_sds(_out_shape(a, k), a.dtype) for a, k in items],
        scratch_shapes=[pltpu.SemaphoreType.DMA((n * (N_DEV - 1),)), pltpu.SemaphoreType.DMA((n * (N_DEV - 1),)),
                        pltpu.SemaphoreType.DMA((n,))],
    )(*[a for a, _ in items], *after)


_HBM = pl.BlockSpec(memory_space=pltpu.HBM)
_SEM = pl.BlockSpec(memory_space=pltpu.SEMAPHORE)
_EFFECT = pltpu.SideEffectType.DATAFLOW_SIDE_EFFECTING


def exchange_start(items, after, *, name):
    n = len(items)
    kinds = [k for _, k in items]

    def body(*refs):
        in_refs, land_refs = refs[:n], refs[n:2 * n]
        send_sems, recv_sems, local_sems = refs[2 * n + 1:2 * n + 4]
        token = refs[-1]
        for cp in (_local_copies(kinds, in_refs, land_refs, local_sems)
                   + _remote_copies(kinds, in_refs, land_refs, send_sems, recv_sems, arriving=False)):
            cp.start()
        token[...] = jnp.zeros_like(token)

    srcs = [pltpu.with_memory_space_constraint(a, pltpu.HBM) for a, _ in items]
    lands = [pltpu.with_memory_space_constraint(lax.empty(_out_shape(a, k), a.dtype), pltpu.HBM) for a, k in items]
    outs = pl.pallas_call(
        body, name=name,
        out_shape=(pltpu.SemaphoreType.DMA((n * (N_DEV - 1),)), pltpu.SemaphoreType.DMA((n * (N_DEV - 1),)),
                   pltpu.SemaphoreType.DMA((n,)),
                   *[pltpu.HBM(a.shape, a.dtype) for a in srcs], *[pltpu.HBM(a.shape, a.dtype) for a in lands],
                   _sds((8, 128), f32)),
        in_specs=[_HBM] * (2 * n) + [pl.BlockSpec(memory_space=pl.ANY)],
        out_specs=(_SEM, _SEM, _SEM, *[_HBM] * (2 * n), pl.BlockSpec(memory_space=pltpu.VMEM)),
        input_output_aliases={i: 3 + i for i in range(2 * n)},
        compiler_params=pltpu.CompilerParams(has_side_effects=_EFFECT),
    )(*srcs, *lands, after)
    return {"kinds": kinds, "sems": outs[:3], "srcs": outs[3:3 + n], "lands": outs[3 + n:3 + 2 * n], "token": outs[-1]}


def exchange_wait(handle, after, *, name):
    kinds = handle["kinds"]
    n = len(kinds)

    def body(*refs):
        in_refs, land_refs = refs[:n], refs[n:2 * n]
        send_sems, recv_sems, local_sems = refs[2 * n:2 * n + 3]
        for cp in _remote_copies(kinds, in_refs, land_refs, send_sems, recv_sems, arriving=True):
            cp.wait_recv()
        for cp in _remote_copies(kinds, in_refs, land_refs, send_sems, recv_sems, arriving=False):
            cp.wait_send()
        for cp in _local_copies(kinds, in_refs, land_refs, local_sems):
            cp.wait()

    srcs, lands = list(handle["srcs"]), list(handle["lands"])
    after = list(after) if isinstance(after, (list, tuple)) else [after]
    outs = pl.pallas_call(
        body, name=name,
        out_shape=tuple(pltpu.HBM(a.shape, a.dtype) for a in srcs + lands),
        in_specs=[_HBM] * (2 * n) + [_SEM] * 3 + [pl.BlockSpec(memory_space=pl.ANY)] * len(after),
        out_specs=tuple([_HBM] * (2 * n)),
        input_output_aliases={i: i for i in range(2 * n)},
        compiler_params=pltpu.CompilerParams(has_side_effects=_EFFECT),
    )(*srcs, *lands, *handle["sems"], *after)
    return list(outs[n:])


CHIP_MASKS = (2, 4, 6)


def _g2_first(in_refs, land_refs, send_sems, recv_sems, *, masks, arriving):
    me = _my_index()
    copies = []
    for i in range(len(land_refs)):
        for j, k in enumerate(masks):
            dev, idx = _peer(k)
            dst = land_refs[i].at[idx if arriving else me]
            copies.append(pltpu.make_async_remote_copy(
                src_ref=dst if in_refs is None else in_refs[i], dst_ref=dst,
                send_sem=send_sems.at[i * len(masks) + j], recv_sem=recv_sems.at[i * len(masks) + j],
                device_id=dev, device_id_type=MESH))
    return copies


def _g2_forward(land_refs, fwd_send, fwd_recv, *, arriving):
    sibling, _ = _peer(1)
    copies = []
    for i in range(len(land_refs)):
        for j, k in enumerate(CHIP_MASKS):
            _, idx = _peer(k | 1 if arriving else k)
            copies.append(pltpu.make_async_remote_copy(
                src_ref=land_refs[i].at[idx], dst_ref=land_refs[i].at[idx],
                send_sem=fwd_send.at[i * 3 + j], recv_sem=fwd_recv.at[i * 3 + j], device_id=sibling,
                device_id_type=MESH))
    return copies


def gather2_start(arrays, after, *, name):
    n = len(arrays)

    def body(*refs):
        in_refs, land_refs = refs[:n], refs[n:2 * n]
        ici_send, ici_recv, d2d_send, d2d_recv, local_sems = refs[2 * n + 1:2 * n + 6]
        token = refs[-1]
        ici = _g2_first(in_refs, land_refs, ici_send, ici_recv, masks=CHIP_MASKS, arriving=False)
        d2d = _g2_first(in_refs, land_refs, d2d_send, d2d_recv, masks=(1,), arriving=False)
        for cp in _local_copies(["gather"] * n, in_refs, land_refs, local_sems) + ici + d2d:
            cp.start()
        token[...] = jnp.zeros_like(token)

    srcs = [pltpu.with_memory_space_constraint(a, pltpu.HBM) for a in arrays]
    lands = [pltpu.with_memory_space_constraint(lax.empty((N_DEV,) + a.shape, a.dtype), pltpu.HBM) for a in arrays]
    sem = pltpu.SemaphoreType.DMA
    outs = pl.pallas_call(
        body, name=name,
        out_shape=(sem((3 * n,)), sem((3 * n,)), sem((n,)), sem((n,)), sem((n,)),
                   *[pltpu.HBM(a.shape, a.dtype) for a in srcs], *[pltpu.HBM(a.shape, a.dtype) for a in lands],
                   _sds((8, 128), f32)),
        in_specs=[_HBM] * (2 * n) + [pl.BlockSpec(memory_space=pl.ANY)],
        out_specs=(*[_SEM] * 5, *[_HBM] * (2 * n), pl.BlockSpec(memory_space=pltpu.VMEM)),
        input_output_aliases={i: 5 + i for i in range(2 * n)},
        compiler_params=pltpu.CompilerParams(has_side_effects=_EFFECT),
    )(*srcs, *lands, after)
    return {"n": n, "sems": outs[:5], "srcs": outs[5:5 + n], "lands": outs[5 + n:5 + 2 * n], "token": outs[-1]}


def gather2_forward(handle, after, *, name):
    n = handle["n"]

    def body(*refs):
        land_refs = refs[:n]
        ici_recv = refs[n]
        fwd_send, fwd_recv = refs[n + 2:n + 4]
        for cp in _g2_first(None, land_refs, fwd_send, ici_recv, masks=CHIP_MASKS, arriving=True):
            cp.wait_recv()
        for cp in _g2_forward(land_refs, fwd_send, fwd_recv, arriving=False):
            cp.start()
        token = refs[-1]
        token[...] = jnp.zeros_like(token)

    lands = list(handle["lands"])
    sem = pltpu.SemaphoreType.DMA
    outs = pl.pallas_call(
        body, name=name,
        out_shape=(sem((3 * n,)), sem((3 * n,)), *[pltpu.HBM(a.shape, a.dtype) for a in lands], _sds((8, 128), f32)),
        in_specs=[_HBM] * n + [_SEM, pl.BlockSpec(memory_space=pl.ANY)],
        out_specs=(_SEM, _SEM, *[_HBM] * n, pl.BlockSpec(memory_space=pltpu.VMEM)),
        input_output_aliases={i: 2 + i for i in range(n)},
        compiler_params=pltpu.CompilerParams(has_side_effects=_EFFECT),
    )(*lands, handle["sems"][1], after)
    return dict(handle, fwd=outs[:2], lands=outs[2:2 + n], token=outs[-1])


def gather2_wait(handle, after, *, name):
    n = handle["n"]

    def body(*refs):
        in_refs, land_refs = refs[:n], refs[n:2 * n]
        ici_send, d2d_send, d2d_recv, local_sems, fwd_send, fwd_recv = refs[2 * n:2 * n + 6]
        for cp in _g2_first(in_refs, land_refs, d2d_send, d2d_recv, masks=(1,), arriving=True):
            cp.wait_recv()
        for cp in _g2_forward(land_refs, fwd_send, fwd_recv, arriving=True):
            cp.wait_recv()
        for cp in (_g2_first(in_refs, land_refs, ici_send, fwd_recv, masks=CHIP_MASKS, arriving=False)
                   + _g2_first(in_refs, land_refs, d2d_send, d2d_recv, masks=(1,), arriving=False)
                   + _g2_forward(land_refs, fwd_send, fwd_recv, arriving=False)):
            cp.wait_send()
        for cp in _local_copies(["gather"] * n, in_refs, land_refs, local_sems):
            cp.wait()

    srcs, lands = list(handle["srcs"]), list(handle["lands"])
    s = handle["sems"]
    outs = pl.pallas_call(
        body, name=name,
        out_shape=tuple(pltpu.HBM(a.shape, a.dtype) for a in srcs + lands),
        in_specs=[_HBM] * (2 * n) + [_SEM] * 6 + [pl.BlockSpec(memory_space=pl.ANY)],
        out_specs=tuple([_HBM] * (2 * n)),
        input_output_aliases={i: i for i in range(2 * n)},
        compiler_params=pltpu.CompilerParams(has_side_effects=_EFFECT),
    )(*srcs, *lands, s[0], s[2], s[3], s[4], *handle["fwd"], after)
    return list(outs[n:])


def adamw(entries, *, name):
    c1 = 1.0 - ADAM_B1 ** ADAM_STEP
    c2 = 1.0 - ADAM_B2 ** ADAM_STEP
    tiles = [_tile(w.shape[-2], (64, 32, 16, 8)) for _, w, _, _, _, _ in entries]
    steps = [w.shape[-2] // tr for (_, w, _, _, _, _), tr in zip(entries, tiles)]
    n = len(entries)

    def body(*refs):
        i = pl.program_id(0)
        for e in range(n):
            s_ref, w_ref, m_ref, v_ref = refs[4 * e:4 * e + 4]
            g_ref, d_ref, m2_ref, v2_ref = refs[len(refs) - 4 * n + 4 * e:len(refs) - 4 * n + 4 * e + 4]

            @pl.when(i < steps[e])
            def _():
                g = s_ref[0].astype(f32)
                for d in range(1, N_DEV):
                    g = g + s_ref[d].astype(f32)
                m2 = ADAM_B1 * m_ref[...] + (1.0 - ADAM_B1) * g
                v2 = ADAM_B2 * v_ref[...] + (1.0 - ADAM_B2) * (g * g)
                g_ref[...] = g
                m2_ref[...] = m2
                v2_ref[...] = v2
                d_ref[...] = -ADAM_LR * ((m2 / c1) / (jnp.sqrt(v2 / c2) + ADAM_EPS) + ADAM_WD * w_ref[...])

    in_specs, out_specs, out_shape, args, extras, aliases = [], [], [], [], [], {}
    for e, ((slots, w, m, v, layer, into), tr, ns) in enumerate(zip(entries, tiles, steps)):
        C = w.shape[-1]
        row = lambda i, ns=ns: jnp.minimum(i, ns - 1)
        if layer is None:
            blk = pl.BlockSpec((tr, C), lambda i, row=row: (row(i), 0))
        else:
            blk = pl.BlockSpec((None, tr, C), lambda i, row=row, layer=layer: (layer, row(i), 0))
        in_specs += [pl.BlockSpec((N_DEV, tr, C), lambda i, row=row: (0, row(i), 0)), blk, blk, blk]
        args += [slots, w, m, v]
        out_specs += [blk] * 4
        out_shape += [_sds(w.shape, f32)] * 4
        if into is not None:
            for t, a in enumerate(into):
                aliases[4 * n + len(extras)] = 4 * e + t
                extras.append(a)
    outs = pl.pallas_call(
        body, name=name, grid=(max(steps),),
        in_specs=in_specs + [pl.BlockSpec(memory_space=pl.ANY)] * len(extras),
        out_specs=out_specs, out_shape=out_shape, input_output_aliases=aliases,
        compiler_params=_params(("arbitrary",)),
    )(*args, *extras)
    return [outs[4 * e:4 * e + 4] for e in range(n)]


WEIGHTS = ("norm_gains", "mem_norm", "w_in", "w_mem_kv", "w_out", "w_pool", "pool_scale", "kv_norm", "w_kv",
           "w_gate_up", "w_down")
LAYER_MATS = ("w_in", "w_mem_kv", "w_out", "w_gate_up", "w_down")
POOL_SHARD = MAIN_W // N_DEV
KV_SHARD = 2 * MAIN_W // N_DEV
LOOKAHEAD = 2
TWO_LEVEL_LAYERS = (0, 1, 2)


def _pack_small(gains, pscale):
    lead = gains.shape[:-3]
    g = gains.reshape(lead + (16, 128))
    p = jnp.zeros(lead + (8, 128), f32).at[..., :2, :POOL_SHARD].set(pscale)
    return jnp.concatenate([g, p], axis=-2)


def _unpack_small(a):
    return a[:16].reshape(4, 4, 128), a[16:18, :POOL_SHARD]


def _pack_repl(mem_norm, kv_norm):
    return jnp.concatenate([mem_norm, kv_norm.reshape(1, D_MODEL), jnp.zeros((3, D_MODEL), f32)], axis=0)


def _unpack_repl(a):
    return a[:4], a[4]


def kernel(x, mem, positions, norm_gains, mem_norm, w_in, w_mem_kv, w_out, w_pool, pool_scale, kv_norm, w_kv, w_gate_up, w_down, loss_target, m_norm_gains, m_mem_norm, m_w_in, m_w_mem_kv, m_w_out, m_w_pool, m_pool_scale, m_kv_norm, m_w_kv, m_w_gate_up, m_w_down, v_norm_gains, v_mem_norm, v_w_in, v_w_mem_kv, v_w_out, v_w_pool, v_pool_scale, v_kv_norm, v_w_kv, v_w_gate_up, v_w_down):
    w = dict(norm_gains=norm_gains, mem_norm=mem_norm, w_in=w_in, w_mem_kv=w_mem_kv, w_out=w_out, w_pool=w_pool,
             pool_scale=pool_scale, kv_norm=kv_norm, w_kv=w_kv, w_gate_up=w_gate_up, w_down=w_down)
    m = dict(norm_gains=m_norm_gains, mem_norm=m_mem_norm, w_in=m_w_in, w_mem_kv=m_w_mem_kv, w_out=m_w_out,
             w_pool=m_w_pool, pool_scale=m_pool_scale, kv_norm=m_kv_norm, w_kv=m_w_kv, w_gate_up=m_w_gate_up,
             w_down=m_w_down)
    v = dict(norm_gains=v_norm_gains, mem_norm=v_mem_norm, w_in=v_w_in, w_mem_kv=v_w_mem_kv, w_out=v_w_out,
             w_pool=v_w_pool, pool_scale=v_pool_scale, kv_norm=v_kv_norm, w_kv=v_w_kv, w_gate_up=v_w_gate_up,
             w_down=v_w_down)

    def transposed_view(d):
        d = dict(d)
        d["w_gate_up"] = jnp.swapaxes(d["w_gate_up"], 1, 2)
        d["w_kv"] = jnp.swapaxes(d["w_kv"], 0, 1)
        return d

    wv, mv, vv = transposed_view(w), transposed_view(m), transposed_view(v)

    PARTS = {"mix": ("w_in", "w_mem_kv", "w_out"), "ffn": ("w_gate_up", "w_down"), "gu": ("w_gate_up",),
             "down": ("w_down",), "all": ("w_in", "w_mem_kv", "w_out", "w_gate_up", "w_down")}

    small = _pack_small(norm_gains, pool_scale)
    handles = {(0, "mix"): gather2_start([wv[k][0].astype(bf16) for k in PARTS["mix"]] + [small], x,
                                         name="gather_start_mix_l0")}
    first = handles[0, "mix"]["token"]

    def parts_of(l):
        return (("mix", "gu", "down"), ("mix", "ffn"))[l] if l < 2 else ("all",)

    wb = [dict(zip(LAYER_MATS, mats))
          for mats in to_bf16_layers([wv[k] for k in LAYER_MATS], first, name="weights_bf16")]

    def part_items(l, part):
        items = [(wb[l][k], "gather") for k in PARTS[part]]
        if part == "ffn" and l == N_A_LAYERS - 1:
            items.append((wv["w_kv"].astype(bf16), "gather"))
        return items

    def start_layer(l, after):
        for part in parts_of(l):
            if (l, part) in handles:
                continue
            if l in TWO_LEVEL_LAYERS:
                handles[l, part] = gather2_start([a for a, _ in part_items(l, part)], after,
                                                 name=f"gather_start_{part}_l{l}")
            else:
                handles[l, part] = exchange_start(part_items(l, part), after, name=f"gather_start_{part}_l{l}")
            after = handles[l, part]["token"]
        return after

    token = first
    for l in range(LOOKAHEAD):
        token = start_layer(l, token)
    landed, fetched = {}, {}

    passed_early = {}

    def layer_weights(l, part, after):
        if (l, part) in fetched:
            return fetched[l, part]
        if part not in parts_of(l):
            if part == "down" and l + 1 in TWO_LEVEL_LAYERS and parts_of(l + 1) == ("all",):
                passed_early[l + 1] = gather2_forward(handles[l + 1, "all"], after, name=f"gather_forward_all_l{l + 1}")
                return {}, passed_early[l + 1]["token"]
            if part == "down" or (part == "gu" and "all" in parts_of(l)):
                return {}, None
            part = "all" if "all" in parts_of(l) else "ffn"
        if l == 0 and part == "mix":
            after = token
        if l in TWO_LEVEL_LAYERS:
            passed = passed_early.pop(l, None)
            if passed is None:
                passed = gather2_forward(handles[l, part], after, name=f"gather_forward_{part}_l{l}")
            got = gather2_wait(passed, after, name=f"gather_wait_{part}_l{l}")
        else:
            got = exchange_wait(handles[l, part], after, name=f"gather_wait_{part}_l{l}")
        landed[l, part] = got
        started = None
        if part in ("mix", "all") and l + LOOKAHEAD < DEPTH:
            started = start_layer(l + LOOKAHEAD, got[0])
        W = {k: g.reshape(-1, g.shape[-1]) for k, g in zip(PARTS[part], got)}
        fetched[l, part] = (W, started)
        return W, started

    layer_weights(0, "mix", x)
    gsmall = landed[0, "mix"][len(PARTS["mix"])]
    P = {"norm_gains": jnp.moveaxis(gsmall[:, :16].reshape(N_DEV, 4, 4, 128), 0, 2).reshape(4, 4, D_MODEL),
         "pool_scale": jnp.moveaxis(gsmall[:, 16:18, :POOL_SHARD], 0, 1).reshape(2, MAIN_W),
         "mem_norm": mem_norm, "kv_norm": kv_norm, "w_pool": w_pool}

    def kv_weight(after):
        g = landed[N_A_LAYERS - 1, "ffn"][len(PARTS["ffn"])]
        return g.reshape(2 * MAIN_W, D_MODEL)

    ghandles = {}

    pending = {}

    def gparts_of(l):
        return ("ffn", "mix") if l < 2 else ("all",)

    def emit_grads(l, part, gw):
        if part not in gparts_of(l):
            pending.setdefault(l, {}).update(gw)
            if part == "ffn":
                return None
            gw, part = pending[l], "all"
        items = [(gw[k].reshape((N_DEV, -1) + gw[k].shape[-1:]), "scatter") for k in PARTS[part]]
        if part != "ffn" and l == N_A_LAYERS:
            items.append((gw["w_kv"].reshape(N_DEV, KV_SHARD, D_MODEL), "scatter"))
        if part != "ffn" and l < N_A_LAYERS:
            items.append((gw["w_pool"], "gather"))
        ghandles[l, part] = exchange_start(items, gsmall, name=f"scatter_start_{part}_l{l}")
        return ghandles[l, part]["token"]

    sq, grad_x, GS, emitted = local_step(x, mem, positions, loss_target, P, layer_weights, kv_weight, emit_grads)

    def pool3(a):
        return a.reshape(N_A_LAYERS, MAIN_W, POOL_GROUP)

    out = {}
    after = [emitted]

    def finish_layer(l, after):
        for part in gparts_of(l):
            got = exchange_wait(ghandles[l, part], after, name=f"scatter_wait_{part}_l{l}")
            names = list(PARTS[part])
            entries = [(slots, wv[k], mv[k], vv[k], l, out.get(k)) for k, slots in zip(names, got)]
            if part != "ffn" and l == N_A_LAYERS:
                names.append("w_kv")
                entries.append((got[-1], wv["w_kv"], mv["w_kv"], vv["w_kv"], None, None))
            if part != "ffn" and l < N_A_LAYERS:
                names.append("w_pool")
                entries.append((got[-1], pool3(w_pool), pool3(m_w_pool), pool3(v_w_pool), l, out.get("w_pool")))
            out.update(zip(names, adamw(entries, name=f"adamw_{part}_l{l}")))
            after = [out[k][0] for k in names]
        return after

    for l in reversed(range(1, DEPTH)):
        after = finish_layer(l, after)

    gs = _pack_small(jnp.moveaxis(GS["norm_gains"].reshape(4, 4, N_DEV, 128), 2, 0),
                     jnp.moveaxis(GS["pool_scale"].reshape(2, N_DEV, POOL_SHARD), 1, 0))
    parts_small, parts_repl, parts_sq = exchange(
        [(gs, "scatter"), (_pack_repl(GS["mem_norm"], GS["kv_norm"]), "gather"),
         (jnp.full((8, 128), sq, f32), "gather")],
        name="exchange_small_grads", after=after)
    loss = (0.5 / D_MODEL) * jnp.sum(parts_sq[:, 0, 0])
    finish_layer(0, [parts_small])
    out["w_gate_up"] = [jnp.swapaxes(r, 1, 2) for r in out["w_gate_up"]]
    out["w_kv"] = [jnp.swapaxes(r, 0, 1) for r in out["w_kv"]]
    out["w_pool"] = [r.reshape(w_pool.shape) for r in out["w_pool"]]

    res_small, res_repl = adamw(
        [(parts_small, small, _pack_small(m_norm_gains, m_pool_scale), _pack_small(v_norm_gains, v_pool_scale),
          None, None),
         (parts_repl, _pack_repl(mem_norm, kv_norm), _pack_repl(m_mem_norm, m_kv_norm),
          _pack_repl(v_mem_norm, v_kv_norm), None, None)], name="adamw_small")
    out["norm_gains"], out["pool_scale"] = zip(*[_unpack_small(r) for r in res_small])
    out["mem_norm"], out["kv_norm"] = zip(*[_unpack_repl(r) for r in res_repl])

    return (loss, grad_x, *[out[k][0] for k in WEIGHTS], *[out[k][1] for k in WEIGHTS],
            *[out[k][2] for k in WEIGHTS], *[out[k][3] for k in WEIGHTS])
```
